```python
import jax, jax.numpy as jnp
from jax import lax
import numpy as np

D_MODEL = 4096
BATCH = 16
SEQ = 2048
DEPTH = 1

MEM_LEN = 256
HEAD_DIM = 128
DIL_GROUPS = ((128, 1), (512, 4), (2048, 16))
ATTN_WIDTH = 3 * D_MODEL // 8
CONV_WIDTH = 3 * D_MODEL // 8
XATTN_WIDTH = D_MODEL // 4
MIX_WIDTH = ATTN_WIDTH + CONV_WIDTH + XATTN_WIDTH
N_ATTN_HEADS = ATTN_WIDTH // HEAD_DIM
HEADS_PER_DIL = N_ATTN_HEADS // len(DIL_GROUPS)
N_XATTN_HEADS = 4
XATTN_HEAD_DIM = XATTN_WIDTH // N_XATTN_HEADS
CONV_K = 3
IN_COLS = 4 * ATTN_WIDTH + 4 * CONV_WIDTH + 2 * XATTN_WIDTH
BLOCK = 128
ROPE_THETA = 10000.0
EPS = 1e-6
NEG_INF = -1e30

kernel_name = "hybrid_dilated_conv_memory_layer"


def rms_norm(x, g):
    x32 = x.astype(jnp.float32)
    y = x32 * lax.rsqrt(jnp.mean(x32 * x32, axis=-1, keepdims=True) + EPS)
    return (y * g.astype(jnp.float32)).astype(x.dtype)


def rope(x, pos):
    half = x.shape[-1] // 2
    inv = 1.0 / (ROPE_THETA ** (jnp.arange(half, dtype=jnp.float32) / half))
    ang = pos.astype(jnp.float32)[:, None] * inv[None, :]
    cos = jnp.cos(ang)[:, None, :]
    sin = jnp.sin(ang)[:, None, :]
    x32 = x.astype(jnp.float32)
    x1, x2 = x32[..., :half], x32[..., half:]
    return jnp.concatenate([x1 * cos - x2 * sin, x2 * cos + x1 * sin], axis=-1).astype(x.dtype)


def dilated_window_attention(q, k, v, window, dilation):
    b, s, h, dh = q.shape
    steps = window // dilation
    L = s // dilation
    nb = -(-L // BLOCK)
    Lp = nb * BLOCK
    bd = b * dilation

    def to_sub(t):
        t = t.reshape(b, L, dilation, h, dh).transpose(0, 2, 1, 3, 4).reshape(bd, L, h, dh)
        return jnp.pad(t, ((0, 0), (0, Lp - L), (0, 0), (0, 0)))

    def band(t):
        tp = jnp.pad(t, ((0, 0), (BLOCK, 0), (0, 0), (0, 0))).reshape(bd, nb + 1, BLOCK, h, dh)
        return jnp.concatenate([tp[:, :-1], tp[:, 1:]], axis=2)

    qb = to_sub(q).reshape(bd, nb, BLOCK, h, dh)
    kb = band(to_sub(k))
    vb = band(to_sub(v))
    scores = jnp.einsum('znqhd,znkhd->znhqk', qb, kb,
                        preferred_element_type=jnp.float32) * (dh ** -0.5)
    qi = jnp.arange(BLOCK)[:, None]
    kk = jnp.arange(2 * BLOCK)[None, :]
    dist = qi + BLOCK - kk
    key_pos = (jnp.arange(nb)[:, None, None] - 1) * BLOCK + kk[None]
    valid = (dist >= 0) & (dist <= steps) & (key_pos >= 0)
    scores = jnp.where(valid[None, :, None], scores, NEG_INF)
    lse = jax.nn.logsumexp(scores, axis=-1)
    p = jnp.exp(scores - lse[..., None])
    out = jnp.einsum('znhqk,znkhd->znqhd', p.astype(v.dtype), vb)
    out = out.reshape(b, dilation, Lp, h, dh)[:, :, :L].transpose(0, 2, 1, 3, 4).reshape(b, s, h, dh)
    lse = lse.transpose(0, 1, 3, 2).reshape(b, dilation, Lp, h)[:, :, :L]
    lse = lse.transpose(0, 2, 1, 3).reshape(b, s, h)
    return out, lse


def short_conv(u, w):
    s = u.shape[1]
    up = jnp.pad(u, ((0, 0), (CONV_K - 1, 0), (0, 0)))
    y = up[:, 0:s] * w[0]
    for j in range(1, CONV_K):
        y = y + up[:, j:j + s] * w[j]
    return y


def memory_cross_attention(q, mk, mv):
    scores = jnp.einsum('bshd,bmhd->bhsm', q, mk,
                        preferred_element_type=jnp.float32) * (q.shape[-1] ** -0.5)
    p = jax.nn.softmax(scores, axis=-1)
    return jnp.einsum('bhsm,bmhd->bshd', p.astype(mv.dtype), mv)


def _fwd_setup_inputs(seed: int = 0) -> dict:
    key = jax.random.key(seed)
    ks = jax.random.split(key, 10)
    f32 = jnp.float32
    x = jax.random.normal(ks[0], (BATCH, SEQ, D_MODEL), f32)
    mem = jax.random.normal(ks[1], (BATCH, MEM_LEN, D_MODEL), f32)
    pre_norm = 1.0 + 0.05 * jax.random.normal(ks[2], (DEPTH, D_MODEL), f32)
    w_in = jax.random.normal(ks[3], (DEPTH, D_MODEL, IN_COLS), f32) * D_MODEL ** -0.5
    conv_w = jax.random.normal(ks[4], (DEPTH, CONV_K, CONV_WIDTH), f32) * CONV_K ** -0.5
    mem_norm = 1.0 + 0.05 * jax.random.normal(ks[5], (DEPTH, D_MODEL), f32)
    w_mem_kv = jax.random.normal(ks[6], (DEPTH, D_MODEL, 2 * XATTN_WIDTH), f32) * D_MODEL ** -0.5
    w_out = jax.random.normal(ks[7], (DEPTH, MIX_WIDTH, D_MODEL), f32) * MIX_WIDTH ** -0.5
    post_norm = 1.0 + 0.05 * jax.random.normal(ks[8], (DEPTH, D_MODEL), f32)
    return {"x": x, "mem": mem, "pre_norm": pre_norm, "w_in": w_in, "conv_w": conv_w,
            "mem_norm": mem_norm, "w_mem_kv": w_mem_kv, "w_out": w_out,
            "post_norm": post_norm}


def _fwd_reference(x, mem, pre_norm, w_in, conv_w, mem_norm, w_mem_kv, w_out, post_norm):
    b, s, _ = x.shape
    m_len = mem.shape[1]
    pos = jnp.arange(s)
    mem_pos = jnp.arange(m_len)
    sizes = (ATTN_WIDTH,) * 4 + (CONV_WIDTH,) * 4 + (XATTN_WIDTH,) * 2
    split_pts = [int(c) for c in np.cumsum(sizes)[:-1]]
    for layer in range(DEPTH):
        h = rms_norm(x, pre_norm[layer])
        proj = h @ w_in[layer]
        q_a, k_a, v_a, z_a, u_c, b_c, c_c, z_c, q_x, z_x = jnp.split(proj, split_pts, axis=-1)

        q_a = rope(q_a.reshape(b, s, N_ATTN_HEADS, HEAD_DIM), pos)
        k_a = rope(k_a.reshape(b, s, N_ATTN_HEADS, HEAD_DIM), pos)
        v_a = v_a.reshape(b, s, N_ATTN_HEADS, HEAD_DIM)
        outs, lses = [], []
        for g, (window, dil) in enumerate(DIL_GROUPS):
            sl = slice(g * HEADS_PER_DIL, (g + 1) * HEADS_PER_DIL)
            o, l = dilated_window_attention(q_a[:, :, sl], k_a[:, :, sl], v_a[:, :, sl], window, dil)
            outs.append(o)
            lses.append(l)
        o = jnp.stack(outs, axis=2)
        alpha = jax.nn.softmax(jnp.stack(lses, axis=2), axis=2)
        y_attn = (o * alpha[..., None].astype(o.dtype)).reshape(b, s, ATTN_WIDTH)
        y_attn = y_attn * jax.nn.silu(z_a)

        y_conv = b_c * short_conv(c_c * u_c, conv_w[layer]) * jax.nn.silu(z_c)

        mkv = rms_norm(mem, mem_norm[layer]) @ w_mem_kv[layer]
        mk, mv = jnp.split(mkv, 2, axis=-1)
        mk = rope(mk.reshape(b, m_len, N_XATTN_HEADS, XATTN_HEAD_DIM), mem_pos)
        mv = mv.reshape(b, m_len, N_XATTN_HEADS, XATTN_HEAD_DIM)
        qx = rope(q_x.reshape(b, s, N_XATTN_HEADS, XATTN_HEAD_DIM), pos + m_len)
        y_x = memory_cross_attention(qx, mk, mv).reshape(b, s, XATTN_WIDTH) * jax.nn.silu(z_x)

        y = jnp.concatenate([y_attn, y_conv, y_x], axis=-1) @ w_out[layer]
        x = x + rms_norm(y, post_norm[layer])
    return x


import jax as _jax
import jax.numpy as _jnp

TWIN_FORMAT = 'train_step'
FWD_PARAMS = ['x', 'mem', 'pre_norm', 'w_in', 'conv_w', 'mem_norm', 'w_mem_kv', 'w_out', 'post_norm']
TWIN_WEIGHTS = ['pre_norm', 'w_in', 'conv_w', 'mem_norm', 'w_mem_kv', 'w_out', 'post_norm']
TWIN_DIFF_INPUT = 'x'
TWIN_INPUTS = ['x', 'mem', 'pre_norm', 'w_in', 'conv_w', 'mem_norm', 'w_mem_kv', 'w_out', 'post_norm', 'loss_target', 'm_pre_norm', 'm_w_in', 'm_conv_w', 'm_mem_norm', 'm_w_mem_kv', 'm_w_out', 'm_post_norm', 'v_pre_norm', 'v_w_in', 'v_conv_w', 'v_mem_norm', 'v_w_mem_kv', 'v_w_out', 'v_post_norm']
TWIN_OUTPUTS = ['loss', 'grad_x', 'grad_pre_norm', 'grad_w_in', 'grad_conv_w', 'grad_mem_norm', 'grad_w_mem_kv', 'grad_w_out', 'grad_post_norm', 'delta_pre_norm', 'delta_w_in', 'delta_conv_w', 'delta_mem_norm', 'delta_w_mem_kv', 'delta_w_out', 'delta_post_norm', 'new_m_pre_norm', 'new_m_w_in', 'new_m_conv_w', 'new_m_mem_norm', 'new_m_w_mem_kv', 'new_m_w_out', 'new_m_post_norm', 'new_v_pre_norm', 'new_v_w_in', 'new_v_conv_w', 'new_v_mem_norm', 'new_v_w_mem_kv', 'new_v_w_out', 'new_v_post_norm']
TWIN_LEAF_KINDS = {'loss': 'loss', 'grad_x': 'grad_x', 'grad_pre_norm': 'grad_w', 'grad_w_in': 'grad_w', 'grad_conv_w': 'grad_w', 'grad_mem_norm': 'grad_w', 'grad_w_mem_kv': 'grad_w', 'grad_w_out': 'grad_w', 'grad_post_norm': 'grad_w', 'delta_pre_norm': 'delta_w', 'delta_w_in': 'delta_w', 'delta_conv_w': 'delta_w', 'delta_mem_norm': 'delta_w', 'delta_w_mem_kv': 'delta_w', 'delta_w_out': 'delta_w', 'delta_post_norm': 'delta_w', 'new_m_pre_norm': 'new_m', 'new_m_w_in': 'new_m', 'new_m_conv_w': 'new_m', 'new_m_mem_norm': 'new_m', 'new_m_w_mem_kv': 'new_m', 'new_m_w_out': 'new_m', 'new_m_post_norm': 'new_m', 'new_v_pre_norm': 'new_v', 'new_v_w_in': 'new_v', 'new_v_conv_w': 'new_v', 'new_v_mem_norm': 'new_v', 'new_v_w_mem_kv': 'new_v', 'new_v_w_out': 'new_v', 'new_v_post_norm': 'new_v'}


def _forward(args):
    return _fwd_reference(*[args[k] for k in FWD_PARAMS])


def _output_shape():
    def fwd():
        inp = _fwd_setup_inputs(0)
        return _fwd_reference(*[inp[k] for k in FWD_PARAMS])
    out = _jax.eval_shape(fwd)
    return out.shape, out.dtype

N_MICROBATCH = 1
ADAM_LR = 0.001
ADAM_B1 = 0.9
ADAM_B2 = 0.999
ADAM_EPS = 1e-08
ADAM_WD = 0.01
ADAM_STEP = 10
PER_EXAMPLE_BATCH_AXIS = {'x': 0, 'mem': 0, 'loss_target': 0}
SHARED_INPUTS = []
_WEIGHT_DTYPES = {'pre_norm': _jnp.float32, 'w_in': _jnp.float32, 'conv_w': _jnp.float32, 'mem_norm': _jnp.float32, 'w_mem_kv': _jnp.float32, 'w_out': _jnp.float32, 'post_norm': _jnp.float32}
MOMENT_SCALE = {'pre_norm': 1.297235e-01, 'w_in': 6.846208e-02, 'conv_w': 1.050247e-01, 'mem_norm': 7.705525e-03, 'w_mem_kv': 1.091277e-02, 'w_out': 6.515535e-02, 'post_norm': 8.018385e+00}


def _to_microbatches(a, axis):
    t = _jnp.moveaxis(a, axis, 0)
    t = t.reshape((N_MICROBATCH, t.shape[0] // N_MICROBATCH) + t.shape[1:])
    return _jnp.moveaxis(t, 1, axis + 1)


def setup_inputs(seed: int = 0) -> dict:
    inp = _fwd_setup_inputs(seed)
    key = _jax.random.fold_in(_jax.random.key(seed), 7919)
    shape, _ = _output_shape()
    out = dict(inp)
    out["loss_target"] = _jax.random.normal(_jax.random.fold_in(key, 0), shape, _jnp.float32)
    for i, name in enumerate(TWIN_WEIGHTS):
        w = inp[name].astype(_jnp.float32)
        if MOMENT_SCALE is None:
            s = _jnp.sqrt(_jnp.mean(_jnp.square(w)) + 1e-30)
        else:
            s = MOMENT_SCALE[name]
        km, kv = _jax.random.split(_jax.random.fold_in(key, i + 1))
        out[name] = w
        out["m_" + name] = s * _jax.random.normal(km, w.shape, _jnp.float32)
        out["v_" + name] = (s * s) * _jax.random.uniform(kv, w.shape, _jnp.float32, 0.5, 1.5)
    if N_MICROBATCH > 1:
        for name, axis in PER_EXAMPLE_BATCH_AXIS.items():
            out[name] = _to_microbatches(out[name], axis)
    return {'x': out['x'], 'mem': out['mem'], 'pre_norm': out['pre_norm'], 'w_in': out['w_in'], 'conv_w': out['conv_w'], 'mem_norm': out['mem_norm'], 'w_mem_kv': out['w_mem_kv'], 'w_out': out['w_out'], 'post_norm': out['post_norm'], 'loss_target': out['loss_target'], 'm_pre_norm': out['m_pre_norm'], 'm_w_in': out['m_w_in'], 'm_conv_w': out['m_conv_w'], 'm_mem_norm': out['m_mem_norm'], 'm_w_mem_kv': out['m_w_mem_kv'], 'm_w_out': out['m_w_out'], 'm_post_norm': out['m_post_norm'], 'v_pre_norm': out['v_pre_norm'], 'v_w_in': out['v_w_in'], 'v_conv_w': out['v_conv_w'], 'v_mem_norm': out['v_mem_norm'], 'v_w_mem_kv': out['v_w_mem_kv'], 'v_w_out': out['v_w_out'], 'v_post_norm': out['v_post_norm']}


def _loss(weights, diff, rest, loss_target):
    with _jax.named_scope("forward"):
        args = {**rest, TWIN_DIFF_INPUT: diff, **{k: w.astype(_WEIGHT_DTYPES[k]) for k, w in weights.items()}}
        y = _forward(args)
    with _jax.named_scope("loss_head"):
        err = _jnp.square(y.astype(_jnp.float32) - loss_target)
        return 0.5 * _jnp.sum(_jnp.mean(err, axis=-1)) if err.ndim else 0.5 * err


def _adamw(w, g, m, v):
    m = ADAM_B1 * m + (1.0 - ADAM_B1) * g
    v = ADAM_B2 * v + (1.0 - ADAM_B2) * _jnp.square(g)
    m_hat = m / (1.0 - ADAM_B1 ** ADAM_STEP)
    v_hat = v / (1.0 - ADAM_B2 ** ADAM_STEP)
    delta = -ADAM_LR * (m_hat / (_jnp.sqrt(v_hat) + ADAM_EPS) + ADAM_WD * w)
    return delta, m, v


def reference(x, mem, pre_norm, w_in, conv_w, mem_norm, w_mem_kv, w_out, post_norm, loss_target, m_pre_norm, m_w_in, m_conv_w, m_mem_norm, m_w_mem_kv, m_w_out, m_post_norm, v_pre_norm, v_w_in, v_conv_w, v_mem_norm, v_w_mem_kv, v_w_out, v_post_norm):
    given = dict(x=x, mem=mem, pre_norm=pre_norm, w_in=w_in, conv_w=conv_w, mem_norm=mem_norm, w_mem_kv=w_mem_kv, w_out=w_out, post_norm=post_norm, loss_target=loss_target, m_pre_norm=m_pre_norm, m_w_in=m_w_in, m_conv_w=m_conv_w, m_mem_norm=m_mem_norm, m_w_mem_kv=m_w_mem_kv, m_w_out=m_w_out, m_post_norm=m_post_norm, v_pre_norm=v_pre_norm, v_w_in=v_w_in, v_conv_w=v_conv_w, v_mem_norm=v_mem_norm, v_w_mem_kv=v_w_mem_kv, v_w_out=v_w_out, v_post_norm=v_post_norm)
    weights = {n: given[n] for n in TWIN_WEIGHTS}
    shared = {n: given[n] for n in SHARED_INPUTS}
    per_example = {n: given[n] for n in ['x', 'mem']}
    grad_fn = _jax.value_and_grad(_loss, argnums=(0, 1))

    def one_microbatch(ex, loss_target):
        ex = dict(ex)
        diff = ex.pop(TWIN_DIFF_INPUT)
        return grad_fn(weights, diff, {**shared, **ex}, loss_target)

    if N_MICROBATCH == 1:
        loss, (grad_w, grad_x) = one_microbatch(per_example, given["loss_target"])
    else:
        def body(carry, xs):
            loss_sum, grad_sum = carry
            l_k, (gw_k, gx_k) = one_microbatch(xs[0], xs[1])
            with _jax.named_scope("update"):
                return (loss_sum + l_k, _jax.tree.map(_jnp.add, grad_sum, gw_k)), gx_k

        init = (_jnp.zeros((), _jnp.float32), _jax.tree.map(_jnp.zeros_like, weights))
        (loss, grad_w), grad_x = _jax.lax.scan(body, init, (per_example, given["loss_target"]))
    with _jax.named_scope("update"):
        delta_w, new_m, new_v = {}, {}, {}
        for n in TWIN_WEIGHTS:
            delta_w[n], new_m[n], new_v[n] = _adamw(weights[n], grad_w[n], given["m_" + n], given["v_" + n])
    return (loss, grad_x, *[grad_w[n] for n in TWIN_WEIGHTS], *[delta_w[n] for n in TWIN_WEIGHTS],
            *[new_m[n] for n in TWIN_WEIGHTS], *[new_v[n] for n in TWIN_WEIGHTS])
```

```python
import functools

import jax
import jax.numpy as jnp
from jax import lax
from jax.experimental import pallas as pl
from jax.experimental.pallas import tpu as pltpu

F32 = jnp.float32
BF16 = jnp.bfloat16

D = 4096
S = 2048
NB = 2
T = NB * S
MLEN = 256
HD = 128
AW = 1536
CW = 1536
XW = 1024
XHD = 256
NXH = 4
NC = 14336
QA, KA, VA, ZA, UC, BC, CC, ZC, QX, ZX = 0, 1536, 3072, 4608, 6144, 7680, 9216, 10752, 12288, 13312
NCHIP = 4
WB = NC // NCHIP
DIL = (1, 4, 16)
HPG = 4
EPS = 1e-6
NEG = -1e30
ROPE_THETA = 10000.0
A_SCALE = HD ** -0.5
X_SCALE = XHD ** -0.5

ADAM_LR = 0.001
ADAM_B1 = 0.9
ADAM_B2 = 0.999
ADAM_EPS = 1e-08
ADAM_WD = 0.01
ADAM_STEP = 10

MESH = pl.DeviceIdType.MESH
MIB = 1024 * 1024


def _cp(sem, vmem_mib=48):
    return pltpu.CompilerParams(dimension_semantics=sem, vmem_limit_bytes=vmem_mib * MIB)


def _sigmoid(z):
    return 1.0 / (1.0 + jnp.exp(-z))


def _rope(x, cos, sin, half):
    return x * cos + pltpu.roll(x, half, 1) * sin


def _rope_t(g, cos, sin, half):
    return g * cos + pltpu.roll(g * sin, half, 1)


def _rms_fwd(x2, g, name):
    rows = x2.shape[0]
    tr = 256

    def body(x_ref, g_ref, o_ref):
        x = x_ref[...]
        r = lax.rsqrt(jnp.mean(x * x, axis=-1, keepdims=True) + EPS)
        o_ref[...] = (x * r * g_ref[...]).astype(BF16)

    return pl.pallas_call(
        body, name=name, grid=(rows // tr,),
        in_specs=[pl.BlockSpec((tr, D), lambda i: (i, 0)), pl.BlockSpec((1, D), lambda i: (0, 0))],
        out_specs=pl.BlockSpec((tr, D), lambda i: (i, 0)),
        out_shape=jax.ShapeDtypeStruct((rows, D), BF16),
        compiler_params=_cp(("parallel",)),
    )(x2, g)


def _norm_gain_grad(dn, x2, name):
    rows = x2.shape[0]
    tr = 256

    def body(dn_ref, x_ref, dg_ref):
        @pl.when(pl.program_id(0) == 0)
        def _():
            dg_ref[...] = jnp.zeros_like(dg_ref)
        x = x_ref[...]
        r = lax.rsqrt(jnp.mean(x * x, axis=-1, keepdims=True) + EPS)
        dg_ref[...] += jnp.sum(dn_ref[...] * (x * r), axis=0, keepdims=True)

    return pl.pallas_call(
        body, name=name, grid=(rows // tr,),
        in_specs=[pl.BlockSpec((tr, D), lambda i: (i, 0)), pl.BlockSpec((tr, D), lambda i: (i, 0))],
        out_specs=pl.BlockSpec((1, D), lambda i: (0, 0)),
        out_shape=jax.ShapeDtypeStruct((1, D), F32),
        compiler_params=_cp(("arbitrary",)),
    )(dn, x2)


def _pre_norm_bwd(dh, x2, g, dout):
    tr = 256

    def body(dh_ref, x_ref, g_ref, dout_ref, gx_ref, dg_ref):
        @pl.when(pl.program_id(0) == 0)
        def _():
            dg_ref[...] = jnp.zeros_like(dg_ref)
        x = x_ref[...]
        dh_ = dh_ref[...]
        r = lax.rsqrt(jnp.mean(x * x, axis=-1, keepdims=True) + EPS)
        xhat = x * r
        dg_ref[...] += jnp.sum(dh_ * xhat, axis=0, keepdims=True)
        dxn = dh_ * g_ref[...]
        gx_ref[...] = dout_ref[...] + r * (dxn - xhat * jnp.mean(dxn * xhat, axis=-1, keepdims=True))

    row = pl.BlockSpec((tr, D), lambda i: (i, 0))
    vec = pl.BlockSpec((1, D), lambda i: (0, 0))
    return pl.pallas_call(
        body, name="pre_norm_bwd", grid=(T // tr,),
        in_specs=[row, row, vec, row],
        out_specs=[row, vec],
        out_shape=[jax.ShapeDtypeStruct((T, D), F32), jax.ShapeDtypeStruct((1, D), F32)],
        compiler_params=_cp(("arbitrary",)),
    )(dh, x2, g, dout)


def _post_norm_loss(y, x2, tgt, g):
    tr = 256

    def body(y_ref, x_ref, t_ref, g_ref, dy_ref, dout_ref, dg_ref, loss_ref):
        @pl.when(pl.program_id(0) == 0)
        def _():
            dg_ref[...] = jnp.zeros_like(dg_ref)
            loss_ref[...] = jnp.zeros_like(loss_ref)
        yv = y_ref[...]
        gv = g_ref[...]
        r = lax.rsqrt(jnp.mean(yv * yv, axis=-1, keepdims=True) + EPS)
        yhat = yv * r
        err = x_ref[...] + yhat * gv - t_ref[...]
        loss_ref[...] += jnp.sum(jnp.sum(err * err, axis=1, keepdims=True), axis=0, keepdims=True) * (0.5 / D)
        dout = err * (1.0 / D)
        dout_ref[...] = dout
        dg_ref[...] += jnp.sum(dout * yhat, axis=0, keepdims=True)
        dyn = dout * gv
        dy_ref[...] = (r * (dyn - yhat * jnp.mean(dyn * yhat, axis=-1, keepdims=True))).astype(BF16)

    row = pl.BlockSpec((tr, D), lambda i: (i, 0))
    vec = pl.BlockSpec((1, D), lambda i: (0, 0))
    return pl.pallas_call(
        body, name="post_norm_loss", grid=(T // tr,),
        in_specs=[row, row, row, vec],
        out_specs=[row, row, vec, pl.BlockSpec((1, 128), lambda i: (0, 0))],
        out_shape=[jax.ShapeDtypeStruct((T, D), BF16), jax.ShapeDtypeStruct((T, D), F32),
                   jax.ShapeDtypeStruct((1, D), F32), jax.ShapeDtypeStruct((1, 128), F32)],
        compiler_params=_cp(("arbitrary",)),
    )(y, x2, tgt, g)


NN = (((1,), (0,)), ((), ()))
NT = (((1,), (1,)), ((), ()))
TN = (((0,), (0,)), ((), ()))


def _matmul(a, b, *, name, dims, grid, a_block, a_map, b_block, b_map, o_block, o_map, out_shape, out_dtype=F32):
    nk = grid[2]

    def body(a_ref, b_ref, o_ref, acc_ref):
        k = pl.program_id(2)

        @pl.when(k == 0)
        def _():
            acc_ref[...] = jnp.zeros_like(acc_ref)

        acc_ref[...] += lax.dot_general(a_ref[...], b_ref[...], dims, preferred_element_type=F32)

        @pl.when(k == nk - 1)
        def _():
            o_ref[...] = acc_ref[...].astype(o_ref.dtype)

    return pl.pallas_call(
        body, name=name, grid=grid,
        in_specs=[pl.BlockSpec(a_block, a_map), pl.BlockSpec(b_block, b_map)],
        out_specs=pl.BlockSpec(o_block, o_map),
        out_shape=jax.ShapeDtypeStruct(out_shape, out_dtype),
        scratch_shapes=[pltpu.VMEM(o_block, F32)],
        compiler_params=_cp(("parallel", "parallel", "arbitrary")),
    )(a, b)


def _mm_nn(a, b, name, tm, tn, tk):
    m, kd = a.shape
    n = b.shape[1]
    return _matmul(a, b, name=name, dims=NN, grid=(m // tm, n // tn, kd // tk),
                   a_block=(tm, tk), a_map=lambda i, j, k: (i, k),
                   b_block=(tk, tn), b_map=lambda i, j, k: (k, j),
                   o_block=(tm, tn), o_map=lambda i, j, k: (i, j), out_shape=(m, n))


def _mm_nt(a, b, name, tm, tn, tk):
    m, kd = a.shape
    n = b.shape[0]
    return _matmul(a, b, name=name, dims=NT, grid=(m // tm, n // tn, kd // tk),
                   a_block=(tm, tk), a_map=lambda i, j, k: (i, k),
                   b_block=(tn, tk), b_map=lambda i, j, k: (j, k),
                   o_block=(tm, tn), o_map=lambda i, j, k: (i, j), out_shape=(m, n))


def _mm_tn(a, b, name, tm, tn, tk):
    kd, m = a.shape
    n = b.shape[1]
    return _matmul(a, b, name=name, dims=TN, grid=(m // tm, n // tn, kd // tk),
                   a_block=(tk, tm), a_map=lambda i, j, k: (k, i),
                   b_block=(tk, tn), b_map=lambda i, j, k: (k, j),
                   o_block=(tm, tn), o_map=lambda i, j, k: (i, j), out_shape=(m, n))


W_TN = 1792
W_NJ = WB // W_TN


def _proj(h, wg):
    tm, tk = 1024, 512
    return _matmul(h, wg, name="proj", dims=NN, grid=(T // tm, NC // W_TN, D // tk),
                   a_block=(tm, tk), a_map=lambda i, j, k: (i, k),
                   b_block=(tk, W_TN), b_map=lambda i, j, k: ((j // W_NJ) * (D // tk) + k, j % W_NJ),
                   o_block=(tm, W_TN), o_map=lambda i, j, k: (i, j), out_shape=(T, NC))


def _dh(dproj, wg):
    tm, tn = 1024, 1024
    return _matmul(dproj, wg, name="dh", dims=NT, grid=(T // tm, D // tn, NC // W_TN),
                   a_block=(tm, W_TN), a_map=lambda i, j, k: (i, k),
                   b_block=(tn, W_TN), b_map=lambda i, j, k: ((k // W_NJ) * (D // tn) + j, k % W_NJ),
                   o_block=(tm, tn), o_map=lambda i, j, k: (i, j), out_shape=(T, D))


def _grad_w_in(h, dproj):
    tm, tk = 1024, 512
    return _matmul(h, dproj, name="grad_w_in", dims=TN, grid=(D // tm, NC // W_TN, T // tk),
                   a_block=(tk, tm), a_map=lambda i, j, k: (k, i),
                   b_block=(tk, W_TN), b_map=lambda i, j, k: (k, j),
                   o_block=(tm, W_TN), o_map=lambda i, j, k: ((j // W_NJ) * (D // tm) + i, j % W_NJ),
                   out_shape=(NCHIP * D, WB))


def _rope_tables(pos, half):
    inv = 1.0 / (ROPE_THETA ** (jnp.arange(half, dtype=F32) / half))
    ang = pos.astype(F32)[:, None] * inv[None, :]
    cos, sin = jnp.cos(ang), jnp.sin(ang)
    return jnp.concatenate([cos, cos], axis=1), jnp.concatenate([-sin, sin], axis=1)


def _band_mask(r0):
    qi = lax.broadcasted_iota(jnp.int32, (128, 256), 0)
    kk = lax.broadcasted_iota(jnp.int32, (128, 256), 1)
    return (kk >= qi) & (kk <= qi + 128) & (kk + r0 >= 128)


def _attn_fwd(proj, cosf, sinf, g):
    d = DIL[g]
    ln = S // d
    nblk = ln // 128
    ncb = NC // HD
    proj_v = proj.reshape(NB, ln, d * NC)
    cos_v = cosf.reshape(ln, d * HD)
    sin_v = sinf.reshape(ln, d * HD)

    def body(q_ref, k_ref, v_ref, cos_ref, sin_ref, o_ref, l_ref, k_s, v_s):
        k_s[pl.ds(0, 128), :] = jnp.zeros((128, HD), BF16)
        v_s[pl.ds(0, 128), :] = jnp.zeros((128, HD), BF16)

        def prep(n, carry):
            r0 = pl.multiple_of(n * 128, 128)
            rows = pl.ds(r0, 128)
            k_s[pl.ds(r0 + 128, 128), :] = _rope(k_ref[rows, :], cos_ref[rows, :], sin_ref[rows, :], HD // 2).astype(BF16)
            v_s[pl.ds(r0 + 128, 128), :] = v_ref[rows, :].astype(BF16)
            return carry

        lax.fori_loop(0, nblk, prep, 0)

        def step(n, carry):
            r0 = pl.multiple_of(n * 128, 128)
            rows = pl.ds(r0, 128)
            qr = _rope(q_ref[rows, :], cos_ref[rows, :], sin_ref[rows, :], HD // 2).astype(BF16)
            kw = k_s[pl.ds(r0, 256), :]
            vw = v_s[pl.ds(r0, 256), :]
            sc = lax.dot_general(qr, kw, NT, preferred_element_type=F32) * A_SCALE
            sc = jnp.where(_band_mask(r0), sc, NEG)
            m = jnp.max(sc, axis=1, keepdims=True)
            p = jnp.exp(sc - m)
            l = jnp.sum(p, axis=1, keepdims=True)
            o_ref[rows, :] = jnp.dot(p.astype(BF16), vw, preferred_element_type=F32) / l
            l_ref[rows, :] = jnp.broadcast_to(m + jnp.log(l), (128, HD))
            return carry

        lax.fori_loop(0, nblk, step, 0)

    def col(off):
        return lambda b, r, h: (b, 0, r * ncb + off // HD + HPG * g + h)

    blk = (None, ln, HD)
    tab = pl.BlockSpec((ln, HD), lambda b, r, h: (0, r))
    out = pl.BlockSpec(blk, lambda b, r, h: (b, 0, r * HPG + h))
    o, l = pl.pallas_call(
        body, name=f"attn_fwd_d{d}", grid=(NB, d, HPG),
        in_specs=[pl.BlockSpec(blk, col(QA)), pl.BlockSpec(blk, col(KA)), pl.BlockSpec(blk, col(VA)), tab, tab],
        out_specs=[out, out],
        out_shape=[jax.ShapeDtypeStruct((NB, ln, d * HPG * HD), F32)] * 2,
        scratch_shapes=[pltpu.VMEM((ln + 128, HD), BF16), pltpu.VMEM((ln + 128, HD), BF16)],
        compiler_params=_cp(("parallel", "parallel", "parallel")),
    )(proj_v, proj_v, proj_v, cos_v, sin_v)
    return o.reshape(T, HPG * HD), l.reshape(T, HPG * HD)


def _attn_bwd(proj, cosf, sinf, da, lse, delta, dproj, g):
    d = DIL[g]
    ln = S // d
    nblk = ln // 128
    ncb = NC // HD
    proj_v = proj.reshape(NB, ln, d * NC)
    dproj_v = dproj.reshape(NB, ln, d * NC)
    cos_v = cosf.reshape(ln, d * HD)
    sin_v = sinf.reshape(ln, d * HD)
    da_v = da.reshape(NB, ln, d * AW)
    lse_v = lse.reshape(NB, ln, d * HPG * HD)
    delta_v = delta.reshape(NB, ln, d * HPG * HD)

    def body(q_ref, k_ref, v_ref, cos_ref, sin_ref, da_ref, lse_ref, dl_ref, dp_in_ref, o_ref,
             k_s, v_s, dq_s, dk_s, dv_s):
        del dp_in_ref
        w = pl.program_id(3)

        @pl.when(w == 0)
        def _():
            k_s[pl.ds(0, 128), :] = jnp.zeros((128, HD), BF16)
            v_s[pl.ds(0, 128), :] = jnp.zeros((128, HD), BF16)
            dk_s[...] = jnp.zeros_like(dk_s)
            dv_s[...] = jnp.zeros_like(dv_s)

            def prep(n, carry):
                r0 = pl.multiple_of(n * 128, 128)
                rows = pl.ds(r0, 128)
                k_s[pl.ds(r0 + 128, 128), :] = _rope(k_ref[rows, :], cos_ref[rows, :], sin_ref[rows, :], HD // 2).astype(BF16)
                v_s[pl.ds(r0 + 128, 128), :] = v_ref[rows, :].astype(BF16)
                return carry

            lax.fori_loop(0, nblk, prep, 0)

            def step(n, carry):
                r0 = pl.multiple_of(n * 128, 128)
                rows = pl.ds(r0, 128)
                win = pl.ds(r0, 256)
                cos, sin = cos_ref[rows, :], sin_ref[rows, :]
                qr = _rope(q_ref[rows, :], cos, sin, HD // 2).astype(BF16)
                kw = k_s[win, :]
                vw = v_s[win, :]
                sc = lax.dot_general(qr, kw, NT, preferred_element_type=F32) * A_SCALE
                sc = jnp.where(_band_mask(r0), sc, NEG)
                p = jnp.exp(sc - lse_ref[rows, :][:, :1])
                da_b = da_ref[rows, :].astype(BF16)
                dp = lax.dot_general(da_b, vw, NT, preferred_element_type=F32)
                ds_b = (p * (dp - dl_ref[rows, :][:, :1]) * A_SCALE).astype(BF16)
                p_b = p.astype(BF16)
                dq = jnp.dot(ds_b, kw, preferred_element_type=F32)
                dq_s[rows, :] = _rope_t(dq, cos, sin, HD // 2)
                dk_s[win, :] += lax.dot_general(ds_b, qr, TN, preferred_element_type=F32)
                dv_s[win, :] += lax.dot_general(p_b, da_b, TN, preferred_element_type=F32)
                return carry

            lax.fori_loop(0, nblk, step, 0)

            def fin(n, carry):
                r0 = pl.multiple_of(n * 128, 128)
                rows = pl.ds(r0, 128)
                o_ref[rows, :] = dq_s[rows, :].astype(BF16)
                hi = pl.ds(r0 + 128, 128)
                dk_s[hi, :] = _rope_t(dk_s[hi, :], cos_ref[rows, :], sin_ref[rows, :], HD // 2)
                return carry

            lax.fori_loop(0, nblk, fin, 0)

        @pl.when(w == 1)
        def _():
            o_ref[...] = dk_s[pl.ds(128, ln), :].astype(BF16)

        @pl.when(w == 2)
        def _():
            o_ref[...] = dv_s[pl.ds(128, ln), :].astype(BF16)

    def col(off):
        return lambda b, r, h, w: (b, 0, r * ncb + off // HD + HPG * g + h)

    blk = (None, ln, HD)
    tab = pl.BlockSpec((ln, HD), lambda b, r, h, w: (0, r))
    per_head = pl.BlockSpec(blk, lambda b, r, h, w: (b, 0, r * HPG + h))
    out = pl.pallas_call(
        body, name=f"attn_bwd_d{d}", grid=(NB, d, HPG, 3),
        in_specs=[pl.BlockSpec(blk, col(QA)), pl.BlockSpec(blk, col(KA)), pl.BlockSpec(blk, col(VA)), tab, tab,
                  pl.BlockSpec(blk, lambda b, r, h, w: (b, 0, r * (AW // HD) + HPG * g + h)),
                  per_head, per_head, pl.BlockSpec(memory_space=pl.ANY)],
        out_specs=pl.BlockSpec(blk, lambda b, r, h, w: (b, 0, r * ncb + (AW // HD) * w + HPG * g + h)),
        out_shape=jax.ShapeDtypeStruct(dproj_v.shape, BF16),
        input_output_aliases={8: 0},
        scratch_shapes=[pltpu.VMEM((ln + 128, HD), BF16), pltpu.VMEM((ln + 128, HD), BF16),
                        pltpu.VMEM((ln, HD), F32), pltpu.VMEM((ln + 128, HD), F32), pltpu.VMEM((ln + 128, HD), F32)],
        compiler_params=_cp(("arbitrary",) * 4),
    )(proj_v, proj_v, proj_v, cos_v, sin_v, da_v, lse_v, delta_v, dproj_v)
    return out.reshape(T, NC)


def _attn_mix(proj, os_, ls_):
    tr = 256
    gw = HPG * HD

    def body(o0, o1, o2, l0, l1, l2, z_ref, cat_ref):
        m = jnp.maximum(jnp.maximum(l0[...], l1[...]), l2[...])
        e = [jnp.exp(l[...] - m) for l in (l0, l1, l2)]
        inv = 1.0 / (e[0] + e[1] + e[2])
        for gi, o in enumerate((o0, o1, o2)):
            z = z_ref[:, gi * gw:(gi + 1) * gw]
            cat_ref[:, gi * gw:(gi + 1) * gw] = (o[...] * (e[gi] * inv) * (z * _sigmoid(z))).astype(BF16)

    grp = pl.BlockSpec((tr, gw), lambda i: (i, 0))
    return pl.pallas_call(
        body, name="attn_mix", grid=(T // tr,),
        in_specs=[grp] * 6 + [pl.BlockSpec((tr, AW), lambda i: (i, ZA // AW))],
        out_specs=pl.BlockSpec((tr, AW), lambda i: (i, 0)),
        out_shape=jax.ShapeDtypeStruct((T, D), BF16),
        compiler_params=_cp(("parallel",)),
    )(*os_, *ls_, proj)


def _attn_mix_bwd(dcat, proj, os_, ls_):
    tr = 256
    gw = HPG * HD

    def body(dy_ref, o0, o1, o2, l0, l1, l2, z_ref, da_ref, lse_ref, dl_ref, dz_ref):
        m = jnp.maximum(jnp.maximum(l0[...], l1[...]), l2[...])
        e = [jnp.exp(l[...] - m) for l in (l0, l1, l2)]
        den = e[0] + e[1] + e[2]
        inv = 1.0 / den
        lse_ref[...] = m + jnp.log(den)
        acc = jnp.zeros((tr, gw), F32)
        for gi, o in enumerate((o0, o1, o2)):
            cols = slice(gi * gw, (gi + 1) * gw)
            z = z_ref[:, cols]
            dy = dy_ref[:, cols]
            sg = _sigmoid(z)
            a = o[...] * (e[gi] * inv)
            da = dy * (z * sg)
            da_ref[:, cols] = da
            dz_ref[:, cols] = (dy * a * (sg * (1.0 + z * (1.0 - sg)))).astype(BF16)
            acc = acc + da * a
        for hh in range(HPG):
            cols = slice(hh * HD, (hh + 1) * HD)
            dl_ref[:, cols] = jnp.broadcast_to(jnp.sum(acc[:, cols], axis=1, keepdims=True), (tr, HD))

    grp = pl.BlockSpec((tr, gw), lambda i: (i, 0))
    return pl.pallas_call(
        body, name="attn_mix_bwd", grid=(T // tr,),
        in_specs=[pl.BlockSpec((tr, AW), lambda i: (i, 0))] + [grp] * 6 + [pl.BlockSpec((tr, AW), lambda i: (i, ZA // AW))],
        out_specs=[pl.BlockSpec((tr, AW), lambda i: (i, 0)), grp, grp, pl.BlockSpec((tr, AW), lambda i: (i, ZA // AW))],
        out_shape=[jax.ShapeDtypeStruct((T, AW), F32), jax.ShapeDtypeStruct((T, gw), F32),
                   jax.ShapeDtypeStruct((T, gw), F32), jax.ShapeDtypeStruct((T, NC), BF16)],
        compiler_params=_cp(("parallel",)),
    )(dcat, *os_, *ls_, proj)


CT = 128


def _shift_down(x, n):
    rows = lax.broadcasted_iota(jnp.int32, x.shape, 0)
    return jnp.where(rows >= n, pltpu.roll(x, n, 0), 0.0)


def _shift_up(x, n):
    rows = lax.broadcasted_iota(jnp.int32, x.shape, 0)
    return jnp.where(rows < x.shape[0] - n, pltpu.roll(x, x.shape[0] - n, 0), 0.0)


def _conv_fwd(proj, conv_w, cat):
    proj_v = proj.reshape(NB, S, NC)
    cat_v = cat.reshape(NB, S, D)

    def body(u_ref, b_ref, c_ref, z_ref, w_ref, cat_in, o_ref):
        del cat_in
        cu = c_ref[...] * u_ref[...]
        cv = _shift_down(cu, 2) * w_ref[0:1, :] + _shift_down(cu, 1) * w_ref[1:2, :] + cu * w_ref[2:3, :]
        z = z_ref[...]
        o_ref[...] = (b_ref[...] * cv * (z * _sigmoid(z))).astype(BF16)

    def seg(off):
        return pl.BlockSpec((None, S, CT), lambda b, j: (b, 0, off // CT + j))

    out = pl.pallas_call(
        body, name="conv_fwd", grid=(NB, CW // CT),
        in_specs=[seg(UC), seg(BC), seg(CC), seg(ZC), pl.BlockSpec((3, CT), lambda b, j: (0, j)),
                  pl.BlockSpec(memory_space=pl.ANY)],
        out_specs=pl.BlockSpec((None, S, CT), lambda b, j: (b, 0, AW // CT + j)),
        out_shape=jax.ShapeDtypeStruct((NB, S, D), BF16),
        input_output_aliases={5: 0},
        compiler_params=_cp(("parallel", "parallel")),
    )(proj_v, proj_v, proj_v, proj_v, conv_w, cat_v)
    return out.reshape(T, D)


def _conv_bwd(dcat, proj, conv_w, dproj):
    proj_v = proj.reshape(NB, S, NC)
    dproj_v = dproj.reshape(NB, S, NC)
    dcat_v = dcat.reshape(NB, S, D)

    def body(dy_ref, u_ref, b_ref, c_ref, z_ref, w_ref, dp_in, o_ref, dw_ref, st):
        del dp_in
        b = pl.program_id(1)
        w = pl.program_id(2)

        @pl.when((b == 0) & (w == 0))
        def _():
            dw_ref[...] = jnp.zeros_like(dw_ref)

        @pl.when(w == 0)
        def _():
            u, c, z, bb, dy = u_ref[...], c_ref[...], z_ref[...], b_ref[...], dy_ref[...]
            cu = c * u
            s1 = _shift_down(cu, 1)
            s2 = _shift_down(cu, 2)
            cv = s2 * w_ref[0:1, :] + s1 * w_ref[1:2, :] + cu * w_ref[2:3, :]
            sg = _sigmoid(z)
            sz = z * sg
            dcv = dy * bb * sz
            st[0] = dy * cv * sz
            st[2] = dy * bb * cv * (sg * (1.0 + z * (1.0 - sg)))
            dw_ref[0:1, :] += jnp.sum(dcv * s2, axis=0, keepdims=True)
            dw_ref[1:2, :] += jnp.sum(dcv * s1, axis=0, keepdims=True)
            dw_ref[2:3, :] += jnp.sum(dcv * cu, axis=0, keepdims=True)
            dcu = dcv * w_ref[2:3, :] + _shift_up(dcv, 1) * w_ref[1:2, :] + _shift_up(dcv, 2) * w_ref[0:1, :]
            st[1] = dcu * u
            o_ref[...] = (dcu * c).astype(BF16)

        for k in range(3):
            @pl.when(w == k + 1)
            def _(k=k):
                o_ref[...] = st[k].astype(BF16)

    def seg(off):
        return pl.BlockSpec((None, S, CT), lambda j, b, w: (b, 0, off // CT + j))

    out, dw = pl.pallas_call(
        body, name="conv_bwd", grid=(CW // CT, NB, 4),
        in_specs=[pl.BlockSpec((None, S, CT), lambda j, b, w: (b, 0, AW // CT + j)),
                  seg(UC), seg(BC), seg(CC), seg(ZC), pl.BlockSpec((3, CT), lambda j, b, w: (0, j)),
                  pl.BlockSpec(memory_space=pl.ANY)],
        out_specs=[pl.BlockSpec((None, S, CT), lambda j, b, w: (b, 0, (UC + w * CW) // CT + j)),
                   pl.BlockSpec((3, CT), lambda j, b, w: (0, j))],
        out_shape=[jax.ShapeDtypeStruct((NB, S, NC), BF16), jax.ShapeDtypeStruct((3, CW), F32)],
        input_output_aliases={6: 0},
        scratch_shapes=[pltpu.VMEM((3, S, CT), F32)],
        compiler_params=_cp(("arbitrary",) * 3),
    )(dcat_v, proj_v, proj_v, proj_v, proj_v, conv_w, dproj_v)
    return out.reshape(T, NC), dw


XT = 512


def _cross_fwd(proj, mkv, cosq, sinq, cosm, sinm, cat):
    proj_v = proj.reshape(NB, S, NC)
    mkv_v = mkv.reshape(NB, MLEN, 2 * XW)
    cat_v = cat.reshape(NB, S, D)

    def body(q_ref, z_ref, mk_ref, mv_ref, cq, sq, cm, sm, cat_in, o_ref):
        del cat_in
        mkr = _rope(mk_ref[...], cm[...], sm[...], XHD // 2).astype(BF16)
        qr = _rope(q_ref[...], cq[...], sq[...], XHD // 2).astype(BF16)
        sc = lax.dot_general(qr, mkr, NT, preferred_element_type=F32) * X_SCALE
        p = jnp.exp(sc - jnp.max(sc, axis=1, keepdims=True))
        p = p / jnp.sum(p, axis=1, keepdims=True)
        ox = jnp.dot(p.astype(BF16), mv_ref[...].astype(BF16), preferred_element_type=F32)
        z = z_ref[...]
        o_ref[...] = (ox * (z * _sigmoid(z))).astype(BF16)

    def seg(off):
        return pl.BlockSpec((None, XT, XHD), lambda b, h, t: (b, t, off // XHD + h))

    qtab = pl.BlockSpec((XT, XHD), lambda b, h, t: (t, 0))
    mtab = pl.BlockSpec((MLEN, XHD), lambda b, h, t: (0, 0))
    out = pl.pallas_call(
        body, name="cross_fwd", grid=(NB, NXH, S // XT),
        in_specs=[seg(QX), seg(ZX),
                  pl.BlockSpec((None, MLEN, XHD), lambda b, h, t: (b, 0, h)),
                  pl.BlockSpec((None, MLEN, XHD), lambda b, h, t: (b, 0, NXH + h)),
                  qtab, qtab, mtab, mtab, pl.BlockSpec(memory_space=pl.ANY)],
        out_specs=pl.BlockSpec((None, XT, XHD), lambda b, h, t: (b, t, (AW + CW) // XHD + h)),
        out_shape=jax.ShapeDtypeStruct((NB, S, D), BF16),
        input_output_aliases={8: 0},
        compiler_params=_cp(("parallel",) * 3),
    )(proj_v, proj_v, mkv_v, mkv_v, cosq, sinq, cosm, sinm, cat_v)
    return out.reshape(T, D)


def _cross_bwd(dcat, proj, mkv, cosq, sinq, cosm, sinm, dproj):
    proj_v = proj.reshape(NB, S, NC)
    dproj_v = dproj.reshape(NB, S, NC)
    dcat_v = dcat.reshape(NB, S, D)
    mkv_v = mkv.reshape(NB, MLEN, 2 * XW)
    nt = S // XT

    def body(dy_ref, q_ref, z_ref, mk_ref, mv_ref, cq, sq, cm, sm, dp_in, o_ref, dmk_ref, dmv_ref, dz_s):
        del dp_in
        t = pl.program_id(2)
        w = pl.program_id(3)

        @pl.when((t == 0) & (w == 0))
        def _():
            dmk_ref[...] = jnp.zeros_like(dmk_ref)
            dmv_ref[...] = jnp.zeros_like(dmv_ref)

        @pl.when(w == 0)
        def _():
            mkr = _rope(mk_ref[...], cm[...], sm[...], XHD // 2).astype(BF16)
            mv_b = mv_ref[...].astype(BF16)
            qr = _rope(q_ref[...], cq[...], sq[...], XHD // 2).astype(BF16)
            sc = lax.dot_general(qr, mkr, NT, preferred_element_type=F32) * X_SCALE
            p = jnp.exp(sc - jnp.max(sc, axis=1, keepdims=True))
            p = p / jnp.sum(p, axis=1, keepdims=True)
            p_b = p.astype(BF16)
            ox = jnp.dot(p_b, mv_b, preferred_element_type=F32)
            z = z_ref[...]
            dy = dy_ref[...]
            sg = _sigmoid(z)
            dz_s[...] = dy * ox * (sg * (1.0 + z * (1.0 - sg)))
            dox_b = (dy * (z * sg)).astype(BF16)
            dp = lax.dot_general(dox_b, mv_b, NT, preferred_element_type=F32)
            ds_b = (p * (dp - jnp.sum(dp * p, axis=1, keepdims=True)) * X_SCALE).astype(BF16)
            dq = jnp.dot(ds_b, mkr, preferred_element_type=F32)
            o_ref[...] = _rope_t(dq, cq[...], sq[...], XHD // 2).astype(BF16)
            dmk_ref[...] += lax.dot_general(ds_b, qr, TN, preferred_element_type=F32)
            dmv_ref[...] += lax.dot_general(p_b, dox_b, TN, preferred_element_type=F32)

        @pl.when(w == 1)
        def _():
            o_ref[...] = dz_s[...].astype(BF16)

        @pl.when((t == nt - 1) & (w == 1))
        def _():
            dmk_ref[...] = _rope_t(dmk_ref[...], cm[...], sm[...], XHD // 2)

    def seg(off):
        return pl.BlockSpec((None, XT, XHD), lambda b, h, t, w: (b, t, off // XHD + h))

    qtab = pl.BlockSpec((XT, XHD), lambda b, h, t, w: (t, 0))
    mtab = pl.BlockSpec((MLEN, XHD), lambda b, h, t, w: (0, 0))
    macc = pl.BlockSpec((None, MLEN, XHD), lambda b, h, t, w: (b, 0, h))
    out, dmk, dmv = pl.pallas_call(
        body, name="cross_bwd", grid=(NB, NXH, nt, 2),
        in_specs=[pl.BlockSpec((None, XT, XHD), lambda b, h, t, w: (b, t, (AW + CW) // XHD + h)),
                  seg(QX), seg(ZX),
                  pl.BlockSpec((None, MLEN, XHD), lambda b, h, t, w: (b, 0, h)),
                  pl.BlockSpec((None, MLEN, XHD), lambda b, h, t, w: (b, 0, NXH + h)),
                  qtab, qtab, mtab, mtab, pl.BlockSpec(memory_space=pl.ANY)],
        out_specs=[pl.BlockSpec((None, XT, XHD), lambda b, h, t, w: (b, t, (QX + w * XW) // XHD + h)), macc, macc],
        out_shape=[jax.ShapeDtypeStruct((NB, S, NC), BF16), jax.ShapeDtypeStruct((NB, MLEN, XW), F32),
                   jax.ShapeDtypeStruct((NB, MLEN, XW), F32)],
        input_output_aliases={9: 0},
        scratch_shapes=[pltpu.VMEM((XT, XHD), F32)],
        compiler_params=_cp(("arbitrary",) * 4),
    )(dcat_v, proj_v, proj_v, mkv_v, mkv_v, cosq, sinq, cosm, sinm, dproj_v)
    return out.reshape(T, NC), dmk, dmv


def _local_step(x, mem, pre_norm, wg_in, conv_w, mem_norm, wg_kv, wg_out, post_norm, tgt):
    x2 = x.reshape(T, D)
    mem2 = mem.reshape(NB * MLEN, D)
    tgt2 = tgt.reshape(T, D)
    cosa, sina = _rope_tables(jnp.arange(S), HD // 2)
    cosq, sinq = _rope_tables(jnp.arange(S) + MLEN, XHD // 2)
    cosm, sinm = _rope_tables(jnp.arange(MLEN), XHD // 2)

    h = _rms_fwd(x2, pre_norm, "pre_norm_fwd")
    proj = _proj(h, wg_in)
    memn = _rms_fwd(mem2, mem_norm, "mem_norm_fwd")
    mkv = _mm_nn(memn, wg_kv, "mkv", NB * MLEN, 1024, 512)

    fw = [_attn_fwd(proj, cosa, sina, g) for g in range(3)]
    os_ = [f[0] for f in fw]
    ls_ = [f[1] for f in fw]
    cat = _attn_mix(proj, os_, ls_)
    cat = _conv_fwd(proj, conv_w, cat)
    cat = _cross_fwd(proj, mkv, cosq, sinq, cosm, sinm, cat)
    y = _mm_nn(cat, wg_out, "out_proj", 1024, 1024, 512)
    dy, dout, d_post, loss = _post_norm_loss(y, x2, tgt2, post_norm)

    dcat = _mm_nt(dy, wg_out, "dcat", 1024, 1024, 512)
    g_out = _mm_tn(cat, dy, "grad_w_out", 1024, 1024, 512)
    da, lse, delta, dproj = _attn_mix_bwd(dcat, proj, os_, ls_)
    for g in range(3):
        dproj = _attn_bwd(proj, cosa, sina, da, lse, delta, dproj, g)
    dproj, d_conv = _conv_bwd(dcat, proj, conv_w, dproj)
    dproj, dmk, dmv = _cross_bwd(dcat, proj, mkv, cosq, sinq, cosm, sinm, dproj)

    dmkv = jnp.concatenate([dmk, dmv], axis=-1).reshape(NB * MLEN, 2 * XW)
    dmkv_b = dmkv.astype(BF16)
    g_kv = _mm_tn(memn, dmkv_b, "grad_w_mem_kv", 1024, 1024, NB * MLEN)
    dmemn = _mm_nt(dmkv_b, wg_kv, "dmemn", NB * MLEN, 1024, 512)
    d_mem = _norm_gain_grad(dmemn, mem2, "mem_norm_bwd")

    g_in = _grad_w_in(h, dproj)
    dh = _dh(dproj, wg_in)
    grad_x, d_pre = _pre_norm_bwd(dh, x2, pre_norm, dout)
    return loss, grad_x.reshape(NB, S, D), d_pre, d_mem, d_post, d_conv, g_in, g_kv, g_out


def _adamw(w, g, m, v, name):
    rows, cols = w.shape
    tr = rows if rows <= 512 else 512
    tc = cols if cols <= 1024 else 1024
    if cols % tc:
        tc = 896

    def body(w_ref, g_ref, m_ref, v_ref, d_ref, nm_ref, nv_ref):
        gv = g_ref[...]
        nm = ADAM_B1 * m_ref[...] + (1.0 - ADAM_B1) * gv
        nv = ADAM_B2 * v_ref[...] + (1.0 - ADAM_B2) * (gv * gv)
        m_hat = nm / (1.0 - ADAM_B1 ** ADAM_STEP)
        v_hat = nv / (1.0 - ADAM_B2 ** ADAM_STEP)
        d_ref[...] = -ADAM_LR * (m_hat / (jnp.sqrt(v_hat) + ADAM_EPS) + ADAM_WD * w_ref[...])
        nm_ref[...] = nm
        nv_ref[...] = nv

    blk = pl.BlockSpec((tr, tc), lambda i, j: (i, j))
    sds = jax.ShapeDtypeStruct((rows, cols), F32)
    return pl.pallas_call(
        body, name=name, grid=(rows // tr, cols // tc),
        in_specs=[blk] * 4, out_specs=[blk] * 3, out_shape=[sds] * 3,
        compiler_params=_cp(("parallel", "parallel")),
    )(w, g, m, v)


def _place():
    return lax.axis_index("x"), lax.axis_index("y"), lax.axis_index("c")


def _other_chips(x, y):
    return [(1 - x, y), (x, 1 - y), (1 - x, 1 - y)]


def _gather_weights(shards):
    n = len(shards)

    def body(*refs):
        ins, outs = refs[:n], refs[n:2 * n]
        send_sems, recv_sems, local_sems = refs[2 * n:]
        x, y, c = _place()
        me_chip = 2 * x + y
        sibling = (x, y, 1 - c)
        chips = _other_chips(x, y)

        def half(ref, chip, hc):
            hr = ref.shape[1] // 2
            return ref.at[chip, pl.ds(hc * hr, hr), :]

        def src_half(ref, hc):
            hr = ref.shape[0] // 2
            return ref.at[pl.ds(hc * hr, hr), :]

        def copy(a, k, src, dst, to):
            return pltpu.make_async_remote_copy(src_ref=src, dst_ref=dst, send_sem=send_sems.at[6 * a + k],
                                                recv_sem=recv_sems.at[6 * a + k], device_id=to, device_id_type=MESH)

        mine = [pltpu.make_async_copy(ins[a], outs[a].at[me_chip], local_sems.at[a]) for a in range(n)]
        for cp in mine:
            cp.start()
        first = []
        for a in range(n):
            for j, (px, py) in enumerate(chips):
                first.append(copy(a, j, src_half(ins[a], c), half(outs[a], me_chip, c), (px, py, c)))
        for cp in first:
            cp.start()
        passed = []
        for a in range(n):
            for j, (px, py) in enumerate(chips):
                blk = half(outs[a], 2 * px + py, c)
                copy(a, j, blk, blk, (px, py, c)).wait_recv()
                fwd = copy(a, 3 + j, blk, blk, sibling)
                fwd.start()
                passed.append(fwd)
        for a in range(n):
            for j, (px, py) in enumerate(chips):
                blk = half(outs[a], 2 * px + py, 1 - c)
                copy(a, 3 + j, blk, blk, sibling).wait_recv()
        for cp in first + passed:
            cp.wait_send()
        for cp in mine:
            cp.wait()

    hbm = pl.BlockSpec(memory_space=pl.ANY)
    return pl.pallas_call(
        body, name="gather_weights",
        in_specs=[hbm] * n, out_specs=[hbm] * n,
        out_shape=[jax.ShapeDtypeStruct((NCHIP,) + s.shape, s.dtype) for s in shards],
        scratch_shapes=[pltpu.SemaphoreType.DMA((6 * n,)), pltpu.SemaphoreType.DMA((6 * n,)), pltpu.SemaphoreType.DMA((n,))],
    )(*shards)


def _sibling_halves(parts):
    n = len(parts)

    def body(*refs):
        ins, outs = refs[:n], refs[n:2 * n]
        send_sems, recv_sems = refs[2 * n:]
        x, y, c = _place()
        cps = []
        for a in range(n):
            hr = ins[a].shape[1] // 2
            cps.append(pltpu.make_async_remote_copy(
                src_ref=ins[a].at[:, pl.ds((1 - c) * hr, hr), :], dst_ref=outs[a],
                send_sem=send_sems.at[a], recv_sem=recv_sems.at[a], device_id=(x, y, 1 - c), device_id_type=MESH))
        for cp in cps:
            cp.start()
        for cp in cps:
            cp.wait()

    hbm = pl.BlockSpec(memory_space=pl.ANY)
    return pl.pallas_call(
        body, name="sibling_halves",
        in_specs=[hbm] * n, out_specs=[hbm] * n,
        out_shape=[jax.ShapeDtypeStruct((NCHIP, p.shape[1] // 2, p.shape[2]), p.dtype) for p in parts],
        scratch_shapes=[pltpu.SemaphoreType.DMA((n,)), pltpu.SemaphoreType.DMA((n,))],
    )(*parts)


def _pair_sum(part, got, name):
    nblk, rows, cols = part.shape
    hr = rows // 2
    tr = 512
    tc = cols if cols <= 1024 else (1024 if cols % 1024 == 0 else 896)
    c = lax.axis_index("c")
    nrt = hr // tr

    def body(c_ref, p_ref, g_ref, o_ref):
        del c_ref
        o_ref[...] = (p_ref[...] + g_ref[...]).astype(BF16)

    grid_spec = pltpu.PrefetchScalarGridSpec(
        num_scalar_prefetch=1, grid=(nblk, nrt, cols // tc),
        in_specs=[pl.BlockSpec((None, tr, tc), lambda b, i, j, cr: (b, cr[0] * nrt + i, j)),
                  pl.BlockSpec((None, tr, tc), lambda b, i, j, cr: (b, i, j))],
        out_specs=pl.BlockSpec((None, tr, tc), lambda b, i, j, cr: (b, i, j)))
    return pl.pallas_call(
        body, name=name, grid_spec=grid_spec,
        out_shape=jax.ShapeDtypeStruct((nblk, hr, cols), BF16),
        compiler_params=_cp(("parallel",) * 3),
    )(jnp.reshape(c, (1,)).astype(jnp.int32), part, got)


def _scatter_to_chips(qs):
    n = len(qs)

    def body(*refs):
        ins, outs = refs[:n], refs[n:2 * n]
        send_sems, recv_sems, local_sems = refs[2 * n:]
        x, y, c = _place()
        me_chip = 2 * x + y
        chips = _other_chips(x, y)
        mine = [pltpu.make_async_copy(ins[a].at[me_chip], outs[a].at[me_chip], local_sems.at[a]) for a in range(n)]
        for cp in mine:
            cp.start()
        cps = []
        for a in range(n):
            for j, (px, py) in enumerate(chips):
                cps.append(pltpu.make_async_remote_copy(
                    src_ref=ins[a].at[2 * px + py], dst_ref=outs[a].at[me_chip],
                    send_sem=send_sems.at[3 * a + j], recv_sem=recv_sems.at[3 * a + j], device_id=(px, py, c), device_id_type=MESH))
        for cp in cps:
            cp.start()
        for a in range(n):
            for j, (px, py) in enumerate(chips):
                blk = outs[a].at[2 * px + py]
                pltpu.make_async_remote_copy(src_ref=blk, dst_ref=blk, send_sem=send_sems.at[3 * a + j], recv_sem=recv_sems.at[3 * a + j],
                                             device_id=(px, py, c), device_id_type=MESH).wait_recv()
        for cp in cps:
            cp.wait_send()
        for cp in mine:
            cp.wait()

    hbm = pl.BlockSpec(memory_space=pl.ANY)
    return pl.pallas_call(
        body, name="scatter_to_chips",
        in_specs=[hbm] * n, out_specs=[hbm] * n,
        out_shape=[jax.ShapeDtypeStruct(q.shape, q.dtype) for q in qs],
        scratch_shapes=[pltpu.SemaphoreType.DMA((3 * n,)), pltpu.SemaphoreType.DMA((3 * n,)), pltpu.SemaphoreType.DMA((n,))],
    )(*qs)


def _chip_sum(got, name):
    _, hr, cols = got.shape
    tr = 512
    tc = cols if cols <= 1024 else (1024 if cols % 1024 == 0 else 896)

    def body(g_ref, o_ref):
        acc = g_ref[0].astype(F32)
        for i in range(1, NCHIP):
            acc = acc + g_ref[i].astype(F32)
        o_ref[...] = acc

    return pl.pallas_call(
        body, name=name, grid=(hr // tr, cols // tc),
        in_specs=[pl.BlockSpec((NCHIP, tr, tc), lambda i, j: (0, i, j))],
        out_specs=pl.BlockSpec((tr, tc), lambda i, j: (i, j)),
        out_shape=jax.ShapeDtypeStruct((hr, cols), F32),
        compiler_params=_cp(("parallel", "parallel")),
    )(got)


def _join_halves(halves):
    n = len(halves)

    def body(*refs):
        ins, outs = refs[:n], refs[n:2 * n]
        send_sems, recv_sems, local_sems = refs[2 * n:]
        x, y, c = _place()
        local, cps = [], []
        for a in range(n):
            hr = ins[a].shape[0]
            dst = outs[a].at[pl.ds(c * hr, hr), :]
            local.append(pltpu.make_async_copy(ins[a], dst, local_sems.at[a]))
            cps.append(pltpu.make_async_remote_copy(src_ref=ins[a], dst_ref=dst, send_sem=send_sems.at[a],
                                                    recv_sem=recv_sems.at[a], device_id=(x, y, 1 - c), device_id_type=MESH))
        for cp in local + cps:
            cp.start()
        for a in range(n):
            hr = ins[a].shape[0]
            theirs = outs[a].at[pl.ds((1 - c) * hr, hr), :]
            pltpu.make_async_remote_copy(src_ref=ins[a], dst_ref=theirs, send_sem=send_sems.at[a], recv_sem=recv_sems.at[a],
                                         device_id=(x, y, 1 - c), device_id_type=MESH).wait_recv()
        for cp in cps:
            cp.wait_send()
        for cp in local:
            cp.wait()

    hbm = pl.BlockSpec(memory_space=pl.ANY)
    return pl.pallas_call(
        body, name="join_halves",
        in_specs=[hbm] * n, out_specs=[hbm] * n,
        out_shape=[jax.ShapeDtypeStruct((2 * h.shape[0], h.shape[1]), h.dtype) for h in halves],
        scratch_shapes=[pltpu.SemaphoreType.DMA((n,)), pltpu.SemaphoreType.DMA((n,)), pltpu.SemaphoreType.DMA((n,))],
    )(*halves)


N_DEV = 8


def _gather_small(block, reduce, name):
    m_per, cols = block.shape

    def body(x_ref, out_ref, all_ref, send_sems, recv_sems, local_sem):
        x, y, c = _place()
        me, sibling = (x, y, c), (x, y, 1 - c)
        chips = _other_chips(x, y)

        def rows(px, py, pc):
            return all_ref.at[pl.ds((4 * px + 2 * py + pc) * m_per, m_per), :]

        def copy(k, block_of, to, src=None):
            return pltpu.make_async_remote_copy(
                src_ref=rows(*block_of) if src is None else src, dst_ref=rows(*block_of),
                send_sem=send_sems.at[k], recv_sem=recv_sems.at[k], device_id=to, device_id_type=MESH)

        mine = pltpu.make_async_copy(x_ref, rows(*me), local_sem)
        mine.start()
        first = [copy(0, me, sibling, src=x_ref)]
        first += [copy(1 + j, me, (*chip, c), src=x_ref) for j, chip in enumerate(chips)]
        for cp in first:
            cp.start()
        passed = [copy(4 + j, (*chip, c), sibling) for j, chip in enumerate(chips)]
        for j, chip in enumerate(chips):
            copy(1 + j, (*chip, c), me).wait_recv()
            passed[j].start()
        copy(0, sibling, me).wait_recv()
        for j, chip in enumerate(chips):
            copy(4 + j, (*chip, 1 - c), me).wait_recv()
        for cp in first + passed:
            cp.wait_send()
        mine.wait()
        if reduce:
            acc = all_ref[pl.ds(0, m_per), :]
            for i in range(1, N_DEV):
                acc = acc + all_ref[pl.ds(i * m_per, m_per), :]
            out_ref[...] = acc
        else:
            out_ref[...] = all_ref[...]

    out_rows = m_per if reduce else N_DEV * m_per
    return pl.pallas_call(
        body, name=name,
        in_specs=[pl.BlockSpec(memory_space=pltpu.VMEM)],
        out_specs=pl.BlockSpec(memory_space=pltpu.VMEM),
        out_shape=jax.ShapeDtypeStruct((out_rows, cols), F32),
        scratch_shapes=[pltpu.VMEM((N_DEV * m_per, cols), F32), pltpu.SemaphoreType.DMA((7,)),
                        pltpu.SemaphoreType.DMA((7,)), pltpu.SemaphoreType.DMA],
    )(block)


def _reduce_scatter(parts):
    got = _sibling_halves(parts)
    qs = [_pair_sum(p, g, f"pair_sum_{i}") for i, (p, g) in enumerate(zip(parts, got))]
    landed = _scatter_to_chips(qs)
    halves = [_chip_sum(l, f"chip_sum_{i}") for i, l in enumerate(landed)]
    return _join_halves(halves)


def kernel(x, mem, pre_norm, w_in, conv_w, mem_norm, w_mem_kv, w_out, post_norm, loss_target, m_pre_norm, m_w_in, m_conv_w, m_mem_norm, m_w_mem_kv, m_w_out, m_post_norm, v_pre_norm, v_w_in, v_conv_w, v_mem_norm, v_w_mem_kv, v_w_out, v_post_norm):
    chip = 2 * lax.axis_index("x") + lax.axis_index("y")

    wg_in, wg_kv, wg_out = _gather_weights([w_in[0].astype(BF16), w_mem_kv[0].astype(BF16), w_out[0].astype(BF16)])
    wg_in = wg_in.reshape(NCHIP * D, WB)
    wg_kv = wg_kv.reshape(D, 2 * XW)
    wg_out = wg_out.reshape(D, D)
    cw_blk = jnp.zeros((8, 384), F32).at[:3].set(conv_w[0])
    cw_all = _gather_small(cw_blk, False, "gather_conv_w").reshape(NCHIP, 2, 8, 384)[:, 0, :3]
    conv_full = jnp.transpose(cw_all, (1, 0, 2)).reshape(3, CW)

    loss, grad_x, d_pre, d_mem, d_post, d_conv, g_in, g_kv, g_out = _local_step(
        x, mem, pre_norm, wg_in, conv_full, mem_norm, wg_kv, wg_out, post_norm, loss_target)

    r_in, r_kv, r_out = _reduce_scatter([g_in.reshape(NCHIP, D, WB), g_kv.reshape(NCHIP, D // NCHIP, 2 * XW),
                                         g_out.reshape(NCHIP, D // NCHIP, D)])

    pack = jnp.concatenate([d_pre, d_mem, d_post, jnp.pad(d_conv, ((0, 0), (0, D - CW))),
                            jnp.pad(loss, ((0, 0), (0, D - 128))), jnp.zeros((1, D), F32)], axis=0)
    tot = _gather_small(pack, True, "reduce_small")
    g_pre, g_mem, g_post = tot[0:1], tot[1:2], tot[2:3]
    g_conv = lax.dynamic_slice(tot[3:6, :CW], (0, chip * 384), (3, 384))
    loss_out = tot[6, 0]

    names = ("pre_norm", "w_in", "conv_w", "mem_norm", "w_mem_kv", "w_out", "post_norm")
    ws = (pre_norm, w_in[0], conv_w[0], mem_norm, w_mem_kv[0], w_out[0], post_norm)
    gs = (g_pre, r_in, g_conv, g_mem, r_kv, r_out, g_post)
    ms = (m_pre_norm, m_w_in[0], m_conv_w[0], m_mem_norm, m_w_mem_kv[0], m_w_out[0], m_post_norm)
    vs = (v_pre_norm, v_w_in[0], v_conv_w[0], v_mem_norm, v_w_mem_kv[0], v_w_out[0], v_post_norm)
    upd = [_adamw(w, g, m, v, "adamw_" + nm) for nm, w, g, m, v in zip(names, ws, gs, ms, vs)]

    def shaped(arrs):
        return [a.reshape(w.shape) if w.ndim == a.ndim else a.reshape((1,) + a.shape)
                for a, w in zip(arrs, (pre_norm, w_in, conv_w, mem_norm, w_mem_kv, w_out, post_norm))]

    grads = shaped(gs)
    deltas = shaped([u[0] for u in upd])
    new_m = shaped([u[1] for u in upd])
    new_v = shaped([u[2] for u in upd])
    return (loss_out, grad_x, *grads, *deltas, *new_m, *new_v)
```

```python
import functools

import jax
import jax.numpy as jnp
from jax import lax
from jax.experimental import pallas as pl
from jax.experimental.pallas import tpu as pltpu

F32 = jnp.float32
BF16 = jnp.bfloat16

D = 4096
S = 2048
NB = 2
T = NB * S
MLEN = 256
HD = 128
AW = 1536
CW = 1536
XW = 1024
XHD = 256
NXH = 4
NC = 14336
QA, KA, VA, ZA, UC, BC, CC, ZC, QX, ZX = 0, 1536, 3072, 4608, 6144, 7680, 9216, 10752, 12288, 13312
NCHIP = 4
WB = NC // NCHIP
DIL = (1, 4, 16)
HPG = 4
EPS = 1e-6
NEG = -1e30
ROPE_THETA = 10000.0
A_SCALE = HD ** -0.5
X_SCALE = XHD ** -0.5

ADAM_LR = 0.001
ADAM_B1 = 0.9
ADAM_B2 = 0.999
ADAM_EPS = 1e-08
ADAM_WD = 0.01
ADAM_STEP = 10

MESH = pl.DeviceIdType.MESH
MIB = 1024 * 1024


def _cp(sem, vmem_mib=48):
    return pltpu.CompilerParams(dimension_semantics=sem, vmem_limit_bytes=vmem_mib * MIB)


def _sigmoid(z):
    return 1.0 / (1.0 + jnp.exp(-z))


def _rope(x, cos, sin, half):
    return x * cos + pltpu.roll(x, half, 1) * sin


def _rope_t(g, cos, sin, half):
    return g * cos + pltpu.roll(g * sin, half, 1)


def _rms_fwd(x2, g, name):
    rows = x2.shape[0]
    tr = 256

    def body(x_ref, g_ref, o_ref):
        x = x_ref[...]
        r = lax.rsqrt(jnp.mean(x * x, axis=-1, keepdims=True) + EPS)
        o_ref[...] = (x * r * g_ref[...]).astype(BF16)

    return pl.pallas_call(
        body, name=name, grid=(rows // tr,),
        in_specs=[pl.BlockSpec((tr, D), lambda i: (i, 0)), pl.BlockSpec((1, D), lambda i: (0, 0))],
        out_specs=pl.BlockSpec((tr, D), lambda i: (i, 0)),
        out_shape=jax.ShapeDtypeStruct((rows, D), BF16),
        compiler_params=_cp(("parallel",)),
    )(x2, g)


def _norm_gain_grad(dn, x2, name):
    rows = x2.shape[0]
    tr = 256

    def body(dn_ref, x_ref, dg_ref):
        @pl.when(pl.program_id(0) == 0)
        def _():
            dg_ref[...] = jnp.zeros_like(dg_ref)
        x = x_ref[...]
        r = lax.rsqrt(jnp.mean(x * x, axis=-1, keepdims=True) + EPS)
        dg_ref[...] += jnp.sum(dn_ref[...] * (x * r), axis=0, keepdims=True)

    return pl.pallas_call(
        body, name=name, grid=(rows // tr,),
        in_specs=[pl.BlockSpec((tr, D), lambda i: (i, 0)), pl.BlockSpec((tr, D), lambda i: (i, 0))],
        out_specs=pl.BlockSpec((1, D), lambda i: (0, 0)),
        out_shape=jax.ShapeDtypeStruct((1, D), F32),
        compiler_params=_cp(("arbitrary",)),
    )(dn, x2)


def _pre_norm_bwd(dh, x2, g, dout):
    tr = 256

    def body(dh_ref, x_ref, g_ref, dout_ref, gx_ref, dg_ref):
        @pl.when(pl.program_id(0) == 0)
        def _():
            dg_ref[...] = jnp.zeros_like(dg_ref)
        x = x_ref[...]
        dh_ = dh_ref[...]
        r = lax.rsqrt(jnp.mean(x * x, axis=-1, keepdims=True) + EPS)
        xhat = x * r
        dg_ref[...] += jnp.sum(dh_ * xhat, axis=0, keepdims=True)
        dxn = dh_ * g_ref[...]
        gx_ref[...] = dout_ref[...] + r * (dxn - xhat * jnp.mean(dxn * xhat, axis=-1, keepdims=True))

    row = pl.BlockSpec((tr, D), lambda i: (i, 0))
    vec = pl.BlockSpec((1, D), lambda i: (0, 0))
    return pl.pallas_call(
        body, name="pre_norm_bwd", grid=(T // tr,),
        in_specs=[row, row, vec, row],
        out_specs=[row, vec],
        out_shape=[jax.ShapeDtypeStruct((T, D), F32), jax.ShapeDtypeStruct((1, D), F32)],
        compiler_params=_cp(("arbitrary",)),
    )(dh, x2, g, dout)


def _post_norm_loss(y, x2, tgt, g):
    tr = 256

    def body(y_ref, x_ref, t_ref, g_ref, dy_ref, dout_ref, dg_ref, loss_ref):
        @pl.when(pl.program_id(0) == 0)
        def _():
            dg_ref[...] = jnp.zeros_like(dg_ref)
            loss_ref[...] = jnp.zeros_like(loss_ref)
        yv = y_ref[...]
        gv = g_ref[...]
        r = lax.rsqrt(jnp.mean(yv * yv, axis=-1, keepdims=True) + EPS)
        yhat = yv * r
        err = x_ref[...] + yhat * gv - t_ref[...]
        loss_ref[...] += jnp.sum(jnp.sum(err * err, axis=1, keepdims=True), axis=0, keepdims=True) * (0.5 / D)
        dout = err * (1.0 / D)
        dout_ref[...] = dout
        dg_ref[...] += jnp.sum(dout * yhat, axis=0, keepdims=True)
        dyn = dout * gv
        dy_ref[...] = (r * (dyn - yhat * jnp.mean(dyn * yhat, axis=-1, keepdims=True))).astype(BF16)

    row = pl.BlockSpec((tr, D), lambda i: (i, 0))
    vec = pl.BlockSpec((1, D), lambda i: (0, 0))
    return pl.pallas_call(
        body, name="post_norm_loss", grid=(T // tr,),
        in_specs=[row, row, row, vec],
        out_specs=[row, row, vec, pl.BlockSpec((1, 128), lambda i: (0, 0))],
        out_shape=[jax.ShapeDtypeStruct((T, D), BF16), jax.ShapeDtypeStruct((T, D), F32),
                   jax.ShapeDtypeStruct((1, D), F32), jax.ShapeDtypeStruct((1, 128), F32)],
        compiler_params=_cp(("arbitrary",)),
    )(y, x2, tgt, g)


NN = (((1,), (0,)), ((), ()))
NT = (((1,), (1,)), ((), ()))
TN = (((0,), (0,)), ((), ()))


def _matmul(a, b, *, name, dims, grid, a_block, a_map, b_block, b_map, o_block, o_map, out_shape, out_dtype=F32):
    assert out_dtype == F32

    def body(a_ref, b_ref, o_ref):
        k = pl.program_id(2)
        part = lax.dot_general(a_ref[...], b_ref[...], dims, preferred_element_type=F32)

        @pl.when(k == 0)
        def _():
            o_ref[...] = part

        @pl.when(k > 0)
        def _():
            o_ref[...] += part

    return pl.pallas_call(
        body, name=name, grid=grid,
        in_specs=[pl.BlockSpec(a_block, a_map), pl.BlockSpec(b_block, b_map)],
        out_specs=pl.BlockSpec(o_block, o_map),
        out_shape=jax.ShapeDtypeStruct(out_shape, out_dtype),
        compiler_params=_cp(("parallel", "parallel", "arbitrary")),
    )(a, b)


def _mm_nn(a, b, name, tm, tn, tk):
    m, kd = a.shape
    n = b.shape[1]
    return _matmul(a, b, name=name, dims=NN, grid=(m // tm, n // tn, kd // tk),
                   a_block=(tm, tk), a_map=lambda i, j, k: (i, k),
                   b_block=(tk, tn), b_map=lambda i, j, k: (k, j),
                   o_block=(tm, tn), o_map=lambda i, j, k: (i, j), out_shape=(m, n))


def _mm_nt(a, b, name, tm, tn, tk):
    m, kd = a.shape
    n = b.shape[0]
    return _matmul(a, b, name=name, dims=NT, grid=(m // tm, n // tn, kd // tk),
                   a_block=(tm, tk), a_map=lambda i, j, k: (i, k),
                   b_block=(tn, tk), b_map=lambda i, j, k: (j, k),
                   o_block=(tm, tn), o_map=lambda i, j, k: (i, j), out_shape=(m, n))


def _mm_tn(a, b, name, tm, tn, tk):
    kd, m = a.shape
    n = b.shape[1]
    return _matmul(a, b, name=name, dims=TN, grid=(m // tm, n // tn, kd // tk),
                   a_block=(tk, tm), a_map=lambda i, j, k: (k, i),
                   b_block=(tk, tn), b_map=lambda i, j, k: (k, j),
                   o_block=(tm, tn), o_map=lambda i, j, k: (i, j), out_shape=(m, n))


W_TN = 1792
W_NJ = WB // W_TN


def _proj(h, wg):
    tm, tk = 1024, 1024
    return _matmul(h, wg, name="proj", dims=NN, grid=(T // tm, NC // W_TN, D // tk),
                   a_block=(tm, tk), a_map=lambda i, j, k: (i, k),
                   b_block=(tk, W_TN), b_map=lambda i, j, k: ((j // W_NJ) * (D // tk) + k, j % W_NJ),
                   o_block=(tm, W_TN), o_map=lambda i, j, k: (i, j), out_shape=(T, NC))


def _dh(dproj, wg):
    tm, tn = 1024, 1024
    return _matmul(dproj, wg, name="dh", dims=NT, grid=(T // tm, D // tn, NC // W_TN),
                   a_block=(tm, W_TN), a_map=lambda i, j, k: (i, k),
                   b_block=(tn, W_TN), b_map=lambda i, j, k: ((k // W_NJ) * (D // tn) + j, k % W_NJ),
                   o_block=(tm, tn), o_map=lambda i, j, k: (i, j), out_shape=(T, D))


def _grad_w_in(h, dproj):
    tm, tk = 1024, 1024
    return _matmul(h, dproj, name="grad_w_in", dims=TN, grid=(D // tm, NC // W_TN, T // tk),
                   a_block=(tk, tm), a_map=lambda i, j, k: (k, i),
                   b_block=(tk, W_TN), b_map=lambda i, j, k: (k, j),
                   o_block=(tm, W_TN), o_map=lambda i, j, k: ((j // W_NJ) * (D // tm) + i, j % W_NJ),
                   out_shape=(NCHIP * D, WB))


def _rope_tables(pos, half):
    inv = 1.0 / (ROPE_THETA ** (jnp.arange(half, dtype=F32) / half))
    ang = pos.astype(F32)[:, None] * inv[None, :]
    cos, sin = jnp.cos(ang), jnp.sin(ang)
    return jnp.concatenate([cos, cos], axis=1), jnp.concatenate([-sin, sin], axis=1)


def _band_mask(r0):
    qi = lax.broadcasted_iota(jnp.int32, (128, 256), 0)
    kk = lax.broadcasted_iota(jnp.int32, (128, 256), 1)
    return (kk >= qi) & (kk <= qi + 128) & (kk + r0 >= 128)


def _dil_rows(r, n, d):
    if d == 1:
        return pl.ds(pl.multiple_of(n * 128, 128), 128)
    return pl.ds(r + d * 128 * n, 128, stride=d)


def _attn_fwd(proj, cosf, sinf, g):
    d = DIL[g]
    ln = S // d
    nblk = ln // 128
    proj_v = proj.reshape(NB, S, NC)

    def body(q_ref, k_ref, v_ref, cos_ref, sin_ref, o_ref, l_ref, k_s, v_s):
        k_s[:, pl.ds(0, 128), :] = jnp.zeros((d, 128, HD), BF16)
        v_s[:, pl.ds(0, 128), :] = jnp.zeros((d, 128, HD), BF16)

        def prep(i, carry):
            r, n = i // nblk, i % nblk
            rows = _dil_rows(r, n, d)
            dst = pl.ds(pl.multiple_of(n * 128 + 128, 128), 128)
            k_s[r, dst, :] = _rope(k_ref[rows, :], cos_ref[rows, :], sin_ref[rows, :], HD // 2).astype(BF16)
            v_s[r, dst, :] = v_ref[rows, :].astype(BF16)
            return carry

        lax.fori_loop(0, d * nblk, prep, 0)

        def step(i, carry):
            r, n = i // nblk, i % nblk
            rows = _dil_rows(r, n, d)
            r0 = pl.multiple_of(n * 128, 128)
            qr = _rope(q_ref[rows, :], cos_ref[rows, :], sin_ref[rows, :], HD // 2).astype(BF16)
            kw = k_s[r, pl.ds(r0, 256), :]
            vw = v_s[r, pl.ds(r0, 256), :]
            sc = lax.dot_general(qr, kw, NT, preferred_element_type=F32) * A_SCALE
            sc = jnp.where(_band_mask(r0), sc, NEG)
            m = jnp.max(sc, axis=1, keepdims=True)
            p = jnp.exp(sc - m)
            l = jnp.sum(p, axis=1, keepdims=True)
            o_ref[rows, :] = jnp.dot(p.astype(BF16), vw, preferred_element_type=F32) / l
            l_ref[rows, :] = jnp.broadcast_to(m + jnp.log(l), (128, HD))
            return carry

        lax.fori_loop(0, d * nblk, step, 0)

    def col(off):
        return lambda b, h: (b, 0, off // HD + HPG * g + h)

    blk = (None, S, HD)
    tab = pl.BlockSpec((S, HD), lambda b, h: (0, 0))
    out = pl.BlockSpec(blk, lambda b, h: (b, 0, h))
    o, l = pl.pallas_call(
        body, name=f"attn_fwd_d{d}", grid=(NB, HPG),
        in_specs=[pl.BlockSpec(blk, col(QA)), pl.BlockSpec(blk, col(KA)), pl.BlockSpec(blk, col(VA)), tab, tab],
        out_specs=[out, out],
        out_shape=[jax.ShapeDtypeStruct((NB, S, HPG * HD), F32)] * 2,
        scratch_shapes=[pltpu.VMEM((d, ln + 128, HD), BF16), pltpu.VMEM((d, ln + 128, HD), BF16)],
        compiler_params=_cp(("parallel", "parallel")),
    )(proj_v, proj_v, proj_v, cosf, sinf)
    return o.reshape(T, HPG * HD), l.reshape(T, HPG * HD)


def _attn_bwd(proj, cosf, sinf, da, lse, delta, dproj, g):
    d = DIL[g]
    ln = S // d
    nblk = ln // 128
    proj_v = proj.reshape(NB, S, NC)
    dproj_v = dproj.reshape(NB, S, NC)
    da_v = da.reshape(NB, S, AW)
    lse_v = lse.reshape(NB, S, HPG * HD)
    delta_v = delta.reshape(NB, S, HPG * HD)

    def body(q_ref, k_ref, v_ref, cos_ref, sin_ref, da_ref, lse_ref, dl_ref, dp_in_ref, o_ref,
             k_s, v_s, stg, dk_s, dv_s):
        del dp_in_ref
        w = pl.program_id(2)

        def emit():
            def cast(n, carry):
                rows = pl.ds(pl.multiple_of(n * 256, 256), 256)
                o_ref[rows, :] = stg[rows, :].astype(BF16)
                return carry

            lax.fori_loop(0, S // 256, cast, 0)

        @pl.when(w == 0)
        def _():
            k_s[:, pl.ds(0, 128), :] = jnp.zeros((d, 128, HD), BF16)
            v_s[:, pl.ds(0, 128), :] = jnp.zeros((d, 128, HD), BF16)
            dk_s[...] = jnp.zeros_like(dk_s)
            dv_s[...] = jnp.zeros_like(dv_s)

            def prep(i, carry):
                r, n = i // nblk, i % nblk
                rows = _dil_rows(r, n, d)
                dst = pl.ds(pl.multiple_of(n * 128 + 128, 128), 128)
                k_s[r, dst, :] = _rope(k_ref[rows, :], cos_ref[rows, :], sin_ref[rows, :], HD // 2).astype(BF16)
                v_s[r, dst, :] = v_ref[rows, :].astype(BF16)
                return carry

            lax.fori_loop(0, d * nblk, prep, 0)

            def step(i, carry):
                r, n = i // nblk, i % nblk
                rows = _dil_rows(r, n, d)
                r0 = pl.multiple_of(n * 128, 128)
                win = pl.ds(r0, 256)
                cos, sin = cos_ref[rows, :], sin_ref[rows, :]
                qr = _rope(q_ref[rows, :], cos, sin, HD // 2).astype(BF16)
                kw = k_s[r, win, :]
                vw = v_s[r, win, :]
                sc = lax.dot_general(qr, kw, NT, preferred_element_type=F32) * A_SCALE
                sc = jnp.where(_band_mask(r0), sc, NEG)
                p = jnp.exp(sc - lse_ref[rows, :][:, :1])
                da_b = da_ref[rows, :].astype(BF16)
                dp = lax.dot_general(da_b, vw, NT, preferred_element_type=F32)
                ds_b = (p * (dp - dl_ref[rows, :][:, :1]) * A_SCALE).astype(BF16)
                p_b = p.astype(BF16)
                dq = jnp.dot(ds_b, kw, preferred_element_type=F32)
                stg[rows, :] = _rope_t(dq, cos, sin, HD // 2)
                dk_s[r, win, :] += lax.dot_general(ds_b, qr, TN, preferred_element_type=F32)
                dv_s[r, win, :] += lax.dot_general(p_b, da_b, TN, preferred_element_type=F32)
                return carry

            lax.fori_loop(0, d * nblk, step, 0)
            emit()

        @pl.when(w == 1)
        def _():
            def put(i, carry):
                r, n = i // nblk, i % nblk
                rows = _dil_rows(r, n, d)
                src = pl.ds(pl.multiple_of(n * 128 + 128, 128), 128)
                stg[rows, :] = _rope_t(dk_s[r, src, :], cos_ref[rows, :], sin_ref[rows, :], HD // 2)
                return carry

            lax.fori_loop(0, d * nblk, put, 0)
            emit()

        @pl.when(w == 2)
        def _():
            def put(i, carry):
                r, n = i // nblk, i % nblk
                src = pl.ds(pl.multiple_of(n * 128 + 128, 128), 128)
                stg[_dil_rows(r, n, d), :] = dv_s[r, src, :]
                return carry

            lax.fori_loop(0, d * nblk, put, 0)
            emit()

    def col(off):
        return lambda b, h, w: (b, 0, off // HD + HPG * g + h)

    blk = (None, S, HD)
    tab = pl.BlockSpec((S, HD), lambda b, h, w: (0, 0))
    per_head = pl.BlockSpec(blk, lambda b, h, w: (b, 0, h))
    out = pl.pallas_call(
        body, name=f"attn_bwd_d{d}", grid=(NB, HPG, 3),
        in_specs=[pl.BlockSpec(blk, col(QA)), pl.BlockSpec(blk, col(KA)), pl.BlockSpec(blk, col(VA)), tab, tab,
                  pl.BlockSpec(blk, col(0)), per_head, per_head, pl.BlockSpec(memory_space=pl.ANY)],
        out_specs=pl.BlockSpec(blk, lambda b, h, w: (b, 0, (AW // HD) * w + HPG * g + h)),
        out_shape=jax.ShapeDtypeStruct(dproj_v.shape, BF16),
        input_output_aliases={8: 0},
        scratch_shapes=[pltpu.VMEM((d, ln + 128, HD), BF16), pltpu.VMEM((d, ln + 128, HD), BF16),
                        pltpu.VMEM((S, HD), F32), pltpu.VMEM((d, ln + 128, HD), F32), pltpu.VMEM((d, ln + 128, HD), F32)],
        compiler_params=_cp(("arbitrary",) * 3),
    )(proj_v, proj_v, proj_v, cosf, sinf, da_v, lse_v, delta_v, dproj_v)
    return out.reshape(T, NC)


def _attn_mix(proj, os_, ls_):
    tr = 256
    gw = HPG * HD

    def body(o0, o1, o2, l0, l1, l2, z_ref, cat_ref):
        m = jnp.maximum(jnp.maximum(l0[...], l1[...]), l2[...])
        e = [jnp.exp(l[...] - m) for l in (l0, l1, l2)]
        inv = 1.0 / (e[0] + e[1] + e[2])
        for gi, o in enumerate((o0, o1, o2)):
            z = z_ref[:, gi * gw:(gi + 1) * gw]
            cat_ref[:, gi * gw:(gi + 1) * gw] = (o[...] * (e[gi] * inv) * (z * _sigmoid(z))).astype(BF16)

    grp = pl.BlockSpec((tr, gw), lambda i: (i, 0))
    return pl.pallas_call(
        body, name="attn_mix", grid=(T // tr,),
        in_specs=[grp] * 6 + [pl.BlockSpec((tr, AW), lambda i: (i, ZA // AW))],
        out_specs=pl.BlockSpec((tr, AW), lambda i: (i, 0)),
        out_shape=jax.ShapeDtypeStruct((T, D), BF16),
        compiler_params=_cp(("parallel",)),
    )(*os_, *ls_, proj)


def _attn_mix_bwd(dcat, proj, os_, ls_):
    tr = 256
    gw = HPG * HD

    def body(dy_ref, o0, o1, o2, l0, l1, l2, z_ref, da_ref, lse_ref, dl_ref, dz_ref):
        m = jnp.maximum(jnp.maximum(l0[...], l1[...]), l2[...])
        e = [jnp.exp(l[...] - m) for l in (l0, l1, l2)]
        den = e[0] + e[1] + e[2]
        inv = 1.0 / den
        lse_ref[...] = m + jnp.log(den)
        acc = jnp.zeros((tr, gw), F32)
        for gi, o in enumerate((o0, o1, o2)):
            cols = slice(gi * gw, (gi + 1) * gw)
            z = z_ref[:, cols]
            dy = dy_ref[:, cols]
            sg = _sigmoid(z)
            a = o[...] * (e[gi] * inv)
            da = dy * (z * sg)
            da_ref[:, cols] = da
            dz_ref[:, cols] = (dy * a * (sg * (1.0 + z * (1.0 - sg)))).astype(BF16)
            acc = acc + da * a
        for hh in range(HPG):
            cols = slice(hh * HD, (hh + 1) * HD)
            dl_ref[:, cols] = jnp.broadcast_to(jnp.sum(acc[:, cols], axis=1, keepdims=True), (tr, HD))

    grp = pl.BlockSpec((tr, gw), lambda i: (i, 0))
    return pl.pallas_call(
        body, name="attn_mix_bwd", grid=(T // tr,),
        in_specs=[pl.BlockSpec((tr, AW), lambda i: (i, 0))] + [grp] * 6 + [pl.BlockSpec((tr, AW), lambda i: (i, ZA // AW))],
        out_specs=[pl.BlockSpec((tr, AW), lambda i: (i, 0)), grp, grp, pl.BlockSpec((tr, AW), lambda i: (i, ZA // AW))],
        out_shape=[jax.ShapeDtypeStruct((T, AW), F32), jax.ShapeDtypeStruct((T, gw), F32),
                   jax.ShapeDtypeStruct((T, gw), F32), jax.ShapeDtypeStruct((T, NC), BF16)],
        compiler_params=_cp(("parallel",)),
    )(dcat, *os_, *ls_, proj)


CT = 128


def _shift_down(x, n):
    rows = lax.broadcasted_iota(jnp.int32, x.shape, 0)
    return jnp.where(rows >= n, pltpu.roll(x, n, 0), 0.0)


def _shift_up(x, n):
    rows = lax.broadcasted_iota(jnp.int32, x.shape, 0)
    return jnp.where(rows < x.shape[0] - n, pltpu.roll(x, x.shape[0] - n, 0), 0.0)


def _conv_fwd(proj, conv_w, cat):
    proj_v = proj.reshape(NB, S, NC)
    cat_v = cat.reshape(NB, S, D)

    def body(u_ref, b_ref, c_ref, z_ref, w_ref, cat_in, o_ref):
        del cat_in
        cu = c_ref[...] * u_ref[...]
        cv = _shift_down(cu, 2) * w_ref[0:1, :] + _shift_down(cu, 1) * w_ref[1:2, :] + cu * w_ref[2:3, :]
        z = z_ref[...]
        o_ref[...] = (b_ref[...] * cv * (z * _sigmoid(z))).astype(BF16)

    def seg(off):
        return pl.BlockSpec((None, S, CT), lambda b, j: (b, 0, off // CT + j))

    out = pl.pallas_call(
        body, name="conv_fwd", grid=(NB, CW // CT),
        in_specs=[seg(UC), seg(BC), seg(CC), seg(ZC), pl.BlockSpec((3, CT), lambda b, j: (0, j)),
                  pl.BlockSpec(memory_space=pl.ANY)],
        out_specs=pl.BlockSpec((None, S, CT), lambda b, j: (b, 0, AW // CT + j)),
        out_shape=jax.ShapeDtypeStruct((NB, S, D), BF16),
        input_output_aliases={5: 0},
        compiler_params=_cp(("parallel", "parallel")),
    )(proj_v, proj_v, proj_v, proj_v, conv_w, cat_v)
    return out.reshape(T, D)


def _conv_bwd(dcat, proj, conv_w, dproj):
    proj_v = proj.reshape(NB, S, NC)
    dproj_v = dproj.reshape(NB, S, NC)
    dcat_v = dcat.reshape(NB, S, D)

    def body(dy_ref, u_ref, b_ref, c_ref, z_ref, w_ref, dp_in, o_ref, dw_ref, st):
        del dp_in
        b = pl.program_id(1)
        w = pl.program_id(2)

        @pl.when((b == 0) & (w == 0))
        def _():
            dw_ref[...] = jnp.zeros_like(dw_ref)

        @pl.when(w == 0)
        def _():
            u, c, z, bb, dy = u_ref[...], c_ref[...], z_ref[...], b_ref[...], dy_ref[...]
            cu = c * u
            s1 = _shift_down(cu, 1)
            s2 = _shift_down(cu, 2)
            cv = s2 * w_ref[0:1, :] + s1 * w_ref[1:2, :] + cu * w_ref[2:3, :]
            sg = _sigmoid(z)
            sz = z * sg
            dcv = dy * bb * sz
            st[0] = dy * cv * sz
            st[2] = dy * bb * cv * (sg * (1.0 + z * (1.0 - sg)))
            dw_ref[0:1, :] += jnp.sum(dcv * s2, axis=0, keepdims=True)
            dw_ref[1:2, :] += jnp.sum(dcv * s1, axis=0, keepdims=True)
            dw_ref[2:3, :] += jnp.sum(dcv * cu, axis=0, keepdims=True)
            dcu = dcv * w_ref[2:3, :] + _shift_up(dcv, 1) * w_ref[1:2, :] + _shift_up(dcv, 2) * w_ref[0:1, :]
            st[1] = dcu * u
            o_ref[...] = (dcu * c).astype(BF16)

        for k in range(3):
            @pl.when(w == k + 1)
            def _(k=k):
                o_ref[...] = st[k].astype(BF16)

    def seg(off):
        return pl.BlockSpec((None, S, CT), lambda j, b, w: (b, 0, off // CT + j))

    out, dw = pl.pallas_call(
        body, name="conv_bwd", grid=(CW // CT, NB, 4),
        in_specs=[pl.BlockSpec((None, S, CT), lambda j, b, w: (b, 0, AW // CT + j)),
                  seg(UC), seg(BC), seg(CC), seg(ZC), pl.BlockSpec((3, CT), lambda j, b, w: (0, j)),
                  pl.BlockSpec(memory_space=pl.ANY)],
        out_specs=[pl.BlockSpec((None, S, CT), lambda j, b, w: (b, 0, (UC + w * CW) // CT + j)),
                   pl.BlockSpec((3, CT), lambda j, b, w: (0, j))],
        out_shape=[jax.ShapeDtypeStruct((NB, S, NC), BF16), jax.ShapeDtypeStruct((3, CW), F32)],
        input_output_aliases={6: 0},
        scratch_shapes=[pltpu.VMEM((3, S, CT), F32)],
        compiler_params=_cp(("arbitrary",) * 3),
    )(dcat_v, proj_v, proj_v, proj_v, proj_v, conv_w, dproj_v)
    return out.reshape(T, NC), dw


XT = 512


def _cross_fwd(proj, mkv, cosq, sinq, cosm, sinm, cat):
    proj_v = proj.reshape(NB, S, NC)
    mkv_v = mkv.reshape(NB, MLEN, 2 * XW)
    cat_v = cat.reshape(NB, S, D)

    def body(q_ref, z_ref, mk_ref, mv_ref, cq, sq, cm, sm, cat_in, o_ref):
        del cat_in
        mkr = _rope(mk_ref[...], cm[...], sm[...], XHD // 2).astype(BF16)
        qr = _rope(q_ref[...], cq[...], sq[...], XHD // 2).astype(BF16)
        sc = lax.dot_general(qr, mkr, NT, preferred_element_type=F32) * X_SCALE
        p = jnp.exp(sc - jnp.max(sc, axis=1, keepdims=True))
        p = p / jnp.sum(p, axis=1, keepdims=True)
        ox = jnp.dot(p.astype(BF16), mv_ref[...].astype(BF16), preferred_element_type=F32)
        z = z_ref[...]
        o_ref[...] = (ox * (z * _sigmoid(z))).astype(BF16)

    def seg(off):
        return pl.BlockSpec((None, XT, XHD), lambda b, h, t: (b, t, off // XHD + h))

    qtab = pl.BlockSpec((XT, XHD), lambda b, h, t: (t, 0))
    mtab = pl.BlockSpec((MLEN, XHD), lambda b, h, t: (0, 0))
    out = pl.pallas_call(
        body, name="cross_fwd", grid=(NB, NXH, S // XT),
        in_specs=[seg(QX), seg(ZX),
                  pl.BlockSpec((None, MLEN, XHD), lambda b, h, t: (b, 0, h)),
                  pl.BlockSpec((None, MLEN, XHD), lambda b, h, t: (b, 0, NXH + h)),
                  qtab, qtab, mtab, mtab, pl.BlockSpec(memory_space=pl.ANY)],
        out_specs=pl.BlockSpec((None, XT, XHD), lambda b, h, t: (b, t, (AW + CW) // XHD + h)),
        out_shape=jax.ShapeDtypeStruct((NB, S, D), BF16),
        input_output_aliases={8: 0},
        compiler_params=_cp(("parallel",) * 3),
    )(proj_v, proj_v, mkv_v, mkv_v, cosq, sinq, cosm, sinm, cat_v)
    return out.reshape(T, D)


def _cross_bwd(dcat, proj, mkv, cosq, sinq, cosm, sinm, dproj):
    proj_v = proj.reshape(NB, S, NC)
    dproj_v = dproj.reshape(NB, S, NC)
    dcat_v = dcat.reshape(NB, S, D)
    mkv_v = mkv.reshape(NB, MLEN, 2 * XW)
    nt = S // XT

    def body(dy_ref, q_ref, z_ref, mk_ref, mv_ref, cq, sq, cm, sm, dp_in, o_ref, dmk_ref, dmv_ref, dz_s):
        del dp_in
        t = pl.program_id(2)
        w = pl.program_id(3)

        @pl.when((t == 0) & (w == 0))
        def _():
            dmk_ref[...] = jnp.zeros_like(dmk_ref)
            dmv_ref[...] = jnp.zeros_like(dmv_ref)

        @pl.when(w == 0)
        def _():
            mkr = _rope(mk_ref[...], cm[...], sm[...], XHD // 2).astype(BF16)
            mv_b = mv_ref[...].astype(BF16)
            qr = _rope(q_ref[...], cq[...], sq[...], XHD // 2).astype(BF16)
            sc = lax.dot_general(qr, mkr, NT, preferred_element_type=F32) * X_SCALE
            p = jnp.exp(sc - jnp.max(sc, axis=1, keepdims=True))
            p = p / jnp.sum(p, axis=1, keepdims=True)
            p_b = p.astype(BF16)
            ox = jnp.dot(p_b, mv_b, preferred_element_type=F32)
            z = z_ref[...]
            dy = dy_ref[...]
            sg = _sigmoid(z)
            dz_s[...] = dy * ox * (sg * (1.0 + z * (1.0 - sg)))
            dox_b = (dy * (z * sg)).astype(BF16)
            dp = lax.dot_general(dox_b, mv_b, NT, preferred_element_type=F32)
            ds_b = (p * (dp - jnp.sum(dp * p, axis=1, keepdims=True)) * X_SCALE).astype(BF16)
            dq = jnp.dot(ds_b, mkr, preferred_element_type=F32)
            o_ref[...] = _rope_t(dq, cq[...], sq[...], XHD // 2).astype(BF16)
            dmk_ref[...] += lax.dot_general(ds_b, qr, TN, preferred_element_type=F32)
            dmv_ref[...] += lax.dot_general(p_b, dox_b, TN, preferred_element_type=F32)

        @pl.when(w == 1)
        def _():
            o_ref[...] = dz_s[...].astype(BF16)

        @pl.when((t == nt - 1) & (w == 1))
        def _():
            dmk_ref[...] = _rope_t(dmk_ref[...], cm[...], sm[...], XHD // 2)

    def seg(off):
        return pl.BlockSpec((None, XT, XHD), lambda b, h, t, w: (b, t, off // XHD + h))

    qtab = pl.BlockSpec((XT, XHD), lambda b, h, t, w: (t, 0))
    mtab = pl.BlockSpec((MLEN, XHD), lambda b, h, t, w: (0, 0))
    macc = pl.BlockSpec((None, MLEN, XHD), lambda b, h, t, w: (b, 0, h))
    out, dmk, dmv = pl.pallas_call(
        body, name="cross_bwd", grid=(NB, NXH, nt, 2),
        in_specs=[pl.BlockSpec((None, XT, XHD), lambda b, h, t, w: (b, t, (AW + CW) // XHD + h)),
                  seg(QX), seg(ZX),
                  pl.BlockSpec((None, MLEN, XHD), lambda b, h, t, w: (b, 0, h)),
                  pl.BlockSpec((None, MLEN, XHD), lambda b, h, t, w: (b, 0, NXH + h)),
                  qtab, qtab, mtab, mtab, pl.BlockSpec(memory_space=pl.ANY)],
        out_specs=[pl.BlockSpec((None, XT, XHD), lambda b, h, t, w: (b, t, (QX + w * XW) // XHD + h)), macc, macc],
        out_shape=[jax.ShapeDtypeStruct((NB, S, NC), BF16), jax.ShapeDtypeStruct((NB, MLEN, XW), F32),
                   jax.ShapeDtypeStruct((NB, MLEN, XW), F32)],
        input_output_aliases={9: 0},
        scratch_shapes=[pltpu.VMEM((XT, XHD), F32)],
        compiler_params=_cp(("arbitrary",) * 4),
    )(dcat_v, proj_v, proj_v, mkv_v, mkv_v, cosq, sinq, cosm, sinm, dproj_v)
    return out.reshape(T, NC), dmk, dmv


def _local_step(x, mem, pre_norm, wg_in, conv_w, mem_norm, wg_kv, wg_out, post_norm, tgt):
    x2 = x.reshape(T, D)
    mem2 = mem.reshape(NB * MLEN, D)
    tgt2 = tgt.reshape(T, D)
    cosa, sina = _rope_tables(jnp.arange(S), HD // 2)
    cosq, sinq = _rope_tables(jnp.arange(S) + MLEN, XHD // 2)
    cosm, sinm = _rope_tables(jnp.arange(MLEN), XHD // 2)

    h = _rms_fwd(x2, pre_norm, "pre_norm_fwd")
    proj = _proj(h, wg_in)
    memn = _rms_fwd(mem2, mem_norm, "mem_norm_fwd")
    mkv = _mm_nn(memn, wg_kv, "mkv", NB * MLEN, 1024, 512)

    fw = [_attn_fwd(proj, cosa, sina, g) for g in range(3)]
    os_ = [f[0] for f in fw]
    ls_ = [f[1] for f in fw]
    cat = _attn_mix(proj, os_, ls_)
    cat = _conv_fwd(proj, conv_w, cat)
    cat = _cross_fwd(proj, mkv, cosq, sinq, cosm, sinm, cat)
    y = _mm_nn(cat, wg_out, "out_proj", 1024, 1024, 2048)
    dy, dout, d_post, loss = _post_norm_loss(y, x2, tgt2, post_norm)

    dcat = _mm_nt(dy, wg_out, "dcat", 1024, 1024, 2048)
    g_out = _mm_tn(cat, dy, "grad_w_out", 1024, 1024, 2048)
    da, lse, delta, dproj = _attn_mix_bwd(dcat, proj, os_, ls_)
    for g in range(3):
        dproj = _attn_bwd(proj, cosa, sina, da, lse, delta, dproj, g)
    dproj, d_conv = _conv_bwd(dcat, proj, conv_w, dproj)
    dproj, dmk, dmv = _cross_bwd(dcat, proj, mkv, cosq, sinq, cosm, sinm, dproj)

    dmkv = jnp.concatenate([dmk, dmv], axis=-1).reshape(NB * MLEN, 2 * XW)
    dmkv_b = dmkv.astype(BF16)
    g_kv = _mm_tn(memn, dmkv_b, "grad_w_mem_kv", 1024, 1024, NB * MLEN)
    dmemn = _mm_nt(dmkv_b, wg_kv, "dmemn", NB * MLEN, 1024, 512)
    d_mem = _norm_gain_grad(dmemn, mem2, "mem_norm_bwd")

    g_in = _grad_w_in(h, dproj)
    dh = _dh(dproj, wg_in)
    grad_x, d_pre = _pre_norm_bwd(dh, x2, pre_norm, dout)
    return loss, grad_x.reshape(NB, S, D), d_pre, d_mem, d_post, d_conv, g_in, g_kv, g_out


def _adamw(w, g, m, v, name):
    rows, cols = w.shape
    tr = rows if rows <= 512 else 512
    tc = cols if cols <= 1024 else 1024
    if cols % tc:
        tc = 896

    def body(w_ref, g_ref, m_ref, v_ref, d_ref, nm_ref, nv_ref):
        gv = g_ref[...]
        nm = ADAM_B1 * m_ref[...] + (1.0 - ADAM_B1) * gv
        nv = ADAM_B2 * v_ref[...] + (1.0 - ADAM_B2) * (gv * gv)
        m_hat = nm / (1.0 - ADAM_B1 ** ADAM_STEP)
        v_hat = nv / (1.0 - ADAM_B2 ** ADAM_STEP)
        d_ref[...] = -ADAM_LR * (m_hat / (jnp.sqrt(v_hat) + ADAM_EPS) + ADAM_WD * w_ref[...])
        nm_ref[...] = nm
        nv_ref[...] = nv

    blk = pl.BlockSpec((tr, tc), lambda i, j: (i, j))
    sds = jax.ShapeDtypeStruct((rows, cols), F32)
    return pl.pallas_call(
        body, name=name, grid=(rows // tr, cols // tc),
        in_specs=[blk] * 4, out_specs=[blk] * 3, out_shape=[sds] * 3,
        compiler_params=_cp(("parallel", "parallel")),
    )(w, g, m, v)


def _place():
    return lax.axis_index("x"), lax.axis_index("y"), lax.axis_index("c")


def _other_chips(x, y):
    return [(1 - x, y), (x, 1 - y), (1 - x, 1 - y)]


def _tile_cols(cols):
    return cols if cols <= 1024 else (1024 if cols % 1024 == 0 else 896)


def _as_index(v):
    return jnp.reshape(v, (1,)).astype(jnp.int32)


def _cast_own(w, chip, name):
    rows, cols = w.shape
    tr, tc = 512, _tile_cols(cols)

    def body(chip_ref, w_ref, o_ref):
        del chip_ref
        o_ref[...] = w_ref[...].astype(BF16)

    grid_spec = pltpu.PrefetchScalarGridSpec(
        num_scalar_prefetch=1, grid=(rows // tr, cols // tc),
        in_specs=[pl.BlockSpec((tr, tc), lambda i, j, cr: (i, j))],
        out_specs=pl.BlockSpec((None, tr, tc), lambda i, j, cr: (cr[0], i, j)))
    return pl.pallas_call(
        body, name=name, grid_spec=grid_spec,
        out_shape=jax.ShapeDtypeStruct((NCHIP, rows, cols), BF16),
        compiler_params=_cp(("parallel", "parallel")),
    )(_as_index(chip), w)


def _gather_weights(bufs):
    n = len(bufs)

    def body(*refs):
        outs = refs[n:2 * n]
        send_sems, recv_sems = refs[2 * n:]
        x, y, c = _place()
        me_chip = 2 * x + y
        sibling = (x, y, 1 - c)
        chips = _other_chips(x, y)

        def half(ref, chip, hc):
            hr = ref.shape[1] // 2
            return ref.at[chip, pl.ds(hc * hr, hr), :]

        def copy(a, k, src, dst, to):
            return pltpu.make_async_remote_copy(src_ref=src, dst_ref=dst, send_sem=send_sems.at[6 * a + k],
                                                recv_sem=recv_sems.at[6 * a + k], device_id=to, device_id_type=MESH)

        first = []
        for a in range(n):
            for j, (px, py) in enumerate(chips):
                mine = half(outs[a], me_chip, c)
                first.append(copy(a, j, mine, mine, (px, py, c)))
        for cp in first:
            cp.start()
        passed = []
        for a in range(n):
            for j, (px, py) in enumerate(chips):
                blk = half(outs[a], 2 * px + py, c)
                copy(a, j, blk, blk, (px, py, c)).wait_recv()
                fwd = copy(a, 3 + j, blk, blk, sibling)
                fwd.start()
                passed.append(fwd)
        for a in range(n):
            for j, (px, py) in enumerate(chips):
                blk = half(outs[a], 2 * px + py, 1 - c)
                copy(a, 3 + j, blk, blk, sibling).wait_recv()
        for cp in first + passed:
            cp.wait_send()

    hbm = pl.BlockSpec(memory_space=pl.ANY)
    return pl.pallas_call(
        body, name="gather_weights",
        in_specs=[hbm] * n, out_specs=[hbm] * n,
        out_shape=[jax.ShapeDtypeStruct(s.shape, s.dtype) for s in bufs],
        input_output_aliases={a: a for a in range(n)},
        scratch_shapes=[pltpu.SemaphoreType.DMA((6 * n,)), pltpu.SemaphoreType.DMA((6 * n,))],
    )(*bufs)


def _sibling_halves(parts):
    n = len(parts)

    def body(*refs):
        ins, outs = refs[:n], refs[n:2 * n]
        send_sems, recv_sems = refs[2 * n:]
        x, y, c = _place()
        cps = []
        for a in range(n):
            hr = ins[a].shape[1] // 2
            cps.append(pltpu.make_async_remote_copy(
                src_ref=ins[a].at[:, pl.ds((1 - c) * hr, hr), :], dst_ref=outs[a],
                send_sem=send_sems.at[a], recv_sem=recv_sems.at[a], device_id=(x, y, 1 - c), device_id_type=MESH))
        for cp in cps:
            cp.start()
        for cp in cps:
            cp.wait()

    hbm = pl.BlockSpec(memory_space=pl.ANY)
    return pl.pallas_call(
        body, name="sibling_halves",
        in_specs=[hbm] * n, out_specs=[hbm] * n,
        out_shape=[jax.ShapeDtypeStruct((NCHIP, p.shape[1] // 2, p.shape[2]), p.dtype) for p in parts],
        scratch_shapes=[pltpu.SemaphoreType.DMA((n,)), pltpu.SemaphoreType.DMA((n,))],
    )(*parts)


def _pair_sum(part, got, name):
    nblk, rows, cols = part.shape
    hr = rows // 2
    tr = 512
    tc = cols if cols <= 1024 else (1024 if cols % 1024 == 0 else 896)
    c = lax.axis_index("c")
    nrt = hr // tr

    def body(c_ref, p_ref, g_ref, o_ref):
        del c_ref
        o_ref[...] = (p_ref[...] + g_ref[...]).astype(BF16)

    grid_spec = pltpu.PrefetchScalarGridSpec(
        num_scalar_prefetch=1, grid=(nblk, nrt, cols // tc),
        in_specs=[pl.BlockSpec((None, tr, tc), lambda b, i, j, cr: (b, cr[0] * nrt + i, j)),
                  pl.BlockSpec((None, tr, tc), lambda b, i, j, cr: (b, i, j))],
        out_specs=pl.BlockSpec((None, tr, tc), lambda b, i, j, cr: (b, i, j)))
    return pl.pallas_call(
        body, name=name, grid_spec=grid_spec,
        out_shape=jax.ShapeDtypeStruct((nblk, hr, cols), BF16),
        compiler_params=_cp(("parallel",) * 3),
    )(jnp.reshape(c, (1,)).astype(jnp.int32), part, got)


def _scatter_to_chips(qs):
    n = len(qs)

    def body(*refs):
        ins, outs = refs[:n], refs[n:2 * n]
        send_sems, recv_sems = refs[2 * n:]
        x, y, c = _place()
        chips = _other_chips(x, y)
        cps = []
        for a in range(n):
            for j, (px, py) in enumerate(chips):
                cps.append(pltpu.make_async_remote_copy(
                    src_ref=ins[a].at[2 * px + py], dst_ref=outs[a].at[j],
                    send_sem=send_sems.at[3 * a + j], recv_sem=recv_sems.at[3 * a + j], device_id=(px, py, c), device_id_type=MESH))
        for cp in cps:
            cp.start()
        for cp in cps:
            cp.wait()

    hbm = pl.BlockSpec(memory_space=pl.ANY)
    return pl.pallas_call(
        body, name="scatter_to_chips",
        in_specs=[hbm] * n, out_specs=[hbm] * n,
        out_shape=[jax.ShapeDtypeStruct((3,) + q.shape[1:], q.dtype) for q in qs],
        scratch_shapes=[pltpu.SemaphoreType.DMA((3 * n,)), pltpu.SemaphoreType.DMA((3 * n,))],
    )(*qs)


def _chip_sum(q, got, name):
    _, hr, cols = got.shape
    tr, tc = 512, _tile_cols(cols)
    chip = 2 * lax.axis_index("x") + lax.axis_index("y")
    c = lax.axis_index("c")

    def body(idx_ref, q_ref, g_ref, o_ref):
        del idx_ref
        acc = q_ref[...].astype(F32)
        for i in range(3):
            acc = acc + g_ref[i].astype(F32)
        o_ref[...] = acc

    grid_spec = pltpu.PrefetchScalarGridSpec(
        num_scalar_prefetch=1, grid=(hr // tr, cols // tc),
        in_specs=[pl.BlockSpec((None, tr, tc), lambda i, j, ix: (ix[0], i, j)),
                  pl.BlockSpec((3, tr, tc), lambda i, j, ix: (0, i, j))],
        out_specs=pl.BlockSpec((None, tr, tc), lambda i, j, ix: (ix[1], i, j)))
    return pl.pallas_call(
        body, name=name, grid_spec=grid_spec,
        out_shape=jax.ShapeDtypeStruct((2, hr, cols), F32),
        compiler_params=_cp(("parallel", "parallel")),
    )(jnp.stack([chip, c]).astype(jnp.int32), q, got)


def _join_halves(bufs):
    n = len(bufs)

    def body(*refs):
        outs = refs[n:2 * n]
        send_sems, recv_sems = refs[2 * n:]
        x, y, c = _place()
        cps = []
        for a in range(n):
            cps.append(pltpu.make_async_remote_copy(src_ref=outs[a].at[c], dst_ref=outs[a].at[c], send_sem=send_sems.at[a],
                                                    recv_sem=recv_sems.at[a], device_id=(x, y, 1 - c), device_id_type=MESH))
        for cp in cps:
            cp.start()
        for a in range(n):
            theirs = outs[a].at[1 - c]
            pltpu.make_async_remote_copy(src_ref=theirs, dst_ref=theirs, send_sem=send_sems.at[a], recv_sem=recv_sems.at[a],
                                         device_id=(x, y, 1 - c), device_id_type=MESH).wait_recv()
        for cp in cps:
            cp.wait_send()

    hbm = pl.BlockSpec(memory_space=pl.ANY)
    return pl.pallas_call(
        body, name="join_halves",
        in_specs=[hbm] * n, out_specs=[hbm] * n,
        out_shape=[jax.ShapeDtypeStruct(b.shape, b.dtype) for b in bufs],
        input_output_aliases={a: a for a in range(n)},
        scratch_shapes=[pltpu.SemaphoreType.DMA((n,)), pltpu.SemaphoreType.DMA((n,))],
    )(*bufs)


N_DEV = 8


def _gather_small(block, reduce, name):
    m_per, cols = block.shape

    def body(x_ref, out_ref, all_ref, send_sems, recv_sems, local_sem):
        x, y, c = _place()
        me, sibling = (x, y, c), (x, y, 1 - c)
        chips = _other_chips(x, y)

        def rows(px, py, pc):
            return all_ref.at[pl.ds((4 * px + 2 * py + pc) * m_per, m_per), :]

        def copy(k, block_of, to, src=None):
            return pltpu.make_async_remote_copy(
                src_ref=rows(*block_of) if src is None else src, dst_ref=rows(*block_of),
                send_sem=send_sems.at[k], recv_sem=recv_sems.at[k], device_id=to, device_id_type=MESH)

        mine = pltpu.make_async_copy(x_ref, rows(*me), local_sem)
        mine.start()
        first = [copy(0, me, sibling, src=x_ref)]
        first += [copy(1 + j, me, (*chip, c), src=x_ref) for j, chip in enumerate(chips)]
        for cp in first:
            cp.start()
        passed = [copy(4 + j, (*chip, c), sibling) for j, chip in enumerate(chips)]
        for j, chip in enumerate(chips):
            copy(1 + j, (*chip, c), me).wait_recv()
            passed[j].start()
        copy(0, sibling, me).wait_recv()
        for j, chip in enumerate(chips):
            copy(4 + j, (*chip, 1 - c), me).wait_recv()
        for cp in first + passed:
            cp.wait_send()
        mine.wait()
        if reduce:
            acc = all_ref[pl.ds(0, m_per), :]
            for i in range(1, N_DEV):
                acc = acc + all_ref[pl.ds(i * m_per, m_per), :]
            out_ref[...] = acc
        else:
            out_ref[...] = all_ref[...]

    out_rows = m_per if reduce else N_DEV * m_per
    return pl.pallas_call(
        body, name=name,
        in_specs=[pl.BlockSpec(memory_space=pltpu.VMEM)],
        out_specs=pl.BlockSpec(memory_space=pltpu.VMEM),
        out_shape=jax.ShapeDtypeStruct((out_rows, cols), F32),
        scratch_shapes=[pltpu.VMEM((N_DEV * m_per, cols), F32), pltpu.SemaphoreType.DMA((7,)),
                        pltpu.SemaphoreType.DMA((7,)), pltpu.SemaphoreType.DMA],
    )(block)


def _reduce_scatter(parts):
    got = _sibling_halves(parts)
    qs = [_pair_sum(p, g, f"pair_sum_{i}") for i, (p, g) in enumerate(zip(parts, got))]
    landed = _scatter_to_chips(qs)
    halves = [_chip_sum(q, l, f"chip_sum_{i}") for i, (q, l) in enumerate(zip(qs, landed))]
    return [j.reshape(2 * j.shape[1], j.shape[2]) for j in _join_halves(halves)]


def kernel(x, mem, pre_norm, w_in, conv_w, mem_norm, w_mem_kv, w_out, post_norm, loss_target, m_pre_norm, m_w_in, m_conv_w, m_mem_norm, m_w_mem_kv, m_w_out, m_post_norm, v_pre_norm, v_w_in, v_conv_w, v_mem_norm, v_w_mem_kv, v_w_out, v_post_norm):
    chip = 2 * lax.axis_index("x") + lax.axis_index("y")

    wg_in, wg_kv, wg_out = _gather_weights([_cast_own(w_in[0], chip, "cast_w_in"), _cast_own(w_mem_kv[0], chip, "cast_w_mem_kv"),
                                            _cast_own(w_out[0], chip, "cast_w_out")])
    wg_in = wg_in.reshape(NCHIP * D, WB)
    wg_kv = wg_kv.reshape(D, 2 * XW)
    wg_out = wg_out.reshape(D, D)
    cw_blk = jnp.zeros((8, 384), F32).at[:3].set(conv_w[0])
    cw_all = _gather_small(cw_blk, False, "gather_conv_w").reshape(NCHIP, 2, 8, 384)[:, 0, :3]
    conv_full = jnp.transpose(cw_all, (1, 0, 2)).reshape(3, CW)

    loss, grad_x, d_pre, d_mem, d_post, d_conv, g_in, g_kv, g_out = _local_step(
        x, mem, pre_norm, wg_in, conv_full, mem_norm, wg_kv, wg_out, post_norm, loss_target)

    r_in, r_kv, r_out = _reduce_scatter([g_in.reshape(NCHIP, D, WB), g_kv.reshape(NCHIP, D // NCHIP, 2 * XW),
                                         g_out.reshape(NCHIP, D // NCHIP, D)])

    pack = jnp.concatenate([d_pre, d_mem, d_post, jnp.pad(d_conv, ((0, 0), (0, D - CW))),
                            jnp.pad(loss, ((0, 0), (0, D - 128))), jnp.zeros((1, D), F32)], axis=0)
    tot = _gather_small(pack, True, "reduce_small")
    g_pre, g_mem, g_post = tot[0:1], tot[1:2], tot[2:3]
    g_conv = lax.dynamic_slice(tot[3:6, :CW], (0, chip * 384), (3, 384))
    loss_out = tot[6, 0]

    names = ("pre_norm", "w_in", "conv_w", "mem_norm", "w_mem_kv", "w_out", "post_norm")
    ws = (pre_norm, w_in[0], conv_w[0], mem_norm, w_mem_kv[0], w_out[0], post_norm)
    gs = (g_pre, r_in, g_conv, g_mem, r_kv, r_out, g_post)
    ms = (m_pre_norm, m_w_in[0], m_conv_w[0], m_mem_norm, m_w_mem_kv[0], m_w_out[0], m_post_norm)
    vs = (v_pre_norm, v_w_in[0], v_conv_w[0], v_mem_norm, v_w_mem_kv[0], v_w_out[0], v_post_norm)
    upd = [_adamw(w, g, m, v, "adamw_" + nm) for nm, w, g, m, v in zip(names, ws, gs, ms, vs)]

    def shaped(arrs):
        return [a.reshape(w.shape) if w.ndim == a.ndim else a.reshape((1,) + a.shape)
                for a, w in zip(arrs, (pre_norm, w_in, conv_w, mem_norm, w_mem_kv, w_out, post_norm))]

    grads = shaped(gs)
    deltas = shaped([u[0] for u in upd])
    new_m = shaped([u[1] for u in upd])
    new_v = shaped([u[2] for u in upd])
    return (loss_out, grad_x, *grads, *deltas, *new_m, *new_v)
```

```python
import functools

import jax
import jax.numpy as jnp
from jax import lax
from jax.experimental import pallas as pl
from jax.experimental.pallas import tpu as pltpu

F32 = jnp.float32
BF16 = jnp.bfloat16

D = 4096
S = 2048
NB = 2
T = NB * S
MLEN = 256
HD = 128
AW = 1536
CW = 1536
XW = 1024
XHD = 256
NXH = 4
NC = 14336
QA, KA, VA, ZA, UC, BC, CC, ZC, QX, ZX = 0, 1536, 3072, 4608, 6144, 7680, 9216, 10752, 12288, 13312
NCHIP = 4
WB = NC // NCHIP
DIL = (1, 4, 16)
HPG = 4
EPS = 1e-6
NEG = -1e30
ROPE_THETA = 10000.0
A_SCALE = HD ** -0.5
X_SCALE = XHD ** -0.5

ADAM_LR = 0.001
ADAM_B1 = 0.9
ADAM_B2 = 0.999
ADAM_EPS = 1e-08
ADAM_WD = 0.01
ADAM_STEP = 10

MESH = pl.DeviceIdType.MESH
MIB = 1024 * 1024


def _cp(sem, vmem_mib=48):
    return pltpu.CompilerParams(dimension_semantics=sem, vmem_limit_bytes=vmem_mib * MIB)


def _sigmoid(z):
    return 1.0 / (1.0 + jnp.exp(-z))


def _rope(x, cos, sin, half):
    return x * cos + pltpu.roll(x, half, 1) * sin


def _rope_t(g, cos, sin, half):
    return g * cos + pltpu.roll(g * sin, half, 1)


def _rms_fwd(x2, g, name, dep=None):
    rows = x2.shape[0]
    tr = 256
    deps = [] if dep is None else [dep]

    def body(x_ref, g_ref, *rest):
        o_ref = rest[-1]
        x = x_ref[...]
        r = lax.rsqrt(jnp.mean(x * x, axis=-1, keepdims=True) + EPS)
        o_ref[...] = (x * r * g_ref[...]).astype(BF16)

    return pl.pallas_call(
        body, name=name, grid=(rows // tr,),
        in_specs=[pl.BlockSpec((tr, D), lambda i: (i, 0)), pl.BlockSpec((1, D), lambda i: (0, 0))]
        + [pl.BlockSpec(memory_space=pl.ANY)] * len(deps),
        out_specs=pl.BlockSpec((tr, D), lambda i: (i, 0)),
        out_shape=jax.ShapeDtypeStruct((rows, D), BF16),
        compiler_params=_cp(("parallel",)),
    )(x2, g, *deps)


def _norm_gain_grad(dn, x2, name):
    rows = x2.shape[0]
    tr = 256

    def body(dn_ref, x_ref, dg_ref):
        @pl.when(pl.program_id(0) == 0)
        def _():
            dg_ref[...] = jnp.zeros_like(dg_ref)
        x = x_ref[...]
        r = lax.rsqrt(jnp.mean(x * x, axis=-1, keepdims=True) + EPS)
        dg_ref[...] += jnp.sum(dn_ref[...] * (x * r), axis=0, keepdims=True)

    return pl.pallas_call(
        body, name=name, grid=(rows // tr,),
        in_specs=[pl.BlockSpec((tr, D), lambda i: (i, 0)), pl.BlockSpec((tr, D), lambda i: (i, 0))],
        out_specs=pl.BlockSpec((1, D), lambda i: (0, 0)),
        out_shape=jax.ShapeDtypeStruct((1, D), F32),
        compiler_params=_cp(("arbitrary",)),
    )(dn, x2)


def _pre_norm_bwd(dh, x2, g, dout):
    tr = 256

    def body(dh_ref, x_ref, g_ref, dout_ref, gx_ref, dg_ref):
        @pl.when(pl.program_id(0) == 0)
        def _():
            dg_ref[...] = jnp.zeros_like(dg_ref)
        x = x_ref[...]
        dh_ = dh_ref[...]
        r = lax.rsqrt(jnp.mean(x * x, axis=-1, keepdims=True) + EPS)
        xhat = x * r
        dg_ref[...] += jnp.sum(dh_ * xhat, axis=0, keepdims=True)
        dxn = dh_ * g_ref[...]
        gx_ref[...] = dout_ref[...] + r * (dxn - xhat * jnp.mean(dxn * xhat, axis=-1, keepdims=True))

    row = pl.BlockSpec((tr, D), lambda i: (i, 0))
    vec = pl.BlockSpec((1, D), lambda i: (0, 0))
    return pl.pallas_call(
        body, name="pre_norm_bwd", grid=(T // tr,),
        in_specs=[row, row, vec, row],
        out_specs=[row, vec],
        out_shape=[jax.ShapeDtypeStruct((T, D), F32), jax.ShapeDtypeStruct((1, D), F32)],
        compiler_params=_cp(("arbitrary",)),
    )(dh, x2, g, dout)


def _post_norm_loss(y, x2, tgt, g):
    tr = 256

    def body(y_ref, x_ref, t_ref, g_ref, dy_ref, dout_ref, dg_ref, loss_ref):
        @pl.when(pl.program_id(0) == 0)
        def _():
            dg_ref[...] = jnp.zeros_like(dg_ref)
            loss_ref[...] = jnp.zeros_like(loss_ref)
        yv = y_ref[...]
        gv = g_ref[...]
        r = lax.rsqrt(jnp.mean(yv * yv, axis=-1, keepdims=True) + EPS)
        yhat = yv * r
        err = x_ref[...] + yhat * gv - t_ref[...]
        loss_ref[...] += jnp.sum(jnp.sum(err * err, axis=1, keepdims=True), axis=0, keepdims=True) * (0.5 / D)
        dout = err * (1.0 / D)
        dout_ref[...] = dout
        dg_ref[...] += jnp.sum(dout * yhat, axis=0, keepdims=True)
        dyn = dout * gv
        dy_ref[...] = (r * (dyn - yhat * jnp.mean(dyn * yhat, axis=-1, keepdims=True))).astype(BF16)

    row = pl.BlockSpec((tr, D), lambda i: (i, 0))
    vec = pl.BlockSpec((1, D), lambda i: (0, 0))
    return pl.pallas_call(
        body, name="post_norm_loss", grid=(T // tr,),
        in_specs=[row, row, row, vec],
        out_specs=[row, row, vec, pl.BlockSpec((1, 128), lambda i: (0, 0))],
        out_shape=[jax.ShapeDtypeStruct((T, D), BF16), jax.ShapeDtypeStruct((T, D), F32),
                   jax.ShapeDtypeStruct((1, D), F32), jax.ShapeDtypeStruct((1, 128), F32)],
        compiler_params=_cp(("arbitrary",)),
    )(y, x2, tgt, g)


NN = (((1,), (0,)), ((), ()))
NT = (((1,), (1,)), ((), ()))
TN = (((0,), (0,)), ((), ()))


def _as_index(v):
    return jnp.reshape(v, (1,)).astype(jnp.int32)


def _matmul(a, b, *, name, dims, grid, a_block, a_map, b_block, b_map, o_block, o_map, out_shape, out_dtype=F32,
            index=None, prev=None, deps=()):
    assert out_dtype == F32
    extra = ([] if prev is None else [prev]) + [d for d in deps if d is not None]
    first = 0 if index is None else 1

    def body(*refs):
        a_ref, b_ref, o_ref = refs[first], refs[first + 1], refs[first + 2 + len(extra)]
        k = pl.program_id(2)
        part = lax.dot_general(a_ref[...], b_ref[...], dims, preferred_element_type=F32)

        @pl.when(k == 0)
        def _():
            o_ref[...] = part

        @pl.when(k > 0)
        def _():
            o_ref[...] += part

    in_specs = [pl.BlockSpec(a_block, a_map), pl.BlockSpec(b_block, b_map)] + [pl.BlockSpec(memory_space=pl.ANY)] * len(extra)
    out_specs = pl.BlockSpec(o_block, o_map)
    kwargs = dict(name=name, out_shape=jax.ShapeDtypeStruct(out_shape, out_dtype),
                  input_output_aliases={} if prev is None else {first + 2: 0},
                  compiler_params=_cp(("parallel", "parallel", "arbitrary")))
    if index is None:
        return pl.pallas_call(body, grid=grid, in_specs=in_specs, out_specs=out_specs, **kwargs)(a, b, *extra)
    grid_spec = pltpu.PrefetchScalarGridSpec(num_scalar_prefetch=1, grid=grid, in_specs=in_specs, out_specs=out_specs)
    return pl.pallas_call(body, grid_spec=grid_spec, **kwargs)(_as_index(index), a, b, *extra)


def _mm_nn(a, b, name, tm, tn, tk):
    m, kd = a.shape
    n = b.shape[1]
    return _matmul(a, b, name=name, dims=NN, grid=(m // tm, n // tn, kd // tk),
                   a_block=(tm, tk), a_map=lambda i, j, k: (i, k),
                   b_block=(tk, tn), b_map=lambda i, j, k: (k, j),
                   o_block=(tm, tn), o_map=lambda i, j, k: (i, j), out_shape=(m, n))


def _mm_nt(a, b, name, tm, tn, tk):
    m, kd = a.shape
    n = b.shape[0]
    return _matmul(a, b, name=name, dims=NT, grid=(m // tm, n // tn, kd // tk),
                   a_block=(tm, tk), a_map=lambda i, j, k: (i, k),
                   b_block=(tn, tk), b_map=lambda i, j, k: (j, k),
                   o_block=(tm, tn), o_map=lambda i, j, k: (i, j), out_shape=(m, n))


def _mm_tn(a, b, name, tm, tn, tk):
    kd, m = a.shape
    n = b.shape[1]
    return _matmul(a, b, name=name, dims=TN, grid=(m // tm, n // tn, kd // tk),
                   a_block=(tk, tm), a_map=lambda i, j, k: (k, i),
                   b_block=(tk, tn), b_map=lambda i, j, k: (k, j),
                   o_block=(tm, tn), o_map=lambda i, j, k: (i, j), out_shape=(m, n))


W_TN = 1792
W_NJ = WB // W_TN


def _proj_part(h, wg, chip, nblocks, name, prev=None):
    tm, tk = 1024, 1024
    shift = 0 if nblocks == 1 else 1

    def blk(j, ix):
        return (ix[0] + shift + j // W_NJ) % NCHIP

    return _matmul(h, wg, name=name, dims=NN, grid=(T // tm, nblocks * W_NJ, D // tk), index=chip, prev=prev,
                   a_block=(tm, tk), a_map=lambda i, j, k, ix: (i, k),
                   b_block=(tk, W_TN), b_map=lambda i, j, k, ix: (blk(j, ix) * (D // tk) + k, j % W_NJ),
                   o_block=(tm, W_TN), o_map=lambda i, j, k, ix: (i, blk(j, ix) * W_NJ + j % W_NJ), out_shape=(T, NC))


def _dh(dproj, wg, dep=None):
    tm, tn = 1024, 1024
    return _matmul(dproj, wg, name="dh", dims=NT, grid=(T // tm, D // tn, NC // W_TN), deps=(dep,),
                   a_block=(tm, W_TN), a_map=lambda i, j, k: (i, k),
                   b_block=(tn, W_TN), b_map=lambda i, j, k: ((k // W_NJ) * (D // tn) + j, k % W_NJ),
                   o_block=(tm, tn), o_map=lambda i, j, k: (i, j), out_shape=(T, D))


def _grad_w_in(h, dproj):
    tm, tk = 1024, 1024
    return _matmul(h, dproj, name="grad_w_in", dims=TN, grid=(D // tm, NC // W_TN, T // tk),
                   a_block=(tk, tm), a_map=lambda i, j, k: (k, i),
                   b_block=(tk, W_TN), b_map=lambda i, j, k: (k, j),
                   o_block=(tm, W_TN), o_map=lambda i, j, k: ((j // W_NJ) * (D // tm) + i, j % W_NJ),
                   out_shape=(NCHIP * D, WB))


def _rope_tables(pos, half):
    inv = 1.0 / (ROPE_THETA ** (jnp.arange(half, dtype=F32) / half))
    ang = pos.astype(F32)[:, None] * inv[None, :]
    cos, sin = jnp.cos(ang), jnp.sin(ang)
    return jnp.concatenate([cos, cos], axis=1), jnp.concatenate([-sin, sin], axis=1)


def _band_mask(r0):
    qi = lax.broadcasted_iota(jnp.int32, (128, 256), 0)
    kk = lax.broadcasted_iota(jnp.int32, (128, 256), 1)
    return (kk >= qi) & (kk <= qi + 128) & (kk + r0 >= 128)


def _dil_rows(r, n, d):
    if d == 1:
        return pl.ds(pl.multiple_of(n * 128, 128), 128)
    return pl.ds(r + d * 128 * n, 128, stride=d)


def _attn_fwd(proj, cosf, sinf, g):
    d = DIL[g]
    ln = S // d
    nblk = ln // 128
    proj_v = proj.reshape(NB, S, NC)

    def body(q_ref, k_ref, v_ref, cos_ref, sin_ref, o_ref, l_ref, k_s, v_s):
        k_s[:, pl.ds(0, 128), :] = jnp.zeros((d, 128, HD), BF16)
        v_s[:, pl.ds(0, 128), :] = jnp.zeros((d, 128, HD), BF16)

        def prep(i, carry):
            r, n = i // nblk, i % nblk
            rows = _dil_rows(r, n, d)
            dst = pl.ds(pl.multiple_of(n * 128 + 128, 128), 128)
            k_s[r, dst, :] = _rope(k_ref[rows, :], cos_ref[rows, :], sin_ref[rows, :], HD // 2).astype(BF16)
            v_s[r, dst, :] = v_ref[rows, :].astype(BF16)
            return carry

        lax.fori_loop(0, d * nblk, prep, 0)

        def step(i, carry):
            r, n = i // nblk, i % nblk
            rows = _dil_rows(r, n, d)
            r0 = pl.multiple_of(n * 128, 128)
            qr = _rope(q_ref[rows, :], cos_ref[rows, :], sin_ref[rows, :], HD // 2).astype(BF16)
            kw = k_s[r, pl.ds(r0, 256), :]
            vw = v_s[r, pl.ds(r0, 256), :]
            sc = lax.dot_general(qr, kw, NT, preferred_element_type=F32) * A_SCALE
            sc = jnp.where(_band_mask(r0), sc, NEG)
            m = jnp.max(sc, axis=1, keepdims=True)
            p = jnp.exp(sc - m)
            l = jnp.sum(p, axis=1, keepdims=True)
            o_ref[rows, :] = jnp.dot(p.astype(BF16), vw, preferred_element_type=F32) / l
            l_ref[rows, :] = jnp.broadcast_to(m + jnp.log(l), (128, HD))
            return carry

        lax.fori_loop(0, d * nblk, step, 0)

    def col(off):
        return lambda b, h: (b, 0, off // HD + HPG * g + h)

    blk = (None, S, HD)
    tab = pl.BlockSpec((S, HD), lambda b, h: (0, 0))
    out = pl.BlockSpec(blk, lambda b, h: (b, 0, h))
    o, l = pl.pallas_call(
        body, name=f"attn_fwd_d{d}", grid=(NB, HPG),
        in_specs=[pl.BlockSpec(blk, col(QA)), pl.BlockSpec(blk, col(KA)), pl.BlockSpec(blk, col(VA)), tab, tab],
        out_specs=[out, out],
        out_shape=[jax.ShapeDtypeStruct((NB, S, HPG * HD), F32)] * 2,
        scratch_shapes=[pltpu.VMEM((d, ln + 128, HD), BF16), pltpu.VMEM((d, ln + 128, HD), BF16)],
        compiler_params=_cp(("parallel", "parallel")),
    )(proj_v, proj_v, proj_v, cosf, sinf)
    return o.reshape(T, HPG * HD), l.reshape(T, HPG * HD)


def _attn_bwd(proj, cosf, sinf, da, lse, delta, dproj, g):
    d = DIL[g]
    ln = S // d
    nblk = ln // 128
    proj_v = proj.reshape(NB, S, NC)
    dproj_v = dproj.reshape(NB, S, NC)
    da_v = da.reshape(NB, S, AW)
    lse_v = lse.reshape(NB, S, HPG * HD)
    delta_v = delta.reshape(NB, S, HPG * HD)

    def body(q_ref, k_ref, v_ref, cos_ref, sin_ref, da_ref, lse_ref, dl_ref, dp_in_ref, o_ref,
             k_s, v_s, stg, dk_s, dv_s):
        del dp_in_ref
        w = pl.program_id(2)

        def emit():
            def cast(n, carry):
                rows = pl.ds(pl.multiple_of(n * 256, 256), 256)
                o_ref[rows, :] = stg[rows, :].astype(BF16)
                return carry

            lax.fori_loop(0, S // 256, cast, 0)

        @pl.when(w == 0)
        def _():
            k_s[:, pl.ds(0, 128), :] = jnp.zeros((d, 128, HD), BF16)
            v_s[:, pl.ds(0, 128), :] = jnp.zeros((d, 128, HD), BF16)
            dk_s[...] = jnp.zeros_like(dk_s)
            dv_s[...] = jnp.zeros_like(dv_s)

            def prep(i, carry):
                r, n = i // nblk, i % nblk
                rows = _dil_rows(r, n, d)
                dst = pl.ds(pl.multiple_of(n * 128 + 128, 128), 128)
                k_s[r, dst, :] = _rope(k_ref[rows, :], cos_ref[rows, :], sin_ref[rows, :], HD // 2).astype(BF16)
                v_s[r, dst, :] = v_ref[rows, :].astype(BF16)
                return carry

            lax.fori_loop(0, d * nblk, prep, 0)

            def step(i, carry):
                r, n = i // nblk, i % nblk
                rows = _dil_rows(r, n, d)
                r0 = pl.multiple_of(n * 128, 128)
                win = pl.ds(r0, 256)
                cos, sin = cos_ref[rows, :], sin_ref[rows, :]
                qr = _rope(q_ref[rows, :], cos, sin, HD // 2).astype(BF16)
                kw = k_s[r, win, :]
                vw = v_s[r, win, :]
                sc = lax.dot_general(qr, kw, NT, preferred_element_type=F32) * A_SCALE
                sc = jnp.where(_band_mask(r0), sc, NEG)
                p = jnp.exp(sc - lse_ref[rows, :][:, :1])
                da_b = da_ref[rows, :].astype(BF16)
                dp = lax.dot_general(da_b, vw, NT, preferred_element_type=F32)
                ds_b = (p * (dp - dl_ref[rows, :][:, :1]) * A_SCALE).astype(BF16)
                p_b = p.astype(BF16)
                dq = jnp.dot(ds_b, kw, preferred_element_type=F32)
                stg[rows, :] = _rope_t(dq, cos, sin, HD // 2)
                dk_s[r, win, :] += lax.dot_general(ds_b, qr, TN, preferred_element_type=F32)
                dv_s[r, win, :] += lax.dot_general(p_b, da_b, TN, preferred_element_type=F32)
                return carry

            lax.fori_loop(0, d * nblk, step, 0)
            emit()

        @pl.when(w == 1)
        def _():
            def put(i, carry):
                r, n = i // nblk, i % nblk
                rows = _dil_rows(r, n, d)
                src = pl.ds(pl.multiple_of(n * 128 + 128, 128), 128)
                stg[rows, :] = _rope_t(dk_s[r, src, :], cos_ref[rows, :], sin_ref[rows, :], HD // 2)
                return carry

            lax.fori_loop(0, d * nblk, put, 0)
            emit()

        @pl.when(w == 2)
        def _():
            def put(i, carry):
                r, n = i // nblk, i % nblk
                src = pl.ds(pl.multiple_of(n * 128 + 128, 128), 128)
                stg[_dil_rows(r, n, d), :] = dv_s[r, src, :]
                return carry

            lax.fori_loop(0, d * nblk, put, 0)
            emit()

    def col(off):
        return lambda b, h, w: (b, 0, off // HD + HPG * g + h)

    blk = (None, S, HD)
    tab = pl.BlockSpec((S, HD), lambda b, h, w: (0, 0))
    per_head = pl.BlockSpec(blk, lambda b, h, w: (b, 0, h))
    out = pl.pallas_call(
        body, name=f"attn_bwd_d{d}", grid=(NB, HPG, 3),
        in_specs=[pl.BlockSpec(blk, col(QA)), pl.BlockSpec(blk, col(KA)), pl.BlockSpec(blk, col(VA)), tab, tab,
                  pl.BlockSpec(blk, col(0)), per_head, per_head, pl.BlockSpec(memory_space=pl.ANY)],
        out_specs=pl.BlockSpec(blk, lambda b, h, w: (b, 0, (AW // HD) * w + HPG * g + h)),
        out_shape=jax.ShapeDtypeStruct(dproj_v.shape, BF16),
        input_output_aliases={8: 0},
        scratch_shapes=[pltpu.VMEM((d, ln + 128, HD), BF16), pltpu.VMEM((d, ln + 128, HD), BF16),
                        pltpu.VMEM((S, HD), F32), pltpu.VMEM((d, ln + 128, HD), F32), pltpu.VMEM((d, ln + 128, HD), F32)],
        compiler_params=_cp(("arbitrary",) * 3),
    )(proj_v, proj_v, proj_v, cosf, sinf, da_v, lse_v, delta_v, dproj_v)
    return out.reshape(T, NC)


def _attn_mix(proj, os_, ls_):
    tr = 256
    gw = HPG * HD

    def body(o0, o1, o2, l0, l1, l2, z_ref, cat_ref):
        m = jnp.maximum(jnp.maximum(l0[...], l1[...]), l2[...])
        e = [jnp.exp(l[...] - m) for l in (l0, l1, l2)]
        inv = 1.0 / (e[0] + e[1] + e[2])
        for gi, o in enumerate((o0, o1, o2)):
            z = z_ref[:, gi * gw:(gi + 1) * gw]
            cat_ref[:, gi * gw:(gi + 1) * gw] = (o[...] * (e[gi] * inv) * (z * _sigmoid(z))).astype(BF16)

    grp = pl.BlockSpec((tr, gw), lambda i: (i, 0))
    return pl.pallas_call(
        body, name="attn_mix", grid=(T // tr,),
        in_specs=[grp] * 6 + [pl.BlockSpec((tr, AW), lambda i: (i, ZA // AW))],
        out_specs=pl.BlockSpec((tr, AW), lambda i: (i, 0)),
        out_shape=jax.ShapeDtypeStruct((T, D), BF16),
        compiler_params=_cp(("parallel",)),
    )(*os_, *ls_, proj)


def _attn_mix_bwd(dcat, proj, os_, ls_, dep=None):
    tr = 256
    gw = HPG * HD
    deps = [] if dep is None else [dep]

    def body(dy_ref, o0, o1, o2, l0, l1, l2, z_ref, *rest):
        da_ref, lse_ref, dl_ref, dz_ref = rest[len(deps):]
        m = jnp.maximum(jnp.maximum(l0[...], l1[...]), l2[...])
        e = [jnp.exp(l[...] - m) for l in (l0, l1, l2)]
        den = e[0] + e[1] + e[2]
        inv = 1.0 / den
        lse_ref[...] = m + jnp.log(den)
        acc = jnp.zeros((tr, gw), F32)
        for gi, o in enumerate((o0, o1, o2)):
            cols = slice(gi * gw, (gi + 1) * gw)
            z = z_ref[:, cols]
            dy = dy_ref[:, cols]
            sg = _sigmoid(z)
            a = o[...] * (e[gi] * inv)
            da = dy * (z * sg)
            da_ref[:, cols] = da
            dz_ref[:, cols] = (dy * a * (sg * (1.0 + z * (1.0 - sg)))).astype(BF16)
            acc = acc + da * a
        for hh in range(HPG):
            cols = slice(hh * HD, (hh + 1) * HD)
            dl_ref[:, cols] = jnp.broadcast_to(jnp.sum(acc[:, cols], axis=1, keepdims=True), (tr, HD))

    grp = pl.BlockSpec((tr, gw), lambda i: (i, 0))
    return pl.pallas_call(
        body, name="attn_mix_bwd", grid=(T // tr,),
        in_specs=[pl.BlockSpec((tr, AW), lambda i: (i, 0))] + [grp] * 6 + [pl.BlockSpec((tr, AW), lambda i: (i, ZA // AW))]
        + [pl.BlockSpec(memory_space=pl.ANY)] * len(deps),
        out_specs=[pl.BlockSpec((tr, AW), lambda i: (i, 0)), grp, grp, pl.BlockSpec((tr, AW), lambda i: (i, ZA // AW))],
        out_shape=[jax.ShapeDtypeStruct((T, AW), F32), jax.ShapeDtypeStruct((T, gw), F32),
                   jax.ShapeDtypeStruct((T, gw), F32), jax.ShapeDtypeStruct((T, NC), BF16)],
        compiler_params=_cp(("parallel",)),
    )(dcat, *os_, *ls_, proj, *deps)


CT = 128


def _shift_down(x, n):
    rows = lax.broadcasted_iota(jnp.int32, x.shape, 0)
    return jnp.where(rows >= n, pltpu.roll(x, n, 0), 0.0)


def _shift_up(x, n):
    rows = lax.broadcasted_iota(jnp.int32, x.shape, 0)
    return jnp.where(rows < x.shape[0] - n, pltpu.roll(x, x.shape[0] - n, 0), 0.0)


def _conv_fwd(proj, conv_w, cat):
    proj_v = proj.reshape(NB, S, NC)
    cat_v = cat.reshape(NB, S, D)

    def body(u_ref, b_ref, c_ref, z_ref, w_ref, cat_in, o_ref):
        del cat_in
        cu = c_ref[...] * u_ref[...]
        cv = _shift_down(cu, 2) * w_ref[0:1, :] + _shift_down(cu, 1) * w_ref[1:2, :] + cu * w_ref[2:3, :]
        z = z_ref[...]
        o_ref[...] = (b_ref[...] * cv * (z * _sigmoid(z))).astype(BF16)

    def seg(off):
        return pl.BlockSpec((None, S, CT), lambda b, j: (b, 0, off // CT + j))

    out = pl.pallas_call(
        body, name="conv_fwd", grid=(NB, CW // CT),
        in_specs=[seg(UC), seg(BC), seg(CC), seg(ZC), pl.BlockSpec((3, CT), lambda b, j: (0, j)),
                  pl.BlockSpec(memory_space=pl.ANY)],
        out_specs=pl.BlockSpec((None, S, CT), lambda b, j: (b, 0, AW // CT + j)),
        out_shape=jax.ShapeDtypeStruct((NB, S, D), BF16),
        input_output_aliases={5: 0},
        compiler_params=_cp(("parallel", "parallel")),
    )(proj_v, proj_v, proj_v, proj_v, conv_w, cat_v)
    return out.reshape(T, D)


def _conv_bwd(dcat, proj, conv_w, dproj):
    proj_v = proj.reshape(NB, S, NC)
    dproj_v = dproj.reshape(NB, S, NC)
    dcat_v = dcat.reshape(NB, S, D)

    def body(dy_ref, u_ref, b_ref, c_ref, z_ref, w_ref, dp_in, o_ref, dw_ref, st):
        del dp_in
        b = pl.program_id(1)
        w = pl.program_id(2)

        @pl.when((b == 0) & (w == 0))
        def _():
            dw_ref[...] = jnp.zeros_like(dw_ref)

        @pl.when(w == 0)
        def _():
            u, c, z, bb, dy = u_ref[...], c_ref[...], z_ref[...], b_ref[...], dy_ref[...]
            cu = c * u
            s1 = _shift_down(cu, 1)
            s2 = _shift_down(cu, 2)
            cv = s2 * w_ref[0:1, :] + s1 * w_ref[1:2, :] + cu * w_ref[2:3, :]
            sg = _sigmoid(z)
            sz = z * sg
            dcv = dy * bb * sz
            st[0] = dy * cv * sz
            st[2] = dy * bb * cv * (sg * (1.0 + z * (1.0 - sg)))
            dw_ref[0:1, :] += jnp.sum(dcv * s2, axis=0, keepdims=True)
            dw_ref[1:2, :] += jnp.sum(dcv * s1, axis=0, keepdims=True)
            dw_ref[2:3, :] += jnp.sum(dcv * cu, axis=0, keepdims=True)
            dcu = dcv * w_ref[2:3, :] + _shift_up(dcv, 1) * w_ref[1:2, :] + _shift_up(dcv, 2) * w_ref[0:1, :]
            st[1] = dcu * u
            o_ref[...] = (dcu * c).astype(BF16)

        for k in range(3):
            @pl.when(w == k + 1)
            def _(k=k):
                o_ref[...] = st[k].astype(BF16)

    def seg(off):
        return pl.BlockSpec((None, S, CT), lambda j, b, w: (b, 0, off // CT + j))

    out, dw = pl.pallas_call(
        body, name="conv_bwd", grid=(CW // CT, NB, 4),
        in_specs=[pl.BlockSpec((None, S, CT), lambda j, b, w: (b, 0, AW // CT + j)),
                  seg(UC), seg(BC), seg(CC), seg(ZC), pl.BlockSpec((3, CT), lambda j, b, w: (0, j)),
                  pl.BlockSpec(memory_space=pl.ANY)],
        out_specs=[pl.BlockSpec((None, S, CT), lambda j, b, w: (b, 0, (UC + w * CW) // CT + j)),
                   pl.BlockSpec((3, CT), lambda j, b, w: (0, j))],
        out_shape=[jax.ShapeDtypeStruct((NB, S, NC), BF16), jax.ShapeDtypeStruct((3, CW), F32)],
        input_output_aliases={6: 0},
        scratch_shapes=[pltpu.VMEM((3, S, CT), F32)],
        compiler_params=_cp(("arbitrary",) * 3),
    )(dcat_v, proj_v, proj_v, proj_v, proj_v, conv_w, dproj_v)
    return out.reshape(T, NC), dw


XT = 512


def _cross_fwd(proj, mkv, cosq, sinq, cosm, sinm, cat):
    proj_v = proj.reshape(NB, S, NC)
    mkv_v = mkv.reshape(NB, MLEN, 2 * XW)
    cat_v = cat.reshape(NB, S, D)

    def body(q_ref, z_ref, mk_ref, mv_ref, cq, sq, cm, sm, cat_in, o_ref):
        del cat_in
        mkr = _rope(mk_ref[...], cm[...], sm[...], XHD // 2).astype(BF16)
        qr = _rope(q_ref[...], cq[...], sq[...], XHD // 2).astype(BF16)
        sc = lax.dot_general(qr, mkr, NT, preferred_element_type=F32) * X_SCALE
        p = jnp.exp(sc - jnp.max(sc, axis=1, keepdims=True))
        p = p / jnp.sum(p, axis=1, keepdims=True)
        ox = jnp.dot(p.astype(BF16), mv_ref[...].astype(BF16), preferred_element_type=F32)
        z = z_ref[...]
        o_ref[...] = (ox * (z * _sigmoid(z))).astype(BF16)

    def seg(off):
        return pl.BlockSpec((None, XT, XHD), lambda b, h, t: (b, t, off // XHD + h))

    qtab = pl.BlockSpec((XT, XHD), lambda b, h, t: (t, 0))
    mtab = pl.BlockSpec((MLEN, XHD), lambda b, h, t: (0, 0))
    out = pl.pallas_call(
        body, name="cross_fwd", grid=(NB, NXH, S // XT),
        in_specs=[seg(QX), seg(ZX),
                  pl.BlockSpec((None, MLEN, XHD), lambda b, h, t: (b, 0, h)),
                  pl.BlockSpec((None, MLEN, XHD), lambda b, h, t: (b, 0, NXH + h)),
                  qtab, qtab, mtab, mtab, pl.BlockSpec(memory_space=pl.ANY)],
        out_specs=pl.BlockSpec((None, XT, XHD), lambda b, h, t: (b, t, (AW + CW) // XHD + h)),
        out_shape=jax.ShapeDtypeStruct((NB, S, D), BF16),
        input_output_aliases={8: 0},
        compiler_params=_cp(("parallel",) * 3),
    )(proj_v, proj_v, mkv_v, mkv_v, cosq, sinq, cosm, sinm, cat_v)
    return out.reshape(T, D)


def _cross_bwd(dcat, proj, mkv, cosq, sinq, cosm, sinm, dproj):
    proj_v = proj.reshape(NB, S, NC)
    dproj_v = dproj.reshape(NB, S, NC)
    dcat_v = dcat.reshape(NB, S, D)
    mkv_v = mkv.reshape(NB, MLEN, 2 * XW)
    nt = S // XT

    def body(dy_ref, q_ref, z_ref, mk_ref, mv_ref, cq, sq, cm, sm, dp_in, o_ref, dmk_ref, dmv_ref, dz_s):
        del dp_in
        t = pl.program_id(2)
        w = pl.program_id(3)

        @pl.when((t == 0) & (w == 0))
        def _():
            dmk_ref[...] = jnp.zeros_like(dmk_ref)
            dmv_ref[...] = jnp.zeros_like(dmv_ref)

        @pl.when(w == 0)
        def _():
            mkr = _rope(mk_ref[...], cm[...], sm[...], XHD // 2).astype(BF16)
            mv_b = mv_ref[...].astype(BF16)
            qr = _rope(q_ref[...], cq[...], sq[...], XHD // 2).astype(BF16)
            sc = lax.dot_general(qr, mkr, NT, preferred_element_type=F32) * X_SCALE
            p = jnp.exp(sc - jnp.max(sc, axis=1, keepdims=True))
            p = p / jnp.sum(p, axis=1, keepdims=True)
            p_b = p.astype(BF16)
            ox = jnp.dot(p_b, mv_b, preferred_element_type=F32)
            z = z_ref[...]
            dy = dy_ref[...]
            sg = _sigmoid(z)
            dz_s[...] = dy * ox * (sg * (1.0 + z * (1.0 - sg)))
            dox_b = (dy * (z * sg)).astype(BF16)
            dp = lax.dot_general(dox_b, mv_b, NT, preferred_element_type=F32)
            ds_b = (p * (dp - jnp.sum(dp * p, axis=1, keepdims=True)) * X_SCALE).astype(BF16)
            dq = jnp.dot(ds_b, mkr, preferred_element_type=F32)
            o_ref[...] = _rope_t(dq, cq[...], sq[...], XHD // 2).astype(BF16)
            dmk_ref[...] += lax.dot_general(ds_b, qr, TN, preferred_element_type=F32)
            dmv_ref[...] += lax.dot_general(p_b, dox_b, TN, preferred_element_type=F32)

        @pl.when(w == 1)
        def _():
            o_ref[...] = dz_s[...].astype(BF16)

        @pl.when((t == nt - 1) & (w == 1))
        def _():
            dmk_ref[...] = _rope_t(dmk_ref[...], cm[...], sm[...], XHD // 2)

    def seg(off):
        return pl.BlockSpec((None, XT, XHD), lambda b, h, t, w: (b, t, off // XHD + h))

    qtab = pl.BlockSpec((XT, XHD), lambda b, h, t, w: (t, 0))
    mtab = pl.BlockSpec((MLEN, XHD), lambda b, h, t, w: (0, 0))
    macc = pl.BlockSpec((None, MLEN, XHD), lambda b, h, t, w: (b, 0, h))
    out, dmk, dmv = pl.pallas_call(
        body, name="cross_bwd", grid=(NB, NXH, nt, 2),
        in_specs=[pl.BlockSpec((None, XT, XHD), lambda b, h, t, w: (b, t, (AW + CW) // XHD + h)),
                  seg(QX), seg(ZX),
                  pl.BlockSpec((None, MLEN, XHD), lambda b, h, t, w: (b, 0, h)),
                  pl.BlockSpec((None, MLEN, XHD), lambda b, h, t, w: (b, 0, NXH + h)),
                  qtab, qtab, mtab, mtab, pl.BlockSpec(memory_space=pl.ANY)],
        out_specs=[pl.BlockSpec((None, XT, XHD), lambda b, h, t, w: (b, t, (QX + w * XW) // XHD + h)), macc, macc],
        out_shape=[jax.ShapeDtypeStruct((NB, S, NC), BF16), jax.ShapeDtypeStruct((NB, MLEN, XW), F32),
                   jax.ShapeDtypeStruct((NB, MLEN, XW), F32)],
        input_output_aliases={9: 0},
        scratch_shapes=[pltpu.VMEM((XT, XHD), F32)],
        compiler_params=_cp(("arbitrary",) * 4),
    )(dcat_v, proj_v, proj_v, mkv_v, mkv_v, cosq, sinq, cosm, sinm, dproj_v)
    return out.reshape(T, NC), dmk, dmv


def _local_step(x, mem, pre_norm, conv_w, mem_norm, post_norm, tgt, chip, comm):
    x2 = x.reshape(T, D)
    mem2 = mem.reshape(NB * MLEN, D)
    tgt2 = tgt.reshape(T, D)
    cosa, sina = _rope_tables(jnp.arange(S), HD // 2)
    cosq, sinq = _rope_tables(jnp.arange(S) + MLEN, XHD // 2)
    cosm, sinm = _rope_tables(jnp.arange(MLEN), XHD // 2)

    h = _rms_fwd(x2, pre_norm, "pre_norm_fwd", dep=comm.gather_started())
    memn = _rms_fwd(mem2, mem_norm, "mem_norm_fwd")
    proj = _proj_part(h, comm.w_in_own(), chip, 1, "proj_own")
    wg_in = comm.w_in_all(after=proj)
    proj = _proj_part(h, wg_in, chip, NCHIP - 1, "proj_rest", prev=proj)
    wg_kv, wg_out = comm.w_rest(after=proj)
    mkv = _mm_nn(memn, wg_kv, "mkv", NB * MLEN, 1024, 512)

    fw = [_attn_fwd(proj, cosa, sina, g) for g in range(3)]
    os_ = [f[0] for f in fw]
    ls_ = [f[1] for f in fw]
    cat = _attn_mix(proj, os_, ls_)
    cat = _conv_fwd(proj, conv_w, cat)
    cat = _cross_fwd(proj, mkv, cosq, sinq, cosm, sinm, cat)
    y = _mm_nn(cat, wg_out, "out_proj", 1024, 1024, 2048)
    dy, dout, d_post, loss = _post_norm_loss(y, x2, tgt2, post_norm)

    dcat = _mm_nt(dy, wg_out, "dcat", 1024, 1024, 2048)
    g_out = _mm_tn(cat, dy, "grad_w_out", 1024, 1024, 2048)
    red_a = comm.reduce_start([g_out.reshape(NCHIP, D // NCHIP, D)], "a")
    da, lse, delta, dproj = _attn_mix_bwd(dcat, proj, os_, ls_, dep=red_a[-1])
    for g in range(3):
        dproj = _attn_bwd(proj, cosa, sina, da, lse, delta, dproj, g)
    dproj, d_conv = _conv_bwd(dcat, proj, conv_w, dproj)
    dproj, dmk, dmv = _cross_bwd(dcat, proj, mkv, cosq, sinq, cosm, sinm, dproj)

    dmkv = jnp.concatenate([dmk, dmv], axis=-1).reshape(NB * MLEN, 2 * XW)
    dmkv_b = dmkv.astype(BF16)
    g_kv = _mm_tn(memn, dmkv_b, "grad_w_mem_kv", 1024, 1024, NB * MLEN)
    dmemn = _mm_nt(dmkv_b, wg_kv, "dmemn", NB * MLEN, 1024, 512)
    d_mem = _norm_gain_grad(dmemn, mem2, "mem_norm_bwd")

    g_in = _grad_w_in(h, dproj)
    red_b = comm.reduce_start([g_in.reshape(NCHIP, D, WB), g_kv.reshape(NCHIP, D // NCHIP, 2 * XW)], "b")
    dh = _dh(dproj, wg_in, dep=red_b[-1])
    grad_x, d_pre = _pre_norm_bwd(dh, x2, pre_norm, dout)
    (r_out,) = comm.reduce_finish(red_a, "a", after=grad_x)
    r_in, r_kv = comm.reduce_finish(red_b, "b", after=grad_x)
    return loss, grad_x.reshape(NB, S, D), d_pre, d_mem, d_post, d_conv, r_in, r_kv, r_out


def _adamw(w, g, m, v, name):
    rows, cols = w.shape
    tr = rows if rows <= 512 else 512
    tc = cols if cols <= 1024 else 1024
    if cols % tc:
        tc = 896

    def body(w_ref, g_ref, m_ref, v_ref, d_ref, nm_ref, nv_ref):
        gv = g_ref[...]
        nm = ADAM_B1 * m_ref[...] + (1.0 - ADAM_B1) * gv
        nv = ADAM_B2 * v_ref[...] + (1.0 - ADAM_B2) * (gv * gv)
        m_hat = nm / (1.0 - ADAM_B1 ** ADAM_STEP)
        v_hat = nv / (1.0 - ADAM_B2 ** ADAM_STEP)
        d_ref[...] = -ADAM_LR * (m_hat / (jnp.sqrt(v_hat) + ADAM_EPS) + ADAM_WD * w_ref[...])
        nm_ref[...] = nm
        nv_ref[...] = nv

    blk = pl.BlockSpec((tr, tc), lambda i, j: (i, j))
    sds = jax.ShapeDtypeStruct((rows, cols), F32)
    return pl.pallas_call(
        body, name=name, grid=(rows // tr, cols // tc),
        in_specs=[blk] * 4, out_specs=[blk] * 3, out_shape=[sds] * 3,
        compiler_params=_cp(("parallel", "parallel")),
    )(w, g, m, v)


def _place():
    return lax.axis_index("x"), lax.axis_index("y"), lax.axis_index("c")


def _other_chips(x, y):
    return [(1 - x, y), (x, 1 - y), (1 - x, 1 - y)]


def _tile_cols(cols):
    return cols if cols <= 1024 else (1024 if cols % 1024 == 0 else 896)


def _cast_own(w, chip, name):
    rows, cols = w.shape
    tr, tc = 512, _tile_cols(cols)

    def body(chip_ref, w_ref, o_ref):
        del chip_ref
        o_ref[...] = w_ref[...].astype(BF16)

    grid_spec = pltpu.PrefetchScalarGridSpec(
        num_scalar_prefetch=1, grid=(rows // tr, cols // tc),
        in_specs=[pl.BlockSpec((tr, tc), lambda i, j, cr: (i, j))],
        out_specs=pl.BlockSpec((None, tr, tc), lambda i, j, cr: (cr[0], i, j)))
    return pl.pallas_call(
        body, name=name, grid_spec=grid_spec,
        out_shape=jax.ShapeDtypeStruct((NCHIP, rows, cols), BF16),
        compiler_params=_cp(("parallel", "parallel")),
    )(_as_index(chip), w)


HBM_SPEC = pl.BlockSpec(memory_space=pltpu.HBM)
SEM_SPEC = pl.BlockSpec(memory_space=pltpu.SEMAPHORE)
ANY_SPEC = pl.BlockSpec(memory_space=pl.ANY)
EFFECT = pltpu.SideEffectType.DATAFLOW_SIDE_EFFECTING
TOKEN = jax.ShapeDtypeStruct((8, 128), F32)


def _half(ref, chip, hc):
    hr = ref.shape[1] // 2
    return ref.at[chip, pl.ds(hc * hr, hr), :]


def _gather_copies(refs, send_sems, recv_sems):
    x, y, c = _place()
    out, inc = [], []
    for a, ref in enumerate(refs):
        for j, (px, py) in enumerate(_other_chips(x, y)):
            mine = _half(ref, 2 * x + y, c)
            theirs = _half(ref, 2 * px + py, c)
            sems = dict(send_sem=send_sems.at[3 * a + j], recv_sem=recv_sems.at[3 * a + j],
                        device_id=(px, py, c), device_id_type=MESH)
            out.append(pltpu.make_async_remote_copy(src_ref=mine, dst_ref=mine, **sems))
            inc.append(pltpu.make_async_remote_copy(src_ref=theirs, dst_ref=theirs, **sems))
    return out, inc


def _gather_start(bufs, name):
    n = len(bufs)

    def body(*refs):
        ins = refs[:n]
        send_sems, recv_sems = refs[n], refs[n + 1]
        token = refs[-1]
        out, _ = _gather_copies(ins, send_sems, recv_sems)
        for cp in out:
            cp.start()
        token[...] = jnp.zeros_like(token)

    res = pl.pallas_call(
        body, name=name,
        in_specs=[HBM_SPEC] * n,
        out_specs=[SEM_SPEC, SEM_SPEC] + [HBM_SPEC] * n + [pl.BlockSpec(memory_space=pltpu.VMEM)],
        out_shape=[pltpu.SemaphoreType.DMA((3 * n,)), pltpu.SemaphoreType.DMA((3 * n,))]
        + [pltpu.HBM(b.shape, b.dtype) for b in bufs] + [TOKEN],
        input_output_aliases={a: 2 + a for a in range(n)},
        compiler_params=pltpu.CompilerParams(has_side_effects=EFFECT),
    )(*[pltpu.with_memory_space_constraint(b, pltpu.HBM) for b in bufs])
    return res[0], res[1], list(res[2:2 + n]), res[-1]


def _gather_wait(bufs, send_sems, recv_sems, after, name):
    n = len(bufs)

    def body(*refs):
        ins = refs[:n]
        out, inc = _gather_copies(ins, refs[n], refs[n + 1])
        for cp in out:
            cp.wait_send()
        for cp in inc:
            cp.wait_recv()

    return pl.pallas_call(
        body, name=name,
        in_specs=[HBM_SPEC] * n + [SEM_SPEC, SEM_SPEC, ANY_SPEC],
        out_specs=[HBM_SPEC] * n,
        out_shape=[pltpu.HBM(b.shape, b.dtype) for b in bufs],
        input_output_aliases={a: a for a in range(n)},
        compiler_params=pltpu.CompilerParams(has_side_effects=EFFECT),
    )(*bufs, send_sems, recv_sems, after)


def _forward_halves(bufs, name):
    n = len(bufs)

    def body(*refs):
        outs = refs[n:2 * n]
        send_sems, recv_sems = refs[2 * n:]
        x, y, c = _place()
        cps, waits = [], []
        for a in range(n):
            for j, (px, py) in enumerate(_other_chips(x, y)):
                sems = dict(send_sem=send_sems.at[3 * a + j], recv_sem=recv_sems.at[3 * a + j],
                            device_id=(x, y, 1 - c), device_id_type=MESH)
                got = _half(outs[a], 2 * px + py, c)
                want = _half(outs[a], 2 * px + py, 1 - c)
                cps.append(pltpu.make_async_remote_copy(src_ref=got, dst_ref=got, **sems))
                waits.append(pltpu.make_async_remote_copy(src_ref=want, dst_ref=want, **sems))
        for cp in cps:
            cp.start()
        for cp in waits:
            cp.wait_recv()
        for cp in cps:
            cp.wait_send()

    return pl.pallas_call(
        body, name=name,
        in_specs=[ANY_SPEC] * n, out_specs=[ANY_SPEC] * n,
        out_shape=[jax.ShapeDtypeStruct(s.shape, s.dtype) for s in bufs],
        input_output_aliases={a: a for a in range(n)},
        scratch_shapes=[pltpu.SemaphoreType.DMA((3 * n,)), pltpu.SemaphoreType.DMA((3 * n,))],
    )(*bufs)


def _sibling_halves(parts, name):
    n = len(parts)

    def body(*refs):
        ins, outs = refs[:n], refs[n:2 * n]
        send_sems, recv_sems = refs[2 * n:]
        x, y, c = _place()
        cps = []
        for a in range(n):
            hr = ins[a].shape[1] // 2
            cps.append(pltpu.make_async_remote_copy(
                src_ref=ins[a].at[:, pl.ds((1 - c) * hr, hr), :], dst_ref=outs[a],
                send_sem=send_sems.at[a], recv_sem=recv_sems.at[a], device_id=(x, y, 1 - c), device_id_type=MESH))
        for cp in cps:
            cp.start()
        for cp in cps:
            cp.wait()

    hbm = pl.BlockSpec(memory_space=pl.ANY)
    return pl.pallas_call(
        body, name=name,
        in_specs=[hbm] * n, out_specs=[hbm] * n,
        out_shape=[jax.ShapeDtypeStruct((NCHIP, p.shape[1] // 2, p.shape[2]), p.dtype) for p in parts],
        scratch_shapes=[pltpu.SemaphoreType.DMA((n,)), pltpu.SemaphoreType.DMA((n,))],
    )(*parts)


def _pair_sum(part, got, name):
    nblk, rows, cols = part.shape
    hr = rows // 2
    tr = 512
    tc = cols if cols <= 1024 else (1024 if cols % 1024 == 0 else 896)
    c = lax.axis_index("c")
    nrt = hr // tr

    def body(c_ref, p_ref, g_ref, o_ref):
        del c_ref
        o_ref[...] = (p_ref[...] + g_ref[...]).astype(BF16)

    grid_spec = pltpu.PrefetchScalarGridSpec(
        num_scalar_prefetch=1, grid=(nblk, nrt, cols // tc),
        in_specs=[pl.BlockSpec((None, tr, tc), lambda b, i, j, cr: (b, cr[0] * nrt + i, j)),
                  pl.BlockSpec((None, tr, tc), lambda b, i, j, cr: (b, i, j))],
        out_specs=pl.BlockSpec((None, tr, tc), lambda b, i, j, cr: (b, i, j)))
    return pl.pallas_call(
        body, name=name, grid_spec=grid_spec,
        out_shape=jax.ShapeDtypeStruct((nblk, hr, cols), BF16),
        compiler_params=_cp(("parallel",) * 3),
    )(jnp.reshape(c, (1,)).astype(jnp.int32), part, got)


def _scatter_start(qs, name):
    n = len(qs)

    def body(*refs):
        ins, lands = refs[:n], refs[n:2 * n]
        token = refs[-1]
        for cp in _scatter_copies(ins, lands, refs[2 * n], refs[2 * n + 1]):
            cp.start()
        token[...] = jnp.zeros_like(token)

    lands = [lax.empty((3,) + q.shape[1:], q.dtype) for q in qs]
    res = pl.pallas_call(
        body, name=name,
        in_specs=[HBM_SPEC] * (2 * n),
        out_specs=[SEM_SPEC, SEM_SPEC] + [HBM_SPEC] * (2 * n) + [pl.BlockSpec(memory_space=pltpu.VMEM)],
        out_shape=[pltpu.SemaphoreType.DMA((3 * n,)), pltpu.SemaphoreType.DMA((3 * n,))]
        + [pltpu.HBM(b.shape, b.dtype) for b in qs + lands] + [TOKEN],
        input_output_aliases={a: 2 + a for a in range(2 * n)},
        compiler_params=pltpu.CompilerParams(has_side_effects=EFFECT),
    )(*[pltpu.with_memory_space_constraint(b, pltpu.HBM) for b in qs + lands])
    return res[0], res[1], list(res[2:2 + n]), list(res[2 + n:2 + 2 * n]), res[-1]


def _scatter_copies(ins, lands, send_sems, recv_sems):
    x, y, c = _place()
    cps = []
    for a in range(len(ins)):
        for j, (px, py) in enumerate(_other_chips(x, y)):
            cps.append(pltpu.make_async_remote_copy(
                src_ref=ins[a].at[2 * px + py], dst_ref=lands[a].at[j],
                send_sem=send_sems.at[3 * a + j], recv_sem=recv_sems.at[3 * a + j], device_id=(px, py, c), device_id_type=MESH))
    return cps


def _scatter_wait(qs, lands, send_sems, recv_sems, after, name):
    n = len(qs)

    def body(*refs):
        for cp in _scatter_copies(refs[:n], refs[n:2 * n], refs[2 * n], refs[2 * n + 1]):
            cp.wait_send()
            cp.wait_recv()

    res = pl.pallas_call(
        body, name=name,
        in_specs=[HBM_SPEC] * (2 * n) + [SEM_SPEC, SEM_SPEC, ANY_SPEC],
        out_specs=[HBM_SPEC] * (2 * n),
        out_shape=[pltpu.HBM(b.shape, b.dtype) for b in qs + lands],
        input_output_aliases={a: a for a in range(2 * n)},
        compiler_params=pltpu.CompilerParams(has_side_effects=EFFECT),
    )(*qs, *lands, send_sems, recv_sems, after)
    return list(res[:n]), list(res[n:])


def _chip_sum(q, got, name):
    _, hr, cols = got.shape
    tr, tc = 512, _tile_cols(cols)
    chip = 2 * lax.axis_index("x") + lax.axis_index("y")
    c = lax.axis_index("c")

    def body(idx_ref, q_ref, g_ref, o_ref):
        del idx_ref
        acc = q_ref[...].astype(F32)
        for i in range(3):
            acc = acc + g_ref[i].astype(F32)
        o_ref[...] = acc

    grid_spec = pltpu.PrefetchScalarGridSpec(
        num_scalar_prefetch=1, grid=(hr // tr, cols // tc),
        in_specs=[pl.BlockSpec((None, tr, tc), lambda i, j, ix: (ix[0], i, j)),
                  pl.BlockSpec((3, tr, tc), lambda i, j, ix: (0, i, j))],
        out_specs=pl.BlockSpec((None, tr, tc), lambda i, j, ix: (ix[1], i, j)))
    return pl.pallas_call(
        body, name=name, grid_spec=grid_spec,
        out_shape=jax.ShapeDtypeStruct((2, hr, cols), F32),
        compiler_params=_cp(("parallel", "parallel")),
    )(jnp.stack([chip, c]).astype(jnp.int32), q, got)


def _join_halves(bufs, name):
    n = len(bufs)

    def body(*refs):
        outs = refs[n:2 * n]
        send_sems, recv_sems = refs[2 * n:]
        x, y, c = _place()
        cps = []
        for a in range(n):
            cps.append(pltpu.make_async_remote_copy(src_ref=outs[a].at[c], dst_ref=outs[a].at[c], send_sem=send_sems.at[a],
                                                    recv_sem=recv_sems.at[a], device_id=(x, y, 1 - c), device_id_type=MESH))
        for cp in cps:
            cp.start()
        for a in range(n):
            theirs = outs[a].at[1 - c]
            pltpu.make_async_remote_copy(src_ref=theirs, dst_ref=theirs, send_sem=send_sems.at[a], recv_sem=recv_sems.at[a],
                                         device_id=(x, y, 1 - c), device_id_type=MESH).wait_recv()
        for cp in cps:
            cp.wait_send()

    hbm = pl.BlockSpec(memory_space=pl.ANY)
    return pl.pallas_call(
        body, name=name,
        in_specs=[hbm] * n, out_specs=[hbm] * n,
        out_shape=[jax.ShapeDtypeStruct(b.shape, b.dtype) for b in bufs],
        input_output_aliases={a: a for a in range(n)},
        scratch_shapes=[pltpu.SemaphoreType.DMA((n,)), pltpu.SemaphoreType.DMA((n,))],
    )(*bufs)


N_DEV = 8


def _gather_small(block, reduce, name):
    m_per, cols = block.shape

    def body(x_ref, out_ref, all_ref, send_sems, recv_sems, local_sem):
        x, y, c = _place()
        me, sibling = (x, y, c), (x, y, 1 - c)
        chips = _other_chips(x, y)

        def rows(px, py, pc):
            return all_ref.at[pl.ds((4 * px + 2 * py + pc) * m_per, m_per), :]

        def copy(k, block_of, to, src=None):
            return pltpu.make_async_remote_copy(
                src_ref=rows(*block_of) if src is None else src, dst_ref=rows(*block_of),
                send_sem=send_sems.at[k], recv_sem=recv_sems.at[k], device_id=to, device_id_type=MESH)

        mine = pltpu.make_async_copy(x_ref, rows(*me), local_sem)
        mine.start()
        first = [copy(0, me, sibling, src=x_ref)]
        first += [copy(1 + j, me, (*chip, c), src=x_ref) for j, chip in enumerate(chips)]
        for cp in first:
            cp.start()
        passed = [copy(4 + j, (*chip, c), sibling) for j, chip in enumerate(chips)]
        for j, chip in enumerate(chips):
            copy(1 + j, (*chip, c), me).wait_recv()
            passed[j].start()
        copy(0, sibling, me).wait_recv()
        for j, chip in enumerate(chips):
            copy(4 + j, (*chip, 1 - c), me).wait_recv()
        for cp in first + passed:
            cp.wait_send()
        mine.wait()
        if reduce:
            acc = all_ref[pl.ds(0, m_per), :]
            for i in range(1, N_DEV):
                acc = acc + all_ref[pl.ds(i * m_per, m_per), :]
            out_ref[...] = acc
        else:
            out_ref[...] = all_ref[...]

    out_rows = m_per if reduce else N_DEV * m_per
    return pl.pallas_call(
        body, name=name,
        in_specs=[pl.BlockSpec(memory_space=pltpu.VMEM)],
        out_specs=pl.BlockSpec(memory_space=pltpu.VMEM),
        out_shape=jax.ShapeDtypeStruct((out_rows, cols), F32),
        scratch_shapes=[pltpu.VMEM((N_DEV * m_per, cols), F32), pltpu.SemaphoreType.DMA((7,)),
                        pltpu.SemaphoreType.DMA((7,)), pltpu.SemaphoreType.DMA],
    )(block)


class _Comm:
    def __init__(self, w_in, w_kv, w_out, chip):
        self.bufs = [_cast_own(w_in, chip, "cast_w_in"), _cast_own(w_kv, chip, "cast_w_mem_kv"),
                     _cast_own(w_out, chip, "cast_w_out")]

    def gather_started(self):
        s_in, r_in, (b_in,), tok_in = _gather_start(self.bufs[:1], "gather_start_in")
        s_re, r_re, b_re, tok_re = _gather_start(self.bufs[1:], "gather_start_rest")
        self.in_flight = (s_in, r_in, b_in)
        self.rest_flight = (s_re, r_re, b_re, tok_re)
        return tok_in

    def w_in_own(self):
        return self.in_flight[2].reshape(NCHIP * D, WB)

    def w_in_all(self, after):
        s_in, r_in, b_in = self.in_flight
        (b_in,) = _gather_wait([b_in], s_in, r_in, after, "gather_wait_in")
        (b_in,) = _forward_halves([b_in], "forward_in")
        return b_in.reshape(NCHIP * D, WB)

    def w_rest(self, after):
        s_re, r_re, b_re, _ = self.rest_flight
        b_re = _gather_wait(b_re, s_re, r_re, after, "gather_wait_rest")
        b_kv, b_out = _forward_halves(b_re, "forward_rest")
        return b_kv.reshape(D, 2 * XW), b_out.reshape(D, D)

    def reduce_start(self, parts, tag):
        got = _sibling_halves(parts, "sibling_halves_" + tag)
        qs = [_pair_sum(p, g, f"pair_sum_{tag}{i}") for i, (p, g) in enumerate(zip(parts, got))]
        return _scatter_start(qs, "scatter_start_" + tag)

    def reduce_finish(self, state, tag, after):
        send_sems, recv_sems, qs, lands, _ = state
        qs, lands = _scatter_wait(qs, lands, send_sems, recv_sems, after, "scatter_wait_" + tag)
        halves = [_chip_sum(q, l, f"chip_sum_{tag}{i}") for i, (q, l) in enumerate(zip(qs, lands))]
        return [j.reshape(2 * j.shape[1], j.shape[2]) for j in _join_halves(halves, "join_halves_" + tag)]


def kernel(x, mem, pre_norm, w_in, conv_w, mem_norm, w_mem_kv, w_out, post_norm, loss_target, m_pre_norm, m_w_in, m_conv_w, m_mem_norm, m_w_mem_kv, m_w_out, m_post_norm, v_pre_norm, v_w_in, v_conv_w, v_mem_norm, v_w_mem_kv, v_w_out, v_post_norm):
    chip = 2 * lax.axis_index("x") + lax.axis_index("y")

    cw_blk = jnp.zeros((8, 384), F32).at[:3].set(conv_w[0])
    cw_all = _gather_small(cw_blk, False, "gather_conv_w").reshape(NCHIP, 2, 8, 384)[:, 0, :3]
    conv_full = jnp.transpose(cw_all, (1, 0, 2)).reshape(3, CW)

    comm = _Comm(w_in[0], w_mem_kv[0], w_out[0], chip)
    loss, grad_x, d_pre, d_mem, d_post, d_conv, r_in, r_kv, r_out = _local_step(
        x, mem, pre_norm, conv_full, mem_norm, post_norm, loss_target, chip, comm)

    pack = jnp.concatenate([d_pre, d_mem, d_post, jnp.pad(d_conv, ((0, 0), (0, D - CW))),
                            jnp.pad(loss, ((0, 0), (0, D - 128))), jnp.zeros((1, D), F32)], axis=0)
    tot = _gather_small(pack, True, "reduce_small")
    g_pre, g_mem, g_post = tot[0:1], tot[1:2], tot[2:3]
    g_conv = lax.dynamic_slice(tot[3:6, :CW], (0, chip * 384), (3, 384))
    loss_out = tot[6, 0]

    names = ("pre_norm", "w_in", "conv_w", "mem_norm", "w_mem_kv", "w_out", "post_norm")
    ws = (pre_norm, w_in[0], conv_w[0], mem_norm, w_mem_kv[0], w_out[0], post_norm)
    gs = (g_pre, r_in, g_conv, g_mem, r_kv, r_out, g_post)
    ms = (m_pre_norm, m_w_in[0], m_conv_w[0], m_mem_norm, m_w_mem_kv[0], m_w_out[0], m_post_norm)
    vs = (v_pre_norm, v_w_in[0], v_conv_w[0], v_mem_norm, v_w_mem_kv[0], v_w_out[0], v_post_norm)
    upd = [_adamw(w, g, m, v, "adamw_" + nm) for nm, w, g, m, v in zip(names, ws, gs, ms, vs)]

    def shaped(arrs):
        return [a.reshape(w.shape) if w.ndim == a.ndim else a.reshape((1,) + a.shape)
                for a, w in zip(arrs, (pre_norm, w_in, conv_w, mem_norm, w_mem_kv, w_out, post_norm))]

    grads = shaped(gs)
    deltas = shaped([u[0] for u in upd])
    new_m = shaped([u[1] for u in upd])
    new_v = shaped([u[2] for u in upd])
    return (loss_out, grad_x, *grads, *deltas, *new_m, *new_v)
```

```python
import functools

import jax
import jax.numpy as jnp
from jax import lax
from jax.experimental import pallas as pl
from jax.experimental.pallas import tpu as pltpu

F32 = jnp.float32
BF16 = jnp.bfloat16

D = 4096
S = 2048
NB = 2
T = NB * S
MLEN = 256
HD = 128
AW = 1536
CW = 1536
XW = 1024
XHD = 256
NXH = 4
NC = 14336
QA, KA, VA, ZA, UC, BC, CC, ZC, QX, ZX = 0, 1536, 3072, 4608, 6144, 7680, 9216, 10752, 12288, 13312
NCHIP = 4
WB = NC // NCHIP
DIL = (1, 4, 16)
HPG = 4
EPS = 1e-6
NEG = -1e30
ROPE_THETA = 10000.0
A_SCALE = HD ** -0.5
X_SCALE = XHD ** -0.5

ADAM_LR = 0.001
ADAM_B1 = 0.9
ADAM_B2 = 0.999
ADAM_EPS = 1e-08
ADAM_WD = 0.01
ADAM_STEP = 10

MESH = pl.DeviceIdType.MESH
MIB = 1024 * 1024


def _cp(sem, vmem_mib=48):
    return pltpu.CompilerParams(dimension_semantics=sem, vmem_limit_bytes=vmem_mib * MIB)


def _sigmoid(z):
    return 1.0 / (1.0 + jnp.exp(-z))


def _rope(x, cos, sin, half):
    return x * cos + pltpu.roll(x, half, 1) * sin


def _rope_t(g, cos, sin, half):
    return g * cos + pltpu.roll(g * sin, half, 1)


def _rms_fwd(x2, g, name, dep=None):
    rows = x2.shape[0]
    tr = 256
    deps = [] if dep is None else list(dep)

    def body(x_ref, g_ref, *rest):
        o_ref = rest[-1]
        x = x_ref[...]
        r = lax.rsqrt(jnp.mean(x * x, axis=-1, keepdims=True) + EPS)
        o_ref[...] = (x * r * g_ref[...]).astype(BF16)

    return pl.pallas_call(
        body, name=name, grid=(rows // tr,),
        in_specs=[pl.BlockSpec((tr, D), lambda i: (i, 0)), pl.BlockSpec((1, D), lambda i: (0, 0))]
        + [pl.BlockSpec(memory_space=pl.ANY)] * len(deps),
        out_specs=pl.BlockSpec((tr, D), lambda i: (i, 0)),
        out_shape=jax.ShapeDtypeStruct((rows, D), BF16),
        compiler_params=_cp(("parallel",)),
    )(x2, g, *deps)


def _norm_gain_grad(dn, x2, name):
    rows = x2.shape[0]
    tr = 256

    def body(dn_ref, x_ref, dg_ref):
        @pl.when(pl.program_id(0) == 0)
        def _():
            dg_ref[...] = jnp.zeros_like(dg_ref)
        x = x_ref[...]
        r = lax.rsqrt(jnp.mean(x * x, axis=-1, keepdims=True) + EPS)
        dg_ref[...] += jnp.sum(dn_ref[...] * (x * r), axis=0, keepdims=True)

    return pl.pallas_call(
        body, name=name, grid=(rows // tr,),
        in_specs=[pl.BlockSpec((tr, D), lambda i: (i, 0)), pl.BlockSpec((tr, D), lambda i: (i, 0))],
        out_specs=pl.BlockSpec((1, D), lambda i: (0, 0)),
        out_shape=jax.ShapeDtypeStruct((1, D), F32),
        compiler_params=_cp(("arbitrary",)),
    )(dn, x2)


def _pre_norm_bwd(dh, x2, g, dout):
    tr = 256

    def body(dh_ref, x_ref, g_ref, dout_ref, gx_ref, dg_ref):
        @pl.when(pl.program_id(0) == 0)
        def _():
            dg_ref[...] = jnp.zeros_like(dg_ref)
        x = x_ref[...]
        dh_ = dh_ref[...]
        r = lax.rsqrt(jnp.mean(x * x, axis=-1, keepdims=True) + EPS)
        xhat = x * r
        dg_ref[...] += jnp.sum(dh_ * xhat, axis=0, keepdims=True)
        dxn = dh_ * g_ref[...]
        gx_ref[...] = dout_ref[...] + r * (dxn - xhat * jnp.mean(dxn * xhat, axis=-1, keepdims=True))

    row = pl.BlockSpec((tr, D), lambda i: (i, 0))
    vec = pl.BlockSpec((1, D), lambda i: (0, 0))
    return pl.pallas_call(
        body, name="pre_norm_bwd", grid=(T // tr,),
        in_specs=[row, row, vec, row],
        out_specs=[row, vec],
        out_shape=[jax.ShapeDtypeStruct((T, D), F32), jax.ShapeDtypeStruct((1, D), F32)],
        compiler_params=_cp(("arbitrary",)),
    )(dh, x2, g, dout)


def _post_norm_loss(y, x2, tgt, g):
    tr = 256

    def body(y_ref, x_ref, t_ref, g_ref, dy_ref, dout_ref, dg_ref, loss_ref):
        @pl.when(pl.program_id(0) == 0)
        def _():
            dg_ref[...] = jnp.zeros_like(dg_ref)
            loss_ref[...] = jnp.zeros_like(loss_ref)
        yv = y_ref[...]
        gv = g_ref[...]
        r = lax.rsqrt(jnp.mean(yv * yv, axis=-1, keepdims=True) + EPS)
        yhat = yv * r
        err = x_ref[...] + yhat * gv - t_ref[...]
        loss_ref[...] += jnp.sum(jnp.sum(err * err, axis=1, keepdims=True), axis=0, keepdims=True) * (0.5 / D)
        dout = err * (1.0 / D)
        dout_ref[...] = dout
        dg_ref[...] += jnp.sum(dout * yhat, axis=0, keepdims=True)
        dyn = dout * gv
        dy_ref[...] = (r * (dyn - yhat * jnp.mean(dyn * yhat, axis=-1, keepdims=True))).astype(BF16)

    row = pl.BlockSpec((tr, D), lambda i: (i, 0))
    vec = pl.BlockSpec((1, D), lambda i: (0, 0))
    return pl.pallas_call(
        body, name="post_norm_loss", grid=(T // tr,),
        in_specs=[row, row, row, vec],
        out_specs=[row, row, vec, pl.BlockSpec((1, 128), lambda i: (0, 0))],
        out_shape=[jax.ShapeDtypeStruct((T, D), BF16), jax.ShapeDtypeStruct((T, D), F32),
                   jax.ShapeDtypeStruct((1, D), F32), jax.ShapeDtypeStruct((1, 128), F32)],
        compiler_params=_cp(("arbitrary",)),
    )(y, x2, tgt, g)


NN = (((1,), (0,)), ((), ()))
NT = (((1,), (1,)), ((), ()))
TN = (((0,), (0,)), ((), ()))


def _as_index(v):
    return jnp.reshape(v, (1,)).astype(jnp.int32)


def _matmul(a, b, *, name, dims, grid, a_block, a_map, b_block, b_map, o_block, o_map, out_shape, out_dtype=F32,
            index=None, prev=None, deps=()):
    assert out_dtype == F32
    extra = ([] if prev is None else [prev]) + [d for d in deps if d is not None]
    first = 0 if index is None else 1

    def body(*refs):
        a_ref, b_ref, o_ref = refs[first], refs[first + 1], refs[first + 2 + len(extra)]
        k = pl.program_id(2)
        part = lax.dot_general(a_ref[...], b_ref[...], dims, preferred_element_type=F32)

        @pl.when(k == 0)
        def _():
            o_ref[...] = part

        @pl.when(k > 0)
        def _():
            o_ref[...] += part

    in_specs = [pl.BlockSpec(a_block, a_map), pl.BlockSpec(b_block, b_map)] + [pl.BlockSpec(memory_space=pl.ANY)] * len(extra)
    out_specs = pl.BlockSpec(o_block, o_map)
    kwargs = dict(name=name, out_shape=jax.ShapeDtypeStruct(out_shape, out_dtype),
                  input_output_aliases={} if prev is None else {first + 2: 0},
                  compiler_params=_cp(("parallel", "parallel", "arbitrary")))
    if index is None:
        return pl.pallas_call(body, grid=grid, in_specs=in_specs, out_specs=out_specs, **kwargs)(a, b, *extra)
    grid_spec = pltpu.PrefetchScalarGridSpec(num_scalar_prefetch=1, grid=grid, in_specs=in_specs, out_specs=out_specs)
    return pl.pallas_call(body, grid_spec=grid_spec, **kwargs)(_as_index(index), a, b, *extra)


def _mm_nn(a, b, name, tm, tn, tk):
    m, kd = a.shape
    n = b.shape[1]
    return _matmul(a, b, name=name, dims=NN, grid=(m // tm, n // tn, kd // tk),
                   a_block=(tm, tk), a_map=lambda i, j, k: (i, k),
                   b_block=(tk, tn), b_map=lambda i, j, k: (k, j),
                   o_block=(tm, tn), o_map=lambda i, j, k: (i, j), out_shape=(m, n))


def _mm_nt(a, b, name, tm, tn, tk):
    m, kd = a.shape
    n = b.shape[0]
    return _matmul(a, b, name=name, dims=NT, grid=(m // tm, n // tn, kd // tk),
                   a_block=(tm, tk), a_map=lambda i, j, k: (i, k),
                   b_block=(tn, tk), b_map=lambda i, j, k: (j, k),
                   o_block=(tm, tn), o_map=lambda i, j, k: (i, j), out_shape=(m, n))


def _mm_tn(a, b, name, tm, tn, tk):
    kd, m = a.shape
    n = b.shape[1]
    return _matmul(a, b, name=name, dims=TN, grid=(m // tm, n // tn, kd // tk),
                   a_block=(tk, tm), a_map=lambda i, j, k: (k, i),
                   b_block=(tk, tn), b_map=lambda i, j, k: (k, j),
                   o_block=(tm, tn), o_map=lambda i, j, k: (i, j), out_shape=(m, n))


W_TN = 1792
W_NJ = WB // W_TN


def _proj_part(h, wg, chip, masks, name, prev=None):
    tm, tk = 1024, 1024

    def blk(j, ix):
        m = masks[0]
        for t in range(1, len(masks)):
            m = jnp.where(j // W_NJ == t, masks[t], m)
        return jnp.bitwise_xor(ix[0], m)

    return _matmul(h, wg, name=name, dims=NN, grid=(T // tm, len(masks) * W_NJ, D // tk), index=chip, prev=prev,
                   a_block=(tm, tk), a_map=lambda i, j, k, ix: (i, k),
                   b_block=(tk, W_TN), b_map=lambda i, j, k, ix: (blk(j, ix) * (D // tk) + k, j % W_NJ),
                   o_block=(tm, W_TN), o_map=lambda i, j, k, ix: (i, blk(j, ix) * W_NJ + j % W_NJ), out_shape=(T, NC))


def _dh(dproj, wg, dep=None):
    tm, tn = 1024, 1024
    return _matmul(dproj, wg, name="dh", dims=NT, grid=(T // tm, D // tn, NC // W_TN), deps=(dep,),
                   a_block=(tm, W_TN), a_map=lambda i, j, k: (i, k),
                   b_block=(tn, W_TN), b_map=lambda i, j, k: ((k // W_NJ) * (D // tn) + j, k % W_NJ),
                   o_block=(tm, tn), o_map=lambda i, j, k: (i, j), out_shape=(T, D))


def _grad_w_in(h, dproj):
    tm, tk = 1024, 1024
    return _matmul(h, dproj, name="grad_w_in", dims=TN, grid=(D // tm, NC // W_TN, T // tk),
                   a_block=(tk, tm), a_map=lambda i, j, k: (k, i),
                   b_block=(tk, W_TN), b_map=lambda i, j, k: (k, j),
                   o_block=(tm, W_TN), o_map=lambda i, j, k: ((j // W_NJ) * (D // tm) + i, j % W_NJ),
                   out_shape=(NCHIP * D, WB))


def _rope_tables(pos, half):
    inv = 1.0 / (ROPE_THETA ** (jnp.arange(half, dtype=F32) / half))
    ang = pos.astype(F32)[:, None] * inv[None, :]
    cos, sin = jnp.cos(ang), jnp.sin(ang)
    return jnp.concatenate([cos, cos], axis=1), jnp.concatenate([-sin, sin], axis=1)


def _band_mask(r0):
    qi = lax.broadcasted_iota(jnp.int32, (128, 256), 0)
    kk = lax.broadcasted_iota(jnp.int32, (128, 256), 1)
    return (kk >= qi) & (kk <= qi + 128) & (kk + r0 >= 128)


def _dil_rows(r, n, d):
    if d == 1:
        return pl.ds(pl.multiple_of(n * 128, 128), 128)
    return pl.ds(r + d * 128 * n, 128, stride=d)


def _attn_fwd(proj, cosf, sinf, g):
    d = DIL[g]
    ln = S // d
    nblk = ln // 128
    proj_v = proj.reshape(NB, S, NC)

    def body(q_ref, k_ref, v_ref, cos_ref, sin_ref, o_ref, l_ref, k_s, v_s):
        k_s[:, pl.ds(0, 128), :] = jnp.zeros((d, 128, HD), BF16)
        v_s[:, pl.ds(0, 128), :] = jnp.zeros((d, 128, HD), BF16)

        def prep(i, carry):
            r, n = i // nblk, i % nblk
            rows = _dil_rows(r, n, d)
            dst = pl.ds(pl.multiple_of(n * 128 + 128, 128), 128)
            k_s[r, dst, :] = _rope(k_ref[rows, :], cos_ref[rows, :], sin_ref[rows, :], HD // 2).astype(BF16)
            v_s[r, dst, :] = v_ref[rows, :].astype(BF16)
            return carry

        lax.fori_loop(0, d * nblk, prep, 0)

        def step(i, carry):
            r, n = i // nblk, i % nblk
            rows = _dil_rows(r, n, d)
            r0 = pl.multiple_of(n * 128, 128)
            qr = _rope(q_ref[rows, :], cos_ref[rows, :], sin_ref[rows, :], HD // 2).astype(BF16)
            kw = k_s[r, pl.ds(r0, 256), :]
            vw = v_s[r, pl.ds(r0, 256), :]
            sc = lax.dot_general(qr, kw, NT, preferred_element_type=F32) * A_SCALE
            sc = jnp.where(_band_mask(r0), sc, NEG)
            m = jnp.max(sc, axis=1, keepdims=True)
            p = jnp.exp(sc - m)
            l = jnp.sum(p, axis=1, keepdims=True)
            o_ref[rows, :] = jnp.dot(p.astype(BF16), vw, preferred_element_type=F32) / l
            l_ref[rows, :] = jnp.broadcast_to(m + jnp.log(l), (128, HD))
            return carry

        lax.fori_loop(0, d * nblk, step, 0)

    def col(off):
        return lambda b, h: (b, 0, off // HD + HPG * g + h)

    blk = (None, S, HD)
    tab = pl.BlockSpec((S, HD), lambda b, h: (0, 0))
    out = pl.BlockSpec(blk, lambda b, h: (b, 0, h))
    o, l = pl.pallas_call(
        body, name=f"attn_fwd_d{d}", grid=(NB, HPG),
        in_specs=[pl.BlockSpec(blk, col(QA)), pl.BlockSpec(blk, col(KA)), pl.BlockSpec(blk, col(VA)), tab, tab],
        out_specs=[out, out],
        out_shape=[jax.ShapeDtypeStruct((NB, S, HPG * HD), F32)] * 2,
        scratch_shapes=[pltpu.VMEM((d, ln + 128, HD), BF16), pltpu.VMEM((d, ln + 128, HD), BF16)],
        compiler_params=_cp(("parallel", "parallel")),
    )(proj_v, proj_v, proj_v, cosf, sinf)
    return o.reshape(T, HPG * HD), l.reshape(T, HPG * HD)


def _attn_bwd(proj, cosf, sinf, da, lse, delta, dproj, g):
    d = DIL[g]
    ln = S // d
    nblk = ln // 128
    proj_v = proj.reshape(NB, S, NC)
    dproj_v = dproj.reshape(NB, S, NC)
    da_v = da.reshape(NB, S, AW)
    lse_v = lse.reshape(NB, S, HPG * HD)
    delta_v = delta.reshape(NB, S, HPG * HD)

    def body(q_ref, k_ref, v_ref, cos_ref, sin_ref, da_ref, lse_ref, dl_ref, dp_in_ref, o_ref,
             k_s, v_s, stg, dk_s, dv_s):
        del dp_in_ref
        w = pl.program_id(2)

        def emit():
            def cast(n, carry):
                rows = pl.ds(pl.multiple_of(n * 256, 256), 256)
                o_ref[rows, :] = stg[rows, :].astype(BF16)
                return carry

            lax.fori_loop(0, S // 256, cast, 0)

        @pl.when(w == 0)
        def _():
            k_s[:, pl.ds(0, 128), :] = jnp.zeros((d, 128, HD), BF16)
            v_s[:, pl.ds(0, 128), :] = jnp.zeros((d, 128, HD), BF16)
            dk_s[...] = jnp.zeros_like(dk_s)
            dv_s[...] = jnp.zeros_like(dv_s)

            def prep(i, carry):
                r, n = i // nblk, i % nblk
                rows = _dil_rows(r, n, d)
                dst = pl.ds(pl.multiple_of(n * 128 + 128, 128), 128)
                k_s[r, dst, :] = _rope(k_ref[rows, :], cos_ref[rows, :], sin_ref[rows, :], HD // 2).astype(BF16)
                v_s[r, dst, :] = v_ref[rows, :].astype(BF16)
                return carry

            lax.fori_loop(0, d * nblk, prep, 0)

            def step(i, carry):
                r, n = i // nblk, i % nblk
                rows = _dil_rows(r, n, d)
                r0 = pl.multiple_of(n * 128, 128)
                win = pl.ds(r0, 256)
                cos, sin = cos_ref[rows, :], sin_ref[rows, :]
                qr = _rope(q_ref[rows, :], cos, sin, HD // 2).astype(BF16)
                kw = k_s[r, win, :]
                vw = v_s[r, win, :]
                sc = lax.dot_general(qr, kw, NT, preferred_element_type=F32) * A_SCALE
                sc = jnp.where(_band_mask(r0), sc, NEG)
                p = jnp.exp(sc - lse_ref[rows, :][:, :1])
                da_b = da_ref[rows, :].astype(BF16)
                dp = lax.dot_general(da_b, vw, NT, preferred_element_type=F32)
                ds_b = (p * (dp - dl_ref[rows, :][:, :1]) * A_SCALE).astype(BF16)
                p_b = p.astype(BF16)
                dq = jnp.dot(ds_b, kw, preferred_element_type=F32)
                stg[rows, :] = _rope_t(dq, cos, sin, HD // 2)
                dk_s[r, win, :] += lax.dot_general(ds_b, qr, TN, preferred_element_type=F32)
                dv_s[r, win, :] += lax.dot_general(p_b, da_b, TN, preferred_element_type=F32)
                return carry

            lax.fori_loop(0, d * nblk, step, 0)
            emit()

        @pl.when(w == 1)
        def _():
            def put(i, carry):
                r, n = i // nblk, i % nblk
                rows = _dil_rows(r, n, d)
                src = pl.ds(pl.multiple_of(n * 128 + 128, 128), 128)
                stg[rows, :] = _rope_t(dk_s[r, src, :], cos_ref[rows, :], sin_ref[rows, :], HD // 2)
                return carry

            lax.fori_loop(0, d * nblk, put, 0)
            emit()

        @pl.when(w == 2)
        def _():
            def put(i, carry):
                r, n = i // nblk, i % nblk
                src = pl.ds(pl.multiple_of(n * 128 + 128, 128), 128)
                stg[_dil_rows(r, n, d), :] = dv_s[r, src, :]
                return carry

            lax.fori_loop(0, d * nblk, put, 0)
            emit()

    def col(off):
        return lambda b, h, w: (b, 0, off // HD + HPG * g + h)

    blk = (None, S, HD)
    tab = pl.BlockSpec((S, HD), lambda b, h, w: (0, 0))
    per_head = pl.BlockSpec(blk, lambda b, h, w: (b, 0, h))
    out = pl.pallas_call(
        body, name=f"attn_bwd_d{d}", grid=(NB, HPG, 3),
        in_specs=[pl.BlockSpec(blk, col(QA)), pl.BlockSpec(blk, col(KA)), pl.BlockSpec(blk, col(VA)), tab, tab,
                  pl.BlockSpec(blk, col(0)), per_head, per_head, pl.BlockSpec(memory_space=pl.ANY)],
        out_specs=pl.BlockSpec(blk, lambda b, h, w: (b, 0, (AW // HD) * w + HPG * g + h)),
        out_shape=jax.ShapeDtypeStruct(dproj_v.shape, BF16),
        input_output_aliases={8: 0},
        scratch_shapes=[pltpu.VMEM((d, ln + 128, HD), BF16), pltpu.VMEM((d, ln + 128, HD), BF16),
                        pltpu.VMEM((S, HD), F32), pltpu.VMEM((d, ln + 128, HD), F32), pltpu.VMEM((d, ln + 128, HD), F32)],
        compiler_params=_cp(("arbitrary",) * 3),
    )(proj_v, proj_v, proj_v, cosf, sinf, da_v, lse_v, delta_v, dproj_v)
    return out.reshape(T, NC)


def _attn_mix(proj, os_, ls_):
    tr = 256
    gw = HPG * HD

    def body(o0, o1, o2, l0, l1, l2, z_ref, cat_ref):
        m = jnp.maximum(jnp.maximum(l0[...], l1[...]), l2[...])
        e = [jnp.exp(l[...] - m) for l in (l0, l1, l2)]
        inv = 1.0 / (e[0] + e[1] + e[2])
        for gi, o in enumerate((o0, o1, o2)):
            z = z_ref[:, gi * gw:(gi + 1) * gw]
            cat_ref[:, gi * gw:(gi + 1) * gw] = (o[...] * (e[gi] * inv) * (z * _sigmoid(z))).astype(BF16)

    grp = pl.BlockSpec((tr, gw), lambda i: (i, 0))
    return pl.pallas_call(
        body, name="attn_mix", grid=(T // tr,),
        in_specs=[grp] * 6 + [pl.BlockSpec((tr, AW), lambda i: (i, ZA // AW))],
        out_specs=pl.BlockSpec((tr, AW), lambda i: (i, 0)),
        out_shape=jax.ShapeDtypeStruct((T, D), BF16),
        compiler_params=_cp(("parallel",)),
    )(*os_, *ls_, proj)


def _attn_mix_bwd(dcat, proj, os_, ls_, dep=None):
    tr = 256
    gw = HPG * HD
    deps = [] if dep is None else [dep]

    def body(dy_ref, o0, o1, o2, l0, l1, l2, z_ref, *rest):
        da_ref, lse_ref, dl_ref, dz_ref = rest[len(deps):]
        m = jnp.maximum(jnp.maximum(l0[...], l1[...]), l2[...])
        e = [jnp.exp(l[...] - m) for l in (l0, l1, l2)]
        den = e[0] + e[1] + e[2]
        inv = 1.0 / den
        lse_ref[...] = m + jnp.log(den)
        acc = jnp.zeros((tr, gw), F32)
        for gi, o in enumerate((o0, o1, o2)):
            cols = slice(gi * gw, (gi + 1) * gw)
            z = z_ref[:, cols]
            dy = dy_ref[:, cols]
            sg = _sigmoid(z)
            a = o[...] * (e[gi] * inv)
            da = dy * (z * sg)
            da_ref[:, cols] = da
            dz_ref[:, cols] = (dy * a * (sg * (1.0 + z * (1.0 - sg)))).astype(BF16)
            acc = acc + da * a
        for hh in range(HPG):
            cols = slice(hh * HD, (hh + 1) * HD)
            dl_ref[:, cols] = jnp.broadcast_to(jnp.sum(acc[:, cols], axis=1, keepdims=True), (tr, HD))

    grp = pl.BlockSpec((tr, gw), lambda i: (i, 0))
    return pl.pallas_call(
        body, name="attn_mix_bwd", grid=(T // tr,),
        in_specs=[pl.BlockSpec((tr, AW), lambda i: (i, 0))] + [grp] * 6 + [pl.BlockSpec((tr, AW), lambda i: (i, ZA // AW))]
        + [pl.BlockSpec(memory_space=pl.ANY)] * len(deps),
        out_specs=[pl.BlockSpec((tr, AW), lambda i: (i, 0)), grp, grp, pl.BlockSpec((tr, AW), lambda i: (i, ZA // AW))],
        out_shape=[jax.ShapeDtypeStruct((T, AW), F32), jax.ShapeDtypeStruct((T, gw), F32),
                   jax.ShapeDtypeStruct((T, gw), F32), jax.ShapeDtypeStruct((T, NC), BF16)],
        compiler_params=_cp(("parallel",)),
    )(dcat, *os_, *ls_, proj, *deps)


CT = 128


def _shift_down(x, n):
    rows = lax.broadcasted_iota(jnp.int32, x.shape, 0)
    return jnp.where(rows >= n, pltpu.roll(x, n, 0), 0.0)


def _shift_up(x, n):
    rows = lax.broadcasted_iota(jnp.int32, x.shape, 0)
    return jnp.where(rows < x.shape[0] - n, pltpu.roll(x, x.shape[0] - n, 0), 0.0)


def _conv_fwd(proj, conv_w, cat):
    proj_v = proj.reshape(NB, S, NC)
    cat_v = cat.reshape(NB, S, D)

    def body(u_ref, b_ref, c_ref, z_ref, w_ref, cat_in, o_ref):
        del cat_in
        cu = c_ref[...] * u_ref[...]
        cv = _shift_down(cu, 2) * w_ref[0:1, :] + _shift_down(cu, 1) * w_ref[1:2, :] + cu * w_ref[2:3, :]
        z = z_ref[...]
        o_ref[...] = (b_ref[...] * cv * (z * _sigmoid(z))).astype(BF16)

    def seg(off):
        return pl.BlockSpec((None, S, CT), lambda b, j: (b, 0, off // CT + j))

    out = pl.pallas_call(
        body, name="conv_fwd", grid=(NB, CW // CT),
        in_specs=[seg(UC), seg(BC), seg(CC), seg(ZC), pl.BlockSpec((3, CT), lambda b, j: (0, j)),
                  pl.BlockSpec(memory_space=pl.ANY)],
        out_specs=pl.BlockSpec((None, S, CT), lambda b, j: (b, 0, AW // CT + j)),
        out_shape=jax.ShapeDtypeStruct((NB, S, D), BF16),
        input_output_aliases={5: 0},
        compiler_params=_cp(("parallel", "parallel")),
    )(proj_v, proj_v, proj_v, proj_v, conv_w, cat_v)
    return out.reshape(T, D)


def _conv_bwd(dcat, proj, conv_w, dproj):
    proj_v = proj.reshape(NB, S, NC)
    dproj_v = dproj.reshape(NB, S, NC)
    dcat_v = dcat.reshape(NB, S, D)

    def body(dy_ref, u_ref, b_ref, c_ref, z_ref, w_ref, dp_in, o_ref, dw_ref, st):
        del dp_in
        b = pl.program_id(1)
        w = pl.program_id(2)

        @pl.when((b == 0) & (w == 0))
        def _():
            dw_ref[...] = jnp.zeros_like(dw_ref)

        @pl.when(w == 0)
        def _():
            u, c, z, bb, dy = u_ref[...], c_ref[...], z_ref[...], b_ref[...], dy_ref[...]
            cu = c * u
            s1 = _shift_down(cu, 1)
            s2 = _shift_down(cu, 2)
            cv = s2 * w_ref[0:1, :] + s1 * w_ref[1:2, :] + cu * w_ref[2:3, :]
            sg = _sigmoid(z)
            sz = z * sg
            dcv = dy * bb * sz
            st[0] = dy * cv * sz
            st[2] = dy * bb * cv * (sg * (1.0 + z * (1.0 - sg)))
            dw_ref[0:1, :] += jnp.sum(dcv * s2, axis=0, keepdims=True)
            dw_ref[1:2, :] += jnp.sum(dcv * s1, axis=0, keepdims=True)
            dw_ref[2:3, :] += jnp.sum(dcv * cu, axis=0, keepdims=True)
            dcu = dcv * w_ref[2:3, :] + _shift_up(dcv, 1) * w_ref[1:2, :] + _shift_up(dcv, 2) * w_ref[0:1, :]
            st[1] = dcu * u
            o_ref[...] = (dcu * c).astype(BF16)

        for k in range(3):
            @pl.when(w == k + 1)
            def _(k=k):
                o_ref[...] = st[k].astype(BF16)

    def seg(off):
        return pl.BlockSpec((None, S, CT), lambda j, b, w: (b, 0, off // CT + j))

    out, dw = pl.pallas_call(
        body, name="conv_bwd", grid=(CW // CT, NB, 4),
        in_specs=[pl.BlockSpec((None, S, CT), lambda j, b, w: (b, 0, AW // CT + j)),
                  seg(UC), seg(BC), seg(CC), seg(ZC), pl.BlockSpec((3, CT), lambda j, b, w: (0, j)),
                  pl.BlockSpec(memory_space=pl.ANY)],
        out_specs=[pl.BlockSpec((None, S, CT), lambda j, b, w: (b, 0, (UC + w * CW) // CT + j)),
                   pl.BlockSpec((3, CT), lambda j, b, w: (0, j))],
        out_shape=[jax.ShapeDtypeStruct((NB, S, NC), BF16), jax.ShapeDtypeStruct((3, CW), F32)],
        input_output_aliases={6: 0},
        scratch_shapes=[pltpu.VMEM((3, S, CT), F32)],
        compiler_params=_cp(("arbitrary",) * 3),
    )(dcat_v, proj_v, proj_v, proj_v, proj_v, conv_w, dproj_v)
    return out.reshape(T, NC), dw


XT = 512


def _cross_fwd(proj, mkv, cosq, sinq, cosm, sinm, cat):
    proj_v = proj.reshape(NB, S, NC)
    mkv_v = mkv.reshape(NB, MLEN, 2 * XW)
    cat_v = cat.reshape(NB, S, D)

    def body(q_ref, z_ref, mk_ref, mv_ref, cq, sq, cm, sm, cat_in, o_ref):
        del cat_in
        mkr = _rope(mk_ref[...], cm[...], sm[...], XHD // 2).astype(BF16)
        qr = _rope(q_ref[...], cq[...], sq[...], XHD // 2).astype(BF16)
        sc = lax.dot_general(qr, mkr, NT, preferred_element_type=F32) * X_SCALE
        p = jnp.exp(sc - jnp.max(sc, axis=1, keepdims=True))
        p = p / jnp.sum(p, axis=1, keepdims=True)
        ox = jnp.dot(p.astype(BF16), mv_ref[...].astype(BF16), preferred_element_type=F32)
        z = z_ref[...]
        o_ref[...] = (ox * (z * _sigmoid(z))).astype(BF16)

    def seg(off):
        return pl.BlockSpec((None, XT, XHD), lambda b, h, t: (b, t, off // XHD + h))

    qtab = pl.BlockSpec((XT, XHD), lambda b, h, t: (t, 0))
    mtab = pl.BlockSpec((MLEN, XHD), lambda b, h, t: (0, 0))
    out = pl.pallas_call(
        body, name="cross_fwd", grid=(NB, NXH, S // XT),
        in_specs=[seg(QX), seg(ZX),
                  pl.BlockSpec((None, MLEN, XHD), lambda b, h, t: (b, 0, h)),
                  pl.BlockSpec((None, MLEN, XHD), lambda b, h, t: (b, 0, NXH + h)),
                  qtab, qtab, mtab, mtab, pl.BlockSpec(memory_space=pl.ANY)],
        out_specs=pl.BlockSpec((None, XT, XHD), lambda b, h, t: (b, t, (AW + CW) // XHD + h)),
        out_shape=jax.ShapeDtypeStruct((NB, S, D), BF16),
        input_output_aliases={8: 0},
        compiler_params=_cp(("parallel",) * 3),
    )(proj_v, proj_v, mkv_v, mkv_v, cosq, sinq, cosm, sinm, cat_v)
    return out.reshape(T, D)


def _cross_bwd(dcat, proj, mkv, cosq, sinq, cosm, sinm, dproj):
    proj_v = proj.reshape(NB, S, NC)
    dproj_v = dproj.reshape(NB, S, NC)
    dcat_v = dcat.reshape(NB, S, D)
    mkv_v = mkv.reshape(NB, MLEN, 2 * XW)
    nt = S // XT

    def body(dy_ref, q_ref, z_ref, mk_ref, mv_ref, cq, sq, cm, sm, dp_in, o_ref, dmk_ref, dmv_ref, dz_s):
        del dp_in
        t = pl.program_id(2)
        w = pl.program_id(3)

        @pl.when((t == 0) & (w == 0))
        def _():
            dmk_ref[...] = jnp.zeros_like(dmk_ref)
            dmv_ref[...] = jnp.zeros_like(dmv_ref)

        @pl.when(w == 0)
        def _():
            mkr = _rope(mk_ref[...], cm[...], sm[...], XHD // 2).astype(BF16)
            mv_b = mv_ref[...].astype(BF16)
            qr = _rope(q_ref[...], cq[...], sq[...], XHD // 2).astype(BF16)
            sc = lax.dot_general(qr, mkr, NT, preferred_element_type=F32) * X_SCALE
            p = jnp.exp(sc - jnp.max(sc, axis=1, keepdims=True))
            p = p / jnp.sum(p, axis=1, keepdims=True)
            p_b = p.astype(BF16)
            ox = jnp.dot(p_b, mv_b, preferred_element_type=F32)
            z = z_ref[...]
            dy = dy_ref[...]
            sg = _sigmoid(z)
            dz_s[...] = dy * ox * (sg * (1.0 + z * (1.0 - sg)))
            dox_b = (dy * (z * sg)).astype(BF16)
            dp = lax.dot_general(dox_b, mv_b, NT, preferred_element_type=F32)
            ds_b = (p * (dp - jnp.sum(dp * p, axis=1, keepdims=True)) * X_SCALE).astype(BF16)
            dq = jnp.dot(ds_b, mkr, preferred_element_type=F32)
            o_ref[...] = _rope_t(dq, cq[...], sq[...], XHD // 2).astype(BF16)
            dmk_ref[...] += lax.dot_general(ds_b, qr, TN, preferred_element_type=F32)
            dmv_ref[...] += lax.dot_general(p_b, dox_b, TN, preferred_element_type=F32)

        @pl.when(w == 1)
        def _():
            o_ref[...] = dz_s[...].astype(BF16)

        @pl.when((t == nt - 1) & (w == 1))
        def _():
            dmk_ref[...] = _rope_t(dmk_ref[...], cm[...], sm[...], XHD // 2)

    def seg(off):
        return pl.BlockSpec((None, XT, XHD), lambda b, h, t, w: (b, t, off // XHD + h))

    qtab = pl.BlockSpec((XT, XHD), lambda b, h, t, w: (t, 0))
    mtab = pl.BlockSpec((MLEN, XHD), lambda b, h, t, w: (0, 0))
    macc = pl.BlockSpec((None, MLEN, XHD), lambda b, h, t, w: (b, 0, h))
    out, dmk, dmv = pl.pallas_call(
        body, name="cross_bwd", grid=(NB, NXH, nt, 2),
        in_specs=[pl.BlockSpec((None, XT, XHD), lambda b, h, t, w: (b, t, (AW + CW) // XHD + h)),
                  seg(QX), seg(ZX),
                  pl.BlockSpec((None, MLEN, XHD), lambda b, h, t, w: (b, 0, h)),
                  pl.BlockSpec((None, MLEN, XHD), lambda b, h, t, w: (b, 0, NXH + h)),
                  qtab, qtab, mtab, mtab, pl.BlockSpec(memory_space=pl.ANY)],
        out_specs=[pl.BlockSpec((None, XT, XHD), lambda b, h, t, w: (b, t, (QX + w * XW) // XHD + h)), macc, macc],
        out_shape=[jax.ShapeDtypeStruct((NB, S, NC), BF16), jax.ShapeDtypeStruct((NB, MLEN, XW), F32),
                   jax.ShapeDtypeStruct((NB, MLEN, XW), F32)],
        input_output_aliases={9: 0},
        scratch_shapes=[pltpu.VMEM((XT, XHD), F32)],
        compiler_params=_cp(("arbitrary",) * 4),
    )(dcat_v, proj_v, proj_v, mkv_v, mkv_v, cosq, sinq, cosm, sinm, dproj_v)
    return out.reshape(T, NC), dmk, dmv


def _local_step(x, mem, pre_norm, conv_w, mem_norm, post_norm, tgt, chip, comm):
    x2 = x.reshape(T, D)
    mem2 = mem.reshape(NB * MLEN, D)
    tgt2 = tgt.reshape(T, D)
    cosa, sina = _rope_tables(jnp.arange(S), HD // 2)
    cosq, sinq = _rope_tables(jnp.arange(S) + MLEN, XHD // 2)
    cosm, sinm = _rope_tables(jnp.arange(MLEN), XHD // 2)

    h = _rms_fwd(x2, pre_norm, "pre_norm_fwd", dep=comm.gather_started())
    memn = _rms_fwd(mem2, mem_norm, "mem_norm_fwd")
    proj = _proj_part(h, comm.w_in_own(), chip, (0,), "proj_own")
    proj = _proj_part(h, comm.w_in_near(after=proj), chip, (REL_XOR[0], REL_XOR[1]), "proj_near", prev=proj)
    wg_in = comm.w_in_all(after=proj)
    proj = _proj_part(h, wg_in, chip, (REL_XOR[2],), "proj_far", prev=proj)
    wg_kv, wg_out = comm.w_rest(after=proj)
    mkv = _mm_nn(memn, wg_kv, "mkv", NB * MLEN, 1024, 512)

    fw = [_attn_fwd(proj, cosa, sina, g) for g in range(3)]
    os_ = [f[0] for f in fw]
    ls_ = [f[1] for f in fw]
    cat = _attn_mix(proj, os_, ls_)
    cat = _conv_fwd(proj, conv_w, cat)
    cat = _cross_fwd(proj, mkv, cosq, sinq, cosm, sinm, cat)
    y = _mm_nn(cat, wg_out, "out_proj", 1024, 1024, 2048)
    dy, dout, d_post, loss = _post_norm_loss(y, x2, tgt2, post_norm)

    dcat = _mm_nt(dy, wg_out, "dcat", 1024, 1024, 2048)
    g_out = _mm_tn(cat, dy, "grad_w_out", 1024, 1024, 2048)
    red_a = comm.reduce_start([g_out.reshape(NCHIP, D // NCHIP, D)], "a")
    da, lse, delta, dproj = _attn_mix_bwd(dcat, proj, os_, ls_, dep=red_a[-1])
    for g in range(3):
        dproj = _attn_bwd(proj, cosa, sina, da, lse, delta, dproj, g)
    dproj, d_conv = _conv_bwd(dcat, proj, conv_w, dproj)
    dproj, dmk, dmv = _cross_bwd(dcat, proj, mkv, cosq, sinq, cosm, sinm, dproj)

    dmkv = jnp.concatenate([dmk, dmv], axis=-1).reshape(NB * MLEN, 2 * XW)
    dmkv_b = dmkv.astype(BF16)
    g_kv = _mm_tn(memn, dmkv_b, "grad_w_mem_kv", 1024, 1024, NB * MLEN)
    dmemn = _mm_nt(dmkv_b, wg_kv, "dmemn", NB * MLEN, 1024, 512)
    d_mem = _norm_gain_grad(dmemn, mem2, "mem_norm_bwd")

    g_in = _grad_w_in(h, dproj)
    red_b = comm.reduce_start([g_in.reshape(NCHIP, D, WB), g_kv.reshape(NCHIP, D // NCHIP, 2 * XW)], "b")
    dh = _dh(dproj, wg_in, dep=red_b[-1])
    grad_x, d_pre = _pre_norm_bwd(dh, x2, pre_norm, dout)
    (r_out,) = comm.reduce_finish(red_a, "a", after=grad_x)
    r_in, r_kv = comm.reduce_finish(red_b, "b", after=grad_x)
    return loss, grad_x.reshape(NB, S, D), d_pre, d_mem, d_post, d_conv, r_in, r_kv, r_out


def _adamw(w, g, m, v, name):
    rows, cols = w.shape
    tr = rows if rows <= 512 else 512
    tc = cols if cols <= 1024 else 1024
    if cols % tc:
        tc = 896

    def body(w_ref, g_ref, m_ref, v_ref, d_ref, nm_ref, nv_ref):
        gv = g_ref[...]
        nm = ADAM_B1 * m_ref[...] + (1.0 - ADAM_B1) * gv
        nv = ADAM_B2 * v_ref[...] + (1.0 - ADAM_B2) * (gv * gv)
        m_hat = nm / (1.0 - ADAM_B1 ** ADAM_STEP)
        v_hat = nv / (1.0 - ADAM_B2 ** ADAM_STEP)
        d_ref[...] = -ADAM_LR * (m_hat / (jnp.sqrt(v_hat) + ADAM_EPS) + ADAM_WD * w_ref[...])
        nm_ref[...] = nm
        nv_ref[...] = nv

    blk = pl.BlockSpec((tr, tc), lambda i, j: (i, j))
    sds = jax.ShapeDtypeStruct((rows, cols), F32)
    return pl.pallas_call(
        body, name=name, grid=(rows // tr, cols // tc),
        in_specs=[blk] * 4, out_specs=[blk] * 3, out_shape=[sds] * 3,
        compiler_params=_cp(("parallel", "parallel")),
    )(w, g, m, v)


def _place():
    return lax.axis_index("x"), lax.axis_index("y"), lax.axis_index("c")


def _other_chips(x, y):
    return [(1 - x, y), (x, 1 - y), (1 - x, 1 - y)]


def _tile_cols(cols):
    return cols if cols <= 1024 else (1024 if cols % 1024 == 0 else 896)


def _cast_own(w, chip, name):
    rows, cols = w.shape
    tr, tc = 512, _tile_cols(cols)

    def body(chip_ref, w_ref, o_ref):
        del chip_ref
        o_ref[...] = w_ref[...].astype(BF16)

    grid_spec = pltpu.PrefetchScalarGridSpec(
        num_scalar_prefetch=1, grid=(rows // tr, cols // tc),
        in_specs=[pl.BlockSpec((tr, tc), lambda i, j, cr: (i, j))],
        out_specs=pl.BlockSpec((None, tr, tc), lambda i, j, cr: (cr[0], i, j)))
    return pl.pallas_call(
        body, name=name, grid_spec=grid_spec,
        out_shape=jax.ShapeDtypeStruct((NCHIP, rows, cols), BF16),
        compiler_params=_cp(("parallel", "parallel")),
    )(_as_index(chip), w)


HBM_SPEC = pl.BlockSpec(memory_space=pltpu.HBM)
SEM_SPEC = pl.BlockSpec(memory_space=pltpu.SEMAPHORE)
ANY_SPEC = pl.BlockSpec(memory_space=pl.ANY)
EFFECT = pltpu.SideEffectType.DATAFLOW_SIDE_EFFECTING
TOKEN = jax.ShapeDtypeStruct((8, 128), F32)


def _half(ref, chip, hc):
    hr = ref.shape[1] // 2
    return ref.at[chip, pl.ds(hc * hr, hr), :]


NEAR = (0, 1)
FAR = (2,)
REL_XOR = (2, 1, 3)


def _gather_copies(refs, send_sems, recv_sems, rels):
    x, y, c = _place()
    chips = _other_chips(x, y)
    out, inc = [], []
    for a, ref in enumerate(refs):
        for p, j in enumerate(rels):
            px, py = chips[j]
            mine = _half(ref, 2 * x + y, c)
            theirs = _half(ref, 2 * px + py, c)
            sems = dict(send_sem=send_sems.at[len(rels) * a + p], recv_sem=recv_sems.at[len(rels) * a + p],
                        device_id=(px, py, c), device_id_type=MESH)
            out.append(pltpu.make_async_remote_copy(src_ref=mine, dst_ref=mine, **sems))
            inc.append(pltpu.make_async_remote_copy(src_ref=theirs, dst_ref=theirs, **sems))
    return out, inc


def _gather_start(bufs, groups, name):
    n = len(bufs)
    ng = len(groups)

    def body(*refs):
        ins = refs[:n]
        token = refs[-1]
        for gi, rels in enumerate(groups):
            out, _ = _gather_copies(ins, refs[n + 2 * gi], refs[n + 2 * gi + 1], rels)
            for cp in out:
                cp.start()
        token[...] = jnp.zeros_like(token)

    sems = []
    for rels in groups:
        sems += [pltpu.SemaphoreType.DMA((len(rels) * n,))] * 2
    res = pl.pallas_call(
        body, name=name,
        in_specs=[HBM_SPEC] * n,
        out_specs=[SEM_SPEC] * (2 * ng) + [HBM_SPEC] * n + [pl.BlockSpec(memory_space=pltpu.VMEM)],
        out_shape=sems + [pltpu.HBM(b.shape, b.dtype) for b in bufs] + [TOKEN],
        input_output_aliases={a: 2 * ng + a for a in range(n)},
        compiler_params=pltpu.CompilerParams(has_side_effects=EFFECT),
    )(*[pltpu.with_memory_space_constraint(b, pltpu.HBM) for b in bufs])
    return [(res[2 * gi], res[2 * gi + 1]) for gi in range(ng)], list(res[2 * ng:2 * ng + n]), res[-1]


def _gather_wait(bufs, sems, rels, after, name):
    n = len(bufs)
    send_sems, recv_sems = sems

    def body(*refs):
        ins = refs[:n]
        out, inc = _gather_copies(ins, refs[n], refs[n + 1], rels)
        for cp in out:
            cp.wait_send()
        for cp in inc:
            cp.wait_recv()

    return pl.pallas_call(
        body, name=name,
        in_specs=[HBM_SPEC] * n + [SEM_SPEC, SEM_SPEC, ANY_SPEC],
        out_specs=[HBM_SPEC] * n,
        out_shape=[pltpu.HBM(b.shape, b.dtype) for b in bufs],
        input_output_aliases={a: a for a in range(n)},
        compiler_params=pltpu.CompilerParams(has_side_effects=EFFECT),
    )(*bufs, send_sems, recv_sems, after)


def _forward_halves(bufs, rels, name):
    n = len(bufs)

    def body(*refs):
        outs = refs[n:2 * n]
        send_sems, recv_sems = refs[2 * n:]
        x, y, c = _place()
        chips = _other_chips(x, y)
        cps, waits = [], []
        for a in range(n):
            for p, j in enumerate(rels):
                px, py = chips[j]
                sems = dict(send_sem=send_sems.at[len(rels) * a + p], recv_sem=recv_sems.at[len(rels) * a + p],
                            device_id=(x, y, 1 - c), device_id_type=MESH)
                got = _half(outs[a], 2 * px + py, c)
                want = _half(outs[a], 2 * px + py, 1 - c)
                cps.append(pltpu.make_async_remote_copy(src_ref=got, dst_ref=got, **sems))
                waits.append(pltpu.make_async_remote_copy(src_ref=want, dst_ref=want, **sems))
        for cp in cps:
            cp.start()
        for cp in waits:
            cp.wait_recv()
        for cp in cps:
            cp.wait_send()

    return pl.pallas_call(
        body, name=name,
        in_specs=[ANY_SPEC] * n, out_specs=[ANY_SPEC] * n,
        out_shape=[jax.ShapeDtypeStruct(s.shape, s.dtype) for s in bufs],
        input_output_aliases={a: a for a in range(n)},
        scratch_shapes=[pltpu.SemaphoreType.DMA((len(rels) * n,)), pltpu.SemaphoreType.DMA((len(rels) * n,))],
    )(*bufs)


def _sibling_halves(parts, name):
    n = len(parts)

    def body(*refs):
        ins, outs = refs[:n], refs[n:2 * n]
        send_sems, recv_sems = refs[2 * n:]
        x, y, c = _place()
        cps = []
        for a in range(n):
            hr = ins[a].shape[1] // 2
            cps.append(pltpu.make_async_remote_copy(
                src_ref=ins[a].at[:, pl.ds((1 - c) * hr, hr), :], dst_ref=outs[a],
                send_sem=send_sems.at[a], recv_sem=recv_sems.at[a], device_id=(x, y, 1 - c), device_id_type=MESH))
        for cp in cps:
            cp.start()
        for cp in cps:
            cp.wait()

    hbm = pl.BlockSpec(memory_space=pl.ANY)
    return pl.pallas_call(
        body, name=name,
        in_specs=[hbm] * n, out_specs=[hbm] * n,
        out_shape=[jax.ShapeDtypeStruct((NCHIP, p.shape[1] // 2, p.shape[2]), p.dtype) for p in parts],
        scratch_shapes=[pltpu.SemaphoreType.DMA((n,)), pltpu.SemaphoreType.DMA((n,))],
    )(*parts)


def _pair_sum(part, got, name):
    nblk, rows, cols = part.shape
    hr = rows // 2
    tr = 512
    tc = cols if cols <= 1024 else (1024 if cols % 1024 == 0 else 896)
    c = lax.axis_index("c")
    nrt = hr // tr

    def body(c_ref, p_ref, g_ref, o_ref):
        del c_ref
        o_ref[...] = (p_ref[...] + g_ref[...]).astype(BF16)

    grid_spec = pltpu.PrefetchScalarGridSpec(
        num_scalar_prefetch=1, grid=(nblk, nrt, cols // tc),
        in_specs=[pl.BlockSpec((None, tr, tc), lambda b, i, j, cr: (b, cr[0] * nrt + i, j)),
                  pl.BlockSpec((None, tr, tc), lambda b, i, j, cr: (b, i, j))],
        out_specs=pl.BlockSpec((None, tr, tc), lambda b, i, j, cr: (b, i, j)))
    return pl.pallas_call(
        body, name=name, grid_spec=grid_spec,
        out_shape=jax.ShapeDtypeStruct((nblk, hr, cols), BF16),
        compiler_params=_cp(("parallel",) * 3),
    )(jnp.reshape(c, (1,)).astype(jnp.int32), part, got)


def _scatter_start(qs, name):
    n = len(qs)

    def body(*refs):
        ins, lands = refs[:n], refs[n:2 * n]
        token = refs[-1]
        for cp in _scatter_copies(ins, lands, refs[2 * n], refs[2 * n + 1]):
            cp.start()
        token[...] = jnp.zeros_like(token)

    lands = [lax.empty((3,) + q.shape[1:], q.dtype) for q in qs]
    res = pl.pallas_call(
        body, name=name,
        in_specs=[HBM_SPEC] * (2 * n),
        out_specs=[SEM_SPEC, SEM_SPEC] + [HBM_SPEC] * (2 * n) + [pl.BlockSpec(memory_space=pltpu.VMEM)],
        out_shape=[pltpu.SemaphoreType.DMA((3 * n,)), pltpu.SemaphoreType.DMA((3 * n,))]
        + [pltpu.HBM(b.shape, b.dtype) for b in qs + lands] + [TOKEN],
        input_output_aliases={a: 2 + a for a in range(2 * n)},
        compiler_params=pltpu.CompilerParams(has_side_effects=EFFECT),
    )(*[pltpu.with_memory_space_constraint(b, pltpu.HBM) for b in qs + lands])
    return res[0], res[1], list(res[2:2 + n]), list(res[2 + n:2 + 2 * n]), res[-1]


def _scatter_copies(ins, lands, send_sems, recv_sems):
    x, y, c = _place()
    cps = []
    for a in range(len(ins)):
        for j, (px, py) in enumerate(_other_chips(x, y)):
            cps.append(pltpu.make_async_remote_copy(
                src_ref=ins[a].at[2 * px + py], dst_ref=lands[a].at[j],
                send_sem=send_sems.at[3 * a + j], recv_sem=recv_sems.at[3 * a + j], device_id=(px, py, c), device_id_type=MESH))
    return cps


def _scatter_wait(qs, lands, send_sems, recv_sems, after, name):
    n = len(qs)

    def body(*refs):
        for cp in _scatter_copies(refs[:n], refs[n:2 * n], refs[2 * n], refs[2 * n + 1]):
            cp.wait_send()
            cp.wait_recv()

    res = pl.pallas_call(
        body, name=name,
        in_specs=[HBM_SPEC] * (2 * n) + [SEM_SPEC, SEM_SPEC, ANY_SPEC],
        out_specs=[HBM_SPEC] * (2 * n),
        out_shape=[pltpu.HBM(b.shape, b.dtype) for b in qs + lands],
        input_output_aliases={a: a for a in range(2 * n)},
        compiler_params=pltpu.CompilerParams(has_side_effects=EFFECT),
    )(*qs, *lands, send_sems, recv_sems, after)
    return list(res[:n]), list(res[n:])


def _chip_sum(q, got, name):
    _, hr, cols = got.shape
    tr, tc = 512, _tile_cols(cols)
    chip = 2 * lax.axis_index("x") + lax.axis_index("y")
    c = lax.axis_index("c")

    def body(idx_ref, q_ref, g_ref, o_ref):
        del idx_ref
        acc = q_ref[...].astype(F32)
        for i in range(3):
            acc = acc + g_ref[i].astype(F32)
        o_ref[...] = acc

    grid_spec = pltpu.PrefetchScalarGridSpec(
        num_scalar_prefetch=1, grid=(hr // tr, cols // tc),
        in_specs=[pl.BlockSpec((None, tr, tc), lambda i, j, ix: (ix[0], i, j)),
                  pl.BlockSpec((3, tr, tc), lambda i, j, ix: (0, i, j))],
        out_specs=pl.BlockSpec((None, tr, tc), lambda i, j, ix: (ix[1], i, j)))
    return pl.pallas_call(
        body, name=name, grid_spec=grid_spec,
        out_shape=jax.ShapeDtypeStruct((2, hr, cols), F32),
        compiler_params=_cp(("parallel", "parallel")),
    )(jnp.stack([chip, c]).astype(jnp.int32), q, got)


def _join_halves(bufs, name):
    n = len(bufs)

    def body(*refs):
        outs = refs[n:2 * n]
        send_sems, recv_sems = refs[2 * n:]
        x, y, c = _place()
        cps = []
        for a in range(n):
            cps.append(pltpu.make_async_remote_copy(src_ref=outs[a].at[c], dst_ref=outs[a].at[c], send_sem=send_sems.at[a],
                                                    recv_sem=recv_sems.at[a], device_id=(x, y, 1 - c), device_id_type=MESH))
        for cp in cps:
            cp.start()
        for a in range(n):
            theirs = outs[a].at[1 - c]
            pltpu.make_async_remote_copy(src_ref=theirs, dst_ref=theirs, send_sem=send_sems.at[a], recv_sem=recv_sems.at[a],
                                         device_id=(x, y, 1 - c), device_id_type=MESH).wait_recv()
        for cp in cps:
            cp.wait_send()

    hbm = pl.BlockSpec(memory_space=pl.ANY)
    return pl.pallas_call(
        body, name=name,
        in_specs=[hbm] * n, out_specs=[hbm] * n,
        out_shape=[jax.ShapeDtypeStruct(b.shape, b.dtype) for b in bufs],
        input_output_aliases={a: a for a in range(n)},
        scratch_shapes=[pltpu.SemaphoreType.DMA((n,)), pltpu.SemaphoreType.DMA((n,))],
    )(*bufs)


N_DEV = 8


def _gather_small(block, reduce, name):
    m_per, cols = block.shape

    def body(x_ref, out_ref, all_ref, send_sems, recv_sems, local_sem):
        x, y, c = _place()
        me, sibling = (x, y, c), (x, y, 1 - c)
        chips = _other_chips(x, y)

        def rows(px, py, pc):
            return all_ref.at[pl.ds((4 * px + 2 * py + pc) * m_per, m_per), :]

        def copy(k, block_of, to, src=None):
            return pltpu.make_async_remote_copy(
                src_ref=rows(*block_of) if src is None else src, dst_ref=rows(*block_of),
                send_sem=send_sems.at[k], recv_sem=recv_sems.at[k], device_id=to, device_id_type=MESH)

        mine = pltpu.make_async_copy(x_ref, rows(*me), local_sem)
        mine.start()
        first = [copy(0, me, sibling, src=x_ref)]
        first += [copy(1 + j, me, (*chip, c), src=x_ref) for j, chip in enumerate(chips)]
        for cp in first:
            cp.start()
        passed = [copy(4 + j, (*chip, c), sibling) for j, chip in enumerate(chips)]
        for j, chip in enumerate(chips):
            copy(1 + j, (*chip, c), me).wait_recv()
            passed[j].start()
        copy(0, sibling, me).wait_recv()
        for j, chip in enumerate(chips):
            copy(4 + j, (*chip, 1 - c), me).wait_recv()
        for cp in first + passed:
            cp.wait_send()
        mine.wait()
        if reduce:
            acc = all_ref[pl.ds(0, m_per), :]
            for i in range(1, N_DEV):
                acc = acc + all_ref[pl.ds(i * m_per, m_per), :]
            out_ref[...] = acc
        else:
            out_ref[...] = all_ref[...]

    out_rows = m_per if reduce else N_DEV * m_per
    return pl.pallas_call(
        body, name=name,
        in_specs=[pl.BlockSpec(memory_space=pltpu.VMEM)],
        out_specs=pl.BlockSpec(memory_space=pltpu.VMEM),
        out_shape=jax.ShapeDtypeStruct((out_rows, cols), F32),
        scratch_shapes=[pltpu.VMEM((N_DEV * m_per, cols), F32), pltpu.SemaphoreType.DMA((7,)),
                        pltpu.SemaphoreType.DMA((7,)), pltpu.SemaphoreType.DMA],
    )(block)


class _Comm:
    def __init__(self, w_in, w_kv, w_out, chip):
        self.bufs = [_cast_own(w_in, chip, "cast_w_in"), _cast_own(w_kv, chip, "cast_w_mem_kv"),
                     _cast_own(w_out, chip, "cast_w_out")]

    def gather_started(self):
        self.in_sems, (self.b_in,), tok_in = _gather_start(self.bufs[:1], (NEAR, FAR), "gather_start_in")
        (self.rest_sems,), self.b_rest, tok_rest = _gather_start(self.bufs[1:], (NEAR + FAR,), "gather_start_rest")
        return [tok_in, tok_rest]

    def w_in_own(self):
        return self.b_in.reshape(NCHIP * D, WB)

    def w_in_near(self, after):
        (b_in,) = _gather_wait([self.b_in], self.in_sems[0], NEAR, after, "gather_wait_in_near")
        (self.b_in,) = _forward_halves([b_in], NEAR, "forward_in_near")
        return self.b_in.reshape(NCHIP * D, WB)

    def w_in_all(self, after):
        (b_in,) = _gather_wait([self.b_in], self.in_sems[1], FAR, after, "gather_wait_in_far")
        (self.b_in,) = _forward_halves([b_in], FAR, "forward_in_far")
        return self.b_in.reshape(NCHIP * D, WB)

    def w_rest(self, after):
        b_rest = _gather_wait(self.b_rest, self.rest_sems, NEAR + FAR, after, "gather_wait_rest")
        b_kv, b_out = _forward_halves(b_rest, NEAR + FAR, "forward_rest")
        return b_kv.reshape(D, 2 * XW), b_out.reshape(D, D)

    def reduce_start(self, parts, tag):
        got = _sibling_halves(parts, "sibling_halves_" + tag)
        qs = [_pair_sum(p, g, f"pair_sum_{tag}{i}") for i, (p, g) in enumerate(zip(parts, got))]
        return _scatter_start(qs, "scatter_start_" + tag)

    def reduce_finish(self, state, tag, after):
        send_sems, recv_sems, qs, lands, _ = state
        qs, lands = _scatter_wait(qs, lands, send_sems, recv_sems, after, "scatter_wait_" + tag)
        halves = [_chip_sum(q, l, f"chip_sum_{tag}{i}") for i, (q, l) in enumerate(zip(qs, lands))]
        return [j.reshape(2 * j.shape[1], j.shape[2]) for j in _join_halves(halves, "join_halves_" + tag)]


def kernel(x, mem, pre_norm, w_in, conv_w, mem_norm, w_mem_kv, w_out, post_norm, loss_target, m_pre_norm, m_w_in, m_conv_w, m_mem_norm, m_w_mem_kv, m_w_out, m_post_norm, v_pre_norm, v_w_in, v_conv_w, v_mem_norm, v_w_mem_kv, v_w_out, v_post_norm):
    chip = 2 * lax.axis_index("x") + lax.axis_index("y")

    cw_blk = jnp.zeros((8, 384), F32).at[:3].set(conv_w[0])
    cw_all = _gather_small(cw_blk, False, "gather_conv_w").reshape(NCHIP, 2, 8, 384)[:, 0, :3]
    conv_full = jnp.transpose(cw_all, (1, 0, 2)).reshape(3, CW)

    comm = _Comm(w_in[0], w_mem_kv[0], w_out[0], chip)
    loss, grad_x, d_pre, d_mem, d_post, d_conv, r_in, r_kv, r_out = _local_step(
        x, mem, pre_norm, conv_full, mem_norm, post_norm, loss_target, chip, comm)

    pack = jnp.concatenate([d_pre, d_mem, d_post, jnp.pad(d_conv, ((0, 0), (0, D - CW))),
                            jnp.pad(loss, ((0, 0), (0, D - 128))), jnp.zeros((1, D), F32)], axis=0)
    tot = _gather_small(pack, True, "reduce_small")
    g_pre, g_mem, g_post = tot[0:1], tot[1:2], tot[2:3]
    g_conv = lax.dynamic_slice(tot[3:6, :CW], (0, chip * 384), (3, 384))
    loss_out = tot[6, 0]

    names = ("pre_norm", "w_in", "conv_w", "mem_norm", "w_mem_kv", "w_out", "post_norm")
    ws = (pre_norm, w_in[0], conv_w[0], mem_norm, w_mem_kv[0], w_out[0], post_norm)
    gs = (g_pre, r_in, g_conv, g_mem, r_kv, r_out, g_post)
    ms = (m_pre_norm, m_w_in[0], m_conv_w[0], m_mem_norm, m_w_mem_kv[0], m_w_out[0], m_post_norm)
    vs = (v_pre_norm, v_w_in[0], v_conv_w[0], v_mem_norm, v_w_mem_kv[0], v_w_out[0], v_post_norm)
    upd = [_adamw(w, g, m, v, "adamw_" + nm) for nm, w, g, m, v in zip(names, ws, gs, ms, vs)]

    def shaped(arrs):
        return [a.reshape(w.shape) if w.ndim == a.ndim else a.reshape((1,) + a.shape)
                for a, w in zip(arrs, (pre_norm, w_in, conv_w, mem_norm, w_mem_kv, w_out, post_norm))]

    grads = shaped(gs)
    deltas = shaped([u[0] for u in upd])
    new_m = shaped([u[1] for u in upd])
    new_v = shaped([u[2] for u in upd])
    return (loss_out, grad_x, *grads, *deltas, *new_m, *new_v)
```

```python
import functools

import jax
import jax.numpy as jnp
from jax import lax
from jax.experimental import pallas as pl
from jax.experimental.pallas import tpu as pltpu

F32 = jnp.float32
BF16 = jnp.bfloat16

D = 4096
S = 2048
NB = 2
T = NB * S
MLEN = 256
HD = 128
AW = 1536
CW = 1536
XW = 1024
XHD = 256
NXH = 4
NC = 14336
QA, KA, VA, ZA, UC, BC, CC, ZC, QX, ZX = 0, 1536, 3072, 4608, 6144, 7680, 9216, 10752, 12288, 13312
NCHIP = 4
WB = NC // NCHIP
DIL = (1, 4, 16)
HPG = 4
EPS = 1e-6
NEG = -1e30
ROPE_THETA = 10000.0
A_SCALE = HD ** -0.5
X_SCALE = XHD ** -0.5

ADAM_LR = 0.001
ADAM_B1 = 0.9
ADAM_B2 = 0.999
ADAM_EPS = 1e-08
ADAM_WD = 0.01
ADAM_STEP = 10

MESH = pl.DeviceIdType.MESH
MIB = 1024 * 1024


def _cp(sem, vmem_mib=48):
    return pltpu.CompilerParams(dimension_semantics=sem, vmem_limit_bytes=vmem_mib * MIB)


def _sigmoid(z):
    return 1.0 / (1.0 + jnp.exp(-z))


def _rope(x, cos, sin, half):
    return x * cos + pltpu.roll(x, half, 1) * sin


def _rope_t(g, cos, sin, half):
    return g * cos + pltpu.roll(g * sin, half, 1)


def _rms_fwd(x2, g, name, dep=None):
    rows = x2.shape[0]
    tr = 256
    deps = [] if dep is None else list(dep)

    def body(x_ref, g_ref, *rest):
        o_ref = rest[-1]
        x = x_ref[...]
        r = lax.rsqrt(jnp.mean(x * x, axis=-1, keepdims=True) + EPS)
        o_ref[...] = (x * r * g_ref[...]).astype(BF16)

    return pl.pallas_call(
        body, name=name, grid=(rows // tr,),
        in_specs=[pl.BlockSpec((tr, D), lambda i: (i, 0)), pl.BlockSpec((1, D), lambda i: (0, 0))]
        + [pl.BlockSpec(memory_space=pl.ANY)] * len(deps),
        out_specs=pl.BlockSpec((tr, D), lambda i: (i, 0)),
        out_shape=jax.ShapeDtypeStruct((rows, D), BF16),
        compiler_params=_cp(("parallel",)),
    )(x2, g, *deps)


def _norm_gain_grad(dn, x2, name):
    rows = x2.shape[0]
    tr = 256

    def body(dn_ref, x_ref, dg_ref):
        @pl.when(pl.program_id(0) == 0)
        def _():
            dg_ref[...] = jnp.zeros_like(dg_ref)
        x = x_ref[...]
        r = lax.rsqrt(jnp.mean(x * x, axis=-1, keepdims=True) + EPS)
        dg_ref[...] += jnp.sum(dn_ref[...] * (x * r), axis=0, keepdims=True)

    return pl.pallas_call(
        body, name=name, grid=(rows // tr,),
        in_specs=[pl.BlockSpec((tr, D), lambda i: (i, 0)), pl.BlockSpec((tr, D), lambda i: (i, 0))],
        out_specs=pl.BlockSpec((1, D), lambda i: (0, 0)),
        out_shape=jax.ShapeDtypeStruct((1, D), F32),
        compiler_params=_cp(("arbitrary",)),
    )(dn, x2)


def _pre_norm_bwd(dh, x2, g, dout):
    tr = 256

    def body(dh_ref, x_ref, g_ref, dout_ref, gx_ref, dg_ref):
        @pl.when(pl.program_id(0) == 0)
        def _():
            dg_ref[...] = jnp.zeros_like(dg_ref)
        x = x_ref[...]
        dh_ = dh_ref[...]
        r = lax.rsqrt(jnp.mean(x * x, axis=-1, keepdims=True) + EPS)
        xhat = x * r
        dg_ref[...] += jnp.sum(dh_ * xhat, axis=0, keepdims=True)
        dxn = dh_ * g_ref[...]
        gx_ref[...] = dout_ref[...] + r * (dxn - xhat * jnp.mean(dxn * xhat, axis=-1, keepdims=True))

    row = pl.BlockSpec((tr, D), lambda i: (i, 0))
    vec = pl.BlockSpec((1, D), lambda i: (0, 0))
    return pl.pallas_call(
        body, name="pre_norm_bwd", grid=(T // tr,),
        in_specs=[row, row, vec, row],
        out_specs=[row, vec],
        out_shape=[jax.ShapeDtypeStruct((T, D), F32), jax.ShapeDtypeStruct((1, D), F32)],
        compiler_params=_cp(("arbitrary",)),
    )(dh, x2, g, dout)


def _post_norm_loss(y, x2, tgt, g):
    tr = 256

    def body(y_ref, x_ref, t_ref, g_ref, dy_ref, dout_ref, dg_ref, loss_ref):
        @pl.when(pl.program_id(0) == 0)
        def _():
            dg_ref[...] = jnp.zeros_like(dg_ref)
            loss_ref[...] = jnp.zeros_like(loss_ref)
        yv = y_ref[...]
        gv = g_ref[...]
        r = lax.rsqrt(jnp.mean(yv * yv, axis=-1, keepdims=True) + EPS)
        yhat = yv * r
        err = x_ref[...] + yhat * gv - t_ref[...]
        loss_ref[...] += jnp.sum(jnp.sum(err * err, axis=1, keepdims=True), axis=0, keepdims=True) * (0.5 / D)
        dout = err * (1.0 / D)
        dout_ref[...] = dout
        dg_ref[...] += jnp.sum(dout * yhat, axis=0, keepdims=True)
        dyn = dout * gv
        dy_ref[...] = (r * (dyn - yhat * jnp.mean(dyn * yhat, axis=-1, keepdims=True))).astype(BF16)

    row = pl.BlockSpec((tr, D), lambda i: (i, 0))
    vec = pl.BlockSpec((1, D), lambda i: (0, 0))
    return pl.pallas_call(
        body, name="post_norm_loss", grid=(T // tr,),
        in_specs=[row, row, row, vec],
        out_specs=[row, row, vec, pl.BlockSpec((1, 128), lambda i: (0, 0))],
        out_shape=[jax.ShapeDtypeStruct((T, D), BF16), jax.ShapeDtypeStruct((T, D), F32),
                   jax.ShapeDtypeStruct((1, D), F32), jax.ShapeDtypeStruct((1, 128), F32)],
        compiler_params=_cp(("arbitrary",)),
    )(y, x2, tgt, g)


NN = (((1,), (0,)), ((), ()))
NT = (((1,), (1,)), ((), ()))
TN = (((0,), (0,)), ((), ()))


def _as_index(v):
    return jnp.reshape(v, (1,)).astype(jnp.int32)


def _matmul(a, b, *, name, dims, grid, a_block, a_map, b_block, b_map, o_block, o_map, out_shape, out_dtype=F32,
            index=None, prev=None, deps=()):
    assert out_dtype == F32
    extra = ([] if prev is None else [prev]) + [d for d in deps if d is not None]
    first = 0 if index is None else 1

    def body(*refs):
        a_ref, b_ref, o_ref = refs[first], refs[first + 1], refs[first + 2 + len(extra)]
        @pl.when(pl.program_id(2) == 0)
        def _():
            o_ref[...] = jnp.zeros_like(o_ref)

        o_ref[...] += lax.dot_general(a_ref[...], b_ref[...], dims, preferred_element_type=F32)

    in_specs = [pl.BlockSpec(a_block, a_map), pl.BlockSpec(b_block, b_map)] + [pl.BlockSpec(memory_space=pl.ANY)] * len(extra)
    out_specs = pl.BlockSpec(o_block, o_map)
    kwargs = dict(name=name, out_shape=jax.ShapeDtypeStruct(out_shape, out_dtype),
                  input_output_aliases={} if prev is None else {first + 2: 0},
                  compiler_params=_cp(("parallel", "parallel", "arbitrary"), vmem_mib=56))
    if index is None:
        return pl.pallas_call(body, grid=grid, in_specs=in_specs, out_specs=out_specs, **kwargs)(a, b, *extra)
    grid_spec = pltpu.PrefetchScalarGridSpec(num_scalar_prefetch=1, grid=grid, in_specs=in_specs, out_specs=out_specs)
    return pl.pallas_call(body, grid_spec=grid_spec, **kwargs)(_as_index(index), a, b, *extra)


def _mm_nn(a, b, name, tm, tn, tk):
    m, kd = a.shape
    n = b.shape[1]
    return _matmul(a, b, name=name, dims=NN, grid=(m // tm, n // tn, kd // tk),
                   a_block=(tm, tk), a_map=lambda i, j, k: (i, k),
                   b_block=(tk, tn), b_map=lambda i, j, k: (k, j),
                   o_block=(tm, tn), o_map=lambda i, j, k: (i, j), out_shape=(m, n))


def _mm_nt(a, b, name, tm, tn, tk):
    m, kd = a.shape
    n = b.shape[0]
    return _matmul(a, b, name=name, dims=NT, grid=(m // tm, n // tn, kd // tk),
                   a_block=(tm, tk), a_map=lambda i, j, k: (i, k),
                   b_block=(tn, tk), b_map=lambda i, j, k: (j, k),
                   o_block=(tm, tn), o_map=lambda i, j, k: (i, j), out_shape=(m, n))


def _mm_tn(a, b, name, tm, tn, tk):
    kd, m = a.shape
    n = b.shape[1]
    return _matmul(a, b, name=name, dims=TN, grid=(m // tm, n // tn, kd // tk),
                   a_block=(tk, tm), a_map=lambda i, j, k: (k, i),
                   b_block=(tk, tn), b_map=lambda i, j, k: (k, j),
                   o_block=(tm, tn), o_map=lambda i, j, k: (i, j), out_shape=(m, n))


W_TN = 1792
W_NJ = WB // W_TN


def _proj_part(h, wg, chip, masks, name, prev=None, dep=None):
    tm, tk = 1024, 2048

    def blk(j, ix):
        m = masks[0]
        for t in range(1, len(masks)):
            m = jnp.where(j // W_NJ == t, masks[t], m)
        return jnp.bitwise_xor(ix[0], m)

    return _matmul(h, wg, name=name, dims=NN, grid=(T // tm, len(masks) * W_NJ, D // tk), index=chip, prev=prev, deps=(dep,),
                   a_block=(tm, tk), a_map=lambda i, j, k, ix: (i, k),
                   b_block=(tk, W_TN), b_map=lambda i, j, k, ix: (blk(j, ix) * (D // tk) + k, j % W_NJ),
                   o_block=(tm, W_TN), o_map=lambda i, j, k, ix: (i, blk(j, ix) * W_NJ + j % W_NJ), out_shape=(T, NC))


def _dh(dproj, wg, dep=None):
    tm, tn = 1024, 1024
    return _matmul(dproj, wg, name="dh", dims=NT, grid=(T // tm, D // tn, NC // W_TN), deps=(dep,),
                   a_block=(tm, W_TN), a_map=lambda i, j, k: (i, k),
                   b_block=(tn, W_TN), b_map=lambda i, j, k: ((k // W_NJ) * (D // tn) + j, k % W_NJ),
                   o_block=(tm, tn), o_map=lambda i, j, k: (i, j), out_shape=(T, D))


def _grad_w_in(h, dproj):
    tm, tk = 1024, 2048
    return _matmul(h, dproj, name="grad_w_in", dims=TN, grid=(D // tm, NC // W_TN, T // tk),
                   a_block=(tk, tm), a_map=lambda i, j, k: (k, i),
                   b_block=(tk, W_TN), b_map=lambda i, j, k: (k, j),
                   o_block=(tm, W_TN), o_map=lambda i, j, k: ((j // W_NJ) * (D // tm) + i, j % W_NJ),
                   out_shape=(NCHIP * D, WB))


def _rope_tables(pos, half):
    inv = 1.0 / (ROPE_THETA ** (jnp.arange(half, dtype=F32) / half))
    ang = pos.astype(F32)[:, None] * inv[None, :]
    cos, sin = jnp.cos(ang), jnp.sin(ang)
    return jnp.concatenate([cos, cos], axis=1), jnp.concatenate([-sin, sin], axis=1)


def _band_mask(r0):
    qi = lax.broadcasted_iota(jnp.int32, (128, 256), 0)
    kk = lax.broadcasted_iota(jnp.int32, (128, 256), 1)
    return (kk >= qi) & (kk <= qi + 128) & (kk + r0 >= 128)


def _dil_rows(r, n, d):
    if d == 1:
        return pl.ds(pl.multiple_of(n * 128, 128), 128)
    return pl.ds(r + d * 128 * n, 128, stride=d)


def _attn_fwd(proj, cosf, sinf, g):
    d = DIL[g]
    ln = S // d
    nblk = ln // 128
    proj_v = proj.reshape(NB, S, NC)

    def body(q_ref, k_ref, v_ref, cos_ref, sin_ref, o_ref, l_ref, k_s, v_s):
        k_s[:, pl.ds(0, 128), :] = jnp.zeros((d, 128, HD), BF16)
        v_s[:, pl.ds(0, 128), :] = jnp.zeros((d, 128, HD), BF16)

        def prep(i, carry):
            r, n = i // nblk, i % nblk
            rows = _dil_rows(r, n, d)
            dst = pl.ds(pl.multiple_of(n * 128 + 128, 128), 128)
            k_s[r, dst, :] = _rope(k_ref[rows, :], cos_ref[rows, :], sin_ref[rows, :], HD // 2).astype(BF16)
            v_s[r, dst, :] = v_ref[rows, :].astype(BF16)
            return carry

        lax.fori_loop(0, d * nblk, prep, 0)

        def step(i, carry):
            r, n = i // nblk, i % nblk
            rows = _dil_rows(r, n, d)
            r0 = pl.multiple_of(n * 128, 128)
            qr = _rope(q_ref[rows, :], cos_ref[rows, :], sin_ref[rows, :], HD // 2).astype(BF16)
            kw = k_s[r, pl.ds(r0, 256), :]
            vw = v_s[r, pl.ds(r0, 256), :]
            sc = lax.dot_general(qr, kw, NT, preferred_element_type=F32) * A_SCALE
            sc = jnp.where(_band_mask(r0), sc, NEG)
            m = jnp.max(sc, axis=1, keepdims=True)
            p = jnp.exp(sc - m)
            l = jnp.sum(p, axis=1, keepdims=True)
            o_ref[rows, :] = jnp.dot(p.astype(BF16), vw, preferred_element_type=F32) / l
            l_ref[rows, :] = jnp.broadcast_to(m + jnp.log(l), (128, HD))
            return carry

        lax.fori_loop(0, d * nblk, step, 0)

    def col(off):
        return lambda b, h: (b, 0, off // HD + HPG * g + h)

    blk = (None, S, HD)
    tab = pl.BlockSpec((S, HD), lambda b, h: (0, 0))
    out = pl.BlockSpec(blk, lambda b, h: (b, 0, h))
    o, l = pl.pallas_call(
        body, name=f"attn_fwd_d{d}", grid=(NB, HPG),
        in_specs=[pl.BlockSpec(blk, col(QA)), pl.BlockSpec(blk, col(KA)), pl.BlockSpec(blk, col(VA)), tab, tab],
        out_specs=[out, out],
        out_shape=[jax.ShapeDtypeStruct((NB, S, HPG * HD), F32)] * 2,
        scratch_shapes=[pltpu.VMEM((d, ln + 128, HD), BF16), pltpu.VMEM((d, ln + 128, HD), BF16)],
        compiler_params=_cp(("parallel", "parallel")),
    )(proj_v, proj_v, proj_v, cosf, sinf)
    return o.reshape(T, HPG * HD), l.reshape(T, HPG * HD)


def _attn_bwd(proj, cosf, sinf, da, lse, delta, dproj, g):
    d = DIL[g]
    ln = S // d
    nblk = ln // 128
    proj_v = proj.reshape(NB, S, NC)
    dproj_v = dproj.reshape(NB, S, NC)
    da_v = da.reshape(NB, S, AW)
    lse_v = lse.reshape(NB, S, HPG * HD)
    delta_v = delta.reshape(NB, S, HPG * HD)

    def body(q_ref, k_ref, v_ref, cos_ref, sin_ref, da_ref, lse_ref, dl_ref, dp_in_ref, o_ref,
             k_s, v_s, stg, dk_s, dv_s):
        del dp_in_ref
        w = pl.program_id(2)

        def emit():
            def cast(n, carry):
                rows = pl.ds(pl.multiple_of(n * 256, 256), 256)
                o_ref[rows, :] = stg[rows, :].astype(BF16)
                return carry

            lax.fori_loop(0, S // 256, cast, 0)

        @pl.when(w == 0)
        def _():
            k_s[:, pl.ds(0, 128), :] = jnp.zeros((d, 128, HD), BF16)
            v_s[:, pl.ds(0, 128), :] = jnp.zeros((d, 128, HD), BF16)
            dk_s[...] = jnp.zeros_like(dk_s)
            dv_s[...] = jnp.zeros_like(dv_s)

            def prep(i, carry):
                r, n = i // nblk, i % nblk
                rows = _dil_rows(r, n, d)
                dst = pl.ds(pl.multiple_of(n * 128 + 128, 128), 128)
                k_s[r, dst, :] = _rope(k_ref[rows, :], cos_ref[rows, :], sin_ref[rows, :], HD // 2).astype(BF16)
                v_s[r, dst, :] = v_ref[rows, :].astype(BF16)
                return carry

            lax.fori_loop(0, d * nblk, prep, 0)

            def step(i, carry):
                r, n = i // nblk, i % nblk
                rows = _dil_rows(r, n, d)
                r0 = pl.multiple_of(n * 128, 128)
                win = pl.ds(r0, 256)
                cos, sin = cos_ref[rows, :], sin_ref[rows, :]
                qr = _rope(q_ref[rows, :], cos, sin, HD // 2).astype(BF16)
                kw = k_s[r, win, :]
                vw = v_s[r, win, :]
                sc = lax.dot_general(qr, kw, NT, preferred_element_type=F32) * A_SCALE
                sc = jnp.where(_band_mask(r0), sc, NEG)
                p = jnp.exp(sc - lse_ref[rows, :][:, :1])
                da_b = da_ref[rows, :].astype(BF16)
                dp = lax.dot_general(da_b, vw, NT, preferred_element_type=F32)
                ds_b = (p * (dp - dl_ref[rows, :][:, :1]) * A_SCALE).astype(BF16)
                p_b = p.astype(BF16)
                dq = jnp.dot(ds_b, kw, preferred_element_type=F32)
                stg[rows, :] = _rope_t(dq, cos, sin, HD // 2)
                dk_s[r, win, :] += lax.dot_general(ds_b, qr, TN, preferred_element_type=F32)
                dv_s[r, win, :] += lax.dot_general(p_b, da_b, TN, preferred_element_type=F32)
                return carry

            lax.fori_loop(0, d * nblk, step, 0)
            emit()

        @pl.when(w == 1)
        def _():
            def put(i, carry):
                r, n = i // nblk, i % nblk
                rows = _dil_rows(r, n, d)
                src = pl.ds(pl.multiple_of(n * 128 + 128, 128), 128)
                stg[rows, :] = _rope_t(dk_s[r, src, :], cos_ref[rows, :], sin_ref[rows, :], HD // 2)
                return carry

            lax.fori_loop(0, d * nblk, put, 0)
            emit()

        @pl.when(w == 2)
        def _():
            def put(i, carry):
                r, n = i // nblk, i % nblk
                src = pl.ds(pl.multiple_of(n * 128 + 128, 128), 128)
                stg[_dil_rows(r, n, d), :] = dv_s[r, src, :]
                return carry

            lax.fori_loop(0, d * nblk, put, 0)
            emit()

    def col(off):
        return lambda b, h, w: (b, 0, off // HD + HPG * g + h)

    blk = (None, S, HD)
    tab = pl.BlockSpec((S, HD), lambda b, h, w: (0, 0))
    per_head = pl.BlockSpec(blk, lambda b, h, w: (b, 0, h))
    out = pl.pallas_call(
        body, name=f"attn_bwd_d{d}", grid=(NB, HPG, 3),
        in_specs=[pl.BlockSpec(blk, col(QA)), pl.BlockSpec(blk, col(KA)), pl.BlockSpec(blk, col(VA)), tab, tab,
                  pl.BlockSpec(blk, col(0)), per_head, per_head, pl.BlockSpec(memory_space=pl.ANY)],
        out_specs=pl.BlockSpec(blk, lambda b, h, w: (b, 0, (AW // HD) * w + HPG * g + h)),
        out_shape=jax.ShapeDtypeStruct(dproj_v.shape, BF16),
        input_output_aliases={8: 0},
        scratch_shapes=[pltpu.VMEM((d, ln + 128, HD), BF16), pltpu.VMEM((d, ln + 128, HD), BF16),
                        pltpu.VMEM((S, HD), F32), pltpu.VMEM((d, ln + 128, HD), F32), pltpu.VMEM((d, ln + 128, HD), F32)],
        compiler_params=_cp(("arbitrary",) * 3),
    )(proj_v, proj_v, proj_v, cosf, sinf, da_v, lse_v, delta_v, dproj_v)
    return out.reshape(T, NC)


def _attn_mix(proj, os_, ls_):
    tr = 256
    gw = HPG * HD

    def body(o0, o1, o2, l0, l1, l2, z_ref, cat_ref):
        m = jnp.maximum(jnp.maximum(l0[...], l1[...]), l2[...])
        e = [jnp.exp(l[...] - m) for l in (l0, l1, l2)]
        inv = 1.0 / (e[0] + e[1] + e[2])
        for gi, o in enumerate((o0, o1, o2)):
            z = z_ref[:, gi * gw:(gi + 1) * gw]
            cat_ref[:, gi * gw:(gi + 1) * gw] = (o[...] * (e[gi] * inv) * (z * _sigmoid(z))).astype(BF16)

    grp = pl.BlockSpec((tr, gw), lambda i: (i, 0))
    return pl.pallas_call(
        body, name="attn_mix", grid=(T // tr,),
        in_specs=[grp] * 6 + [pl.BlockSpec((tr, AW), lambda i: (i, ZA // AW))],
        out_specs=pl.BlockSpec((tr, AW), lambda i: (i, 0)),
        out_shape=jax.ShapeDtypeStruct((T, D), BF16),
        compiler_params=_cp(("parallel",)),
    )(*os_, *ls_, proj)


def _attn_mix_bwd(dcat, proj, os_, ls_, dep=None):
    tr = 256
    gw = HPG * HD
    deps = [] if dep is None else [dep]

    def body(dy_ref, o0, o1, o2, l0, l1, l2, z_ref, *rest):
        da_ref, lse_ref, dl_ref, dz_ref = rest[len(deps):]
        m = jnp.maximum(jnp.maximum(l0[...], l1[...]), l2[...])
        e = [jnp.exp(l[...] - m) for l in (l0, l1, l2)]
        den = e[0] + e[1] + e[2]
        inv = 1.0 / den
        lse_ref[...] = m + jnp.log(den)
        acc = jnp.zeros((tr, gw), F32)
        for gi, o in enumerate((o0, o1, o2)):
            cols = slice(gi * gw, (gi + 1) * gw)
            z = z_ref[:, cols]
            dy = dy_ref[:, cols]
            sg = _sigmoid(z)
            a = o[...] * (e[gi] * inv)
            da = dy * (z * sg)
            da_ref[:, cols] = da
            dz_ref[:, cols] = (dy * a * (sg * (1.0 + z * (1.0 - sg)))).astype(BF16)
            acc = acc + da * a
        for hh in range(HPG):
            cols = slice(hh * HD, (hh + 1) * HD)
            dl_ref[:, cols] = jnp.broadcast_to(jnp.sum(acc[:, cols], axis=1, keepdims=True), (tr, HD))

    grp = pl.BlockSpec((tr, gw), lambda i: (i, 0))
    return pl.pallas_call(
        body, name="attn_mix_bwd", grid=(T // tr,),
        in_specs=[pl.BlockSpec((tr, AW), lambda i: (i, 0))] + [grp] * 6 + [pl.BlockSpec((tr, AW), lambda i: (i, ZA // AW))]
        + [pl.BlockSpec(memory_space=pl.ANY)] * len(deps),
        out_specs=[pl.BlockSpec((tr, AW), lambda i: (i, 0)), grp, grp, pl.BlockSpec((tr, AW), lambda i: (i, ZA // AW))],
        out_shape=[jax.ShapeDtypeStruct((T, AW), F32), jax.ShapeDtypeStruct((T, gw), F32),
                   jax.ShapeDtypeStruct((T, gw), F32), jax.ShapeDtypeStruct((T, NC), BF16)],
        compiler_params=_cp(("parallel",)),
    )(dcat, *os_, *ls_, proj, *deps)


CT = 128


def _shift_down(x, n):
    rows = lax.broadcasted_iota(jnp.int32, x.shape, 0)
    return jnp.where(rows >= n, pltpu.roll(x, n, 0), 0.0)


def _shift_up(x, n):
    rows = lax.broadcasted_iota(jnp.int32, x.shape, 0)
    return jnp.where(rows < x.shape[0] - n, pltpu.roll(x, x.shape[0] - n, 0), 0.0)


def _conv_fwd(proj, conv_w, cat):
    proj_v = proj.reshape(NB, S, NC)
    cat_v = cat.reshape(NB, S, D)

    def body(u_ref, b_ref, c_ref, z_ref, w_ref, cat_in, o_ref):
        del cat_in
        cu = c_ref[...] * u_ref[...]
        cv = _shift_down(cu, 2) * w_ref[0:1, :] + _shift_down(cu, 1) * w_ref[1:2, :] + cu * w_ref[2:3, :]
        z = z_ref[...]
        o_ref[...] = (b_ref[...] * cv * (z * _sigmoid(z))).astype(BF16)

    def seg(off):
        return pl.BlockSpec((None, S, CT), lambda b, j: (b, 0, off // CT + j))

    out = pl.pallas_call(
        body, name="conv_fwd", grid=(NB, CW // CT),
        in_specs=[seg(UC), seg(BC), seg(CC), seg(ZC), pl.BlockSpec((3, CT), lambda b, j: (0, j)),
                  pl.BlockSpec(memory_space=pl.ANY)],
        out_specs=pl.BlockSpec((None, S, CT), lambda b, j: (b, 0, AW // CT + j)),
        out_shape=jax.ShapeDtypeStruct((NB, S, D), BF16),
        input_output_aliases={5: 0},
        compiler_params=_cp(("parallel", "parallel")),
    )(proj_v, proj_v, proj_v, proj_v, conv_w, cat_v)
    return out.reshape(T, D)


def _conv_bwd(dcat, proj, conv_w, dproj):
    proj_v = proj.reshape(NB, S, NC)
    dproj_v = dproj.reshape(NB, S, NC)
    dcat_v = dcat.reshape(NB, S, D)

    def body(dy_ref, u_ref, b_ref, c_ref, z_ref, w_ref, dp_in, o_ref, dw_ref, st):
        del dp_in
        b = pl.program_id(1)
        w = pl.program_id(2)

        @pl.when((b == 0) & (w == 0))
        def _():
            dw_ref[...] = jnp.zeros_like(dw_ref)

        @pl.when(w == 0)
        def _():
            u, c, z, bb, dy = u_ref[...], c_ref[...], z_ref[...], b_ref[...], dy_ref[...]
            cu = c * u
            s1 = _shift_down(cu, 1)
            s2 = _shift_down(cu, 2)
            cv = s2 * w_ref[0:1, :] + s1 * w_ref[1:2, :] + cu * w_ref[2:3, :]
            sg = _sigmoid(z)
            sz = z * sg
            dcv = dy * bb * sz
            st[0] = dy * cv * sz
            st[2] = dy * bb * cv * (sg * (1.0 + z * (1.0 - sg)))
            dw_ref[0:1, :] += jnp.sum(dcv * s2, axis=0, keepdims=True)
            dw_ref[1:2, :] += jnp.sum(dcv * s1, axis=0, keepdims=True)
            dw_ref[2:3, :] += jnp.sum(dcv * cu, axis=0, keepdims=True)
            dcu = dcv * w_ref[2:3, :] + _shift_up(dcv, 1) * w_ref[1:2, :] + _shift_up(dcv, 2) * w_ref[0:1, :]
            st[1] = dcu * u
            o_ref[...] = (dcu * c).astype(BF16)

        for k in range(3):
            @pl.when(w == k + 1)
            def _(k=k):
                o_ref[...] = st[k].astype(BF16)

    def seg(off):
        return pl.BlockSpec((None, S, CT), lambda j, b, w: (b, 0, off // CT + j))

    out, dw = pl.pallas_call(
        body, name="conv_bwd", grid=(CW // CT, NB, 4),
        in_specs=[pl.BlockSpec((None, S, CT), lambda j, b, w: (b, 0, AW // CT + j)),
                  seg(UC), seg(BC), seg(CC), seg(ZC), pl.BlockSpec((3, CT), lambda j, b, w: (0, j)),
                  pl.BlockSpec(memory_space=pl.ANY)],
        out_specs=[pl.BlockSpec((None, S, CT), lambda j, b, w: (b, 0, (UC + w * CW) // CT + j)),
                   pl.BlockSpec((3, CT), lambda j, b, w: (0, j))],
        out_shape=[jax.ShapeDtypeStruct((NB, S, NC), BF16), jax.ShapeDtypeStruct((3, CW), F32)],
        input_output_aliases={6: 0},
        scratch_shapes=[pltpu.VMEM((3, S, CT), F32)],
        compiler_params=_cp(("arbitrary",) * 3),
    )(dcat_v, proj_v, proj_v, proj_v, proj_v, conv_w, dproj_v)
    return out.reshape(T, NC), dw


XT = 512


def _cross_fwd(proj, mkv, cosq, sinq, cosm, sinm, cat):
    proj_v = proj.reshape(NB, S, NC)
    mkv_v = mkv.reshape(NB, MLEN, 2 * XW)
    cat_v = cat.reshape(NB, S, D)

    def body(q_ref, z_ref, mk_ref, mv_ref, cq, sq, cm, sm, cat_in, o_ref):
        del cat_in
        mkr = _rope(mk_ref[...], cm[...], sm[...], XHD // 2).astype(BF16)
        qr = _rope(q_ref[...], cq[...], sq[...], XHD // 2).astype(BF16)
        sc = lax.dot_general(qr, mkr, NT, preferred_element_type=F32) * X_SCALE
        p = jnp.exp(sc - jnp.max(sc, axis=1, keepdims=True))
        p = p / jnp.sum(p, axis=1, keepdims=True)
        ox = jnp.dot(p.astype(BF16), mv_ref[...].astype(BF16), preferred_element_type=F32)
        z = z_ref[...]
        o_ref[...] = (ox * (z * _sigmoid(z))).astype(BF16)

    def seg(off):
        return pl.BlockSpec((None, XT, XHD), lambda b, h, t: (b, t, off // XHD + h))

    qtab = pl.BlockSpec((XT, XHD), lambda b, h, t: (t, 0))
    mtab = pl.BlockSpec((MLEN, XHD), lambda b, h, t: (0, 0))
    out = pl.pallas_call(
        body, name="cross_fwd", grid=(NB, NXH, S // XT),
        in_specs=[seg(QX), seg(ZX),
                  pl.BlockSpec((None, MLEN, XHD), lambda b, h, t: (b, 0, h)),
                  pl.BlockSpec((None, MLEN, XHD), lambda b, h, t: (b, 0, NXH + h)),
                  qtab, qtab, mtab, mtab, pl.BlockSpec(memory_space=pl.ANY)],
        out_specs=pl.BlockSpec((None, XT, XHD), lambda b, h, t: (b, t, (AW + CW) // XHD + h)),
        out_shape=jax.ShapeDtypeStruct((NB, S, D), BF16),
        input_output_aliases={8: 0},
        compiler_params=_cp(("parallel",) * 3),
    )(proj_v, proj_v, mkv_v, mkv_v, cosq, sinq, cosm, sinm, cat_v)
    return out.reshape(T, D)


def _cross_bwd(dcat, proj, mkv, cosq, sinq, cosm, sinm, dproj):
    proj_v = proj.reshape(NB, S, NC)
    dproj_v = dproj.reshape(NB, S, NC)
    dcat_v = dcat.reshape(NB, S, D)
    mkv_v = mkv.reshape(NB, MLEN, 2 * XW)
    nt = S // XT

    def body(dy_ref, q_ref, z_ref, mk_ref, mv_ref, cq, sq, cm, sm, dp_in, o_ref, dmk_ref, dmv_ref, dz_s):
        del dp_in
        t = pl.program_id(2)
        w = pl.program_id(3)

        @pl.when((t == 0) & (w == 0))
        def _():
            dmk_ref[...] = jnp.zeros_like(dmk_ref)
            dmv_ref[...] = jnp.zeros_like(dmv_ref)

        @pl.when(w == 0)
        def _():
            mkr = _rope(mk_ref[...], cm[...], sm[...], XHD // 2).astype(BF16)
            mv_b = mv_ref[...].astype(BF16)
            qr = _rope(q_ref[...], cq[...], sq[...], XHD // 2).astype(BF16)
            sc = lax.dot_general(qr, mkr, NT, preferred_element_type=F32) * X_SCALE
            p = jnp.exp(sc - jnp.max(sc, axis=1, keepdims=True))
            p = p / jnp.sum(p, axis=1, keepdims=True)
            p_b = p.astype(BF16)
            ox = jnp.dot(p_b, mv_b, preferred_element_type=F32)
            z = z_ref[...]
            dy = dy_ref[...]
            sg = _sigmoid(z)
            dz_s[...] = dy * ox * (sg * (1.0 + z * (1.0 - sg)))
            dox_b = (dy * (z * sg)).astype(BF16)
            dp = lax.dot_general(dox_b, mv_b, NT, preferred_element_type=F32)
            ds_b = (p * (dp - jnp.sum(dp * p, axis=1, keepdims=True)) * X_SCALE).astype(BF16)
            dq = jnp.dot(ds_b, mkr, preferred_element_type=F32)
            o_ref[...] = _rope_t(dq, cq[...], sq[...], XHD // 2).astype(BF16)
            dmk_ref[...] += lax.dot_general(ds_b, qr, TN, preferred_element_type=F32)
            dmv_ref[...] += lax.dot_general(p_b, dox_b, TN, preferred_element_type=F32)

        @pl.when(w == 1)
        def _():
            o_ref[...] = dz_s[...].astype(BF16)

        @pl.when((t == nt - 1) & (w == 1))
        def _():
            dmk_ref[...] = _rope_t(dmk_ref[...], cm[...], sm[...], XHD // 2)

    def seg(off):
        return pl.BlockSpec((None, XT, XHD), lambda b, h, t, w: (b, t, off // XHD + h))

    qtab = pl.BlockSpec((XT, XHD), lambda b, h, t, w: (t, 0))
    mtab = pl.BlockSpec((MLEN, XHD), lambda b, h, t, w: (0, 0))
    macc = pl.BlockSpec((None, MLEN, XHD), lambda b, h, t, w: (b, 0, h))
    out, dmk, dmv = pl.pallas_call(
        body, name="cross_bwd", grid=(NB, NXH, nt, 2),
        in_specs=[pl.BlockSpec((None, XT, XHD), lambda b, h, t, w: (b, t, (AW + CW) // XHD + h)),
                  seg(QX), seg(ZX),
                  pl.BlockSpec((None, MLEN, XHD), lambda b, h, t, w: (b, 0, h)),
                  pl.BlockSpec((None, MLEN, XHD), lambda b, h, t, w: (b, 0, NXH + h)),
                  qtab, qtab, mtab, mtab, pl.BlockSpec(memory_space=pl.ANY)],
        out_specs=[pl.BlockSpec((None, XT, XHD), lambda b, h, t, w: (b, t, (QX + w * XW) // XHD + h)), macc, macc],
        out_shape=[jax.ShapeDtypeStruct((NB, S, NC), BF16), jax.ShapeDtypeStruct((NB, MLEN, XW), F32),
                   jax.ShapeDtypeStruct((NB, MLEN, XW), F32)],
        input_output_aliases={9: 0},
        scratch_shapes=[pltpu.VMEM((XT, XHD), F32)],
        compiler_params=_cp(("arbitrary",) * 4),
    )(dcat_v, proj_v, proj_v, mkv_v, mkv_v, cosq, sinq, cosm, sinm, dproj_v)
    return out.reshape(T, NC), dmk, dmv


def _local_step(x, mem, pre_norm, conv_w, mem_norm, post_norm, tgt, chip, comm):
    x2 = x.reshape(T, D)
    mem2 = mem.reshape(NB * MLEN, D)
    tgt2 = tgt.reshape(T, D)
    cosa, sina = _rope_tables(jnp.arange(S), HD // 2)
    cosq, sinq = _rope_tables(jnp.arange(S) + MLEN, XHD // 2)
    cosm, sinm = _rope_tables(jnp.arange(MLEN), XHD // 2)

    h = _rms_fwd(x2, pre_norm, "pre_norm_fwd", dep=comm.gather_started())
    memn = _rms_fwd(mem2, mem_norm, "mem_norm_fwd")
    proj = _proj_part(h, comm.w_in_own(), chip, (0,), "proj_own")
    wg_in, tok = comm.w_in_near(after=proj)
    proj = _proj_part(h, wg_in, chip, (REL_XOR[0], REL_XOR[1]), "proj_near", prev=proj, dep=tok)
    wg_in, tok = comm.w_in_all(after=proj)
    proj = _proj_part(h, wg_in, chip, (REL_XOR[2],), "proj_far", prev=proj, dep=tok)
    wg_kv, wg_out = comm.w_rest(after=proj)
    mkv = _mm_nn(memn, wg_kv, "mkv", NB * MLEN, 1024, 512)

    fw = [_attn_fwd(proj, cosa, sina, g) for g in range(3)]
    os_ = [f[0] for f in fw]
    ls_ = [f[1] for f in fw]
    cat = _attn_mix(proj, os_, ls_)
    cat = _conv_fwd(proj, conv_w, cat)
    cat = _cross_fwd(proj, mkv, cosq, sinq, cosm, sinm, cat)
    y = _mm_nn(cat, wg_out, "out_proj", 1024, 1024, 2048)
    dy, dout, d_post, loss = _post_norm_loss(y, x2, tgt2, post_norm)

    dcat = _mm_nt(dy, wg_out, "dcat", 1024, 1024, 2048)
    g_out = _mm_tn(cat, dy, "grad_w_out", 1024, 1024, 2048)
    red_a = comm.reduce_start([g_out.reshape(NCHIP, D // NCHIP, D)], "a")
    da, lse, delta, dproj = _attn_mix_bwd(dcat, proj, os_, ls_, dep=red_a[-1])
    for g in range(3):
        dproj = _attn_bwd(proj, cosa, sina, da, lse, delta, dproj, g)
    dproj, d_conv = _conv_bwd(dcat, proj, conv_w, dproj)
    dproj, dmk, dmv = _cross_bwd(dcat, proj, mkv, cosq, sinq, cosm, sinm, dproj)

    dmkv = jnp.concatenate([dmk, dmv], axis=-1).reshape(NB * MLEN, 2 * XW)
    dmkv_b = dmkv.astype(BF16)
    g_kv = _mm_tn(memn, dmkv_b, "grad_w_mem_kv", 1024, 1024, NB * MLEN)
    dmemn = _mm_nt(dmkv_b, wg_kv, "dmemn", NB * MLEN, 1024, 512)
    d_mem = _norm_gain_grad(dmemn, mem2, "mem_norm_bwd")

    g_in = _grad_w_in(h, dproj)
    red_b = comm.reduce_start([g_in.reshape(NCHIP, D, WB), g_kv.reshape(NCHIP, D // NCHIP, 2 * XW)], "b")
    dh = _dh(dproj, wg_in, dep=red_b[-1])
    grad_x, d_pre = _pre_norm_bwd(dh, x2, pre_norm, dout)
    (r_out,) = comm.reduce_finish(red_a, "a", after=grad_x)
    r_in, r_kv = comm.reduce_finish(red_b, "b", after=grad_x)
    return loss, grad_x.reshape(NB, S, D), d_pre, d_mem, d_post, d_conv, r_in, r_kv, r_out


def _adamw(w, g, m, v, name):
    rows, cols = w.shape
    tr = rows if rows <= 512 else 512
    tc = cols if cols <= 1024 else 1024
    if cols % tc:
        tc = 896

    def body(w_ref, g_ref, m_ref, v_ref, d_ref, nm_ref, nv_ref):
        gv = g_ref[...]
        nm = ADAM_B1 * m_ref[...] + (1.0 - ADAM_B1) * gv
        nv = ADAM_B2 * v_ref[...] + (1.0 - ADAM_B2) * (gv * gv)
        m_hat = nm / (1.0 - ADAM_B1 ** ADAM_STEP)
        v_hat = nv / (1.0 - ADAM_B2 ** ADAM_STEP)
        d_ref[...] = -ADAM_LR * (m_hat / (jnp.sqrt(v_hat) + ADAM_EPS) + ADAM_WD * w_ref[...])
        nm_ref[...] = nm
        nv_ref[...] = nv

    blk = pl.BlockSpec((tr, tc), lambda i, j: (i, j))
    sds = jax.ShapeDtypeStruct((rows, cols), F32)
    return pl.pallas_call(
        body, name=name, grid=(rows // tr, cols // tc),
        in_specs=[blk] * 4, out_specs=[blk] * 3, out_shape=[sds] * 3,
        compiler_params=_cp(("parallel", "parallel")),
    )(w, g, m, v)


def _place():
    return lax.axis_index("x"), lax.axis_index("y"), lax.axis_index("c")


def _other_chips(x, y):
    return [(1 - x, y), (x, 1 - y), (1 - x, 1 - y)]


def _tile_cols(cols):
    return cols if cols <= 1024 else (1024 if cols % 1024 == 0 else 896)


def _cast_own(w, chip, name):
    rows, cols = w.shape
    tr, tc = 512, _tile_cols(cols)

    def body(chip_ref, w_ref, o_ref):
        del chip_ref
        o_ref[...] = w_ref[...].astype(BF16)

    grid_spec = pltpu.PrefetchScalarGridSpec(
        num_scalar_prefetch=1, grid=(rows // tr, cols // tc),
        in_specs=[pl.BlockSpec((tr, tc), lambda i, j, cr: (i, j))],
        out_specs=pl.BlockSpec((None, tr, tc), lambda i, j, cr: (cr[0], i, j)))
    return pl.pallas_call(
        body, name=name, grid_spec=grid_spec,
        out_shape=jax.ShapeDtypeStruct((NCHIP, rows, cols), BF16),
        compiler_params=_cp(("parallel", "parallel")),
    )(_as_index(chip), w)


HBM_SPEC = pl.BlockSpec(memory_space=pltpu.HBM)
SEM_SPEC = pl.BlockSpec(memory_space=pltpu.SEMAPHORE)
ANY_SPEC = pl.BlockSpec(memory_space=pl.ANY)
EFFECT = pltpu.SideEffectType.DATAFLOW_SIDE_EFFECTING
TOKEN = jax.ShapeDtypeStruct((8, 128), F32)


def _half(ref, chip, hc):
    hr = ref.shape[1] // 2
    return ref.at[chip, pl.ds(hc * hr, hr), :]


NEAR = (0, 1)
FAR = (2,)
REL_XOR = (2, 1, 3)


def _gather_copies(refs, send_sems, recv_sems, rels):
    x, y, c = _place()
    chips = _other_chips(x, y)
    out, inc = [], []
    for a, ref in enumerate(refs):
        for p, j in enumerate(rels):
            px, py = chips[j]
            mine = _half(ref, 2 * x + y, c)
            theirs = _half(ref, 2 * px + py, c)
            sems = dict(send_sem=send_sems.at[len(rels) * a + p], recv_sem=recv_sems.at[len(rels) * a + p],
                        device_id=(px, py, c), device_id_type=MESH)
            out.append(pltpu.make_async_remote_copy(src_ref=mine, dst_ref=mine, **sems))
            inc.append(pltpu.make_async_remote_copy(src_ref=theirs, dst_ref=theirs, **sems))
    return out, inc


def _gather_start(bufs, groups, name):
    n = len(bufs)
    ng = len(groups)

    def body(*refs):
        ins = refs[:n]
        token = refs[-1]
        for gi, rels in enumerate(groups):
            out, _ = _gather_copies(ins, refs[n + 2 * gi], refs[n + 2 * gi + 1], rels)
            for cp in out:
                cp.start()
        token[...] = jnp.zeros_like(token)

    sems = []
    for rels in groups:
        sems += [pltpu.SemaphoreType.DMA((len(rels) * n,))] * 2
    res = pl.pallas_call(
        body, name=name,
        in_specs=[HBM_SPEC] * n,
        out_specs=[SEM_SPEC] * (2 * ng) + [HBM_SPEC] * n + [pl.BlockSpec(memory_space=pltpu.VMEM)],
        out_shape=sems + [pltpu.HBM(b.shape, b.dtype) for b in bufs] + [TOKEN],
        input_output_aliases={a: 2 * ng + a for a in range(n)},
        compiler_params=pltpu.CompilerParams(has_side_effects=EFFECT),
    )(*[pltpu.with_memory_space_constraint(b, pltpu.HBM) for b in bufs])
    return [(res[2 * gi], res[2 * gi + 1]) for gi in range(ng)], list(res[2 * ng:2 * ng + n]), res[-1]


def _gather_wait(bufs, sems, rels, after, name):
    n = len(bufs)
    send_sems, recv_sems = sems

    def body(*refs):
        ins = refs[:n]
        out, inc = _gather_copies(ins, refs[n], refs[n + 1], rels)
        for cp in out:
            cp.wait_send()
        for cp in inc:
            cp.wait_recv()

    return pl.pallas_call(
        body, name=name,
        in_specs=[HBM_SPEC] * n + [SEM_SPEC, SEM_SPEC, ANY_SPEC],
        out_specs=[HBM_SPEC] * n,
        out_shape=[pltpu.HBM(b.shape, b.dtype) for b in bufs],
        input_output_aliases={a: a for a in range(n)},
        compiler_params=pltpu.CompilerParams(has_side_effects=EFFECT),
    )(*bufs, send_sems, recv_sems, after)


def _forward_halves(bufs, rels, name):
    n = len(bufs)

    def body(*refs):
        outs = refs[n:2 * n]
        send_sems, recv_sems = refs[2 * n:]
        x, y, c = _place()
        chips = _other_chips(x, y)
        cps, waits = [], []
        for a in range(n):
            for p, j in enumerate(rels):
                px, py = chips[j]
                sems = dict(send_sem=send_sems.at[len(rels) * a + p], recv_sem=recv_sems.at[len(rels) * a + p],
                            device_id=(x, y, 1 - c), device_id_type=MESH)
                got = _half(outs[a], 2 * px + py, c)
                want = _half(outs[a], 2 * px + py, 1 - c)
                cps.append(pltpu.make_async_remote_copy(src_ref=got, dst_ref=got, **sems))
                waits.append(pltpu.make_async_remote_copy(src_ref=want, dst_ref=want, **sems))
        for cp in cps:
            cp.start()
        for cp in waits:
            cp.wait_recv()
        for cp in cps:
            cp.wait_send()

    return pl.pallas_call(
        body, name=name,
        in_specs=[ANY_SPEC] * n, out_specs=[ANY_SPEC] * n,
        out_shape=[jax.ShapeDtypeStruct(s.shape, s.dtype) for s in bufs],
        input_output_aliases={a: a for a in range(n)},
        scratch_shapes=[pltpu.SemaphoreType.DMA((len(rels) * n,)), pltpu.SemaphoreType.DMA((len(rels) * n,))],
    )(*bufs)


def _sibling_halves(parts, name):
    n = len(parts)

    def body(*refs):
        ins, outs = refs[:n], refs[n:2 * n]
        send_sems, recv_sems = refs[2 * n:]
        x, y, c = _place()
        cps = []
        for a in range(n):
            hr = ins[a].shape[1] // 2
            cps.append(pltpu.make_async_remote_copy(
                src_ref=ins[a].at[:, pl.ds((1 - c) * hr, hr), :], dst_ref=outs[a],
                send_sem=send_sems.at[a], recv_sem=recv_sems.at[a], device_id=(x, y, 1 - c), device_id_type=MESH))
        for cp in cps:
            cp.start()
        for cp in cps:
            cp.wait()

    hbm = pl.BlockSpec(memory_space=pl.ANY)
    return pl.pallas_call(
        body, name=name,
        in_specs=[hbm] * n, out_specs=[hbm] * n,
        out_shape=[jax.ShapeDtypeStruct((NCHIP, p.shape[1] // 2, p.shape[2]), p.dtype) for p in parts],
        scratch_shapes=[pltpu.SemaphoreType.DMA((n,)), pltpu.SemaphoreType.DMA((n,))],
    )(*parts)


def _pair_sum(part, got, name):
    nblk, rows, cols = part.shape
    hr = rows // 2
    tr = 512
    tc = cols if cols <= 1024 else (1024 if cols % 1024 == 0 else 896)
    c = lax.axis_index("c")
    nrt = hr // tr

    def body(c_ref, p_ref, g_ref, o_ref):
        del c_ref
        o_ref[...] = (p_ref[...] + g_ref[...]).astype(BF16)

    grid_spec = pltpu.PrefetchScalarGridSpec(
        num_scalar_prefetch=1, grid=(nblk, nrt, cols // tc),
        in_specs=[pl.BlockSpec((None, tr, tc), lambda b, i, j, cr: (b, cr[0] * nrt + i, j)),
                  pl.BlockSpec((None, tr, tc), lambda b, i, j, cr: (b, i, j))],
        out_specs=pl.BlockSpec((None, tr, tc), lambda b, i, j, cr: (b, i, j)))
    return pl.pallas_call(
        body, name=name, grid_spec=grid_spec,
        out_shape=jax.ShapeDtypeStruct((nblk, hr, cols), BF16),
        compiler_params=_cp(("parallel",) * 3),
    )(jnp.reshape(c, (1,)).astype(jnp.int32), part, got)


def _scatter_start(qs, name):
    n = len(qs)

    def body(*refs):
        ins, lands = refs[:n], refs[n:2 * n]
        token = refs[-1]
        for cp in _scatter_copies(ins, lands, refs[2 * n], refs[2 * n + 1]):
            cp.start()
        token[...] = jnp.zeros_like(token)

    lands = [lax.empty((3,) + q.shape[1:], q.dtype) for q in qs]
    res = pl.pallas_call(
        body, name=name,
        in_specs=[HBM_SPEC] * (2 * n),
        out_specs=[SEM_SPEC, SEM_SPEC] + [HBM_SPEC] * (2 * n) + [pl.BlockSpec(memory_space=pltpu.VMEM)],
        out_shape=[pltpu.SemaphoreType.DMA((3 * n,)), pltpu.SemaphoreType.DMA((3 * n,))]
        + [pltpu.HBM(b.shape, b.dtype) for b in qs + lands] + [TOKEN],
        input_output_aliases={a: 2 + a for a in range(2 * n)},
        compiler_params=pltpu.CompilerParams(has_side_effects=EFFECT),
    )(*[pltpu.with_memory_space_constraint(b, pltpu.HBM) for b in qs + lands])
    return res[0], res[1], list(res[2:2 + n]), list(res[2 + n:2 + 2 * n]), res[-1]


def _scatter_copies(ins, lands, send_sems, recv_sems):
    x, y, c = _place()
    cps = []
    for a in range(len(ins)):
        for j, (px, py) in enumerate(_other_chips(x, y)):
            cps.append(pltpu.make_async_remote_copy(
                src_ref=ins[a].at[2 * px + py], dst_ref=lands[a].at[j],
                send_sem=send_sems.at[3 * a + j], recv_sem=recv_sems.at[3 * a + j], device_id=(px, py, c), device_id_type=MESH))
    return cps


def _scatter_wait(qs, lands, send_sems, recv_sems, after, name):
    n = len(qs)

    def body(*refs):
        for cp in _scatter_copies(refs[:n], refs[n:2 * n], refs[2 * n], refs[2 * n + 1]):
            cp.wait_send()
            cp.wait_recv()

    res = pl.pallas_call(
        body, name=name,
        in_specs=[HBM_SPEC] * (2 * n) + [SEM_SPEC, SEM_SPEC, ANY_SPEC],
        out_specs=[HBM_SPEC] * (2 * n),
        out_shape=[pltpu.HBM(b.shape, b.dtype) for b in qs + lands],
        input_output_aliases={a: a for a in range(2 * n)},
        compiler_params=pltpu.CompilerParams(has_side_effects=EFFECT),
    )(*qs, *lands, send_sems, recv_sems, after)
    return list(res[:n]), list(res[n:])


def _chip_sum(q, got, name):
    _, hr, cols = got.shape
    tr, tc = 512, _tile_cols(cols)
    chip = 2 * lax.axis_index("x") + lax.axis_index("y")
    c = lax.axis_index("c")

    def body(idx_ref, q_ref, g_ref, o_ref):
        del idx_ref
        acc = q_ref[...].astype(F32)
        for i in range(3):
            acc = acc + g_ref[i].astype(F32)
        o_ref[...] = acc

    grid_spec = pltpu.PrefetchScalarGridSpec(
        num_scalar_prefetch=1, grid=(hr // tr, cols // tc),
        in_specs=[pl.BlockSpec((None, tr, tc), lambda i, j, ix: (ix[0], i, j)),
                  pl.BlockSpec((3, tr, tc), lambda i, j, ix: (0, i, j))],
        out_specs=pl.BlockSpec((None, tr, tc), lambda i, j, ix: (ix[1], i, j)))
    return pl.pallas_call(
        body, name=name, grid_spec=grid_spec,
        out_shape=jax.ShapeDtypeStruct((2, hr, cols), F32),
        compiler_params=_cp(("parallel", "parallel")),
    )(jnp.stack([chip, c]).astype(jnp.int32), q, got)


def _join_halves(bufs, name):
    n = len(bufs)

    def body(*refs):
        outs = refs[n:2 * n]
        send_sems, recv_sems = refs[2 * n:]
        x, y, c = _place()
        cps = []
        for a in range(n):
            cps.append(pltpu.make_async_remote_copy(src_ref=outs[a].at[c], dst_ref=outs[a].at[c], send_sem=send_sems.at[a],
                                                    recv_sem=recv_sems.at[a], device_id=(x, y, 1 - c), device_id_type=MESH))
        for cp in cps:
            cp.start()
        for a in range(n):
            theirs = outs[a].at[1 - c]
            pltpu.make_async_remote_copy(src_ref=theirs, dst_ref=theirs, send_sem=send_sems.at[a], recv_sem=recv_sems.at[a],
                                         device_id=(x, y, 1 - c), device_id_type=MESH).wait_recv()
        for cp in cps:
            cp.wait_send()

    hbm = pl.BlockSpec(memory_space=pl.ANY)
    return pl.pallas_call(
        body, name=name,
        in_specs=[hbm] * n, out_specs=[hbm] * n,
        out_shape=[jax.ShapeDtypeStruct(b.shape, b.dtype) for b in bufs],
        input_output_aliases={a: a for a in range(n)},
        scratch_shapes=[pltpu.SemaphoreType.DMA((n,)), pltpu.SemaphoreType.DMA((n,))],
    )(*bufs)


N_DEV = 8


def _gather_small(block, reduce, name):
    m_per, cols = block.shape

    def body(x_ref, out_ref, all_ref, send_sems, recv_sems, local_sem):
        x, y, c = _place()
        me, sibling = (x, y, c), (x, y, 1 - c)
        chips = _other_chips(x, y)

        def rows(px, py, pc):
            return all_ref.at[pl.ds((4 * px + 2 * py + pc) * m_per, m_per), :]

        def copy(k, block_of, to, src=None):
            return pltpu.make_async_remote_copy(
                src_ref=rows(*block_of) if src is None else src, dst_ref=rows(*block_of),
                send_sem=send_sems.at[k], recv_sem=recv_sems.at[k], device_id=to, device_id_type=MESH)

        mine = pltpu.make_async_copy(x_ref, rows(*me), local_sem)
        mine.start()
        first = [copy(0, me, sibling, src=x_ref)]
        first += [copy(1 + j, me, (*chip, c), src=x_ref) for j, chip in enumerate(chips)]
        for cp in first:
            cp.start()
        passed = [copy(4 + j, (*chip, c), sibling) for j, chip in enumerate(chips)]
        for j, chip in enumerate(chips):
            copy(1 + j, (*chip, c), me).wait_recv()
            passed[j].start()
        copy(0, sibling, me).wait_recv()
        for j, chip in enumerate(chips):
            copy(4 + j, (*chip, 1 - c), me).wait_recv()
        for cp in first + passed:
            cp.wait_send()
        mine.wait()
        if reduce:
            acc = all_ref[pl.ds(0, m_per), :]
            for i in range(1, N_DEV):
                acc = acc + all_ref[pl.ds(i * m_per, m_per), :]
            out_ref[...] = acc
        else:
            out_ref[...] = all_ref[...]

    out_rows = m_per if reduce else N_DEV * m_per
    return pl.pallas_call(
        body, name=name,
        in_specs=[pl.BlockSpec(memory_space=pltpu.VMEM)],
        out_specs=pl.BlockSpec(memory_space=pltpu.VMEM),
        out_shape=jax.ShapeDtypeStruct((out_rows, cols), F32),
        scratch_shapes=[pltpu.VMEM((N_DEV * m_per, cols), F32), pltpu.SemaphoreType.DMA((7,)),
                        pltpu.SemaphoreType.DMA((7,)), pltpu.SemaphoreType.DMA],
    )(block)


class _Comm:
    def __init__(self, w_in, w_kv, w_out, chip):
        self.bufs = [_cast_own(w_in, chip, "cast_w_in"), _cast_own(w_kv, chip, "cast_w_mem_kv"),
                     _cast_own(w_out, chip, "cast_w_out")]


    def gather_started(self):
        (self.sems,), (self.b_in,), tok = _gather_start(self.bufs[:1], (NEAR,), "gather_start_in_near")
        return [tok]

    def w_in_own(self):
        return self.b_in.reshape(NCHIP * D, WB)

    def w_in_near(self, after):
        (b_in,) = _gather_wait([self.b_in], self.sems, NEAR, after, "gather_wait_in_near")
        (b_in,) = _forward_halves([b_in], NEAR, "forward_in_near")
        (self.sems,), (self.b_in,), tok = _gather_start([b_in], (FAR,), "gather_start_in_far")
        return self.b_in.reshape(NCHIP * D, WB), tok

    def w_in_all(self, after):
        (b_in,) = _gather_wait([self.b_in], self.sems, FAR, after, "gather_wait_in_far")
        (b_in,) = _forward_halves([b_in], FAR, "forward_in_far")
        (self.sems,), self.b_rest, tok = _gather_start(self.bufs[1:], (NEAR + FAR,), "gather_start_rest")
        return b_in.reshape(NCHIP * D, WB), tok

    def w_rest(self, after):
        b_rest = _gather_wait(self.b_rest, self.sems, NEAR + FAR, after, "gather_wait_rest")
        b_kv, b_out = _forward_halves(b_rest, NEAR + FAR, "forward_rest")
        return b_kv.reshape(D, 2 * XW), b_out.reshape(D, D)

    def reduce_start(self, parts, tag):
        got = _sibling_halves(parts, "sibling_halves_" + tag)
        qs = [_pair_sum(p, g, f"pair_sum_{tag}{i}") for i, (p, g) in enumerate(zip(parts, got))]
        return _scatter_start(qs, "scatter_start_" + tag)

    def reduce_finish(self, state, tag, after):
        send_sems, recv_sems, qs, lands, _ = state
        qs, lands = _scatter_wait(qs, lands, send_sems, recv_sems, after, "scatter_wait_" + tag)
        halves = [_chip_sum(q, l, f"chip_sum_{tag}{i}") for i, (q, l) in enumerate(zip(qs, lands))]
        return [j.reshape(2 * j.shape[1], j.shape[2]) for j in _join_halves(halves, "join_halves_" + tag)]


def kernel(x, mem, pre_norm, w_in, conv_w, mem_norm, w_mem_kv, w_out, post_norm, loss_target, m_pre_norm, m_w_in, m_conv_w, m_mem_norm, m_w_mem_kv, m_w_out, m_post_norm, v_pre_norm, v_w_in, v_conv_w, v_mem_norm, v_w_mem_kv, v_w_out, v_post_norm):
    chip = 2 * lax.axis_index("x") + lax.axis_index("y")

    cw_blk = jnp.zeros((8, 384), F32).at[:3].set(conv_w[0])
    cw_all = _gather_small(cw_blk, False, "gather_conv_w").reshape(NCHIP, 2, 8, 384)[:, 0, :3]
    conv_full = jnp.transpose(cw_all, (1, 0, 2)).reshape(3, CW)

    comm = _Comm(w_in[0], w_mem_kv[0], w_out[0], chip)
    loss, grad_x, d_pre, d_mem, d_post, d_conv, r_in, r_kv, r_out = _local_step(
        x, mem, pre_norm, conv_full, mem_norm, post_norm, loss_target, chip, comm)

    pack = jnp.concatenate([d_pre, d_mem, d_post, jnp.pad(d_conv, ((0, 0), (0, D - CW))),
                            jnp.pad(loss, ((0, 0), (0, D - 128))), jnp.zeros((1, D), F32)], axis=0)
    tot = _gather_small(pack, True, "reduce_small")
    g_pre, g_mem, g_post = tot[0:1], tot[1:2], tot[2:3]
    g_conv = lax.dynamic_slice(tot[3:6, :CW], (0, chip * 384), (3, 384))
    loss_out = tot[6, 0]

    names = ("pre_norm", "w_in", "conv_w", "mem_norm", "w_mem_kv", "w_out", "post_norm")
    ws = (pre_norm, w_in[0], conv_w[0], mem_norm, w_mem_kv[0], w_out[0], post_norm)
    gs = (g_pre, r_in, g_conv, g_mem, r_kv, r_out, g_post)
    ms = (m_pre_norm, m_w_in[0], m_conv_w[0], m_mem_norm, m_w_mem_kv[0], m_w_out[0], m_post_norm)
    vs = (v_pre_norm, v_w_in[0], v_conv_w[0], v_mem_norm, v_w_mem_kv[0], v_w_out[0], v_post_norm)
    upd = [_adamw(w, g, m, v, "adamw_" + nm) for nm, w, g, m, v in zip(names, ws, gs, ms, vs)]

    def shaped(arrs):
        return [a.reshape(w.shape) if w.ndim == a.ndim else a.reshape((1,) + a.shape)
                for a, w in zip(arrs, (pre_norm, w_in, conv_w, mem_norm, w_mem_kv, w_out, post_norm))]

    grads = shaped(gs)
    deltas = shaped([u[0] for u in upd])
    new_m = shaped([u[1] for u in upd])
    new_v = shaped([u[2] for u in upd])
    return (loss_out, grad_x, *grads, *deltas, *new_m, *new_v)
```

```python
import functools

import jax
import jax.numpy as jnp
from jax import lax
from jax.experimental import pallas as pl
from jax.experimental.pallas import tpu as pltpu

F32 = jnp.float32
BF16 = jnp.bfloat16

D = 4096
S = 2048
NB = 2
T = NB * S
MLEN = 256
HD = 128
AW = 1536
CW = 1536
XW = 1024
XHD = 256
NXH = 4
NC = 14336
QA, KA, VA, ZA, UC, BC, CC, ZC, QX, ZX = 0, 1536, 3072, 4608, 6144, 7680, 9216, 10752, 12288, 13312
NCHIP = 4
WB = NC // NCHIP
DIL = (1, 4, 16)
HPG = 4
EPS = 1e-6
NEG = -1e30
ROPE_THETA = 10000.0
A_SCALE = HD ** -0.5
X_SCALE = XHD ** -0.5

ADAM_LR = 0.001
ADAM_B1 = 0.9
ADAM_B2 = 0.999
ADAM_EPS = 1e-08
ADAM_WD = 0.01
ADAM_STEP = 10

MESH = pl.DeviceIdType.MESH
MIB = 1024 * 1024


def _cp(sem, vmem_mib=48):
    return pltpu.CompilerParams(dimension_semantics=sem, vmem_limit_bytes=vmem_mib * MIB)


def _sigmoid(z):
    return 1.0 / (1.0 + jnp.exp(-z))


def _rope(x, cos, sin, half):
    return x * cos + pltpu.roll(x, half, 1) * sin


def _rope_t(g, cos, sin, half):
    return g * cos + pltpu.roll(g * sin, half, 1)


def _rms_fwd(x2, g, name, dep=None):
    rows = x2.shape[0]
    tr = 256
    deps = [] if dep is None else list(dep)

    def body(x_ref, g_ref, *rest):
        o_ref = rest[-1]
        x = x_ref[...]
        r = lax.rsqrt(jnp.mean(x * x, axis=-1, keepdims=True) + EPS)
        o_ref[...] = (x * r * g_ref[...]).astype(BF16)

    return pl.pallas_call(
        body, name=name, grid=(rows // tr,),
        in_specs=[pl.BlockSpec((tr, D), lambda i: (i, 0)), pl.BlockSpec((1, D), lambda i: (0, 0))]
        + [pl.BlockSpec(memory_space=pl.ANY)] * len(deps),
        out_specs=pl.BlockSpec((tr, D), lambda i: (i, 0)),
        out_shape=jax.ShapeDtypeStruct((rows, D), BF16),
        compiler_params=_cp(("parallel",)),
    )(x2, g, *deps)


def _norm_gain_grad(dn, x2, name):
    rows = x2.shape[0]
    tr = 256

    def body(dn_ref, x_ref, dg_ref):
        @pl.when(pl.program_id(0) == 0)
        def _():
            dg_ref[...] = jnp.zeros_like(dg_ref)
        x = x_ref[...]
        r = lax.rsqrt(jnp.mean(x * x, axis=-1, keepdims=True) + EPS)
        dg_ref[...] += jnp.sum(dn_ref[...] * (x * r), axis=0, keepdims=True)

    return pl.pallas_call(
        body, name=name, grid=(rows // tr,),
        in_specs=[pl.BlockSpec((tr, D), lambda i: (i, 0)), pl.BlockSpec((tr, D), lambda i: (i, 0))],
        out_specs=pl.BlockSpec((1, D), lambda i: (0, 0)),
        out_shape=jax.ShapeDtypeStruct((1, D), F32),
        compiler_params=_cp(("arbitrary",)),
    )(dn, x2)


def _pre_norm_bwd(dh, x2, g, dout):
    tr = 256

    def body(dh_ref, x_ref, g_ref, dout_ref, gx_ref, dg_ref):
        @pl.when(pl.program_id(0) == 0)
        def _():
            dg_ref[...] = jnp.zeros_like(dg_ref)
        x = x_ref[...]
        dh_ = dh_ref[...]
        r = lax.rsqrt(jnp.mean(x * x, axis=-1, keepdims=True) + EPS)
        xhat = x * r
        dg_ref[...] += jnp.sum(dh_ * xhat, axis=0, keepdims=True)
        dxn = dh_ * g_ref[...]
        gx_ref[...] = dout_ref[...] + r * (dxn - xhat * jnp.mean(dxn * xhat, axis=-1, keepdims=True))

    row = pl.BlockSpec((tr, D), lambda i: (i, 0))
    vec = pl.BlockSpec((1, D), lambda i: (0, 0))
    return pl.pallas_call(
        body, name="pre_norm_bwd", grid=(T // tr,),
        in_specs=[row, row, vec, row],
        out_specs=[row, vec],
        out_shape=[jax.ShapeDtypeStruct((T, D), F32), jax.ShapeDtypeStruct((1, D), F32)],
        compiler_params=_cp(("arbitrary",)),
    )(dh, x2, g, dout)


def _post_norm_loss(y, x2, tgt, g):
    tr = 256

    def body(y_ref, x_ref, t_ref, g_ref, dy_ref, dout_ref, dg_ref, loss_ref):
        @pl.when(pl.program_id(0) == 0)
        def _():
            dg_ref[...] = jnp.zeros_like(dg_ref)
            loss_ref[...] = jnp.zeros_like(loss_ref)
        yv = y_ref[...]
        gv = g_ref[...]
        r = lax.rsqrt(jnp.mean(yv * yv, axis=-1, keepdims=True) + EPS)
        yhat = yv * r
        err = x_ref[...] + yhat * gv - t_ref[...]
        loss_ref[...] += jnp.sum(jnp.sum(err * err, axis=1, keepdims=True), axis=0, keepdims=True) * (0.5 / D)
        dout = err * (1.0 / D)
        dout_ref[...] = dout
        dg_ref[...] += jnp.sum(dout * yhat, axis=0, keepdims=True)
        dyn = dout * gv
        dy_ref[...] = (r * (dyn - yhat * jnp.mean(dyn * yhat, axis=-1, keepdims=True))).astype(BF16)

    row = pl.BlockSpec((tr, D), lambda i: (i, 0))
    vec = pl.BlockSpec((1, D), lambda i: (0, 0))
    return pl.pallas_call(
        body, name="post_norm_loss", grid=(T // tr,),
        in_specs=[row, row, row, vec],
        out_specs=[row, row, vec, pl.BlockSpec((1, 128), lambda i: (0, 0))],
        out_shape=[jax.ShapeDtypeStruct((T, D), BF16), jax.ShapeDtypeStruct((T, D), F32),
                   jax.ShapeDtypeStruct((1, D), F32), jax.ShapeDtypeStruct((1, 128), F32)],
        compiler_params=_cp(("arbitrary",)),
    )(y, x2, tgt, g)


NN = (((1,), (0,)), ((), ()))
NT = (((1,), (1,)), ((), ()))
TN = (((0,), (0,)), ((), ()))


def _as_index(v):
    return jnp.reshape(v, (1,)).astype(jnp.int32)


def _matmul(a, b, *, name, dims, grid, a_block, a_map, b_block, b_map, o_block, o_map, out_shape, out_dtype=F32,
            index=None, prev=None, deps=()):
    assert out_dtype == F32
    extra = ([] if prev is None else [prev]) + [d for d in deps if d is not None]
    first = 0 if index is None else 1

    def body(*refs):
        a_ref, b_ref, o_ref = refs[first], refs[first + 1], refs[first + 2 + len(extra)]
        @pl.when(pl.program_id(2) == 0)
        def _():
            o_ref[...] = jnp.zeros_like(o_ref)

        o_ref[...] += lax.dot_general(a_ref[...], b_ref[...], dims, preferred_element_type=F32)

    in_specs = [pl.BlockSpec(a_block, a_map), pl.BlockSpec(b_block, b_map)] + [pl.BlockSpec(memory_space=pl.ANY)] * len(extra)
    out_specs = pl.BlockSpec(o_block, o_map)
    kwargs = dict(name=name, out_shape=jax.ShapeDtypeStruct(out_shape, out_dtype),
                  input_output_aliases={} if prev is None else {first + 2: 0},
                  compiler_params=_cp(("parallel", "parallel", "arbitrary"), vmem_mib=56))
    if index is None:
        return pl.pallas_call(body, grid=grid, in_specs=in_specs, out_specs=out_specs, **kwargs)(a, b, *extra)
    grid_spec = pltpu.PrefetchScalarGridSpec(num_scalar_prefetch=1, grid=grid, in_specs=in_specs, out_specs=out_specs)
    return pl.pallas_call(body, grid_spec=grid_spec, **kwargs)(_as_index(index), a, b, *extra)


def _mm_nn(a, b, name, tm, tn, tk):
    m, kd = a.shape
    n = b.shape[1]
    return _matmul(a, b, name=name, dims=NN, grid=(m // tm, n // tn, kd // tk),
                   a_block=(tm, tk), a_map=lambda i, j, k: (i, k),
                   b_block=(tk, tn), b_map=lambda i, j, k: (k, j),
                   o_block=(tm, tn), o_map=lambda i, j, k: (i, j), out_shape=(m, n))


def _mm_nt(a, b, name, tm, tn, tk):
    m, kd = a.shape
    n = b.shape[0]
    return _matmul(a, b, name=name, dims=NT, grid=(m // tm, n // tn, kd // tk),
                   a_block=(tm, tk), a_map=lambda i, j, k: (i, k),
                   b_block=(tn, tk), b_map=lambda i, j, k: (j, k),
                   o_block=(tm, tn), o_map=lambda i, j, k: (i, j), out_shape=(m, n))


def _mm_tn(a, b, name, tm, tn, tk):
    kd, m = a.shape
    n = b.shape[1]
    return _matmul(a, b, name=name, dims=TN, grid=(m // tm, n // tn, kd // tk),
                   a_block=(tk, tm), a_map=lambda i, j, k: (k, i),
                   b_block=(tk, tn), b_map=lambda i, j, k: (k, j),
                   o_block=(tm, tn), o_map=lambda i, j, k: (i, j), out_shape=(m, n))


W_TN = 1792
W_NJ = WB // W_TN


def _proj_part(h, wg, chip, masks, name, prev=None, dep=None):
    tm, tk = 1024, 2048

    def blk(j, ix):
        m = masks[0]
        for t in range(1, len(masks)):
            m = jnp.where(j // W_NJ == t, masks[t], m)
        return jnp.bitwise_xor(ix[0], m)

    return _matmul(h, wg, name=name, dims=NN, grid=(T // tm, len(masks) * W_NJ, D // tk), index=chip, prev=prev, deps=(dep,),
                   a_block=(tm, tk), a_map=lambda i, j, k, ix: (i, k),
                   b_block=(tk, W_TN), b_map=lambda i, j, k, ix: (blk(j, ix) * (D // tk) + k, j % W_NJ),
                   o_block=(tm, W_TN), o_map=lambda i, j, k, ix: (i, blk(j, ix) * W_NJ + j % W_NJ), out_shape=(T, NC))


def _dh(dproj, wg, dep=None):
    tm, tn = 1024, 1024
    return _matmul(dproj, wg, name="dh", dims=NT, grid=(T // tm, D // tn, NC // W_TN), deps=(dep,),
                   a_block=(tm, W_TN), a_map=lambda i, j, k: (i, k),
                   b_block=(tn, W_TN), b_map=lambda i, j, k: ((k // W_NJ) * (D // tn) + j, k % W_NJ),
                   o_block=(tm, tn), o_map=lambda i, j, k: (i, j), out_shape=(T, D))


def _grad_w_in(h, dproj):
    tm, tk = 1024, 2048
    return _matmul(h, dproj, name="grad_w_in", dims=TN, grid=(D // tm, NC // W_TN, T // tk),
                   a_block=(tk, tm), a_map=lambda i, j, k: (k, i),
                   b_block=(tk, W_TN), b_map=lambda i, j, k: (k, j),
                   o_block=(tm, W_TN), o_map=lambda i, j, k: ((j // W_NJ) * (D // tm) + i, j % W_NJ),
                   out_shape=(NCHIP * D, WB))


def _rope_tables(pos, half):
    inv = 1.0 / (ROPE_THETA ** (jnp.arange(half, dtype=F32) / half))
    ang = pos.astype(F32)[:, None] * inv[None, :]
    cos, sin = jnp.cos(ang), jnp.sin(ang)
    return jnp.concatenate([cos, cos], axis=1), jnp.concatenate([-sin, sin], axis=1)


def _band_mask(r0):
    qi = lax.broadcasted_iota(jnp.int32, (128, 256), 0)
    kk = lax.broadcasted_iota(jnp.int32, (128, 256), 1)
    return (kk >= qi) & (kk <= qi + 128) & (kk + r0 >= 128)


def _dil_rows(r, n, d):
    if d == 1:
        return pl.ds(pl.multiple_of(n * 128, 128), 128)
    return pl.ds(r + d * 128 * n, 128, stride=d)


def _attn_fwd(proj, cosf, sinf, g):
    d = DIL[g]
    ln = S // d
    nblk = ln // 128
    proj_v = proj.reshape(NB, S, NC)

    def body(q_ref, k_ref, v_ref, cos_ref, sin_ref, o_ref, l_ref, k_s, v_s):
        k_s[:, pl.ds(0, 128), :] = jnp.zeros((d, 128, HD), BF16)
        v_s[:, pl.ds(0, 128), :] = jnp.zeros((d, 128, HD), BF16)

        def prep(i, carry):
            r, n = i // nblk, i % nblk
            rows = _dil_rows(r, n, d)
            dst = pl.ds(pl.multiple_of(n * 128 + 128, 128), 128)
            k_s[r, dst, :] = _rope(k_ref[rows, :], cos_ref[rows, :], sin_ref[rows, :], HD // 2).astype(BF16)
            v_s[r, dst, :] = v_ref[rows, :].astype(BF16)
            return carry

        lax.fori_loop(0, d * nblk, prep, 0, unroll=4)

        def step(i, carry):
            r, n = i // nblk, i % nblk
            rows = _dil_rows(r, n, d)
            r0 = pl.multiple_of(n * 128, 128)
            qr = _rope(q_ref[rows, :], cos_ref[rows, :], sin_ref[rows, :], HD // 2).astype(BF16)
            kw = k_s[r, pl.ds(r0, 256), :]
            vw = v_s[r, pl.ds(r0, 256), :]
            sc = lax.dot_general(qr, kw, NT, preferred_element_type=F32) * A_SCALE
            sc = jnp.where(_band_mask(r0), sc, NEG)
            m = jnp.max(sc, axis=1, keepdims=True)
            p = jnp.exp(sc - m)
            l = jnp.sum(p, axis=1, keepdims=True)
            o_ref[rows, :] = jnp.dot(p.astype(BF16), vw, preferred_element_type=F32) / l
            l_ref[rows, :] = jnp.broadcast_to(m + jnp.log(l), (128, HD))
            return carry

        lax.fori_loop(0, d * nblk, step, 0, unroll=4)

    def col(off):
        return lambda b, h: (b, 0, off // HD + HPG * g + h)

    blk = (None, S, HD)
    tab = pl.BlockSpec((S, HD), lambda b, h: (0, 0))
    out = pl.BlockSpec(blk, lambda b, h: (b, 0, h))
    o, l = pl.pallas_call(
        body, name=f"attn_fwd_d{d}", grid=(NB, HPG),
        in_specs=[pl.BlockSpec(blk, col(QA)), pl.BlockSpec(blk, col(KA)), pl.BlockSpec(blk, col(VA)), tab, tab],
        out_specs=[out, out],
        out_shape=[jax.ShapeDtypeStruct((NB, S, HPG * HD), F32)] * 2,
        scratch_shapes=[pltpu.VMEM((d, ln + 128, HD), BF16), pltpu.VMEM((d, ln + 128, HD), BF16)],
        compiler_params=_cp(("parallel", "parallel")),
    )(proj_v, proj_v, proj_v, cosf, sinf)
    return o.reshape(T, HPG * HD), l.reshape(T, HPG * HD)


def _attn_bwd(proj, cosf, sinf, da, lse, delta, dproj, g):
    d = DIL[g]
    ln = S // d
    nblk = ln // 128
    proj_v = proj.reshape(NB, S, NC)
    dproj_v = dproj.reshape(NB, S, NC)
    da_v = da.reshape(NB, S, AW)
    lse_v = lse.reshape(NB, S, HPG * HD)
    delta_v = delta.reshape(NB, S, HPG * HD)

    def body(q_ref, k_ref, v_ref, cos_ref, sin_ref, da_ref, lse_ref, dl_ref, dp_in_ref, o_ref,
             k_s, v_s, stg, dk_s, dv_s):
        del dp_in_ref
        w = pl.program_id(2)

        def emit():
            def cast(n, carry):
                rows = pl.ds(pl.multiple_of(n * 256, 256), 256)
                o_ref[rows, :] = stg[rows, :].astype(BF16)
                return carry

            lax.fori_loop(0, S // 256, cast, 0)

        @pl.when(w == 0)
        def _():
            k_s[:, pl.ds(0, 128), :] = jnp.zeros((d, 128, HD), BF16)
            v_s[:, pl.ds(0, 128), :] = jnp.zeros((d, 128, HD), BF16)
            dk_s[...] = jnp.zeros_like(dk_s)
            dv_s[...] = jnp.zeros_like(dv_s)

            def prep(i, carry):
                r, n = i // nblk, i % nblk
                rows = _dil_rows(r, n, d)
                dst = pl.ds(pl.multiple_of(n * 128 + 128, 128), 128)
                k_s[r, dst, :] = _rope(k_ref[rows, :], cos_ref[rows, :], sin_ref[rows, :], HD // 2).astype(BF16)
                v_s[r, dst, :] = v_ref[rows, :].astype(BF16)
                return carry

            lax.fori_loop(0, d * nblk, prep, 0, unroll=4)

            def step(i, carry):
                r, n = i // nblk, i % nblk
                rows = _dil_rows(r, n, d)
                r0 = pl.multiple_of(n * 128, 128)
                win = pl.ds(r0, 256)
                cos, sin = cos_ref[rows, :], sin_ref[rows, :]
                qr = _rope(q_ref[rows, :], cos, sin, HD // 2).astype(BF16)
                kw = k_s[r, win, :]
                vw = v_s[r, win, :]
                sc = lax.dot_general(qr, kw, NT, preferred_element_type=F32) * A_SCALE
                sc = jnp.where(_band_mask(r0), sc, NEG)
                p = jnp.exp(sc - lse_ref[rows, :][:, :1])
                da_b = da_ref[rows, :].astype(BF16)
                dp = lax.dot_general(da_b, vw, NT, preferred_element_type=F32)
                ds_b = (p * (dp - dl_ref[rows, :][:, :1]) * A_SCALE).astype(BF16)
                p_b = p.astype(BF16)
                dq = jnp.dot(ds_b, kw, preferred_element_type=F32)
                stg[rows, :] = _rope_t(dq, cos, sin, HD // 2)
                dk_s[r, win, :] += lax.dot_general(ds_b, qr, TN, preferred_element_type=F32)
                dv_s[r, win, :] += lax.dot_general(p_b, da_b, TN, preferred_element_type=F32)
                return carry

            lax.fori_loop(0, d * nblk, step, 0, unroll=4)
            emit()

        @pl.when(w == 1)
        def _():
            def put(i, carry):
                r, n = i // nblk, i % nblk
                rows = _dil_rows(r, n, d)
                src = pl.ds(pl.multiple_of(n * 128 + 128, 128), 128)
                stg[rows, :] = _rope_t(dk_s[r, src, :], cos_ref[rows, :], sin_ref[rows, :], HD // 2)
                return carry

            lax.fori_loop(0, d * nblk, put, 0, unroll=4)
            emit()

        @pl.when(w == 2)
        def _():
            def put(i, carry):
                r, n = i // nblk, i % nblk
                src = pl.ds(pl.multiple_of(n * 128 + 128, 128), 128)
                stg[_dil_rows(r, n, d), :] = dv_s[r, src, :]
                return carry

            lax.fori_loop(0, d * nblk, put, 0, unroll=4)
            emit()

    def col(off):
        return lambda b, h, w: (b, 0, off // HD + HPG * g + h)

    blk = (None, S, HD)
    tab = pl.BlockSpec((S, HD), lambda b, h, w: (0, 0))
    per_head = pl.BlockSpec(blk, lambda b, h, w: (b, 0, h))
    out = pl.pallas_call(
        body, name=f"attn_bwd_d{d}", grid=(NB, HPG, 3),
        in_specs=[pl.BlockSpec(blk, col(QA)), pl.BlockSpec(blk, col(KA)), pl.BlockSpec(blk, col(VA)), tab, tab,
                  pl.BlockSpec(blk, col(0)), per_head, per_head, pl.BlockSpec(memory_space=pl.ANY)],
        out_specs=pl.BlockSpec(blk, lambda b, h, w: (b, 0, (AW // HD) * w + HPG * g + h)),
        out_shape=jax.ShapeDtypeStruct(dproj_v.shape, BF16),
        input_output_aliases={8: 0},
        scratch_shapes=[pltpu.VMEM((d, ln + 128, HD), BF16), pltpu.VMEM((d, ln + 128, HD), BF16),
                        pltpu.VMEM((S, HD), F32), pltpu.VMEM((d, ln + 128, HD), F32), pltpu.VMEM((d, ln + 128, HD), F32)],
        compiler_params=_cp(("arbitrary",) * 3),
    )(proj_v, proj_v, proj_v, cosf, sinf, da_v, lse_v, delta_v, dproj_v)
    return out.reshape(T, NC)


def _attn_mix(proj, os_, ls_):
    tr = 256
    gw = HPG * HD

    def body(o0, o1, o2, l0, l1, l2, z_ref, cat_ref):
        m = jnp.maximum(jnp.maximum(l0[...], l1[...]), l2[...])
        e = [jnp.exp(l[...] - m) for l in (l0, l1, l2)]
        inv = 1.0 / (e[0] + e[1] + e[2])
        for gi, o in enumerate((o0, o1, o2)):
            z = z_ref[:, gi * gw:(gi + 1) * gw]
            cat_ref[:, gi * gw:(gi + 1) * gw] = (o[...] * (e[gi] * inv) * (z * _sigmoid(z))).astype(BF16)

    grp = pl.BlockSpec((tr, gw), lambda i: (i, 0))
    return pl.pallas_call(
        body, name="attn_mix", grid=(T // tr,),
        in_specs=[grp] * 6 + [pl.BlockSpec((tr, AW), lambda i: (i, ZA // AW))],
        out_specs=pl.BlockSpec((tr, AW), lambda i: (i, 0)),
        out_shape=jax.ShapeDtypeStruct((T, D), BF16),
        compiler_params=_cp(("parallel",)),
    )(*os_, *ls_, proj)


def _attn_mix_bwd(dcat, proj, os_, ls_, dep=None):
    tr = 256
    gw = HPG * HD
    deps = [] if dep is None else [dep]

    def body(dy_ref, o0, o1, o2, l0, l1, l2, z_ref, *rest):
        da_ref, lse_ref, dl_ref, dz_ref = rest[len(deps):]
        m = jnp.maximum(jnp.maximum(l0[...], l1[...]), l2[...])
        e = [jnp.exp(l[...] - m) for l in (l0, l1, l2)]
        den = e[0] + e[1] + e[2]
        inv = 1.0 / den
        lse_ref[...] = m + jnp.log(den)
        acc = jnp.zeros((tr, gw), F32)
        for gi, o in enumerate((o0, o1, o2)):
            cols = slice(gi * gw, (gi + 1) * gw)
            z = z_ref[:, cols]
            dy = dy_ref[:, cols]
            sg = _sigmoid(z)
            a = o[...] * (e[gi] * inv)
            da = dy * (z * sg)
            da_ref[:, cols] = da
            dz_ref[:, cols] = (dy * a * (sg * (1.0 + z * (1.0 - sg)))).astype(BF16)
            acc = acc + da * a
        for hh in range(HPG):
            cols = slice(hh * HD, (hh + 1) * HD)
            dl_ref[:, cols] = jnp.broadcast_to(jnp.sum(acc[:, cols], axis=1, keepdims=True), (tr, HD))

    grp = pl.BlockSpec((tr, gw), lambda i: (i, 0))
    return pl.pallas_call(
        body, name="attn_mix_bwd", grid=(T // tr,),
        in_specs=[pl.BlockSpec((tr, AW), lambda i: (i, 0))] + [grp] * 6 + [pl.BlockSpec((tr, AW), lambda i: (i, ZA // AW))]
        + [pl.BlockSpec(memory_space=pl.ANY)] * len(deps),
        out_specs=[pl.BlockSpec((tr, AW), lambda i: (i, 0)), grp, grp, pl.BlockSpec((tr, AW), lambda i: (i, ZA // AW))],
        out_shape=[jax.ShapeDtypeStruct((T, AW), F32), jax.ShapeDtypeStruct((T, gw), F32),
                   jax.ShapeDtypeStruct((T, gw), F32), jax.ShapeDtypeStruct((T, NC), BF16)],
        compiler_params=_cp(("parallel",)),
    )(dcat, *os_, *ls_, proj, *deps)


CT = 256


def _shift_down(x, n):
    rows = lax.broadcasted_iota(jnp.int32, x.shape, 0)
    return jnp.where(rows >= n, pltpu.roll(x, n, 0), 0.0)


def _shift_up(x, n):
    rows = lax.broadcasted_iota(jnp.int32, x.shape, 0)
    return jnp.where(rows < x.shape[0] - n, pltpu.roll(x, x.shape[0] - n, 0), 0.0)


def _conv_fwd(proj, conv_w, cat):
    proj_v = proj.reshape(NB, S, NC)
    cat_v = cat.reshape(NB, S, D)

    def body(u_ref, b_ref, c_ref, z_ref, w_ref, cat_in, o_ref):
        del cat_in
        cu = c_ref[...] * u_ref[...]
        cv = _shift_down(cu, 2) * w_ref[0:1, :] + _shift_down(cu, 1) * w_ref[1:2, :] + cu * w_ref[2:3, :]
        z = z_ref[...]
        o_ref[...] = (b_ref[...] * cv * (z * _sigmoid(z))).astype(BF16)

    def seg(off):
        return pl.BlockSpec((None, S, CT), lambda b, j: (b, 0, off // CT + j))

    out = pl.pallas_call(
        body, name="conv_fwd", grid=(NB, CW // CT),
        in_specs=[seg(UC), seg(BC), seg(CC), seg(ZC), pl.BlockSpec((3, CT), lambda b, j: (0, j)),
                  pl.BlockSpec(memory_space=pl.ANY)],
        out_specs=pl.BlockSpec((None, S, CT), lambda b, j: (b, 0, AW // CT + j)),
        out_shape=jax.ShapeDtypeStruct((NB, S, D), BF16),
        input_output_aliases={5: 0},
        compiler_params=_cp(("parallel", "parallel")),
    )(proj_v, proj_v, proj_v, proj_v, conv_w, cat_v)
    return out.reshape(T, D)


def _conv_bwd(dcat, proj, conv_w, dproj):
    proj_v = proj.reshape(NB, S, NC)
    dproj_v = dproj.reshape(NB, S, NC)
    dcat_v = dcat.reshape(NB, S, D)

    def body(dy_ref, u_ref, b_ref, c_ref, z_ref, w_ref, dp_in, o_ref, dw_ref, st):
        del dp_in
        b = pl.program_id(1)
        w = pl.program_id(2)

        @pl.when((b == 0) & (w == 0))
        def _():
            dw_ref[...] = jnp.zeros_like(dw_ref)

        @pl.when(w == 0)
        def _():
            u, c, z, bb, dy = u_ref[...], c_ref[...], z_ref[...], b_ref[...], dy_ref[...]
            cu = c * u
            s1 = _shift_down(cu, 1)
            s2 = _shift_down(cu, 2)
            cv = s2 * w_ref[0:1, :] + s1 * w_ref[1:2, :] + cu * w_ref[2:3, :]
            sg = _sigmoid(z)
            sz = z * sg
            dcv = dy * bb * sz
            st[0] = dy * cv * sz
            st[2] = dy * bb * cv * (sg * (1.0 + z * (1.0 - sg)))
            dw_ref[0:1, :] += jnp.sum(dcv * s2, axis=0, keepdims=True)
            dw_ref[1:2, :] += jnp.sum(dcv * s1, axis=0, keepdims=True)
            dw_ref[2:3, :] += jnp.sum(dcv * cu, axis=0, keepdims=True)
            dcu = dcv * w_ref[2:3, :] + _shift_up(dcv, 1) * w_ref[1:2, :] + _shift_up(dcv, 2) * w_ref[0:1, :]
            st[1] = dcu * u
            o_ref[...] = (dcu * c).astype(BF16)

        for k in range(3):
            @pl.when(w == k + 1)
            def _(k=k):
                o_ref[...] = st[k].astype(BF16)

    def seg(off):
        return pl.BlockSpec((None, S, CT), lambda j, b, w: (b, 0, off // CT + j))

    out, dw = pl.pallas_call(
        body, name="conv_bwd", grid=(CW // CT, NB, 4),
        in_specs=[pl.BlockSpec((None, S, CT), lambda j, b, w: (b, 0, AW // CT + j)),
                  seg(UC), seg(BC), seg(CC), seg(ZC), pl.BlockSpec((3, CT), lambda j, b, w: (0, j)),
                  pl.BlockSpec(memory_space=pl.ANY)],
        out_specs=[pl.BlockSpec((None, S, CT), lambda j, b, w: (b, 0, (UC + w * CW) // CT + j)),
                   pl.BlockSpec((3, CT), lambda j, b, w: (0, j))],
        out_shape=[jax.ShapeDtypeStruct((NB, S, NC), BF16), jax.ShapeDtypeStruct((3, CW), F32)],
        input_output_aliases={6: 0},
        scratch_shapes=[pltpu.VMEM((3, S, CT), F32)],
        compiler_params=_cp(("arbitrary",) * 3),
    )(dcat_v, proj_v, proj_v, proj_v, proj_v, conv_w, dproj_v)
    return out.reshape(T, NC), dw


XT = 512


def _cross_fwd(proj, mkv, cosq, sinq, cosm, sinm, cat):
    proj_v = proj.reshape(NB, S, NC)
    mkv_v = mkv.reshape(NB, MLEN, 2 * XW)
    cat_v = cat.reshape(NB, S, D)

    def body(q_ref, z_ref, mk_ref, mv_ref, cq, sq, cm, sm, cat_in, o_ref):
        del cat_in
        mkr = _rope(mk_ref[...], cm[...], sm[...], XHD // 2).astype(BF16)
        qr = _rope(q_ref[...], cq[...], sq[...], XHD // 2).astype(BF16)
        sc = lax.dot_general(qr, mkr, NT, preferred_element_type=F32) * X_SCALE
        p = jnp.exp(sc - jnp.max(sc, axis=1, keepdims=True))
        p = p / jnp.sum(p, axis=1, keepdims=True)
        ox = jnp.dot(p.astype(BF16), mv_ref[...].astype(BF16), preferred_element_type=F32)
        z = z_ref[...]
        o_ref[...] = (ox * (z * _sigmoid(z))).astype(BF16)

    def seg(off):
        return pl.BlockSpec((None, XT, XHD), lambda b, h, t: (b, t, off // XHD + h))

    qtab = pl.BlockSpec((XT, XHD), lambda b, h, t: (t, 0))
    mtab = pl.BlockSpec((MLEN, XHD), lambda b, h, t: (0, 0))
    out = pl.pallas_call(
        body, name="cross_fwd", grid=(NB, NXH, S // XT),
        in_specs=[seg(QX), seg(ZX),
                  pl.BlockSpec((None, MLEN, XHD), lambda b, h, t: (b, 0, h)),
                  pl.BlockSpec((None, MLEN, XHD), lambda b, h, t: (b, 0, NXH + h)),
                  qtab, qtab, mtab, mtab, pl.BlockSpec(memory_space=pl.ANY)],
        out_specs=pl.BlockSpec((None, XT, XHD), lambda b, h, t: (b, t, (AW + CW) // XHD + h)),
        out_shape=jax.ShapeDtypeStruct((NB, S, D), BF16),
        input_output_aliases={8: 0},
        compiler_params=_cp(("parallel",) * 3),
    )(proj_v, proj_v, mkv_v, mkv_v, cosq, sinq, cosm, sinm, cat_v)
    return out.reshape(T, D)


def _cross_bwd(dcat, proj, mkv, cosq, sinq, cosm, sinm, dproj):
    proj_v = proj.reshape(NB, S, NC)
    dproj_v = dproj.reshape(NB, S, NC)
    dcat_v = dcat.reshape(NB, S, D)
    mkv_v = mkv.reshape(NB, MLEN, 2 * XW)
    nt = S // XT

    def body(dy_ref, q_ref, z_ref, mk_ref, mv_ref, cq, sq, cm, sm, dp_in, o_ref, dmk_ref, dmv_ref, dz_s):
        del dp_in
        t = pl.program_id(2)
        w = pl.program_id(3)

        @pl.when((t == 0) & (w == 0))
        def _():
            dmk_ref[...] = jnp.zeros_like(dmk_ref)
            dmv_ref[...] = jnp.zeros_like(dmv_ref)

        @pl.when(w == 0)
        def _():
            mkr = _rope(mk_ref[...], cm[...], sm[...], XHD // 2).astype(BF16)
            mv_b = mv_ref[...].astype(BF16)
            qr = _rope(q_ref[...], cq[...], sq[...], XHD // 2).astype(BF16)
            sc = lax.dot_general(qr, mkr, NT, preferred_element_type=F32) * X_SCALE
            p = jnp.exp(sc - jnp.max(sc, axis=1, keepdims=True))
            p = p / jnp.sum(p, axis=1, keepdims=True)
            p_b = p.astype(BF16)
            ox = jnp.dot(p_b, mv_b, preferred_element_type=F32)
            z = z_ref[...]
            dy = dy_ref[...]
            sg = _sigmoid(z)
            dz_s[...] = dy * ox * (sg * (1.0 + z * (1.0 - sg)))
            dox_b = (dy * (z * sg)).astype(BF16)
            dp = lax.dot_general(dox_b, mv_b, NT, preferred_element_type=F32)
            ds_b = (p * (dp - jnp.sum(dp * p, axis=1, keepdims=True)) * X_SCALE).astype(BF16)
            dq = jnp.dot(ds_b, mkr, preferred_element_type=F32)
            o_ref[...] = _rope_t(dq, cq[...], sq[...], XHD // 2).astype(BF16)
            dmk_ref[...] += lax.dot_general(ds_b, qr, TN, preferred_element_type=F32)
            dmv_ref[...] += lax.dot_general(p_b, dox_b, TN, preferred_element_type=F32)

        @pl.when(w == 1)
        def _():
            o_ref[...] = dz_s[...].astype(BF16)

        @pl.when((t == nt - 1) & (w == 1))
        def _():
            dmk_ref[...] = _rope_t(dmk_ref[...], cm[...], sm[...], XHD // 2)

    def seg(off):
        return pl.BlockSpec((None, XT, XHD), lambda b, h, t, w: (b, t, off // XHD + h))

    qtab = pl.BlockSpec((XT, XHD), lambda b, h, t, w: (t, 0))
    mtab = pl.BlockSpec((MLEN, XHD), lambda b, h, t, w: (0, 0))
    macc = pl.BlockSpec((None, MLEN, XHD), lambda b, h, t, w: (b, 0, h))
    out, dmk, dmv = pl.pallas_call(
        body, name="cross_bwd", grid=(NB, NXH, nt, 2),
        in_specs=[pl.BlockSpec((None, XT, XHD), lambda b, h, t, w: (b, t, (AW + CW) // XHD + h)),
                  seg(QX), seg(ZX),
                  pl.BlockSpec((None, MLEN, XHD), lambda b, h, t, w: (b, 0, h)),
                  pl.BlockSpec((None, MLEN, XHD), lambda b, h, t, w: (b, 0, NXH + h)),
                  qtab, qtab, mtab, mtab, pl.BlockSpec(memory_space=pl.ANY)],
        out_specs=[pl.BlockSpec((None, XT, XHD), lambda b, h, t, w: (b, t, (QX + w * XW) // XHD + h)), macc, macc],
        out_shape=[jax.ShapeDtypeStruct((NB, S, NC), BF16), jax.ShapeDtypeStruct((NB, MLEN, XW), F32),
                   jax.ShapeDtypeStruct((NB, MLEN, XW), F32)],
        input_output_aliases={9: 0},
        scratch_shapes=[pltpu.VMEM((XT, XHD), F32)],
        compiler_params=_cp(("arbitrary",) * 4),
    )(dcat_v, proj_v, proj_v, mkv_v, mkv_v, cosq, sinq, cosm, sinm, dproj_v)
    return out.reshape(T, NC), dmk, dmv


def _local_step(x, mem, pre_norm, conv_w, mem_norm, post_norm, tgt, chip, comm):
    x2 = x.reshape(T, D)
    mem2 = mem.reshape(NB * MLEN, D)
    tgt2 = tgt.reshape(T, D)
    cosa, sina = _rope_tables(jnp.arange(S), HD // 2)
    cosq, sinq = _rope_tables(jnp.arange(S) + MLEN, XHD // 2)
    cosm, sinm = _rope_tables(jnp.arange(MLEN), XHD // 2)

    h = _rms_fwd(x2, pre_norm, "pre_norm_fwd", dep=comm.gather_started())
    memn = _rms_fwd(mem2, mem_norm, "mem_norm_fwd")
    proj = _proj_part(h, comm.w_in_own(), chip, (0,), "proj_own")
    wg_in, tok = comm.w_in_near(after=proj)
    proj = _proj_part(h, wg_in, chip, (REL_XOR[0], REL_XOR[1]), "proj_near", prev=proj, dep=tok)
    wg_in, tok = comm.w_in_all(after=proj)
    proj = _proj_part(h, wg_in, chip, (REL_XOR[2],), "proj_far", prev=proj, dep=tok)
    wg_kv, wg_out = comm.w_rest(after=proj)
    mkv = _mm_nn(memn, wg_kv, "mkv", NB * MLEN, 1024, 512)

    fw = [_attn_fwd(proj, cosa, sina, g) for g in range(3)]
    os_ = [f[0] for f in fw]
    ls_ = [f[1] for f in fw]
    cat = _attn_mix(proj, os_, ls_)
    cat = _conv_fwd(proj, conv_w, cat)
    cat = _cross_fwd(proj, mkv, cosq, sinq, cosm, sinm, cat)
    y = _mm_nn(cat, wg_out, "out_proj", 1024, 1024, 2048)
    dy, dout, d_post, loss = _post_norm_loss(y, x2, tgt2, post_norm)

    dcat = _mm_nt(dy, wg_out, "dcat", 1024, 1024, 2048)
    g_out = _mm_tn(cat, dy, "grad_w_out", 1024, 1024, 2048)
    red_a = comm.reduce_start([g_out.reshape(NCHIP, D // NCHIP, D)], "a")
    da, lse, delta, dproj = _attn_mix_bwd(dcat, proj, os_, ls_, dep=red_a[-1])
    for g in range(3):
        dproj = _attn_bwd(proj, cosa, sina, da, lse, delta, dproj, g)
    dproj, d_conv = _conv_bwd(dcat, proj, conv_w, dproj)
    dproj, dmk, dmv = _cross_bwd(dcat, proj, mkv, cosq, sinq, cosm, sinm, dproj)

    dmkv = jnp.concatenate([dmk, dmv], axis=-1).reshape(NB * MLEN, 2 * XW)
    dmkv_b = dmkv.astype(BF16)
    g_kv = _mm_tn(memn, dmkv_b, "grad_w_mem_kv", 1024, 1024, NB * MLEN)
    dmemn = _mm_nt(dmkv_b, wg_kv, "dmemn", NB * MLEN, 1024, 512)
    d_mem = _norm_gain_grad(dmemn, mem2, "mem_norm_bwd")

    g_in = _grad_w_in(h, dproj)
    red_b = comm.reduce_start([g_in.reshape(NCHIP, D, WB), g_kv.reshape(NCHIP, D // NCHIP, 2 * XW)], "b")
    dh = _dh(dproj, wg_in, dep=red_b[-1])
    grad_x, d_pre = _pre_norm_bwd(dh, x2, pre_norm, dout)
    (r_out,) = comm.reduce_finish(red_a, "a", after=grad_x)
    r_in, r_kv = comm.reduce_finish(red_b, "b", after=grad_x)
    return loss, grad_x.reshape(NB, S, D), d_pre, d_mem, d_post, d_conv, r_in, r_kv, r_out


def _adamw(w, g, m, v, name):
    rows, cols = w.shape
    tr = rows if rows <= 512 else 512
    tc = cols if cols <= 1024 else 1024
    if cols % tc:
        tc = 896

    def body(w_ref, g_ref, m_ref, v_ref, go_ref, d_ref, nm_ref, nv_ref):
        gv = g_ref[...]
        go_ref[...] = gv
        nm = ADAM_B1 * m_ref[...] + (1.0 - ADAM_B1) * gv
        nv = ADAM_B2 * v_ref[...] + (1.0 - ADAM_B2) * (gv * gv)
        m_hat = nm / (1.0 - ADAM_B1 ** ADAM_STEP)
        v_hat = nv / (1.0 - ADAM_B2 ** ADAM_STEP)
        d_ref[...] = -ADAM_LR * (m_hat / (jnp.sqrt(v_hat) + ADAM_EPS) + ADAM_WD * w_ref[...])
        nm_ref[...] = nm
        nv_ref[...] = nv

    blk = pl.BlockSpec((tr, tc), lambda i, j: (i, j))
    sds = jax.ShapeDtypeStruct((rows, cols), F32)
    return pl.pallas_call(
        body, name=name, grid=(rows // tr, cols // tc),
        in_specs=[blk] * 4, out_specs=[blk] * 4, out_shape=[sds] * 4,
        compiler_params=_cp(("parallel", "parallel")),
    )(w, g, m, v)


def _place():
    return lax.axis_index("x"), lax.axis_index("y"), lax.axis_index("c")


def _other_chips(x, y):
    return [(1 - x, y), (x, 1 - y), (1 - x, 1 - y)]


def _tile_cols(cols):
    return cols if cols <= 1024 else (1024 if cols % 1024 == 0 else 896)


def _cast_own(w, chip, name):
    rows, cols = w.shape
    tr, tc = 512, _tile_cols(cols)

    def body(chip_ref, w_ref, o_ref):
        del chip_ref
        o_ref[...] = w_ref[...].astype(BF16)

    grid_spec = pltpu.PrefetchScalarGridSpec(
        num_scalar_prefetch=1, grid=(rows // tr, cols // tc),
        in_specs=[pl.BlockSpec((tr, tc), lambda i, j, cr: (i, j))],
        out_specs=pl.BlockSpec((None, tr, tc), lambda i, j, cr: (cr[0], i, j)))
    return pl.pallas_call(
        body, name=name, grid_spec=grid_spec,
        out_shape=jax.ShapeDtypeStruct((NCHIP, rows, cols), BF16),
        compiler_params=_cp(("parallel", "parallel")),
    )(_as_index(chip), w)


HBM_SPEC = pl.BlockSpec(memory_space=pltpu.HBM)
SEM_SPEC = pl.BlockSpec(memory_space=pltpu.SEMAPHORE)
ANY_SPEC = pl.BlockSpec(memory_space=pl.ANY)
EFFECT = pltpu.SideEffectType.DATAFLOW_SIDE_EFFECTING
TOKEN = jax.ShapeDtypeStruct((8, 128), F32)


def _half(ref, chip, hc):
    hr = ref.shape[1] // 2
    return ref.at[chip, pl.ds(hc * hr, hr), :]


NEAR = (0, 1)
FAR = (2,)
REL_XOR = (2, 1, 3)


def _gather_copies(refs, send_sems, recv_sems, rels):
    x, y, c = _place()
    chips = _other_chips(x, y)
    out, inc = [], []
    for a, ref in enumerate(refs):
        for p, j in enumerate(rels):
            px, py = chips[j]
            mine = _half(ref, 2 * x + y, c)
            theirs = _half(ref, 2 * px + py, c)
            sems = dict(send_sem=send_sems.at[len(rels) * a + p], recv_sem=recv_sems.at[len(rels) * a + p],
                        device_id=(px, py, c), device_id_type=MESH)
            out.append(pltpu.make_async_remote_copy(src_ref=mine, dst_ref=mine, **sems))
            inc.append(pltpu.make_async_remote_copy(src_ref=theirs, dst_ref=theirs, **sems))
    return out, inc


def _gather_start(bufs, groups, name):
    n = len(bufs)
    ng = len(groups)

    def body(*refs):
        ins = refs[:n]
        token = refs[-1]
        for gi, rels in enumerate(groups):
            out, _ = _gather_copies(ins, refs[n + 2 * gi], refs[n + 2 * gi + 1], rels)
            for cp in out:
                cp.start()
        token[...] = jnp.zeros_like(token)

    sems = []
    for rels in groups:
        sems += [pltpu.SemaphoreType.DMA((len(rels) * n,))] * 2
    res = pl.pallas_call(
        body, name=name,
        in_specs=[HBM_SPEC] * n,
        out_specs=[SEM_SPEC] * (2 * ng) + [HBM_SPEC] * n + [pl.BlockSpec(memory_space=pltpu.VMEM)],
        out_shape=sems + [pltpu.HBM(b.shape, b.dtype) for b in bufs] + [TOKEN],
        input_output_aliases={a: 2 * ng + a for a in range(n)},
        compiler_params=pltpu.CompilerParams(has_side_effects=EFFECT),
    )(*[pltpu.with_memory_space_constraint(b, pltpu.HBM) for b in bufs])
    return [(res[2 * gi], res[2 * gi + 1]) for gi in range(ng)], list(res[2 * ng:2 * ng + n]), res[-1]


def _gather_wait(bufs, sems, rels, after, name):
    n = len(bufs)
    send_sems, recv_sems = sems

    def body(*refs):
        ins = refs[:n]
        out, inc = _gather_copies(ins, refs[n], refs[n + 1], rels)
        for cp in out:
            cp.wait_send()
        for cp in inc:
            cp.wait_recv()

    return pl.pallas_call(
        body, name=name,
        in_specs=[HBM_SPEC] * n + [SEM_SPEC, SEM_SPEC, ANY_SPEC],
        out_specs=[HBM_SPEC] * n,
        out_shape=[pltpu.HBM(b.shape, b.dtype) for b in bufs],
        input_output_aliases={a: a for a in range(n)},
        compiler_params=pltpu.CompilerParams(has_side_effects=EFFECT),
    )(*bufs, send_sems, recv_sems, after)


def _forward_halves(bufs, rels, name):
    n = len(bufs)

    def body(*refs):
        outs = refs[n:2 * n]
        send_sems, recv_sems = refs[2 * n:]
        x, y, c = _place()
        chips = _other_chips(x, y)
        cps, waits = [], []
        for a in range(n):
            for p, j in enumerate(rels):
                px, py = chips[j]
                sems = dict(send_sem=send_sems.at[len(rels) * a + p], recv_sem=recv_sems.at[len(rels) * a + p],
                            device_id=(x, y, 1 - c), device_id_type=MESH)
                got = _half(outs[a], 2 * px + py, c)
                want = _half(outs[a], 2 * px + py, 1 - c)
                cps.append(pltpu.make_async_remote_copy(src_ref=got, dst_ref=got, **sems))
                waits.append(pltpu.make_async_remote_copy(src_ref=want, dst_ref=want, **sems))
        for cp in cps:
            cp.start()
        for cp in waits:
            cp.wait_recv()
        for cp in cps:
            cp.wait_send()

    return pl.pallas_call(
        body, name=name,
        in_specs=[ANY_SPEC] * n, out_specs=[ANY_SPEC] * n,
        out_shape=[jax.ShapeDtypeStruct(s.shape, s.dtype) for s in bufs],
        input_output_aliases={a: a for a in range(n)},
        scratch_shapes=[pltpu.SemaphoreType.DMA((len(rels) * n,)), pltpu.SemaphoreType.DMA((len(rels) * n,))],
    )(*bufs)


def _sibling_halves(parts, name):
    n = len(parts)

    def body(*refs):
        ins, outs = refs[:n], refs[n:2 * n]
        send_sems, recv_sems = refs[2 * n:]
        x, y, c = _place()
        cps = []
        for a in range(n):
            hr = ins[a].shape[1] // 2
            cps.append(pltpu.make_async_remote_copy(
                src_ref=ins[a].at[:, pl.ds((1 - c) * hr, hr), :], dst_ref=outs[a],
                send_sem=send_sems.at[a], recv_sem=recv_sems.at[a], device_id=(x, y, 1 - c), device_id_type=MESH))
        for cp in cps:
            cp.start()
        for cp in cps:
            cp.wait()

    hbm = pl.BlockSpec(memory_space=pl.ANY)
    return pl.pallas_call(
        body, name=name,
        in_specs=[hbm] * n, out_specs=[hbm] * n,
        out_shape=[jax.ShapeDtypeStruct((NCHIP, p.shape[1] // 2, p.shape[2]), p.dtype) for p in parts],
        scratch_shapes=[pltpu.SemaphoreType.DMA((n,)), pltpu.SemaphoreType.DMA((n,))],
    )(*parts)


def _pair_sum(part, got, name):
    nblk, rows, cols = part.shape
    hr = rows // 2
    tr = 512
    tc = cols if cols <= 1024 else (1024 if cols % 1024 == 0 else 896)
    c = lax.axis_index("c")
    nrt = hr // tr

    def body(c_ref, p_ref, g_ref, o_ref):
        del c_ref
        o_ref[...] = (p_ref[...] + g_ref[...]).astype(BF16)

    grid_spec = pltpu.PrefetchScalarGridSpec(
        num_scalar_prefetch=1, grid=(nblk, nrt, cols // tc),
        in_specs=[pl.BlockSpec((None, tr, tc), lambda b, i, j, cr: (b, cr[0] * nrt + i, j)),
                  pl.BlockSpec((None, tr, tc), lambda b, i, j, cr: (b, i, j))],
        out_specs=pl.BlockSpec((None, tr, tc), lambda b, i, j, cr: (b, i, j)))
    return pl.pallas_call(
        body, name=name, grid_spec=grid_spec,
        out_shape=jax.ShapeDtypeStruct((nblk, hr, cols), BF16),
        compiler_params=_cp(("parallel",) * 3),
    )(jnp.reshape(c, (1,)).astype(jnp.int32), part, got)


def _scatter_start(qs, name):
    n = len(qs)

    def body(*refs):
        ins, lands = refs[:n], refs[n:2 * n]
        token = refs[-1]
        for cp in _scatter_copies(ins, lands, refs[2 * n], refs[2 * n + 1]):
            cp.start()
        token[...] = jnp.zeros_like(token)

    lands = [lax.empty((3,) + q.shape[1:], q.dtype) for q in qs]
    res = pl.pallas_call(
        body, name=name,
        in_specs=[HBM_SPEC] * (2 * n),
        out_specs=[SEM_SPEC, SEM_SPEC] + [HBM_SPEC] * (2 * n) + [pl.BlockSpec(memory_space=pltpu.VMEM)],
        out_shape=[pltpu.SemaphoreType.DMA((3 * n,)), pltpu.SemaphoreType.DMA((3 * n,))]
        + [pltpu.HBM(b.shape, b.dtype) for b in qs + lands] + [TOKEN],
        input_output_aliases={a: 2 + a for a in range(2 * n)},
        compiler_params=pltpu.CompilerParams(has_side_effects=EFFECT),
    )(*[pltpu.with_memory_space_constraint(b, pltpu.HBM) for b in qs + lands])
    return res[0], res[1], list(res[2:2 + n]), list(res[2 + n:2 + 2 * n]), res[-1]


def _scatter_copies(ins, lands, send_sems, recv_sems):
    x, y, c = _place()
    cps = []
    for a in range(len(ins)):
        for j, (px, py) in enumerate(_other_chips(x, y)):
            cps.append(pltpu.make_async_remote_copy(
                src_ref=ins[a].at[2 * px + py], dst_ref=lands[a].at[j],
                send_sem=send_sems.at[3 * a + j], recv_sem=recv_sems.at[3 * a + j], device_id=(px, py, c), device_id_type=MESH))
    return cps


def _scatter_wait(qs, lands, send_sems, recv_sems, after, name):
    n = len(qs)

    def body(*refs):
        for cp in _scatter_copies(refs[:n], refs[n:2 * n], refs[2 * n], refs[2 * n + 1]):
            cp.wait_send()
            cp.wait_recv()

    res = pl.pallas_call(
        body, name=name,
        in_specs=[HBM_SPEC] * (2 * n) + [SEM_SPEC, SEM_SPEC, ANY_SPEC],
        out_specs=[HBM_SPEC] * (2 * n),
        out_shape=[pltpu.HBM(b.shape, b.dtype) for b in qs + lands],
        input_output_aliases={a: a for a in range(2 * n)},
        compiler_params=pltpu.CompilerParams(has_side_effects=EFFECT),
    )(*qs, *lands, send_sems, recv_sems, after)
    return list(res[:n]), list(res[n:])


def _chip_sum(q, got, name):
    _, hr, cols = got.shape
    tr, tc = 512, _tile_cols(cols)
    chip = 2 * lax.axis_index("x") + lax.axis_index("y")
    c = lax.axis_index("c")

    def body(idx_ref, q_ref, g_ref, o_ref):
        del idx_ref
        acc = q_ref[...].astype(F32)
        for i in range(3):
            acc = acc + g_ref[i].astype(F32)
        o_ref[...] = acc

    grid_spec = pltpu.PrefetchScalarGridSpec(
        num_scalar_prefetch=1, grid=(hr // tr, cols // tc),
        in_specs=[pl.BlockSpec((None, tr, tc), lambda i, j, ix: (ix[0], i, j)),
                  pl.BlockSpec((3, tr, tc), lambda i, j, ix: (0, i, j))],
        out_specs=pl.BlockSpec((None, tr, tc), lambda i, j, ix: (ix[1], i, j)))
    return pl.pallas_call(
        body, name=name, grid_spec=grid_spec,
        out_shape=jax.ShapeDtypeStruct((2, hr, cols), F32),
        compiler_params=_cp(("parallel", "parallel")),
    )(jnp.stack([chip, c]).astype(jnp.int32), q, got)


def _join_halves(bufs, name):
    n = len(bufs)

    def body(*refs):
        outs = refs[n:2 * n]
        send_sems, recv_sems = refs[2 * n:]
        x, y, c = _place()
        cps = []
        for a in range(n):
            cps.append(pltpu.make_async_remote_copy(src_ref=outs[a].at[c], dst_ref=outs[a].at[c], send_sem=send_sems.at[a],
                                                    recv_sem=recv_sems.at[a], device_id=(x, y, 1 - c), device_id_type=MESH))
        for cp in cps:
            cp.start()
        for a in range(n):
            theirs = outs[a].at[1 - c]
            pltpu.make_async_remote_copy(src_ref=theirs, dst_ref=theirs, send_sem=send_sems.at[a], recv_sem=recv_sems.at[a],
                                         device_id=(x, y, 1 - c), device_id_type=MESH).wait_recv()
        for cp in cps:
            cp.wait_send()

    hbm = pl.BlockSpec(memory_space=pl.ANY)
    return pl.pallas_call(
        body, name=name,
        in_specs=[hbm] * n, out_specs=[hbm] * n,
        out_shape=[jax.ShapeDtypeStruct(b.shape, b.dtype) for b in bufs],
        input_output_aliases={a: a for a in range(n)},
        scratch_shapes=[pltpu.SemaphoreType.DMA((n,)), pltpu.SemaphoreType.DMA((n,))],
    )(*bufs)


N_DEV = 8


def _gather_small(block, reduce, name):
    m_per, cols = block.shape

    def body(x_ref, out_ref, all_ref, send_sems, recv_sems, local_sem):
        x, y, c = _place()
        me, sibling = (x, y, c), (x, y, 1 - c)
        chips = _other_chips(x, y)

        def rows(px, py, pc):
            return all_ref.at[pl.ds((4 * px + 2 * py + pc) * m_per, m_per), :]

        def copy(k, block_of, to, src=None):
            return pltpu.make_async_remote_copy(
                src_ref=rows(*block_of) if src is None else src, dst_ref=rows(*block_of),
                send_sem=send_sems.at[k], recv_sem=recv_sems.at[k], device_id=to, device_id_type=MESH)

        mine = pltpu.make_async_copy(x_ref, rows(*me), local_sem)
        mine.start()
        first = [copy(0, me, sibling, src=x_ref)]
        first += [copy(1 + j, me, (*chip, c), src=x_ref) for j, chip in enumerate(chips)]
        for cp in first:
            cp.start()
        passed = [copy(4 + j, (*chip, c), sibling) for j, chip in enumerate(chips)]
        for j, chip in enumerate(chips):
            copy(1 + j, (*chip, c), me).wait_recv()
            passed[j].start()
        copy(0, sibling, me).wait_recv()
        for j, chip in enumerate(chips):
            copy(4 + j, (*chip, 1 - c), me).wait_recv()
        for cp in first + passed:
            cp.wait_send()
        mine.wait()
        if reduce:
            acc = all_ref[pl.ds(0, m_per), :]
            for i in range(1, N_DEV):
                acc = acc + all_ref[pl.ds(i * m_per, m_per), :]
            out_ref[...] = acc
        else:
            out_ref[...] = all_ref[...]

    out_rows = m_per if reduce else N_DEV * m_per
    return pl.pallas_call(
        body, name=name,
        in_specs=[pl.BlockSpec(memory_space=pltpu.VMEM)],
        out_specs=pl.BlockSpec(memory_space=pltpu.VMEM),
        out_shape=jax.ShapeDtypeStruct((out_rows, cols), F32),
        scratch_shapes=[pltpu.VMEM((N_DEV * m_per, cols), F32), pltpu.SemaphoreType.DMA((7,)),
                        pltpu.SemaphoreType.DMA((7,)), pltpu.SemaphoreType.DMA],
    )(block)


class _Comm:
    def __init__(self, w_in, w_kv, w_out, chip):
        self.bufs = [_cast_own(w_in, chip, "cast_w_in"), _cast_own(w_kv, chip, "cast_w_mem_kv"),
                     _cast_own(w_out, chip, "cast_w_out")]


    def gather_started(self):
        (self.sems,), (self.b_in,), tok = _gather_start(self.bufs[:1], (NEAR,), "gather_start_in_near")
        return [tok]

    def w_in_own(self):
        return self.b_in.reshape(NCHIP * D, WB)

    def w_in_near(self, after):
        (b_in,) = _gather_wait([self.b_in], self.sems, NEAR, after, "gather_wait_in_near")
        (b_in,) = _forward_halves([b_in], NEAR, "forward_in_near")
        (self.sems,), (self.b_in,), tok = _gather_start([b_in], (FAR,), "gather_start_in_far")
        return self.b_in.reshape(NCHIP * D, WB), tok

    def w_in_all(self, after):
        (b_in,) = _gather_wait([self.b_in], self.sems, FAR, after, "gather_wait_in_far")
        (b_in,) = _forward_halves([b_in], FAR, "forward_in_far")
        (self.sems,), self.b_rest, tok = _gather_start(self.bufs[1:], (NEAR + FAR,), "gather_start_rest")
        return b_in.reshape(NCHIP * D, WB), tok

    def w_rest(self, after):
        b_rest = _gather_wait(self.b_rest, self.sems, NEAR + FAR, after, "gather_wait_rest")
        b_kv, b_out = _forward_halves(b_rest, NEAR + FAR, "forward_rest")
        return b_kv.reshape(D, 2 * XW), b_out.reshape(D, D)

    def reduce_start(self, parts, tag):
        got = _sibling_halves(parts, "sibling_halves_" + tag)
        qs = [_pair_sum(p, g, f"pair_sum_{tag}{i}") for i, (p, g) in enumerate(zip(parts, got))]
        return _scatter_start(qs, "scatter_start_" + tag)

    def reduce_finish(self, state, tag, after):
        send_sems, recv_sems, qs, lands, _ = state
        qs, lands = _scatter_wait(qs, lands, send_sems, recv_sems, after, "scatter_wait_" + tag)
        halves = [_chip_sum(q, l, f"chip_sum_{tag}{i}") for i, (q, l) in enumerate(zip(qs, lands))]
        return [j.reshape(2 * j.shape[1], j.shape[2]) for j in _join_halves(halves, "join_halves_" + tag)]


def kernel(x, mem, pre_norm, w_in, conv_w, mem_norm, w_mem_kv, w_out, post_norm, loss_target, m_pre_norm, m_w_in, m_conv_w, m_mem_norm, m_w_mem_kv, m_w_out, m_post_norm, v_pre_norm, v_w_in, v_conv_w, v_mem_norm, v_w_mem_kv, v_w_out, v_post_norm):
    chip = 2 * lax.axis_index("x") + lax.axis_index("y")

    cw_blk = jnp.zeros((8, 384), F32).at[:3].set(conv_w[0])
    cw_all = _gather_small(cw_blk, False, "gather_conv_w").reshape(NCHIP, 2, 8, 384)[:, 0, :3]
    conv_full = jnp.transpose(cw_all, (1, 0, 2)).reshape(3, CW)

    comm = _Comm(w_in[0], w_mem_kv[0], w_out[0], chip)
    loss, grad_x, d_pre, d_mem, d_post, d_conv, r_in, r_kv, r_out = _local_step(
        x, mem, pre_norm, conv_full, mem_norm, post_norm, loss_target, chip, comm)

    pack = jnp.concatenate([d_pre, d_mem, d_post, jnp.pad(d_conv, ((0, 0), (0, D - CW))),
                            jnp.pad(loss, ((0, 0), (0, D - 128))), jnp.zeros((1, D), F32)], axis=0)
    tot = _gather_small(pack, True, "reduce_small")
    g_pre, g_mem, g_post = tot[0:1], tot[1:2], tot[2:3]
    g_conv = lax.dynamic_slice(tot[3:6, :CW], (0, chip * 384), (3, 384))
    loss_out = tot[6, 0]

    names = ("pre_norm", "w_in", "conv_w", "mem_norm", "w_mem_kv", "w_out", "post_norm")
    ws = (pre_norm, w_in[0], conv_w[0], mem_norm, w_mem_kv[0], w_out[0], post_norm)
    gs = (g_pre, r_in, g_conv, g_mem, r_kv, r_out, g_post)
    ms = (m_pre_norm, m_w_in[0], m_conv_w[0], m_mem_norm, m_w_mem_kv[0], m_w_out[0], m_post_norm)
    vs = (v_pre_norm, v_w_in[0], v_conv_w[0], v_mem_norm, v_w_mem_kv[0], v_w_out[0], v_post_norm)
    upd = [_adamw(w, g, m, v, "adamw_" + nm) for nm, w, g, m, v in zip(names, ws, gs, ms, vs)]

    def shaped(arrs):
        return [a.reshape(w.shape) if w.ndim == a.ndim else a.reshape((1,) + a.shape)
                for a, w in zip(arrs, (pre_norm, w_in, conv_w, mem_norm, w_mem_kv, w_out, post_norm))]

    grads = shaped([u[0] for u in upd])
    deltas = shaped([u[1] for u in upd])
    new_m = shaped([u[2] for u in upd])
    new_v = shaped([u[3] for u in upd])
    return (loss_out, grad_x, *grads, *deltas, *new_m, *new_v)
```

```python
import functools

import jax
import jax.numpy as jnp
from jax import lax
from jax.experimental import pallas as pl
from jax.experimental.pallas import tpu as pltpu

F32 = jnp.float32
BF16 = jnp.bfloat16

D = 4096
S = 2048
NB = 2
T = NB * S
MLEN = 256
HD = 128
AW = 1536
CW = 1536
XW = 1024
XHD = 256
NXH = 4
NC = 14336
QA, KA, VA, ZA, UC, BC, CC, ZC, QX, ZX = 0, 1536, 3072, 4608, 6144, 7680, 9216, 10752, 12288, 13312
NCHIP = 4
WB = NC // NCHIP
DIL = (1, 4, 16)
HPG = 4
EPS = 1e-6
NEG = -1e30
ROPE_THETA = 10000.0
A_SCALE = HD ** -0.5
X_SCALE = XHD ** -0.5

ADAM_LR = 0.001
ADAM_B1 = 0.9
ADAM_B2 = 0.999
ADAM_EPS = 1e-08
ADAM_WD = 0.01
ADAM_STEP = 10

MESH = pl.DeviceIdType.MESH
MIB = 1024 * 1024


def _cp(sem, vmem_mib=48):
    return pltpu.CompilerParams(dimension_semantics=sem, vmem_limit_bytes=vmem_mib * MIB)


def _sigmoid(z):
    return 1.0 / (1.0 + jnp.exp(-z))


def _rope(x, cos, sin, half):
    return x * cos + pltpu.roll(x, half, 1) * sin


def _rope_t(g, cos, sin, half):
    return g * cos + pltpu.roll(g * sin, half, 1)


def _rms_fwd(x2, g, name, dep=None):
    rows = x2.shape[0]
    tr = 256
    deps = [] if dep is None else list(dep)

    def body(x_ref, g_ref, *rest):
        o_ref = rest[-1]
        x = x_ref[...]
        r = lax.rsqrt(jnp.mean(x * x, axis=-1, keepdims=True) + EPS)
        o_ref[...] = (x * r * g_ref[...]).astype(BF16)

    return pl.pallas_call(
        body, name=name, grid=(rows // tr,),
        in_specs=[pl.BlockSpec((tr, D), lambda i: (i, 0)), pl.BlockSpec((1, D), lambda i: (0, 0))]
        + [pl.BlockSpec(memory_space=pl.ANY)] * len(deps),
        out_specs=pl.BlockSpec((tr, D), lambda i: (i, 0)),
        out_shape=jax.ShapeDtypeStruct((rows, D), BF16),
        compiler_params=_cp(("parallel",)),
    )(x2, g, *deps)


def _norm_gain_grad(dn, x2, name):
    rows = x2.shape[0]
    tr = 256

    def body(dn_ref, x_ref, dg_ref):
        @pl.when(pl.program_id(0) == 0)
        def _():
            dg_ref[...] = jnp.zeros_like(dg_ref)
        x = x_ref[...]
        r = lax.rsqrt(jnp.mean(x * x, axis=-1, keepdims=True) + EPS)
        dg_ref[...] += jnp.sum(dn_ref[...] * (x * r), axis=0, keepdims=True)

    return pl.pallas_call(
        body, name=name, grid=(rows // tr,),
        in_specs=[pl.BlockSpec((tr, D), lambda i: (i, 0)), pl.BlockSpec((tr, D), lambda i: (i, 0))],
        out_specs=pl.BlockSpec((1, D), lambda i: (0, 0)),
        out_shape=jax.ShapeDtypeStruct((1, D), F32),
        compiler_params=_cp(("arbitrary",)),
    )(dn, x2)


def _pre_norm_bwd(dh, x2, g, dout):
    tr = 256

    def body(dh_ref, x_ref, g_ref, dout_ref, gx_ref, dg_ref):
        @pl.when(pl.program_id(0) == 0)
        def _():
            dg_ref[...] = jnp.zeros_like(dg_ref)
        x = x_ref[...]
        dh_ = dh_ref[...]
        r = lax.rsqrt(jnp.mean(x * x, axis=-1, keepdims=True) + EPS)
        xhat = x * r
        dg_ref[...] += jnp.sum(dh_ * xhat, axis=0, keepdims=True)
        dxn = dh_ * g_ref[...]
        gx_ref[...] = dout_ref[...] + r * (dxn - xhat * jnp.mean(dxn * xhat, axis=-1, keepdims=True))

    row = pl.BlockSpec((tr, D), lambda i: (i, 0))
    vec = pl.BlockSpec((1, D), lambda i: (0, 0))
    return pl.pallas_call(
        body, name="pre_norm_bwd", grid=(T // tr,),
        in_specs=[row, row, vec, row],
        out_specs=[row, vec],
        out_shape=[jax.ShapeDtypeStruct((T, D), F32), jax.ShapeDtypeStruct((1, D), F32)],
        compiler_params=_cp(("arbitrary",)),
    )(dh, x2, g, dout)


def _post_norm_loss(y, x2, tgt, g):
    tr = 256

    def body(y_ref, x_ref, t_ref, g_ref, dy_ref, dout_ref, dg_ref, loss_ref):
        @pl.when(pl.program_id(0) == 0)
        def _():
            dg_ref[...] = jnp.zeros_like(dg_ref)
            loss_ref[...] = jnp.zeros_like(loss_ref)
        yv = y_ref[...]
        gv = g_ref[...]
        r = lax.rsqrt(jnp.mean(yv * yv, axis=-1, keepdims=True) + EPS)
        yhat = yv * r
        err = x_ref[...] + yhat * gv - t_ref[...]
        loss_ref[...] += jnp.sum(jnp.sum(err * err, axis=1, keepdims=True), axis=0, keepdims=True) * (0.5 / D)
        dout = err * (1.0 / D)
        dout_ref[...] = dout
        dg_ref[...] += jnp.sum(dout * yhat, axis=0, keepdims=True)
        dyn = dout * gv
        dy_ref[...] = (r * (dyn - yhat * jnp.mean(dyn * yhat, axis=-1, keepdims=True))).astype(BF16)

    row = pl.BlockSpec((tr, D), lambda i: (i, 0))
    vec = pl.BlockSpec((1, D), lambda i: (0, 0))
    return pl.pallas_call(
        body, name="post_norm_loss", grid=(T // tr,),
        in_specs=[row, row, row, vec],
        out_specs=[row, row, vec, pl.BlockSpec((1, 128), lambda i: (0, 0))],
        out_shape=[jax.ShapeDtypeStruct((T, D), BF16), jax.ShapeDtypeStruct((T, D), F32),
                   jax.ShapeDtypeStruct((1, D), F32), jax.ShapeDtypeStruct((1, 128), F32)],
        compiler_params=_cp(("arbitrary",)),
    )(y, x2, tgt, g)


NN = (((1,), (0,)), ((), ()))
NT = (((1,), (1,)), ((), ()))
TN = (((0,), (0,)), ((), ()))


def _as_index(v):
    return jnp.reshape(v, (1,)).astype(jnp.int32)


def _matmul(a, b, *, name, dims, grid, a_block, a_map, b_block, b_map, o_block, o_map, out_shape, out_dtype=F32,
            index=None, prev=None, deps=()):
    extra = ([] if prev is None else [prev]) + [d for d in deps if d is not None]
    first = 0 if index is None else 1
    nk = grid[2]
    in_place = out_dtype == F32

    def body(*refs):
        a_ref, b_ref, o_ref = refs[first], refs[first + 1], refs[first + 2 + len(extra)]
        acc_ref = o_ref if in_place else refs[-1]

        @pl.when(pl.program_id(2) == 0)
        def _():
            acc_ref[...] = jnp.zeros_like(acc_ref)

        acc_ref[...] += lax.dot_general(a_ref[...], b_ref[...], dims, preferred_element_type=F32)

        if not in_place:
            @pl.when(pl.program_id(2) == nk - 1)
            def _():
                o_ref[...] = acc_ref[...].astype(o_ref.dtype)

    in_specs = [pl.BlockSpec(a_block, a_map), pl.BlockSpec(b_block, b_map)] + [pl.BlockSpec(memory_space=pl.ANY)] * len(extra)
    out_specs = pl.BlockSpec(o_block, o_map)
    scratch = [] if in_place else [pltpu.VMEM(o_block, F32)]
    kwargs = dict(name=name, out_shape=jax.ShapeDtypeStruct(out_shape, out_dtype),
                  input_output_aliases={} if prev is None else {first + 2: 0},
                  compiler_params=_cp(("parallel", "parallel", "arbitrary"), vmem_mib=56))
    if index is None:
        return pl.pallas_call(body, grid=grid, in_specs=in_specs, out_specs=out_specs, scratch_shapes=scratch,
                              **kwargs)(a, b, *extra)
    grid_spec = pltpu.PrefetchScalarGridSpec(num_scalar_prefetch=1, grid=grid, in_specs=in_specs, out_specs=out_specs,
                                             scratch_shapes=scratch)
    return pl.pallas_call(body, grid_spec=grid_spec, **kwargs)(_as_index(index), a, b, *extra)


def _mm_nn(a, b, name, tm, tn, tk):
    m, kd = a.shape
    n = b.shape[1]
    return _matmul(a, b, name=name, dims=NN, grid=(m // tm, n // tn, kd // tk),
                   a_block=(tm, tk), a_map=lambda i, j, k: (i, k),
                   b_block=(tk, tn), b_map=lambda i, j, k: (k, j),
                   o_block=(tm, tn), o_map=lambda i, j, k: (i, j), out_shape=(m, n))


def _mm_nt(a, b, name, tm, tn, tk):
    m, kd = a.shape
    n = b.shape[0]
    return _matmul(a, b, name=name, dims=NT, grid=(m // tm, n // tn, kd // tk),
                   a_block=(tm, tk), a_map=lambda i, j, k: (i, k),
                   b_block=(tn, tk), b_map=lambda i, j, k: (j, k),
                   o_block=(tm, tn), o_map=lambda i, j, k: (i, j), out_shape=(m, n))


def _mm_tn(a, b, name, tm, tn, tk):
    kd, m = a.shape
    n = b.shape[1]
    return _matmul(a, b, name=name, dims=TN, grid=(m // tm, n // tn, kd // tk),
                   a_block=(tk, tm), a_map=lambda i, j, k: (k, i),
                   b_block=(tk, tn), b_map=lambda i, j, k: (k, j),
                   o_block=(tm, tn), o_map=lambda i, j, k: (i, j), out_shape=(m, n))


W_TN = 1792
W_NJ = WB // W_TN


def _proj_part(h, wg, chip, masks, name, prev=None, dep=None):
    tm, tk = 1024, 2048

    def blk(j, ix):
        m = masks[0]
        for t in range(1, len(masks)):
            m = jnp.where(j // W_NJ == t, masks[t], m)
        return jnp.bitwise_xor(ix[0], m)

    return _matmul(h, wg, name=name, dims=NN, grid=(T // tm, len(masks) * W_NJ, D // tk), index=chip, prev=prev, deps=(dep,),
                   a_block=(tm, tk), a_map=lambda i, j, k, ix: (i, k),
                   b_block=(tk, W_TN), b_map=lambda i, j, k, ix: (blk(j, ix) * (D // tk) + k, j % W_NJ),
                   o_block=(tm, W_TN), o_map=lambda i, j, k, ix: (i, blk(j, ix) * W_NJ + j % W_NJ), out_shape=(T, NC))


def _dh(dproj, wg, dep=None):
    tm, tn = 1024, 1024
    return _matmul(dproj, wg, name="dh", dims=NT, grid=(T // tm, D // tn, NC // W_TN), deps=(dep,),
                   a_block=(tm, W_TN), a_map=lambda i, j, k: (i, k),
                   b_block=(tn, W_TN), b_map=lambda i, j, k: ((k // W_NJ) * (D // tn) + j, k % W_NJ),
                   o_block=(tm, tn), o_map=lambda i, j, k: (i, j), out_shape=(T, D))


def _grad_w_in(h, dproj, half, out_dtype, name, dep=None):
    tm, tk = 1024, 2048
    nh = D // 2 // tm
    return _matmul(h, dproj, name=name, dims=TN, grid=(nh, NC // W_TN, T // tk), index=half, deps=(dep,),
                   a_block=(tk, tm), a_map=lambda i, j, k, ix: (k, ix[0] * nh + i),
                   b_block=(tk, W_TN), b_map=lambda i, j, k, ix: (k, j),
                   o_block=(tm, W_TN), o_map=lambda i, j, k, ix: ((j // W_NJ) * nh + i, j % W_NJ),
                   out_shape=(NCHIP * D // 2, WB), out_dtype=out_dtype)


def _rope_tables(pos, half):
    inv = 1.0 / (ROPE_THETA ** (jnp.arange(half, dtype=F32) / half))
    ang = pos.astype(F32)[:, None] * inv[None, :]
    cos, sin = jnp.cos(ang), jnp.sin(ang)
    return jnp.concatenate([cos, cos], axis=1), jnp.concatenate([-sin, sin], axis=1)


def _band_mask(r0):
    qi = lax.broadcasted_iota(jnp.int32, (128, 256), 0)
    kk = lax.broadcasted_iota(jnp.int32, (128, 256), 1)
    return (kk >= qi) & (kk <= qi + 128) & (kk + r0 >= 128)


def _dil_rows(r, n, d):
    if d == 1:
        return pl.ds(pl.multiple_of(n * 128, 128), 128)
    return pl.ds(r + d * 128 * n, 128, stride=d)


def _attn_fwd(proj, cosf, sinf, g):
    d = DIL[g]
    ln = S // d
    nblk = ln // 128
    proj_v = proj.reshape(NB, S, NC)

    def body(q_ref, k_ref, v_ref, cos_ref, sin_ref, o_ref, l_ref, k_s, v_s):
        k_s[:, pl.ds(0, 128), :] = jnp.zeros((d, 128, HD), BF16)
        v_s[:, pl.ds(0, 128), :] = jnp.zeros((d, 128, HD), BF16)

        def prep(i, carry):
            r, n = i // nblk, i % nblk
            rows = _dil_rows(r, n, d)
            dst = pl.ds(pl.multiple_of(n * 128 + 128, 128), 128)
            k_s[r, dst, :] = _rope(k_ref[rows, :], cos_ref[rows, :], sin_ref[rows, :], HD // 2).astype(BF16)
            v_s[r, dst, :] = v_ref[rows, :].astype(BF16)
            return carry

        lax.fori_loop(0, d * nblk, prep, 0, unroll=4)

        def step(i, carry):
            r, n = i // nblk, i % nblk
            rows = _dil_rows(r, n, d)
            r0 = pl.multiple_of(n * 128, 128)
            qr = _rope(q_ref[rows, :], cos_ref[rows, :], sin_ref[rows, :], HD // 2).astype(BF16)
            kw = k_s[r, pl.ds(r0, 256), :]
            vw = v_s[r, pl.ds(r0, 256), :]
            sc = lax.dot_general(qr, kw, NT, preferred_element_type=F32) * A_SCALE
            sc = jnp.where(_band_mask(r0), sc, NEG)
            m = jnp.max(sc, axis=1, keepdims=True)
            p = jnp.exp(sc - m)
            l = jnp.sum(p, axis=1, keepdims=True)
            o_ref[rows, :] = jnp.dot(p.astype(BF16), vw, preferred_element_type=F32) / l
            l_ref[rows, :] = jnp.broadcast_to(m + jnp.log(l), (128, HD))
            return carry

        lax.fori_loop(0, d * nblk, step, 0, unroll=4)

    def col(off):
        return lambda b, h: (b, 0, off // HD + HPG * g + h)

    blk = (None, S, HD)
    tab = pl.BlockSpec((S, HD), lambda b, h: (0, 0))
    out = pl.BlockSpec(blk, lambda b, h: (b, 0, h))
    o, l = pl.pallas_call(
        body, name=f"attn_fwd_d{d}", grid=(NB, HPG),
        in_specs=[pl.BlockSpec(blk, col(QA)), pl.BlockSpec(blk, col(KA)), pl.BlockSpec(blk, col(VA)), tab, tab],
        out_specs=[out, out],
        out_shape=[jax.ShapeDtypeStruct((NB, S, HPG * HD), F32)] * 2,
        scratch_shapes=[pltpu.VMEM((d, ln + 128, HD), BF16), pltpu.VMEM((d, ln + 128, HD), BF16)],
        compiler_params=_cp(("parallel", "parallel")),
    )(proj_v, proj_v, proj_v, cosf, sinf)
    return o.reshape(T, HPG * HD), l.reshape(T, HPG * HD)


def _attn_bwd(proj, cosf, sinf, da, lse, delta, dproj, g):
    d = DIL[g]
    ln = S // d
    nblk = ln // 128
    proj_v = proj.reshape(NB, S, NC)
    dproj_v = dproj.reshape(NB, S, NC)
    da_v = da.reshape(NB, S, AW)
    lse_v = lse.reshape(NB, S, HPG * HD)
    delta_v = delta.reshape(NB, S, HPG * HD)

    def body(q_ref, k_ref, v_ref, cos_ref, sin_ref, da_ref, lse_ref, dl_ref, dp_in_ref, o_ref,
             k_s, v_s, stg, dk_s, dv_s):
        del dp_in_ref
        w = pl.program_id(2)

        def emit():
            def cast(n, carry):
                rows = pl.ds(pl.multiple_of(n * 256, 256), 256)
                o_ref[rows, :] = stg[rows, :].astype(BF16)
                return carry

            lax.fori_loop(0, S // 256, cast, 0)

        @pl.when(w == 0)
        def _():
            k_s[:, pl.ds(0, 128), :] = jnp.zeros((d, 128, HD), BF16)
            v_s[:, pl.ds(0, 128), :] = jnp.zeros((d, 128, HD), BF16)
            dk_s[...] = jnp.zeros_like(dk_s)
            dv_s[...] = jnp.zeros_like(dv_s)

            def prep(i, carry):
                r, n = i // nblk, i % nblk
                rows = _dil_rows(r, n, d)
                dst = pl.ds(pl.multiple_of(n * 128 + 128, 128), 128)
                k_s[r, dst, :] = _rope(k_ref[rows, :], cos_ref[rows, :], sin_ref[rows, :], HD // 2).astype(BF16)
                v_s[r, dst, :] = v_ref[rows, :].astype(BF16)
                return carry

            lax.fori_loop(0, d * nblk, prep, 0, unroll=4)

            def step(i, carry):
                r, n = i // nblk, i % nblk
                rows = _dil_rows(r, n, d)
                r0 = pl.multiple_of(n * 128, 128)
                win = pl.ds(r0, 256)
                cos, sin = cos_ref[rows, :], sin_ref[rows, :]
                qr = _rope(q_ref[rows, :], cos, sin, HD // 2).astype(BF16)
                kw = k_s[r, win, :]
                vw = v_s[r, win, :]
                sc = lax.dot_general(qr, kw, NT, preferred_element_type=F32) * A_SCALE
                sc = jnp.where(_band_mask(r0), sc, NEG)
                p = jnp.exp(sc - lse_ref[rows, :][:, :1])
                da_b = da_ref[rows, :].astype(BF16)
                dp = lax.dot_general(da_b, vw, NT, preferred_element_type=F32)
                ds_b = (p * (dp - dl_ref[rows, :][:, :1]) * A_SCALE).astype(BF16)
                p_b = p.astype(BF16)
                dq = jnp.dot(ds_b, kw, preferred_element_type=F32)
                stg[rows, :] = _rope_t(dq, cos, sin, HD // 2)
                dk_s[r, win, :] += lax.dot_general(ds_b, qr, TN, preferred_element_type=F32)
                dv_s[r, win, :] += lax.dot_general(p_b, da_b, TN, preferred_element_type=F32)
                return carry

            lax.fori_loop(0, d * nblk, step, 0, unroll=4)
            emit()

        @pl.when(w == 1)
        def _():
            def put(i, carry):
                r, n = i // nblk, i % nblk
                rows = _dil_rows(r, n, d)
                src = pl.ds(pl.multiple_of(n * 128 + 128, 128), 128)
                stg[rows, :] = _rope_t(dk_s[r, src, :], cos_ref[rows, :], sin_ref[rows, :], HD // 2)
                return carry

            lax.fori_loop(0, d * nblk, put, 0, unroll=4)
            emit()

        @pl.when(w == 2)
        def _():
            def put(i, carry):
                r, n = i // nblk, i % nblk
                src = pl.ds(pl.multiple_of(n * 128 + 128, 128), 128)
                stg[_dil_rows(r, n, d), :] = dv_s[r, src, :]
                return carry

            lax.fori_loop(0, d * nblk, put, 0, unroll=4)
            emit()

    def col(off):
        return lambda b, h, w: (b, 0, off // HD + HPG * g + h)

    blk = (None, S, HD)
    tab = pl.BlockSpec((S, HD), lambda b, h, w: (0, 0))
    per_head = pl.BlockSpec(blk, lambda b, h, w: (b, 0, h))
    out = pl.pallas_call(
        body, name=f"attn_bwd_d{d}", grid=(NB, HPG, 3),
        in_specs=[pl.BlockSpec(blk, col(QA)), pl.BlockSpec(blk, col(KA)), pl.BlockSpec(blk, col(VA)), tab, tab,
                  pl.BlockSpec(blk, col(0)), per_head, per_head, pl.BlockSpec(memory_space=pl.ANY)],
        out_specs=pl.BlockSpec(blk, lambda b, h, w: (b, 0, (AW // HD) * w + HPG * g + h)),
        out_shape=jax.ShapeDtypeStruct(dproj_v.shape, BF16),
        input_output_aliases={8: 0},
        scratch_shapes=[pltpu.VMEM((d, ln + 128, HD), BF16), pltpu.VMEM((d, ln + 128, HD), BF16),
                        pltpu.VMEM((S, HD), F32), pltpu.VMEM((d, ln + 128, HD), F32), pltpu.VMEM((d, ln + 128, HD), F32)],
        compiler_params=_cp(("arbitrary",) * 3),
    )(proj_v, proj_v, proj_v, cosf, sinf, da_v, lse_v, delta_v, dproj_v)
    return out.reshape(T, NC)


def _attn_mix(proj, os_, ls_):
    tr = 256
    gw = HPG * HD

    def body(o0, o1, o2, l0, l1, l2, z_ref, cat_ref):
        m = jnp.maximum(jnp.maximum(l0[...], l1[...]), l2[...])
        e = [jnp.exp(l[...] - m) for l in (l0, l1, l2)]
        inv = 1.0 / (e[0] + e[1] + e[2])
        for gi, o in enumerate((o0, o1, o2)):
            z = z_ref[:, gi * gw:(gi + 1) * gw]
            cat_ref[:, gi * gw:(gi + 1) * gw] = (o[...] * (e[gi] * inv) * (z * _sigmoid(z))).astype(BF16)

    grp = pl.BlockSpec((tr, gw), lambda i: (i, 0))
    return pl.pallas_call(
        body, name="attn_mix", grid=(T // tr,),
        in_specs=[grp] * 6 + [pl.BlockSpec((tr, AW), lambda i: (i, ZA // AW))],
        out_specs=pl.BlockSpec((tr, AW), lambda i: (i, 0)),
        out_shape=jax.ShapeDtypeStruct((T, D), BF16),
        compiler_params=_cp(("parallel",)),
    )(*os_, *ls_, proj)


def _attn_mix_bwd(dcat, proj, os_, ls_, dep=None):
    tr = 256
    gw = HPG * HD
    deps = [] if dep is None else [dep]

    def body(dy_ref, o0, o1, o2, l0, l1, l2, z_ref, *rest):
        da_ref, lse_ref, dl_ref, dz_ref = rest[len(deps):]
        m = jnp.maximum(jnp.maximum(l0[...], l1[...]), l2[...])
        e = [jnp.exp(l[...] - m) for l in (l0, l1, l2)]
        den = e[0] + e[1] + e[2]
        inv = 1.0 / den
        lse_ref[...] = m + jnp.log(den)
        acc = jnp.zeros((tr, gw), F32)
        for gi, o in enumerate((o0, o1, o2)):
            cols = slice(gi * gw, (gi + 1) * gw)
            z = z_ref[:, cols]
            dy = dy_ref[:, cols]
            sg = _sigmoid(z)
            a = o[...] * (e[gi] * inv)
            da = dy * (z * sg)
            da_ref[:, cols] = da
            dz_ref[:, cols] = (dy * a * (sg * (1.0 + z * (1.0 - sg)))).astype(BF16)
            acc = acc + da * a
        for hh in range(HPG):
            cols = slice(hh * HD, (hh + 1) * HD)
            dl_ref[:, cols] = jnp.broadcast_to(jnp.sum(acc[:, cols], axis=1, keepdims=True), (tr, HD))

    grp = pl.BlockSpec((tr, gw), lambda i: (i, 0))
    return pl.pallas_call(
        body, name="attn_mix_bwd", grid=(T // tr,),
        in_specs=[pl.BlockSpec((tr, AW), lambda i: (i, 0))] + [grp] * 6 + [pl.BlockSpec((tr, AW), lambda i: (i, ZA // AW))]
        + [pl.BlockSpec(memory_space=pl.ANY)] * len(deps),
        out_specs=[pl.BlockSpec((tr, AW), lambda i: (i, 0)), grp, grp, pl.BlockSpec((tr, AW), lambda i: (i, ZA // AW))],
        out_shape=[jax.ShapeDtypeStruct((T, AW), F32), jax.ShapeDtypeStruct((T, gw), F32),
                   jax.ShapeDtypeStruct((T, gw), F32), jax.ShapeDtypeStruct((T, NC), BF16)],
        compiler_params=_cp(("parallel",)),
    )(dcat, *os_, *ls_, proj, *deps)


CT = 256


def _shift_down(x, n):
    rows = lax.broadcasted_iota(jnp.int32, x.shape, 0)
    return jnp.where(rows >= n, pltpu.roll(x, n, 0), 0.0)


def _shift_up(x, n):
    rows = lax.broadcasted_iota(jnp.int32, x.shape, 0)
    return jnp.where(rows < x.shape[0] - n, pltpu.roll(x, x.shape[0] - n, 0), 0.0)


def _conv_fwd(proj, conv_w, cat):
    proj_v = proj.reshape(NB, S, NC)
    cat_v = cat.reshape(NB, S, D)

    def body(u_ref, b_ref, c_ref, z_ref, w_ref, cat_in, o_ref):
        del cat_in
        cu = c_ref[...] * u_ref[...]
        cv = _shift_down(cu, 2) * w_ref[0:1, :] + _shift_down(cu, 1) * w_ref[1:2, :] + cu * w_ref[2:3, :]
        z = z_ref[...]
        o_ref[...] = (b_ref[...] * cv * (z * _sigmoid(z))).astype(BF16)

    def seg(off):
        return pl.BlockSpec((None, S, CT), lambda b, j: (b, 0, off // CT + j))

    out = pl.pallas_call(
        body, name="conv_fwd", grid=(NB, CW // CT),
        in_specs=[seg(UC), seg(BC), seg(CC), seg(ZC), pl.BlockSpec((3, CT), lambda b, j: (0, j)),
                  pl.BlockSpec(memory_space=pl.ANY)],
        out_specs=pl.BlockSpec((None, S, CT), lambda b, j: (b, 0, AW // CT + j)),
        out_shape=jax.ShapeDtypeStruct((NB, S, D), BF16),
        input_output_aliases={5: 0},
        compiler_params=_cp(("parallel", "parallel")),
    )(proj_v, proj_v, proj_v, proj_v, conv_w, cat_v)
    return out.reshape(T, D)


def _conv_bwd(dcat, proj, conv_w, dproj):
    proj_v = proj.reshape(NB, S, NC)
    dproj_v = dproj.reshape(NB, S, NC)
    dcat_v = dcat.reshape(NB, S, D)

    def body(dy_ref, u_ref, b_ref, c_ref, z_ref, w_ref, dp_in, o_ref, dw_ref, st):
        del dp_in
        b = pl.program_id(1)
        w = pl.program_id(2)

        @pl.when((b == 0) & (w == 0))
        def _():
            dw_ref[...] = jnp.zeros_like(dw_ref)

        @pl.when(w == 0)
        def _():
            u, c, z, bb, dy = u_ref[...], c_ref[...], z_ref[...], b_ref[...], dy_ref[...]
            cu = c * u
            s1 = _shift_down(cu, 1)
            s2 = _shift_down(cu, 2)
            cv = s2 * w_ref[0:1, :] + s1 * w_ref[1:2, :] + cu * w_ref[2:3, :]
            sg = _sigmoid(z)
            sz = z * sg
            dcv = dy * bb * sz
            st[0] = dy * cv * sz
            st[2] = dy * bb * cv * (sg * (1.0 + z * (1.0 - sg)))
            dw_ref[0:1, :] += jnp.sum(dcv * s2, axis=0, keepdims=True)
            dw_ref[1:2, :] += jnp.sum(dcv * s1, axis=0, keepdims=True)
            dw_ref[2:3, :] += jnp.sum(dcv * cu, axis=0, keepdims=True)
            dcu = dcv * w_ref[2:3, :] + _shift_up(dcv, 1) * w_ref[1:2, :] + _shift_up(dcv, 2) * w_ref[0:1, :]
            st[1] = dcu * u
            o_ref[...] = (dcu * c).astype(BF16)

        for k in range(3):
            @pl.when(w == k + 1)
            def _(k=k):
                o_ref[...] = st[k].astype(BF16)

    def seg(off):
        return pl.BlockSpec((None, S, CT), lambda j, b, w: (b, 0, off // CT + j))

    out, dw = pl.pallas_call(
        body, name="conv_bwd", grid=(CW // CT, NB, 4),
        in_specs=[pl.BlockSpec((None, S, CT), lambda j, b, w: (b, 0, AW // CT + j)),
                  seg(UC), seg(BC), seg(CC), seg(ZC), pl.BlockSpec((3, CT), lambda j, b, w: (0, j)),
                  pl.BlockSpec(memory_space=pl.ANY)],
        out_specs=[pl.BlockSpec((None, S, CT), lambda j, b, w: (b, 0, (UC + w * CW) // CT + j)),
                   pl.BlockSpec((3, CT), lambda j, b, w: (0, j))],
        out_shape=[jax.ShapeDtypeStruct((NB, S, NC), BF16), jax.ShapeDtypeStruct((3, CW), F32)],
        input_output_aliases={6: 0},
        scratch_shapes=[pltpu.VMEM((3, S, CT), F32)],
        compiler_params=_cp(("arbitrary",) * 3),
    )(dcat_v, proj_v, proj_v, proj_v, proj_v, conv_w, dproj_v)
    return out.reshape(T, NC), dw


XT = 512


def _cross_fwd(proj, mkv, cosq, sinq, cosm, sinm, cat):
    proj_v = proj.reshape(NB, S, NC)
    mkv_v = mkv.reshape(NB, MLEN, 2 * XW)
    cat_v = cat.reshape(NB, S, D)

    def body(q_ref, z_ref, mk_ref, mv_ref, cq, sq, cm, sm, cat_in, o_ref):
        del cat_in
        mkr = _rope(mk_ref[...], cm[...], sm[...], XHD // 2).astype(BF16)
        qr = _rope(q_ref[...], cq[...], sq[...], XHD // 2).astype(BF16)
        sc = lax.dot_general(qr, mkr, NT, preferred_element_type=F32) * X_SCALE
        p = jnp.exp(sc - jnp.max(sc, axis=1, keepdims=True))
        p = p / jnp.sum(p, axis=1, keepdims=True)
        ox = jnp.dot(p.astype(BF16), mv_ref[...].astype(BF16), preferred_element_type=F32)
        z = z_ref[...]
        o_ref[...] = (ox * (z * _sigmoid(z))).astype(BF16)

    def seg(off):
        return pl.BlockSpec((None, XT, XHD), lambda b, h, t: (b, t, off // XHD + h))

    qtab = pl.BlockSpec((XT, XHD), lambda b, h, t: (t, 0))
    mtab = pl.BlockSpec((MLEN, XHD), lambda b, h, t: (0, 0))
    out = pl.pallas_call(
        body, name="cross_fwd", grid=(NB, NXH, S // XT),
        in_specs=[seg(QX), seg(ZX),
                  pl.BlockSpec((None, MLEN, XHD), lambda b, h, t: (b, 0, h)),
                  pl.BlockSpec((None, MLEN, XHD), lambda b, h, t: (b, 0, NXH + h)),
                  qtab, qtab, mtab, mtab, pl.BlockSpec(memory_space=pl.ANY)],
        out_specs=pl.BlockSpec((None, XT, XHD), lambda b, h, t: (b, t, (AW + CW) // XHD + h)),
        out_shape=jax.ShapeDtypeStruct((NB, S, D), BF16),
        input_output_aliases={8: 0},
        compiler_params=_cp(("parallel",) * 3),
    )(proj_v, proj_v, mkv_v, mkv_v, cosq, sinq, cosm, sinm, cat_v)
    return out.reshape(T, D)


def _cross_bwd(dcat, proj, mkv, cosq, sinq, cosm, sinm, dproj):
    proj_v = proj.reshape(NB, S, NC)
    dproj_v = dproj.reshape(NB, S, NC)
    dcat_v = dcat.reshape(NB, S, D)
    mkv_v = mkv.reshape(NB, MLEN, 2 * XW)
    nt = S // XT

    def body(dy_ref, q_ref, z_ref, mk_ref, mv_ref, cq, sq, cm, sm, dp_in, o_ref, dmk_ref, dmv_ref, dz_s):
        del dp_in
        t = pl.program_id(2)
        w = pl.program_id(3)

        @pl.when((t == 0) & (w == 0))
        def _():
            dmk_ref[...] = jnp.zeros_like(dmk_ref)
            dmv_ref[...] = jnp.zeros_like(dmv_ref)

        @pl.when(w == 0)
        def _():
            mkr = _rope(mk_ref[...], cm[...], sm[...], XHD // 2).astype(BF16)
            mv_b = mv_ref[...].astype(BF16)
            qr = _rope(q_ref[...], cq[...], sq[...], XHD // 2).astype(BF16)
            sc = lax.dot_general(qr, mkr, NT, preferred_element_type=F32) * X_SCALE
            p = jnp.exp(sc - jnp.max(sc, axis=1, keepdims=True))
            p = p / jnp.sum(p, axis=1, keepdims=True)
            p_b = p.astype(BF16)
            ox = jnp.dot(p_b, mv_b, preferred_element_type=F32)
            z = z_ref[...]
            dy = dy_ref[...]
            sg = _sigmoid(z)
            dz_s[...] = dy * ox * (sg * (1.0 + z * (1.0 - sg)))
            dox_b = (dy * (z * sg)).astype(BF16)
            dp = lax.dot_general(dox_b, mv_b, NT, preferred_element_type=F32)
            ds_b = (p * (dp - jnp.sum(dp * p, axis=1, keepdims=True)) * X_SCALE).astype(BF16)
            dq = jnp.dot(ds_b, mkr, preferred_element_type=F32)
            o_ref[...] = _rope_t(dq, cq[...], sq[...], XHD // 2).astype(BF16)
            dmk_ref[...] += lax.dot_general(ds_b, qr, TN, preferred_element_type=F32)
            dmv_ref[...] += lax.dot_general(p_b, dox_b, TN, preferred_element_type=F32)

        @pl.when(w == 1)
        def _():
            o_ref[...] = dz_s[...].astype(BF16)

        @pl.when((t == nt - 1) & (w == 1))
        def _():
            dmk_ref[...] = _rope_t(dmk_ref[...], cm[...], sm[...], XHD // 2)

    def seg(off):
        return pl.BlockSpec((None, XT, XHD), lambda b, h, t, w: (b, t, off // XHD + h))

    qtab = pl.BlockSpec((XT, XHD), lambda b, h, t, w: (t, 0))
    mtab = pl.BlockSpec((MLEN, XHD), lambda b, h, t, w: (0, 0))
    macc = pl.BlockSpec((None, MLEN, XHD), lambda b, h, t, w: (b, 0, h))
    out, dmk, dmv = pl.pallas_call(
        body, name="cross_bwd", grid=(NB, NXH, nt, 2),
        in_specs=[pl.BlockSpec((None, XT, XHD), lambda b, h, t, w: (b, t, (AW + CW) // XHD + h)),
                  seg(QX), seg(ZX),
                  pl.BlockSpec((None, MLEN, XHD), lambda b, h, t, w: (b, 0, h)),
                  pl.BlockSpec((None, MLEN, XHD), lambda b, h, t, w: (b, 0, NXH + h)),
                  qtab, qtab, mtab, mtab, pl.BlockSpec(memory_space=pl.ANY)],
        out_specs=[pl.BlockSpec((None, XT, XHD), lambda b, h, t, w: (b, t, (QX + w * XW) // XHD + h)), macc, macc],
        out_shape=[jax.ShapeDtypeStruct((NB, S, NC), BF16), jax.ShapeDtypeStruct((NB, MLEN, XW), F32),
                   jax.ShapeDtypeStruct((NB, MLEN, XW), F32)],
        input_output_aliases={9: 0},
        scratch_shapes=[pltpu.VMEM((XT, XHD), F32)],
        compiler_params=_cp(("arbitrary",) * 4),
    )(dcat_v, proj_v, proj_v, mkv_v, mkv_v, cosq, sinq, cosm, sinm, dproj_v)
    return out.reshape(T, NC), dmk, dmv


def _local_step(x, mem, pre_norm, conv_w, mem_norm, post_norm, tgt, chip, core, comm):
    x2 = x.reshape(T, D)
    mem2 = mem.reshape(NB * MLEN, D)
    tgt2 = tgt.reshape(T, D)
    cosa, sina = _rope_tables(jnp.arange(S), HD // 2)
    cosq, sinq = _rope_tables(jnp.arange(S) + MLEN, XHD // 2)
    cosm, sinm = _rope_tables(jnp.arange(MLEN), XHD // 2)

    h = _rms_fwd(x2, pre_norm, "pre_norm_fwd", dep=comm.gather_started())
    memn = _rms_fwd(mem2, mem_norm, "mem_norm_fwd")
    proj = _proj_part(h, comm.w_in_own(), chip, (0,), "proj_own")
    wg_in, tok = comm.w_in_near(after=proj)
    proj = _proj_part(h, wg_in, chip, (REL_XOR[0], REL_XOR[1]), "proj_near", prev=proj, dep=tok)
    wg_in, tok = comm.w_in_all(after=proj)
    proj = _proj_part(h, wg_in, chip, (REL_XOR[2],), "proj_far", prev=proj, dep=tok)
    wg_kv, wg_out = comm.w_rest(after=proj)
    mkv = _mm_nn(memn, wg_kv, "mkv", NB * MLEN, 1024, 512)

    fw = [_attn_fwd(proj, cosa, sina, g) for g in range(3)]
    os_ = [f[0] for f in fw]
    ls_ = [f[1] for f in fw]
    cat = _attn_mix(proj, os_, ls_)
    cat = _conv_fwd(proj, conv_w, cat)
    cat = _cross_fwd(proj, mkv, cosq, sinq, cosm, sinm, cat)
    y = _mm_nn(cat, wg_out, "out_proj", 1024, 1024, 2048)
    dy, dout, d_post, loss = _post_norm_loss(y, x2, tgt2, post_norm)

    dcat = _mm_nt(dy, wg_out, "dcat", 1024, 1024, 2048)
    g_out = _mm_tn(cat, dy, "grad_w_out", 1024, 1024, 2048)
    red_a = comm.reduce_start([g_out.reshape(NCHIP, D // NCHIP, D)], "a")
    da, lse, delta, dproj = _attn_mix_bwd(dcat, proj, os_, ls_, dep=red_a[-1])
    for g in range(3):
        dproj = _attn_bwd(proj, cosa, sina, da, lse, delta, dproj, g)
    dproj, d_conv = _conv_bwd(dcat, proj, conv_w, dproj)
    dproj, dmk, dmv = _cross_bwd(dcat, proj, mkv, cosq, sinq, cosm, sinm, dproj)

    dmkv = jnp.concatenate([dmk, dmv], axis=-1).reshape(NB * MLEN, 2 * XW)
    dmkv_b = dmkv.astype(BF16)
    g_kv = _mm_tn(memn, dmkv_b, "grad_w_mem_kv", 1024, 1024, NB * MLEN)
    dmemn = _mm_nt(dmkv_b, wg_kv, "dmemn", NB * MLEN, 1024, 512)
    d_mem = _norm_gain_grad(dmemn, mem2, "mem_norm_bwd")

    red_k = comm.reduce_start([g_kv.reshape(NCHIP, D // NCHIP, 2 * XW)], "k")
    g_send = _grad_w_in(h, dproj, 1 - core, BF16, "grad_w_in_send", dep=red_k[-1])
    sent = comm.sibling_start(g_send.reshape(NCHIP, D // 2, WB))
    g_keep = _grad_w_in(h, dproj, core, F32, "grad_w_in_keep", dep=sent[-1])
    red_b = comm.reduce_start_summed(g_keep.reshape(NCHIP, D // 2, WB), sent, "b")
    dh = _dh(dproj, wg_in, dep=red_b[-1])
    grad_x, d_pre = _pre_norm_bwd(dh, x2, pre_norm, dout)
    (r_out,) = comm.reduce_finish(red_a, "a", after=grad_x)
    (r_kv,) = comm.reduce_finish(red_k, "k", after=grad_x)
    (r_in,) = comm.reduce_finish(red_b, "b", after=grad_x)
    return loss, grad_x.reshape(NB, S, D), d_pre, d_mem, d_post, d_conv, r_in, r_kv, r_out


def _adamw(w, g, m, v, name):
    rows, cols = w.shape
    tr = rows if rows <= 512 else 512
    tc = cols if cols <= 1024 else 1024
    if cols % tc:
        tc = 896

    def body(w_ref, g_ref, m_ref, v_ref, go_ref, d_ref, nm_ref, nv_ref):
        gv = g_ref[...]
        go_ref[...] = gv
        nm = ADAM_B1 * m_ref[...] + (1.0 - ADAM_B1) * gv
        nv = ADAM_B2 * v_ref[...] + (1.0 - ADAM_B2) * (gv * gv)
        m_hat = nm / (1.0 - ADAM_B1 ** ADAM_STEP)
        v_hat = nv / (1.0 - ADAM_B2 ** ADAM_STEP)
        d_ref[...] = -ADAM_LR * (m_hat / (jnp.sqrt(v_hat) + ADAM_EPS) + ADAM_WD * w_ref[...])
        nm_ref[...] = nm
        nv_ref[...] = nv

    blk = pl.BlockSpec((tr, tc), lambda i, j: (i, j))
    sds = jax.ShapeDtypeStruct((rows, cols), F32)
    return pl.pallas_call(
        body, name=name, grid=(rows // tr, cols // tc),
        in_specs=[blk] * 4, out_specs=[blk] * 4, out_shape=[sds] * 4,
        compiler_params=_cp(("parallel", "parallel")),
    )(w, g, m, v)


def _place():
    return lax.axis_index("x"), lax.axis_index("y"), lax.axis_index("c")


def _other_chips(x, y):
    return [(1 - x, y), (x, 1 - y), (1 - x, 1 - y)]


def _tile_cols(cols):
    return cols if cols <= 1024 else (1024 if cols % 1024 == 0 else 896)


def _cast_own(w, chip, name):
    rows, cols = w.shape
    tr, tc = 512, _tile_cols(cols)

    def body(chip_ref, w_ref, o_ref):
        del chip_ref
        o_ref[...] = w_ref[...].astype(BF16)

    grid_spec = pltpu.PrefetchScalarGridSpec(
        num_scalar_prefetch=1, grid=(rows // tr, cols // tc),
        in_specs=[pl.BlockSpec((tr, tc), lambda i, j, cr: (i, j))],
        out_specs=pl.BlockSpec((None, tr, tc), lambda i, j, cr: (cr[0], i, j)))
    return pl.pallas_call(
        body, name=name, grid_spec=grid_spec,
        out_shape=jax.ShapeDtypeStruct((NCHIP, rows, cols), BF16),
        compiler_params=_cp(("parallel", "parallel")),
    )(_as_index(chip), w)


HBM_SPEC = pl.BlockSpec(memory_space=pltpu.HBM)
SEM_SPEC = pl.BlockSpec(memory_space=pltpu.SEMAPHORE)
ANY_SPEC = pl.BlockSpec(memory_space=pl.ANY)
EFFECT = pltpu.SideEffectType.DATAFLOW_SIDE_EFFECTING
TOKEN = jax.ShapeDtypeStruct((8, 128), F32)


def _half(ref, chip, hc):
    hr = ref.shape[1] // 2
    return ref.at[chip, pl.ds(hc * hr, hr), :]


NEAR = (0, 1)
FAR = (2,)
REL_XOR = (2, 1, 3)


def _gather_copies(refs, send_sems, recv_sems, rels):
    x, y, c = _place()
    chips = _other_chips(x, y)
    out, inc = [], []
    for a, ref in enumerate(refs):
        for p, j in enumerate(rels):
            px, py = chips[j]
            mine = _half(ref, 2 * x + y, c)
            theirs = _half(ref, 2 * px + py, c)
            sems = dict(send_sem=send_sems.at[len(rels) * a + p], recv_sem=recv_sems.at[len(rels) * a + p],
                        device_id=(px, py, c), device_id_type=MESH)
            out.append(pltpu.make_async_remote_copy(src_ref=mine, dst_ref=mine, **sems))
            inc.append(pltpu.make_async_remote_copy(src_ref=theirs, dst_ref=theirs, **sems))
    return out, inc


def _gather_start(bufs, groups, name):
    n = len(bufs)
    ng = len(groups)

    def body(*refs):
        ins = refs[:n]
        token = refs[-1]
        for gi, rels in enumerate(groups):
            out, _ = _gather_copies(ins, refs[n + 2 * gi], refs[n + 2 * gi + 1], rels)
            for cp in out:
                cp.start()
        token[...] = jnp.zeros_like(token)

    sems = []
    for rels in groups:
        sems += [pltpu.SemaphoreType.DMA((len(rels) * n,))] * 2
    res = pl.pallas_call(
        body, name=name,
        in_specs=[HBM_SPEC] * n,
        out_specs=[SEM_SPEC] * (2 * ng) + [HBM_SPEC] * n + [pl.BlockSpec(memory_space=pltpu.VMEM)],
        out_shape=sems + [pltpu.HBM(b.shape, b.dtype) for b in bufs] + [TOKEN],
        input_output_aliases={a: 2 * ng + a for a in range(n)},
        compiler_params=pltpu.CompilerParams(has_side_effects=EFFECT),
    )(*[pltpu.with_memory_space_constraint(b, pltpu.HBM) for b in bufs])
    return [(res[2 * gi], res[2 * gi + 1]) for gi in range(ng)], list(res[2 * ng:2 * ng + n]), res[-1]


def _gather_wait(bufs, sems, rels, after, name):
    n = len(bufs)
    send_sems, recv_sems = sems

    def body(*refs):
        ins = refs[:n]
        out, inc = _gather_copies(ins, refs[n], refs[n + 1], rels)
        for cp in out:
            cp.wait_send()
        for cp in inc:
            cp.wait_recv()

    return pl.pallas_call(
        body, name=name,
        in_specs=[HBM_SPEC] * n + [SEM_SPEC, SEM_SPEC, ANY_SPEC],
        out_specs=[HBM_SPEC] * n,
        out_shape=[pltpu.HBM(b.shape, b.dtype) for b in bufs],
        input_output_aliases={a: a for a in range(n)},
        compiler_params=pltpu.CompilerParams(has_side_effects=EFFECT),
    )(*bufs, send_sems, recv_sems, after)


def _forward_halves(bufs, rels, name):
    n = len(bufs)

    def body(*refs):
        outs = refs[n:2 * n]
        send_sems, recv_sems = refs[2 * n:]
        x, y, c = _place()
        chips = _other_chips(x, y)
        cps, waits = [], []
        for a in range(n):
            for p, j in enumerate(rels):
                px, py = chips[j]
                sems = dict(send_sem=send_sems.at[len(rels) * a + p], recv_sem=recv_sems.at[len(rels) * a + p],
                            device_id=(x, y, 1 - c), device_id_type=MESH)
                got = _half(outs[a], 2 * px + py, c)
                want = _half(outs[a], 2 * px + py, 1 - c)
                cps.append(pltpu.make_async_remote_copy(src_ref=got, dst_ref=got, **sems))
                waits.append(pltpu.make_async_remote_copy(src_ref=want, dst_ref=want, **sems))
        for cp in cps:
            cp.start()
        for cp in waits:
            cp.wait_recv()
        for cp in cps:
            cp.wait_send()

    return pl.pallas_call(
        body, name=name,
        in_specs=[ANY_SPEC] * n, out_specs=[ANY_SPEC] * n,
        out_shape=[jax.ShapeDtypeStruct(s.shape, s.dtype) for s in bufs],
        input_output_aliases={a: a for a in range(n)},
        scratch_shapes=[pltpu.SemaphoreType.DMA((len(rels) * n,)), pltpu.SemaphoreType.DMA((len(rels) * n,))],
    )(*bufs)


def _sibling_halves(parts, name):
    n = len(parts)

    def body(*refs):
        ins, outs = refs[:n], refs[n:2 * n]
        send_sems, recv_sems = refs[2 * n:]
        x, y, c = _place()
        cps = []
        for a in range(n):
            hr = ins[a].shape[1] // 2
            cps.append(pltpu.make_async_remote_copy(
                src_ref=ins[a].at[:, pl.ds((1 - c) * hr, hr), :], dst_ref=outs[a],
                send_sem=send_sems.at[a], recv_sem=recv_sems.at[a], device_id=(x, y, 1 - c), device_id_type=MESH))
        for cp in cps:
            cp.start()
        for cp in cps:
            cp.wait()

    hbm = pl.BlockSpec(memory_space=pl.ANY)
    return pl.pallas_call(
        body, name=name,
        in_specs=[hbm] * n, out_specs=[hbm] * n,
        out_shape=[jax.ShapeDtypeStruct((NCHIP, p.shape[1] // 2, p.shape[2]), p.dtype) for p in parts],
        scratch_shapes=[pltpu.SemaphoreType.DMA((n,)), pltpu.SemaphoreType.DMA((n,))],
    )(*parts)


def _sibling_copy(src, land, send_sems, recv_sems):
    x, y, c = _place()
    return pltpu.make_async_remote_copy(src_ref=src, dst_ref=land, send_sem=send_sems.at[0], recv_sem=recv_sems.at[0],
                                        device_id=(x, y, 1 - c), device_id_type=MESH)


def _sibling_start(part, name):
    def body(src, land, send_sems, recv_sems, src_thru, land_thru, token):
        _sibling_copy(src, land, send_sems, recv_sems).start()
        token[...] = jnp.zeros_like(token)

    land = lax.empty(part.shape, part.dtype)
    return pl.pallas_call(
        body, name=name,
        in_specs=[HBM_SPEC] * 2,
        out_specs=[SEM_SPEC, SEM_SPEC, HBM_SPEC, HBM_SPEC, pl.BlockSpec(memory_space=pltpu.VMEM)],
        out_shape=[pltpu.SemaphoreType.DMA((1,)), pltpu.SemaphoreType.DMA((1,)), pltpu.HBM(part.shape, part.dtype),
                   pltpu.HBM(part.shape, part.dtype), TOKEN],
        input_output_aliases={0: 2, 1: 3},
        compiler_params=pltpu.CompilerParams(has_side_effects=EFFECT),
    )(pltpu.with_memory_space_constraint(part, pltpu.HBM), pltpu.with_memory_space_constraint(land, pltpu.HBM))


def _sibling_wait(state, after, name):
    send_sems, recv_sems, part, land, _ = state

    def body(src, land_ref, send_ref, recv_ref, after_ref, src_thru, land_thru):
        cp = _sibling_copy(src, land_ref, send_ref, recv_ref)
        cp.wait_send()
        cp.wait_recv()

    return pl.pallas_call(
        body, name=name,
        in_specs=[HBM_SPEC, HBM_SPEC, SEM_SPEC, SEM_SPEC, ANY_SPEC],
        out_specs=[HBM_SPEC, HBM_SPEC],
        out_shape=[pltpu.HBM(part.shape, part.dtype), pltpu.HBM(land.shape, land.dtype)],
        input_output_aliases={0: 0, 1: 1},
        compiler_params=pltpu.CompilerParams(has_side_effects=EFFECT),
    )(part, land, send_sems, recv_sems, after)[1]


def _pair_sum_rows(keep, got, name):
    nblk, rows, cols = keep.shape
    tr, tc = 512, _tile_cols(cols)

    def body(k_ref, g_ref, o_ref):
        o_ref[...] = (k_ref[...] + g_ref[...].astype(F32)).astype(BF16)

    blk = pl.BlockSpec((None, tr, tc), lambda b, i, j: (b, i, j))
    return pl.pallas_call(
        body, name=name, grid=(nblk, rows // tr, cols // tc),
        in_specs=[blk, blk], out_specs=blk,
        out_shape=jax.ShapeDtypeStruct(keep.shape, BF16),
        compiler_params=_cp(("parallel",) * 3),
    )(keep, got)


def _pair_sum(part, got, name):
    nblk, rows, cols = part.shape
    hr = rows // 2
    tr = 512
    tc = cols if cols <= 1024 else (1024 if cols % 1024 == 0 else 896)
    c = lax.axis_index("c")
    nrt = hr // tr

    def body(c_ref, p_ref, g_ref, o_ref):
        del c_ref
        o_ref[...] = (p_ref[...] + g_ref[...]).astype(BF16)

    grid_spec = pltpu.PrefetchScalarGridSpec(
        num_scalar_prefetch=1, grid=(nblk, nrt, cols // tc),
        in_specs=[pl.BlockSpec((None, tr, tc), lambda b, i, j, cr: (b, cr[0] * nrt + i, j)),
                  pl.BlockSpec((None, tr, tc), lambda b, i, j, cr: (b, i, j))],
        out_specs=pl.BlockSpec((None, tr, tc), lambda b, i, j, cr: (b, i, j)))
    return pl.pallas_call(
        body, name=name, grid_spec=grid_spec,
        out_shape=jax.ShapeDtypeStruct((nblk, hr, cols), BF16),
        compiler_params=_cp(("parallel",) * 3),
    )(jnp.reshape(c, (1,)).astype(jnp.int32), part, got)


def _scatter_start(qs, name):
    n = len(qs)

    def body(*refs):
        ins, lands = refs[:n], refs[n:2 * n]
        token = refs[-1]
        for cp in _scatter_copies(ins, lands, refs[2 * n], refs[2 * n + 1]):
            cp.start()
        token[...] = jnp.zeros_like(token)

    lands = [lax.empty((3,) + q.shape[1:], q.dtype) for q in qs]
    res = pl.pallas_call(
        body, name=name,
        in_specs=[HBM_SPEC] * (2 * n),
        out_specs=[SEM_SPEC, SEM_SPEC] + [HBM_SPEC] * (2 * n) + [pl.BlockSpec(memory_space=pltpu.VMEM)],
        out_shape=[pltpu.SemaphoreType.DMA((3 * n,)), pltpu.SemaphoreType.DMA((3 * n,))]
        + [pltpu.HBM(b.shape, b.dtype) for b in qs + lands] + [TOKEN],
        input_output_aliases={a: 2 + a for a in range(2 * n)},
        compiler_params=pltpu.CompilerParams(has_side_effects=EFFECT),
    )(*[pltpu.with_memory_space_constraint(b, pltpu.HBM) for b in qs + lands])
    return res[0], res[1], list(res[2:2 + n]), list(res[2 + n:2 + 2 * n]), res[-1]


def _scatter_copies(ins, lands, send_sems, recv_sems):
    x, y, c = _place()
    cps = []
    for a in range(len(ins)):
        for j, (px, py) in enumerate(_other_chips(x, y)):
            cps.append(pltpu.make_async_remote_copy(
                src_ref=ins[a].at[2 * px + py], dst_ref=lands[a].at[j],
                send_sem=send_sems.at[3 * a + j], recv_sem=recv_sems.at[3 * a + j], device_id=(px, py, c), device_id_type=MESH))
    return cps


def _scatter_wait(qs, lands, send_sems, recv_sems, after, name):
    n = len(qs)

    def body(*refs):
        for cp in _scatter_copies(refs[:n], refs[n:2 * n], refs[2 * n], refs[2 * n + 1]):
            cp.wait_send()
            cp.wait_recv()

    res = pl.pallas_call(
        body, name=name,
        in_specs=[HBM_SPEC] * (2 * n) + [SEM_SPEC, SEM_SPEC, ANY_SPEC],
        out_specs=[HBM_SPEC] * (2 * n),
        out_shape=[pltpu.HBM(b.shape, b.dtype) for b in qs + lands],
        input_output_aliases={a: a for a in range(2 * n)},
        compiler_params=pltpu.CompilerParams(has_side_effects=EFFECT),
    )(*qs, *lands, send_sems, recv_sems, after)
    return list(res[:n]), list(res[n:])


def _chip_sum(q, got, name):
    _, hr, cols = got.shape
    tr, tc = 512, _tile_cols(cols)
    chip = 2 * lax.axis_index("x") + lax.axis_index("y")
    c = lax.axis_index("c")

    def body(idx_ref, q_ref, g_ref, o_ref):
        del idx_ref
        acc = q_ref[...].astype(F32)
        for i in range(3):
            acc = acc + g_ref[i].astype(F32)
        o_ref[...] = acc

    grid_spec = pltpu.PrefetchScalarGridSpec(
        num_scalar_prefetch=1, grid=(hr // tr, cols // tc),
        in_specs=[pl.BlockSpec((None, tr, tc), lambda i, j, ix: (ix[0], i, j)),
                  pl.BlockSpec((3, tr, tc), lambda i, j, ix: (0, i, j))],
        out_specs=pl.BlockSpec((None, tr, tc), lambda i, j, ix: (ix[1], i, j)))
    return pl.pallas_call(
        body, name=name, grid_spec=grid_spec,
        out_shape=jax.ShapeDtypeStruct((2, hr, cols), F32),
        compiler_params=_cp(("parallel", "parallel")),
    )(jnp.stack([chip, c]).astype(jnp.int32), q, got)


def _join_halves(bufs, name):
    n = len(bufs)

    def body(*refs):
        outs = refs[n:2 * n]
        send_sems, recv_sems = refs[2 * n:]
        x, y, c = _place()
        cps = []
        for a in range(n):
            cps.append(pltpu.make_async_remote_copy(src_ref=outs[a].at[c], dst_ref=outs[a].at[c], send_sem=send_sems.at[a],
                                                    recv_sem=recv_sems.at[a], device_id=(x, y, 1 - c), device_id_type=MESH))
        for cp in cps:
            cp.start()
        for a in range(n):
            theirs = outs[a].at[1 - c]
            pltpu.make_async_remote_copy(src_ref=theirs, dst_ref=theirs, send_sem=send_sems.at[a], recv_sem=recv_sems.at[a],
                                         device_id=(x, y, 1 - c), device_id_type=MESH).wait_recv()
        for cp in cps:
            cp.wait_send()

    hbm = pl.BlockSpec(memory_space=pl.ANY)
    return pl.pallas_call(
        body, name=name,
        in_specs=[hbm] * n, out_specs=[hbm] * n,
        out_shape=[jax.ShapeDtypeStruct(b.shape, b.dtype) for b in bufs],
        input_output_aliases={a: a for a in range(n)},
        scratch_shapes=[pltpu.SemaphoreType.DMA((n,)), pltpu.SemaphoreType.DMA((n,))],
    )(*bufs)


N_DEV = 8


def _gather_small(block, reduce, name):
    m_per, cols = block.shape

    def body(x_ref, out_ref, all_ref, send_sems, recv_sems, local_sem):
        x, y, c = _place()
        me, sibling = (x, y, c), (x, y, 1 - c)
        chips = _other_chips(x, y)

        def rows(px, py, pc):
            return all_ref.at[pl.ds((4 * px + 2 * py + pc) * m_per, m_per), :]

        def copy(k, block_of, to, src=None):
            return pltpu.make_async_remote_copy(
                src_ref=rows(*block_of) if src is None else src, dst_ref=rows(*block_of),
                send_sem=send_sems.at[k], recv_sem=recv_sems.at[k], device_id=to, device_id_type=MESH)

        mine = pltpu.make_async_copy(x_ref, rows(*me), local_sem)
        mine.start()
        first = [copy(0, me, sibling, src=x_ref)]
        first += [copy(1 + j, me, (*chip, c), src=x_ref) for j, chip in enumerate(chips)]
        for cp in first:
            cp.start()
        passed = [copy(4 + j, (*chip, c), sibling) for j, chip in enumerate(chips)]
        for j, chip in enumerate(chips):
            copy(1 + j, (*chip, c), me).wait_recv()
            passed[j].start()
        copy(0, sibling, me).wait_recv()
        for j, chip in enumerate(chips):
            copy(4 + j, (*chip, 1 - c), me).wait_recv()
        for cp in first + passed:
            cp.wait_send()
        mine.wait()
        if reduce:
            acc = all_ref[pl.ds(0, m_per), :]
            for i in range(1, N_DEV):
                acc = acc + all_ref[pl.ds(i * m_per, m_per), :]
            out_ref[...] = acc
        else:
            out_ref[...] = all_ref[...]

    out_rows = m_per if reduce else N_DEV * m_per
    return pl.pallas_call(
        body, name=name,
        in_specs=[pl.BlockSpec(memory_space=pltpu.VMEM)],
        out_specs=pl.BlockSpec(memory_space=pltpu.VMEM),
        out_shape=jax.ShapeDtypeStruct((out_rows, cols), F32),
        scratch_shapes=[pltpu.VMEM((N_DEV * m_per, cols), F32), pltpu.SemaphoreType.DMA((7,)),
                        pltpu.SemaphoreType.DMA((7,)), pltpu.SemaphoreType.DMA],
    )(block)


class _Comm:
    def __init__(self, w_in, w_kv, w_out, chip):
        self.bufs = [_cast_own(w_in, chip, "cast_w_in"), _cast_own(w_kv, chip, "cast_w_mem_kv"),
                     _cast_own(w_out, chip, "cast_w_out")]


    def gather_started(self):
        (self.sems,), (self.b_in,), tok = _gather_start(self.bufs[:1], (NEAR,), "gather_start_in_near")
        return [tok]

    def w_in_own(self):
        return self.b_in.reshape(NCHIP * D, WB)

    def w_in_near(self, after):
        (b_in,) = _gather_wait([self.b_in], self.sems, NEAR, after, "gather_wait_in_near")
        (b_in,) = _forward_halves([b_in], NEAR, "forward_in_near")
        (self.sems,), (self.b_in,), tok = _gather_start([b_in], (FAR,), "gather_start_in_far")
        return self.b_in.reshape(NCHIP * D, WB), tok

    def w_in_all(self, after):
        (b_in,) = _gather_wait([self.b_in], self.sems, FAR, after, "gather_wait_in_far")
        (b_in,) = _forward_halves([b_in], FAR, "forward_in_far")
        (self.sems,), self.b_rest, tok = _gather_start(self.bufs[1:], (NEAR + FAR,), "gather_start_rest")
        return b_in.reshape(NCHIP * D, WB), tok

    def w_rest(self, after):
        b_rest = _gather_wait(self.b_rest, self.sems, NEAR + FAR, after, "gather_wait_rest")
        b_kv, b_out = _forward_halves(b_rest, NEAR + FAR, "forward_rest")
        return b_kv.reshape(D, 2 * XW), b_out.reshape(D, D)

    def reduce_start(self, parts, tag):
        got = _sibling_halves(parts, "sibling_halves_" + tag)
        qs = [_pair_sum(p, g, f"pair_sum_{tag}{i}") for i, (p, g) in enumerate(zip(parts, got))]
        return _scatter_start(qs, "scatter_start_" + tag)

    def sibling_start(self, send):
        return _sibling_start(send, "sibling_start_b")

    def reduce_start_summed(self, keep, sent, tag):
        got = _sibling_wait(sent, keep, "sibling_wait_" + tag)
        return _scatter_start([_pair_sum_rows(keep, got, "pair_sum_" + tag)], "scatter_start_" + tag)

    def reduce_finish(self, state, tag, after):
        send_sems, recv_sems, qs, lands, _ = state
        qs, lands = _scatter_wait(qs, lands, send_sems, recv_sems, after, "scatter_wait_" + tag)
        halves = [_chip_sum(q, l, f"chip_sum_{tag}{i}") for i, (q, l) in enumerate(zip(qs, lands))]
        return [j.reshape(2 * j.shape[1], j.shape[2]) for j in _join_halves(halves, "join_halves_" + tag)]


def kernel(x, mem, pre_norm, w_in, conv_w, mem_norm, w_mem_kv, w_out, post_norm, loss_target, m_pre_norm, m_w_in, m_conv_w, m_mem_norm, m_w_mem_kv, m_w_out, m_post_norm, v_pre_norm, v_w_in, v_conv_w, v_mem_norm, v_w_mem_kv, v_w_out, v_post_norm):
    chip = 2 * lax.axis_index("x") + lax.axis_index("y")

    cw_blk = jnp.zeros((8, 384), F32).at[:3].set(conv_w[0])
    cw_all = _gather_small(cw_blk, False, "gather_conv_w").reshape(NCHIP, 2, 8, 384)[:, 0, :3]
    conv_full = jnp.transpose(cw_all, (1, 0, 2)).reshape(3, CW)

    comm = _Comm(w_in[0], w_mem_kv[0], w_out[0], chip)
    loss, grad_x, d_pre, d_mem, d_post, d_conv, r_in, r_kv, r_out = _local_step(
        x, mem, pre_norm, conv_full, mem_norm, post_norm, loss_target, chip, lax.axis_index("c"), comm)

    pack = jnp.concatenate([d_pre, d_mem, d_post, jnp.pad(d_conv, ((0, 0), (0, D - CW))),
                            jnp.pad(loss, ((0, 0), (0, D - 128))), jnp.zeros((1, D), F32)], axis=0)
    tot = _gather_small(pack, True, "reduce_small")
    g_pre, g_mem, g_post = tot[0:1], tot[1:2], tot[2:3]
    g_conv = lax.dynamic_slice(tot[3:6, :CW], (0, chip * 384), (3, 384))
    loss_out = tot[6, 0]

    names = ("pre_norm", "w_in", "conv_w", "mem_norm", "w_mem_kv", "w_out", "post_norm")
    ws = (pre_norm, w_in[0], conv_w[0], mem_norm, w_mem_kv[0], w_out[0], post_norm)
    gs = (g_pre, r_in, g_conv, g_mem, r_kv, r_out, g_post)
    ms = (m_pre_norm, m_w_in[0], m_conv_w[0], m_mem_norm, m_w_mem_kv[0], m_w_out[0], m_post_norm)
    vs = (v_pre_norm, v_w_in[0], v_conv_w[0], v_mem_norm, v_w_mem_kv[0], v_w_out[0], v_post_norm)
    upd = [_adamw(w, g, m, v, "adamw_" + nm) for nm, w, g, m, v in zip(names, ws, gs, ms, vs)]

    def shaped(arrs):
        return [a.reshape(w.shape) if w.ndim == a.ndim else a.reshape((1,) + a.shape)
                for a, w in zip(arrs, (pre_norm, w_in, conv_w, mem_norm, w_mem_kv, w_out, post_norm))]

    grads = shaped([u[0] for u in upd])
    deltas = shaped([u[1] for u in upd])
    new_m = shaped([u[2] for u in upd])
    new_v = shaped([u[3] for u in upd])
    return (loss_out, grad_x, *grads, *deltas, *new_m, *new_v)
```

```python
import functools

import jax
import jax.numpy as jnp
from jax import lax
from jax.experimental import pallas as pl
from jax.experimental.pallas import tpu as pltpu

F32 = jnp.float32
BF16 = jnp.bfloat16

D = 4096
S = 2048
NB = 2
T = NB * S
MLEN = 256
HD = 128
AW = 1536
CW = 1536
XW = 1024
XHD = 256
NXH = 4
NC = 14336
QA, KA, VA, ZA, UC, BC, CC, ZC, QX, ZX = 0, 1536, 3072, 4608, 6144, 7680, 9216, 10752, 12288, 13312
NCHIP = 4
WB = NC // NCHIP
DIL = (1, 4, 16)
HPG = 4
EPS = 1e-6
NEG = -1e30
ROPE_THETA = 10000.0
A_SCALE = HD ** -0.5
X_SCALE = XHD ** -0.5

ADAM_LR = 0.001
ADAM_B1 = 0.9
ADAM_B2 = 0.999
ADAM_EPS = 1e-08
ADAM_WD = 0.01
ADAM_STEP = 10

MESH = pl.DeviceIdType.MESH
MIB = 1024 * 1024


def _cp(sem, vmem_mib=48):
    return pltpu.CompilerParams(dimension_semantics=sem, vmem_limit_bytes=vmem_mib * MIB)


def _sigmoid(z):
    return 1.0 / (1.0 + jnp.exp(-z))


def _rope(x, cos, sin, half):
    return x * cos + pltpu.roll(x, half, 1) * sin


def _rope_t(g, cos, sin, half):
    return g * cos + pltpu.roll(g * sin, half, 1)


def _rms_fwd(x2, g, name, dep=None):
    rows = x2.shape[0]
    tr = 256
    deps = [] if dep is None else list(dep)

    def body(x_ref, g_ref, *rest):
        o_ref = rest[-1]
        x = x_ref[...]
        r = lax.rsqrt(jnp.mean(x * x, axis=-1, keepdims=True) + EPS)
        o_ref[...] = (x * r * g_ref[...]).astype(BF16)

    return pl.pallas_call(
        body, name=name, grid=(rows // tr,),
        in_specs=[pl.BlockSpec((tr, D), lambda i: (i, 0)), pl.BlockSpec((1, D), lambda i: (0, 0))]
        + [pl.BlockSpec(memory_space=pl.ANY)] * len(deps),
        out_specs=pl.BlockSpec((tr, D), lambda i: (i, 0)),
        out_shape=jax.ShapeDtypeStruct((rows, D), BF16),
        compiler_params=_cp(("parallel",)),
    )(x2, g, *deps)


def _norm_gain_grad(dn, x2, name):
    rows = x2.shape[0]
    tr = 256

    def body(dn_ref, x_ref, dg_ref):
        @pl.when(pl.program_id(0) == 0)
        def _():
            dg_ref[...] = jnp.zeros_like(dg_ref)
        x = x_ref[...]
        r = lax.rsqrt(jnp.mean(x * x, axis=-1, keepdims=True) + EPS)
        dg_ref[...] += jnp.sum(dn_ref[...] * (x * r), axis=0, keepdims=True)

    return pl.pallas_call(
        body, name=name, grid=(rows // tr,),
        in_specs=[pl.BlockSpec((tr, D), lambda i: (i, 0)), pl.BlockSpec((tr, D), lambda i: (i, 0))],
        out_specs=pl.BlockSpec((1, D), lambda i: (0, 0)),
        out_shape=jax.ShapeDtypeStruct((1, D), F32),
        compiler_params=_cp(("arbitrary",)),
    )(dn, x2)


def _pre_norm_bwd(dh, x2, g, dout):
    tr = 256

    def body(dh_ref, x_ref, g_ref, dout_ref, gx_ref, dg_ref):
        @pl.when(pl.program_id(0) == 0)
        def _():
            dg_ref[...] = jnp.zeros_like(dg_ref)
        x = x_ref[...]
        dh_ = dh_ref[...]
        r = lax.rsqrt(jnp.mean(x * x, axis=-1, keepdims=True) + EPS)
        xhat = x * r
        dg_ref[...] += jnp.sum(dh_ * xhat, axis=0, keepdims=True)
        dxn = dh_ * g_ref[...]
        gx_ref[...] = dout_ref[...] + r * (dxn - xhat * jnp.mean(dxn * xhat, axis=-1, keepdims=True))

    row = pl.BlockSpec((tr, D), lambda i: (i, 0))
    vec = pl.BlockSpec((1, D), lambda i: (0, 0))
    return pl.pallas_call(
        body, name="pre_norm_bwd", grid=(T // tr,),
        in_specs=[row, row, vec, row],
        out_specs=[row, vec],
        out_shape=[jax.ShapeDtypeStruct((T, D), F32), jax.ShapeDtypeStruct((1, D), F32)],
        compiler_params=_cp(("arbitrary",)),
    )(dh, x2, g, dout)


def _post_norm_loss(y, x2, tgt, g):
    tr = 256

    def body(y_ref, x_ref, t_ref, g_ref, dy_ref, dout_ref, dg_ref, loss_ref):
        @pl.when(pl.program_id(0) == 0)
        def _():
            dg_ref[...] = jnp.zeros_like(dg_ref)
            loss_ref[...] = jnp.zeros_like(loss_ref)
        yv = y_ref[...]
        gv = g_ref[...]
        r = lax.rsqrt(jnp.mean(yv * yv, axis=-1, keepdims=True) + EPS)
        yhat = yv * r
        err = x_ref[...] + yhat * gv - t_ref[...]
        loss_ref[...] += jnp.sum(jnp.sum(err * err, axis=1, keepdims=True), axis=0, keepdims=True) * (0.5 / D)
        dout = err * (1.0 / D)
        dout_ref[...] = dout
        dg_ref[...] += jnp.sum(dout * yhat, axis=0, keepdims=True)
        dyn = dout * gv
        dy_ref[...] = (r * (dyn - yhat * jnp.mean(dyn * yhat, axis=-1, keepdims=True))).astype(BF16)

    row = pl.BlockSpec((tr, D), lambda i: (i, 0))
    vec = pl.BlockSpec((1, D), lambda i: (0, 0))
    return pl.pallas_call(
        body, name="post_norm_loss", grid=(T // tr,),
        in_specs=[row, row, row, vec],
        out_specs=[row, row, vec, pl.BlockSpec((1, 128), lambda i: (0, 0))],
        out_shape=[jax.ShapeDtypeStruct((T, D), BF16), jax.ShapeDtypeStruct((T, D), F32),
                   jax.ShapeDtypeStruct((1, D), F32), jax.ShapeDtypeStruct((1, 128), F32)],
        compiler_params=_cp(("arbitrary",)),
    )(y, x2, tgt, g)


NN = (((1,), (0,)), ((), ()))
NT = (((1,), (1,)), ((), ()))
TN = (((0,), (0,)), ((), ()))


def _as_index(v):
    return jnp.reshape(v, (1,)).astype(jnp.int32)


def _matmul(a, b, *, name, dims, grid, a_block, a_map, b_block, b_map, o_block, o_map, out_shape, out_dtype=F32,
            index=None, prev=None, deps=()):
    extra = ([] if prev is None else [prev]) + [d for d in deps if d is not None]
    first = 0 if index is None else 1
    nk = grid[2]
    in_place = out_dtype == F32

    def body(*refs):
        a_ref, b_ref, o_ref = refs[first], refs[first + 1], refs[first + 2 + len(extra)]
        acc_ref = o_ref if in_place else refs[-1]

        @pl.when(pl.program_id(2) == 0)
        def _():
            acc_ref[...] = jnp.zeros_like(acc_ref)

        acc_ref[...] += lax.dot_general(a_ref[...], b_ref[...], dims, preferred_element_type=F32)

        if not in_place:
            @pl.when(pl.program_id(2) == nk - 1)
            def _():
                o_ref[...] = acc_ref[...].astype(o_ref.dtype)

    in_specs = [pl.BlockSpec(a_block, a_map), pl.BlockSpec(b_block, b_map)] + [pl.BlockSpec(memory_space=pl.ANY)] * len(extra)
    out_specs = pl.BlockSpec(o_block, o_map)
    scratch = [] if in_place else [pltpu.VMEM(o_block, F32)]
    kwargs = dict(name=name, out_shape=jax.ShapeDtypeStruct(out_shape, out_dtype),
                  input_output_aliases={} if prev is None else {first + 2: 0},
                  compiler_params=_cp(("parallel", "parallel", "arbitrary"), vmem_mib=56))
    if index is None:
        return pl.pallas_call(body, grid=grid, in_specs=in_specs, out_specs=out_specs, scratch_shapes=scratch,
                              **kwargs)(a, b, *extra)
    grid_spec = pltpu.PrefetchScalarGridSpec(num_scalar_prefetch=1, grid=grid, in_specs=in_specs, out_specs=out_specs,
                                             scratch_shapes=scratch)
    return pl.pallas_call(body, grid_spec=grid_spec, **kwargs)(_as_index(index), a, b, *extra)


def _mm_nn(a, b, name, tm, tn, tk):
    m, kd = a.shape
    n = b.shape[1]
    return _matmul(a, b, name=name, dims=NN, grid=(m // tm, n // tn, kd // tk),
                   a_block=(tm, tk), a_map=lambda i, j, k: (i, k),
                   b_block=(tk, tn), b_map=lambda i, j, k: (k, j),
                   o_block=(tm, tn), o_map=lambda i, j, k: (i, j), out_shape=(m, n))


def _mm_nt(a, b, name, tm, tn, tk):
    m, kd = a.shape
    n = b.shape[0]
    return _matmul(a, b, name=name, dims=NT, grid=(m // tm, n // tn, kd // tk),
                   a_block=(tm, tk), a_map=lambda i, j, k: (i, k),
                   b_block=(tn, tk), b_map=lambda i, j, k: (j, k),
                   o_block=(tm, tn), o_map=lambda i, j, k: (i, j), out_shape=(m, n))


def _mm_tn(a, b, name, tm, tn, tk):
    kd, m = a.shape
    n = b.shape[1]
    return _matmul(a, b, name=name, dims=TN, grid=(m // tm, n // tn, kd // tk),
                   a_block=(tk, tm), a_map=lambda i, j, k: (k, i),
                   b_block=(tk, tn), b_map=lambda i, j, k: (k, j),
                   o_block=(tm, tn), o_map=lambda i, j, k: (i, j), out_shape=(m, n))


W_TN = 1792
W_NJ = WB // W_TN


def _proj_part(h, wg, chip, masks, name, prev=None, dep=None):
    tm, tk = 1024, 2048

    def blk(j, ix):
        m = masks[0]
        for t in range(1, len(masks)):
            m = jnp.where(j // W_NJ == t, masks[t], m)
        return jnp.bitwise_xor(ix[0], m)

    return _matmul(h, wg, name=name, dims=NN, grid=(T // tm, len(masks) * W_NJ, D // tk), index=chip, prev=prev, deps=(dep,),
                   a_block=(tm, tk), a_map=lambda i, j, k, ix: (i, k),
                   b_block=(tk, W_TN), b_map=lambda i, j, k, ix: (blk(j, ix) * (D // tk) + k, j % W_NJ),
                   o_block=(tm, W_TN), o_map=lambda i, j, k, ix: (i, blk(j, ix) * W_NJ + j % W_NJ), out_shape=(T, NC))


def _dh(dproj, wg, dep=None):
    tm, tn = 1024, 1024
    return _matmul(dproj, wg, name="dh", dims=NT, grid=(T // tm, D // tn, NC // W_TN), deps=(dep,),
                   a_block=(tm, W_TN), a_map=lambda i, j, k: (i, k),
                   b_block=(tn, W_TN), b_map=lambda i, j, k: ((k // W_NJ) * (D // tn) + j, k % W_NJ),
                   o_block=(tm, tn), o_map=lambda i, j, k: (i, j), out_shape=(T, D))


def _grad_w_out(cat, dy, half, out_dtype, name, dep=None):
    tm, tn, tk = D // NCHIP // 2, 2048, 2048
    return _matmul(cat, dy, name=name, dims=TN, grid=(NCHIP, D // tn, T // tk), index=half, deps=(dep,),
                   a_block=(tk, tm), a_map=lambda i, j, k, ix: (k, 2 * i + ix[0]),
                   b_block=(tk, tn), b_map=lambda i, j, k, ix: (k, j),
                   o_block=(tm, tn), o_map=lambda i, j, k, ix: (i, j),
                   out_shape=(NCHIP * tm, D), out_dtype=out_dtype)


def _grad_w_in(h, dproj, half, out_dtype, name, dep=None):
    tm, tk = 1024, 2048
    nh = D // 2 // tm
    return _matmul(h, dproj, name=name, dims=TN, grid=(nh, NC // W_TN, T // tk), index=half, deps=(dep,),
                   a_block=(tk, tm), a_map=lambda i, j, k, ix: (k, ix[0] * nh + i),
                   b_block=(tk, W_TN), b_map=lambda i, j, k, ix: (k, j),
                   o_block=(tm, W_TN), o_map=lambda i, j, k, ix: ((j // W_NJ) * nh + i, j % W_NJ),
                   out_shape=(NCHIP * D // 2, WB), out_dtype=out_dtype)


def _rope_tables(pos, half):
    inv = 1.0 / (ROPE_THETA ** (jnp.arange(half, dtype=F32) / half))
    ang = pos.astype(F32)[:, None] * inv[None, :]
    cos, sin = jnp.cos(ang), jnp.sin(ang)
    return jnp.concatenate([cos, cos], axis=1), jnp.concatenate([-sin, sin], axis=1)


def _band_mask(r0):
    qi = lax.broadcasted_iota(jnp.int32, (128, 256), 0)
    kk = lax.broadcasted_iota(jnp.int32, (128, 256), 1)
    return (kk >= qi) & (kk <= qi + 128) & (kk + r0 >= 128)


def _dil_rows(r, n, d):
    if d == 1:
        return pl.ds(pl.multiple_of(n * 128, 128), 128)
    return pl.ds(r + d * 128 * n, 128, stride=d)


def _attn_fwd(proj, cosf, sinf, g):
    d = DIL[g]
    ln = S // d
    nblk = ln // 128
    proj_v = proj.reshape(NB, S, NC)

    def body(q_ref, k_ref, v_ref, cos_ref, sin_ref, o_ref, l_ref, k_s, v_s):
        k_s[:, pl.ds(0, 128), :] = jnp.zeros((d, 128, HD), BF16)
        v_s[:, pl.ds(0, 128), :] = jnp.zeros((d, 128, HD), BF16)

        def prep(i, carry):
            r, n = i // nblk, i % nblk
            rows = _dil_rows(r, n, d)
            dst = pl.ds(pl.multiple_of(n * 128 + 128, 128), 128)
            k_s[r, dst, :] = _rope(k_ref[rows, :], cos_ref[rows, :], sin_ref[rows, :], HD // 2).astype(BF16)
            v_s[r, dst, :] = v_ref[rows, :].astype(BF16)
            return carry

        lax.fori_loop(0, d * nblk, prep, 0, unroll=4)

        def step(i, carry):
            r, n = i // nblk, i % nblk
            rows = _dil_rows(r, n, d)
            r0 = pl.multiple_of(n * 128, 128)
            qr = _rope(q_ref[rows, :], cos_ref[rows, :], sin_ref[rows, :], HD // 2).astype(BF16)
            kw = k_s[r, pl.ds(r0, 256), :]
            vw = v_s[r, pl.ds(r0, 256), :]
            sc = lax.dot_general(qr, kw, NT, preferred_element_type=F32) * A_SCALE
            sc = jnp.where(_band_mask(r0), sc, NEG)
            m = jnp.max(sc, axis=1, keepdims=True)
            p = jnp.exp(sc - m)
            l = jnp.sum(p, axis=1, keepdims=True)
            o_ref[rows, :] = jnp.dot(p.astype(BF16), vw, preferred_element_type=F32) / l
            l_ref[rows, :] = jnp.broadcast_to(m + jnp.log(l), (128, HD))
            return carry

        lax.fori_loop(0, d * nblk, step, 0, unroll=4)

    def col(off):
        return lambda b, h: (b, 0, off // HD + HPG * g + h)

    blk = (None, S, HD)
    tab = pl.BlockSpec((S, HD), lambda b, h: (0, 0))
    out = pl.BlockSpec(blk, lambda b, h: (b, 0, h))
    o, l = pl.pallas_call(
        body, name=f"attn_fwd_d{d}", grid=(NB, HPG),
        in_specs=[pl.BlockSpec(blk, col(QA)), pl.BlockSpec(blk, col(KA)), pl.BlockSpec(blk, col(VA)), tab, tab],
        out_specs=[out, out],
        out_shape=[jax.ShapeDtypeStruct((NB, S, HPG * HD), F32)] * 2,
        scratch_shapes=[pltpu.VMEM((d, ln + 128, HD), BF16), pltpu.VMEM((d, ln + 128, HD), BF16)],
        compiler_params=_cp(("parallel", "parallel")),
    )(proj_v, proj_v, proj_v, cosf, sinf)
    return o.reshape(T, HPG * HD), l.reshape(T, HPG * HD)


def _attn_bwd(proj, cosf, sinf, da, lse, delta, dproj, g):
    d = DIL[g]
    ln = S // d
    nblk = ln // 128
    proj_v = proj.reshape(NB, S, NC)
    dproj_v = dproj.reshape(NB, S, NC)
    da_v = da.reshape(NB, S, AW)
    lse_v = lse.reshape(NB, S, HPG * HD)
    delta_v = delta.reshape(NB, S, HPG * HD)

    def body(q_ref, k_ref, v_ref, cos_ref, sin_ref, da_ref, lse_ref, dl_ref, dp_in_ref, o_ref,
             k_s, v_s, stg, dk_s, dv_s):
        del dp_in_ref
        w = pl.program_id(2)

        def emit():
            def cast(n, carry):
                rows = pl.ds(pl.multiple_of(n * 256, 256), 256)
                o_ref[rows, :] = stg[rows, :].astype(BF16)
                return carry

            lax.fori_loop(0, S // 256, cast, 0)

        @pl.when(w == 0)
        def _():
            k_s[:, pl.ds(0, 128), :] = jnp.zeros((d, 128, HD), BF16)
            v_s[:, pl.ds(0, 128), :] = jnp.zeros((d, 128, HD), BF16)
            dk_s[...] = jnp.zeros_like(dk_s)
            dv_s[...] = jnp.zeros_like(dv_s)

            def prep(i, carry):
                r, n = i // nblk, i % nblk
                rows = _dil_rows(r, n, d)
                dst = pl.ds(pl.multiple_of(n * 128 + 128, 128), 128)
                k_s[r, dst, :] = _rope(k_ref[rows, :], cos_ref[rows, :], sin_ref[rows, :], HD // 2).astype(BF16)
                v_s[r, dst, :] = v_ref[rows, :].astype(BF16)
                return carry

            lax.fori_loop(0, d * nblk, prep, 0, unroll=4)

            def step(i, carry):
                r, n = i // nblk, i % nblk
                rows = _dil_rows(r, n, d)
                r0 = pl.multiple_of(n * 128, 128)
                win = pl.ds(r0, 256)
                cos, sin = cos_ref[rows, :], sin_ref[rows, :]
                qr = _rope(q_ref[rows, :], cos, sin, HD // 2).astype(BF16)
                kw = k_s[r, win, :]
                vw = v_s[r, win, :]
                sc = lax.dot_general(qr, kw, NT, preferred_element_type=F32) * A_SCALE
                sc = jnp.where(_band_mask(r0), sc, NEG)
                p = jnp.exp(sc - lse_ref[rows, :][:, :1])
                da_b = da_ref[rows, :].astype(BF16)
                dp = lax.dot_general(da_b, vw, NT, preferred_element_type=F32)
                ds_b = (p * (dp - dl_ref[rows, :][:, :1]) * A_SCALE).astype(BF16)
                p_b = p.astype(BF16)
                dq = jnp.dot(ds_b, kw, preferred_element_type=F32)
                stg[rows, :] = _rope_t(dq, cos, sin, HD // 2)
                dk_s[r, win, :] += lax.dot_general(ds_b, qr, TN, preferred_element_type=F32)
                dv_s[r, win, :] += lax.dot_general(p_b, da_b, TN, preferred_element_type=F32)
                return carry

            lax.fori_loop(0, d * nblk, step, 0, unroll=4)
            emit()

        @pl.when(w == 1)
        def _():
            def put(i, carry):
                r, n = i // nblk, i % nblk
                rows = _dil_rows(r, n, d)
                src = pl.ds(pl.multiple_of(n * 128 + 128, 128), 128)
                stg[rows, :] = _rope_t(dk_s[r, src, :], cos_ref[rows, :], sin_ref[rows, :], HD // 2)
                return carry

            lax.fori_loop(0, d * nblk, put, 0, unroll=4)
            emit()

        @pl.when(w == 2)
        def _():
            def put(i, carry):
                r, n = i // nblk, i % nblk
                src = pl.ds(pl.multiple_of(n * 128 + 128, 128), 128)
                stg[_dil_rows(r, n, d), :] = dv_s[r, src, :]
                return carry

            lax.fori_loop(0, d * nblk, put, 0, unroll=4)
            emit()

    def col(off):
        return lambda b, h, w: (ahead(b, h, w)[0], 0, off // HD + HPG * g + ahead(b, h, w)[1])

    def ahead(b, h, w):
        flat = jnp.minimum(b * HPG + h + jnp.where(w > 0, 1, 0), NB * HPG - 1)
        return flat // HPG, flat % HPG

    blk = (None, S, HD)
    tab = pl.BlockSpec((S, HD), lambda b, h, w: (0, 0))
    per_head = pl.BlockSpec(blk, lambda b, h, w: (ahead(b, h, w)[0], 0, ahead(b, h, w)[1]))
    out = pl.pallas_call(
        body, name=f"attn_bwd_d{d}", grid=(NB, HPG, 3),
        in_specs=[pl.BlockSpec(blk, col(QA)), pl.BlockSpec(blk, col(KA)), pl.BlockSpec(blk, col(VA)), tab, tab,
                  pl.BlockSpec(blk, col(0)), per_head, per_head, pl.BlockSpec(memory_space=pl.ANY)],
        out_specs=pl.BlockSpec(blk, lambda b, h, w: (b, 0, (AW // HD) * w + HPG * g + h)),
        out_shape=jax.ShapeDtypeStruct(dproj_v.shape, BF16),
        input_output_aliases={8: 0},
        scratch_shapes=[pltpu.VMEM((d, ln + 128, HD), BF16), pltpu.VMEM((d, ln + 128, HD), BF16),
                        pltpu.VMEM((S, HD), F32), pltpu.VMEM((d, ln + 128, HD), F32), pltpu.VMEM((d, ln + 128, HD), F32)],
        compiler_params=_cp(("arbitrary",) * 3),
    )(proj_v, proj_v, proj_v, cosf, sinf, da_v, lse_v, delta_v, dproj_v)
    return out.reshape(T, NC)


def _attn_mix(proj, os_, ls_):
    tr = 256
    gw = HPG * HD

    def body(o0, o1, o2, l0, l1, l2, z_ref, cat_ref):
        m = jnp.maximum(jnp.maximum(l0[...], l1[...]), l2[...])
        e = [jnp.exp(l[...] - m) for l in (l0, l1, l2)]
        inv = 1.0 / (e[0] + e[1] + e[2])
        for gi, o in enumerate((o0, o1, o2)):
            z = z_ref[:, gi * gw:(gi + 1) * gw]
            cat_ref[:, gi * gw:(gi + 1) * gw] = (o[...] * (e[gi] * inv) * (z * _sigmoid(z))).astype(BF16)

    grp = pl.BlockSpec((tr, gw), lambda i: (i, 0))
    return pl.pallas_call(
        body, name="attn_mix", grid=(T // tr,),
        in_specs=[grp] * 6 + [pl.BlockSpec((tr, AW), lambda i: (i, ZA // AW))],
        out_specs=pl.BlockSpec((tr, AW), lambda i: (i, 0)),
        out_shape=jax.ShapeDtypeStruct((T, D), BF16),
        compiler_params=_cp(("parallel",)),
    )(*os_, *ls_, proj)


def _attn_mix_bwd(dcat, proj, os_, ls_, dep=None):
    tr = 256
    gw = HPG * HD
    deps = [] if dep is None else [dep]

    def body(dy_ref, o0, o1, o2, l0, l1, l2, z_ref, *rest):
        da_ref, lse_ref, dl_ref, dz_ref = rest[len(deps):]
        m = jnp.maximum(jnp.maximum(l0[...], l1[...]), l2[...])
        e = [jnp.exp(l[...] - m) for l in (l0, l1, l2)]
        den = e[0] + e[1] + e[2]
        inv = 1.0 / den
        lse_ref[...] = m + jnp.log(den)
        acc = jnp.zeros((tr, gw), F32)
        for gi, o in enumerate((o0, o1, o2)):
            cols = slice(gi * gw, (gi + 1) * gw)
            z = z_ref[:, cols]
            dy = dy_ref[:, cols]
            sg = _sigmoid(z)
            a = o[...] * (e[gi] * inv)
            da = dy * (z * sg)
            da_ref[:, cols] = da
            dz_ref[:, cols] = (dy * a * (sg * (1.0 + z * (1.0 - sg)))).astype(BF16)
            acc = acc + da * a
        for hh in range(HPG):
            cols = slice(hh * HD, (hh + 1) * HD)
            dl_ref[:, cols] = jnp.broadcast_to(jnp.sum(acc[:, cols], axis=1, keepdims=True), (tr, HD))

    grp = pl.BlockSpec((tr, gw), lambda i: (i, 0))
    return pl.pallas_call(
        body, name="attn_mix_bwd", grid=(T // tr,),
        in_specs=[pl.BlockSpec((tr, AW), lambda i: (i, 0))] + [grp] * 6 + [pl.BlockSpec((tr, AW), lambda i: (i, ZA // AW))]
        + [pl.BlockSpec(memory_space=pl.ANY)] * len(deps),
        out_specs=[pl.BlockSpec((tr, AW), lambda i: (i, 0)), grp, grp, pl.BlockSpec((tr, AW), lambda i: (i, ZA // AW))],
        out_shape=[jax.ShapeDtypeStruct((T, AW), F32), jax.ShapeDtypeStruct((T, gw), F32),
                   jax.ShapeDtypeStruct((T, gw), F32), jax.ShapeDtypeStruct((T, NC), BF16)],
        compiler_params=_cp(("parallel",)),
    )(dcat, *os_, *ls_, proj, *deps)


CT = 256


def _shift_down(x, n):
    rows = lax.broadcasted_iota(jnp.int32, x.shape, 0)
    return jnp.where(rows >= n, pltpu.roll(x, n, 0), 0.0)


def _shift_up(x, n):
    rows = lax.broadcasted_iota(jnp.int32, x.shape, 0)
    return jnp.where(rows < x.shape[0] - n, pltpu.roll(x, x.shape[0] - n, 0), 0.0)


def _conv_fwd(proj, conv_w, cat):
    proj_v = proj.reshape(NB, S, NC)
    cat_v = cat.reshape(NB, S, D)

    def body(u_ref, b_ref, c_ref, z_ref, w_ref, cat_in, o_ref):
        del cat_in
        cu = c_ref[...] * u_ref[...]
        cv = _shift_down(cu, 2) * w_ref[0:1, :] + _shift_down(cu, 1) * w_ref[1:2, :] + cu * w_ref[2:3, :]
        z = z_ref[...]
        o_ref[...] = (b_ref[...] * cv * (z * _sigmoid(z))).astype(BF16)

    def seg(off):
        return pl.BlockSpec((None, S, CT), lambda b, j: (b, 0, off // CT + j))

    out = pl.pallas_call(
        body, name="conv_fwd", grid=(NB, CW // CT),
        in_specs=[seg(UC), seg(BC), seg(CC), seg(ZC), pl.BlockSpec((3, CT), lambda b, j: (0, j)),
                  pl.BlockSpec(memory_space=pl.ANY)],
        out_specs=pl.BlockSpec((None, S, CT), lambda b, j: (b, 0, AW // CT + j)),
        out_shape=jax.ShapeDtypeStruct((NB, S, D), BF16),
        input_output_aliases={5: 0},
        compiler_params=_cp(("parallel", "parallel")),
    )(proj_v, proj_v, proj_v, proj_v, conv_w, cat_v)
    return out.reshape(T, D)


def _conv_bwd(dcat, proj, conv_w, dproj):
    proj_v = proj.reshape(NB, S, NC)
    dproj_v = dproj.reshape(NB, S, NC)
    dcat_v = dcat.reshape(NB, S, D)

    def body(dy_ref, u_ref, b_ref, c_ref, z_ref, w_ref, dp_in, o_ref, dw_ref, st):
        del dp_in
        b = pl.program_id(1)
        w = pl.program_id(2)

        @pl.when((b == 0) & (w == 0))
        def _():
            dw_ref[...] = jnp.zeros_like(dw_ref)

        @pl.when(w == 0)
        def _():
            u, c, z, bb, dy = u_ref[...], c_ref[...], z_ref[...], b_ref[...], dy_ref[...]
            cu = c * u
            s1 = _shift_down(cu, 1)
            s2 = _shift_down(cu, 2)
            cv = s2 * w_ref[0:1, :] + s1 * w_ref[1:2, :] + cu * w_ref[2:3, :]
            sg = _sigmoid(z)
            sz = z * sg
            dcv = dy * bb * sz
            st[0] = dy * cv * sz
            st[2] = dy * bb * cv * (sg * (1.0 + z * (1.0 - sg)))
            dw_ref[0:1, :] += jnp.sum(dcv * s2, axis=0, keepdims=True)
            dw_ref[1:2, :] += jnp.sum(dcv * s1, axis=0, keepdims=True)
            dw_ref[2:3, :] += jnp.sum(dcv * cu, axis=0, keepdims=True)
            dcu = dcv * w_ref[2:3, :] + _shift_up(dcv, 1) * w_ref[1:2, :] + _shift_up(dcv, 2) * w_ref[0:1, :]
            st[1] = dcu * u
            o_ref[...] = (dcu * c).astype(BF16)

        for k in range(3):
            @pl.when(w == k + 1)
            def _(k=k):
                o_ref[...] = st[k].astype(BF16)

    def ahead(j, b, w):
        flat = jnp.minimum(j * NB + b + jnp.where(w > 0, 1, 0), (CW // CT) * NB - 1)
        return flat // NB, flat % NB

    def seg(off):
        return pl.BlockSpec((None, S, CT), lambda j, b, w: (ahead(j, b, w)[1], 0, off // CT + ahead(j, b, w)[0]))

    out, dw = pl.pallas_call(
        body, name="conv_bwd", grid=(CW // CT, NB, 4),
        in_specs=[seg(AW), seg(UC), seg(BC), seg(CC), seg(ZC),
                  pl.BlockSpec((3, CT), lambda j, b, w: (0, ahead(j, b, w)[0])),
                  pl.BlockSpec(memory_space=pl.ANY)],
        out_specs=[pl.BlockSpec((None, S, CT), lambda j, b, w: (b, 0, (UC + w * CW) // CT + j)),
                   pl.BlockSpec((3, CT), lambda j, b, w: (0, j))],
        out_shape=[jax.ShapeDtypeStruct((NB, S, NC), BF16), jax.ShapeDtypeStruct((3, CW), F32)],
        input_output_aliases={6: 0},
        scratch_shapes=[pltpu.VMEM((3, S, CT), F32)],
        compiler_params=_cp(("arbitrary",) * 3),
    )(dcat_v, proj_v, proj_v, proj_v, proj_v, conv_w, dproj_v)
    return out.reshape(T, NC), dw


XT = 1024


def _cross_fwd(proj, mkv, cosq, sinq, cosm, sinm, cat):
    proj_v = proj.reshape(NB, S, NC)
    mkv_v = mkv.reshape(NB, MLEN, 2 * XW)
    cat_v = cat.reshape(NB, S, D)

    def body(q_ref, z_ref, mk_ref, mv_ref, cq, sq, cm, sm, cat_in, o_ref):
        del cat_in
        mkr = _rope(mk_ref[...], cm[...], sm[...], XHD // 2).astype(BF16)
        qr = _rope(q_ref[...], cq[...], sq[...], XHD // 2).astype(BF16)
        sc = lax.dot_general(qr, mkr, NT, preferred_element_type=F32) * X_SCALE
        p = jnp.exp(sc - jnp.max(sc, axis=1, keepdims=True))
        p = p / jnp.sum(p, axis=1, keepdims=True)
        ox = jnp.dot(p.astype(BF16), mv_ref[...].astype(BF16), preferred_element_type=F32)
        z = z_ref[...]
        o_ref[...] = (ox * (z * _sigmoid(z))).astype(BF16)

    def seg(off):
        return pl.BlockSpec((None, XT, XHD), lambda b, h, t: (b, t, off // XHD + h))

    qtab = pl.BlockSpec((XT, XHD), lambda b, h, t: (t, 0))
    mtab = pl.BlockSpec((MLEN, XHD), lambda b, h, t: (0, 0))
    out = pl.pallas_call(
        body, name="cross_fwd", grid=(NB, NXH, S // XT),
        in_specs=[seg(QX), seg(ZX),
                  pl.BlockSpec((None, MLEN, XHD), lambda b, h, t: (b, 0, h)),
                  pl.BlockSpec((None, MLEN, XHD), lambda b, h, t: (b, 0, NXH + h)),
                  qtab, qtab, mtab, mtab, pl.BlockSpec(memory_space=pl.ANY)],
        out_specs=pl.BlockSpec((None, XT, XHD), lambda b, h, t: (b, t, (AW + CW) // XHD + h)),
        out_shape=jax.ShapeDtypeStruct((NB, S, D), BF16),
        input_output_aliases={8: 0},
        compiler_params=_cp(("parallel",) * 3),
    )(proj_v, proj_v, mkv_v, mkv_v, cosq, sinq, cosm, sinm, cat_v)
    return out.reshape(T, D)


def _cross_bwd(dcat, proj, mkv, cosq, sinq, cosm, sinm, dproj):
    proj_v = proj.reshape(NB, S, NC)
    dproj_v = dproj.reshape(NB, S, NC)
    dcat_v = dcat.reshape(NB, S, D)
    mkv_v = mkv.reshape(NB, MLEN, 2 * XW)
    nt = S // XT

    def body(dy_ref, q_ref, z_ref, mk_ref, mv_ref, cq, sq, cm, sm, dp_in, o_ref, dmk_ref, dmv_ref, dz_s):
        del dp_in
        t = pl.program_id(2)
        w = pl.program_id(3)

        @pl.when((t == 0) & (w == 0))
        def _():
            dmk_ref[...] = jnp.zeros_like(dmk_ref)
            dmv_ref[...] = jnp.zeros_like(dmv_ref)

        @pl.when(w == 0)
        def _():
            mkr = _rope(mk_ref[...], cm[...], sm[...], XHD // 2).astype(BF16)
            mv_b = mv_ref[...].astype(BF16)
            qr = _rope(q_ref[...], cq[...], sq[...], XHD // 2).astype(BF16)
            sc = lax.dot_general(qr, mkr, NT, preferred_element_type=F32) * X_SCALE
            p = jnp.exp(sc - jnp.max(sc, axis=1, keepdims=True))
            p = p / jnp.sum(p, axis=1, keepdims=True)
            p_b = p.astype(BF16)
            ox = jnp.dot(p_b, mv_b, preferred_element_type=F32)
            z = z_ref[...]
            dy = dy_ref[...]
            sg = _sigmoid(z)
            dz_s[...] = dy * ox * (sg * (1.0 + z * (1.0 - sg)))
            dox_b = (dy * (z * sg)).astype(BF16)
            dp = lax.dot_general(dox_b, mv_b, NT, preferred_element_type=F32)
            ds_b = (p * (dp - jnp.sum(dp * p, axis=1, keepdims=True)) * X_SCALE).astype(BF16)
            dq = jnp.dot(ds_b, mkr, preferred_element_type=F32)
            o_ref[...] = _rope_t(dq, cq[...], sq[...], XHD // 2).astype(BF16)
            dmk_ref[...] += lax.dot_general(ds_b, qr, TN, preferred_element_type=F32)
            dmv_ref[...] += lax.dot_general(p_b, dox_b, TN, preferred_element_type=F32)

        @pl.when(w == 1)
        def _():
            o_ref[...] = dz_s[...].astype(BF16)

        @pl.when((t == nt - 1) & (w == 1))
        def _():
            dmk_ref[...] = _rope_t(dmk_ref[...], cm[...], sm[...], XHD // 2)

    def seg(off):
        return pl.BlockSpec((None, XT, XHD), lambda b, h, t, w: (b, t, off // XHD + h))

    qtab = pl.BlockSpec((XT, XHD), lambda b, h, t, w: (t, 0))
    mtab = pl.BlockSpec((MLEN, XHD), lambda b, h, t, w: (0, 0))
    macc = pl.BlockSpec((None, MLEN, XHD), lambda b, h, t, w: (b, 0, h))
    out, dmk, dmv = pl.pallas_call(
        body, name="cross_bwd", grid=(NB, NXH, nt, 2),
        in_specs=[pl.BlockSpec((None, XT, XHD), lambda b, h, t, w: (b, t, (AW + CW) // XHD + h)),
                  seg(QX), seg(ZX),
                  pl.BlockSpec((None, MLEN, XHD), lambda b, h, t, w: (b, 0, h)),
                  pl.BlockSpec((None, MLEN, XHD), lambda b, h, t, w: (b, 0, NXH + h)),
                  qtab, qtab, mtab, mtab, pl.BlockSpec(memory_space=pl.ANY)],
        out_specs=[pl.BlockSpec((None, XT, XHD), lambda b, h, t, w: (b, t, (QX + w * XW) // XHD + h)), macc, macc],
        out_shape=[jax.ShapeDtypeStruct((NB, S, NC), BF16), jax.ShapeDtypeStruct((NB, MLEN, XW), F32),
                   jax.ShapeDtypeStruct((NB, MLEN, XW), F32)],
        input_output_aliases={9: 0},
        scratch_shapes=[pltpu.VMEM((XT, XHD), F32)],
        compiler_params=_cp(("arbitrary",) * 4),
    )(dcat_v, proj_v, proj_v, mkv_v, mkv_v, cosq, sinq, cosm, sinm, dproj_v)
    return out.reshape(T, NC), dmk, dmv


def _local_step(x, mem, pre_norm, conv_w, mem_norm, post_norm, tgt, chip, core, comm):
    x2 = x.reshape(T, D)
    mem2 = mem.reshape(NB * MLEN, D)
    tgt2 = tgt.reshape(T, D)
    cosa, sina = _rope_tables(jnp.arange(S), HD // 2)
    cosq, sinq = _rope_tables(jnp.arange(S) + MLEN, XHD // 2)
    cosm, sinm = _rope_tables(jnp.arange(MLEN), XHD // 2)

    h = _rms_fwd(x2, pre_norm, "pre_norm_fwd", dep=comm.gather_started())
    memn = _rms_fwd(mem2, mem_norm, "mem_norm_fwd")
    proj = _proj_part(h, comm.w_in_own(), chip, (0,), "proj_own")
    wg_in, tok = comm.w_in_near(after=proj)
    proj = _proj_part(h, wg_in, chip, (REL_XOR[0], REL_XOR[1]), "proj_near", prev=proj, dep=tok)
    wg_in, tok = comm.w_in_all(after=proj)
    proj = _proj_part(h, wg_in, chip, (REL_XOR[2],), "proj_far", prev=proj, dep=tok)
    wg_kv, wg_out = comm.w_rest(after=proj)
    mkv = _mm_nn(memn, wg_kv, "mkv", NB * MLEN, 1024, 512)

    fw = [_attn_fwd(proj, cosa, sina, g) for g in range(3)]
    os_ = [f[0] for f in fw]
    ls_ = [f[1] for f in fw]
    cat = _attn_mix(proj, os_, ls_)
    cat = _conv_fwd(proj, conv_w, cat)
    cat = _cross_fwd(proj, mkv, cosq, sinq, cosm, sinm, cat)
    y = _mm_nn(cat, wg_out, "out_proj", 1024, 1024, 2048)
    dy, dout, d_post, loss = _post_norm_loss(y, x2, tgt2, post_norm)

    dcat = _mm_nt(dy, wg_out, "dcat", 1024, 1024, 2048)
    g_send = _grad_w_out(cat, dy, 1 - core, BF16, "grad_w_out_send")
    sent = comm.sibling_start(g_send.reshape(NCHIP, D // NCHIP // 2, D), "a")
    g_keep = _grad_w_out(cat, dy, core, F32, "grad_w_out_keep", dep=sent[-1])
    red_a = comm.reduce_start_summed(g_keep.reshape(NCHIP, D // NCHIP // 2, D), sent, "a")
    da, lse, delta, dproj = _attn_mix_bwd(dcat, proj, os_, ls_, dep=red_a[-1])
    for g in range(3):
        dproj = _attn_bwd(proj, cosa, sina, da, lse, delta, dproj, g)
    dproj, d_conv = _conv_bwd(dcat, proj, conv_w, dproj)
    dproj, dmk, dmv = _cross_bwd(dcat, proj, mkv, cosq, sinq, cosm, sinm, dproj)

    dmkv = jnp.concatenate([dmk, dmv], axis=-1).reshape(NB * MLEN, 2 * XW)
    dmkv_b = dmkv.astype(BF16)
    g_kv = _mm_tn(memn, dmkv_b, "grad_w_mem_kv", 1024, 1024, NB * MLEN)
    dmemn = _mm_nt(dmkv_b, wg_kv, "dmemn", NB * MLEN, 1024, 512)
    d_mem = _norm_gain_grad(dmemn, mem2, "mem_norm_bwd")

    red_k = comm.reduce_start([g_kv.reshape(NCHIP, D // NCHIP, 2 * XW)], "k")
    g_send = _grad_w_in(h, dproj, 1 - core, BF16, "grad_w_in_send", dep=red_k[-1])
    sent = comm.sibling_start(g_send.reshape(NCHIP, D // 2, WB), "b")
    g_keep = _grad_w_in(h, dproj, core, F32, "grad_w_in_keep", dep=sent[-1])
    red_b = comm.reduce_start_summed(g_keep.reshape(NCHIP, D // 2, WB), sent, "b")
    dh = _dh(dproj, wg_in, dep=red_b[-1])
    grad_x, d_pre = _pre_norm_bwd(dh, x2, pre_norm, dout)
    (r_out,) = comm.reduce_finish(red_a, "a", after=grad_x)
    (r_kv,) = comm.reduce_finish(red_k, "k", after=grad_x)
    (r_in,) = comm.reduce_finish(red_b, "b", after=grad_x)
    return loss, grad_x.reshape(NB, S, D), d_pre, d_mem, d_post, d_conv, r_in, r_kv, r_out


def _adamw(w, g, m, v, name):
    rows, cols = w.shape
    tr = rows if rows <= 512 else 512
    tc = cols if cols <= 1024 else 1024
    if cols % tc:
        tc = 896

    def body(w_ref, g_ref, m_ref, v_ref, go_ref, d_ref, nm_ref, nv_ref):
        gv = g_ref[...]
        go_ref[...] = gv
        nm = ADAM_B1 * m_ref[...] + (1.0 - ADAM_B1) * gv
        nv = ADAM_B2 * v_ref[...] + (1.0 - ADAM_B2) * (gv * gv)
        m_hat = nm / (1.0 - ADAM_B1 ** ADAM_STEP)
        v_hat = nv / (1.0 - ADAM_B2 ** ADAM_STEP)
        d_ref[...] = -ADAM_LR * (m_hat / (jnp.sqrt(v_hat) + ADAM_EPS) + ADAM_WD * w_ref[...])
        nm_ref[...] = nm
        nv_ref[...] = nv

    blk = pl.BlockSpec((tr, tc), lambda i, j: (i, j))
    sds = jax.ShapeDtypeStruct((rows, cols), F32)
    return pl.pallas_call(
        body, name=name, grid=(rows // tr, cols // tc),
        in_specs=[blk] * 4, out_specs=[blk] * 4, out_shape=[sds] * 4,
        compiler_params=_cp(("parallel", "parallel")),
    )(w, g, m, v)


def _place():
    return lax.axis_index("x"), lax.axis_index("y"), lax.axis_index("c")


def _other_chips(x, y):
    return [(1 - x, y), (x, 1 - y), (1 - x, 1 - y)]


def _tile_cols(cols):
    return cols if cols <= 1024 else (1024 if cols % 1024 == 0 else 896)


def _cast_own(w, chip, name):
    rows, cols = w.shape
    tr, tc = 512, _tile_cols(cols)

    def body(chip_ref, w_ref, o_ref):
        del chip_ref
        o_ref[...] = w_ref[...].astype(BF16)

    grid_spec = pltpu.PrefetchScalarGridSpec(
        num_scalar_prefetch=1, grid=(rows // tr, cols // tc),
        in_specs=[pl.BlockSpec((tr, tc), lambda i, j, cr: (i, j))],
        out_specs=pl.BlockSpec((None, tr, tc), lambda i, j, cr: (cr[0], i, j)))
    return pl.pallas_call(
        body, name=name, grid_spec=grid_spec,
        out_shape=jax.ShapeDtypeStruct((NCHIP, rows, cols), BF16),
        compiler_params=_cp(("parallel", "parallel")),
    )(_as_index(chip), w)


HBM_SPEC = pl.BlockSpec(memory_space=pltpu.HBM)
SEM_SPEC = pl.BlockSpec(memory_space=pltpu.SEMAPHORE)
ANY_SPEC = pl.BlockSpec(memory_space=pl.ANY)
EFFECT = pltpu.SideEffectType.DATAFLOW_SIDE_EFFECTING
TOKEN = jax.ShapeDtypeStruct((8, 128), F32)


def _half(ref, chip, hc):
    hr = ref.shape[1] // 2
    return ref.at[chip, pl.ds(hc * hr, hr), :]


NEAR = (0, 1)
FAR = (2,)
REL_XOR = (2, 1, 3)


def _gather_copies(refs, send_sems, recv_sems, rels):
    x, y, c = _place()
    chips = _other_chips(x, y)
    out, inc = [], []
    for a, ref in enumerate(refs):
        for p, j in enumerate(rels):
            px, py = chips[j]
            mine = _half(ref, 2 * x + y, c)
            theirs = _half(ref, 2 * px + py, c)
            sems = dict(send_sem=send_sems.at[len(rels) * a + p], recv_sem=recv_sems.at[len(rels) * a + p],
                        device_id=(px, py, c), device_id_type=MESH)
            out.append(pltpu.make_async_remote_copy(src_ref=mine, dst_ref=mine, **sems))
            inc.append(pltpu.make_async_remote_copy(src_ref=theirs, dst_ref=theirs, **sems))
    return out, inc


def _gather_start(bufs, groups, name):
    n = len(bufs)
    ng = len(groups)

    def body(*refs):
        ins = refs[:n]
        token = refs[-1]
        for gi, rels in enumerate(groups):
            out, _ = _gather_copies(ins, refs[n + 2 * gi], refs[n + 2 * gi + 1], rels)
            for cp in out:
                cp.start()
        token[...] = jnp.zeros_like(token)

    sems = []
    for rels in groups:
        sems += [pltpu.SemaphoreType.DMA((len(rels) * n,))] * 2
    res = pl.pallas_call(
        body, name=name,
        in_specs=[HBM_SPEC] * n,
        out_specs=[SEM_SPEC] * (2 * ng) + [HBM_SPEC] * n + [pl.BlockSpec(memory_space=pltpu.VMEM)],
        out_shape=sems + [pltpu.HBM(b.shape, b.dtype) for b in bufs] + [TOKEN],
        input_output_aliases={a: 2 * ng + a for a in range(n)},
        compiler_params=pltpu.CompilerParams(has_side_effects=EFFECT),
    )(*[pltpu.with_memory_space_constraint(b, pltpu.HBM) for b in bufs])
    return [(res[2 * gi], res[2 * gi + 1]) for gi in range(ng)], list(res[2 * ng:2 * ng + n]), res[-1]


def _gather_wait(bufs, sems, rels, after, name):
    n = len(bufs)
    send_sems, recv_sems = sems

    def body(*refs):
        ins = refs[:n]
        out, inc = _gather_copies(ins, refs[n], refs[n + 1], rels)
        for cp in out:
            cp.wait_send()
        for cp in inc:
            cp.wait_recv()

    return pl.pallas_call(
        body, name=name,
        in_specs=[HBM_SPEC] * n + [SEM_SPEC, SEM_SPEC, ANY_SPEC],
        out_specs=[HBM_SPEC] * n,
        out_shape=[pltpu.HBM(b.shape, b.dtype) for b in bufs],
        input_output_aliases={a: a for a in range(n)},
        compiler_params=pltpu.CompilerParams(has_side_effects=EFFECT),
    )(*bufs, send_sems, recv_sems, after)


def _forward_halves(bufs, rels, name):
    n = len(bufs)

    def body(*refs):
        outs = refs[n:2 * n]
        send_sems, recv_sems = refs[2 * n:]
        x, y, c = _place()
        chips = _other_chips(x, y)
        cps, waits = [], []
        for a in range(n):
            for p, j in enumerate(rels):
                px, py = chips[j]
                sems = dict(send_sem=send_sems.at[len(rels) * a + p], recv_sem=recv_sems.at[len(rels) * a + p],
                            device_id=(x, y, 1 - c), device_id_type=MESH)
                got = _half(outs[a], 2 * px + py, c)
                want = _half(outs[a], 2 * px + py, 1 - c)
                cps.append(pltpu.make_async_remote_copy(src_ref=got, dst_ref=got, **sems))
                waits.append(pltpu.make_async_remote_copy(src_ref=want, dst_ref=want, **sems))
        for cp in cps:
            cp.start()
        for cp in waits:
            cp.wait_recv()
        for cp in cps:
            cp.wait_send()

    return pl.pallas_call(
        body, name=name,
        in_specs=[ANY_SPEC] * n, out_specs=[ANY_SPEC] * n,
        out_shape=[jax.ShapeDtypeStruct(s.shape, s.dtype) for s in bufs],
        input_output_aliases={a: a for a in range(n)},
        scratch_shapes=[pltpu.SemaphoreType.DMA((len(rels) * n,)), pltpu.SemaphoreType.DMA((len(rels) * n,))],
    )(*bufs)


def _sibling_halves(parts, name):
    n = len(parts)

    def body(*refs):
        ins, outs = refs[:n], refs[n:2 * n]
        send_sems, recv_sems = refs[2 * n:]
        x, y, c = _place()
        cps = []
        for a in range(n):
            hr = ins[a].shape[1] // 2
            cps.append(pltpu.make_async_remote_copy(
                src_ref=ins[a].at[:, pl.ds((1 - c) * hr, hr), :], dst_ref=outs[a],
                send_sem=send_sems.at[a], recv_sem=recv_sems.at[a], device_id=(x, y, 1 - c), device_id_type=MESH))
        for cp in cps:
            cp.start()
        for cp in cps:
            cp.wait()

    hbm = pl.BlockSpec(memory_space=pl.ANY)
    return pl.pallas_call(
        body, name=name,
        in_specs=[hbm] * n, out_specs=[hbm] * n,
        out_shape=[jax.ShapeDtypeStruct((NCHIP, p.shape[1] // 2, p.shape[2]), p.dtype) for p in parts],
        scratch_shapes=[pltpu.SemaphoreType.DMA((n,)), pltpu.SemaphoreType.DMA((n,))],
    )(*parts)


def _sibling_copy(src, land, send_sems, recv_sems):
    x, y, c = _place()
    return pltpu.make_async_remote_copy(src_ref=src, dst_ref=land, send_sem=send_sems.at[0], recv_sem=recv_sems.at[0],
                                        device_id=(x, y, 1 - c), device_id_type=MESH)


def _sibling_start(part, name):
    def body(src, land, send_sems, recv_sems, src_thru, land_thru, token):
        _sibling_copy(src, land, send_sems, recv_sems).start()
        token[...] = jnp.zeros_like(token)

    land = lax.empty(part.shape, part.dtype)
    return pl.pallas_call(
        body, name=name,
        in_specs=[HBM_SPEC] * 2,
        out_specs=[SEM_SPEC, SEM_SPEC, HBM_SPEC, HBM_SPEC, pl.BlockSpec(memory_space=pltpu.VMEM)],
        out_shape=[pltpu.SemaphoreType.DMA((1,)), pltpu.SemaphoreType.DMA((1,)), pltpu.HBM(part.shape, part.dtype),
                   pltpu.HBM(part.shape, part.dtype), TOKEN],
        input_output_aliases={0: 2, 1: 3},
        compiler_params=pltpu.CompilerParams(has_side_effects=EFFECT),
    )(pltpu.with_memory_space_constraint(part, pltpu.HBM), pltpu.with_memory_space_constraint(land, pltpu.HBM))


def _sibling_wait(state, after, name):
    send_sems, recv_sems, part, land, _ = state

    def body(src, land_ref, send_ref, recv_ref, after_ref, src_thru, land_thru):
        cp = _sibling_copy(src, land_ref, send_ref, recv_ref)
        cp.wait_send()
        cp.wait_recv()

    return pl.pallas_call(
        body, name=name,
        in_specs=[HBM_SPEC, HBM_SPEC, SEM_SPEC, SEM_SPEC, ANY_SPEC],
        out_specs=[HBM_SPEC, HBM_SPEC],
        out_shape=[pltpu.HBM(part.shape, part.dtype), pltpu.HBM(land.shape, land.dtype)],
        input_output_aliases={0: 0, 1: 1},
        compiler_params=pltpu.CompilerParams(has_side_effects=EFFECT),
    )(part, land, send_sems, recv_sems, after)[1]


def _pair_sum_rows(keep, got, name):
    nblk, rows, cols = keep.shape
    tr, tc = 512, _tile_cols(cols)

    def body(k_ref, g_ref, o_ref):
        o_ref[...] = (k_ref[...] + g_ref[...].astype(F32)).astype(BF16)

    blk = pl.BlockSpec((None, tr, tc), lambda b, i, j: (b, i, j))
    return pl.pallas_call(
        body, name=name, grid=(nblk, rows // tr, cols // tc),
        in_specs=[blk, blk], out_specs=blk,
        out_shape=jax.ShapeDtypeStruct(keep.shape, BF16),
        compiler_params=_cp(("parallel",) * 3),
    )(keep, got)


def _pair_sum(part, got, name):
    nblk, rows, cols = part.shape
    hr = rows // 2
    tr = 512
    tc = cols if cols <= 1024 else (1024 if cols % 1024 == 0 else 896)
    c = lax.axis_index("c")
    nrt = hr // tr

    def body(c_ref, p_ref, g_ref, o_ref):
        del c_ref
        o_ref[...] = (p_ref[...] + g_ref[...]).astype(BF16)

    grid_spec = pltpu.PrefetchScalarGridSpec(
        num_scalar_prefetch=1, grid=(nblk, nrt, cols // tc),
        in_specs=[pl.BlockSpec((None, tr, tc), lambda b, i, j, cr: (b, cr[0] * nrt + i, j)),
                  pl.BlockSpec((None, tr, tc), lambda b, i, j, cr: (b, i, j))],
        out_specs=pl.BlockSpec((None, tr, tc), lambda b, i, j, cr: (b, i, j)))
    return pl.pallas_call(
        body, name=name, grid_spec=grid_spec,
        out_shape=jax.ShapeDtypeStruct((nblk, hr, cols), BF16),
        compiler_params=_cp(("parallel",) * 3),
    )(jnp.reshape(c, (1,)).astype(jnp.int32), part, got)


def _scatter_start(qs, name):
    n = len(qs)

    def body(*refs):
        ins, lands = refs[:n], refs[n:2 * n]
        token = refs[-1]
        for cp in _scatter_copies(ins, lands, refs[2 * n], refs[2 * n + 1]):
            cp.start()
        token[...] = jnp.zeros_like(token)

    lands = [lax.empty((3,) + q.shape[1:], q.dtype) for q in qs]
    res = pl.pallas_call(
        body, name=name,
        in_specs=[HBM_SPEC] * (2 * n),
        out_specs=[SEM_SPEC, SEM_SPEC] + [HBM_SPEC] * (2 * n) + [pl.BlockSpec(memory_space=pltpu.VMEM)],
        out_shape=[pltpu.SemaphoreType.DMA((3 * n,)), pltpu.SemaphoreType.DMA((3 * n,))]
        + [pltpu.HBM(b.shape, b.dtype) for b in qs + lands] + [TOKEN],
        input_output_aliases={a: 2 + a for a in range(2 * n)},
        compiler_params=pltpu.CompilerParams(has_side_effects=EFFECT),
    )(*[pltpu.with_memory_space_constraint(b, pltpu.HBM) for b in qs + lands])
    return res[0], res[1], list(res[2:2 + n]), list(res[2 + n:2 + 2 * n]), res[-1]


def _scatter_copies(ins, lands, send_sems, recv_sems):
    x, y, c = _place()
    cps = []
    for a in range(len(ins)):
        for j, (px, py) in enumerate(_other_chips(x, y)):
            cps.append(pltpu.make_async_remote_copy(
                src_ref=ins[a].at[2 * px + py], dst_ref=lands[a].at[j],
                send_sem=send_sems.at[3 * a + j], recv_sem=recv_sems.at[3 * a + j], device_id=(px, py, c), device_id_type=MESH))
    return cps


def _scatter_wait(qs, lands, send_sems, recv_sems, after, name):
    n = len(qs)

    def body(*refs):
        for cp in _scatter_copies(refs[:n], refs[n:2 * n], refs[2 * n], refs[2 * n + 1]):
            cp.wait_send()
            cp.wait_recv()

    res = pl.pallas_call(
        body, name=name,
        in_specs=[HBM_SPEC] * (2 * n) + [SEM_SPEC, SEM_SPEC, ANY_SPEC],
        out_specs=[HBM_SPEC] * (2 * n),
        out_shape=[pltpu.HBM(b.shape, b.dtype) for b in qs + lands],
        input_output_aliases={a: a for a in range(2 * n)},
        compiler_params=pltpu.CompilerParams(has_side_effects=EFFECT),
    )(*qs, *lands, send_sems, recv_sems, after)
    return list(res[:n]), list(res[n:])


def _chip_sum(q, got, name):
    _, hr, cols = got.shape
    tr, tc = 512, _tile_cols(cols)
    chip = 2 * lax.axis_index("x") + lax.axis_index("y")
    c = lax.axis_index("c")

    def body(idx_ref, q_ref, g_ref, o_ref):
        del idx_ref
        acc = q_ref[...].astype(F32)
        for i in range(3):
            acc = acc + g_ref[i].astype(F32)
        o_ref[...] = acc

    grid_spec = pltpu.PrefetchScalarGridSpec(
        num_scalar_prefetch=1, grid=(hr // tr, cols // tc),
        in_specs=[pl.BlockSpec((None, tr, tc), lambda i, j, ix: (ix[0], i, j)),
                  pl.BlockSpec((3, tr, tc), lambda i, j, ix: (0, i, j))],
        out_specs=pl.BlockSpec((None, tr, tc), lambda i, j, ix: (ix[1], i, j)))
    return pl.pallas_call(
        body, name=name, grid_spec=grid_spec,
        out_shape=jax.ShapeDtypeStruct((2, hr, cols), F32),
        compiler_params=_cp(("parallel", "parallel")),
    )(jnp.stack([chip, c]).astype(jnp.int32), q, got)


def _join_halves(bufs, name):
    n = len(bufs)

    def body(*refs):
        outs = refs[n:2 * n]
        send_sems, recv_sems = refs[2 * n:]
        x, y, c = _place()
        cps = []
        for a in range(n):
            cps.append(pltpu.make_async_remote_copy(src_ref=outs[a].at[c], dst_ref=outs[a].at[c], send_sem=send_sems.at[a],
                                                    recv_sem=recv_sems.at[a], device_id=(x, y, 1 - c), device_id_type=MESH))
        for cp in cps:
            cp.start()
        for a in range(n):
            theirs = outs[a].at[1 - c]
            pltpu.make_async_remote_copy(src_ref=theirs, dst_ref=theirs, send_sem=send_sems.at[a], recv_sem=recv_sems.at[a],
                                         device_id=(x, y, 1 - c), device_id_type=MESH).wait_recv()
        for cp in cps:
            cp.wait_send()

    hbm = pl.BlockSpec(memory_space=pl.ANY)
    return pl.pallas_call(
        body, name=name,
        in_specs=[hbm] * n, out_specs=[hbm] * n,
        out_shape=[jax.ShapeDtypeStruct(b.shape, b.dtype) for b in bufs],
        input_output_aliases={a: a for a in range(n)},
        scratch_shapes=[pltpu.SemaphoreType.DMA((n,)), pltpu.SemaphoreType.DMA((n,))],
    )(*bufs)


N_DEV = 8


def _gather_small(block, reduce, name):
    m_per, cols = block.shape

    def body(x_ref, out_ref, all_ref, send_sems, recv_sems, local_sem):
        x, y, c = _place()
        me, sibling = (x, y, c), (x, y, 1 - c)
        chips = _other_chips(x, y)

        def rows(px, py, pc):
            return all_ref.at[pl.ds((4 * px + 2 * py + pc) * m_per, m_per), :]

        def copy(k, block_of, to, src=None):
            return pltpu.make_async_remote_copy(
                src_ref=rows(*block_of) if src is None else src, dst_ref=rows(*block_of),
                send_sem=send_sems.at[k], recv_sem=recv_sems.at[k], device_id=to, device_id_type=MESH)

        mine = pltpu.make_async_copy(x_ref, rows(*me), local_sem)
        mine.start()
        first = [copy(0, me, sibling, src=x_ref)]
        first += [copy(1 + j, me, (*chip, c), src=x_ref) for j, chip in enumerate(chips)]
        for cp in first:
            cp.start()
        passed = [copy(4 + j, (*chip, c), sibling) for j, chip in enumerate(chips)]
        for j, chip in enumerate(chips):
            copy(1 + j, (*chip, c), me).wait_recv()
            passed[j].start()
        copy(0, sibling, me).wait_recv()
        for j, chip in enumerate(chips):
            copy(4 + j, (*chip, 1 - c), me).wait_recv()
        for cp in first + passed:
            cp.wait_send()
        mine.wait()
        if reduce:
            acc = all_ref[pl.ds(0, m_per), :]
            for i in range(1, N_DEV):
                acc = acc + all_ref[pl.ds(i * m_per, m_per), :]
            out_ref[...] = acc
        else:
            out_ref[...] = all_ref[...]

    out_rows = m_per if reduce else N_DEV * m_per
    return pl.pallas_call(
        body, name=name,
        in_specs=[pl.BlockSpec(memory_space=pltpu.VMEM)],
        out_specs=pl.BlockSpec(memory_space=pltpu.VMEM),
        out_shape=jax.ShapeDtypeStruct((out_rows, cols), F32),
        scratch_shapes=[pltpu.VMEM((N_DEV * m_per, cols), F32), pltpu.SemaphoreType.DMA((7,)),
                        pltpu.SemaphoreType.DMA((7,)), pltpu.SemaphoreType.DMA],
    )(block)


class _Comm:
    def __init__(self, w_in, w_kv, w_out, chip):
        self.bufs = [_cast_own(w_in, chip, "cast_w_in"), _cast_own(w_kv, chip, "cast_w_mem_kv"),
                     _cast_own(w_out, chip, "cast_w_out")]


    def gather_started(self):
        (self.sems,), (self.b_in,), tok = _gather_start(self.bufs[:1], (NEAR,), "gather_start_in_near")
        return [tok]

    def w_in_own(self):
        return self.b_in.reshape(NCHIP * D, WB)

    def w_in_near(self, after):
        (b_in,) = _gather_wait([self.b_in], self.sems, NEAR, after, "gather_wait_in_near")
        (b_in,) = _forward_halves([b_in], NEAR, "forward_in_near")
        (self.sems,), (self.b_in,), tok = _gather_start([b_in], (FAR,), "gather_start_in_far")
        return self.b_in.reshape(NCHIP * D, WB), tok

    def w_in_all(self, after):
        (b_in,) = _gather_wait([self.b_in], self.sems, FAR, after, "gather_wait_in_far")
        (b_in,) = _forward_halves([b_in], FAR, "forward_in_far")
        (self.sems,), self.b_rest, tok = _gather_start(self.bufs[1:], (NEAR + FAR,), "gather_start_rest")
        return b_in.reshape(NCHIP * D, WB), tok

    def w_rest(self, after):
        b_rest = _gather_wait(self.b_rest, self.sems, NEAR + FAR, after, "gather_wait_rest")
        b_kv, b_out = _forward_halves(b_rest, NEAR + FAR, "forward_rest")
        return b_kv.reshape(D, 2 * XW), b_out.reshape(D, D)

    def reduce_start(self, parts, tag):
        got = _sibling_halves(parts, "sibling_halves_" + tag)
        qs = [_pair_sum(p, g, f"pair_sum_{tag}{i}") for i, (p, g) in enumerate(zip(parts, got))]
        return _scatter_start(qs, "scatter_start_" + tag)

    def sibling_start(self, send, tag):
        return _sibling_start(send, "sibling_start_" + tag)

    def reduce_start_summed(self, keep, sent, tag):
        got = _sibling_wait(sent, keep, "sibling_wait_" + tag)
        return _scatter_start([_pair_sum_rows(keep, got, "pair_sum_" + tag)], "scatter_start_" + tag)

    def reduce_finish(self, state, tag, after):
        send_sems, recv_sems, qs, lands, _ = state
        qs, lands = _scatter_wait(qs, lands, send_sems, recv_sems, after, "scatter_wait_" + tag)
        halves = [_chip_sum(q, l, f"chip_sum_{tag}{i}") for i, (q, l) in enumerate(zip(qs, lands))]
        return [j.reshape(2 * j.shape[1], j.shape[2]) for j in _join_halves(halves, "join_halves_" + tag)]


def kernel(x, mem, pre_norm, w_in, conv_w, mem_norm, w_mem_kv, w_out, post_norm, loss_target, m_pre_norm, m_w_in, m_conv_w, m_mem_norm, m_w_mem_kv, m_w_out, m_post_norm, v_pre_norm, v_w_in, v_conv_w, v_mem_norm, v_w_mem_kv, v_w_out, v_post_norm):
    chip = 2 * lax.axis_index("x") + lax.axis_index("y")

    cw_blk = jnp.zeros((8, 384), F32).at[:3].set(conv_w[0])
    cw_all = _gather_small(cw_blk, False, "gather_conv_w").reshape(NCHIP, 2, 8, 384)[:, 0, :3]
    conv_full = jnp.transpose(cw_all, (1, 0, 2)).reshape(3, CW)

    comm = _Comm(w_in[0], w_mem_kv[0], w_out[0], chip)
    loss, grad_x, d_pre, d_mem, d_post, d_conv, r_in, r_kv, r_out = _local_step(
        x, mem, pre_norm, conv_full, mem_norm, post_norm, loss_target, chip, lax.axis_index("c"), comm)

    pack = jnp.concatenate([d_pre, d_mem, d_post, jnp.pad(d_conv, ((0, 0), (0, D - CW))),
                            jnp.pad(loss, ((0, 0), (0, D - 128))), jnp.zeros((1, D), F32)], axis=0)
    tot = _gather_small(pack, True, "reduce_small")
    g_pre, g_mem, g_post = tot[0:1], tot[1:2], tot[2:3]
    g_conv = lax.dynamic_slice(tot[3:6, :CW], (0, chip * 384), (3, 384))
    loss_out = tot[6, 0]

    names = ("pre_norm", "w_in", "conv_w", "mem_norm", "w_mem_kv", "w_out", "post_norm")
    ws = (pre_norm, w_in[0], conv_w[0], mem_norm, w_mem_kv[0], w_out[0], post_norm)
    gs = (g_pre, r_in, g_conv, g_mem, r_kv, r_out, g_post)
    ms = (m_pre_norm, m_w_in[0], m_conv_w[0], m_mem_norm, m_w_mem_kv[0], m_w_out[0], m_post_norm)
    vs = (v_pre_norm, v_w_in[0], v_conv_w[0], v_mem_norm, v_w_mem_kv[0], v_w_out[0], v_post_norm)
    upd = [_adamw(w, g, m, v, "adamw_" + nm) for nm, w, g, m, v in zip(names, ws, gs, ms, vs)]

    def shaped(arrs):
        return [a.reshape(w.shape) if w.ndim == a.ndim else a.reshape((1,) + a.shape)
                for a, w in zip(arrs, (pre_norm, w_in, conv_w, mem_norm, w_mem_kv, w_out, post_norm))]

    grads = shaped([u[0] for u in upd])
    deltas = shaped([u[1] for u in upd])
    new_m = shaped([u[2] for u in upd])
    new_v = shaped([u[3] for u in upd])
    return (loss_out, grad_x, *grads, *deltas, *new_m, *new_v)
```

```python
import functools

import jax
import jax.numpy as jnp
from jax import lax
from jax.experimental import pallas as pl
from jax.experimental.pallas import tpu as pltpu

F32 = jnp.float32
BF16 = jnp.bfloat16

D = 4096
S = 2048
NB = 2
T = NB * S
MLEN = 256
HD = 128
AW = 1536
CW = 1536
XW = 1024
XHD = 256
NXH = 4
NC = 14336
QA, KA, VA, ZA, UC, BC, CC, ZC, QX, ZX = 0, 1536, 3072, 4608, 6144, 7680, 9216, 10752, 12288, 13312
NCHIP = 4
WB = NC // NCHIP
DIL = (1, 4, 16)
HPG = 4
EPS = 1e-6
NEG = -1e30
ROPE_THETA = 10000.0
A_SCALE = HD ** -0.5
X_SCALE = XHD ** -0.5

ADAM_LR = 0.001
ADAM_B1 = 0.9
ADAM_B2 = 0.999
ADAM_EPS = 1e-08
ADAM_WD = 0.01
ADAM_STEP = 10

MESH = pl.DeviceIdType.MESH
MIB = 1024 * 1024


def _cp(sem, vmem_mib=48):
    return pltpu.CompilerParams(dimension_semantics=sem, vmem_limit_bytes=vmem_mib * MIB)


def _sigmoid(z):
    return 1.0 / (1.0 + jnp.exp(-z))


def _rope(x, cos, sin, half):
    return x * cos + pltpu.roll(x, half, 1) * sin


def _rope_t(g, cos, sin, half):
    return g * cos + pltpu.roll(g * sin, half, 1)


def _rms_fwd(x2, g, name, dep=None):
    rows = x2.shape[0]
    tr = 256
    deps = [] if dep is None else list(dep)

    def body(x_ref, g_ref, *rest):
        o_ref = rest[-1]
        x = x_ref[...]
        r = lax.rsqrt(jnp.mean(x * x, axis=-1, keepdims=True) + EPS)
        o_ref[...] = (x * r * g_ref[...]).astype(BF16)

    return pl.pallas_call(
        body, name=name, grid=(rows // tr,),
        in_specs=[pl.BlockSpec((tr, D), lambda i: (i, 0)), pl.BlockSpec((1, D), lambda i: (0, 0))]
        + [pl.BlockSpec(memory_space=pl.ANY)] * len(deps),
        out_specs=pl.BlockSpec((tr, D), lambda i: (i, 0)),
        out_shape=jax.ShapeDtypeStruct((rows, D), BF16),
        compiler_params=_cp(("parallel",)),
    )(x2, g, *deps)


def _norm_gain_grad(dn, x2, name):
    rows = x2.shape[0]
    tr = 256

    def body(dn_ref, x_ref, dg_ref):
        @pl.when(pl.program_id(0) == 0)
        def _():
            dg_ref[...] = jnp.zeros_like(dg_ref)
        x = x_ref[...]
        r = lax.rsqrt(jnp.mean(x * x, axis=-1, keepdims=True) + EPS)
        dg_ref[...] += jnp.sum(dn_ref[...] * (x * r), axis=0, keepdims=True)

    return pl.pallas_call(
        body, name=name, grid=(rows // tr,),
        in_specs=[pl.BlockSpec((tr, D), lambda i: (i, 0)), pl.BlockSpec((tr, D), lambda i: (i, 0))],
        out_specs=pl.BlockSpec((1, D), lambda i: (0, 0)),
        out_shape=jax.ShapeDtypeStruct((1, D), F32),
        compiler_params=_cp(("arbitrary",)),
    )(dn, x2)


def _pre_norm_bwd(dh, x2, g, dout):
    tr = 256

    def body(dh_ref, x_ref, g_ref, dout_ref, gx_ref, dg_ref):
        @pl.when(pl.program_id(0) == 0)
        def _():
            dg_ref[...] = jnp.zeros_like(dg_ref)
        x = x_ref[...]
        dh_ = dh_ref[...]
        r = lax.rsqrt(jnp.mean(x * x, axis=-1, keepdims=True) + EPS)
        xhat = x * r
        dg_ref[...] += jnp.sum(dh_ * xhat, axis=0, keepdims=True)
        dxn = dh_ * g_ref[...]
        gx_ref[...] = dout_ref[...] + r * (dxn - xhat * jnp.mean(dxn * xhat, axis=-1, keepdims=True))

    row = pl.BlockSpec((tr, D), lambda i: (i, 0))
    vec = pl.BlockSpec((1, D), lambda i: (0, 0))
    return pl.pallas_call(
        body, name="pre_norm_bwd", grid=(T // tr,),
        in_specs=[row, row, vec, row],
        out_specs=[row, vec],
        out_shape=[jax.ShapeDtypeStruct((T, D), F32), jax.ShapeDtypeStruct((1, D), F32)],
        compiler_params=_cp(("arbitrary",)),
    )(dh, x2, g, dout)


def _post_norm_loss(y, x2, tgt, g):
    tr = 256

    def body(y_ref, x_ref, t_ref, g_ref, dy_ref, dout_ref, dg_ref, loss_ref):
        @pl.when(pl.program_id(0) == 0)
        def _():
            dg_ref[...] = jnp.zeros_like(dg_ref)
            loss_ref[...] = jnp.zeros_like(loss_ref)
        yv = y_ref[...]
        gv = g_ref[...]
        r = lax.rsqrt(jnp.mean(yv * yv, axis=-1, keepdims=True) + EPS)
        yhat = yv * r
        err = x_ref[...] + yhat * gv - t_ref[...]
        loss_ref[...] += jnp.sum(jnp.sum(err * err, axis=1, keepdims=True), axis=0, keepdims=True) * (0.5 / D)
        dout = err * (1.0 / D)
        dout_ref[...] = dout
        dg_ref[...] += jnp.sum(dout * yhat, axis=0, keepdims=True)
        dyn = dout * gv
        dy_ref[...] = (r * (dyn - yhat * jnp.mean(dyn * yhat, axis=-1, keepdims=True))).astype(BF16)

    row = pl.BlockSpec((tr, D), lambda i: (i, 0))
    vec = pl.BlockSpec((1, D), lambda i: (0, 0))
    return pl.pallas_call(
        body, name="post_norm_loss", grid=(T // tr,),
        in_specs=[row, row, row, vec],
        out_specs=[row, row, vec, pl.BlockSpec((1, 128), lambda i: (0, 0))],
        out_shape=[jax.ShapeDtypeStruct((T, D), BF16), jax.ShapeDtypeStruct((T, D), F32),
                   jax.ShapeDtypeStruct((1, D), F32), jax.ShapeDtypeStruct((1, 128), F32)],
        compiler_params=_cp(("arbitrary",)),
    )(y, x2, tgt, g)


NN = (((1,), (0,)), ((), ()))
NT = (((1,), (1,)), ((), ()))
TN = (((0,), (0,)), ((), ()))


def _as_index(v):
    return jnp.reshape(v, (1,)).astype(jnp.int32)


def _matmul(a, b, *, name, dims, grid, a_block, a_map, b_block, b_map, o_block, o_map, out_shape, out_dtype=F32,
            index=None, prev=None, deps=()):
    extra = ([] if prev is None else [prev]) + [d for d in deps if d is not None]
    first = 0 if index is None else 1
    nk = grid[2]
    in_place = out_dtype == F32

    def body(*refs):
        a_ref, b_ref, o_ref = refs[first], refs[first + 1], refs[first + 2 + len(extra)]
        acc_ref = o_ref if in_place else refs[-1]

        @pl.when(pl.program_id(2) == 0)
        def _():
            acc_ref[...] = jnp.zeros_like(acc_ref)

        acc_ref[...] += lax.dot_general(a_ref[...], b_ref[...], dims, preferred_element_type=F32)

        if not in_place:
            @pl.when(pl.program_id(2) == nk - 1)
            def _():
                o_ref[...] = acc_ref[...].astype(o_ref.dtype)

    in_specs = [pl.BlockSpec(a_block, a_map), pl.BlockSpec(b_block, b_map)] + [pl.BlockSpec(memory_space=pl.ANY)] * len(extra)
    out_specs = pl.BlockSpec(o_block, o_map)
    scratch = [] if in_place else [pltpu.VMEM(o_block, F32)]
    kwargs = dict(name=name, out_shape=jax.ShapeDtypeStruct(out_shape, out_dtype),
                  input_output_aliases={} if prev is None else {first + 2: 0},
                  compiler_params=_cp(("parallel", "parallel", "arbitrary"), vmem_mib=56))
    if index is None:
        return pl.pallas_call(body, grid=grid, in_specs=in_specs, out_specs=out_specs, scratch_shapes=scratch,
                              **kwargs)(a, b, *extra)
    grid_spec = pltpu.PrefetchScalarGridSpec(num_scalar_prefetch=1, grid=grid, in_specs=in_specs, out_specs=out_specs,
                                             scratch_shapes=scratch)
    return pl.pallas_call(body, grid_spec=grid_spec, **kwargs)(_as_index(index), a, b, *extra)


def _mm_nn(a, b, name, tm, tn, tk):
    m, kd = a.shape
    n = b.shape[1]
    return _matmul(a, b, name=name, dims=NN, grid=(m // tm, n // tn, kd // tk),
                   a_block=(tm, tk), a_map=lambda i, j, k: (i, k),
                   b_block=(tk, tn), b_map=lambda i, j, k: (k, j),
                   o_block=(tm, tn), o_map=lambda i, j, k: (i, j), out_shape=(m, n))


def _mm_nt(a, b, name, tm, tn, tk):
    m, kd = a.shape
    n = b.shape[0]
    return _matmul(a, b, name=name, dims=NT, grid=(m // tm, n // tn, kd // tk),
                   a_block=(tm, tk), a_map=lambda i, j, k: (i, k),
                   b_block=(tn, tk), b_map=lambda i, j, k: (j, k),
                   o_block=(tm, tn), o_map=lambda i, j, k: (i, j), out_shape=(m, n))


def _mm_tn(a, b, name, tm, tn, tk):
    kd, m = a.shape
    n = b.shape[1]
    return _matmul(a, b, name=name, dims=TN, grid=(m // tm, n // tn, kd // tk),
                   a_block=(tk, tm), a_map=lambda i, j, k: (k, i),
                   b_block=(tk, tn), b_map=lambda i, j, k: (k, j),
                   o_block=(tm, tn), o_map=lambda i, j, k: (i, j), out_shape=(m, n))


W_TN = 1792
W_NJ = WB // W_TN


def _proj_part(h, wg, chip, masks, name, prev=None, dep=None):
    tm, tk = 1024, 2048

    def blk(j, ix):
        m = masks[0]
        for t in range(1, len(masks)):
            m = jnp.where(j // W_NJ == t, masks[t], m)
        return jnp.bitwise_xor(ix[0], m)

    return _matmul(h, wg, name=name, dims=NN, grid=(T // tm, len(masks) * W_NJ, D // tk), index=chip, prev=prev, deps=(dep,),
                   a_block=(tm, tk), a_map=lambda i, j, k, ix: (i, k),
                   b_block=(tk, W_TN), b_map=lambda i, j, k, ix: (blk(j, ix) * (D // tk) + k, j % W_NJ),
                   o_block=(tm, W_TN), o_map=lambda i, j, k, ix: (i, blk(j, ix) * W_NJ + j % W_NJ), out_shape=(T, NC))


def _dh(dproj, wg, dep=None):
    tm, tn = 1024, 1024
    return _matmul(dproj, wg, name="dh", dims=NT, grid=(T // tm, D // tn, NC // W_TN), deps=(dep,),
                   a_block=(tm, W_TN), a_map=lambda i, j, k: (i, k),
                   b_block=(tn, W_TN), b_map=lambda i, j, k: ((k // W_NJ) * (D // tn) + j, k % W_NJ),
                   o_block=(tm, tn), o_map=lambda i, j, k: (i, j), out_shape=(T, D))


def _grad_w_out(cat, dy, half, out_dtype, name, dep=None):
    tm, tn, tk = D // NCHIP // 2, 2048, 2048
    return _matmul(cat, dy, name=name, dims=TN, grid=(NCHIP, D // tn, T // tk), index=half, deps=(dep,),
                   a_block=(tk, tm), a_map=lambda i, j, k, ix: (k, 2 * i + ix[0]),
                   b_block=(tk, tn), b_map=lambda i, j, k, ix: (k, j),
                   o_block=(tm, tn), o_map=lambda i, j, k, ix: (i, j),
                   out_shape=(NCHIP * tm, D), out_dtype=out_dtype)


def _grad_w_in(h, dproj, half, out_dtype, name, dep=None):
    tm, tk = 1024, 2048
    nh = D // 2 // tm
    return _matmul(h, dproj, name=name, dims=TN, grid=(nh, NC // W_TN, T // tk), index=half, deps=(dep,),
                   a_block=(tk, tm), a_map=lambda i, j, k, ix: (k, ix[0] * nh + i),
                   b_block=(tk, W_TN), b_map=lambda i, j, k, ix: (k, j),
                   o_block=(tm, W_TN), o_map=lambda i, j, k, ix: ((j // W_NJ) * nh + i, j % W_NJ),
                   out_shape=(NCHIP * D // 2, WB), out_dtype=out_dtype)


def _rope_tables(pos, half):
    inv = 1.0 / (ROPE_THETA ** (jnp.arange(half, dtype=F32) / half))
    ang = pos.astype(F32)[:, None] * inv[None, :]
    cos, sin = jnp.cos(ang), jnp.sin(ang)
    return jnp.concatenate([cos, cos], axis=1), jnp.concatenate([-sin, sin], axis=1)


def _band_mask(r0):
    qi = lax.broadcasted_iota(jnp.int32, (128, 256), 0)
    kk = lax.broadcasted_iota(jnp.int32, (128, 256), 1)
    return (kk >= qi) & (kk <= qi + 128) & (kk + r0 >= 128)


def _window(r0, nblk):
    if nblk == 1:
        qi = lax.broadcasted_iota(jnp.int32, (128, 128), 0)
        kk = lax.broadcasted_iota(jnp.int32, (128, 128), 1)
        return pl.ds(128, 128), kk <= qi
    return pl.ds(r0, 256), _band_mask(r0)


def _dil_rows(r, n, d):
    if d == 1:
        return pl.ds(pl.multiple_of(n * 128, 128), 128)
    return pl.ds(r + d * 128 * n, 128, stride=d)


def _attn_fwd(proj, cosf, sinf, g):
    d = DIL[g]
    ln = S // d
    nblk = ln // 128
    proj_v = proj.reshape(NB, S, NC)

    def body(q_ref, k_ref, v_ref, cos_ref, sin_ref, o_ref, l_ref, k_s, v_s):
        k_s[:, pl.ds(0, 128), :] = jnp.zeros((d, 128, HD), BF16)
        v_s[:, pl.ds(0, 128), :] = jnp.zeros((d, 128, HD), BF16)

        def prep(i, carry):
            r, n = i // nblk, i % nblk
            rows = _dil_rows(r, n, d)
            dst = pl.ds(pl.multiple_of(n * 128 + 128, 128), 128)
            k_s[r, dst, :] = _rope(k_ref[rows, :], cos_ref[rows, :], sin_ref[rows, :], HD // 2).astype(BF16)
            v_s[r, dst, :] = v_ref[rows, :].astype(BF16)
            return carry

        lax.fori_loop(0, d * nblk, prep, 0, unroll=4)

        def step(i, carry):
            r, n = i // nblk, i % nblk
            rows = _dil_rows(r, n, d)
            r0 = pl.multiple_of(n * 128, 128)
            qr = _rope(q_ref[rows, :], cos_ref[rows, :], sin_ref[rows, :], HD // 2).astype(BF16)
            win, mask = _window(r0, nblk)
            kw = k_s[r, win, :]
            vw = v_s[r, win, :]
            sc = lax.dot_general(qr, kw, NT, preferred_element_type=F32) * A_SCALE
            sc = jnp.where(mask, sc, NEG)
            m = jnp.max(sc, axis=1, keepdims=True)
            p = jnp.exp(sc - m)
            l = jnp.sum(p, axis=1, keepdims=True)
            o_ref[rows, :] = jnp.dot(p.astype(BF16), vw, preferred_element_type=F32) / l
            l_ref[rows, :] = jnp.broadcast_to(m + jnp.log(l), (128, HD))
            return carry

        lax.fori_loop(0, d * nblk, step, 0, unroll=4)

    def col(off):
        return lambda b, h: (b, 0, off // HD + HPG * g + h)

    blk = (None, S, HD)
    tab = pl.BlockSpec((S, HD), lambda b, h: (0, 0))
    out = pl.BlockSpec(blk, lambda b, h: (b, 0, h))
    o, l = pl.pallas_call(
        body, name=f"attn_fwd_d{d}", grid=(NB, HPG),
        in_specs=[pl.BlockSpec(blk, col(QA)), pl.BlockSpec(blk, col(KA)), pl.BlockSpec(blk, col(VA)), tab, tab],
        out_specs=[out, out],
        out_shape=[jax.ShapeDtypeStruct((NB, S, HPG * HD), F32)] * 2,
        scratch_shapes=[pltpu.VMEM((d, ln + 128, HD), BF16), pltpu.VMEM((d, ln + 128, HD), BF16)],
        compiler_params=_cp(("parallel", "parallel")),
    )(proj_v, proj_v, proj_v, cosf, sinf)
    return o.reshape(T, HPG * HD), l.reshape(T, HPG * HD)


def _attn_bwd(proj, cosf, sinf, da, lse, delta, dproj, g):
    d = DIL[g]
    ln = S // d
    nblk = ln // 128
    proj_v = proj.reshape(NB, S, NC)
    dproj_v = dproj.reshape(NB, S, NC)
    da_v = da.reshape(NB, S, AW)
    lse_v = lse.reshape(NB, S, HPG * HD)
    delta_v = delta.reshape(NB, S, HPG * HD)

    def body(q_ref, k_ref, v_ref, cos_ref, sin_ref, da_ref, lse_ref, dl_ref, dp_in_ref, o_ref,
             k_s, v_s, stg, dk_s, dv_s):
        del dp_in_ref
        w = pl.program_id(2)

        def emit():
            def cast(n, carry):
                rows = pl.ds(pl.multiple_of(n * 256, 256), 256)
                o_ref[rows, :] = stg[rows, :].astype(BF16)
                return carry

            lax.fori_loop(0, S // 256, cast, 0)

        @pl.when(w == 0)
        def _():
            k_s[:, pl.ds(0, 128), :] = jnp.zeros((d, 128, HD), BF16)
            v_s[:, pl.ds(0, 128), :] = jnp.zeros((d, 128, HD), BF16)
            dk_s[...] = jnp.zeros_like(dk_s)
            dv_s[...] = jnp.zeros_like(dv_s)

            def prep(i, carry):
                r, n = i // nblk, i % nblk
                rows = _dil_rows(r, n, d)
                dst = pl.ds(pl.multiple_of(n * 128 + 128, 128), 128)
                k_s[r, dst, :] = _rope(k_ref[rows, :], cos_ref[rows, :], sin_ref[rows, :], HD // 2).astype(BF16)
                v_s[r, dst, :] = v_ref[rows, :].astype(BF16)
                return carry

            lax.fori_loop(0, d * nblk, prep, 0, unroll=4)

            def step(i, carry):
                r, n = i // nblk, i % nblk
                rows = _dil_rows(r, n, d)
                r0 = pl.multiple_of(n * 128, 128)
                win, mask = _window(r0, nblk)
                cos, sin = cos_ref[rows, :], sin_ref[rows, :]
                qr = _rope(q_ref[rows, :], cos, sin, HD // 2).astype(BF16)
                kw = k_s[r, win, :]
                vw = v_s[r, win, :]
                sc = lax.dot_general(qr, kw, NT, preferred_element_type=F32) * A_SCALE
                sc = jnp.where(mask, sc, NEG)
                p = jnp.exp(sc - lse_ref[rows, :][:, :1])
                da_b = da_ref[rows, :].astype(BF16)
                dp = lax.dot_general(da_b, vw, NT, preferred_element_type=F32)
                ds_b = (p * (dp - dl_ref[rows, :][:, :1]) * A_SCALE).astype(BF16)
                p_b = p.astype(BF16)
                dq = jnp.dot(ds_b, kw, preferred_element_type=F32)
                stg[rows, :] = _rope_t(dq, cos, sin, HD // 2)
                dk_s[r, win, :] += lax.dot_general(ds_b, qr, TN, preferred_element_type=F32)
                dv_s[r, win, :] += lax.dot_general(p_b, da_b, TN, preferred_element_type=F32)
                return carry

            lax.fori_loop(0, d * nblk, step, 0, unroll=4)
            emit()

        @pl.when(w == 1)
        def _():
            def put(i, carry):
                r, n = i // nblk, i % nblk
                rows = _dil_rows(r, n, d)
                src = pl.ds(pl.multiple_of(n * 128 + 128, 128), 128)
                stg[rows, :] = _rope_t(dk_s[r, src, :], cos_ref[rows, :], sin_ref[rows, :], HD // 2)
                return carry

            lax.fori_loop(0, d * nblk, put, 0, unroll=4)
            emit()

        @pl.when(w == 2)
        def _():
            def put(i, carry):
                r, n = i // nblk, i % nblk
                src = pl.ds(pl.multiple_of(n * 128 + 128, 128), 128)
                stg[_dil_rows(r, n, d), :] = dv_s[r, src, :]
                return carry

            lax.fori_loop(0, d * nblk, put, 0, unroll=4)
            emit()

    def col(off):
        return lambda b, h, w: (ahead(b, h, w)[0], 0, off // HD + HPG * g + ahead(b, h, w)[1])

    def ahead(b, h, w):
        flat = jnp.minimum(b * HPG + h + jnp.where(w > 0, 1, 0), NB * HPG - 1)
        return flat // HPG, flat % HPG

    blk = (None, S, HD)
    tab = pl.BlockSpec((S, HD), lambda b, h, w: (0, 0))
    per_head = pl.BlockSpec(blk, lambda b, h, w: (ahead(b, h, w)[0], 0, ahead(b, h, w)[1]))
    out = pl.pallas_call(
        body, name=f"attn_bwd_d{d}", grid=(NB, HPG, 3),
        in_specs=[pl.BlockSpec(blk, col(QA)), pl.BlockSpec(blk, col(KA)), pl.BlockSpec(blk, col(VA)), tab, tab,
                  pl.BlockSpec(blk, col(0)), per_head, per_head, pl.BlockSpec(memory_space=pl.ANY)],
        out_specs=pl.BlockSpec(blk, lambda b, h, w: (b, 0, (AW // HD) * w + HPG * g + h)),
        out_shape=jax.ShapeDtypeStruct(dproj_v.shape, BF16),
        input_output_aliases={8: 0},
        scratch_shapes=[pltpu.VMEM((d, ln + 128, HD), BF16), pltpu.VMEM((d, ln + 128, HD), BF16),
                        pltpu.VMEM((S, HD), F32), pltpu.VMEM((d, ln + 128, HD), F32), pltpu.VMEM((d, ln + 128, HD), F32)],
        compiler_params=_cp(("arbitrary",) * 3),
    )(proj_v, proj_v, proj_v, cosf, sinf, da_v, lse_v, delta_v, dproj_v)
    return out.reshape(T, NC)


def _attn_mix(proj, os_, ls_):
    tr = 256
    gw = HPG * HD

    def body(o0, o1, o2, l0, l1, l2, z_ref, cat_ref):
        m = jnp.maximum(jnp.maximum(l0[...], l1[...]), l2[...])
        e = [jnp.exp(l[...] - m) for l in (l0, l1, l2)]
        inv = 1.0 / (e[0] + e[1] + e[2])
        for gi, o in enumerate((o0, o1, o2)):
            z = z_ref[:, gi * gw:(gi + 1) * gw]
            cat_ref[:, gi * gw:(gi + 1) * gw] = (o[...] * (e[gi] * inv) * (z * _sigmoid(z))).astype(BF16)

    grp = pl.BlockSpec((tr, gw), lambda i: (i, 0))
    return pl.pallas_call(
        body, name="attn_mix", grid=(T // tr,),
        in_specs=[grp] * 6 + [pl.BlockSpec((tr, AW), lambda i: (i, ZA // AW))],
        out_specs=pl.BlockSpec((tr, AW), lambda i: (i, 0)),
        out_shape=jax.ShapeDtypeStruct((T, D), BF16),
        compiler_params=_cp(("parallel",)),
    )(*os_, *ls_, proj)


def _attn_mix_bwd(dcat, proj, os_, ls_, dep=None):
    tr = 256
    gw = HPG * HD
    deps = [] if dep is None else [dep]

    def body(dy_ref, o0, o1, o2, l0, l1, l2, z_ref, *rest):
        da_ref, lse_ref, dl_ref, dz_ref = rest[len(deps):]
        m = jnp.maximum(jnp.maximum(l0[...], l1[...]), l2[...])
        e = [jnp.exp(l[...] - m) for l in (l0, l1, l2)]
        den = e[0] + e[1] + e[2]
        inv = 1.0 / den
        lse_ref[...] = m + jnp.log(den)
        acc = jnp.zeros((tr, gw), F32)
        for gi, o in enumerate((o0, o1, o2)):
            cols = slice(gi * gw, (gi + 1) * gw)
            z = z_ref[:, cols]
            dy = dy_ref[:, cols]
            sg = _sigmoid(z)
            a = o[...] * (e[gi] * inv)
            da = dy * (z * sg)
            da_ref[:, cols] = da
            dz_ref[:, cols] = (dy * a * (sg * (1.0 + z * (1.0 - sg)))).astype(BF16)
            acc = acc + da * a
        for hh in range(HPG):
            cols = slice(hh * HD, (hh + 1) * HD)
            dl_ref[:, cols] = jnp.broadcast_to(jnp.sum(acc[:, cols], axis=1, keepdims=True), (tr, HD))

    grp = pl.BlockSpec((tr, gw), lambda i: (i, 0))
    return pl.pallas_call(
        body, name="attn_mix_bwd", grid=(T // tr,),
        in_specs=[pl.BlockSpec((tr, AW), lambda i: (i, 0))] + [grp] * 6 + [pl.BlockSpec((tr, AW), lambda i: (i, ZA // AW))]
        + [pl.BlockSpec(memory_space=pl.ANY)] * len(deps),
        out_specs=[pl.BlockSpec((tr, AW), lambda i: (i, 0)), grp, grp, pl.BlockSpec((tr, AW), lambda i: (i, ZA // AW))],
        out_shape=[jax.ShapeDtypeStruct((T, AW), F32), jax.ShapeDtypeStruct((T, gw), F32),
                   jax.ShapeDtypeStruct((T, gw), F32), jax.ShapeDtypeStruct((T, NC), BF16)],
        compiler_params=_cp(("parallel",)),
    )(dcat, *os_, *ls_, proj, *deps)


CT = 256


def _shift_down(x, n):
    rows = lax.broadcasted_iota(jnp.int32, x.shape, 0)
    return jnp.where(rows >= n, pltpu.roll(x, n, 0), 0.0)


def _shift_up(x, n):
    rows = lax.broadcasted_iota(jnp.int32, x.shape, 0)
    return jnp.where(rows < x.shape[0] - n, pltpu.roll(x, x.shape[0] - n, 0), 0.0)


def _conv_fwd(proj, conv_w, cat):
    proj_v = proj.reshape(NB, S, NC)
    cat_v = cat.reshape(NB, S, D)

    def body(u_ref, b_ref, c_ref, z_ref, w_ref, cat_in, o_ref):
        del cat_in
        cu = c_ref[...] * u_ref[...]
        cv = _shift_down(cu, 2) * w_ref[0:1, :] + _shift_down(cu, 1) * w_ref[1:2, :] + cu * w_ref[2:3, :]
        z = z_ref[...]
        o_ref[...] = (b_ref[...] * cv * (z * _sigmoid(z))).astype(BF16)

    def seg(off):
        return pl.BlockSpec((None, S, CT), lambda b, j: (b, 0, off // CT + j))

    out = pl.pallas_call(
        body, name="conv_fwd", grid=(NB, CW // CT),
        in_specs=[seg(UC), seg(BC), seg(CC), seg(ZC), pl.BlockSpec((3, CT), lambda b, j: (0, j)),
                  pl.BlockSpec(memory_space=pl.ANY)],
        out_specs=pl.BlockSpec((None, S, CT), lambda b, j: (b, 0, AW // CT + j)),
        out_shape=jax.ShapeDtypeStruct((NB, S, D), BF16),
        input_output_aliases={5: 0},
        compiler_params=_cp(("parallel", "parallel")),
    )(proj_v, proj_v, proj_v, proj_v, conv_w, cat_v)
    return out.reshape(T, D)


def _conv_bwd(dcat, proj, conv_w, dproj):
    proj_v = proj.reshape(NB, S, NC)
    dproj_v = dproj.reshape(NB, S, NC)
    dcat_v = dcat.reshape(NB, S, D)

    def body(dy_ref, u_ref, b_ref, c_ref, z_ref, w_ref, dp_in, o_ref, dw_ref, st):
        del dp_in
        b = pl.program_id(1)
        w = pl.program_id(2)

        @pl.when((b == 0) & (w == 0))
        def _():
            dw_ref[...] = jnp.zeros_like(dw_ref)

        @pl.when(w == 0)
        def _():
            u, c, z, bb, dy = u_ref[...], c_ref[...], z_ref[...], b_ref[...], dy_ref[...]
            cu = c * u
            s1 = _shift_down(cu, 1)
            s2 = _shift_down(cu, 2)
            cv = s2 * w_ref[0:1, :] + s1 * w_ref[1:2, :] + cu * w_ref[2:3, :]
            sg = _sigmoid(z)
            sz = z * sg
            dcv = dy * bb * sz
            st[0] = dy * cv * sz
            st[2] = dy * bb * cv * (sg * (1.0 + z * (1.0 - sg)))
            dw_ref[0:1, :] += jnp.sum(dcv * s2, axis=0, keepdims=True)
            dw_ref[1:2, :] += jnp.sum(dcv * s1, axis=0, keepdims=True)
            dw_ref[2:3, :] += jnp.sum(dcv * cu, axis=0, keepdims=True)
            dcu = dcv * w_ref[2:3, :] + _shift_up(dcv, 1) * w_ref[1:2, :] + _shift_up(dcv, 2) * w_ref[0:1, :]
            st[1] = dcu * u
            o_ref[...] = (dcu * c).astype(BF16)

        for k in range(3):
            @pl.when(w == k + 1)
            def _(k=k):
                o_ref[...] = st[k].astype(BF16)

    def ahead(j, b, w):
        flat = jnp.minimum(j * NB + b + jnp.where(w > 0, 1, 0), (CW // CT) * NB - 1)
        return flat // NB, flat % NB

    def seg(off):
        return pl.BlockSpec((None, S, CT), lambda j, b, w: (ahead(j, b, w)[1], 0, off // CT + ahead(j, b, w)[0]))

    out, dw = pl.pallas_call(
        body, name="conv_bwd", grid=(CW // CT, NB, 4),
        in_specs=[seg(AW), seg(UC), seg(BC), seg(CC), seg(ZC),
                  pl.BlockSpec((3, CT), lambda j, b, w: (0, ahead(j, b, w)[0])),
                  pl.BlockSpec(memory_space=pl.ANY)],
        out_specs=[pl.BlockSpec((None, S, CT), lambda j, b, w: (b, 0, (UC + w * CW) // CT + j)),
                   pl.BlockSpec((3, CT), lambda j, b, w: (0, j))],
        out_shape=[jax.ShapeDtypeStruct((NB, S, NC), BF16), jax.ShapeDtypeStruct((3, CW), F32)],
        input_output_aliases={6: 0},
        scratch_shapes=[pltpu.VMEM((3, S, CT), F32)],
        compiler_params=_cp(("arbitrary",) * 3),
    )(dcat_v, proj_v, proj_v, proj_v, proj_v, conv_w, dproj_v)
    return out.reshape(T, NC), dw


XT = 1024


def _cross_fwd(proj, mkv, cosq, sinq, cosm, sinm, cat):
    proj_v = proj.reshape(NB, S, NC)
    mkv_v = mkv.reshape(NB, MLEN, 2 * XW)
    cat_v = cat.reshape(NB, S, D)

    def body(q_ref, z_ref, mk_ref, mv_ref, cq, sq, cm, sm, cat_in, o_ref):
        del cat_in
        mkr = _rope(mk_ref[...], cm[...], sm[...], XHD // 2).astype(BF16)
        qr = _rope(q_ref[...], cq[...], sq[...], XHD // 2).astype(BF16)
        sc = lax.dot_general(qr, mkr, NT, preferred_element_type=F32) * X_SCALE
        p = jnp.exp(sc - jnp.max(sc, axis=1, keepdims=True))
        p = p / jnp.sum(p, axis=1, keepdims=True)
        ox = jnp.dot(p.astype(BF16), mv_ref[...].astype(BF16), preferred_element_type=F32)
        z = z_ref[...]
        o_ref[...] = (ox * (z * _sigmoid(z))).astype(BF16)

    def seg(off):
        return pl.BlockSpec((None, XT, XHD), lambda b, h, t: (b, t, off // XHD + h))

    qtab = pl.BlockSpec((XT, XHD), lambda b, h, t: (t, 0))
    mtab = pl.BlockSpec((MLEN, XHD), lambda b, h, t: (0, 0))
    out = pl.pallas_call(
        body, name="cross_fwd", grid=(NB, NXH, S // XT),
        in_specs=[seg(QX), seg(ZX),
                  pl.BlockSpec((None, MLEN, XHD), lambda b, h, t: (b, 0, h)),
                  pl.BlockSpec((None, MLEN, XHD), lambda b, h, t: (b, 0, NXH + h)),
                  qtab, qtab, mtab, mtab, pl.BlockSpec(memory_space=pl.ANY)],
        out_specs=pl.BlockSpec((None, XT, XHD), lambda b, h, t: (b, t, (AW + CW) // XHD + h)),
        out_shape=jax.ShapeDtypeStruct((NB, S, D), BF16),
        input_output_aliases={8: 0},
        compiler_params=_cp(("parallel",) * 3),
    )(proj_v, proj_v, mkv_v, mkv_v, cosq, sinq, cosm, sinm, cat_v)
    return out.reshape(T, D)


def _cross_bwd(dcat, proj, mkv, cosq, sinq, cosm, sinm, dproj):
    proj_v = proj.reshape(NB, S, NC)
    dproj_v = dproj.reshape(NB, S, NC)
    dcat_v = dcat.reshape(NB, S, D)
    mkv_v = mkv.reshape(NB, MLEN, 2 * XW)
    nt = S // XT

    def body(dy_ref, q_ref, z_ref, mk_ref, mv_ref, cq, sq, cm, sm, dp_in, o_ref, dmk_ref, dmv_ref, dz_s):
        del dp_in
        t = pl.program_id(2)
        w = pl.program_id(3)

        @pl.when((t == 0) & (w == 0))
        def _():
            dmk_ref[...] = jnp.zeros_like(dmk_ref)
            dmv_ref[...] = jnp.zeros_like(dmv_ref)

        @pl.when(w == 0)
        def _():
            mkr = _rope(mk_ref[...], cm[...], sm[...], XHD // 2).astype(BF16)
            mv_b = mv_ref[...].astype(BF16)
            qr = _rope(q_ref[...], cq[...], sq[...], XHD // 2).astype(BF16)
            sc = lax.dot_general(qr, mkr, NT, preferred_element_type=F32) * X_SCALE
            p = jnp.exp(sc - jnp.max(sc, axis=1, keepdims=True))
            p = p / jnp.sum(p, axis=1, keepdims=True)
            p_b = p.astype(BF16)
            ox = jnp.dot(p_b, mv_b, preferred_element_type=F32)
            z = z_ref[...]
            dy = dy_ref[...]
            sg = _sigmoid(z)
            dz_s[...] = dy * ox * (sg * (1.0 + z * (1.0 - sg)))
            dox_b = (dy * (z * sg)).astype(BF16)
            dp = lax.dot_general(dox_b, mv_b, NT, preferred_element_type=F32)
            ds_b = (p * (dp - jnp.sum(dp * p, axis=1, keepdims=True)) * X_SCALE).astype(BF16)
            dq = jnp.dot(ds_b, mkr, preferred_element_type=F32)
            o_ref[...] = _rope_t(dq, cq[...], sq[...], XHD // 2).astype(BF16)
            dmk_ref[...] += lax.dot_general(ds_b, qr, TN, preferred_element_type=F32)
            dmv_ref[...] += lax.dot_general(p_b, dox_b, TN, preferred_element_type=F32)

        @pl.when(w == 1)
        def _():
            o_ref[...] = dz_s[...].astype(BF16)

        @pl.when((t == nt - 1) & (w == 1))
        def _():
            dmk_ref[...] = _rope_t(dmk_ref[...], cm[...], sm[...], XHD // 2)

    def seg(off):
        return pl.BlockSpec((None, XT, XHD), lambda b, h, t, w: (b, t, off // XHD + h))

    qtab = pl.BlockSpec((XT, XHD), lambda b, h, t, w: (t, 0))
    mtab = pl.BlockSpec((MLEN, XHD), lambda b, h, t, w: (0, 0))
    macc = pl.BlockSpec((None, MLEN, XHD), lambda b, h, t, w: (b, 0, h))
    out, dmk, dmv = pl.pallas_call(
        body, name="cross_bwd", grid=(NB, NXH, nt, 2),
        in_specs=[pl.BlockSpec((None, XT, XHD), lambda b, h, t, w: (b, t, (AW + CW) // XHD + h)),
                  seg(QX), seg(ZX),
                  pl.BlockSpec((None, MLEN, XHD), lambda b, h, t, w: (b, 0, h)),
                  pl.BlockSpec((None, MLEN, XHD), lambda b, h, t, w: (b, 0, NXH + h)),
                  qtab, qtab, mtab, mtab, pl.BlockSpec(memory_space=pl.ANY)],
        out_specs=[pl.BlockSpec((None, XT, XHD), lambda b, h, t, w: (b, t, (QX + w * XW) // XHD + h)), macc, macc],
        out_shape=[jax.ShapeDtypeStruct((NB, S, NC), BF16), jax.ShapeDtypeStruct((NB, MLEN, XW), F32),
                   jax.ShapeDtypeStruct((NB, MLEN, XW), F32)],
        input_output_aliases={9: 0},
        scratch_shapes=[pltpu.VMEM((XT, XHD), F32)],
        compiler_params=_cp(("arbitrary",) * 4),
    )(dcat_v, proj_v, proj_v, mkv_v, mkv_v, cosq, sinq, cosm, sinm, dproj_v)
    return out.reshape(T, NC), dmk, dmv


def _local_step(x, mem, pre_norm, conv_w, mem_norm, post_norm, tgt, chip, core, comm):
    x2 = x.reshape(T, D)
    mem2 = mem.reshape(NB * MLEN, D)
    tgt2 = tgt.reshape(T, D)
    cosa, sina = _rope_tables(jnp.arange(S), HD // 2)
    cosq, sinq = _rope_tables(jnp.arange(S) + MLEN, XHD // 2)
    cosm, sinm = _rope_tables(jnp.arange(MLEN), XHD // 2)

    h = _rms_fwd(x2, pre_norm, "pre_norm_fwd", dep=comm.gather_started())
    memn = _rms_fwd(mem2, mem_norm, "mem_norm_fwd")
    proj = _proj_part(h, comm.w_in_own(), chip, (0,), "proj_own")
    wg_in, tok = comm.w_in_near(after=[proj, memn, conv_w])
    proj = _proj_part(h, wg_in, chip, (REL_XOR[0], REL_XOR[1]), "proj_near", prev=proj, dep=tok)
    wg_in, tok = comm.w_in_all(after=proj)
    proj = _proj_part(h, wg_in, chip, (REL_XOR[2],), "proj_far", prev=proj, dep=tok)
    wg_kv, wg_out = comm.w_rest(after=proj)
    mkv = _mm_nn(memn, wg_kv, "mkv", NB * MLEN, 1024, 512)

    fw = [_attn_fwd(proj, cosa, sina, g) for g in range(3)]
    os_ = [f[0] for f in fw]
    ls_ = [f[1] for f in fw]
    cat = _attn_mix(proj, os_, ls_)
    cat = _conv_fwd(proj, conv_w, cat)
    cat = _cross_fwd(proj, mkv, cosq, sinq, cosm, sinm, cat)
    y = _mm_nn(cat, wg_out, "out_proj", 1024, 1024, 2048)
    dy, dout, d_post, loss = _post_norm_loss(y, x2, tgt2, post_norm)

    dcat = _mm_nt(dy, wg_out, "dcat", 1024, 1024, 2048)
    g_send = _grad_w_out(cat, dy, 1 - core, BF16, "grad_w_out_send")
    sent = comm.sibling_start(g_send.reshape(NCHIP, D // NCHIP // 2, D), "a")
    g_keep = _grad_w_out(cat, dy, core, F32, "grad_w_out_keep", dep=sent[-1])
    red_a = comm.reduce_start_summed(g_keep.reshape(NCHIP, D // NCHIP // 2, D), sent, "a")
    da, lse, delta, dproj = _attn_mix_bwd(dcat, proj, os_, ls_, dep=red_a[-1])
    for g in range(3):
        dproj = _attn_bwd(proj, cosa, sina, da, lse, delta, dproj, g)
    dproj, d_conv = _conv_bwd(dcat, proj, conv_w, dproj)
    dproj, dmk, dmv = _cross_bwd(dcat, proj, mkv, cosq, sinq, cosm, sinm, dproj)

    dmkv = jnp.concatenate([dmk, dmv], axis=-1).reshape(NB * MLEN, 2 * XW)
    dmkv_b = dmkv.astype(BF16)
    g_kv = _mm_tn(memn, dmkv_b, "grad_w_mem_kv", 1024, 1024, NB * MLEN)
    dmemn = _mm_nt(dmkv_b, wg_kv, "dmemn", NB * MLEN, 1024, 512)
    d_mem = _norm_gain_grad(dmemn, mem2, "mem_norm_bwd")

    red_k = comm.reduce_start([g_kv.reshape(NCHIP, D // NCHIP, 2 * XW)], "k")
    g_send = _grad_w_in(h, dproj, 1 - core, BF16, "grad_w_in_send", dep=red_k[-1])
    sent = comm.sibling_start(g_send.reshape(NCHIP, D // 2, WB), "b")
    g_keep = _grad_w_in(h, dproj, core, F32, "grad_w_in_keep", dep=sent[-1])
    red_b = comm.reduce_start_summed(g_keep.reshape(NCHIP, D // 2, WB), sent, "b")
    dh = _dh(dproj, wg_in, dep=red_b[-1])
    grad_x, d_pre = _pre_norm_bwd(dh, x2, pre_norm, dout)
    r_in = comm.reduce_finish_start(red_b, "b", after=grad_x)
    (r_out,) = comm.reduce_finish(red_a, "a", after=grad_x)
    (r_kv,) = comm.reduce_finish(red_k, "k", after=grad_x)
    return loss, grad_x.reshape(NB, S, D), d_pre, d_mem, d_post, d_conv, r_in, r_kv, r_out


def _adamw(w, g, m, v, name):
    rows, cols = w.shape
    tr = rows if rows <= 512 else 512
    tc = cols if cols <= 1024 else 1024
    if cols % tc:
        tc = 896

    def body(w_ref, g_ref, m_ref, v_ref, go_ref, d_ref, nm_ref, nv_ref):
        gv = g_ref[...]
        go_ref[...] = gv
        nm = ADAM_B1 * m_ref[...] + (1.0 - ADAM_B1) * gv
        nv = ADAM_B2 * v_ref[...] + (1.0 - ADAM_B2) * (gv * gv)
        m_hat = nm / (1.0 - ADAM_B1 ** ADAM_STEP)
        v_hat = nv / (1.0 - ADAM_B2 ** ADAM_STEP)
        d_ref[...] = -ADAM_LR * (m_hat / (jnp.sqrt(v_hat) + ADAM_EPS) + ADAM_WD * w_ref[...])
        nm_ref[...] = nm
        nv_ref[...] = nv

    blk = pl.BlockSpec((tr, tc), lambda i, j: (i, j))
    sds = jax.ShapeDtypeStruct((rows, cols), F32)
    return pl.pallas_call(
        body, name=name, grid=(rows // tr, cols // tc),
        in_specs=[blk] * 4, out_specs=[blk] * 4, out_shape=[sds] * 4,
        compiler_params=_cp(("parallel", "parallel")),
    )(w, g, m, v)


def _place():
    return lax.axis_index("x"), lax.axis_index("y"), lax.axis_index("c")


def _other_chips(x, y):
    return [(1 - x, y), (x, 1 - y), (1 - x, 1 - y)]


def _tile_cols(cols):
    return cols if cols <= 1024 else (1024 if cols % 1024 == 0 else 896)


def _cast_own(w, chip, name):
    rows, cols = w.shape
    tr, tc = 512, _tile_cols(cols)

    def body(chip_ref, w_ref, o_ref):
        del chip_ref
        o_ref[...] = w_ref[...].astype(BF16)

    grid_spec = pltpu.PrefetchScalarGridSpec(
        num_scalar_prefetch=1, grid=(rows // tr, cols // tc),
        in_specs=[pl.BlockSpec((tr, tc), lambda i, j, cr: (i, j))],
        out_specs=pl.BlockSpec((None, tr, tc), lambda i, j, cr: (cr[0], i, j)))
    return pl.pallas_call(
        body, name=name, grid_spec=grid_spec,
        out_shape=jax.ShapeDtypeStruct((NCHIP, rows, cols), BF16),
        compiler_params=_cp(("parallel", "parallel")),
    )(_as_index(chip), w)


HBM_SPEC = pl.BlockSpec(memory_space=pltpu.HBM)
SEM_SPEC = pl.BlockSpec(memory_space=pltpu.SEMAPHORE)
ANY_SPEC = pl.BlockSpec(memory_space=pl.ANY)
EFFECT = pltpu.SideEffectType.DATAFLOW_SIDE_EFFECTING
TOKEN = jax.ShapeDtypeStruct((8, 128), F32)


def _half(ref, chip, hc):
    hr = ref.shape[1] // 2
    return ref.at[chip, pl.ds(hc * hr, hr), :]


NEAR = (0, 1)
FAR = (2,)
REL_XOR = (2, 1, 3)


def _gather_copies(refs, send_sems, recv_sems, rels):
    x, y, c = _place()
    chips = _other_chips(x, y)
    out, inc = [], []
    for a, ref in enumerate(refs):
        for p, j in enumerate(rels):
            px, py = chips[j]
            mine = _half(ref, 2 * x + y, c)
            theirs = _half(ref, 2 * px + py, c)
            sems = dict(send_sem=send_sems.at[len(rels) * a + p], recv_sem=recv_sems.at[len(rels) * a + p],
                        device_id=(px, py, c), device_id_type=MESH)
            out.append(pltpu.make_async_remote_copy(src_ref=mine, dst_ref=mine, **sems))
            inc.append(pltpu.make_async_remote_copy(src_ref=theirs, dst_ref=theirs, **sems))
    return out, inc


def _gather_start(bufs, groups, name):
    n = len(bufs)
    ng = len(groups)

    def body(*refs):
        ins = refs[:n]
        token = refs[-1]
        for gi, rels in enumerate(groups):
            out, _ = _gather_copies(ins, refs[n + 2 * gi], refs[n + 2 * gi + 1], rels)
            for cp in out:
                cp.start()
        token[...] = jnp.zeros_like(token)

    sems = []
    for rels in groups:
        sems += [pltpu.SemaphoreType.DMA((len(rels) * n,))] * 2
    res = pl.pallas_call(
        body, name=name,
        in_specs=[HBM_SPEC] * n,
        out_specs=[SEM_SPEC] * (2 * ng) + [HBM_SPEC] * n + [pl.BlockSpec(memory_space=pltpu.VMEM)],
        out_shape=sems + [pltpu.HBM(b.shape, b.dtype) for b in bufs] + [TOKEN],
        input_output_aliases={a: 2 * ng + a for a in range(n)},
        compiler_params=pltpu.CompilerParams(has_side_effects=EFFECT),
    )(*[pltpu.with_memory_space_constraint(b, pltpu.HBM) for b in bufs])
    return [(res[2 * gi], res[2 * gi + 1]) for gi in range(ng)], list(res[2 * ng:2 * ng + n]), res[-1]


def _gather_wait(bufs, sems, rels, after, name):
    n = len(bufs)
    send_sems, recv_sems = sems
    after = list(after) if isinstance(after, (list, tuple)) else [after]

    def body(*refs):
        ins = refs[:n]
        out, inc = _gather_copies(ins, refs[n], refs[n + 1], rels)
        for cp in out:
            cp.wait_send()
        for cp in inc:
            cp.wait_recv()

    return pl.pallas_call(
        body, name=name,
        in_specs=[HBM_SPEC] * n + [SEM_SPEC, SEM_SPEC] + [ANY_SPEC] * len(after),
        out_specs=[HBM_SPEC] * n,
        out_shape=[pltpu.HBM(b.shape, b.dtype) for b in bufs],
        input_output_aliases={a: a for a in range(n)},
        compiler_params=pltpu.CompilerParams(has_side_effects=EFFECT),
    )(*bufs, send_sems, recv_sems, *after)


def _forward_halves(bufs, rels, name):
    n = len(bufs)

    def body(*refs):
        outs = refs[n:2 * n]
        send_sems, recv_sems = refs[2 * n:]
        x, y, c = _place()
        chips = _other_chips(x, y)
        cps, waits = [], []
        for a in range(n):
            for p, j in enumerate(rels):
                px, py = chips[j]
                sems = dict(send_sem=send_sems.at[len(rels) * a + p], recv_sem=recv_sems.at[len(rels) * a + p],
                            device_id=(x, y, 1 - c), device_id_type=MESH)
                got = _half(outs[a], 2 * px + py, c)
                want = _half(outs[a], 2 * px + py, 1 - c)
                cps.append(pltpu.make_async_remote_copy(src_ref=got, dst_ref=got, **sems))
                waits.append(pltpu.make_async_remote_copy(src_ref=want, dst_ref=want, **sems))
        for cp in cps:
            cp.start()
        for cp in waits:
            cp.wait_recv()
        for cp in cps:
            cp.wait_send()

    return pl.pallas_call(
        body, name=name,
        in_specs=[ANY_SPEC] * n, out_specs=[ANY_SPEC] * n,
        out_shape=[jax.ShapeDtypeStruct(s.shape, s.dtype) for s in bufs],
        input_output_aliases={a: a for a in range(n)},
        scratch_shapes=[pltpu.SemaphoreType.DMA((len(rels) * n,)), pltpu.SemaphoreType.DMA((len(rels) * n,))],
    )(*bufs)


def _sibling_halves(parts, name):
    n = len(parts)

    def body(*refs):
        ins, outs = refs[:n], refs[n:2 * n]
        send_sems, recv_sems = refs[2 * n:]
        x, y, c = _place()
        cps = []
        for a in range(n):
            hr = ins[a].shape[1] // 2
            cps.append(pltpu.make_async_remote_copy(
                src_ref=ins[a].at[:, pl.ds((1 - c) * hr, hr), :], dst_ref=outs[a],
                send_sem=send_sems.at[a], recv_sem=recv_sems.at[a], device_id=(x, y, 1 - c), device_id_type=MESH))
        for cp in cps:
            cp.start()
        for cp in cps:
            cp.wait()

    hbm = pl.BlockSpec(memory_space=pl.ANY)
    return pl.pallas_call(
        body, name=name,
        in_specs=[hbm] * n, out_specs=[hbm] * n,
        out_shape=[jax.ShapeDtypeStruct((NCHIP, p.shape[1] // 2, p.shape[2]), p.dtype) for p in parts],
        scratch_shapes=[pltpu.SemaphoreType.DMA((n,)), pltpu.SemaphoreType.DMA((n,))],
    )(*parts)


def _sibling_copy(src, land, send_sems, recv_sems):
    x, y, c = _place()
    return pltpu.make_async_remote_copy(src_ref=src, dst_ref=land, send_sem=send_sems.at[0], recv_sem=recv_sems.at[0],
                                        device_id=(x, y, 1 - c), device_id_type=MESH)


def _sibling_start(part, name):
    def body(src, land, send_sems, recv_sems, src_thru, land_thru, token):
        _sibling_copy(src, land, send_sems, recv_sems).start()
        token[...] = jnp.zeros_like(token)

    land = lax.empty(part.shape, part.dtype)
    return pl.pallas_call(
        body, name=name,
        in_specs=[HBM_SPEC] * 2,
        out_specs=[SEM_SPEC, SEM_SPEC, HBM_SPEC, HBM_SPEC, pl.BlockSpec(memory_space=pltpu.VMEM)],
        out_shape=[pltpu.SemaphoreType.DMA((1,)), pltpu.SemaphoreType.DMA((1,)), pltpu.HBM(part.shape, part.dtype),
                   pltpu.HBM(part.shape, part.dtype), TOKEN],
        input_output_aliases={0: 2, 1: 3},
        compiler_params=pltpu.CompilerParams(has_side_effects=EFFECT),
    )(pltpu.with_memory_space_constraint(part, pltpu.HBM), pltpu.with_memory_space_constraint(land, pltpu.HBM))


def _sibling_wait(state, after, name):
    send_sems, recv_sems, part, land, _ = state

    def body(src, land_ref, send_ref, recv_ref, after_ref, src_thru, land_thru):
        cp = _sibling_copy(src, land_ref, send_ref, recv_ref)
        cp.wait_send()
        cp.wait_recv()

    return pl.pallas_call(
        body, name=name,
        in_specs=[HBM_SPEC, HBM_SPEC, SEM_SPEC, SEM_SPEC, ANY_SPEC],
        out_specs=[HBM_SPEC, HBM_SPEC],
        out_shape=[pltpu.HBM(part.shape, part.dtype), pltpu.HBM(land.shape, land.dtype)],
        input_output_aliases={0: 0, 1: 1},
        compiler_params=pltpu.CompilerParams(has_side_effects=EFFECT),
    )(part, land, send_sems, recv_sems, after)[1]


def _pair_sum_rows(keep, got, name):
    nblk, rows, cols = keep.shape
    tr, tc = 512, _tile_cols(cols)

    def body(k_ref, g_ref, o_ref):
        o_ref[...] = (k_ref[...] + g_ref[...].astype(F32)).astype(BF16)

    blk = pl.BlockSpec((None, tr, tc), lambda b, i, j: (b, i, j))
    return pl.pallas_call(
        body, name=name, grid=(nblk, rows // tr, cols // tc),
        in_specs=[blk, blk], out_specs=blk,
        out_shape=jax.ShapeDtypeStruct(keep.shape, BF16),
        compiler_params=_cp(("parallel",) * 3),
    )(keep, got)


def _pair_sum(part, got, name):
    nblk, rows, cols = part.shape
    hr = rows // 2
    tr = 512
    tc = cols if cols <= 1024 else (1024 if cols % 1024 == 0 else 896)
    c = lax.axis_index("c")
    nrt = hr // tr

    def body(c_ref, p_ref, g_ref, o_ref):
        del c_ref
        o_ref[...] = (p_ref[...] + g_ref[...]).astype(BF16)

    grid_spec = pltpu.PrefetchScalarGridSpec(
        num_scalar_prefetch=1, grid=(nblk, nrt, cols // tc),
        in_specs=[pl.BlockSpec((None, tr, tc), lambda b, i, j, cr: (b, cr[0] * nrt + i, j)),
                  pl.BlockSpec((None, tr, tc), lambda b, i, j, cr: (b, i, j))],
        out_specs=pl.BlockSpec((None, tr, tc), lambda b, i, j, cr: (b, i, j)))
    return pl.pallas_call(
        body, name=name, grid_spec=grid_spec,
        out_shape=jax.ShapeDtypeStruct((nblk, hr, cols), BF16),
        compiler_params=_cp(("parallel",) * 3),
    )(jnp.reshape(c, (1,)).astype(jnp.int32), part, got)


def _scatter_start(qs, name):
    n = len(qs)

    def body(*refs):
        ins, lands = refs[:n], refs[n:2 * n]
        token = refs[-1]
        for cp in _scatter_copies(ins, lands, refs[2 * n], refs[2 * n + 1]):
            cp.start()
        token[...] = jnp.zeros_like(token)

    lands = [lax.empty((3,) + q.shape[1:], q.dtype) for q in qs]
    res = pl.pallas_call(
        body, name=name,
        in_specs=[HBM_SPEC] * (2 * n),
        out_specs=[SEM_SPEC, SEM_SPEC] + [HBM_SPEC] * (2 * n) + [pl.BlockSpec(memory_space=pltpu.VMEM)],
        out_shape=[pltpu.SemaphoreType.DMA((3 * n,)), pltpu.SemaphoreType.DMA((3 * n,))]
        + [pltpu.HBM(b.shape, b.dtype) for b in qs + lands] + [TOKEN],
        input_output_aliases={a: 2 + a for a in range(2 * n)},
        compiler_params=pltpu.CompilerParams(has_side_effects=EFFECT),
    )(*[pltpu.with_memory_space_constraint(b, pltpu.HBM) for b in qs + lands])
    return res[0], res[1], list(res[2:2 + n]), list(res[2 + n:2 + 2 * n]), res[-1]


def _scatter_copies(ins, lands, send_sems, recv_sems):
    x, y, c = _place()
    cps = []
    for a in range(len(ins)):
        for j, (px, py) in enumerate(_other_chips(x, y)):
            cps.append(pltpu.make_async_remote_copy(
                src_ref=ins[a].at[2 * px + py], dst_ref=lands[a].at[j],
                send_sem=send_sems.at[3 * a + j], recv_sem=recv_sems.at[3 * a + j], device_id=(px, py, c), device_id_type=MESH))
    return cps


def _scatter_wait(qs, lands, send_sems, recv_sems, after, name):
    n = len(qs)

    def body(*refs):
        for cp in _scatter_copies(refs[:n], refs[n:2 * n], refs[2 * n], refs[2 * n + 1]):
            cp.wait_send()
            cp.wait_recv()

    res = pl.pallas_call(
        body, name=name,
        in_specs=[HBM_SPEC] * (2 * n) + [SEM_SPEC, SEM_SPEC, ANY_SPEC],
        out_specs=[HBM_SPEC] * (2 * n),
        out_shape=[pltpu.HBM(b.shape, b.dtype) for b in qs + lands],
        input_output_aliases={a: a for a in range(2 * n)},
        compiler_params=pltpu.CompilerParams(has_side_effects=EFFECT),
    )(*qs, *lands, send_sems, recv_sems, after)
    return list(res[:n]), list(res[n:])


def _chip_sum(q, got, name):
    _, hr, cols = got.shape
    tr, tc = 512, _tile_cols(cols)
    chip = 2 * lax.axis_index("x") + lax.axis_index("y")
    c = lax.axis_index("c")

    def body(idx_ref, q_ref, g_ref, o_ref):
        del idx_ref
        acc = q_ref[...].astype(F32)
        for i in range(3):
            acc = acc + g_ref[i].astype(F32)
        o_ref[...] = acc

    grid_spec = pltpu.PrefetchScalarGridSpec(
        num_scalar_prefetch=1, grid=(hr // tr, cols // tc),
        in_specs=[pl.BlockSpec((None, tr, tc), lambda i, j, ix: (ix[0], i, j)),
                  pl.BlockSpec((3, tr, tc), lambda i, j, ix: (0, i, j))],
        out_specs=pl.BlockSpec((None, tr, tc), lambda i, j, ix: (ix[1], i, j)))
    return pl.pallas_call(
        body, name=name, grid_spec=grid_spec,
        out_shape=jax.ShapeDtypeStruct((2, hr, cols), F32),
        compiler_params=_cp(("parallel", "parallel")),
    )(jnp.stack([chip, c]).astype(jnp.int32), q, got)


def _join_halves(bufs, name):
    n = len(bufs)

    def body(*refs):
        outs = refs[n:2 * n]
        send_sems, recv_sems = refs[2 * n:]
        x, y, c = _place()
        cps = []
        for a in range(n):
            cps.append(pltpu.make_async_remote_copy(src_ref=outs[a].at[c], dst_ref=outs[a].at[c], send_sem=send_sems.at[a],
                                                    recv_sem=recv_sems.at[a], device_id=(x, y, 1 - c), device_id_type=MESH))
        for cp in cps:
            cp.start()
        for a in range(n):
            theirs = outs[a].at[1 - c]
            pltpu.make_async_remote_copy(src_ref=theirs, dst_ref=theirs, send_sem=send_sems.at[a], recv_sem=recv_sems.at[a],
                                         device_id=(x, y, 1 - c), device_id_type=MESH).wait_recv()
        for cp in cps:
            cp.wait_send()

    hbm = pl.BlockSpec(memory_space=pl.ANY)
    return pl.pallas_call(
        body, name=name,
        in_specs=[hbm] * n, out_specs=[hbm] * n,
        out_shape=[jax.ShapeDtypeStruct(b.shape, b.dtype) for b in bufs],
        input_output_aliases={a: a for a in range(n)},
        scratch_shapes=[pltpu.SemaphoreType.DMA((n,)), pltpu.SemaphoreType.DMA((n,))],
    )(*bufs)


def _join_copy(buf, send_sems, recv_sems):
    x, y, c = _place()
    sems = dict(send_sem=send_sems.at[0], recv_sem=recv_sems.at[0], device_id=(x, y, 1 - c), device_id_type=MESH)
    return (pltpu.make_async_remote_copy(src_ref=buf.at[c], dst_ref=buf.at[c], **sems),
            pltpu.make_async_remote_copy(src_ref=buf.at[1 - c], dst_ref=buf.at[1 - c], **sems))


def _join_start(buf, name):
    def body(b_ref, send_sems, recv_sems, thru, token):
        _join_copy(b_ref, send_sems, recv_sems)[0].start()
        token[...] = jnp.zeros_like(token)

    return pl.pallas_call(
        body, name=name,
        in_specs=[HBM_SPEC],
        out_specs=[SEM_SPEC, SEM_SPEC, HBM_SPEC, pl.BlockSpec(memory_space=pltpu.VMEM)],
        out_shape=[pltpu.SemaphoreType.DMA((1,)), pltpu.SemaphoreType.DMA((1,)), pltpu.HBM(buf.shape, buf.dtype), TOKEN],
        input_output_aliases={0: 2},
        compiler_params=pltpu.CompilerParams(has_side_effects=EFFECT),
    )(pltpu.with_memory_space_constraint(buf, pltpu.HBM))


def _join_wait(state, after, name):
    send_sems, recv_sems, buf, _ = state

    def body(b_ref, send_ref, recv_ref, *rest):
        out, inc = _join_copy(b_ref, send_ref, recv_ref)
        out.wait_send()
        inc.wait_recv()

    return pl.pallas_call(
        body, name=name,
        in_specs=[HBM_SPEC, SEM_SPEC, SEM_SPEC] + [ANY_SPEC] * len(after),
        out_specs=HBM_SPEC,
        out_shape=pltpu.HBM(buf.shape, buf.dtype),
        input_output_aliases={0: 0},
        compiler_params=pltpu.CompilerParams(has_side_effects=EFFECT),
    )(buf, send_sems, recv_sems, *after)


N_DEV = 8


def _gather_small(block, reduce, name):
    m_per, cols = block.shape

    def body(x_ref, out_ref, all_ref, send_sems, recv_sems, local_sem):
        x, y, c = _place()
        me, sibling = (x, y, c), (x, y, 1 - c)
        chips = _other_chips(x, y)

        def rows(px, py, pc):
            return all_ref.at[pl.ds((4 * px + 2 * py + pc) * m_per, m_per), :]

        def copy(k, block_of, to, src=None):
            return pltpu.make_async_remote_copy(
                src_ref=rows(*block_of) if src is None else src, dst_ref=rows(*block_of),
                send_sem=send_sems.at[k], recv_sem=recv_sems.at[k], device_id=to, device_id_type=MESH)

        mine = pltpu.make_async_copy(x_ref, rows(*me), local_sem)
        mine.start()
        first = [copy(0, me, sibling, src=x_ref)]
        first += [copy(1 + j, me, (*chip, c), src=x_ref) for j, chip in enumerate(chips)]
        for cp in first:
            cp.start()
        passed = [copy(4 + j, (*chip, c), sibling) for j, chip in enumerate(chips)]
        for j, chip in enumerate(chips):
            copy(1 + j, (*chip, c), me).wait_recv()
            passed[j].start()
        copy(0, sibling, me).wait_recv()
        for j, chip in enumerate(chips):
            copy(4 + j, (*chip, 1 - c), me).wait_recv()
        for cp in first + passed:
            cp.wait_send()
        mine.wait()
        if reduce:
            acc = all_ref[pl.ds(0, m_per), :]
            for i in range(1, N_DEV):
                acc = acc + all_ref[pl.ds(i * m_per, m_per), :]
            out_ref[...] = acc
        else:
            out_ref[...] = all_ref[...]

    out_rows = m_per if reduce else N_DEV * m_per
    return pl.pallas_call(
        body, name=name,
        in_specs=[pl.BlockSpec(memory_space=pltpu.VMEM)],
        out_specs=pl.BlockSpec(memory_space=pltpu.VMEM),
        out_shape=jax.ShapeDtypeStruct((out_rows, cols), F32),
        scratch_shapes=[pltpu.VMEM((N_DEV * m_per, cols), F32), pltpu.SemaphoreType.DMA((7,)),
                        pltpu.SemaphoreType.DMA((7,)), pltpu.SemaphoreType.DMA],
    )(block)


class _Comm:
    def __init__(self, w_in, w_kv, w_out, chip):
        self.bufs = [_cast_own(w_in, chip, "cast_w_in"), _cast_own(w_kv, chip, "cast_w_mem_kv"),
                     _cast_own(w_out, chip, "cast_w_out")]


    def gather_started(self):
        (self.sems,), (self.b_in,), tok = _gather_start(self.bufs[:1], (NEAR,), "gather_start_in_near")
        return [tok]

    def w_in_own(self):
        return self.b_in.reshape(NCHIP * D, WB)

    def w_in_near(self, after):
        (b_in,) = _gather_wait([self.b_in], self.sems, NEAR, list(after) + self.bufs[1:], "gather_wait_in_near")
        (b_in,) = _forward_halves([b_in], NEAR, "forward_in_near")
        (self.sems,), (self.b_in,), tok = _gather_start([b_in], (FAR,), "gather_start_in_far")
        return self.b_in.reshape(NCHIP * D, WB), tok

    def w_in_all(self, after):
        (b_in,) = _gather_wait([self.b_in], self.sems, FAR, after, "gather_wait_in_far")
        (b_in,) = _forward_halves([b_in], FAR, "forward_in_far")
        (self.sems,), self.b_rest, tok = _gather_start(self.bufs[1:], (NEAR + FAR,), "gather_start_rest")
        return b_in.reshape(NCHIP * D, WB), tok

    def w_rest(self, after):
        b_rest = _gather_wait(self.b_rest, self.sems, NEAR + FAR, after, "gather_wait_rest")
        b_kv, b_out = _forward_halves(b_rest, NEAR + FAR, "forward_rest")
        return b_kv.reshape(D, 2 * XW), b_out.reshape(D, D)

    def reduce_start(self, parts, tag):
        got = _sibling_halves(parts, "sibling_halves_" + tag)
        qs = [_pair_sum(p, g, f"pair_sum_{tag}{i}") for i, (p, g) in enumerate(zip(parts, got))]
        return _scatter_start(qs, "scatter_start_" + tag)

    def sibling_start(self, send, tag):
        return _sibling_start(send, "sibling_start_" + tag)

    def reduce_start_summed(self, keep, sent, tag):
        got = _sibling_wait(sent, keep, "sibling_wait_" + tag)
        return _scatter_start([_pair_sum_rows(keep, got, "pair_sum_" + tag)], "scatter_start_" + tag)

    def reduce_finish_start(self, state, tag, after):
        send_sems, recv_sems, qs, lands, _ = state
        qs, lands = _scatter_wait(qs, lands, send_sems, recv_sems, after, "scatter_wait_" + tag)
        return _join_start(_chip_sum(qs[0], lands[0], f"chip_sum_{tag}0"), "join_start_" + tag)

    def reduce_finish_wait(self, pending, tag, after):
        j = _join_wait(pending, after, "join_wait_" + tag)
        return j.reshape(2 * j.shape[1], j.shape[2])

    def reduce_finish(self, state, tag, after):
        send_sems, recv_sems, qs, lands, _ = state
        qs, lands = _scatter_wait(qs, lands, send_sems, recv_sems, after, "scatter_wait_" + tag)
        halves = [_chip_sum(q, l, f"chip_sum_{tag}{i}") for i, (q, l) in enumerate(zip(qs, lands))]
        return [j.reshape(2 * j.shape[1], j.shape[2]) for j in _join_halves(halves, "join_halves_" + tag)]


def kernel(x, mem, pre_norm, w_in, conv_w, mem_norm, w_mem_kv, w_out, post_norm, loss_target, m_pre_norm, m_w_in, m_conv_w, m_mem_norm, m_w_mem_kv, m_w_out, m_post_norm, v_pre_norm, v_w_in, v_conv_w, v_mem_norm, v_w_mem_kv, v_w_out, v_post_norm):
    chip = 2 * lax.axis_index("x") + lax.axis_index("y")

    cw_blk = jnp.zeros((8, 384), F32).at[:3].set(conv_w[0])
    cw_all = _gather_small(cw_blk, False, "gather_conv_w").reshape(NCHIP, 2, 8, 384)[:, 0, :3]
    conv_full = jnp.transpose(cw_all, (1, 0, 2)).reshape(3, CW)

    comm = _Comm(w_in[0], w_mem_kv[0], w_out[0], chip)
    loss, grad_x, d_pre, d_mem, d_post, d_conv, r_in, r_kv, r_out = _local_step(
        x, mem, pre_norm, conv_full, mem_norm, post_norm, loss_target, chip, lax.axis_index("c"), comm)

    pack = jnp.concatenate([d_pre, d_mem, d_post, jnp.pad(d_conv, ((0, 0), (0, D - CW))),
                            jnp.pad(loss, ((0, 0), (0, D - 128))), jnp.zeros((1, D), F32)], axis=0)
    tot = _gather_small(pack, True, "reduce_small")
    g_pre, g_mem, g_post = tot[0:1], tot[1:2], tot[2:3]
    g_conv = lax.dynamic_slice(tot[3:6, :CW], (0, chip * 384), (3, 384))
    loss_out = tot[6, 0]

    names = ("pre_norm", "w_in", "conv_w", "mem_norm", "w_mem_kv", "w_out", "post_norm")
    ws = (pre_norm, w_in[0], conv_w[0], mem_norm, w_mem_kv[0], w_out[0], post_norm)
    gs = [g_pre, None, g_conv, g_mem, r_kv, r_out, g_post]
    ms = (m_pre_norm, m_w_in[0], m_conv_w[0], m_mem_norm, m_w_mem_kv[0], m_w_out[0], m_post_norm)
    vs = (v_pre_norm, v_w_in[0], v_conv_w[0], v_mem_norm, v_w_mem_kv[0], v_w_out[0], v_post_norm)
    upd = [None if g is None else _adamw(w, g, m, v, "adamw_" + nm) for nm, w, g, m, v in zip(names, ws, gs, ms, vs)]
    gs[1] = comm.reduce_finish_wait(r_in, "b", after=[u[1] for u in upd if u is not None])
    upd[1] = _adamw(ws[1], gs[1], ms[1], vs[1], "adamw_w_in")

    def shaped(arrs):
        return [a.reshape(w.shape) if w.ndim == a.ndim else a.reshape((1,) + a.shape)
                for a, w in zip(arrs, (pre_norm, w_in, conv_w, mem_norm, w_mem_kv, w_out, post_norm))]

    grads = shaped([u[0] for u in upd])
    deltas = shaped([u[1] for u in upd])
    new_m = shaped([u[2] for u in upd])
    new_v = shaped([u[3] for u in upd])
    return (loss_out, grad_x, *grads, *deltas, *new_m, *new_v)
```

```python
import functools

import jax
import jax.numpy as jnp
from jax import lax
from jax.experimental import pallas as pl
from jax.experimental.pallas import tpu as pltpu

F32 = jnp.float32
BF16 = jnp.bfloat16

D = 4096
S = 2048
NB = 2
T = NB * S
MLEN = 256
HD = 128
AW = 1536
CW = 1536
XW = 1024
XHD = 256
NXH = 4
NC = 14336
QA, KA, VA, ZA, UC, BC, CC, ZC, QX, ZX = 0, 1536, 3072, 4608, 6144, 7680, 9216, 10752, 12288, 13312
NCHIP = 4
WB = NC // NCHIP
DIL = (1, 4, 16)
HPG = 4
EPS = 1e-6
NEG = -1e30
ROPE_THETA = 10000.0
A_SCALE = HD ** -0.5
X_SCALE = XHD ** -0.5

ADAM_LR = 0.001
ADAM_B1 = 0.9
ADAM_B2 = 0.999
ADAM_EPS = 1e-08
ADAM_WD = 0.01
ADAM_STEP = 10

MESH = pl.DeviceIdType.MESH
MIB = 1024 * 1024


def _cp(sem, vmem_mib=48):
    return pltpu.CompilerParams(dimension_semantics=sem, vmem_limit_bytes=vmem_mib * MIB)


def _sigmoid(z):
    return 1.0 / (1.0 + jnp.exp(-z))


def _rope(x, cos, sin, half):
    return x * cos + pltpu.roll(x, half, 1) * sin


def _rope_t(g, cos, sin, half):
    return g * cos + pltpu.roll(g * sin, half, 1)


def _rms_fwd(x2, g, name, dep=None):
    rows = x2.shape[0]
    tr = 256
    deps = [] if dep is None else list(dep)

    def body(x_ref, g_ref, *rest):
        o_ref = rest[-1]
        x = x_ref[...]
        r = lax.rsqrt(jnp.mean(x * x, axis=-1, keepdims=True) + EPS)
        o_ref[...] = (x * r * g_ref[...]).astype(BF16)

    return pl.pallas_call(
        body, name=name, grid=(rows // tr,),
        in_specs=[pl.BlockSpec((tr, D), lambda i: (i, 0)), pl.BlockSpec((1, D), lambda i: (0, 0))]
        + [pl.BlockSpec(memory_space=pl.ANY)] * len(deps),
        out_specs=pl.BlockSpec((tr, D), lambda i: (i, 0)),
        out_shape=jax.ShapeDtypeStruct((rows, D), BF16),
        compiler_params=_cp(("parallel",)),
    )(x2, g, *deps)


def _norm_gain_grad(dn, x2, name):
    rows = x2.shape[0]
    tr = 256

    def body(dn_ref, x_ref, dg_ref):
        @pl.when(pl.program_id(0) == 0)
        def _():
            dg_ref[...] = jnp.zeros_like(dg_ref)
        x = x_ref[...]
        r = lax.rsqrt(jnp.mean(x * x, axis=-1, keepdims=True) + EPS)
        dg_ref[...] += jnp.sum(dn_ref[...] * (x * r), axis=0, keepdims=True)

    return pl.pallas_call(
        body, name=name, grid=(rows // tr,),
        in_specs=[pl.BlockSpec((tr, D), lambda i: (i, 0)), pl.BlockSpec((tr, D), lambda i: (i, 0))],
        out_specs=pl.BlockSpec((1, D), lambda i: (0, 0)),
        out_shape=jax.ShapeDtypeStruct((1, D), F32),
        compiler_params=_cp(("arbitrary",)),
    )(dn, x2)


def _pre_norm_bwd(dh, x2, g, dout):
    tr = 256

    def body(dh_ref, x_ref, g_ref, dout_ref, gx_ref, dg_ref):
        @pl.when(pl.program_id(0) == 0)
        def _():
            dg_ref[...] = jnp.zeros_like(dg_ref)
        x = x_ref[...]
        dh_ = dh_ref[...].astype(F32)
        r = lax.rsqrt(jnp.mean(x * x, axis=-1, keepdims=True) + EPS)
        xhat = x * r
        dg_ref[...] += jnp.sum(dh_ * xhat, axis=0, keepdims=True)
        dxn = dh_ * g_ref[...]
        gx_ref[...] = dout_ref[...].astype(F32) + r * (dxn - xhat * jnp.mean(dxn * xhat, axis=-1, keepdims=True))

    row = pl.BlockSpec((tr, D), lambda i: (i, 0))
    vec = pl.BlockSpec((1, D), lambda i: (0, 0))
    return pl.pallas_call(
        body, name="pre_norm_bwd", grid=(T // tr,),
        in_specs=[row, row, vec, row],
        out_specs=[row, vec],
        out_shape=[jax.ShapeDtypeStruct((T, D), F32), jax.ShapeDtypeStruct((1, D), F32)],
        compiler_params=_cp(("arbitrary",)),
    )(dh, x2, g, dout)


def _post_norm_loss(y, x2, tgt, g):
    tr = 256

    def body(y_ref, x_ref, t_ref, g_ref, dy_ref, dout_ref, dg_ref, loss_ref):
        @pl.when(pl.program_id(0) == 0)
        def _():
            dg_ref[...] = jnp.zeros_like(dg_ref)
            loss_ref[...] = jnp.zeros_like(loss_ref)
        yv = y_ref[...]
        gv = g_ref[...]
        r = lax.rsqrt(jnp.mean(yv * yv, axis=-1, keepdims=True) + EPS)
        yhat = yv * r
        err = x_ref[...] + yhat * gv - t_ref[...]
        loss_ref[...] += jnp.sum(jnp.sum(err * err, axis=1, keepdims=True), axis=0, keepdims=True) * (0.5 / D)
        dout = err * (1.0 / D)
        dout_ref[...] = dout.astype(BF16)
        dg_ref[...] += jnp.sum(dout * yhat, axis=0, keepdims=True)
        dyn = dout * gv
        dy_ref[...] = (r * (dyn - yhat * jnp.mean(dyn * yhat, axis=-1, keepdims=True))).astype(BF16)

    row = pl.BlockSpec((tr, D), lambda i: (i, 0))
    vec = pl.BlockSpec((1, D), lambda i: (0, 0))
    return pl.pallas_call(
        body, name="post_norm_loss", grid=(T // tr,),
        in_specs=[row, row, row, vec],
        out_specs=[row, row, vec, pl.BlockSpec((1, 128), lambda i: (0, 0))],
        out_shape=[jax.ShapeDtypeStruct((T, D), BF16), jax.ShapeDtypeStruct((T, D), BF16),
                   jax.ShapeDtypeStruct((1, D), F32), jax.ShapeDtypeStruct((1, 128), F32)],
        compiler_params=_cp(("arbitrary",)),
    )(y, x2, tgt, g)


NN = (((1,), (0,)), ((), ()))
NT = (((1,), (1,)), ((), ()))
TN = (((0,), (0,)), ((), ()))


def _as_index(v):
    return jnp.reshape(v, (1,)).astype(jnp.int32)


def _matmul(a, b, *, name, dims, grid, a_block, a_map, b_block, b_map, o_block, o_map, out_shape, out_dtype=F32,
            index=None, prev=None, deps=()):
    extra = ([] if prev is None else [prev]) + [d for d in deps if d is not None]
    first = 0 if index is None else 1
    nk = grid[2]
    in_place = out_dtype == F32

    def body(*refs):
        a_ref, b_ref, o_ref = refs[first], refs[first + 1], refs[first + 2 + len(extra)]
        acc_ref = o_ref if in_place else refs[-1]

        @pl.when(pl.program_id(2) == 0)
        def _():
            acc_ref[...] = lax.dot_general(a_ref[...], b_ref[...], dims, preferred_element_type=F32)

        @pl.when(pl.program_id(2) > 0)
        def _():
            acc_ref[...] += lax.dot_general(a_ref[...], b_ref[...], dims, preferred_element_type=F32)

        if not in_place:
            @pl.when(pl.program_id(2) == nk - 1)
            def _():
                o_ref[...] = acc_ref[...].astype(o_ref.dtype)

    in_specs = [pl.BlockSpec(a_block, a_map), pl.BlockSpec(b_block, b_map)] + [pl.BlockSpec(memory_space=pl.ANY)] * len(extra)
    out_specs = pl.BlockSpec(o_block, o_map)
    scratch = [] if in_place else [pltpu.VMEM(o_block, F32)]
    kwargs = dict(name=name, out_shape=jax.ShapeDtypeStruct(out_shape, out_dtype),
                  input_output_aliases={} if prev is None else {first + 2: 0},
                  compiler_params=_cp(("parallel", "parallel", "arbitrary"), vmem_mib=56))
    if index is None:
        return pl.pallas_call(body, grid=grid, in_specs=in_specs, out_specs=out_specs, scratch_shapes=scratch,
                              **kwargs)(a, b, *extra)
    grid_spec = pltpu.PrefetchScalarGridSpec(num_scalar_prefetch=1, grid=grid, in_specs=in_specs, out_specs=out_specs,
                                             scratch_shapes=scratch)
    return pl.pallas_call(body, grid_spec=grid_spec, **kwargs)(_as_index(index), a, b, *extra)


def _mm_nn(a, b, name, tm, tn, tk):
    m, kd = a.shape
    n = b.shape[1]
    return _matmul(a, b, name=name, dims=NN, grid=(m // tm, n // tn, kd // tk),
                   a_block=(tm, tk), a_map=lambda i, j, k: (i, k),
                   b_block=(tk, tn), b_map=lambda i, j, k: (k, j),
                   o_block=(tm, tn), o_map=lambda i, j, k: (i, j), out_shape=(m, n))


def _mm_nt(a, b, name, tm, tn, tk):
    m, kd = a.shape
    n = b.shape[0]
    return _matmul(a, b, name=name, dims=NT, grid=(m // tm, n // tn, kd // tk),
                   a_block=(tm, tk), a_map=lambda i, j, k: (i, k),
                   b_block=(tn, tk), b_map=lambda i, j, k: (j, k),
                   o_block=(tm, tn), o_map=lambda i, j, k: (i, j), out_shape=(m, n))


def _mm_tn(a, b, name, tm, tn, tk):
    kd, m = a.shape
    n = b.shape[1]
    return _matmul(a, b, name=name, dims=TN, grid=(m // tm, n // tn, kd // tk),
                   a_block=(tk, tm), a_map=lambda i, j, k: (k, i),
                   b_block=(tk, tn), b_map=lambda i, j, k: (k, j),
                   o_block=(tm, tn), o_map=lambda i, j, k: (i, j), out_shape=(m, n))


W_TN = 1792
W_NJ = WB // W_TN


def _proj_part(h, wg, chip, masks, name, prev=None, dep=None):
    tm, tk = 1024, 2048

    def blk(j, ix):
        m = masks[0]
        for t in range(1, len(masks)):
            m = jnp.where(j // W_NJ == t, masks[t], m)
        return jnp.bitwise_xor(ix[0], m)

    return _matmul(h, wg, name=name, dims=NN, grid=(T // tm, len(masks) * W_NJ, D // tk), index=chip, prev=prev, deps=(dep,),
                   a_block=(tm, tk), a_map=lambda i, j, k, ix: (i, k),
                   b_block=(tk, W_TN), b_map=lambda i, j, k, ix: (blk(j, ix) * (D // tk) + k, j % W_NJ),
                   o_block=(tm, W_TN), o_map=lambda i, j, k, ix: (i, blk(j, ix) * W_NJ + j % W_NJ), out_shape=(T, NC))


def _dh(dproj, wg, dep=None):
    tm, tn = 1024, 1024
    return _matmul(dproj, wg, name="dh", dims=NT, grid=(T // tm, D // tn, NC // W_TN), deps=(dep,),
                   a_block=(tm, W_TN), a_map=lambda i, j, k: (i, k),
                   b_block=(tn, W_TN), b_map=lambda i, j, k: ((k // W_NJ) * (D // tn) + j, k % W_NJ),
                   o_block=(tm, tn), o_map=lambda i, j, k: (i, j), out_shape=(T, D), out_dtype=BF16)


def _grad_w_out(cat, dy, half, out_dtype, name, dep=None):
    tm, tn, tk = D // NCHIP // 2, 2048, 2048
    return _matmul(cat, dy, name=name, dims=TN, grid=(NCHIP, D // tn, T // tk), index=half, deps=(dep,),
                   a_block=(tk, tm), a_map=lambda i, j, k, ix: (k, 2 * i + ix[0]),
                   b_block=(tk, tn), b_map=lambda i, j, k, ix: (k, j),
                   o_block=(tm, tn), o_map=lambda i, j, k, ix: (i, j),
                   out_shape=(NCHIP * tm, D), out_dtype=out_dtype)


def _grad_w_in(h, dproj, half, out_dtype, name, dep=None):
    tm, tk = 1024, 2048
    nh = D // 2 // tm
    return _matmul(h, dproj, name=name, dims=TN, grid=(nh, NC // W_TN, T // tk), index=half, deps=(dep,),
                   a_block=(tk, tm), a_map=lambda i, j, k, ix: (k, ix[0] * nh + i),
                   b_block=(tk, W_TN), b_map=lambda i, j, k, ix: (k, j),
                   o_block=(tm, W_TN), o_map=lambda i, j, k, ix: ((j // W_NJ) * nh + i, j % W_NJ),
                   out_shape=(NCHIP * D // 2, WB), out_dtype=out_dtype)


def _rope_tables(pos, half):
    inv = 1.0 / (ROPE_THETA ** (jnp.arange(half, dtype=F32) / half))
    ang = pos.astype(F32)[:, None] * inv[None, :]
    cos, sin = jnp.cos(ang), jnp.sin(ang)
    return jnp.concatenate([cos, cos], axis=1), jnp.concatenate([-sin, sin], axis=1)


def _band_mask(r0):
    qi = lax.broadcasted_iota(jnp.int32, (128, 256), 0)
    kk = lax.broadcasted_iota(jnp.int32, (128, 256), 1)
    return (kk >= qi) & (kk <= qi + 128) & (kk + r0 >= 128)


def _window(r0, nblk):
    if nblk == 1:
        qi = lax.broadcasted_iota(jnp.int32, (128, 128), 0)
        kk = lax.broadcasted_iota(jnp.int32, (128, 128), 1)
        return pl.ds(128, 128), kk <= qi
    return pl.ds(r0, 256), _band_mask(r0)


def _dil_rows(r, n, d):
    if d == 1:
        return pl.ds(pl.multiple_of(n * 128, 128), 128)
    return pl.ds(r + d * 128 * n, 128, stride=d)


def _attn_fwd(proj, cosf, sinf, g):
    d = DIL[g]
    ln = S // d
    nblk = ln // 128
    proj_v = proj.reshape(NB, S, NC)

    def body(q_ref, k_ref, v_ref, cos_ref, sin_ref, o_ref, l_ref, k_s, v_s):
        k_s[:, pl.ds(0, 128), :] = jnp.zeros((d, 128, HD), BF16)
        v_s[:, pl.ds(0, 128), :] = jnp.zeros((d, 128, HD), BF16)

        def prep(i, carry):
            r, n = i // nblk, i % nblk
            rows = _dil_rows(r, n, d)
            dst = pl.ds(pl.multiple_of(n * 128 + 128, 128), 128)
            k_s[r, dst, :] = _rope(k_ref[rows, :], cos_ref[rows, :], sin_ref[rows, :], HD // 2).astype(BF16)
            v_s[r, dst, :] = v_ref[rows, :].astype(BF16)
            return carry

        lax.fori_loop(0, d * nblk, prep, 0, unroll=4)

        def step(i, carry):
            r, n = i // nblk, i % nblk
            rows = _dil_rows(r, n, d)
            r0 = pl.multiple_of(n * 128, 128)
            qr = _rope(q_ref[rows, :], cos_ref[rows, :], sin_ref[rows, :], HD // 2).astype(BF16)
            win, mask = _window(r0, nblk)
            kw = k_s[r, win, :]
            vw = v_s[r, win, :]
            sc = lax.dot_general(qr, kw, NT, preferred_element_type=F32) * A_SCALE
            sc = jnp.where(mask, sc, NEG)
            m = jnp.max(sc, axis=1, keepdims=True)
            p = jnp.exp(sc - m)
            l = jnp.sum(p, axis=1, keepdims=True)
            o_ref[rows, :] = jnp.dot(p.astype(BF16), vw, preferred_element_type=F32) / l
            l_ref[rows, :] = jnp.broadcast_to(m + jnp.log(l), (128, HD))
            return carry

        lax.fori_loop(0, d * nblk, step, 0, unroll=4)

    def col(off):
        return lambda b, h: (b, 0, off // HD + HPG * g + h)

    blk = (None, S, HD)
    tab = pl.BlockSpec((S, HD), lambda b, h: (0, 0))
    out = pl.BlockSpec(blk, lambda b, h: (b, 0, h))
    o, l = pl.pallas_call(
        body, name=f"attn_fwd_d{d}", grid=(NB, HPG),
        in_specs=[pl.BlockSpec(blk, col(QA)), pl.BlockSpec(blk, col(KA)), pl.BlockSpec(blk, col(VA)), tab, tab],
        out_specs=[out, out],
        out_shape=[jax.ShapeDtypeStruct((NB, S, HPG * HD), F32)] * 2,
        scratch_shapes=[pltpu.VMEM((d, ln + 128, HD), BF16), pltpu.VMEM((d, ln + 128, HD), BF16)],
        compiler_params=_cp(("parallel", "parallel")),
    )(proj_v, proj_v, proj_v, cosf, sinf)
    return o.reshape(T, HPG * HD), l.reshape(T, HPG * HD)


def _attn_bwd(proj, cosf, sinf, da, lse, delta, dproj, g):
    d = DIL[g]
    ln = S // d
    nblk = ln // 128
    proj_v = proj.reshape(NB, S, NC)
    dproj_v = dproj.reshape(NB, S, NC)
    da_v = da.reshape(NB, S, AW)
    lse_v = lse.reshape(NB, S, HPG * HD)
    delta_v = delta.reshape(NB, S, HPG * HD)

    def body(q_ref, k_ref, v_ref, cos_ref, sin_ref, da_ref, lse_ref, dl_ref, dp_in_ref, o_ref,
             k_s, v_s, stg, dk_s, dv_s):
        del dp_in_ref
        w = pl.program_id(2)

        def emit():
            def cast(n, carry):
                rows = pl.ds(pl.multiple_of(n * 256, 256), 256)
                o_ref[rows, :] = stg[rows, :].astype(BF16)
                return carry

            lax.fori_loop(0, S // 256, cast, 0)

        @pl.when(w == 0)
        def _():
            k_s[:, pl.ds(0, 128), :] = jnp.zeros((d, 128, HD), BF16)
            v_s[:, pl.ds(0, 128), :] = jnp.zeros((d, 128, HD), BF16)
            dk_s[...] = jnp.zeros_like(dk_s)
            dv_s[...] = jnp.zeros_like(dv_s)

            def prep(i, carry):
                r, n = i // nblk, i % nblk
                rows = _dil_rows(r, n, d)
                dst = pl.ds(pl.multiple_of(n * 128 + 128, 128), 128)
                k_s[r, dst, :] = _rope(k_ref[rows, :], cos_ref[rows, :], sin_ref[rows, :], HD // 2).astype(BF16)
                v_s[r, dst, :] = v_ref[rows, :].astype(BF16)
                return carry

            lax.fori_loop(0, d * nblk, prep, 0, unroll=4)

            def step(i, carry):
                r, n = i // nblk, i % nblk
                rows = _dil_rows(r, n, d)
                r0 = pl.multiple_of(n * 128, 128)
                win, mask = _window(r0, nblk)
                cos, sin = cos_ref[rows, :], sin_ref[rows, :]
                qr = _rope(q_ref[rows, :], cos, sin, HD // 2).astype(BF16)
                kw = k_s[r, win, :]
                vw = v_s[r, win, :]
                sc = lax.dot_general(qr, kw, NT, preferred_element_type=F32) * A_SCALE
                sc = jnp.where(mask, sc, NEG)
                p = jnp.exp(sc - lse_ref[rows, :][:, :1])
                da_b = da_ref[rows, :].astype(BF16)
                dp = lax.dot_general(da_b, vw, NT, preferred_element_type=F32)
                ds_b = (p * (dp - dl_ref[rows, :][:, :1]) * A_SCALE).astype(BF16)
                p_b = p.astype(BF16)
                dq = jnp.dot(ds_b, kw, preferred_element_type=F32)
                stg[rows, :] = _rope_t(dq, cos, sin, HD // 2)
                dk_s[r, win, :] += lax.dot_general(ds_b, qr, TN, preferred_element_type=F32)
                dv_s[r, win, :] += lax.dot_general(p_b, da_b, TN, preferred_element_type=F32)
                return carry

            lax.fori_loop(0, d * nblk, step, 0, unroll=4)
            emit()

        @pl.when(w == 1)
        def _():
            def put(i, carry):
                r, n = i // nblk, i % nblk
                rows = _dil_rows(r, n, d)
                src = pl.ds(pl.multiple_of(n * 128 + 128, 128), 128)
                stg[rows, :] = _rope_t(dk_s[r, src, :], cos_ref[rows, :], sin_ref[rows, :], HD // 2)
                return carry

            lax.fori_loop(0, d * nblk, put, 0, unroll=4)
            emit()

        @pl.when(w == 2)
        def _():
            def put(i, carry):
                r, n = i // nblk, i % nblk
                src = pl.ds(pl.multiple_of(n * 128 + 128, 128), 128)
                stg[_dil_rows(r, n, d), :] = dv_s[r, src, :]
                return carry

            lax.fori_loop(0, d * nblk, put, 0, unroll=4)
            emit()

    def col(off):
        return lambda b, h, w: (ahead(b, h, w)[0], 0, off // HD + HPG * g + ahead(b, h, w)[1])

    def ahead(b, h, w):
        flat = jnp.minimum(b * HPG + h + jnp.where(w > 0, 1, 0), NB * HPG - 1)
        return flat // HPG, flat % HPG

    blk = (None, S, HD)
    tab = pl.BlockSpec((S, HD), lambda b, h, w: (0, 0))
    per_head = pl.BlockSpec(blk, lambda b, h, w: (ahead(b, h, w)[0], 0, ahead(b, h, w)[1]))
    out = pl.pallas_call(
        body, name=f"attn_bwd_d{d}", grid=(NB, HPG, 3),
        in_specs=[pl.BlockSpec(blk, col(QA)), pl.BlockSpec(blk, col(KA)), pl.BlockSpec(blk, col(VA)), tab, tab,
                  pl.BlockSpec(blk, col(0)), per_head, per_head, pl.BlockSpec(memory_space=pl.ANY)],
        out_specs=pl.BlockSpec(blk, lambda b, h, w: (b, 0, (AW // HD) * w + HPG * g + h)),
        out_shape=jax.ShapeDtypeStruct(dproj_v.shape, BF16),
        input_output_aliases={8: 0},
        scratch_shapes=[pltpu.VMEM((d, ln + 128, HD), BF16), pltpu.VMEM((d, ln + 128, HD), BF16),
                        pltpu.VMEM((S, HD), F32), pltpu.VMEM((d, ln + 128, HD), F32), pltpu.VMEM((d, ln + 128, HD), F32)],
        compiler_params=_cp(("arbitrary",) * 3),
    )(proj_v, proj_v, proj_v, cosf, sinf, da_v, lse_v, delta_v, dproj_v)
    return out.reshape(T, NC)


def _attn_mix(proj, os_, ls_):
    tr = 256
    gw = HPG * HD

    def body(o0, o1, o2, l0, l1, l2, z_ref, cat_ref):
        m = jnp.maximum(jnp.maximum(l0[...], l1[...]), l2[...])
        e = [jnp.exp(l[...] - m) for l in (l0, l1, l2)]
        inv = 1.0 / (e[0] + e[1] + e[2])
        for gi, o in enumerate((o0, o1, o2)):
            z = z_ref[:, gi * gw:(gi + 1) * gw]
            cat_ref[:, gi * gw:(gi + 1) * gw] = (o[...] * (e[gi] * inv) * (z * _sigmoid(z))).astype(BF16)

    grp = pl.BlockSpec((tr, gw), lambda i: (i, 0))
    return pl.pallas_call(
        body, name="attn_mix", grid=(T // tr,),
        in_specs=[grp] * 6 + [pl.BlockSpec((tr, AW), lambda i: (i, ZA // AW))],
        out_specs=pl.BlockSpec((tr, AW), lambda i: (i, 0)),
        out_shape=jax.ShapeDtypeStruct((T, D), BF16),
        compiler_params=_cp(("parallel",)),
    )(*os_, *ls_, proj)


def _attn_mix_bwd(dcat, proj, os_, ls_, dep=None):
    tr = 256
    gw = HPG * HD
    deps = [] if dep is None else [dep]

    def body(dy_ref, o0, o1, o2, l0, l1, l2, z_ref, *rest):
        da_ref, lse_ref, dl_ref, dz_ref = rest[len(deps):]
        m = jnp.maximum(jnp.maximum(l0[...], l1[...]), l2[...])
        e = [jnp.exp(l[...] - m) for l in (l0, l1, l2)]
        den = e[0] + e[1] + e[2]
        inv = 1.0 / den
        lse_ref[...] = m + jnp.log(den)
        acc = jnp.zeros((tr, gw), F32)
        for gi, o in enumerate((o0, o1, o2)):
            cols = slice(gi * gw, (gi + 1) * gw)
            z = z_ref[:, cols]
            dy = dy_ref[:, cols]
            sg = _sigmoid(z)
            a = o[...] * (e[gi] * inv)
            da = dy * (z * sg)
            da_ref[:, cols] = da
            dz_ref[:, cols] = (dy * a * (sg * (1.0 + z * (1.0 - sg)))).astype(BF16)
            acc = acc + da * a
        for hh in range(HPG):
            cols = slice(hh * HD, (hh + 1) * HD)
            dl_ref[:, cols] = jnp.broadcast_to(jnp.sum(acc[:, cols], axis=1, keepdims=True), (tr, HD))

    grp = pl.BlockSpec((tr, gw), lambda i: (i, 0))
    return pl.pallas_call(
        body, name="attn_mix_bwd", grid=(T // tr,),
        in_specs=[pl.BlockSpec((tr, AW), lambda i: (i, 0))] + [grp] * 6 + [pl.BlockSpec((tr, AW), lambda i: (i, ZA // AW))]
        + [pl.BlockSpec(memory_space=pl.ANY)] * len(deps),
        out_specs=[pl.BlockSpec((tr, AW), lambda i: (i, 0)), grp, grp, pl.BlockSpec((tr, AW), lambda i: (i, ZA // AW))],
        out_shape=[jax.ShapeDtypeStruct((T, AW), F32), jax.ShapeDtypeStruct((T, gw), F32),
                   jax.ShapeDtypeStruct((T, gw), F32), jax.ShapeDtypeStruct((T, NC), BF16)],
        compiler_params=_cp(("parallel",)),
    )(dcat, *os_, *ls_, proj, *deps)


CT = 256


def _shift_down(x, n):
    rows = lax.broadcasted_iota(jnp.int32, x.shape, 0)
    return jnp.where(rows >= n, pltpu.roll(x, n, 0), 0.0)


def _shift_up(x, n):
    rows = lax.broadcasted_iota(jnp.int32, x.shape, 0)
    return jnp.where(rows < x.shape[0] - n, pltpu.roll(x, x.shape[0] - n, 0), 0.0)


def _conv_fwd(proj, conv_w, cat):
    proj_v = proj.reshape(NB, S, NC)
    cat_v = cat.reshape(NB, S, D)

    def body(u_ref, b_ref, c_ref, z_ref, w_ref, cat_in, o_ref):
        del cat_in
        cu = c_ref[...] * u_ref[...]
        cv = _shift_down(cu, 2) * w_ref[0:1, :] + _shift_down(cu, 1) * w_ref[1:2, :] + cu * w_ref[2:3, :]
        z = z_ref[...]
        o_ref[...] = (b_ref[...] * cv * (z * _sigmoid(z))).astype(BF16)

    def seg(off):
        return pl.BlockSpec((None, S, CT), lambda b, j: (b, 0, off // CT + j))

    out = pl.pallas_call(
        body, name="conv_fwd", grid=(NB, CW // CT),
        in_specs=[seg(UC), seg(BC), seg(CC), seg(ZC), pl.BlockSpec((3, CT), lambda b, j: (0, j)),
                  pl.BlockSpec(memory_space=pl.ANY)],
        out_specs=pl.BlockSpec((None, S, CT), lambda b, j: (b, 0, AW // CT + j)),
        out_shape=jax.ShapeDtypeStruct((NB, S, D), BF16),
        input_output_aliases={5: 0},
        compiler_params=_cp(("parallel", "parallel")),
    )(proj_v, proj_v, proj_v, proj_v, conv_w, cat_v)
    return out.reshape(T, D)


def _conv_bwd(dcat, proj, conv_w, dproj):
    proj_v = proj.reshape(NB, S, NC)
    dproj_v = dproj.reshape(NB, S, NC)
    dcat_v = dcat.reshape(NB, S, D)

    def body(dy_ref, u_ref, b_ref, c_ref, z_ref, w_ref, dp_in, o_ref, dw_ref, st):
        del dp_in
        b = pl.program_id(1)
        w = pl.program_id(2)

        @pl.when((b == 0) & (w == 0))
        def _():
            dw_ref[...] = jnp.zeros_like(dw_ref)

        @pl.when(w == 0)
        def _():
            u, c, z, bb, dy = u_ref[...], c_ref[...], z_ref[...], b_ref[...], dy_ref[...]
            cu = c * u
            s1 = _shift_down(cu, 1)
            s2 = _shift_down(cu, 2)
            cv = s2 * w_ref[0:1, :] + s1 * w_ref[1:2, :] + cu * w_ref[2:3, :]
            sg = _sigmoid(z)
            sz = z * sg
            dcv = dy * bb * sz
            st[0] = dy * cv * sz
            st[2] = dy * bb * cv * (sg * (1.0 + z * (1.0 - sg)))
            dw_ref[0:1, :] += jnp.sum(dcv * s2, axis=0, keepdims=True)
            dw_ref[1:2, :] += jnp.sum(dcv * s1, axis=0, keepdims=True)
            dw_ref[2:3, :] += jnp.sum(dcv * cu, axis=0, keepdims=True)
            dcu = dcv * w_ref[2:3, :] + _shift_up(dcv, 1) * w_ref[1:2, :] + _shift_up(dcv, 2) * w_ref[0:1, :]
            st[1] = dcu * u
            o_ref[...] = (dcu * c).astype(BF16)

        for k in range(3):
            @pl.when(w == k + 1)
            def _(k=k):
                o_ref[...] = st[k].astype(BF16)

    def ahead(j, b, w):
        flat = jnp.minimum(j * NB + b + jnp.where(w > 0, 1, 0), (CW // CT) * NB - 1)
        return flat // NB, flat % NB

    def seg(off):
        return pl.BlockSpec((None, S, CT), lambda j, b, w: (ahead(j, b, w)[1], 0, off // CT + ahead(j, b, w)[0]))

    out, dw = pl.pallas_call(
        body, name="conv_bwd", grid=(CW // CT, NB, 4),
        in_specs=[seg(AW), seg(UC), seg(BC), seg(CC), seg(ZC),
                  pl.BlockSpec((3, CT), lambda j, b, w: (0, ahead(j, b, w)[0])),
                  pl.BlockSpec(memory_space=pl.ANY)],
        out_specs=[pl.BlockSpec((None, S, CT), lambda j, b, w: (b, 0, (UC + w * CW) // CT + j)),
                   pl.BlockSpec((3, CT), lambda j, b, w: (0, j))],
        out_shape=[jax.ShapeDtypeStruct((NB, S, NC), BF16), jax.ShapeDtypeStruct((3, CW), F32)],
        input_output_aliases={6: 0},
        scratch_shapes=[pltpu.VMEM((3, S, CT), F32)],
        compiler_params=_cp(("arbitrary",) * 3),
    )(dcat_v, proj_v, proj_v, proj_v, proj_v, conv_w, dproj_v)
    return out.reshape(T, NC), dw


XT = 1024


def _cross_fwd(proj, mkv, cosq, sinq, cosm, sinm, cat):
    proj_v = proj.reshape(NB, S, NC)
    mkv_v = mkv.reshape(NB, MLEN, 2 * XW)
    cat_v = cat.reshape(NB, S, D)

    def body(q_ref, z_ref, mk_ref, mv_ref, cq, sq, cm, sm, cat_in, o_ref):
        del cat_in
        mkr = _rope(mk_ref[...], cm[...], sm[...], XHD // 2).astype(BF16)
        qr = _rope(q_ref[...], cq[...], sq[...], XHD // 2).astype(BF16)
        sc = lax.dot_general(qr, mkr, NT, preferred_element_type=F32) * X_SCALE
        p = jnp.exp(sc - jnp.max(sc, axis=1, keepdims=True))
        p = p / jnp.sum(p, axis=1, keepdims=True)
        ox = jnp.dot(p.astype(BF16), mv_ref[...].astype(BF16), preferred_element_type=F32)
        z = z_ref[...]
        o_ref[...] = (ox * (z * _sigmoid(z))).astype(BF16)

    def seg(off):
        return pl.BlockSpec((None, XT, XHD), lambda b, h, t: (b, t, off // XHD + h))

    qtab = pl.BlockSpec((XT, XHD), lambda b, h, t: (t, 0))
    mtab = pl.BlockSpec((MLEN, XHD), lambda b, h, t: (0, 0))
    out = pl.pallas_call(
        body, name="cross_fwd", grid=(NB, NXH, S // XT),
        in_specs=[seg(QX), seg(ZX),
                  pl.BlockSpec((None, MLEN, XHD), lambda b, h, t: (b, 0, h)),
                  pl.BlockSpec((None, MLEN, XHD), lambda b, h, t: (b, 0, NXH + h)),
                  qtab, qtab, mtab, mtab, pl.BlockSpec(memory_space=pl.ANY)],
        out_specs=pl.BlockSpec((None, XT, XHD), lambda b, h, t: (b, t, (AW + CW) // XHD + h)),
        out_shape=jax.ShapeDtypeStruct((NB, S, D), BF16),
        input_output_aliases={8: 0},
        compiler_params=_cp(("parallel",) * 3),
    )(proj_v, proj_v, mkv_v, mkv_v, cosq, sinq, cosm, sinm, cat_v)
    return out.reshape(T, D)


def _cross_bwd(dcat, proj, mkv, cosq, sinq, cosm, sinm, dproj):
    proj_v = proj.reshape(NB, S, NC)
    dproj_v = dproj.reshape(NB, S, NC)
    dcat_v = dcat.reshape(NB, S, D)
    mkv_v = mkv.reshape(NB, MLEN, 2 * XW)
    nt = S // XT

    def body(dy_ref, q_ref, z_ref, mk_ref, mv_ref, cq, sq, cm, sm, dp_in, o_ref, dmk_ref, dmv_ref, dz_s):
        del dp_in
        t = pl.program_id(2)
        w = pl.program_id(3)

        @pl.when((t == 0) & (w == 0))
        def _():
            dmk_ref[...] = jnp.zeros_like(dmk_ref)
            dmv_ref[...] = jnp.zeros_like(dmv_ref)

        @pl.when(w == 0)
        def _():
            mkr = _rope(mk_ref[...], cm[...], sm[...], XHD // 2).astype(BF16)
            mv_b = mv_ref[...].astype(BF16)
            qr = _rope(q_ref[...], cq[...], sq[...], XHD // 2).astype(BF16)
            sc = lax.dot_general(qr, mkr, NT, preferred_element_type=F32) * X_SCALE
            p = jnp.exp(sc - jnp.max(sc, axis=1, keepdims=True))
            p = p / jnp.sum(p, axis=1, keepdims=True)
            p_b = p.astype(BF16)
            ox = jnp.dot(p_b, mv_b, preferred_element_type=F32)
            z = z_ref[...]
            dy = dy_ref[...]
            sg = _sigmoid(z)
            dz_s[...] = dy * ox * (sg * (1.0 + z * (1.0 - sg)))
            dox_b = (dy * (z * sg)).astype(BF16)
            dp = lax.dot_general(dox_b, mv_b, NT, preferred_element_type=F32)
            ds_b = (p * (dp - jnp.sum(dp * p, axis=1, keepdims=True)) * X_SCALE).astype(BF16)
            dq = jnp.dot(ds_b, mkr, preferred_element_type=F32)
            o_ref[...] = _rope_t(dq, cq[...], sq[...], XHD // 2).astype(BF16)
            dmk_ref[...] += lax.dot_general(ds_b, qr, TN, preferred_element_type=F32)
            dmv_ref[...] += lax.dot_general(p_b, dox_b, TN, preferred_element_type=F32)

        @pl.when(w == 1)
        def _():
            o_ref[...] = dz_s[...].astype(BF16)

        @pl.when((t == nt - 1) & (w == 1))
        def _():
            dmk_ref[...] = _rope_t(dmk_ref[...], cm[...], sm[...], XHD // 2)

    def seg(off):
        return pl.BlockSpec((None, XT, XHD), lambda b, h, t, w: (b, t, off // XHD + h))

    qtab = pl.BlockSpec((XT, XHD), lambda b, h, t, w: (t, 0))
    mtab = pl.BlockSpec((MLEN, XHD), lambda b, h, t, w: (0, 0))
    macc = pl.BlockSpec((None, MLEN, XHD), lambda b, h, t, w: (b, 0, h))
    out, dmk, dmv = pl.pallas_call(
        body, name="cross_bwd", grid=(NB, NXH, nt, 2),
        in_specs=[pl.BlockSpec((None, XT, XHD), lambda b, h, t, w: (b, t, (AW + CW) // XHD + h)),
                  seg(QX), seg(ZX),
                  pl.BlockSpec((None, MLEN, XHD), lambda b, h, t, w: (b, 0, h)),
                  pl.BlockSpec((None, MLEN, XHD), lambda b, h, t, w: (b, 0, NXH + h)),
                  qtab, qtab, mtab, mtab, pl.BlockSpec(memory_space=pl.ANY)],
        out_specs=[pl.BlockSpec((None, XT, XHD), lambda b, h, t, w: (b, t, (QX + w * XW) // XHD + h)), macc, macc],
        out_shape=[jax.ShapeDtypeStruct((NB, S, NC), BF16), jax.ShapeDtypeStruct((NB, MLEN, XW), F32),
                   jax.ShapeDtypeStruct((NB, MLEN, XW), F32)],
        input_output_aliases={9: 0},
        scratch_shapes=[pltpu.VMEM((XT, XHD), F32)],
        compiler_params=_cp(("arbitrary",) * 4),
    )(dcat_v, proj_v, proj_v, mkv_v, mkv_v, cosq, sinq, cosm, sinm, dproj_v)
    return out.reshape(T, NC), dmk, dmv


def _local_step(x, mem, pre_norm, conv_w, mem_norm, post_norm, tgt, chip, core, comm):
    x2 = x.reshape(T, D)
    mem2 = mem.reshape(NB * MLEN, D)
    tgt2 = tgt.reshape(T, D)
    cosa, sina = _rope_tables(jnp.arange(S), HD // 2)
    cosq, sinq = _rope_tables(jnp.arange(S) + MLEN, XHD // 2)
    cosm, sinm = _rope_tables(jnp.arange(MLEN), XHD // 2)

    h = _rms_fwd(x2, pre_norm, "pre_norm_fwd", dep=comm.gather_started())
    memn = _rms_fwd(mem2, mem_norm, "mem_norm_fwd")
    proj = _proj_part(h, comm.w_in_own(), chip, (0,), "proj_own")
    wg_in, tok = comm.w_in_near(after=[proj, memn, conv_w])
    proj = _proj_part(h, wg_in, chip, (REL_XOR[0], REL_XOR[1]), "proj_near", prev=proj, dep=tok)
    wg_in, tok = comm.w_in_all(after=proj)
    proj = _proj_part(h, wg_in, chip, (REL_XOR[2],), "proj_far", prev=proj, dep=tok)
    wg_kv, wg_out = comm.w_rest(after=proj)
    mkv = _mm_nn(memn, wg_kv, "mkv", NB * MLEN, 1024, 512)

    fw = [_attn_fwd(proj, cosa, sina, g) for g in range(3)]
    os_ = [f[0] for f in fw]
    ls_ = [f[1] for f in fw]
    cat = _attn_mix(proj, os_, ls_)
    cat = _conv_fwd(proj, conv_w, cat)
    cat = _cross_fwd(proj, mkv, cosq, sinq, cosm, sinm, cat)
    y = _mm_nn(cat, wg_out, "out_proj", 1024, 1024, 2048)
    dy, dout, d_post, loss = _post_norm_loss(y, x2, tgt2, post_norm)

    dcat = _mm_nt(dy, wg_out, "dcat", 1024, 1024, 2048)
    g_send = _grad_w_out(cat, dy, 1 - core, BF16, "grad_w_out_send")
    sent = comm.sibling_start(g_send.reshape(NCHIP, D // NCHIP // 2, D), "a")
    g_keep = _grad_w_out(cat, dy, core, BF16, "grad_w_out_keep", dep=sent[-1])
    red_a = comm.reduce_start_summed(g_keep.reshape(NCHIP, D // NCHIP // 2, D), sent, "a")
    da, lse, delta, dproj = _attn_mix_bwd(dcat, proj, os_, ls_, dep=red_a[-1])
    for g in range(3):
        dproj = _attn_bwd(proj, cosa, sina, da, lse, delta, dproj, g)
    dproj, d_conv = _conv_bwd(dcat, proj, conv_w, dproj)
    dproj, dmk, dmv = _cross_bwd(dcat, proj, mkv, cosq, sinq, cosm, sinm, dproj)

    dmkv = jnp.concatenate([dmk, dmv], axis=-1).reshape(NB * MLEN, 2 * XW)
    dmkv_b = dmkv.astype(BF16)
    g_kv = _mm_tn(memn, dmkv_b, "grad_w_mem_kv", 1024, 1024, NB * MLEN)
    dmemn = _mm_nt(dmkv_b, wg_kv, "dmemn", NB * MLEN, 1024, 512)
    d_mem = _norm_gain_grad(dmemn, mem2, "mem_norm_bwd")

    red_k = comm.reduce_start([g_kv.reshape(NCHIP, D // NCHIP, 2 * XW)], "k")
    g_send = _grad_w_in(h, dproj, 1 - core, BF16, "grad_w_in_send", dep=red_k[-1])
    sent = comm.sibling_start(g_send.reshape(NCHIP, D // 2, WB), "b")
    g_keep = _grad_w_in(h, dproj, core, BF16, "grad_w_in_keep", dep=sent[-1])
    red_b = comm.reduce_start_summed(g_keep.reshape(NCHIP, D // 2, WB), sent, "b")
    dh = _dh(dproj, wg_in, dep=red_b[-1])
    grad_x, d_pre = _pre_norm_bwd(dh, x2, pre_norm, dout)
    r_in = comm.reduce_finish_start(red_b, "b", after=grad_x)
    (r_out,) = comm.reduce_finish(red_a, "a", after=grad_x)
    (r_kv,) = comm.reduce_finish(red_k, "k", after=grad_x)
    return loss, grad_x.reshape(NB, S, D), d_pre, d_mem, d_post, d_conv, r_in, r_kv, r_out


def _adamw(w, g, m, v, name):
    rows, cols = w.shape
    tr = rows if rows <= 512 else 512
    tc = cols if cols <= 1024 else 1024
    if cols % tc:
        tc = 896

    def body(w_ref, g_ref, m_ref, v_ref, go_ref, d_ref, nm_ref, nv_ref):
        gv = g_ref[...]
        go_ref[...] = gv
        nm = ADAM_B1 * m_ref[...] + (1.0 - ADAM_B1) * gv
        nv = ADAM_B2 * v_ref[...] + (1.0 - ADAM_B2) * (gv * gv)
        m_hat = nm / (1.0 - ADAM_B1 ** ADAM_STEP)
        v_hat = nv / (1.0 - ADAM_B2 ** ADAM_STEP)
        d_ref[...] = -ADAM_LR * (m_hat / (jnp.sqrt(v_hat) + ADAM_EPS) + ADAM_WD * w_ref[...])
        nm_ref[...] = nm
        nv_ref[...] = nv

    blk = pl.BlockSpec((tr, tc), lambda i, j: (i, j))
    sds = jax.ShapeDtypeStruct((rows, cols), F32)
    return pl.pallas_call(
        body, name=name, grid=(rows // tr, cols // tc),
        in_specs=[blk] * 4, out_specs=[blk] * 4, out_shape=[sds] * 4,
        compiler_params=_cp(("parallel", "parallel")),
    )(w, g, m, v)


def _place():
    return lax.axis_index("x"), lax.axis_index("y"), lax.axis_index("c")


def _other_chips(x, y):
    return [(1 - x, y), (x, 1 - y), (1 - x, 1 - y)]


def _tile_cols(cols):
    return cols if cols <= 1024 else (1024 if cols % 1024 == 0 else 896)


def _cast_own(w, chip, name):
    rows, cols = w.shape
    tr, tc = 512, _tile_cols(cols)

    def body(chip_ref, w_ref, o_ref):
        del chip_ref
        o_ref[...] = w_ref[...].astype(BF16)

    grid_spec = pltpu.PrefetchScalarGridSpec(
        num_scalar_prefetch=1, grid=(rows // tr, cols // tc),
        in_specs=[pl.BlockSpec((tr, tc), lambda i, j, cr: (i, j))],
        out_specs=pl.BlockSpec((None, tr, tc), lambda i, j, cr: (cr[0], i, j)))
    return pl.pallas_call(
        body, name=name, grid_spec=grid_spec,
        out_shape=jax.ShapeDtypeStruct((NCHIP, rows, cols), BF16),
        compiler_params=_cp(("parallel", "parallel")),
    )(_as_index(chip), w)


HBM_SPEC = pl.BlockSpec(memory_space=pltpu.HBM)
SEM_SPEC = pl.BlockSpec(memory_space=pltpu.SEMAPHORE)
ANY_SPEC = pl.BlockSpec(memory_space=pl.ANY)
EFFECT = pltpu.SideEffectType.DATAFLOW_SIDE_EFFECTING
TOKEN = jax.ShapeDtypeStruct((8, 128), F32)


def _half(ref, chip, hc):
    hr = ref.shape[1] // 2
    return ref.at[chip, pl.ds(hc * hr, hr), :]


NEAR = (0, 1)
FAR = (2,)
REL_XOR = (2, 1, 3)


def _gather_copies(refs, send_sems, recv_sems, rels):
    x, y, c = _place()
    chips = _other_chips(x, y)
    out, inc = [], []
    for a, ref in enumerate(refs):
        for p, j in enumerate(rels):
            px, py = chips[j]
            mine = _half(ref, 2 * x + y, c)
            theirs = _half(ref, 2 * px + py, c)
            sems = dict(send_sem=send_sems.at[len(rels) * a + p], recv_sem=recv_sems.at[len(rels) * a + p],
                        device_id=(px, py, c), device_id_type=MESH)
            out.append(pltpu.make_async_remote_copy(src_ref=mine, dst_ref=mine, **sems))
            inc.append(pltpu.make_async_remote_copy(src_ref=theirs, dst_ref=theirs, **sems))
    return out, inc


def _gather_start(bufs, groups, name):
    n = len(bufs)
    ng = len(groups)

    def body(*refs):
        ins = refs[:n]
        token = refs[-1]
        for gi, rels in enumerate(groups):
            out, _ = _gather_copies(ins, refs[n + 2 * gi], refs[n + 2 * gi + 1], rels)
            for cp in out:
                cp.start()
        token[...] = jnp.zeros_like(token)

    sems = []
    for rels in groups:
        sems += [pltpu.SemaphoreType.DMA((len(rels) * n,))] * 2
    res = pl.pallas_call(
        body, name=name,
        in_specs=[HBM_SPEC] * n,
        out_specs=[SEM_SPEC] * (2 * ng) + [HBM_SPEC] * n + [pl.BlockSpec(memory_space=pltpu.VMEM)],
        out_shape=sems + [pltpu.HBM(b.shape, b.dtype) for b in bufs] + [TOKEN],
        input_output_aliases={a: 2 * ng + a for a in range(n)},
        compiler_params=pltpu.CompilerParams(has_side_effects=EFFECT),
    )(*[pltpu.with_memory_space_constraint(b, pltpu.HBM) for b in bufs])
    return [(res[2 * gi], res[2 * gi + 1]) for gi in range(ng)], list(res[2 * ng:2 * ng + n]), res[-1]


def _gather_wait(bufs, sems, rels, after, name):
    n = len(bufs)
    send_sems, recv_sems = sems
    after = list(after) if isinstance(after, (list, tuple)) else [after]

    def body(*refs):
        ins = refs[:n]
        out, inc = _gather_copies(ins, refs[n], refs[n + 1], rels)
        for cp in out:
            cp.wait_send()
        for cp in inc:
            cp.wait_recv()

    return pl.pallas_call(
        body, name=name,
        in_specs=[HBM_SPEC] * n + [SEM_SPEC, SEM_SPEC] + [ANY_SPEC] * len(after),
        out_specs=[HBM_SPEC] * n,
        out_shape=[pltpu.HBM(b.shape, b.dtype) for b in bufs],
        input_output_aliases={a: a for a in range(n)},
        compiler_params=pltpu.CompilerParams(has_side_effects=EFFECT),
    )(*bufs, send_sems, recv_sems, *after)


def _forward_halves(bufs, rels, name):
    n = len(bufs)

    def body(*refs):
        outs = refs[n:2 * n]
        send_sems, recv_sems = refs[2 * n:]
        x, y, c = _place()
        chips = _other_chips(x, y)
        cps, waits = [], []
        for a in range(n):
            for p, j in enumerate(rels):
                px, py = chips[j]
                sems = dict(send_sem=send_sems.at[len(rels) * a + p], recv_sem=recv_sems.at[len(rels) * a + p],
                            device_id=(x, y, 1 - c), device_id_type=MESH)
                got = _half(outs[a], 2 * px + py, c)
                want = _half(outs[a], 2 * px + py, 1 - c)
                cps.append(pltpu.make_async_remote_copy(src_ref=got, dst_ref=got, **sems))
                waits.append(pltpu.make_async_remote_copy(src_ref=want, dst_ref=want, **sems))
        for cp in cps:
            cp.start()
        for cp in waits:
            cp.wait_recv()
        for cp in cps:
            cp.wait_send()

    return pl.pallas_call(
        body, name=name,
        in_specs=[ANY_SPEC] * n, out_specs=[ANY_SPEC] * n,
        out_shape=[jax.ShapeDtypeStruct(s.shape, s.dtype) for s in bufs],
        input_output_aliases={a: a for a in range(n)},
        scratch_shapes=[pltpu.SemaphoreType.DMA((len(rels) * n,)), pltpu.SemaphoreType.DMA((len(rels) * n,))],
    )(*bufs)


def _sibling_halves(parts, name):
    n = len(parts)

    def body(*refs):
        ins, outs = refs[:n], refs[n:2 * n]
        send_sems, recv_sems = refs[2 * n:]
        x, y, c = _place()
        cps = []
        for a in range(n):
            hr = ins[a].shape[1] // 2
            cps.append(pltpu.make_async_remote_copy(
                src_ref=ins[a].at[:, pl.ds((1 - c) * hr, hr), :], dst_ref=outs[a],
                send_sem=send_sems.at[a], recv_sem=recv_sems.at[a], device_id=(x, y, 1 - c), device_id_type=MESH))
        for cp in cps:
            cp.start()
        for cp in cps:
            cp.wait()

    hbm = pl.BlockSpec(memory_space=pl.ANY)
    return pl.pallas_call(
        body, name=name,
        in_specs=[hbm] * n, out_specs=[hbm] * n,
        out_shape=[jax.ShapeDtypeStruct((NCHIP, p.shape[1] // 2, p.shape[2]), p.dtype) for p in parts],
        scratch_shapes=[pltpu.SemaphoreType.DMA((n,)), pltpu.SemaphoreType.DMA((n,))],
    )(*parts)


def _sibling_copy(src, land, send_sems, recv_sems):
    x, y, c = _place()
    return pltpu.make_async_remote_copy(src_ref=src, dst_ref=land, send_sem=send_sems.at[0], recv_sem=recv_sems.at[0],
                                        device_id=(x, y, 1 - c), device_id_type=MESH)


def _sibling_start(part, name):
    def body(src, land, send_sems, recv_sems, src_thru, land_thru, token):
        _sibling_copy(src, land, send_sems, recv_sems).start()
        token[...] = jnp.zeros_like(token)

    land = lax.empty(part.shape, part.dtype)
    return pl.pallas_call(
        body, name=name,
        in_specs=[HBM_SPEC] * 2,
        out_specs=[SEM_SPEC, SEM_SPEC, HBM_SPEC, HBM_SPEC, pl.BlockSpec(memory_space=pltpu.VMEM)],
        out_shape=[pltpu.SemaphoreType.DMA((1,)), pltpu.SemaphoreType.DMA((1,)), pltpu.HBM(part.shape, part.dtype),
                   pltpu.HBM(part.shape, part.dtype), TOKEN],
        input_output_aliases={0: 2, 1: 3},
        compiler_params=pltpu.CompilerParams(has_side_effects=EFFECT),
    )(pltpu.with_memory_space_constraint(part, pltpu.HBM), pltpu.with_memory_space_constraint(land, pltpu.HBM))


def _sibling_wait(state, after, name):
    send_sems, recv_sems, part, land, _ = state

    def body(src, land_ref, send_ref, recv_ref, after_ref, src_thru, land_thru):
        cp = _sibling_copy(src, land_ref, send_ref, recv_ref)
        cp.wait_send()
        cp.wait_recv()

    return pl.pallas_call(
        body, name=name,
        in_specs=[HBM_SPEC, HBM_SPEC, SEM_SPEC, SEM_SPEC, ANY_SPEC],
        out_specs=[HBM_SPEC, HBM_SPEC],
        out_shape=[pltpu.HBM(part.shape, part.dtype), pltpu.HBM(land.shape, land.dtype)],
        input_output_aliases={0: 0, 1: 1},
        compiler_params=pltpu.CompilerParams(has_side_effects=EFFECT),
    )(part, land, send_sems, recv_sems, after)[1]


def _pair_sum_rows(keep, got, name):
    nblk, rows, cols = keep.shape
    tr, tc = 512, _tile_cols(cols)

    def body(k_ref, g_ref, o_ref):
        o_ref[...] = (k_ref[...].astype(F32) + g_ref[...].astype(F32)).astype(BF16)

    blk = pl.BlockSpec((None, tr, tc), lambda b, i, j: (b, i, j))
    return pl.pallas_call(
        body, name=name, grid=(nblk, rows // tr, cols // tc),
        in_specs=[blk, blk], out_specs=blk,
        out_shape=jax.ShapeDtypeStruct(keep.shape, BF16),
        compiler_params=_cp(("parallel",) * 3),
    )(keep, got)


def _pair_sum(part, got, name):
    nblk, rows, cols = part.shape
    hr = rows // 2
    tr = 512
    tc = cols if cols <= 1024 else (1024 if cols % 1024 == 0 else 896)
    c = lax.axis_index("c")
    nrt = hr // tr

    def body(c_ref, p_ref, g_ref, o_ref):
        del c_ref
        o_ref[...] = (p_ref[...] + g_ref[...]).astype(BF16)

    grid_spec = pltpu.PrefetchScalarGridSpec(
        num_scalar_prefetch=1, grid=(nblk, nrt, cols // tc),
        in_specs=[pl.BlockSpec((None, tr, tc), lambda b, i, j, cr: (b, cr[0] * nrt + i, j)),
                  pl.BlockSpec((None, tr, tc), lambda b, i, j, cr: (b, i, j))],
        out_specs=pl.BlockSpec((None, tr, tc), lambda b, i, j, cr: (b, i, j)))
    return pl.pallas_call(
        body, name=name, grid_spec=grid_spec,
        out_shape=jax.ShapeDtypeStruct((nblk, hr, cols), BF16),
        compiler_params=_cp(("parallel",) * 3),
    )(jnp.reshape(c, (1,)).astype(jnp.int32), part, got)


def _scatter_start(qs, name):
    n = len(qs)

    def body(*refs):
        ins, lands = refs[:n], refs[n:2 * n]
        token = refs[-1]
        for cp in _scatter_copies(ins, lands, refs[2 * n], refs[2 * n + 1]):
            cp.start()
        token[...] = jnp.zeros_like(token)

    lands = [lax.empty((3,) + q.shape[1:], q.dtype) for q in qs]
    res = pl.pallas_call(
        body, name=name,
        in_specs=[HBM_SPEC] * (2 * n),
        out_specs=[SEM_SPEC, SEM_SPEC] + [HBM_SPEC] * (2 * n) + [pl.BlockSpec(memory_space=pltpu.VMEM)],
        out_shape=[pltpu.SemaphoreType.DMA((3 * n,)), pltpu.SemaphoreType.DMA((3 * n,))]
        + [pltpu.HBM(b.shape, b.dtype) for b in qs + lands] + [TOKEN],
        input_output_aliases={a: 2 + a for a in range(2 * n)},
        compiler_params=pltpu.CompilerParams(has_side_effects=EFFECT),
    )(*[pltpu.with_memory_space_constraint(b, pltpu.HBM) for b in qs + lands])
    return res[0], res[1], list(res[2:2 + n]), list(res[2 + n:2 + 2 * n]), res[-1]


def _scatter_copies(ins, lands, send_sems, recv_sems):
    x, y, c = _place()
    cps = []
    for a in range(len(ins)):
        for j, (px, py) in enumerate(_other_chips(x, y)):
            cps.append(pltpu.make_async_remote_copy(
                src_ref=ins[a].at[2 * px + py], dst_ref=lands[a].at[j],
                send_sem=send_sems.at[3 * a + j], recv_sem=recv_sems.at[3 * a + j], device_id=(px, py, c), device_id_type=MESH))
    return cps


def _scatter_wait(qs, lands, send_sems, recv_sems, after, name):
    n = len(qs)

    def body(*refs):
        for cp in _scatter_copies(refs[:n], refs[n:2 * n], refs[2 * n], refs[2 * n + 1]):
            cp.wait_send()
            cp.wait_recv()

    res = pl.pallas_call(
        body, name=name,
        in_specs=[HBM_SPEC] * (2 * n) + [SEM_SPEC, SEM_SPEC, ANY_SPEC],
        out_specs=[HBM_SPEC] * (2 * n),
        out_shape=[pltpu.HBM(b.shape, b.dtype) for b in qs + lands],
        input_output_aliases={a: a for a in range(2 * n)},
        compiler_params=pltpu.CompilerParams(has_side_effects=EFFECT),
    )(*qs, *lands, send_sems, recv_sems, after)
    return list(res[:n]), list(res[n:])


def _chip_sum(q, got, name):
    _, hr, cols = got.shape
    tr, tc = 512, _tile_cols(cols)
    chip = 2 * lax.axis_index("x") + lax.axis_index("y")
    c = lax.axis_index("c")

    def body(idx_ref, q_ref, g_ref, o_ref):
        del idx_ref
        acc = q_ref[...].astype(F32)
        for i in range(3):
            acc = acc + g_ref[i].astype(F32)
        o_ref[...] = acc

    grid_spec = pltpu.PrefetchScalarGridSpec(
        num_scalar_prefetch=1, grid=(hr // tr, cols // tc),
        in_specs=[pl.BlockSpec((None, tr, tc), lambda i, j, ix: (ix[0], i, j)),
                  pl.BlockSpec((3, tr, tc), lambda i, j, ix: (0, i, j))],
        out_specs=pl.BlockSpec((None, tr, tc), lambda i, j, ix: (ix[1], i, j)))
    return pl.pallas_call(
        body, name=name, grid_spec=grid_spec,
        out_shape=jax.ShapeDtypeStruct((2, hr, cols), F32),
        compiler_params=_cp(("parallel", "parallel")),
    )(jnp.stack([chip, c]).astype(jnp.int32), q, got)


def _join_halves(bufs, name):
    n = len(bufs)

    def body(*refs):
        outs = refs[n:2 * n]
        send_sems, recv_sems = refs[2 * n:]
        x, y, c = _place()
        cps = []
        for a in range(n):
            cps.append(pltpu.make_async_remote_copy(src_ref=outs[a].at[c], dst_ref=outs[a].at[c], send_sem=send_sems.at[a],
                                                    recv_sem=recv_sems.at[a], device_id=(x, y, 1 - c), device_id_type=MESH))
        for cp in cps:
            cp.start()
        for a in range(n):
            theirs = outs[a].at[1 - c]
            pltpu.make_async_remote_copy(src_ref=theirs, dst_ref=theirs, send_sem=send_sems.at[a], recv_sem=recv_sems.at[a],
                                         device_id=(x, y, 1 - c), device_id_type=MESH).wait_recv()
        for cp in cps:
            cp.wait_send()

    hbm = pl.BlockSpec(memory_space=pl.ANY)
    return pl.pallas_call(
        body, name=name,
        in_specs=[hbm] * n, out_specs=[hbm] * n,
        out_shape=[jax.ShapeDtypeStruct(b.shape, b.dtype) for b in bufs],
        input_output_aliases={a: a for a in range(n)},
        scratch_shapes=[pltpu.SemaphoreType.DMA((n,)), pltpu.SemaphoreType.DMA((n,))],
    )(*bufs)


def _join_copy(buf, send_sems, recv_sems):
    x, y, c = _place()
    sems = dict(send_sem=send_sems.at[0], recv_sem=recv_sems.at[0], device_id=(x, y, 1 - c), device_id_type=MESH)
    return (pltpu.make_async_remote_copy(src_ref=buf.at[c], dst_ref=buf.at[c], **sems),
            pltpu.make_async_remote_copy(src_ref=buf.at[1 - c], dst_ref=buf.at[1 - c], **sems))


def _join_start(buf, name):
    def body(b_ref, send_sems, recv_sems, thru, token):
        _join_copy(b_ref, send_sems, recv_sems)[0].start()
        token[...] = jnp.zeros_like(token)

    return pl.pallas_call(
        body, name=name,
        in_specs=[HBM_SPEC],
        out_specs=[SEM_SPEC, SEM_SPEC, HBM_SPEC, pl.BlockSpec(memory_space=pltpu.VMEM)],
        out_shape=[pltpu.SemaphoreType.DMA((1,)), pltpu.SemaphoreType.DMA((1,)), pltpu.HBM(buf.shape, buf.dtype), TOKEN],
        input_output_aliases={0: 2},
        compiler_params=pltpu.CompilerParams(has_side_effects=EFFECT),
    )(pltpu.with_memory_space_constraint(buf, pltpu.HBM))


def _join_wait(state, after, name):
    send_sems, recv_sems, buf, _ = state

    def body(b_ref, send_ref, recv_ref, *rest):
        out, inc = _join_copy(b_ref, send_ref, recv_ref)
        out.wait_send()
        inc.wait_recv()

    return pl.pallas_call(
        body, name=name,
        in_specs=[HBM_SPEC, SEM_SPEC, SEM_SPEC] + [ANY_SPEC] * len(after),
        out_specs=HBM_SPEC,
        out_shape=pltpu.HBM(buf.shape, buf.dtype),
        input_output_aliases={0: 0},
        compiler_params=pltpu.CompilerParams(has_side_effects=EFFECT),
    )(buf, send_sems, recv_sems, *after)


N_DEV = 8


def _gather_small(block, reduce, name):
    m_per, cols = block.shape

    def body(x_ref, out_ref, all_ref, send_sems, recv_sems, local_sem):
        x, y, c = _place()
        me, sibling = (x, y, c), (x, y, 1 - c)
        chips = _other_chips(x, y)

        def rows(px, py, pc):
            return all_ref.at[pl.ds((4 * px + 2 * py + pc) * m_per, m_per), :]

        def copy(k, block_of, to, src=None):
            return pltpu.make_async_remote_copy(
                src_ref=rows(*block_of) if src is None else src, dst_ref=rows(*block_of),
                send_sem=send_sems.at[k], recv_sem=recv_sems.at[k], device_id=to, device_id_type=MESH)

        mine = pltpu.make_async_copy(x_ref, rows(*me), local_sem)
        mine.start()
        first = [copy(0, me, sibling, src=x_ref)]
        first += [copy(1 + j, me, (*chip, c), src=x_ref) for j, chip in enumerate(chips)]
        for cp in first:
            cp.start()
        passed = [copy(4 + j, (*chip, c), sibling) for j, chip in enumerate(chips)]
        for j, chip in enumerate(chips):
            copy(1 + j, (*chip, c), me).wait_recv()
            passed[j].start()
        copy(0, sibling, me).wait_recv()
        for j, chip in enumerate(chips):
            copy(4 + j, (*chip, 1 - c), me).wait_recv()
        for cp in first + passed:
            cp.wait_send()
        mine.wait()
        if reduce:
            acc = all_ref[pl.ds(0, m_per), :]
            for i in range(1, N_DEV):
                acc = acc + all_ref[pl.ds(i * m_per, m_per), :]
            out_ref[...] = acc
        else:
            out_ref[...] = all_ref[...]

    out_rows = m_per if reduce else N_DEV * m_per
    return pl.pallas_call(
        body, name=name,
        in_specs=[pl.BlockSpec(memory_space=pltpu.VMEM)],
        out_specs=pl.BlockSpec(memory_space=pltpu.VMEM),
        out_shape=jax.ShapeDtypeStruct((out_rows, cols), F32),
        scratch_shapes=[pltpu.VMEM((N_DEV * m_per, cols), F32), pltpu.SemaphoreType.DMA((7,)),
                        pltpu.SemaphoreType.DMA((7,)), pltpu.SemaphoreType.DMA],
    )(block)


class _Comm:
    def __init__(self, w_in, w_kv, w_out, chip):
        self.bufs = [_cast_own(w_in, chip, "cast_w_in"), _cast_own(w_kv, chip, "cast_w_mem_kv"),
                     _cast_own(w_out, chip, "cast_w_out")]


    def gather_started(self):
        (self.sems,), (self.b_in,), tok = _gather_start(self.bufs[:1], (NEAR,), "gather_start_in_near")
        return [tok]

    def w_in_own(self):
        return self.b_in.reshape(NCHIP * D, WB)

    def w_in_near(self, after):
        (b_in,) = _gather_wait([self.b_in], self.sems, NEAR, list(after) + self.bufs[1:], "gather_wait_in_near")
        (b_in,) = _forward_halves([b_in], NEAR, "forward_in_near")
        (self.sems,), (self.b_in,), tok = _gather_start([b_in], (FAR,), "gather_start_in_far")
        return self.b_in.reshape(NCHIP * D, WB), tok

    def w_in_all(self, after):
        (b_in,) = _gather_wait([self.b_in], self.sems, FAR, after, "gather_wait_in_far")
        (b_in,) = _forward_halves([b_in], FAR, "forward_in_far")
        (self.sems,), self.b_rest, tok = _gather_start(self.bufs[1:], (NEAR + FAR,), "gather_start_rest")
        return b_in.reshape(NCHIP * D, WB), tok

    def w_rest(self, after):
        b_rest = _gather_wait(self.b_rest, self.sems, NEAR + FAR, after, "gather_wait_rest")
        b_kv, b_out = _forward_halves(b_rest, NEAR + FAR, "forward_rest")
        return b_kv.reshape(D, 2 * XW), b_out.reshape(D, D)

    def reduce_start(self, parts, tag):
        got = _sibling_halves(parts, "sibling_halves_" + tag)
        qs = [_pair_sum(p, g, f"pair_sum_{tag}{i}") for i, (p, g) in enumerate(zip(parts, got))]
        return _scatter_start(qs, "scatter_start_" + tag)

    def sibling_start(self, send, tag):
        return _sibling_start(send, "sibling_start_" + tag)

    def reduce_start_summed(self, keep, sent, tag):
        got = _sibling_wait(sent, keep, "sibling_wait_" + tag)
        return _scatter_start([_pair_sum_rows(keep, got, "pair_sum_" + tag)], "scatter_start_" + tag)

    def reduce_finish_start(self, state, tag, after):
        send_sems, recv_sems, qs, lands, _ = state
        qs, lands = _scatter_wait(qs, lands, send_sems, recv_sems, after, "scatter_wait_" + tag)
        return _join_start(_chip_sum(qs[0], lands[0], f"chip_sum_{tag}0"), "join_start_" + tag)

    def reduce_finish_wait(self, pending, tag, after):
        j = _join_wait(pending, after, "join_wait_" + tag)
        return j.reshape(2 * j.shape[1], j.shape[2])

    def reduce_finish(self, state, tag, after):
        send_sems, recv_sems, qs, lands, _ = state
        qs, lands = _scatter_wait(qs, lands, send_sems, recv_sems, after, "scatter_wait_" + tag)
        halves = [_chip_sum(q, l, f"chip_sum_{tag}{i}") for i, (q, l) in enumerate(zip(qs, lands))]
        return [j.reshape(2 * j.shape[1], j.shape[2]) for j in _join_halves(halves, "join_halves_" + tag)]


def kernel(x, mem, pre_norm, w_in, conv_w, mem_norm, w_mem_kv, w_out, post_norm, loss_target, m_pre_norm, m_w_in, m_conv_w, m_mem_norm, m_w_mem_kv, m_w_out, m_post_norm, v_pre_norm, v_w_in, v_conv_w, v_mem_norm, v_w_mem_kv, v_w_out, v_post_norm):
    chip = 2 * lax.axis_index("x") + lax.axis_index("y")

    cw_blk = jnp.zeros((8, 384), F32).at[:3].set(conv_w[0])
    cw_all = _gather_small(cw_blk, False, "gather_conv_w").reshape(NCHIP, 2, 8, 384)[:, 0, :3]
    conv_full = jnp.transpose(cw_all, (1, 0, 2)).reshape(3, CW)

    comm = _Comm(w_in[0], w_mem_kv[0], w_out[0], chip)
    loss, grad_x, d_pre, d_mem, d_post, d_conv, r_in, r_kv, r_out = _local_step(
        x, mem, pre_norm, conv_full, mem_norm, post_norm, loss_target, chip, lax.axis_index("c"), comm)

    pack = jnp.concatenate([d_pre, d_mem, d_post, jnp.pad(d_conv, ((0, 0), (0, D - CW))),
                            jnp.pad(loss, ((0, 0), (0, D - 128))), jnp.zeros((1, D), F32)], axis=0)
    tot = _gather_small(pack, True, "reduce_small")
    g_pre, g_mem, g_post = tot[0:1], tot[1:2], tot[2:3]
    g_conv = lax.dynamic_slice(tot[3:6, :CW], (0, chip * 384), (3, 384))
    loss_out = tot[6, 0]

    names = ("pre_norm", "w_in", "conv_w", "mem_norm", "w_mem_kv", "w_out", "post_norm")
    ws = (pre_norm, w_in[0], conv_w[0], mem_norm, w_mem_kv[0], w_out[0], post_norm)
    gs = [g_pre, None, g_conv, g_mem, r_kv, r_out, g_post]
    ms = (m_pre_norm, m_w_in[0], m_conv_w[0], m_mem_norm, m_w_mem_kv[0], m_w_out[0], m_post_norm)
    vs = (v_pre_norm, v_w_in[0], v_conv_w[0], v_mem_norm, v_w_mem_kv[0], v_w_out[0], v_post_norm)
    upd = [None if g is None else _adamw(w, g, m, v, "adamw_" + nm) for nm, w, g, m, v in zip(names, ws, gs, ms, vs)]
    gs[1] = comm.reduce_finish_wait(r_in, "b", after=[u[1] for u in upd if u is not None])
    upd[1] = _adamw(ws[1], gs[1], ms[1], vs[1], "adamw_w_in")

    def shaped(arrs):
        return [a.reshape(w.shape) if w.ndim == a.ndim else a.reshape((1,) + a.shape)
                for a, w in zip(arrs, (pre_norm, w_in, conv_w, mem_norm, w_mem_kv, w_out, post_norm))]

    grads = shaped([u[0] for u in upd])
    deltas = shaped([u[1] for u in upd])
    new_m = shaped([u[2] for u in upd])
    new_v = shaped([u[3] for u in upd])
    return (loss_out, grad_x, *grads, *deltas, *new_m, *new_v)
```

```python
import jax
import jax.numpy as jnp
from jax import lax
from jax.experimental import pallas as pl
from jax.experimental.pallas import tpu as pltpu

F32 = jnp.float32
BF16 = jnp.bfloat16

D = 4096
S = 2048
NB = 2
T = NB * S
MLEN = 256
HD = 128
AW = 1536
CW = 1536
XW = 1024
XHD = 256
NXH = 4
NC = 14336
QA, KA, VA, ZA, UC, BC, CC, ZC, QX, ZX = 0, 1536, 3072, 4608, 6144, 7680, 9216, 10752, 12288, 13312
NCHIP = 4
WB = NC // NCHIP
DIL = (1, 4, 16)
HPG = 4
EPS = 1e-6
NEG = -1e30
ROPE_THETA = 10000.0
A_SCALE = HD ** -0.5
X_SCALE = XHD ** -0.5

ADAM_LR = 0.001
ADAM_B1 = 0.9
ADAM_B2 = 0.999
ADAM_EPS = 1e-08
ADAM_WD = 0.01
ADAM_STEP = 10

MESH = pl.DeviceIdType.MESH
MIB = 1024 * 1024


def _cp(sem, vmem_mib=48):
    return pltpu.CompilerParams(dimension_semantics=sem, vmem_limit_bytes=vmem_mib * MIB)


def _sigmoid(z):
    return 1.0 / (1.0 + jnp.exp(-z))


def _rope(x, cos, sin, half):
    return x * cos + pltpu.roll(x, half, 1) * sin


def _rope_t(g, cos, sin, half):
    return g * cos + pltpu.roll(g * sin, half, 1)


def _rms_fwd(x2, g, name, dep=None):
    rows = x2.shape[0]
    tr = 256
    deps = [] if dep is None else list(dep)

    def body(x_ref, g_ref, *rest):
        o_ref = rest[-1]
        x = x_ref[...]
        r = lax.rsqrt(jnp.mean(x * x, axis=-1, keepdims=True) + EPS)
        o_ref[...] = (x * r * g_ref[...]).astype(BF16)

    return pl.pallas_call(
        body, name=name, grid=(rows // tr,),
        in_specs=[pl.BlockSpec((tr, D), lambda i: (i, 0)), pl.BlockSpec((1, D), lambda i: (0, 0))]
        + [pl.BlockSpec(memory_space=pl.ANY)] * len(deps),
        out_specs=pl.BlockSpec((tr, D), lambda i: (i, 0)),
        out_shape=jax.ShapeDtypeStruct((rows, D), BF16),
        compiler_params=_cp(("parallel",)),
    )(x2, g, *deps)


def _norm_gain_grad(dn, x2, name):
    rows = x2.shape[0]
    tr = 256

    def body(dn_ref, x_ref, dg_ref):
        @pl.when(pl.program_id(0) == 0)
        def _():
            dg_ref[...] = jnp.zeros_like(dg_ref)
        x = x_ref[...]
        r = lax.rsqrt(jnp.mean(x * x, axis=-1, keepdims=True) + EPS)
        dg_ref[...] += jnp.sum(dn_ref[...] * (x * r), axis=0, keepdims=True)

    return pl.pallas_call(
        body, name=name, grid=(rows // tr,),
        in_specs=[pl.BlockSpec((tr, D), lambda i: (i, 0)), pl.BlockSpec((tr, D), lambda i: (i, 0))],
        out_specs=pl.BlockSpec((1, D), lambda i: (0, 0)),
        out_shape=jax.ShapeDtypeStruct((1, D), F32),
        compiler_params=_cp(("arbitrary",)),
    )(dn, x2)


def _pre_norm_bwd(dh, x2, g, dout):
    tr = 256

    def body(dh_ref, x_ref, g_ref, dout_ref, gx_ref, dg_ref):
        @pl.when(pl.program_id(0) == 0)
        def _():
            dg_ref[...] = jnp.zeros_like(dg_ref)
        x = x_ref[...]
        dh_ = dh_ref[...].astype(F32)
        r = lax.rsqrt(jnp.mean(x * x, axis=-1, keepdims=True) + EPS)
        xhat = x * r
        dg_ref[...] += jnp.sum(dh_ * xhat, axis=0, keepdims=True)
        dxn = dh_ * g_ref[...]
        gx_ref[...] = dout_ref[...].astype(F32) + r * (dxn - xhat * jnp.mean(dxn * xhat, axis=-1, keepdims=True))

    row = pl.BlockSpec((tr, D), lambda i: (i, 0))
    vec = pl.BlockSpec((1, D), lambda i: (0, 0))
    return pl.pallas_call(
        body, name="pre_norm_bwd", grid=(T // tr,),
        in_specs=[row, row, vec, row],
        out_specs=[row, vec],
        out_shape=[jax.ShapeDtypeStruct((T, D), F32), jax.ShapeDtypeStruct((1, D), F32)],
        compiler_params=_cp(("arbitrary",)),
    )(dh, x2, g, dout)


def _post_norm_loss(y, x2, tgt, g):
    tr = 256

    def body(y_ref, x_ref, t_ref, g_ref, dy_ref, dout_ref, dg_ref, loss_ref):
        @pl.when(pl.program_id(0) == 0)
        def _():
            dg_ref[...] = jnp.zeros_like(dg_ref)
            loss_ref[...] = jnp.zeros_like(loss_ref)
        yv = y_ref[...]
        gv = g_ref[...]
        r = lax.rsqrt(jnp.mean(yv * yv, axis=-1, keepdims=True) + EPS)
        yhat = yv * r
        err = x_ref[...] + yhat * gv - t_ref[...]
        loss_ref[...] += jnp.sum(jnp.sum(err * err, axis=1, keepdims=True), axis=0, keepdims=True) * (0.5 / D)
        dout = err * (1.0 / D)
        dout_ref[...] = dout.astype(BF16)
        dg_ref[...] += jnp.sum(dout * yhat, axis=0, keepdims=True)
        dyn = dout * gv
        dy_ref[...] = (r * (dyn - yhat * jnp.mean(dyn * yhat, axis=-1, keepdims=True))).astype(BF16)

    row = pl.BlockSpec((tr, D), lambda i: (i, 0))
    vec = pl.BlockSpec((1, D), lambda i: (0, 0))
    return pl.pallas_call(
        body, name="post_norm_loss", grid=(T // tr,),
        in_specs=[row, row, row, vec],
        out_specs=[row, row, vec, pl.BlockSpec((1, 128), lambda i: (0, 0))],
        out_shape=[jax.ShapeDtypeStruct((T, D), BF16), jax.ShapeDtypeStruct((T, D), BF16),
                   jax.ShapeDtypeStruct((1, D), F32), jax.ShapeDtypeStruct((1, 128), F32)],
        compiler_params=_cp(("arbitrary",)),
    )(y, x2, tgt, g)


NN = (((1,), (0,)), ((), ()))
NT = (((1,), (1,)), ((), ()))
TN = (((0,), (0,)), ((), ()))


def _as_index(v):
    return jnp.reshape(v, (1,)).astype(jnp.int32)


def _matmul(a, b, *, name, dims, grid, a_block, a_map, b_block, b_map, o_block, o_map, out_shape, out_dtype=F32,
            index=None, prev=None, deps=()):
    extra = ([] if prev is None else [prev]) + [d for d in deps if d is not None]
    first = 0 if index is None else 1
    nk = grid[2]
    in_place = out_dtype == F32

    def body(*refs):
        a_ref, b_ref, o_ref = refs[first], refs[first + 1], refs[first + 2 + len(extra)]
        acc_ref = o_ref if in_place else refs[-1]

        @pl.when(pl.program_id(2) == 0)
        def _():
            acc_ref[...] = lax.dot_general(a_ref[...], b_ref[...], dims, preferred_element_type=F32)

        @pl.when(pl.program_id(2) > 0)
        def _():
            acc_ref[...] += lax.dot_general(a_ref[...], b_ref[...], dims, preferred_element_type=F32)

        if not in_place:
            @pl.when(pl.program_id(2) == nk - 1)
            def _():
                o_ref[...] = acc_ref[...].astype(o_ref.dtype)

    in_specs = [pl.BlockSpec(a_block, a_map), pl.BlockSpec(b_block, b_map)] + [pl.BlockSpec(memory_space=pl.ANY)] * len(extra)
    out_specs = pl.BlockSpec(o_block, o_map)
    scratch = [] if in_place else [pltpu.VMEM(o_block, F32)]
    kwargs = dict(name=name, out_shape=jax.ShapeDtypeStruct(out_shape, out_dtype),
                  input_output_aliases={} if prev is None else {first + 2: 0},
                  compiler_params=_cp(("parallel", "parallel", "arbitrary"), vmem_mib=56))
    if index is None:
        return pl.pallas_call(body, grid=grid, in_specs=in_specs, out_specs=out_specs, scratch_shapes=scratch,
                              **kwargs)(a, b, *extra)
    grid_spec = pltpu.PrefetchScalarGridSpec(num_scalar_prefetch=1, grid=grid, in_specs=in_specs, out_specs=out_specs,
                                             scratch_shapes=scratch)
    return pl.pallas_call(body, grid_spec=grid_spec, **kwargs)(_as_index(index), a, b, *extra)


def _mm_nn(a, b, name, tm, tn, tk):
    m, kd = a.shape
    n = b.shape[1]
    return _matmul(a, b, name=name, dims=NN, grid=(m // tm, n // tn, kd // tk),
                   a_block=(tm, tk), a_map=lambda i, j, k: (i, k),
                   b_block=(tk, tn), b_map=lambda i, j, k: (k, j),
                   o_block=(tm, tn), o_map=lambda i, j, k: (i, j), out_shape=(m, n))


def _mm_nt(a, b, name, tm, tn, tk):
    m, kd = a.shape
    n = b.shape[0]
    return _matmul(a, b, name=name, dims=NT, grid=(m // tm, n // tn, kd // tk),
                   a_block=(tm, tk), a_map=lambda i, j, k: (i, k),
                   b_block=(tn, tk), b_map=lambda i, j, k: (j, k),
                   o_block=(tm, tn), o_map=lambda i, j, k: (i, j), out_shape=(m, n))


W_TN = 1792
W_NJ = WB // W_TN


def _proj_part(h, wg, chip, masks, name, prev=None, dep=None):
    tm, tk = 1024, 2048

    def blk(j, ix):
        m = masks[0]
        for t in range(1, len(masks)):
            m = jnp.where(j // W_NJ == t, masks[t], m)
        return jnp.bitwise_xor(ix[0], m)

    return _matmul(h, wg, name=name, dims=NN, grid=(T // tm, len(masks) * W_NJ, D // tk), index=chip, prev=prev, deps=(dep,),
                   a_block=(tm, tk), a_map=lambda i, j, k, ix: (i, k),
                   b_block=(tk, W_TN), b_map=lambda i, j, k, ix: (blk(j, ix) * (D // tk) + k, j % W_NJ),
                   o_block=(tm, W_TN), o_map=lambda i, j, k, ix: (i, blk(j, ix) * W_NJ + j % W_NJ), out_shape=(T, NC))


def _dh(dproj, wg, dep=None):
    tm, tn = 1024, 1024
    return _matmul(dproj, wg, name="dh", dims=NT, grid=(T // tm, D // tn, NC // W_TN), deps=(dep,),
                   a_block=(tm, W_TN), a_map=lambda i, j, k: (i, k),
                   b_block=(tn, W_TN), b_map=lambda i, j, k: ((k // W_NJ) * (D // tn) + j, k % W_NJ),
                   o_block=(tm, tn), o_map=lambda i, j, k: (i, j), out_shape=(T, D), out_dtype=BF16)


def _grad_rows(a, b, half, out_dtype, name, dep=None):
    kd, n = b.shape
    tm, tn, tk = D // NCHIP // 2, min(n, 2048), min(kd, 2048)
    return _matmul(a, b, name=name, dims=TN, grid=(NCHIP, n // tn, kd // tk), index=half, deps=(dep,),
                   a_block=(tk, tm), a_map=lambda i, j, k, ix: (k, 2 * i + ix[0]),
                   b_block=(tk, tn), b_map=lambda i, j, k, ix: (k, j),
                   o_block=(tm, tn), o_map=lambda i, j, k, ix: (i, j),
                   out_shape=(NCHIP * tm, n), out_dtype=out_dtype)


def _grad_w_in(h, dproj, half, out_dtype, name, dep=None):
    tm, tk = 1024, 2048
    nh = D // 2 // tm
    return _matmul(h, dproj, name=name, dims=TN, grid=(nh, NC // W_TN, T // tk), index=half, deps=(dep,),
                   a_block=(tk, tm), a_map=lambda i, j, k, ix: (k, ix[0] * nh + i),
                   b_block=(tk, W_TN), b_map=lambda i, j, k, ix: (k, j),
                   o_block=(tm, W_TN), o_map=lambda i, j, k, ix: ((j // W_NJ) * nh + i, j % W_NJ),
                   out_shape=(NCHIP * D // 2, WB), out_dtype=out_dtype)


def _rope_tables(pos, half):
    inv = 1.0 / (ROPE_THETA ** (jnp.arange(half, dtype=F32) / half))
    ang = pos.astype(F32)[:, None] * inv[None, :]
    cos, sin = jnp.cos(ang), jnp.sin(ang)
    return jnp.concatenate([cos, cos], axis=1), jnp.concatenate([-sin, sin], axis=1)


def _band_mask(r0):
    qi = lax.broadcasted_iota(jnp.int32, (128, 256), 0)
    kk = lax.broadcasted_iota(jnp.int32, (128, 256), 1)
    return (kk >= qi) & (kk <= qi + 128) & (kk + r0 >= 128)


def _window(r0, nblk):
    if nblk == 1:
        qi = lax.broadcasted_iota(jnp.int32, (128, 128), 0)
        kk = lax.broadcasted_iota(jnp.int32, (128, 128), 1)
        return pl.ds(128, 128), kk <= qi
    return pl.ds(r0, 256), _band_mask(r0)


def _dil_rows(r, n, d):
    if d == 1:
        return pl.ds(pl.multiple_of(n * 128, 128), 128)
    return pl.ds(r + d * 128 * n, 128, stride=d)


def _attn_fwd(proj, cosf, sinf, g):
    d = DIL[g]
    ln = S // d
    nblk = ln // 128
    proj_v = proj.reshape(NB, S, NC)

    def body(q_ref, k_ref, v_ref, cos_ref, sin_ref, o_ref, l_ref, k_s, v_s):
        k_s[:, pl.ds(0, 128), :] = jnp.zeros((d, 128, HD), BF16)
        v_s[:, pl.ds(0, 128), :] = jnp.zeros((d, 128, HD), BF16)

        def prep(i, carry):
            r, n = i // nblk, i % nblk
            rows = _dil_rows(r, n, d)
            dst = pl.ds(pl.multiple_of(n * 128 + 128, 128), 128)
            k_s[r, dst, :] = _rope(k_ref[rows, :], cos_ref[rows, :], sin_ref[rows, :], HD // 2).astype(BF16)
            v_s[r, dst, :] = v_ref[rows, :].astype(BF16)
            return carry

        lax.fori_loop(0, d * nblk, prep, 0, unroll=4)

        def step(i, carry):
            r, n = i // nblk, i % nblk
            rows = _dil_rows(r, n, d)
            r0 = pl.multiple_of(n * 128, 128)
            qr = _rope(q_ref[rows, :], cos_ref[rows, :], sin_ref[rows, :], HD // 2).astype(BF16)
            win, mask = _window(r0, nblk)
            kw = k_s[r, win, :]
            vw = v_s[r, win, :]
            sc = lax.dot_general(qr, kw, NT, preferred_element_type=F32) * A_SCALE
            sc = jnp.where(mask, sc, NEG)
            m = jnp.max(sc, axis=1, keepdims=True)
            p = jnp.exp(sc - m)
            l = jnp.sum(p, axis=1, keepdims=True)
            o_ref[rows, :] = jnp.dot(p.astype(BF16), vw, preferred_element_type=F32) / l
            l_ref[rows, :] = jnp.broadcast_to(m + jnp.log(l), (128, HD))
            return carry

        lax.fori_loop(0, d * nblk, step, 0, unroll=4)

    def col(off):
        return lambda b, h: (b, 0, off // HD + HPG * g + h)

    blk = (None, S, HD)
    tab = pl.BlockSpec((S, HD), lambda b, h: (0, 0))
    out = pl.BlockSpec(blk, lambda b, h: (b, 0, h))
    o, l = pl.pallas_call(
        body, name=f"attn_fwd_d{d}", grid=(NB, HPG),
        in_specs=[pl.BlockSpec(blk, col(QA)), pl.BlockSpec(blk, col(KA)), pl.BlockSpec(blk, col(VA)), tab, tab],
        out_specs=[out, out],
        out_shape=[jax.ShapeDtypeStruct((NB, S, HPG * HD), F32)] * 2,
        scratch_shapes=[pltpu.VMEM((d, ln + 128, HD), BF16), pltpu.VMEM((d, ln + 128, HD), BF16)],
        compiler_params=_cp(("parallel", "parallel")),
    )(proj_v, proj_v, proj_v, cosf, sinf)
    return o.reshape(T, HPG * HD), l.reshape(T, HPG * HD)


def _attn_bwd(proj, cosf, sinf, da, lse, delta, dproj, g):
    d = DIL[g]
    ln = S // d
    nblk = ln // 128
    proj_v = proj.reshape(NB, S, NC)
    dproj_v = dproj.reshape(NB, S, NC)
    da_v = da.reshape(NB, S, AW)
    lse_v = lse.reshape(NB, S, HPG * HD)
    delta_v = delta.reshape(NB, S, HPG * HD)

    def body(q_ref, k_ref, v_ref, cos_ref, sin_ref, da_ref, lse_ref, dl_ref, dp_in_ref, o_ref,
             k_s, v_s, stg, dk_s, dv_s):
        del dp_in_ref
        w = pl.program_id(2)

        def emit():
            def cast(n, carry):
                rows = pl.ds(pl.multiple_of(n * 256, 256), 256)
                o_ref[rows, :] = stg[rows, :].astype(BF16)
                return carry

            lax.fori_loop(0, S // 256, cast, 0)

        @pl.when(w == 0)
        def _():
            k_s[:, pl.ds(0, 128), :] = jnp.zeros((d, 128, HD), BF16)
            v_s[:, pl.ds(0, 128), :] = jnp.zeros((d, 128, HD), BF16)
            dk_s[...] = jnp.zeros_like(dk_s)
            dv_s[...] = jnp.zeros_like(dv_s)

            def prep(i, carry):
                r, n = i // nblk, i % nblk
                rows = _dil_rows(r, n, d)
                dst = pl.ds(pl.multiple_of(n * 128 + 128, 128), 128)
                k_s[r, dst, :] = _rope(k_ref[rows, :], cos_ref[rows, :], sin_ref[rows, :], HD // 2).astype(BF16)
                v_s[r, dst, :] = v_ref[rows, :].astype(BF16)
                return carry

            lax.fori_loop(0, d * nblk, prep, 0, unroll=4)

            def step(i, carry):
                r, n = i // nblk, i % nblk
                rows = _dil_rows(r, n, d)
                r0 = pl.multiple_of(n * 128, 128)
                win, mask = _window(r0, nblk)
                cos, sin = cos_ref[rows, :], sin_ref[rows, :]
                qr = _rope(q_ref[rows, :], cos, sin, HD // 2).astype(BF16)
                kw = k_s[r, win, :]
                vw = v_s[r, win, :]
                sc = lax.dot_general(qr, kw, NT, preferred_element_type=F32) * A_SCALE
                sc = jnp.where(mask, sc, NEG)
                p = jnp.exp(sc - lse_ref[rows, :][:, :1])
                da_b = da_ref[rows, :].astype(BF16)
                dp = lax.dot_general(da_b, vw, NT, preferred_element_type=F32)
                ds_b = (p * (dp - dl_ref[rows, :][:, :1]) * A_SCALE).astype(BF16)
                p_b = p.astype(BF16)
                dq = jnp.dot(ds_b, kw, preferred_element_type=F32)
                stg[rows, :] = _rope_t(dq, cos, sin, HD // 2)
                dk_s[r, win, :] += lax.dot_general(ds_b, qr, TN, preferred_element_type=F32)
                dv_s[r, win, :] += lax.dot_general(p_b, da_b, TN, preferred_element_type=F32)
                return carry

            lax.fori_loop(0, d * nblk, step, 0, unroll=4)
            emit()

        @pl.when(w == 1)
        def _():
            def put(i, carry):
                r, n = i // nblk, i % nblk
                rows = _dil_rows(r, n, d)
                src = pl.ds(pl.multiple_of(n * 128 + 128, 128), 128)
                stg[rows, :] = _rope_t(dk_s[r, src, :], cos_ref[rows, :], sin_ref[rows, :], HD // 2)
                return carry

            lax.fori_loop(0, d * nblk, put, 0, unroll=4)
            emit()

        @pl.when(w == 2)
        def _():
            def put(i, carry):
                r, n = i // nblk, i % nblk
                src = pl.ds(pl.multiple_of(n * 128 + 128, 128), 128)
                stg[_dil_rows(r, n, d), :] = dv_s[r, src, :]
                return carry

            lax.fori_loop(0, d * nblk, put, 0, unroll=4)
            emit()

    def col(off):
        return lambda b, h, w: (ahead(b, h, w)[0], 0, off // HD + HPG * g + ahead(b, h, w)[1])

    def ahead(b, h, w):
        flat = jnp.minimum(b * HPG + h + jnp.where(w > 0, 1, 0), NB * HPG - 1)
        return flat // HPG, flat % HPG

    blk = (None, S, HD)
    tab = pl.BlockSpec((S, HD), lambda b, h, w: (0, 0))
    per_head = pl.BlockSpec(blk, lambda b, h, w: (ahead(b, h, w)[0], 0, ahead(b, h, w)[1]))
    out = pl.pallas_call(
        body, name=f"attn_bwd_d{d}", grid=(NB, HPG, 3),
        in_specs=[pl.BlockSpec(blk, col(QA)), pl.BlockSpec(blk, col(KA)), pl.BlockSpec(blk, col(VA)), tab, tab,
                  pl.BlockSpec(blk, col(0)), per_head, per_head, pl.BlockSpec(memory_space=pl.ANY)],
        out_specs=pl.BlockSpec(blk, lambda b, h, w: (b, 0, (AW // HD) * w + HPG * g + h)),
        out_shape=jax.ShapeDtypeStruct(dproj_v.shape, BF16),
        input_output_aliases={8: 0},
        scratch_shapes=[pltpu.VMEM((d, ln + 128, HD), BF16), pltpu.VMEM((d, ln + 128, HD), BF16),
                        pltpu.VMEM((S, HD), F32), pltpu.VMEM((d, ln + 128, HD), F32), pltpu.VMEM((d, ln + 128, HD), F32)],
        compiler_params=_cp(("arbitrary",) * 3),
    )(proj_v, proj_v, proj_v, cosf, sinf, da_v, lse_v, delta_v, dproj_v)
    return out.reshape(T, NC)


def _attn_mix(proj, os_, ls_, dep=None):
    tr = 256
    gw = HPG * HD
    deps = [] if dep is None else [dep]

    def body(o0, o1, o2, l0, l1, l2, z_ref, *rest):
        cat_ref = rest[-1]
        m = jnp.maximum(jnp.maximum(l0[...], l1[...]), l2[...])
        e = [jnp.exp(l[...] - m) for l in (l0, l1, l2)]
        inv = 1.0 / (e[0] + e[1] + e[2])
        for gi, o in enumerate((o0, o1, o2)):
            z = z_ref[:, gi * gw:(gi + 1) * gw]
            cat_ref[:, gi * gw:(gi + 1) * gw] = (o[...] * (e[gi] * inv) * (z * _sigmoid(z))).astype(BF16)

    grp = pl.BlockSpec((tr, gw), lambda i: (i, 0))
    return pl.pallas_call(
        body, name="attn_mix", grid=(T // tr,),
        in_specs=[grp] * 6 + [pl.BlockSpec((tr, AW), lambda i: (i, ZA // AW))] + [ANY_SPEC] * len(deps),
        out_specs=pl.BlockSpec((tr, AW), lambda i: (i, 0)),
        out_shape=jax.ShapeDtypeStruct((T, D), BF16),
        compiler_params=_cp(("parallel",)),
    )(*os_, *ls_, proj, *deps)


def _attn_mix_bwd(dcat, proj, os_, ls_, dep=None):
    tr = 256
    gw = HPG * HD
    deps = [] if dep is None else [dep]

    def body(dy_ref, o0, o1, o2, l0, l1, l2, z_ref, *rest):
        da_ref, lse_ref, dl_ref, dz_ref = rest[len(deps):]
        m = jnp.maximum(jnp.maximum(l0[...], l1[...]), l2[...])
        e = [jnp.exp(l[...] - m) for l in (l0, l1, l2)]
        den = e[0] + e[1] + e[2]
        inv = 1.0 / den
        lse_ref[...] = m + jnp.log(den)
        acc = jnp.zeros((tr, gw), F32)
        for gi, o in enumerate((o0, o1, o2)):
            cols = slice(gi * gw, (gi + 1) * gw)
            z = z_ref[:, cols]
            dy = dy_ref[:, cols]
            sg = _sigmoid(z)
            a = o[...] * (e[gi] * inv)
            da = dy * (z * sg)
            da_ref[:, cols] = da
            dz_ref[:, cols] = (dy * a * (sg * (1.0 + z * (1.0 - sg)))).astype(BF16)
            acc = acc + da * a
        for hh in range(HPG):
            cols = slice(hh * HD, (hh + 1) * HD)
            dl_ref[:, cols] = jnp.broadcast_to(jnp.sum(acc[:, cols], axis=1, keepdims=True), (tr, HD))

    grp = pl.BlockSpec((tr, gw), lambda i: (i, 0))
    return pl.pallas_call(
        body, name="attn_mix_bwd", grid=(T // tr,),
        in_specs=[pl.BlockSpec((tr, AW), lambda i: (i, 0))] + [grp] * 6 + [pl.BlockSpec((tr, AW), lambda i: (i, ZA // AW))]
        + [pl.BlockSpec(memory_space=pl.ANY)] * len(deps),
        out_specs=[pl.BlockSpec((tr, AW), lambda i: (i, 0)), grp, grp, pl.BlockSpec((tr, AW), lambda i: (i, ZA // AW))],
        out_shape=[jax.ShapeDtypeStruct((T, AW), F32), jax.ShapeDtypeStruct((T, gw), F32),
                   jax.ShapeDtypeStruct((T, gw), F32), jax.ShapeDtypeStruct((T, NC), BF16)],
        compiler_params=_cp(("parallel",)),
    )(dcat, *os_, *ls_, proj, *deps)


CT = 256


def _shift_down(x, n):
    rows = lax.broadcasted_iota(jnp.int32, x.shape, 0)
    return jnp.where(rows >= n, pltpu.roll(x, n, 0), 0.0)


def _shift_up(x, n):
    rows = lax.broadcasted_iota(jnp.int32, x.shape, 0)
    return jnp.where(rows < x.shape[0] - n, pltpu.roll(x, x.shape[0] - n, 0), 0.0)


def _conv_fwd(proj, conv_w, cat):
    proj_v = proj.reshape(NB, S, NC)
    cat_v = cat.reshape(NB, S, D)

    def body(u_ref, b_ref, c_ref, z_ref, w_ref, cat_in, o_ref):
        del cat_in
        cu = c_ref[...] * u_ref[...]
        cv = _shift_down(cu, 2) * w_ref[0:1, :] + _shift_down(cu, 1) * w_ref[1:2, :] + cu * w_ref[2:3, :]
        z = z_ref[...]
        o_ref[...] = (b_ref[...] * cv * (z * _sigmoid(z))).astype(BF16)

    def seg(off):
        return pl.BlockSpec((None, S, CT), lambda b, j: (b, 0, off // CT + j))

    out = pl.pallas_call(
        body, name="conv_fwd", grid=(NB, CW // CT),
        in_specs=[seg(UC), seg(BC), seg(CC), seg(ZC), pl.BlockSpec((3, CT), lambda b, j: (0, j)),
                  pl.BlockSpec(memory_space=pl.ANY)],
        out_specs=pl.BlockSpec((None, S, CT), lambda b, j: (b, 0, AW // CT + j)),
        out_shape=jax.ShapeDtypeStruct((NB, S, D), BF16),
        input_output_aliases={5: 0},
        compiler_params=_cp(("parallel", "parallel")),
    )(proj_v, proj_v, proj_v, proj_v, conv_w, cat_v)
    return out.reshape(T, D)


def _conv_bwd(dcat, proj, conv_w, dproj):
    proj_v = proj.reshape(NB, S, NC)
    dproj_v = dproj.reshape(NB, S, NC)
    dcat_v = dcat.reshape(NB, S, D)

    def body(dy_ref, u_ref, b_ref, c_ref, z_ref, w_ref, dp_in, o_ref, dw_ref, st):
        del dp_in
        b = pl.program_id(1)
        w = pl.program_id(2)

        @pl.when((b == 0) & (w == 0))
        def _():
            dw_ref[...] = jnp.zeros_like(dw_ref)

        @pl.when(w == 0)
        def _():
            u, c, z, bb, dy = u_ref[...], c_ref[...], z_ref[...], b_ref[...], dy_ref[...]
            cu = c * u
            s1 = _shift_down(cu, 1)
            s2 = _shift_down(cu, 2)
            cv = s2 * w_ref[0:1, :] + s1 * w_ref[1:2, :] + cu * w_ref[2:3, :]
            sg = _sigmoid(z)
            sz = z * sg
            dcv = dy * bb * sz
            st[0] = dy * cv * sz
            st[2] = dy * bb * cv * (sg * (1.0 + z * (1.0 - sg)))
            dw_ref[0:1, :] += jnp.sum(dcv * s2, axis=0, keepdims=True)
            dw_ref[1:2, :] += jnp.sum(dcv * s1, axis=0, keepdims=True)
            dw_ref[2:3, :] += jnp.sum(dcv * cu, axis=0, keepdims=True)
            dcu = dcv * w_ref[2:3, :] + _shift_up(dcv, 1) * w_ref[1:2, :] + _shift_up(dcv, 2) * w_ref[0:1, :]
            st[1] = dcu * u
            o_ref[...] = (dcu * c).astype(BF16)

        for k in range(3):
            @pl.when(w == k + 1)
            def _(k=k):
                o_ref[...] = st[k].astype(BF16)

    def ahead(j, b, w):
        flat = jnp.minimum(j * NB + b + jnp.where(w > 0, 1, 0), (CW // CT) * NB - 1)
        return flat // NB, flat % NB

    def seg(off):
        return pl.BlockSpec((None, S, CT), lambda j, b, w: (ahead(j, b, w)[1], 0, off // CT + ahead(j, b, w)[0]))

    out, dw = pl.pallas_call(
        body, name="conv_bwd", grid=(CW // CT, NB, 4),
        in_specs=[seg(AW), seg(UC), seg(BC), seg(CC), seg(ZC),
                  pl.BlockSpec((3, CT), lambda j, b, w: (0, ahead(j, b, w)[0])),
                  pl.BlockSpec(memory_space=pl.ANY)],
        out_specs=[pl.BlockSpec((None, S, CT), lambda j, b, w: (b, 0, (UC + w * CW) // CT + j)),
                   pl.BlockSpec((3, CT), lambda j, b, w: (0, j))],
        out_shape=[jax.ShapeDtypeStruct((NB, S, NC), BF16), jax.ShapeDtypeStruct((3, CW), F32)],
        input_output_aliases={6: 0},
        scratch_shapes=[pltpu.VMEM((3, S, CT), F32)],
        compiler_params=_cp(("arbitrary",) * 3),
    )(dcat_v, proj_v, proj_v, proj_v, proj_v, conv_w, dproj_v)
    return out.reshape(T, NC), dw


XT = 1024


def _cross_fwd(proj, mkv, cosq, sinq, cosm, sinm, cat):
    proj_v = proj.reshape(NB, S, NC)
    mkv_v = mkv.reshape(NB, MLEN, 2 * XW)
    cat_v = cat.reshape(NB, S, D)

    def body(q_ref, z_ref, mk_ref, mv_ref, cq, sq, cm, sm, cat_in, o_ref):
        del cat_in
        mkr = _rope(mk_ref[...], cm[...], sm[...], XHD // 2).astype(BF16)
        qr = _rope(q_ref[...], cq[...], sq[...], XHD // 2).astype(BF16)
        sc = lax.dot_general(qr, mkr, NT, preferred_element_type=F32) * X_SCALE
        p = jnp.exp(sc - jnp.max(sc, axis=1, keepdims=True))
        p = p / jnp.sum(p, axis=1, keepdims=True)
        ox = jnp.dot(p.astype(BF16), mv_ref[...].astype(BF16), preferred_element_type=F32)
        z = z_ref[...]
        o_ref[...] = (ox * (z * _sigmoid(z))).astype(BF16)

    def seg(off):
        return pl.BlockSpec((None, XT, XHD), lambda b, h, t: (b, t, off // XHD + h))

    qtab = pl.BlockSpec((XT, XHD), lambda b, h, t: (t, 0))
    mtab = pl.BlockSpec((MLEN, XHD), lambda b, h, t: (0, 0))
    out = pl.pallas_call(
        body, name="cross_fwd", grid=(NB, NXH, S // XT),
        in_specs=[seg(QX), seg(ZX),
                  pl.BlockSpec((None, MLEN, XHD), lambda b, h, t: (b, 0, h)),
                  pl.BlockSpec((None, MLEN, XHD), lambda b, h, t: (b, 0, NXH + h)),
                  qtab, qtab, mtab, mtab, pl.BlockSpec(memory_space=pl.ANY)],
        out_specs=pl.BlockSpec((None, XT, XHD), lambda b, h, t: (b, t, (AW + CW) // XHD + h)),
        out_shape=jax.ShapeDtypeStruct((NB, S, D), BF16),
        input_output_aliases={8: 0},
        compiler_params=_cp(("parallel",) * 3),
    )(proj_v, proj_v, mkv_v, mkv_v, cosq, sinq, cosm, sinm, cat_v)
    return out.reshape(T, D)


def _cross_bwd(dcat, proj, mkv, cosq, sinq, cosm, sinm, dproj):
    proj_v = proj.reshape(NB, S, NC)
    dproj_v = dproj.reshape(NB, S, NC)
    dcat_v = dcat.reshape(NB, S, D)
    mkv_v = mkv.reshape(NB, MLEN, 2 * XW)
    nt = S // XT

    def body(dy_ref, q_ref, z_ref, mk_ref, mv_ref, cq, sq, cm, sm, dp_in, o_ref, dmk_ref, dmv_ref, dz_s):
        del dp_in
        t = pl.program_id(2)
        w = pl.program_id(3)

        @pl.when((t == 0) & (w == 0))
        def _():
            dmk_ref[...] = jnp.zeros_like(dmk_ref)
            dmv_ref[...] = jnp.zeros_like(dmv_ref)

        @pl.when(w == 0)
        def _():
            mkr = _rope(mk_ref[...], cm[...], sm[...], XHD // 2).astype(BF16)
            mv_b = mv_ref[...].astype(BF16)
            qr = _rope(q_ref[...], cq[...], sq[...], XHD // 2).astype(BF16)
            sc = lax.dot_general(qr, mkr, NT, preferred_element_type=F32) * X_SCALE
            p = jnp.exp(sc - jnp.max(sc, axis=1, keepdims=True))
            p = p / jnp.sum(p, axis=1, keepdims=True)
            p_b = p.astype(BF16)
            ox = jnp.dot(p_b, mv_b, preferred_element_type=F32)
            z = z_ref[...]
            dy = dy_ref[...]
            sg = _sigmoid(z)
            dz_s[...] = dy * ox * (sg * (1.0 + z * (1.0 - sg)))
            dox_b = (dy * (z * sg)).astype(BF16)
            dp = lax.dot_general(dox_b, mv_b, NT, preferred_element_type=F32)
            ds_b = (p * (dp - jnp.sum(dp * p, axis=1, keepdims=True)) * X_SCALE).astype(BF16)
            dq = jnp.dot(ds_b, mkr, preferred_element_type=F32)
            o_ref[...] = _rope_t(dq, cq[...], sq[...], XHD // 2).astype(BF16)
            dmk_ref[...] += lax.dot_general(ds_b, qr, TN, preferred_element_type=F32)
            dmv_ref[...] += lax.dot_general(p_b, dox_b, TN, preferred_element_type=F32)

        @pl.when(w == 1)
        def _():
            o_ref[...] = dz_s[...].astype(BF16)

        @pl.when((t == nt - 1) & (w == 1))
        def _():
            dmk_ref[...] = _rope_t(dmk_ref[...], cm[...], sm[...], XHD // 2)

    def seg(off):
        return pl.BlockSpec((None, XT, XHD), lambda b, h, t, w: (b, t, off // XHD + h))

    qtab = pl.BlockSpec((XT, XHD), lambda b, h, t, w: (t, 0))
    mtab = pl.BlockSpec((MLEN, XHD), lambda b, h, t, w: (0, 0))
    macc = pl.BlockSpec((None, MLEN, XHD), lambda b, h, t, w: (b, 0, h))
    out, dmk, dmv = pl.pallas_call(
        body, name="cross_bwd", grid=(NB, NXH, nt, 2),
        in_specs=[pl.BlockSpec((None, XT, XHD), lambda b, h, t, w: (b, t, (AW + CW) // XHD + h)),
                  seg(QX), seg(ZX),
                  pl.BlockSpec((None, MLEN, XHD), lambda b, h, t, w: (b, 0, h)),
                  pl.BlockSpec((None, MLEN, XHD), lambda b, h, t, w: (b, 0, NXH + h)),
                  qtab, qtab, mtab, mtab, pl.BlockSpec(memory_space=pl.ANY)],
        out_specs=[pl.BlockSpec((None, XT, XHD), lambda b, h, t, w: (b, t, (QX + w * XW) // XHD + h)), macc, macc],
        out_shape=[jax.ShapeDtypeStruct((NB, S, NC), BF16), jax.ShapeDtypeStruct((NB, MLEN, XW), F32),
                   jax.ShapeDtypeStruct((NB, MLEN, XW), F32)],
        input_output_aliases={9: 0},
        scratch_shapes=[pltpu.VMEM((XT, XHD), F32)],
        compiler_params=_cp(("arbitrary",) * 4),
    )(dcat_v, proj_v, proj_v, mkv_v, mkv_v, cosq, sinq, cosm, sinm, dproj_v)
    return out.reshape(T, NC), dmk, dmv


def _local_step(x, mem, pre_norm, conv_w, mem_norm, post_norm, tgt, chip, core, comm):
    x2 = x.reshape(T, D)
    mem2 = mem.reshape(NB * MLEN, D)
    tgt2 = tgt.reshape(T, D)
    cosa, sina = _rope_tables(jnp.arange(S), HD // 2)
    cosq, sinq = _rope_tables(jnp.arange(S) + MLEN, XHD // 2)
    cosm, sinm = _rope_tables(jnp.arange(MLEN), XHD // 2)

    h = _rms_fwd(x2, pre_norm, "pre_norm_fwd", dep=comm.gather_started())
    memn = _rms_fwd(mem2, mem_norm, "mem_norm_fwd")
    proj = _proj_part(h, comm.w_in_own(), chip, (0,), "proj_own")
    wg_in, tok = comm.w_in_near(after=[proj, memn, conv_w])
    proj = _proj_part(h, wg_in, chip, (REL_XOR[0], REL_XOR[1]), "proj_near", prev=proj, dep=tok)
    wg_in, tok = comm.w_in_all(after=proj)
    proj = _proj_part(h, wg_in, chip, (REL_XOR[2],), "proj_far", prev=proj, dep=tok)
    fw = [_attn_fwd(proj, cosa, sina, g) for g in range(3)]
    os_ = [f[0] for f in fw]
    ls_ = [f[1] for f in fw]
    tok = comm.w_rest_landed(after=[ls_[2]])
    cat = _attn_mix(proj, os_, ls_, dep=tok)
    cat = _conv_fwd(proj, conv_w, cat)
    wg_kv, wg_out = comm.w_rest(after=[cat])
    mkv = _mm_nn(memn, wg_kv, "mkv", NB * MLEN, 1024, 512)
    cat = _cross_fwd(proj, mkv, cosq, sinq, cosm, sinm, cat)
    y = _mm_nn(cat, wg_out, "out_proj", 1024, 1024, 2048)
    dy, dout, d_post, loss = _post_norm_loss(y, x2, tgt2, post_norm)

    dcat = _mm_nt(dy, wg_out, "dcat", 1024, 1024, 2048)
    g_send = _grad_rows(cat, dy,1 - core, BF16, "grad_w_out_send")
    sent = comm.sibling_start(g_send.reshape(NCHIP, D // NCHIP // 2, D), "a")
    g_keep = _grad_rows(cat, dy,core, BF16, "grad_w_out_keep", dep=sent[-1])
    red_a = comm.reduce_start_summed(g_keep.reshape(NCHIP, D // NCHIP // 2, D), sent, "a")
    da, lse, delta, dproj = _attn_mix_bwd(dcat, proj, os_, ls_, dep=red_a[-1])
    for g in range(3):
        dproj = _attn_bwd(proj, cosa, sina, da, lse, delta, dproj, g)
    dproj, d_conv = _conv_bwd(dcat, proj, conv_w, dproj)
    dproj, dmk, dmv = _cross_bwd(dcat, proj, mkv, cosq, sinq, cosm, sinm, dproj)

    dmkv = jnp.concatenate([dmk, dmv], axis=-1).reshape(NB * MLEN, 2 * XW)
    dmkv_b = dmkv.astype(BF16)
    gk_send = _grad_rows(memn, dmkv_b, 1 - core, BF16, "grad_w_mem_kv_send")
    sent = comm.sibling_start(gk_send.reshape(NCHIP, D // NCHIP // 2, 2 * XW), "k")
    gk_keep = _grad_rows(memn, dmkv_b, core, BF16, "grad_w_mem_kv_keep", dep=sent[-1])
    dmemn = _mm_nt(dmkv_b, wg_kv, "dmemn", NB * MLEN, 1024, 512)
    d_mem = _norm_gain_grad(dmemn, mem2, "mem_norm_bwd")
    red_k = comm.reduce_start_summed(gk_keep.reshape(NCHIP, D // NCHIP // 2, 2 * XW), sent, "k", after=d_mem)
    g_send = _grad_w_in(h, dproj, 1 - core, BF16, "grad_w_in_send", dep=red_k[-1])
    sent = comm.sibling_start(g_send.reshape(NCHIP, D // 2, WB), "b")
    g_keep = _grad_w_in(h, dproj, core, BF16, "grad_w_in_keep", dep=sent[-1])
    red_b = comm.reduce_start_summed(g_keep.reshape(NCHIP, D // 2, WB), sent, "b")
    dh = _dh(dproj, wg_in, dep=red_b[-1])
    grad_x, d_pre = _pre_norm_bwd(dh, x2, pre_norm, dout)
    r_in = comm.reduce_finish_start(red_b, "b", after=grad_x)
    (r_out,) = comm.reduce_finish(red_a, "a", after=grad_x)
    (r_kv,) = comm.reduce_finish(red_k, "k", after=grad_x)
    return loss, grad_x.reshape(NB, S, D), d_pre, d_mem, d_post, d_conv, r_in, r_kv, r_out


def _adamw(w, g, m, v, name):
    rows, cols = w.shape
    tr = rows if rows <= 512 else 512
    tc = cols if cols <= 1024 else 1024
    if cols % tc:
        tc = 896

    def body(w_ref, g_ref, m_ref, v_ref, go_ref, d_ref, nm_ref, nv_ref):
        gv = g_ref[...]
        go_ref[...] = gv
        nm = ADAM_B1 * m_ref[...] + (1.0 - ADAM_B1) * gv
        nv = ADAM_B2 * v_ref[...] + (1.0 - ADAM_B2) * (gv * gv)
        m_hat = nm / (1.0 - ADAM_B1 ** ADAM_STEP)
        v_hat = nv / (1.0 - ADAM_B2 ** ADAM_STEP)
        d_ref[...] = -ADAM_LR * (m_hat / (jnp.sqrt(v_hat) + ADAM_EPS) + ADAM_WD * w_ref[...])
        nm_ref[...] = nm
        nv_ref[...] = nv

    blk = pl.BlockSpec((tr, tc), lambda i, j: (i, j))
    sds = jax.ShapeDtypeStruct((rows, cols), F32)
    return pl.pallas_call(
        body, name=name, grid=(rows // tr, cols // tc),
        in_specs=[blk] * 4, out_specs=[blk] * 4, out_shape=[sds] * 4,
        compiler_params=_cp(("parallel", "parallel")),
    )(w, g, m, v)


def _place():
    return lax.axis_index("x"), lax.axis_index("y"), lax.axis_index("c")


def _other_chips(x, y):
    return [(1 - x, y), (x, 1 - y), (1 - x, 1 - y)]


def _tile_cols(cols):
    return cols if cols <= 1024 else (1024 if cols % 1024 == 0 else 896)


def _cast_own(w, chip, name):
    rows, cols = w.shape
    tr, tc = 512, _tile_cols(cols)

    def body(chip_ref, w_ref, o_ref):
        del chip_ref
        o_ref[...] = w_ref[...].astype(BF16)

    grid_spec = pltpu.PrefetchScalarGridSpec(
        num_scalar_prefetch=1, grid=(rows // tr, cols // tc),
        in_specs=[pl.BlockSpec((tr, tc), lambda i, j, cr: (i, j))],
        out_specs=pl.BlockSpec((None, tr, tc), lambda i, j, cr: (cr[0], i, j)))
    return pl.pallas_call(
        body, name=name, grid_spec=grid_spec,
        out_shape=jax.ShapeDtypeStruct((NCHIP, rows, cols), BF16),
        compiler_params=_cp(("parallel", "parallel")),
    )(_as_index(chip), w)


HBM_SPEC = pl.BlockSpec(memory_space=pltpu.HBM)
SEM_SPEC = pl.BlockSpec(memory_space=pltpu.SEMAPHORE)
ANY_SPEC = pl.BlockSpec(memory_space=pl.ANY)
EFFECT = pltpu.SideEffectType.DATAFLOW_SIDE_EFFECTING
TOKEN = jax.ShapeDtypeStruct((8, 128), F32)


def _half(ref, chip, hc):
    hr = ref.shape[1] // 2
    return ref.at[chip, pl.ds(hc * hr, hr), :]


NEAR = (0, 1)
FAR = (2,)
REL_XOR = (2, 1, 3)


def _gather_copies(refs, send_sems, recv_sems, rels):
    x, y, c = _place()
    chips = _other_chips(x, y)
    out, inc = [], []
    for a, ref in enumerate(refs):
        for p, j in enumerate(rels):
            px, py = chips[j]
            mine = _half(ref, 2 * x + y, c)
            theirs = _half(ref, 2 * px + py, c)
            sems = dict(send_sem=send_sems.at[len(rels) * a + p], recv_sem=recv_sems.at[len(rels) * a + p],
                        device_id=(px, py, c), device_id_type=MESH)
            out.append(pltpu.make_async_remote_copy(src_ref=mine, dst_ref=mine, **sems))
            inc.append(pltpu.make_async_remote_copy(src_ref=theirs, dst_ref=theirs, **sems))
    return out, inc


def _gather_start(bufs, groups, name):
    n = len(bufs)
    ng = len(groups)

    def body(*refs):
        ins = refs[:n]
        token = refs[-1]
        for gi, rels in enumerate(groups):
            out, _ = _gather_copies(ins, refs[n + 2 * gi], refs[n + 2 * gi + 1], rels)
            for cp in out:
                cp.start()
        token[...] = jnp.zeros_like(token)

    sems = []
    for rels in groups:
        sems += [pltpu.SemaphoreType.DMA((len(rels) * n,))] * 2
    res = pl.pallas_call(
        body, name=name,
        in_specs=[HBM_SPEC] * n,
        out_specs=[SEM_SPEC] * (2 * ng) + [HBM_SPEC] * n + [pl.BlockSpec(memory_space=pltpu.VMEM)],
        out_shape=sems + [pltpu.HBM(b.shape, b.dtype) for b in bufs] + [TOKEN],
        input_output_aliases={a: 2 * ng + a for a in range(n)},
        compiler_params=pltpu.CompilerParams(has_side_effects=EFFECT),
    )(*[pltpu.with_memory_space_constraint(b, pltpu.HBM) for b in bufs])
    return [(res[2 * gi], res[2 * gi + 1]) for gi in range(ng)], list(res[2 * ng:2 * ng + n]), res[-1]


def _gather_wait(bufs, sems, rels, after, name):
    n = len(bufs)
    send_sems, recv_sems = sems
    after = list(after) if isinstance(after, (list, tuple)) else [after]

    def body(*refs):
        ins = refs[:n]
        out, inc = _gather_copies(ins, refs[n], refs[n + 1], rels)
        for cp in out:
            cp.wait_send()
        for cp in inc:
            cp.wait_recv()

    return pl.pallas_call(
        body, name=name,
        in_specs=[HBM_SPEC] * n + [SEM_SPEC, SEM_SPEC] + [ANY_SPEC] * len(after),
        out_specs=[HBM_SPEC] * n,
        out_shape=[pltpu.HBM(b.shape, b.dtype) for b in bufs],
        input_output_aliases={a: a for a in range(n)},
        compiler_params=pltpu.CompilerParams(has_side_effects=EFFECT),
    )(*bufs, send_sems, recv_sems, *after)


def _forward_halves(bufs, rels, name):
    n = len(bufs)

    def body(*refs):
        cps, waits = _forward_copies(refs[n:2 * n], rels, refs[2 * n], refs[2 * n + 1])
        for cp in cps:
            cp.start()
        for cp in waits:
            cp.wait_recv()
        for cp in cps:
            cp.wait_send()

    return pl.pallas_call(
        body, name=name,
        in_specs=[ANY_SPEC] * n, out_specs=[ANY_SPEC] * n,
        out_shape=[jax.ShapeDtypeStruct(s.shape, s.dtype) for s in bufs],
        input_output_aliases={a: a for a in range(n)},
        scratch_shapes=[pltpu.SemaphoreType.DMA((len(rels) * n,)), pltpu.SemaphoreType.DMA((len(rels) * n,))],
    )(*bufs)


def _forward_copies(refs, rels, send_sems, recv_sems):
    x, y, c = _place()
    chips = _other_chips(x, y)
    cps, waits = [], []
    for a, ref in enumerate(refs):
        for p, j in enumerate(rels):
            px, py = chips[j]
            sems = dict(send_sem=send_sems.at[len(rels) * a + p], recv_sem=recv_sems.at[len(rels) * a + p],
                        device_id=(x, y, 1 - c), device_id_type=MESH)
            got = _half(ref, 2 * px + py, c)
            want = _half(ref, 2 * px + py, 1 - c)
            cps.append(pltpu.make_async_remote_copy(src_ref=got, dst_ref=got, **sems))
            waits.append(pltpu.make_async_remote_copy(src_ref=want, dst_ref=want, **sems))
    return cps, waits


def _forward_start(bufs, rels, name):
    n = len(bufs)

    def body(*refs):
        cps, _ = _forward_copies(refs[:n], rels, refs[n], refs[n + 1])
        for cp in cps:
            cp.start()
        refs[-1][...] = jnp.zeros_like(refs[-1])

    res = pl.pallas_call(
        body, name=name,
        in_specs=[HBM_SPEC] * n,
        out_specs=[SEM_SPEC, SEM_SPEC] + [HBM_SPEC] * n + [pl.BlockSpec(memory_space=pltpu.VMEM)],
        out_shape=[pltpu.SemaphoreType.DMA((len(rels) * n,))] * 2 + [pltpu.HBM(b.shape, b.dtype) for b in bufs] + [TOKEN],
        input_output_aliases={a: 2 + a for a in range(n)},
        compiler_params=pltpu.CompilerParams(has_side_effects=EFFECT),
    )(*[pltpu.with_memory_space_constraint(b, pltpu.HBM) for b in bufs])
    return (res[0], res[1]), list(res[2:2 + n]), res[-1]


def _forward_wait(bufs, sems, rels, after, name):
    n = len(bufs)

    def body(*refs):
        cps, waits = _forward_copies(refs[:n], rels, refs[n], refs[n + 1])
        for cp in cps:
            cp.wait_send()
        for cp in waits:
            cp.wait_recv()

    return pl.pallas_call(
        body, name=name,
        in_specs=[HBM_SPEC] * n + [SEM_SPEC, SEM_SPEC] + [ANY_SPEC] * len(after),
        out_specs=[HBM_SPEC] * n,
        out_shape=[pltpu.HBM(b.shape, b.dtype) for b in bufs],
        input_output_aliases={a: a for a in range(n)},
        compiler_params=pltpu.CompilerParams(has_side_effects=EFFECT),
    )(*bufs, sems[0], sems[1], *after)


def _sibling_copy(src, land, send_sems, recv_sems):
    x, y, c = _place()
    return pltpu.make_async_remote_copy(src_ref=src, dst_ref=land, send_sem=send_sems.at[0], recv_sem=recv_sems.at[0],
                                        device_id=(x, y, 1 - c), device_id_type=MESH)


def _sibling_start(part, name):
    def body(src, land, send_sems, recv_sems, src_thru, land_thru, token):
        _sibling_copy(src, land, send_sems, recv_sems).start()
        token[...] = jnp.zeros_like(token)

    land = lax.empty(part.shape, part.dtype)
    return pl.pallas_call(
        body, name=name,
        in_specs=[HBM_SPEC] * 2,
        out_specs=[SEM_SPEC, SEM_SPEC, HBM_SPEC, HBM_SPEC, pl.BlockSpec(memory_space=pltpu.VMEM)],
        out_shape=[pltpu.SemaphoreType.DMA((1,)), pltpu.SemaphoreType.DMA((1,)), pltpu.HBM(part.shape, part.dtype),
                   pltpu.HBM(part.shape, part.dtype), TOKEN],
        input_output_aliases={0: 2, 1: 3},
        compiler_params=pltpu.CompilerParams(has_side_effects=EFFECT),
    )(pltpu.with_memory_space_constraint(part, pltpu.HBM), pltpu.with_memory_space_constraint(land, pltpu.HBM))


def _sibling_wait(state, after, name):
    send_sems, recv_sems, part, land, _ = state

    def body(src, land_ref, send_ref, recv_ref, *rest):
        cp = _sibling_copy(src, land_ref, send_ref, recv_ref)
        cp.wait_send()
        cp.wait_recv()

    return pl.pallas_call(
        body, name=name,
        in_specs=[HBM_SPEC, HBM_SPEC, SEM_SPEC, SEM_SPEC] + [ANY_SPEC] * len(after),
        out_specs=[HBM_SPEC, HBM_SPEC],
        out_shape=[pltpu.HBM(part.shape, part.dtype), pltpu.HBM(land.shape, land.dtype)],
        input_output_aliases={0: 0, 1: 1},
        compiler_params=pltpu.CompilerParams(has_side_effects=EFFECT),
    )(part, land, send_sems, recv_sems, *after)[1]


def _pair_sum_rows(keep, got, name):
    nblk, rows, cols = keep.shape
    tr, tc = 512, _tile_cols(cols)

    def body(k_ref, g_ref, o_ref):
        o_ref[...] = (k_ref[...].astype(F32) + g_ref[...].astype(F32)).astype(BF16)

    blk = pl.BlockSpec((None, tr, tc), lambda b, i, j: (b, i, j))
    return pl.pallas_call(
        body, name=name, grid=(nblk, rows // tr, cols // tc),
        in_specs=[blk, blk], out_specs=blk,
        out_shape=jax.ShapeDtypeStruct(keep.shape, BF16),
        compiler_params=_cp(("parallel",) * 3),
    )(keep, got)


def _scatter_start(qs, name):
    n = len(qs)

    def body(*refs):
        ins, lands = refs[:n], refs[n:2 * n]
        token = refs[-1]
        for cp in _scatter_copies(ins, lands, refs[2 * n], refs[2 * n + 1]):
            cp.start()
        token[...] = jnp.zeros_like(token)

    lands = [lax.empty((3,) + q.shape[1:], q.dtype) for q in qs]
    res = pl.pallas_call(
        body, name=name,
        in_specs=[HBM_SPEC] * (2 * n),
        out_specs=[SEM_SPEC, SEM_SPEC] + [HBM_SPEC] * (2 * n) + [pl.BlockSpec(memory_space=pltpu.VMEM)],
        out_shape=[pltpu.SemaphoreType.DMA((3 * n,)), pltpu.SemaphoreType.DMA((3 * n,))]
        + [pltpu.HBM(b.shape, b.dtype) for b in qs + lands] + [TOKEN],
        input_output_aliases={a: 2 + a for a in range(2 * n)},
        compiler_params=pltpu.CompilerParams(has_side_effects=EFFECT),
    )(*[pltpu.with_memory_space_constraint(b, pltpu.HBM) for b in qs + lands])
    return res[0], res[1], list(res[2:2 + n]), list(res[2 + n:2 + 2 * n]), res[-1]


def _scatter_copies(ins, lands, send_sems, recv_sems):
    x, y, c = _place()
    cps = []
    for a in range(len(ins)):
        for j, (px, py) in enumerate(_other_chips(x, y)):
            cps.append(pltpu.make_async_remote_copy(
                src_ref=ins[a].at[2 * px + py], dst_ref=lands[a].at[j],
                send_sem=send_sems.at[3 * a + j], recv_sem=recv_sems.at[3 * a + j], device_id=(px, py, c), device_id_type=MESH))
    return cps


def _scatter_wait(qs, lands, send_sems, recv_sems, after, name):
    n = len(qs)

    def body(*refs):
        for cp in _scatter_copies(refs[:n], refs[n:2 * n], refs[2 * n], refs[2 * n + 1]):
            cp.wait_send()
            cp.wait_recv()

    res = pl.pallas_call(
        body, name=name,
        in_specs=[HBM_SPEC] * (2 * n) + [SEM_SPEC, SEM_SPEC, ANY_SPEC],
        out_specs=[HBM_SPEC] * (2 * n),
        out_shape=[pltpu.HBM(b.shape, b.dtype) for b in qs + lands],
        input_output_aliases={a: a for a in range(2 * n)},
        compiler_params=pltpu.CompilerParams(has_side_effects=EFFECT),
    )(*qs, *lands, send_sems, recv_sems, after)
    return list(res[:n]), list(res[n:])


def _chip_sum(q, got, name):
    _, hr, cols = got.shape
    tr, tc = 512, _tile_cols(cols)
    chip = 2 * lax.axis_index("x") + lax.axis_index("y")
    c = lax.axis_index("c")

    def body(idx_ref, q_ref, g_ref, o_ref):
        del idx_ref
        acc = q_ref[...].astype(F32)
        for i in range(3):
            acc = acc + g_ref[i].astype(F32)
        o_ref[...] = acc

    grid_spec = pltpu.PrefetchScalarGridSpec(
        num_scalar_prefetch=1, grid=(hr // tr, cols // tc),
        in_specs=[pl.BlockSpec((None, tr, tc), lambda i, j, ix: (ix[0], i, j)),
                  pl.BlockSpec((3, tr, tc), lambda i, j, ix: (0, i, j))],
        out_specs=pl.BlockSpec((None, tr, tc), lambda i, j, ix: (ix[1], i, j)))
    return pl.pallas_call(
        body, name=name, grid_spec=grid_spec,
        out_shape=jax.ShapeDtypeStruct((2, hr, cols), F32),
        compiler_params=_cp(("parallel", "parallel")),
    )(jnp.stack([chip, c]).astype(jnp.int32), q, got)


def _join_halves(bufs, name):
    n = len(bufs)

    def body(*refs):
        outs = refs[n:2 * n]
        send_sems, recv_sems = refs[2 * n:]
        x, y, c = _place()
        cps = []
        for a in range(n):
            cps.append(pltpu.make_async_remote_copy(src_ref=outs[a].at[c], dst_ref=outs[a].at[c], send_sem=send_sems.at[a],
                                                    recv_sem=recv_sems.at[a], device_id=(x, y, 1 - c), device_id_type=MESH))
        for cp in cps:
            cp.start()
        for a in range(n):
            theirs = outs[a].at[1 - c]
            pltpu.make_async_remote_copy(src_ref=theirs, dst_ref=theirs, send_sem=send_sems.at[a], recv_sem=recv_sems.at[a],
                                         device_id=(x, y, 1 - c), device_id_type=MESH).wait_recv()
        for cp in cps:
            cp.wait_send()

    hbm = pl.BlockSpec(memory_space=pl.ANY)
    return pl.pallas_call(
        body, name=name,
        in_specs=[hbm] * n, out_specs=[hbm] * n,
        out_shape=[jax.ShapeDtypeStruct(b.shape, b.dtype) for b in bufs],
        input_output_aliases={a: a for a in range(n)},
        scratch_shapes=[pltpu.SemaphoreType.DMA((n,)), pltpu.SemaphoreType.DMA((n,))],
    )(*bufs)


def _join_copy(buf, send_sems, recv_sems):
    x, y, c = _place()
    sems = dict(send_sem=send_sems.at[0], recv_sem=recv_sems.at[0], device_id=(x, y, 1 - c), device_id_type=MESH)
    return (pltpu.make_async_remote_copy(src_ref=buf.at[c], dst_ref=buf.at[c], **sems),
            pltpu.make_async_remote_copy(src_ref=buf.at[1 - c], dst_ref=buf.at[1 - c], **sems))


def _join_start(buf, name):
    def body(b_ref, send_sems, recv_sems, thru, token):
        _join_copy(b_ref, send_sems, recv_sems)[0].start()
        token[...] = jnp.zeros_like(token)

    return pl.pallas_call(
        body, name=name,
        in_specs=[HBM_SPEC],
        out_specs=[SEM_SPEC, SEM_SPEC, HBM_SPEC, pl.BlockSpec(memory_space=pltpu.VMEM)],
        out_shape=[pltpu.SemaphoreType.DMA((1,)), pltpu.SemaphoreType.DMA((1,)), pltpu.HBM(buf.shape, buf.dtype), TOKEN],
        input_output_aliases={0: 2},
        compiler_params=pltpu.CompilerParams(has_side_effects=EFFECT),
    )(pltpu.with_memory_space_constraint(buf, pltpu.HBM))


def _join_wait(state, after, name):
    send_sems, recv_sems, buf, _ = state

    def body(b_ref, send_ref, recv_ref, *rest):
        out, inc = _join_copy(b_ref, send_ref, recv_ref)
        out.wait_send()
        inc.wait_recv()

    return pl.pallas_call(
        body, name=name,
        in_specs=[HBM_SPEC, SEM_SPEC, SEM_SPEC] + [ANY_SPEC] * len(after),
        out_specs=HBM_SPEC,
        out_shape=pltpu.HBM(buf.shape, buf.dtype),
        input_output_aliases={0: 0},
        compiler_params=pltpu.CompilerParams(has_side_effects=EFFECT),
    )(buf, send_sems, recv_sems, *after)


N_DEV = 8


def _gather_small(block, reduce, name):
    m_per, cols = block.shape

    def body(x_ref, out_ref, all_ref, send_sems, recv_sems, local_sem):
        x, y, c = _place()
        me, sibling = (x, y, c), (x, y, 1 - c)
        chips = _other_chips(x, y)

        def rows(px, py, pc):
            return all_ref.at[pl.ds((4 * px + 2 * py + pc) * m_per, m_per), :]

        def copy(k, block_of, to, src=None):
            return pltpu.make_async_remote_copy(
                src_ref=rows(*block_of) if src is None else src, dst_ref=rows(*block_of),
                send_sem=send_sems.at[k], recv_sem=recv_sems.at[k], device_id=to, device_id_type=MESH)

        mine = pltpu.make_async_copy(x_ref, rows(*me), local_sem)
        mine.start()
        first = [copy(0, me, sibling, src=x_ref)]
        first += [copy(1 + j, me, (*chip, c), src=x_ref) for j, chip in enumerate(chips)]
        for cp in first:
            cp.start()
        passed = [copy(4 + j, (*chip, c), sibling) for j, chip in enumerate(chips)]
        for j, chip in enumerate(chips):
            copy(1 + j, (*chip, c), me).wait_recv()
            passed[j].start()
        copy(0, sibling, me).wait_recv()
        for j, chip in enumerate(chips):
            copy(4 + j, (*chip, 1 - c), me).wait_recv()
        for cp in first + passed:
            cp.wait_send()
        mine.wait()
        if reduce:
            acc = all_ref[pl.ds(0, m_per), :]
            for i in range(1, N_DEV):
                acc = acc + all_ref[pl.ds(i * m_per, m_per), :]
            out_ref[...] = acc
        else:
            out_ref[...] = all_ref[...]

    out_rows = m_per if reduce else N_DEV * m_per
    return pl.pallas_call(
        body, name=name,
        in_specs=[pl.BlockSpec(memory_space=pltpu.VMEM)],
        out_specs=pl.BlockSpec(memory_space=pltpu.VMEM),
        out_shape=jax.ShapeDtypeStruct((out_rows, cols), F32),
        scratch_shapes=[pltpu.VMEM((N_DEV * m_per, cols), F32), pltpu.SemaphoreType.DMA((7,)),
                        pltpu.SemaphoreType.DMA((7,)), pltpu.SemaphoreType.DMA],
    )(block)


class _Comm:
    def __init__(self, w_in, w_kv, w_out, chip):
        self.bufs = [_cast_own(w_in, chip, "cast_w_in"), _cast_own(w_kv, chip, "cast_w_mem_kv"),
                     _cast_own(w_out, chip, "cast_w_out")]


    def gather_started(self):
        (self.sems,), (self.b_in,), tok = _gather_start(self.bufs[:1], (NEAR,), "gather_start_in_near")
        return [tok]

    def w_in_own(self):
        return self.b_in.reshape(NCHIP * D, WB)

    def w_in_near(self, after):
        (b_in,) = _gather_wait([self.b_in], self.sems, NEAR, list(after) + self.bufs[1:], "gather_wait_in_near")
        (b_in,) = _forward_halves([b_in], NEAR, "forward_in_near")
        (self.sems,), (self.b_in,), tok = _gather_start([b_in], (FAR,), "gather_start_in_far")
        return self.b_in.reshape(NCHIP * D, WB), tok

    def w_in_all(self, after):
        (b_in,) = _gather_wait([self.b_in], self.sems, FAR, after, "gather_wait_in_far")
        (b_in,) = _forward_halves([b_in], FAR, "forward_in_far")
        (self.sems,), self.b_rest, tok = _gather_start(self.bufs[1:], (NEAR + FAR,), "gather_start_rest")
        return b_in.reshape(NCHIP * D, WB), tok

    def w_rest_landed(self, after):
        b_rest = _gather_wait(self.b_rest, self.sems, NEAR + FAR, after, "gather_wait_rest")
        self.sems, self.b_rest, tok = _forward_start(b_rest, NEAR + FAR, "forward_start_rest")
        return tok

    def w_rest(self, after):
        b_kv, b_out = _forward_wait(self.b_rest, self.sems, NEAR + FAR, after, "forward_wait_rest")
        return b_kv.reshape(D, 2 * XW), b_out.reshape(D, D)

    def sibling_start(self, send, tag):
        return _sibling_start(send, "sibling_start_" + tag)

    def reduce_start_summed(self, keep, sent, tag, after=None):
        got = _sibling_wait(sent, [keep] + ([] if after is None else [after]), "sibling_wait_" + tag)
        return _scatter_start([_pair_sum_rows(keep, got, "pair_sum_" + tag)], "scatter_start_" + tag)

    def reduce_finish_start(self, state, tag, after):
        send_sems, recv_sems, qs, lands, _ = state
        qs, lands = _scatter_wait(qs, lands, send_sems, recv_sems, after, "scatter_wait_" + tag)
        return _join_start(_chip_sum(qs[0], lands[0], f"chip_sum_{tag}0"), "join_start_" + tag)

    def reduce_finish_wait(self, pending, tag, after):
        j = _join_wait(pending, after, "join_wait_" + tag)
        return j.reshape(2 * j.shape[1], j.shape[2])

    def reduce_finish(self, state, tag, after):
        send_sems, recv_sems, qs, lands, _ = state
        qs, lands = _scatter_wait(qs, lands, send_sems, recv_sems, after, "scatter_wait_" + tag)
        halves = [_chip_sum(q, l, f"chip_sum_{tag}{i}") for i, (q, l) in enumerate(zip(qs, lands))]
        return [j.reshape(2 * j.shape[1], j.shape[2]) for j in _join_halves(halves, "join_halves_" + tag)]


def kernel(x, mem, pre_norm, w_in, conv_w, mem_norm, w_mem_kv, w_out, post_norm, loss_target, m_pre_norm, m_w_in, m_conv_w, m_mem_norm, m_w_mem_kv, m_w_out, m_post_norm, v_pre_norm, v_w_in, v_conv_w, v_mem_norm, v_w_mem_kv, v_w_out, v_post_norm):
    chip = 2 * lax.axis_index("x") + lax.axis_index("y")

    cw_blk = jnp.zeros((8, 384), F32).at[:3].set(conv_w[0])
    cw_all = _gather_small(cw_blk, False, "gather_conv_w").reshape(NCHIP, 2, 8, 384)[:, 0, :3]
    conv_full = jnp.transpose(cw_all, (1, 0, 2)).reshape(3, CW)

    comm = _Comm(w_in[0], w_mem_kv[0], w_out[0], chip)
    loss, grad_x, d_pre, d_mem, d_post, d_conv, r_in, r_kv, r_out = _local_step(
        x, mem, pre_norm, conv_full, mem_norm, post_norm, loss_target, chip, lax.axis_index("c"), comm)

    pack = jnp.concatenate([d_pre, d_mem, d_post, jnp.pad(d_conv, ((0, 0), (0, D - CW))),
                            jnp.pad(loss, ((0, 0), (0, D - 128))), jnp.zeros((1, D), F32)], axis=0)
    tot = _gather_small(pack, True, "reduce_small")
    g_pre, g_mem, g_post = tot[0:1], tot[1:2], tot[2:3]
    g_conv = lax.dynamic_slice(tot[3:6, :CW], (0, chip * 384), (3, 384))
    loss_out = tot[6, 0]

    names = ("pre_norm", "w_in", "conv_w", "mem_norm", "w_mem_kv", "w_out", "post_norm")
    ws = (pre_norm, w_in[0], conv_w[0], mem_norm, w_mem_kv[0], w_out[0], post_norm)
    gs = [g_pre, None, g_conv, g_mem, r_kv, r_out, g_post]
    ms = (m_pre_norm, m_w_in[0], m_conv_w[0], m_mem_norm, m_w_mem_kv[0], m_w_out[0], m_post_norm)
    vs = (v_pre_norm, v_w_in[0], v_conv_w[0], v_mem_norm, v_w_mem_kv[0], v_w_out[0], v_post_norm)
    upd = [None if g is None else _adamw(w, g, m, v, "adamw_" + nm) for nm, w, g, m, v in zip(names, ws, gs, ms, vs)]
    gs[1] = comm.reduce_finish_wait(r_in, "b", after=[u[1] for u in upd if u is not None])
    upd[1] = _adamw(ws[1], gs[1], ms[1], vs[1], "adamw_w_in")

    def shaped(arrs):
        return [a.reshape(w.shape) if w.ndim == a.ndim else a.reshape((1,) + a.shape)
                for a, w in zip(arrs, (pre_norm, w_in, conv_w, mem_norm, w_mem_kv, w_out, post_norm))]

    grads = shaped([u[0] for u in upd])
    deltas = shaped([u[1] for u in upd])
    new_m = shaped([u[2] for u in upd])
    new_v = shaped([u[3] for u in upd])
    return (loss_out, grad_x, *grads, *deltas, *new_m, *new_v)
```

```python
import jax
import jax.numpy as jnp
from jax import lax
from jax.experimental import pallas as pl
from jax.experimental.pallas import tpu as pltpu

F32 = jnp.float32
BF16 = jnp.bfloat16

D = 4096
S = 2048
NB = 2
T = NB * S
MLEN = 256
HD = 128
AW = 1536
CW = 1536
XW = 1024
XHD = 256
NXH = 4
NC = 14336
QA, KA, VA, ZA, UC, BC, CC, ZC, QX, ZX = 0, 1536, 3072, 4608, 6144, 7680, 9216, 10752, 12288, 13312
NCHIP = 4
WB = NC // NCHIP
DIL = (1, 4, 16)
HPG = 4
EPS = 1e-6
NEG = -1e30
ROPE_THETA = 10000.0
A_SCALE = HD ** -0.5
X_SCALE = XHD ** -0.5

ADAM_LR = 0.001
ADAM_B1 = 0.9
ADAM_B2 = 0.999
ADAM_EPS = 1e-08
ADAM_WD = 0.01
ADAM_STEP = 10

MESH = pl.DeviceIdType.MESH
MIB = 1024 * 1024


def _cp(sem, vmem_mib=48):
    return pltpu.CompilerParams(dimension_semantics=sem, vmem_limit_bytes=vmem_mib * MIB)


def _sigmoid(z):
    return 1.0 / (1.0 + jnp.exp(-z))


def _rope(x, cos, sin, half):
    return x * cos + pltpu.roll(x, half, 1) * sin


def _rope_t(g, cos, sin, half):
    return g * cos + pltpu.roll(g * sin, half, 1)


def _rms_fwd(x2, g, name, dep=None):
    rows = x2.shape[0]
    tr = 256
    deps = [] if dep is None else list(dep)

    def body(x_ref, g_ref, *rest):
        o_ref = rest[-1]
        x = x_ref[...]
        r = lax.rsqrt(jnp.mean(x * x, axis=-1, keepdims=True) + EPS)
        o_ref[...] = (x * r * g_ref[...]).astype(BF16)

    return pl.pallas_call(
        body, name=name, grid=(rows // tr,),
        in_specs=[pl.BlockSpec((tr, D), lambda i: (i, 0)), pl.BlockSpec((1, D), lambda i: (0, 0))]
        + [pl.BlockSpec(memory_space=pl.ANY)] * len(deps),
        out_specs=pl.BlockSpec((tr, D), lambda i: (i, 0)),
        out_shape=jax.ShapeDtypeStruct((rows, D), BF16),
        compiler_params=_cp(("parallel",)),
    )(x2, g, *deps)


def _norm_gain_grad(dn, x2, name):
    rows = x2.shape[0]
    tr = 256

    def body(dn_ref, x_ref, dg_ref):
        @pl.when(pl.program_id(0) == 0)
        def _():
            dg_ref[...] = jnp.zeros_like(dg_ref)
        x = x_ref[...]
        r = lax.rsqrt(jnp.mean(x * x, axis=-1, keepdims=True) + EPS)
        dg_ref[...] += jnp.sum(dn_ref[...] * (x * r), axis=0, keepdims=True)

    return pl.pallas_call(
        body, name=name, grid=(rows // tr,),
        in_specs=[pl.BlockSpec((tr, D), lambda i: (i, 0)), pl.BlockSpec((tr, D), lambda i: (i, 0))],
        out_specs=pl.BlockSpec((1, D), lambda i: (0, 0)),
        out_shape=jax.ShapeDtypeStruct((1, D), F32),
        compiler_params=_cp(("arbitrary",)),
    )(dn, x2)


def _pre_norm_bwd(dh, x2, g, dout):
    tr = 256

    def body(dh_ref, x_ref, g_ref, dout_ref, gx_ref, dg_ref):
        @pl.when(pl.program_id(0) == 0)
        def _():
            dg_ref[...] = jnp.zeros_like(dg_ref)
        x = x_ref[...]
        dh_ = dh_ref[...].astype(F32)
        r = lax.rsqrt(jnp.mean(x * x, axis=-1, keepdims=True) + EPS)
        xhat = x * r
        dg_ref[...] += jnp.sum(dh_ * xhat, axis=0, keepdims=True)
        dxn = dh_ * g_ref[...]
        gx_ref[...] = dout_ref[...].astype(F32) + r * (dxn - xhat * jnp.mean(dxn * xhat, axis=-1, keepdims=True))

    row = pl.BlockSpec((tr, D), lambda i: (i, 0))
    vec = pl.BlockSpec((1, D), lambda i: (0, 0))
    return pl.pallas_call(
        body, name="pre_norm_bwd", grid=(T // tr,),
        in_specs=[row, row, vec, row],
        out_specs=[row, vec],
        out_shape=[jax.ShapeDtypeStruct((T, D), F32), jax.ShapeDtypeStruct((1, D), F32)],
        compiler_params=_cp(("arbitrary",)),
    )(dh, x2, g, dout)


def _post_norm_loss(y, x2, tgt, g):
    tr = 256

    def body(y_ref, x_ref, t_ref, g_ref, dy_ref, dout_ref, dg_ref, loss_ref):
        @pl.when(pl.program_id(0) == 0)
        def _():
            dg_ref[...] = jnp.zeros_like(dg_ref)
            loss_ref[...] = jnp.zeros_like(loss_ref)
        yv = y_ref[...]
        gv = g_ref[...]
        r = lax.rsqrt(jnp.mean(yv * yv, axis=-1, keepdims=True) + EPS)
        yhat = yv * r
        err = x_ref[...] + yhat * gv - t_ref[...]
        loss_ref[...] += jnp.sum(jnp.sum(err * err, axis=1, keepdims=True), axis=0, keepdims=True) * (0.5 / D)
        dout = err * (1.0 / D)
        dout_ref[...] = dout.astype(BF16)
        dg_ref[...] += jnp.sum(dout * yhat, axis=0, keepdims=True)
        dyn = dout * gv
        dy_ref[...] = (r * (dyn - yhat * jnp.mean(dyn * yhat, axis=-1, keepdims=True))).astype(BF16)

    row = pl.BlockSpec((tr, D), lambda i: (i, 0))
    vec = pl.BlockSpec((1, D), lambda i: (0, 0))
    return pl.pallas_call(
        body, name="post_norm_loss", grid=(T // tr,),
        in_specs=[row, row, row, vec],
        out_specs=[row, row, vec, pl.BlockSpec((1, 128), lambda i: (0, 0))],
        out_shape=[jax.ShapeDtypeStruct((T, D), BF16), jax.ShapeDtypeStruct((T, D), BF16),
                   jax.ShapeDtypeStruct((1, D), F32), jax.ShapeDtypeStruct((1, 128), F32)],
        compiler_params=_cp(("arbitrary",)),
    )(y, x2, tgt, g)


NN = (((1,), (0,)), ((), ()))
NT = (((1,), (1,)), ((), ()))
TN = (((0,), (0,)), ((), ()))


def _as_index(v):
    return jnp.reshape(v, (1,)).astype(jnp.int32)


def _matmul(a, b, *, name, dims, grid, a_block, a_map, b_block, b_map, o_block, o_map, out_shape, out_dtype=F32,
            index=None, prev=None, deps=()):
    extra = ([] if prev is None else [prev]) + [d for d in deps if d is not None]
    first = 0 if index is None else 1
    nk = grid[2]
    in_place = out_dtype == F32

    def body(*refs):
        a_ref, b_ref, o_ref = refs[first], refs[first + 1], refs[first + 2 + len(extra)]
        acc_ref = o_ref if in_place else refs[-1]

        @pl.when(pl.program_id(2) == 0)
        def _():
            acc_ref[...] = lax.dot_general(a_ref[...], b_ref[...], dims, preferred_element_type=F32)

        @pl.when(pl.program_id(2) > 0)
        def _():
            acc_ref[...] += lax.dot_general(a_ref[...], b_ref[...], dims, preferred_element_type=F32)

        if not in_place:
            @pl.when(pl.program_id(2) == nk - 1)
            def _():
                o_ref[...] = acc_ref[...].astype(o_ref.dtype)

    in_specs = [pl.BlockSpec(a_block, a_map), pl.BlockSpec(b_block, b_map)] + [pl.BlockSpec(memory_space=pl.ANY)] * len(extra)
    out_specs = pl.BlockSpec(o_block, o_map)
    scratch = [] if in_place else [pltpu.VMEM(o_block, F32)]
    kwargs = dict(name=name, out_shape=jax.ShapeDtypeStruct(out_shape, out_dtype),
                  input_output_aliases={} if prev is None else {first + 2: 0},
                  compiler_params=_cp(("parallel", "parallel", "arbitrary"), vmem_mib=56))
    if index is None:
        return pl.pallas_call(body, grid=grid, in_specs=in_specs, out_specs=out_specs, scratch_shapes=scratch,
                              **kwargs)(a, b, *extra)
    grid_spec = pltpu.PrefetchScalarGridSpec(num_scalar_prefetch=1, grid=grid, in_specs=in_specs, out_specs=out_specs,
                                             scratch_shapes=scratch)
    return pl.pallas_call(body, grid_spec=grid_spec, **kwargs)(_as_index(index), a, b, *extra)


def _mm_nn(a, b, name, tm, tn, tk):
    m, kd = a.shape
    n = b.shape[1]
    return _matmul(a, b, name=name, dims=NN, grid=(m // tm, n // tn, kd // tk),
                   a_block=(tm, tk), a_map=lambda i, j, k: (i, k),
                   b_block=(tk, tn), b_map=lambda i, j, k: (k, j),
                   o_block=(tm, tn), o_map=lambda i, j, k: (i, j), out_shape=(m, n))


def _mm_nt(a, b, name, tm, tn, tk):
    m, kd = a.shape
    n = b.shape[0]
    return _matmul(a, b, name=name, dims=NT, grid=(m // tm, n // tn, kd // tk),
                   a_block=(tm, tk), a_map=lambda i, j, k: (i, k),
                   b_block=(tn, tk), b_map=lambda i, j, k: (j, k),
                   o_block=(tm, tn), o_map=lambda i, j, k: (i, j), out_shape=(m, n))


W_TN = 1792
W_NJ = WB // W_TN


def _proj_part(h, wg, chip, masks, name, prev=None, dep=None):
    tm, tk = 1024, 2048

    def blk(j, ix):
        m = masks[0]
        for t in range(1, len(masks)):
            m = jnp.where(j // W_NJ == t, masks[t], m)
        return jnp.bitwise_xor(ix[0], m)

    return _matmul(h, wg, name=name, dims=NN, grid=(T // tm, len(masks) * W_NJ, D // tk), index=chip, prev=prev, deps=(dep,),
                   a_block=(tm, tk), a_map=lambda i, j, k, ix: (i, k),
                   b_block=(tk, W_TN), b_map=lambda i, j, k, ix: (blk(j, ix) * (D // tk) + k, j % W_NJ),
                   o_block=(tm, W_TN), o_map=lambda i, j, k, ix: (i, blk(j, ix) * W_NJ + j % W_NJ), out_shape=(T, NC))


def _dh(dproj, wg, dep=None):
    tm, tn = 1024, 1024
    return _matmul(dproj, wg, name="dh", dims=NT, grid=(T // tm, D // tn, NC // W_TN), deps=(dep,),
                   a_block=(tm, W_TN), a_map=lambda i, j, k: (i, k),
                   b_block=(tn, W_TN), b_map=lambda i, j, k: ((k // W_NJ) * (D // tn) + j, k % W_NJ),
                   o_block=(tm, tn), o_map=lambda i, j, k: (i, j), out_shape=(T, D), out_dtype=BF16)


def _grad_rows(a, b, half, out_dtype, name, dep=None):
    kd, n = b.shape
    tm, tn, tk = D // NCHIP // 2, min(n, 2048), min(kd, 2048)
    return _matmul(a, b, name=name, dims=TN, grid=(NCHIP, n // tn, kd // tk), index=half, deps=(dep,),
                   a_block=(tk, tm), a_map=lambda i, j, k, ix: (k, 2 * i + ix[0]),
                   b_block=(tk, tn), b_map=lambda i, j, k, ix: (k, j),
                   o_block=(tm, tn), o_map=lambda i, j, k, ix: (i, j),
                   out_shape=(NCHIP * tm, n), out_dtype=out_dtype)


def _grad_w_in(h, dproj, half, out_dtype, name, dep=None):
    tm, tk = 1024, 2048
    nh = D // 2 // tm
    return _matmul(h, dproj, name=name, dims=TN, grid=(nh, NC // W_TN, T // tk), index=half, deps=(dep,),
                   a_block=(tk, tm), a_map=lambda i, j, k, ix: (k, ix[0] * nh + i),
                   b_block=(tk, W_TN), b_map=lambda i, j, k, ix: (k, j),
                   o_block=(tm, W_TN), o_map=lambda i, j, k, ix: ((j // W_NJ) * nh + i, j % W_NJ),
                   out_shape=(NCHIP * D // 2, WB), out_dtype=out_dtype)


def _rope_tables(pos, half):
    inv = 1.0 / (ROPE_THETA ** (jnp.arange(half, dtype=F32) / half))
    ang = pos.astype(F32)[:, None] * inv[None, :]
    cos, sin = jnp.cos(ang), jnp.sin(ang)
    return jnp.concatenate([cos, cos], axis=1), jnp.concatenate([-sin, sin], axis=1)


def _band_mask(r0):
    qi = lax.broadcasted_iota(jnp.int32, (128, 256), 0)
    kk = lax.broadcasted_iota(jnp.int32, (128, 256), 1)
    return (kk >= qi) & (kk <= qi + 128) & (kk + r0 >= 128)


def _window(r0, nblk):
    if nblk == 1:
        qi = lax.broadcasted_iota(jnp.int32, (128, 128), 0)
        kk = lax.broadcasted_iota(jnp.int32, (128, 128), 1)
        return pl.ds(128, 128), kk <= qi
    return pl.ds(r0, 256), _band_mask(r0)


def _dil_rows(r, n, d):
    if d == 1:
        return pl.ds(pl.multiple_of(n * 128, 128), 128)
    return pl.ds(r + d * 128 * n, 128, stride=d)


def _attn_fwd(proj, cosf, sinf, g):
    d = DIL[g]
    ln = S // d
    nblk = ln // 128
    proj_v = proj.reshape(NB, S, NC)

    def body(q_ref, k_ref, v_ref, cos_ref, sin_ref, o_ref, l_ref, k_s, v_s):
        k_s[:, pl.ds(0, 128), :] = jnp.zeros((d, 128, HD), BF16)
        v_s[:, pl.ds(0, 128), :] = jnp.zeros((d, 128, HD), BF16)

        def prep(i, carry):
            r, n = i // nblk, i % nblk
            rows = _dil_rows(r, n, d)
            dst = pl.ds(pl.multiple_of(n * 128 + 128, 128), 128)
            k_s[r, dst, :] = _rope(k_ref[rows, :], cos_ref[rows, :], sin_ref[rows, :], HD // 2).astype(BF16)
            v_s[r, dst, :] = v_ref[rows, :].astype(BF16)
            return carry

        lax.fori_loop(0, d * nblk, prep, 0, unroll=4)

        def step(i, carry):
            r, n = i // nblk, i % nblk
            rows = _dil_rows(r, n, d)
            r0 = pl.multiple_of(n * 128, 128)
            qr = _rope(q_ref[rows, :], cos_ref[rows, :], sin_ref[rows, :], HD // 2).astype(BF16)
            win, mask = _window(r0, nblk)
            kw = k_s[r, win, :]
            vw = v_s[r, win, :]
            sc = lax.dot_general(qr, kw, NT, preferred_element_type=F32) * A_SCALE
            sc = jnp.where(mask, sc, NEG)
            m = jnp.max(sc, axis=1, keepdims=True)
            p = jnp.exp(sc - m)
            l = jnp.sum(p, axis=1, keepdims=True)
            o_ref[rows, :] = jnp.dot(p.astype(BF16), vw, preferred_element_type=F32) / l
            l_ref[rows, :] = jnp.broadcast_to(m + jnp.log(l), (128, HD))
            return carry

        lax.fori_loop(0, d * nblk, step, 0, unroll=4)

    def col(off):
        return lambda b, h: (b, 0, off // HD + HPG * g + h)

    blk = (None, S, HD)
    tab = pl.BlockSpec((S, HD), lambda b, h: (0, 0))
    out = pl.BlockSpec(blk, lambda b, h: (b, 0, h))
    kv = pl.BlockSpec((None, None, d, ln + 128, HD), lambda b, h: (b, h, 0, 0, 0))
    o, l, kr, vr = pl.pallas_call(
        body, name=f"attn_fwd_d{d}", grid=(NB, HPG),
        in_specs=[pl.BlockSpec(blk, col(QA)), pl.BlockSpec(blk, col(KA)), pl.BlockSpec(blk, col(VA)), tab, tab],
        out_specs=[out, out, kv, kv],
        out_shape=[jax.ShapeDtypeStruct((NB, S, HPG * HD), F32)] * 2
        + [jax.ShapeDtypeStruct((NB, HPG, d, ln + 128, HD), BF16)] * 2,
        compiler_params=_cp(("parallel", "parallel")),
    )(proj_v, proj_v, proj_v, cosf, sinf)
    return o.reshape(T, HPG * HD), l.reshape(T, HPG * HD), kr, vr


def _attn_bwd(proj, kr, vr, cosf, sinf, da, lse, delta, dproj, g):
    d = DIL[g]
    ln = S // d
    nblk = ln // 128
    proj_v = proj.reshape(NB, S, NC)
    dproj_v = dproj.reshape(NB, S, NC)
    da_v = da.reshape(NB, S, AW)
    lse_v = lse.reshape(NB, S, HPG * HD)
    delta_v = delta.reshape(NB, S, HPG * HD)

    def body(q_ref, k_s, v_s, cos_ref, sin_ref, da_ref, lse_ref, dl_ref, dp_in_ref, o_ref, stg, dk_s, dv_s):
        del dp_in_ref
        w = pl.program_id(2)

        def emit():
            def cast(n, carry):
                rows = pl.ds(pl.multiple_of(n * 256, 256), 256)
                o_ref[rows, :] = stg[rows, :].astype(BF16)
                return carry

            lax.fori_loop(0, S // 256, cast, 0)

        @pl.when(w == 0)
        def _():
            dk_s[...] = jnp.zeros_like(dk_s)
            dv_s[...] = jnp.zeros_like(dv_s)

            def step(i, carry):
                r, n = i // nblk, i % nblk
                rows = _dil_rows(r, n, d)
                r0 = pl.multiple_of(n * 128, 128)
                win, mask = _window(r0, nblk)
                cos, sin = cos_ref[rows, :], sin_ref[rows, :]
                qr = _rope(q_ref[rows, :], cos, sin, HD // 2).astype(BF16)
                kw = k_s[r, win, :]
                vw = v_s[r, win, :]
                sc = lax.dot_general(qr, kw, NT, preferred_element_type=F32) * A_SCALE
                sc = jnp.where(mask, sc, NEG)
                p = jnp.exp(sc - lse_ref[rows, :][:, :1])
                da_b = da_ref[rows, :].astype(BF16)
                dp = lax.dot_general(da_b, vw, NT, preferred_element_type=F32)
                ds_b = (p * (dp - dl_ref[rows, :][:, :1]) * A_SCALE).astype(BF16)
                p_b = p.astype(BF16)
                dq = jnp.dot(ds_b, kw, preferred_element_type=F32)
                stg[rows, :] = _rope_t(dq, cos, sin, HD // 2)
                dk_s[r, win, :] += lax.dot_general(ds_b, qr, TN, preferred_element_type=F32)
                dv_s[r, win, :] += lax.dot_general(p_b, da_b, TN, preferred_element_type=F32)
                return carry

            lax.fori_loop(0, d * nblk, step, 0, unroll=4)
            emit()

        @pl.when(w == 1)
        def _():
            def put(i, carry):
                r, n = i // nblk, i % nblk
                rows = _dil_rows(r, n, d)
                src = pl.ds(pl.multiple_of(n * 128 + 128, 128), 128)
                stg[rows, :] = _rope_t(dk_s[r, src, :], cos_ref[rows, :], sin_ref[rows, :], HD // 2)
                return carry

            lax.fori_loop(0, d * nblk, put, 0, unroll=4)
            emit()

        @pl.when(w == 2)
        def _():
            def put(i, carry):
                r, n = i // nblk, i % nblk
                src = pl.ds(pl.multiple_of(n * 128 + 128, 128), 128)
                stg[_dil_rows(r, n, d), :] = dv_s[r, src, :]
                return carry

            lax.fori_loop(0, d * nblk, put, 0, unroll=4)
            emit()

    def col(off):
        return lambda b, h, w: (ahead(b, h, w)[0], 0, off // HD + HPG * g + ahead(b, h, w)[1])

    def ahead(b, h, w):
        flat = jnp.minimum(b * HPG + h + jnp.where(w > 0, 1, 0), NB * HPG - 1)
        return flat // HPG, flat % HPG

    blk = (None, S, HD)
    tab = pl.BlockSpec((S, HD), lambda b, h, w: (0, 0))
    per_head = pl.BlockSpec(blk, lambda b, h, w: (ahead(b, h, w)[0], 0, ahead(b, h, w)[1]))
    kv = pl.BlockSpec((None, None, d, ln + 128, HD), lambda b, h, w: (*ahead(b, h, w), 0, 0, 0))
    out = pl.pallas_call(
        body, name=f"attn_bwd_d{d}", grid=(NB, HPG, 3),
        in_specs=[pl.BlockSpec(blk, col(QA)), kv, kv, tab, tab,
                  pl.BlockSpec(blk, col(0)), per_head, per_head, pl.BlockSpec(memory_space=pl.ANY)],
        out_specs=pl.BlockSpec(blk, lambda b, h, w: (b, 0, (AW // HD) * w + HPG * g + h)),
        out_shape=jax.ShapeDtypeStruct(dproj_v.shape, BF16),
        input_output_aliases={8: 0},
        scratch_shapes=[pltpu.VMEM((S, HD), F32), pltpu.VMEM((d, ln + 128, HD), F32), pltpu.VMEM((d, ln + 128, HD), F32)],
        compiler_params=_cp(("arbitrary",) * 3),
    )(proj_v, kr, vr, cosf, sinf, da_v, lse_v, delta_v, dproj_v)
    return out.reshape(T, NC)


def _attn_mix(proj, os_, ls_, dep=None):
    tr = 256
    gw = HPG * HD
    deps = [] if dep is None else [dep]

    def body(o0, o1, o2, l0, l1, l2, z_ref, *rest):
        cat_ref = rest[-1]
        m = jnp.maximum(jnp.maximum(l0[...], l1[...]), l2[...])
        e = [jnp.exp(l[...] - m) for l in (l0, l1, l2)]
        inv = 1.0 / (e[0] + e[1] + e[2])
        for gi, o in enumerate((o0, o1, o2)):
            z = z_ref[:, gi * gw:(gi + 1) * gw]
            cat_ref[:, gi * gw:(gi + 1) * gw] = (o[...] * (e[gi] * inv) * (z * _sigmoid(z))).astype(BF16)

    grp = pl.BlockSpec((tr, gw), lambda i: (i, 0))
    return pl.pallas_call(
        body, name="attn_mix", grid=(T // tr,),
        in_specs=[grp] * 6 + [pl.BlockSpec((tr, AW), lambda i: (i, ZA // AW))] + [ANY_SPEC] * len(deps),
        out_specs=pl.BlockSpec((tr, AW), lambda i: (i, 0)),
        out_shape=jax.ShapeDtypeStruct((T, D), BF16),
        compiler_params=_cp(("parallel",)),
    )(*os_, *ls_, proj, *deps)


def _attn_mix_bwd(dcat, proj, os_, ls_, dep=None):
    tr = 256
    gw = HPG * HD
    deps = [] if dep is None else [dep]

    def body(dy_ref, o0, o1, o2, l0, l1, l2, z_ref, *rest):
        da_ref, lse_ref, dl_ref, dz_ref = rest[len(deps):]
        m = jnp.maximum(jnp.maximum(l0[...], l1[...]), l2[...])
        e = [jnp.exp(l[...] - m) for l in (l0, l1, l2)]
        den = e[0] + e[1] + e[2]
        inv = 1.0 / den
        lse_ref[...] = m + jnp.log(den)
        acc = jnp.zeros((tr, gw), F32)
        for gi, o in enumerate((o0, o1, o2)):
            cols = slice(gi * gw, (gi + 1) * gw)
            z = z_ref[:, cols]
            dy = dy_ref[:, cols]
            sg = _sigmoid(z)
            a = o[...] * (e[gi] * inv)
            da = dy * (z * sg)
            da_ref[:, cols] = da
            dz_ref[:, cols] = (dy * a * (sg * (1.0 + z * (1.0 - sg)))).astype(BF16)
            acc = acc + da * a
        for hh in range(HPG):
            cols = slice(hh * HD, (hh + 1) * HD)
            dl_ref[:, cols] = jnp.broadcast_to(jnp.sum(acc[:, cols], axis=1, keepdims=True), (tr, HD))

    grp = pl.BlockSpec((tr, gw), lambda i: (i, 0))
    return pl.pallas_call(
        body, name="attn_mix_bwd", grid=(T // tr,),
        in_specs=[pl.BlockSpec((tr, AW), lambda i: (i, 0))] + [grp] * 6 + [pl.BlockSpec((tr, AW), lambda i: (i, ZA // AW))]
        + [pl.BlockSpec(memory_space=pl.ANY)] * len(deps),
        out_specs=[pl.BlockSpec((tr, AW), lambda i: (i, 0)), grp, grp, pl.BlockSpec((tr, AW), lambda i: (i, ZA // AW))],
        out_shape=[jax.ShapeDtypeStruct((T, AW), F32), jax.ShapeDtypeStruct((T, gw), F32),
                   jax.ShapeDtypeStruct((T, gw), F32), jax.ShapeDtypeStruct((T, NC), BF16)],
        compiler_params=_cp(("parallel",)),
    )(dcat, *os_, *ls_, proj, *deps)


CT = 256


def _shift_down(x, n):
    rows = lax.broadcasted_iota(jnp.int32, x.shape, 0)
    return jnp.where(rows >= n, pltpu.roll(x, n, 0), 0.0)


def _shift_up(x, n):
    rows = lax.broadcasted_iota(jnp.int32, x.shape, 0)
    return jnp.where(rows < x.shape[0] - n, pltpu.roll(x, x.shape[0] - n, 0), 0.0)


def _conv_fwd(proj, conv_w, cat):
    proj_v = proj.reshape(NB, S, NC)
    cat_v = cat.reshape(NB, S, D)

    def body(u_ref, b_ref, c_ref, z_ref, w_ref, cat_in, o_ref):
        del cat_in
        cu = c_ref[...] * u_ref[...]
        cv = _shift_down(cu, 2) * w_ref[0:1, :] + _shift_down(cu, 1) * w_ref[1:2, :] + cu * w_ref[2:3, :]
        z = z_ref[...]
        o_ref[...] = (b_ref[...] * cv * (z * _sigmoid(z))).astype(BF16)

    def seg(off):
        return pl.BlockSpec((None, S, CT), lambda b, j: (b, 0, off // CT + j))

    out = pl.pallas_call(
        body, name="conv_fwd", grid=(NB, CW // CT),
        in_specs=[seg(UC), seg(BC), seg(CC), seg(ZC), pl.BlockSpec((3, CT), lambda b, j: (0, j)),
                  pl.BlockSpec(memory_space=pl.ANY)],
        out_specs=pl.BlockSpec((None, S, CT), lambda b, j: (b, 0, AW // CT + j)),
        out_shape=jax.ShapeDtypeStruct((NB, S, D), BF16),
        input_output_aliases={5: 0},
        compiler_params=_cp(("parallel", "parallel")),
    )(proj_v, proj_v, proj_v, proj_v, conv_w, cat_v)
    return out.reshape(T, D)


def _conv_bwd(dcat, proj, conv_w, dproj):
    proj_v = proj.reshape(NB, S, NC)
    dproj_v = dproj.reshape(NB, S, NC)
    dcat_v = dcat.reshape(NB, S, D)

    def body(dy_ref, u_ref, b_ref, c_ref, z_ref, w_ref, dp_in, o_ref, dw_ref, st):
        del dp_in
        b = pl.program_id(1)
        w = pl.program_id(2)

        @pl.when((b == 0) & (w == 0))
        def _():
            dw_ref[...] = jnp.zeros_like(dw_ref)

        @pl.when(w == 0)
        def _():
            u, c, z, bb, dy = u_ref[...], c_ref[...], z_ref[...], b_ref[...], dy_ref[...]
            cu = c * u
            s1 = _shift_down(cu, 1)
            s2 = _shift_down(cu, 2)
            cv = s2 * w_ref[0:1, :] + s1 * w_ref[1:2, :] + cu * w_ref[2:3, :]
            sg = _sigmoid(z)
            sz = z * sg
            dcv = dy * bb * sz
            st[0] = dy * cv * sz
            st[2] = dy * bb * cv * (sg * (1.0 + z * (1.0 - sg)))
            dw_ref[0:1, :] += jnp.sum(dcv * s2, axis=0, keepdims=True)
            dw_ref[1:2, :] += jnp.sum(dcv * s1, axis=0, keepdims=True)
            dw_ref[2:3, :] += jnp.sum(dcv * cu, axis=0, keepdims=True)
            dcu = dcv * w_ref[2:3, :] + _shift_up(dcv, 1) * w_ref[1:2, :] + _shift_up(dcv, 2) * w_ref[0:1, :]
            st[1] = dcu * u
            o_ref[...] = (dcu * c).astype(BF16)

        for k in range(3):
            @pl.when(w == k + 1)
            def _(k=k):
                o_ref[...] = st[k].astype(BF16)

    def ahead(j, b, w):
        flat = jnp.minimum(j * NB + b + jnp.where(w > 0, 1, 0), (CW // CT) * NB - 1)
        return flat // NB, flat % NB

    def seg(off):
        return pl.BlockSpec((None, S, CT), lambda j, b, w: (ahead(j, b, w)[1], 0, off // CT + ahead(j, b, w)[0]))

    out, dw = pl.pallas_call(
        body, name="conv_bwd", grid=(CW // CT, NB, 4),
        in_specs=[seg(AW), seg(UC), seg(BC), seg(CC), seg(ZC),
                  pl.BlockSpec((3, CT), lambda j, b, w: (0, ahead(j, b, w)[0])),
                  pl.BlockSpec(memory_space=pl.ANY)],
        out_specs=[pl.BlockSpec((None, S, CT), lambda j, b, w: (b, 0, (UC + w * CW) // CT + j)),
                   pl.BlockSpec((3, CT), lambda j, b, w: (0, j))],
        out_shape=[jax.ShapeDtypeStruct((NB, S, NC), BF16), jax.ShapeDtypeStruct((3, CW), F32)],
        input_output_aliases={6: 0},
        scratch_shapes=[pltpu.VMEM((3, S, CT), F32)],
        compiler_params=_cp(("arbitrary",) * 3),
    )(dcat_v, proj_v, proj_v, proj_v, proj_v, conv_w, dproj_v)
    return out.reshape(T, NC), dw


XT = 1024


def _cross_fwd(proj, mkv, cosq, sinq, cosm, sinm, cat):
    proj_v = proj.reshape(NB, S, NC)
    mkv_v = mkv.reshape(NB, MLEN, 2 * XW)
    cat_v = cat.reshape(NB, S, D)

    def body(q_ref, z_ref, mk_ref, mv_ref, cq, sq, cm, sm, cat_in, o_ref):
        del cat_in
        mkr = _rope(mk_ref[...], cm[...], sm[...], XHD // 2).astype(BF16)
        qr = _rope(q_ref[...], cq[...], sq[...], XHD // 2).astype(BF16)
        sc = lax.dot_general(qr, mkr, NT, preferred_element_type=F32) * X_SCALE
        p = jnp.exp(sc - jnp.max(sc, axis=1, keepdims=True))
        p = p / jnp.sum(p, axis=1, keepdims=True)
        ox = jnp.dot(p.astype(BF16), mv_ref[...].astype(BF16), preferred_element_type=F32)
        z = z_ref[...]
        o_ref[...] = (ox * (z * _sigmoid(z))).astype(BF16)

    def seg(off):
        return pl.BlockSpec((None, XT, XHD), lambda b, h, t: (b, t, off // XHD + h))

    qtab = pl.BlockSpec((XT, XHD), lambda b, h, t: (t, 0))
    mtab = pl.BlockSpec((MLEN, XHD), lambda b, h, t: (0, 0))
    out = pl.pallas_call(
        body, name="cross_fwd", grid=(NB, NXH, S // XT),
        in_specs=[seg(QX), seg(ZX),
                  pl.BlockSpec((None, MLEN, XHD), lambda b, h, t: (b, 0, h)),
                  pl.BlockSpec((None, MLEN, XHD), lambda b, h, t: (b, 0, NXH + h)),
                  qtab, qtab, mtab, mtab, pl.BlockSpec(memory_space=pl.ANY)],
        out_specs=pl.BlockSpec((None, XT, XHD), lambda b, h, t: (b, t, (AW + CW) // XHD + h)),
        out_shape=jax.ShapeDtypeStruct((NB, S, D), BF16),
        input_output_aliases={8: 0},
        compiler_params=_cp(("parallel",) * 3),
    )(proj_v, proj_v, mkv_v, mkv_v, cosq, sinq, cosm, sinm, cat_v)
    return out.reshape(T, D)


def _cross_bwd(dcat, proj, mkv, cosq, sinq, cosm, sinm, dproj):
    proj_v = proj.reshape(NB, S, NC)
    dproj_v = dproj.reshape(NB, S, NC)
    dcat_v = dcat.reshape(NB, S, D)
    mkv_v = mkv.reshape(NB, MLEN, 2 * XW)
    nt = S // XT

    def body(dy_ref, q_ref, z_ref, mk_ref, mv_ref, cq, sq, cm, sm, dp_in, o_ref, dmk_ref, dmv_ref, dz_s):
        del dp_in
        t = pl.program_id(2)
        w = pl.program_id(3)

        @pl.when((t == 0) & (w == 0))
        def _():
            dmk_ref[...] = jnp.zeros_like(dmk_ref)
            dmv_ref[...] = jnp.zeros_like(dmv_ref)

        @pl.when(w == 0)
        def _():
            mkr = _rope(mk_ref[...], cm[...], sm[...], XHD // 2).astype(BF16)
            mv_b = mv_ref[...].astype(BF16)
            qr = _rope(q_ref[...], cq[...], sq[...], XHD // 2).astype(BF16)
            sc = lax.dot_general(qr, mkr, NT, preferred_element_type=F32) * X_SCALE
            p = jnp.exp(sc - jnp.max(sc, axis=1, keepdims=True))
            p = p / jnp.sum(p, axis=1, keepdims=True)
            p_b = p.astype(BF16)
            ox = jnp.dot(p_b, mv_b, preferred_element_type=F32)
            z = z_ref[...]
            dy = dy_ref[...]
            sg = _sigmoid(z)
            dz_s[...] = dy * ox * (sg * (1.0 + z * (1.0 - sg)))
            dox_b = (dy * (z * sg)).astype(BF16)
            dp = lax.dot_general(dox_b, mv_b, NT, preferred_element_type=F32)
            ds_b = (p * (dp - jnp.sum(dp * p, axis=1, keepdims=True)) * X_SCALE).astype(BF16)
            dq = jnp.dot(ds_b, mkr, preferred_element_type=F32)
            o_ref[...] = _rope_t(dq, cq[...], sq[...], XHD // 2).astype(BF16)
            dmk_ref[...] += lax.dot_general(ds_b, qr, TN, preferred_element_type=F32)
            dmv_ref[...] += lax.dot_general(p_b, dox_b, TN, preferred_element_type=F32)

        @pl.when(w == 1)
        def _():
            o_ref[...] = dz_s[...].astype(BF16)

        @pl.when((t == nt - 1) & (w == 1))
        def _():
            dmk_ref[...] = _rope_t(dmk_ref[...], cm[...], sm[...], XHD // 2)

    def seg(off):
        return pl.BlockSpec((None, XT, XHD), lambda b, h, t, w: (b, t, off // XHD + h))

    qtab = pl.BlockSpec((XT, XHD), lambda b, h, t, w: (t, 0))
    mtab = pl.BlockSpec((MLEN, XHD), lambda b, h, t, w: (0, 0))
    macc = pl.BlockSpec((None, MLEN, XHD), lambda b, h, t, w: (b, 0, h))
    out, dmk, dmv = pl.pallas_call(
        body, name="cross_bwd", grid=(NB, NXH, nt, 2),
        in_specs=[pl.BlockSpec((None, XT, XHD), lambda b, h, t, w: (b, t, (AW + CW) // XHD + h)),
                  seg(QX), seg(ZX),
                  pl.BlockSpec((None, MLEN, XHD), lambda b, h, t, w: (b, 0, h)),
                  pl.BlockSpec((None, MLEN, XHD), lambda b, h, t, w: (b, 0, NXH + h)),
                  qtab, qtab, mtab, mtab, pl.BlockSpec(memory_space=pl.ANY)],
        out_specs=[pl.BlockSpec((None, XT, XHD), lambda b, h, t, w: (b, t, (QX + w * XW) // XHD + h)), macc, macc],
        out_shape=[jax.ShapeDtypeStruct((NB, S, NC), BF16), jax.ShapeDtypeStruct((NB, MLEN, XW), F32),
                   jax.ShapeDtypeStruct((NB, MLEN, XW), F32)],
        input_output_aliases={9: 0},
        scratch_shapes=[pltpu.VMEM((XT, XHD), F32)],
        compiler_params=_cp(("arbitrary",) * 4),
    )(dcat_v, proj_v, proj_v, mkv_v, mkv_v, cosq, sinq, cosm, sinm, dproj_v)
    return out.reshape(T, NC), dmk, dmv


def _local_step(x, mem, pre_norm, conv_w, mem_norm, post_norm, tgt, chip, core, comm):
    x2 = x.reshape(T, D)
    mem2 = mem.reshape(NB * MLEN, D)
    tgt2 = tgt.reshape(T, D)
    cosa, sina = _rope_tables(jnp.arange(S), HD // 2)
    cosq, sinq = _rope_tables(jnp.arange(S) + MLEN, XHD // 2)
    cosm, sinm = _rope_tables(jnp.arange(MLEN), XHD // 2)

    h = _rms_fwd(x2, pre_norm, "pre_norm_fwd", dep=comm.gather_started())
    memn = _rms_fwd(mem2, mem_norm, "mem_norm_fwd")
    proj = _proj_part(h, comm.w_in_own(), chip, (0,), "proj_own")
    wg_in, tok = comm.w_in_near(after=[proj, memn, conv_w])
    proj = _proj_part(h, wg_in, chip, (REL_XOR[0], REL_XOR[1]), "proj_near", prev=proj, dep=tok)
    wg_in, tok = comm.w_in_all(after=proj)
    proj = _proj_part(h, wg_in, chip, (REL_XOR[2],), "proj_far", prev=proj, dep=tok)
    fw = [_attn_fwd(proj, cosa, sina, g) for g in range(3)]
    os_ = [f[0] for f in fw]
    ls_ = [f[1] for f in fw]
    tok = comm.w_rest_landed(after=ls_)
    cat = _attn_mix(proj, os_, ls_, dep=tok)
    cat = _conv_fwd(proj, conv_w, cat)
    wg_kv, wg_out = comm.w_rest(after=[cat])
    mkv = _mm_nn(memn, wg_kv, "mkv", NB * MLEN, 1024, 512)
    cat = _cross_fwd(proj, mkv, cosq, sinq, cosm, sinm, cat)
    y = _mm_nn(cat, wg_out, "out_proj", 1024, 1024, 2048)
    dy, dout, d_post, loss = _post_norm_loss(y, x2, tgt2, post_norm)

    dcat = _mm_nt(dy, wg_out, "dcat", 1024, 1024, 2048)
    g_send = _grad_rows(cat, dy,1 - core, BF16, "grad_w_out_send")
    sent = comm.sibling_start(g_send.reshape(NCHIP, D // NCHIP // 2, D), "a")
    g_keep = _grad_rows(cat, dy,core, BF16, "grad_w_out_keep", dep=sent[-1])
    red_a = comm.reduce_start_summed(g_keep.reshape(NCHIP, D // NCHIP // 2, D), sent, "a")
    da, lse, delta, dproj = _attn_mix_bwd(dcat, proj, os_, ls_, dep=red_a[-1])
    for g in range(3):
        dproj = _attn_bwd(proj, fw[g][2], fw[g][3], cosa, sina, da, lse, delta, dproj, g)
    dproj, d_conv = _conv_bwd(dcat, proj, conv_w, dproj)
    dproj, dmk, dmv = _cross_bwd(dcat, proj, mkv, cosq, sinq, cosm, sinm, dproj)

    dmkv = jnp.concatenate([dmk, dmv], axis=-1).reshape(NB * MLEN, 2 * XW)
    dmkv_b = dmkv.astype(BF16)
    gk_send = _grad_rows(memn, dmkv_b, 1 - core, BF16, "grad_w_mem_kv_send")
    sent = comm.sibling_start(gk_send.reshape(NCHIP, D // NCHIP // 2, 2 * XW), "k")
    gk_keep = _grad_rows(memn, dmkv_b, core, BF16, "grad_w_mem_kv_keep", dep=sent[-1])
    dmemn = _mm_nt(dmkv_b, wg_kv, "dmemn", NB * MLEN, 1024, 512)
    d_mem = _norm_gain_grad(dmemn, mem2, "mem_norm_bwd")
    red_k = comm.reduce_start_summed(gk_keep.reshape(NCHIP, D // NCHIP // 2, 2 * XW), sent, "k", after=d_mem)
    g_send = _grad_w_in(h, dproj, 1 - core, BF16, "grad_w_in_send", dep=red_k[-1])
    sent = comm.sibling_start(g_send.reshape(NCHIP, D // 2, WB), "b")
    g_keep = _grad_w_in(h, dproj, core, BF16, "grad_w_in_keep", dep=sent[-1])
    red_b = comm.reduce_start_summed(g_keep.reshape(NCHIP, D // 2, WB), sent, "b")
    dh = _dh(dproj, wg_in, dep=red_b[-1])
    grad_x, d_pre = _pre_norm_bwd(dh, x2, pre_norm, dout)
    r_in = comm.reduce_finish_start(red_b, "b", after=grad_x)
    (r_out,) = comm.reduce_finish(red_a, "a", after=grad_x)
    (r_kv,) = comm.reduce_finish(red_k, "k", after=grad_x)
    return loss, grad_x.reshape(NB, S, D), d_pre, d_mem, d_post, d_conv, r_in, r_kv, r_out


def _adamw(w, g, m, v, name):
    rows, cols = w.shape
    tr = rows if rows <= 512 else 512
    tc = cols if cols <= 1024 else 1024
    if cols % tc:
        tc = 896

    def body(w_ref, g_ref, m_ref, v_ref, go_ref, d_ref, nm_ref, nv_ref):
        gv = g_ref[...]
        go_ref[...] = gv
        nm = ADAM_B1 * m_ref[...] + (1.0 - ADAM_B1) * gv
        nv = ADAM_B2 * v_ref[...] + (1.0 - ADAM_B2) * (gv * gv)
        m_hat = nm / (1.0 - ADAM_B1 ** ADAM_STEP)
        v_hat = nv / (1.0 - ADAM_B2 ** ADAM_STEP)
        d_ref[...] = -ADAM_LR * (m_hat / (jnp.sqrt(v_hat) + ADAM_EPS) + ADAM_WD * w_ref[...])
        nm_ref[...] = nm
        nv_ref[...] = nv

    blk = pl.BlockSpec((tr, tc), lambda i, j: (i, j))
    sds = jax.ShapeDtypeStruct((rows, cols), F32)
    return pl.pallas_call(
        body, name=name, grid=(rows // tr, cols // tc),
        in_specs=[blk] * 4, out_specs=[blk] * 4, out_shape=[sds] * 4,
        compiler_params=_cp(("parallel", "parallel")),
    )(w, g, m, v)


def _place():
    return lax.axis_index("x"), lax.axis_index("y"), lax.axis_index("c")


def _other_chips(x, y):
    return [(1 - x, y), (x, 1 - y), (1 - x, 1 - y)]


def _tile_cols(cols):
    return cols if cols <= 1024 else (1024 if cols % 1024 == 0 else 896)


def _cast_own(w, chip, name):
    rows, cols = w.shape
    tr, tc = 512, _tile_cols(cols)

    def body(chip_ref, w_ref, o_ref):
        del chip_ref
        o_ref[...] = w_ref[...].astype(BF16)

    grid_spec = pltpu.PrefetchScalarGridSpec(
        num_scalar_prefetch=1, grid=(rows // tr, cols // tc),
        in_specs=[pl.BlockSpec((tr, tc), lambda i, j, cr: (i, j))],
        out_specs=pl.BlockSpec((None, tr, tc), lambda i, j, cr: (cr[0], i, j)))
    return pl.pallas_call(
        body, name=name, grid_spec=grid_spec,
        out_shape=jax.ShapeDtypeStruct((NCHIP, rows, cols), BF16),
        compiler_params=_cp(("parallel", "parallel")),
    )(_as_index(chip), w)


HBM_SPEC = pl.BlockSpec(memory_space=pltpu.HBM)
SEM_SPEC = pl.BlockSpec(memory_space=pltpu.SEMAPHORE)
ANY_SPEC = pl.BlockSpec(memory_space=pl.ANY)
EFFECT = pltpu.SideEffectType.DATAFLOW_SIDE_EFFECTING
TOKEN = jax.ShapeDtypeStruct((8, 128), F32)


def _half(ref, chip, hc):
    hr = ref.shape[1] // 2
    return ref.at[chip, pl.ds(hc * hr, hr), :]


NEAR = (0, 1)
FAR = (2,)
REL_XOR = (2, 1, 3)


def _gather_copies(refs, send_sems, recv_sems, rels):
    x, y, c = _place()
    chips = _other_chips(x, y)
    out, inc = [], []
    for a, ref in enumerate(refs):
        for p, j in enumerate(rels):
            px, py = chips[j]
            mine = _half(ref, 2 * x + y, c)
            theirs = _half(ref, 2 * px + py, c)
            sems = dict(send_sem=send_sems.at[len(rels) * a + p], recv_sem=recv_sems.at[len(rels) * a + p],
                        device_id=(px, py, c), device_id_type=MESH)
            out.append(pltpu.make_async_remote_copy(src_ref=mine, dst_ref=mine, **sems))
            inc.append(pltpu.make_async_remote_copy(src_ref=theirs, dst_ref=theirs, **sems))
    return out, inc


def _gather_start(bufs, groups, name):
    n = len(bufs)
    ng = len(groups)

    def body(*refs):
        ins = refs[:n]
        token = refs[-1]
        for gi, rels in enumerate(groups):
            out, _ = _gather_copies(ins, refs[n + 2 * gi], refs[n + 2 * gi + 1], rels)
            for cp in out:
                cp.start()
        token[...] = jnp.zeros_like(token)

    sems = []
    for rels in groups:
        sems += [pltpu.SemaphoreType.DMA((len(rels) * n,))] * 2
    res = pl.pallas_call(
        body, name=name,
        in_specs=[HBM_SPEC] * n,
        out_specs=[SEM_SPEC] * (2 * ng) + [HBM_SPEC] * n + [pl.BlockSpec(memory_space=pltpu.VMEM)],
        out_shape=sems + [pltpu.HBM(b.shape, b.dtype) for b in bufs] + [TOKEN],
        input_output_aliases={a: 2 * ng + a for a in range(n)},
        compiler_params=pltpu.CompilerParams(has_side_effects=EFFECT),
    )(*[pltpu.with_memory_space_constraint(b, pltpu.HBM) for b in bufs])
    return [(res[2 * gi], res[2 * gi + 1]) for gi in range(ng)], list(res[2 * ng:2 * ng + n]), res[-1]


def _gather_wait(bufs, sems, rels, after, name):
    n = len(bufs)
    send_sems, recv_sems = sems
    after = list(after) if isinstance(after, (list, tuple)) else [after]

    def body(*refs):
        ins = refs[:n]
        out, inc = _gather_copies(ins, refs[n], refs[n + 1], rels)
        for cp in out:
            cp.wait_send()
        for cp in inc:
            cp.wait_recv()

    return pl.pallas_call(
        body, name=name,
        in_specs=[HBM_SPEC] * n + [SEM_SPEC, SEM_SPEC] + [ANY_SPEC] * len(after),
        out_specs=[HBM_SPEC] * n,
        out_shape=[pltpu.HBM(b.shape, b.dtype) for b in bufs],
        input_output_aliases={a: a for a in range(n)},
        compiler_params=pltpu.CompilerParams(has_side_effects=EFFECT),
    )(*bufs, send_sems, recv_sems, *after)


def _forward_halves(bufs, rels, name):
    n = len(bufs)

    def body(*refs):
        cps, waits = _forward_copies(refs[n:2 * n], rels, refs[2 * n], refs[2 * n + 1])
        for cp in cps:
            cp.start()
        for cp in waits:
            cp.wait_recv()
        for cp in cps:
            cp.wait_send()

    return pl.pallas_call(
        body, name=name,
        in_specs=[ANY_SPEC] * n, out_specs=[ANY_SPEC] * n,
        out_shape=[jax.ShapeDtypeStruct(s.shape, s.dtype) for s in bufs],
        input_output_aliases={a: a for a in range(n)},
        scratch_shapes=[pltpu.SemaphoreType.DMA((len(rels) * n,)), pltpu.SemaphoreType.DMA((len(rels) * n,))],
    )(*bufs)


def _forward_copies(refs, rels, send_sems, recv_sems):
    x, y, c = _place()
    chips = _other_chips(x, y)
    cps, waits = [], []
    for a, ref in enumerate(refs):
        for p, j in enumerate(rels):
            px, py = chips[j]
            sems = dict(send_sem=send_sems.at[len(rels) * a + p], recv_sem=recv_sems.at[len(rels) * a + p],
                        device_id=(x, y, 1 - c), device_id_type=MESH)
            got = _half(ref, 2 * px + py, c)
            want = _half(ref, 2 * px + py, 1 - c)
            cps.append(pltpu.make_async_remote_copy(src_ref=got, dst_ref=got, **sems))
            waits.append(pltpu.make_async_remote_copy(src_ref=want, dst_ref=want, **sems))
    return cps, waits


def _forward_start(bufs, rels, name):
    n = len(bufs)

    def body(*refs):
        cps, _ = _forward_copies(refs[:n], rels, refs[n], refs[n + 1])
        for cp in cps:
            cp.start()
        refs[-1][...] = jnp.zeros_like(refs[-1])

    res = pl.pallas_call(
        body, name=name,
        in_specs=[HBM_SPEC] * n,
        out_specs=[SEM_SPEC, SEM_SPEC] + [HBM_SPEC] * n + [pl.BlockSpec(memory_space=pltpu.VMEM)],
        out_shape=[pltpu.SemaphoreType.DMA((len(rels) * n,))] * 2 + [pltpu.HBM(b.shape, b.dtype) for b in bufs] + [TOKEN],
        input_output_aliases={a: 2 + a for a in range(n)},
        compiler_params=pltpu.CompilerParams(has_side_effects=EFFECT),
    )(*[pltpu.with_memory_space_constraint(b, pltpu.HBM) for b in bufs])
    return (res[0], res[1]), list(res[2:2 + n]), res[-1]


def _forward_wait(bufs, sems, rels, after, name):
    n = len(bufs)

    def body(*refs):
        cps, waits = _forward_copies(refs[:n], rels, refs[n], refs[n + 1])
        for cp in cps:
            cp.wait_send()
        for cp in waits:
            cp.wait_recv()

    return pl.pallas_call(
        body, name=name,
        in_specs=[HBM_SPEC] * n + [SEM_SPEC, SEM_SPEC] + [ANY_SPEC] * len(after),
        out_specs=[HBM_SPEC] * n,
        out_shape=[pltpu.HBM(b.shape, b.dtype) for b in bufs],
        input_output_aliases={a: a for a in range(n)},
        compiler_params=pltpu.CompilerParams(has_side_effects=EFFECT),
    )(*bufs, sems[0], sems[1], *after)


def _sibling_copy(src, land, send_sems, recv_sems):
    x, y, c = _place()
    return pltpu.make_async_remote_copy(src_ref=src, dst_ref=land, send_sem=send_sems.at[0], recv_sem=recv_sems.at[0],
                                        device_id=(x, y, 1 - c), device_id_type=MESH)


def _sibling_start(part, name):
    def body(src, land, send_sems, recv_sems, src_thru, land_thru, token):
        _sibling_copy(src, land, send_sems, recv_sems).start()
        token[...] = jnp.zeros_like(token)

    land = lax.empty(part.shape, part.dtype)
    return pl.pallas_call(
        body, name=name,
        in_specs=[HBM_SPEC] * 2,
        out_specs=[SEM_SPEC, SEM_SPEC, HBM_SPEC, HBM_SPEC, pl.BlockSpec(memory_space=pltpu.VMEM)],
        out_shape=[pltpu.SemaphoreType.DMA((1,)), pltpu.SemaphoreType.DMA((1,)), pltpu.HBM(part.shape, part.dtype),
                   pltpu.HBM(part.shape, part.dtype), TOKEN],
        input_output_aliases={0: 2, 1: 3},
        compiler_params=pltpu.CompilerParams(has_side_effects=EFFECT),
    )(pltpu.with_memory_space_constraint(part, pltpu.HBM), pltpu.with_memory_space_constraint(land, pltpu.HBM))


def _sibling_wait(state, after, name):
    send_sems, recv_sems, part, land, _ = state

    def body(src, land_ref, send_ref, recv_ref, *rest):
        cp = _sibling_copy(src, land_ref, send_ref, recv_ref)
        cp.wait_send()
        cp.wait_recv()

    return pl.pallas_call(
        body, name=name,
        in_specs=[HBM_SPEC, HBM_SPEC, SEM_SPEC, SEM_SPEC] + [ANY_SPEC] * len(after),
        out_specs=[HBM_SPEC, HBM_SPEC],
        out_shape=[pltpu.HBM(part.shape, part.dtype), pltpu.HBM(land.shape, land.dtype)],
        input_output_aliases={0: 0, 1: 1},
        compiler_params=pltpu.CompilerParams(has_side_effects=EFFECT),
    )(part, land, send_sems, recv_sems, *after)[1]


def _pair_sum_rows(keep, got, name):
    nblk, rows, cols = keep.shape
    tr, tc = 512, _tile_cols(cols)

    def body(k_ref, g_ref, o_ref):
        o_ref[...] = (k_ref[...].astype(F32) + g_ref[...].astype(F32)).astype(BF16)

    blk = pl.BlockSpec((None, tr, tc), lambda b, i, j: (b, i, j))
    return pl.pallas_call(
        body, name=name, grid=(nblk, rows // tr, cols // tc),
        in_specs=[blk, blk], out_specs=blk,
        out_shape=jax.ShapeDtypeStruct(keep.shape, BF16),
        compiler_params=_cp(("parallel",) * 3),
    )(keep, got)


def _scatter_start(qs, name):
    n = len(qs)

    def body(*refs):
        ins, lands = refs[:n], refs[n:2 * n]
        token = refs[-1]
        for cp in _scatter_copies(ins, lands, refs[2 * n], refs[2 * n + 1]):
            cp.start()
        token[...] = jnp.zeros_like(token)

    lands = [lax.empty((3,) + q.shape[1:], q.dtype) for q in qs]
    res = pl.pallas_call(
        body, name=name,
        in_specs=[HBM_SPEC] * (2 * n),
        out_specs=[SEM_SPEC, SEM_SPEC] + [HBM_SPEC] * (2 * n) + [pl.BlockSpec(memory_space=pltpu.VMEM)],
        out_shape=[pltpu.SemaphoreType.DMA((3 * n,)), pltpu.SemaphoreType.DMA((3 * n,))]
        + [pltpu.HBM(b.shape, b.dtype) for b in qs + lands] + [TOKEN],
        input_output_aliases={a: 2 + a for a in range(2 * n)},
        compiler_params=pltpu.CompilerParams(has_side_effects=EFFECT),
    )(*[pltpu.with_memory_space_constraint(b, pltpu.HBM) for b in qs + lands])
    return res[0], res[1], list(res[2:2 + n]), list(res[2 + n:2 + 2 * n]), res[-1]


def _scatter_copies(ins, lands, send_sems, recv_sems):
    x, y, c = _place()
    cps = []
    for a in range(len(ins)):
        for j, (px, py) in enumerate(_other_chips(x, y)):
            cps.append(pltpu.make_async_remote_copy(
                src_ref=ins[a].at[2 * px + py], dst_ref=lands[a].at[j],
                send_sem=send_sems.at[3 * a + j], recv_sem=recv_sems.at[3 * a + j], device_id=(px, py, c), device_id_type=MESH))
    return cps


def _scatter_wait(qs, lands, send_sems, recv_sems, after, name):
    n = len(qs)

    def body(*refs):
        for cp in _scatter_copies(refs[:n], refs[n:2 * n], refs[2 * n], refs[2 * n + 1]):
            cp.wait_send()
            cp.wait_recv()

    res = pl.pallas_call(
        body, name=name,
        in_specs=[HBM_SPEC] * (2 * n) + [SEM_SPEC, SEM_SPEC, ANY_SPEC],
        out_specs=[HBM_SPEC] * (2 * n),
        out_shape=[pltpu.HBM(b.shape, b.dtype) for b in qs + lands],
        input_output_aliases={a: a for a in range(2 * n)},
        compiler_params=pltpu.CompilerParams(has_side_effects=EFFECT),
    )(*qs, *lands, send_sems, recv_sems, after)
    return list(res[:n]), list(res[n:])


def _chip_sum(q, got, name):
    _, hr, cols = got.shape
    tr, tc = 512, _tile_cols(cols)
    chip = 2 * lax.axis_index("x") + lax.axis_index("y")
    c = lax.axis_index("c")

    def body(idx_ref, q_ref, g_ref, o_ref):
        del idx_ref
        acc = q_ref[...].astype(F32)
        for i in range(3):
            acc = acc + g_ref[i].astype(F32)
        o_ref[...] = acc

    grid_spec = pltpu.PrefetchScalarGridSpec(
        num_scalar_prefetch=1, grid=(hr // tr, cols // tc),
        in_specs=[pl.BlockSpec((None, tr, tc), lambda i, j, ix: (ix[0], i, j)),
                  pl.BlockSpec((3, tr, tc), lambda i, j, ix: (0, i, j))],
        out_specs=pl.BlockSpec((None, tr, tc), lambda i, j, ix: (ix[1], i, j)))
    return pl.pallas_call(
        body, name=name, grid_spec=grid_spec,
        out_shape=jax.ShapeDtypeStruct((2, hr, cols), F32),
        compiler_params=_cp(("parallel", "parallel")),
    )(jnp.stack([chip, c]).astype(jnp.int32), q, got)


def _join_halves(bufs, name):
    n = len(bufs)

    def body(*refs):
        outs = refs[n:2 * n]
        send_sems, recv_sems = refs[2 * n:]
        x, y, c = _place()
        cps = []
        for a in range(n):
            cps.append(pltpu.make_async_remote_copy(src_ref=outs[a].at[c], dst_ref=outs[a].at[c], send_sem=send_sems.at[a],
                                                    recv_sem=recv_sems.at[a], device_id=(x, y, 1 - c), device_id_type=MESH))
        for cp in cps:
            cp.start()
        for a in range(n):
            theirs = outs[a].at[1 - c]
            pltpu.make_async_remote_copy(src_ref=theirs, dst_ref=theirs, send_sem=send_sems.at[a], recv_sem=recv_sems.at[a],
                                         device_id=(x, y, 1 - c), device_id_type=MESH).wait_recv()
        for cp in cps:
            cp.wait_send()

    hbm = pl.BlockSpec(memory_space=pl.ANY)
    return pl.pallas_call(
        body, name=name,
        in_specs=[hbm] * n, out_specs=[hbm] * n,
        out_shape=[jax.ShapeDtypeStruct(b.shape, b.dtype) for b in bufs],
        input_output_aliases={a: a for a in range(n)},
        scratch_shapes=[pltpu.SemaphoreType.DMA((n,)), pltpu.SemaphoreType.DMA((n,))],
    )(*bufs)


def _join_copy(buf, send_sems, recv_sems):
    x, y, c = _place()
    sems = dict(send_sem=send_sems.at[0], recv_sem=recv_sems.at[0], device_id=(x, y, 1 - c), device_id_type=MESH)
    return (pltpu.make_async_remote_copy(src_ref=buf.at[c], dst_ref=buf.at[c], **sems),
            pltpu.make_async_remote_copy(src_ref=buf.at[1 - c], dst_ref=buf.at[1 - c], **sems))


def _join_start(buf, name):
    def body(b_ref, send_sems, recv_sems, thru, token):
        _join_copy(b_ref, send_sems, recv_sems)[0].start()
        token[...] = jnp.zeros_like(token)

    return pl.pallas_call(
        body, name=name,
        in_specs=[HBM_SPEC],
        out_specs=[SEM_SPEC, SEM_SPEC, HBM_SPEC, pl.BlockSpec(memory_space=pltpu.VMEM)],
        out_shape=[pltpu.SemaphoreType.DMA((1,)), pltpu.SemaphoreType.DMA((1,)), pltpu.HBM(buf.shape, buf.dtype), TOKEN],
        input_output_aliases={0: 2},
        compiler_params=pltpu.CompilerParams(has_side_effects=EFFECT),
    )(pltpu.with_memory_space_constraint(buf, pltpu.HBM))


def _join_wait(state, after, name):
    send_sems, recv_sems, buf, _ = state

    def body(b_ref, send_ref, recv_ref, *rest):
        out, inc = _join_copy(b_ref, send_ref, recv_ref)
        out.wait_send()
        inc.wait_recv()

    return pl.pallas_call(
        body, name=name,
        in_specs=[HBM_SPEC, SEM_SPEC, SEM_SPEC] + [ANY_SPEC] * len(after),
        out_specs=HBM_SPEC,
        out_shape=pltpu.HBM(buf.shape, buf.dtype),
        input_output_aliases={0: 0},
        compiler_params=pltpu.CompilerParams(has_side_effects=EFFECT),
    )(buf, send_sems, recv_sems, *after)


N_DEV = 8


def _gather_small(block, reduce, name):
    m_per, cols = block.shape

    def body(x_ref, out_ref, all_ref, send_sems, recv_sems, local_sem):
        x, y, c = _place()
        me, sibling = (x, y, c), (x, y, 1 - c)
        chips = _other_chips(x, y)

        def rows(px, py, pc):
            return all_ref.at[pl.ds((4 * px + 2 * py + pc) * m_per, m_per), :]

        def copy(k, block_of, to, src=None):
            return pltpu.make_async_remote_copy(
                src_ref=rows(*block_of) if src is None else src, dst_ref=rows(*block_of),
                send_sem=send_sems.at[k], recv_sem=recv_sems.at[k], device_id=to, device_id_type=MESH)

        mine = pltpu.make_async_copy(x_ref, rows(*me), local_sem)
        mine.start()
        first = [copy(0, me, sibling, src=x_ref)]
        first += [copy(1 + j, me, (*chip, c), src=x_ref) for j, chip in enumerate(chips)]
        for cp in first:
            cp.start()
        passed = [copy(4 + j, (*chip, c), sibling) for j, chip in enumerate(chips)]
        for j, chip in enumerate(chips):
            copy(1 + j, (*chip, c), me).wait_recv()
            passed[j].start()
        copy(0, sibling, me).wait_recv()
        for j, chip in enumerate(chips):
            copy(4 + j, (*chip, 1 - c), me).wait_recv()
        for cp in first + passed:
            cp.wait_send()
        mine.wait()
        if reduce:
            acc = all_ref[pl.ds(0, m_per), :]
            for i in range(1, N_DEV):
                acc = acc + all_ref[pl.ds(i * m_per, m_per), :]
            out_ref[...] = acc
        else:
            out_ref[...] = all_ref[...]

    out_rows = m_per if reduce else N_DEV * m_per
    return pl.pallas_call(
        body, name=name,
        in_specs=[pl.BlockSpec(memory_space=pltpu.VMEM)],
        out_specs=pl.BlockSpec(memory_space=pltpu.VMEM),
        out_shape=jax.ShapeDtypeStruct((out_rows, cols), F32),
        scratch_shapes=[pltpu.VMEM((N_DEV * m_per, cols), F32), pltpu.SemaphoreType.DMA((7,)),
                        pltpu.SemaphoreType.DMA((7,)), pltpu.SemaphoreType.DMA],
    )(block)


class _Comm:
    def __init__(self, w_in, w_kv, w_out, chip):
        self.bufs = [_cast_own(w_in, chip, "cast_w_in"), _cast_own(w_kv, chip, "cast_w_mem_kv"),
                     _cast_own(w_out, chip, "cast_w_out")]


    def gather_started(self):
        (self.sems,), (self.b_in,), tok = _gather_start(self.bufs[:1], (NEAR,), "gather_start_in_near")
        return [tok]

    def w_in_own(self):
        return self.b_in.reshape(NCHIP * D, WB)

    def w_in_near(self, after):
        (b_in,) = _gather_wait([self.b_in], self.sems, NEAR, list(after) + self.bufs[1:], "gather_wait_in_near")
        (b_in,) = _forward_halves([b_in], NEAR, "forward_in_near")
        (self.sems,), (self.b_in,), tok = _gather_start([b_in], (FAR,), "gather_start_in_far")
        return self.b_in.reshape(NCHIP * D, WB), tok

    def w_in_all(self, after):
        (b_in,) = _gather_wait([self.b_in], self.sems, FAR, after, "gather_wait_in_far")
        (b_in,) = _forward_halves([b_in], FAR, "forward_in_far")
        (self.sems,), self.b_rest, tok = _gather_start(self.bufs[1:], (NEAR + FAR,), "gather_start_rest")
        return b_in.reshape(NCHIP * D, WB), tok

    def w_rest_landed(self, after):
        b_rest = _gather_wait(self.b_rest, self.sems, NEAR + FAR, after, "gather_wait_rest")
        self.sems, self.b_rest, tok = _forward_start(b_rest, NEAR + FAR, "forward_start_rest")
        return tok

    def w_rest(self, after):
        b_kv, b_out = _forward_wait(self.b_rest, self.sems, NEAR + FAR, after, "forward_wait_rest")
        return b_kv.reshape(D, 2 * XW), b_out.reshape(D, D)

    def sibling_start(self, send, tag):
        return _sibling_start(send, "sibling_start_" + tag)

    def reduce_start_summed(self, keep, sent, tag, after=None):
        got = _sibling_wait(sent, [keep] + ([] if after is None else [after]), "sibling_wait_" + tag)
        return _scatter_start([_pair_sum_rows(keep, got, "pair_sum_" + tag)], "scatter_start_" + tag)

    def reduce_finish_start(self, state, tag, after):
        send_sems, recv_sems, qs, lands, _ = state
        qs, lands = _scatter_wait(qs, lands, send_sems, recv_sems, after, "scatter_wait_" + tag)
        return _join_start(_chip_sum(qs[0], lands[0], f"chip_sum_{tag}0"), "join_start_" + tag)

    def reduce_finish_wait(self, pending, tag, after):
        j = _join_wait(pending, after, "join_wait_" + tag)
        return j.reshape(2 * j.shape[1], j.shape[2])

    def reduce_finish(self, state, tag, after):
        send_sems, recv_sems, qs, lands, _ = state
        qs, lands = _scatter_wait(qs, lands, send_sems, recv_sems, after, "scatter_wait_" + tag)
        halves = [_chip_sum(q, l, f"chip_sum_{tag}{i}") for i, (q, l) in enumerate(zip(qs, lands))]
        return [j.reshape(2 * j.shape[1], j.shape[2]) for j in _join_halves(halves, "join_halves_" + tag)]


def kernel(x, mem, pre_norm, w_in, conv_w, mem_norm, w_mem_kv, w_out, post_norm, loss_target, m_pre_norm, m_w_in, m_conv_w, m_mem_norm, m_w_mem_kv, m_w_out, m_post_norm, v_pre_norm, v_w_in, v_conv_w, v_mem_norm, v_w_mem_kv, v_w_out, v_post_norm):
    chip = 2 * lax.axis_index("x") + lax.axis_index("y")

    cw_blk = jnp.zeros((8, 384), F32).at[:3].set(conv_w[0])
    cw_all = _gather_small(cw_blk, False, "gather_conv_w").reshape(NCHIP, 2, 8, 384)[:, 0, :3]
    conv_full = jnp.transpose(cw_all, (1, 0, 2)).reshape(3, CW)

    comm = _Comm(w_in[0], w_mem_kv[0], w_out[0], chip)
    loss, grad_x, d_pre, d_mem, d_post, d_conv, r_in, r_kv, r_out = _local_step(
        x, mem, pre_norm, conv_full, mem_norm, post_norm, loss_target, chip, lax.axis_index("c"), comm)

    pack = jnp.concatenate([d_pre, d_mem, d_post, jnp.pad(d_conv, ((0, 0), (0, D - CW))),
                            jnp.pad(loss, ((0, 0), (0, D - 128))), jnp.zeros((1, D), F32)], axis=0)
    tot = _gather_small(pack, True, "reduce_small")
    g_pre, g_mem, g_post = tot[0:1], tot[1:2], tot[2:3]
    g_conv = lax.dynamic_slice(tot[3:6, :CW], (0, chip * 384), (3, 384))
    loss_out = tot[6, 0]

    names = ("pre_norm", "w_in", "conv_w", "mem_norm", "w_mem_kv", "w_out", "post_norm")
    ws = (pre_norm, w_in[0], conv_w[0], mem_norm, w_mem_kv[0], w_out[0], post_norm)
    gs = [g_pre, None, g_conv, g_mem, r_kv, r_out, g_post]
    ms = (m_pre_norm, m_w_in[0], m_conv_w[0], m_mem_norm, m_w_mem_kv[0], m_w_out[0], m_post_norm)
    vs = (v_pre_norm, v_w_in[0], v_conv_w[0], v_mem_norm, v_w_mem_kv[0], v_w_out[0], v_post_norm)
    upd = [None if g is None else _adamw(w, g, m, v, "adamw_" + nm) for nm, w, g, m, v in zip(names, ws, gs, ms, vs)]
    gs[1] = comm.reduce_finish_wait(r_in, "b", after=[u[1] for u in upd if u is not None])
    upd[1] = _adamw(ws[1], gs[1], ms[1], vs[1], "adamw_w_in")

    def shaped(arrs):
        return [a.reshape(w.shape) if w.ndim == a.ndim else a.reshape((1,) + a.shape)
                for a, w in zip(arrs, (pre_norm, w_in, conv_w, mem_norm, w_mem_kv, w_out, post_norm))]

    grads = shaped([u[0] for u in upd])
    deltas = shaped([u[1] for u in upd])
    new_m = shaped([u[2] for u in upd])
    new_v = shaped([u[3] for u in upd])
    return (loss_out, grad_x, *grads, *deltas, *new_m, *new_v)
```

```python
import jax
import jax.numpy as jnp
from jax import lax
from jax.experimental import pallas as pl
from jax.experimental.pallas import tpu as pltpu

F32 = jnp.float32
BF16 = jnp.bfloat16

D = 4096
S = 2048
NB = 2
T = NB * S
MLEN = 256
HD = 128
AW = 1536
CW = 1536
XW = 1024
XHD = 256
NXH = 4
NC = 14336
QA, KA, VA, ZA, UC, BC, CC, ZC, QX, ZX = 0, 1536, 3072, 4608, 6144, 7680, 9216, 10752, 12288, 13312
NCHIP = 4
WB = NC // NCHIP
DIL = (1, 4, 16)
HPG = 4
EPS = 1e-6
NEG = -1e30
ROPE_THETA = 10000.0
A_SCALE = HD ** -0.5
X_SCALE = XHD ** -0.5

ADAM_LR = 0.001
ADAM_B1 = 0.9
ADAM_B2 = 0.999
ADAM_EPS = 1e-08
ADAM_WD = 0.01
ADAM_STEP = 10

MESH = pl.DeviceIdType.MESH
MIB = 1024 * 1024


def _cp(sem, vmem_mib=48):
    return pltpu.CompilerParams(dimension_semantics=sem, vmem_limit_bytes=vmem_mib * MIB)


def _sigmoid(z):
    return 1.0 / (1.0 + jnp.exp(-z))


def _rope(x, cos, sin, half):
    return x * cos + pltpu.roll(x, half, 1) * sin


def _rope_t(g, cos, sin, half):
    return g * cos + pltpu.roll(g * sin, half, 1)


def _rms_fwd(x2, g, name, dep=None):
    rows = x2.shape[0]
    tr = 256
    deps = [] if dep is None else list(dep)

    def body(x_ref, g_ref, *rest):
        o_ref = rest[-1]
        x = x_ref[...]
        r = lax.rsqrt(jnp.mean(x * x, axis=-1, keepdims=True) + EPS)
        o_ref[...] = (x * r * g_ref[...]).astype(BF16)

    return pl.pallas_call(
        body, name=name, grid=(rows // tr,),
        in_specs=[pl.BlockSpec((tr, D), lambda i: (i, 0)), pl.BlockSpec((1, D), lambda i: (0, 0))]
        + [pl.BlockSpec(memory_space=pl.ANY)] * len(deps),
        out_specs=pl.BlockSpec((tr, D), lambda i: (i, 0)),
        out_shape=jax.ShapeDtypeStruct((rows, D), BF16),
        compiler_params=_cp(("parallel",)),
    )(x2, g, *deps)


def _norm_gain_grad(dn, x2, name):
    rows = x2.shape[0]
    tr = 256

    def body(dn_ref, x_ref, dg_ref):
        @pl.when(pl.program_id(0) == 0)
        def _():
            dg_ref[...] = jnp.zeros_like(dg_ref)
        x = x_ref[...]
        r = lax.rsqrt(jnp.mean(x * x, axis=-1, keepdims=True) + EPS)
        dg_ref[...] += jnp.sum(dn_ref[...] * (x * r), axis=0, keepdims=True)

    return pl.pallas_call(
        body, name=name, grid=(rows // tr,),
        in_specs=[pl.BlockSpec((tr, D), lambda i: (i, 0)), pl.BlockSpec((tr, D), lambda i: (i, 0))],
        out_specs=pl.BlockSpec((1, D), lambda i: (0, 0)),
        out_shape=jax.ShapeDtypeStruct((1, D), F32),
        compiler_params=_cp(("arbitrary",)),
    )(dn, x2)


def _pre_norm_bwd(dh, x2, g, dout):
    tr = 256

    def body(dh_ref, x_ref, g_ref, dout_ref, gx_ref, dg_ref):
        @pl.when(pl.program_id(0) == 0)
        def _():
            dg_ref[...] = jnp.zeros_like(dg_ref)
        x = x_ref[...]
        dh_ = dh_ref[...].astype(F32)
        r = lax.rsqrt(jnp.mean(x * x, axis=-1, keepdims=True) + EPS)
        xhat = x * r
        dg_ref[...] += jnp.sum(dh_ * xhat, axis=0, keepdims=True)
        dxn = dh_ * g_ref[...]
        gx_ref[...] = dout_ref[...].astype(F32) + r * (dxn - xhat * jnp.mean(dxn * xhat, axis=-1, keepdims=True))

    row = pl.BlockSpec((tr, D), lambda i: (i, 0))
    vec = pl.BlockSpec((1, D), lambda i: (0, 0))
    return pl.pallas_call(
        body, name="pre_norm_bwd", grid=(T // tr,),
        in_specs=[row, row, vec, row],
        out_specs=[row, vec],
        out_shape=[jax.ShapeDtypeStruct((T, D), F32), jax.ShapeDtypeStruct((1, D), F32)],
        compiler_params=_cp(("arbitrary",)),
    )(dh, x2, g, dout)


def _post_norm_loss(y, x2, tgt, g):
    tr = 256

    def body(y_ref, x_ref, t_ref, g_ref, dy_ref, dout_ref, dg_ref, loss_ref):
        @pl.when(pl.program_id(0) == 0)
        def _():
            dg_ref[...] = jnp.zeros_like(dg_ref)
            loss_ref[...] = jnp.zeros_like(loss_ref)
        yv = y_ref[...]
        gv = g_ref[...]
        r = lax.rsqrt(jnp.mean(yv * yv, axis=-1, keepdims=True) + EPS)
        yhat = yv * r
        err = x_ref[...] + yhat * gv - t_ref[...]
        loss_ref[...] += jnp.sum(jnp.sum(err * err, axis=1, keepdims=True), axis=0, keepdims=True) * (0.5 / D)
        dout = err * (1.0 / D)
        dout_ref[...] = dout.astype(BF16)
        dg_ref[...] += jnp.sum(dout * yhat, axis=0, keepdims=True)
        dyn = dout * gv
        dy_ref[...] = (r * (dyn - yhat * jnp.mean(dyn * yhat, axis=-1, keepdims=True))).astype(BF16)

    row = pl.BlockSpec((tr, D), lambda i: (i, 0))
    vec = pl.BlockSpec((1, D), lambda i: (0, 0))
    return pl.pallas_call(
        body, name="post_norm_loss", grid=(T // tr,),
        in_specs=[row, row, row, vec],
        out_specs=[row, row, vec, pl.BlockSpec((1, 128), lambda i: (0, 0))],
        out_shape=[jax.ShapeDtypeStruct((T, D), BF16), jax.ShapeDtypeStruct((T, D), BF16),
                   jax.ShapeDtypeStruct((1, D), F32), jax.ShapeDtypeStruct((1, 128), F32)],
        compiler_params=_cp(("arbitrary",)),
    )(y, x2, tgt, g)


NN = (((1,), (0,)), ((), ()))
NT = (((1,), (1,)), ((), ()))
TN = (((0,), (0,)), ((), ()))


def _as_index(v):
    return jnp.reshape(v, (1,)).astype(jnp.int32)


def _matmul(a, b, *, name, dims, grid, a_block, a_map, b_block, b_map, o_block, o_map, out_shape, out_dtype=F32,
            index=None, prev=None, deps=()):
    extra = ([] if prev is None else [prev]) + [d for d in deps if d is not None]
    first = 0 if index is None else 1
    nk = grid[2]
    in_place = out_dtype == F32

    def body(*refs):
        a_ref, b_ref, o_ref = refs[first], refs[first + 1], refs[first + 2 + len(extra)]
        acc_ref = o_ref if in_place else refs[-1]

        @pl.when(pl.program_id(2) == 0)
        def _():
            acc_ref[...] = lax.dot_general(a_ref[...], b_ref[...], dims, preferred_element_type=F32)

        @pl.when(pl.program_id(2) > 0)
        def _():
            acc_ref[...] += lax.dot_general(a_ref[...], b_ref[...], dims, preferred_element_type=F32)

        if not in_place:
            @pl.when(pl.program_id(2) == nk - 1)
            def _():
                o_ref[...] = acc_ref[...].astype(o_ref.dtype)

    in_specs = [pl.BlockSpec(a_block, a_map), pl.BlockSpec(b_block, b_map)] + [pl.BlockSpec(memory_space=pl.ANY)] * len(extra)
    out_specs = pl.BlockSpec(o_block, o_map)
    scratch = [] if in_place else [pltpu.VMEM(o_block, F32)]
    kwargs = dict(name=name, out_shape=jax.ShapeDtypeStruct(out_shape, out_dtype),
                  input_output_aliases={} if prev is None else {first + 2: 0},
                  compiler_params=_cp(("parallel", "parallel", "arbitrary"), vmem_mib=56))
    if index is None:
        return pl.pallas_call(body, grid=grid, in_specs=in_specs, out_specs=out_specs, scratch_shapes=scratch,
                              **kwargs)(a, b, *extra)
    grid_spec = pltpu.PrefetchScalarGridSpec(num_scalar_prefetch=1, grid=grid, in_specs=in_specs, out_specs=out_specs,
                                             scratch_shapes=scratch)
    return pl.pallas_call(body, grid_spec=grid_spec, **kwargs)(_as_index(index), a, b, *extra)


def _mm_nn(a, b, name, tm, tn, tk):
    m, kd = a.shape
    n = b.shape[1]
    return _matmul(a, b, name=name, dims=NN, grid=(m // tm, n // tn, kd // tk),
                   a_block=(tm, tk), a_map=lambda i, j, k: (i, k),
                   b_block=(tk, tn), b_map=lambda i, j, k: (k, j),
                   o_block=(tm, tn), o_map=lambda i, j, k: (i, j), out_shape=(m, n))


def _mm_nt(a, b, name, tm, tn, tk):
    m, kd = a.shape
    n = b.shape[0]
    return _matmul(a, b, name=name, dims=NT, grid=(m // tm, n // tn, kd // tk),
                   a_block=(tm, tk), a_map=lambda i, j, k: (i, k),
                   b_block=(tn, tk), b_map=lambda i, j, k: (j, k),
                   o_block=(tm, tn), o_map=lambda i, j, k: (i, j), out_shape=(m, n))


W_TN = 1792
W_NJ = WB // W_TN


def _proj_part(h, wg, chip, masks, name, prev=None, dep=None):
    tm, tk = 1024, 2048

    def blk(j, ix):
        m = masks[0]
        for t in range(1, len(masks)):
            m = jnp.where(j // W_NJ == t, masks[t], m)
        return jnp.bitwise_xor(ix[0], m)

    return _matmul(h, wg, name=name, dims=NN, grid=(T // tm, len(masks) * W_NJ, D // tk), index=chip, prev=prev, deps=(dep,),
                   a_block=(tm, tk), a_map=lambda i, j, k, ix: (i, k),
                   b_block=(tk, W_TN), b_map=lambda i, j, k, ix: (blk(j, ix) * (D // tk) + k, j % W_NJ),
                   o_block=(tm, W_TN), o_map=lambda i, j, k, ix: (i, blk(j, ix) * W_NJ + j % W_NJ), out_shape=(T, NC))


def _adamw_math(w_ref, g_ref, m_ref, v_ref, go_ref, d_ref, nm_ref, nv_ref):
    gv = g_ref[...]
    go_ref[...] = gv
    nm = ADAM_B1 * m_ref[...] + (1.0 - ADAM_B1) * gv
    nv = ADAM_B2 * v_ref[...] + (1.0 - ADAM_B2) * (gv * gv)
    m_hat = nm / (1.0 - ADAM_B1 ** ADAM_STEP)
    v_hat = nv / (1.0 - ADAM_B2 ** ADAM_STEP)
    d_ref[...] = -ADAM_LR * (m_hat / (jnp.sqrt(v_hat) + ADAM_EPS) + ADAM_WD * w_ref[...])
    nm_ref[...] = nm
    nv_ref[...] = nv


def _dh(dproj, wg, dep=None, adam=()):
    tm, tn = 1024, 1024
    grid = (T // tm, D // tn, NC // W_TN)
    nsteps = grid[0] * grid[1] * grid[2]
    deps = [] if dep is None else [dep]
    na = len(adam)
    nk = grid[2]

    def body(*refs):
        a_ref, b_ref = refs[0], refs[1]
        o_ref = refs[2 + 4 * na + len(deps)]
        acc_ref = refs[-1]

        @pl.when(pl.program_id(2) == 0)
        def _():
            acc_ref[...] = lax.dot_general(a_ref[...], b_ref[...], NT, preferred_element_type=F32)

        @pl.when(pl.program_id(2) > 0)
        def _():
            acc_ref[...] += lax.dot_general(a_ref[...], b_ref[...], NT, preferred_element_type=F32)

        @pl.when(pl.program_id(2) == nk - 1)
        def _():
            o_ref[...] = acc_ref[...].astype(BF16)

        for s in range(na):
            ins = refs[2 + 4 * s:6 + 4 * s]
            outs = refs[3 + 4 * na + len(deps) + 4 * s:7 + 4 * na + len(deps) + 4 * s]
            _adamw_math(*ins, *outs)

    def rows_of(arr):
        r, c = arr.shape
        return pl.BlockSpec((r // nsteps, c), lambda i, j, k: ((i * grid[1] + j) * grid[2] + k, 0))

    adam_specs = [rows_of(a) for st in adam for a in st]
    res = pl.pallas_call(
        body, name="dh", grid=grid,
        in_specs=[pl.BlockSpec((tm, W_TN), lambda i, j, k: (i, k)),
                  pl.BlockSpec((tn, W_TN), lambda i, j, k: ((k // W_NJ) * (D // tn) + j, k % W_NJ))]
        + adam_specs + [pl.BlockSpec(memory_space=pl.ANY)] * len(deps),
        out_specs=[pl.BlockSpec((tm, tn), lambda i, j, k: (i, j))] + adam_specs,
        out_shape=[jax.ShapeDtypeStruct((T, D), BF16)] + [jax.ShapeDtypeStruct(a.shape, F32) for st in adam for a in st],
        scratch_shapes=[pltpu.VMEM((tm, tn), F32)],
        compiler_params=_cp(("arbitrary",) * 3, vmem_mib=56),
    )(dproj, wg, *[a for st in adam for a in st], *deps)
    return res[0], [tuple(res[1 + 4 * s:5 + 4 * s]) for s in range(na)]


def _grad_rows(a, b, half, out_dtype, name, dep=None):
    kd, n = b.shape
    tm, tn, tk = D // NCHIP // 2, min(n, 2048), min(kd, 2048)
    return _matmul(a, b, name=name, dims=TN, grid=(NCHIP, n // tn, kd // tk), index=half, deps=(dep,),
                   a_block=(tk, tm), a_map=lambda i, j, k, ix: (k, 2 * i + ix[0]),
                   b_block=(tk, tn), b_map=lambda i, j, k, ix: (k, j),
                   o_block=(tm, tn), o_map=lambda i, j, k, ix: (i, j),
                   out_shape=(NCHIP * tm, n), out_dtype=out_dtype)


def _grad_w_in(h, dproj, half, out_dtype, name, dep=None):
    tm, tk = 1024, 2048
    nh = D // 2 // tm
    return _matmul(h, dproj, name=name, dims=TN, grid=(nh, NC // W_TN, T // tk), index=half, deps=(dep,),
                   a_block=(tk, tm), a_map=lambda i, j, k, ix: (k, ix[0] * nh + i),
                   b_block=(tk, W_TN), b_map=lambda i, j, k, ix: (k, j),
                   o_block=(tm, W_TN), o_map=lambda i, j, k, ix: ((j // W_NJ) * nh + i, j % W_NJ),
                   out_shape=(NCHIP * D // 2, WB), out_dtype=out_dtype)


def _rope_tables(pos, half):
    inv = 1.0 / (ROPE_THETA ** (jnp.arange(half, dtype=F32) / half))
    ang = pos.astype(F32)[:, None] * inv[None, :]
    cos, sin = jnp.cos(ang), jnp.sin(ang)
    return jnp.concatenate([cos, cos], axis=1), jnp.concatenate([-sin, sin], axis=1)


def _band_mask(r0):
    qi = lax.broadcasted_iota(jnp.int32, (128, 256), 0)
    kk = lax.broadcasted_iota(jnp.int32, (128, 256), 1)
    return (kk >= qi) & (kk <= qi + 128) & (kk + r0 >= 128)


def _window(r0, nblk):
    if nblk == 1:
        qi = lax.broadcasted_iota(jnp.int32, (128, 128), 0)
        kk = lax.broadcasted_iota(jnp.int32, (128, 128), 1)
        return pl.ds(128, 128), kk <= qi
    return pl.ds(r0, 256), _band_mask(r0)


def _dil_rows(r, n, d):
    if d == 1:
        return pl.ds(pl.multiple_of(n * 128, 128), 128)
    return pl.ds(r + d * 128 * n, 128, stride=d)


def _attn_fwd(proj, cosf, sinf, g):
    d = DIL[g]
    ln = S // d
    nblk = ln // 128
    proj_v = proj.reshape(NB, S, NC)

    def body(q_ref, k_ref, v_ref, cos_ref, sin_ref, o_ref, l_ref, k_s, v_s):
        k_s[:, pl.ds(0, 128), :] = jnp.zeros((d, 128, HD), BF16)
        v_s[:, pl.ds(0, 128), :] = jnp.zeros((d, 128, HD), BF16)

        def prep(i, carry):
            r, n = i // nblk, i % nblk
            rows = _dil_rows(r, n, d)
            dst = pl.ds(pl.multiple_of(n * 128 + 128, 128), 128)
            k_s[r, dst, :] = _rope(k_ref[rows, :], cos_ref[rows, :], sin_ref[rows, :], HD // 2).astype(BF16)
            v_s[r, dst, :] = v_ref[rows, :].astype(BF16)
            return carry

        lax.fori_loop(0, d * nblk, prep, 0, unroll=4)

        def step(i, carry):
            r, n = i // nblk, i % nblk
            rows = _dil_rows(r, n, d)
            r0 = pl.multiple_of(n * 128, 128)
            qr = _rope(q_ref[rows, :], cos_ref[rows, :], sin_ref[rows, :], HD // 2).astype(BF16)
            win, mask = _window(r0, nblk)
            kw = k_s[r, win, :]
            vw = v_s[r, win, :]
            sc = lax.dot_general(qr, kw, NT, preferred_element_type=F32) * A_SCALE
            sc = jnp.where(mask, sc, NEG)
            m = jnp.max(sc, axis=1, keepdims=True)
            p = jnp.exp(sc - m)
            l = jnp.sum(p, axis=1, keepdims=True)
            o_ref[rows, :] = jnp.dot(p.astype(BF16), vw, preferred_element_type=F32) / l
            l_ref[rows, :] = jnp.broadcast_to(m + jnp.log(l), (128, HD))
            return carry

        lax.fori_loop(0, d * nblk, step, 0, unroll=4)

    def col(off):
        return lambda b, h: (b, 0, off // HD + HPG * g + h)

    blk = (None, S, HD)
    tab = pl.BlockSpec((S, HD), lambda b, h: (0, 0))
    out = pl.BlockSpec(blk, lambda b, h: (b, 0, h))
    kv = pl.BlockSpec((None, None, d, ln + 128, HD), lambda b, h: (b, h, 0, 0, 0))
    o, l, kr, vr = pl.pallas_call(
        body, name=f"attn_fwd_d{d}", grid=(NB, HPG),
        in_specs=[pl.BlockSpec(blk, col(QA)), pl.BlockSpec(blk, col(KA)), pl.BlockSpec(blk, col(VA)), tab, tab],
        out_specs=[out, out, kv, kv],
        out_shape=[jax.ShapeDtypeStruct((NB, S, HPG * HD), F32)] * 2
        + [jax.ShapeDtypeStruct((NB, HPG, d, ln + 128, HD), BF16)] * 2,
        compiler_params=_cp(("parallel", "parallel")),
    )(proj_v, proj_v, proj_v, cosf, sinf)
    return o.reshape(T, HPG * HD), l.reshape(T, HPG * HD), kr, vr


def _attn_bwd(proj, kr, vr, cosf, sinf, da, lse, delta, dproj, g):
    d = DIL[g]
    ln = S // d
    nblk = ln // 128
    proj_v = proj.reshape(NB, S, NC)
    dproj_v = dproj.reshape(NB, S, NC)
    da_v = da.reshape(NB, S, AW)
    lse_v = lse.reshape(NB, S, HPG * HD)
    delta_v = delta.reshape(NB, S, HPG * HD)

    def body(q_ref, k_s, v_s, cos_ref, sin_ref, da_ref, lse_ref, dl_ref, dp_in_ref, o_ref, stg, dk_s, dv_s):
        del dp_in_ref
        w = pl.program_id(2)

        def emit():
            def cast(n, carry):
                rows = pl.ds(pl.multiple_of(n * 256, 256), 256)
                o_ref[rows, :] = stg[rows, :].astype(BF16)
                return carry

            lax.fori_loop(0, S // 256, cast, 0)

        @pl.when(w == 0)
        def _():
            dk_s[...] = jnp.zeros_like(dk_s)
            dv_s[...] = jnp.zeros_like(dv_s)

            def step(i, carry):
                r, n = i // nblk, i % nblk
                rows = _dil_rows(r, n, d)
                r0 = pl.multiple_of(n * 128, 128)
                win, mask = _window(r0, nblk)
                cos, sin = cos_ref[rows, :], sin_ref[rows, :]
                qr = _rope(q_ref[rows, :], cos, sin, HD // 2).astype(BF16)
                kw = k_s[r, win, :]
                vw = v_s[r, win, :]
                sc = lax.dot_general(qr, kw, NT, preferred_element_type=F32) * A_SCALE
                sc = jnp.where(mask, sc, NEG)
                p = jnp.exp(sc - lse_ref[rows, :][:, :1])
                da_b = da_ref[rows, :].astype(BF16)
                dp = lax.dot_general(da_b, vw, NT, preferred_element_type=F32)
                ds_b = (p * (dp - dl_ref[rows, :][:, :1]) * A_SCALE).astype(BF16)
                p_b = p.astype(BF16)
                dq = jnp.dot(ds_b, kw, preferred_element_type=F32)
                stg[rows, :] = _rope_t(dq, cos, sin, HD // 2)
                dk_s[r, win, :] += lax.dot_general(ds_b, qr, TN, preferred_element_type=F32)
                dv_s[r, win, :] += lax.dot_general(p_b, da_b, TN, preferred_element_type=F32)
                return carry

            lax.fori_loop(0, d * nblk, step, 0, unroll=4)
            emit()

        @pl.when(w == 1)
        def _():
            def put(i, carry):
                r, n = i // nblk, i % nblk
                rows = _dil_rows(r, n, d)
                src = pl.ds(pl.multiple_of(n * 128 + 128, 128), 128)
                stg[rows, :] = _rope_t(dk_s[r, src, :], cos_ref[rows, :], sin_ref[rows, :], HD // 2)
                return carry

            lax.fori_loop(0, d * nblk, put, 0, unroll=4)
            emit()

        @pl.when(w == 2)
        def _():
            def put(i, carry):
                r, n = i // nblk, i % nblk
                src = pl.ds(pl.multiple_of(n * 128 + 128, 128), 128)
                stg[_dil_rows(r, n, d), :] = dv_s[r, src, :]
                return carry

            lax.fori_loop(0, d * nblk, put, 0, unroll=4)
            emit()

    def col(off):
        return lambda b, h, w: (ahead(b, h, w)[0], 0, off // HD + HPG * g + ahead(b, h, w)[1])

    def ahead(b, h, w):
        flat = jnp.minimum(b * HPG + h + jnp.where(w > 0, 1, 0), NB * HPG - 1)
        return flat // HPG, flat % HPG

    blk = (None, S, HD)
    tab = pl.BlockSpec((S, HD), lambda b, h, w: (0, 0))
    per_head = pl.BlockSpec(blk, lambda b, h, w: (ahead(b, h, w)[0], 0, ahead(b, h, w)[1]))
    kv = pl.BlockSpec((None, None, d, ln + 128, HD), lambda b, h, w: (*ahead(b, h, w), 0, 0, 0))
    out = pl.pallas_call(
        body, name=f"attn_bwd_d{d}", grid=(NB, HPG, 3),
        in_specs=[pl.BlockSpec(blk, col(QA)), kv, kv, tab, tab,
                  pl.BlockSpec(blk, col(0)), per_head, per_head, pl.BlockSpec(memory_space=pl.ANY)],
        out_specs=pl.BlockSpec(blk, lambda b, h, w: (b, 0, (AW // HD) * w + HPG * g + h)),
        out_shape=jax.ShapeDtypeStruct(dproj_v.shape, BF16),
        input_output_aliases={8: 0},
        scratch_shapes=[pltpu.VMEM((S, HD), F32), pltpu.VMEM((d, ln + 128, HD), F32), pltpu.VMEM((d, ln + 128, HD), F32)],
        compiler_params=_cp(("arbitrary",) * 3),
    )(proj_v, kr, vr, cosf, sinf, da_v, lse_v, delta_v, dproj_v)
    return out.reshape(T, NC)


def _attn_mix(proj, os_, ls_, dep=None):
    tr = 256
    gw = HPG * HD
    deps = [] if dep is None else [dep]

    def body(o0, o1, o2, l0, l1, l2, z_ref, *rest):
        cat_ref = rest[-1]
        m = jnp.maximum(jnp.maximum(l0[...], l1[...]), l2[...])
        e = [jnp.exp(l[...] - m) for l in (l0, l1, l2)]
        inv = 1.0 / (e[0] + e[1] + e[2])
        for gi, o in enumerate((o0, o1, o2)):
            z = z_ref[:, gi * gw:(gi + 1) * gw]
            cat_ref[:, gi * gw:(gi + 1) * gw] = (o[...] * (e[gi] * inv) * (z * _sigmoid(z))).astype(BF16)

    grp = pl.BlockSpec((tr, gw), lambda i: (i, 0))
    return pl.pallas_call(
        body, name="attn_mix", grid=(T // tr,),
        in_specs=[grp] * 6 + [pl.BlockSpec((tr, AW), lambda i: (i, ZA // AW))] + [ANY_SPEC] * len(deps),
        out_specs=pl.BlockSpec((tr, AW), lambda i: (i, 0)),
        out_shape=jax.ShapeDtypeStruct((T, D), BF16),
        compiler_params=_cp(("parallel",)),
    )(*os_, *ls_, proj, *deps)


def _attn_mix_bwd(dcat, proj, os_, ls_, dep=None):
    tr = 256
    gw = HPG * HD
    deps = [] if dep is None else [dep]

    def body(dy_ref, o0, o1, o2, l0, l1, l2, z_ref, *rest):
        da_ref, lse_ref, dl_ref, dz_ref = rest[len(deps):]
        m = jnp.maximum(jnp.maximum(l0[...], l1[...]), l2[...])
        e = [jnp.exp(l[...] - m) for l in (l0, l1, l2)]
        den = e[0] + e[1] + e[2]
        inv = 1.0 / den
        lse_ref[...] = m + jnp.log(den)
        acc = jnp.zeros((tr, gw), F32)
        for gi, o in enumerate((o0, o1, o2)):
            cols = slice(gi * gw, (gi + 1) * gw)
            z = z_ref[:, cols]
            dy = dy_ref[:, cols]
            sg = _sigmoid(z)
            a = o[...] * (e[gi] * inv)
            da = dy * (z * sg)
            da_ref[:, cols] = da
            dz_ref[:, cols] = (dy * a * (sg * (1.0 + z * (1.0 - sg)))).astype(BF16)
            acc = acc + da * a
        for hh in range(HPG):
            cols = slice(hh * HD, (hh + 1) * HD)
            dl_ref[:, cols] = jnp.broadcast_to(jnp.sum(acc[:, cols], axis=1, keepdims=True), (tr, HD))

    grp = pl.BlockSpec((tr, gw), lambda i: (i, 0))
    return pl.pallas_call(
        body, name="attn_mix_bwd", grid=(T // tr,),
        in_specs=[pl.BlockSpec((tr, AW), lambda i: (i, 0))] + [grp] * 6 + [pl.BlockSpec((tr, AW), lambda i: (i, ZA // AW))]
        + [pl.BlockSpec(memory_space=pl.ANY)] * len(deps),
        out_specs=[pl.BlockSpec((tr, AW), lambda i: (i, 0)), grp, grp, pl.BlockSpec((tr, AW), lambda i: (i, ZA // AW))],
        out_shape=[jax.ShapeDtypeStruct((T, AW), F32), jax.ShapeDtypeStruct((T, gw), F32),
                   jax.ShapeDtypeStruct((T, gw), F32), jax.ShapeDtypeStruct((T, NC), BF16)],
        compiler_params=_cp(("parallel",)),
    )(dcat, *os_, *ls_, proj, *deps)


CT = 256


def _shift_down(x, n):
    rows = lax.broadcasted_iota(jnp.int32, x.shape, 0)
    return jnp.where(rows >= n, pltpu.roll(x, n, 0), 0.0)


def _shift_up(x, n):
    rows = lax.broadcasted_iota(jnp.int32, x.shape, 0)
    return jnp.where(rows < x.shape[0] - n, pltpu.roll(x, x.shape[0] - n, 0), 0.0)


def _conv_fwd(proj, conv_w, cat):
    proj_v = proj.reshape(NB, S, NC)
    cat_v = cat.reshape(NB, S, D)

    def body(u_ref, b_ref, c_ref, z_ref, w_ref, cat_in, o_ref):
        del cat_in
        cu = c_ref[...] * u_ref[...]
        cv = _shift_down(cu, 2) * w_ref[0:1, :] + _shift_down(cu, 1) * w_ref[1:2, :] + cu * w_ref[2:3, :]
        z = z_ref[...]
        o_ref[...] = (b_ref[...] * cv * (z * _sigmoid(z))).astype(BF16)

    def seg(off):
        return pl.BlockSpec((None, S, CT), lambda b, j: (b, 0, off // CT + j))

    out = pl.pallas_call(
        body, name="conv_fwd", grid=(NB, CW // CT),
        in_specs=[seg(UC), seg(BC), seg(CC), seg(ZC), pl.BlockSpec((3, CT), lambda b, j: (0, j)),
                  pl.BlockSpec(memory_space=pl.ANY)],
        out_specs=pl.BlockSpec((None, S, CT), lambda b, j: (b, 0, AW // CT + j)),
        out_shape=jax.ShapeDtypeStruct((NB, S, D), BF16),
        input_output_aliases={5: 0},
        compiler_params=_cp(("parallel", "parallel")),
    )(proj_v, proj_v, proj_v, proj_v, conv_w, cat_v)
    return out.reshape(T, D)


def _conv_bwd(dcat, proj, conv_w, dproj, dep=None):
    deps = [] if dep is None else [dep]
    proj_v = proj.reshape(NB, S, NC)
    dproj_v = dproj.reshape(NB, S, NC)
    dcat_v = dcat.reshape(NB, S, D)

    def body(dy_ref, u_ref, b_ref, c_ref, z_ref, w_ref, *rest):
        o_ref, dw_ref, st = rest[1 + len(deps):]
        b = pl.program_id(1)
        w = pl.program_id(2)

        @pl.when((b == 0) & (w == 0))
        def _():
            dw_ref[...] = jnp.zeros_like(dw_ref)

        @pl.when(w == 0)
        def _():
            u, c, z, bb, dy = u_ref[...], c_ref[...], z_ref[...], b_ref[...], dy_ref[...]
            cu = c * u
            s1 = _shift_down(cu, 1)
            s2 = _shift_down(cu, 2)
            cv = s2 * w_ref[0:1, :] + s1 * w_ref[1:2, :] + cu * w_ref[2:3, :]
            sg = _sigmoid(z)
            sz = z * sg
            dcv = dy * bb * sz
            st[0] = dy * cv * sz
            st[2] = dy * bb * cv * (sg * (1.0 + z * (1.0 - sg)))
            dw_ref[0:1, :] += jnp.sum(dcv * s2, axis=0, keepdims=True)
            dw_ref[1:2, :] += jnp.sum(dcv * s1, axis=0, keepdims=True)
            dw_ref[2:3, :] += jnp.sum(dcv * cu, axis=0, keepdims=True)
            dcu = dcv * w_ref[2:3, :] + _shift_up(dcv, 1) * w_ref[1:2, :] + _shift_up(dcv, 2) * w_ref[0:1, :]
            st[1] = dcu * u
            o_ref[...] = (dcu * c).astype(BF16)

        for k in range(3):
            @pl.when(w == k + 1)
            def _(k=k):
                o_ref[...] = st[k].astype(BF16)

    def ahead(j, b, w):
        flat = jnp.minimum(j * NB + b + jnp.where(w > 0, 1, 0), (CW // CT) * NB - 1)
        return flat // NB, flat % NB

    def seg(off):
        return pl.BlockSpec((None, S, CT), lambda j, b, w: (ahead(j, b, w)[1], 0, off // CT + ahead(j, b, w)[0]))

    out, dw = pl.pallas_call(
        body, name="conv_bwd", grid=(CW // CT, NB, 4),
        in_specs=[seg(AW), seg(UC), seg(BC), seg(CC), seg(ZC),
                  pl.BlockSpec((3, CT), lambda j, b, w: (0, ahead(j, b, w)[0])),
                  pl.BlockSpec(memory_space=pl.ANY)] + [pl.BlockSpec(memory_space=pl.ANY)] * len(deps),
        out_specs=[pl.BlockSpec((None, S, CT), lambda j, b, w: (b, 0, (UC + w * CW) // CT + j)),
                   pl.BlockSpec((3, CT), lambda j, b, w: (0, j))],
        out_shape=[jax.ShapeDtypeStruct((NB, S, NC), BF16), jax.ShapeDtypeStruct((3, CW), F32)],
        input_output_aliases={6: 0},
        scratch_shapes=[pltpu.VMEM((3, S, CT), F32)],
        compiler_params=_cp(("arbitrary",) * 3),
    )(dcat_v, proj_v, proj_v, proj_v, proj_v, conv_w, dproj_v, *deps)
    return out.reshape(T, NC), dw


XT = 1024


def _cross_fwd(proj, mkv, cosq, sinq, cosm, sinm, cat):
    proj_v = proj.reshape(NB, S, NC)
    mkv_v = mkv.reshape(NB, MLEN, 2 * XW)
    cat_v = cat.reshape(NB, S, D)

    def body(q_ref, z_ref, mk_ref, mv_ref, cq, sq, cm, sm, cat_in, o_ref):
        del cat_in
        mkr = _rope(mk_ref[...], cm[...], sm[...], XHD // 2).astype(BF16)
        qr = _rope(q_ref[...], cq[...], sq[...], XHD // 2).astype(BF16)
        sc = lax.dot_general(qr, mkr, NT, preferred_element_type=F32) * X_SCALE
        p = jnp.exp(sc - jnp.max(sc, axis=1, keepdims=True))
        p = p / jnp.sum(p, axis=1, keepdims=True)
        ox = jnp.dot(p.astype(BF16), mv_ref[...].astype(BF16), preferred_element_type=F32)
        z = z_ref[...]
        o_ref[...] = (ox * (z * _sigmoid(z))).astype(BF16)

    def seg(off):
        return pl.BlockSpec((None, XT, XHD), lambda b, h, t: (b, t, off // XHD + h))

    qtab = pl.BlockSpec((XT, XHD), lambda b, h, t: (t, 0))
    mtab = pl.BlockSpec((MLEN, XHD), lambda b, h, t: (0, 0))
    out = pl.pallas_call(
        body, name="cross_fwd", grid=(NB, NXH, S // XT),
        in_specs=[seg(QX), seg(ZX),
                  pl.BlockSpec((None, MLEN, XHD), lambda b, h, t: (b, 0, h)),
                  pl.BlockSpec((None, MLEN, XHD), lambda b, h, t: (b, 0, NXH + h)),
                  qtab, qtab, mtab, mtab, pl.BlockSpec(memory_space=pl.ANY)],
        out_specs=pl.BlockSpec((None, XT, XHD), lambda b, h, t: (b, t, (AW + CW) // XHD + h)),
        out_shape=jax.ShapeDtypeStruct((NB, S, D), BF16),
        input_output_aliases={8: 0},
        compiler_params=_cp(("parallel",) * 3),
    )(proj_v, proj_v, mkv_v, mkv_v, cosq, sinq, cosm, sinm, cat_v)
    return out.reshape(T, D)


def _cross_bwd(dcat, proj, mkv, cosq, sinq, cosm, sinm, dproj):
    proj_v = proj.reshape(NB, S, NC)
    dproj_v = dproj.reshape(NB, S, NC)
    dcat_v = dcat.reshape(NB, S, D)
    mkv_v = mkv.reshape(NB, MLEN, 2 * XW)
    nt = S // XT

    def body(dy_ref, q_ref, z_ref, mk_ref, mv_ref, cq, sq, cm, sm, dp_in, o_ref, dmk_ref, dmv_ref, dz_s):
        del dp_in
        t = pl.program_id(2)
        w = pl.program_id(3)

        @pl.when((t == 0) & (w == 0))
        def _():
            dmk_ref[...] = jnp.zeros_like(dmk_ref)
            dmv_ref[...] = jnp.zeros_like(dmv_ref)

        @pl.when(w == 0)
        def _():
            mkr = _rope(mk_ref[...], cm[...], sm[...], XHD // 2).astype(BF16)
            mv_b = mv_ref[...].astype(BF16)
            qr = _rope(q_ref[...], cq[...], sq[...], XHD // 2).astype(BF16)
            sc = lax.dot_general(qr, mkr, NT, preferred_element_type=F32) * X_SCALE
            p = jnp.exp(sc - jnp.max(sc, axis=1, keepdims=True))
            p = p / jnp.sum(p, axis=1, keepdims=True)
            p_b = p.astype(BF16)
            ox = jnp.dot(p_b, mv_b, preferred_element_type=F32)
            z = z_ref[...]
            dy = dy_ref[...]
            sg = _sigmoid(z)
            dz_s[...] = dy * ox * (sg * (1.0 + z * (1.0 - sg)))
            dox_b = (dy * (z * sg)).astype(BF16)
            dp = lax.dot_general(dox_b, mv_b, NT, preferred_element_type=F32)
            ds_b = (p * (dp - jnp.sum(dp * p, axis=1, keepdims=True)) * X_SCALE).astype(BF16)
            dq = jnp.dot(ds_b, mkr, preferred_element_type=F32)
            o_ref[...] = _rope_t(dq, cq[...], sq[...], XHD // 2).astype(BF16)
            dmk_ref[...] += lax.dot_general(ds_b, qr, TN, preferred_element_type=F32)
            dmv_ref[...] += lax.dot_general(p_b, dox_b, TN, preferred_element_type=F32)

        @pl.when(w == 1)
        def _():
            o_ref[...] = dz_s[...].astype(BF16)

        @pl.when((t == nt - 1) & (w == 1))
        def _():
            dmk_ref[...] = _rope_t(dmk_ref[...], cm[...], sm[...], XHD // 2)

    def seg(off):
        return pl.BlockSpec((None, XT, XHD), lambda b, h, t, w: (b, t, off // XHD + h))

    qtab = pl.BlockSpec((XT, XHD), lambda b, h, t, w: (t, 0))
    mtab = pl.BlockSpec((MLEN, XHD), lambda b, h, t, w: (0, 0))
    macc = pl.BlockSpec((None, MLEN, XHD), lambda b, h, t, w: (b, 0, h))
    out, dmk, dmv = pl.pallas_call(
        body, name="cross_bwd", grid=(NB, NXH, nt, 2),
        in_specs=[pl.BlockSpec((None, XT, XHD), lambda b, h, t, w: (b, t, (AW + CW) // XHD + h)),
                  seg(QX), seg(ZX),
                  pl.BlockSpec((None, MLEN, XHD), lambda b, h, t, w: (b, 0, h)),
                  pl.BlockSpec((None, MLEN, XHD), lambda b, h, t, w: (b, 0, NXH + h)),
                  qtab, qtab, mtab, mtab, pl.BlockSpec(memory_space=pl.ANY)],
        out_specs=[pl.BlockSpec((None, XT, XHD), lambda b, h, t, w: (b, t, (QX + w * XW) // XHD + h)), macc, macc],
        out_shape=[jax.ShapeDtypeStruct((NB, S, NC), BF16), jax.ShapeDtypeStruct((NB, MLEN, XW), F32),
                   jax.ShapeDtypeStruct((NB, MLEN, XW), F32)],
        input_output_aliases={9: 0},
        scratch_shapes=[pltpu.VMEM((XT, XHD), F32)],
        compiler_params=_cp(("arbitrary",) * 4),
    )(dcat_v, proj_v, proj_v, mkv_v, mkv_v, cosq, sinq, cosm, sinm, dproj_v)
    return out.reshape(T, NC), dmk, dmv


def _local_step(x, mem, pre_norm, conv_w, mem_norm, post_norm, tgt, chip, core, comm, moments=None):
    x2 = x.reshape(T, D)
    mem2 = mem.reshape(NB * MLEN, D)
    tgt2 = tgt.reshape(T, D)
    cosa, sina = _rope_tables(jnp.arange(S), HD // 2)
    cosq, sinq = _rope_tables(jnp.arange(S) + MLEN, XHD // 2)
    cosm, sinm = _rope_tables(jnp.arange(MLEN), XHD // 2)

    h = _rms_fwd(x2, pre_norm, "pre_norm_fwd", dep=comm.gather_started())
    memn = _rms_fwd(mem2, mem_norm, "mem_norm_fwd")
    proj = _proj_part(h, comm.w_in_own(), chip, (0,), "proj_own")
    wg_in, tok = comm.w_in_near(after=[proj, memn, conv_w])
    proj = _proj_part(h, wg_in, chip, (REL_XOR[0], REL_XOR[1]), "proj_near", prev=proj, dep=tok)
    wg_in, tok = comm.w_in_all(after=proj)
    proj = _proj_part(h, wg_in, chip, (REL_XOR[2],), "proj_far", prev=proj, dep=tok)
    fw = [_attn_fwd(proj, cosa, sina, g) for g in range(3)]
    os_ = [f[0] for f in fw]
    ls_ = [f[1] for f in fw]
    tok = comm.w_rest_landed(after=ls_)
    cat = _attn_mix(proj, os_, ls_, dep=tok)
    cat = _conv_fwd(proj, conv_w, cat)
    wg_kv, wg_out = comm.w_rest(after=[cat])
    mkv = _mm_nn(memn, wg_kv, "mkv", NB * MLEN, 1024, 512)
    cat = _cross_fwd(proj, mkv, cosq, sinq, cosm, sinm, cat)
    y = _mm_nn(cat, wg_out, "out_proj", 1024, 1024, 2048)
    dy, dout, d_post, loss = _post_norm_loss(y, x2, tgt2, post_norm)

    dcat = _mm_nt(dy, wg_out, "dcat", 1024, 1024, 2048)
    g_send = _grad_rows(cat, dy,1 - core, BF16, "grad_w_out_send")
    sent = comm.sibling_start(g_send.reshape(NCHIP, D // NCHIP // 2, D), "a")
    g_keep = _grad_rows(cat, dy,core, BF16, "grad_w_out_keep", dep=sent[-1])
    red_a = comm.reduce_start_summed(g_keep.reshape(NCHIP, D // NCHIP // 2, D), sent, "a")
    da, lse, delta, dproj = _attn_mix_bwd(dcat, proj, os_, ls_, dep=red_a[-1])
    for g in range(3):
        dproj = _attn_bwd(proj, fw[g][2], fw[g][3], cosa, sina, da, lse, delta, dproj, g)
    dproj, dmk, dmv = _cross_bwd(dcat, proj, mkv, cosq, sinq, cosm, sinm, dproj)
    dmkv = jnp.concatenate([dmk, dmv], axis=-1).reshape(NB * MLEN, 2 * XW)
    dmkv_b = dmkv.astype(BF16)
    gk_send = _grad_rows(memn, dmkv_b, 1 - core, BF16, "grad_w_mem_kv_send")
    sent = comm.sibling_start(gk_send.reshape(NCHIP, D // NCHIP // 2, 2 * XW), "k")
    gk_keep = _grad_rows(memn, dmkv_b, core, BF16, "grad_w_mem_kv_keep", dep=sent[-1])
    dmemn = _mm_nt(dmkv_b, wg_kv, "dmemn", NB * MLEN, 1024, 512)
    d_mem = _norm_gain_grad(dmemn, mem2, "mem_norm_bwd")
    red_k = comm.reduce_start_summed(gk_keep.reshape(NCHIP, D // NCHIP // 2, 2 * XW), sent, "k", after=d_mem)
    dproj, d_conv = _conv_bwd(dcat, proj, conv_w, dproj, dep=red_k[-1])
    (r_out,) = comm.reduce_finish(red_a, "a", after=d_conv)
    (r_kv,) = comm.reduce_finish(red_k, "k", after=d_conv)

    g_send = _grad_w_in(h, dproj, 1 - core, BF16, "grad_w_in_send", dep=r_kv)
    sent = comm.sibling_start(g_send.reshape(NCHIP, D // 2, WB), "b")
    g_keep = _grad_w_in(h, dproj, core, BF16, "grad_w_in_keep", dep=sent[-1])
    red_b = comm.reduce_start_summed(g_keep.reshape(NCHIP, D // 2, WB), sent, "b")
    sets = [] if moments is None else [(moments["w_out"][0], r_out, *moments["w_out"][1:]),
                                      (moments["w_mem_kv"][0], r_kv, *moments["w_mem_kv"][1:])]
    dh, updates = _dh(dproj, wg_in, dep=red_b[-1], adam=sets)
    grad_x, d_pre = _pre_norm_bwd(dh, x2, pre_norm, dout)
    r_in = comm.reduce_finish_start(red_b, "b", after=grad_x)
    return loss, grad_x.reshape(NB, S, D), d_pre, d_mem, d_post, d_conv, r_in, (r_kv, r_out), updates


def _adamw(w, g, m, v, name):
    rows, cols = w.shape
    tr = rows if rows <= 512 else 512
    tc = cols if cols <= 1024 else 1024
    if cols % tc:
        tc = 896

    blk = pl.BlockSpec((tr, tc), lambda i, j: (i, j))
    sds = jax.ShapeDtypeStruct((rows, cols), F32)
    def body(*refs):
        _adamw_math(*refs)

    return pl.pallas_call(
        body, name=name, grid=(rows // tr, cols // tc),
        in_specs=[blk] * 4, out_specs=[blk] * 4, out_shape=[sds] * 4,
        compiler_params=_cp(("parallel", "parallel")),
    )(w, g, m, v)


def _place():
    return lax.axis_index("x"), lax.axis_index("y"), lax.axis_index("c")


def _other_chips(x, y):
    return [(1 - x, y), (x, 1 - y), (1 - x, 1 - y)]


def _tile_cols(cols):
    return cols if cols <= 1024 else (1024 if cols % 1024 == 0 else 896)


def _cast_own(w, chip, name):
    rows, cols = w.shape
    tr, tc = 512, _tile_cols(cols)

    def body(chip_ref, w_ref, o_ref):
        del chip_ref
        o_ref[...] = w_ref[...].astype(BF16)

    grid_spec = pltpu.PrefetchScalarGridSpec(
        num_scalar_prefetch=1, grid=(rows // tr, cols // tc),
        in_specs=[pl.BlockSpec((tr, tc), lambda i, j, cr: (i, j))],
        out_specs=pl.BlockSpec((None, tr, tc), lambda i, j, cr: (cr[0], i, j)))
    return pl.pallas_call(
        body, name=name, grid_spec=grid_spec,
        out_shape=jax.ShapeDtypeStruct((NCHIP, rows, cols), BF16),
        compiler_params=_cp(("parallel", "parallel")),
    )(_as_index(chip), w)


HBM_SPEC = pl.BlockSpec(memory_space=pltpu.HBM)
SEM_SPEC = pl.BlockSpec(memory_space=pltpu.SEMAPHORE)
ANY_SPEC = pl.BlockSpec(memory_space=pl.ANY)
EFFECT = pltpu.SideEffectType.DATAFLOW_SIDE_EFFECTING
TOKEN = jax.ShapeDtypeStruct((8, 128), F32)


def _half(ref, chip, hc):
    hr = ref.shape[1] // 2
    return ref.at[chip, pl.ds(hc * hr, hr), :]


NEAR = (0, 1)
FAR = (2,)
REL_XOR = (2, 1, 3)


def _gather_copies(refs, send_sems, recv_sems, rels):
    x, y, c = _place()
    chips = _other_chips(x, y)
    out, inc = [], []
    for a, ref in enumerate(refs):
        for p, j in enumerate(rels):
            px, py = chips[j]
            mine = _half(ref, 2 * x + y, c)
            theirs = _half(ref, 2 * px + py, c)
            sems = dict(send_sem=send_sems.at[len(rels) * a + p], recv_sem=recv_sems.at[len(rels) * a + p],
                        device_id=(px, py, c), device_id_type=MESH)
            out.append(pltpu.make_async_remote_copy(src_ref=mine, dst_ref=mine, **sems))
            inc.append(pltpu.make_async_remote_copy(src_ref=theirs, dst_ref=theirs, **sems))
    return out, inc


def _gather_start(bufs, groups, name):
    n = len(bufs)
    ng = len(groups)

    def body(*refs):
        ins = refs[:n]
        token = refs[-1]
        for gi, rels in enumerate(groups):
            out, _ = _gather_copies(ins, refs[n + 2 * gi], refs[n + 2 * gi + 1], rels)
            for cp in out:
                cp.start()
        token[...] = jnp.zeros_like(token)

    sems = []
    for rels in groups:
        sems += [pltpu.SemaphoreType.DMA((len(rels) * n,))] * 2
    res = pl.pallas_call(
        body, name=name,
        in_specs=[HBM_SPEC] * n,
        out_specs=[SEM_SPEC] * (2 * ng) + [HBM_SPEC] * n + [pl.BlockSpec(memory_space=pltpu.VMEM)],
        out_shape=sems + [pltpu.HBM(b.shape, b.dtype) for b in bufs] + [TOKEN],
        input_output_aliases={a: 2 * ng + a for a in range(n)},
        compiler_params=pltpu.CompilerParams(has_side_effects=EFFECT),
    )(*[pltpu.with_memory_space_constraint(b, pltpu.HBM) for b in bufs])
    return [(res[2 * gi], res[2 * gi + 1]) for gi in range(ng)], list(res[2 * ng:2 * ng + n]), res[-1]


def _gather_wait(bufs, sems, rels, after, name):
    n = len(bufs)
    send_sems, recv_sems = sems
    after = list(after) if isinstance(after, (list, tuple)) else [after]

    def body(*refs):
        ins = refs[:n]
        out, inc = _gather_copies(ins, refs[n], refs[n + 1], rels)
        for cp in out:
            cp.wait_send()
        for cp in inc:
            cp.wait_recv()

    return pl.pallas_call(
        body, name=name,
        in_specs=[HBM_SPEC] * n + [SEM_SPEC, SEM_SPEC] + [ANY_SPEC] * len(after),
        out_specs=[HBM_SPEC] * n,
        out_shape=[pltpu.HBM(b.shape, b.dtype) for b in bufs],
        input_output_aliases={a: a for a in range(n)},
        compiler_params=pltpu.CompilerParams(has_side_effects=EFFECT),
    )(*bufs, send_sems, recv_sems, *after)


def _forward_halves(bufs, rels, name):
    n = len(bufs)

    def body(*refs):
        cps, waits = _forward_copies(refs[n:2 * n], rels, refs[2 * n], refs[2 * n + 1])
        for cp in cps:
            cp.start()
        for cp in waits:
            cp.wait_recv()
        for cp in cps:
            cp.wait_send()

    return pl.pallas_call(
        body, name=name,
        in_specs=[ANY_SPEC] * n, out_specs=[ANY_SPEC] * n,
        out_shape=[jax.ShapeDtypeStruct(s.shape, s.dtype) for s in bufs],
        input_output_aliases={a: a for a in range(n)},
        scratch_shapes=[pltpu.SemaphoreType.DMA((len(rels) * n,)), pltpu.SemaphoreType.DMA((len(rels) * n,))],
    )(*bufs)


def _forward_copies(refs, rels, send_sems, recv_sems):
    x, y, c = _place()
    chips = _other_chips(x, y)
    cps, waits = [], []
    for a, ref in enumerate(refs):
        for p, j in enumerate(rels):
            px, py = chips[j]
            sems = dict(send_sem=send_sems.at[len(rels) * a + p], recv_sem=recv_sems.at[len(rels) * a + p],
                        device_id=(x, y, 1 - c), device_id_type=MESH)
            got = _half(ref, 2 * px + py, c)
            want = _half(ref, 2 * px + py, 1 - c)
            cps.append(pltpu.make_async_remote_copy(src_ref=got, dst_ref=got, **sems))
            waits.append(pltpu.make_async_remote_copy(src_ref=want, dst_ref=want, **sems))
    return cps, waits


def _forward_start(bufs, rels, name):
    n = len(bufs)

    def body(*refs):
        cps, _ = _forward_copies(refs[:n], rels, refs[n], refs[n + 1])
        for cp in cps:
            cp.start()
        refs[-1][...] = jnp.zeros_like(refs[-1])

    res = pl.pallas_call(
        body, name=name,
        in_specs=[HBM_SPEC] * n,
        out_specs=[SEM_SPEC, SEM_SPEC] + [HBM_SPEC] * n + [pl.BlockSpec(memory_space=pltpu.VMEM)],
        out_shape=[pltpu.SemaphoreType.DMA((len(rels) * n,))] * 2 + [pltpu.HBM(b.shape, b.dtype) for b in bufs] + [TOKEN],
        input_output_aliases={a: 2 + a for a in range(n)},
        compiler_params=pltpu.CompilerParams(has_side_effects=EFFECT),
    )(*[pltpu.with_memory_space_constraint(b, pltpu.HBM) for b in bufs])
    return (res[0], res[1]), list(res[2:2 + n]), res[-1]


def _forward_wait(bufs, sems, rels, after, name):
    n = len(bufs)

    def body(*refs):
        cps, waits = _forward_copies(refs[:n], rels, refs[n], refs[n + 1])
        for cp in cps:
            cp.wait_send()
        for cp in waits:
            cp.wait_recv()

    return pl.pallas_call(
        body, name=name,
        in_specs=[HBM_SPEC] * n + [SEM_SPEC, SEM_SPEC] + [ANY_SPEC] * len(after),
        out_specs=[HBM_SPEC] * n,
        out_shape=[pltpu.HBM(b.shape, b.dtype) for b in bufs],
        input_output_aliases={a: a for a in range(n)},
        compiler_params=pltpu.CompilerParams(has_side_effects=EFFECT),
    )(*bufs, sems[0], sems[1], *after)


def _sibling_copy(src, land, send_sems, recv_sems):
    x, y, c = _place()
    return pltpu.make_async_remote_copy(src_ref=src, dst_ref=land, send_sem=send_sems.at[0], recv_sem=recv_sems.at[0],
                                        device_id=(x, y, 1 - c), device_id_type=MESH)


def _sibling_start(part, name):
    def body(src, land, send_sems, recv_sems, src_thru, land_thru, token):
        _sibling_copy(src, land, send_sems, recv_sems).start()
        token[...] = jnp.zeros_like(token)

    land = lax.empty(part.shape, part.dtype)
    return pl.pallas_call(
        body, name=name,
        in_specs=[HBM_SPEC] * 2,
        out_specs=[SEM_SPEC, SEM_SPEC, HBM_SPEC, HBM_SPEC, pl.BlockSpec(memory_space=pltpu.VMEM)],
        out_shape=[pltpu.SemaphoreType.DMA((1,)), pltpu.SemaphoreType.DMA((1,)), pltpu.HBM(part.shape, part.dtype),
                   pltpu.HBM(part.shape, part.dtype), TOKEN],
        input_output_aliases={0: 2, 1: 3},
        compiler_params=pltpu.CompilerParams(has_side_effects=EFFECT),
    )(pltpu.with_memory_space_constraint(part, pltpu.HBM), pltpu.with_memory_space_constraint(land, pltpu.HBM))


def _sibling_wait(state, after, name):
    send_sems, recv_sems, part, land, _ = state

    def body(src, land_ref, send_ref, recv_ref, *rest):
        cp = _sibling_copy(src, land_ref, send_ref, recv_ref)
        cp.wait_send()
        cp.wait_recv()

    return pl.pallas_call(
        body, name=name,
        in_specs=[HBM_SPEC, HBM_SPEC, SEM_SPEC, SEM_SPEC] + [ANY_SPEC] * len(after),
        out_specs=[HBM_SPEC, HBM_SPEC],
        out_shape=[pltpu.HBM(part.shape, part.dtype), pltpu.HBM(land.shape, land.dtype)],
        input_output_aliases={0: 0, 1: 1},
        compiler_params=pltpu.CompilerParams(has_side_effects=EFFECT),
    )(part, land, send_sems, recv_sems, *after)[1]


def _pair_sum_rows(keep, got, name):
    nblk, rows, cols = keep.shape
    tr, tc = 512, _tile_cols(cols)

    def body(k_ref, g_ref, o_ref):
        o_ref[...] = (k_ref[...].astype(F32) + g_ref[...].astype(F32)).astype(BF16)

    blk = pl.BlockSpec((None, tr, tc), lambda b, i, j: (b, i, j))
    return pl.pallas_call(
        body, name=name, grid=(nblk, rows // tr, cols // tc),
        in_specs=[blk, blk], out_specs=blk,
        out_shape=jax.ShapeDtypeStruct(keep.shape, BF16),
        compiler_params=_cp(("parallel",) * 3),
    )(keep, got)


def _scatter_start(qs, name):
    n = len(qs)

    def body(*refs):
        ins, lands = refs[:n], refs[n:2 * n]
        token = refs[-1]
        for cp in _scatter_copies(ins, lands, refs[2 * n], refs[2 * n + 1]):
            cp.start()
        token[...] = jnp.zeros_like(token)

    lands = [lax.empty((3,) + q.shape[1:], q.dtype) for q in qs]
    res = pl.pallas_call(
        body, name=name,
        in_specs=[HBM_SPEC] * (2 * n),
        out_specs=[SEM_SPEC, SEM_SPEC] + [HBM_SPEC] * (2 * n) + [pl.BlockSpec(memory_space=pltpu.VMEM)],
        out_shape=[pltpu.SemaphoreType.DMA((3 * n,)), pltpu.SemaphoreType.DMA((3 * n,))]
        + [pltpu.HBM(b.shape, b.dtype) for b in qs + lands] + [TOKEN],
        input_output_aliases={a: 2 + a for a in range(2 * n)},
        compiler_params=pltpu.CompilerParams(has_side_effects=EFFECT),
    )(*[pltpu.with_memory_space_constraint(b, pltpu.HBM) for b in qs + lands])
    return res[0], res[1], list(res[2:2 + n]), list(res[2 + n:2 + 2 * n]), res[-1]


def _scatter_copies(ins, lands, send_sems, recv_sems):
    x, y, c = _place()
    cps = []
    for a in range(len(ins)):
        for j, (px, py) in enumerate(_other_chips(x, y)):
            cps.append(pltpu.make_async_remote_copy(
                src_ref=ins[a].at[2 * px + py], dst_ref=lands[a].at[j],
                send_sem=send_sems.at[3 * a + j], recv_sem=recv_sems.at[3 * a + j], device_id=(px, py, c), device_id_type=MESH))
    return cps


def _scatter_wait(qs, lands, send_sems, recv_sems, after, name):
    n = len(qs)

    def body(*refs):
        for cp in _scatter_copies(refs[:n], refs[n:2 * n], refs[2 * n], refs[2 * n + 1]):
            cp.wait_send()
            cp.wait_recv()

    res = pl.pallas_call(
        body, name=name,
        in_specs=[HBM_SPEC] * (2 * n) + [SEM_SPEC, SEM_SPEC, ANY_SPEC],
        out_specs=[HBM_SPEC] * (2 * n),
        out_shape=[pltpu.HBM(b.shape, b.dtype) for b in qs + lands],
        input_output_aliases={a: a for a in range(2 * n)},
        compiler_params=pltpu.CompilerParams(has_side_effects=EFFECT),
    )(*qs, *lands, send_sems, recv_sems, after)
    return list(res[:n]), list(res[n:])


def _chip_sum(q, got, name):
    _, hr, cols = got.shape
    tr, tc = 512, _tile_cols(cols)
    chip = 2 * lax.axis_index("x") + lax.axis_index("y")
    c = lax.axis_index("c")

    def body(idx_ref, q_ref, g_ref, o_ref):
        del idx_ref
        acc = q_ref[...].astype(F32)
        for i in range(3):
            acc = acc + g_ref[i].astype(F32)
        o_ref[...] = acc

    grid_spec = pltpu.PrefetchScalarGridSpec(
        num_scalar_prefetch=1, grid=(hr // tr, cols // tc),
        in_specs=[pl.BlockSpec((None, tr, tc), lambda i, j, ix: (ix[0], i, j)),
                  pl.BlockSpec((3, tr, tc), lambda i, j, ix: (0, i, j))],
        out_specs=pl.BlockSpec((None, tr, tc), lambda i, j, ix: (ix[1], i, j)))
    return pl.pallas_call(
        body, name=name, grid_spec=grid_spec,
        out_shape=jax.ShapeDtypeStruct((2, hr, cols), F32),
        compiler_params=_cp(("parallel", "parallel")),
    )(jnp.stack([chip, c]).astype(jnp.int32), q, got)


def _join_halves(bufs, name):
    n = len(bufs)

    def body(*refs):
        outs = refs[n:2 * n]
        send_sems, recv_sems = refs[2 * n:]
        x, y, c = _place()
        cps = []
        for a in range(n):
            cps.append(pltpu.make_async_remote_copy(src_ref=outs[a].at[c], dst_ref=outs[a].at[c], send_sem=send_sems.at[a],
                                                    recv_sem=recv_sems.at[a], device_id=(x, y, 1 - c), device_id_type=MESH))
        for cp in cps:
            cp.start()
        for a in range(n):
            theirs = outs[a].at[1 - c]
            pltpu.make_async_remote_copy(src_ref=theirs, dst_ref=theirs, send_sem=send_sems.at[a], recv_sem=recv_sems.at[a],
                                         device_id=(x, y, 1 - c), device_id_type=MESH).wait_recv()
        for cp in cps:
            cp.wait_send()

    hbm = pl.BlockSpec(memory_space=pl.ANY)
    return pl.pallas_call(
        body, name=name,
        in_specs=[hbm] * n, out_specs=[hbm] * n,
        out_shape=[jax.ShapeDtypeStruct(b.shape, b.dtype) for b in bufs],
        input_output_aliases={a: a for a in range(n)},
        scratch_shapes=[pltpu.SemaphoreType.DMA((n,)), pltpu.SemaphoreType.DMA((n,))],
    )(*bufs)


def _join_copy(buf, send_sems, recv_sems):
    x, y, c = _place()
    sems = dict(send_sem=send_sems.at[0], recv_sem=recv_sems.at[0], device_id=(x, y, 1 - c), device_id_type=MESH)
    return (pltpu.make_async_remote_copy(src_ref=buf.at[c], dst_ref=buf.at[c], **sems),
            pltpu.make_async_remote_copy(src_ref=buf.at[1 - c], dst_ref=buf.at[1 - c], **sems))


def _join_start(buf, name):
    def body(b_ref, send_sems, recv_sems, thru, token):
        _join_copy(b_ref, send_sems, recv_sems)[0].start()
        token[...] = jnp.zeros_like(token)

    return pl.pallas_call(
        body, name=name,
        in_specs=[HBM_SPEC],
        out_specs=[SEM_SPEC, SEM_SPEC, HBM_SPEC, pl.BlockSpec(memory_space=pltpu.VMEM)],
        out_shape=[pltpu.SemaphoreType.DMA((1,)), pltpu.SemaphoreType.DMA((1,)), pltpu.HBM(buf.shape, buf.dtype), TOKEN],
        input_output_aliases={0: 2},
        compiler_params=pltpu.CompilerParams(has_side_effects=EFFECT),
    )(pltpu.with_memory_space_constraint(buf, pltpu.HBM))


def _join_wait(state, after, name):
    send_sems, recv_sems, buf, _ = state

    def body(b_ref, send_ref, recv_ref, *rest):
        out, inc = _join_copy(b_ref, send_ref, recv_ref)
        out.wait_send()
        inc.wait_recv()

    return pl.pallas_call(
        body, name=name,
        in_specs=[HBM_SPEC, SEM_SPEC, SEM_SPEC] + [ANY_SPEC] * len(after),
        out_specs=HBM_SPEC,
        out_shape=pltpu.HBM(buf.shape, buf.dtype),
        input_output_aliases={0: 0},
        compiler_params=pltpu.CompilerParams(has_side_effects=EFFECT),
    )(buf, send_sems, recv_sems, *after)


N_DEV = 8


def _gather_small(block, reduce, name):
    m_per, cols = block.shape

    def body(x_ref, out_ref, all_ref, send_sems, recv_sems, local_sem):
        x, y, c = _place()
        me, sibling = (x, y, c), (x, y, 1 - c)
        chips = _other_chips(x, y)

        def rows(px, py, pc):
            return all_ref.at[pl.ds((4 * px + 2 * py + pc) * m_per, m_per), :]

        def copy(k, block_of, to, src=None):
            return pltpu.make_async_remote_copy(
                src_ref=rows(*block_of) if src is None else src, dst_ref=rows(*block_of),
                send_sem=send_sems.at[k], recv_sem=recv_sems.at[k], device_id=to, device_id_type=MESH)

        mine = pltpu.make_async_copy(x_ref, rows(*me), local_sem)
        mine.start()
        first = [copy(0, me, sibling, src=x_ref)]
        first += [copy(1 + j, me, (*chip, c), src=x_ref) for j, chip in enumerate(chips)]
        for cp in first:
            cp.start()
        passed = [copy(4 + j, (*chip, c), sibling) for j, chip in enumerate(chips)]
        for j, chip in enumerate(chips):
            copy(1 + j, (*chip, c), me).wait_recv()
            passed[j].start()
        copy(0, sibling, me).wait_recv()
        for j, chip in enumerate(chips):
            copy(4 + j, (*chip, 1 - c), me).wait_recv()
        for cp in first + passed:
            cp.wait_send()
        mine.wait()
        if reduce:
            acc = all_ref[pl.ds(0, m_per), :]
            for i in range(1, N_DEV):
                acc = acc + all_ref[pl.ds(i * m_per, m_per), :]
            out_ref[...] = acc
        else:
            out_ref[...] = all_ref[...]

    out_rows = m_per if reduce else N_DEV * m_per
    return pl.pallas_call(
        body, name=name,
        in_specs=[pl.BlockSpec(memory_space=pltpu.VMEM)],
        out_specs=pl.BlockSpec(memory_space=pltpu.VMEM),
        out_shape=jax.ShapeDtypeStruct((out_rows, cols), F32),
        scratch_shapes=[pltpu.VMEM((N_DEV * m_per, cols), F32), pltpu.SemaphoreType.DMA((7,)),
                        pltpu.SemaphoreType.DMA((7,)), pltpu.SemaphoreType.DMA],
    )(block)


class _Comm:
    def __init__(self, w_in, w_kv, w_out, chip):
        self.bufs = [_cast_own(w_in, chip, "cast_w_in"), _cast_own(w_kv, chip, "cast_w_mem_kv"),
                     _cast_own(w_out, chip, "cast_w_out")]


    def gather_started(self):
        (self.sems,), (self.b_in,), tok = _gather_start(self.bufs[:1], (NEAR,), "gather_start_in_near")
        return [tok]

    def w_in_own(self):
        return self.b_in.reshape(NCHIP * D, WB)

    def w_in_near(self, after):
        (b_in,) = _gather_wait([self.b_in], self.sems, NEAR, list(after) + self.bufs[1:], "gather_wait_in_near")
        (b_in,) = _forward_halves([b_in], NEAR, "forward_in_near")
        (self.sems,), (self.b_in,), tok = _gather_start([b_in], (FAR,), "gather_start_in_far")
        return self.b_in.reshape(NCHIP * D, WB), tok

    def w_in_all(self, after):
        (b_in,) = _gather_wait([self.b_in], self.sems, FAR, after, "gather_wait_in_far")
        (b_in,) = _forward_halves([b_in], FAR, "forward_in_far")
        (self.sems,), self.b_rest, tok = _gather_start(self.bufs[1:], (NEAR + FAR,), "gather_start_rest")
        return b_in.reshape(NCHIP * D, WB), tok

    def w_rest_landed(self, after):
        b_rest = _gather_wait(self.b_rest, self.sems, NEAR + FAR, after, "gather_wait_rest")
        self.sems, self.b_rest, tok = _forward_start(b_rest, NEAR + FAR, "forward_start_rest")
        return tok

    def w_rest(self, after):
        b_kv, b_out = _forward_wait(self.b_rest, self.sems, NEAR + FAR, after, "forward_wait_rest")
        return b_kv.reshape(D, 2 * XW), b_out.reshape(D, D)

    def sibling_start(self, send, tag):
        return _sibling_start(send, "sibling_start_" + tag)

    def reduce_start_summed(self, keep, sent, tag, after=None):
        got = _sibling_wait(sent, [keep] + ([] if after is None else [after]), "sibling_wait_" + tag)
        return _scatter_start([_pair_sum_rows(keep, got, "pair_sum_" + tag)], "scatter_start_" + tag)

    def reduce_finish_start(self, state, tag, after):
        send_sems, recv_sems, qs, lands, _ = state
        qs, lands = _scatter_wait(qs, lands, send_sems, recv_sems, after, "scatter_wait_" + tag)
        return _join_start(_chip_sum(qs[0], lands[0], f"chip_sum_{tag}0"), "join_start_" + tag)

    def reduce_finish_wait(self, pending, tag, after):
        j = _join_wait(pending, after, "join_wait_" + tag)
        return j.reshape(2 * j.shape[1], j.shape[2])

    def reduce_finish(self, state, tag, after):
        send_sems, recv_sems, qs, lands, _ = state
        qs, lands = _scatter_wait(qs, lands, send_sems, recv_sems, after, "scatter_wait_" + tag)
        halves = [_chip_sum(q, l, f"chip_sum_{tag}{i}") for i, (q, l) in enumerate(zip(qs, lands))]
        return [j.reshape(2 * j.shape[1], j.shape[2]) for j in _join_halves(halves, "join_halves_" + tag)]


def kernel(x, mem, pre_norm, w_in, conv_w, mem_norm, w_mem_kv, w_out, post_norm, loss_target, m_pre_norm, m_w_in, m_conv_w, m_mem_norm, m_w_mem_kv, m_w_out, m_post_norm, v_pre_norm, v_w_in, v_conv_w, v_mem_norm, v_w_mem_kv, v_w_out, v_post_norm):
    chip = 2 * lax.axis_index("x") + lax.axis_index("y")

    cw_blk = jnp.zeros((8, 384), F32).at[:3].set(conv_w[0])
    cw_all = _gather_small(cw_blk, False, "gather_conv_w").reshape(NCHIP, 2, 8, 384)[:, 0, :3]
    conv_full = jnp.transpose(cw_all, (1, 0, 2)).reshape(3, CW)

    comm = _Comm(w_in[0], w_mem_kv[0], w_out[0], chip)
    moments = {"w_out": (w_out[0], m_w_out[0], v_w_out[0]), "w_mem_kv": (w_mem_kv[0], m_w_mem_kv[0], v_w_mem_kv[0])}
    loss, grad_x, d_pre, d_mem, d_post, d_conv, r_in, _, (upd_out, upd_kv) = _local_step(
        x, mem, pre_norm, conv_full, mem_norm, post_norm, loss_target, chip, lax.axis_index("c"), comm, moments)

    pack = jnp.concatenate([d_pre, d_mem, d_post, jnp.pad(d_conv, ((0, 0), (0, D - CW))),
                            jnp.pad(loss, ((0, 0), (0, D - 128))), jnp.zeros((1, D), F32)], axis=0)
    tot = _gather_small(pack, True, "reduce_small")
    g_pre, g_mem, g_post = tot[0:1], tot[1:2], tot[2:3]
    g_conv = lax.dynamic_slice(tot[3:6, :CW], (0, chip * 384), (3, 384))
    loss_out = tot[6, 0]

    names = ("pre_norm", "w_in", "conv_w", "mem_norm", "w_mem_kv", "w_out", "post_norm")
    ws = (pre_norm, w_in[0], conv_w[0], mem_norm, w_mem_kv[0], w_out[0], post_norm)
    gs = [g_pre, None, g_conv, g_mem, None, None, g_post]
    ms = (m_pre_norm, m_w_in[0], m_conv_w[0], m_mem_norm, m_w_mem_kv[0], m_w_out[0], m_post_norm)
    vs = (v_pre_norm, v_w_in[0], v_conv_w[0], v_mem_norm, v_w_mem_kv[0], v_w_out[0], v_post_norm)
    upd = [None if g is None else _adamw(w, g, m, v, "adamw_" + nm) for nm, w, g, m, v in zip(names, ws, gs, ms, vs)]
    upd[4], upd[5] = upd_kv, upd_out
    g_in = comm.reduce_finish_wait(r_in, "b", after=[u[1] for u in upd if u is not None])
    upd[1] = _adamw(ws[1], g_in, ms[1], vs[1], "adamw_w_in")

    def shaped(arrs):
        return [a.reshape(w.shape) if w.ndim == a.ndim else a.reshape((1,) + a.shape)
                for a, w in zip(arrs, (pre_norm, w_in, conv_w, mem_norm, w_mem_kv, w_out, post_norm))]

    grads = shaped([u[0] for u in upd])
    deltas = shaped([u[1] for u in upd])
    new_m = shaped([u[2] for u in upd])
    new_v = shaped([u[3] for u in upd])
    return (loss_out, grad_x, *grads, *deltas, *new_m, *new_v)
```

```python
import jax
import jax.numpy as jnp
from jax import lax
from jax.experimental import pallas as pl
from jax.experimental.pallas import tpu as pltpu

F32 = jnp.float32
BF16 = jnp.bfloat16

D = 4096
S = 2048
NB = 2
T = NB * S
MLEN = 256
HD = 128
AW = 1536
CW = 1536
XW = 1024
XHD = 256
NXH = 4
NC = 14336
QA, KA, VA, ZA, UC, BC, CC, ZC, QX, ZX = 0, 1536, 3072, 4608, 6144, 7680, 9216, 10752, 12288, 13312
NCHIP = 4
WB = NC // NCHIP
DIL = (1, 4, 16)
HPG = 4
EPS = 1e-6
NEG = -1e30
ROPE_THETA = 10000.0
A_SCALE = HD ** -0.5
X_SCALE = XHD ** -0.5

ADAM_LR = 0.001
ADAM_B1 = 0.9
ADAM_B2 = 0.999
ADAM_EPS = 1e-08
ADAM_WD = 0.01
ADAM_STEP = 10

MESH = pl.DeviceIdType.MESH
MIB = 1024 * 1024


def _cp(sem, vmem_mib=48):
    return pltpu.CompilerParams(dimension_semantics=sem, vmem_limit_bytes=vmem_mib * MIB)


def _sigmoid(z):
    return 1.0 / (1.0 + jnp.exp(-z))


def _rope(x, cos, sin, half):
    return x * cos + pltpu.roll(x, half, 1) * sin


def _rope_t(g, cos, sin, half):
    return g * cos + pltpu.roll(g * sin, half, 1)


def _rms_fwd(x2, g, name, dep=None):
    rows = x2.shape[0]
    tr = 256
    deps = [] if dep is None else list(dep)

    def body(x_ref, g_ref, *rest):
        o_ref = rest[-1]
        x = x_ref[...]
        r = lax.rsqrt(jnp.mean(x * x, axis=-1, keepdims=True) + EPS)
        o_ref[...] = (x * r * g_ref[...]).astype(BF16)

    return pl.pallas_call(
        body, name=name, grid=(rows // tr,),
        in_specs=[pl.BlockSpec((tr, D), lambda i: (i, 0)), pl.BlockSpec((1, D), lambda i: (0, 0))]
        + [pl.BlockSpec(memory_space=pl.ANY)] * len(deps),
        out_specs=pl.BlockSpec((tr, D), lambda i: (i, 0)),
        out_shape=jax.ShapeDtypeStruct((rows, D), BF16),
        compiler_params=_cp(("parallel",)),
    )(x2, g, *deps)


def _norm_gain_grad(dn, x2, name):
    rows = x2.shape[0]
    tr = 256

    def body(dn_ref, x_ref, dg_ref):
        @pl.when(pl.program_id(0) == 0)
        def _():
            dg_ref[...] = jnp.zeros_like(dg_ref)
        x = x_ref[...]
        r = lax.rsqrt(jnp.mean(x * x, axis=-1, keepdims=True) + EPS)
        dg_ref[...] += jnp.sum(dn_ref[...] * (x * r), axis=0, keepdims=True)

    return pl.pallas_call(
        body, name=name, grid=(rows // tr,),
        in_specs=[pl.BlockSpec((tr, D), lambda i: (i, 0)), pl.BlockSpec((tr, D), lambda i: (i, 0))],
        out_specs=pl.BlockSpec((1, D), lambda i: (0, 0)),
        out_shape=jax.ShapeDtypeStruct((1, D), F32),
        compiler_params=_cp(("arbitrary",)),
    )(dn, x2)


def _pre_norm_bwd(dh, x2, g, dout, part, prev=None, dep=None):
    tr = 256
    nsteps = T // tr // 2
    extra = ([] if prev is None else list(prev)) + ([] if dep is None else [dep])

    def body(dh_ref, x_ref, g_ref, dout_ref, *rest):
        gx_ref, dg_ref = rest[len(extra):]

        @pl.when(pl.program_id(0) == 0)
        def _():
            dg_ref[...] = jnp.zeros_like(dg_ref) if prev is None else rest[1][...]

        x = x_ref[...]
        dh_ = dh_ref[...].astype(F32)
        r = lax.rsqrt(jnp.mean(x * x, axis=-1, keepdims=True) + EPS)
        xhat = x * r
        dg_ref[...] += jnp.sum(dh_ * xhat, axis=0, keepdims=True)
        dxn = dh_ * g_ref[...]
        gx_ref[...] = dout_ref[...].astype(F32) + r * (dxn - xhat * jnp.mean(dxn * xhat, axis=-1, keepdims=True))

    row = pl.BlockSpec((tr, D), lambda i: (i + part * nsteps, 0))
    vec = pl.BlockSpec((1, D), lambda i: (0, 0))
    return pl.pallas_call(
        body, name=f"pre_norm_bwd_{part}", grid=(nsteps,),
        in_specs=[row, row, vec, row] + ([] if prev is None else [pl.BlockSpec(memory_space=pl.ANY), vec])
        + [pl.BlockSpec(memory_space=pl.ANY)] * (0 if dep is None else 1),
        out_specs=[row, vec],
        out_shape=[jax.ShapeDtypeStruct((T, D), F32), jax.ShapeDtypeStruct((1, D), F32)],
        input_output_aliases={} if prev is None else {4: 0},
        compiler_params=_cp(("arbitrary",)),
    )(dh, x2, g, dout, *extra)


def _post_norm_loss(y, x2, tgt, g):
    tr = 256

    def body(y_ref, x_ref, t_ref, g_ref, dy_ref, dout_ref, dg_ref, loss_ref):
        @pl.when(pl.program_id(0) == 0)
        def _():
            dg_ref[...] = jnp.zeros_like(dg_ref)
            loss_ref[...] = jnp.zeros_like(loss_ref)
        yv = y_ref[...]
        gv = g_ref[...]
        r = lax.rsqrt(jnp.mean(yv * yv, axis=-1, keepdims=True) + EPS)
        yhat = yv * r
        err = x_ref[...] + yhat * gv - t_ref[...]
        loss_ref[...] += jnp.sum(jnp.sum(err * err, axis=1, keepdims=True), axis=0, keepdims=True) * (0.5 / D)
        dout = err * (1.0 / D)
        dout_ref[...] = dout.astype(BF16)
        dg_ref[...] += jnp.sum(dout * yhat, axis=0, keepdims=True)
        dyn = dout * gv
        dy_ref[...] = (r * (dyn - yhat * jnp.mean(dyn * yhat, axis=-1, keepdims=True))).astype(BF16)

    row = pl.BlockSpec((tr, D), lambda i: (i, 0))
    vec = pl.BlockSpec((1, D), lambda i: (0, 0))
    return pl.pallas_call(
        body, name="post_norm_loss", grid=(T // tr,),
        in_specs=[row, row, row, vec],
        out_specs=[row, row, vec, pl.BlockSpec((1, 128), lambda i: (0, 0))],
        out_shape=[jax.ShapeDtypeStruct((T, D), BF16), jax.ShapeDtypeStruct((T, D), BF16),
                   jax.ShapeDtypeStruct((1, D), F32), jax.ShapeDtypeStruct((1, 128), F32)],
        compiler_params=_cp(("arbitrary",)),
    )(y, x2, tgt, g)


NN = (((1,), (0,)), ((), ()))
NT = (((1,), (1,)), ((), ()))
TN = (((0,), (0,)), ((), ()))


def _as_index(v):
    return jnp.reshape(v, (1,)).astype(jnp.int32)


def _matmul(a, b, *, name, dims, grid, a_block, a_map, b_block, b_map, o_block, o_map, out_shape, out_dtype=F32,
            index=None, prev=None, deps=()):
    extra = ([] if prev is None else [prev]) + [d for d in deps if d is not None]
    first = 0 if index is None else 1
    nk = grid[2]
    in_place = out_dtype == F32

    def body(*refs):
        a_ref, b_ref, o_ref = refs[first], refs[first + 1], refs[first + 2 + len(extra)]
        acc_ref = o_ref if in_place else refs[-1]

        @pl.when(pl.program_id(2) == 0)
        def _():
            acc_ref[...] = lax.dot_general(a_ref[...], b_ref[...], dims, preferred_element_type=F32)

        @pl.when(pl.program_id(2) > 0)
        def _():
            acc_ref[...] += lax.dot_general(a_ref[...], b_ref[...], dims, preferred_element_type=F32)

        if not in_place:
            @pl.when(pl.program_id(2) == nk - 1)
            def _():
                o_ref[...] = acc_ref[...].astype(o_ref.dtype)

    in_specs = [pl.BlockSpec(a_block, a_map), pl.BlockSpec(b_block, b_map)] + [pl.BlockSpec(memory_space=pl.ANY)] * len(extra)
    out_specs = pl.BlockSpec(o_block, o_map)
    scratch = [] if in_place else [pltpu.VMEM(o_block, F32)]
    kwargs = dict(name=name, out_shape=jax.ShapeDtypeStruct(out_shape, out_dtype),
                  input_output_aliases={} if prev is None else {first + 2: 0},
                  compiler_params=_cp(("parallel", "parallel", "arbitrary"), vmem_mib=56))
    if index is None:
        return pl.pallas_call(body, grid=grid, in_specs=in_specs, out_specs=out_specs, scratch_shapes=scratch,
                              **kwargs)(a, b, *extra)
    grid_spec = pltpu.PrefetchScalarGridSpec(num_scalar_prefetch=1, grid=grid, in_specs=in_specs, out_specs=out_specs,
                                             scratch_shapes=scratch)
    return pl.pallas_call(body, grid_spec=grid_spec, **kwargs)(_as_index(index), a, b, *extra)


def _mm_nn(a, b, name, tm, tn, tk):
    m, kd = a.shape
    n = b.shape[1]
    return _matmul(a, b, name=name, dims=NN, grid=(m // tm, n // tn, kd // tk),
                   a_block=(tm, tk), a_map=lambda i, j, k: (i, k),
                   b_block=(tk, tn), b_map=lambda i, j, k: (k, j),
                   o_block=(tm, tn), o_map=lambda i, j, k: (i, j), out_shape=(m, n))


def _mm_nt(a, b, name, tm, tn, tk):
    m, kd = a.shape
    n = b.shape[0]
    return _matmul(a, b, name=name, dims=NT, grid=(m // tm, n // tn, kd // tk),
                   a_block=(tm, tk), a_map=lambda i, j, k: (i, k),
                   b_block=(tn, tk), b_map=lambda i, j, k: (j, k),
                   o_block=(tm, tn), o_map=lambda i, j, k: (i, j), out_shape=(m, n))


W_TN = 1792
W_NJ = WB // W_TN


def _proj_part(h, wg, chip, masks, name, prev=None, dep=None):
    tm, tk = 1024, 2048

    def blk(j, ix):
        m = masks[0]
        for t in range(1, len(masks)):
            m = jnp.where(j // W_NJ == t, masks[t], m)
        return jnp.bitwise_xor(ix[0], m)

    return _matmul(h, wg, name=name, dims=NN, grid=(T // tm, len(masks) * W_NJ, D // tk), index=chip, prev=prev, deps=(dep,),
                   a_block=(tm, tk), a_map=lambda i, j, k, ix: (i, k),
                   b_block=(tk, W_TN), b_map=lambda i, j, k, ix: (blk(j, ix) * (D // tk) + k, j % W_NJ),
                   o_block=(tm, W_TN), o_map=lambda i, j, k, ix: (i, blk(j, ix) * W_NJ + j % W_NJ), out_shape=(T, NC))


def _adamw_math(w_ref, g_ref, m_ref, v_ref, go_ref, d_ref, nm_ref, nv_ref):
    gv = g_ref[...]
    go_ref[...] = gv
    nm = ADAM_B1 * m_ref[...] + (1.0 - ADAM_B1) * gv
    nv = ADAM_B2 * v_ref[...] + (1.0 - ADAM_B2) * (gv * gv)
    m_hat = nm / (1.0 - ADAM_B1 ** ADAM_STEP)
    v_hat = nv / (1.0 - ADAM_B2 ** ADAM_STEP)
    d_ref[...] = -ADAM_LR * (m_hat / (jnp.sqrt(v_hat) + ADAM_EPS) + ADAM_WD * w_ref[...])
    nm_ref[...] = nm
    nv_ref[...] = nv


def _dh(dproj, wg, dep=None, adam=()):
    tm, tn = 1024, 1024
    grid = (T // tm, D // tn, NC // W_TN)
    nsteps = grid[0] * grid[1] * grid[2]
    deps = [] if dep is None else [dep]
    na = len(adam)
    nk = grid[2]

    def body(*refs):
        a_ref, b_ref = refs[0], refs[1]
        o_ref = refs[2 + 4 * na + len(deps)]
        acc_ref = refs[-1]

        @pl.when(pl.program_id(2) == 0)
        def _():
            acc_ref[...] = lax.dot_general(a_ref[...], b_ref[...], NT, preferred_element_type=F32)

        @pl.when(pl.program_id(2) > 0)
        def _():
            acc_ref[...] += lax.dot_general(a_ref[...], b_ref[...], NT, preferred_element_type=F32)

        @pl.when(pl.program_id(2) == nk - 1)
        def _():
            o_ref[...] = acc_ref[...].astype(BF16)

        for s in range(na):
            ins = refs[2 + 4 * s:6 + 4 * s]
            outs = refs[3 + 4 * na + len(deps) + 4 * s:7 + 4 * na + len(deps) + 4 * s]
            _adamw_math(*ins, *outs)

    def rows_of(arr):
        r, c = arr.shape
        return pl.BlockSpec((r // nsteps, c), lambda i, j, k: ((i * grid[1] + j) * grid[2] + k, 0))

    adam_specs = [rows_of(a) for st in adam for a in st]
    res = pl.pallas_call(
        body, name="dh", grid=grid,
        in_specs=[pl.BlockSpec((tm, W_TN), lambda i, j, k: (i, k)),
                  pl.BlockSpec((tn, W_TN), lambda i, j, k: ((k // W_NJ) * (D // tn) + j, k % W_NJ))]
        + adam_specs + [pl.BlockSpec(memory_space=pl.ANY)] * len(deps),
        out_specs=[pl.BlockSpec((tm, tn), lambda i, j, k: (i, j))] + adam_specs,
        out_shape=[jax.ShapeDtypeStruct((T, D), BF16)] + [jax.ShapeDtypeStruct(a.shape, F32) for st in adam for a in st],
        scratch_shapes=[pltpu.VMEM((tm, tn), F32)],
        compiler_params=_cp(("arbitrary",) * 3, vmem_mib=56),
    )(dproj, wg, *[a for st in adam for a in st], *deps)
    return res[0], [tuple(res[1 + 4 * s:5 + 4 * s]) for s in range(na)]


def _grad_rows(a, b, half, out_dtype, name, dep=None):
    kd, n = b.shape
    tm, tn, tk = D // NCHIP // 2, min(n, 2048), min(kd, 2048)
    return _matmul(a, b, name=name, dims=TN, grid=(NCHIP, n // tn, kd // tk), index=half, deps=(dep,),
                   a_block=(tk, tm), a_map=lambda i, j, k, ix: (k, 2 * i + ix[0]),
                   b_block=(tk, tn), b_map=lambda i, j, k, ix: (k, j),
                   o_block=(tm, tn), o_map=lambda i, j, k, ix: (i, j),
                   out_shape=(NCHIP * tm, n), out_dtype=out_dtype)


def _grad_w_in(h, dproj, half, out_dtype, name, dep=None):
    tm, tk = 1024, 2048
    nh = D // 2 // tm
    return _matmul(h, dproj, name=name, dims=TN, grid=(nh, NC // W_TN, T // tk), index=half, deps=(dep,),
                   a_block=(tk, tm), a_map=lambda i, j, k, ix: (k, ix[0] * nh + i),
                   b_block=(tk, W_TN), b_map=lambda i, j, k, ix: (k, j),
                   o_block=(tm, W_TN), o_map=lambda i, j, k, ix: ((j // W_NJ) * nh + i, j % W_NJ),
                   out_shape=(NCHIP * D // 2, WB), out_dtype=out_dtype)


def _rope_tables(pos, half):
    inv = 1.0 / (ROPE_THETA ** (jnp.arange(half, dtype=F32) / half))
    ang = pos.astype(F32)[:, None] * inv[None, :]
    cos, sin = jnp.cos(ang), jnp.sin(ang)
    return jnp.concatenate([cos, cos], axis=1), jnp.concatenate([-sin, sin], axis=1)


def _band_mask(r0):
    qi = lax.broadcasted_iota(jnp.int32, (128, 256), 0)
    kk = lax.broadcasted_iota(jnp.int32, (128, 256), 1)
    return (kk >= qi) & (kk <= qi + 128) & (kk + r0 >= 128)


def _window(r0, nblk):
    if nblk == 1:
        qi = lax.broadcasted_iota(jnp.int32, (128, 128), 0)
        kk = lax.broadcasted_iota(jnp.int32, (128, 128), 1)
        return pl.ds(128, 128), kk <= qi
    return pl.ds(r0, 256), _band_mask(r0)


def _dil_rows(r, n, d):
    if d == 1:
        return pl.ds(pl.multiple_of(n * 128, 128), 128)
    return pl.ds(r + d * 128 * n, 128, stride=d)


def _attn_fwd(proj, cosf, sinf, g):
    d = DIL[g]
    ln = S // d
    nblk = ln // 128
    proj_v = proj.reshape(NB, S, NC)

    def body(q_ref, k_ref, v_ref, cos_ref, sin_ref, o_ref, l_ref, k_s, v_s):
        k_s[:, pl.ds(0, 128), :] = jnp.zeros((d, 128, HD), BF16)
        v_s[:, pl.ds(0, 128), :] = jnp.zeros((d, 128, HD), BF16)

        def prep(i, carry):
            r, n = i // nblk, i % nblk
            rows = _dil_rows(r, n, d)
            dst = pl.ds(pl.multiple_of(n * 128 + 128, 128), 128)
            k_s[r, dst, :] = _rope(k_ref[rows, :], cos_ref[rows, :], sin_ref[rows, :], HD // 2).astype(BF16)
            v_s[r, dst, :] = v_ref[rows, :].astype(BF16)
            return carry

        lax.fori_loop(0, d * nblk, prep, 0, unroll=4)

        def step(i, carry):
            r, n = i // nblk, i % nblk
            rows = _dil_rows(r, n, d)
            r0 = pl.multiple_of(n * 128, 128)
            qr = _rope(q_ref[rows, :], cos_ref[rows, :], sin_ref[rows, :], HD // 2).astype(BF16)
            win, mask = _window(r0, nblk)
            kw = k_s[r, win, :]
            vw = v_s[r, win, :]
            sc = lax.dot_general(qr, kw, NT, preferred_element_type=F32) * A_SCALE
            sc = jnp.where(mask, sc, NEG)
            m = jnp.max(sc, axis=1, keepdims=True)
            p = jnp.exp(sc - m)
            l = jnp.sum(p, axis=1, keepdims=True)
            o_ref[rows, :] = jnp.dot(p.astype(BF16), vw, preferred_element_type=F32) / l
            l_ref[rows, :] = jnp.broadcast_to(m + jnp.log(l), (128, HD))
            return carry

        lax.fori_loop(0, d * nblk, step, 0, unroll=4)

    def col(off):
        return lambda b, h: (b, 0, off // HD + HPG * g + h)

    blk = (None, S, HD)
    tab = pl.BlockSpec((S, HD), lambda b, h: (0, 0))
    out = pl.BlockSpec(blk, lambda b, h: (b, 0, h))
    kv = pl.BlockSpec((None, None, d, ln + 128, HD), lambda b, h: (b, h, 0, 0, 0))
    o, l, kr, vr = pl.pallas_call(
        body, name=f"attn_fwd_d{d}", grid=(NB, HPG),
        in_specs=[pl.BlockSpec(blk, col(QA)), pl.BlockSpec(blk, col(KA)), pl.BlockSpec(blk, col(VA)), tab, tab],
        out_specs=[out, out, kv, kv],
        out_shape=[jax.ShapeDtypeStruct((NB, S, HPG * HD), F32)] * 2
        + [jax.ShapeDtypeStruct((NB, HPG, d, ln + 128, HD), BF16)] * 2,
        compiler_params=_cp(("parallel", "parallel")),
    )(proj_v, proj_v, proj_v, cosf, sinf)
    return o.reshape(T, HPG * HD), l.reshape(T, HPG * HD), kr, vr


def _attn_bwd(proj, kr, vr, cosf, sinf, da, lse, delta, dproj, g):
    d = DIL[g]
    ln = S // d
    nblk = ln // 128
    proj_v = proj.reshape(NB, S, NC)
    dproj_v = dproj.reshape(NB, S, NC)
    da_v = da.reshape(NB, S, AW)
    lse_v = lse.reshape(NB, S, HPG * HD)
    delta_v = delta.reshape(NB, S, HPG * HD)

    def body(q_ref, k_s, v_s, cos_ref, sin_ref, da_ref, lse_ref, dl_ref, dp_in_ref, o_ref, stg, dk_s, dv_s):
        del dp_in_ref
        w = pl.program_id(2)

        def emit():
            def cast(n, carry):
                rows = pl.ds(pl.multiple_of(n * 256, 256), 256)
                o_ref[rows, :] = stg[rows, :].astype(BF16)
                return carry

            lax.fori_loop(0, S // 256, cast, 0)

        @pl.when(w == 0)
        def _():
            dk_s[...] = jnp.zeros_like(dk_s)
            dv_s[...] = jnp.zeros_like(dv_s)

            def step(i, carry):
                r, n = i // nblk, i % nblk
                rows = _dil_rows(r, n, d)
                r0 = pl.multiple_of(n * 128, 128)
                win, mask = _window(r0, nblk)
                cos, sin = cos_ref[rows, :], sin_ref[rows, :]
                qr = _rope(q_ref[rows, :], cos, sin, HD // 2).astype(BF16)
                kw = k_s[r, win, :]
                vw = v_s[r, win, :]
                sc = lax.dot_general(qr, kw, NT, preferred_element_type=F32) * A_SCALE
                sc = jnp.where(mask, sc, NEG)
                p = jnp.exp(sc - lse_ref[rows, :][:, :1])
                da_b = da_ref[rows, :].astype(BF16)
                dp = lax.dot_general(da_b, vw, NT, preferred_element_type=F32)
                ds_b = (p * (dp - dl_ref[rows, :][:, :1]) * A_SCALE).astype(BF16)
                p_b = p.astype(BF16)
                dq = jnp.dot(ds_b, kw, preferred_element_type=F32)
                stg[rows, :] = _rope_t(dq, cos, sin, HD // 2)
                dk_s[r, win, :] += lax.dot_general(ds_b, qr, TN, preferred_element_type=F32)
                dv_s[r, win, :] += lax.dot_general(p_b, da_b, TN, preferred_element_type=F32)
                return carry

            lax.fori_loop(0, d * nblk, step, 0, unroll=4)
            emit()

        @pl.when(w == 1)
        def _():
            def put(i, carry):
                r, n = i // nblk, i % nblk
                rows = _dil_rows(r, n, d)
                src = pl.ds(pl.multiple_of(n * 128 + 128, 128), 128)
                stg[rows, :] = _rope_t(dk_s[r, src, :], cos_ref[rows, :], sin_ref[rows, :], HD // 2)
                return carry

            lax.fori_loop(0, d * nblk, put, 0, unroll=4)
            emit()

        @pl.when(w == 2)
        def _():
            def put(i, carry):
                r, n = i // nblk, i % nblk
                src = pl.ds(pl.multiple_of(n * 128 + 128, 128), 128)
                stg[_dil_rows(r, n, d), :] = dv_s[r, src, :]
                return carry

            lax.fori_loop(0, d * nblk, put, 0, unroll=4)
            emit()

    def col(off):
        return lambda b, h, w: (ahead(b, h, w)[0], 0, off // HD + HPG * g + ahead(b, h, w)[1])

    def ahead(b, h, w):
        flat = jnp.minimum(b * HPG + h + jnp.where(w > 0, 1, 0), NB * HPG - 1)
        return flat // HPG, flat % HPG

    blk = (None, S, HD)
    tab = pl.BlockSpec((S, HD), lambda b, h, w: (0, 0))
    per_head = pl.BlockSpec(blk, lambda b, h, w: (ahead(b, h, w)[0], 0, ahead(b, h, w)[1]))
    kv = pl.BlockSpec((None, None, d, ln + 128, HD), lambda b, h, w: (*ahead(b, h, w), 0, 0, 0))
    out = pl.pallas_call(
        body, name=f"attn_bwd_d{d}", grid=(NB, HPG, 3),
        in_specs=[pl.BlockSpec(blk, col(QA)), kv, kv, tab, tab,
                  pl.BlockSpec(blk, col(0)), per_head, per_head, pl.BlockSpec(memory_space=pl.ANY)],
        out_specs=pl.BlockSpec(blk, lambda b, h, w: (b, 0, (AW // HD) * w + HPG * g + h)),
        out_shape=jax.ShapeDtypeStruct(dproj_v.shape, BF16),
        input_output_aliases={8: 0},
        scratch_shapes=[pltpu.VMEM((S, HD), F32), pltpu.VMEM((d, ln + 128, HD), F32), pltpu.VMEM((d, ln + 128, HD), F32)],
        compiler_params=_cp(("arbitrary",) * 3),
    )(proj_v, kr, vr, cosf, sinf, da_v, lse_v, delta_v, dproj_v)
    return out.reshape(T, NC)


def _attn_mix(proj, os_, ls_, dep=None):
    tr = 256
    gw = HPG * HD
    deps = [] if dep is None else [dep]

    def body(o0, o1, o2, l0, l1, l2, z_ref, *rest):
        cat_ref = rest[-1]
        m = jnp.maximum(jnp.maximum(l0[...], l1[...]), l2[...])
        e = [jnp.exp(l[...] - m) for l in (l0, l1, l2)]
        inv = 1.0 / (e[0] + e[1] + e[2])
        for gi, o in enumerate((o0, o1, o2)):
            z = z_ref[:, gi * gw:(gi + 1) * gw]
            cat_ref[:, gi * gw:(gi + 1) * gw] = (o[...] * (e[gi] * inv) * (z * _sigmoid(z))).astype(BF16)

    grp = pl.BlockSpec((tr, gw), lambda i: (i, 0))
    return pl.pallas_call(
        body, name="attn_mix", grid=(T // tr,),
        in_specs=[grp] * 6 + [pl.BlockSpec((tr, AW), lambda i: (i, ZA // AW))] + [ANY_SPEC] * len(deps),
        out_specs=pl.BlockSpec((tr, AW), lambda i: (i, 0)),
        out_shape=jax.ShapeDtypeStruct((T, D), BF16),
        compiler_params=_cp(("parallel",)),
    )(*os_, *ls_, proj, *deps)


def _attn_mix_bwd(dcat, proj, os_, ls_, dep=None):
    tr = 256
    gw = HPG * HD
    deps = [] if dep is None else [dep]

    def body(dy_ref, o0, o1, o2, l0, l1, l2, z_ref, *rest):
        da_ref, lse_ref, dl_ref, dz_ref = rest[len(deps):]
        m = jnp.maximum(jnp.maximum(l0[...], l1[...]), l2[...])
        e = [jnp.exp(l[...] - m) for l in (l0, l1, l2)]
        den = e[0] + e[1] + e[2]
        inv = 1.0 / den
        lse_ref[...] = m + jnp.log(den)
        acc = jnp.zeros((tr, gw), F32)
        for gi, o in enumerate((o0, o1, o2)):
            cols = slice(gi * gw, (gi + 1) * gw)
            z = z_ref[:, cols]
            dy = dy_ref[:, cols]
            sg = _sigmoid(z)
            a = o[...] * (e[gi] * inv)
            da = dy * (z * sg)
            da_ref[:, cols] = da
            dz_ref[:, cols] = (dy * a * (sg * (1.0 + z * (1.0 - sg)))).astype(BF16)
            acc = acc + da * a
        for hh in range(HPG):
            cols = slice(hh * HD, (hh + 1) * HD)
            dl_ref[:, cols] = jnp.broadcast_to(jnp.sum(acc[:, cols], axis=1, keepdims=True), (tr, HD))

    grp = pl.BlockSpec((tr, gw), lambda i: (i, 0))
    return pl.pallas_call(
        body, name="attn_mix_bwd", grid=(T // tr,),
        in_specs=[pl.BlockSpec((tr, AW), lambda i: (i, 0))] + [grp] * 6 + [pl.BlockSpec((tr, AW), lambda i: (i, ZA // AW))]
        + [pl.BlockSpec(memory_space=pl.ANY)] * len(deps),
        out_specs=[pl.BlockSpec((tr, AW), lambda i: (i, 0)), grp, grp, pl.BlockSpec((tr, AW), lambda i: (i, ZA // AW))],
        out_shape=[jax.ShapeDtypeStruct((T, AW), F32), jax.ShapeDtypeStruct((T, gw), F32),
                   jax.ShapeDtypeStruct((T, gw), F32), jax.ShapeDtypeStruct((T, NC), BF16)],
        compiler_params=_cp(("parallel",)),
    )(dcat, *os_, *ls_, proj, *deps)


CT = 256


def _shift_down(x, n):
    rows = lax.broadcasted_iota(jnp.int32, x.shape, 0)
    return jnp.where(rows >= n, pltpu.roll(x, n, 0), 0.0)


def _shift_up(x, n):
    rows = lax.broadcasted_iota(jnp.int32, x.shape, 0)
    return jnp.where(rows < x.shape[0] - n, pltpu.roll(x, x.shape[0] - n, 0), 0.0)


def _conv_fwd(proj, conv_w, cat):
    proj_v = proj.reshape(NB, S, NC)
    cat_v = cat.reshape(NB, S, D)

    def body(u_ref, b_ref, c_ref, z_ref, w_ref, cat_in, o_ref):
        del cat_in
        cu = c_ref[...] * u_ref[...]
        cv = _shift_down(cu, 2) * w_ref[0:1, :] + _shift_down(cu, 1) * w_ref[1:2, :] + cu * w_ref[2:3, :]
        z = z_ref[...]
        o_ref[...] = (b_ref[...] * cv * (z * _sigmoid(z))).astype(BF16)

    def seg(off):
        return pl.BlockSpec((None, S, CT), lambda b, j: (b, 0, off // CT + j))

    out = pl.pallas_call(
        body, name="conv_fwd", grid=(NB, CW // CT),
        in_specs=[seg(UC), seg(BC), seg(CC), seg(ZC), pl.BlockSpec((3, CT), lambda b, j: (0, j)),
                  pl.BlockSpec(memory_space=pl.ANY)],
        out_specs=pl.BlockSpec((None, S, CT), lambda b, j: (b, 0, AW // CT + j)),
        out_shape=jax.ShapeDtypeStruct((NB, S, D), BF16),
        input_output_aliases={5: 0},
        compiler_params=_cp(("parallel", "parallel")),
    )(proj_v, proj_v, proj_v, proj_v, conv_w, cat_v)
    return out.reshape(T, D)


def _conv_bwd(dcat, proj, conv_w, dproj, dep=None):
    deps = [] if dep is None else [dep]
    proj_v = proj.reshape(NB, S, NC)
    dproj_v = dproj.reshape(NB, S, NC)
    dcat_v = dcat.reshape(NB, S, D)

    def body(dy_ref, u_ref, b_ref, c_ref, z_ref, w_ref, *rest):
        o_ref, dw_ref, st = rest[1 + len(deps):]
        b = pl.program_id(1)
        w = pl.program_id(2)

        @pl.when((b == 0) & (w == 0))
        def _():
            dw_ref[...] = jnp.zeros_like(dw_ref)

        @pl.when(w == 0)
        def _():
            u, c, z, bb, dy = u_ref[...], c_ref[...], z_ref[...], b_ref[...], dy_ref[...]
            cu = c * u
            s1 = _shift_down(cu, 1)
            s2 = _shift_down(cu, 2)
            cv = s2 * w_ref[0:1, :] + s1 * w_ref[1:2, :] + cu * w_ref[2:3, :]
            sg = _sigmoid(z)
            sz = z * sg
            dcv = dy * bb * sz
            st[0] = dy * cv * sz
            st[2] = dy * bb * cv * (sg * (1.0 + z * (1.0 - sg)))
            dw_ref[0:1, :] += jnp.sum(dcv * s2, axis=0, keepdims=True)
            dw_ref[1:2, :] += jnp.sum(dcv * s1, axis=0, keepdims=True)
            dw_ref[2:3, :] += jnp.sum(dcv * cu, axis=0, keepdims=True)
            dcu = dcv * w_ref[2:3, :] + _shift_up(dcv, 1) * w_ref[1:2, :] + _shift_up(dcv, 2) * w_ref[0:1, :]
            st[1] = dcu * u
            o_ref[...] = (dcu * c).astype(BF16)

        for k in range(3):
            @pl.when(w == k + 1)
            def _(k=k):
                o_ref[...] = st[k].astype(BF16)

    def ahead(j, b, w):
        flat = jnp.minimum(j * NB + b + jnp.where(w > 0, 1, 0), (CW // CT) * NB - 1)
        return flat // NB, flat % NB

    def seg(off):
        return pl.BlockSpec((None, S, CT), lambda j, b, w: (ahead(j, b, w)[1], 0, off // CT + ahead(j, b, w)[0]))

    out, dw = pl.pallas_call(
        body, name="conv_bwd", grid=(CW // CT, NB, 4),
        in_specs=[seg(AW), seg(UC), seg(BC), seg(CC), seg(ZC),
                  pl.BlockSpec((3, CT), lambda j, b, w: (0, ahead(j, b, w)[0])),
                  pl.BlockSpec(memory_space=pl.ANY)] + [pl.BlockSpec(memory_space=pl.ANY)] * len(deps),
        out_specs=[pl.BlockSpec((None, S, CT), lambda j, b, w: (b, 0, (UC + w * CW) // CT + j)),
                   pl.BlockSpec((3, CT), lambda j, b, w: (0, j))],
        out_shape=[jax.ShapeDtypeStruct((NB, S, NC), BF16), jax.ShapeDtypeStruct((3, CW), F32)],
        input_output_aliases={6: 0},
        scratch_shapes=[pltpu.VMEM((3, S, CT), F32)],
        compiler_params=_cp(("arbitrary",) * 3),
    )(dcat_v, proj_v, proj_v, proj_v, proj_v, conv_w, dproj_v, *deps)
    return out.reshape(T, NC), dw


XT = 1024


def _cross_fwd(proj, mkv, cosq, sinq, cosm, sinm, cat):
    proj_v = proj.reshape(NB, S, NC)
    mkv_v = mkv.reshape(NB, MLEN, 2 * XW)
    cat_v = cat.reshape(NB, S, D)

    def body(q_ref, z_ref, mk_ref, mv_ref, cq, sq, cm, sm, cat_in, o_ref):
        del cat_in
        mkr = _rope(mk_ref[...], cm[...], sm[...], XHD // 2).astype(BF16)
        qr = _rope(q_ref[...], cq[...], sq[...], XHD // 2).astype(BF16)
        sc = lax.dot_general(qr, mkr, NT, preferred_element_type=F32) * X_SCALE
        p = jnp.exp(sc - jnp.max(sc, axis=1, keepdims=True))
        p = p / jnp.sum(p, axis=1, keepdims=True)
        ox = jnp.dot(p.astype(BF16), mv_ref[...].astype(BF16), preferred_element_type=F32)
        z = z_ref[...]
        o_ref[...] = (ox * (z * _sigmoid(z))).astype(BF16)

    def seg(off):
        return pl.BlockSpec((None, XT, XHD), lambda b, h, t: (b, t, off // XHD + h))

    qtab = pl.BlockSpec((XT, XHD), lambda b, h, t: (t, 0))
    mtab = pl.BlockSpec((MLEN, XHD), lambda b, h, t: (0, 0))
    out = pl.pallas_call(
        body, name="cross_fwd", grid=(NB, NXH, S // XT),
        in_specs=[seg(QX), seg(ZX),
                  pl.BlockSpec((None, MLEN, XHD), lambda b, h, t: (b, 0, h)),
                  pl.BlockSpec((None, MLEN, XHD), lambda b, h, t: (b, 0, NXH + h)),
                  qtab, qtab, mtab, mtab, pl.BlockSpec(memory_space=pl.ANY)],
        out_specs=pl.BlockSpec((None, XT, XHD), lambda b, h, t: (b, t, (AW + CW) // XHD + h)),
        out_shape=jax.ShapeDtypeStruct((NB, S, D), BF16),
        input_output_aliases={8: 0},
        compiler_params=_cp(("parallel",) * 3),
    )(proj_v, proj_v, mkv_v, mkv_v, cosq, sinq, cosm, sinm, cat_v)
    return out.reshape(T, D)


def _cross_bwd(dcat, proj, mkv, cosq, sinq, cosm, sinm, dproj):
    proj_v = proj.reshape(NB, S, NC)
    dproj_v = dproj.reshape(NB, S, NC)
    dcat_v = dcat.reshape(NB, S, D)
    mkv_v = mkv.reshape(NB, MLEN, 2 * XW)
    nt = S // XT

    def body(dy_ref, q_ref, z_ref, mk_ref, mv_ref, cq, sq, cm, sm, dp_in, o_ref, dmk_ref, dmv_ref, dz_s):
        del dp_in
        t = pl.program_id(2)
        w = pl.program_id(3)

        @pl.when((t == 0) & (w == 0))
        def _():
            dmk_ref[...] = jnp.zeros_like(dmk_ref)
            dmv_ref[...] = jnp.zeros_like(dmv_ref)

        @pl.when(w == 0)
        def _():
            mkr = _rope(mk_ref[...], cm[...], sm[...], XHD // 2).astype(BF16)
            mv_b = mv_ref[...].astype(BF16)
            qr = _rope(q_ref[...], cq[...], sq[...], XHD // 2).astype(BF16)
            sc = lax.dot_general(qr, mkr, NT, preferred_element_type=F32) * X_SCALE
            p = jnp.exp(sc - jnp.max(sc, axis=1, keepdims=True))
            p = p / jnp.sum(p, axis=1, keepdims=True)
            p_b = p.astype(BF16)
            ox = jnp.dot(p_b, mv_b, preferred_element_type=F32)
            z = z_ref[...]
            dy = dy_ref[...]
            sg = _sigmoid(z)
            dz_s[...] = dy * ox * (sg * (1.0 + z * (1.0 - sg)))
            dox_b = (dy * (z * sg)).astype(BF16)
            dp = lax.dot_general(dox_b, mv_b, NT, preferred_element_type=F32)
            ds_b = (p * (dp - jnp.sum(dp * p, axis=1, keepdims=True)) * X_SCALE).astype(BF16)
            dq = jnp.dot(ds_b, mkr, preferred_element_type=F32)
            o_ref[...] = _rope_t(dq, cq[...], sq[...], XHD // 2).astype(BF16)
            dmk_ref[...] += lax.dot_general(ds_b, qr, TN, preferred_element_type=F32)
            dmv_ref[...] += lax.dot_general(p_b, dox_b, TN, preferred_element_type=F32)

        @pl.when(w == 1)
        def _():
            o_ref[...] = dz_s[...].astype(BF16)

        @pl.when((t == nt - 1) & (w == 1))
        def _():
            dmk_ref[...] = _rope_t(dmk_ref[...], cm[...], sm[...], XHD // 2)

    def seg(off):
        return pl.BlockSpec((None, XT, XHD), lambda b, h, t, w: (b, t, off // XHD + h))

    qtab = pl.BlockSpec((XT, XHD), lambda b, h, t, w: (t, 0))
    mtab = pl.BlockSpec((MLEN, XHD), lambda b, h, t, w: (0, 0))
    macc = pl.BlockSpec((None, MLEN, XHD), lambda b, h, t, w: (b, 0, h))
    out, dmk, dmv = pl.pallas_call(
        body, name="cross_bwd", grid=(NB, NXH, nt, 2),
        in_specs=[pl.BlockSpec((None, XT, XHD), lambda b, h, t, w: (b, t, (AW + CW) // XHD + h)),
                  seg(QX), seg(ZX),
                  pl.BlockSpec((None, MLEN, XHD), lambda b, h, t, w: (b, 0, h)),
                  pl.BlockSpec((None, MLEN, XHD), lambda b, h, t, w: (b, 0, NXH + h)),
                  qtab, qtab, mtab, mtab, pl.BlockSpec(memory_space=pl.ANY)],
        out_specs=[pl.BlockSpec((None, XT, XHD), lambda b, h, t, w: (b, t, (QX + w * XW) // XHD + h)), macc, macc],
        out_shape=[jax.ShapeDtypeStruct((NB, S, NC), BF16), jax.ShapeDtypeStruct((NB, MLEN, XW), F32),
                   jax.ShapeDtypeStruct((NB, MLEN, XW), F32)],
        input_output_aliases={9: 0},
        scratch_shapes=[pltpu.VMEM((XT, XHD), F32)],
        compiler_params=_cp(("arbitrary",) * 4),
    )(dcat_v, proj_v, proj_v, mkv_v, mkv_v, cosq, sinq, cosm, sinm, dproj_v)
    return out.reshape(T, NC), dmk, dmv


def _local_step(x, mem, pre_norm, conv_w, mem_norm, post_norm, tgt, chip, core, comm, moments=None):
    x2 = x.reshape(T, D)
    mem2 = mem.reshape(NB * MLEN, D)
    tgt2 = tgt.reshape(T, D)
    cosa, sina = _rope_tables(jnp.arange(S), HD // 2)
    cosq, sinq = _rope_tables(jnp.arange(S) + MLEN, XHD // 2)
    cosm, sinm = _rope_tables(jnp.arange(MLEN), XHD // 2)

    h = _rms_fwd(x2, pre_norm, "pre_norm_fwd", dep=comm.gather_started())
    memn = _rms_fwd(mem2, mem_norm, "mem_norm_fwd")
    proj = _proj_part(h, comm.w_in_own(), chip, (0,), "proj_own")
    wg_in, tok = comm.w_in_near(after=[proj, memn, conv_w])
    proj = _proj_part(h, wg_in, chip, (REL_XOR[0], REL_XOR[1]), "proj_near", prev=proj, dep=tok)
    wg_in, tok = comm.w_in_all(after=proj)
    proj = _proj_part(h, wg_in, chip, (REL_XOR[2],), "proj_far", prev=proj, dep=tok)
    fw = [_attn_fwd(proj, cosa, sina, g) for g in range(3)]
    os_ = [f[0] for f in fw]
    ls_ = [f[1] for f in fw]
    tok = comm.w_rest_landed(after=ls_)
    cat = _attn_mix(proj, os_, ls_, dep=tok)
    cat = _conv_fwd(proj, conv_w, cat)
    wg_kv, wg_out = comm.w_rest(after=[cat])
    mkv = _mm_nn(memn, wg_kv, "mkv", NB * MLEN, 1024, 512)
    cat = _cross_fwd(proj, mkv, cosq, sinq, cosm, sinm, cat)
    y = _mm_nn(cat, wg_out, "out_proj", 1024, 1024, 2048)
    dy, dout, d_post, loss = _post_norm_loss(y, x2, tgt2, post_norm)

    dcat = _mm_nt(dy, wg_out, "dcat", 1024, 1024, 2048)
    g_send = _grad_rows(cat, dy,1 - core, BF16, "grad_w_out_send")
    sent = comm.sibling_start(g_send.reshape(NCHIP, D // NCHIP // 2, D), "a")
    g_keep = _grad_rows(cat, dy,core, BF16, "grad_w_out_keep", dep=sent[-1])
    red_a = comm.reduce_start_summed(g_keep.reshape(NCHIP, D // NCHIP // 2, D), sent, "a")
    da, lse, delta, dproj = _attn_mix_bwd(dcat, proj, os_, ls_, dep=red_a[-1])
    for g in range(3):
        dproj = _attn_bwd(proj, fw[g][2], fw[g][3], cosa, sina, da, lse, delta, dproj, g)
    dproj, dmk, dmv = _cross_bwd(dcat, proj, mkv, cosq, sinq, cosm, sinm, dproj)
    dmkv = jnp.concatenate([dmk, dmv], axis=-1).reshape(NB * MLEN, 2 * XW)
    dmkv_b = dmkv.astype(BF16)
    gk_send = _grad_rows(memn, dmkv_b, 1 - core, BF16, "grad_w_mem_kv_send")
    sent = comm.sibling_start(gk_send.reshape(NCHIP, D // NCHIP // 2, 2 * XW), "k")
    gk_keep = _grad_rows(memn, dmkv_b, core, BF16, "grad_w_mem_kv_keep", dep=sent[-1])
    dmemn = _mm_nt(dmkv_b, wg_kv, "dmemn", NB * MLEN, 1024, 512)
    d_mem = _norm_gain_grad(dmemn, mem2, "mem_norm_bwd")
    red_k = comm.reduce_start_summed(gk_keep.reshape(NCHIP, D // NCHIP // 2, 2 * XW), sent, "k", after=d_mem)
    dproj, d_conv = _conv_bwd(dcat, proj, conv_w, dproj, dep=red_k[-1])
    (r_out,) = comm.reduce_finish(red_a, "a", after=d_conv)
    (r_kv,) = comm.reduce_finish(red_k, "k", after=d_conv)

    g_send = _grad_w_in(h, dproj, 1 - core, BF16, "grad_w_in_send", dep=r_kv)
    sent = comm.sibling_start(g_send.reshape(NCHIP, D // 2, WB), "b")
    g_keep = _grad_w_in(h, dproj, core, BF16, "grad_w_in_keep", dep=sent[-1])
    red_b = comm.reduce_start_summed(g_keep.reshape(NCHIP, D // 2, WB), sent, "b")
    sets = [] if moments is None else [(moments["w_out"][0], r_out, *moments["w_out"][1:]),
                                      (moments["w_mem_kv"][0], r_kv, *moments["w_mem_kv"][1:])]
    dh, updates = _dh(dproj, wg_in, dep=red_b[-1], adam=sets)
    grad_x, d_pre = _pre_norm_bwd(dh, x2, pre_norm, dout, 0)
    r_in = comm.reduce_finish_start(red_b, "b", after=grad_x)
    grad_x, d_pre = _pre_norm_bwd(dh, x2, pre_norm, dout, 1, prev=(grad_x, d_pre), dep=r_in[-1])
    return loss, grad_x.reshape(NB, S, D), d_pre, d_mem, d_post, d_conv, r_in, (r_kv, r_out), updates


def _adamw(w, g, m, v, name):
    rows, cols = w.shape
    tr = rows if rows <= 512 else 512
    tc = cols if cols <= 1024 else 1024
    if cols % tc:
        tc = 896

    blk = pl.BlockSpec((tr, tc), lambda i, j: (i, j))
    sds = jax.ShapeDtypeStruct((rows, cols), F32)
    def body(*refs):
        _adamw_math(*refs)

    return pl.pallas_call(
        body, name=name, grid=(rows // tr, cols // tc),
        in_specs=[blk] * 4, out_specs=[blk] * 4, out_shape=[sds] * 4,
        compiler_params=_cp(("parallel", "parallel")),
    )(w, g, m, v)


def _place():
    return lax.axis_index("x"), lax.axis_index("y"), lax.axis_index("c")


def _other_chips(x, y):
    return [(1 - x, y), (x, 1 - y), (1 - x, 1 - y)]


def _tile_cols(cols):
    return cols if cols <= 1024 else (1024 if cols % 1024 == 0 else 896)


def _cast_own(w, chip, name, half=None, prev=None):
    rows, cols = w.shape
    tr, tc = 512, _tile_cols(cols)
    nrt = rows // tr if half is None else rows // tr // 2
    index = jnp.stack([chip, 0 if half is None else half]).astype(jnp.int32)
    extra = [] if prev is None else [prev]

    def body(ix_ref, w_ref, *rest):
        rest[-1][...] = w_ref[...].astype(BF16)

    grid_spec = pltpu.PrefetchScalarGridSpec(
        num_scalar_prefetch=1, grid=(nrt, cols // tc),
        in_specs=[pl.BlockSpec((tr, tc), lambda i, j, ix: (ix[1] * nrt + i, j))] + [ANY_SPEC] * len(extra),
        out_specs=pl.BlockSpec((None, tr, tc), lambda i, j, ix: (ix[0], ix[1] * nrt + i, j)))
    return pl.pallas_call(
        body, name=name, grid_spec=grid_spec,
        out_shape=jax.ShapeDtypeStruct((NCHIP, rows, cols), BF16),
        input_output_aliases={} if prev is None else {2: 0},
        compiler_params=_cp(("parallel", "parallel")),
    )(index, w, *extra)


HBM_SPEC = pl.BlockSpec(memory_space=pltpu.HBM)
SEM_SPEC = pl.BlockSpec(memory_space=pltpu.SEMAPHORE)
ANY_SPEC = pl.BlockSpec(memory_space=pl.ANY)
EFFECT = pltpu.SideEffectType.DATAFLOW_SIDE_EFFECTING
TOKEN = jax.ShapeDtypeStruct((8, 128), F32)


def _half(ref, chip, hc):
    hr = ref.shape[1] // 2
    return ref.at[chip, pl.ds(hc * hr, hr), :]


NEAR = (0, 1)
FAR = (2,)
REL_XOR = (2, 1, 3)


def _gather_copies(refs, send_sems, recv_sems, rels):
    x, y, c = _place()
    chips = _other_chips(x, y)
    out, inc = [], []
    for a, ref in enumerate(refs):
        for p, j in enumerate(rels):
            px, py = chips[j]
            mine = _half(ref, 2 * x + y, c)
            theirs = _half(ref, 2 * px + py, c)
            sems = dict(send_sem=send_sems.at[len(rels) * a + p], recv_sem=recv_sems.at[len(rels) * a + p],
                        device_id=(px, py, c), device_id_type=MESH)
            out.append(pltpu.make_async_remote_copy(src_ref=mine, dst_ref=mine, **sems))
            inc.append(pltpu.make_async_remote_copy(src_ref=theirs, dst_ref=theirs, **sems))
    return out, inc


def _gather_start(bufs, groups, name):
    n = len(bufs)
    ng = len(groups)

    def body(*refs):
        ins = refs[:n]
        token = refs[-1]
        for gi, rels in enumerate(groups):
            out, _ = _gather_copies(ins, refs[n + 2 * gi], refs[n + 2 * gi + 1], rels)
            for cp in out:
                cp.start()
        token[...] = jnp.zeros_like(token)

    sems = []
    for rels in groups:
        sems += [pltpu.SemaphoreType.DMA((len(rels) * n,))] * 2
    res = pl.pallas_call(
        body, name=name,
        in_specs=[HBM_SPEC] * n,
        out_specs=[SEM_SPEC] * (2 * ng) + [HBM_SPEC] * n + [pl.BlockSpec(memory_space=pltpu.VMEM)],
        out_shape=sems + [pltpu.HBM(b.shape, b.dtype) for b in bufs] + [TOKEN],
        input_output_aliases={a: 2 * ng + a for a in range(n)},
        compiler_params=pltpu.CompilerParams(has_side_effects=EFFECT),
    )(*[pltpu.with_memory_space_constraint(b, pltpu.HBM) for b in bufs])
    return [(res[2 * gi], res[2 * gi + 1]) for gi in range(ng)], list(res[2 * ng:2 * ng + n]), res[-1]


def _gather_wait(bufs, sems, rels, after, name):
    n = len(bufs)
    send_sems, recv_sems = sems
    after = list(after) if isinstance(after, (list, tuple)) else [after]

    def body(*refs):
        ins = refs[:n]
        out, inc = _gather_copies(ins, refs[n], refs[n + 1], rels)
        for cp in out:
            cp.wait_send()
        for cp in inc:
            cp.wait_recv()

    return pl.pallas_call(
        body, name=name,
        in_specs=[HBM_SPEC] * n + [SEM_SPEC, SEM_SPEC] + [ANY_SPEC] * len(after),
        out_specs=[HBM_SPEC] * n,
        out_shape=[pltpu.HBM(b.shape, b.dtype) for b in bufs],
        input_output_aliases={a: a for a in range(n)},
        compiler_params=pltpu.CompilerParams(has_side_effects=EFFECT),
    )(*bufs, send_sems, recv_sems, *after)


def _forward_halves(bufs, rels, name):
    n = len(bufs)

    def body(*refs):
        cps, waits = _forward_copies(refs[n:2 * n], rels, refs[2 * n], refs[2 * n + 1])
        for cp in cps:
            cp.start()
        for cp in waits:
            cp.wait_recv()
        for cp in cps:
            cp.wait_send()

    return pl.pallas_call(
        body, name=name,
        in_specs=[ANY_SPEC] * n, out_specs=[ANY_SPEC] * n,
        out_shape=[jax.ShapeDtypeStruct(s.shape, s.dtype) for s in bufs],
        input_output_aliases={a: a for a in range(n)},
        scratch_shapes=[pltpu.SemaphoreType.DMA((len(rels) * n,)), pltpu.SemaphoreType.DMA((len(rels) * n,))],
    )(*bufs)


def _forward_copies(refs, rels, send_sems, recv_sems):
    x, y, c = _place()
    chips = _other_chips(x, y)
    cps, waits = [], []
    for a, ref in enumerate(refs):
        for p, j in enumerate(rels):
            px, py = chips[j]
            sems = dict(send_sem=send_sems.at[len(rels) * a + p], recv_sem=recv_sems.at[len(rels) * a + p],
                        device_id=(x, y, 1 - c), device_id_type=MESH)
            got = _half(ref, 2 * px + py, c)
            want = _half(ref, 2 * px + py, 1 - c)
            cps.append(pltpu.make_async_remote_copy(src_ref=got, dst_ref=got, **sems))
            waits.append(pltpu.make_async_remote_copy(src_ref=want, dst_ref=want, **sems))
    return cps, waits


def _forward_start(bufs, rels, name):
    n = len(bufs)

    def body(*refs):
        cps, _ = _forward_copies(refs[:n], rels, refs[n], refs[n + 1])
        for cp in cps:
            cp.start()
        refs[-1][...] = jnp.zeros_like(refs[-1])

    res = pl.pallas_call(
        body, name=name,
        in_specs=[HBM_SPEC] * n,
        out_specs=[SEM_SPEC, SEM_SPEC] + [HBM_SPEC] * n + [pl.BlockSpec(memory_space=pltpu.VMEM)],
        out_shape=[pltpu.SemaphoreType.DMA((len(rels) * n,))] * 2 + [pltpu.HBM(b.shape, b.dtype) for b in bufs] + [TOKEN],
        input_output_aliases={a: 2 + a for a in range(n)},
        compiler_params=pltpu.CompilerParams(has_side_effects=EFFECT),
    )(*[pltpu.with_memory_space_constraint(b, pltpu.HBM) for b in bufs])
    return (res[0], res[1]), list(res[2:2 + n]), res[-1]


def _forward_wait(bufs, sems, rels, after, name):
    n = len(bufs)

    def body(*refs):
        cps, waits = _forward_copies(refs[:n], rels, refs[n], refs[n + 1])
        for cp in cps:
            cp.wait_send()
        for cp in waits:
            cp.wait_recv()

    return pl.pallas_call(
        body, name=name,
        in_specs=[HBM_SPEC] * n + [SEM_SPEC, SEM_SPEC] + [ANY_SPEC] * len(after),
        out_specs=[HBM_SPEC] * n,
        out_shape=[pltpu.HBM(b.shape, b.dtype) for b in bufs],
        input_output_aliases={a: a for a in range(n)},
        compiler_params=pltpu.CompilerParams(has_side_effects=EFFECT),
    )(*bufs, sems[0], sems[1], *after)


def _sibling_copy(src, land, send_sems, recv_sems):
    x, y, c = _place()
    return pltpu.make_async_remote_copy(src_ref=src, dst_ref=land, send_sem=send_sems.at[0], recv_sem=recv_sems.at[0],
                                        device_id=(x, y, 1 - c), device_id_type=MESH)


def _sibling_start(part, name):
    def body(src, land, send_sems, recv_sems, src_thru, land_thru, token):
        _sibling_copy(src, land, send_sems, recv_sems).start()
        token[...] = jnp.zeros_like(token)

    land = lax.empty(part.shape, part.dtype)
    return pl.pallas_call(
        body, name=name,
        in_specs=[HBM_SPEC] * 2,
        out_specs=[SEM_SPEC, SEM_SPEC, HBM_SPEC, HBM_SPEC, pl.BlockSpec(memory_space=pltpu.VMEM)],
        out_shape=[pltpu.SemaphoreType.DMA((1,)), pltpu.SemaphoreType.DMA((1,)), pltpu.HBM(part.shape, part.dtype),
                   pltpu.HBM(part.shape, part.dtype), TOKEN],
        input_output_aliases={0: 2, 1: 3},
        compiler_params=pltpu.CompilerParams(has_side_effects=EFFECT),
    )(pltpu.with_memory_space_constraint(part, pltpu.HBM), pltpu.with_memory_space_constraint(land, pltpu.HBM))


def _sibling_wait(state, after, name):
    send_sems, recv_sems, part, land, _ = state

    def body(src, land_ref, send_ref, recv_ref, *rest):
        cp = _sibling_copy(src, land_ref, send_ref, recv_ref)
        cp.wait_send()
        cp.wait_recv()

    return pl.pallas_call(
        body, name=name,
        in_specs=[HBM_SPEC, HBM_SPEC, SEM_SPEC, SEM_SPEC] + [ANY_SPEC] * len(after),
        out_specs=[HBM_SPEC, HBM_SPEC],
        out_shape=[pltpu.HBM(part.shape, part.dtype), pltpu.HBM(land.shape, land.dtype)],
        input_output_aliases={0: 0, 1: 1},
        compiler_params=pltpu.CompilerParams(has_side_effects=EFFECT),
    )(part, land, send_sems, recv_sems, *after)[1]


def _pair_sum_rows(keep, got, name):
    nblk, rows, cols = keep.shape
    tr, tc = 512, _tile_cols(cols)

    def body(k_ref, g_ref, o_ref):
        o_ref[...] = (k_ref[...].astype(F32) + g_ref[...].astype(F32)).astype(BF16)

    blk = pl.BlockSpec((None, tr, tc), lambda b, i, j: (b, i, j))
    return pl.pallas_call(
        body, name=name, grid=(nblk, rows // tr, cols // tc),
        in_specs=[blk, blk], out_specs=blk,
        out_shape=jax.ShapeDtypeStruct(keep.shape, BF16),
        compiler_params=_cp(("parallel",) * 3),
    )(keep, got)


def _scatter_start(qs, name):
    n = len(qs)

    def body(*refs):
        ins, lands = refs[:n], refs[n:2 * n]
        token = refs[-1]
        for cp in _scatter_copies(ins, lands, refs[2 * n], refs[2 * n + 1]):
            cp.start()
        token[...] = jnp.zeros_like(token)

    lands = [lax.empty((3,) + q.shape[1:], q.dtype) for q in qs]
    res = pl.pallas_call(
        body, name=name,
        in_specs=[HBM_SPEC] * (2 * n),
        out_specs=[SEM_SPEC, SEM_SPEC] + [HBM_SPEC] * (2 * n) + [pl.BlockSpec(memory_space=pltpu.VMEM)],
        out_shape=[pltpu.SemaphoreType.DMA((3 * n,)), pltpu.SemaphoreType.DMA((3 * n,))]
        + [pltpu.HBM(b.shape, b.dtype) for b in qs + lands] + [TOKEN],
        input_output_aliases={a: 2 + a for a in range(2 * n)},
        compiler_params=pltpu.CompilerParams(has_side_effects=EFFECT),
    )(*[pltpu.with_memory_space_constraint(b, pltpu.HBM) for b in qs + lands])
    return res[0], res[1], list(res[2:2 + n]), list(res[2 + n:2 + 2 * n]), res[-1]


def _scatter_copies(ins, lands, send_sems, recv_sems):
    x, y, c = _place()
    cps = []
    for a in range(len(ins)):
        for j, (px, py) in enumerate(_other_chips(x, y)):
            cps.append(pltpu.make_async_remote_copy(
                src_ref=ins[a].at[2 * px + py], dst_ref=lands[a].at[j],
                send_sem=send_sems.at[3 * a + j], recv_sem=recv_sems.at[3 * a + j], device_id=(px, py, c), device_id_type=MESH))
    return cps


def _scatter_wait(qs, lands, send_sems, recv_sems, after, name):
    n = len(qs)

    def body(*refs):
        for cp in _scatter_copies(refs[:n], refs[n:2 * n], refs[2 * n], refs[2 * n + 1]):
            cp.wait_send()
            cp.wait_recv()

    res = pl.pallas_call(
        body, name=name,
        in_specs=[HBM_SPEC] * (2 * n) + [SEM_SPEC, SEM_SPEC, ANY_SPEC],
        out_specs=[HBM_SPEC] * (2 * n),
        out_shape=[pltpu.HBM(b.shape, b.dtype) for b in qs + lands],
        input_output_aliases={a: a for a in range(2 * n)},
        compiler_params=pltpu.CompilerParams(has_side_effects=EFFECT),
    )(*qs, *lands, send_sems, recv_sems, after)
    return list(res[:n]), list(res[n:])


def _chip_sum(q, got, name):
    _, hr, cols = got.shape
    tr, tc = 512, _tile_cols(cols)
    chip = 2 * lax.axis_index("x") + lax.axis_index("y")
    c = lax.axis_index("c")

    def body(idx_ref, q_ref, g_ref, o_ref):
        del idx_ref
        acc = q_ref[...].astype(F32)
        for i in range(3):
            acc = acc + g_ref[i].astype(F32)
        o_ref[...] = acc

    grid_spec = pltpu.PrefetchScalarGridSpec(
        num_scalar_prefetch=1, grid=(hr // tr, cols // tc),
        in_specs=[pl.BlockSpec((None, tr, tc), lambda i, j, ix: (ix[0], i, j)),
                  pl.BlockSpec((3, tr, tc), lambda i, j, ix: (0, i, j))],
        out_specs=pl.BlockSpec((None, tr, tc), lambda i, j, ix: (ix[1], i, j)))
    return pl.pallas_call(
        body, name=name, grid_spec=grid_spec,
        out_shape=jax.ShapeDtypeStruct((2, hr, cols), F32),
        compiler_params=_cp(("parallel", "parallel")),
    )(jnp.stack([chip, c]).astype(jnp.int32), q, got)


def _join_halves(bufs, name):
    n = len(bufs)

    def body(*refs):
        outs = refs[n:2 * n]
        send_sems, recv_sems = refs[2 * n:]
        x, y, c = _place()
        cps = []
        for a in range(n):
            cps.append(pltpu.make_async_remote_copy(src_ref=outs[a].at[c], dst_ref=outs[a].at[c], send_sem=send_sems.at[a],
                                                    recv_sem=recv_sems.at[a], device_id=(x, y, 1 - c), device_id_type=MESH))
        for cp in cps:
            cp.start()
        for a in range(n):
            theirs = outs[a].at[1 - c]
            pltpu.make_async_remote_copy(src_ref=theirs, dst_ref=theirs, send_sem=send_sems.at[a], recv_sem=recv_sems.at[a],
                                         device_id=(x, y, 1 - c), device_id_type=MESH).wait_recv()
        for cp in cps:
            cp.wait_send()

    hbm = pl.BlockSpec(memory_space=pl.ANY)
    return pl.pallas_call(
        body, name=name,
        in_specs=[hbm] * n, out_specs=[hbm] * n,
        out_shape=[jax.ShapeDtypeStruct(b.shape, b.dtype) for b in bufs],
        input_output_aliases={a: a for a in range(n)},
        scratch_shapes=[pltpu.SemaphoreType.DMA((n,)), pltpu.SemaphoreType.DMA((n,))],
    )(*bufs)


def _join_copy(buf, send_sems, recv_sems):
    x, y, c = _place()
    sems = dict(send_sem=send_sems.at[0], recv_sem=recv_sems.at[0], device_id=(x, y, 1 - c), device_id_type=MESH)
    return (pltpu.make_async_remote_copy(src_ref=buf.at[c], dst_ref=buf.at[c], **sems),
            pltpu.make_async_remote_copy(src_ref=buf.at[1 - c], dst_ref=buf.at[1 - c], **sems))


def _join_start(buf, name):
    def body(b_ref, send_sems, recv_sems, thru, token):
        _join_copy(b_ref, send_sems, recv_sems)[0].start()
        token[...] = jnp.zeros_like(token)

    return pl.pallas_call(
        body, name=name,
        in_specs=[HBM_SPEC],
        out_specs=[SEM_SPEC, SEM_SPEC, HBM_SPEC, pl.BlockSpec(memory_space=pltpu.VMEM)],
        out_shape=[pltpu.SemaphoreType.DMA((1,)), pltpu.SemaphoreType.DMA((1,)), pltpu.HBM(buf.shape, buf.dtype), TOKEN],
        input_output_aliases={0: 2},
        compiler_params=pltpu.CompilerParams(has_side_effects=EFFECT),
    )(pltpu.with_memory_space_constraint(buf, pltpu.HBM))


def _join_wait(state, after, name):
    send_sems, recv_sems, buf, _ = state

    def body(b_ref, send_ref, recv_ref, *rest):
        out, inc = _join_copy(b_ref, send_ref, recv_ref)
        out.wait_send()
        inc.wait_recv()

    return pl.pallas_call(
        body, name=name,
        in_specs=[HBM_SPEC, SEM_SPEC, SEM_SPEC] + [ANY_SPEC] * len(after),
        out_specs=HBM_SPEC,
        out_shape=pltpu.HBM(buf.shape, buf.dtype),
        input_output_aliases={0: 0},
        compiler_params=pltpu.CompilerParams(has_side_effects=EFFECT),
    )(buf, send_sems, recv_sems, *after)


N_DEV = 8


def _gather_small(block, reduce, name):
    m_per, cols = block.shape

    def body(x_ref, out_ref, all_ref, send_sems, recv_sems, local_sem):
        x, y, c = _place()
        me, sibling = (x, y, c), (x, y, 1 - c)
        chips = _other_chips(x, y)

        def rows(px, py, pc):
            return all_ref.at[pl.ds((4 * px + 2 * py + pc) * m_per, m_per), :]

        def copy(k, block_of, to, src=None):
            return pltpu.make_async_remote_copy(
                src_ref=rows(*block_of) if src is None else src, dst_ref=rows(*block_of),
                send_sem=send_sems.at[k], recv_sem=recv_sems.at[k], device_id=to, device_id_type=MESH)

        mine = pltpu.make_async_copy(x_ref, rows(*me), local_sem)
        mine.start()
        first = [copy(0, me, sibling, src=x_ref)]
        first += [copy(1 + j, me, (*chip, c), src=x_ref) for j, chip in enumerate(chips)]
        for cp in first:
            cp.start()
        passed = [copy(4 + j, (*chip, c), sibling) for j, chip in enumerate(chips)]
        for j, chip in enumerate(chips):
            copy(1 + j, (*chip, c), me).wait_recv()
            passed[j].start()
        copy(0, sibling, me).wait_recv()
        for j, chip in enumerate(chips):
            copy(4 + j, (*chip, 1 - c), me).wait_recv()
        for cp in first + passed:
            cp.wait_send()
        mine.wait()
        if reduce:
            acc = all_ref[pl.ds(0, m_per), :]
            for i in range(1, N_DEV):
                acc = acc + all_ref[pl.ds(i * m_per, m_per), :]
            out_ref[...] = acc
        else:
            out_ref[...] = all_ref[...]

    out_rows = m_per if reduce else N_DEV * m_per
    return pl.pallas_call(
        body, name=name,
        in_specs=[pl.BlockSpec(memory_space=pltpu.VMEM)],
        out_specs=pl.BlockSpec(memory_space=pltpu.VMEM),
        out_shape=jax.ShapeDtypeStruct((out_rows, cols), F32),
        scratch_shapes=[pltpu.VMEM((N_DEV * m_per, cols), F32), pltpu.SemaphoreType.DMA((7,)),
                        pltpu.SemaphoreType.DMA((7,)), pltpu.SemaphoreType.DMA],
    )(block)


class _Comm:
    def __init__(self, w_in, w_kv, w_out, chip, core):
        self.w_in, self.chip, self.core = w_in, chip, core
        self.bufs = [_cast_own(w_in, chip, "cast_w_in_sent", half=core), _cast_own(w_kv, chip, "cast_w_mem_kv"),
                     _cast_own(w_out, chip, "cast_w_out")]


    def gather_started(self):
        (self.sems,), (b_in,), tok = _gather_start(self.bufs[:1], (NEAR,), "gather_start_in_near")
        self.b_in = _cast_own(self.w_in, self.chip, "cast_w_in_kept", half=1 - self.core, prev=b_in)
        return [tok]

    def w_in_own(self):
        return self.b_in.reshape(NCHIP * D, WB)

    def w_in_near(self, after):
        (b_in,) = _gather_wait([self.b_in], self.sems, NEAR, list(after) + self.bufs[1:], "gather_wait_in_near")
        (b_in,) = _forward_halves([b_in], NEAR, "forward_in_near")
        (self.sems,), (self.b_in,), tok = _gather_start([b_in], (FAR,), "gather_start_in_far")
        return self.b_in.reshape(NCHIP * D, WB), tok

    def w_in_all(self, after):
        (b_in,) = _gather_wait([self.b_in], self.sems, FAR, after, "gather_wait_in_far")
        (b_in,) = _forward_halves([b_in], FAR, "forward_in_far")
        (self.sems,), self.b_rest, tok = _gather_start(self.bufs[1:], (NEAR + FAR,), "gather_start_rest")
        return b_in.reshape(NCHIP * D, WB), tok

    def w_rest_landed(self, after):
        b_rest = _gather_wait(self.b_rest, self.sems, NEAR + FAR, after, "gather_wait_rest")
        self.sems, self.b_rest, tok = _forward_start(b_rest, NEAR + FAR, "forward_start_rest")
        return tok

    def w_rest(self, after):
        b_kv, b_out = _forward_wait(self.b_rest, self.sems, NEAR + FAR, after, "forward_wait_rest")
        return b_kv.reshape(D, 2 * XW), b_out.reshape(D, D)

    def sibling_start(self, send, tag):
        return _sibling_start(send, "sibling_start_" + tag)

    def reduce_start_summed(self, keep, sent, tag, after=None):
        got = _sibling_wait(sent, [keep] + ([] if after is None else [after]), "sibling_wait_" + tag)
        return _scatter_start([_pair_sum_rows(keep, got, "pair_sum_" + tag)], "scatter_start_" + tag)

    def reduce_finish_start(self, state, tag, after):
        send_sems, recv_sems, qs, lands, _ = state
        qs, lands = _scatter_wait(qs, lands, send_sems, recv_sems, after, "scatter_wait_" + tag)
        return _join_start(_chip_sum(qs[0], lands[0], f"chip_sum_{tag}0"), "join_start_" + tag)

    def reduce_finish_wait(self, pending, tag, after):
        j = _join_wait(pending, after, "join_wait_" + tag)
        return j.reshape(2 * j.shape[1], j.shape[2])

    def reduce_finish(self, state, tag, after):
        send_sems, recv_sems, qs, lands, _ = state
        qs, lands = _scatter_wait(qs, lands, send_sems, recv_sems, after, "scatter_wait_" + tag)
        halves = [_chip_sum(q, l, f"chip_sum_{tag}{i}") for i, (q, l) in enumerate(zip(qs, lands))]
        return [j.reshape(2 * j.shape[1], j.shape[2]) for j in _join_halves(halves, "join_halves_" + tag)]


def kernel(x, mem, pre_norm, w_in, conv_w, mem_norm, w_mem_kv, w_out, post_norm, loss_target, m_pre_norm, m_w_in, m_conv_w, m_mem_norm, m_w_mem_kv, m_w_out, m_post_norm, v_pre_norm, v_w_in, v_conv_w, v_mem_norm, v_w_mem_kv, v_w_out, v_post_norm):
    chip = 2 * lax.axis_index("x") + lax.axis_index("y")

    cw_blk = jnp.zeros((8, 384), F32).at[:3].set(conv_w[0])
    cw_all = _gather_small(cw_blk, False, "gather_conv_w").reshape(NCHIP, 2, 8, 384)[:, 0, :3]
    conv_full = jnp.transpose(cw_all, (1, 0, 2)).reshape(3, CW)

    comm = _Comm(w_in[0], w_mem_kv[0], w_out[0], chip, lax.axis_index("c"))
    moments = {"w_out": (w_out[0], m_w_out[0], v_w_out[0]), "w_mem_kv": (w_mem_kv[0], m_w_mem_kv[0], v_w_mem_kv[0])}
    loss, grad_x, d_pre, d_mem, d_post, d_conv, r_in, _, (upd_out, upd_kv) = _local_step(
        x, mem, pre_norm, conv_full, mem_norm, post_norm, loss_target, chip, lax.axis_index("c"), comm, moments)

    pack = jnp.concatenate([d_pre, d_mem, d_post, jnp.pad(d_conv, ((0, 0), (0, D - CW))),
                            jnp.pad(loss, ((0, 0), (0, D - 128))), jnp.zeros((1, D), F32)], axis=0)
    tot = _gather_small(pack, True, "reduce_small")
    g_pre, g_mem, g_post = tot[0:1], tot[1:2], tot[2:3]
    g_conv = lax.dynamic_slice(tot[3:6, :CW], (0, chip * 384), (3, 384))
    loss_out = tot[6, 0]

    names = ("pre_norm", "w_in", "conv_w", "mem_norm", "w_mem_kv", "w_out", "post_norm")
    ws = (pre_norm, w_in[0], conv_w[0], mem_norm, w_mem_kv[0], w_out[0], post_norm)
    gs = [g_pre, None, g_conv, g_mem, None, None, g_post]
    ms = (m_pre_norm, m_w_in[0], m_conv_w[0], m_mem_norm, m_w_mem_kv[0], m_w_out[0], m_post_norm)
    vs = (v_pre_norm, v_w_in[0], v_conv_w[0], v_mem_norm, v_w_mem_kv[0], v_w_out[0], v_post_norm)
    upd = [None if g is None else _adamw(w, g, m, v, "adamw_" + nm) for nm, w, g, m, v in zip(names, ws, gs, ms, vs)]
    upd[4], upd[5] = upd_kv, upd_out
    g_in = comm.reduce_finish_wait(r_in, "b", after=[u[1] for u in upd if u is not None])
    upd[1] = _adamw(ws[1], g_in, ms[1], vs[1], "adamw_w_in")

    def shaped(arrs):
        return [a.reshape(w.shape) if w.ndim == a.ndim else a.reshape((1,) + a.shape)
                for a, w in zip(arrs, (pre_norm, w_in, conv_w, mem_norm, w_mem_kv, w_out, post_norm))]

    grads = shaped([u[0] for u in upd])
    deltas = shaped([u[1] for u in upd])
    new_m = shaped([u[2] for u in upd])
    new_v = shaped([u[3] for u in upd])
    return (loss_out, grad_x, *grads, *deltas, *new_m, *new_v)
```

```python
import jax
import jax.numpy as jnp
from jax import lax
from jax.experimental import pallas as pl
from jax.experimental.pallas import tpu as pltpu

F32 = jnp.float32
BF16 = jnp.bfloat16

D = 4096
S = 2048
NB = 2
T = NB * S
MLEN = 256
HD = 128
AW = 1536
CW = 1536
XW = 1024
XHD = 256
NXH = 4
NC = 14336
QA, KA, VA, ZA, UC, BC, CC, ZC, QX, ZX = 0, 1536, 3072, 4608, 6144, 7680, 9216, 10752, 12288, 13312
NCHIP = 4
WB = NC // NCHIP
DIL = (1, 4, 16)
HPG = 4
EPS = 1e-6
NEG = -1e30
ROPE_THETA = 10000.0
A_SCALE = HD ** -0.5
X_SCALE = XHD ** -0.5

ADAM_LR = 0.001
ADAM_B1 = 0.9
ADAM_B2 = 0.999
ADAM_EPS = 1e-08
ADAM_WD = 0.01
ADAM_STEP = 10

MESH = pl.DeviceIdType.MESH
MIB = 1024 * 1024


def _cp(sem, vmem_mib=48):
    return pltpu.CompilerParams(dimension_semantics=sem, vmem_limit_bytes=vmem_mib * MIB)


def _sigmoid(z):
    return 1.0 / (1.0 + jnp.exp(-z))


def _rope(x, cos, sin, half):
    return x * cos + pltpu.roll(x, half, 1) * sin


def _rope_t(g, cos, sin, half):
    return g * cos + pltpu.roll(g * sin, half, 1)


def _rms_fwd(x2, g, name, dep=None):
    rows = x2.shape[0]
    tr = 256
    deps = [] if dep is None else list(dep)

    def body(x_ref, g_ref, *rest):
        o_ref = rest[-1]
        x = x_ref[...]
        r = lax.rsqrt(jnp.mean(x * x, axis=-1, keepdims=True) + EPS)
        o_ref[...] = (x * r * g_ref[...]).astype(BF16)

    return pl.pallas_call(
        body, name=name, grid=(rows // tr,),
        in_specs=[pl.BlockSpec((tr, D), lambda i: (i, 0)), pl.BlockSpec((1, D), lambda i: (0, 0))]
        + [pl.BlockSpec(memory_space=pl.ANY)] * len(deps),
        out_specs=pl.BlockSpec((tr, D), lambda i: (i, 0)),
        out_shape=jax.ShapeDtypeStruct((rows, D), BF16),
        compiler_params=_cp(("parallel",)),
    )(x2, g, *deps)


def _norm_gain_grad(dn, x2, name):
    rows = x2.shape[0]
    tr = 256

    def body(dn_ref, x_ref, dg_ref):
        @pl.when(pl.program_id(0) == 0)
        def _():
            dg_ref[...] = jnp.zeros_like(dg_ref)
        x = x_ref[...]
        r = lax.rsqrt(jnp.mean(x * x, axis=-1, keepdims=True) + EPS)
        dg_ref[...] += jnp.sum(dn_ref[...] * (x * r), axis=0, keepdims=True)

    return pl.pallas_call(
        body, name=name, grid=(rows // tr,),
        in_specs=[pl.BlockSpec((tr, D), lambda i: (i, 0)), pl.BlockSpec((tr, D), lambda i: (i, 0))],
        out_specs=pl.BlockSpec((1, D), lambda i: (0, 0)),
        out_shape=jax.ShapeDtypeStruct((1, D), F32),
        compiler_params=_cp(("arbitrary",)),
    )(dn, x2)


def _pre_norm_bwd(dh, x2, g, dout, part, prev=None, dep=None):
    tr = 256
    nsteps = T // tr // 2
    extra = ([] if prev is None else list(prev)) + ([] if dep is None else [dep])

    def body(dh_ref, x_ref, g_ref, dout_ref, *rest):
        gx_ref, dg_ref = rest[len(extra):]

        @pl.when(pl.program_id(0) == 0)
        def _():
            dg_ref[...] = jnp.zeros_like(dg_ref) if prev is None else rest[1][...]

        x = x_ref[...]
        dh_ = dh_ref[...].astype(F32)
        r = lax.rsqrt(jnp.mean(x * x, axis=-1, keepdims=True) + EPS)
        xhat = x * r
        dg_ref[...] += jnp.sum(dh_ * xhat, axis=0, keepdims=True)
        dxn = dh_ * g_ref[...]
        gx_ref[...] = dout_ref[...].astype(F32) + r * (dxn - xhat * jnp.mean(dxn * xhat, axis=-1, keepdims=True))

    row = pl.BlockSpec((tr, D), lambda i: (i + part * nsteps, 0))
    vec = pl.BlockSpec((1, D), lambda i: (0, 0))
    return pl.pallas_call(
        body, name=f"pre_norm_bwd_{part}", grid=(nsteps,),
        in_specs=[row, row, vec, row] + ([] if prev is None else [pl.BlockSpec(memory_space=pl.ANY), vec])
        + [pl.BlockSpec(memory_space=pl.ANY)] * (0 if dep is None else 1),
        out_specs=[row, vec],
        out_shape=[jax.ShapeDtypeStruct((T, D), F32), jax.ShapeDtypeStruct((1, D), F32)],
        input_output_aliases={} if prev is None else {4: 0},
        compiler_params=_cp(("arbitrary",)),
    )(dh, x2, g, dout, *extra)


def _post_norm_loss(y, x2, tgt, g):
    tr = 256

    def body(y_ref, x_ref, t_ref, g_ref, dy_ref, dout_ref, dg_ref, loss_ref):
        @pl.when(pl.program_id(0) == 0)
        def _():
            dg_ref[...] = jnp.zeros_like(dg_ref)
            loss_ref[...] = jnp.zeros_like(loss_ref)
        yv = y_ref[...]
        gv = g_ref[...]
        r = lax.rsqrt(jnp.mean(yv * yv, axis=-1, keepdims=True) + EPS)
        yhat = yv * r
        err = x_ref[...] + yhat * gv - t_ref[...]
        loss_ref[...] += jnp.sum(jnp.sum(err * err, axis=1, keepdims=True), axis=0, keepdims=True) * (0.5 / D)
        dout = err * (1.0 / D)
        dout_ref[...] = dout.astype(BF16)
        dg_ref[...] += jnp.sum(dout * yhat, axis=0, keepdims=True)
        dyn = dout * gv
        dy_ref[...] = (r * (dyn - yhat * jnp.mean(dyn * yhat, axis=-1, keepdims=True))).astype(BF16)

    row = pl.BlockSpec((tr, D), lambda i: (i, 0))
    vec = pl.BlockSpec((1, D), lambda i: (0, 0))
    return pl.pallas_call(
        body, name="post_norm_loss", grid=(T // tr,),
        in_specs=[row, row, row, vec],
        out_specs=[row, row, vec, pl.BlockSpec((1, 128), lambda i: (0, 0))],
        out_shape=[jax.ShapeDtypeStruct((T, D), BF16), jax.ShapeDtypeStruct((T, D), BF16),
                   jax.ShapeDtypeStruct((1, D), F32), jax.ShapeDtypeStruct((1, 128), F32)],
        compiler_params=_cp(("arbitrary",)),
    )(y, x2, tgt, g)


NN = (((1,), (0,)), ((), ()))
NT = (((1,), (1,)), ((), ()))
TN = (((0,), (0,)), ((), ()))


def _as_index(v):
    return jnp.reshape(v, (1,)).astype(jnp.int32)


def _matmul(a, b, *, name, dims, grid, a_block, a_map, b_block, b_map, o_block, o_map, out_shape, out_dtype=F32,
            index=None, prev=None, deps=()):
    extra = ([] if prev is None else [prev]) + [d for d in deps if d is not None]
    first = 0 if index is None else 1
    nk = grid[2]
    in_place = out_dtype == F32

    def body(*refs):
        a_ref, b_ref, o_ref = refs[first], refs[first + 1], refs[first + 2 + len(extra)]
        acc_ref = o_ref if in_place else refs[-1]

        @pl.when(pl.program_id(2) == 0)
        def _():
            acc_ref[...] = lax.dot_general(a_ref[...], b_ref[...], dims, preferred_element_type=F32)

        @pl.when(pl.program_id(2) > 0)
        def _():
            acc_ref[...] += lax.dot_general(a_ref[...], b_ref[...], dims, preferred_element_type=F32)

        if not in_place:
            @pl.when(pl.program_id(2) == nk - 1)
            def _():
                o_ref[...] = acc_ref[...].astype(o_ref.dtype)

    in_specs = [pl.BlockSpec(a_block, a_map), pl.BlockSpec(b_block, b_map)] + [pl.BlockSpec(memory_space=pl.ANY)] * len(extra)
    out_specs = pl.BlockSpec(o_block, o_map)
    scratch = [] if in_place else [pltpu.VMEM(o_block, F32)]
    kwargs = dict(name=name, out_shape=jax.ShapeDtypeStruct(out_shape, out_dtype),
                  input_output_aliases={} if prev is None else {first + 2: 0},
                  compiler_params=_cp(("parallel", "parallel", "arbitrary"), vmem_mib=56))
    if index is None:
        return pl.pallas_call(body, grid=grid, in_specs=in_specs, out_specs=out_specs, scratch_shapes=scratch,
                              **kwargs)(a, b, *extra)
    grid_spec = pltpu.PrefetchScalarGridSpec(num_scalar_prefetch=1, grid=grid, in_specs=in_specs, out_specs=out_specs,
                                             scratch_shapes=scratch)
    return pl.pallas_call(body, grid_spec=grid_spec, **kwargs)(_as_index(index), a, b, *extra)


def _mm_nn(a, b, name, tm, tn, tk):
    m, kd = a.shape
    n = b.shape[1]
    return _matmul(a, b, name=name, dims=NN, grid=(m // tm, n // tn, kd // tk),
                   a_block=(tm, tk), a_map=lambda i, j, k: (i, k),
                   b_block=(tk, tn), b_map=lambda i, j, k: (k, j),
                   o_block=(tm, tn), o_map=lambda i, j, k: (i, j), out_shape=(m, n))


def _mm_nt(a, b, name, tm, tn, tk):
    m, kd = a.shape
    n = b.shape[0]
    return _matmul(a, b, name=name, dims=NT, grid=(m // tm, n // tn, kd // tk),
                   a_block=(tm, tk), a_map=lambda i, j, k: (i, k),
                   b_block=(tn, tk), b_map=lambda i, j, k: (j, k),
                   o_block=(tm, tn), o_map=lambda i, j, k: (i, j), out_shape=(m, n))


W_TN = 1792
W_NJ = WB // W_TN


def _proj_part(h, wg, chip, masks, name, prev=None, dep=None):
    tm, tk = 1024, 2048

    def blk(j, ix):
        m = masks[0]
        for t in range(1, len(masks)):
            m = jnp.where(j // W_NJ == t, masks[t], m)
        return jnp.bitwise_xor(ix[0], m)

    return _matmul(h, wg, name=name, dims=NN, grid=(T // tm, len(masks) * W_NJ, D // tk), index=chip, prev=prev, deps=(dep,),
                   a_block=(tm, tk), a_map=lambda i, j, k, ix: (i, k),
                   b_block=(tk, W_TN), b_map=lambda i, j, k, ix: (blk(j, ix) * (D // tk) + k, j % W_NJ),
                   o_block=(tm, W_TN), o_map=lambda i, j, k, ix: (i, blk(j, ix) * W_NJ + j % W_NJ), out_shape=(T, NC),
                   out_dtype=BF16)


def _adamw_math(w_ref, g_ref, m_ref, v_ref, go_ref, d_ref, nm_ref, nv_ref):
    gv = g_ref[...]
    go_ref[...] = gv
    nm = ADAM_B1 * m_ref[...] + (1.0 - ADAM_B1) * gv
    nv = ADAM_B2 * v_ref[...] + (1.0 - ADAM_B2) * (gv * gv)
    m_hat = nm / (1.0 - ADAM_B1 ** ADAM_STEP)
    v_hat = nv / (1.0 - ADAM_B2 ** ADAM_STEP)
    d_ref[...] = -ADAM_LR * (m_hat / (jnp.sqrt(v_hat) + ADAM_EPS) + ADAM_WD * w_ref[...])
    nm_ref[...] = nm
    nv_ref[...] = nv


def _dh(dproj, wg, dep=None, adam=()):
    tm, tn = 1024, 1024
    grid = (T // tm, D // tn, NC // W_TN)
    nsteps = grid[0] * grid[1] * grid[2]
    deps = [] if dep is None else [dep]
    na = len(adam)
    nk = grid[2]

    def body(*refs):
        a_ref, b_ref = refs[0], refs[1]
        o_ref = refs[2 + 4 * na + len(deps)]
        acc_ref = refs[-1]

        @pl.when(pl.program_id(2) == 0)
        def _():
            acc_ref[...] = lax.dot_general(a_ref[...], b_ref[...], NT, preferred_element_type=F32)

        @pl.when(pl.program_id(2) > 0)
        def _():
            acc_ref[...] += lax.dot_general(a_ref[...], b_ref[...], NT, preferred_element_type=F32)

        @pl.when(pl.program_id(2) == nk - 1)
        def _():
            o_ref[...] = acc_ref[...].astype(BF16)

        for s in range(na):
            ins = refs[2 + 4 * s:6 + 4 * s]
            outs = refs[3 + 4 * na + len(deps) + 4 * s:7 + 4 * na + len(deps) + 4 * s]
            _adamw_math(*ins, *outs)

    def rows_of(arr):
        r, c = arr.shape
        return pl.BlockSpec((r // nsteps, c), lambda i, j, k: ((i * grid[1] + j) * grid[2] + k, 0))

    adam_specs = [rows_of(a) for st in adam for a in st]
    res = pl.pallas_call(
        body, name="dh", grid=grid,
        in_specs=[pl.BlockSpec((tm, W_TN), lambda i, j, k: (i, k)),
                  pl.BlockSpec((tn, W_TN), lambda i, j, k: ((k // W_NJ) * (D // tn) + j, k % W_NJ))]
        + adam_specs + [pl.BlockSpec(memory_space=pl.ANY)] * len(deps),
        out_specs=[pl.BlockSpec((tm, tn), lambda i, j, k: (i, j))] + adam_specs,
        out_shape=[jax.ShapeDtypeStruct((T, D), BF16)] + [jax.ShapeDtypeStruct(a.shape, F32) for st in adam for a in st],
        scratch_shapes=[pltpu.VMEM((tm, tn), F32)],
        compiler_params=_cp(("arbitrary",) * 3, vmem_mib=56),
    )(dproj, wg, *[a for st in adam for a in st], *deps)
    return res[0], [tuple(res[1 + 4 * s:5 + 4 * s]) for s in range(na)]


def _grad_rows(a, b, half, out_dtype, name, dep=None):
    kd, n = b.shape
    tm, tn, tk = D // NCHIP // 2, min(n, 2048), min(kd, 2048)
    return _matmul(a, b, name=name, dims=TN, grid=(NCHIP, n // tn, kd // tk), index=half, deps=(dep,),
                   a_block=(tk, tm), a_map=lambda i, j, k, ix: (k, 2 * i + ix[0]),
                   b_block=(tk, tn), b_map=lambda i, j, k, ix: (k, j),
                   o_block=(tm, tn), o_map=lambda i, j, k, ix: (i, j),
                   out_shape=(NCHIP * tm, n), out_dtype=out_dtype)


def _grad_w_in(h, dproj, half, out_dtype, name, dep=None):
    tm, tk = 1024, 2048
    nh = D // 2 // tm
    return _matmul(h, dproj, name=name, dims=TN, grid=(nh, NC // W_TN, T // tk), index=half, deps=(dep,),
                   a_block=(tk, tm), a_map=lambda i, j, k, ix: (k, ix[0] * nh + i),
                   b_block=(tk, W_TN), b_map=lambda i, j, k, ix: (k, j),
                   o_block=(tm, W_TN), o_map=lambda i, j, k, ix: ((j // W_NJ) * nh + i, j % W_NJ),
                   out_shape=(NCHIP * D // 2, WB), out_dtype=out_dtype)


def _rope_tables(pos, half):
    inv = 1.0 / (ROPE_THETA ** (jnp.arange(half, dtype=F32) / half))
    ang = pos.astype(F32)[:, None] * inv[None, :]
    cos, sin = jnp.cos(ang), jnp.sin(ang)
    return jnp.concatenate([cos, cos], axis=1), jnp.concatenate([-sin, sin], axis=1)


def _band_mask(r0):
    qi = lax.broadcasted_iota(jnp.int32, (128, 256), 0)
    kk = lax.broadcasted_iota(jnp.int32, (128, 256), 1)
    return (kk >= qi) & (kk <= qi + 128) & (kk + r0 >= 128)


def _window(r0, nblk):
    if nblk == 1:
        qi = lax.broadcasted_iota(jnp.int32, (128, 128), 0)
        kk = lax.broadcasted_iota(jnp.int32, (128, 128), 1)
        return pl.ds(128, 128), kk <= qi
    return pl.ds(r0, 256), _band_mask(r0)


def _dil_rows(r, n, d):
    if d == 1:
        return pl.ds(pl.multiple_of(n * 128, 128), 128)
    return pl.ds(r + d * 128 * n, 128, stride=d)


def _widen(refs, wide):
    if not wide:
        return refs

    def copy(n, carry):
        rows = pl.ds(pl.multiple_of(n * 256, 256), 256)
        for src, dst in zip(refs, wide):
            dst[rows, :] = src[rows, :].astype(F32)
        return carry

    lax.fori_loop(0, S // 256, copy, 0)
    return wide


def _attn_fwd(proj, cosf, sinf, g):
    d = DIL[g]
    ln = S // d
    nblk = ln // 128
    proj_v = proj.reshape(NB, S, NC)

    def body(q_ref, k_ref, v_ref, cos_ref, sin_ref, o_ref, l_ref, k_s, v_s, *wide):
        k_s[:, pl.ds(0, 128), :] = jnp.zeros((d, 128, HD), BF16)
        v_s[:, pl.ds(0, 128), :] = jnp.zeros((d, 128, HD), BF16)
        q_ref, k_ref, v_ref = _widen((q_ref, k_ref, v_ref), wide)

        def prep(i, carry):
            r, n = i // nblk, i % nblk
            rows = _dil_rows(r, n, d)
            dst = pl.ds(pl.multiple_of(n * 128 + 128, 128), 128)
            k_s[r, dst, :] = _rope(k_ref[rows, :].astype(F32), cos_ref[rows, :], sin_ref[rows, :], HD // 2).astype(BF16)
            v_s[r, dst, :] = v_ref[rows, :].astype(BF16)
            return carry

        lax.fori_loop(0, d * nblk, prep, 0, unroll=4)

        def step(i, carry):
            r, n = i // nblk, i % nblk
            rows = _dil_rows(r, n, d)
            r0 = pl.multiple_of(n * 128, 128)
            qr = _rope(q_ref[rows, :].astype(F32), cos_ref[rows, :], sin_ref[rows, :], HD // 2).astype(BF16)
            win, mask = _window(r0, nblk)
            kw = k_s[r, win, :]
            vw = v_s[r, win, :]
            sc = lax.dot_general(qr, kw, NT, preferred_element_type=F32) * A_SCALE
            sc = jnp.where(mask, sc, NEG)
            m = jnp.max(sc, axis=1, keepdims=True)
            p = jnp.exp(sc - m)
            l = jnp.sum(p, axis=1, keepdims=True)
            o_ref[rows, :] = jnp.dot(p.astype(BF16), vw, preferred_element_type=F32) / l
            l_ref[rows, :] = jnp.broadcast_to(m + jnp.log(l), (128, HD))
            return carry

        lax.fori_loop(0, d * nblk, step, 0, unroll=4)

    def col(off):
        return lambda b, h: (b, 0, off // HD + HPG * g + h)

    blk = (None, S, HD)
    tab = pl.BlockSpec((S, HD), lambda b, h: (0, 0))
    out = pl.BlockSpec(blk, lambda b, h: (b, 0, h))
    kv = pl.BlockSpec((None, None, d, ln + 128, HD), lambda b, h: (b, h, 0, 0, 0))
    o, l, kr, vr = pl.pallas_call(
        body, name=f"attn_fwd_d{d}", grid=(NB, HPG),
        in_specs=[pl.BlockSpec(blk, col(QA)), pl.BlockSpec(blk, col(KA)), pl.BlockSpec(blk, col(VA)), tab, tab],
        out_specs=[out, out, kv, kv],
        out_shape=[jax.ShapeDtypeStruct((NB, S, HPG * HD), F32)] * 2
        + [jax.ShapeDtypeStruct((NB, HPG, d, ln + 128, HD), BF16)] * 2,
        scratch_shapes=[pltpu.VMEM((S, HD), F32)] * (3 if d > 1 else 0),
        compiler_params=_cp(("parallel", "parallel")),
    )(proj_v, proj_v, proj_v, cosf, sinf)
    return o.reshape(T, HPG * HD), l.reshape(T, HPG * HD), kr, vr


def _attn_bwd(proj, kr, vr, cosf, sinf, da, lse, delta, dproj, g):
    d = DIL[g]
    ln = S // d
    nblk = ln // 128
    proj_v = proj.reshape(NB, S, NC)
    dproj_v = dproj.reshape(NB, S, NC)
    da_v = da.reshape(NB, S, AW)
    lse_v = lse.reshape(NB, S, HPG * HD)
    delta_v = delta.reshape(NB, S, HPG * HD)

    def body(q_ref, k_s, v_s, cos_ref, sin_ref, da_ref, lse_ref, dl_ref, dp_in_ref, o_ref, stg, dk_s, dv_s, *wide):
        del dp_in_ref
        w = pl.program_id(2)

        def emit():
            def cast(n, carry):
                rows = pl.ds(pl.multiple_of(n * 256, 256), 256)
                o_ref[rows, :] = stg[rows, :].astype(BF16)
                return carry

            lax.fori_loop(0, S // 256, cast, 0)

        @pl.when(w == 0)
        def _():
            dk_s[...] = jnp.zeros_like(dk_s)
            dv_s[...] = jnp.zeros_like(dv_s)
            (q_src,) = _widen((q_ref,), wide)

            def step(i, carry):
                r, n = i // nblk, i % nblk
                rows = _dil_rows(r, n, d)
                r0 = pl.multiple_of(n * 128, 128)
                win, mask = _window(r0, nblk)
                cos, sin = cos_ref[rows, :], sin_ref[rows, :]
                qr = _rope(q_src[rows, :].astype(F32), cos, sin, HD // 2).astype(BF16)
                kw = k_s[r, win, :]
                vw = v_s[r, win, :]
                sc = lax.dot_general(qr, kw, NT, preferred_element_type=F32) * A_SCALE
                sc = jnp.where(mask, sc, NEG)
                p = jnp.exp(sc - lse_ref[rows, :][:, :1])
                da_b = da_ref[rows, :].astype(BF16)
                dp = lax.dot_general(da_b, vw, NT, preferred_element_type=F32)
                ds_b = (p * (dp - dl_ref[rows, :][:, :1]) * A_SCALE).astype(BF16)
                p_b = p.astype(BF16)
                dq = jnp.dot(ds_b, kw, preferred_element_type=F32)
                stg[rows, :] = _rope_t(dq, cos, sin, HD // 2)
                dk_s[r, win, :] += lax.dot_general(ds_b, qr, TN, preferred_element_type=F32)
                dv_s[r, win, :] += lax.dot_general(p_b, da_b, TN, preferred_element_type=F32)
                return carry

            lax.fori_loop(0, d * nblk, step, 0, unroll=4)
            emit()

        @pl.when(w == 1)
        def _():
            def put(i, carry):
                r, n = i // nblk, i % nblk
                rows = _dil_rows(r, n, d)
                src = pl.ds(pl.multiple_of(n * 128 + 128, 128), 128)
                stg[rows, :] = _rope_t(dk_s[r, src, :], cos_ref[rows, :], sin_ref[rows, :], HD // 2)
                return carry

            lax.fori_loop(0, d * nblk, put, 0, unroll=4)
            emit()

        @pl.when(w == 2)
        def _():
            def put(i, carry):
                r, n = i // nblk, i % nblk
                src = pl.ds(pl.multiple_of(n * 128 + 128, 128), 128)
                stg[_dil_rows(r, n, d), :] = dv_s[r, src, :]
                return carry

            lax.fori_loop(0, d * nblk, put, 0, unroll=4)
            emit()

    def col(off):
        return lambda b, h, w: (ahead(b, h, w)[0], 0, off // HD + HPG * g + ahead(b, h, w)[1])

    def ahead(b, h, w):
        flat = jnp.minimum(b * HPG + h + jnp.where(w > 0, 1, 0), NB * HPG - 1)
        return flat // HPG, flat % HPG

    blk = (None, S, HD)
    tab = pl.BlockSpec((S, HD), lambda b, h, w: (0, 0))
    per_head = pl.BlockSpec(blk, lambda b, h, w: (ahead(b, h, w)[0], 0, ahead(b, h, w)[1]))
    kv = pl.BlockSpec((None, None, d, ln + 128, HD), lambda b, h, w: (*ahead(b, h, w), 0, 0, 0))
    out = pl.pallas_call(
        body, name=f"attn_bwd_d{d}", grid=(NB, HPG, 3),
        in_specs=[pl.BlockSpec(blk, col(QA)), kv, kv, tab, tab,
                  pl.BlockSpec(blk, col(0)), per_head, per_head, pl.BlockSpec(memory_space=pl.ANY)],
        out_specs=pl.BlockSpec(blk, lambda b, h, w: (b, 0, (AW // HD) * w + HPG * g + h)),
        out_shape=jax.ShapeDtypeStruct(dproj_v.shape, BF16),
        input_output_aliases={8: 0},
        scratch_shapes=[pltpu.VMEM((S, HD), F32), pltpu.VMEM((d, ln + 128, HD), F32), pltpu.VMEM((d, ln + 128, HD), F32)]
        + [pltpu.VMEM((S, HD), F32)] * (1 if d > 1 else 0),
        compiler_params=_cp(("arbitrary",) * 3),
    )(proj_v, kr, vr, cosf, sinf, da_v, lse_v, delta_v, dproj_v)
    return out.reshape(T, NC)


def _attn_mix(proj, os_, ls_, dep=None):
    tr = 256
    gw = HPG * HD
    deps = [] if dep is None else [dep]

    def body(o0, o1, o2, l0, l1, l2, z_ref, *rest):
        cat_ref = rest[-1]
        m = jnp.maximum(jnp.maximum(l0[...], l1[...]), l2[...])
        e = [jnp.exp(l[...] - m) for l in (l0, l1, l2)]
        inv = 1.0 / (e[0] + e[1] + e[2])
        for gi, o in enumerate((o0, o1, o2)):
            z = z_ref[:, gi * gw:(gi + 1) * gw].astype(F32)
            cat_ref[:, gi * gw:(gi + 1) * gw] = (o[...] * (e[gi] * inv) * (z * _sigmoid(z))).astype(BF16)

    grp = pl.BlockSpec((tr, gw), lambda i: (i, 0))
    return pl.pallas_call(
        body, name="attn_mix", grid=(T // tr,),
        in_specs=[grp] * 6 + [pl.BlockSpec((tr, AW), lambda i: (i, ZA // AW))] + [ANY_SPEC] * len(deps),
        out_specs=pl.BlockSpec((tr, AW), lambda i: (i, 0)),
        out_shape=jax.ShapeDtypeStruct((T, D), BF16),
        compiler_params=_cp(("parallel",)),
    )(*os_, *ls_, proj, *deps)


def _attn_mix_bwd(dcat, proj, os_, ls_, dep=None):
    tr = 256
    gw = HPG * HD
    deps = [] if dep is None else [dep]

    def body(dy_ref, o0, o1, o2, l0, l1, l2, z_ref, *rest):
        da_ref, lse_ref, dl_ref, dz_ref = rest[len(deps):]
        m = jnp.maximum(jnp.maximum(l0[...], l1[...]), l2[...])
        e = [jnp.exp(l[...] - m) for l in (l0, l1, l2)]
        den = e[0] + e[1] + e[2]
        inv = 1.0 / den
        lse_ref[...] = m + jnp.log(den)
        acc = jnp.zeros((tr, gw), F32)
        for gi, o in enumerate((o0, o1, o2)):
            cols = slice(gi * gw, (gi + 1) * gw)
            z = z_ref[:, cols].astype(F32)
            dy = dy_ref[:, cols]
            sg = _sigmoid(z)
            a = o[...] * (e[gi] * inv)
            da = dy * (z * sg)
            da_ref[:, cols] = da
            dz_ref[:, cols] = (dy * a * (sg * (1.0 + z * (1.0 - sg)))).astype(BF16)
            acc = acc + da * a
        for hh in range(HPG):
            cols = slice(hh * HD, (hh + 1) * HD)
            dl_ref[:, cols] = jnp.broadcast_to(jnp.sum(acc[:, cols], axis=1, keepdims=True), (tr, HD))

    grp = pl.BlockSpec((tr, gw), lambda i: (i, 0))
    return pl.pallas_call(
        body, name="attn_mix_bwd", grid=(T // tr,),
        in_specs=[pl.BlockSpec((tr, AW), lambda i: (i, 0))] + [grp] * 6 + [pl.BlockSpec((tr, AW), lambda i: (i, ZA // AW))]
        + [pl.BlockSpec(memory_space=pl.ANY)] * len(deps),
        out_specs=[pl.BlockSpec((tr, AW), lambda i: (i, 0)), grp, grp, pl.BlockSpec((tr, AW), lambda i: (i, ZA // AW))],
        out_shape=[jax.ShapeDtypeStruct((T, AW), F32), jax.ShapeDtypeStruct((T, gw), F32),
                   jax.ShapeDtypeStruct((T, gw), F32), jax.ShapeDtypeStruct((T, NC), BF16)],
        compiler_params=_cp(("parallel",)),
    )(dcat, *os_, *ls_, proj, *deps)


CT = 256


def _shift_down(x, n):
    rows = lax.broadcasted_iota(jnp.int32, x.shape, 0)
    return jnp.where(rows >= n, pltpu.roll(x, n, 0), 0.0)


def _shift_up(x, n):
    rows = lax.broadcasted_iota(jnp.int32, x.shape, 0)
    return jnp.where(rows < x.shape[0] - n, pltpu.roll(x, x.shape[0] - n, 0), 0.0)


def _conv_fwd(proj, conv_w, cat):
    proj_v = proj.reshape(NB, S, NC)
    cat_v = cat.reshape(NB, S, D)

    def body(u_ref, b_ref, c_ref, z_ref, w_ref, cat_in, o_ref):
        del cat_in
        cu = c_ref[...].astype(F32) * u_ref[...].astype(F32)
        cv = _shift_down(cu, 2) * w_ref[0:1, :] + _shift_down(cu, 1) * w_ref[1:2, :] + cu * w_ref[2:3, :]
        z = z_ref[...].astype(F32)
        o_ref[...] = (b_ref[...].astype(F32) * cv * (z * _sigmoid(z))).astype(BF16)

    def seg(off):
        return pl.BlockSpec((None, S, CT), lambda b, j: (b, 0, off // CT + j))

    out = pl.pallas_call(
        body, name="conv_fwd", grid=(NB, CW // CT),
        in_specs=[seg(UC), seg(BC), seg(CC), seg(ZC), pl.BlockSpec((3, CT), lambda b, j: (0, j)),
                  pl.BlockSpec(memory_space=pl.ANY)],
        out_specs=pl.BlockSpec((None, S, CT), lambda b, j: (b, 0, AW // CT + j)),
        out_shape=jax.ShapeDtypeStruct((NB, S, D), BF16),
        input_output_aliases={5: 0},
        compiler_params=_cp(("parallel", "parallel")),
    )(proj_v, proj_v, proj_v, proj_v, conv_w, cat_v)
    return out.reshape(T, D)


def _conv_bwd(dcat, proj, conv_w, dproj, dep=None):
    deps = [] if dep is None else [dep]
    proj_v = proj.reshape(NB, S, NC)
    dproj_v = dproj.reshape(NB, S, NC)
    dcat_v = dcat.reshape(NB, S, D)

    def body(dy_ref, u_ref, b_ref, c_ref, z_ref, w_ref, *rest):
        o_ref, dw_ref, st = rest[1 + len(deps):]
        b = pl.program_id(1)
        w = pl.program_id(2)

        @pl.when((b == 0) & (w == 0))
        def _():
            dw_ref[...] = jnp.zeros_like(dw_ref)

        @pl.when(w == 0)
        def _():
            u, c, z, bb = (r[...].astype(F32) for r in (u_ref, c_ref, z_ref, b_ref))
            dy = dy_ref[...]
            cu = c * u
            s1 = _shift_down(cu, 1)
            s2 = _shift_down(cu, 2)
            cv = s2 * w_ref[0:1, :] + s1 * w_ref[1:2, :] + cu * w_ref[2:3, :]
            sg = _sigmoid(z)
            sz = z * sg
            dcv = dy * bb * sz
            st[0] = dy * cv * sz
            st[2] = dy * bb * cv * (sg * (1.0 + z * (1.0 - sg)))
            dw_ref[0:1, :] += jnp.sum(dcv * s2, axis=0, keepdims=True)
            dw_ref[1:2, :] += jnp.sum(dcv * s1, axis=0, keepdims=True)
            dw_ref[2:3, :] += jnp.sum(dcv * cu, axis=0, keepdims=True)
            dcu = dcv * w_ref[2:3, :] + _shift_up(dcv, 1) * w_ref[1:2, :] + _shift_up(dcv, 2) * w_ref[0:1, :]
            st[1] = dcu * u
            o_ref[...] = (dcu * c).astype(BF16)

        for k in range(3):
            @pl.when(w == k + 1)
            def _(k=k):
                o_ref[...] = st[k].astype(BF16)

    def ahead(j, b, w):
        flat = jnp.minimum(j * NB + b + jnp.where(w > 0, 1, 0), (CW // CT) * NB - 1)
        return flat // NB, flat % NB

    def seg(off):
        return pl.BlockSpec((None, S, CT), lambda j, b, w: (ahead(j, b, w)[1], 0, off // CT + ahead(j, b, w)[0]))

    out, dw = pl.pallas_call(
        body, name="conv_bwd", grid=(CW // CT, NB, 4),
        in_specs=[seg(AW), seg(UC), seg(BC), seg(CC), seg(ZC),
                  pl.BlockSpec((3, CT), lambda j, b, w: (0, ahead(j, b, w)[0])),
                  pl.BlockSpec(memory_space=pl.ANY)] + [pl.BlockSpec(memory_space=pl.ANY)] * len(deps),
        out_specs=[pl.BlockSpec((None, S, CT), lambda j, b, w: (b, 0, (UC + w * CW) // CT + j)),
                   pl.BlockSpec((3, CT), lambda j, b, w: (0, j))],
        out_shape=[jax.ShapeDtypeStruct((NB, S, NC), BF16), jax.ShapeDtypeStruct((3, CW), F32)],
        input_output_aliases={6: 0},
        scratch_shapes=[pltpu.VMEM((3, S, CT), F32)],
        compiler_params=_cp(("arbitrary",) * 3),
    )(dcat_v, proj_v, proj_v, proj_v, proj_v, conv_w, dproj_v, *deps)
    return out.reshape(T, NC), dw


XT = 1024


def _cross_fwd(proj, mkv, cosq, sinq, cosm, sinm, cat):
    proj_v = proj.reshape(NB, S, NC)
    mkv_v = mkv.reshape(NB, MLEN, 2 * XW)
    cat_v = cat.reshape(NB, S, D)

    def body(q_ref, z_ref, mk_ref, mv_ref, cq, sq, cm, sm, cat_in, o_ref):
        del cat_in
        mkr = _rope(mk_ref[...], cm[...], sm[...], XHD // 2).astype(BF16)
        qr = _rope(q_ref[...].astype(F32), cq[...], sq[...], XHD // 2).astype(BF16)
        sc = lax.dot_general(qr, mkr, NT, preferred_element_type=F32) * X_SCALE
        p = jnp.exp(sc - jnp.max(sc, axis=1, keepdims=True))
        p = p / jnp.sum(p, axis=1, keepdims=True)
        ox = jnp.dot(p.astype(BF16), mv_ref[...].astype(BF16), preferred_element_type=F32)
        z = z_ref[...].astype(F32)
        o_ref[...] = (ox * (z * _sigmoid(z))).astype(BF16)

    def seg(off):
        return pl.BlockSpec((None, XT, XHD), lambda b, h, t: (b, t, off // XHD + h))

    qtab = pl.BlockSpec((XT, XHD), lambda b, h, t: (t, 0))
    mtab = pl.BlockSpec((MLEN, XHD), lambda b, h, t: (0, 0))
    out = pl.pallas_call(
        body, name="cross_fwd", grid=(NB, NXH, S // XT),
        in_specs=[seg(QX), seg(ZX),
                  pl.BlockSpec((None, MLEN, XHD), lambda b, h, t: (b, 0, h)),
                  pl.BlockSpec((None, MLEN, XHD), lambda b, h, t: (b, 0, NXH + h)),
                  qtab, qtab, mtab, mtab, pl.BlockSpec(memory_space=pl.ANY)],
        out_specs=pl.BlockSpec((None, XT, XHD), lambda b, h, t: (b, t, (AW + CW) // XHD + h)),
        out_shape=jax.ShapeDtypeStruct((NB, S, D), BF16),
        input_output_aliases={8: 0},
        compiler_params=_cp(("parallel",) * 3),
    )(proj_v, proj_v, mkv_v, mkv_v, cosq, sinq, cosm, sinm, cat_v)
    return out.reshape(T, D)


def _cross_bwd(dcat, proj, mkv, cosq, sinq, cosm, sinm, dproj):
    proj_v = proj.reshape(NB, S, NC)
    dproj_v = dproj.reshape(NB, S, NC)
    dcat_v = dcat.reshape(NB, S, D)
    mkv_v = mkv.reshape(NB, MLEN, 2 * XW)
    nt = S // XT

    def body(dy_ref, q_ref, z_ref, mk_ref, mv_ref, cq, sq, cm, sm, dp_in, o_ref, dmk_ref, dmv_ref, dz_s):
        del dp_in
        t = pl.program_id(2)
        w = pl.program_id(3)

        @pl.when((t == 0) & (w == 0))
        def _():
            dmk_ref[...] = jnp.zeros_like(dmk_ref)
            dmv_ref[...] = jnp.zeros_like(dmv_ref)

        @pl.when(w == 0)
        def _():
            mkr = _rope(mk_ref[...], cm[...], sm[...], XHD // 2).astype(BF16)
            mv_b = mv_ref[...].astype(BF16)
            qr = _rope(q_ref[...].astype(F32), cq[...], sq[...], XHD // 2).astype(BF16)
            sc = lax.dot_general(qr, mkr, NT, preferred_element_type=F32) * X_SCALE
            p = jnp.exp(sc - jnp.max(sc, axis=1, keepdims=True))
            p = p / jnp.sum(p, axis=1, keepdims=True)
            p_b = p.astype(BF16)
            ox = jnp.dot(p_b, mv_b, preferred_element_type=F32)
            z = z_ref[...].astype(F32)
            dy = dy_ref[...]
            sg = _sigmoid(z)
            dz_s[...] = dy * ox * (sg * (1.0 + z * (1.0 - sg)))
            dox_b = (dy * (z * sg)).astype(BF16)
            dp = lax.dot_general(dox_b, mv_b, NT, preferred_element_type=F32)
            ds_b = (p * (dp - jnp.sum(dp * p, axis=1, keepdims=True)) * X_SCALE).astype(BF16)
            dq = jnp.dot(ds_b, mkr, preferred_element_type=F32)
            o_ref[...] = _rope_t(dq, cq[...], sq[...], XHD // 2).astype(BF16)
            dmk_ref[...] += lax.dot_general(ds_b, qr, TN, preferred_element_type=F32)
            dmv_ref[...] += lax.dot_general(p_b, dox_b, TN, preferred_element_type=F32)

        @pl.when(w == 1)
        def _():
            o_ref[...] = dz_s[...].astype(BF16)

        @pl.when((t == nt - 1) & (w == 1))
        def _():
            dmk_ref[...] = _rope_t(dmk_ref[...], cm[...], sm[...], XHD // 2)

    def seg(off):
        return pl.BlockSpec((None, XT, XHD), lambda b, h, t, w: (b, t, off // XHD + h))

    qtab = pl.BlockSpec((XT, XHD), lambda b, h, t, w: (t, 0))
    mtab = pl.BlockSpec((MLEN, XHD), lambda b, h, t, w: (0, 0))
    macc = pl.BlockSpec((None, MLEN, XHD), lambda b, h, t, w: (b, 0, h))
    out, dmk, dmv = pl.pallas_call(
        body, name="cross_bwd", grid=(NB, NXH, nt, 2),
        in_specs=[pl.BlockSpec((None, XT, XHD), lambda b, h, t, w: (b, t, (AW + CW) // XHD + h)),
                  seg(QX), seg(ZX),
                  pl.BlockSpec((None, MLEN, XHD), lambda b, h, t, w: (b, 0, h)),
                  pl.BlockSpec((None, MLEN, XHD), lambda b, h, t, w: (b, 0, NXH + h)),
                  qtab, qtab, mtab, mtab, pl.BlockSpec(memory_space=pl.ANY)],
        out_specs=[pl.BlockSpec((None, XT, XHD), lambda b, h, t, w: (b, t, (QX + w * XW) // XHD + h)), macc, macc],
        out_shape=[jax.ShapeDtypeStruct((NB, S, NC), BF16), jax.ShapeDtypeStruct((NB, MLEN, XW), F32),
                   jax.ShapeDtypeStruct((NB, MLEN, XW), F32)],
        input_output_aliases={9: 0},
        scratch_shapes=[pltpu.VMEM((XT, XHD), F32)],
        compiler_params=_cp(("arbitrary",) * 4),
    )(dcat_v, proj_v, proj_v, mkv_v, mkv_v, cosq, sinq, cosm, sinm, dproj_v)
    return out.reshape(T, NC), dmk, dmv


def _local_step(x, mem, pre_norm, conv_w, mem_norm, post_norm, tgt, chip, core, comm, moments=None):
    x2 = x.reshape(T, D)
    mem2 = mem.reshape(NB * MLEN, D)
    tgt2 = tgt.reshape(T, D)
    cosa, sina = _rope_tables(jnp.arange(S), HD // 2)
    cosq, sinq = _rope_tables(jnp.arange(S) + MLEN, XHD // 2)
    cosm, sinm = _rope_tables(jnp.arange(MLEN), XHD // 2)

    h = _rms_fwd(x2, pre_norm, "pre_norm_fwd", dep=comm.gather_started())
    memn = _rms_fwd(mem2, mem_norm, "mem_norm_fwd")
    proj = _proj_part(h, comm.w_in_own(), chip, (0,), "proj_own")
    wg_in, tok = comm.w_in_near(after=[proj, memn, conv_w])
    proj = _proj_part(h, wg_in, chip, (REL_XOR[0], REL_XOR[1]), "proj_near", prev=proj, dep=tok)
    wg_in, tok = comm.w_in_all(after=proj)
    proj = _proj_part(h, wg_in, chip, (REL_XOR[2],), "proj_far", prev=proj, dep=tok)
    fw = [_attn_fwd(proj, cosa, sina, g) for g in range(3)]
    os_ = [f[0] for f in fw]
    ls_ = [f[1] for f in fw]
    tok = comm.w_rest_landed(after=ls_)
    cat = _attn_mix(proj, os_, ls_, dep=tok)
    cat = _conv_fwd(proj, conv_w, cat)
    wg_kv, wg_out = comm.w_rest(after=[cat])
    mkv = _mm_nn(memn, wg_kv, "mkv", NB * MLEN, 1024, 512)
    cat = _cross_fwd(proj, mkv, cosq, sinq, cosm, sinm, cat)
    y = _mm_nn(cat, wg_out, "out_proj", 1024, 1024, 2048)
    dy, dout, d_post, loss = _post_norm_loss(y, x2, tgt2, post_norm)

    dcat = _mm_nt(dy, wg_out, "dcat", 1024, 1024, 2048)
    g_send = _grad_rows(cat, dy,1 - core, BF16, "grad_w_out_send")
    sent = comm.sibling_start(g_send.reshape(NCHIP, D // NCHIP // 2, D), "a")
    g_keep = _grad_rows(cat, dy,core, BF16, "grad_w_out_keep", dep=sent[-1])
    red_a = comm.reduce_start_summed(g_keep.reshape(NCHIP, D // NCHIP // 2, D), sent, "a")
    da, lse, delta, dproj = _attn_mix_bwd(dcat, proj, os_, ls_, dep=red_a[-1])
    for g in range(3):
        dproj = _attn_bwd(proj, fw[g][2], fw[g][3], cosa, sina, da, lse, delta, dproj, g)
    dproj, dmk, dmv = _cross_bwd(dcat, proj, mkv, cosq, sinq, cosm, sinm, dproj)
    dmkv = jnp.concatenate([dmk, dmv], axis=-1).reshape(NB * MLEN, 2 * XW)
    dmkv_b = dmkv.astype(BF16)
    gk_send = _grad_rows(memn, dmkv_b, 1 - core, BF16, "grad_w_mem_kv_send")
    sent = comm.sibling_start(gk_send.reshape(NCHIP, D // NCHIP // 2, 2 * XW), "k")
    gk_keep = _grad_rows(memn, dmkv_b, core, BF16, "grad_w_mem_kv_keep", dep=sent[-1])
    dmemn = _mm_nt(dmkv_b, wg_kv, "dmemn", NB * MLEN, 1024, 512)
    d_mem = _norm_gain_grad(dmemn, mem2, "mem_norm_bwd")
    red_k = comm.reduce_start_summed(gk_keep.reshape(NCHIP, D // NCHIP // 2, 2 * XW), sent, "k", after=d_mem)
    dproj, d_conv = _conv_bwd(dcat, proj, conv_w, dproj, dep=red_k[-1])
    (r_out,) = comm.reduce_finish(red_a, "a", after=d_conv)
    (r_kv,) = comm.reduce_finish(red_k, "k", after=d_conv)

    g_send = _grad_w_in(h, dproj, 1 - core, BF16, "grad_w_in_send", dep=r_kv)
    sent = comm.sibling_start(g_send.reshape(NCHIP, D // 2, WB), "b")
    g_keep = _grad_w_in(h, dproj, core, BF16, "grad_w_in_keep", dep=sent[-1])
    red_b = comm.reduce_start_summed(g_keep.reshape(NCHIP, D // 2, WB), sent, "b")
    sets = [] if moments is None else [(moments["w_out"][0], r_out, *moments["w_out"][1:]),
                                      (moments["w_mem_kv"][0], r_kv, *moments["w_mem_kv"][1:])]
    dh, updates = _dh(dproj, wg_in, dep=red_b[-1], adam=sets)
    grad_x, d_pre = _pre_norm_bwd(dh, x2, pre_norm, dout, 0)
    r_in = comm.reduce_finish_start(red_b, "b", after=grad_x)
    grad_x, d_pre = _pre_norm_bwd(dh, x2, pre_norm, dout, 1, prev=(grad_x, d_pre), dep=r_in[-1])
    return loss, grad_x.reshape(NB, S, D), d_pre, d_mem, d_post, d_conv, r_in, (r_kv, r_out), updates


def _adamw(w, g, m, v, name):
    rows, cols = w.shape
    tr = rows if rows <= 512 else 512
    tc = cols if cols <= 1024 else 1024
    if cols % tc:
        tc = 896

    blk = pl.BlockSpec((tr, tc), lambda i, j: (i, j))
    sds = jax.ShapeDtypeStruct((rows, cols), F32)
    def body(*refs):
        _adamw_math(*refs)

    return pl.pallas_call(
        body, name=name, grid=(rows // tr, cols // tc),
        in_specs=[blk] * 4, out_specs=[blk] * 4, out_shape=[sds] * 4,
        compiler_params=_cp(("parallel", "parallel")),
    )(w, g, m, v)


def _place():
    return lax.axis_index("x"), lax.axis_index("y"), lax.axis_index("c")


def _other_chips(x, y):
    return [(1 - x, y), (x, 1 - y), (1 - x, 1 - y)]


def _tile_cols(cols):
    return cols if cols <= 1024 else (1024 if cols % 1024 == 0 else 896)


def _cast_own(w, chip, name, half=None, prev=None):
    rows, cols = w.shape
    tr, tc = 512, _tile_cols(cols)
    nrt = rows // tr if half is None else rows // tr // 2
    index = jnp.stack([chip, 0 if half is None else half]).astype(jnp.int32)
    extra = [] if prev is None else [prev]

    def body(ix_ref, w_ref, *rest):
        rest[-1][...] = w_ref[...].astype(BF16)

    grid_spec = pltpu.PrefetchScalarGridSpec(
        num_scalar_prefetch=1, grid=(nrt, cols // tc),
        in_specs=[pl.BlockSpec((tr, tc), lambda i, j, ix: (ix[1] * nrt + i, j))] + [ANY_SPEC] * len(extra),
        out_specs=pl.BlockSpec((None, tr, tc), lambda i, j, ix: (ix[0], ix[1] * nrt + i, j)))
    return pl.pallas_call(
        body, name=name, grid_spec=grid_spec,
        out_shape=jax.ShapeDtypeStruct((NCHIP, rows, cols), BF16),
        input_output_aliases={} if prev is None else {2: 0},
        compiler_params=_cp(("parallel", "parallel")),
    )(index, w, *extra)


HBM_SPEC = pl.BlockSpec(memory_space=pltpu.HBM)
SEM_SPEC = pl.BlockSpec(memory_space=pltpu.SEMAPHORE)
ANY_SPEC = pl.BlockSpec(memory_space=pl.ANY)
EFFECT = pltpu.SideEffectType.DATAFLOW_SIDE_EFFECTING
TOKEN = jax.ShapeDtypeStruct((8, 128), F32)


def _half(ref, chip, hc):
    hr = ref.shape[1] // 2
    return ref.at[chip, pl.ds(hc * hr, hr), :]


NEAR = (0, 1)
FAR = (2,)
REL_XOR = (2, 1, 3)


def _gather_copies(refs, send_sems, recv_sems, rels):
    x, y, c = _place()
    chips = _other_chips(x, y)
    out, inc = [], []
    for a, ref in enumerate(refs):
        for p, j in enumerate(rels):
            px, py = chips[j]
            mine = _half(ref, 2 * x + y, c)
            theirs = _half(ref, 2 * px + py, c)
            sems = dict(send_sem=send_sems.at[len(rels) * a + p], recv_sem=recv_sems.at[len(rels) * a + p],
                        device_id=(px, py, c), device_id_type=MESH)
            out.append(pltpu.make_async_remote_copy(src_ref=mine, dst_ref=mine, **sems))
            inc.append(pltpu.make_async_remote_copy(src_ref=theirs, dst_ref=theirs, **sems))
    return out, inc


def _gather_start(bufs, groups, name):
    n = len(bufs)
    ng = len(groups)

    def body(*refs):
        ins = refs[:n]
        token = refs[-1]
        for gi, rels in enumerate(groups):
            out, _ = _gather_copies(ins, refs[n + 2 * gi], refs[n + 2 * gi + 1], rels)
            for cp in out:
                cp.start()
        token[...] = jnp.zeros_like(token)

    sems = []
    for rels in groups:
        sems += [pltpu.SemaphoreType.DMA((len(rels) * n,))] * 2
    res = pl.pallas_call(
        body, name=name,
        in_specs=[HBM_SPEC] * n,
        out_specs=[SEM_SPEC] * (2 * ng) + [HBM_SPEC] * n + [pl.BlockSpec(memory_space=pltpu.VMEM)],
        out_shape=sems + [pltpu.HBM(b.shape, b.dtype) for b in bufs] + [TOKEN],
        input_output_aliases={a: 2 * ng + a for a in range(n)},
        compiler_params=pltpu.CompilerParams(has_side_effects=EFFECT),
    )(*[pltpu.with_memory_space_constraint(b, pltpu.HBM) for b in bufs])
    return [(res[2 * gi], res[2 * gi + 1]) for gi in range(ng)], list(res[2 * ng:2 * ng + n]), res[-1]


def _gather_wait(bufs, sems, rels, after, name):
    n = len(bufs)
    send_sems, recv_sems = sems
    after = list(after) if isinstance(after, (list, tuple)) else [after]

    def body(*refs):
        ins = refs[:n]
        out, inc = _gather_copies(ins, refs[n], refs[n + 1], rels)
        for cp in out:
            cp.wait_send()
        for cp in inc:
            cp.wait_recv()

    return pl.pallas_call(
        body, name=name,
        in_specs=[HBM_SPEC] * n + [SEM_SPEC, SEM_SPEC] + [ANY_SPEC] * len(after),
        out_specs=[HBM_SPEC] * n,
        out_shape=[pltpu.HBM(b.shape, b.dtype) for b in bufs],
        input_output_aliases={a: a for a in range(n)},
        compiler_params=pltpu.CompilerParams(has_side_effects=EFFECT),
    )(*bufs, send_sems, recv_sems, *after)


def _forward_halves(bufs, rels, name):
    n = len(bufs)

    def body(*refs):
        cps, waits = _forward_copies(refs[n:2 * n], rels, refs[2 * n], refs[2 * n + 1])
        for cp in cps:
            cp.start()
        for cp in waits:
            cp.wait_recv()
        for cp in cps:
            cp.wait_send()

    return pl.pallas_call(
        body, name=name,
        in_specs=[ANY_SPEC] * n, out_specs=[ANY_SPEC] * n,
        out_shape=[jax.ShapeDtypeStruct(s.shape, s.dtype) for s in bufs],
        input_output_aliases={a: a for a in range(n)},
        scratch_shapes=[pltpu.SemaphoreType.DMA((len(rels) * n,)), pltpu.SemaphoreType.DMA((len(rels) * n,))],
    )(*bufs)


def _forward_copies(refs, rels, send_sems, recv_sems):
    x, y, c = _place()
    chips = _other_chips(x, y)
    cps, waits = [], []
    for a, ref in enumerate(refs):
        for p, j in enumerate(rels):
            px, py = chips[j]
            sems = dict(send_sem=send_sems.at[len(rels) * a + p], recv_sem=recv_sems.at[len(rels) * a + p],
                        device_id=(x, y, 1 - c), device_id_type=MESH)
            got = _half(ref, 2 * px + py, c)
            want = _half(ref, 2 * px + py, 1 - c)
            cps.append(pltpu.make_async_remote_copy(src_ref=got, dst_ref=got, **sems))
            waits.append(pltpu.make_async_remote_copy(src_ref=want, dst_ref=want, **sems))
    return cps, waits


def _forward_start(bufs, rels, name):
    n = len(bufs)

    def body(*refs):
        cps, _ = _forward_copies(refs[:n], rels, refs[n], refs[n + 1])
        for cp in cps:
            cp.start()
        refs[-1][...] = jnp.zeros_like(refs[-1])

    res = pl.pallas_call(
        body, name=name,
        in_specs=[HBM_SPEC] * n,
        out_specs=[SEM_SPEC, SEM_SPEC] + [HBM_SPEC] * n + [pl.BlockSpec(memory_space=pltpu.VMEM)],
        out_shape=[pltpu.SemaphoreType.DMA((len(rels) * n,))] * 2 + [pltpu.HBM(b.shape, b.dtype) for b in bufs] + [TOKEN],
        input_output_aliases={a: 2 + a for a in range(n)},
        compiler_params=pltpu.CompilerParams(has_side_effects=EFFECT),
    )(*[pltpu.with_memory_space_constraint(b, pltpu.HBM) for b in bufs])
    return (res[0], res[1]), list(res[2:2 + n]), res[-1]


def _forward_wait(bufs, sems, rels, after, name):
    n = len(bufs)

    def body(*refs):
        cps, waits = _forward_copies(refs[:n], rels, refs[n], refs[n + 1])
        for cp in cps:
            cp.wait_send()
        for cp in waits:
            cp.wait_recv()

    return pl.pallas_call(
        body, name=name,
        in_specs=[HBM_SPEC] * n + [SEM_SPEC, SEM_SPEC] + [ANY_SPEC] * len(after),
        out_specs=[HBM_SPEC] * n,
        out_shape=[pltpu.HBM(b.shape, b.dtype) for b in bufs],
        input_output_aliases={a: a for a in range(n)},
        compiler_params=pltpu.CompilerParams(has_side_effects=EFFECT),
    )(*bufs, sems[0], sems[1], *after)


def _sibling_copy(src, land, send_sems, recv_sems):
    x, y, c = _place()
    return pltpu.make_async_remote_copy(src_ref=src, dst_ref=land, send_sem=send_sems.at[0], recv_sem=recv_sems.at[0],
                                        device_id=(x, y, 1 - c), device_id_type=MESH)


def _sibling_start(part, name):
    def body(src, land, send_sems, recv_sems, src_thru, land_thru, token):
        _sibling_copy(src, land, send_sems, recv_sems).start()
        token[...] = jnp.zeros_like(token)

    land = lax.empty(part.shape, part.dtype)
    return pl.pallas_call(
        body, name=name,
        in_specs=[HBM_SPEC] * 2,
        out_specs=[SEM_SPEC, SEM_SPEC, HBM_SPEC, HBM_SPEC, pl.BlockSpec(memory_space=pltpu.VMEM)],
        out_shape=[pltpu.SemaphoreType.DMA((1,)), pltpu.SemaphoreType.DMA((1,)), pltpu.HBM(part.shape, part.dtype),
                   pltpu.HBM(part.shape, part.dtype), TOKEN],
        input_output_aliases={0: 2, 1: 3},
        compiler_params=pltpu.CompilerParams(has_side_effects=EFFECT),
    )(pltpu.with_memory_space_constraint(part, pltpu.HBM), pltpu.with_memory_space_constraint(land, pltpu.HBM))


def _sibling_wait(state, after, name):
    send_sems, recv_sems, part, land, _ = state

    def body(src, land_ref, send_ref, recv_ref, *rest):
        cp = _sibling_copy(src, land_ref, send_ref, recv_ref)
        cp.wait_send()
        cp.wait_recv()

    return pl.pallas_call(
        body, name=name,
        in_specs=[HBM_SPEC, HBM_SPEC, SEM_SPEC, SEM_SPEC] + [ANY_SPEC] * len(after),
        out_specs=[HBM_SPEC, HBM_SPEC],
        out_shape=[pltpu.HBM(part.shape, part.dtype), pltpu.HBM(land.shape, land.dtype)],
        input_output_aliases={0: 0, 1: 1},
        compiler_params=pltpu.CompilerParams(has_side_effects=EFFECT),
    )(part, land, send_sems, recv_sems, *after)[1]


def _pair_sum_rows(keep, got, name):
    nblk, rows, cols = keep.shape
    tr, tc = 512, _tile_cols(cols)

    def body(k_ref, g_ref, o_ref):
        o_ref[...] = (k_ref[...].astype(F32) + g_ref[...].astype(F32)).astype(BF16)

    blk = pl.BlockSpec((None, tr, tc), lambda b, i, j: (b, i, j))
    return pl.pallas_call(
        body, name=name, grid=(nblk, rows // tr, cols // tc),
        in_specs=[blk, blk], out_specs=blk,
        out_shape=jax.ShapeDtypeStruct(keep.shape, BF16),
        compiler_params=_cp(("parallel",) * 3),
    )(keep, got)


def _scatter_start(qs, name):
    n = len(qs)

    def body(*refs):
        ins, lands = refs[:n], refs[n:2 * n]
        token = refs[-1]
        for cp in _scatter_copies(ins, lands, refs[2 * n], refs[2 * n + 1]):
            cp.start()
        token[...] = jnp.zeros_like(token)

    lands = [lax.empty((3,) + q.shape[1:], q.dtype) for q in qs]
    res = pl.pallas_call(
        body, name=name,
        in_specs=[HBM_SPEC] * (2 * n),
        out_specs=[SEM_SPEC, SEM_SPEC] + [HBM_SPEC] * (2 * n) + [pl.BlockSpec(memory_space=pltpu.VMEM)],
        out_shape=[pltpu.SemaphoreType.DMA((3 * n,)), pltpu.SemaphoreType.DMA((3 * n,))]
        + [pltpu.HBM(b.shape, b.dtype) for b in qs + lands] + [TOKEN],
        input_output_aliases={a: 2 + a for a in range(2 * n)},
        compiler_params=pltpu.CompilerParams(has_side_effects=EFFECT),
    )(*[pltpu.with_memory_space_constraint(b, pltpu.HBM) for b in qs + lands])
    return res[0], res[1], list(res[2:2 + n]), list(res[2 + n:2 + 2 * n]), res[-1]


def _scatter_copies(ins, lands, send_sems, recv_sems):
    x, y, c = _place()
    cps = []
    for a in range(len(ins)):
        for j, (px, py) in enumerate(_other_chips(x, y)):
            cps.append(pltpu.make_async_remote_copy(
                src_ref=ins[a].at[2 * px + py], dst_ref=lands[a].at[j],
                send_sem=send_sems.at[3 * a + j], recv_sem=recv_sems.at[3 * a + j], device_id=(px, py, c), device_id_type=MESH))
    return cps


def _scatter_wait(qs, lands, send_sems, recv_sems, after, name):
    n = len(qs)

    def body(*refs):
        for cp in _scatter_copies(refs[:n], refs[n:2 * n], refs[2 * n], refs[2 * n + 1]):
            cp.wait_send()
            cp.wait_recv()

    res = pl.pallas_call(
        body, name=name,
        in_specs=[HBM_SPEC] * (2 * n) + [SEM_SPEC, SEM_SPEC, ANY_SPEC],
        out_specs=[HBM_SPEC] * (2 * n),
        out_shape=[pltpu.HBM(b.shape, b.dtype) for b in qs + lands],
        input_output_aliases={a: a for a in range(2 * n)},
        compiler_params=pltpu.CompilerParams(has_side_effects=EFFECT),
    )(*qs, *lands, send_sems, recv_sems, after)
    return list(res[:n]), list(res[n:])


def _chip_sum(q, got, name):
    _, hr, cols = got.shape
    tr, tc = 512, _tile_cols(cols)
    chip = 2 * lax.axis_index("x") + lax.axis_index("y")
    c = lax.axis_index("c")

    def body(idx_ref, q_ref, g_ref, o_ref):
        del idx_ref
        acc = q_ref[...].astype(F32)
        for i in range(3):
            acc = acc + g_ref[i].astype(F32)
        o_ref[...] = acc

    grid_spec = pltpu.PrefetchScalarGridSpec(
        num_scalar_prefetch=1, grid=(hr // tr, cols // tc),
        in_specs=[pl.BlockSpec((None, tr, tc), lambda i, j, ix: (ix[0], i, j)),
                  pl.BlockSpec((3, tr, tc), lambda i, j, ix: (0, i, j))],
        out_specs=pl.BlockSpec((None, tr, tc), lambda i, j, ix: (ix[1], i, j)))
    return pl.pallas_call(
        body, name=name, grid_spec=grid_spec,
        out_shape=jax.ShapeDtypeStruct((2, hr, cols), F32),
        compiler_params=_cp(("parallel", "parallel")),
    )(jnp.stack([chip, c]).astype(jnp.int32), q, got)


def _join_halves(bufs, name):
    n = len(bufs)

    def body(*refs):
        outs = refs[n:2 * n]
        send_sems, recv_sems = refs[2 * n:]
        x, y, c = _place()
        cps = []
        for a in range(n):
            cps.append(pltpu.make_async_remote_copy(src_ref=outs[a].at[c], dst_ref=outs[a].at[c], send_sem=send_sems.at[a],
                                                    recv_sem=recv_sems.at[a], device_id=(x, y, 1 - c), device_id_type=MESH))
        for cp in cps:
            cp.start()
        for a in range(n):
            theirs = outs[a].at[1 - c]
            pltpu.make_async_remote_copy(src_ref=theirs, dst_ref=theirs, send_sem=send_sems.at[a], recv_sem=recv_sems.at[a],
                                         device_id=(x, y, 1 - c), device_id_type=MESH).wait_recv()
        for cp in cps:
            cp.wait_send()

    hbm = pl.BlockSpec(memory_space=pl.ANY)
    return pl.pallas_call(
        body, name=name,
        in_specs=[hbm] * n, out_specs=[hbm] * n,
        out_shape=[jax.ShapeDtypeStruct(b.shape, b.dtype) for b in bufs],
        input_output_aliases={a: a for a in range(n)},
        scratch_shapes=[pltpu.SemaphoreType.DMA((n,)), pltpu.SemaphoreType.DMA((n,))],
    )(*bufs)


def _join_copy(buf, send_sems, recv_sems):
    x, y, c = _place()
    sems = dict(send_sem=send_sems.at[0], recv_sem=recv_sems.at[0], device_id=(x, y, 1 - c), device_id_type=MESH)
    return (pltpu.make_async_remote_copy(src_ref=buf.at[c], dst_ref=buf.at[c], **sems),
            pltpu.make_async_remote_copy(src_ref=buf.at[1 - c], dst_ref=buf.at[1 - c], **sems))


def _join_start(buf, name):
    def body(b_ref, send_sems, recv_sems, thru, token):
        _join_copy(b_ref, send_sems, recv_sems)[0].start()
        token[...] = jnp.zeros_like(token)

    return pl.pallas_call(
        body, name=name,
        in_specs=[HBM_SPEC],
        out_specs=[SEM_SPEC, SEM_SPEC, HBM_SPEC, pl.BlockSpec(memory_space=pltpu.VMEM)],
        out_shape=[pltpu.SemaphoreType.DMA((1,)), pltpu.SemaphoreType.DMA((1,)), pltpu.HBM(buf.shape, buf.dtype), TOKEN],
        input_output_aliases={0: 2},
        compiler_params=pltpu.CompilerParams(has_side_effects=EFFECT),
    )(pltpu.with_memory_space_constraint(buf, pltpu.HBM))


def _join_wait(state, after, name):
    send_sems, recv_sems, buf, _ = state

    def body(b_ref, send_ref, recv_ref, *rest):
        out, inc = _join_copy(b_ref, send_ref, recv_ref)
        out.wait_send()
        inc.wait_recv()

    return pl.pallas_call(
        body, name=name,
        in_specs=[HBM_SPEC, SEM_SPEC, SEM_SPEC] + [ANY_SPEC] * len(after),
        out_specs=HBM_SPEC,
        out_shape=pltpu.HBM(buf.shape, buf.dtype),
        input_output_aliases={0: 0},
        compiler_params=pltpu.CompilerParams(has_side_effects=EFFECT),
    )(buf, send_sems, recv_sems, *after)


N_DEV = 8


def _gather_small(block, reduce, name, deps=()):
    m_per, cols = block.shape
    deps = list(deps)

    def body(x_ref, *rest):
        out_ref, all_ref, send_sems, recv_sems, local_sem = rest[len(deps):]
        x, y, c = _place()
        me, sibling = (x, y, c), (x, y, 1 - c)
        chips = _other_chips(x, y)

        def rows(px, py, pc):
            return all_ref.at[pl.ds((4 * px + 2 * py + pc) * m_per, m_per), :]

        def copy(k, block_of, to, src=None):
            return pltpu.make_async_remote_copy(
                src_ref=rows(*block_of) if src is None else src, dst_ref=rows(*block_of),
                send_sem=send_sems.at[k], recv_sem=recv_sems.at[k], device_id=to, device_id_type=MESH)

        mine = pltpu.make_async_copy(x_ref, rows(*me), local_sem)
        mine.start()
        first = [copy(0, me, sibling, src=x_ref)]
        first += [copy(1 + j, me, (*chip, c), src=x_ref) for j, chip in enumerate(chips)]
        for cp in first:
            cp.start()
        passed = [copy(4 + j, (*chip, c), sibling) for j, chip in enumerate(chips)]
        for j, chip in enumerate(chips):
            copy(1 + j, (*chip, c), me).wait_recv()
            passed[j].start()
        copy(0, sibling, me).wait_recv()
        for j, chip in enumerate(chips):
            copy(4 + j, (*chip, 1 - c), me).wait_recv()
        for cp in first + passed:
            cp.wait_send()
        mine.wait()
        if reduce:
            acc = all_ref[pl.ds(0, m_per), :]
            for i in range(1, N_DEV):
                acc = acc + all_ref[pl.ds(i * m_per, m_per), :]
            out_ref[...] = acc
        else:
            out_ref[...] = all_ref[...]

    out_rows = m_per if reduce else N_DEV * m_per
    return pl.pallas_call(
        body, name=name,
        in_specs=[pl.BlockSpec(memory_space=pltpu.VMEM)] + [pl.BlockSpec(memory_space=pl.ANY)] * len(deps),
        out_specs=pl.BlockSpec(memory_space=pltpu.VMEM),
        out_shape=jax.ShapeDtypeStruct((out_rows, cols), F32),
        scratch_shapes=[pltpu.VMEM((N_DEV * m_per, cols), F32), pltpu.SemaphoreType.DMA((7,)),
                        pltpu.SemaphoreType.DMA((7,)), pltpu.SemaphoreType.DMA],
    )(block, *deps)


class _Comm:
    def __init__(self, w_in, w_kv, w_out, chip, core):
        self.w_in, self.chip, self.core = w_in, chip, core
        self.bufs = [_cast_own(w_in, chip, "cast_w_in_sent", half=core), _cast_own(w_kv, chip, "cast_w_mem_kv"),
                     _cast_own(w_out, chip, "cast_w_out")]


    def gather_started(self):
        (self.sems,), (b_in,), tok = _gather_start(self.bufs[:1], (NEAR,), "gather_start_in_near")
        self.b_in = _cast_own(self.w_in, self.chip, "cast_w_in_kept", half=1 - self.core, prev=b_in)
        return [tok]

    def w_in_own(self):
        return self.b_in.reshape(NCHIP * D, WB)

    def w_in_near(self, after):
        (b_in,) = _gather_wait([self.b_in], self.sems, NEAR, list(after) + self.bufs[1:], "gather_wait_in_near")
        (b_in,) = _forward_halves([b_in], NEAR, "forward_in_near")
        (self.sems,), (self.b_in,), tok = _gather_start([b_in], (FAR,), "gather_start_in_far")
        return self.b_in.reshape(NCHIP * D, WB), tok

    def w_in_all(self, after):
        (b_in,) = _gather_wait([self.b_in], self.sems, FAR, after, "gather_wait_in_far")
        (b_in,) = _forward_halves([b_in], FAR, "forward_in_far")
        (self.sems,), self.b_rest, tok = _gather_start(self.bufs[1:], (NEAR + FAR,), "gather_start_rest")
        return b_in.reshape(NCHIP * D, WB), tok

    def w_rest_landed(self, after):
        b_rest = _gather_wait(self.b_rest, self.sems, NEAR + FAR, after, "gather_wait_rest")
        self.sems, self.b_rest, tok = _forward_start(b_rest, NEAR + FAR, "forward_start_rest")
        return tok

    def w_rest(self, after):
        b_kv, b_out = _forward_wait(self.b_rest, self.sems, NEAR + FAR, after, "forward_wait_rest")
        return b_kv.reshape(D, 2 * XW), b_out.reshape(D, D)

    def sibling_start(self, send, tag):
        return _sibling_start(send, "sibling_start_" + tag)

    def reduce_start_summed(self, keep, sent, tag, after=None):
        got = _sibling_wait(sent, [keep] + ([] if after is None else [after]), "sibling_wait_" + tag)
        return _scatter_start([_pair_sum_rows(keep, got, "pair_sum_" + tag)], "scatter_start_" + tag)

    def reduce_finish_start(self, state, tag, after):
        send_sems, recv_sems, qs, lands, _ = state
        qs, lands = _scatter_wait(qs, lands, send_sems, recv_sems, after, "scatter_wait_" + tag)
        return _join_start(_chip_sum(qs[0], lands[0], f"chip_sum_{tag}0"), "join_start_" + tag)

    def reduce_finish_wait(self, pending, tag, after):
        j = _join_wait(pending, after, "join_wait_" + tag)
        return j.reshape(2 * j.shape[1], j.shape[2])

    def reduce_finish(self, state, tag, after):
        send_sems, recv_sems, qs, lands, _ = state
        qs, lands = _scatter_wait(qs, lands, send_sems, recv_sems, after, "scatter_wait_" + tag)
        halves = [_chip_sum(q, l, f"chip_sum_{tag}{i}") for i, (q, l) in enumerate(zip(qs, lands))]
        return [j.reshape(2 * j.shape[1], j.shape[2]) for j in _join_halves(halves, "join_halves_" + tag)]


def kernel(x, mem, pre_norm, w_in, conv_w, mem_norm, w_mem_kv, w_out, post_norm, loss_target, m_pre_norm, m_w_in, m_conv_w, m_mem_norm, m_w_mem_kv, m_w_out, m_post_norm, v_pre_norm, v_w_in, v_conv_w, v_mem_norm, v_w_mem_kv, v_w_out, v_post_norm):
    chip = 2 * lax.axis_index("x") + lax.axis_index("y")

    comm = _Comm(w_in[0], w_mem_kv[0], w_out[0], chip, lax.axis_index("c"))
    cw_blk = jnp.zeros((8, 384), F32).at[:3].set(conv_w[0])
    cw_all = _gather_small(cw_blk, False, "gather_conv_w", deps=comm.bufs[1:]).reshape(NCHIP, 2, 8, 384)[:, 0, :3]
    conv_full = jnp.transpose(cw_all, (1, 0, 2)).reshape(3, CW)
    moments = {"w_out": (w_out[0], m_w_out[0], v_w_out[0]), "w_mem_kv": (w_mem_kv[0], m_w_mem_kv[0], v_w_mem_kv[0])}
    loss, grad_x, d_pre, d_mem, d_post, d_conv, r_in, _, (upd_out, upd_kv) = _local_step(
        x, mem, pre_norm, conv_full, mem_norm, post_norm, loss_target, chip, lax.axis_index("c"), comm, moments)

    pack = jnp.concatenate([d_pre, d_mem, d_post, jnp.pad(d_conv, ((0, 0), (0, D - CW))),
                            jnp.pad(loss, ((0, 0), (0, D - 128))), jnp.zeros((1, D), F32)], axis=0)
    tot = _gather_small(pack, True, "reduce_small")
    g_pre, g_mem, g_post = tot[0:1], tot[1:2], tot[2:3]
    g_conv = lax.dynamic_slice(tot[3:6, :CW], (0, chip * 384), (3, 384))
    loss_out = tot[6, 0]

    names = ("pre_norm", "w_in", "conv_w", "mem_norm", "w_mem_kv", "w_out", "post_norm")
    ws = (pre_norm, w_in[0], conv_w[0], mem_norm, w_mem_kv[0], w_out[0], post_norm)
    gs = [g_pre, None, g_conv, g_mem, None, None, g_post]
    ms = (m_pre_norm, m_w_in[0], m_conv_w[0], m_mem_norm, m_w_mem_kv[0], m_w_out[0], m_post_norm)
    vs = (v_pre_norm, v_w_in[0], v_conv_w[0], v_mem_norm, v_w_mem_kv[0], v_w_out[0], v_post_norm)
    upd = [None if g is None else _adamw(w, g, m, v, "adamw_" + nm) for nm, w, g, m, v in zip(names, ws, gs, ms, vs)]
    upd[4], upd[5] = upd_kv, upd_out
    g_in = comm.reduce_finish_wait(r_in, "b", after=[u[1] for u in upd if u is not None])
    upd[1] = _adamw(ws[1], g_in, ms[1], vs[1], "adamw_w_in")

    def shaped(arrs):
        return [a.reshape(w.shape) if w.ndim == a.ndim else a.reshape((1,) + a.shape)
                for a, w in zip(arrs, (pre_norm, w_in, conv_w, mem_norm, w_mem_kv, w_out, post_norm))]

    grads = shaped([u[0] for u in upd])
    deltas = shaped([u[1] for u in upd])
    new_m = shaped([u[2] for u in upd])
    new_v = shaped([u[3] for u in upd])
    return (loss_out, grad_x, *grads, *deltas, *new_m, *new_v)
```

```python
import jax
import jax.numpy as jnp
from jax import lax
from jax.experimental import pallas as pl
from jax.experimental.pallas import tpu as pltpu

F32 = jnp.float32
BF16 = jnp.bfloat16

D = 4096
S = 2048
NB = 2
T = NB * S
MLEN = 256
HD = 128
AW = 1536
CW = 1536
XW = 1024
XHD = 256
NXH = 4
NC = 14336
QA, KA, VA, ZA, UC, BC, CC, ZC, QX, ZX = 0, 1536, 3072, 4608, 6144, 7680, 9216, 10752, 12288, 13312
NCHIP = 4
WB = NC // NCHIP
DIL = (1, 4, 16)
HPG = 4
EPS = 1e-6
NEG = -1e30
ROPE_THETA = 10000.0
A_SCALE = HD ** -0.5
X_SCALE = XHD ** -0.5

ADAM_LR = 0.001
ADAM_B1 = 0.9
ADAM_B2 = 0.999
ADAM_EPS = 1e-08
ADAM_WD = 0.01
ADAM_STEP = 10

MESH = pl.DeviceIdType.MESH
MIB = 1024 * 1024


def _cp(sem, vmem_mib=48):
    return pltpu.CompilerParams(dimension_semantics=sem, vmem_limit_bytes=vmem_mib * MIB)


def _sigmoid(z):
    return 1.0 / (1.0 + jnp.exp(-z))


def _rope(x, cos, sin, half):
    return x * cos + pltpu.roll(x, half, 1) * sin


def _rope_t(g, cos, sin, half):
    return g * cos + pltpu.roll(g * sin, half, 1)


def _rms_fwd(x2, g, name, dep=None):
    rows = x2.shape[0]
    tr = 256
    deps = [] if dep is None else list(dep)

    def body(x_ref, g_ref, *rest):
        o_ref = rest[-1]
        x = x_ref[...]
        r = lax.rsqrt(jnp.mean(x * x, axis=-1, keepdims=True) + EPS)
        o_ref[...] = (x * r * g_ref[...]).astype(BF16)

    return pl.pallas_call(
        body, name=name, grid=(rows // tr,),
        in_specs=[pl.BlockSpec((tr, D), lambda i: (i, 0)), pl.BlockSpec((1, D), lambda i: (0, 0))]
        + [pl.BlockSpec(memory_space=pl.ANY)] * len(deps),
        out_specs=pl.BlockSpec((tr, D), lambda i: (i, 0)),
        out_shape=jax.ShapeDtypeStruct((rows, D), BF16),
        compiler_params=_cp(("parallel",)),
    )(x2, g, *deps)


def _norm_gain_grad(dn, x2, name):
    rows = x2.shape[0]
    tr = 256

    def body(dn_ref, x_ref, dg_ref):
        @pl.when(pl.program_id(0) == 0)
        def _():
            dg_ref[...] = jnp.zeros_like(dg_ref)
        x = x_ref[...]
        r = lax.rsqrt(jnp.mean(x * x, axis=-1, keepdims=True) + EPS)
        dg_ref[...] += jnp.sum(dn_ref[...] * (x * r), axis=0, keepdims=True)

    return pl.pallas_call(
        body, name=name, grid=(rows // tr,),
        in_specs=[pl.BlockSpec((tr, D), lambda i: (i, 0)), pl.BlockSpec((tr, D), lambda i: (i, 0))],
        out_specs=pl.BlockSpec((1, D), lambda i: (0, 0)),
        out_shape=jax.ShapeDtypeStruct((1, D), F32),
        compiler_params=_cp(("arbitrary",)),
    )(dn, x2)


def _pre_norm_bwd(dh, x2, g, dout, part, prev=None, dep=None):
    tr = 256
    nsteps = T // tr // 2
    extra = ([] if prev is None else list(prev)) + ([] if dep is None else [dep])

    def body(dh_ref, x_ref, g_ref, dout_ref, *rest):
        gx_ref, dg_ref = rest[len(extra):]

        @pl.when(pl.program_id(0) == 0)
        def _():
            dg_ref[...] = jnp.zeros_like(dg_ref) if prev is None else rest[1][...]

        x = x_ref[...]
        dh_ = dh_ref[...].astype(F32)
        r = lax.rsqrt(jnp.mean(x * x, axis=-1, keepdims=True) + EPS)
        xhat = x * r
        dg_ref[...] += jnp.sum(dh_ * xhat, axis=0, keepdims=True)
        dxn = dh_ * g_ref[...]
        gx_ref[...] = dout_ref[...].astype(F32) + r * (dxn - xhat * jnp.mean(dxn * xhat, axis=-1, keepdims=True))

    row = pl.BlockSpec((tr, D), lambda i: (i + part * nsteps, 0))
    vec = pl.BlockSpec((1, D), lambda i: (0, 0))
    return pl.pallas_call(
        body, name=f"pre_norm_bwd_{part}", grid=(nsteps,),
        in_specs=[row, row, vec, row] + ([] if prev is None else [pl.BlockSpec(memory_space=pl.ANY), vec])
        + [pl.BlockSpec(memory_space=pl.ANY)] * (0 if dep is None else 1),
        out_specs=[row, vec],
        out_shape=[jax.ShapeDtypeStruct((T, D), F32), jax.ShapeDtypeStruct((1, D), F32)],
        input_output_aliases={} if prev is None else {4: 0},
        compiler_params=_cp(("arbitrary",)),
    )(dh, x2, g, dout, *extra)


def _post_norm_loss(y, x2, tgt, g):
    tr = 256

    def body(y_ref, x_ref, t_ref, g_ref, dy_ref, dout_ref, dg_ref, loss_ref):
        @pl.when(pl.program_id(0) == 0)
        def _():
            dg_ref[...] = jnp.zeros_like(dg_ref)
            loss_ref[...] = jnp.zeros_like(loss_ref)
        yv = y_ref[...]
        gv = g_ref[...]
        r = lax.rsqrt(jnp.mean(yv * yv, axis=-1, keepdims=True) + EPS)
        yhat = yv * r
        err = x_ref[...] + yhat * gv - t_ref[...]
        loss_ref[...] += jnp.sum(jnp.sum(err * err, axis=1, keepdims=True), axis=0, keepdims=True) * (0.5 / D)
        dout = err * (1.0 / D)
        dout_ref[...] = dout.astype(BF16)
        dg_ref[...] += jnp.sum(dout * yhat, axis=0, keepdims=True)
        dyn = dout * gv
        dy_ref[...] = (r * (dyn - yhat * jnp.mean(dyn * yhat, axis=-1, keepdims=True))).astype(BF16)

    row = pl.BlockSpec((tr, D), lambda i: (i, 0))
    vec = pl.BlockSpec((1, D), lambda i: (0, 0))
    return pl.pallas_call(
        body, name="post_norm_loss", grid=(T // tr,),
        in_specs=[row, row, row, vec],
        out_specs=[row, row, vec, pl.BlockSpec((1, 128), lambda i: (0, 0))],
        out_shape=[jax.ShapeDtypeStruct((T, D), BF16), jax.ShapeDtypeStruct((T, D), BF16),
                   jax.ShapeDtypeStruct((1, D), F32), jax.ShapeDtypeStruct((1, 128), F32)],
        compiler_params=_cp(("arbitrary",)),
    )(y, x2, tgt, g)


NN = (((1,), (0,)), ((), ()))
NT = (((1,), (1,)), ((), ()))
TN = (((0,), (0,)), ((), ()))


def _as_index(v):
    return jnp.reshape(v, (1,)).astype(jnp.int32)


def _matmul(a, b, *, name, dims, grid, a_block, a_map, b_block, b_map, o_block, o_map, out_shape, out_dtype=F32,
            index=None, prev=None, deps=()):
    extra = ([] if prev is None else [prev]) + [d for d in deps if d is not None]
    first = 0 if index is None else 1
    nk = grid[2]
    in_place = out_dtype == F32

    def body(*refs):
        a_ref, b_ref, o_ref = refs[first], refs[first + 1], refs[first + 2 + len(extra)]
        acc_ref = o_ref if in_place else refs[-1]

        @pl.when(pl.program_id(2) == 0)
        def _():
            acc_ref[...] = lax.dot_general(a_ref[...], b_ref[...], dims, preferred_element_type=F32)

        @pl.when(pl.program_id(2) > 0)
        def _():
            acc_ref[...] += lax.dot_general(a_ref[...], b_ref[...], dims, preferred_element_type=F32)

        if not in_place:
            @pl.when(pl.program_id(2) == nk - 1)
            def _():
                o_ref[...] = acc_ref[...].astype(o_ref.dtype)

    in_specs = [pl.BlockSpec(a_block, a_map), pl.BlockSpec(b_block, b_map)] + [pl.BlockSpec(memory_space=pl.ANY)] * len(extra)
    out_specs = pl.BlockSpec(o_block, o_map)
    scratch = [] if in_place else [pltpu.VMEM(o_block, F32)]
    kwargs = dict(name=name, out_shape=jax.ShapeDtypeStruct(out_shape, out_dtype),
                  input_output_aliases={} if prev is None else {first + 2: 0},
                  compiler_params=_cp(("parallel", "parallel", "arbitrary"), vmem_mib=56))
    if index is None:
        return pl.pallas_call(body, grid=grid, in_specs=in_specs, out_specs=out_specs, scratch_shapes=scratch,
                              **kwargs)(a, b, *extra)
    grid_spec = pltpu.PrefetchScalarGridSpec(num_scalar_prefetch=1, grid=grid, in_specs=in_specs, out_specs=out_specs,
                                             scratch_shapes=scratch)
    return pl.pallas_call(body, grid_spec=grid_spec, **kwargs)(_as_index(index), a, b, *extra)


def _mm_nn(a, b, name, tm, tn, tk):
    m, kd = a.shape
    n = b.shape[1]
    return _matmul(a, b, name=name, dims=NN, grid=(m // tm, n // tn, kd // tk),
                   a_block=(tm, tk), a_map=lambda i, j, k: (i, k),
                   b_block=(tk, tn), b_map=lambda i, j, k: (k, j),
                   o_block=(tm, tn), o_map=lambda i, j, k: (i, j), out_shape=(m, n))


def _mm_nt(a, b, name, tm, tn, tk):
    m, kd = a.shape
    n = b.shape[0]
    return _matmul(a, b, name=name, dims=NT, grid=(m // tm, n // tn, kd // tk),
                   a_block=(tm, tk), a_map=lambda i, j, k: (i, k),
                   b_block=(tn, tk), b_map=lambda i, j, k: (j, k),
                   o_block=(tm, tn), o_map=lambda i, j, k: (i, j), out_shape=(m, n))


W_TN = 1792
W_NJ = WB // W_TN


def _proj_part(h, wg, chip, masks, name, prev=None, dep=None):
    tm, tk = 1024, 2048

    def blk(j, ix):
        m = masks[0]
        for t in range(1, len(masks)):
            m = jnp.where(j // W_NJ == t, masks[t], m)
        return jnp.bitwise_xor(ix[0], m)

    return _matmul(h, wg, name=name, dims=NN, grid=(T // tm, len(masks) * W_NJ, D // tk), index=chip, prev=prev, deps=(dep,),
                   a_block=(tm, tk), a_map=lambda i, j, k, ix: (i, k),
                   b_block=(tk, W_TN), b_map=lambda i, j, k, ix: (blk(j, ix) * (D // tk) + k, j % W_NJ),
                   o_block=(tm, W_TN), o_map=lambda i, j, k, ix: (i, blk(j, ix) * W_NJ + j % W_NJ), out_shape=(T, NC),
                   out_dtype=BF16)


def _adamw_math(w_ref, g_ref, m_ref, v_ref, go_ref, d_ref, nm_ref, nv_ref):
    gv = g_ref[...]
    go_ref[...] = gv
    nm = ADAM_B1 * m_ref[...] + (1.0 - ADAM_B1) * gv
    nv = ADAM_B2 * v_ref[...] + (1.0 - ADAM_B2) * (gv * gv)
    m_hat = nm / (1.0 - ADAM_B1 ** ADAM_STEP)
    v_hat = nv / (1.0 - ADAM_B2 ** ADAM_STEP)
    d_ref[...] = -ADAM_LR * (m_hat / (jnp.sqrt(v_hat) + ADAM_EPS) + ADAM_WD * w_ref[...])
    nm_ref[...] = nm
    nv_ref[...] = nv


def _dh(dproj, wg, dep=None, adam=()):
    tm, tn = 1024, 2048
    grid = (T // tm, D // tn, NC // W_TN)
    nsteps = grid[0] * grid[1] * grid[2]
    deps = [] if dep is None else [dep]
    na = len(adam)
    nk = grid[2]

    def body(*refs):
        a_ref, b_ref = refs[0], refs[1]
        o_ref = refs[2 + 4 * na + len(deps)]
        acc_ref = refs[-1]

        @pl.when(pl.program_id(2) == 0)
        def _():
            acc_ref[...] = lax.dot_general(a_ref[...], b_ref[...], NT, preferred_element_type=F32)

        @pl.when(pl.program_id(2) > 0)
        def _():
            acc_ref[...] += lax.dot_general(a_ref[...], b_ref[...], NT, preferred_element_type=F32)

        @pl.when(pl.program_id(2) == nk - 1)
        def _():
            o_ref[...] = acc_ref[...].astype(BF16)

        for s in range(na):
            ins = refs[2 + 4 * s:6 + 4 * s]
            outs = refs[3 + 4 * na + len(deps) + 4 * s:7 + 4 * na + len(deps) + 4 * s]
            _adamw_math(*ins, *outs)

    def rows_of(arr):
        r, c = arr.shape
        return pl.BlockSpec((r // nsteps, c), lambda i, j, k: ((i * grid[1] + j) * grid[2] + k, 0))

    adam_specs = [rows_of(a) for st in adam for a in st]
    res = pl.pallas_call(
        body, name="dh", grid=grid,
        in_specs=[pl.BlockSpec((tm, W_TN), lambda i, j, k: (i, k)),
                  pl.BlockSpec((tn, W_TN), lambda i, j, k: ((k // W_NJ) * (D // tn) + j, k % W_NJ))]
        + adam_specs + [pl.BlockSpec(memory_space=pl.ANY)] * len(deps),
        out_specs=[pl.BlockSpec((tm, tn), lambda i, j, k: (i, j))] + adam_specs,
        out_shape=[jax.ShapeDtypeStruct((T, D), BF16)] + [jax.ShapeDtypeStruct(a.shape, F32) for st in adam for a in st],
        scratch_shapes=[pltpu.VMEM((tm, tn), F32)],
        compiler_params=_cp(("arbitrary",) * 3, vmem_mib=56),
    )(dproj, wg, *[a for st in adam for a in st], *deps)
    return res[0], [tuple(res[1 + 4 * s:5 + 4 * s]) for s in range(na)]


def _grad_rows(a, b, half, out_dtype, name, dep=None):
    kd, n = b.shape
    tm, tn, tk = D // NCHIP // 2, min(n, 2048), min(kd, 2048)
    return _matmul(a, b, name=name, dims=TN, grid=(NCHIP, n // tn, kd // tk), index=half, deps=(dep,),
                   a_block=(tk, tm), a_map=lambda i, j, k, ix: (k, 2 * i + ix[0]),
                   b_block=(tk, tn), b_map=lambda i, j, k, ix: (k, j),
                   o_block=(tm, tn), o_map=lambda i, j, k, ix: (i, j),
                   out_shape=(NCHIP * tm, n), out_dtype=out_dtype)


def _grad_w_in(h, dproj, half, out_dtype, name, dep=None):
    tm, tk = 1024, 2048
    nh = D // 2 // tm
    return _matmul(h, dproj, name=name, dims=TN, grid=(nh, NC // W_TN, T // tk), index=half, deps=(dep,),
                   a_block=(tk, tm), a_map=lambda i, j, k, ix: (k, ix[0] * nh + i),
                   b_block=(tk, W_TN), b_map=lambda i, j, k, ix: (k, j),
                   o_block=(tm, W_TN), o_map=lambda i, j, k, ix: ((j // W_NJ) * nh + i, j % W_NJ),
                   out_shape=(NCHIP * D // 2, WB), out_dtype=out_dtype)


def _rope_tables(pos, half):
    inv = 1.0 / (ROPE_THETA ** (jnp.arange(half, dtype=F32) / half))
    ang = pos.astype(F32)[:, None] * inv[None, :]
    cos, sin = jnp.cos(ang), jnp.sin(ang)
    return jnp.concatenate([cos, cos], axis=1), jnp.concatenate([-sin, sin], axis=1)


def _band_mask(r0):
    qi = lax.broadcasted_iota(jnp.int32, (128, 256), 0)
    kk = lax.broadcasted_iota(jnp.int32, (128, 256), 1)
    return (kk >= qi) & (kk <= qi + 128) & (kk + r0 >= 128)


def _window(r0, nblk):
    if nblk == 1:
        qi = lax.broadcasted_iota(jnp.int32, (128, 128), 0)
        kk = lax.broadcasted_iota(jnp.int32, (128, 128), 1)
        return pl.ds(128, 128), kk <= qi
    return pl.ds(r0, 256), _band_mask(r0)


def _dil_rows(r, n, d):
    if d == 1:
        return pl.ds(pl.multiple_of(n * 128, 128), 128)
    return pl.ds(r + d * 128 * n, 128, stride=d)


def _widen(refs, wide):
    if not wide:
        return refs

    def copy(n, carry):
        rows = pl.ds(pl.multiple_of(n * 256, 256), 256)
        for src, dst in zip(refs, wide):
            dst[rows, :] = src[rows, :].astype(F32)
        return carry

    lax.fori_loop(0, S // 256, copy, 0)
    return wide


def _attn_fwd(proj, cosf, sinf, g):
    d = DIL[g]
    ln = S // d
    nblk = ln // 128
    proj_v = proj.reshape(NB, S, NC)

    def body(q_ref, k_ref, v_ref, cos_ref, sin_ref, o_ref, l_ref, k_s, v_s, *wide):
        k_s[:, pl.ds(0, 128), :] = jnp.zeros((d, 128, HD), BF16)
        v_s[:, pl.ds(0, 128), :] = jnp.zeros((d, 128, HD), BF16)
        q_ref, k_ref, v_ref = _widen((q_ref, k_ref, v_ref), wide)

        def prep(i, carry):
            r, n = i // nblk, i % nblk
            rows = _dil_rows(r, n, d)
            dst = pl.ds(pl.multiple_of(n * 128 + 128, 128), 128)
            k_s[r, dst, :] = _rope(k_ref[rows, :].astype(F32), cos_ref[rows, :], sin_ref[rows, :], HD // 2).astype(BF16)
            v_s[r, dst, :] = v_ref[rows, :].astype(BF16)
            return carry

        lax.fori_loop(0, d * nblk, prep, 0, unroll=4)

        def step(i, carry):
            r, n = i // nblk, i % nblk
            rows = _dil_rows(r, n, d)
            r0 = pl.multiple_of(n * 128, 128)
            qr = _rope(q_ref[rows, :].astype(F32), cos_ref[rows, :], sin_ref[rows, :], HD // 2).astype(BF16)
            win, mask = _window(r0, nblk)
            kw = k_s[r, win, :]
            vw = v_s[r, win, :]
            sc = lax.dot_general(qr, kw, NT, preferred_element_type=F32) * A_SCALE
            sc = jnp.where(mask, sc, NEG)
            m = jnp.max(sc, axis=1, keepdims=True)
            p = jnp.exp(sc - m)
            l = jnp.sum(p, axis=1, keepdims=True)
            o_ref[rows, :] = jnp.dot(p.astype(BF16), vw, preferred_element_type=F32) / l
            l_ref[rows, :] = jnp.broadcast_to(m + jnp.log(l), (128, HD))
            return carry

        lax.fori_loop(0, d * nblk, step, 0, unroll=4)

    def col(off):
        return lambda b, h: (b, 0, off // HD + HPG * g + h)

    blk = (None, S, HD)
    tab = pl.BlockSpec((S, HD), lambda b, h: (0, 0))
    out = pl.BlockSpec(blk, lambda b, h: (b, 0, h))
    kv = pl.BlockSpec((None, None, d, ln + 128, HD), lambda b, h: (b, h, 0, 0, 0))
    o, l, kr, vr = pl.pallas_call(
        body, name=f"attn_fwd_d{d}", grid=(NB, HPG),
        in_specs=[pl.BlockSpec(blk, col(QA)), pl.BlockSpec(blk, col(KA)), pl.BlockSpec(blk, col(VA)), tab, tab],
        out_specs=[out, out, kv, kv],
        out_shape=[jax.ShapeDtypeStruct((NB, S, HPG * HD), F32)] * 2
        + [jax.ShapeDtypeStruct((NB, HPG, d, ln + 128, HD), BF16)] * 2,
        scratch_shapes=[pltpu.VMEM((S, HD), F32)] * (3 if d > 1 else 0),
        compiler_params=_cp(("parallel", "parallel")),
    )(proj_v, proj_v, proj_v, cosf, sinf)
    return o.reshape(T, HPG * HD), l.reshape(T, HPG * HD), kr, vr


def _attn_bwd(proj, kr, vr, cosf, sinf, da, lse, delta, dproj, g):
    d = DIL[g]
    ln = S // d
    nblk = ln // 128
    proj_v = proj.reshape(NB, S, NC)
    dproj_v = dproj.reshape(NB, S, NC)
    da_v = da.reshape(NB, S, AW)
    lse_v = lse.reshape(NB, S, HPG * HD)
    delta_v = delta.reshape(NB, S, HPG * HD)

    def body(q_ref, k_s, v_s, cos_ref, sin_ref, da_ref, lse_ref, dl_ref, dp_in_ref, o_ref, stg, dk_s, dv_s, *wide):
        del dp_in_ref
        w = pl.program_id(2)

        def emit():
            def cast(n, carry):
                rows = pl.ds(pl.multiple_of(n * 256, 256), 256)
                o_ref[rows, :] = stg[rows, :].astype(BF16)
                return carry

            lax.fori_loop(0, S // 256, cast, 0)

        @pl.when(w == 0)
        def _():
            dk_s[...] = jnp.zeros_like(dk_s)
            dv_s[...] = jnp.zeros_like(dv_s)
            (q_src,) = _widen((q_ref,), wide)

            def step(i, carry):
                r, n = i // nblk, i % nblk
                rows = _dil_rows(r, n, d)
                r0 = pl.multiple_of(n * 128, 128)
                win, mask = _window(r0, nblk)
                cos, sin = cos_ref[rows, :], sin_ref[rows, :]
                qr = _rope(q_src[rows, :].astype(F32), cos, sin, HD // 2).astype(BF16)
                kw = k_s[r, win, :]
                vw = v_s[r, win, :]
                sc = lax.dot_general(qr, kw, NT, preferred_element_type=F32) * A_SCALE
                sc = jnp.where(mask, sc, NEG)
                p = jnp.exp(sc - lse_ref[rows, :][:, :1])
                da_b = da_ref[rows, :].astype(BF16)
                dp = lax.dot_general(da_b, vw, NT, preferred_element_type=F32)
                ds_b = (p * (dp - dl_ref[rows, :][:, :1]) * A_SCALE).astype(BF16)
                p_b = p.astype(BF16)
                dq = jnp.dot(ds_b, kw, preferred_element_type=F32)
                stg[rows, :] = _rope_t(dq, cos, sin, HD // 2)
                dk_s[r, win, :] += lax.dot_general(ds_b, qr, TN, preferred_element_type=F32)
                dv_s[r, win, :] += lax.dot_general(p_b, da_b, TN, preferred_element_type=F32)
                return carry

            lax.fori_loop(0, d * nblk, step, 0, unroll=4)
            emit()

        @pl.when(w == 1)
        def _():
            def put(i, carry):
                r, n = i // nblk, i % nblk
                rows = _dil_rows(r, n, d)
                src = pl.ds(pl.multiple_of(n * 128 + 128, 128), 128)
                stg[rows, :] = _rope_t(dk_s[r, src, :], cos_ref[rows, :], sin_ref[rows, :], HD // 2)
                return carry

            lax.fori_loop(0, d * nblk, put, 0, unroll=4)
            emit()

        @pl.when(w == 2)
        def _():
            def put(i, carry):
                r, n = i // nblk, i % nblk
                src = pl.ds(pl.multiple_of(n * 128 + 128, 128), 128)
                stg[_dil_rows(r, n, d), :] = dv_s[r, src, :]
                return carry

            lax.fori_loop(0, d * nblk, put, 0, unroll=4)
            emit()

    def col(off):
        return lambda b, h, w: (ahead(b, h, w)[0], 0, off // HD + HPG * g + ahead(b, h, w)[1])

    def ahead(b, h, w):
        flat = jnp.minimum(b * HPG + h + jnp.where(w > 0, 1, 0), NB * HPG - 1)
        return flat // HPG, flat % HPG

    blk = (None, S, HD)
    tab = pl.BlockSpec((S, HD), lambda b, h, w: (0, 0))
    per_head = pl.BlockSpec(blk, lambda b, h, w: (ahead(b, h, w)[0], 0, ahead(b, h, w)[1]))
    kv = pl.BlockSpec((None, None, d, ln + 128, HD), lambda b, h, w: (*ahead(b, h, w), 0, 0, 0))
    out = pl.pallas_call(
        body, name=f"attn_bwd_d{d}", grid=(NB, HPG, 3),
        in_specs=[pl.BlockSpec(blk, col(QA)), kv, kv, tab, tab,
                  pl.BlockSpec(blk, col(0)), per_head, per_head, pl.BlockSpec(memory_space=pl.ANY)],
        out_specs=pl.BlockSpec(blk, lambda b, h, w: (b, 0, (AW // HD) * w + HPG * g + h)),
        out_shape=jax.ShapeDtypeStruct(dproj_v.shape, BF16),
        input_output_aliases={8: 0},
        scratch_shapes=[pltpu.VMEM((S, HD), F32), pltpu.VMEM((d, ln + 128, HD), F32), pltpu.VMEM((d, ln + 128, HD), F32)]
        + [pltpu.VMEM((S, HD), F32)] * (1 if d > 1 else 0),
        compiler_params=_cp(("arbitrary",) * 3),
    )(proj_v, kr, vr, cosf, sinf, da_v, lse_v, delta_v, dproj_v)
    return out.reshape(T, NC)


def _attn_mix(proj, os_, ls_, dep=None):
    tr = 256
    gw = HPG * HD
    deps = [] if dep is None else [dep]

    def body(o0, o1, o2, l0, l1, l2, z_ref, *rest):
        cat_ref = rest[-1]
        m = jnp.maximum(jnp.maximum(l0[...], l1[...]), l2[...])
        e = [jnp.exp(l[...] - m) for l in (l0, l1, l2)]
        inv = 1.0 / (e[0] + e[1] + e[2])
        for gi, o in enumerate((o0, o1, o2)):
            z = z_ref[:, gi * gw:(gi + 1) * gw].astype(F32)
            cat_ref[:, gi * gw:(gi + 1) * gw] = (o[...] * (e[gi] * inv) * (z * _sigmoid(z))).astype(BF16)

    grp = pl.BlockSpec((tr, gw), lambda i: (i, 0))
    return pl.pallas_call(
        body, name="attn_mix", grid=(T // tr,),
        in_specs=[grp] * 6 + [pl.BlockSpec((tr, AW), lambda i: (i, ZA // AW))] + [ANY_SPEC] * len(deps),
        out_specs=pl.BlockSpec((tr, AW), lambda i: (i, 0)),
        out_shape=jax.ShapeDtypeStruct((T, D), BF16),
        compiler_params=_cp(("parallel",)),
    )(*os_, *ls_, proj, *deps)


def _attn_mix_bwd(dcat, proj, os_, ls_, dep=None):
    tr = 256
    gw = HPG * HD
    deps = [] if dep is None else [dep]

    def body(dy_ref, o0, o1, o2, l0, l1, l2, z_ref, *rest):
        da_ref, lse_ref, dl_ref, dz_ref = rest[len(deps):]
        m = jnp.maximum(jnp.maximum(l0[...], l1[...]), l2[...])
        e = [jnp.exp(l[...] - m) for l in (l0, l1, l2)]
        den = e[0] + e[1] + e[2]
        inv = 1.0 / den
        lse_ref[...] = m + jnp.log(den)
        acc = jnp.zeros((tr, gw), F32)
        for gi, o in enumerate((o0, o1, o2)):
            cols = slice(gi * gw, (gi + 1) * gw)
            z = z_ref[:, cols].astype(F32)
            dy = dy_ref[:, cols]
            sg = _sigmoid(z)
            a = o[...] * (e[gi] * inv)
            da = dy * (z * sg)
            da_ref[:, cols] = da
            dz_ref[:, cols] = (dy * a * (sg * (1.0 + z * (1.0 - sg)))).astype(BF16)
            acc = acc + da * a
        for hh in range(HPG):
            cols = slice(hh * HD, (hh + 1) * HD)
            dl_ref[:, cols] = jnp.broadcast_to(jnp.sum(acc[:, cols], axis=1, keepdims=True), (tr, HD))

    grp = pl.BlockSpec((tr, gw), lambda i: (i, 0))
    return pl.pallas_call(
        body, name="attn_mix_bwd", grid=(T // tr,),
        in_specs=[pl.BlockSpec((tr, AW), lambda i: (i, 0))] + [grp] * 6 + [pl.BlockSpec((tr, AW), lambda i: (i, ZA // AW))]
        + [pl.BlockSpec(memory_space=pl.ANY)] * len(deps),
        out_specs=[pl.BlockSpec((tr, AW), lambda i: (i, 0)), grp, grp, pl.BlockSpec((tr, AW), lambda i: (i, ZA // AW))],
        out_shape=[jax.ShapeDtypeStruct((T, AW), F32), jax.ShapeDtypeStruct((T, gw), F32),
                   jax.ShapeDtypeStruct((T, gw), F32), jax.ShapeDtypeStruct((T, NC), BF16)],
        compiler_params=_cp(("parallel",)),
    )(dcat, *os_, *ls_, proj, *deps)


CT = 256


def _shift_down(x, n):
    rows = lax.broadcasted_iota(jnp.int32, x.shape, 0)
    return jnp.where(rows >= n, pltpu.roll(x, n, 0), 0.0)


def _shift_up(x, n):
    rows = lax.broadcasted_iota(jnp.int32, x.shape, 0)
    return jnp.where(rows < x.shape[0] - n, pltpu.roll(x, x.shape[0] - n, 0), 0.0)


def _conv_fwd(proj, conv_w, cat):
    proj_v = proj.reshape(NB, S, NC)
    cat_v = cat.reshape(NB, S, D)

    def body(u_ref, b_ref, c_ref, z_ref, w_ref, cat_in, o_ref):
        del cat_in
        cu = c_ref[...].astype(F32) * u_ref[...].astype(F32)
        cv = _shift_down(cu, 2) * w_ref[0:1, :] + _shift_down(cu, 1) * w_ref[1:2, :] + cu * w_ref[2:3, :]
        z = z_ref[...].astype(F32)
        o_ref[...] = (b_ref[...].astype(F32) * cv * (z * _sigmoid(z))).astype(BF16)

    def seg(off):
        return pl.BlockSpec((None, S, CT), lambda b, j: (b, 0, off // CT + j))

    out = pl.pallas_call(
        body, name="conv_fwd", grid=(NB, CW // CT),
        in_specs=[seg(UC), seg(BC), seg(CC), seg(ZC), pl.BlockSpec((3, CT), lambda b, j: (0, j)),
                  pl.BlockSpec(memory_space=pl.ANY)],
        out_specs=pl.BlockSpec((None, S, CT), lambda b, j: (b, 0, AW // CT + j)),
        out_shape=jax.ShapeDtypeStruct((NB, S, D), BF16),
        input_output_aliases={5: 0},
        compiler_params=_cp(("parallel", "parallel")),
    )(proj_v, proj_v, proj_v, proj_v, conv_w, cat_v)
    return out.reshape(T, D)


def _conv_bwd(dcat, proj, conv_w, dproj, dep=None):
    deps = [] if dep is None else [dep]
    proj_v = proj.reshape(NB, S, NC)
    dproj_v = dproj.reshape(NB, S, NC)
    dcat_v = dcat.reshape(NB, S, D)

    def body(dy_ref, u_ref, b_ref, c_ref, z_ref, w_ref, *rest):
        o_ref, dw_ref, st = rest[1 + len(deps):]
        b = pl.program_id(1)
        w = pl.program_id(2)

        @pl.when((b == 0) & (w == 0))
        def _():
            dw_ref[...] = jnp.zeros_like(dw_ref)

        @pl.when(w == 0)
        def _():
            u, c, z, bb = (r[...].astype(F32) for r in (u_ref, c_ref, z_ref, b_ref))
            dy = dy_ref[...]
            cu = c * u
            s1 = _shift_down(cu, 1)
            s2 = _shift_down(cu, 2)
            cv = s2 * w_ref[0:1, :] + s1 * w_ref[1:2, :] + cu * w_ref[2:3, :]
            sg = _sigmoid(z)
            sz = z * sg
            dcv = dy * bb * sz
            st[0] = dy * cv * sz
            st[2] = dy * bb * cv * (sg * (1.0 + z * (1.0 - sg)))
            dw_ref[0:1, :] += jnp.sum(dcv * s2, axis=0, keepdims=True)
            dw_ref[1:2, :] += jnp.sum(dcv * s1, axis=0, keepdims=True)
            dw_ref[2:3, :] += jnp.sum(dcv * cu, axis=0, keepdims=True)
            dcu = dcv * w_ref[2:3, :] + _shift_up(dcv, 1) * w_ref[1:2, :] + _shift_up(dcv, 2) * w_ref[0:1, :]
            st[1] = dcu * u
            o_ref[...] = (dcu * c).astype(BF16)

        for k in range(3):
            @pl.when(w == k + 1)
            def _(k=k):
                o_ref[...] = st[k].astype(BF16)

    def ahead(j, b, w):
        flat = jnp.minimum(j * NB + b + jnp.where(w > 0, 1, 0), (CW // CT) * NB - 1)
        return flat // NB, flat % NB

    def seg(off):
        return pl.BlockSpec((None, S, CT), lambda j, b, w: (ahead(j, b, w)[1], 0, off // CT + ahead(j, b, w)[0]))

    out, dw = pl.pallas_call(
        body, name="conv_bwd", grid=(CW // CT, NB, 4),
        in_specs=[seg(AW), seg(UC), seg(BC), seg(CC), seg(ZC),
                  pl.BlockSpec((3, CT), lambda j, b, w: (0, ahead(j, b, w)[0])),
                  pl.BlockSpec(memory_space=pl.ANY)] + [pl.BlockSpec(memory_space=pl.ANY)] * len(deps),
        out_specs=[pl.BlockSpec((None, S, CT), lambda j, b, w: (b, 0, (UC + w * CW) // CT + j)),
                   pl.BlockSpec((3, CT), lambda j, b, w: (0, j))],
        out_shape=[jax.ShapeDtypeStruct((NB, S, NC), BF16), jax.ShapeDtypeStruct((3, CW), F32)],
        input_output_aliases={6: 0},
        scratch_shapes=[pltpu.VMEM((3, S, CT), F32)],
        compiler_params=_cp(("arbitrary",) * 3),
    )(dcat_v, proj_v, proj_v, proj_v, proj_v, conv_w, dproj_v, *deps)
    return out.reshape(T, NC), dw


XT = 1024


def _cross_fwd(proj, mkv, cosq, sinq, cosm, sinm, cat):
    proj_v = proj.reshape(NB, S, NC)
    mkv_v = mkv.reshape(NB, MLEN, 2 * XW)
    cat_v = cat.reshape(NB, S, D)

    def body(q_ref, z_ref, mk_ref, mv_ref, cq, sq, cm, sm, cat_in, o_ref):
        del cat_in
        mkr = _rope(mk_ref[...], cm[...], sm[...], XHD // 2).astype(BF16)
        qr = _rope(q_ref[...].astype(F32), cq[...], sq[...], XHD // 2).astype(BF16)
        sc = lax.dot_general(qr, mkr, NT, preferred_element_type=F32) * X_SCALE
        p = jnp.exp(sc - jnp.max(sc, axis=1, keepdims=True))
        p = p / jnp.sum(p, axis=1, keepdims=True)
        ox = jnp.dot(p.astype(BF16), mv_ref[...].astype(BF16), preferred_element_type=F32)
        z = z_ref[...].astype(F32)
        o_ref[...] = (ox * (z * _sigmoid(z))).astype(BF16)

    def seg(off):
        return pl.BlockSpec((None, XT, XHD), lambda b, h, t: (b, t, off // XHD + h))

    qtab = pl.BlockSpec((XT, XHD), lambda b, h, t: (t, 0))
    mtab = pl.BlockSpec((MLEN, XHD), lambda b, h, t: (0, 0))
    out = pl.pallas_call(
        body, name="cross_fwd", grid=(NB, NXH, S // XT),
        in_specs=[seg(QX), seg(ZX),
                  pl.BlockSpec((None, MLEN, XHD), lambda b, h, t: (b, 0, h)),
                  pl.BlockSpec((None, MLEN, XHD), lambda b, h, t: (b, 0, NXH + h)),
                  qtab, qtab, mtab, mtab, pl.BlockSpec(memory_space=pl.ANY)],
        out_specs=pl.BlockSpec((None, XT, XHD), lambda b, h, t: (b, t, (AW + CW) // XHD + h)),
        out_shape=jax.ShapeDtypeStruct((NB, S, D), BF16),
        input_output_aliases={8: 0},
        compiler_params=_cp(("parallel",) * 3),
    )(proj_v, proj_v, mkv_v, mkv_v, cosq, sinq, cosm, sinm, cat_v)
    return out.reshape(T, D)


def _cross_bwd(dcat, proj, mkv, cosq, sinq, cosm, sinm, dproj):
    proj_v = proj.reshape(NB, S, NC)
    dproj_v = dproj.reshape(NB, S, NC)
    dcat_v = dcat.reshape(NB, S, D)
    mkv_v = mkv.reshape(NB, MLEN, 2 * XW)
    nt = S // XT

    def body(dy_ref, q_ref, z_ref, mk_ref, mv_ref, cq, sq, cm, sm, dp_in, o_ref, dmk_ref, dmv_ref, dz_s):
        del dp_in
        t = pl.program_id(2)
        w = pl.program_id(3)

        @pl.when((t == 0) & (w == 0))
        def _():
            dmk_ref[...] = jnp.zeros_like(dmk_ref)
            dmv_ref[...] = jnp.zeros_like(dmv_ref)

        @pl.when(w == 0)
        def _():
            mkr = _rope(mk_ref[...], cm[...], sm[...], XHD // 2).astype(BF16)
            mv_b = mv_ref[...].astype(BF16)
            qr = _rope(q_ref[...].astype(F32), cq[...], sq[...], XHD // 2).astype(BF16)
            sc = lax.dot_general(qr, mkr, NT, preferred_element_type=F32) * X_SCALE
            p = jnp.exp(sc - jnp.max(sc, axis=1, keepdims=True))
            p = p / jnp.sum(p, axis=1, keepdims=True)
            p_b = p.astype(BF16)
            ox = jnp.dot(p_b, mv_b, preferred_element_type=F32)
            z = z_ref[...].astype(F32)
            dy = dy_ref[...]
            sg = _sigmoid(z)
            dz_s[...] = dy * ox * (sg * (1.0 + z * (1.0 - sg)))
            dox_b = (dy * (z * sg)).astype(BF16)
            dp = lax.dot_general(dox_b, mv_b, NT, preferred_element_type=F32)
            ds_b = (p * (dp - jnp.sum(dp * p, axis=1, keepdims=True)) * X_SCALE).astype(BF16)
            dq = jnp.dot(ds_b, mkr, preferred_element_type=F32)
            o_ref[...] = _rope_t(dq, cq[...], sq[...], XHD // 2).astype(BF16)
            dmk_ref[...] += lax.dot_general(ds_b, qr, TN, preferred_element_type=F32)
            dmv_ref[...] += lax.dot_general(p_b, dox_b, TN, preferred_element_type=F32)

        @pl.when(w == 1)
        def _():
            o_ref[...] = dz_s[...].astype(BF16)

        @pl.when((t == nt - 1) & (w == 1))
        def _():
            dmk_ref[...] = _rope_t(dmk_ref[...], cm[...], sm[...], XHD // 2)

    def seg(off):
        return pl.BlockSpec((None, XT, XHD), lambda b, h, t, w: (b, t, off // XHD + h))

    qtab = pl.BlockSpec((XT, XHD), lambda b, h, t, w: (t, 0))
    mtab = pl.BlockSpec((MLEN, XHD), lambda b, h, t, w: (0, 0))
    macc = pl.BlockSpec((None, MLEN, XHD), lambda b, h, t, w: (b, 0, h))
    out, dmk, dmv = pl.pallas_call(
        body, name="cross_bwd", grid=(NB, NXH, nt, 2),
        in_specs=[pl.BlockSpec((None, XT, XHD), lambda b, h, t, w: (b, t, (AW + CW) // XHD + h)),
                  seg(QX), seg(ZX),
                  pl.BlockSpec((None, MLEN, XHD), lambda b, h, t, w: (b, 0, h)),
                  pl.BlockSpec((None, MLEN, XHD), lambda b, h, t, w: (b, 0, NXH + h)),
                  qtab, qtab, mtab, mtab, pl.BlockSpec(memory_space=pl.ANY)],
        out_specs=[pl.BlockSpec((None, XT, XHD), lambda b, h, t, w: (b, t, (QX + w * XW) // XHD + h)), macc, macc],
        out_shape=[jax.ShapeDtypeStruct((NB, S, NC), BF16), jax.ShapeDtypeStruct((NB, MLEN, XW), F32),
                   jax.ShapeDtypeStruct((NB, MLEN, XW), F32)],
        input_output_aliases={9: 0},
        scratch_shapes=[pltpu.VMEM((XT, XHD), F32)],
        compiler_params=_cp(("arbitrary",) * 4),
    )(dcat_v, proj_v, proj_v, mkv_v, mkv_v, cosq, sinq, cosm, sinm, dproj_v)
    return out.reshape(T, NC), dmk, dmv


def _local_step(x, mem, pre_norm, conv_w, mem_norm, post_norm, tgt, chip, core, comm, moments=None):
    x2 = x.reshape(T, D)
    mem2 = mem.reshape(NB * MLEN, D)
    tgt2 = tgt.reshape(T, D)
    cosa, sina = _rope_tables(jnp.arange(S), HD // 2)
    cosq, sinq = _rope_tables(jnp.arange(S) + MLEN, XHD // 2)
    cosm, sinm = _rope_tables(jnp.arange(MLEN), XHD // 2)

    h = _rms_fwd(x2, pre_norm, "pre_norm_fwd", dep=comm.gather_started())
    memn = _rms_fwd(mem2, mem_norm, "mem_norm_fwd")
    proj = _proj_part(h, comm.w_in_own(), chip, (0,), "proj_own")
    wg_in, tok = comm.w_in_near(after=[proj, memn, conv_w])
    proj = _proj_part(h, wg_in, chip, (REL_XOR[0], REL_XOR[1]), "proj_near", prev=proj, dep=tok)
    wg_in, tok = comm.w_in_all(after=proj)
    proj = _proj_part(h, wg_in, chip, (REL_XOR[2],), "proj_far", prev=proj, dep=tok)
    fw = [_attn_fwd(proj, cosa, sina, g) for g in range(3)]
    os_ = [f[0] for f in fw]
    ls_ = [f[1] for f in fw]
    tok = comm.w_rest_landed(after=ls_)
    cat = _attn_mix(proj, os_, ls_, dep=tok)
    cat = _conv_fwd(proj, conv_w, cat)
    wg_kv, wg_out = comm.w_rest(after=[cat])
    mkv = _mm_nn(memn, wg_kv, "mkv", NB * MLEN, 1024, 512)
    cat = _cross_fwd(proj, mkv, cosq, sinq, cosm, sinm, cat)
    y = _mm_nn(cat, wg_out, "out_proj", 1024, 2048, 2048)
    dy, dout, d_post, loss = _post_norm_loss(y, x2, tgt2, post_norm)

    dcat = _mm_nt(dy, wg_out, "dcat", 1024, 2048, 2048)
    g_send = _grad_rows(cat, dy,1 - core, BF16, "grad_w_out_send")
    sent = comm.sibling_start(g_send.reshape(NCHIP, D // NCHIP // 2, D), "a")
    g_keep = _grad_rows(cat, dy,core, BF16, "grad_w_out_keep", dep=sent[-1])
    red_a = comm.reduce_start_summed(g_keep.reshape(NCHIP, D // NCHIP // 2, D), sent, "a")
    da, lse, delta, dproj = _attn_mix_bwd(dcat, proj, os_, ls_, dep=red_a[-1])
    for g in range(3):
        dproj = _attn_bwd(proj, fw[g][2], fw[g][3], cosa, sina, da, lse, delta, dproj, g)
    dproj, dmk, dmv = _cross_bwd(dcat, proj, mkv, cosq, sinq, cosm, sinm, dproj)
    dmkv = jnp.concatenate([dmk, dmv], axis=-1).reshape(NB * MLEN, 2 * XW)
    dmkv_b = dmkv.astype(BF16)
    gk_send = _grad_rows(memn, dmkv_b, 1 - core, BF16, "grad_w_mem_kv_send")
    sent = comm.sibling_start(gk_send.reshape(NCHIP, D // NCHIP // 2, 2 * XW), "k")
    gk_keep = _grad_rows(memn, dmkv_b, core, BF16, "grad_w_mem_kv_keep", dep=sent[-1])
    dmemn = _mm_nt(dmkv_b, wg_kv, "dmemn", NB * MLEN, 1024, 512)
    d_mem = _norm_gain_grad(dmemn, mem2, "mem_norm_bwd")
    red_k = comm.reduce_start_summed(gk_keep.reshape(NCHIP, D // NCHIP // 2, 2 * XW), sent, "k", after=d_mem)
    dproj, d_conv = _conv_bwd(dcat, proj, conv_w, dproj, dep=red_k[-1])
    (r_out,) = comm.reduce_finish(red_a, "a", after=d_conv)
    (r_kv,) = comm.reduce_finish(red_k, "k", after=d_conv)

    g_send = _grad_w_in(h, dproj, 1 - core, BF16, "grad_w_in_send", dep=r_kv)
    sent = comm.sibling_start(g_send.reshape(NCHIP, D // 2, WB), "b")
    g_keep = _grad_w_in(h, dproj, core, BF16, "grad_w_in_keep", dep=sent[-1])
    red_b = comm.reduce_start_summed(g_keep.reshape(NCHIP, D // 2, WB), sent, "b")
    sets = [] if moments is None else [(moments["w_out"][0], r_out, *moments["w_out"][1:]),
                                      (moments["w_mem_kv"][0], r_kv, *moments["w_mem_kv"][1:])]
    dh, updates = _dh(dproj, wg_in, dep=red_b[-1], adam=sets)
    grad_x, d_pre = _pre_norm_bwd(dh, x2, pre_norm, dout, 0)
    r_in = comm.reduce_finish_start(red_b, "b", after=grad_x)
    grad_x, d_pre = _pre_norm_bwd(dh, x2, pre_norm, dout, 1, prev=(grad_x, d_pre), dep=r_in[-1])
    return loss, grad_x.reshape(NB, S, D), d_pre, d_mem, d_post, d_conv, r_in, (r_kv, r_out), updates


def _adamw(w, g, m, v, name):
    rows, cols = w.shape
    tr = rows if rows <= 512 else 512
    tc = cols if cols <= 1024 else 1024
    if cols % tc:
        tc = 896

    blk = pl.BlockSpec((tr, tc), lambda i, j: (i, j))
    sds = jax.ShapeDtypeStruct((rows, cols), F32)
    def body(*refs):
        _adamw_math(*refs)

    return pl.pallas_call(
        body, name=name, grid=(rows // tr, cols // tc),
        in_specs=[blk] * 4, out_specs=[blk] * 4, out_shape=[sds] * 4,
        compiler_params=_cp(("parallel", "parallel")),
    )(w, g, m, v)


def _place():
    return lax.axis_index("x"), lax.axis_index("y"), lax.axis_index("c")


def _other_chips(x, y):
    return [(1 - x, y), (x, 1 - y), (1 - x, 1 - y)]


def _tile_cols(cols):
    return cols if cols <= 1024 else (1024 if cols % 1024 == 0 else 896)


def _cast_own(w, chip, name, half=None, prev=None):
    rows, cols = w.shape
    tr, tc = 512, _tile_cols(cols)
    nrt = rows // tr if half is None else rows // tr // 2
    index = jnp.stack([chip, 0 if half is None else half]).astype(jnp.int32)
    extra = [] if prev is None else [prev]

    def body(ix_ref, w_ref, *rest):
        rest[-1][...] = w_ref[...].astype(BF16)

    grid_spec = pltpu.PrefetchScalarGridSpec(
        num_scalar_prefetch=1, grid=(nrt, cols // tc),
        in_specs=[pl.BlockSpec((tr, tc), lambda i, j, ix: (ix[1] * nrt + i, j))] + [ANY_SPEC] * len(extra),
        out_specs=pl.BlockSpec((None, tr, tc), lambda i, j, ix: (ix[0], ix[1] * nrt + i, j)))
    return pl.pallas_call(
        body, name=name, grid_spec=grid_spec,
        out_shape=jax.ShapeDtypeStruct((NCHIP, rows, cols), BF16),
        input_output_aliases={} if prev is None else {2: 0},
        compiler_params=_cp(("parallel", "parallel")),
    )(index, w, *extra)


HBM_SPEC = pl.BlockSpec(memory_space=pltpu.HBM)
SEM_SPEC = pl.BlockSpec(memory_space=pltpu.SEMAPHORE)
ANY_SPEC = pl.BlockSpec(memory_space=pl.ANY)
EFFECT = pltpu.SideEffectType.DATAFLOW_SIDE_EFFECTING
TOKEN = jax.ShapeDtypeStruct((8, 128), F32)


def _half(ref, chip, hc):
    hr = ref.shape[1] // 2
    return ref.at[chip, pl.ds(hc * hr, hr), :]


NEAR = (0, 1)
FAR = (2,)
REL_XOR = (2, 1, 3)


def _gather_copies(refs, send_sems, recv_sems, rels):
    x, y, c = _place()
    chips = _other_chips(x, y)
    out, inc = [], []
    for a, ref in enumerate(refs):
        for p, j in enumerate(rels):
            px, py = chips[j]
            mine = _half(ref, 2 * x + y, c)
            theirs = _half(ref, 2 * px + py, c)
            sems = dict(send_sem=send_sems.at[len(rels) * a + p], recv_sem=recv_sems.at[len(rels) * a + p],
                        device_id=(px, py, c), device_id_type=MESH)
            out.append(pltpu.make_async_remote_copy(src_ref=mine, dst_ref=mine, **sems))
            inc.append(pltpu.make_async_remote_copy(src_ref=theirs, dst_ref=theirs, **sems))
    return out, inc


def _gather_start(bufs, groups, name):
    n = len(bufs)
    ng = len(groups)

    def body(*refs):
        ins = refs[:n]
        token = refs[-1]
        for gi, rels in enumerate(groups):
            out, _ = _gather_copies(ins, refs[n + 2 * gi], refs[n + 2 * gi + 1], rels)
            for cp in out:
                cp.start()
        token[...] = jnp.zeros_like(token)

    sems = []
    for rels in groups:
        sems += [pltpu.SemaphoreType.DMA((len(rels) * n,))] * 2
    res = pl.pallas_call(
        body, name=name,
        in_specs=[HBM_SPEC] * n,
        out_specs=[SEM_SPEC] * (2 * ng) + [HBM_SPEC] * n + [pl.BlockSpec(memory_space=pltpu.VMEM)],
        out_shape=sems + [pltpu.HBM(b.shape, b.dtype) for b in bufs] + [TOKEN],
        input_output_aliases={a: 2 * ng + a for a in range(n)},
        compiler_params=pltpu.CompilerParams(has_side_effects=EFFECT),
    )(*[pltpu.with_memory_space_constraint(b, pltpu.HBM) for b in bufs])
    return [(res[2 * gi], res[2 * gi + 1]) for gi in range(ng)], list(res[2 * ng:2 * ng + n]), res[-1]


def _gather_wait(bufs, sems, rels, after, name):
    n = len(bufs)
    send_sems, recv_sems = sems
    after = list(after) if isinstance(after, (list, tuple)) else [after]

    def body(*refs):
        ins = refs[:n]
        out, inc = _gather_copies(ins, refs[n], refs[n + 1], rels)
        for cp in out:
            cp.wait_send()
        for cp in inc:
            cp.wait_recv()

    return pl.pallas_call(
        body, name=name,
        in_specs=[HBM_SPEC] * n + [SEM_SPEC, SEM_SPEC] + [ANY_SPEC] * len(after),
        out_specs=[HBM_SPEC] * n,
        out_shape=[pltpu.HBM(b.shape, b.dtype) for b in bufs],
        input_output_aliases={a: a for a in range(n)},
        compiler_params=pltpu.CompilerParams(has_side_effects=EFFECT),
    )(*bufs, send_sems, recv_sems, *after)


def _forward_halves(bufs, rels, name):
    n = len(bufs)

    def body(*refs):
        cps, waits = _forward_copies(refs[n:2 * n], rels, refs[2 * n], refs[2 * n + 1])
        for cp in cps:
            cp.start()
        for cp in waits:
            cp.wait_recv()
        for cp in cps:
            cp.wait_send()

    return pl.pallas_call(
        body, name=name,
        in_specs=[ANY_SPEC] * n, out_specs=[ANY_SPEC] * n,
        out_shape=[jax.ShapeDtypeStruct(s.shape, s.dtype) for s in bufs],
        input_output_aliases={a: a for a in range(n)},
        scratch_shapes=[pltpu.SemaphoreType.DMA((len(rels) * n,)), pltpu.SemaphoreType.DMA((len(rels) * n,))],
    )(*bufs)


def _forward_copies(refs, rels, send_sems, recv_sems):
    x, y, c = _place()
    chips = _other_chips(x, y)
    cps, waits = [], []
    for a, ref in enumerate(refs):
        for p, j in enumerate(rels):
            px, py = chips[j]
            sems = dict(send_sem=send_sems.at[len(rels) * a + p], recv_sem=recv_sems.at[len(rels) * a + p],
                        device_id=(x, y, 1 - c), device_id_type=MESH)
            got = _half(ref, 2 * px + py, c)
            want = _half(ref, 2 * px + py, 1 - c)
            cps.append(pltpu.make_async_remote_copy(src_ref=got, dst_ref=got, **sems))
            waits.append(pltpu.make_async_remote_copy(src_ref=want, dst_ref=want, **sems))
    return cps, waits


def _forward_start(bufs, rels, name):
    n = len(bufs)

    def body(*refs):
        cps, _ = _forward_copies(refs[:n], rels, refs[n], refs[n + 1])
        for cp in cps:
            cp.start()
        refs[-1][...] = jnp.zeros_like(refs[-1])

    res = pl.pallas_call(
        body, name=name,
        in_specs=[HBM_SPEC] * n,
        out_specs=[SEM_SPEC, SEM_SPEC] + [HBM_SPEC] * n + [pl.BlockSpec(memory_space=pltpu.VMEM)],
        out_shape=[pltpu.SemaphoreType.DMA((len(rels) * n,))] * 2 + [pltpu.HBM(b.shape, b.dtype) for b in bufs] + [TOKEN],
        input_output_aliases={a: 2 + a for a in range(n)},
        compiler_params=pltpu.CompilerParams(has_side_effects=EFFECT),
    )(*[pltpu.with_memory_space_constraint(b, pltpu.HBM) for b in bufs])
    return (res[0], res[1]), list(res[2:2 + n]), res[-1]


def _forward_wait(bufs, sems, rels, after, name):
    n = len(bufs)

    def body(*refs):
        cps, waits = _forward_copies(refs[:n], rels, refs[n], refs[n + 1])
        for cp in cps:
            cp.wait_send()
        for cp in waits:
            cp.wait_recv()

    return pl.pallas_call(
        body, name=name,
        in_specs=[HBM_SPEC] * n + [SEM_SPEC, SEM_SPEC] + [ANY_SPEC] * len(after),
        out_specs=[HBM_SPEC] * n,
        out_shape=[pltpu.HBM(b.shape, b.dtype) for b in bufs],
        input_output_aliases={a: a for a in range(n)},
        compiler_params=pltpu.CompilerParams(has_side_effects=EFFECT),
    )(*bufs, sems[0], sems[1], *after)


def _sibling_copy(src, land, send_sems, recv_sems):
    x, y, c = _place()
    return pltpu.make_async_remote_copy(src_ref=src, dst_ref=land, send_sem=send_sems.at[0], recv_sem=recv_sems.at[0],
                                        device_id=(x, y, 1 - c), device_id_type=MESH)


def _sibling_start(part, name):
    def body(src, land, send_sems, recv_sems, src_thru, land_thru, token):
        _sibling_copy(src, land, send_sems, recv_sems).start()
        token[...] = jnp.zeros_like(token)

    land = lax.empty(part.shape, part.dtype)
    return pl.pallas_call(
        body, name=name,
        in_specs=[HBM_SPEC] * 2,
        out_specs=[SEM_SPEC, SEM_SPEC, HBM_SPEC, HBM_SPEC, pl.BlockSpec(memory_space=pltpu.VMEM)],
        out_shape=[pltpu.SemaphoreType.DMA((1,)), pltpu.SemaphoreType.DMA((1,)), pltpu.HBM(part.shape, part.dtype),
                   pltpu.HBM(part.shape, part.dtype), TOKEN],
        input_output_aliases={0: 2, 1: 3},
        compiler_params=pltpu.CompilerParams(has_side_effects=EFFECT),
    )(pltpu.with_memory_space_constraint(part, pltpu.HBM), pltpu.with_memory_space_constraint(land, pltpu.HBM))


def _sibling_wait(state, after, name):
    send_sems, recv_sems, part, land, _ = state

    def body(src, land_ref, send_ref, recv_ref, *rest):
        cp = _sibling_copy(src, land_ref, send_ref, recv_ref)
        cp.wait_send()
        cp.wait_recv()

    return pl.pallas_call(
        body, name=name,
        in_specs=[HBM_SPEC, HBM_SPEC, SEM_SPEC, SEM_SPEC] + [ANY_SPEC] * len(after),
        out_specs=[HBM_SPEC, HBM_SPEC],
        out_shape=[pltpu.HBM(part.shape, part.dtype), pltpu.HBM(land.shape, land.dtype)],
        input_output_aliases={0: 0, 1: 1},
        compiler_params=pltpu.CompilerParams(has_side_effects=EFFECT),
    )(part, land, send_sems, recv_sems, *after)[1]


def _pair_sum_rows(keep, got, name):
    nblk, rows, cols = keep.shape
    tr, tc = 512, _tile_cols(cols)

    def body(k_ref, g_ref, o_ref):
        o_ref[...] = (k_ref[...].astype(F32) + g_ref[...].astype(F32)).astype(BF16)

    blk = pl.BlockSpec((None, tr, tc), lambda b, i, j: (b, i, j))
    return pl.pallas_call(
        body, name=name, grid=(nblk, rows // tr, cols // tc),
        in_specs=[blk, blk], out_specs=blk,
        out_shape=jax.ShapeDtypeStruct(keep.shape, BF16),
        compiler_params=_cp(("parallel",) * 3),
    )(keep, got)


def _scatter_start(qs, name):
    n = len(qs)

    def body(*refs):
        ins, lands = refs[:n], refs[n:2 * n]
        token = refs[-1]
        for cp in _scatter_copies(ins, lands, refs[2 * n], refs[2 * n + 1]):
            cp.start()
        token[...] = jnp.zeros_like(token)

    lands = [lax.empty((3,) + q.shape[1:], q.dtype) for q in qs]
    res = pl.pallas_call(
        body, name=name,
        in_specs=[HBM_SPEC] * (2 * n),
        out_specs=[SEM_SPEC, SEM_SPEC] + [HBM_SPEC] * (2 * n) + [pl.BlockSpec(memory_space=pltpu.VMEM)],
        out_shape=[pltpu.SemaphoreType.DMA((3 * n,)), pltpu.SemaphoreType.DMA((3 * n,))]
        + [pltpu.HBM(b.shape, b.dtype) for b in qs + lands] + [TOKEN],
        input_output_aliases={a: 2 + a for a in range(2 * n)},
        compiler_params=pltpu.CompilerParams(has_side_effects=EFFECT),
    )(*[pltpu.with_memory_space_constraint(b, pltpu.HBM) for b in qs + lands])
    return res[0], res[1], list(res[2:2 + n]), list(res[2 + n:2 + 2 * n]), res[-1]


def _scatter_copies(ins, lands, send_sems, recv_sems):
    x, y, c = _place()
    cps = []
    for a in range(len(ins)):
        for j, (px, py) in enumerate(_other_chips(x, y)):
            cps.append(pltpu.make_async_remote_copy(
                src_ref=ins[a].at[2 * px + py], dst_ref=lands[a].at[j],
                send_sem=send_sems.at[3 * a + j], recv_sem=recv_sems.at[3 * a + j], device_id=(px, py, c), device_id_type=MESH))
    return cps


def _scatter_wait(qs, lands, send_sems, recv_sems, after, name):
    n = len(qs)

    def body(*refs):
        for cp in _scatter_copies(refs[:n], refs[n:2 * n], refs[2 * n], refs[2 * n + 1]):
            cp.wait_send()
            cp.wait_recv()

    res = pl.pallas_call(
        body, name=name,
        in_specs=[HBM_SPEC] * (2 * n) + [SEM_SPEC, SEM_SPEC, ANY_SPEC],
        out_specs=[HBM_SPEC] * (2 * n),
        out_shape=[pltpu.HBM(b.shape, b.dtype) for b in qs + lands],
        input_output_aliases={a: a for a in range(2 * n)},
        compiler_params=pltpu.CompilerParams(has_side_effects=EFFECT),
    )(*qs, *lands, send_sems, recv_sems, after)
    return list(res[:n]), list(res[n:])


def _chip_sum(q, got, name):
    _, hr, cols = got.shape
    tr, tc = 512, _tile_cols(cols)
    chip = 2 * lax.axis_index("x") + lax.axis_index("y")
    c = lax.axis_index("c")

    def body(idx_ref, q_ref, g_ref, o_ref):
        del idx_ref
        acc = q_ref[...].astype(F32)
        for i in range(3):
            acc = acc + g_ref[i].astype(F32)
        o_ref[...] = acc

    grid_spec = pltpu.PrefetchScalarGridSpec(
        num_scalar_prefetch=1, grid=(hr // tr, cols // tc),
        in_specs=[pl.BlockSpec((None, tr, tc), lambda i, j, ix: (ix[0], i, j)),
                  pl.BlockSpec((3, tr, tc), lambda i, j, ix: (0, i, j))],
        out_specs=pl.BlockSpec((None, tr, tc), lambda i, j, ix: (ix[1], i, j)))
    return pl.pallas_call(
        body, name=name, grid_spec=grid_spec,
        out_shape=jax.ShapeDtypeStruct((2, hr, cols), F32),
        compiler_params=_cp(("parallel", "parallel")),
    )(jnp.stack([chip, c]).astype(jnp.int32), q, got)


def _join_halves(bufs, name):
    n = len(bufs)

    def body(*refs):
        outs = refs[n:2 * n]
        send_sems, recv_sems = refs[2 * n:]
        x, y, c = _place()
        cps = []
        for a in range(n):
            cps.append(pltpu.make_async_remote_copy(src_ref=outs[a].at[c], dst_ref=outs[a].at[c], send_sem=send_sems.at[a],
                                                    recv_sem=recv_sems.at[a], device_id=(x, y, 1 - c), device_id_type=MESH))
        for cp in cps:
            cp.start()
        for a in range(n):
            theirs = outs[a].at[1 - c]
            pltpu.make_async_remote_copy(src_ref=theirs, dst_ref=theirs, send_sem=send_sems.at[a], recv_sem=recv_sems.at[a],
                                         device_id=(x, y, 1 - c), device_id_type=MESH).wait_recv()
        for cp in cps:
            cp.wait_send()

    hbm = pl.BlockSpec(memory_space=pl.ANY)
    return pl.pallas_call(
        body, name=name,
        in_specs=[hbm] * n, out_specs=[hbm] * n,
        out_shape=[jax.ShapeDtypeStruct(b.shape, b.dtype) for b in bufs],
        input_output_aliases={a: a for a in range(n)},
        scratch_shapes=[pltpu.SemaphoreType.DMA((n,)), pltpu.SemaphoreType.DMA((n,))],
    )(*bufs)


def _join_copy(buf, send_sems, recv_sems):
    x, y, c = _place()
    sems = dict(send_sem=send_sems.at[0], recv_sem=recv_sems.at[0], device_id=(x, y, 1 - c), device_id_type=MESH)
    return (pltpu.make_async_remote_copy(src_ref=buf.at[c], dst_ref=buf.at[c], **sems),
            pltpu.make_async_remote_copy(src_ref=buf.at[1 - c], dst_ref=buf.at[1 - c], **sems))


def _join_start(buf, name):
    def body(b_ref, send_sems, recv_sems, thru, token):
        _join_copy(b_ref, send_sems, recv_sems)[0].start()
        token[...] = jnp.zeros_like(token)

    return pl.pallas_call(
        body, name=name,
        in_specs=[HBM_SPEC],
        out_specs=[SEM_SPEC, SEM_SPEC, HBM_SPEC, pl.BlockSpec(memory_space=pltpu.VMEM)],
        out_shape=[pltpu.SemaphoreType.DMA((1,)), pltpu.SemaphoreType.DMA((1,)), pltpu.HBM(buf.shape, buf.dtype), TOKEN],
        input_output_aliases={0: 2},
        compiler_params=pltpu.CompilerParams(has_side_effects=EFFECT),
    )(pltpu.with_memory_space_constraint(buf, pltpu.HBM))


def _join_wait(state, after, name):
    send_sems, recv_sems, buf, _ = state

    def body(b_ref, send_ref, recv_ref, *rest):
        out, inc = _join_copy(b_ref, send_ref, recv_ref)
        out.wait_send()
        inc.wait_recv()

    return pl.pallas_call(
        body, name=name,
        in_specs=[HBM_SPEC, SEM_SPEC, SEM_SPEC] + [ANY_SPEC] * len(after),
        out_specs=HBM_SPEC,
        out_shape=pltpu.HBM(buf.shape, buf.dtype),
        input_output_aliases={0: 0},
        compiler_params=pltpu.CompilerParams(has_side_effects=EFFECT),
    )(buf, send_sems, recv_sems, *after)


N_DEV = 8


def _gather_small(block, reduce, name, deps=()):
    m_per, cols = block.shape
    deps = list(deps)

    def body(x_ref, *rest):
        out_ref, all_ref, send_sems, recv_sems, local_sem = rest[len(deps):]
        x, y, c = _place()
        me, sibling = (x, y, c), (x, y, 1 - c)
        chips = _other_chips(x, y)

        def rows(px, py, pc):
            return all_ref.at[pl.ds((4 * px + 2 * py + pc) * m_per, m_per), :]

        def copy(k, block_of, to, src=None):
            return pltpu.make_async_remote_copy(
                src_ref=rows(*block_of) if src is None else src, dst_ref=rows(*block_of),
                send_sem=send_sems.at[k], recv_sem=recv_sems.at[k], device_id=to, device_id_type=MESH)

        mine = pltpu.make_async_copy(x_ref, rows(*me), local_sem)
        mine.start()
        first = [copy(0, me, sibling, src=x_ref)]
        first += [copy(1 + j, me, (*chip, c), src=x_ref) for j, chip in enumerate(chips)]
        for cp in first:
            cp.start()
        passed = [copy(4 + j, (*chip, c), sibling) for j, chip in enumerate(chips)]
        for j, chip in enumerate(chips):
            copy(1 + j, (*chip, c), me).wait_recv()
            passed[j].start()
        copy(0, sibling, me).wait_recv()
        for j, chip in enumerate(chips):
            copy(4 + j, (*chip, 1 - c), me).wait_recv()
        for cp in first + passed:
            cp.wait_send()
        mine.wait()
        if reduce:
            acc = all_ref[pl.ds(0, m_per), :]
            for i in range(1, N_DEV):
                acc = acc + all_ref[pl.ds(i * m_per, m_per), :]
            out_ref[...] = acc
        else:
            out_ref[...] = all_ref[...]

    out_rows = m_per if reduce else N_DEV * m_per
    return pl.pallas_call(
        body, name=name,
        in_specs=[pl.BlockSpec(memory_space=pltpu.VMEM)] + [pl.BlockSpec(memory_space=pl.ANY)] * len(deps),
        out_specs=pl.BlockSpec(memory_space=pltpu.VMEM),
        out_shape=jax.ShapeDtypeStruct((out_rows, cols), F32),
        scratch_shapes=[pltpu.VMEM((N_DEV * m_per, cols), F32), pltpu.SemaphoreType.DMA((7,)),
                        pltpu.SemaphoreType.DMA((7,)), pltpu.SemaphoreType.DMA],
    )(block, *deps)


class _Comm:
    def __init__(self, w_in, w_kv, w_out, chip, core):
        self.w_in, self.chip, self.core = w_in, chip, core
        self.bufs = [_cast_own(w_in, chip, "cast_w_in_sent", half=core), _cast_own(w_kv, chip, "cast_w_mem_kv"),
                     _cast_own(w_out, chip, "cast_w_out")]


    def gather_started(self):
        (self.sems,), (b_in,), tok = _gather_start(self.bufs[:1], (NEAR,), "gather_start_in_near")
        self.b_in = _cast_own(self.w_in, self.chip, "cast_w_in_kept", half=1 - self.core, prev=b_in)
        return [tok]

    def w_in_own(self):
        return self.b_in.reshape(NCHIP * D, WB)

    def w_in_near(self, after):
        (b_in,) = _gather_wait([self.b_in], self.sems, NEAR, list(after) + self.bufs[1:], "gather_wait_in_near")
        (b_in,) = _forward_halves([b_in], NEAR, "forward_in_near")
        (self.sems,), (self.b_in,), tok = _gather_start([b_in], (FAR,), "gather_start_in_far")
        return self.b_in.reshape(NCHIP * D, WB), tok

    def w_in_all(self, after):
        (b_in,) = _gather_wait([self.b_in], self.sems, FAR, after, "gather_wait_in_far")
        (b_in,) = _forward_halves([b_in], FAR, "forward_in_far")
        (self.sems,), self.b_rest, tok = _gather_start(self.bufs[1:], (NEAR + FAR,), "gather_start_rest")
        return b_in.reshape(NCHIP * D, WB), tok

    def w_rest_landed(self, after):
        b_rest = _gather_wait(self.b_rest, self.sems, NEAR + FAR, after, "gather_wait_rest")
        self.sems, self.b_rest, tok = _forward_start(b_rest, NEAR + FAR, "forward_start_rest")
        return tok

    def w_rest(self, after):
        b_kv, b_out = _forward_wait(self.b_rest, self.sems, NEAR + FAR, after, "forward_wait_rest")
        return b_kv.reshape(D, 2 * XW), b_out.reshape(D, D)

    def sibling_start(self, send, tag):
        return _sibling_start(send, "sibling_start_" + tag)

    def reduce_start_summed(self, keep, sent, tag, after=None):
        got = _sibling_wait(sent, [keep] + ([] if after is None else [after]), "sibling_wait_" + tag)
        return _scatter_start([_pair_sum_rows(keep, got, "pair_sum_" + tag)], "scatter_start_" + tag)

    def reduce_finish_start(self, state, tag, after):
        send_sems, recv_sems, qs, lands, _ = state
        qs, lands = _scatter_wait(qs, lands, send_sems, recv_sems, after, "scatter_wait_" + tag)
        return _join_start(_chip_sum(qs[0], lands[0], f"chip_sum_{tag}0"), "join_start_" + tag)

    def reduce_finish_wait(self, pending, tag, after):
        j = _join_wait(pending, after, "join_wait_" + tag)
        return j.reshape(2 * j.shape[1], j.shape[2])

    def reduce_finish(self, state, tag, after):
        send_sems, recv_sems, qs, lands, _ = state
        qs, lands = _scatter_wait(qs, lands, send_sems, recv_sems, after, "scatter_wait_" + tag)
        halves = [_chip_sum(q, l, f"chip_sum_{tag}{i}") for i, (q, l) in enumerate(zip(qs, lands))]
        return [j.reshape(2 * j.shape[1], j.shape[2]) for j in _join_halves(halves, "join_halves_" + tag)]


def kernel(x, mem, pre_norm, w_in, conv_w, mem_norm, w_mem_kv, w_out, post_norm, loss_target, m_pre_norm, m_w_in, m_conv_w, m_mem_norm, m_w_mem_kv, m_w_out, m_post_norm, v_pre_norm, v_w_in, v_conv_w, v_mem_norm, v_w_mem_kv, v_w_out, v_post_norm):
    chip = 2 * lax.axis_index("x") + lax.axis_index("y")

    comm = _Comm(w_in[0], w_mem_kv[0], w_out[0], chip, lax.axis_index("c"))
    cw_blk = jnp.zeros((8, 384), F32).at[:3].set(conv_w[0])
    cw_all = _gather_small(cw_blk, False, "gather_conv_w", deps=comm.bufs[1:]).reshape(NCHIP, 2, 8, 384)[:, 0, :3]
    conv_full = jnp.transpose(cw_all, (1, 0, 2)).reshape(3, CW)
    moments = {"w_out": (w_out[0], m_w_out[0], v_w_out[0]), "w_mem_kv": (w_mem_kv[0], m_w_mem_kv[0], v_w_mem_kv[0])}
    loss, grad_x, d_pre, d_mem, d_post, d_conv, r_in, _, (upd_out, upd_kv) = _local_step(
        x, mem, pre_norm, conv_full, mem_norm, post_norm, loss_target, chip, lax.axis_index("c"), comm, moments)

    pack = jnp.concatenate([d_pre, d_mem, d_post, jnp.pad(d_conv, ((0, 0), (0, D - CW))),
                            jnp.pad(loss, ((0, 0), (0, D - 128))), jnp.zeros((1, D), F32)], axis=0)
    tot = _gather_small(pack, True, "reduce_small")
    g_pre, g_mem, g_post = tot[0:1], tot[1:2], tot[2:3]
    g_conv = lax.dynamic_slice(tot[3:6, :CW], (0, chip * 384), (3, 384))
    loss_out = tot[6, 0]

    names = ("pre_norm", "w_in", "conv_w", "mem_norm", "w_mem_kv", "w_out", "post_norm")
    ws = (pre_norm, w_in[0], conv_w[0], mem_norm, w_mem_kv[0], w_out[0], post_norm)
    gs = [g_pre, None, g_conv, g_mem, None, None, g_post]
    ms = (m_pre_norm, m_w_in[0], m_conv_w[0], m_mem_norm, m_w_mem_kv[0], m_w_out[0], m_post_norm)
    vs = (v_pre_norm, v_w_in[0], v_conv_w[0], v_mem_norm, v_w_mem_kv[0], v_w_out[0], v_post_norm)
    upd = [None if g is None else _adamw(w, g, m, v, "adamw_" + nm) for nm, w, g, m, v in zip(names, ws, gs, ms, vs)]
    upd[4], upd[5] = upd_kv, upd_out
    g_in = comm.reduce_finish_wait(r_in, "b", after=[u[1] for u in upd if u is not None])
    upd[1] = _adamw(ws[1], g_in, ms[1], vs[1], "adamw_w_in")

    def shaped(arrs):
        return [a.reshape(w.shape) if w.ndim == a.ndim else a.reshape((1,) + a.shape)
                for a, w in zip(arrs, (pre_norm, w_in, conv_w, mem_norm, w_mem_kv, w_out, post_norm))]

    grads = shaped([u[0] for u in upd])
    deltas = shaped([u[1] for u in upd])
    new_m = shaped([u[2] for u in upd])
    new_v = shaped([u[3] for u in upd])
    return (loss_out, grad_x, *grads, *deltas, *new_m, *new_v)
```

```python
import jax
import jax.numpy as jnp
from jax import lax
from jax.experimental import pallas as pl
from jax.experimental.pallas import tpu as pltpu

F32 = jnp.float32
BF16 = jnp.bfloat16

D = 4096
S = 2048
NB = 2
T = NB * S
MLEN = 256
HD = 128
AW = 1536
CW = 1536
XW = 1024
XHD = 256
NXH = 4
NC = 14336
QA, KA, VA, ZA, UC, BC, CC, ZC, QX, ZX = 0, 1536, 3072, 4608, 6144, 7680, 9216, 10752, 12288, 13312
NCHIP = 4
WB = NC // NCHIP
DIL = (1, 4, 16)
HPG = 4
EPS = 1e-6
NEG = -1e30
ROPE_THETA = 10000.0
A_SCALE = HD ** -0.5
X_SCALE = XHD ** -0.5

ADAM_LR = 0.001
ADAM_B1 = 0.9
ADAM_B2 = 0.999
ADAM_EPS = 1e-08
ADAM_WD = 0.01
ADAM_STEP = 10

MESH = pl.DeviceIdType.MESH
MIB = 1024 * 1024


def _cp(sem, vmem_mib=48):
    return pltpu.CompilerParams(dimension_semantics=sem, vmem_limit_bytes=vmem_mib * MIB)


def _sigmoid(z):
    return 1.0 / (1.0 + jnp.exp(-z))


def _rope(x, cos, sin, half):
    return x * cos + pltpu.roll(x, half, 1) * sin


def _rope_t(g, cos, sin, half):
    return g * cos + pltpu.roll(g * sin, half, 1)


def _rms_fwd(x2, g, name, dep=None):
    rows = x2.shape[0]
    tr = 256
    deps = [] if dep is None else list(dep)

    def body(x_ref, g_ref, *rest):
        o_ref = rest[-1]
        x = x_ref[...]
        r = lax.rsqrt(jnp.mean(x * x, axis=-1, keepdims=True) + EPS)
        o_ref[...] = (x * r * g_ref[...]).astype(BF16)

    return pl.pallas_call(
        body, name=name, grid=(rows // tr,),
        in_specs=[pl.BlockSpec((tr, D), lambda i: (i, 0)), pl.BlockSpec((1, D), lambda i: (0, 0))]
        + [pl.BlockSpec(memory_space=pl.ANY)] * len(deps),
        out_specs=pl.BlockSpec((tr, D), lambda i: (i, 0)),
        out_shape=jax.ShapeDtypeStruct((rows, D), BF16),
        compiler_params=_cp(("parallel",)),
    )(x2, g, *deps)


def _norm_gain_grad(dn, x2, name):
    rows = x2.shape[0]
    tr = 256

    def body(dn_ref, x_ref, dg_ref):
        @pl.when(pl.program_id(0) == 0)
        def _():
            dg_ref[...] = jnp.zeros_like(dg_ref)
        x = x_ref[...]
        r = lax.rsqrt(jnp.mean(x * x, axis=-1, keepdims=True) + EPS)
        dg_ref[...] += jnp.sum(dn_ref[...] * (x * r), axis=0, keepdims=True)

    return pl.pallas_call(
        body, name=name, grid=(rows // tr,),
        in_specs=[pl.BlockSpec((tr, D), lambda i: (i, 0)), pl.BlockSpec((tr, D), lambda i: (i, 0))],
        out_specs=pl.BlockSpec((1, D), lambda i: (0, 0)),
        out_shape=jax.ShapeDtypeStruct((1, D), F32),
        compiler_params=_cp(("arbitrary",)),
    )(dn, x2)


def _pre_norm_bwd(dh, x2, g, dout, part, prev=None, dep=None):
    tr = 256
    nsteps = T // tr // 2
    extra = ([] if prev is None else list(prev)) + ([] if dep is None else [dep])

    def body(dh_ref, x_ref, g_ref, dout_ref, *rest):
        gx_ref, dg_ref = rest[len(extra):]

        @pl.when(pl.program_id(0) == 0)
        def _():
            dg_ref[...] = jnp.zeros_like(dg_ref) if prev is None else rest[1][...]

        x = x_ref[...]
        dh_ = dh_ref[...].astype(F32)
        r = lax.rsqrt(jnp.mean(x * x, axis=-1, keepdims=True) + EPS)
        xhat = x * r
        dg_ref[...] += jnp.sum(dh_ * xhat, axis=0, keepdims=True)
        dxn = dh_ * g_ref[...]
        gx_ref[...] = dout_ref[...].astype(F32) + r * (dxn - xhat * jnp.mean(dxn * xhat, axis=-1, keepdims=True))

    row = pl.BlockSpec((tr, D), lambda i: (i + part * nsteps, 0))
    vec = pl.BlockSpec((1, D), lambda i: (0, 0))
    return pl.pallas_call(
        body, name=f"pre_norm_bwd_{part}", grid=(nsteps,),
        in_specs=[row, row, vec, row] + ([] if prev is None else [pl.BlockSpec(memory_space=pl.ANY), vec])
        + [pl.BlockSpec(memory_space=pl.ANY)] * (0 if dep is None else 1),
        out_specs=[row, vec],
        out_shape=[jax.ShapeDtypeStruct((T, D), F32), jax.ShapeDtypeStruct((1, D), F32)],
        input_output_aliases={} if prev is None else {4: 0},
        compiler_params=_cp(("arbitrary",)),
    )(dh, x2, g, dout, *extra)


def _post_norm_loss(y, x2, tgt, g):
    tr = 256

    def body(y_ref, x_ref, t_ref, g_ref, dy_ref, dout_ref, dg_ref, loss_ref):
        @pl.when(pl.program_id(0) == 0)
        def _():
            dg_ref[...] = jnp.zeros_like(dg_ref)
            loss_ref[...] = jnp.zeros_like(loss_ref)
        yv = y_ref[...]
        gv = g_ref[...]
        r = lax.rsqrt(jnp.mean(yv * yv, axis=-1, keepdims=True) + EPS)
        yhat = yv * r
        err = x_ref[...] + yhat * gv - t_ref[...]
        loss_ref[...] += jnp.sum(jnp.sum(err * err, axis=1, keepdims=True), axis=0, keepdims=True) * (0.5 / D)
        dout = err * (1.0 / D)
        dout_ref[...] = dout.astype(BF16)
        dg_ref[...] += jnp.sum(dout * yhat, axis=0, keepdims=True)
        dyn = dout * gv
        dy_ref[...] = (r * (dyn - yhat * jnp.mean(dyn * yhat, axis=-1, keepdims=True))).astype(BF16)

    row = pl.BlockSpec((tr, D), lambda i: (i, 0))
    vec = pl.BlockSpec((1, D), lambda i: (0, 0))
    return pl.pallas_call(
        body, name="post_norm_loss", grid=(T // tr,),
        in_specs=[row, row, row, vec],
        out_specs=[row, row, vec, pl.BlockSpec((1, 128), lambda i: (0, 0))],
        out_shape=[jax.ShapeDtypeStruct((T, D), BF16), jax.ShapeDtypeStruct((T, D), BF16),
                   jax.ShapeDtypeStruct((1, D), F32), jax.ShapeDtypeStruct((1, 128), F32)],
        compiler_params=_cp(("arbitrary",)),
    )(y, x2, tgt, g)


NN = (((1,), (0,)), ((), ()))
NT = (((1,), (1,)), ((), ()))
TN = (((0,), (0,)), ((), ()))


def _as_index(v):
    return jnp.reshape(v, (1,)).astype(jnp.int32)


def _matmul(a, b, *, name, dims, grid, a_block, a_map, b_block, b_map, o_block, o_map, out_shape, out_dtype=F32,
            index=None, prev=None, deps=(), addend=None):
    extra = ([] if prev is None else [prev]) + [d for d in deps if d is not None]
    first = 0 if index is None else 1
    nk = grid[2]
    in_place = out_dtype == F32
    assert addend is None or not in_place
    n_add = 0 if addend is None else 1

    def body(*refs):
        a_ref, b_ref, o_ref = refs[first], refs[first + 1], refs[first + 2 + n_add + len(extra)]
        acc_ref = o_ref if in_place else refs[-1]

        @pl.when(pl.program_id(2) == 0)
        def _():
            acc_ref[...] = lax.dot_general(a_ref[...], b_ref[...], dims, preferred_element_type=F32)

        @pl.when(pl.program_id(2) > 0)
        def _():
            acc_ref[...] += lax.dot_general(a_ref[...], b_ref[...], dims, preferred_element_type=F32)

        if not in_place:
            @pl.when(pl.program_id(2) == nk - 1)
            def _():
                total = acc_ref[...] if addend is None else acc_ref[...] + refs[first + 2][...].astype(F32)
                o_ref[...] = total.astype(o_ref.dtype)

    adds = [] if addend is None else [addend]
    in_specs = ([pl.BlockSpec(a_block, a_map), pl.BlockSpec(b_block, b_map)] + [pl.BlockSpec(o_block, o_map)] * n_add
                + [pl.BlockSpec(memory_space=pl.ANY)] * len(extra))
    out_specs = pl.BlockSpec(o_block, o_map)
    scratch = [] if in_place else [pltpu.VMEM(o_block, F32)]
    kwargs = dict(name=name, out_shape=jax.ShapeDtypeStruct(out_shape, out_dtype),
                  input_output_aliases={} if prev is None else {first + 2 + n_add: 0},
                  compiler_params=_cp(("parallel", "parallel", "arbitrary"), vmem_mib=56))
    if index is None:
        return pl.pallas_call(body, grid=grid, in_specs=in_specs, out_specs=out_specs, scratch_shapes=scratch,
                              **kwargs)(a, b, *adds, *extra)
    grid_spec = pltpu.PrefetchScalarGridSpec(num_scalar_prefetch=1, grid=grid, in_specs=in_specs, out_specs=out_specs,
                                             scratch_shapes=scratch)
    return pl.pallas_call(body, grid_spec=grid_spec, **kwargs)(_as_index(index), a, b, *adds, *extra)


def _mm_nn(a, b, name, tm, tn, tk):
    m, kd = a.shape
    n = b.shape[1]
    return _matmul(a, b, name=name, dims=NN, grid=(m // tm, n // tn, kd // tk),
                   a_block=(tm, tk), a_map=lambda i, j, k: (i, k),
                   b_block=(tk, tn), b_map=lambda i, j, k: (k, j),
                   o_block=(tm, tn), o_map=lambda i, j, k: (i, j), out_shape=(m, n))


def _mm_nt(a, b, name, tm, tn, tk):
    m, kd = a.shape
    n = b.shape[0]
    return _matmul(a, b, name=name, dims=NT, grid=(m // tm, n // tn, kd // tk),
                   a_block=(tm, tk), a_map=lambda i, j, k: (i, k),
                   b_block=(tn, tk), b_map=lambda i, j, k: (j, k),
                   o_block=(tm, tn), o_map=lambda i, j, k: (i, j), out_shape=(m, n))


W_TN = 1792
W_NJ = WB // W_TN


def _proj_part(h, wg, chip, masks, name, prev=None, dep=None):
    tm, tk = 1024, 2048

    def blk(j, ix):
        m = masks[0]
        for t in range(1, len(masks)):
            m = jnp.where(j // W_NJ == t, masks[t], m)
        return jnp.bitwise_xor(ix[0], m)

    return _matmul(h, wg, name=name, dims=NN, grid=(T // tm, len(masks) * W_NJ, D // tk), index=chip, prev=prev, deps=(dep,),
                   a_block=(tm, tk), a_map=lambda i, j, k, ix: (i, k),
                   b_block=(tk, W_TN), b_map=lambda i, j, k, ix: (blk(j, ix) * (D // tk) + k, j % W_NJ),
                   o_block=(tm, W_TN), o_map=lambda i, j, k, ix: (i, blk(j, ix) * W_NJ + j % W_NJ), out_shape=(T, NC),
                   out_dtype=BF16)


def _adamw_math(w_ref, g_ref, m_ref, v_ref, go_ref, d_ref, nm_ref, nv_ref):
    gv = g_ref[...]
    go_ref[...] = gv
    nm = ADAM_B1 * m_ref[...] + (1.0 - ADAM_B1) * gv
    nv = ADAM_B2 * v_ref[...] + (1.0 - ADAM_B2) * (gv * gv)
    m_hat = nm / (1.0 - ADAM_B1 ** ADAM_STEP)
    v_hat = nv / (1.0 - ADAM_B2 ** ADAM_STEP)
    d_ref[...] = -ADAM_LR * (m_hat / (jnp.sqrt(v_hat) + ADAM_EPS) + ADAM_WD * w_ref[...])
    nm_ref[...] = nm
    nv_ref[...] = nv


def _dh(dproj, wg, dep=None, adam=()):
    tm, tn = 1024, 2048
    grid = (T // tm, D // tn, NC // W_TN)
    nsteps = grid[0] * grid[1] * grid[2]
    deps = [] if dep is None else [dep]
    na = len(adam)
    nk = grid[2]

    def body(*refs):
        a_ref, b_ref = refs[0], refs[1]
        o_ref = refs[2 + 4 * na + len(deps)]
        acc_ref = refs[-1]

        @pl.when(pl.program_id(2) == 0)
        def _():
            acc_ref[...] = lax.dot_general(a_ref[...], b_ref[...], NT, preferred_element_type=F32)

        @pl.when(pl.program_id(2) > 0)
        def _():
            acc_ref[...] += lax.dot_general(a_ref[...], b_ref[...], NT, preferred_element_type=F32)

        @pl.when(pl.program_id(2) == nk - 1)
        def _():
            o_ref[...] = acc_ref[...].astype(BF16)

        for s in range(na):
            ins = refs[2 + 4 * s:6 + 4 * s]
            outs = refs[3 + 4 * na + len(deps) + 4 * s:7 + 4 * na + len(deps) + 4 * s]
            _adamw_math(*ins, *outs)

    def rows_of(arr):
        r, c = arr.shape
        return pl.BlockSpec((r // nsteps, c), lambda i, j, k: ((i * grid[1] + j) * grid[2] + k, 0))

    adam_specs = [rows_of(a) for st in adam for a in st]
    res = pl.pallas_call(
        body, name="dh", grid=grid,
        in_specs=[pl.BlockSpec((tm, W_TN), lambda i, j, k: (i, k)),
                  pl.BlockSpec((tn, W_TN), lambda i, j, k: ((k // W_NJ) * (D // tn) + j, k % W_NJ))]
        + adam_specs + [pl.BlockSpec(memory_space=pl.ANY)] * len(deps),
        out_specs=[pl.BlockSpec((tm, tn), lambda i, j, k: (i, j))] + adam_specs,
        out_shape=[jax.ShapeDtypeStruct((T, D), BF16)] + [jax.ShapeDtypeStruct(a.shape, F32) for st in adam for a in st],
        scratch_shapes=[pltpu.VMEM((tm, tn), F32)],
        compiler_params=_cp(("arbitrary",) * 3, vmem_mib=56),
    )(dproj, wg, *[a for st in adam for a in st], *deps)
    return res[0], [tuple(res[1 + 4 * s:5 + 4 * s]) for s in range(na)]


def _grad_rows(a, b, half, out_dtype, name, dep=None):
    kd, n = b.shape
    tm, tn, tk = D // NCHIP // 2, min(n, 2048), min(kd, 2048)
    return _matmul(a, b, name=name, dims=TN, grid=(NCHIP, n // tn, kd // tk), index=half, deps=(dep,),
                   a_block=(tk, tm), a_map=lambda i, j, k, ix: (k, 2 * i + ix[0]),
                   b_block=(tk, tn), b_map=lambda i, j, k, ix: (k, j),
                   o_block=(tm, tn), o_map=lambda i, j, k, ix: (i, j),
                   out_shape=(NCHIP * tm, n), out_dtype=out_dtype)


GW_TM = 1024


def _grad_w_in(h, dproj, half, out_dtype, name, dep=None, tiles=(0, 2), addend=None, prev=None):
    tm, tk = GW_TM, 2048
    nh = D // 2 // tm
    t0, nt = tiles

    def o_map(i, j, k, ix):
        return ((j // W_NJ) * nh + t0 + i, j % W_NJ)

    return _matmul(h, dproj, name=name, dims=TN, grid=(nt, NC // W_TN, T // tk), index=half, deps=(dep,), prev=prev,
                   a_block=(tk, tm), a_map=lambda i, j, k, ix: (k, ix[0] * nh + t0 + i),
                   b_block=(tk, W_TN), b_map=lambda i, j, k, ix: (k, j),
                   o_block=(tm, W_TN), o_map=o_map, addend=addend,
                   out_shape=(NCHIP * D // 2, WB), out_dtype=out_dtype)


def _rope_tables(pos, half):
    inv = 1.0 / (ROPE_THETA ** (jnp.arange(half, dtype=F32) / half))
    ang = pos.astype(F32)[:, None] * inv[None, :]
    cos, sin = jnp.cos(ang), jnp.sin(ang)
    return jnp.concatenate([cos, cos], axis=1), jnp.concatenate([-sin, sin], axis=1)


def _band_mask(r0):
    qi = lax.broadcasted_iota(jnp.int32, (128, 256), 0)
    kk = lax.broadcasted_iota(jnp.int32, (128, 256), 1)
    return (kk >= qi) & (kk <= qi + 128) & (kk + r0 >= 128)


def _window(r0, nblk):
    if nblk == 1:
        qi = lax.broadcasted_iota(jnp.int32, (128, 128), 0)
        kk = lax.broadcasted_iota(jnp.int32, (128, 128), 1)
        return pl.ds(128, 128), kk <= qi
    return pl.ds(r0, 256), _band_mask(r0)


def _dil_rows(r, n, d):
    if d == 1:
        return pl.ds(pl.multiple_of(n * 128, 128), 128)
    return pl.ds(r + d * 128 * n, 128, stride=d)


def _widen(refs, wide):
    if not wide:
        return refs

    def copy(n, carry):
        rows = pl.ds(pl.multiple_of(n * 256, 256), 256)
        for src, dst in zip(refs, wide):
            dst[rows, :] = src[rows, :].astype(F32)
        return carry

    lax.fori_loop(0, S // 256, copy, 0)
    return wide


def _attn_fwd(proj, cosf, sinf, g):
    d = DIL[g]
    ln = S // d
    nblk = ln // 128
    proj_v = proj.reshape(NB, S, NC)

    def body(q_ref, k_ref, v_ref, cos_ref, sin_ref, o_ref, l_ref, k_s, v_s, *wide):
        k_s[:, pl.ds(0, 128), :] = jnp.zeros((d, 128, HD), BF16)
        v_s[:, pl.ds(0, 128), :] = jnp.zeros((d, 128, HD), BF16)
        q_ref, k_ref, v_ref = _widen((q_ref, k_ref, v_ref), wide)

        def prep(i, carry):
            r, n = i // nblk, i % nblk
            rows = _dil_rows(r, n, d)
            dst = pl.ds(pl.multiple_of(n * 128 + 128, 128), 128)
            k_s[r, dst, :] = _rope(k_ref[rows, :].astype(F32), cos_ref[rows, :], sin_ref[rows, :], HD // 2).astype(BF16)
            v_s[r, dst, :] = v_ref[rows, :].astype(BF16)
            return carry

        lax.fori_loop(0, d * nblk, prep, 0, unroll=4)

        def step(i, carry):
            r, n = i // nblk, i % nblk
            rows = _dil_rows(r, n, d)
            r0 = pl.multiple_of(n * 128, 128)
            qr = _rope(q_ref[rows, :].astype(F32), cos_ref[rows, :], sin_ref[rows, :], HD // 2).astype(BF16)
            win, mask = _window(r0, nblk)
            kw = k_s[r, win, :]
            vw = v_s[r, win, :]
            sc = lax.dot_general(qr, kw, NT, preferred_element_type=F32) * A_SCALE
            sc = jnp.where(mask, sc, NEG)
            m = jnp.max(sc, axis=1, keepdims=True)
            p = jnp.exp(sc - m)
            l = jnp.sum(p, axis=1, keepdims=True)
            o_ref[rows, :] = jnp.dot(p.astype(BF16), vw, preferred_element_type=F32) / l
            l_ref[rows, :] = jnp.broadcast_to(m + jnp.log(l), (128, HD))
            return carry

        lax.fori_loop(0, d * nblk, step, 0, unroll=4)

    def col(off):
        return lambda b, h: (b, 0, off // HD + HPG * g + h)

    blk = (None, S, HD)
    tab = pl.BlockSpec((S, HD), lambda b, h: (0, 0))
    out = pl.BlockSpec(blk, lambda b, h: (b, 0, h))
    kv = pl.BlockSpec((None, None, d, ln + 128, HD), lambda b, h: (b, h, 0, 0, 0))
    o, l, kr, vr = pl.pallas_call(
        body, name=f"attn_fwd_d{d}", grid=(NB, HPG),
        in_specs=[pl.BlockSpec(blk, col(QA)), pl.BlockSpec(blk, col(KA)), pl.BlockSpec(blk, col(VA)), tab, tab],
        out_specs=[out, out, kv, kv],
        out_shape=[jax.ShapeDtypeStruct((NB, S, HPG * HD), F32)] * 2
        + [jax.ShapeDtypeStruct((NB, HPG, d, ln + 128, HD), BF16)] * 2,
        scratch_shapes=[pltpu.VMEM((S, HD), F32)] * (3 if d > 1 else 0),
        compiler_params=_cp(("parallel", "parallel")),
    )(proj_v, proj_v, proj_v, cosf, sinf)
    return o.reshape(T, HPG * HD), l.reshape(T, HPG * HD), kr, vr


def _attn_bwd(proj, kr, vr, cosf, sinf, da, lse, delta, dproj, g):
    d = DIL[g]
    ln = S // d
    nblk = ln // 128
    proj_v = proj.reshape(NB, S, NC)
    dproj_v = dproj.reshape(NB, S, NC)
    da_v = da.reshape(NB, S, AW)
    lse_v = lse.reshape(NB, S, HPG * HD)
    delta_v = delta.reshape(NB, S, HPG * HD)

    def body(q_ref, k_s, v_s, cos_ref, sin_ref, da_ref, lse_ref, dl_ref, dp_in_ref, o_ref, stg, dk_s, dv_s, *wide):
        del dp_in_ref
        w = pl.program_id(2)

        def emit():
            def cast(n, carry):
                rows = pl.ds(pl.multiple_of(n * 256, 256), 256)
                o_ref[rows, :] = stg[rows, :].astype(BF16)
                return carry

            lax.fori_loop(0, S // 256, cast, 0)

        @pl.when(w == 0)
        def _():
            dk_s[...] = jnp.zeros_like(dk_s)
            dv_s[...] = jnp.zeros_like(dv_s)
            (q_src,) = _widen((q_ref,), wide)

            def step(i, carry):
                r, n = i // nblk, i % nblk
                rows = _dil_rows(r, n, d)
                r0 = pl.multiple_of(n * 128, 128)
                win, mask = _window(r0, nblk)
                cos, sin = cos_ref[rows, :], sin_ref[rows, :]
                qr = _rope(q_src[rows, :].astype(F32), cos, sin, HD // 2).astype(BF16)
                kw = k_s[r, win, :]
                vw = v_s[r, win, :]
                sc = lax.dot_general(qr, kw, NT, preferred_element_type=F32) * A_SCALE
                sc = jnp.where(mask, sc, NEG)
                p = jnp.exp(sc - lse_ref[rows, :][:, :1])
                da_b = da_ref[rows, :].astype(BF16)
                dp = lax.dot_general(da_b, vw, NT, preferred_element_type=F32)
                ds_b = (p * (dp - dl_ref[rows, :][:, :1]) * A_SCALE).astype(BF16)
                p_b = p.astype(BF16)
                dq = jnp.dot(ds_b, kw, preferred_element_type=F32)
                stg[rows, :] = _rope_t(dq, cos, sin, HD // 2)
                dk_s[r, win, :] += lax.dot_general(ds_b, qr, TN, preferred_element_type=F32)
                dv_s[r, win, :] += lax.dot_general(p_b, da_b, TN, preferred_element_type=F32)
                return carry

            lax.fori_loop(0, d * nblk, step, 0, unroll=4)
            emit()

        @pl.when(w == 1)
        def _():
            def put(i, carry):
                r, n = i // nblk, i % nblk
                rows = _dil_rows(r, n, d)
                src = pl.ds(pl.multiple_of(n * 128 + 128, 128), 128)
                stg[rows, :] = _rope_t(dk_s[r, src, :], cos_ref[rows, :], sin_ref[rows, :], HD // 2)
                return carry

            lax.fori_loop(0, d * nblk, put, 0, unroll=4)
            emit()

        @pl.when(w == 2)
        def _():
            def put(i, carry):
                r, n = i // nblk, i % nblk
                src = pl.ds(pl.multiple_of(n * 128 + 128, 128), 128)
                stg[_dil_rows(r, n, d), :] = dv_s[r, src, :]
                return carry

            lax.fori_loop(0, d * nblk, put, 0, unroll=4)
            emit()

    def col(off):
        return lambda b, h, w: (ahead(b, h, w)[0], 0, off // HD + HPG * g + ahead(b, h, w)[1])

    def ahead(b, h, w):
        flat = jnp.minimum(b * HPG + h + jnp.where(w > 0, 1, 0), NB * HPG - 1)
        return flat // HPG, flat % HPG

    blk = (None, S, HD)
    tab = pl.BlockSpec((S, HD), lambda b, h, w: (0, 0))
    per_head = pl.BlockSpec(blk, lambda b, h, w: (ahead(b, h, w)[0], 0, ahead(b, h, w)[1]))
    kv = pl.BlockSpec((None, None, d, ln + 128, HD), lambda b, h, w: (*ahead(b, h, w), 0, 0, 0))
    out = pl.pallas_call(
        body, name=f"attn_bwd_d{d}", grid=(NB, HPG, 3),
        in_specs=[pl.BlockSpec(blk, col(QA)), kv, kv, tab, tab,
                  pl.BlockSpec(blk, col(0)), per_head, per_head, pl.BlockSpec(memory_space=pl.ANY)],
        out_specs=pl.BlockSpec(blk, lambda b, h, w: (b, 0, (AW // HD) * w + HPG * g + h)),
        out_shape=jax.ShapeDtypeStruct(dproj_v.shape, BF16),
        input_output_aliases={8: 0},
        scratch_shapes=[pltpu.VMEM((S, HD), F32), pltpu.VMEM((d, ln + 128, HD), F32), pltpu.VMEM((d, ln + 128, HD), F32)]
        + [pltpu.VMEM((S, HD), F32)] * (1 if d > 1 else 0),
        compiler_params=_cp(("arbitrary",) * 3),
    )(proj_v, kr, vr, cosf, sinf, da_v, lse_v, delta_v, dproj_v)
    return out.reshape(T, NC)


def _attn_mix(proj, os_, ls_, dep=None):
    tr = 256
    gw = HPG * HD
    deps = [] if dep is None else [dep]

    def body(o0, o1, o2, l0, l1, l2, z_ref, *rest):
        cat_ref = rest[-1]
        m = jnp.maximum(jnp.maximum(l0[...], l1[...]), l2[...])
        e = [jnp.exp(l[...] - m) for l in (l0, l1, l2)]
        inv = 1.0 / (e[0] + e[1] + e[2])
        for gi, o in enumerate((o0, o1, o2)):
            z = z_ref[:, gi * gw:(gi + 1) * gw].astype(F32)
            cat_ref[:, gi * gw:(gi + 1) * gw] = (o[...] * (e[gi] * inv) * (z * _sigmoid(z))).astype(BF16)

    grp = pl.BlockSpec((tr, gw), lambda i: (i, 0))
    return pl.pallas_call(
        body, name="attn_mix", grid=(T // tr,),
        in_specs=[grp] * 6 + [pl.BlockSpec((tr, AW), lambda i: (i, ZA // AW))] + [ANY_SPEC] * len(deps),
        out_specs=pl.BlockSpec((tr, AW), lambda i: (i, 0)),
        out_shape=jax.ShapeDtypeStruct((T, D), BF16),
        compiler_params=_cp(("parallel",)),
    )(*os_, *ls_, proj, *deps)


def _attn_mix_bwd(dcat, proj, os_, ls_, dep=None):
    tr = 256
    gw = HPG * HD
    deps = [] if dep is None else [dep]

    def body(dy_ref, o0, o1, o2, l0, l1, l2, z_ref, *rest):
        da_ref, lse_ref, dl_ref, dz_ref = rest[len(deps):]
        m = jnp.maximum(jnp.maximum(l0[...], l1[...]), l2[...])
        e = [jnp.exp(l[...] - m) for l in (l0, l1, l2)]
        den = e[0] + e[1] + e[2]
        inv = 1.0 / den
        lse_ref[...] = m + jnp.log(den)
        acc = jnp.zeros((tr, gw), F32)
        for gi, o in enumerate((o0, o1, o2)):
            cols = slice(gi * gw, (gi + 1) * gw)
            z = z_ref[:, cols].astype(F32)
            dy = dy_ref[:, cols]
            sg = _sigmoid(z)
            a = o[...] * (e[gi] * inv)
            da = dy * (z * sg)
            da_ref[:, cols] = da
            dz_ref[:, cols] = (dy * a * (sg * (1.0 + z * (1.0 - sg)))).astype(BF16)
            acc = acc + da * a
        for hh in range(HPG):
            cols = slice(hh * HD, (hh + 1) * HD)
            dl_ref[:, cols] = jnp.broadcast_to(jnp.sum(acc[:, cols], axis=1, keepdims=True), (tr, HD))

    grp = pl.BlockSpec((tr, gw), lambda i: (i, 0))
    return pl.pallas_call(
        body, name="attn_mix_bwd", grid=(T // tr,),
        in_specs=[pl.BlockSpec((tr, AW), lambda i: (i, 0))] + [grp] * 6 + [pl.BlockSpec((tr, AW), lambda i: (i, ZA // AW))]
        + [pl.BlockSpec(memory_space=pl.ANY)] * len(deps),
        out_specs=[pl.BlockSpec((tr, AW), lambda i: (i, 0)), grp, grp, pl.BlockSpec((tr, AW), lambda i: (i, ZA // AW))],
        out_shape=[jax.ShapeDtypeStruct((T, AW), F32), jax.ShapeDtypeStruct((T, gw), F32),
                   jax.ShapeDtypeStruct((T, gw), F32), jax.ShapeDtypeStruct((T, NC), BF16)],
        compiler_params=_cp(("parallel",)),
    )(dcat, *os_, *ls_, proj, *deps)


CT = 256


def _shift_down(x, n):
    rows = lax.broadcasted_iota(jnp.int32, x.shape, 0)
    return jnp.where(rows >= n, pltpu.roll(x, n, 0), 0.0)


def _shift_up(x, n):
    rows = lax.broadcasted_iota(jnp.int32, x.shape, 0)
    return jnp.where(rows < x.shape[0] - n, pltpu.roll(x, x.shape[0] - n, 0), 0.0)


def _conv_fwd(proj, conv_w, cat):
    proj_v = proj.reshape(NB, S, NC)
    cat_v = cat.reshape(NB, S, D)

    def body(u_ref, b_ref, c_ref, z_ref, w_ref, cat_in, o_ref):
        del cat_in
        cu = c_ref[...].astype(F32) * u_ref[...].astype(F32)
        cv = _shift_down(cu, 2) * w_ref[0:1, :] + _shift_down(cu, 1) * w_ref[1:2, :] + cu * w_ref[2:3, :]
        z = z_ref[...].astype(F32)
        o_ref[...] = (b_ref[...].astype(F32) * cv * (z * _sigmoid(z))).astype(BF16)

    def seg(off):
        return pl.BlockSpec((None, S, CT), lambda b, j: (b, 0, off // CT + j))

    out = pl.pallas_call(
        body, name="conv_fwd", grid=(NB, CW // CT),
        in_specs=[seg(UC), seg(BC), seg(CC), seg(ZC), pl.BlockSpec((3, CT), lambda b, j: (0, j)),
                  pl.BlockSpec(memory_space=pl.ANY)],
        out_specs=pl.BlockSpec((None, S, CT), lambda b, j: (b, 0, AW // CT + j)),
        out_shape=jax.ShapeDtypeStruct((NB, S, D), BF16),
        input_output_aliases={5: 0},
        compiler_params=_cp(("parallel", "parallel")),
    )(proj_v, proj_v, proj_v, proj_v, conv_w, cat_v)
    return out.reshape(T, D)


def _conv_bwd(dcat, proj, conv_w, dproj, dep=None):
    deps = [] if dep is None else [dep]
    proj_v = proj.reshape(NB, S, NC)
    dproj_v = dproj.reshape(NB, S, NC)
    dcat_v = dcat.reshape(NB, S, D)

    def body(dy_ref, u_ref, b_ref, c_ref, z_ref, w_ref, *rest):
        o_ref, dw_ref, st = rest[1 + len(deps):]
        b = pl.program_id(1)
        w = pl.program_id(2)

        @pl.when((b == 0) & (w == 0))
        def _():
            dw_ref[...] = jnp.zeros_like(dw_ref)

        @pl.when(w == 0)
        def _():
            u, c, z, bb = (r[...].astype(F32) for r in (u_ref, c_ref, z_ref, b_ref))
            dy = dy_ref[...]
            cu = c * u
            s1 = _shift_down(cu, 1)
            s2 = _shift_down(cu, 2)
            cv = s2 * w_ref[0:1, :] + s1 * w_ref[1:2, :] + cu * w_ref[2:3, :]
            sg = _sigmoid(z)
            sz = z * sg
            dcv = dy * bb * sz
            st[0] = dy * cv * sz
            st[2] = dy * bb * cv * (sg * (1.0 + z * (1.0 - sg)))
            dw_ref[0:1, :] += jnp.sum(dcv * s2, axis=0, keepdims=True)
            dw_ref[1:2, :] += jnp.sum(dcv * s1, axis=0, keepdims=True)
            dw_ref[2:3, :] += jnp.sum(dcv * cu, axis=0, keepdims=True)
            dcu = dcv * w_ref[2:3, :] + _shift_up(dcv, 1) * w_ref[1:2, :] + _shift_up(dcv, 2) * w_ref[0:1, :]
            st[1] = dcu * u
            o_ref[...] = (dcu * c).astype(BF16)

        for k in range(3):
            @pl.when(w == k + 1)
            def _(k=k):
                o_ref[...] = st[k].astype(BF16)

    def ahead(j, b, w):
        flat = jnp.minimum(j * NB + b + jnp.where(w > 0, 1, 0), (CW // CT) * NB - 1)
        return flat // NB, flat % NB

    def seg(off):
        return pl.BlockSpec((None, S, CT), lambda j, b, w: (ahead(j, b, w)[1], 0, off // CT + ahead(j, b, w)[0]))

    out, dw = pl.pallas_call(
        body, name="conv_bwd", grid=(CW // CT, NB, 4),
        in_specs=[seg(AW), seg(UC), seg(BC), seg(CC), seg(ZC),
                  pl.BlockSpec((3, CT), lambda j, b, w: (0, ahead(j, b, w)[0])),
                  pl.BlockSpec(memory_space=pl.ANY)] + [pl.BlockSpec(memory_space=pl.ANY)] * len(deps),
        out_specs=[pl.BlockSpec((None, S, CT), lambda j, b, w: (b, 0, (UC + w * CW) // CT + j)),
                   pl.BlockSpec((3, CT), lambda j, b, w: (0, j))],
        out_shape=[jax.ShapeDtypeStruct((NB, S, NC), BF16), jax.ShapeDtypeStruct((3, CW), F32)],
        input_output_aliases={6: 0},
        scratch_shapes=[pltpu.VMEM((3, S, CT), F32)],
        compiler_params=_cp(("arbitrary",) * 3),
    )(dcat_v, proj_v, proj_v, proj_v, proj_v, conv_w, dproj_v, *deps)
    return out.reshape(T, NC), dw


XT = 1024


def _cross_fwd(proj, mkv, cosq, sinq, cosm, sinm, cat):
    proj_v = proj.reshape(NB, S, NC)
    mkv_v = mkv.reshape(NB, MLEN, 2 * XW)
    cat_v = cat.reshape(NB, S, D)

    def body(q_ref, z_ref, mk_ref, mv_ref, cq, sq, cm, sm, cat_in, o_ref):
        del cat_in
        mkr = _rope(mk_ref[...], cm[...], sm[...], XHD // 2).astype(BF16)
        qr = _rope(q_ref[...].astype(F32), cq[...], sq[...], XHD // 2).astype(BF16)
        sc = lax.dot_general(qr, mkr, NT, preferred_element_type=F32) * X_SCALE
        p = jnp.exp(sc - jnp.max(sc, axis=1, keepdims=True))
        p = p / jnp.sum(p, axis=1, keepdims=True)
        ox = jnp.dot(p.astype(BF16), mv_ref[...].astype(BF16), preferred_element_type=F32)
        z = z_ref[...].astype(F32)
        o_ref[...] = (ox * (z * _sigmoid(z))).astype(BF16)

    def seg(off):
        return pl.BlockSpec((None, XT, XHD), lambda b, h, t: (b, t, off // XHD + h))

    qtab = pl.BlockSpec((XT, XHD), lambda b, h, t: (t, 0))
    mtab = pl.BlockSpec((MLEN, XHD), lambda b, h, t: (0, 0))
    out = pl.pallas_call(
        body, name="cross_fwd", grid=(NB, NXH, S // XT),
        in_specs=[seg(QX), seg(ZX),
                  pl.BlockSpec((None, MLEN, XHD), lambda b, h, t: (b, 0, h)),
                  pl.BlockSpec((None, MLEN, XHD), lambda b, h, t: (b, 0, NXH + h)),
                  qtab, qtab, mtab, mtab, pl.BlockSpec(memory_space=pl.ANY)],
        out_specs=pl.BlockSpec((None, XT, XHD), lambda b, h, t: (b, t, (AW + CW) // XHD + h)),
        out_shape=jax.ShapeDtypeStruct((NB, S, D), BF16),
        input_output_aliases={8: 0},
        compiler_params=_cp(("parallel",) * 3),
    )(proj_v, proj_v, mkv_v, mkv_v, cosq, sinq, cosm, sinm, cat_v)
    return out.reshape(T, D)


def _cross_bwd(dcat, proj, mkv, cosq, sinq, cosm, sinm, dproj):
    proj_v = proj.reshape(NB, S, NC)
    dproj_v = dproj.reshape(NB, S, NC)
    dcat_v = dcat.reshape(NB, S, D)
    mkv_v = mkv.reshape(NB, MLEN, 2 * XW)
    nt = S // XT

    def body(dy_ref, q_ref, z_ref, mk_ref, mv_ref, cq, sq, cm, sm, dp_in, o_ref, dmk_ref, dmv_ref, dz_s):
        del dp_in
        t = pl.program_id(2)
        w = pl.program_id(3)

        @pl.when((t == 0) & (w == 0))
        def _():
            dmk_ref[...] = jnp.zeros_like(dmk_ref)
            dmv_ref[...] = jnp.zeros_like(dmv_ref)

        @pl.when(w == 0)
        def _():
            mkr = _rope(mk_ref[...], cm[...], sm[...], XHD // 2).astype(BF16)
            mv_b = mv_ref[...].astype(BF16)
            qr = _rope(q_ref[...].astype(F32), cq[...], sq[...], XHD // 2).astype(BF16)
            sc = lax.dot_general(qr, mkr, NT, preferred_element_type=F32) * X_SCALE
            p = jnp.exp(sc - jnp.max(sc, axis=1, keepdims=True))
            p = p / jnp.sum(p, axis=1, keepdims=True)
            p_b = p.astype(BF16)
            ox = jnp.dot(p_b, mv_b, preferred_element_type=F32)
            z = z_ref[...].astype(F32)
            dy = dy_ref[...]
            sg = _sigmoid(z)
            dz_s[...] = dy * ox * (sg * (1.0 + z * (1.0 - sg)))
            dox_b = (dy * (z * sg)).astype(BF16)
            dp = lax.dot_general(dox_b, mv_b, NT, preferred_element_type=F32)
            ds_b = (p * (dp - jnp.sum(dp * p, axis=1, keepdims=True)) * X_SCALE).astype(BF16)
            dq = jnp.dot(ds_b, mkr, preferred_element_type=F32)
            o_ref[...] = _rope_t(dq, cq[...], sq[...], XHD // 2).astype(BF16)
            dmk_ref[...] += lax.dot_general(ds_b, qr, TN, preferred_element_type=F32)
            dmv_ref[...] += lax.dot_general(p_b, dox_b, TN, preferred_element_type=F32)

        @pl.when(w == 1)
        def _():
            o_ref[...] = dz_s[...].astype(BF16)

        @pl.when((t == nt - 1) & (w == 1))
        def _():
            dmk_ref[...] = _rope_t(dmk_ref[...], cm[...], sm[...], XHD // 2)

    def seg(off):
        return pl.BlockSpec((None, XT, XHD), lambda b, h, t, w: (b, t, off // XHD + h))

    qtab = pl.BlockSpec((XT, XHD), lambda b, h, t, w: (t, 0))
    mtab = pl.BlockSpec((MLEN, XHD), lambda b, h, t, w: (0, 0))
    macc = pl.BlockSpec((None, MLEN, XHD), lambda b, h, t, w: (b, 0, h))
    out, dmk, dmv = pl.pallas_call(
        body, name="cross_bwd", grid=(NB, NXH, nt, 2),
        in_specs=[pl.BlockSpec((None, XT, XHD), lambda b, h, t, w: (b, t, (AW + CW) // XHD + h)),
                  seg(QX), seg(ZX),
                  pl.BlockSpec((None, MLEN, XHD), lambda b, h, t, w: (b, 0, h)),
                  pl.BlockSpec((None, MLEN, XHD), lambda b, h, t, w: (b, 0, NXH + h)),
                  qtab, qtab, mtab, mtab, pl.BlockSpec(memory_space=pl.ANY)],
        out_specs=[pl.BlockSpec((None, XT, XHD), lambda b, h, t, w: (b, t, (QX + w * XW) // XHD + h)), macc, macc],
        out_shape=[jax.ShapeDtypeStruct((NB, S, NC), BF16), jax.ShapeDtypeStruct((NB, MLEN, XW), F32),
                   jax.ShapeDtypeStruct((NB, MLEN, XW), F32)],
        input_output_aliases={9: 0},
        scratch_shapes=[pltpu.VMEM((XT, XHD), F32)],
        compiler_params=_cp(("arbitrary",) * 4),
    )(dcat_v, proj_v, proj_v, mkv_v, mkv_v, cosq, sinq, cosm, sinm, dproj_v)
    return out.reshape(T, NC), dmk, dmv


def _local_step(x, mem, pre_norm, conv_w, mem_norm, post_norm, tgt, chip, core, comm, moments=None):
    x2 = x.reshape(T, D)
    mem2 = mem.reshape(NB * MLEN, D)
    tgt2 = tgt.reshape(T, D)
    cosa, sina = _rope_tables(jnp.arange(S), HD // 2)
    cosq, sinq = _rope_tables(jnp.arange(S) + MLEN, XHD // 2)
    cosm, sinm = _rope_tables(jnp.arange(MLEN), XHD // 2)

    h = _rms_fwd(x2, pre_norm, "pre_norm_fwd", dep=comm.gather_started())
    memn = _rms_fwd(mem2, mem_norm, "mem_norm_fwd")
    proj = _proj_part(h, comm.w_in_own(), chip, (0,), "proj_own")
    wg_in, tok = comm.w_in_near(after=[proj, memn, conv_w])
    proj = _proj_part(h, wg_in, chip, (REL_XOR[0], REL_XOR[1]), "proj_near", prev=proj, dep=tok)
    wg_in, tok = comm.w_in_all(after=proj)
    proj = _proj_part(h, wg_in, chip, (REL_XOR[2],), "proj_far", prev=proj, dep=tok)
    fw = [_attn_fwd(proj, cosa, sina, g) for g in range(3)]
    os_ = [f[0] for f in fw]
    ls_ = [f[1] for f in fw]
    tok = comm.w_rest_landed(after=ls_)
    cat = _attn_mix(proj, os_, ls_, dep=tok)
    cat = _conv_fwd(proj, conv_w, cat)
    wg_kv, wg_out = comm.w_rest(after=[cat])
    mkv = _mm_nn(memn, wg_kv, "mkv", NB * MLEN, 1024, 512)
    cat = _cross_fwd(proj, mkv, cosq, sinq, cosm, sinm, cat)
    y = _mm_nn(cat, wg_out, "out_proj", 1024, 2048, 2048)
    dy, dout, d_post, loss = _post_norm_loss(y, x2, tgt2, post_norm)

    dcat = _mm_nt(dy, wg_out, "dcat", 1024, 2048, 2048)
    g_send = _grad_rows(cat, dy,1 - core, BF16, "grad_w_out_send")
    sent = comm.sibling_start(g_send.reshape(NCHIP, D // NCHIP // 2, D), "a")
    g_keep = _grad_rows(cat, dy,core, BF16, "grad_w_out_keep", dep=sent[-1])
    red_a = comm.reduce_start_summed(g_keep.reshape(NCHIP, D // NCHIP // 2, D), sent, "a")
    da, lse, delta, dproj = _attn_mix_bwd(dcat, proj, os_, ls_, dep=red_a[-1])
    for g in range(3):
        dproj = _attn_bwd(proj, fw[g][2], fw[g][3], cosa, sina, da, lse, delta, dproj, g)
    dproj, dmk, dmv = _cross_bwd(dcat, proj, mkv, cosq, sinq, cosm, sinm, dproj)
    dmkv = jnp.concatenate([dmk, dmv], axis=-1).reshape(NB * MLEN, 2 * XW)
    dmkv_b = dmkv.astype(BF16)
    gk_send = _grad_rows(memn, dmkv_b, 1 - core, BF16, "grad_w_mem_kv_send")
    sent = comm.sibling_start(gk_send.reshape(NCHIP, D // NCHIP // 2, 2 * XW), "k")
    gk_keep = _grad_rows(memn, dmkv_b, core, BF16, "grad_w_mem_kv_keep", dep=sent[-1])
    dmemn = _mm_nt(dmkv_b, wg_kv, "dmemn", NB * MLEN, 1024, 512)
    d_mem = _norm_gain_grad(dmemn, mem2, "mem_norm_bwd")
    red_k = comm.reduce_start_summed(gk_keep.reshape(NCHIP, D // NCHIP // 2, 2 * XW), sent, "k", after=d_mem)
    dproj, d_conv = _conv_bwd(dcat, proj, conv_w, dproj, dep=red_k[-1])
    (r_out,) = comm.reduce_finish(red_a, "a", after=d_conv)
    (r_kv,) = comm.reduce_finish(red_k, "k", after=d_conv)

    g_send = _grad_w_in(h, dproj, 1 - core, BF16, "grad_w_in_send", dep=r_kv)
    sent = comm.sibling_start(g_send.reshape(NCHIP, D // 2, WB), "b")
    q = _grad_w_in(h, dproj, core, BF16, "grad_w_in_keep_0", dep=sent[-1], tiles=(0, 1))
    got = comm.sibling_wait(sent, [q], "b")
    q = _pair_sum_rows(q.reshape(NCHIP, D // 2, WB), got, "pair_sum_b", rows=(0, GW_TM))
    red_b0 = comm.scatter_rows(q, "b0", (0, GW_TM))
    q = _grad_w_in(h, dproj, core, BF16, "grad_w_in_keep_1", dep=red_b0[4], tiles=(1, 1),
                   addend=got.reshape(NCHIP * D // 2, WB), prev=red_b0[2][0].reshape(NCHIP * D // 2, WB))
    red_b1 = comm.scatter_rows(q.reshape(NCHIP, D // 2, WB), "b1", (GW_TM, GW_TM), before=red_b0)
    sets = [] if moments is None else [(moments["w_out"][0], r_out, *moments["w_out"][1:]),
                                      (moments["w_mem_kv"][0], r_kv, *moments["w_mem_kv"][1:])]
    dh, updates = _dh(dproj, wg_in, dep=red_b1[4], adam=sets)
    grad_x, d_pre = _pre_norm_bwd(dh, x2, pre_norm, dout, 0)
    r_in = comm.reduce_finish_start([red_b0, red_b1], "b", after=grad_x)
    grad_x, d_pre = _pre_norm_bwd(dh, x2, pre_norm, dout, 1, prev=(grad_x, d_pre), dep=r_in[-1])
    return loss, grad_x.reshape(NB, S, D), d_pre, d_mem, d_post, d_conv, r_in, (r_kv, r_out), updates


def _adamw(w, g, m, v, name):
    rows, cols = w.shape
    tr = rows if rows <= 512 else 512
    tc = cols if cols <= 1024 else 1024
    if cols % tc:
        tc = 896

    blk = pl.BlockSpec((tr, tc), lambda i, j: (i, j))
    sds = jax.ShapeDtypeStruct((rows, cols), F32)
    def body(*refs):
        _adamw_math(*refs)

    return pl.pallas_call(
        body, name=name, grid=(rows // tr, cols // tc),
        in_specs=[blk] * 4, out_specs=[blk] * 4, out_shape=[sds] * 4,
        compiler_params=_cp(("parallel", "parallel")),
    )(w, g, m, v)


def _place():
    return lax.axis_index("x"), lax.axis_index("y"), lax.axis_index("c")


def _other_chips(x, y):
    return [(1 - x, y), (x, 1 - y), (1 - x, 1 - y)]


def _tile_cols(cols):
    return cols if cols <= 1024 else (1024 if cols % 1024 == 0 else 896)


def _cast_own(w, chip, name, half=None, prev=None):
    rows, cols = w.shape
    tr, tc = 512, _tile_cols(cols)
    nrt = rows // tr if half is None else rows // tr // 2
    index = jnp.stack([chip, 0 if half is None else half]).astype(jnp.int32)
    extra = [] if prev is None else [prev]

    def body(ix_ref, w_ref, *rest):
        rest[-1][...] = w_ref[...].astype(BF16)

    grid_spec = pltpu.PrefetchScalarGridSpec(
        num_scalar_prefetch=1, grid=(nrt, cols // tc),
        in_specs=[pl.BlockSpec((tr, tc), lambda i, j, ix: (ix[1] * nrt + i, j))] + [ANY_SPEC] * len(extra),
        out_specs=pl.BlockSpec((None, tr, tc), lambda i, j, ix: (ix[0], ix[1] * nrt + i, j)))
    return pl.pallas_call(
        body, name=name, grid_spec=grid_spec,
        out_shape=jax.ShapeDtypeStruct((NCHIP, rows, cols), BF16),
        input_output_aliases={} if prev is None else {2: 0},
        compiler_params=_cp(("parallel", "parallel")),
    )(index, w, *extra)


HBM_SPEC = pl.BlockSpec(memory_space=pltpu.HBM)
SEM_SPEC = pl.BlockSpec(memory_space=pltpu.SEMAPHORE)
ANY_SPEC = pl.BlockSpec(memory_space=pl.ANY)
EFFECT = pltpu.SideEffectType.DATAFLOW_SIDE_EFFECTING
TOKEN = jax.ShapeDtypeStruct((8, 128), F32)


def _half(ref, chip, hc):
    hr = ref.shape[1] // 2
    return ref.at[chip, pl.ds(hc * hr, hr), :]


NEAR = (0, 1)
FAR = (2,)
REL_XOR = (2, 1, 3)


def _gather_copies(refs, send_sems, recv_sems, rels):
    x, y, c = _place()
    chips = _other_chips(x, y)
    out, inc = [], []
    for a, ref in enumerate(refs):
        for p, j in enumerate(rels):
            px, py = chips[j]
            mine = _half(ref, 2 * x + y, c)
            theirs = _half(ref, 2 * px + py, c)
            sems = dict(send_sem=send_sems.at[len(rels) * a + p], recv_sem=recv_sems.at[len(rels) * a + p],
                        device_id=(px, py, c), device_id_type=MESH)
            out.append(pltpu.make_async_remote_copy(src_ref=mine, dst_ref=mine, **sems))
            inc.append(pltpu.make_async_remote_copy(src_ref=theirs, dst_ref=theirs, **sems))
    return out, inc


def _gather_start(bufs, groups, name):
    n = len(bufs)
    ng = len(groups)

    def body(*refs):
        ins = refs[:n]
        token = refs[-1]
        for gi, rels in enumerate(groups):
            out, _ = _gather_copies(ins, refs[n + 2 * gi], refs[n + 2 * gi + 1], rels)
            for cp in out:
                cp.start()
        token[...] = jnp.zeros_like(token)

    sems = []
    for rels in groups:
        sems += [pltpu.SemaphoreType.DMA((len(rels) * n,))] * 2
    res = pl.pallas_call(
        body, name=name,
        in_specs=[HBM_SPEC] * n,
        out_specs=[SEM_SPEC] * (2 * ng) + [HBM_SPEC] * n + [pl.BlockSpec(memory_space=pltpu.VMEM)],
        out_shape=sems + [pltpu.HBM(b.shape, b.dtype) for b in bufs] + [TOKEN],
        input_output_aliases={a: 2 * ng + a for a in range(n)},
        compiler_params=pltpu.CompilerParams(has_side_effects=EFFECT),
    )(*[pltpu.with_memory_space_constraint(b, pltpu.HBM) for b in bufs])
    return [(res[2 * gi], res[2 * gi + 1]) for gi in range(ng)], list(res[2 * ng:2 * ng + n]), res[-1]


def _gather_wait(bufs, sems, rels, after, name):
    n = len(bufs)
    send_sems, recv_sems = sems
    after = list(after) if isinstance(after, (list, tuple)) else [after]

    def body(*refs):
        ins = refs[:n]
        out, inc = _gather_copies(ins, refs[n], refs[n + 1], rels)
        for cp in out:
            cp.wait_send()
        for cp in inc:
            cp.wait_recv()

    return pl.pallas_call(
        body, name=name,
        in_specs=[HBM_SPEC] * n + [SEM_SPEC, SEM_SPEC] + [ANY_SPEC] * len(after),
        out_specs=[HBM_SPEC] * n,
        out_shape=[pltpu.HBM(b.shape, b.dtype) for b in bufs],
        input_output_aliases={a: a for a in range(n)},
        compiler_params=pltpu.CompilerParams(has_side_effects=EFFECT),
    )(*bufs, send_sems, recv_sems, *after)


def _forward_halves(bufs, rels, name):
    n = len(bufs)

    def body(*refs):
        cps, waits = _forward_copies(refs[n:2 * n], rels, refs[2 * n], refs[2 * n + 1])
        for cp in cps:
            cp.start()
        for cp in waits:
            cp.wait_recv()
        for cp in cps:
            cp.wait_send()

    return pl.pallas_call(
        body, name=name,
        in_specs=[ANY_SPEC] * n, out_specs=[ANY_SPEC] * n,
        out_shape=[jax.ShapeDtypeStruct(s.shape, s.dtype) for s in bufs],
        input_output_aliases={a: a for a in range(n)},
        scratch_shapes=[pltpu.SemaphoreType.DMA((len(rels) * n,)), pltpu.SemaphoreType.DMA((len(rels) * n,))],
    )(*bufs)


def _forward_copies(refs, rels, send_sems, recv_sems):
    x, y, c = _place()
    chips = _other_chips(x, y)
    cps, waits = [], []
    for a, ref in enumerate(refs):
        for p, j in enumerate(rels):
            px, py = chips[j]
            sems = dict(send_sem=send_sems.at[len(rels) * a + p], recv_sem=recv_sems.at[len(rels) * a + p],
                        device_id=(x, y, 1 - c), device_id_type=MESH)
            got = _half(ref, 2 * px + py, c)
            want = _half(ref, 2 * px + py, 1 - c)
            cps.append(pltpu.make_async_remote_copy(src_ref=got, dst_ref=got, **sems))
            waits.append(pltpu.make_async_remote_copy(src_ref=want, dst_ref=want, **sems))
    return cps, waits


def _forward_start(bufs, rels, name):
    n = len(bufs)

    def body(*refs):
        cps, _ = _forward_copies(refs[:n], rels, refs[n], refs[n + 1])
        for cp in cps:
            cp.start()
        refs[-1][...] = jnp.zeros_like(refs[-1])

    res = pl.pallas_call(
        body, name=name,
        in_specs=[HBM_SPEC] * n,
        out_specs=[SEM_SPEC, SEM_SPEC] + [HBM_SPEC] * n + [pl.BlockSpec(memory_space=pltpu.VMEM)],
        out_shape=[pltpu.SemaphoreType.DMA((len(rels) * n,))] * 2 + [pltpu.HBM(b.shape, b.dtype) for b in bufs] + [TOKEN],
        input_output_aliases={a: 2 + a for a in range(n)},
        compiler_params=pltpu.CompilerParams(has_side_effects=EFFECT),
    )(*[pltpu.with_memory_space_constraint(b, pltpu.HBM) for b in bufs])
    return (res[0], res[1]), list(res[2:2 + n]), res[-1]


def _forward_wait(bufs, sems, rels, after, name):
    n = len(bufs)

    def body(*refs):
        cps, waits = _forward_copies(refs[:n], rels, refs[n], refs[n + 1])
        for cp in cps:
            cp.wait_send()
        for cp in waits:
            cp.wait_recv()

    return pl.pallas_call(
        body, name=name,
        in_specs=[HBM_SPEC] * n + [SEM_SPEC, SEM_SPEC] + [ANY_SPEC] * len(after),
        out_specs=[HBM_SPEC] * n,
        out_shape=[pltpu.HBM(b.shape, b.dtype) for b in bufs],
        input_output_aliases={a: a for a in range(n)},
        compiler_params=pltpu.CompilerParams(has_side_effects=EFFECT),
    )(*bufs, sems[0], sems[1], *after)


def _sibling_copy(src, land, send_sems, recv_sems):
    x, y, c = _place()
    return pltpu.make_async_remote_copy(src_ref=src, dst_ref=land, send_sem=send_sems.at[0], recv_sem=recv_sems.at[0],
                                        device_id=(x, y, 1 - c), device_id_type=MESH)


def _sibling_start(part, name):
    def body(src, land, send_sems, recv_sems, src_thru, land_thru, token):
        _sibling_copy(src, land, send_sems, recv_sems).start()
        token[...] = jnp.zeros_like(token)

    land = lax.empty(part.shape, part.dtype)
    return pl.pallas_call(
        body, name=name,
        in_specs=[HBM_SPEC] * 2,
        out_specs=[SEM_SPEC, SEM_SPEC, HBM_SPEC, HBM_SPEC, pl.BlockSpec(memory_space=pltpu.VMEM)],
        out_shape=[pltpu.SemaphoreType.DMA((1,)), pltpu.SemaphoreType.DMA((1,)), pltpu.HBM(part.shape, part.dtype),
                   pltpu.HBM(part.shape, part.dtype), TOKEN],
        input_output_aliases={0: 2, 1: 3},
        compiler_params=pltpu.CompilerParams(has_side_effects=EFFECT),
    )(pltpu.with_memory_space_constraint(part, pltpu.HBM), pltpu.with_memory_space_constraint(land, pltpu.HBM))


def _sibling_wait(state, after, name):
    send_sems, recv_sems, part, land, _ = state

    def body(src, land_ref, send_ref, recv_ref, *rest):
        cp = _sibling_copy(src, land_ref, send_ref, recv_ref)
        cp.wait_send()
        cp.wait_recv()

    return pl.pallas_call(
        body, name=name,
        in_specs=[HBM_SPEC, HBM_SPEC, SEM_SPEC, SEM_SPEC] + [ANY_SPEC] * len(after),
        out_specs=[HBM_SPEC, HBM_SPEC],
        out_shape=[pltpu.HBM(part.shape, part.dtype), pltpu.HBM(land.shape, land.dtype)],
        input_output_aliases={0: 0, 1: 1},
        compiler_params=pltpu.CompilerParams(has_side_effects=EFFECT),
    )(part, land, send_sems, recv_sems, *after)[1]


def _pair_sum_rows(keep, got, name, rows=None):
    nblk, nrows, cols = keep.shape
    tr, tc = 512, _tile_cols(cols)
    r0, nr = (0, nrows) if rows is None else rows

    def body(k_ref, g_ref, o_ref):
        o_ref[...] = (k_ref[...].astype(F32) + g_ref[...].astype(F32)).astype(BF16)

    blk = pl.BlockSpec((None, tr, tc), lambda b, i, j: (b, r0 // tr + i, j))
    return pl.pallas_call(
        body, name=name, grid=(nblk, nr // tr, cols // tc),
        in_specs=[blk, blk], out_specs=blk,
        out_shape=jax.ShapeDtypeStruct(keep.shape, BF16),
        input_output_aliases={} if rows is None else {0: 0},
        compiler_params=_cp(("parallel",) * 3),
    )(keep, got)


def _scatter_start(qs, name, rows=None, lands=None):
    n = len(qs)

    def body(*refs):
        ins, lands = refs[:n], refs[n:2 * n]
        token = refs[-1]
        for cp in _scatter_copies(ins, lands, refs[2 * n], refs[2 * n + 1], rows):
            cp.start()
        token[...] = jnp.zeros_like(token)

    if lands is None:
        lands = [lax.empty((3,) + q.shape[1:], q.dtype) for q in qs]
    res = pl.pallas_call(
        body, name=name,
        in_specs=[HBM_SPEC] * (2 * n),
        out_specs=[SEM_SPEC, SEM_SPEC] + [HBM_SPEC] * (2 * n) + [pl.BlockSpec(memory_space=pltpu.VMEM)],
        out_shape=[pltpu.SemaphoreType.DMA((3 * n,)), pltpu.SemaphoreType.DMA((3 * n,))]
        + [pltpu.HBM(b.shape, b.dtype) for b in qs + lands] + [TOKEN],
        input_output_aliases={a: 2 + a for a in range(2 * n)},
        compiler_params=pltpu.CompilerParams(has_side_effects=EFFECT),
    )(*[pltpu.with_memory_space_constraint(b, pltpu.HBM) for b in qs + lands])
    return res[0], res[1], list(res[2:2 + n]), list(res[2 + n:2 + 2 * n]), res[-1]


def _scatter_copies(ins, lands, send_sems, recv_sems, rows=None):
    x, y, c = _place()
    sel = slice(None) if rows is None else pl.ds(rows[0], rows[1])
    cps = []
    for a in range(len(ins)):
        for j, (px, py) in enumerate(_other_chips(x, y)):
            cps.append(pltpu.make_async_remote_copy(
                src_ref=ins[a].at[2 * px + py, sel, :], dst_ref=lands[a].at[j, sel, :],
                send_sem=send_sems.at[3 * a + j], recv_sem=recv_sems.at[3 * a + j], device_id=(px, py, c), device_id_type=MESH))
    return cps


def _scatter_wait(qs, lands, send_sems, recv_sems, after, name, rows=None):
    n = len(qs)

    def body(*refs):
        for cp in _scatter_copies(refs[:n], refs[n:2 * n], refs[2 * n], refs[2 * n + 1], rows):
            cp.wait_send()
            cp.wait_recv()

    res = pl.pallas_call(
        body, name=name,
        in_specs=[HBM_SPEC] * (2 * n) + [SEM_SPEC, SEM_SPEC, ANY_SPEC],
        out_specs=[HBM_SPEC] * (2 * n),
        out_shape=[pltpu.HBM(b.shape, b.dtype) for b in qs + lands],
        input_output_aliases={a: a for a in range(2 * n)},
        compiler_params=pltpu.CompilerParams(has_side_effects=EFFECT),
    )(*qs, *lands, send_sems, recv_sems, after)
    return list(res[:n]), list(res[n:])


def _chip_sum(q, got, name):
    _, hr, cols = got.shape
    tr, tc = 512, _tile_cols(cols)
    chip = 2 * lax.axis_index("x") + lax.axis_index("y")
    c = lax.axis_index("c")

    def body(idx_ref, q_ref, g_ref, o_ref):
        del idx_ref
        acc = q_ref[...].astype(F32)
        for i in range(3):
            acc = acc + g_ref[i].astype(F32)
        o_ref[...] = acc

    grid_spec = pltpu.PrefetchScalarGridSpec(
        num_scalar_prefetch=1, grid=(hr // tr, cols // tc),
        in_specs=[pl.BlockSpec((None, tr, tc), lambda i, j, ix: (ix[0], i, j)),
                  pl.BlockSpec((3, tr, tc), lambda i, j, ix: (0, i, j))],
        out_specs=pl.BlockSpec((None, tr, tc), lambda i, j, ix: (ix[1], i, j)))
    return pl.pallas_call(
        body, name=name, grid_spec=grid_spec,
        out_shape=jax.ShapeDtypeStruct((2, hr, cols), F32),
        compiler_params=_cp(("parallel", "parallel")),
    )(jnp.stack([chip, c]).astype(jnp.int32), q, got)


def _join_halves(bufs, name):
    n = len(bufs)

    def body(*refs):
        outs = refs[n:2 * n]
        send_sems, recv_sems = refs[2 * n:]
        x, y, c = _place()
        cps = []
        for a in range(n):
            cps.append(pltpu.make_async_remote_copy(src_ref=outs[a].at[c], dst_ref=outs[a].at[c], send_sem=send_sems.at[a],
                                                    recv_sem=recv_sems.at[a], device_id=(x, y, 1 - c), device_id_type=MESH))
        for cp in cps:
            cp.start()
        for a in range(n):
            theirs = outs[a].at[1 - c]
            pltpu.make_async_remote_copy(src_ref=theirs, dst_ref=theirs, send_sem=send_sems.at[a], recv_sem=recv_sems.at[a],
                                         device_id=(x, y, 1 - c), device_id_type=MESH).wait_recv()
        for cp in cps:
            cp.wait_send()

    hbm = pl.BlockSpec(memory_space=pl.ANY)
    return pl.pallas_call(
        body, name=name,
        in_specs=[hbm] * n, out_specs=[hbm] * n,
        out_shape=[jax.ShapeDtypeStruct(b.shape, b.dtype) for b in bufs],
        input_output_aliases={a: a for a in range(n)},
        scratch_shapes=[pltpu.SemaphoreType.DMA((n,)), pltpu.SemaphoreType.DMA((n,))],
    )(*bufs)


def _join_copy(buf, send_sems, recv_sems):
    x, y, c = _place()
    sems = dict(send_sem=send_sems.at[0], recv_sem=recv_sems.at[0], device_id=(x, y, 1 - c), device_id_type=MESH)
    return (pltpu.make_async_remote_copy(src_ref=buf.at[c], dst_ref=buf.at[c], **sems),
            pltpu.make_async_remote_copy(src_ref=buf.at[1 - c], dst_ref=buf.at[1 - c], **sems))


def _join_start(buf, name):
    def body(b_ref, send_sems, recv_sems, thru, token):
        _join_copy(b_ref, send_sems, recv_sems)[0].start()
        token[...] = jnp.zeros_like(token)

    return pl.pallas_call(
        body, name=name,
        in_specs=[HBM_SPEC],
        out_specs=[SEM_SPEC, SEM_SPEC, HBM_SPEC, pl.BlockSpec(memory_space=pltpu.VMEM)],
        out_shape=[pltpu.SemaphoreType.DMA((1,)), pltpu.SemaphoreType.DMA((1,)), pltpu.HBM(buf.shape, buf.dtype), TOKEN],
        input_output_aliases={0: 2},
        compiler_params=pltpu.CompilerParams(has_side_effects=EFFECT),
    )(pltpu.with_memory_space_constraint(buf, pltpu.HBM))


def _join_wait(state, after, name):
    send_sems, recv_sems, buf, _ = state

    def body(b_ref, send_ref, recv_ref, *rest):
        out, inc = _join_copy(b_ref, send_ref, recv_ref)
        out.wait_send()
        inc.wait_recv()

    return pl.pallas_call(
        body, name=name,
        in_specs=[HBM_SPEC, SEM_SPEC, SEM_SPEC] + [ANY_SPEC] * len(after),
        out_specs=HBM_SPEC,
        out_shape=pltpu.HBM(buf.shape, buf.dtype),
        input_output_aliases={0: 0},
        compiler_params=pltpu.CompilerParams(has_side_effects=EFFECT),
    )(buf, send_sems, recv_sems, *after)


N_DEV = 8


def _gather_small(block, reduce, name, deps=()):
    m_per, cols = block.shape
    deps = list(deps)

    def body(x_ref, *rest):
        out_ref, all_ref, send_sems, recv_sems, local_sem = rest[len(deps):]
        x, y, c = _place()
        me, sibling = (x, y, c), (x, y, 1 - c)
        chips = _other_chips(x, y)

        def rows(px, py, pc):
            return all_ref.at[pl.ds((4 * px + 2 * py + pc) * m_per, m_per), :]

        def copy(k, block_of, to, src=None):
            return pltpu.make_async_remote_copy(
                src_ref=rows(*block_of) if src is None else src, dst_ref=rows(*block_of),
                send_sem=send_sems.at[k], recv_sem=recv_sems.at[k], device_id=to, device_id_type=MESH)

        mine = pltpu.make_async_copy(x_ref, rows(*me), local_sem)
        mine.start()
        first = [copy(0, me, sibling, src=x_ref)]
        first += [copy(1 + j, me, (*chip, c), src=x_ref) for j, chip in enumerate(chips)]
        for cp in first:
            cp.start()
        passed = [copy(4 + j, (*chip, c), sibling) for j, chip in enumerate(chips)]
        for j, chip in enumerate(chips):
            copy(1 + j, (*chip, c), me).wait_recv()
            passed[j].start()
        copy(0, sibling, me).wait_recv()
        for j, chip in enumerate(chips):
            copy(4 + j, (*chip, 1 - c), me).wait_recv()
        for cp in first + passed:
            cp.wait_send()
        mine.wait()
        if reduce:
            acc = all_ref[pl.ds(0, m_per), :]
            for i in range(1, N_DEV):
                acc = acc + all_ref[pl.ds(i * m_per, m_per), :]
            out_ref[...] = acc
        else:
            out_ref[...] = all_ref[...]

    out_rows = m_per if reduce else N_DEV * m_per
    return pl.pallas_call(
        body, name=name,
        in_specs=[pl.BlockSpec(memory_space=pltpu.VMEM)] + [pl.BlockSpec(memory_space=pl.ANY)] * len(deps),
        out_specs=pl.BlockSpec(memory_space=pltpu.VMEM),
        out_shape=jax.ShapeDtypeStruct((out_rows, cols), F32),
        scratch_shapes=[pltpu.VMEM((N_DEV * m_per, cols), F32), pltpu.SemaphoreType.DMA((7,)),
                        pltpu.SemaphoreType.DMA((7,)), pltpu.SemaphoreType.DMA],
    )(block, *deps)


class _Comm:
    def __init__(self, w_in, w_kv, w_out, chip, core):
        self.w_in, self.chip, self.core = w_in, chip, core
        self.bufs = [_cast_own(w_in, chip, "cast_w_in_sent", half=core), _cast_own(w_kv, chip, "cast_w_mem_kv"),
                     _cast_own(w_out, chip, "cast_w_out")]


    def gather_started(self):
        (self.sems,), (b_in,), tok = _gather_start(self.bufs[:1], (NEAR,), "gather_start_in_near")
        self.b_in = _cast_own(self.w_in, self.chip, "cast_w_in_kept", half=1 - self.core, prev=b_in)
        return [tok]

    def w_in_own(self):
        return self.b_in.reshape(NCHIP * D, WB)

    def w_in_near(self, after):
        (b_in,) = _gather_wait([self.b_in], self.sems, NEAR, list(after) + self.bufs[1:], "gather_wait_in_near")
        (b_in,) = _forward_halves([b_in], NEAR, "forward_in_near")
        (self.sems,), (self.b_in,), tok = _gather_start([b_in], (FAR,), "gather_start_in_far")
        return self.b_in.reshape(NCHIP * D, WB), tok

    def w_in_all(self, after):
        (b_in,) = _gather_wait([self.b_in], self.sems, FAR, after, "gather_wait_in_far")
        (b_in,) = _forward_halves([b_in], FAR, "forward_in_far")
        (self.sems,), self.b_rest, tok = _gather_start(self.bufs[1:], (NEAR + FAR,), "gather_start_rest")
        return b_in.reshape(NCHIP * D, WB), tok

    def w_rest_landed(self, after):
        b_rest = _gather_wait(self.b_rest, self.sems, NEAR + FAR, after, "gather_wait_rest")
        self.sems, self.b_rest, tok = _forward_start(b_rest, NEAR + FAR, "forward_start_rest")
        return tok

    def w_rest(self, after):
        b_kv, b_out = _forward_wait(self.b_rest, self.sems, NEAR + FAR, after, "forward_wait_rest")
        return b_kv.reshape(D, 2 * XW), b_out.reshape(D, D)

    def sibling_start(self, send, tag):
        return _sibling_start(send, "sibling_start_" + tag)

    def reduce_start_summed(self, keep, sent, tag, after=None):
        got = _sibling_wait(sent, [keep] + ([] if after is None else [after]), "sibling_wait_" + tag)
        return _scatter_start([_pair_sum_rows(keep, got, "pair_sum_" + tag)], "scatter_start_" + tag)

    def sibling_wait(self, sent, after, tag):
        return _sibling_wait(sent, after, "sibling_wait_" + tag)

    def scatter_rows(self, q, tag, rows, before=None):
        lands = None if before is None else before[3]
        return _scatter_start([q], "scatter_start_" + tag, rows=rows, lands=lands) + (rows,)

    def reduce_finish_start(self, states, tag, after):
        qs, lands = states[-1][2], states[-1][3]
        for i, st in enumerate(states):
            qs, lands = _scatter_wait(qs, lands, st[0], st[1], after, f"scatter_wait_{tag}{i}", rows=st[5])
        return _join_start(_chip_sum(qs[0], lands[0], f"chip_sum_{tag}0"), "join_start_" + tag)

    def reduce_finish_wait(self, pending, tag, after):
        j = _join_wait(pending, after, "join_wait_" + tag)
        return j.reshape(2 * j.shape[1], j.shape[2])

    def reduce_finish(self, state, tag, after):
        send_sems, recv_sems, qs, lands, _ = state
        qs, lands = _scatter_wait(qs, lands, send_sems, recv_sems, after, "scatter_wait_" + tag)
        halves = [_chip_sum(q, l, f"chip_sum_{tag}{i}") for i, (q, l) in enumerate(zip(qs, lands))]
        return [j.reshape(2 * j.shape[1], j.shape[2]) for j in _join_halves(halves, "join_halves_" + tag)]


def kernel(x, mem, pre_norm, w_in, conv_w, mem_norm, w_mem_kv, w_out, post_norm, loss_target, m_pre_norm, m_w_in, m_conv_w, m_mem_norm, m_w_mem_kv, m_w_out, m_post_norm, v_pre_norm, v_w_in, v_conv_w, v_mem_norm, v_w_mem_kv, v_w_out, v_post_norm):
    chip = 2 * lax.axis_index("x") + lax.axis_index("y")

    comm = _Comm(w_in[0], w_mem_kv[0], w_out[0], chip, lax.axis_index("c"))
    cw_blk = jnp.zeros((8, 384), F32).at[:3].set(conv_w[0])
    cw_all = _gather_small(cw_blk, False, "gather_conv_w", deps=comm.bufs[1:]).reshape(NCHIP, 2, 8, 384)[:, 0, :3]
    conv_full = jnp.transpose(cw_all, (1, 0, 2)).reshape(3, CW)
    moments = {"w_out": (w_out[0], m_w_out[0], v_w_out[0]), "w_mem_kv": (w_mem_kv[0], m_w_mem_kv[0], v_w_mem_kv[0])}
    loss, grad_x, d_pre, d_mem, d_post, d_conv, r_in, _, (upd_out, upd_kv) = _local_step(
        x, mem, pre_norm, conv_full, mem_norm, post_norm, loss_target, chip, lax.axis_index("c"), comm, moments)

    pack = jnp.concatenate([d_pre, d_mem, d_post, jnp.pad(d_conv, ((0, 0), (0, D - CW))),
                            jnp.pad(loss, ((0, 0), (0, D - 128))), jnp.zeros((1, D), F32)], axis=0)
    tot = _gather_small(pack, True, "reduce_small")
    g_pre, g_mem, g_post = tot[0:1], tot[1:2], tot[2:3]
    g_conv = lax.dynamic_slice(tot[3:6, :CW], (0, chip * 384), (3, 384))
    loss_out = tot[6, 0]

    names = ("pre_norm", "w_in", "conv_w", "mem_norm", "w_mem_kv", "w_out", "post_norm")
    ws = (pre_norm, w_in[0], conv_w[0], mem_norm, w_mem_kv[0], w_out[0], post_norm)
    gs = [g_pre, None, g_conv, g_mem, None, None, g_post]
    ms = (m_pre_norm, m_w_in[0], m_conv_w[0], m_mem_norm, m_w_mem_kv[0], m_w_out[0], m_post_norm)
    vs = (v_pre_norm, v_w_in[0], v_conv_w[0], v_mem_norm, v_w_mem_kv[0], v_w_out[0], v_post_norm)
    upd = [None if g is None else _adamw(w, g, m, v, "adamw_" + nm) for nm, w, g, m, v in zip(names, ws, gs, ms, vs)]
    upd[4], upd[5] = upd_kv, upd_out
    g_in = comm.reduce_finish_wait(r_in, "b", after=[u[1] for u in upd if u is not None])
    upd[1] = _adamw(ws[1], g_in, ms[1], vs[1], "adamw_w_in")

    def shaped(arrs):
        return [a.reshape(w.shape) if w.ndim == a.ndim else a.reshape((1,) + a.shape)
                for a, w in zip(arrs, (pre_norm, w_in, conv_w, mem_norm, w_mem_kv, w_out, post_norm))]

    grads = shaped([u[0] for u in upd])
    deltas = shaped([u[1] for u in upd])
    new_m = shaped([u[2] for u in upd])
    new_v = shaped([u[3] for u in upd])
    return (loss_out, grad_x, *grads, *deltas, *new_m, *new_v)
```

```python
import jax
import jax.numpy as jnp
from jax import lax
from jax.experimental import pallas as pl
from jax.experimental.pallas import tpu as pltpu

F32 = jnp.float32
BF16 = jnp.bfloat16

D = 4096
S = 2048
NB = 2
T = NB * S
MLEN = 256
HD = 128
AW = 1536
CW = 1536
XW = 1024
XHD = 256
NXH = 4
NC = 14336
QA, KA, VA, ZA, UC, BC, CC, ZC, QX, ZX = 0, 1536, 3072, 4608, 6144, 7680, 9216, 10752, 12288, 13312
NCHIP = 4
WB = NC // NCHIP
DIL = (1, 4, 16)
HPG = 4
EPS = 1e-6
NEG = -1e30
ROPE_THETA = 10000.0
A_SCALE = HD ** -0.5
X_SCALE = XHD ** -0.5

ADAM_LR = 0.001
ADAM_B1 = 0.9
ADAM_B2 = 0.999
ADAM_EPS = 1e-08
ADAM_WD = 0.01
ADAM_STEP = 10

MESH = pl.DeviceIdType.MESH
MIB = 1024 * 1024


def _cp(sem, vmem_mib=48):
    return pltpu.CompilerParams(dimension_semantics=sem, vmem_limit_bytes=vmem_mib * MIB)


def _sigmoid(z):
    return 1.0 / (1.0 + jnp.exp(-z))


def _rope(x, cos, sin, half):
    return x * cos + pltpu.roll(x, half, 1) * sin


def _rope_t(g, cos, sin, half):
    return g * cos + pltpu.roll(g * sin, half, 1)


def _rms_fwd(x2, g, name, dep=None):
    rows = x2.shape[0]
    tr = 256
    deps = [] if dep is None else list(dep)

    def body(x_ref, g_ref, *rest):
        o_ref = rest[-1]
        x = x_ref[...]
        r = lax.rsqrt(jnp.mean(x * x, axis=-1, keepdims=True) + EPS)
        o_ref[...] = (x * r * g_ref[...]).astype(BF16)

    return pl.pallas_call(
        body, name=name, grid=(rows // tr,),
        in_specs=[pl.BlockSpec((tr, D), lambda i: (i, 0)), pl.BlockSpec((1, D), lambda i: (0, 0))]
        + [pl.BlockSpec(memory_space=pl.ANY)] * len(deps),
        out_specs=pl.BlockSpec((tr, D), lambda i: (i, 0)),
        out_shape=jax.ShapeDtypeStruct((rows, D), BF16),
        compiler_params=_cp(("parallel",)),
    )(x2, g, *deps)


def _norm_gain_grad(dn, x2, name):
    rows = x2.shape[0]
    tr = 256

    def body(dn_ref, x_ref, dg_ref):
        @pl.when(pl.program_id(0) == 0)
        def _():
            dg_ref[...] = jnp.zeros_like(dg_ref)
        x = x_ref[...]
        r = lax.rsqrt(jnp.mean(x * x, axis=-1, keepdims=True) + EPS)
        dg_ref[...] += jnp.sum(dn_ref[...] * (x * r), axis=0, keepdims=True)

    return pl.pallas_call(
        body, name=name, grid=(rows // tr,),
        in_specs=[pl.BlockSpec((tr, D), lambda i: (i, 0)), pl.BlockSpec((tr, D), lambda i: (i, 0))],
        out_specs=pl.BlockSpec((1, D), lambda i: (0, 0)),
        out_shape=jax.ShapeDtypeStruct((1, D), F32),
        compiler_params=_cp(("arbitrary",)),
    )(dn, x2)


def _pre_norm_bwd(dh, x2, g, dout, dep=None):
    tr = 256
    deps = [] if dep is None else [dep]

    def body(dh_ref, x_ref, g_ref, dout_ref, *rest):
        gx_ref, dg_ref = rest[len(deps):]

        @pl.when(pl.program_id(0) == 0)
        def _():
            dg_ref[...] = jnp.zeros_like(dg_ref)

        x = x_ref[...]
        dh_ = dh_ref[...].astype(F32)
        r = lax.rsqrt(jnp.mean(x * x, axis=-1, keepdims=True) + EPS)
        xhat = x * r
        dg_ref[...] += jnp.sum(dh_ * xhat, axis=0, keepdims=True)
        dxn = dh_ * g_ref[...]
        gx_ref[...] = dout_ref[...].astype(F32) + r * (dxn - xhat * jnp.mean(dxn * xhat, axis=-1, keepdims=True))

    row = pl.BlockSpec((tr, D), lambda i: (i, 0))
    vec = pl.BlockSpec((1, D), lambda i: (0, 0))
    return pl.pallas_call(
        body, name="pre_norm_bwd", grid=(T // tr,),
        in_specs=[row, row, vec, row] + [pl.BlockSpec(memory_space=pl.ANY)] * len(deps),
        out_specs=[row, vec],
        out_shape=[jax.ShapeDtypeStruct((T, D), F32), jax.ShapeDtypeStruct((1, D), F32)],
        compiler_params=_cp(("arbitrary",)),
    )(dh, x2, g, dout, *deps)


def _post_norm_loss(y, x2, tgt, g):
    tr = 256

    def body(y_ref, x_ref, t_ref, g_ref, dy_ref, dout_ref, dg_ref, loss_ref):
        @pl.when(pl.program_id(0) == 0)
        def _():
            dg_ref[...] = jnp.zeros_like(dg_ref)
            loss_ref[...] = jnp.zeros_like(loss_ref)
        yv = y_ref[...]
        gv = g_ref[...]
        r = lax.rsqrt(jnp.mean(yv * yv, axis=-1, keepdims=True) + EPS)
        yhat = yv * r
        err = x_ref[...] + yhat * gv - t_ref[...]
        loss_ref[...] += jnp.sum(jnp.sum(err * err, axis=1, keepdims=True), axis=0, keepdims=True) * (0.5 / D)
        dout = err * (1.0 / D)
        dout_ref[...] = dout.astype(BF16)
        dg_ref[...] += jnp.sum(dout * yhat, axis=0, keepdims=True)
        dyn = dout * gv
        dy_ref[...] = (r * (dyn - yhat * jnp.mean(dyn * yhat, axis=-1, keepdims=True))).astype(BF16)

    row = pl.BlockSpec((tr, D), lambda i: (i, 0))
    vec = pl.BlockSpec((1, D), lambda i: (0, 0))
    return pl.pallas_call(
        body, name="post_norm_loss", grid=(T // tr,),
        in_specs=[row, row, row, vec],
        out_specs=[row, row, vec, pl.BlockSpec((1, 128), lambda i: (0, 0))],
        out_shape=[jax.ShapeDtypeStruct((T, D), BF16), jax.ShapeDtypeStruct((T, D), BF16),
                   jax.ShapeDtypeStruct((1, D), F32), jax.ShapeDtypeStruct((1, 128), F32)],
        compiler_params=_cp(("arbitrary",)),
    )(y, x2, tgt, g)


NN = (((1,), (0,)), ((), ()))
NT = (((1,), (1,)), ((), ()))
TN = (((0,), (0,)), ((), ()))


def _as_index(v):
    return jnp.reshape(v, (1,)).astype(jnp.int32)


def _matmul(a, b, *, name, dims, grid, a_block, a_map, b_block, b_map, o_block, o_map, out_shape, out_dtype=F32,
            index=None, prev=None, deps=(), addend=None):
    extra = ([] if prev is None else [prev]) + [d for d in deps if d is not None]
    first = 0 if index is None else 1
    nk = grid[2]
    in_place = out_dtype == F32
    assert addend is None or not in_place
    n_add = 0 if addend is None else 1

    def body(*refs):
        a_ref, b_ref, o_ref = refs[first], refs[first + 1], refs[first + 2 + n_add + len(extra)]
        acc_ref = o_ref if in_place else refs[-1]

        @pl.when(pl.program_id(2) == 0)
        def _():
            acc_ref[...] = lax.dot_general(a_ref[...], b_ref[...], dims, preferred_element_type=F32)

        @pl.when(pl.program_id(2) > 0)
        def _():
            acc_ref[...] += lax.dot_general(a_ref[...], b_ref[...], dims, preferred_element_type=F32)

        if not in_place:
            @pl.when(pl.program_id(2) == nk - 1)
            def _():
                total = acc_ref[...] if addend is None else acc_ref[...] + refs[first + 2][...].astype(F32)
                o_ref[...] = total.astype(o_ref.dtype)

    adds = [] if addend is None else [addend]
    in_specs = ([pl.BlockSpec(a_block, a_map), pl.BlockSpec(b_block, b_map)] + [pl.BlockSpec(o_block, o_map)] * n_add
                + [pl.BlockSpec(memory_space=pl.ANY)] * len(extra))
    out_specs = pl.BlockSpec(o_block, o_map)
    scratch = [] if in_place else [pltpu.VMEM(o_block, F32)]
    kwargs = dict(name=name, out_shape=jax.ShapeDtypeStruct(out_shape, out_dtype),
                  input_output_aliases={} if prev is None else {first + 2 + n_add: 0},
                  compiler_params=_cp(("parallel", "parallel", "arbitrary"), vmem_mib=56))
    if index is None:
        return pl.pallas_call(body, grid=grid, in_specs=in_specs, out_specs=out_specs, scratch_shapes=scratch,
                              **kwargs)(a, b, *adds, *extra)
    grid_spec = pltpu.PrefetchScalarGridSpec(num_scalar_prefetch=1, grid=grid, in_specs=in_specs, out_specs=out_specs,
                                             scratch_shapes=scratch)
    return pl.pallas_call(body, grid_spec=grid_spec, **kwargs)(_as_index(index), a, b, *adds, *extra)


def _mm_nn(a, b, name, tm, tn, tk):
    m, kd = a.shape
    n = b.shape[1]
    return _matmul(a, b, name=name, dims=NN, grid=(m // tm, n // tn, kd // tk),
                   a_block=(tm, tk), a_map=lambda i, j, k: (i, k),
                   b_block=(tk, tn), b_map=lambda i, j, k: (k, j),
                   o_block=(tm, tn), o_map=lambda i, j, k: (i, j), out_shape=(m, n))


def _mm_nt(a, b, name, tm, tn, tk):
    m, kd = a.shape
    n = b.shape[0]
    return _matmul(a, b, name=name, dims=NT, grid=(m // tm, n // tn, kd // tk),
                   a_block=(tm, tk), a_map=lambda i, j, k: (i, k),
                   b_block=(tn, tk), b_map=lambda i, j, k: (j, k),
                   o_block=(tm, tn), o_map=lambda i, j, k: (i, j), out_shape=(m, n))


W_TN = 1792
W_NJ = WB // W_TN


def _proj_part(h, wg, chip, masks, name, prev=None, dep=None):
    tm, tk = 1024, 2048

    def blk(j, ix):
        m = masks[0]
        for t in range(1, len(masks)):
            m = jnp.where(j // W_NJ == t, masks[t], m)
        return jnp.bitwise_xor(ix[0], m)

    return _matmul(h, wg, name=name, dims=NN, grid=(T // tm, len(masks) * W_NJ, D // tk), index=chip, prev=prev, deps=(dep,),
                   a_block=(tm, tk), a_map=lambda i, j, k, ix: (i, k),
                   b_block=(tk, W_TN), b_map=lambda i, j, k, ix: (blk(j, ix) * (D // tk) + k, j % W_NJ),
                   o_block=(tm, W_TN), o_map=lambda i, j, k, ix: (i, blk(j, ix) * W_NJ + j % W_NJ), out_shape=(T, NC),
                   out_dtype=BF16)


def _adamw_math(w_ref, g_ref, m_ref, v_ref, go_ref, d_ref, nm_ref, nv_ref):
    gv = g_ref[...]
    go_ref[...] = gv
    nm = ADAM_B1 * m_ref[...] + (1.0 - ADAM_B1) * gv
    nv = ADAM_B2 * v_ref[...] + (1.0 - ADAM_B2) * (gv * gv)
    m_hat = nm / (1.0 - ADAM_B1 ** ADAM_STEP)
    v_hat = nv / (1.0 - ADAM_B2 ** ADAM_STEP)
    d_ref[...] = -ADAM_LR * (m_hat / (jnp.sqrt(v_hat) + ADAM_EPS) + ADAM_WD * w_ref[...])
    nm_ref[...] = nm
    nv_ref[...] = nv


def _dh(dproj, wg, dep=None, adam=()):
    tm, tn = 1024, 2048
    grid = (T // tm, D // tn, NC // W_TN)
    nsteps = grid[0] * grid[1] * grid[2]
    deps = [] if dep is None else [dep]
    na = len(adam)
    nk = grid[2]

    def body(*refs):
        a_ref, b_ref = refs[0], refs[1]
        o_ref = refs[2 + 4 * na + len(deps)]
        acc_ref = refs[-1]

        @pl.when(pl.program_id(2) == 0)
        def _():
            acc_ref[...] = lax.dot_general(a_ref[...], b_ref[...], NT, preferred_element_type=F32)

        @pl.when(pl.program_id(2) > 0)
        def _():
            acc_ref[...] += lax.dot_general(a_ref[...], b_ref[...], NT, preferred_element_type=F32)

        @pl.when(pl.program_id(2) == nk - 1)
        def _():
            o_ref[...] = acc_ref[...].astype(BF16)

        for s in range(na):
            ins = refs[2 + 4 * s:6 + 4 * s]
            outs = refs[3 + 4 * na + len(deps) + 4 * s:7 + 4 * na + len(deps) + 4 * s]
            _adamw_math(*ins, *outs)

    def rows_of(arr):
        r, c = arr.shape
        return pl.BlockSpec((r // nsteps, c), lambda i, j, k: ((i * grid[1] + j) * grid[2] + k, 0))

    adam_specs = [rows_of(a) for st in adam for a in st]
    res = pl.pallas_call(
        body, name="dh", grid=grid,
        in_specs=[pl.BlockSpec((tm, W_TN), lambda i, j, k: (i, k)),
                  pl.BlockSpec((tn, W_TN), lambda i, j, k: ((k // W_NJ) * (D // tn) + j, k % W_NJ))]
        + adam_specs + [pl.BlockSpec(memory_space=pl.ANY)] * len(deps),
        out_specs=[pl.BlockSpec((tm, tn), lambda i, j, k: (i, j))] + adam_specs,
        out_shape=[jax.ShapeDtypeStruct((T, D), BF16)] + [jax.ShapeDtypeStruct(a.shape, F32) for st in adam for a in st],
        scratch_shapes=[pltpu.VMEM((tm, tn), F32)],
        compiler_params=_cp(("arbitrary",) * 3, vmem_mib=56),
    )(dproj, wg, *[a for st in adam for a in st], *deps)
    return res[0], [tuple(res[1 + 4 * s:5 + 4 * s]) for s in range(na)]


def _grad_rows(a, b, half, out_dtype, name, dep=None):
    kd, n = b.shape
    tm, tn, tk = D // NCHIP // 2, min(n, 2048), min(kd, 2048)
    return _matmul(a, b, name=name, dims=TN, grid=(NCHIP, n // tn, kd // tk), index=half, deps=(dep,),
                   a_block=(tk, tm), a_map=lambda i, j, k, ix: (k, 2 * i + ix[0]),
                   b_block=(tk, tn), b_map=lambda i, j, k, ix: (k, j),
                   o_block=(tm, tn), o_map=lambda i, j, k, ix: (i, j),
                   out_shape=(NCHIP * tm, n), out_dtype=out_dtype)


GW_TM = 1024


def _grad_w_in(h, dproj, half, out_dtype, name, dep=None, tiles=(0, 2), addend=None, prev=None):
    tm, tk = GW_TM, 2048
    nh = D // 2 // tm
    t0, nt = tiles

    def o_map(i, j, k, ix):
        return ((j // W_NJ) * nh + t0 + i, j % W_NJ)

    return _matmul(h, dproj, name=name, dims=TN, grid=(nt, NC // W_TN, T // tk), index=half, deps=(dep,), prev=prev,
                   a_block=(tk, tm), a_map=lambda i, j, k, ix: (k, ix[0] * nh + t0 + i),
                   b_block=(tk, W_TN), b_map=lambda i, j, k, ix: (k, j),
                   o_block=(tm, W_TN), o_map=o_map, addend=addend,
                   out_shape=(NCHIP * D // 2, WB), out_dtype=out_dtype)


def _rope_tables(pos, half):
    inv = 1.0 / (ROPE_THETA ** (jnp.arange(half, dtype=F32) / half))
    ang = pos.astype(F32)[:, None] * inv[None, :]
    cos, sin = jnp.cos(ang), jnp.sin(ang)
    return jnp.concatenate([cos, cos], axis=1), jnp.concatenate([-sin, sin], axis=1)


def _band_mask(r0):
    qi = lax.broadcasted_iota(jnp.int32, (128, 256), 0)
    kk = lax.broadcasted_iota(jnp.int32, (128, 256), 1)
    return (kk >= qi) & (kk <= qi + 128) & (kk + r0 >= 128)


def _window(r0, nblk):
    if nblk == 1:
        qi = lax.broadcasted_iota(jnp.int32, (128, 128), 0)
        kk = lax.broadcasted_iota(jnp.int32, (128, 128), 1)
        return pl.ds(128, 128), kk <= qi
    return pl.ds(r0, 256), _band_mask(r0)


def _dil_rows(r, n, d):
    if d == 1:
        return pl.ds(pl.multiple_of(n * 128, 128), 128)
    return pl.ds(r + d * 128 * n, 128, stride=d)


def _widen(refs, wide):
    if not wide:
        return refs

    def copy(n, carry):
        rows = pl.ds(pl.multiple_of(n * 256, 256), 256)
        for src, dst in zip(refs, wide):
            dst[rows, :] = src[rows, :].astype(F32)
        return carry

    lax.fori_loop(0, S // 256, copy, 0)
    return wide


def _attn_fwd(proj, cosf, sinf, g):
    d = DIL[g]
    ln = S // d
    nblk = ln // 128
    proj_v = proj.reshape(NB, S, NC)

    def body(q_ref, k_ref, v_ref, cos_ref, sin_ref, o_ref, l_ref, k_s, v_s, *wide):
        k_s[:, pl.ds(0, 128), :] = jnp.zeros((d, 128, HD), BF16)
        v_s[:, pl.ds(0, 128), :] = jnp.zeros((d, 128, HD), BF16)
        q_ref, k_ref, v_ref = _widen((q_ref, k_ref, v_ref), wide)

        def prep(i, carry):
            r, n = i // nblk, i % nblk
            rows = _dil_rows(r, n, d)
            dst = pl.ds(pl.multiple_of(n * 128 + 128, 128), 128)
            k_s[r, dst, :] = _rope(k_ref[rows, :].astype(F32), cos_ref[rows, :], sin_ref[rows, :], HD // 2).astype(BF16)
            v_s[r, dst, :] = v_ref[rows, :].astype(BF16)
            return carry

        lax.fori_loop(0, d * nblk, prep, 0, unroll=4)

        def step(i, carry):
            r, n = i // nblk, i % nblk
            rows = _dil_rows(r, n, d)
            r0 = pl.multiple_of(n * 128, 128)
            qr = _rope(q_ref[rows, :].astype(F32), cos_ref[rows, :], sin_ref[rows, :], HD // 2).astype(BF16)
            win, mask = _window(r0, nblk)
            kw = k_s[r, win, :]
            vw = v_s[r, win, :]
            sc = lax.dot_general(qr, kw, NT, preferred_element_type=F32) * A_SCALE
            sc = jnp.where(mask, sc, NEG)
            m = jnp.max(sc, axis=1, keepdims=True)
            p = jnp.exp(sc - m)
            l = jnp.sum(p, axis=1, keepdims=True)
            o_ref[rows, :] = jnp.dot(p.astype(BF16), vw, preferred_element_type=F32) / l
            l_ref[rows, :] = jnp.broadcast_to(m + jnp.log(l), (128, HD))
            return carry

        lax.fori_loop(0, d * nblk, step, 0, unroll=4)

    def col(off):
        return lambda b, h: (b, 0, off // HD + HPG * g + h)

    blk = (None, S, HD)
    tab = pl.BlockSpec((S, HD), lambda b, h: (0, 0))
    out = pl.BlockSpec(blk, lambda b, h: (b, 0, h))
    kv = pl.BlockSpec((None, None, d, ln + 128, HD), lambda b, h: (b, h, 0, 0, 0))
    o, l, kr, vr = pl.pallas_call(
        body, name=f"attn_fwd_d{d}", grid=(NB, HPG),
        in_specs=[pl.BlockSpec(blk, col(QA)), pl.BlockSpec(blk, col(KA)), pl.BlockSpec(blk, col(VA)), tab, tab],
        out_specs=[out, out, kv, kv],
        out_shape=[jax.ShapeDtypeStruct((NB, S, HPG * HD), F32)] * 2
        + [jax.ShapeDtypeStruct((NB, HPG, d, ln + 128, HD), BF16)] * 2,
        scratch_shapes=[pltpu.VMEM((S, HD), F32)] * (3 if d > 1 else 0),
        compiler_params=_cp(("parallel", "parallel")),
    )(proj_v, proj_v, proj_v, cosf, sinf)
    return o.reshape(T, HPG * HD), l.reshape(T, HPG * HD), kr, vr


def _attn_bwd(proj, kr, vr, cosf, sinf, da, lse, delta, dproj, g):
    d = DIL[g]
    ln = S // d
    nblk = ln // 128
    proj_v = proj.reshape(NB, S, NC)
    dproj_v = dproj.reshape(NB, S, NC)
    da_v = da.reshape(NB, S, AW)
    lse_v = lse.reshape(NB, S, HPG * HD)
    delta_v = delta.reshape(NB, S, HPG * HD)

    def body(q_ref, k_s, v_s, cos_ref, sin_ref, da_ref, lse_ref, dl_ref, dp_in_ref, o_ref, stg, dk_s, dv_s, *wide):
        del dp_in_ref
        w = pl.program_id(2)

        def emit():
            def cast(n, carry):
                rows = pl.ds(pl.multiple_of(n * 256, 256), 256)
                o_ref[rows, :] = stg[rows, :].astype(BF16)
                return carry

            lax.fori_loop(0, S // 256, cast, 0)

        @pl.when(w == 0)
        def _():
            dk_s[...] = jnp.zeros_like(dk_s)
            dv_s[...] = jnp.zeros_like(dv_s)
            (q_src,) = _widen((q_ref,), wide)

            def step(i, carry):
                r, n = i // nblk, i % nblk
                rows = _dil_rows(r, n, d)
                r0 = pl.multiple_of(n * 128, 128)
                win, mask = _window(r0, nblk)
                cos, sin = cos_ref[rows, :], sin_ref[rows, :]
                qr = _rope(q_src[rows, :].astype(F32), cos, sin, HD // 2).astype(BF16)
                kw = k_s[r, win, :]
                vw = v_s[r, win, :]
                sc = lax.dot_general(qr, kw, NT, preferred_element_type=F32) * A_SCALE
                sc = jnp.where(mask, sc, NEG)
                p = jnp.exp(sc - lse_ref[rows, :][:, :1])
                da_b = da_ref[rows, :].astype(BF16)
                dp = lax.dot_general(da_b, vw, NT, preferred_element_type=F32)
                ds_b = (p * (dp - dl_ref[rows, :][:, :1]) * A_SCALE).astype(BF16)
                p_b = p.astype(BF16)
                dq = jnp.dot(ds_b, kw, preferred_element_type=F32)
                stg[rows, :] = _rope_t(dq, cos, sin, HD // 2)
                dk_s[r, win, :] += lax.dot_general(ds_b, qr, TN, preferred_element_type=F32)
                dv_s[r, win, :] += lax.dot_general(p_b, da_b, TN, preferred_element_type=F32)
                return carry

            lax.fori_loop(0, d * nblk, step, 0, unroll=4)
            emit()

        @pl.when(w == 1)
        def _():
            def put(i, carry):
                r, n = i // nblk, i % nblk
                rows = _dil_rows(r, n, d)
                src = pl.ds(pl.multiple_of(n * 128 + 128, 128), 128)
                stg[rows, :] = _rope_t(dk_s[r, src, :], cos_ref[rows, :], sin_ref[rows, :], HD // 2)
                return carry

            lax.fori_loop(0, d * nblk, put, 0, unroll=4)
            emit()

        @pl.when(w == 2)
        def _():
            def put(i, carry):
                r, n = i // nblk, i % nblk
                src = pl.ds(pl.multiple_of(n * 128 + 128, 128), 128)
                stg[_dil_rows(r, n, d), :] = dv_s[r, src, :]
                return carry

            lax.fori_loop(0, d * nblk, put, 0, unroll=4)
            emit()

    def col(off):
        return lambda b, h, w: (ahead(b, h, w)[0], 0, off // HD + HPG * g + ahead(b, h, w)[1])

    def ahead(b, h, w):
        flat = jnp.minimum(b * HPG + h + jnp.where(w > 0, 1, 0), NB * HPG - 1)
        return flat // HPG, flat % HPG

    blk = (None, S, HD)
    tab = pl.BlockSpec((S, HD), lambda b, h, w: (0, 0))
    per_head = pl.BlockSpec(blk, lambda b, h, w: (ahead(b, h, w)[0], 0, ahead(b, h, w)[1]))
    kv = pl.BlockSpec((None, None, d, ln + 128, HD), lambda b, h, w: (*ahead(b, h, w), 0, 0, 0))
    out = pl.pallas_call(
        body, name=f"attn_bwd_d{d}", grid=(NB, HPG, 3),
        in_specs=[pl.BlockSpec(blk, col(QA)), kv, kv, tab, tab,
                  pl.BlockSpec(blk, col(0)), per_head, per_head, pl.BlockSpec(memory_space=pl.ANY)],
        out_specs=pl.BlockSpec(blk, lambda b, h, w: (b, 0, (AW // HD) * w + HPG * g + h)),
        out_shape=jax.ShapeDtypeStruct(dproj_v.shape, BF16),
        input_output_aliases={8: 0},
        scratch_shapes=[pltpu.VMEM((S, HD), F32), pltpu.VMEM((d, ln + 128, HD), F32), pltpu.VMEM((d, ln + 128, HD), F32)]
        + [pltpu.VMEM((S, HD), F32)] * (1 if d > 1 else 0),
        compiler_params=_cp(("arbitrary",) * 3),
    )(proj_v, kr, vr, cosf, sinf, da_v, lse_v, delta_v, dproj_v)
    return out.reshape(T, NC)


def _attn_mix(proj, os_, ls_, dep=None):
    tr = 256
    gw = HPG * HD
    deps = [] if dep is None else [dep]

    def body(o0, o1, o2, l0, l1, l2, z_ref, *rest):
        cat_ref = rest[-1]
        m = jnp.maximum(jnp.maximum(l0[...], l1[...]), l2[...])
        e = [jnp.exp(l[...] - m) for l in (l0, l1, l2)]
        inv = 1.0 / (e[0] + e[1] + e[2])
        for gi, o in enumerate((o0, o1, o2)):
            z = z_ref[:, gi * gw:(gi + 1) * gw].astype(F32)
            cat_ref[:, gi * gw:(gi + 1) * gw] = (o[...] * (e[gi] * inv) * (z * _sigmoid(z))).astype(BF16)

    grp = pl.BlockSpec((tr, gw), lambda i: (i, 0))
    return pl.pallas_call(
        body, name="attn_mix", grid=(T // tr,),
        in_specs=[grp] * 6 + [pl.BlockSpec((tr, AW), lambda i: (i, ZA // AW))] + [ANY_SPEC] * len(deps),
        out_specs=pl.BlockSpec((tr, AW), lambda i: (i, 0)),
        out_shape=jax.ShapeDtypeStruct((T, D), BF16),
        compiler_params=_cp(("parallel",)),
    )(*os_, *ls_, proj, *deps)


def _attn_mix_bwd(dcat, proj, os_, ls_, dep=None):
    tr = 256
    gw = HPG * HD
    deps = [] if dep is None else [dep]

    def body(dy_ref, o0, o1, o2, l0, l1, l2, z_ref, *rest):
        da_ref, lse_ref, dl_ref, dz_ref = rest[len(deps):]
        m = jnp.maximum(jnp.maximum(l0[...], l1[...]), l2[...])
        e = [jnp.exp(l[...] - m) for l in (l0, l1, l2)]
        den = e[0] + e[1] + e[2]
        inv = 1.0 / den
        lse_ref[...] = m + jnp.log(den)
        acc = jnp.zeros((tr, gw), F32)
        for gi, o in enumerate((o0, o1, o2)):
            cols = slice(gi * gw, (gi + 1) * gw)
            z = z_ref[:, cols].astype(F32)
            dy = dy_ref[:, cols]
            sg = _sigmoid(z)
            a = o[...] * (e[gi] * inv)
            da = dy * (z * sg)
            da_ref[:, cols] = da
            dz_ref[:, cols] = (dy * a * (sg * (1.0 + z * (1.0 - sg)))).astype(BF16)
            acc = acc + da * a
        for hh in range(HPG):
            cols = slice(hh * HD, (hh + 1) * HD)
            dl_ref[:, cols] = jnp.broadcast_to(jnp.sum(acc[:, cols], axis=1, keepdims=True), (tr, HD))

    grp = pl.BlockSpec((tr, gw), lambda i: (i, 0))
    return pl.pallas_call(
        body, name="attn_mix_bwd", grid=(T // tr,),
        in_specs=[pl.BlockSpec((tr, AW), lambda i: (i, 0))] + [grp] * 6 + [pl.BlockSpec((tr, AW), lambda i: (i, ZA // AW))]
        + [pl.BlockSpec(memory_space=pl.ANY)] * len(deps),
        out_specs=[pl.BlockSpec((tr, AW), lambda i: (i, 0)), grp, grp, pl.BlockSpec((tr, AW), lambda i: (i, ZA // AW))],
        out_shape=[jax.ShapeDtypeStruct((T, AW), F32), jax.ShapeDtypeStruct((T, gw), F32),
                   jax.ShapeDtypeStruct((T, gw), F32), jax.ShapeDtypeStruct((T, NC), BF16)],
        compiler_params=_cp(("parallel",)),
    )(dcat, *os_, *ls_, proj, *deps)


CT = 256


def _shift_down(x, n):
    rows = lax.broadcasted_iota(jnp.int32, x.shape, 0)
    return jnp.where(rows >= n, pltpu.roll(x, n, 0), 0.0)


def _shift_up(x, n):
    rows = lax.broadcasted_iota(jnp.int32, x.shape, 0)
    return jnp.where(rows < x.shape[0] - n, pltpu.roll(x, x.shape[0] - n, 0), 0.0)


def _conv_fwd(proj, conv_w, cat):
    proj_v = proj.reshape(NB, S, NC)
    cat_v = cat.reshape(NB, S, D)

    def body(u_ref, b_ref, c_ref, z_ref, w_ref, cat_in, o_ref):
        del cat_in
        cu = c_ref[...].astype(F32) * u_ref[...].astype(F32)
        cv = _shift_down(cu, 2) * w_ref[0:1, :] + _shift_down(cu, 1) * w_ref[1:2, :] + cu * w_ref[2:3, :]
        z = z_ref[...].astype(F32)
        o_ref[...] = (b_ref[...].astype(F32) * cv * (z * _sigmoid(z))).astype(BF16)

    def seg(off):
        return pl.BlockSpec((None, S, CT), lambda b, j: (b, 0, off // CT + j))

    out = pl.pallas_call(
        body, name="conv_fwd", grid=(NB, CW // CT),
        in_specs=[seg(UC), seg(BC), seg(CC), seg(ZC), pl.BlockSpec((3, CT), lambda b, j: (0, j)),
                  pl.BlockSpec(memory_space=pl.ANY)],
        out_specs=pl.BlockSpec((None, S, CT), lambda b, j: (b, 0, AW // CT + j)),
        out_shape=jax.ShapeDtypeStruct((NB, S, D), BF16),
        input_output_aliases={5: 0},
        compiler_params=_cp(("parallel", "parallel")),
    )(proj_v, proj_v, proj_v, proj_v, conv_w, cat_v)
    return out.reshape(T, D)


def _conv_bwd(dcat, proj, conv_w, dproj, dep=None):
    deps = [] if dep is None else [dep]
    proj_v = proj.reshape(NB, S, NC)
    dproj_v = dproj.reshape(NB, S, NC)
    dcat_v = dcat.reshape(NB, S, D)

    def body(dy_ref, u_ref, b_ref, c_ref, z_ref, w_ref, *rest):
        o_ref, dw_ref, st = rest[1 + len(deps):]
        b = pl.program_id(1)
        w = pl.program_id(2)

        @pl.when((b == 0) & (w == 0))
        def _():
            dw_ref[...] = jnp.zeros_like(dw_ref)

        @pl.when(w == 0)
        def _():
            u, c, z, bb = (r[...].astype(F32) for r in (u_ref, c_ref, z_ref, b_ref))
            dy = dy_ref[...]
            cu = c * u
            s1 = _shift_down(cu, 1)
            s2 = _shift_down(cu, 2)
            cv = s2 * w_ref[0:1, :] + s1 * w_ref[1:2, :] + cu * w_ref[2:3, :]
            sg = _sigmoid(z)
            sz = z * sg
            dcv = dy * bb * sz
            st[0] = dy * cv * sz
            st[2] = dy * bb * cv * (sg * (1.0 + z * (1.0 - sg)))
            dw_ref[0:1, :] += jnp.sum(dcv * s2, axis=0, keepdims=True)
            dw_ref[1:2, :] += jnp.sum(dcv * s1, axis=0, keepdims=True)
            dw_ref[2:3, :] += jnp.sum(dcv * cu, axis=0, keepdims=True)
            dcu = dcv * w_ref[2:3, :] + _shift_up(dcv, 1) * w_ref[1:2, :] + _shift_up(dcv, 2) * w_ref[0:1, :]
            st[1] = dcu * u
            o_ref[...] = (dcu * c).astype(BF16)

        for k in range(3):
            @pl.when(w == k + 1)
            def _(k=k):
                o_ref[...] = st[k].astype(BF16)

    def ahead(j, b, w):
        flat = jnp.minimum(j * NB + b + jnp.where(w > 0, 1, 0), (CW // CT) * NB - 1)
        return flat // NB, flat % NB

    def seg(off):
        return pl.BlockSpec((None, S, CT), lambda j, b, w: (ahead(j, b, w)[1], 0, off // CT + ahead(j, b, w)[0]))

    out, dw = pl.pallas_call(
        body, name="conv_bwd", grid=(CW // CT, NB, 4),
        in_specs=[seg(AW), seg(UC), seg(BC), seg(CC), seg(ZC),
                  pl.BlockSpec((3, CT), lambda j, b, w: (0, ahead(j, b, w)[0])),
                  pl.BlockSpec(memory_space=pl.ANY)] + [pl.BlockSpec(memory_space=pl.ANY)] * len(deps),
        out_specs=[pl.BlockSpec((None, S, CT), lambda j, b, w: (b, 0, (UC + w * CW) // CT + j)),
                   pl.BlockSpec((3, CT), lambda j, b, w: (0, j))],
        out_shape=[jax.ShapeDtypeStruct((NB, S, NC), BF16), jax.ShapeDtypeStruct((3, CW), F32)],
        input_output_aliases={6: 0},
        scratch_shapes=[pltpu.VMEM((3, S, CT), F32)],
        compiler_params=_cp(("arbitrary",) * 3),
    )(dcat_v, proj_v, proj_v, proj_v, proj_v, conv_w, dproj_v, *deps)
    return out.reshape(T, NC), dw


XT = 1024


def _cross_fwd(proj, mkv, cosq, sinq, cosm, sinm, cat):
    proj_v = proj.reshape(NB, S, NC)
    mkv_v = mkv.reshape(NB, MLEN, 2 * XW)
    cat_v = cat.reshape(NB, S, D)

    def body(q_ref, z_ref, mk_ref, mv_ref, cq, sq, cm, sm, cat_in, o_ref):
        del cat_in
        mkr = _rope(mk_ref[...], cm[...], sm[...], XHD // 2).astype(BF16)
        qr = _rope(q_ref[...].astype(F32), cq[...], sq[...], XHD // 2).astype(BF16)
        sc = lax.dot_general(qr, mkr, NT, preferred_element_type=F32) * X_SCALE
        p = jnp.exp(sc - jnp.max(sc, axis=1, keepdims=True))
        p = p / jnp.sum(p, axis=1, keepdims=True)
        ox = jnp.dot(p.astype(BF16), mv_ref[...].astype(BF16), preferred_element_type=F32)
        z = z_ref[...].astype(F32)
        o_ref[...] = (ox * (z * _sigmoid(z))).astype(BF16)

    def seg(off):
        return pl.BlockSpec((None, XT, XHD), lambda b, h, t: (b, t, off // XHD + h))

    qtab = pl.BlockSpec((XT, XHD), lambda b, h, t: (t, 0))
    mtab = pl.BlockSpec((MLEN, XHD), lambda b, h, t: (0, 0))
    out = pl.pallas_call(
        body, name="cross_fwd", grid=(NB, NXH, S // XT),
        in_specs=[seg(QX), seg(ZX),
                  pl.BlockSpec((None, MLEN, XHD), lambda b, h, t: (b, 0, h)),
                  pl.BlockSpec((None, MLEN, XHD), lambda b, h, t: (b, 0, NXH + h)),
                  qtab, qtab, mtab, mtab, pl.BlockSpec(memory_space=pl.ANY)],
        out_specs=pl.BlockSpec((None, XT, XHD), lambda b, h, t: (b, t, (AW + CW) // XHD + h)),
        out_shape=jax.ShapeDtypeStruct((NB, S, D), BF16),
        input_output_aliases={8: 0},
        compiler_params=_cp(("parallel",) * 3),
    )(proj_v, proj_v, mkv_v, mkv_v, cosq, sinq, cosm, sinm, cat_v)
    return out.reshape(T, D)


def _cross_bwd(dcat, proj, mkv, cosq, sinq, cosm, sinm, dproj):
    proj_v = proj.reshape(NB, S, NC)
    dproj_v = dproj.reshape(NB, S, NC)
    dcat_v = dcat.reshape(NB, S, D)
    mkv_v = mkv.reshape(NB, MLEN, 2 * XW)
    nt = S // XT

    def body(dy_ref, q_ref, z_ref, mk_ref, mv_ref, cq, sq, cm, sm, dp_in, o_ref, dmk_ref, dmv_ref, dz_s):
        del dp_in
        t = pl.program_id(2)
        w = pl.program_id(3)

        @pl.when((t == 0) & (w == 0))
        def _():
            dmk_ref[...] = jnp.zeros_like(dmk_ref)
            dmv_ref[...] = jnp.zeros_like(dmv_ref)

        @pl.when(w == 0)
        def _():
            mkr = _rope(mk_ref[...], cm[...], sm[...], XHD // 2).astype(BF16)
            mv_b = mv_ref[...].astype(BF16)
            qr = _rope(q_ref[...].astype(F32), cq[...], sq[...], XHD // 2).astype(BF16)
            sc = lax.dot_general(qr, mkr, NT, preferred_element_type=F32) * X_SCALE
            p = jnp.exp(sc - jnp.max(sc, axis=1, keepdims=True))
            p = p / jnp.sum(p, axis=1, keepdims=True)
            p_b = p.astype(BF16)
            ox = jnp.dot(p_b, mv_b, preferred_element_type=F32)
            z = z_ref[...].astype(F32)
            dy = dy_ref[...]
            sg = _sigmoid(z)
            dz_s[...] = dy * ox * (sg * (1.0 + z * (1.0 - sg)))
            dox_b = (dy * (z * sg)).astype(BF16)
            dp = lax.dot_general(dox_b, mv_b, NT, preferred_element_type=F32)
            ds_b = (p * (dp - jnp.sum(dp * p, axis=1, keepdims=True)) * X_SCALE).astype(BF16)
            dq = jnp.dot(ds_b, mkr, preferred_element_type=F32)
            o_ref[...] = _rope_t(dq, cq[...], sq[...], XHD // 2).astype(BF16)
            dmk_ref[...] += lax.dot_general(ds_b, qr, TN, preferred_element_type=F32)
            dmv_ref[...] += lax.dot_general(p_b, dox_b, TN, preferred_element_type=F32)

        @pl.when(w == 1)
        def _():
            o_ref[...] = dz_s[...].astype(BF16)

        @pl.when((t == nt - 1) & (w == 1))
        def _():
            dmk_ref[...] = _rope_t(dmk_ref[...], cm[...], sm[...], XHD // 2)

    def seg(off):
        return pl.BlockSpec((None, XT, XHD), lambda b, h, t, w: (b, t, off // XHD + h))

    qtab = pl.BlockSpec((XT, XHD), lambda b, h, t, w: (t, 0))
    mtab = pl.BlockSpec((MLEN, XHD), lambda b, h, t, w: (0, 0))
    macc = pl.BlockSpec((None, MLEN, XHD), lambda b, h, t, w: (b, 0, h))
    out, dmk, dmv = pl.pallas_call(
        body, name="cross_bwd", grid=(NB, NXH, nt, 2),
        in_specs=[pl.BlockSpec((None, XT, XHD), lambda b, h, t, w: (b, t, (AW + CW) // XHD + h)),
                  seg(QX), seg(ZX),
                  pl.BlockSpec((None, MLEN, XHD), lambda b, h, t, w: (b, 0, h)),
                  pl.BlockSpec((None, MLEN, XHD), lambda b, h, t, w: (b, 0, NXH + h)),
                  qtab, qtab, mtab, mtab, pl.BlockSpec(memory_space=pl.ANY)],
        out_specs=[pl.BlockSpec((None, XT, XHD), lambda b, h, t, w: (b, t, (QX + w * XW) // XHD + h)), macc, macc],
        out_shape=[jax.ShapeDtypeStruct((NB, S, NC), BF16), jax.ShapeDtypeStruct((NB, MLEN, XW), F32),
                   jax.ShapeDtypeStruct((NB, MLEN, XW), F32)],
        input_output_aliases={9: 0},
        scratch_shapes=[pltpu.VMEM((XT, XHD), F32)],
        compiler_params=_cp(("arbitrary",) * 4),
    )(dcat_v, proj_v, proj_v, mkv_v, mkv_v, cosq, sinq, cosm, sinm, dproj_v)
    return out.reshape(T, NC), dmk, dmv


def _local_step(x, mem, pre_norm, conv_w, mem_norm, post_norm, tgt, chip, core, comm, moments=None):
    x2 = x.reshape(T, D)
    mem2 = mem.reshape(NB * MLEN, D)
    tgt2 = tgt.reshape(T, D)
    cosa, sina = _rope_tables(jnp.arange(S), HD // 2)
    cosq, sinq = _rope_tables(jnp.arange(S) + MLEN, XHD // 2)
    cosm, sinm = _rope_tables(jnp.arange(MLEN), XHD // 2)

    h = _rms_fwd(x2, pre_norm, "pre_norm_fwd", dep=comm.gather_started())
    memn = _rms_fwd(mem2, mem_norm, "mem_norm_fwd")
    proj = _proj_part(h, comm.w_in_own(), chip, (0,), "proj_own")
    wg_in, tok = comm.w_in_near(after=[proj, memn, conv_w])
    proj = _proj_part(h, wg_in, chip, (REL_XOR[0], REL_XOR[1]), "proj_near", prev=proj, dep=tok)
    wg_in, tok = comm.w_in_all(after=proj)
    proj = _proj_part(h, wg_in, chip, (REL_XOR[2],), "proj_far", prev=proj, dep=tok)
    fw = [_attn_fwd(proj, cosa, sina, g) for g in range(3)]
    os_ = [f[0] for f in fw]
    ls_ = [f[1] for f in fw]
    tok = comm.w_rest_landed(after=ls_)
    cat = _attn_mix(proj, os_, ls_, dep=tok)
    cat = _conv_fwd(proj, conv_w, cat)
    wg_kv, wg_out = comm.w_rest(after=[cat])
    mkv = _mm_nn(memn, wg_kv, "mkv", NB * MLEN, 1024, 512)
    cat = _cross_fwd(proj, mkv, cosq, sinq, cosm, sinm, cat)
    y = _mm_nn(cat, wg_out, "out_proj", 1024, 2048, 2048)
    dy, dout, d_post, loss = _post_norm_loss(y, x2, tgt2, post_norm)

    dcat = _mm_nt(dy, wg_out, "dcat", 1024, 2048, 2048)
    g_send = _grad_rows(cat, dy,1 - core, BF16, "grad_w_out_send")
    sent = comm.sibling_start(g_send.reshape(NCHIP, D // NCHIP // 2, D), "a")
    g_keep = _grad_rows(cat, dy,core, BF16, "grad_w_out_keep", dep=sent[-1])
    red_a = comm.reduce_start_summed(g_keep.reshape(NCHIP, D // NCHIP // 2, D), sent, "a")
    da, lse, delta, dproj = _attn_mix_bwd(dcat, proj, os_, ls_, dep=red_a[-1])
    for g in range(3):
        dproj = _attn_bwd(proj, fw[g][2], fw[g][3], cosa, sina, da, lse, delta, dproj, g)
    dproj, dmk, dmv = _cross_bwd(dcat, proj, mkv, cosq, sinq, cosm, sinm, dproj)
    dmkv = jnp.concatenate([dmk, dmv], axis=-1).reshape(NB * MLEN, 2 * XW)
    dmkv_b = dmkv.astype(BF16)
    gk_send = _grad_rows(memn, dmkv_b, 1 - core, BF16, "grad_w_mem_kv_send")
    sent = comm.sibling_start(gk_send.reshape(NCHIP, D // NCHIP // 2, 2 * XW), "k")
    gk_keep = _grad_rows(memn, dmkv_b, core, BF16, "grad_w_mem_kv_keep", dep=sent[-1])
    dmemn = _mm_nt(dmkv_b, wg_kv, "dmemn", NB * MLEN, 1024, 512)
    d_mem = _norm_gain_grad(dmemn, mem2, "mem_norm_bwd")
    red_k = comm.reduce_start_summed(gk_keep.reshape(NCHIP, D // NCHIP // 2, 2 * XW), sent, "k", after=d_mem)
    dproj, d_conv = _conv_bwd(dcat, proj, conv_w, dproj, dep=red_k[-1])
    (r_out,) = comm.reduce_finish(red_a, "a", after=d_conv)
    (r_kv,) = comm.reduce_finish(red_k, "k", after=d_conv)

    g_send = _grad_w_in(h, dproj, 1 - core, BF16, "grad_w_in_send", dep=r_kv)
    sent = comm.sibling_start(g_send.reshape(NCHIP, D // 2, WB), "b")
    q = _grad_w_in(h, dproj, core, BF16, "grad_w_in_keep_0", dep=sent[-1], tiles=(0, 1))
    got = comm.sibling_wait(sent, [q], "b")
    q = _pair_sum_rows(q.reshape(NCHIP, D // 2, WB), got, "pair_sum_b", rows=(0, GW_TM))
    red_b0 = comm.scatter_rows(q, "b0", (0, GW_TM))
    q = _grad_w_in(h, dproj, core, BF16, "grad_w_in_keep_1", dep=red_b0[4], tiles=(1, 1),
                   addend=got.reshape(NCHIP * D // 2, WB), prev=red_b0[2][0].reshape(NCHIP * D // 2, WB))
    red_b1 = comm.scatter_rows(q.reshape(NCHIP, D // 2, WB), "b1", (GW_TM, GW_TM), before=red_b0)
    sets = [] if moments is None else [(moments["w_out"][0], r_out, *moments["w_out"][1:]),
                                      (moments["w_mem_kv"][0], r_kv, *moments["w_mem_kv"][1:])]
    dh, updates = _dh(dproj, wg_in, dep=red_b1[4], adam=sets)
    r_in = comm.reduce_finish_start([red_b0, red_b1], "b", after=dh)
    grad_x, d_pre = _pre_norm_bwd(dh, x2, pre_norm, dout, dep=r_in[-1])
    return loss, grad_x.reshape(NB, S, D), d_pre, d_mem, d_post, d_conv, r_in, (r_kv, r_out), updates


def _adamw(w, g, m, v, name):
    rows, cols = w.shape
    tr = rows if rows <= 512 else 512
    tc = cols if cols <= 1024 else 1024
    if cols % tc:
        tc = 896

    blk = pl.BlockSpec((tr, tc), lambda i, j: (i, j))
    sds = jax.ShapeDtypeStruct((rows, cols), F32)
    def body(*refs):
        _adamw_math(*refs)

    return pl.pallas_call(
        body, name=name, grid=(rows // tr, cols // tc),
        in_specs=[blk] * 4, out_specs=[blk] * 4, out_shape=[sds] * 4,
        compiler_params=_cp(("parallel", "parallel")),
    )(w, g, m, v)


def _place():
    return lax.axis_index("x"), lax.axis_index("y"), lax.axis_index("c")


def _other_chips(x, y):
    return [(1 - x, y), (x, 1 - y), (1 - x, 1 - y)]


def _tile_cols(cols):
    return cols if cols <= 1024 else (1024 if cols % 1024 == 0 else 896)


def _cast_own(w, chip, name, half=None, prev=None):
    rows, cols = w.shape
    tr, tc = 512, _tile_cols(cols)
    nrt = rows // tr if half is None else rows // tr // 2
    index = jnp.stack([chip, 0 if half is None else half]).astype(jnp.int32)
    extra = [] if prev is None else [prev]

    def body(ix_ref, w_ref, *rest):
        rest[-1][...] = w_ref[...].astype(BF16)

    grid_spec = pltpu.PrefetchScalarGridSpec(
        num_scalar_prefetch=1, grid=(nrt, cols // tc),
        in_specs=[pl.BlockSpec((tr, tc), lambda i, j, ix: (ix[1] * nrt + i, j))] + [ANY_SPEC] * len(extra),
        out_specs=pl.BlockSpec((None, tr, tc), lambda i, j, ix: (ix[0], ix[1] * nrt + i, j)))
    return pl.pallas_call(
        body, name=name, grid_spec=grid_spec,
        out_shape=jax.ShapeDtypeStruct((NCHIP, rows, cols), BF16),
        input_output_aliases={} if prev is None else {2: 0},
        compiler_params=_cp(("parallel", "parallel")),
    )(index, w, *extra)


HBM_SPEC = pl.BlockSpec(memory_space=pltpu.HBM)
SEM_SPEC = pl.BlockSpec(memory_space=pltpu.SEMAPHORE)
ANY_SPEC = pl.BlockSpec(memory_space=pl.ANY)
EFFECT = pltpu.SideEffectType.DATAFLOW_SIDE_EFFECTING
TOKEN = jax.ShapeDtypeStruct((8, 128), F32)


def _half(ref, chip, hc):
    hr = ref.shape[1] // 2
    return ref.at[chip, pl.ds(hc * hr, hr), :]


NEAR = (0, 1)
FAR = (2,)
REL_XOR = (2, 1, 3)


def _gather_copies(refs, send_sems, recv_sems, rels):
    x, y, c = _place()
    chips = _other_chips(x, y)
    out, inc = [], []
    for a, ref in enumerate(refs):
        for p, j in enumerate(rels):
            px, py = chips[j]
            mine = _half(ref, 2 * x + y, c)
            theirs = _half(ref, 2 * px + py, c)
            sems = dict(send_sem=send_sems.at[len(rels) * a + p], recv_sem=recv_sems.at[len(rels) * a + p],
                        device_id=(px, py, c), device_id_type=MESH)
            out.append(pltpu.make_async_remote_copy(src_ref=mine, dst_ref=mine, **sems))
            inc.append(pltpu.make_async_remote_copy(src_ref=theirs, dst_ref=theirs, **sems))
    return out, inc


def _gather_start(bufs, groups, name):
    n = len(bufs)
    ng = len(groups)

    def body(*refs):
        ins = refs[:n]
        token = refs[-1]
        for gi, rels in enumerate(groups):
            out, _ = _gather_copies(ins, refs[n + 2 * gi], refs[n + 2 * gi + 1], rels)
            for cp in out:
                cp.start()
        token[...] = jnp.zeros_like(token)

    sems = []
    for rels in groups:
        sems += [pltpu.SemaphoreType.DMA((len(rels) * n,))] * 2
    res = pl.pallas_call(
        body, name=name,
        in_specs=[HBM_SPEC] * n,
        out_specs=[SEM_SPEC] * (2 * ng) + [HBM_SPEC] * n + [pl.BlockSpec(memory_space=pltpu.VMEM)],
        out_shape=sems + [pltpu.HBM(b.shape, b.dtype) for b in bufs] + [TOKEN],
        input_output_aliases={a: 2 * ng + a for a in range(n)},
        compiler_params=pltpu.CompilerParams(has_side_effects=EFFECT),
    )(*[pltpu.with_memory_space_constraint(b, pltpu.HBM) for b in bufs])
    return [(res[2 * gi], res[2 * gi + 1]) for gi in range(ng)], list(res[2 * ng:2 * ng + n]), res[-1]


def _gather_wait(bufs, sems, rels, after, name):
    n = len(bufs)
    send_sems, recv_sems = sems
    after = list(after) if isinstance(after, (list, tuple)) else [after]

    def body(*refs):
        ins = refs[:n]
        out, inc = _gather_copies(ins, refs[n], refs[n + 1], rels)
        for cp in out:
            cp.wait_send()
        for cp in inc:
            cp.wait_recv()

    return pl.pallas_call(
        body, name=name,
        in_specs=[HBM_SPEC] * n + [SEM_SPEC, SEM_SPEC] + [ANY_SPEC] * len(after),
        out_specs=[HBM_SPEC] * n,
        out_shape=[pltpu.HBM(b.shape, b.dtype) for b in bufs],
        input_output_aliases={a: a for a in range(n)},
        compiler_params=pltpu.CompilerParams(has_side_effects=EFFECT),
    )(*bufs, send_sems, recv_sems, *after)


def _forward_halves(bufs, rels, name):
    n = len(bufs)

    def body(*refs):
        cps, waits = _forward_copies(refs[n:2 * n], rels, refs[2 * n], refs[2 * n + 1])
        for cp in cps:
            cp.start()
        for cp in waits:
            cp.wait_recv()
        for cp in cps:
            cp.wait_send()

    return pl.pallas_call(
        body, name=name,
        in_specs=[ANY_SPEC] * n, out_specs=[ANY_SPEC] * n,
        out_shape=[jax.ShapeDtypeStruct(s.shape, s.dtype) for s in bufs],
        input_output_aliases={a: a for a in range(n)},
        scratch_shapes=[pltpu.SemaphoreType.DMA((len(rels) * n,)), pltpu.SemaphoreType.DMA((len(rels) * n,))],
    )(*bufs)


def _forward_copies(refs, rels, send_sems, recv_sems):
    x, y, c = _place()
    chips = _other_chips(x, y)
    cps, waits = [], []
    for a, ref in enumerate(refs):
        for p, j in enumerate(rels):
            px, py = chips[j]
            sems = dict(send_sem=send_sems.at[len(rels) * a + p], recv_sem=recv_sems.at[len(rels) * a + p],
                        device_id=(x, y, 1 - c), device_id_type=MESH)
            got = _half(ref, 2 * px + py, c)
            want = _half(ref, 2 * px + py, 1 - c)
            cps.append(pltpu.make_async_remote_copy(src_ref=got, dst_ref=got, **sems))
            waits.append(pltpu.make_async_remote_copy(src_ref=want, dst_ref=want, **sems))
    return cps, waits


def _forward_start(bufs, rels, name):
    n = len(bufs)

    def body(*refs):
        cps, _ = _forward_copies(refs[:n], rels, refs[n], refs[n + 1])
        for cp in cps:
            cp.start()
        refs[-1][...] = jnp.zeros_like(refs[-1])

    res = pl.pallas_call(
        body, name=name,
        in_specs=[HBM_SPEC] * n,
        out_specs=[SEM_SPEC, SEM_SPEC] + [HBM_SPEC] * n + [pl.BlockSpec(memory_space=pltpu.VMEM)],
        out_shape=[pltpu.SemaphoreType.DMA((len(rels) * n,))] * 2 + [pltpu.HBM(b.shape, b.dtype) for b in bufs] + [TOKEN],
        input_output_aliases={a: 2 + a for a in range(n)},
        compiler_params=pltpu.CompilerParams(has_side_effects=EFFECT),
    )(*[pltpu.with_memory_space_constraint(b, pltpu.HBM) for b in bufs])
    return (res[0], res[1]), list(res[2:2 + n]), res[-1]


def _forward_wait(bufs, sems, rels, after, name):
    n = len(bufs)

    def body(*refs):
        cps, waits = _forward_copies(refs[:n], rels, refs[n], refs[n + 1])
        for cp in cps:
            cp.wait_send()
        for cp in waits:
            cp.wait_recv()

    return pl.pallas_call(
        body, name=name,
        in_specs=[HBM_SPEC] * n + [SEM_SPEC, SEM_SPEC] + [ANY_SPEC] * len(after),
        out_specs=[HBM_SPEC] * n,
        out_shape=[pltpu.HBM(b.shape, b.dtype) for b in bufs],
        input_output_aliases={a: a for a in range(n)},
        compiler_params=pltpu.CompilerParams(has_side_effects=EFFECT),
    )(*bufs, sems[0], sems[1], *after)


def _sibling_copy(src, land, send_sems, recv_sems):
    x, y, c = _place()
    return pltpu.make_async_remote_copy(src_ref=src, dst_ref=land, send_sem=send_sems.at[0], recv_sem=recv_sems.at[0],
                                        device_id=(x, y, 1 - c), device_id_type=MESH)


def _sibling_start(part, name):
    def body(src, land, send_sems, recv_sems, src_thru, land_thru, token):
        _sibling_copy(src, land, send_sems, recv_sems).start()
        token[...] = jnp.zeros_like(token)

    land = lax.empty(part.shape, part.dtype)
    return pl.pallas_call(
        body, name=name,
        in_specs=[HBM_SPEC] * 2,
        out_specs=[SEM_SPEC, SEM_SPEC, HBM_SPEC, HBM_SPEC, pl.BlockSpec(memory_space=pltpu.VMEM)],
        out_shape=[pltpu.SemaphoreType.DMA((1,)), pltpu.SemaphoreType.DMA((1,)), pltpu.HBM(part.shape, part.dtype),
                   pltpu.HBM(part.shape, part.dtype), TOKEN],
        input_output_aliases={0: 2, 1: 3},
        compiler_params=pltpu.CompilerParams(has_side_effects=EFFECT),
    )(pltpu.with_memory_space_constraint(part, pltpu.HBM), pltpu.with_memory_space_constraint(land, pltpu.HBM))


def _sibling_wait(state, after, name):
    send_sems, recv_sems, part, land, _ = state

    def body(src, land_ref, send_ref, recv_ref, *rest):
        cp = _sibling_copy(src, land_ref, send_ref, recv_ref)
        cp.wait_send()
        cp.wait_recv()

    return pl.pallas_call(
        body, name=name,
        in_specs=[HBM_SPEC, HBM_SPEC, SEM_SPEC, SEM_SPEC] + [ANY_SPEC] * len(after),
        out_specs=[HBM_SPEC, HBM_SPEC],
        out_shape=[pltpu.HBM(part.shape, part.dtype), pltpu.HBM(land.shape, land.dtype)],
        input_output_aliases={0: 0, 1: 1},
        compiler_params=pltpu.CompilerParams(has_side_effects=EFFECT),
    )(part, land, send_sems, recv_sems, *after)[1]


def _pair_sum_rows(keep, got, name, rows=None):
    nblk, nrows, cols = keep.shape
    tr, tc = 512, _tile_cols(cols)
    r0, nr = (0, nrows) if rows is None else rows

    def body(k_ref, g_ref, o_ref):
        o_ref[...] = (k_ref[...].astype(F32) + g_ref[...].astype(F32)).astype(BF16)

    blk = pl.BlockSpec((None, tr, tc), lambda b, i, j: (b, r0 // tr + i, j))
    return pl.pallas_call(
        body, name=name, grid=(nblk, nr // tr, cols // tc),
        in_specs=[blk, blk], out_specs=blk,
        out_shape=jax.ShapeDtypeStruct(keep.shape, BF16),
        input_output_aliases={} if rows is None else {0: 0},
        compiler_params=_cp(("parallel",) * 3),
    )(keep, got)


def _scatter_start(qs, name, rows=None, lands=None):
    n = len(qs)

    def body(*refs):
        ins, lands = refs[:n], refs[n:2 * n]
        token = refs[-1]
        for cp in _scatter_copies(ins, lands, refs[2 * n], refs[2 * n + 1], rows):
            cp.start()
        token[...] = jnp.zeros_like(token)

    if lands is None:
        lands = [lax.empty((3,) + q.shape[1:], q.dtype) for q in qs]
    res = pl.pallas_call(
        body, name=name,
        in_specs=[HBM_SPEC] * (2 * n),
        out_specs=[SEM_SPEC, SEM_SPEC] + [HBM_SPEC] * (2 * n) + [pl.BlockSpec(memory_space=pltpu.VMEM)],
        out_shape=[pltpu.SemaphoreType.DMA((3 * n,)), pltpu.SemaphoreType.DMA((3 * n,))]
        + [pltpu.HBM(b.shape, b.dtype) for b in qs + lands] + [TOKEN],
        input_output_aliases={a: 2 + a for a in range(2 * n)},
        compiler_params=pltpu.CompilerParams(has_side_effects=EFFECT),
    )(*[pltpu.with_memory_space_constraint(b, pltpu.HBM) for b in qs + lands])
    return res[0], res[1], list(res[2:2 + n]), list(res[2 + n:2 + 2 * n]), res[-1]


def _scatter_copies(ins, lands, send_sems, recv_sems, rows=None):
    x, y, c = _place()
    sel = slice(None) if rows is None else pl.ds(rows[0], rows[1])
    cps = []
    for a in range(len(ins)):
        for j, (px, py) in enumerate(_other_chips(x, y)):
            cps.append(pltpu.make_async_remote_copy(
                src_ref=ins[a].at[2 * px + py, sel, :], dst_ref=lands[a].at[j, sel, :],
                send_sem=send_sems.at[3 * a + j], recv_sem=recv_sems.at[3 * a + j], device_id=(px, py, c), device_id_type=MESH))
    return cps


def _scatter_wait(qs, lands, send_sems, recv_sems, after, name, rows=None):
    n = len(qs)

    def body(*refs):
        for cp in _scatter_copies(refs[:n], refs[n:2 * n], refs[2 * n], refs[2 * n + 1], rows):
            cp.wait_send()
            cp.wait_recv()

    res = pl.pallas_call(
        body, name=name,
        in_specs=[HBM_SPEC] * (2 * n) + [SEM_SPEC, SEM_SPEC, ANY_SPEC],
        out_specs=[HBM_SPEC] * (2 * n),
        out_shape=[pltpu.HBM(b.shape, b.dtype) for b in qs + lands],
        input_output_aliases={a: a for a in range(2 * n)},
        compiler_params=pltpu.CompilerParams(has_side_effects=EFFECT),
    )(*qs, *lands, send_sems, recv_sems, after)
    return list(res[:n]), list(res[n:])


def _chip_sum(q, got, name):
    _, hr, cols = got.shape
    tr, tc = 512, _tile_cols(cols)
    chip = 2 * lax.axis_index("x") + lax.axis_index("y")
    c = lax.axis_index("c")

    def body(idx_ref, q_ref, g_ref, o_ref):
        del idx_ref
        acc = q_ref[...].astype(F32)
        for i in range(3):
            acc = acc + g_ref[i].astype(F32)
        o_ref[...] = acc

    grid_spec = pltpu.PrefetchScalarGridSpec(
        num_scalar_prefetch=1, grid=(hr // tr, cols // tc),
        in_specs=[pl.BlockSpec((None, tr, tc), lambda i, j, ix: (ix[0], i, j)),
                  pl.BlockSpec((3, tr, tc), lambda i, j, ix: (0, i, j))],
        out_specs=pl.BlockSpec((None, tr, tc), lambda i, j, ix: (ix[1], i, j)))
    return pl.pallas_call(
        body, name=name, grid_spec=grid_spec,
        out_shape=jax.ShapeDtypeStruct((2, hr, cols), F32),
        compiler_params=_cp(("parallel", "parallel")),
    )(jnp.stack([chip, c]).astype(jnp.int32), q, got)


def _join_halves(bufs, name):
    n = len(bufs)

    def body(*refs):
        outs = refs[n:2 * n]
        send_sems, recv_sems = refs[2 * n:]
        x, y, c = _place()
        cps = []
        for a in range(n):
            cps.append(pltpu.make_async_remote_copy(src_ref=outs[a].at[c], dst_ref=outs[a].at[c], send_sem=send_sems.at[a],
                                                    recv_sem=recv_sems.at[a], device_id=(x, y, 1 - c), device_id_type=MESH))
        for cp in cps:
            cp.start()
        for a in range(n):
            theirs = outs[a].at[1 - c]
            pltpu.make_async_remote_copy(src_ref=theirs, dst_ref=theirs, send_sem=send_sems.at[a], recv_sem=recv_sems.at[a],
                                         device_id=(x, y, 1 - c), device_id_type=MESH).wait_recv()
        for cp in cps:
            cp.wait_send()

    hbm = pl.BlockSpec(memory_space=pl.ANY)
    return pl.pallas_call(
        body, name=name,
        in_specs=[hbm] * n, out_specs=[hbm] * n,
        out_shape=[jax.ShapeDtypeStruct(b.shape, b.dtype) for b in bufs],
        input_output_aliases={a: a for a in range(n)},
        scratch_shapes=[pltpu.SemaphoreType.DMA((n,)), pltpu.SemaphoreType.DMA((n,))],
    )(*bufs)


def _join_copy(buf, send_sems, recv_sems):
    x, y, c = _place()
    sems = dict(send_sem=send_sems.at[0], recv_sem=recv_sems.at[0], device_id=(x, y, 1 - c), device_id_type=MESH)
    return (pltpu.make_async_remote_copy(src_ref=buf.at[c], dst_ref=buf.at[c], **sems),
            pltpu.make_async_remote_copy(src_ref=buf.at[1 - c], dst_ref=buf.at[1 - c], **sems))


def _join_start(buf, name):
    def body(b_ref, send_sems, recv_sems, thru, token):
        _join_copy(b_ref, send_sems, recv_sems)[0].start()
        token[...] = jnp.zeros_like(token)

    return pl.pallas_call(
        body, name=name,
        in_specs=[HBM_SPEC],
        out_specs=[SEM_SPEC, SEM_SPEC, HBM_SPEC, pl.BlockSpec(memory_space=pltpu.VMEM)],
        out_shape=[pltpu.SemaphoreType.DMA((1,)), pltpu.SemaphoreType.DMA((1,)), pltpu.HBM(buf.shape, buf.dtype), TOKEN],
        input_output_aliases={0: 2},
        compiler_params=pltpu.CompilerParams(has_side_effects=EFFECT),
    )(pltpu.with_memory_space_constraint(buf, pltpu.HBM))


def _join_wait(state, after, name):
    send_sems, recv_sems, buf, _ = state

    def body(b_ref, send_ref, recv_ref, *rest):
        out, inc = _join_copy(b_ref, send_ref, recv_ref)
        out.wait_send()
        inc.wait_recv()

    return pl.pallas_call(
        body, name=name,
        in_specs=[HBM_SPEC, SEM_SPEC, SEM_SPEC] + [ANY_SPEC] * len(after),
        out_specs=HBM_SPEC,
        out_shape=pltpu.HBM(buf.shape, buf.dtype),
        input_output_aliases={0: 0},
        compiler_params=pltpu.CompilerParams(has_side_effects=EFFECT),
    )(buf, send_sems, recv_sems, *after)


N_DEV = 8


def _gather_small(block, reduce, name, deps=()):
    m_per, cols = block.shape
    deps = list(deps)

    def body(x_ref, *rest):
        out_ref, all_ref, send_sems, recv_sems, local_sem = rest[len(deps):]
        x, y, c = _place()
        me, sibling = (x, y, c), (x, y, 1 - c)
        chips = _other_chips(x, y)

        def rows(px, py, pc):
            return all_ref.at[pl.ds((4 * px + 2 * py + pc) * m_per, m_per), :]

        def copy(k, block_of, to, src=None):
            return pltpu.make_async_remote_copy(
                src_ref=rows(*block_of) if src is None else src, dst_ref=rows(*block_of),
                send_sem=send_sems.at[k], recv_sem=recv_sems.at[k], device_id=to, device_id_type=MESH)

        mine = pltpu.make_async_copy(x_ref, rows(*me), local_sem)
        mine.start()
        first = [copy(0, me, sibling, src=x_ref)]
        first += [copy(1 + j, me, (*chip, c), src=x_ref) for j, chip in enumerate(chips)]
        for cp in first:
            cp.start()
        passed = [copy(4 + j, (*chip, c), sibling) for j, chip in enumerate(chips)]
        for j, chip in enumerate(chips):
            copy(1 + j, (*chip, c), me).wait_recv()
            passed[j].start()
        copy(0, sibling, me).wait_recv()
        for j, chip in enumerate(chips):
            copy(4 + j, (*chip, 1 - c), me).wait_recv()
        for cp in first + passed:
            cp.wait_send()
        mine.wait()
        if reduce:
            acc = all_ref[pl.ds(0, m_per), :]
            for i in range(1, N_DEV):
                acc = acc + all_ref[pl.ds(i * m_per, m_per), :]
            out_ref[...] = acc
        else:
            out_ref[...] = all_ref[...]

    out_rows = m_per if reduce else N_DEV * m_per
    return pl.pallas_call(
        body, name=name,
        in_specs=[pl.BlockSpec(memory_space=pltpu.VMEM)] + [pl.BlockSpec(memory_space=pl.ANY)] * len(deps),
        out_specs=pl.BlockSpec(memory_space=pltpu.VMEM),
        out_shape=jax.ShapeDtypeStruct((out_rows, cols), F32),
        scratch_shapes=[pltpu.VMEM((N_DEV * m_per, cols), F32), pltpu.SemaphoreType.DMA((7,)),
                        pltpu.SemaphoreType.DMA((7,)), pltpu.SemaphoreType.DMA],
    )(block, *deps)


class _Comm:
    def __init__(self, w_in, w_kv, w_out, chip, core):
        self.w_in, self.chip, self.core = w_in, chip, core
        self.bufs = [_cast_own(w_in, chip, "cast_w_in_sent", half=core), _cast_own(w_kv, chip, "cast_w_mem_kv"),
                     _cast_own(w_out, chip, "cast_w_out")]


    def gather_started(self):
        (self.sems,), (b_in,), tok = _gather_start(self.bufs[:1], (NEAR,), "gather_start_in_near")
        self.b_in = _cast_own(self.w_in, self.chip, "cast_w_in_kept", half=1 - self.core, prev=b_in)
        return [tok]

    def w_in_own(self):
        return self.b_in.reshape(NCHIP * D, WB)

    def w_in_near(self, after):
        (b_in,) = _gather_wait([self.b_in], self.sems, NEAR, list(after) + self.bufs[1:], "gather_wait_in_near")
        (b_in,) = _forward_halves([b_in], NEAR, "forward_in_near")
        (self.sems,), (self.b_in,), tok = _gather_start([b_in], (FAR,), "gather_start_in_far")
        return self.b_in.reshape(NCHIP * D, WB), tok

    def w_in_all(self, after):
        (b_in,) = _gather_wait([self.b_in], self.sems, FAR, after, "gather_wait_in_far")
        (b_in,) = _forward_halves([b_in], FAR, "forward_in_far")
        (self.sems,), self.b_rest, tok = _gather_start(self.bufs[1:], (NEAR + FAR,), "gather_start_rest")
        return b_in.reshape(NCHIP * D, WB), tok

    def w_rest_landed(self, after):
        b_rest = _gather_wait(self.b_rest, self.sems, NEAR + FAR, after, "gather_wait_rest")
        self.sems, self.b_rest, tok = _forward_start(b_rest, NEAR + FAR, "forward_start_rest")
        return tok

    def w_rest(self, after):
        b_kv, b_out = _forward_wait(self.b_rest, self.sems, NEAR + FAR, after, "forward_wait_rest")
        return b_kv.reshape(D, 2 * XW), b_out.reshape(D, D)

    def sibling_start(self, send, tag):
        return _sibling_start(send, "sibling_start_" + tag)

    def reduce_start_summed(self, keep, sent, tag, after=None):
        got = _sibling_wait(sent, [keep] + ([] if after is None else [after]), "sibling_wait_" + tag)
        return _scatter_start([_pair_sum_rows(keep, got, "pair_sum_" + tag)], "scatter_start_" + tag)

    def sibling_wait(self, sent, after, tag):
        return _sibling_wait(sent, after, "sibling_wait_" + tag)

    def scatter_rows(self, q, tag, rows, before=None):
        lands = None if before is None else before[3]
        return _scatter_start([q], "scatter_start_" + tag, rows=rows, lands=lands) + (rows,)

    def reduce_finish_start(self, states, tag, after):
        qs, lands = states[-1][2], states[-1][3]
        for i, st in enumerate(states):
            qs, lands = _scatter_wait(qs, lands, st[0], st[1], after, f"scatter_wait_{tag}{i}", rows=st[5])
        return _join_start(_chip_sum(qs[0], lands[0], f"chip_sum_{tag}0"), "join_start_" + tag)

    def reduce_finish_wait(self, pending, tag, after):
        j = _join_wait(pending, after, "join_wait_" + tag)
        return j.reshape(2 * j.shape[1], j.shape[2])

    def reduce_finish(self, state, tag, after):
        send_sems, recv_sems, qs, lands, _ = state
        qs, lands = _scatter_wait(qs, lands, send_sems, recv_sems, after, "scatter_wait_" + tag)
        halves = [_chip_sum(q, l, f"chip_sum_{tag}{i}") for i, (q, l) in enumerate(zip(qs, lands))]
        return [j.reshape(2 * j.shape[1], j.shape[2]) for j in _join_halves(halves, "join_halves_" + tag)]


def kernel(x, mem, pre_norm, w_in, conv_w, mem_norm, w_mem_kv, w_out, post_norm, loss_target, m_pre_norm, m_w_in, m_conv_w, m_mem_norm, m_w_mem_kv, m_w_out, m_post_norm, v_pre_norm, v_w_in, v_conv_w, v_mem_norm, v_w_mem_kv, v_w_out, v_post_norm):
    chip = 2 * lax.axis_index("x") + lax.axis_index("y")

    comm = _Comm(w_in[0], w_mem_kv[0], w_out[0], chip, lax.axis_index("c"))
    cw_blk = jnp.zeros((8, 384), F32).at[:3].set(conv_w[0])
    cw_all = _gather_small(cw_blk, False, "gather_conv_w", deps=comm.bufs[1:]).reshape(NCHIP, 2, 8, 384)[:, 0, :3]
    conv_full = jnp.transpose(cw_all, (1, 0, 2)).reshape(3, CW)
    moments = {"w_out": (w_out[0], m_w_out[0], v_w_out[0]), "w_mem_kv": (w_mem_kv[0], m_w_mem_kv[0], v_w_mem_kv[0])}
    loss, grad_x, d_pre, d_mem, d_post, d_conv, r_in, _, (upd_out, upd_kv) = _local_step(
        x, mem, pre_norm, conv_full, mem_norm, post_norm, loss_target, chip, lax.axis_index("c"), comm, moments)

    pack = jnp.concatenate([d_pre, d_mem, d_post, jnp.pad(d_conv, ((0, 0), (0, D - CW))),
                            jnp.pad(loss, ((0, 0), (0, D - 128))), jnp.zeros((1, D), F32)], axis=0)
    tot = _gather_small(pack, True, "reduce_small")
    g_pre, g_mem, g_post = tot[0:1], tot[1:2], tot[2:3]
    g_conv = lax.dynamic_slice(tot[3:6, :CW], (0, chip * 384), (3, 384))
    loss_out = tot[6, 0]

    names = ("pre_norm", "w_in", "conv_w", "mem_norm", "w_mem_kv", "w_out", "post_norm")
    ws = (pre_norm, w_in[0], conv_w[0], mem_norm, w_mem_kv[0], w_out[0], post_norm)
    gs = [g_pre, None, g_conv, g_mem, None, None, g_post]
    ms = (m_pre_norm, m_w_in[0], m_conv_w[0], m_mem_norm, m_w_mem_kv[0], m_w_out[0], m_post_norm)
    vs = (v_pre_norm, v_w_in[0], v_conv_w[0], v_mem_norm, v_w_mem_kv[0], v_w_out[0], v_post_norm)
    upd = [None if g is None else _adamw(w, g, m, v, "adamw_" + nm) for nm, w, g, m, v in zip(names, ws, gs, ms, vs)]
    upd[4], upd[5] = upd_kv, upd_out
    g_in = comm.reduce_finish_wait(r_in, "b", after=[u[1] for u in upd if u is not None])
    upd[1] = _adamw(ws[1], g_in, ms[1], vs[1], "adamw_w_in")

    def shaped(arrs):
        return [a.reshape(w.shape) if w.ndim == a.ndim else a.reshape((1,) + a.shape)
                for a, w in zip(arrs, (pre_norm, w_in, conv_w, mem_norm, w_mem_kv, w_out, post_norm))]

    grads = shaped([u[0] for u in upd])
    deltas = shaped([u[1] for u in upd])
    new_m = shaped([u[2] for u in upd])
    new_v = shaped([u[3] for u in upd])
    return (loss_out, grad_x, *grads, *deltas, *new_m, *new_v)
```

```python
import jax
import jax.numpy as jnp
from jax import lax
from jax.experimental import pallas as pl
from jax.experimental.pallas import tpu as pltpu

F32 = jnp.float32
BF16 = jnp.bfloat16

D = 4096
S = 2048
NB = 2
T = NB * S
MLEN = 256
HD = 128
AW = 1536
CW = 1536
XW = 1024
XHD = 256
NXH = 4
NC = 14336
QA, KA, VA, ZA, UC, BC, CC, ZC, QX, ZX = 0, 1536, 3072, 4608, 6144, 7680, 9216, 10752, 12288, 13312
NCHIP = 4
WB = NC // NCHIP
DIL = (1, 4, 16)
HPG = 4
EPS = 1e-6
NEG = -1e30
ROPE_THETA = 10000.0
A_SCALE = HD ** -0.5
X_SCALE = XHD ** -0.5

ADAM_LR = 0.001
ADAM_B1 = 0.9
ADAM_B2 = 0.999
ADAM_EPS = 1e-08
ADAM_WD = 0.01
ADAM_STEP = 10

MESH = pl.DeviceIdType.MESH
MIB = 1024 * 1024


def _cp(sem, vmem_mib=48):
    return pltpu.CompilerParams(dimension_semantics=sem, vmem_limit_bytes=vmem_mib * MIB)


def _sigmoid(z):
    return 1.0 / (1.0 + jnp.exp(-z))


def _rope(x, cos, sin, half):
    return x * cos + pltpu.roll(x, half, 1) * sin


def _rope_t(g, cos, sin, half):
    return g * cos + pltpu.roll(g * sin, half, 1)


def _rms_fwd(x2, g, name, dep=None):
    rows = x2.shape[0]
    tr = 256
    deps = [] if dep is None else list(dep)

    def body(x_ref, g_ref, *rest):
        o_ref = rest[-1]
        x = x_ref[...]
        r = lax.rsqrt(jnp.mean(x * x, axis=-1, keepdims=True) + EPS)
        o_ref[...] = (x * r * g_ref[...]).astype(BF16)

    return pl.pallas_call(
        body, name=name, grid=(rows // tr,),
        in_specs=[pl.BlockSpec((tr, D), lambda i: (i, 0)), pl.BlockSpec((1, D), lambda i: (0, 0))]
        + [pl.BlockSpec(memory_space=pl.ANY)] * len(deps),
        out_specs=pl.BlockSpec((tr, D), lambda i: (i, 0)),
        out_shape=jax.ShapeDtypeStruct((rows, D), BF16),
        compiler_params=_cp(("parallel",)),
    )(x2, g, *deps)


def _norm_gain_grad(dn, x2, name):
    rows = x2.shape[0]
    tr = 256

    def body(dn_ref, x_ref, dg_ref):
        @pl.when(pl.program_id(0) == 0)
        def _():
            dg_ref[...] = jnp.zeros_like(dg_ref)
        x = x_ref[...]
        r = lax.rsqrt(jnp.mean(x * x, axis=-1, keepdims=True) + EPS)
        dg_ref[...] += jnp.sum(dn_ref[...] * (x * r), axis=0, keepdims=True)

    return pl.pallas_call(
        body, name=name, grid=(rows // tr,),
        in_specs=[pl.BlockSpec((tr, D), lambda i: (i, 0)), pl.BlockSpec((tr, D), lambda i: (i, 0))],
        out_specs=pl.BlockSpec((1, D), lambda i: (0, 0)),
        out_shape=jax.ShapeDtypeStruct((1, D), F32),
        compiler_params=_cp(("arbitrary",)),
    )(dn, x2)


def _pre_norm_bwd(dh, x2, g, dout, dep=None):
    tr = 256
    deps = [] if dep is None else [dep]

    def body(dh_ref, x_ref, g_ref, dout_ref, *rest):
        gx_ref, dg_ref = rest[len(deps):]

        @pl.when(pl.program_id(0) == 0)
        def _():
            dg_ref[...] = jnp.zeros_like(dg_ref)

        x = x_ref[...]
        dh_ = dh_ref[...].astype(F32)
        r = lax.rsqrt(jnp.mean(x * x, axis=-1, keepdims=True) + EPS)
        xhat = x * r
        dg_ref[...] += jnp.sum(dh_ * xhat, axis=0, keepdims=True)
        dxn = dh_ * g_ref[...]
        gx_ref[...] = dout_ref[...].astype(F32) + r * (dxn - xhat * jnp.mean(dxn * xhat, axis=-1, keepdims=True))

    row = pl.BlockSpec((tr, D), lambda i: (i, 0))
    vec = pl.BlockSpec((1, D), lambda i: (0, 0))
    return pl.pallas_call(
        body, name="pre_norm_bwd", grid=(T // tr,),
        in_specs=[row, row, vec, row] + [pl.BlockSpec(memory_space=pl.ANY)] * len(deps),
        out_specs=[row, vec],
        out_shape=[jax.ShapeDtypeStruct((T, D), F32), jax.ShapeDtypeStruct((1, D), F32)],
        compiler_params=_cp(("arbitrary",)),
    )(dh, x2, g, dout, *deps)


def _post_norm_loss(y, x2, tgt, g):
    tr = 256

    def body(y_ref, x_ref, t_ref, g_ref, dy_ref, dout_ref, dg_ref, loss_ref):
        @pl.when(pl.program_id(0) == 0)
        def _():
            dg_ref[...] = jnp.zeros_like(dg_ref)
            loss_ref[...] = jnp.zeros_like(loss_ref)
        yv = y_ref[...]
        gv = g_ref[...]
        r = lax.rsqrt(jnp.mean(yv * yv, axis=-1, keepdims=True) + EPS)
        yhat = yv * r
        err = x_ref[...] + yhat * gv - t_ref[...]
        loss_ref[...] += jnp.sum(jnp.sum(err * err, axis=1, keepdims=True), axis=0, keepdims=True) * (0.5 / D)
        dout = err * (1.0 / D)
        dout_ref[...] = dout.astype(BF16)
        dg_ref[...] += jnp.sum(dout * yhat, axis=0, keepdims=True)
        dyn = dout * gv
        dy_ref[...] = (r * (dyn - yhat * jnp.mean(dyn * yhat, axis=-1, keepdims=True))).astype(BF16)

    row = pl.BlockSpec((tr, D), lambda i: (i, 0))
    vec = pl.BlockSpec((1, D), lambda i: (0, 0))
    return pl.pallas_call(
        body, name="post_norm_loss", grid=(T // tr,),
        in_specs=[row, row, row, vec],
        out_specs=[row, row, vec, pl.BlockSpec((1, 128), lambda i: (0, 0))],
        out_shape=[jax.ShapeDtypeStruct((T, D), BF16), jax.ShapeDtypeStruct((T, D), BF16),
                   jax.ShapeDtypeStruct((1, D), F32), jax.ShapeDtypeStruct((1, 128), F32)],
        compiler_params=_cp(("arbitrary",)),
    )(y, x2, tgt, g)


NN = (((1,), (0,)), ((), ()))
NT = (((1,), (1,)), ((), ()))
TN = (((0,), (0,)), ((), ()))


def _as_index(v):
    return jnp.reshape(v, (1,)).astype(jnp.int32)


def _matmul(a, b, *, name, dims, grid, a_block, a_map, b_block, b_map, o_block, o_map, out_shape, out_dtype=F32,
            index=None, prev=None, deps=(), addend=None):
    extra = ([] if prev is None else [prev]) + [d for d in deps if d is not None]
    first = 0 if index is None else 1
    nk = grid[2]
    in_place = out_dtype == F32
    assert addend is None or not in_place
    n_add = 0 if addend is None else 1

    def body(*refs):
        a_ref, b_ref, o_ref = refs[first], refs[first + 1], refs[first + 2 + n_add + len(extra)]
        acc_ref = o_ref if in_place else refs[-1]

        @pl.when(pl.program_id(2) == 0)
        def _():
            acc_ref[...] = lax.dot_general(a_ref[...], b_ref[...], dims, preferred_element_type=F32)

        @pl.when(pl.program_id(2) > 0)
        def _():
            acc_ref[...] += lax.dot_general(a_ref[...], b_ref[...], dims, preferred_element_type=F32)

        if not in_place:
            @pl.when(pl.program_id(2) == nk - 1)
            def _():
                total = acc_ref[...] if addend is None else acc_ref[...] + refs[first + 2][...].astype(F32)
                o_ref[...] = total.astype(o_ref.dtype)

    adds = [] if addend is None else [addend]
    in_specs = ([pl.BlockSpec(a_block, a_map), pl.BlockSpec(b_block, b_map)] + [pl.BlockSpec(o_block, o_map)] * n_add
                + [pl.BlockSpec(memory_space=pl.ANY)] * len(extra))
    out_specs = pl.BlockSpec(o_block, o_map)
    scratch = [] if in_place else [pltpu.VMEM(o_block, F32)]
    kwargs = dict(name=name, out_shape=jax.ShapeDtypeStruct(out_shape, out_dtype),
                  input_output_aliases={} if prev is None else {first + 2 + n_add: 0},
                  compiler_params=_cp(("parallel", "parallel", "arbitrary"), vmem_mib=56))
    if index is None:
        return pl.pallas_call(body, grid=grid, in_specs=in_specs, out_specs=out_specs, scratch_shapes=scratch,
                              **kwargs)(a, b, *adds, *extra)
    grid_spec = pltpu.PrefetchScalarGridSpec(num_scalar_prefetch=1, grid=grid, in_specs=in_specs, out_specs=out_specs,
                                             scratch_shapes=scratch)
    return pl.pallas_call(body, grid_spec=grid_spec, **kwargs)(_as_index(index), a, b, *adds, *extra)


def _mm_nn(a, b, name, tm, tn, tk):
    m, kd = a.shape
    n = b.shape[1]
    return _matmul(a, b, name=name, dims=NN, grid=(m // tm, n // tn, kd // tk),
                   a_block=(tm, tk), a_map=lambda i, j, k: (i, k),
                   b_block=(tk, tn), b_map=lambda i, j, k: (k, j),
                   o_block=(tm, tn), o_map=lambda i, j, k: (i, j), out_shape=(m, n))


def _mm_nt(a, b, name, tm, tn, tk):
    m, kd = a.shape
    n = b.shape[0]
    return _matmul(a, b, name=name, dims=NT, grid=(m // tm, n // tn, kd // tk),
                   a_block=(tm, tk), a_map=lambda i, j, k: (i, k),
                   b_block=(tn, tk), b_map=lambda i, j, k: (j, k),
                   o_block=(tm, tn), o_map=lambda i, j, k: (i, j), out_shape=(m, n))


W_TN = 1792
W_NJ = WB // W_TN


def _proj_part(h, wg, chip, masks, name, prev=None, dep=None):
    tm, tk = 1024, 2048

    def blk(j, ix):
        m = masks[0]
        for t in range(1, len(masks)):
            m = jnp.where(j // W_NJ == t, masks[t], m)
        return jnp.bitwise_xor(ix[0], m)

    return _matmul(h, wg, name=name, dims=NN, grid=(T // tm, len(masks) * W_NJ, D // tk), index=chip, prev=prev, deps=(dep,),
                   a_block=(tm, tk), a_map=lambda i, j, k, ix: (i, k),
                   b_block=(tk, W_TN), b_map=lambda i, j, k, ix: (blk(j, ix) * (D // tk) + k, j % W_NJ),
                   o_block=(tm, W_TN), o_map=lambda i, j, k, ix: (i, blk(j, ix) * W_NJ + j % W_NJ), out_shape=(T, NC),
                   out_dtype=BF16)


def _adamw_math(w_ref, g_ref, m_ref, v_ref, go_ref, d_ref, nm_ref, nv_ref):
    gv = g_ref[...]
    go_ref[...] = gv
    nm = ADAM_B1 * m_ref[...] + (1.0 - ADAM_B1) * gv
    nv = ADAM_B2 * v_ref[...] + (1.0 - ADAM_B2) * (gv * gv)
    m_hat = nm / (1.0 - ADAM_B1 ** ADAM_STEP)
    v_hat = nv / (1.0 - ADAM_B2 ** ADAM_STEP)
    d_ref[...] = -ADAM_LR * (m_hat / (jnp.sqrt(v_hat) + ADAM_EPS) + ADAM_WD * w_ref[...])
    nm_ref[...] = nm
    nv_ref[...] = nv


def _dh(dproj, wg, dep=None, adam=()):
    tm, tn = 1024, 2048
    grid = (T // tm, D // tn, NC // W_TN)
    nsteps = grid[0] * grid[1] * grid[2]
    deps = [] if dep is None else [dep]
    na = len(adam)
    nk = grid[2]

    def body(*refs):
        a_ref, b_ref = refs[0], refs[1]
        o_ref = refs[2 + 4 * na + len(deps)]
        acc_ref = refs[-1]

        @pl.when(pl.program_id(2) == 0)
        def _():
            acc_ref[...] = lax.dot_general(a_ref[...], b_ref[...], NT, preferred_element_type=F32)

        @pl.when(pl.program_id(2) > 0)
        def _():
            acc_ref[...] += lax.dot_general(a_ref[...], b_ref[...], NT, preferred_element_type=F32)

        @pl.when(pl.program_id(2) == nk - 1)
        def _():
            o_ref[...] = acc_ref[...].astype(BF16)

        for s in range(na):
            ins = refs[2 + 4 * s:6 + 4 * s]
            outs = refs[3 + 4 * na + len(deps) + 4 * s:7 + 4 * na + len(deps) + 4 * s]
            _adamw_math(*ins, *outs)

    def rows_of(arr):
        r, c = arr.shape
        return pl.BlockSpec((r // nsteps, c), lambda i, j, k: ((i * grid[1] + j) * grid[2] + k, 0))

    adam_specs = [rows_of(a) for st in adam for a in st]
    res = pl.pallas_call(
        body, name="dh", grid=grid,
        in_specs=[pl.BlockSpec((tm, W_TN), lambda i, j, k: (i, k)),
                  pl.BlockSpec((tn, W_TN), lambda i, j, k: ((k // W_NJ) * (D // tn) + j, k % W_NJ))]
        + adam_specs + [pl.BlockSpec(memory_space=pl.ANY)] * len(deps),
        out_specs=[pl.BlockSpec((tm, tn), lambda i, j, k: (i, j))] + adam_specs,
        out_shape=[jax.ShapeDtypeStruct((T, D), BF16)] + [jax.ShapeDtypeStruct(a.shape, F32) for st in adam for a in st],
        scratch_shapes=[pltpu.VMEM((tm, tn), F32)],
        compiler_params=_cp(("arbitrary",) * 3, vmem_mib=56),
    )(dproj, wg, *[a for st in adam for a in st], *deps)
    return res[0], [tuple(res[1 + 4 * s:5 + 4 * s]) for s in range(na)]


def _grad_rows(a, b, half, out_dtype, name, dep=None):
    kd, n = b.shape
    tm, tn, tk = D // NCHIP // 2, min(n, 2048), min(kd, 2048)
    return _matmul(a, b, name=name, dims=TN, grid=(NCHIP, n // tn, kd // tk), index=half, deps=(dep,),
                   a_block=(tk, tm), a_map=lambda i, j, k, ix: (k, 2 * i + ix[0]),
                   b_block=(tk, tn), b_map=lambda i, j, k, ix: (k, j),
                   o_block=(tm, tn), o_map=lambda i, j, k, ix: (i, j),
                   out_shape=(NCHIP * tm, n), out_dtype=out_dtype)


GW_TM = 1024


def _grad_w_in(h, dproj, half, out_dtype, name, dep=None, tiles=(0, 2), addend=None, prev=None):
    tm, tk = GW_TM, 2048
    nh = D // 2 // tm
    t0, nt = tiles

    def o_map(i, j, k, ix):
        return ((j // W_NJ) * nh + t0 + i, j % W_NJ)

    return _matmul(h, dproj, name=name, dims=TN, grid=(nt, NC // W_TN, T // tk), index=half, deps=(dep,), prev=prev,
                   a_block=(tk, tm), a_map=lambda i, j, k, ix: (k, ix[0] * nh + t0 + i),
                   b_block=(tk, W_TN), b_map=lambda i, j, k, ix: (k, j),
                   o_block=(tm, W_TN), o_map=o_map, addend=addend,
                   out_shape=(NCHIP * D // 2, WB), out_dtype=out_dtype)


def _rope_tables(pos, half):
    inv = 1.0 / (ROPE_THETA ** (jnp.arange(half, dtype=F32) / half))
    ang = pos.astype(F32)[:, None] * inv[None, :]
    cos, sin = jnp.cos(ang), jnp.sin(ang)
    return jnp.concatenate([cos, cos], axis=1), jnp.concatenate([-sin, sin], axis=1)


def _band_mask(r0):
    qi = lax.broadcasted_iota(jnp.int32, (128, 256), 0)
    kk = lax.broadcasted_iota(jnp.int32, (128, 256), 1)
    return (kk >= qi) & (kk <= qi + 128) & (kk + r0 >= 128)


def _window(r0, nblk):
    if nblk == 1:
        qi = lax.broadcasted_iota(jnp.int32, (128, 128), 0)
        kk = lax.broadcasted_iota(jnp.int32, (128, 128), 1)
        return pl.ds(128, 128), kk <= qi
    return pl.ds(r0, 256), _band_mask(r0)


def _dil_rows(r, n, d):
    if d == 1:
        return pl.ds(pl.multiple_of(n * 128, 128), 128)
    return pl.ds(r + d * 128 * n, 128, stride=d)


def _widen(refs, wide):
    if not wide:
        return refs

    def copy(n, carry):
        rows = pl.ds(pl.multiple_of(n * 256, 256), 256)
        for src, dst in zip(refs, wide):
            dst[rows, :] = src[rows, :].astype(F32)
        return carry

    lax.fori_loop(0, S // 256, copy, 0)
    return wide


def _attn_fwd(proj, cosf, sinf, g):
    d = DIL[g]
    ln = S // d
    nblk = ln // 128
    proj_v = proj.reshape(NB, S, NC)

    def body(q_ref, k_ref, v_ref, cos_ref, sin_ref, o_ref, l_ref, k_s, v_s, q_s, *wide):
        k_s[:, pl.ds(0, 128), :] = jnp.zeros((d, 128, HD), BF16)
        v_s[:, pl.ds(0, 128), :] = jnp.zeros((d, 128, HD), BF16)
        q_ref, k_ref, v_ref = _widen((q_ref, k_ref, v_ref), wide)

        def prep(i, carry):
            r, n = i // nblk, i % nblk
            rows = _dil_rows(r, n, d)
            dst = pl.ds(pl.multiple_of(n * 128 + 128, 128), 128)
            k_s[r, dst, :] = _rope(k_ref[rows, :].astype(F32), cos_ref[rows, :], sin_ref[rows, :], HD // 2).astype(BF16)
            v_s[r, dst, :] = v_ref[rows, :].astype(BF16)
            return carry

        lax.fori_loop(0, d * nblk, prep, 0, unroll=4)

        def step(i, carry):
            r, n = i // nblk, i % nblk
            rows = _dil_rows(r, n, d)
            r0 = pl.multiple_of(n * 128, 128)
            qr = _rope(q_ref[rows, :].astype(F32), cos_ref[rows, :], sin_ref[rows, :], HD // 2).astype(BF16)
            q_s[r, pl.ds(r0, 128), :] = qr
            win, mask = _window(r0, nblk)
            kw = k_s[r, win, :]
            vw = v_s[r, win, :]
            sc = lax.dot_general(qr, kw, NT, preferred_element_type=F32) * A_SCALE
            sc = jnp.where(mask, sc, NEG)
            m = jnp.max(sc, axis=1, keepdims=True)
            p = jnp.exp(sc - m)
            l = jnp.sum(p, axis=1, keepdims=True)
            o_ref[rows, :] = jnp.dot(p.astype(BF16), vw, preferred_element_type=F32) / l
            l_ref[rows, :] = jnp.broadcast_to(m + jnp.log(l), (128, HD))
            return carry

        lax.fori_loop(0, d * nblk, step, 0, unroll=4)

    def col(off):
        return lambda b, h: (b, 0, off // HD + HPG * g + h)

    blk = (None, S, HD)
    tab = pl.BlockSpec((S, HD), lambda b, h: (0, 0))
    out = pl.BlockSpec(blk, lambda b, h: (b, 0, h))
    kv = pl.BlockSpec((None, None, d, ln + 128, HD), lambda b, h: (b, h, 0, 0, 0))
    qq = pl.BlockSpec((None, None, d, ln, HD), lambda b, h: (b, h, 0, 0, 0))
    o, l, kr, vr, qr = pl.pallas_call(
        body, name=f"attn_fwd_d{d}", grid=(NB, HPG),
        in_specs=[pl.BlockSpec(blk, col(QA)), pl.BlockSpec(blk, col(KA)), pl.BlockSpec(blk, col(VA)), tab, tab],
        out_specs=[out, out, kv, kv, qq],
        out_shape=[jax.ShapeDtypeStruct((NB, S, HPG * HD), F32)] * 2
        + [jax.ShapeDtypeStruct((NB, HPG, d, ln + 128, HD), BF16)] * 2 + [jax.ShapeDtypeStruct((NB, HPG, d, ln, HD), BF16)],
        scratch_shapes=[pltpu.VMEM((S, HD), F32)] * (3 if d > 1 else 0),
        compiler_params=_cp(("parallel", "parallel")),
    )(proj_v, proj_v, proj_v, cosf, sinf)
    return o.reshape(T, HPG * HD), l.reshape(T, HPG * HD), (qr, kr, vr)


def _attn_bwd(qkv, cosf, sinf, da, lse, delta, dproj, g):
    d = DIL[g]
    ln = S // d
    nblk = ln // 128
    dproj_v = dproj.reshape(NB, S, NC)
    da_v = da.reshape(NB, S, AW)
    lse_v = lse.reshape(NB, S, HPG * HD)
    delta_v = delta.reshape(NB, S, HPG * HD)

    def body(q_s, k_s, v_s, cos_ref, sin_ref, da_ref, lse_ref, dl_ref, dp_in_ref, o_ref, stg, dk_s, dv_s):
        del dp_in_ref
        w = pl.program_id(2)

        def emit():
            def cast(n, carry):
                rows = pl.ds(pl.multiple_of(n * 256, 256), 256)
                o_ref[rows, :] = stg[rows, :].astype(BF16)
                return carry

            lax.fori_loop(0, S // 256, cast, 0)

        @pl.when(w == 0)
        def _():
            dk_s[...] = jnp.zeros_like(dk_s)
            dv_s[...] = jnp.zeros_like(dv_s)

            def step(i, carry):
                r, n = i // nblk, i % nblk
                rows = _dil_rows(r, n, d)
                r0 = pl.multiple_of(n * 128, 128)
                win, mask = _window(r0, nblk)
                cos, sin = cos_ref[rows, :], sin_ref[rows, :]
                qr = q_s[r, pl.ds(r0, 128), :]
                kw = k_s[r, win, :]
                vw = v_s[r, win, :]
                sc = lax.dot_general(qr, kw, NT, preferred_element_type=F32) * A_SCALE
                sc = jnp.where(mask, sc, NEG)
                p = jnp.exp(sc - lse_ref[rows, :][:, :1])
                da_b = da_ref[rows, :].astype(BF16)
                dp = lax.dot_general(da_b, vw, NT, preferred_element_type=F32)
                ds_b = (p * (dp - dl_ref[rows, :][:, :1]) * A_SCALE).astype(BF16)
                p_b = p.astype(BF16)
                dq = jnp.dot(ds_b, kw, preferred_element_type=F32)
                stg[rows, :] = _rope_t(dq, cos, sin, HD // 2)
                dk_s[r, win, :] += lax.dot_general(ds_b, qr, TN, preferred_element_type=F32)
                dv_s[r, win, :] += lax.dot_general(p_b, da_b, TN, preferred_element_type=F32)
                return carry

            lax.fori_loop(0, d * nblk, step, 0, unroll=4)
            emit()

        @pl.when(w == 1)
        def _():
            def put(i, carry):
                r, n = i // nblk, i % nblk
                rows = _dil_rows(r, n, d)
                src = pl.ds(pl.multiple_of(n * 128 + 128, 128), 128)
                stg[rows, :] = _rope_t(dk_s[r, src, :], cos_ref[rows, :], sin_ref[rows, :], HD // 2)
                return carry

            lax.fori_loop(0, d * nblk, put, 0, unroll=4)
            emit()

        @pl.when(w == 2)
        def _():
            def put(i, carry):
                r, n = i // nblk, i % nblk
                src = pl.ds(pl.multiple_of(n * 128 + 128, 128), 128)
                stg[_dil_rows(r, n, d), :] = dv_s[r, src, :]
                return carry

            lax.fori_loop(0, d * nblk, put, 0, unroll=4)
            emit()

    def col(off):
        return lambda b, h, w: (ahead(b, h, w)[0], 0, off // HD + HPG * g + ahead(b, h, w)[1])

    def ahead(b, h, w):
        flat = jnp.minimum(b * HPG + h + jnp.where(w > 0, 1, 0), NB * HPG - 1)
        return flat // HPG, flat % HPG

    blk = (None, S, HD)
    tab = pl.BlockSpec((S, HD), lambda b, h, w: (0, 0))
    per_head = pl.BlockSpec(blk, lambda b, h, w: (ahead(b, h, w)[0], 0, ahead(b, h, w)[1]))
    kv = pl.BlockSpec((None, None, d, ln + 128, HD), lambda b, h, w: (*ahead(b, h, w), 0, 0, 0))
    qq = pl.BlockSpec((None, None, d, ln, HD), lambda b, h, w: (*ahead(b, h, w), 0, 0, 0))
    out = pl.pallas_call(
        body, name=f"attn_bwd_d{d}", grid=(NB, HPG, 3),
        in_specs=[qq, kv, kv, tab, tab,
                  pl.BlockSpec(blk, col(0)), per_head, per_head, pl.BlockSpec(memory_space=pl.ANY)],
        out_specs=pl.BlockSpec(blk, lambda b, h, w: (b, 0, (AW // HD) * w + HPG * g + h)),
        out_shape=jax.ShapeDtypeStruct(dproj_v.shape, BF16),
        input_output_aliases={8: 0},
        scratch_shapes=[pltpu.VMEM((S, HD), F32), pltpu.VMEM((d, ln + 128, HD), F32), pltpu.VMEM((d, ln + 128, HD), F32)],
        compiler_params=_cp(("arbitrary",) * 3),
    )(*qkv, cosf, sinf, da_v, lse_v, delta_v, dproj_v)
    return out.reshape(T, NC)


def _attn_mix(proj, os_, ls_, dep=None):
    tr = 256
    gw = HPG * HD
    deps = [] if dep is None else [dep]

    def body(o0, o1, o2, l0, l1, l2, z_ref, *rest):
        cat_ref = rest[-1]
        m = jnp.maximum(jnp.maximum(l0[...], l1[...]), l2[...])
        e = [jnp.exp(l[...] - m) for l in (l0, l1, l2)]
        inv = 1.0 / (e[0] + e[1] + e[2])
        for gi, o in enumerate((o0, o1, o2)):
            z = z_ref[:, gi * gw:(gi + 1) * gw].astype(F32)
            cat_ref[:, gi * gw:(gi + 1) * gw] = (o[...] * (e[gi] * inv) * (z * _sigmoid(z))).astype(BF16)

    grp = pl.BlockSpec((tr, gw), lambda i: (i, 0))
    return pl.pallas_call(
        body, name="attn_mix", grid=(T // tr,),
        in_specs=[grp] * 6 + [pl.BlockSpec((tr, AW), lambda i: (i, ZA // AW))] + [ANY_SPEC] * len(deps),
        out_specs=pl.BlockSpec((tr, AW), lambda i: (i, 0)),
        out_shape=jax.ShapeDtypeStruct((T, D), BF16),
        compiler_params=_cp(("parallel",)),
    )(*os_, *ls_, proj, *deps)


def _attn_mix_bwd(dcat, proj, os_, ls_, dep=None):
    tr = 256
    gw = HPG * HD
    deps = [] if dep is None else [dep]

    def body(dy_ref, o0, o1, o2, l0, l1, l2, z_ref, *rest):
        da_ref, lse_ref, dl_ref, dz_ref = rest[len(deps):]
        m = jnp.maximum(jnp.maximum(l0[...], l1[...]), l2[...])
        e = [jnp.exp(l[...] - m) for l in (l0, l1, l2)]
        den = e[0] + e[1] + e[2]
        inv = 1.0 / den
        lse_ref[...] = m + jnp.log(den)
        acc = jnp.zeros((tr, gw), F32)
        for gi, o in enumerate((o0, o1, o2)):
            cols = slice(gi * gw, (gi + 1) * gw)
            z = z_ref[:, cols].astype(F32)
            dy = dy_ref[:, cols]
            sg = _sigmoid(z)
            a = o[...] * (e[gi] * inv)
            da = dy * (z * sg)
            da_ref[:, cols] = da
            dz_ref[:, cols] = (dy * a * (sg * (1.0 + z * (1.0 - sg)))).astype(BF16)
            acc = acc + da * a
        for hh in range(HPG):
            cols = slice(hh * HD, (hh + 1) * HD)
            dl_ref[:, cols] = jnp.broadcast_to(jnp.sum(acc[:, cols], axis=1, keepdims=True), (tr, HD))

    grp = pl.BlockSpec((tr, gw), lambda i: (i, 0))
    return pl.pallas_call(
        body, name="attn_mix_bwd", grid=(T // tr,),
        in_specs=[pl.BlockSpec((tr, AW), lambda i: (i, 0))] + [grp] * 6 + [pl.BlockSpec((tr, AW), lambda i: (i, ZA // AW))]
        + [pl.BlockSpec(memory_space=pl.ANY)] * len(deps),
        out_specs=[pl.BlockSpec((tr, AW), lambda i: (i, 0)), grp, grp, pl.BlockSpec((tr, AW), lambda i: (i, ZA // AW))],
        out_shape=[jax.ShapeDtypeStruct((T, AW), F32), jax.ShapeDtypeStruct((T, gw), F32),
                   jax.ShapeDtypeStruct((T, gw), F32), jax.ShapeDtypeStruct((T, NC), BF16)],
        compiler_params=_cp(("parallel",)),
    )(dcat, *os_, *ls_, proj, *deps)


CT = 256


def _shift_down(x, n):
    rows = lax.broadcasted_iota(jnp.int32, x.shape, 0)
    return jnp.where(rows >= n, pltpu.roll(x, n, 0), 0.0)


def _shift_up(x, n):
    rows = lax.broadcasted_iota(jnp.int32, x.shape, 0)
    return jnp.where(rows < x.shape[0] - n, pltpu.roll(x, x.shape[0] - n, 0), 0.0)


def _conv_fwd(proj, conv_w, cat):
    proj_v = proj.reshape(NB, S, NC)
    cat_v = cat.reshape(NB, S, D)

    def body(u_ref, b_ref, c_ref, z_ref, w_ref, cat_in, o_ref):
        del cat_in
        cu = c_ref[...].astype(F32) * u_ref[...].astype(F32)
        cv = _shift_down(cu, 2) * w_ref[0:1, :] + _shift_down(cu, 1) * w_ref[1:2, :] + cu * w_ref[2:3, :]
        z = z_ref[...].astype(F32)
        o_ref[...] = (b_ref[...].astype(F32) * cv * (z * _sigmoid(z))).astype(BF16)

    def seg(off):
        return pl.BlockSpec((None, S, CT), lambda b, j: (b, 0, off // CT + j))

    out = pl.pallas_call(
        body, name="conv_fwd", grid=(NB, CW // CT),
        in_specs=[seg(UC), seg(BC), seg(CC), seg(ZC), pl.BlockSpec((3, CT), lambda b, j: (0, j)),
                  pl.BlockSpec(memory_space=pl.ANY)],
        out_specs=pl.BlockSpec((None, S, CT), lambda b, j: (b, 0, AW // CT + j)),
        out_shape=jax.ShapeDtypeStruct((NB, S, D), BF16),
        input_output_aliases={5: 0},
        compiler_params=_cp(("parallel", "parallel")),
    )(proj_v, proj_v, proj_v, proj_v, conv_w, cat_v)
    return out.reshape(T, D)


def _conv_bwd(dcat, proj, conv_w, dproj, dep=None):
    deps = [] if dep is None else [dep]
    proj_v = proj.reshape(NB, S, NC)
    dproj_v = dproj.reshape(NB, S, NC)
    dcat_v = dcat.reshape(NB, S, D)

    def body(dy_ref, u_ref, b_ref, c_ref, z_ref, w_ref, *rest):
        o_ref, dw_ref, st = rest[1 + len(deps):]
        b = pl.program_id(1)
        w = pl.program_id(2)

        @pl.when((b == 0) & (w == 0))
        def _():
            dw_ref[...] = jnp.zeros_like(dw_ref)

        @pl.when(w == 0)
        def _():
            u, c, z, bb = (r[...].astype(F32) for r in (u_ref, c_ref, z_ref, b_ref))
            dy = dy_ref[...]
            cu = c * u
            s1 = _shift_down(cu, 1)
            s2 = _shift_down(cu, 2)
            cv = s2 * w_ref[0:1, :] + s1 * w_ref[1:2, :] + cu * w_ref[2:3, :]
            sg = _sigmoid(z)
            sz = z * sg
            dcv = dy * bb * sz
            st[0] = dy * cv * sz
            st[2] = dy * bb * cv * (sg * (1.0 + z * (1.0 - sg)))
            dw_ref[0:1, :] += jnp.sum(dcv * s2, axis=0, keepdims=True)
            dw_ref[1:2, :] += jnp.sum(dcv * s1, axis=0, keepdims=True)
            dw_ref[2:3, :] += jnp.sum(dcv * cu, axis=0, keepdims=True)
            dcu = dcv * w_ref[2:3, :] + _shift_up(dcv, 1) * w_ref[1:2, :] + _shift_up(dcv, 2) * w_ref[0:1, :]
            st[1] = dcu * u
            o_ref[...] = (dcu * c).astype(BF16)

        for k in range(3):
            @pl.when(w == k + 1)
            def _(k=k):
                o_ref[...] = st[k].astype(BF16)

    def ahead(j, b, w):
        flat = jnp.minimum(j * NB + b + jnp.where(w > 0, 1, 0), (CW // CT) * NB - 1)
        return flat // NB, flat % NB

    def seg(off):
        return pl.BlockSpec((None, S, CT), lambda j, b, w: (ahead(j, b, w)[1], 0, off // CT + ahead(j, b, w)[0]))

    out, dw = pl.pallas_call(
        body, name="conv_bwd", grid=(CW // CT, NB, 4),
        in_specs=[seg(AW), seg(UC), seg(BC), seg(CC), seg(ZC),
                  pl.BlockSpec((3, CT), lambda j, b, w: (0, ahead(j, b, w)[0])),
                  pl.BlockSpec(memory_space=pl.ANY)] + [pl.BlockSpec(memory_space=pl.ANY)] * len(deps),
        out_specs=[pl.BlockSpec((None, S, CT), lambda j, b, w: (b, 0, (UC + w * CW) // CT + j)),
                   pl.BlockSpec((3, CT), lambda j, b, w: (0, j))],
        out_shape=[jax.ShapeDtypeStruct((NB, S, NC), BF16), jax.ShapeDtypeStruct((3, CW), F32)],
        input_output_aliases={6: 0},
        scratch_shapes=[pltpu.VMEM((3, S, CT), F32)],
        compiler_params=_cp(("arbitrary",) * 3),
    )(dcat_v, proj_v, proj_v, proj_v, proj_v, conv_w, dproj_v, *deps)
    return out.reshape(T, NC), dw


XT = 1024


def _cross_fwd(proj, mkv, cosq, sinq, cosm, sinm, cat):
    proj_v = proj.reshape(NB, S, NC)
    mkv_v = mkv.reshape(NB, MLEN, 2 * XW)
    cat_v = cat.reshape(NB, S, D)

    def body(q_ref, z_ref, mk_ref, mv_ref, cq, sq, cm, sm, cat_in, o_ref):
        del cat_in
        mkr = _rope(mk_ref[...], cm[...], sm[...], XHD // 2).astype(BF16)
        qr = _rope(q_ref[...].astype(F32), cq[...], sq[...], XHD // 2).astype(BF16)
        sc = lax.dot_general(qr, mkr, NT, preferred_element_type=F32) * X_SCALE
        p = jnp.exp(sc - jnp.max(sc, axis=1, keepdims=True))
        p = p / jnp.sum(p, axis=1, keepdims=True)
        ox = jnp.dot(p.astype(BF16), mv_ref[...].astype(BF16), preferred_element_type=F32)
        z = z_ref[...].astype(F32)
        o_ref[...] = (ox * (z * _sigmoid(z))).astype(BF16)

    def seg(off):
        return pl.BlockSpec((None, XT, XHD), lambda b, h, t: (b, t, off // XHD + h))

    qtab = pl.BlockSpec((XT, XHD), lambda b, h, t: (t, 0))
    mtab = pl.BlockSpec((MLEN, XHD), lambda b, h, t: (0, 0))
    out = pl.pallas_call(
        body, name="cross_fwd", grid=(NB, NXH, S // XT),
        in_specs=[seg(QX), seg(ZX),
                  pl.BlockSpec((None, MLEN, XHD), lambda b, h, t: (b, 0, h)),
                  pl.BlockSpec((None, MLEN, XHD), lambda b, h, t: (b, 0, NXH + h)),
                  qtab, qtab, mtab, mtab, pl.BlockSpec(memory_space=pl.ANY)],
        out_specs=pl.BlockSpec((None, XT, XHD), lambda b, h, t: (b, t, (AW + CW) // XHD + h)),
        out_shape=jax.ShapeDtypeStruct((NB, S, D), BF16),
        input_output_aliases={8: 0},
        compiler_params=_cp(("parallel",) * 3),
    )(proj_v, proj_v, mkv_v, mkv_v, cosq, sinq, cosm, sinm, cat_v)
    return out.reshape(T, D)


def _cross_bwd(dcat, proj, mkv, cosq, sinq, cosm, sinm, dproj):
    proj_v = proj.reshape(NB, S, NC)
    dproj_v = dproj.reshape(NB, S, NC)
    dcat_v = dcat.reshape(NB, S, D)
    mkv_v = mkv.reshape(NB, MLEN, 2 * XW)
    nt = S // XT

    def body(dy_ref, q_ref, z_ref, mk_ref, mv_ref, cq, sq, cm, sm, dp_in, o_ref, dmk_ref, dmv_ref, dz_s):
        del dp_in
        t = pl.program_id(2)
        w = pl.program_id(3)

        @pl.when((t == 0) & (w == 0))
        def _():
            dmk_ref[...] = jnp.zeros_like(dmk_ref)
            dmv_ref[...] = jnp.zeros_like(dmv_ref)

        @pl.when(w == 0)
        def _():
            mkr = _rope(mk_ref[...], cm[...], sm[...], XHD // 2).astype(BF16)
            mv_b = mv_ref[...].astype(BF16)
            qr = _rope(q_ref[...].astype(F32), cq[...], sq[...], XHD // 2).astype(BF16)
            sc = lax.dot_general(qr, mkr, NT, preferred_element_type=F32) * X_SCALE
            p = jnp.exp(sc - jnp.max(sc, axis=1, keepdims=True))
            p = p / jnp.sum(p, axis=1, keepdims=True)
            p_b = p.astype(BF16)
            ox = jnp.dot(p_b, mv_b, preferred_element_type=F32)
            z = z_ref[...].astype(F32)
            dy = dy_ref[...]
            sg = _sigmoid(z)
            dz_s[...] = dy * ox * (sg * (1.0 + z * (1.0 - sg)))
            dox_b = (dy * (z * sg)).astype(BF16)
            dp = lax.dot_general(dox_b, mv_b, NT, preferred_element_type=F32)
            ds_b = (p * (dp - jnp.sum(dp * p, axis=1, keepdims=True)) * X_SCALE).astype(BF16)
            dq = jnp.dot(ds_b, mkr, preferred_element_type=F32)
            o_ref[...] = _rope_t(dq, cq[...], sq[...], XHD // 2).astype(BF16)
            dmk_ref[...] += lax.dot_general(ds_b, qr, TN, preferred_element_type=F32)
            dmv_ref[...] += lax.dot_general(p_b, dox_b, TN, preferred_element_type=F32)

        @pl.when(w == 1)
        def _():
            o_ref[...] = dz_s[...].astype(BF16)

        @pl.when((t == nt - 1) & (w == 1))
        def _():
            dmk_ref[...] = _rope_t(dmk_ref[...], cm[...], sm[...], XHD // 2)

    def seg(off):
        return pl.BlockSpec((None, XT, XHD), lambda b, h, t, w: (b, t, off // XHD + h))

    qtab = pl.BlockSpec((XT, XHD), lambda b, h, t, w: (t, 0))
    mtab = pl.BlockSpec((MLEN, XHD), lambda b, h, t, w: (0, 0))
    macc = pl.BlockSpec((None, MLEN, XHD), lambda b, h, t, w: (b, 0, h))
    out, dmk, dmv = pl.pallas_call(
        body, name="cross_bwd", grid=(NB, NXH, nt, 2),
        in_specs=[pl.BlockSpec((None, XT, XHD), lambda b, h, t, w: (b, t, (AW + CW) // XHD + h)),
                  seg(QX), seg(ZX),
                  pl.BlockSpec((None, MLEN, XHD), lambda b, h, t, w: (b, 0, h)),
                  pl.BlockSpec((None, MLEN, XHD), lambda b, h, t, w: (b, 0, NXH + h)),
                  qtab, qtab, mtab, mtab, pl.BlockSpec(memory_space=pl.ANY)],
        out_specs=[pl.BlockSpec((None, XT, XHD), lambda b, h, t, w: (b, t, (QX + w * XW) // XHD + h)), macc, macc],
        out_shape=[jax.ShapeDtypeStruct((NB, S, NC), BF16), jax.ShapeDtypeStruct((NB, MLEN, XW), F32),
                   jax.ShapeDtypeStruct((NB, MLEN, XW), F32)],
        input_output_aliases={9: 0},
        scratch_shapes=[pltpu.VMEM((XT, XHD), F32)],
        compiler_params=_cp(("arbitrary",) * 4),
    )(dcat_v, proj_v, proj_v, mkv_v, mkv_v, cosq, sinq, cosm, sinm, dproj_v)
    return out.reshape(T, NC), dmk, dmv


def _local_step(x, mem, pre_norm, conv_w, mem_norm, post_norm, tgt, chip, core, comm, moments=None):
    x2 = x.reshape(T, D)
    mem2 = mem.reshape(NB * MLEN, D)
    tgt2 = tgt.reshape(T, D)
    cosa, sina = _rope_tables(jnp.arange(S), HD // 2)
    cosq, sinq = _rope_tables(jnp.arange(S) + MLEN, XHD // 2)
    cosm, sinm = _rope_tables(jnp.arange(MLEN), XHD // 2)

    h = _rms_fwd(x2, pre_norm, "pre_norm_fwd", dep=comm.gather_started())
    memn = _rms_fwd(mem2, mem_norm, "mem_norm_fwd")
    proj = _proj_part(h, comm.w_in_own(), chip, (0,), "proj_own")
    wg_in, tok = comm.w_in_near(after=[proj, memn, conv_w])
    proj = _proj_part(h, wg_in, chip, (REL_XOR[0], REL_XOR[1]), "proj_near", prev=proj, dep=tok)
    wg_in, tok = comm.w_in_all(after=proj)
    proj = _proj_part(h, wg_in, chip, (REL_XOR[2],), "proj_far", prev=proj, dep=tok)
    fw = [_attn_fwd(proj, cosa, sina, g) for g in range(3)]
    os_ = [f[0] for f in fw]
    ls_ = [f[1] for f in fw]
    tok = comm.w_rest_landed(after=ls_)
    cat = _attn_mix(proj, os_, ls_, dep=tok)
    cat = _conv_fwd(proj, conv_w, cat)
    wg_kv, wg_out = comm.w_rest(after=[cat])
    mkv = _mm_nn(memn, wg_kv, "mkv", NB * MLEN, 1024, 512)
    cat = _cross_fwd(proj, mkv, cosq, sinq, cosm, sinm, cat)
    y = _mm_nn(cat, wg_out, "out_proj", 1024, 2048, 2048)
    dy, dout, d_post, loss = _post_norm_loss(y, x2, tgt2, post_norm)

    dcat = _mm_nt(dy, wg_out, "dcat", 1024, 2048, 2048)
    g_send = _grad_rows(cat, dy,1 - core, BF16, "grad_w_out_send")
    sent = comm.sibling_start(g_send.reshape(NCHIP, D // NCHIP // 2, D), "a")
    g_keep = _grad_rows(cat, dy,core, BF16, "grad_w_out_keep", dep=sent[-1])
    red_a = comm.reduce_start_summed(g_keep.reshape(NCHIP, D // NCHIP // 2, D), sent, "a")
    da, lse, delta, dproj = _attn_mix_bwd(dcat, proj, os_, ls_, dep=red_a[-1])
    for g in range(3):
        dproj = _attn_bwd(fw[g][2], cosa, sina, da, lse, delta, dproj, g)
    dproj, dmk, dmv = _cross_bwd(dcat, proj, mkv, cosq, sinq, cosm, sinm, dproj)
    dmkv = jnp.concatenate([dmk, dmv], axis=-1).reshape(NB * MLEN, 2 * XW)
    dmkv_b = dmkv.astype(BF16)
    gk_send = _grad_rows(memn, dmkv_b, 1 - core, BF16, "grad_w_mem_kv_send")
    sent = comm.sibling_start(gk_send.reshape(NCHIP, D // NCHIP // 2, 2 * XW), "k")
    gk_keep = _grad_rows(memn, dmkv_b, core, BF16, "grad_w_mem_kv_keep", dep=sent[-1])
    dmemn = _mm_nt(dmkv_b, wg_kv, "dmemn", NB * MLEN, 1024, 512)
    d_mem = _norm_gain_grad(dmemn, mem2, "mem_norm_bwd")
    red_k = comm.reduce_start_summed(gk_keep.reshape(NCHIP, D // NCHIP // 2, 2 * XW), sent, "k", after=d_mem)
    dproj, d_conv = _conv_bwd(dcat, proj, conv_w, dproj, dep=red_k[-1])
    (r_out,) = comm.reduce_finish(red_a, "a", after=d_conv)
    (r_kv,) = comm.reduce_finish(red_k, "k", after=d_conv)

    g_send = _grad_w_in(h, dproj, 1 - core, BF16, "grad_w_in_send", dep=r_kv)
    sent = comm.sibling_start(g_send.reshape(NCHIP, D // 2, WB), "b")
    q = _grad_w_in(h, dproj, core, BF16, "grad_w_in_keep_0", dep=sent[-1], tiles=(0, 1))
    got = comm.sibling_wait(sent, [q], "b")
    q = _pair_sum_rows(q.reshape(NCHIP, D // 2, WB), got, "pair_sum_b", rows=(0, GW_TM))
    red_b0 = comm.scatter_rows(q, "b0", (0, GW_TM))
    q = _grad_w_in(h, dproj, core, BF16, "grad_w_in_keep_1", dep=red_b0[4], tiles=(1, 1),
                   addend=got.reshape(NCHIP * D // 2, WB), prev=red_b0[2][0].reshape(NCHIP * D // 2, WB))
    red_b1 = comm.scatter_rows(q.reshape(NCHIP, D // 2, WB), "b1", (GW_TM, GW_TM), before=red_b0)
    sets = [] if moments is None else [(moments["w_out"][0], r_out, *moments["w_out"][1:]),
                                      (moments["w_mem_kv"][0], r_kv, *moments["w_mem_kv"][1:])]
    dh, updates = _dh(dproj, wg_in, dep=red_b1[4], adam=sets)
    r_in = comm.reduce_finish_start([red_b0, red_b1], "b", after=dh)
    grad_x, d_pre = _pre_norm_bwd(dh, x2, pre_norm, dout, dep=r_in[-1])
    return loss, grad_x.reshape(NB, S, D), d_pre, d_mem, d_post, d_conv, r_in, (r_kv, r_out), updates


def _adamw(w, g, m, v, name):
    rows, cols = w.shape
    tr = rows if rows <= 512 else 512
    tc = cols if cols <= 1024 else 1024
    if cols % tc:
        tc = 896

    blk = pl.BlockSpec((tr, tc), lambda i, j: (i, j))
    sds = jax.ShapeDtypeStruct((rows, cols), F32)
    def body(*refs):
        _adamw_math(*refs)

    return pl.pallas_call(
        body, name=name, grid=(rows // tr, cols // tc),
        in_specs=[blk] * 4, out_specs=[blk] * 4, out_shape=[sds] * 4,
        compiler_params=_cp(("parallel", "parallel")),
    )(w, g, m, v)


def _place():
    return lax.axis_index("x"), lax.axis_index("y"), lax.axis_index("c")


def _other_chips(x, y):
    return [(1 - x, y), (x, 1 - y), (1 - x, 1 - y)]


def _tile_cols(cols):
    return cols if cols <= 1024 else (1024 if cols % 1024 == 0 else 896)


def _cast_own(w, chip, name, half=None, prev=None):
    rows, cols = w.shape
    tr, tc = 512, _tile_cols(cols)
    nrt = rows // tr if half is None else rows // tr // 2
    index = jnp.stack([chip, 0 if half is None else half]).astype(jnp.int32)
    extra = [] if prev is None else [prev]

    def body(ix_ref, w_ref, *rest):
        rest[-1][...] = w_ref[...].astype(BF16)

    grid_spec = pltpu.PrefetchScalarGridSpec(
        num_scalar_prefetch=1, grid=(nrt, cols // tc),
        in_specs=[pl.BlockSpec((tr, tc), lambda i, j, ix: (ix[1] * nrt + i, j))] + [ANY_SPEC] * len(extra),
        out_specs=pl.BlockSpec((None, tr, tc), lambda i, j, ix: (ix[0], ix[1] * nrt + i, j)))
    return pl.pallas_call(
        body, name=name, grid_spec=grid_spec,
        out_shape=jax.ShapeDtypeStruct((NCHIP, rows, cols), BF16),
        input_output_aliases={} if prev is None else {2: 0},
        compiler_params=_cp(("parallel", "parallel")),
    )(index, w, *extra)


HBM_SPEC = pl.BlockSpec(memory_space=pltpu.HBM)
SEM_SPEC = pl.BlockSpec(memory_space=pltpu.SEMAPHORE)
ANY_SPEC = pl.BlockSpec(memory_space=pl.ANY)
EFFECT = pltpu.SideEffectType.DATAFLOW_SIDE_EFFECTING
TOKEN = jax.ShapeDtypeStruct((8, 128), F32)


def _half(ref, chip, hc):
    hr = ref.shape[1] // 2
    return ref.at[chip, pl.ds(hc * hr, hr), :]


NEAR = (0, 1)
FAR = (2,)
REL_XOR = (2, 1, 3)


def _gather_copies(refs, send_sems, recv_sems, rels):
    x, y, c = _place()
    chips = _other_chips(x, y)
    out, inc = [], []
    for a, ref in enumerate(refs):
        for p, j in enumerate(rels):
            px, py = chips[j]
            mine = _half(ref, 2 * x + y, c)
            theirs = _half(ref, 2 * px + py, c)
            sems = dict(send_sem=send_sems.at[len(rels) * a + p], recv_sem=recv_sems.at[len(rels) * a + p],
                        device_id=(px, py, c), device_id_type=MESH)
            out.append(pltpu.make_async_remote_copy(src_ref=mine, dst_ref=mine, **sems))
            inc.append(pltpu.make_async_remote_copy(src_ref=theirs, dst_ref=theirs, **sems))
    return out, inc


def _gather_start(bufs, groups, name):
    n = len(bufs)
    ng = len(groups)

    def body(*refs):
        ins = refs[:n]
        token = refs[-1]
        for gi, rels in enumerate(groups):
            out, _ = _gather_copies(ins, refs[n + 2 * gi], refs[n + 2 * gi + 1], rels)
            for cp in out:
                cp.start()
        token[...] = jnp.zeros_like(token)

    sems = []
    for rels in groups:
        sems += [pltpu.SemaphoreType.DMA((len(rels) * n,))] * 2
    res = pl.pallas_call(
        body, name=name,
        in_specs=[HBM_SPEC] * n,
        out_specs=[SEM_SPEC] * (2 * ng) + [HBM_SPEC] * n + [pl.BlockSpec(memory_space=pltpu.VMEM)],
        out_shape=sems + [pltpu.HBM(b.shape, b.dtype) for b in bufs] + [TOKEN],
        input_output_aliases={a: 2 * ng + a for a in range(n)},
        compiler_params=pltpu.CompilerParams(has_side_effects=EFFECT),
    )(*[pltpu.with_memory_space_constraint(b, pltpu.HBM) for b in bufs])
    return [(res[2 * gi], res[2 * gi + 1]) for gi in range(ng)], list(res[2 * ng:2 * ng + n]), res[-1]


def _gather_wait(bufs, sems, rels, after, name):
    n = len(bufs)
    send_sems, recv_sems = sems
    after = list(after) if isinstance(after, (list, tuple)) else [after]

    def body(*refs):
        ins = refs[:n]
        out, inc = _gather_copies(ins, refs[n], refs[n + 1], rels)
        for cp in out:
            cp.wait_send()
        for cp in inc:
            cp.wait_recv()

    return pl.pallas_call(
        body, name=name,
        in_specs=[HBM_SPEC] * n + [SEM_SPEC, SEM_SPEC] + [ANY_SPEC] * len(after),
        out_specs=[HBM_SPEC] * n,
        out_shape=[pltpu.HBM(b.shape, b.dtype) for b in bufs],
        input_output_aliases={a: a for a in range(n)},
        compiler_params=pltpu.CompilerParams(has_side_effects=EFFECT),
    )(*bufs, send_sems, recv_sems, *after)


def _forward_halves(bufs, rels, name):
    n = len(bufs)

    def body(*refs):
        cps, waits = _forward_copies(refs[n:2 * n], rels, refs[2 * n], refs[2 * n + 1])
        for cp in cps:
            cp.start()
        for cp in waits:
            cp.wait_recv()
        for cp in cps:
            cp.wait_send()

    return pl.pallas_call(
        body, name=name,
        in_specs=[ANY_SPEC] * n, out_specs=[ANY_SPEC] * n,
        out_shape=[jax.ShapeDtypeStruct(s.shape, s.dtype) for s in bufs],
        input_output_aliases={a: a for a in range(n)},
        scratch_shapes=[pltpu.SemaphoreType.DMA((len(rels) * n,)), pltpu.SemaphoreType.DMA((len(rels) * n,))],
    )(*bufs)


def _forward_copies(refs, rels, send_sems, recv_sems):
    x, y, c = _place()
    chips = _other_chips(x, y)
    cps, waits = [], []
    for a, ref in enumerate(refs):
        for p, j in enumerate(rels):
            px, py = chips[j]
            sems = dict(send_sem=send_sems.at[len(rels) * a + p], recv_sem=recv_sems.at[len(rels) * a + p],
                        device_id=(x, y, 1 - c), device_id_type=MESH)
            got = _half(ref, 2 * px + py, c)
            want = _half(ref, 2 * px + py, 1 - c)
            cps.append(pltpu.make_async_remote_copy(src_ref=got, dst_ref=got, **sems))
            waits.append(pltpu.make_async_remote_copy(src_ref=want, dst_ref=want, **sems))
    return cps, waits


def _forward_start(bufs, rels, name):
    n = len(bufs)

    def body(*refs):
        cps, _ = _forward_copies(refs[:n], rels, refs[n], refs[n + 1])
        for cp in cps:
            cp.start()
        refs[-1][...] = jnp.zeros_like(refs[-1])

    res = pl.pallas_call(
        body, name=name,
        in_specs=[HBM_SPEC] * n,
        out_specs=[SEM_SPEC, SEM_SPEC] + [HBM_SPEC] * n + [pl.BlockSpec(memory_space=pltpu.VMEM)],
        out_shape=[pltpu.SemaphoreType.DMA((len(rels) * n,))] * 2 + [pltpu.HBM(b.shape, b.dtype) for b in bufs] + [TOKEN],
        input_output_aliases={a: 2 + a for a in range(n)},
        compiler_params=pltpu.CompilerParams(has_side_effects=EFFECT),
    )(*[pltpu.with_memory_space_constraint(b, pltpu.HBM) for b in bufs])
    return (res[0], res[1]), list(res[2:2 + n]), res[-1]


def _forward_wait(bufs, sems, rels, after, name):
    n = len(bufs)

    def body(*refs):
        cps, waits = _forward_copies(refs[:n], rels, refs[n], refs[n + 1])
        for cp in cps:
            cp.wait_send()
        for cp in waits:
            cp.wait_recv()

    return pl.pallas_call(
        body, name=name,
        in_specs=[HBM_SPEC] * n + [SEM_SPEC, SEM_SPEC] + [ANY_SPEC] * len(after),
        out_specs=[HBM_SPEC] * n,
        out_shape=[pltpu.HBM(b.shape, b.dtype) for b in bufs],
        input_output_aliases={a: a for a in range(n)},
        compiler_params=pltpu.CompilerParams(has_side_effects=EFFECT),
    )(*bufs, sems[0], sems[1], *after)


def _sibling_copy(src, land, send_sems, recv_sems):
    x, y, c = _place()
    return pltpu.make_async_remote_copy(src_ref=src, dst_ref=land, send_sem=send_sems.at[0], recv_sem=recv_sems.at[0],
                                        device_id=(x, y, 1 - c), device_id_type=MESH)


def _sibling_start(part, name):
    def body(src, land, send_sems, recv_sems, src_thru, land_thru, token):
        _sibling_copy(src, land, send_sems, recv_sems).start()
        token[...] = jnp.zeros_like(token)

    land = lax.empty(part.shape, part.dtype)
    return pl.pallas_call(
        body, name=name,
        in_specs=[HBM_SPEC] * 2,
        out_specs=[SEM_SPEC, SEM_SPEC, HBM_SPEC, HBM_SPEC, pl.BlockSpec(memory_space=pltpu.VMEM)],
        out_shape=[pltpu.SemaphoreType.DMA((1,)), pltpu.SemaphoreType.DMA((1,)), pltpu.HBM(part.shape, part.dtype),
                   pltpu.HBM(part.shape, part.dtype), TOKEN],
        input_output_aliases={0: 2, 1: 3},
        compiler_params=pltpu.CompilerParams(has_side_effects=EFFECT),
    )(pltpu.with_memory_space_constraint(part, pltpu.HBM), pltpu.with_memory_space_constraint(land, pltpu.HBM))


def _sibling_wait(state, after, name):
    send_sems, recv_sems, part, land, _ = state

    def body(src, land_ref, send_ref, recv_ref, *rest):
        cp = _sibling_copy(src, land_ref, send_ref, recv_ref)
        cp.wait_send()
        cp.wait_recv()

    return pl.pallas_call(
        body, name=name,
        in_specs=[HBM_SPEC, HBM_SPEC, SEM_SPEC, SEM_SPEC] + [ANY_SPEC] * len(after),
        out_specs=[HBM_SPEC, HBM_SPEC],
        out_shape=[pltpu.HBM(part.shape, part.dtype), pltpu.HBM(land.shape, land.dtype)],
        input_output_aliases={0: 0, 1: 1},
        compiler_params=pltpu.CompilerParams(has_side_effects=EFFECT),
    )(part, land, send_sems, recv_sems, *after)[1]


def _pair_sum_rows(keep, got, name, rows=None):
    nblk, nrows, cols = keep.shape
    tr, tc = 512, _tile_cols(cols)
    r0, nr = (0, nrows) if rows is None else rows

    def body(k_ref, g_ref, o_ref):
        o_ref[...] = (k_ref[...].astype(F32) + g_ref[...].astype(F32)).astype(BF16)

    blk = pl.BlockSpec((None, tr, tc), lambda b, i, j: (b, r0 // tr + i, j))
    return pl.pallas_call(
        body, name=name, grid=(nblk, nr // tr, cols // tc),
        in_specs=[blk, blk], out_specs=blk,
        out_shape=jax.ShapeDtypeStruct(keep.shape, BF16),
        input_output_aliases={} if rows is None else {0: 0},
        compiler_params=_cp(("parallel",) * 3),
    )(keep, got)


def _scatter_start(qs, name, rows=None, lands=None):
    n = len(qs)

    def body(*refs):
        ins, lands = refs[:n], refs[n:2 * n]
        token = refs[-1]
        for cp in _scatter_copies(ins, lands, refs[2 * n], refs[2 * n + 1], rows):
            cp.start()
        token[...] = jnp.zeros_like(token)

    if lands is None:
        lands = [lax.empty((3,) + q.shape[1:], q.dtype) for q in qs]
    res = pl.pallas_call(
        body, name=name,
        in_specs=[HBM_SPEC] * (2 * n),
        out_specs=[SEM_SPEC, SEM_SPEC] + [HBM_SPEC] * (2 * n) + [pl.BlockSpec(memory_space=pltpu.VMEM)],
        out_shape=[pltpu.SemaphoreType.DMA((3 * n,)), pltpu.SemaphoreType.DMA((3 * n,))]
        + [pltpu.HBM(b.shape, b.dtype) for b in qs + lands] + [TOKEN],
        input_output_aliases={a: 2 + a for a in range(2 * n)},
        compiler_params=pltpu.CompilerParams(has_side_effects=EFFECT),
    )(*[pltpu.with_memory_space_constraint(b, pltpu.HBM) for b in qs + lands])
    return res[0], res[1], list(res[2:2 + n]), list(res[2 + n:2 + 2 * n]), res[-1]


def _scatter_copies(ins, lands, send_sems, recv_sems, rows=None):
    x, y, c = _place()
    sel = slice(None) if rows is None else pl.ds(rows[0], rows[1])
    cps = []
    for a in range(len(ins)):
        for j, (px, py) in enumerate(_other_chips(x, y)):
            cps.append(pltpu.make_async_remote_copy(
                src_ref=ins[a].at[2 * px + py, sel, :], dst_ref=lands[a].at[j, sel, :],
                send_sem=send_sems.at[3 * a + j], recv_sem=recv_sems.at[3 * a + j], device_id=(px, py, c), device_id_type=MESH))
    return cps


def _scatter_wait(qs, lands, send_sems, recv_sems, after, name, rows=None):
    n = len(qs)

    def body(*refs):
        for cp in _scatter_copies(refs[:n], refs[n:2 * n], refs[2 * n], refs[2 * n + 1], rows):
            cp.wait_send()
            cp.wait_recv()

    res = pl.pallas_call(
        body, name=name,
        in_specs=[HBM_SPEC] * (2 * n) + [SEM_SPEC, SEM_SPEC, ANY_SPEC],
        out_specs=[HBM_SPEC] * (2 * n),
        out_shape=[pltpu.HBM(b.shape, b.dtype) for b in qs + lands],
        input_output_aliases={a: a for a in range(2 * n)},
        compiler_params=pltpu.CompilerParams(has_side_effects=EFFECT),
    )(*qs, *lands, send_sems, recv_sems, after)
    return list(res[:n]), list(res[n:])


def _chip_sum(q, got, name):
    _, hr, cols = got.shape
    tr, tc = 512, _tile_cols(cols)
    chip = 2 * lax.axis_index("x") + lax.axis_index("y")
    c = lax.axis_index("c")

    def body(idx_ref, q_ref, g_ref, o_ref):
        del idx_ref
        acc = q_ref[...].astype(F32)
        for i in range(3):
            acc = acc + g_ref[i].astype(F32)
        o_ref[...] = acc

    grid_spec = pltpu.PrefetchScalarGridSpec(
        num_scalar_prefetch=1, grid=(hr // tr, cols // tc),
        in_specs=[pl.BlockSpec((None, tr, tc), lambda i, j, ix: (ix[0], i, j)),
                  pl.BlockSpec((3, tr, tc), lambda i, j, ix: (0, i, j))],
        out_specs=pl.BlockSpec((None, tr, tc), lambda i, j, ix: (ix[1], i, j)))
    return pl.pallas_call(
        body, name=name, grid_spec=grid_spec,
        out_shape=jax.ShapeDtypeStruct((2, hr, cols), F32),
        compiler_params=_cp(("parallel", "parallel")),
    )(jnp.stack([chip, c]).astype(jnp.int32), q, got)


def _join_halves(bufs, name):
    n = len(bufs)

    def body(*refs):
        outs = refs[n:2 * n]
        send_sems, recv_sems = refs[2 * n:]
        x, y, c = _place()
        cps = []
        for a in range(n):
            cps.append(pltpu.make_async_remote_copy(src_ref=outs[a].at[c], dst_ref=outs[a].at[c], send_sem=send_sems.at[a],
                                                    recv_sem=recv_sems.at[a], device_id=(x, y, 1 - c), device_id_type=MESH))
        for cp in cps:
            cp.start()
        for a in range(n):
            theirs = outs[a].at[1 - c]
            pltpu.make_async_remote_copy(src_ref=theirs, dst_ref=theirs, send_sem=send_sems.at[a], recv_sem=recv_sems.at[a],
                                         device_id=(x, y, 1 - c), device_id_type=MESH).wait_recv()
        for cp in cps:
            cp.wait_send()

    hbm = pl.BlockSpec(memory_space=pl.ANY)
    return pl.pallas_call(
        body, name=name,
        in_specs=[hbm] * n, out_specs=[hbm] * n,
        out_shape=[jax.ShapeDtypeStruct(b.shape, b.dtype) for b in bufs],
        input_output_aliases={a: a for a in range(n)},
        scratch_shapes=[pltpu.SemaphoreType.DMA((n,)), pltpu.SemaphoreType.DMA((n,))],
    )(*bufs)


def _join_copy(buf, send_sems, recv_sems):
    x, y, c = _place()
    sems = dict(send_sem=send_sems.at[0], recv_sem=recv_sems.at[0], device_id=(x, y, 1 - c), device_id_type=MESH)
    return (pltpu.make_async_remote_copy(src_ref=buf.at[c], dst_ref=buf.at[c], **sems),
            pltpu.make_async_remote_copy(src_ref=buf.at[1 - c], dst_ref=buf.at[1 - c], **sems))


def _join_start(buf, name):
    def body(b_ref, send_sems, recv_sems, thru, token):
        _join_copy(b_ref, send_sems, recv_sems)[0].start()
        token[...] = jnp.zeros_like(token)

    return pl.pallas_call(
        body, name=name,
        in_specs=[HBM_SPEC],
        out_specs=[SEM_SPEC, SEM_SPEC, HBM_SPEC, pl.BlockSpec(memory_space=pltpu.VMEM)],
        out_shape=[pltpu.SemaphoreType.DMA((1,)), pltpu.SemaphoreType.DMA((1,)), pltpu.HBM(buf.shape, buf.dtype), TOKEN],
        input_output_aliases={0: 2},
        compiler_params=pltpu.CompilerParams(has_side_effects=EFFECT),
    )(pltpu.with_memory_space_constraint(buf, pltpu.HBM))


def _join_wait(state, after, name):
    send_sems, recv_sems, buf, _ = state

    def body(b_ref, send_ref, recv_ref, *rest):
        out, inc = _join_copy(b_ref, send_ref, recv_ref)
        out.wait_send()
        inc.wait_recv()

    return pl.pallas_call(
        body, name=name,
        in_specs=[HBM_SPEC, SEM_SPEC, SEM_SPEC] + [ANY_SPEC] * len(after),
        out_specs=HBM_SPEC,
        out_shape=pltpu.HBM(buf.shape, buf.dtype),
        input_output_aliases={0: 0},
        compiler_params=pltpu.CompilerParams(has_side_effects=EFFECT),
    )(buf, send_sems, recv_sems, *after)


N_DEV = 8


def _gather_small(block, reduce, name, deps=()):
    m_per, cols = block.shape
    deps = list(deps)

    def body(x_ref, *rest):
        out_ref, all_ref, send_sems, recv_sems, local_sem = rest[len(deps):]
        x, y, c = _place()
        me, sibling = (x, y, c), (x, y, 1 - c)
        chips = _other_chips(x, y)

        def rows(px, py, pc):
            return all_ref.at[pl.ds((4 * px + 2 * py + pc) * m_per, m_per), :]

        def copy(k, block_of, to, src=None):
            return pltpu.make_async_remote_copy(
                src_ref=rows(*block_of) if src is None else src, dst_ref=rows(*block_of),
                send_sem=send_sems.at[k], recv_sem=recv_sems.at[k], device_id=to, device_id_type=MESH)

        mine = pltpu.make_async_copy(x_ref, rows(*me), local_sem)
        mine.start()
        first = [copy(0, me, sibling, src=x_ref)]
        first += [copy(1 + j, me, (*chip, c), src=x_ref) for j, chip in enumerate(chips)]
        for cp in first:
            cp.start()
        passed = [copy(4 + j, (*chip, c), sibling) for j, chip in enumerate(chips)]
        for j, chip in enumerate(chips):
            copy(1 + j, (*chip, c), me).wait_recv()
            passed[j].start()
        copy(0, sibling, me).wait_recv()
        for j, chip in enumerate(chips):
            copy(4 + j, (*chip, 1 - c), me).wait_recv()
        for cp in first + passed:
            cp.wait_send()
        mine.wait()
        if reduce:
            acc = all_ref[pl.ds(0, m_per), :]
            for i in range(1, N_DEV):
                acc = acc + all_ref[pl.ds(i * m_per, m_per), :]
            out_ref[...] = acc
        else:
            out_ref[...] = all_ref[...]

    out_rows = m_per if reduce else N_DEV * m_per
    return pl.pallas_call(
        body, name=name,
        in_specs=[pl.BlockSpec(memory_space=pltpu.VMEM)] + [pl.BlockSpec(memory_space=pl.ANY)] * len(deps),
        out_specs=pl.BlockSpec(memory_space=pltpu.VMEM),
        out_shape=jax.ShapeDtypeStruct((out_rows, cols), F32),
        scratch_shapes=[pltpu.VMEM((N_DEV * m_per, cols), F32), pltpu.SemaphoreType.DMA((7,)),
                        pltpu.SemaphoreType.DMA((7,)), pltpu.SemaphoreType.DMA],
    )(block, *deps)


class _Comm:
    def __init__(self, w_in, w_kv, w_out, chip, core):
        self.w_in, self.chip, self.core = w_in, chip, core
        self.bufs = [_cast_own(w_in, chip, "cast_w_in_sent", half=core), _cast_own(w_kv, chip, "cast_w_mem_kv"),
                     _cast_own(w_out, chip, "cast_w_out")]


    def gather_started(self):
        (self.sems,), (b_in,), tok = _gather_start(self.bufs[:1], (NEAR,), "gather_start_in_near")
        self.b_in = _cast_own(self.w_in, self.chip, "cast_w_in_kept", half=1 - self.core, prev=b_in)
        return [tok]

    def w_in_own(self):
        return self.b_in.reshape(NCHIP * D, WB)

    def w_in_near(self, after):
        (b_in,) = _gather_wait([self.b_in], self.sems, NEAR, list(after) + self.bufs[1:], "gather_wait_in_near")
        (b_in,) = _forward_halves([b_in], NEAR, "forward_in_near")
        (self.sems,), (self.b_in,), tok = _gather_start([b_in], (FAR,), "gather_start_in_far")
        return self.b_in.reshape(NCHIP * D, WB), tok

    def w_in_all(self, after):
        (b_in,) = _gather_wait([self.b_in], self.sems, FAR, after, "gather_wait_in_far")
        (b_in,) = _forward_halves([b_in], FAR, "forward_in_far")
        (self.sems,), self.b_rest, tok = _gather_start(self.bufs[1:], (NEAR + FAR,), "gather_start_rest")
        return b_in.reshape(NCHIP * D, WB), tok

    def w_rest_landed(self, after):
        b_rest = _gather_wait(self.b_rest, self.sems, NEAR + FAR, after, "gather_wait_rest")
        self.sems, self.b_rest, tok = _forward_start(b_rest, NEAR + FAR, "forward_start_rest")
        return tok

    def w_rest(self, after):
        b_kv, b_out = _forward_wait(self.b_rest, self.sems, NEAR + FAR, after, "forward_wait_rest")
        return b_kv.reshape(D, 2 * XW), b_out.reshape(D, D)

    def sibling_start(self, send, tag):
        return _sibling_start(send, "sibling_start_" + tag)

    def reduce_start_summed(self, keep, sent, tag, after=None):
        got = _sibling_wait(sent, [keep] + ([] if after is None else [after]), "sibling_wait_" + tag)
        return _scatter_start([_pair_sum_rows(keep, got, "pair_sum_" + tag)], "scatter_start_" + tag)

    def sibling_wait(self, sent, after, tag):
        return _sibling_wait(sent, after, "sibling_wait_" + tag)

    def scatter_rows(self, q, tag, rows, before=None):
        lands = None if before is None else before[3]
        return _scatter_start([q], "scatter_start_" + tag, rows=rows, lands=lands) + (rows,)

    def reduce_finish_start(self, states, tag, after):
        qs, lands = states[-1][2], states[-1][3]
        for i, st in enumerate(states):
            qs, lands = _scatter_wait(qs, lands, st[0], st[1], after, f"scatter_wait_{tag}{i}", rows=st[5])
        return _join_start(_chip_sum(qs[0], lands[0], f"chip_sum_{tag}0"), "join_start_" + tag)

    def reduce_finish_wait(self, pending, tag, after):
        j = _join_wait(pending, after, "join_wait_" + tag)
        return j.reshape(2 * j.shape[1], j.shape[2])

    def reduce_finish(self, state, tag, after):
        send_sems, recv_sems, qs, lands, _ = state
        qs, lands = _scatter_wait(qs, lands, send_sems, recv_sems, after, "scatter_wait_" + tag)
        halves = [_chip_sum(q, l, f"chip_sum_{tag}{i}") for i, (q, l) in enumerate(zip(qs, lands))]
        return [j.reshape(2 * j.shape[1], j.shape[2]) for j in _join_halves(halves, "join_halves_" + tag)]


def kernel(x, mem, pre_norm, w_in, conv_w, mem_norm, w_mem_kv, w_out, post_norm, loss_target, m_pre_norm, m_w_in, m_conv_w, m_mem_norm, m_w_mem_kv, m_w_out, m_post_norm, v_pre_norm, v_w_in, v_conv_w, v_mem_norm, v_w_mem_kv, v_w_out, v_post_norm):
    chip = 2 * lax.axis_index("x") + lax.axis_index("y")

    comm = _Comm(w_in[0], w_mem_kv[0], w_out[0], chip, lax.axis_index("c"))
    cw_blk = jnp.zeros((8, 384), F32).at[:3].set(conv_w[0])
    cw_all = _gather_small(cw_blk, False, "gather_conv_w", deps=comm.bufs[1:]).reshape(NCHIP, 2, 8, 384)[:, 0, :3]
    conv_full = jnp.transpose(cw_all, (1, 0, 2)).reshape(3, CW)
    moments = {"w_out": (w_out[0], m_w_out[0], v_w_out[0]), "w_mem_kv": (w_mem_kv[0], m_w_mem_kv[0], v_w_mem_kv[0])}
    loss, grad_x, d_pre, d_mem, d_post, d_conv, r_in, _, (upd_out, upd_kv) = _local_step(
        x, mem, pre_norm, conv_full, mem_norm, post_norm, loss_target, chip, lax.axis_index("c"), comm, moments)

    pack = jnp.concatenate([d_pre, d_mem, d_post, jnp.pad(d_conv, ((0, 0), (0, D - CW))),
                            jnp.pad(loss, ((0, 0), (0, D - 128))), jnp.zeros((1, D), F32)], axis=0)
    tot = _gather_small(pack, True, "reduce_small")
    g_pre, g_mem, g_post = tot[0:1], tot[1:2], tot[2:3]
    g_conv = lax.dynamic_slice(tot[3:6, :CW], (0, chip * 384), (3, 384))
    loss_out = tot[6, 0]

    names = ("pre_norm", "w_in", "conv_w", "mem_norm", "w_mem_kv", "w_out", "post_norm")
    ws = (pre_norm, w_in[0], conv_w[0], mem_norm, w_mem_kv[0], w_out[0], post_norm)
    gs = [g_pre, None, g_conv, g_mem, None, None, g_post]
    ms = (m_pre_norm, m_w_in[0], m_conv_w[0], m_mem_norm, m_w_mem_kv[0], m_w_out[0], m_post_norm)
    vs = (v_pre_norm, v_w_in[0], v_conv_w[0], v_mem_norm, v_w_mem_kv[0], v_w_out[0], v_post_norm)
    upd = [None if g is None else _adamw(w, g, m, v, "adamw_" + nm) for nm, w, g, m, v in zip(names, ws, gs, ms, vs)]
    upd[4], upd[5] = upd_kv, upd_out
    g_in = comm.reduce_finish_wait(r_in, "b", after=[u[1] for u in upd if u is not None])
    upd[1] = _adamw(ws[1], g_in, ms[1], vs[1], "adamw_w_in")

    def shaped(arrs):
        return [a.reshape(w.shape) if w.ndim == a.ndim else a.reshape((1,) + a.shape)
                for a, w in zip(arrs, (pre_norm, w_in, conv_w, mem_norm, w_mem_kv, w_out, post_norm))]

    grads = shaped([u[0] for u in upd])
    deltas = shaped([u[1] for u in upd])
    new_m = shaped([u[2] for u in upd])
    new_v = shaped([u[3] for u in upd])
    return (loss_out, grad_x, *grads, *deltas, *new_m, *new_v)
```

```python
import jax
import jax.numpy as jnp
from jax import lax
from jax.experimental import pallas as pl
from jax.experimental.pallas import tpu as pltpu

F32 = jnp.float32
BF16 = jnp.bfloat16

D = 4096
S = 2048
NB = 2
T = NB * S
MLEN = 256
HD = 128
AW = 1536
CW = 1536
XW = 1024
XHD = 256
NXH = 4
NC = 14336
QA, KA, VA, ZA, UC, BC, CC, ZC, QX, ZX = 0, 1536, 3072, 4608, 6144, 7680, 9216, 10752, 12288, 13312
NCHIP = 4
WB = NC // NCHIP
DIL = (1, 4, 16)
HPG = 4
EPS = 1e-6
NEG = -1e30
ROPE_THETA = 10000.0
A_SCALE = HD ** -0.5
X_SCALE = XHD ** -0.5

ADAM_LR = 0.001
ADAM_B1 = 0.9
ADAM_B2 = 0.999
ADAM_EPS = 1e-08
ADAM_WD = 0.01
ADAM_STEP = 10

MESH = pl.DeviceIdType.MESH
MIB = 1024 * 1024


def _cp(sem, vmem_mib=48):
    return pltpu.CompilerParams(dimension_semantics=sem, vmem_limit_bytes=vmem_mib * MIB)


def _sigmoid(z):
    return 1.0 / (1.0 + jnp.exp(-z))


def _rope(x, cos, sin, half):
    return x * cos + pltpu.roll(x, half, 1) * sin


def _rope_t(g, cos, sin, half):
    return g * cos + pltpu.roll(g * sin, half, 1)


def _rms_fwd(x2, g, name, dep=None):
    rows = x2.shape[0]
    tr = 256
    deps = [] if dep is None else list(dep)

    def body(x_ref, g_ref, *rest):
        o_ref = rest[-1]
        x = x_ref[...]
        r = lax.rsqrt(jnp.mean(x * x, axis=-1, keepdims=True) + EPS)
        o_ref[...] = (x * r * g_ref[...]).astype(BF16)

    return pl.pallas_call(
        body, name=name, grid=(rows // tr,),
        in_specs=[pl.BlockSpec((tr, D), lambda i: (i, 0)), pl.BlockSpec((1, D), lambda i: (0, 0))]
        + [pl.BlockSpec(memory_space=pl.ANY)] * len(deps),
        out_specs=pl.BlockSpec((tr, D), lambda i: (i, 0)),
        out_shape=jax.ShapeDtypeStruct((rows, D), BF16),
        compiler_params=_cp(("parallel",)),
    )(x2, g, *deps)


def _norm_gain_grad(dn, x2, name):
    rows = x2.shape[0]
    tr = 256

    def body(dn_ref, x_ref, dg_ref):
        @pl.when(pl.program_id(0) == 0)
        def _():
            dg_ref[...] = jnp.zeros_like(dg_ref)
        x = x_ref[...]
        r = lax.rsqrt(jnp.mean(x * x, axis=-1, keepdims=True) + EPS)
        dg_ref[...] += jnp.sum(dn_ref[...] * (x * r), axis=0, keepdims=True)

    return pl.pallas_call(
        body, name=name, grid=(rows // tr,),
        in_specs=[pl.BlockSpec((tr, D), lambda i: (i, 0)), pl.BlockSpec((tr, D), lambda i: (i, 0))],
        out_specs=pl.BlockSpec((1, D), lambda i: (0, 0)),
        out_shape=jax.ShapeDtypeStruct((1, D), F32),
        compiler_params=_cp(("arbitrary",)),
    )(dn, x2)


def _pre_norm_bwd(dh, x2, g, dout, dep=None):
    tr = 256
    deps = [] if dep is None else [dep]

    def body(dh_ref, x_ref, g_ref, dout_ref, *rest):
        gx_ref, dg_ref = rest[len(deps):]

        @pl.when(pl.program_id(0) == 0)
        def _():
            dg_ref[...] = jnp.zeros_like(dg_ref)

        x = x_ref[...]
        dh_ = dh_ref[...].astype(F32)
        r = lax.rsqrt(jnp.mean(x * x, axis=-1, keepdims=True) + EPS)
        xhat = x * r
        dg_ref[...] += jnp.sum(dh_ * xhat, axis=0, keepdims=True)
        dxn = dh_ * g_ref[...]
        gx_ref[...] = dout_ref[...].astype(F32) + r * (dxn - xhat * jnp.mean(dxn * xhat, axis=-1, keepdims=True))

    row = pl.BlockSpec((tr, D), lambda i: (i, 0))
    vec = pl.BlockSpec((1, D), lambda i: (0, 0))
    return pl.pallas_call(
        body, name="pre_norm_bwd", grid=(T // tr,),
        in_specs=[row, row, vec, row] + [pl.BlockSpec(memory_space=pl.ANY)] * len(deps),
        out_specs=[row, vec],
        out_shape=[jax.ShapeDtypeStruct((T, D), F32), jax.ShapeDtypeStruct((1, D), F32)],
        compiler_params=_cp(("arbitrary",)),
    )(dh, x2, g, dout, *deps)


def _post_norm_loss(y, x2, tgt, g):
    tr = 256

    def body(y_ref, x_ref, t_ref, g_ref, dy_ref, dout_ref, dg_ref, loss_ref):
        @pl.when(pl.program_id(0) == 0)
        def _():
            dg_ref[...] = jnp.zeros_like(dg_ref)
            loss_ref[...] = jnp.zeros_like(loss_ref)
        yv = y_ref[...]
        gv = g_ref[...]
        r = lax.rsqrt(jnp.mean(yv * yv, axis=-1, keepdims=True) + EPS)
        yhat = yv * r
        err = x_ref[...] + yhat * gv - t_ref[...]
        loss_ref[...] += jnp.sum(jnp.sum(err * err, axis=1, keepdims=True), axis=0, keepdims=True) * (0.5 / D)
        dout = err * (1.0 / D)
        dout_ref[...] = dout.astype(BF16)
        dg_ref[...] += jnp.sum(dout * yhat, axis=0, keepdims=True)
        dyn = dout * gv
        dy_ref[...] = (r * (dyn - yhat * jnp.mean(dyn * yhat, axis=-1, keepdims=True))).astype(BF16)

    row = pl.BlockSpec((tr, D), lambda i: (i, 0))
    vec = pl.BlockSpec((1, D), lambda i: (0, 0))
    return pl.pallas_call(
        body, name="post_norm_loss", grid=(T // tr,),
        in_specs=[row, row, row, vec],
        out_specs=[row, row, vec, pl.BlockSpec((1, 128), lambda i: (0, 0))],
        out_shape=[jax.ShapeDtypeStruct((T, D), BF16), jax.ShapeDtypeStruct((T, D), BF16),
                   jax.ShapeDtypeStruct((1, D), F32), jax.ShapeDtypeStruct((1, 128), F32)],
        compiler_params=_cp(("arbitrary",)),
    )(y, x2, tgt, g)


NN = (((1,), (0,)), ((), ()))
NT = (((1,), (1,)), ((), ()))
TN = (((0,), (0,)), ((), ()))


def _as_index(v):
    return jnp.reshape(v, (1,)).astype(jnp.int32)


def _matmul(a, b, *, name, dims, grid, a_block, a_map, b_block, b_map, o_block, o_map, out_shape, out_dtype=F32,
            index=None, prev=None, deps=(), addend=None):
    extra = ([] if prev is None else [prev]) + [d for d in deps if d is not None]
    first = 0 if index is None else 1
    nk = grid[2]
    in_place = out_dtype == F32
    assert addend is None or not in_place
    n_add = 0 if addend is None else 1

    def body(*refs):
        a_ref, b_ref, o_ref = refs[first], refs[first + 1], refs[first + 2 + n_add + len(extra)]
        acc_ref = o_ref if in_place else refs[-1]

        @pl.when(pl.program_id(2) == 0)
        def _():
            acc_ref[...] = lax.dot_general(a_ref[...], b_ref[...], dims, preferred_element_type=F32)

        @pl.when(pl.program_id(2) > 0)
        def _():
            acc_ref[...] += lax.dot_general(a_ref[...], b_ref[...], dims, preferred_element_type=F32)

        if not in_place:
            @pl.when(pl.program_id(2) == nk - 1)
            def _():
                total = acc_ref[...] if addend is None else acc_ref[...] + refs[first + 2][...].astype(F32)
                o_ref[...] = total.astype(o_ref.dtype)

    adds = [] if addend is None else [addend]
    in_specs = ([pl.BlockSpec(a_block, a_map), pl.BlockSpec(b_block, b_map)] + [pl.BlockSpec(o_block, o_map)] * n_add
                + [pl.BlockSpec(memory_space=pl.ANY)] * len(extra))
    out_specs = pl.BlockSpec(o_block, o_map)
    scratch = [] if in_place else [pltpu.VMEM(o_block, F32)]
    kwargs = dict(name=name, out_shape=jax.ShapeDtypeStruct(out_shape, out_dtype),
                  input_output_aliases={} if prev is None else {first + 2 + n_add: 0},
                  compiler_params=_cp(("parallel", "parallel", "arbitrary"), vmem_mib=56))
    if index is None:
        return pl.pallas_call(body, grid=grid, in_specs=in_specs, out_specs=out_specs, scratch_shapes=scratch,
                              **kwargs)(a, b, *adds, *extra)
    grid_spec = pltpu.PrefetchScalarGridSpec(num_scalar_prefetch=1, grid=grid, in_specs=in_specs, out_specs=out_specs,
                                             scratch_shapes=scratch)
    return pl.pallas_call(body, grid_spec=grid_spec, **kwargs)(_as_index(index), a, b, *adds, *extra)


def _mm_nn(a, b, name, tm, tn, tk):
    m, kd = a.shape
    n = b.shape[1]
    return _matmul(a, b, name=name, dims=NN, grid=(m // tm, n // tn, kd // tk),
                   a_block=(tm, tk), a_map=lambda i, j, k: (i, k),
                   b_block=(tk, tn), b_map=lambda i, j, k: (k, j),
                   o_block=(tm, tn), o_map=lambda i, j, k: (i, j), out_shape=(m, n))


def _mm_nt(a, b, name, tm, tn, tk):
    m, kd = a.shape
    n = b.shape[0]
    return _matmul(a, b, name=name, dims=NT, grid=(m // tm, n // tn, kd // tk),
                   a_block=(tm, tk), a_map=lambda i, j, k: (i, k),
                   b_block=(tn, tk), b_map=lambda i, j, k: (j, k),
                   o_block=(tm, tn), o_map=lambda i, j, k: (i, j), out_shape=(m, n))


W_TN = 1792
W_NJ = WB // W_TN


def _proj_part(h, wg, chip, masks, name, prev=None, dep=None):
    tm, tk = 1024, 2048

    def blk(j, ix):
        m = masks[0]
        for t in range(1, len(masks)):
            m = jnp.where(j // W_NJ == t, masks[t], m)
        return jnp.bitwise_xor(ix[0], m)

    return _matmul(h, wg, name=name, dims=NN, grid=(T // tm, len(masks) * W_NJ, D // tk), index=chip, prev=prev, deps=(dep,),
                   a_block=(tm, tk), a_map=lambda i, j, k, ix: (i, k),
                   b_block=(tk, W_TN), b_map=lambda i, j, k, ix: (blk(j, ix) * (D // tk) + k, j % W_NJ),
                   o_block=(tm, W_TN), o_map=lambda i, j, k, ix: (i, blk(j, ix) * W_NJ + j % W_NJ), out_shape=(T, NC),
                   out_dtype=BF16)


def _adamw_math(w_ref, g_ref, m_ref, v_ref, go_ref, d_ref, nm_ref, nv_ref):
    gv = g_ref[...]
    go_ref[...] = gv
    nm = ADAM_B1 * m_ref[...] + (1.0 - ADAM_B1) * gv
    nv = ADAM_B2 * v_ref[...] + (1.0 - ADAM_B2) * (gv * gv)
    m_hat = nm / (1.0 - ADAM_B1 ** ADAM_STEP)
    v_hat = nv / (1.0 - ADAM_B2 ** ADAM_STEP)
    d_ref[...] = -ADAM_LR * (m_hat / (jnp.sqrt(v_hat) + ADAM_EPS) + ADAM_WD * w_ref[...])
    nm_ref[...] = nm
    nv_ref[...] = nv


def _dh(dproj, wg, dep=None, adam=()):
    tm, tn = 1024, 2048
    grid = (T // tm, D // tn, NC // W_TN)
    nsteps = grid[0] * grid[1] * grid[2]
    deps = [] if dep is None else [dep]
    na = len(adam)
    nk = grid[2]

    def body(*refs):
        a_ref, b_ref = refs[0], refs[1]
        o_ref = refs[2 + 4 * na + len(deps)]
        acc_ref = refs[-1]

        @pl.when(pl.program_id(2) == 0)
        def _():
            acc_ref[...] = lax.dot_general(a_ref[...], b_ref[...], NT, preferred_element_type=F32)

        @pl.when(pl.program_id(2) > 0)
        def _():
            acc_ref[...] += lax.dot_general(a_ref[...], b_ref[...], NT, preferred_element_type=F32)

        @pl.when(pl.program_id(2) == nk - 1)
        def _():
            o_ref[...] = acc_ref[...].astype(BF16)

        for s in range(na):
            ins = refs[2 + 4 * s:6 + 4 * s]
            outs = refs[3 + 4 * na + len(deps) + 4 * s:7 + 4 * na + len(deps) + 4 * s]
            _adamw_math(*ins, *outs)

    def rows_of(arr):
        r, c = arr.shape
        return pl.BlockSpec((r // nsteps, c), lambda i, j, k: ((i * grid[1] + j) * grid[2] + k, 0))

    adam_specs = [rows_of(a) for st in adam for a in st]
    res = pl.pallas_call(
        body, name="dh", grid=grid,
        in_specs=[pl.BlockSpec((tm, W_TN), lambda i, j, k: (i, k)),
                  pl.BlockSpec((tn, W_TN), lambda i, j, k: ((k // W_NJ) * (D // tn) + j, k % W_NJ))]
        + adam_specs + [pl.BlockSpec(memory_space=pl.ANY)] * len(deps),
        out_specs=[pl.BlockSpec((tm, tn), lambda i, j, k: (i, j))] + adam_specs,
        out_shape=[jax.ShapeDtypeStruct((T, D), BF16)] + [jax.ShapeDtypeStruct(a.shape, F32) for st in adam for a in st],
        scratch_shapes=[pltpu.VMEM((tm, tn), F32)],
        compiler_params=_cp(("arbitrary",) * 3, vmem_mib=56),
    )(dproj, wg, *[a for st in adam for a in st], *deps)
    return res[0], [tuple(res[1 + 4 * s:5 + 4 * s]) for s in range(na)]


def _grad_rows(a, b, half, out_dtype, name, dep=None):
    kd, n = b.shape
    tm, tn, tk = D // NCHIP // 2, min(n, 2048), min(kd, 2048)
    return _matmul(a, b, name=name, dims=TN, grid=(NCHIP, n // tn, kd // tk), index=half, deps=(dep,),
                   a_block=(tk, tm), a_map=lambda i, j, k, ix: (k, 2 * i + ix[0]),
                   b_block=(tk, tn), b_map=lambda i, j, k, ix: (k, j),
                   o_block=(tm, tn), o_map=lambda i, j, k, ix: (i, j),
                   out_shape=(NCHIP * tm, n), out_dtype=out_dtype)


GW_TM = 1024


def _grad_w_in(h, dproj, half, out_dtype, name, dep=None, tiles=(0, 2), addend=None, prev=None):
    tm, tk = GW_TM, 2048
    nh = D // 2 // tm
    t0, nt = tiles

    def o_map(i, j, k, ix):
        return ((j // W_NJ) * nh + t0 + i, j % W_NJ)

    return _matmul(h, dproj, name=name, dims=TN, grid=(nt, NC // W_TN, T // tk), index=half, deps=(dep,), prev=prev,
                   a_block=(tk, tm), a_map=lambda i, j, k, ix: (k, ix[0] * nh + t0 + i),
                   b_block=(tk, W_TN), b_map=lambda i, j, k, ix: (k, j),
                   o_block=(tm, W_TN), o_map=o_map, addend=addend,
                   out_shape=(NCHIP * D // 2, WB), out_dtype=out_dtype)


def _rope_tables(pos, half):
    inv = 1.0 / (ROPE_THETA ** (jnp.arange(half, dtype=F32) / half))
    ang = pos.astype(F32)[:, None] * inv[None, :]
    cos, sin = jnp.cos(ang), jnp.sin(ang)
    return jnp.concatenate([cos, cos], axis=1), jnp.concatenate([-sin, sin], axis=1)


def _band_mask(r0):
    qi = lax.broadcasted_iota(jnp.int32, (128, 256), 0)
    kk = lax.broadcasted_iota(jnp.int32, (128, 256), 1)
    return (kk >= qi) & (kk <= qi + 128) & (kk + r0 >= 128)


def _window(r0, nblk):
    if nblk == 1:
        qi = lax.broadcasted_iota(jnp.int32, (128, 128), 0)
        kk = lax.broadcasted_iota(jnp.int32, (128, 128), 1)
        return pl.ds(128, 128), kk <= qi
    return pl.ds(r0, 256), _band_mask(r0)


def _dil_rows(r, n, d):
    if d == 1:
        return pl.ds(pl.multiple_of(n * 128, 128), 128)
    return pl.ds(r + d * 128 * n, 128, stride=d)


def _widen(refs, wide):
    if not wide:
        return refs

    def copy(n, carry):
        rows = pl.ds(pl.multiple_of(n * 256, 256), 256)
        for src, dst in zip(refs, wide):
            dst[rows, :] = src[rows, :].astype(F32)
        return carry

    lax.fori_loop(0, S // 256, copy, 0)
    return wide


def _attn_fwd(proj, cosf, sinf, g):
    d = DIL[g]
    ln = S // d
    nblk = ln // 128
    proj_v = proj.reshape(NB, S, NC)

    def body(q_ref, k_ref, v_ref, cos_ref, sin_ref, o_ref, l_ref, k_s, v_s, q_s, *wide):
        k_s[:, pl.ds(0, 128), :] = jnp.zeros((d, 128, HD), BF16)
        v_s[:, pl.ds(0, 128), :] = jnp.zeros((d, 128, HD), BF16)
        q_ref, k_ref, v_ref = _widen((q_ref, k_ref, v_ref), wide)

        def prep(i, carry):
            r, n = i // nblk, i % nblk
            rows = _dil_rows(r, n, d)
            dst = pl.ds(pl.multiple_of(n * 128 + 128, 128), 128)
            k_s[r, dst, :] = _rope(k_ref[rows, :].astype(F32), cos_ref[rows, :], sin_ref[rows, :], HD // 2).astype(BF16)
            v_s[r, dst, :] = v_ref[rows, :].astype(BF16)
            return carry

        lax.fori_loop(0, d * nblk, prep, 0, unroll=4)

        def step(i, carry):
            r, n = i // nblk, i % nblk
            rows = _dil_rows(r, n, d)
            r0 = pl.multiple_of(n * 128, 128)
            qr = _rope(q_ref[rows, :].astype(F32), cos_ref[rows, :], sin_ref[rows, :], HD // 2).astype(BF16)
            q_s[r, pl.ds(r0, 128), :] = qr
            win, mask = _window(r0, nblk)
            kw = k_s[r, win, :]
            vw = v_s[r, win, :]
            sc = lax.dot_general(qr, kw, NT, preferred_element_type=F32) * A_SCALE
            sc = jnp.where(mask, sc, NEG)
            m = jnp.max(sc, axis=1, keepdims=True)
            p = jnp.exp(sc - m)
            l = jnp.sum(p, axis=1, keepdims=True)
            o_ref[rows, :] = jnp.dot(p.astype(BF16), vw, preferred_element_type=F32) / l
            l_ref[rows, :] = jnp.broadcast_to(m + jnp.log(l), (128, HD))
            return carry

        lax.fori_loop(0, d * nblk, step, 0, unroll=8 if d == 1 else 4)

    def col(off):
        return lambda b, h: (b, 0, off // HD + HPG * g + h)

    blk = (None, S, HD)
    tab = pl.BlockSpec((S, HD), lambda b, h: (0, 0))
    out = pl.BlockSpec(blk, lambda b, h: (b, 0, h))
    kv = pl.BlockSpec((None, None, d, ln + 128, HD), lambda b, h: (b, h, 0, 0, 0))
    qq = pl.BlockSpec((None, None, d, ln, HD), lambda b, h: (b, h, 0, 0, 0))
    o, l, kr, vr, qr = pl.pallas_call(
        body, name=f"attn_fwd_d{d}", grid=(NB, HPG),
        in_specs=[pl.BlockSpec(blk, col(QA)), pl.BlockSpec(blk, col(KA)), pl.BlockSpec(blk, col(VA)), tab, tab],
        out_specs=[out, out, kv, kv, qq],
        out_shape=[jax.ShapeDtypeStruct((NB, S, HPG * HD), F32)] * 2
        + [jax.ShapeDtypeStruct((NB, HPG, d, ln + 128, HD), BF16)] * 2 + [jax.ShapeDtypeStruct((NB, HPG, d, ln, HD), BF16)],
        scratch_shapes=[pltpu.VMEM((S, HD), F32)] * (3 if d > 1 else 0),
        compiler_params=_cp(("parallel", "parallel")),
    )(proj_v, proj_v, proj_v, cosf, sinf)
    return o.reshape(T, HPG * HD), l.reshape(T, HPG * HD), (qr, kr, vr)


def _attn_bwd(qkv, cosf, sinf, da, lse, delta, dproj, g):
    d = DIL[g]
    ln = S // d
    nblk = ln // 128
    dproj_v = dproj.reshape(NB, S, NC)
    da_v = da.reshape(NB, S, AW)
    lse_v = lse.reshape(NB, S, HPG * HD)
    delta_v = delta.reshape(NB, S, HPG * HD)

    def body(q_s, k_s, v_s, cos_ref, sin_ref, da_ref, lse_ref, dl_ref, dp_in_ref, o_ref, stg, dk_s, dv_s):
        del dp_in_ref
        w = pl.program_id(2)

        def emit():
            def cast(n, carry):
                rows = pl.ds(pl.multiple_of(n * 256, 256), 256)
                o_ref[rows, :] = stg[rows, :].astype(BF16)
                return carry

            lax.fori_loop(0, S // 256, cast, 0)

        @pl.when(w == 0)
        def _():
            dk_s[...] = jnp.zeros_like(dk_s)
            dv_s[...] = jnp.zeros_like(dv_s)

            def step(i, carry):
                r, n = i // nblk, i % nblk
                rows = _dil_rows(r, n, d)
                r0 = pl.multiple_of(n * 128, 128)
                win, mask = _window(r0, nblk)
                cos, sin = cos_ref[rows, :], sin_ref[rows, :]
                qr = q_s[r, pl.ds(r0, 128), :]
                kw = k_s[r, win, :]
                vw = v_s[r, win, :]
                sc = lax.dot_general(qr, kw, NT, preferred_element_type=F32) * A_SCALE
                sc = jnp.where(mask, sc, NEG)
                p = jnp.exp(sc - lse_ref[rows, :][:, :1])
                da_b = da_ref[rows, :].astype(BF16)
                dp = lax.dot_general(da_b, vw, NT, preferred_element_type=F32)
                ds_b = (p * (dp - dl_ref[rows, :][:, :1]) * A_SCALE).astype(BF16)
                p_b = p.astype(BF16)
                dq = jnp.dot(ds_b, kw, preferred_element_type=F32)
                stg[rows, :] = _rope_t(dq, cos, sin, HD // 2)
                dk_s[r, win, :] += lax.dot_general(ds_b, qr, TN, preferred_element_type=F32)
                dv_s[r, win, :] += lax.dot_general(p_b, da_b, TN, preferred_element_type=F32)
                return carry

            lax.fori_loop(0, d * nblk, step, 0, unroll=8 if d == 1 else 4)
            emit()

        @pl.when(w == 1)
        def _():
            def put(i, carry):
                r, n = i // nblk, i % nblk
                rows = _dil_rows(r, n, d)
                src = pl.ds(pl.multiple_of(n * 128 + 128, 128), 128)
                stg[rows, :] = _rope_t(dk_s[r, src, :], cos_ref[rows, :], sin_ref[rows, :], HD // 2)
                return carry

            lax.fori_loop(0, d * nblk, put, 0, unroll=4)
            emit()

        @pl.when(w == 2)
        def _():
            def put(i, carry):
                r, n = i // nblk, i % nblk
                src = pl.ds(pl.multiple_of(n * 128 + 128, 128), 128)
                stg[_dil_rows(r, n, d), :] = dv_s[r, src, :]
                return carry

            lax.fori_loop(0, d * nblk, put, 0, unroll=4)
            emit()

    def col(off):
        return lambda b, h, w: (ahead(b, h, w)[0], 0, off // HD + HPG * g + ahead(b, h, w)[1])

    def ahead(b, h, w):
        flat = jnp.minimum(b * HPG + h + jnp.where(w > 0, 1, 0), NB * HPG - 1)
        return flat // HPG, flat % HPG

    blk = (None, S, HD)
    tab = pl.BlockSpec((S, HD), lambda b, h, w: (0, 0))
    per_head = pl.BlockSpec(blk, lambda b, h, w: (ahead(b, h, w)[0], 0, ahead(b, h, w)[1]))
    kv = pl.BlockSpec((None, None, d, ln + 128, HD), lambda b, h, w: (*ahead(b, h, w), 0, 0, 0))
    qq = pl.BlockSpec((None, None, d, ln, HD), lambda b, h, w: (*ahead(b, h, w), 0, 0, 0))
    out = pl.pallas_call(
        body, name=f"attn_bwd_d{d}", grid=(NB, HPG, 3),
        in_specs=[qq, kv, kv, tab, tab,
                  pl.BlockSpec(blk, col(0)), per_head, per_head, pl.BlockSpec(memory_space=pl.ANY)],
        out_specs=pl.BlockSpec(blk, lambda b, h, w: (b, 0, (AW // HD) * w + HPG * g + h)),
        out_shape=jax.ShapeDtypeStruct(dproj_v.shape, BF16),
        input_output_aliases={8: 0},
        scratch_shapes=[pltpu.VMEM((S, HD), F32), pltpu.VMEM((d, ln + 128, HD), F32), pltpu.VMEM((d, ln + 128, HD), F32)],
        compiler_params=_cp(("arbitrary",) * 3),
    )(*qkv, cosf, sinf, da_v, lse_v, delta_v, dproj_v)
    return out.reshape(T, NC)


def _attn_mix(proj, os_, ls_, dep=None):
    tr = 256
    gw = HPG * HD
    deps = [] if dep is None else [dep]

    def body(o0, o1, o2, l0, l1, l2, z_ref, *rest):
        cat_ref = rest[-1]
        m = jnp.maximum(jnp.maximum(l0[...], l1[...]), l2[...])
        e = [jnp.exp(l[...] - m) for l in (l0, l1, l2)]
        inv = 1.0 / (e[0] + e[1] + e[2])
        for gi, o in enumerate((o0, o1, o2)):
            z = z_ref[:, gi * gw:(gi + 1) * gw].astype(F32)
            cat_ref[:, gi * gw:(gi + 1) * gw] = (o[...] * (e[gi] * inv) * (z * _sigmoid(z))).astype(BF16)

    grp = pl.BlockSpec((tr, gw), lambda i: (i, 0))
    return pl.pallas_call(
        body, name="attn_mix", grid=(T // tr,),
        in_specs=[grp] * 6 + [pl.BlockSpec((tr, AW), lambda i: (i, ZA // AW))] + [ANY_SPEC] * len(deps),
        out_specs=pl.BlockSpec((tr, AW), lambda i: (i, 0)),
        out_shape=jax.ShapeDtypeStruct((T, D), BF16),
        compiler_params=_cp(("parallel",)),
    )(*os_, *ls_, proj, *deps)


def _attn_mix_bwd(dcat, proj, os_, ls_, dep=None):
    tr = 256
    gw = HPG * HD
    deps = [] if dep is None else [dep]

    def body(dy_ref, o0, o1, o2, l0, l1, l2, z_ref, *rest):
        da_ref, lse_ref, dl_ref, dz_ref = rest[len(deps):]
        m = jnp.maximum(jnp.maximum(l0[...], l1[...]), l2[...])
        e = [jnp.exp(l[...] - m) for l in (l0, l1, l2)]
        den = e[0] + e[1] + e[2]
        inv = 1.0 / den
        lse_ref[...] = m + jnp.log(den)
        acc = jnp.zeros((tr, gw), F32)
        for gi, o in enumerate((o0, o1, o2)):
            cols = slice(gi * gw, (gi + 1) * gw)
            z = z_ref[:, cols].astype(F32)
            dy = dy_ref[:, cols]
            sg = _sigmoid(z)
            a = o[...] * (e[gi] * inv)
            da = dy * (z * sg)
            da_ref[:, cols] = da
            dz_ref[:, cols] = (dy * a * (sg * (1.0 + z * (1.0 - sg)))).astype(BF16)
            acc = acc + da * a
        for hh in range(HPG):
            cols = slice(hh * HD, (hh + 1) * HD)
            dl_ref[:, cols] = jnp.broadcast_to(jnp.sum(acc[:, cols], axis=1, keepdims=True), (tr, HD))

    grp = pl.BlockSpec((tr, gw), lambda i: (i, 0))
    return pl.pallas_call(
        body, name="attn_mix_bwd", grid=(T // tr,),
        in_specs=[pl.BlockSpec((tr, AW), lambda i: (i, 0))] + [grp] * 6 + [pl.BlockSpec((tr, AW), lambda i: (i, ZA // AW))]
        + [pl.BlockSpec(memory_space=pl.ANY)] * len(deps),
        out_specs=[pl.BlockSpec((tr, AW), lambda i: (i, 0)), grp, grp, pl.BlockSpec((tr, AW), lambda i: (i, ZA // AW))],
        out_shape=[jax.ShapeDtypeStruct((T, AW), F32), jax.ShapeDtypeStruct((T, gw), F32),
                   jax.ShapeDtypeStruct((T, gw), F32), jax.ShapeDtypeStruct((T, NC), BF16)],
        compiler_params=_cp(("parallel",)),
    )(dcat, *os_, *ls_, proj, *deps)


CT = 256


def _shift_down(x, n):
    rows = lax.broadcasted_iota(jnp.int32, x.shape, 0)
    return jnp.where(rows >= n, pltpu.roll(x, n, 0), 0.0)


def _shift_up(x, n):
    rows = lax.broadcasted_iota(jnp.int32, x.shape, 0)
    return jnp.where(rows < x.shape[0] - n, pltpu.roll(x, x.shape[0] - n, 0), 0.0)


def _conv_fwd(proj, conv_w, cat):
    proj_v = proj.reshape(NB, S, NC)
    cat_v = cat.reshape(NB, S, D)

    def body(u_ref, b_ref, c_ref, z_ref, w_ref, cat_in, o_ref):
        del cat_in
        cu = c_ref[...].astype(F32) * u_ref[...].astype(F32)
        cv = _shift_down(cu, 2) * w_ref[0:1, :] + _shift_down(cu, 1) * w_ref[1:2, :] + cu * w_ref[2:3, :]
        z = z_ref[...].astype(F32)
        o_ref[...] = (b_ref[...].astype(F32) * cv * (z * _sigmoid(z))).astype(BF16)

    def seg(off):
        return pl.BlockSpec((None, S, CT), lambda b, j: (b, 0, off // CT + j))

    out = pl.pallas_call(
        body, name="conv_fwd", grid=(NB, CW // CT),
        in_specs=[seg(UC), seg(BC), seg(CC), seg(ZC), pl.BlockSpec((3, CT), lambda b, j: (0, j)),
                  pl.BlockSpec(memory_space=pl.ANY)],
        out_specs=pl.BlockSpec((None, S, CT), lambda b, j: (b, 0, AW // CT + j)),
        out_shape=jax.ShapeDtypeStruct((NB, S, D), BF16),
        input_output_aliases={5: 0},
        compiler_params=_cp(("parallel", "parallel")),
    )(proj_v, proj_v, proj_v, proj_v, conv_w, cat_v)
    return out.reshape(T, D)


def _conv_bwd(dcat, proj, conv_w, dproj, dep=None):
    deps = [] if dep is None else [dep]
    proj_v = proj.reshape(NB, S, NC)
    dproj_v = dproj.reshape(NB, S, NC)
    dcat_v = dcat.reshape(NB, S, D)

    def body(dy_ref, u_ref, b_ref, c_ref, z_ref, w_ref, *rest):
        o_ref, dw_ref, st = rest[1 + len(deps):]
        b = pl.program_id(1)
        w = pl.program_id(2)

        @pl.when((b == 0) & (w == 0))
        def _():
            dw_ref[...] = jnp.zeros_like(dw_ref)

        @pl.when(w == 0)
        def _():
            u, c, z, bb = (r[...].astype(F32) for r in (u_ref, c_ref, z_ref, b_ref))
            dy = dy_ref[...]
            cu = c * u
            s1 = _shift_down(cu, 1)
            s2 = _shift_down(cu, 2)
            cv = s2 * w_ref[0:1, :] + s1 * w_ref[1:2, :] + cu * w_ref[2:3, :]
            sg = _sigmoid(z)
            sz = z * sg
            dcv = dy * bb * sz
            st[0] = dy * cv * sz
            st[2] = dy * bb * cv * (sg * (1.0 + z * (1.0 - sg)))
            dw_ref[0:1, :] += jnp.sum(dcv * s2, axis=0, keepdims=True)
            dw_ref[1:2, :] += jnp.sum(dcv * s1, axis=0, keepdims=True)
            dw_ref[2:3, :] += jnp.sum(dcv * cu, axis=0, keepdims=True)
            dcu = dcv * w_ref[2:3, :] + _shift_up(dcv, 1) * w_ref[1:2, :] + _shift_up(dcv, 2) * w_ref[0:1, :]
            st[1] = dcu * u
            o_ref[...] = (dcu * c).astype(BF16)

        for k in range(3):
            @pl.when(w == k + 1)
            def _(k=k):
                o_ref[...] = st[k].astype(BF16)

    def ahead(j, b, w):
        flat = jnp.minimum(j * NB + b + jnp.where(w > 0, 1, 0), (CW // CT) * NB - 1)
        return flat // NB, flat % NB

    def seg(off):
        return pl.BlockSpec((None, S, CT), lambda j, b, w: (ahead(j, b, w)[1], 0, off // CT + ahead(j, b, w)[0]))

    out, dw = pl.pallas_call(
        body, name="conv_bwd", grid=(CW // CT, NB, 4),
        in_specs=[seg(AW), seg(UC), seg(BC), seg(CC), seg(ZC),
                  pl.BlockSpec((3, CT), lambda j, b, w: (0, ahead(j, b, w)[0])),
                  pl.BlockSpec(memory_space=pl.ANY)] + [pl.BlockSpec(memory_space=pl.ANY)] * len(deps),
        out_specs=[pl.BlockSpec((None, S, CT), lambda j, b, w: (b, 0, (UC + w * CW) // CT + j)),
                   pl.BlockSpec((3, CT), lambda j, b, w: (0, j))],
        out_shape=[jax.ShapeDtypeStruct((NB, S, NC), BF16), jax.ShapeDtypeStruct((3, CW), F32)],
        input_output_aliases={6: 0},
        scratch_shapes=[pltpu.VMEM((3, S, CT), F32)],
        compiler_params=_cp(("arbitrary",) * 3),
    )(dcat_v, proj_v, proj_v, proj_v, proj_v, conv_w, dproj_v, *deps)
    return out.reshape(T, NC), dw


XT = 1024


def _cross_fwd(proj, mkv, cosq, sinq, cosm, sinm, cat):
    proj_v = proj.reshape(NB, S, NC)
    mkv_v = mkv.reshape(NB, MLEN, 2 * XW)
    cat_v = cat.reshape(NB, S, D)

    def body(q_ref, z_ref, mk_ref, mv_ref, cq, sq, cm, sm, cat_in, o_ref):
        del cat_in
        mkr = _rope(mk_ref[...], cm[...], sm[...], XHD // 2).astype(BF16)
        qr = _rope(q_ref[...].astype(F32), cq[...], sq[...], XHD // 2).astype(BF16)
        sc = lax.dot_general(qr, mkr, NT, preferred_element_type=F32) * X_SCALE
        p = jnp.exp(sc - jnp.max(sc, axis=1, keepdims=True))
        p = p / jnp.sum(p, axis=1, keepdims=True)
        ox = jnp.dot(p.astype(BF16), mv_ref[...].astype(BF16), preferred_element_type=F32)
        z = z_ref[...].astype(F32)
        o_ref[...] = (ox * (z * _sigmoid(z))).astype(BF16)

    def seg(off):
        return pl.BlockSpec((None, XT, XHD), lambda b, h, t: (b, t, off // XHD + h))

    qtab = pl.BlockSpec((XT, XHD), lambda b, h, t: (t, 0))
    mtab = pl.BlockSpec((MLEN, XHD), lambda b, h, t: (0, 0))
    out = pl.pallas_call(
        body, name="cross_fwd", grid=(NB, NXH, S // XT),
        in_specs=[seg(QX), seg(ZX),
                  pl.BlockSpec((None, MLEN, XHD), lambda b, h, t: (b, 0, h)),
                  pl.BlockSpec((None, MLEN, XHD), lambda b, h, t: (b, 0, NXH + h)),
                  qtab, qtab, mtab, mtab, pl.BlockSpec(memory_space=pl.ANY)],
        out_specs=pl.BlockSpec((None, XT, XHD), lambda b, h, t: (b, t, (AW + CW) // XHD + h)),
        out_shape=jax.ShapeDtypeStruct((NB, S, D), BF16),
        input_output_aliases={8: 0},
        compiler_params=_cp(("parallel",) * 3),
    )(proj_v, proj_v, mkv_v, mkv_v, cosq, sinq, cosm, sinm, cat_v)
    return out.reshape(T, D)


def _cross_bwd(dcat, proj, mkv, cosq, sinq, cosm, sinm, dproj):
    proj_v = proj.reshape(NB, S, NC)
    dproj_v = dproj.reshape(NB, S, NC)
    dcat_v = dcat.reshape(NB, S, D)
    mkv_v = mkv.reshape(NB, MLEN, 2 * XW)
    nt = S // XT

    def body(dy_ref, q_ref, z_ref, mk_ref, mv_ref, cq, sq, cm, sm, dp_in, o_ref, dmk_ref, dmv_ref, dz_s):
        del dp_in
        t = pl.program_id(2)
        w = pl.program_id(3)

        @pl.when((t == 0) & (w == 0))
        def _():
            dmk_ref[...] = jnp.zeros_like(dmk_ref)
            dmv_ref[...] = jnp.zeros_like(dmv_ref)

        @pl.when(w == 0)
        def _():
            mkr = _rope(mk_ref[...], cm[...], sm[...], XHD // 2).astype(BF16)
            mv_b = mv_ref[...].astype(BF16)
            qr = _rope(q_ref[...].astype(F32), cq[...], sq[...], XHD // 2).astype(BF16)
            sc = lax.dot_general(qr, mkr, NT, preferred_element_type=F32) * X_SCALE
            p = jnp.exp(sc - jnp.max(sc, axis=1, keepdims=True))
            p = p / jnp.sum(p, axis=1, keepdims=True)
            p_b = p.astype(BF16)
            ox = jnp.dot(p_b, mv_b, preferred_element_type=F32)
            z = z_ref[...].astype(F32)
            dy = dy_ref[...]
            sg = _sigmoid(z)
            dz_s[...] = dy * ox * (sg * (1.0 + z * (1.0 - sg)))
            dox_b = (dy * (z * sg)).astype(BF16)
            dp = lax.dot_general(dox_b, mv_b, NT, preferred_element_type=F32)
            ds_b = (p * (dp - jnp.sum(dp * p, axis=1, keepdims=True)) * X_SCALE).astype(BF16)
            dq = jnp.dot(ds_b, mkr, preferred_element_type=F32)
            o_ref[...] = _rope_t(dq, cq[...], sq[...], XHD // 2).astype(BF16)
            dmk_ref[...] += lax.dot_general(ds_b, qr, TN, preferred_element_type=F32)
            dmv_ref[...] += lax.dot_general(p_b, dox_b, TN, preferred_element_type=F32)

        @pl.when(w == 1)
        def _():
            o_ref[...] = dz_s[...].astype(BF16)

        @pl.when((t == nt - 1) & (w == 1))
        def _():
            dmk_ref[...] = _rope_t(dmk_ref[...], cm[...], sm[...], XHD // 2)

    def seg(off):
        return pl.BlockSpec((None, XT, XHD), lambda b, h, t, w: (b, t, off // XHD + h))

    qtab = pl.BlockSpec((XT, XHD), lambda b, h, t, w: (t, 0))
    mtab = pl.BlockSpec((MLEN, XHD), lambda b, h, t, w: (0, 0))
    macc = pl.BlockSpec((None, MLEN, XHD), lambda b, h, t, w: (b, 0, h))
    out, dmk, dmv = pl.pallas_call(
        body, name="cross_bwd", grid=(NB, NXH, nt, 2),
        in_specs=[pl.BlockSpec((None, XT, XHD), lambda b, h, t, w: (b, t, (AW + CW) // XHD + h)),
                  seg(QX), seg(ZX),
                  pl.BlockSpec((None, MLEN, XHD), lambda b, h, t, w: (b, 0, h)),
                  pl.BlockSpec((None, MLEN, XHD), lambda b, h, t, w: (b, 0, NXH + h)),
                  qtab, qtab, mtab, mtab, pl.BlockSpec(memory_space=pl.ANY)],
        out_specs=[pl.BlockSpec((None, XT, XHD), lambda b, h, t, w: (b, t, (QX + w * XW) // XHD + h)), macc, macc],
        out_shape=[jax.ShapeDtypeStruct((NB, S, NC), BF16), jax.ShapeDtypeStruct((NB, MLEN, XW), F32),
                   jax.ShapeDtypeStruct((NB, MLEN, XW), F32)],
        input_output_aliases={9: 0},
        scratch_shapes=[pltpu.VMEM((XT, XHD), F32)],
        compiler_params=_cp(("arbitrary",) * 4),
    )(dcat_v, proj_v, proj_v, mkv_v, mkv_v, cosq, sinq, cosm, sinm, dproj_v)
    return out.reshape(T, NC), dmk, dmv


def _local_step(x, mem, pre_norm, conv_w, mem_norm, post_norm, tgt, chip, core, comm, moments=None):
    x2 = x.reshape(T, D)
    mem2 = mem.reshape(NB * MLEN, D)
    tgt2 = tgt.reshape(T, D)
    cosa, sina = _rope_tables(jnp.arange(S), HD // 2)
    cosq, sinq = _rope_tables(jnp.arange(S) + MLEN, XHD // 2)
    cosm, sinm = _rope_tables(jnp.arange(MLEN), XHD // 2)

    h = _rms_fwd(x2, pre_norm, "pre_norm_fwd", dep=comm.gather_started())
    memn = _rms_fwd(mem2, mem_norm, "mem_norm_fwd")
    proj = _proj_part(h, comm.w_in_own(), chip, (0,), "proj_own")
    wg_in, tok = comm.w_in_near(after=[proj, memn, conv_w])
    proj = _proj_part(h, wg_in, chip, (REL_XOR[0], REL_XOR[1]), "proj_near", prev=proj, dep=tok)
    wg_in, tok = comm.w_in_all(after=proj)
    proj = _proj_part(h, wg_in, chip, (REL_XOR[2],), "proj_far", prev=proj, dep=tok)
    fw = [_attn_fwd(proj, cosa, sina, g) for g in range(3)]
    os_ = [f[0] for f in fw]
    ls_ = [f[1] for f in fw]
    tok = comm.w_rest_landed(after=ls_)
    cat = _attn_mix(proj, os_, ls_, dep=tok)
    cat = _conv_fwd(proj, conv_w, cat)
    wg_kv, wg_out = comm.w_rest(after=[cat])
    mkv = _mm_nn(memn, wg_kv, "mkv", NB * MLEN, 1024, 512)
    cat = _cross_fwd(proj, mkv, cosq, sinq, cosm, sinm, cat)
    y = _mm_nn(cat, wg_out, "out_proj", 1024, 2048, 2048)
    dy, dout, d_post, loss = _post_norm_loss(y, x2, tgt2, post_norm)

    dcat = _mm_nt(dy, wg_out, "dcat", 1024, 2048, 2048)
    g_send = _grad_rows(cat, dy,1 - core, BF16, "grad_w_out_send")
    sent = comm.sibling_start(g_send.reshape(NCHIP, D // NCHIP // 2, D), "a")
    g_keep = _grad_rows(cat, dy,core, BF16, "grad_w_out_keep", dep=sent[-1])
    red_a = comm.reduce_start_summed(g_keep.reshape(NCHIP, D // NCHIP // 2, D), sent, "a")
    da, lse, delta, dproj = _attn_mix_bwd(dcat, proj, os_, ls_, dep=red_a[-1])
    for g in range(3):
        dproj = _attn_bwd(fw[g][2], cosa, sina, da, lse, delta, dproj, g)
    dproj, dmk, dmv = _cross_bwd(dcat, proj, mkv, cosq, sinq, cosm, sinm, dproj)
    dmkv = jnp.concatenate([dmk, dmv], axis=-1).reshape(NB * MLEN, 2 * XW)
    dmkv_b = dmkv.astype(BF16)
    gk_send = _grad_rows(memn, dmkv_b, 1 - core, BF16, "grad_w_mem_kv_send")
    sent = comm.sibling_start(gk_send.reshape(NCHIP, D // NCHIP // 2, 2 * XW), "k")
    gk_keep = _grad_rows(memn, dmkv_b, core, BF16, "grad_w_mem_kv_keep", dep=sent[-1])
    dmemn = _mm_nt(dmkv_b, wg_kv, "dmemn", NB * MLEN, 1024, 512)
    d_mem = _norm_gain_grad(dmemn, mem2, "mem_norm_bwd")
    red_k = comm.reduce_start_summed(gk_keep.reshape(NCHIP, D // NCHIP // 2, 2 * XW), sent, "k", after=d_mem)
    dproj, d_conv = _conv_bwd(dcat, proj, conv_w, dproj, dep=red_k[-1])
    (r_out,) = comm.reduce_finish(red_a, "a", after=d_conv)
    (r_kv,) = comm.reduce_finish(red_k, "k", after=d_conv)

    g_send = _grad_w_in(h, dproj, 1 - core, BF16, "grad_w_in_send", dep=r_kv)
    sent = comm.sibling_start(g_send.reshape(NCHIP, D // 2, WB), "b")
    q = _grad_w_in(h, dproj, core, BF16, "grad_w_in_keep_0", dep=sent[-1], tiles=(0, 1))
    got = comm.sibling_wait(sent, [q], "b")
    q = _pair_sum_rows(q.reshape(NCHIP, D // 2, WB), got, "pair_sum_b", rows=(0, GW_TM))
    red_b0 = comm.scatter_rows(q, "b0", (0, GW_TM))
    q = _grad_w_in(h, dproj, core, BF16, "grad_w_in_keep_1", dep=red_b0[4], tiles=(1, 1),
                   addend=got.reshape(NCHIP * D // 2, WB), prev=red_b0[2][0].reshape(NCHIP * D // 2, WB))
    red_b1 = comm.scatter_rows(q.reshape(NCHIP, D // 2, WB), "b1", (GW_TM, GW_TM), before=red_b0)
    sets = [] if moments is None else [(moments["w_out"][0], r_out, *moments["w_out"][1:]),
                                      (moments["w_mem_kv"][0], r_kv, *moments["w_mem_kv"][1:])]
    dh, updates = _dh(dproj, wg_in, dep=red_b1[4], adam=sets)
    r_in = comm.reduce_finish_start([red_b0, red_b1], "b", after=dh)
    grad_x, d_pre = _pre_norm_bwd(dh, x2, pre_norm, dout, dep=r_in[-1])
    return loss, grad_x.reshape(NB, S, D), d_pre, d_mem, d_post, d_conv, r_in, (r_kv, r_out), updates


def _adamw(w, g, m, v, name):
    rows, cols = w.shape
    tr = rows if rows <= 512 else 512
    tc = cols if cols <= 1024 else 1024
    if cols % tc:
        tc = 896

    blk = pl.BlockSpec((tr, tc), lambda i, j: (i, j))
    sds = jax.ShapeDtypeStruct((rows, cols), F32)
    def body(*refs):
        _adamw_math(*refs)

    return pl.pallas_call(
        body, name=name, grid=(rows // tr, cols // tc),
        in_specs=[blk] * 4, out_specs=[blk] * 4, out_shape=[sds] * 4,
        compiler_params=_cp(("parallel", "parallel")),
    )(w, g, m, v)


def _place():
    return lax.axis_index("x"), lax.axis_index("y"), lax.axis_index("c")


def _other_chips(x, y):
    return [(1 - x, y), (x, 1 - y), (1 - x, 1 - y)]


def _tile_cols(cols):
    return cols if cols <= 1024 else (1024 if cols % 1024 == 0 else 896)


def _cast_own(w, chip, name, half=None, prev=None):
    rows, cols = w.shape
    tr, tc = 512, _tile_cols(cols)
    nrt = rows // tr if half is None else rows // tr // 2
    index = jnp.stack([chip, 0 if half is None else half]).astype(jnp.int32)
    extra = [] if prev is None else [prev]

    def body(ix_ref, w_ref, *rest):
        rest[-1][...] = w_ref[...].astype(BF16)

    grid_spec = pltpu.PrefetchScalarGridSpec(
        num_scalar_prefetch=1, grid=(nrt, cols // tc),
        in_specs=[pl.BlockSpec((tr, tc), lambda i, j, ix: (ix[1] * nrt + i, j))] + [ANY_SPEC] * len(extra),
        out_specs=pl.BlockSpec((None, tr, tc), lambda i, j, ix: (ix[0], ix[1] * nrt + i, j)))
    return pl.pallas_call(
        body, name=name, grid_spec=grid_spec,
        out_shape=jax.ShapeDtypeStruct((NCHIP, rows, cols), BF16),
        input_output_aliases={} if prev is None else {2: 0},
        compiler_params=_cp(("parallel", "parallel")),
    )(index, w, *extra)


HBM_SPEC = pl.BlockSpec(memory_space=pltpu.HBM)
SEM_SPEC = pl.BlockSpec(memory_space=pltpu.SEMAPHORE)
ANY_SPEC = pl.BlockSpec(memory_space=pl.ANY)
EFFECT = pltpu.SideEffectType.DATAFLOW_SIDE_EFFECTING
TOKEN = jax.ShapeDtypeStruct((8, 128), F32)


def _half(ref, chip, hc):
    hr = ref.shape[1] // 2
    return ref.at[chip, pl.ds(hc * hr, hr), :]


NEAR = (0, 1)
FAR = (2,)
REL_XOR = (2, 1, 3)


def _gather_copies(refs, send_sems, recv_sems, rels):
    x, y, c = _place()
    chips = _other_chips(x, y)
    out, inc = [], []
    for a, ref in enumerate(refs):
        for p, j in enumerate(rels):
            px, py = chips[j]
            mine = _half(ref, 2 * x + y, c)
            theirs = _half(ref, 2 * px + py, c)
            sems = dict(send_sem=send_sems.at[len(rels) * a + p], recv_sem=recv_sems.at[len(rels) * a + p],
                        device_id=(px, py, c), device_id_type=MESH)
            out.append(pltpu.make_async_remote_copy(src_ref=mine, dst_ref=mine, **sems))
            inc.append(pltpu.make_async_remote_copy(src_ref=theirs, dst_ref=theirs, **sems))
    return out, inc


def _gather_start(bufs, groups, name):
    n = len(bufs)
    ng = len(groups)

    def body(*refs):
        ins = refs[:n]
        token = refs[-1]
        for gi, rels in enumerate(groups):
            out, _ = _gather_copies(ins, refs[n + 2 * gi], refs[n + 2 * gi + 1], rels)
            for cp in out:
                cp.start()
        token[...] = jnp.zeros_like(token)

    sems = []
    for rels in groups:
        sems += [pltpu.SemaphoreType.DMA((len(rels) * n,))] * 2
    res = pl.pallas_call(
        body, name=name,
        in_specs=[HBM_SPEC] * n,
        out_specs=[SEM_SPEC] * (2 * ng) + [HBM_SPEC] * n + [pl.BlockSpec(memory_space=pltpu.VMEM)],
        out_shape=sems + [pltpu.HBM(b.shape, b.dtype) for b in bufs] + [TOKEN],
        input_output_aliases={a: 2 * ng + a for a in range(n)},
        compiler_params=pltpu.CompilerParams(has_side_effects=EFFECT),
    )(*[pltpu.with_memory_space_constraint(b, pltpu.HBM) for b in bufs])
    return [(res[2 * gi], res[2 * gi + 1]) for gi in range(ng)], list(res[2 * ng:2 * ng + n]), res[-1]


def _gather_wait(bufs, sems, rels, after, name):
    n = len(bufs)
    send_sems, recv_sems = sems
    after = list(after) if isinstance(after, (list, tuple)) else [after]

    def body(*refs):
        ins = refs[:n]
        out, inc = _gather_copies(ins, refs[n], refs[n + 1], rels)
        for cp in out:
            cp.wait_send()
        for cp in inc:
            cp.wait_recv()

    return pl.pallas_call(
        body, name=name,
        in_specs=[HBM_SPEC] * n + [SEM_SPEC, SEM_SPEC] + [ANY_SPEC] * len(after),
        out_specs=[HBM_SPEC] * n,
        out_shape=[pltpu.HBM(b.shape, b.dtype) for b in bufs],
        input_output_aliases={a: a for a in range(n)},
        compiler_params=pltpu.CompilerParams(has_side_effects=EFFECT),
    )(*bufs, send_sems, recv_sems, *after)


def _forward_halves(bufs, rels, name):
    n = len(bufs)

    def body(*refs):
        cps, waits = _forward_copies(refs[n:2 * n], rels, refs[2 * n], refs[2 * n + 1])
        for cp in cps:
            cp.start()
        for cp in waits:
            cp.wait_recv()
        for cp in cps:
            cp.wait_send()

    return pl.pallas_call(
        body, name=name,
        in_specs=[ANY_SPEC] * n, out_specs=[ANY_SPEC] * n,
        out_shape=[jax.ShapeDtypeStruct(s.shape, s.dtype) for s in bufs],
        input_output_aliases={a: a for a in range(n)},
        scratch_shapes=[pltpu.SemaphoreType.DMA((len(rels) * n,)), pltpu.SemaphoreType.DMA((len(rels) * n,))],
    )(*bufs)


def _forward_copies(refs, rels, send_sems, recv_sems):
    x, y, c = _place()
    chips = _other_chips(x, y)
    cps, waits = [], []
    for a, ref in enumerate(refs):
        for p, j in enumerate(rels):
            px, py = chips[j]
            sems = dict(send_sem=send_sems.at[len(rels) * a + p], recv_sem=recv_sems.at[len(rels) * a + p],
                        device_id=(x, y, 1 - c), device_id_type=MESH)
            got = _half(ref, 2 * px + py, c)
            want = _half(ref, 2 * px + py, 1 - c)
            cps.append(pltpu.make_async_remote_copy(src_ref=got, dst_ref=got, **sems))
            waits.append(pltpu.make_async_remote_copy(src_ref=want, dst_ref=want, **sems))
    return cps, waits


def _forward_start(bufs, rels, name):
    n = len(bufs)

    def body(*refs):
        cps, _ = _forward_copies(refs[:n], rels, refs[n], refs[n + 1])
        for cp in cps:
            cp.start()
        refs[-1][...] = jnp.zeros_like(refs[-1])

    res = pl.pallas_call(
        body, name=name,
        in_specs=[HBM_SPEC] * n,
        out_specs=[SEM_SPEC, SEM_SPEC] + [HBM_SPEC] * n + [pl.BlockSpec(memory_space=pltpu.VMEM)],
        out_shape=[pltpu.SemaphoreType.DMA((len(rels) * n,))] * 2 + [pltpu.HBM(b.shape, b.dtype) for b in bufs] + [TOKEN],
        input_output_aliases={a: 2 + a for a in range(n)},
        compiler_params=pltpu.CompilerParams(has_side_effects=EFFECT),
    )(*[pltpu.with_memory_space_constraint(b, pltpu.HBM) for b in bufs])
    return (res[0], res[1]), list(res[2:2 + n]), res[-1]


def _forward_wait(bufs, sems, rels, after, name):
    n = len(bufs)

    def body(*refs):
        cps, waits = _forward_copies(refs[:n], rels, refs[n], refs[n + 1])
        for cp in cps:
            cp.wait_send()
        for cp in waits:
            cp.wait_recv()

    return pl.pallas_call(
        body, name=name,
        in_specs=[HBM_SPEC] * n + [SEM_SPEC, SEM_SPEC] + [ANY_SPEC] * len(after),
        out_specs=[HBM_SPEC] * n,
        out_shape=[pltpu.HBM(b.shape, b.dtype) for b in bufs],
        input_output_aliases={a: a for a in range(n)},
        compiler_params=pltpu.CompilerParams(has_side_effects=EFFECT),
    )(*bufs, sems[0], sems[1], *after)


def _sibling_copy(src, land, send_sems, recv_sems):
    x, y, c = _place()
    return pltpu.make_async_remote_copy(src_ref=src, dst_ref=land, send_sem=send_sems.at[0], recv_sem=recv_sems.at[0],
                                        device_id=(x, y, 1 - c), device_id_type=MESH)


def _sibling_start(part, name):
    def body(src, land, send_sems, recv_sems, src_thru, land_thru, token):
        _sibling_copy(src, land, send_sems, recv_sems).start()
        token[...] = jnp.zeros_like(token)

    land = lax.empty(part.shape, part.dtype)
    return pl.pallas_call(
        body, name=name,
        in_specs=[HBM_SPEC] * 2,
        out_specs=[SEM_SPEC, SEM_SPEC, HBM_SPEC, HBM_SPEC, pl.BlockSpec(memory_space=pltpu.VMEM)],
        out_shape=[pltpu.SemaphoreType.DMA((1,)), pltpu.SemaphoreType.DMA((1,)), pltpu.HBM(part.shape, part.dtype),
                   pltpu.HBM(part.shape, part.dtype), TOKEN],
        input_output_aliases={0: 2, 1: 3},
        compiler_params=pltpu.CompilerParams(has_side_effects=EFFECT),
    )(pltpu.with_memory_space_constraint(part, pltpu.HBM), pltpu.with_memory_space_constraint(land, pltpu.HBM))


def _sibling_wait(state, after, name):
    send_sems, recv_sems, part, land, _ = state

    def body(src, land_ref, send_ref, recv_ref, *rest):
        cp = _sibling_copy(src, land_ref, send_ref, recv_ref)
        cp.wait_send()
        cp.wait_recv()

    return pl.pallas_call(
        body, name=name,
        in_specs=[HBM_SPEC, HBM_SPEC, SEM_SPEC, SEM_SPEC] + [ANY_SPEC] * len(after),
        out_specs=[HBM_SPEC, HBM_SPEC],
        out_shape=[pltpu.HBM(part.shape, part.dtype), pltpu.HBM(land.shape, land.dtype)],
        input_output_aliases={0: 0, 1: 1},
        compiler_params=pltpu.CompilerParams(has_side_effects=EFFECT),
    )(part, land, send_sems, recv_sems, *after)[1]


def _pair_sum_rows(keep, got, name, rows=None):
    nblk, nrows, cols = keep.shape
    tr, tc = 512, _tile_cols(cols)
    r0, nr = (0, nrows) if rows is None else rows

    def body(k_ref, g_ref, o_ref):
        o_ref[...] = (k_ref[...].astype(F32) + g_ref[...].astype(F32)).astype(BF16)

    blk = pl.BlockSpec((None, tr, tc), lambda b, i, j: (b, r0 // tr + i, j))
    return pl.pallas_call(
        body, name=name, grid=(nblk, nr // tr, cols // tc),
        in_specs=[blk, blk], out_specs=blk,
        out_shape=jax.ShapeDtypeStruct(keep.shape, BF16),
        input_output_aliases={} if rows is None else {0: 0},
        compiler_params=_cp(("parallel",) * 3),
    )(keep, got)


def _scatter_start(qs, name, rows=None, lands=None):
    n = len(qs)

    def body(*refs):
        ins, lands = refs[:n], refs[n:2 * n]
        token = refs[-1]
        for cp in _scatter_copies(ins, lands, refs[2 * n], refs[2 * n + 1], rows):
            cp.start()
        token[...] = jnp.zeros_like(token)

    if lands is None:
        lands = [lax.empty((3,) + q.shape[1:], q.dtype) for q in qs]
    res = pl.pallas_call(
        body, name=name,
        in_specs=[HBM_SPEC] * (2 * n),
        out_specs=[SEM_SPEC, SEM_SPEC] + [HBM_SPEC] * (2 * n) + [pl.BlockSpec(memory_space=pltpu.VMEM)],
        out_shape=[pltpu.SemaphoreType.DMA((3 * n,)), pltpu.SemaphoreType.DMA((3 * n,))]
        + [pltpu.HBM(b.shape, b.dtype) for b in qs + lands] + [TOKEN],
        input_output_aliases={a: 2 + a for a in range(2 * n)},
        compiler_params=pltpu.CompilerParams(has_side_effects=EFFECT),
    )(*[pltpu.with_memory_space_constraint(b, pltpu.HBM) for b in qs + lands])
    return res[0], res[1], list(res[2:2 + n]), list(res[2 + n:2 + 2 * n]), res[-1]


def _scatter_copies(ins, lands, send_sems, recv_sems, rows=None):
    x, y, c = _place()
    sel = slice(None) if rows is None else pl.ds(rows[0], rows[1])
    cps = []
    for a in range(len(ins)):
        for j, (px, py) in enumerate(_other_chips(x, y)):
            cps.append(pltpu.make_async_remote_copy(
                src_ref=ins[a].at[2 * px + py, sel, :], dst_ref=lands[a].at[j, sel, :],
                send_sem=send_sems.at[3 * a + j], recv_sem=recv_sems.at[3 * a + j], device_id=(px, py, c), device_id_type=MESH))
    return cps


def _scatter_wait(qs, lands, send_sems, recv_sems, after, name, rows=None):
    n = len(qs)

    def body(*refs):
        for cp in _scatter_copies(refs[:n], refs[n:2 * n], refs[2 * n], refs[2 * n + 1], rows):
            cp.wait_send()
            cp.wait_recv()

    res = pl.pallas_call(
        body, name=name,
        in_specs=[HBM_SPEC] * (2 * n) + [SEM_SPEC, SEM_SPEC, ANY_SPEC],
        out_specs=[HBM_SPEC] * (2 * n),
        out_shape=[pltpu.HBM(b.shape, b.dtype) for b in qs + lands],
        input_output_aliases={a: a for a in range(2 * n)},
        compiler_params=pltpu.CompilerParams(has_side_effects=EFFECT),
    )(*qs, *lands, send_sems, recv_sems, after)
    return list(res[:n]), list(res[n:])


def _chip_sum(q, got, name):
    _, hr, cols = got.shape
    tr, tc = 512, _tile_cols(cols)
    chip = 2 * lax.axis_index("x") + lax.axis_index("y")
    c = lax.axis_index("c")

    def body(idx_ref, q_ref, g_ref, o_ref):
        del idx_ref
        acc = q_ref[...].astype(F32)
        for i in range(3):
            acc = acc + g_ref[i].astype(F32)
        o_ref[...] = acc

    grid_spec = pltpu.PrefetchScalarGridSpec(
        num_scalar_prefetch=1, grid=(hr // tr, cols // tc),
        in_specs=[pl.BlockSpec((None, tr, tc), lambda i, j, ix: (ix[0], i, j)),
                  pl.BlockSpec((3, tr, tc), lambda i, j, ix: (0, i, j))],
        out_specs=pl.BlockSpec((None, tr, tc), lambda i, j, ix: (ix[1], i, j)))
    return pl.pallas_call(
        body, name=name, grid_spec=grid_spec,
        out_shape=jax.ShapeDtypeStruct((2, hr, cols), F32),
        compiler_params=_cp(("parallel", "parallel")),
    )(jnp.stack([chip, c]).astype(jnp.int32), q, got)


def _join_halves(bufs, name):
    n = len(bufs)

    def body(*refs):
        outs = refs[n:2 * n]
        send_sems, recv_sems = refs[2 * n:]
        x, y, c = _place()
        cps = []
        for a in range(n):
            cps.append(pltpu.make_async_remote_copy(src_ref=outs[a].at[c], dst_ref=outs[a].at[c], send_sem=send_sems.at[a],
                                                    recv_sem=recv_sems.at[a], device_id=(x, y, 1 - c), device_id_type=MESH))
        for cp in cps:
            cp.start()
        for a in range(n):
            theirs = outs[a].at[1 - c]
            pltpu.make_async_remote_copy(src_ref=theirs, dst_ref=theirs, send_sem=send_sems.at[a], recv_sem=recv_sems.at[a],
                                         device_id=(x, y, 1 - c), device_id_type=MESH).wait_recv()
        for cp in cps:
            cp.wait_send()

    hbm = pl.BlockSpec(memory_space=pl.ANY)
    return pl.pallas_call(
        body, name=name,
        in_specs=[hbm] * n, out_specs=[hbm] * n,
        out_shape=[jax.ShapeDtypeStruct(b.shape, b.dtype) for b in bufs],
        input_output_aliases={a: a for a in range(n)},
        scratch_shapes=[pltpu.SemaphoreType.DMA((n,)), pltpu.SemaphoreType.DMA((n,))],
    )(*bufs)


def _join_copy(buf, send_sems, recv_sems):
    x, y, c = _place()
    sems = dict(send_sem=send_sems.at[0], recv_sem=recv_sems.at[0], device_id=(x, y, 1 - c), device_id_type=MESH)
    return (pltpu.make_async_remote_copy(src_ref=buf.at[c], dst_ref=buf.at[c], **sems),
            pltpu.make_async_remote_copy(src_ref=buf.at[1 - c], dst_ref=buf.at[1 - c], **sems))


def _join_start(buf, name):
    def body(b_ref, send_sems, recv_sems, thru, token):
        _join_copy(b_ref, send_sems, recv_sems)[0].start()
        token[...] = jnp.zeros_like(token)

    return pl.pallas_call(
        body, name=name,
        in_specs=[HBM_SPEC],
        out_specs=[SEM_SPEC, SEM_SPEC, HBM_SPEC, pl.BlockSpec(memory_space=pltpu.VMEM)],
        out_shape=[pltpu.SemaphoreType.DMA((1,)), pltpu.SemaphoreType.DMA((1,)), pltpu.HBM(buf.shape, buf.dtype), TOKEN],
        input_output_aliases={0: 2},
        compiler_params=pltpu.CompilerParams(has_side_effects=EFFECT),
    )(pltpu.with_memory_space_constraint(buf, pltpu.HBM))


def _join_wait(state, after, name):
    send_sems, recv_sems, buf, _ = state

    def body(b_ref, send_ref, recv_ref, *rest):
        out, inc = _join_copy(b_ref, send_ref, recv_ref)
        out.wait_send()
        inc.wait_recv()

    return pl.pallas_call(
        body, name=name,
        in_specs=[HBM_SPEC, SEM_SPEC, SEM_SPEC] + [ANY_SPEC] * len(after),
        out_specs=HBM_SPEC,
        out_shape=pltpu.HBM(buf.shape, buf.dtype),
        input_output_aliases={0: 0},
        compiler_params=pltpu.CompilerParams(has_side_effects=EFFECT),
    )(buf, send_sems, recv_sems, *after)


N_DEV = 8


def _gather_small(block, reduce, name, deps=()):
    m_per, cols = block.shape
    deps = list(deps)

    def body(x_ref, *rest):
        out_ref, all_ref, send_sems, recv_sems, local_sem = rest[len(deps):]
        x, y, c = _place()
        me, sibling = (x, y, c), (x, y, 1 - c)
        chips = _other_chips(x, y)

        def rows(px, py, pc):
            return all_ref.at[pl.ds((4 * px + 2 * py + pc) * m_per, m_per), :]

        def copy(k, block_of, to, src=None):
            return pltpu.make_async_remote_copy(
                src_ref=rows(*block_of) if src is None else src, dst_ref=rows(*block_of),
                send_sem=send_sems.at[k], recv_sem=recv_sems.at[k], device_id=to, device_id_type=MESH)

        mine = pltpu.make_async_copy(x_ref, rows(*me), local_sem)
        mine.start()
        first = [copy(0, me, sibling, src=x_ref)]
        first += [copy(1 + j, me, (*chip, c), src=x_ref) for j, chip in enumerate(chips)]
        for cp in first:
            cp.start()
        passed = [copy(4 + j, (*chip, c), sibling) for j, chip in enumerate(chips)]
        for j, chip in enumerate(chips):
            copy(1 + j, (*chip, c), me).wait_recv()
            passed[j].start()
        copy(0, sibling, me).wait_recv()
        for j, chip in enumerate(chips):
            copy(4 + j, (*chip, 1 - c), me).wait_recv()
        for cp in first + passed:
            cp.wait_send()
        mine.wait()
        if reduce:
            acc = all_ref[pl.ds(0, m_per), :]
            for i in range(1, N_DEV):
                acc = acc + all_ref[pl.ds(i * m_per, m_per), :]
            out_ref[...] = acc
        else:
            out_ref[...] = all_ref[...]

    out_rows = m_per if reduce else N_DEV * m_per
    return pl.pallas_call(
        body, name=name,
        in_specs=[pl.BlockSpec(memory_space=pltpu.VMEM)] + [pl.BlockSpec(memory_space=pl.ANY)] * len(deps),
        out_specs=pl.BlockSpec(memory_space=pltpu.VMEM),
        out_shape=jax.ShapeDtypeStruct((out_rows, cols), F32),
        scratch_shapes=[pltpu.VMEM((N_DEV * m_per, cols), F32), pltpu.SemaphoreType.DMA((7,)),
                        pltpu.SemaphoreType.DMA((7,)), pltpu.SemaphoreType.DMA],
    )(block, *deps)


class _Comm:
    def __init__(self, w_in, w_kv, w_out, chip, core):
        self.w_in, self.chip, self.core = w_in, chip, core
        self.bufs = [_cast_own(w_in, chip, "cast_w_in_sent", half=core), _cast_own(w_kv, chip, "cast_w_mem_kv"),
                     _cast_own(w_out, chip, "cast_w_out")]


    def gather_started(self):
        (self.sems,), (b_in,), tok = _gather_start(self.bufs[:1], (NEAR,), "gather_start_in_near")
        self.b_in = _cast_own(self.w_in, self.chip, "cast_w_in_kept", half=1 - self.core, prev=b_in)
        return [tok]

    def w_in_own(self):
        return self.b_in.reshape(NCHIP * D, WB)

    def w_in_near(self, after):
        (b_in,) = _gather_wait([self.b_in], self.sems, NEAR, list(after) + self.bufs[1:], "gather_wait_in_near")
        (b_in,) = _forward_halves([b_in], NEAR, "forward_in_near")
        (self.sems,), (self.b_in,), tok = _gather_start([b_in], (FAR,), "gather_start_in_far")
        return self.b_in.reshape(NCHIP * D, WB), tok

    def w_in_all(self, after):
        (b_in,) = _gather_wait([self.b_in], self.sems, FAR, after, "gather_wait_in_far")
        (b_in,) = _forward_halves([b_in], FAR, "forward_in_far")
        (self.sems,), self.b_rest, tok = _gather_start(self.bufs[1:], (NEAR + FAR,), "gather_start_rest")
        return b_in.reshape(NCHIP * D, WB), tok

    def w_rest_landed(self, after):
        b_rest = _gather_wait(self.b_rest, self.sems, NEAR + FAR, after, "gather_wait_rest")
        self.sems, self.b_rest, tok = _forward_start(b_rest, NEAR + FAR, "forward_start_rest")
        return tok

    def w_rest(self, after):
        b_kv, b_out = _forward_wait(self.b_rest, self.sems, NEAR + FAR, after, "forward_wait_rest")
        return b_kv.reshape(D, 2 * XW), b_out.reshape(D, D)

    def sibling_start(self, send, tag):
        return _sibling_start(send, "sibling_start_" + tag)

    def reduce_start_summed(self, keep, sent, tag, after=None):
        got = _sibling_wait(sent, [keep] + ([] if after is None else [after]), "sibling_wait_" + tag)
        return _scatter_start([_pair_sum_rows(keep, got, "pair_sum_" + tag)], "scatter_start_" + tag)

    def sibling_wait(self, sent, after, tag):
        return _sibling_wait(sent, after, "sibling_wait_" + tag)

    def scatter_rows(self, q, tag, rows, before=None):
        lands = None if before is None else before[3]
        return _scatter_start([q], "scatter_start_" + tag, rows=rows, lands=lands) + (rows,)

    def reduce_finish_start(self, states, tag, after):
        qs, lands = states[-1][2], states[-1][3]
        for i, st in enumerate(states):
            qs, lands = _scatter_wait(qs, lands, st[0], st[1], after, f"scatter_wait_{tag}{i}", rows=st[5])
        return _join_start(_chip_sum(qs[0], lands[0], f"chip_sum_{tag}0"), "join_start_" + tag)

    def reduce_finish_wait(self, pending, tag, after):
        j = _join_wait(pending, after, "join_wait_" + tag)
        return j.reshape(2 * j.shape[1], j.shape[2])

    def reduce_finish(self, state, tag, after):
        send_sems, recv_sems, qs, lands, _ = state
        qs, lands = _scatter_wait(qs, lands, send_sems, recv_sems, after, "scatter_wait_" + tag)
        halves = [_chip_sum(q, l, f"chip_sum_{tag}{i}") for i, (q, l) in enumerate(zip(qs, lands))]
        return [j.reshape(2 * j.shape[1], j.shape[2]) for j in _join_halves(halves, "join_halves_" + tag)]


def kernel(x, mem, pre_norm, w_in, conv_w, mem_norm, w_mem_kv, w_out, post_norm, loss_target, m_pre_norm, m_w_in, m_conv_w, m_mem_norm, m_w_mem_kv, m_w_out, m_post_norm, v_pre_norm, v_w_in, v_conv_w, v_mem_norm, v_w_mem_kv, v_w_out, v_post_norm):
    chip = 2 * lax.axis_index("x") + lax.axis_index("y")

    comm = _Comm(w_in[0], w_mem_kv[0], w_out[0], chip, lax.axis_index("c"))
    cw_blk = jnp.zeros((8, 384), F32).at[:3].set(conv_w[0])
    cw_all = _gather_small(cw_blk, False, "gather_conv_w", deps=comm.bufs[1:]).reshape(NCHIP, 2, 8, 384)[:, 0, :3]
    conv_full = jnp.transpose(cw_all, (1, 0, 2)).reshape(3, CW)
    moments = {"w_out": (w_out[0], m_w_out[0], v_w_out[0]), "w_mem_kv": (w_mem_kv[0], m_w_mem_kv[0], v_w_mem_kv[0])}
    loss, grad_x, d_pre, d_mem, d_post, d_conv, r_in, _, (upd_out, upd_kv) = _local_step(
        x, mem, pre_norm, conv_full, mem_norm, post_norm, loss_target, chip, lax.axis_index("c"), comm, moments)

    pack = jnp.concatenate([d_pre, d_mem, d_post, jnp.pad(d_conv, ((0, 0), (0, D - CW))),
                            jnp.pad(loss, ((0, 0), (0, D - 128))), jnp.zeros((1, D), F32)], axis=0)
    tot = _gather_small(pack, True, "reduce_small")
    g_pre, g_mem, g_post = tot[0:1], tot[1:2], tot[2:3]
    g_conv = lax.dynamic_slice(tot[3:6, :CW], (0, chip * 384), (3, 384))
    loss_out = tot[6, 0]

    names = ("pre_norm", "w_in", "conv_w", "mem_norm", "w_mem_kv", "w_out", "post_norm")
    ws = (pre_norm, w_in[0], conv_w[0], mem_norm, w_mem_kv[0], w_out[0], post_norm)
    gs = [g_pre, None, g_conv, g_mem, None, None, g_post]
    ms = (m_pre_norm, m_w_in[0], m_conv_w[0], m_mem_norm, m_w_mem_kv[0], m_w_out[0], m_post_norm)
    vs = (v_pre_norm, v_w_in[0], v_conv_w[0], v_mem_norm, v_w_mem_kv[0], v_w_out[0], v_post_norm)
    upd = [None if g is None else _adamw(w, g, m, v, "adamw_" + nm) for nm, w, g, m, v in zip(names, ws, gs, ms, vs)]
    upd[4], upd[5] = upd_kv, upd_out
    g_in = comm.reduce_finish_wait(r_in, "b", after=[u[1] for u in upd if u is not None])
    upd[1] = _adamw(ws[1], g_in, ms[1], vs[1], "adamw_w_in")

    def shaped(arrs):
        return [a.reshape(w.shape) if w.ndim == a.ndim else a.reshape((1,) + a.shape)
                for a, w in zip(arrs, (pre_norm, w_in, conv_w, mem_norm, w_mem_kv, w_out, post_norm))]

    grads = shaped([u[0] for u in upd])
    deltas = shaped([u[1] for u in upd])
    new_m = shaped([u[2] for u in upd])
    new_v = shaped([u[3] for u in upd])
    return (loss_out, grad_x, *grads, *deltas, *new_m, *new_v)
```

```python
import jax
import jax.numpy as jnp
from jax import lax
from jax.experimental import pallas as pl
from jax.experimental.pallas import tpu as pltpu

F32 = jnp.float32
BF16 = jnp.bfloat16

D = 4096
S = 2048
NB = 2
T = NB * S
MLEN = 256
HD = 128
AW = 1536
CW = 1536
XW = 1024
XHD = 256
NXH = 4
NC = 14336
QA, KA, VA, ZA, UC, BC, CC, ZC, QX, ZX = 0, 1536, 3072, 4608, 6144, 7680, 9216, 10752, 12288, 13312
NCHIP = 4
WB = NC // NCHIP
DIL = (1, 4, 16)
HPG = 4
EPS = 1e-6
NEG = -1e30
ROPE_THETA = 10000.0
A_SCALE = HD ** -0.5
X_SCALE = XHD ** -0.5

ADAM_LR = 0.001
ADAM_B1 = 0.9
ADAM_B2 = 0.999
ADAM_EPS = 1e-08
ADAM_WD = 0.01
ADAM_STEP = 10

MESH = pl.DeviceIdType.MESH
MIB = 1024 * 1024


def _cp(sem, vmem_mib=48):
    return pltpu.CompilerParams(dimension_semantics=sem, vmem_limit_bytes=vmem_mib * MIB)


def _sigmoid(z):
    return 1.0 / (1.0 + jnp.exp(-z))


def _rope(x, cos, sin, half):
    return x * cos + pltpu.roll(x, half, 1) * sin


def _rope_t(g, cos, sin, half):
    return g * cos + pltpu.roll(g * sin, half, 1)


def _rms_fwd(x2, g, name, dep=None):
    rows = x2.shape[0]
    tr = 256
    deps = [] if dep is None else list(dep)

    def body(x_ref, g_ref, *rest):
        o_ref = rest[-1]
        x = x_ref[...]
        r = lax.rsqrt(jnp.mean(x * x, axis=-1, keepdims=True) + EPS)
        o_ref[...] = (x * r * g_ref[...]).astype(BF16)

    return pl.pallas_call(
        body, name=name, grid=(rows // tr,),
        in_specs=[pl.BlockSpec((tr, D), lambda i: (i, 0)), pl.BlockSpec((1, D), lambda i: (0, 0))]
        + [pl.BlockSpec(memory_space=pl.ANY)] * len(deps),
        out_specs=pl.BlockSpec((tr, D), lambda i: (i, 0)),
        out_shape=jax.ShapeDtypeStruct((rows, D), BF16),
        compiler_params=_cp(("parallel",)),
    )(x2, g, *deps)


def _norm_gain_grad(dn, x2, name):
    rows = x2.shape[0]
    tr = 256

    def body(dn_ref, x_ref, dg_ref):
        @pl.when(pl.program_id(0) == 0)
        def _():
            dg_ref[...] = jnp.zeros_like(dg_ref)
        x = x_ref[...]
        r = lax.rsqrt(jnp.mean(x * x, axis=-1, keepdims=True) + EPS)
        dg_ref[...] += jnp.sum(dn_ref[...] * (x * r), axis=0, keepdims=True)

    return pl.pallas_call(
        body, name=name, grid=(rows // tr,),
        in_specs=[pl.BlockSpec((tr, D), lambda i: (i, 0)), pl.BlockSpec((tr, D), lambda i: (i, 0))],
        out_specs=pl.BlockSpec((1, D), lambda i: (0, 0)),
        out_shape=jax.ShapeDtypeStruct((1, D), F32),
        compiler_params=_cp(("arbitrary",)),
    )(dn, x2)


def _pre_norm_bwd(dh, x2, g, dout, dep=None):
    tr = 256
    deps = [] if dep is None else [dep]

    def body(dh_ref, x_ref, g_ref, dout_ref, *rest):
        gx_ref, dg_ref = rest[len(deps):]

        @pl.when(pl.program_id(0) == 0)
        def _():
            dg_ref[...] = jnp.zeros_like(dg_ref)

        x = x_ref[...]
        dh_ = dh_ref[...].astype(F32)
        r = lax.rsqrt(jnp.mean(x * x, axis=-1, keepdims=True) + EPS)
        xhat = x * r
        dg_ref[...] += jnp.sum(dh_ * xhat, axis=0, keepdims=True)
        dxn = dh_ * g_ref[...]
        gx_ref[...] = dout_ref[...].astype(F32) + r * (dxn - xhat * jnp.mean(dxn * xhat, axis=-1, keepdims=True))

    row = pl.BlockSpec((tr, D), lambda i: (i, 0))
    vec = pl.BlockSpec((1, D), lambda i: (0, 0))
    return pl.pallas_call(
        body, name="pre_norm_bwd", grid=(T // tr,),
        in_specs=[row, row, vec, row] + [pl.BlockSpec(memory_space=pl.ANY)] * len(deps),
        out_specs=[row, vec],
        out_shape=[jax.ShapeDtypeStruct((T, D), F32), jax.ShapeDtypeStruct((1, D), F32)],
        compiler_params=_cp(("arbitrary",)),
    )(dh, x2, g, dout, *deps)


def _post_norm_loss(y, x2, tgt, g):
    tr = 256

    def body(y_ref, x_ref, t_ref, g_ref, dy_ref, dout_ref, dg_ref, loss_ref):
        @pl.when(pl.program_id(0) == 0)
        def _():
            dg_ref[...] = jnp.zeros_like(dg_ref)
            loss_ref[...] = jnp.zeros_like(loss_ref)
        yv = y_ref[...]
        gv = g_ref[...]
        r = lax.rsqrt(jnp.mean(yv * yv, axis=-1, keepdims=True) + EPS)
        yhat = yv * r
        err = x_ref[...] + yhat * gv - t_ref[...]
        loss_ref[...] += jnp.sum(jnp.sum(err * err, axis=1, keepdims=True), axis=0, keepdims=True) * (0.5 / D)
        dout = err * (1.0 / D)
        dout_ref[...] = dout.astype(BF16)
        dg_ref[...] += jnp.sum(dout * yhat, axis=0, keepdims=True)
        dyn = dout * gv
        dy_ref[...] = (r * (dyn - yhat * jnp.mean(dyn * yhat, axis=-1, keepdims=True))).astype(BF16)

    row = pl.BlockSpec((tr, D), lambda i: (i, 0))
    vec = pl.BlockSpec((1, D), lambda i: (0, 0))
    return pl.pallas_call(
        body, name="post_norm_loss", grid=(T // tr,),
        in_specs=[row, row, row, vec],
        out_specs=[row, row, vec, pl.BlockSpec((1, 128), lambda i: (0, 0))],
        out_shape=[jax.ShapeDtypeStruct((T, D), BF16), jax.ShapeDtypeStruct((T, D), BF16),
                   jax.ShapeDtypeStruct((1, D), F32), jax.ShapeDtypeStruct((1, 128), F32)],
        compiler_params=_cp(("arbitrary",)),
    )(y, x2, tgt, g)


NN = (((1,), (0,)), ((), ()))
NT = (((1,), (1,)), ((), ()))
TN = (((0,), (0,)), ((), ()))


def _as_index(v):
    return jnp.reshape(v, (1,)).astype(jnp.int32)


def _matmul(a, b, *, name, dims, grid, a_block, a_map, b_block, b_map, o_block, o_map, out_shape, out_dtype=F32,
            index=None, prev=None, deps=(), addend=None):
    extra = ([] if prev is None else [prev]) + [d for d in deps if d is not None]
    first = 0 if index is None else 1
    nk = grid[2]
    in_place = out_dtype == F32
    assert addend is None or not in_place
    n_add = 0 if addend is None else 1

    def body(*refs):
        a_ref, b_ref, o_ref = refs[first], refs[first + 1], refs[first + 2 + n_add + len(extra)]
        acc_ref = o_ref if in_place else refs[-1]

        @pl.when(pl.program_id(2) == 0)
        def _():
            acc_ref[...] = lax.dot_general(a_ref[...], b_ref[...], dims, preferred_element_type=F32)

        @pl.when(pl.program_id(2) > 0)
        def _():
            acc_ref[...] += lax.dot_general(a_ref[...], b_ref[...], dims, preferred_element_type=F32)

        if not in_place:
            @pl.when(pl.program_id(2) == nk - 1)
            def _():
                total = acc_ref[...] if addend is None else acc_ref[...] + refs[first + 2][...].astype(F32)
                o_ref[...] = total.astype(o_ref.dtype)

    adds = [] if addend is None else [addend]
    in_specs = ([pl.BlockSpec(a_block, a_map), pl.BlockSpec(b_block, b_map)] + [pl.BlockSpec(o_block, o_map)] * n_add
                + [pl.BlockSpec(memory_space=pl.ANY)] * len(extra))
    out_specs = pl.BlockSpec(o_block, o_map)
    scratch = [] if in_place else [pltpu.VMEM(o_block, F32)]
    kwargs = dict(name=name, out_shape=jax.ShapeDtypeStruct(out_shape, out_dtype),
                  input_output_aliases={} if prev is None else {first + 2 + n_add: 0},
                  compiler_params=_cp(("parallel", "parallel", "arbitrary"), vmem_mib=56))
    if index is None:
        return pl.pallas_call(body, grid=grid, in_specs=in_specs, out_specs=out_specs, scratch_shapes=scratch,
                              **kwargs)(a, b, *adds, *extra)
    grid_spec = pltpu.PrefetchScalarGridSpec(num_scalar_prefetch=1, grid=grid, in_specs=in_specs, out_specs=out_specs,
                                             scratch_shapes=scratch)
    return pl.pallas_call(body, grid_spec=grid_spec, **kwargs)(_as_index(index), a, b, *adds, *extra)


def _mm_nn(a, b, name, tm, tn, tk):
    m, kd = a.shape
    n = b.shape[1]
    return _matmul(a, b, name=name, dims=NN, grid=(m // tm, n // tn, kd // tk),
                   a_block=(tm, tk), a_map=lambda i, j, k: (i, k),
                   b_block=(tk, tn), b_map=lambda i, j, k: (k, j),
                   o_block=(tm, tn), o_map=lambda i, j, k: (i, j), out_shape=(m, n))


def _mm_nt(a, b, name, tm, tn, tk):
    m, kd = a.shape
    n = b.shape[0]
    return _matmul(a, b, name=name, dims=NT, grid=(m // tm, n // tn, kd // tk),
                   a_block=(tm, tk), a_map=lambda i, j, k: (i, k),
                   b_block=(tn, tk), b_map=lambda i, j, k: (j, k),
                   o_block=(tm, tn), o_map=lambda i, j, k: (i, j), out_shape=(m, n))


W_TN = 1792
W_NJ = WB // W_TN


def _proj_part(h, wg, chip, masks, name, prev=None, dep=None):
    tm, tk = 1024, 2048

    def blk(j, ix):
        m = masks[0]
        for t in range(1, len(masks)):
            m = jnp.where(j // W_NJ == t, masks[t], m)
        return jnp.bitwise_xor(ix[0], m)

    return _matmul(h, wg, name=name, dims=NN, grid=(T // tm, len(masks) * W_NJ, D // tk), index=chip, prev=prev, deps=(dep,),
                   a_block=(tm, tk), a_map=lambda i, j, k, ix: (i, k),
                   b_block=(tk, W_TN), b_map=lambda i, j, k, ix: (blk(j, ix) * (D // tk) + k, j % W_NJ),
                   o_block=(tm, W_TN), o_map=lambda i, j, k, ix: (i, blk(j, ix) * W_NJ + j % W_NJ), out_shape=(T, NC),
                   out_dtype=BF16)


def _adamw_math(w_ref, g_ref, m_ref, v_ref, go_ref, d_ref, nm_ref, nv_ref):
    gv = g_ref[...]
    go_ref[...] = gv
    nm = ADAM_B1 * m_ref[...] + (1.0 - ADAM_B1) * gv
    nv = ADAM_B2 * v_ref[...] + (1.0 - ADAM_B2) * (gv * gv)
    m_hat = nm / (1.0 - ADAM_B1 ** ADAM_STEP)
    v_hat = nv / (1.0 - ADAM_B2 ** ADAM_STEP)
    d_ref[...] = -ADAM_LR * (m_hat / (jnp.sqrt(v_hat) + ADAM_EPS) + ADAM_WD * w_ref[...])
    nm_ref[...] = nm
    nv_ref[...] = nv


def _dh(dproj, wg, dep=None, adam=()):
    tm, tn = 1024, 2048
    grid = (T // tm, D // tn, NC // W_TN)
    nsteps = grid[0] * grid[1] * grid[2]
    deps = [] if dep is None else [dep]
    na = len(adam)
    nk = grid[2]

    def body(*refs):
        a_ref, b_ref = refs[0], refs[1]
        o_ref = refs[2 + 4 * na + len(deps)]
        acc_ref = refs[-1]

        @pl.when(pl.program_id(2) == 0)
        def _():
            acc_ref[...] = lax.dot_general(a_ref[...], b_ref[...], NT, preferred_element_type=F32)

        @pl.when(pl.program_id(2) > 0)
        def _():
            acc_ref[...] += lax.dot_general(a_ref[...], b_ref[...], NT, preferred_element_type=F32)

        @pl.when(pl.program_id(2) == nk - 1)
        def _():
            o_ref[...] = acc_ref[...].astype(BF16)

        for s in range(na):
            ins = refs[2 + 4 * s:6 + 4 * s]
            outs = refs[3 + 4 * na + len(deps) + 4 * s:7 + 4 * na + len(deps) + 4 * s]
            _adamw_math(*ins, *outs)

    def rows_of(arr):
        r, c = arr.shape
        return pl.BlockSpec((r // nsteps, c), lambda i, j, k: ((i * grid[1] + j) * grid[2] + k, 0))

    adam_specs = [rows_of(a) for st in adam for a in st]
    res = pl.pallas_call(
        body, name="dh", grid=grid,
        in_specs=[pl.BlockSpec((tm, W_TN), lambda i, j, k: (i, k)),
                  pl.BlockSpec((tn, W_TN), lambda i, j, k: ((k // W_NJ) * (D // tn) + j, k % W_NJ))]
        + adam_specs + [pl.BlockSpec(memory_space=pl.ANY)] * len(deps),
        out_specs=[pl.BlockSpec((tm, tn), lambda i, j, k: (i, j))] + adam_specs,
        out_shape=[jax.ShapeDtypeStruct((T, D), BF16)] + [jax.ShapeDtypeStruct(a.shape, F32) for st in adam for a in st],
        scratch_shapes=[pltpu.VMEM((tm, tn), F32)],
        compiler_params=_cp(("arbitrary",) * 3, vmem_mib=56),
    )(dproj, wg, *[a for st in adam for a in st], *deps)
    return res[0], [tuple(res[1 + 4 * s:5 + 4 * s]) for s in range(na)]


def _grad_rows(a, b, half, out_dtype, name, dep=None):
    kd, n = b.shape
    tm, tn, tk = D // NCHIP // 2, min(n, 2048), min(kd, 2048)
    return _matmul(a, b, name=name, dims=TN, grid=(NCHIP, n // tn, kd // tk), index=half, deps=(dep,),
                   a_block=(tk, tm), a_map=lambda i, j, k, ix: (k, 2 * i + ix[0]),
                   b_block=(tk, tn), b_map=lambda i, j, k, ix: (k, j),
                   o_block=(tm, tn), o_map=lambda i, j, k, ix: (i, j),
                   out_shape=(NCHIP * tm, n), out_dtype=out_dtype)


GW_TM = 1024


def _grad_w_in(h, dproj, half, out_dtype, name, dep=None, tiles=(0, 2), addend=None, prev=None):
    tm, tk = GW_TM, 2048
    nh = D // 2 // tm
    t0, nt = tiles

    def o_map(i, j, k, ix):
        return ((j // W_NJ) * nh + t0 + i, j % W_NJ)

    return _matmul(h, dproj, name=name, dims=TN, grid=(nt, NC // W_TN, T // tk), index=half, deps=(dep,), prev=prev,
                   a_block=(tk, tm), a_map=lambda i, j, k, ix: (k, ix[0] * nh + t0 + i),
                   b_block=(tk, W_TN), b_map=lambda i, j, k, ix: (k, j),
                   o_block=(tm, W_TN), o_map=o_map, addend=addend,
                   out_shape=(NCHIP * D // 2, WB), out_dtype=out_dtype)


def _rope_tables(pos, half):
    inv = 1.0 / (ROPE_THETA ** (jnp.arange(half, dtype=F32) / half))
    ang = pos.astype(F32)[:, None] * inv[None, :]
    cos, sin = jnp.cos(ang), jnp.sin(ang)
    return jnp.concatenate([cos, cos], axis=1), jnp.concatenate([-sin, sin], axis=1)


def _band_mask(r0):
    qi = lax.broadcasted_iota(jnp.int32, (128, 256), 0)
    kk = lax.broadcasted_iota(jnp.int32, (128, 256), 1)
    return (kk >= qi) & (kk <= qi + 128) & (kk + r0 >= 128)


def _window(r0, nblk):
    if nblk == 1:
        qi = lax.broadcasted_iota(jnp.int32, (128, 128), 0)
        kk = lax.broadcasted_iota(jnp.int32, (128, 128), 1)
        return pl.ds(128, 128), kk <= qi
    return pl.ds(r0, 256), _band_mask(r0)


def _dil_rows(r, n, d):
    if d == 1:
        return pl.ds(pl.multiple_of(n * 128, 128), 128)
    return pl.ds(r + d * 128 * n, 128, stride=d)


def _widen(refs, wide):
    if not wide:
        return refs

    def copy(n, carry):
        rows = pl.ds(pl.multiple_of(n * 256, 256), 256)
        for src, dst in zip(refs, wide):
            dst[rows, :] = src[rows, :].astype(F32)
        return carry

    lax.fori_loop(0, S // 256, copy, 0)
    return wide


def _attn_fwd(proj, cosf, sinf, g):
    d = DIL[g]
    ln = S // d
    nblk = ln // 128
    proj_v = proj.reshape(NB, S, NC)

    def body(q_ref, k_ref, v_ref, cos_ref, sin_ref, o_ref, l_ref, k_s, v_s, q_s, *wide):
        k_s[:, pl.ds(0, 128), :] = jnp.zeros((d, 128, HD), BF16)
        v_s[:, pl.ds(0, 128), :] = jnp.zeros((d, 128, HD), BF16)
        q_ref, k_ref, v_ref = _widen((q_ref, k_ref, v_ref), wide)

        def prep(i, carry):
            r, n = i // nblk, i % nblk
            rows = _dil_rows(r, n, d)
            dst = pl.ds(pl.multiple_of(n * 128 + 128, 128), 128)
            k_s[r, dst, :] = _rope(k_ref[rows, :].astype(F32), cos_ref[rows, :], sin_ref[rows, :], HD // 2).astype(BF16)
            v_s[r, dst, :] = v_ref[rows, :].astype(BF16)
            return carry

        lax.fori_loop(0, d * nblk, prep, 0, unroll=4)

        def step(i, carry):
            r, n = i // nblk, i % nblk
            rows = _dil_rows(r, n, d)
            r0 = pl.multiple_of(n * 128, 128)
            qr = _rope(q_ref[rows, :].astype(F32), cos_ref[rows, :], sin_ref[rows, :], HD // 2).astype(BF16)
            q_s[r, pl.ds(r0, 128), :] = qr
            win, mask = _window(r0, nblk)
            kw = k_s[r, win, :]
            vw = v_s[r, win, :]
            sc = lax.dot_general(qr, kw, NT, preferred_element_type=F32) * A_SCALE
            sc = jnp.where(mask, sc, NEG)
            m = jnp.max(sc, axis=1, keepdims=True)
            p = jnp.exp(sc - m)
            l = jnp.sum(p, axis=1, keepdims=True)
            o_ref[rows, :] = jnp.dot(p.astype(BF16), vw, preferred_element_type=F32) / l
            l_ref[rows, :] = jnp.broadcast_to(m + jnp.log(l), (128, HD))
            return carry

        lax.fori_loop(0, d * nblk, step, 0, unroll=8 if d == 1 else 4)

    def col(off):
        return lambda b, h: (b, 0, off // HD + HPG * g + h)

    blk = (None, S, HD)
    tab = pl.BlockSpec((S, HD), lambda b, h: (0, 0))
    out = pl.BlockSpec(blk, lambda b, h: (b, 0, h))
    kv = pl.BlockSpec((None, None, d, ln + 128, HD), lambda b, h: (b, h, 0, 0, 0))
    qq = pl.BlockSpec((None, None, d, ln, HD), lambda b, h: (b, h, 0, 0, 0))
    o, l, kr, vr, qr = pl.pallas_call(
        body, name=f"attn_fwd_d{d}", grid=(NB, HPG),
        in_specs=[pl.BlockSpec(blk, col(QA)), pl.BlockSpec(blk, col(KA)), pl.BlockSpec(blk, col(VA)), tab, tab],
        out_specs=[out, out, kv, kv, qq],
        out_shape=[jax.ShapeDtypeStruct((NB, S, HPG * HD), F32)] * 2
        + [jax.ShapeDtypeStruct((NB, HPG, d, ln + 128, HD), BF16)] * 2 + [jax.ShapeDtypeStruct((NB, HPG, d, ln, HD), BF16)],
        scratch_shapes=[pltpu.VMEM((S, HD), F32)] * (3 if d > 1 else 0),
        compiler_params=_cp(("parallel", "parallel")),
    )(proj_v, proj_v, proj_v, cosf, sinf)
    return o.reshape(T, HPG * HD), l.reshape(T, HPG * HD), (qr, kr, vr)


def _attn_bwd(qkv, cosf, sinf, da, lse, delta, dproj, g):
    d = DIL[g]
    ln = S // d
    nblk = ln // 128
    dproj_v = dproj.reshape(NB, S, NC)
    da_v = da.reshape(NB, S, AW)
    lse_v = lse.reshape(NB, S, HPG * HD)
    delta_v = delta.reshape(NB, S, HPG * HD)

    def body(q_s, k_s, v_s, cos_ref, sin_ref, da_ref, lse_ref, dl_ref, dp_in_ref, o_ref, stg, dk_s, dv_s):
        del dp_in_ref
        w = pl.program_id(2)

        def emit():
            def cast(n, carry):
                rows = pl.ds(pl.multiple_of(n * 256, 256), 256)
                o_ref[rows, :] = stg[rows, :].astype(BF16)
                return carry

            lax.fori_loop(0, S // 256, cast, 0)

        @pl.when(w == 0)
        def _():
            dk_s[...] = jnp.zeros_like(dk_s)
            dv_s[...] = jnp.zeros_like(dv_s)

            def step(i, carry):
                r, n = i // nblk, i % nblk
                rows = _dil_rows(r, n, d)
                r0 = pl.multiple_of(n * 128, 128)
                win, mask = _window(r0, nblk)
                cos, sin = cos_ref[rows, :], sin_ref[rows, :]
                qr = q_s[r, pl.ds(r0, 128), :]
                kw = k_s[r, win, :]
                vw = v_s[r, win, :]
                sc = lax.dot_general(qr, kw, NT, preferred_element_type=F32) * A_SCALE
                sc = jnp.where(mask, sc, NEG)
                p = jnp.exp(sc - lse_ref[rows, :][:, :1])
                da_b = da_ref[rows, :].astype(BF16)
                dp = lax.dot_general(da_b, vw, NT, preferred_element_type=F32)
                ds_b = (p * (dp - dl_ref[rows, :][:, :1]) * A_SCALE).astype(BF16)
                p_b = p.astype(BF16)
                dq = jnp.dot(ds_b, kw, preferred_element_type=F32)
                stg[rows, :] = _rope_t(dq, cos, sin, HD // 2)
                dk_s[r, win, :] += lax.dot_general(ds_b, qr, TN, preferred_element_type=F32)
                dv_s[r, win, :] += lax.dot_general(p_b, da_b, TN, preferred_element_type=F32)
                return carry

            lax.fori_loop(0, d * nblk, step, 0, unroll=8 if d == 1 else 4)
            emit()

        @pl.when(w == 1)
        def _():
            def put(i, carry):
                r, n = i // nblk, i % nblk
                rows = _dil_rows(r, n, d)
                src = pl.ds(pl.multiple_of(n * 128 + 128, 128), 128)
                stg[rows, :] = _rope_t(dk_s[r, src, :], cos_ref[rows, :], sin_ref[rows, :], HD // 2)
                return carry

            lax.fori_loop(0, d * nblk, put, 0, unroll=4)
            emit()

        @pl.when(w == 2)
        def _():
            def put(i, carry):
                r, n = i // nblk, i % nblk
                src = pl.ds(pl.multiple_of(n * 128 + 128, 128), 128)
                stg[_dil_rows(r, n, d), :] = dv_s[r, src, :]
                return carry

            lax.fori_loop(0, d * nblk, put, 0, unroll=4)
            emit()

    def col(off):
        return lambda b, h, w: (ahead(b, h, w)[0], 0, off // HD + HPG * g + ahead(b, h, w)[1])

    def ahead(b, h, w):
        flat = jnp.minimum(b * HPG + h + jnp.where(w > 0, 1, 0), NB * HPG - 1)
        return flat // HPG, flat % HPG

    blk = (None, S, HD)
    tab = pl.BlockSpec((S, HD), lambda b, h, w: (0, 0))
    per_head = pl.BlockSpec(blk, lambda b, h, w: (ahead(b, h, w)[0], 0, ahead(b, h, w)[1]))
    kv = pl.BlockSpec((None, None, d, ln + 128, HD), lambda b, h, w: (*ahead(b, h, w), 0, 0, 0))
    qq = pl.BlockSpec((None, None, d, ln, HD), lambda b, h, w: (*ahead(b, h, w), 0, 0, 0))
    out = pl.pallas_call(
        body, name=f"attn_bwd_d{d}", grid=(NB, HPG, 3),
        in_specs=[qq, kv, kv, tab, tab,
                  pl.BlockSpec(blk, col(0)), per_head, per_head, pl.BlockSpec(memory_space=pl.ANY)],
        out_specs=pl.BlockSpec(blk, lambda b, h, w: (b, 0, (AW // HD) * w + HPG * g + h)),
        out_shape=jax.ShapeDtypeStruct(dproj_v.shape, BF16),
        input_output_aliases={8: 0},
        scratch_shapes=[pltpu.VMEM((S, HD), F32), pltpu.VMEM((d, ln + 128, HD), F32), pltpu.VMEM((d, ln + 128, HD), F32)],
        compiler_params=_cp(("arbitrary",) * 3),
    )(*qkv, cosf, sinf, da_v, lse_v, delta_v, dproj_v)
    return out.reshape(T, NC)


def _attn_mix(proj, os_, ls_, dep=None):
    tr = 256
    gw = HPG * HD
    deps = [] if dep is None else [dep]

    def body(o0, o1, o2, l0, l1, l2, z_ref, *rest):
        cat_ref = rest[-1]
        m = jnp.maximum(jnp.maximum(l0[...], l1[...]), l2[...])
        e = [jnp.exp(l[...] - m) for l in (l0, l1, l2)]
        inv = 1.0 / (e[0] + e[1] + e[2])
        for gi, o in enumerate((o0, o1, o2)):
            z = z_ref[:, gi * gw:(gi + 1) * gw].astype(F32)
            cat_ref[:, gi * gw:(gi + 1) * gw] = (o[...] * (e[gi] * inv) * (z * _sigmoid(z))).astype(BF16)

    grp = pl.BlockSpec((tr, gw), lambda i: (i, 0))
    return pl.pallas_call(
        body, name="attn_mix", grid=(T // tr,),
        in_specs=[grp] * 6 + [pl.BlockSpec((tr, AW), lambda i: (i, ZA // AW))] + [ANY_SPEC] * len(deps),
        out_specs=pl.BlockSpec((tr, AW), lambda i: (i, 0)),
        out_shape=jax.ShapeDtypeStruct((T, D), BF16),
        compiler_params=_cp(("parallel",)),
    )(*os_, *ls_, proj, *deps)


def _attn_mix_bwd(dcat, proj, os_, ls_, dep=None):
    tr = 256
    gw = HPG * HD
    deps = [] if dep is None else [dep]

    def body(dy_ref, o0, o1, o2, l0, l1, l2, z_ref, *rest):
        da_ref, lse_ref, dl_ref, dz_ref = rest[len(deps):]
        m = jnp.maximum(jnp.maximum(l0[...], l1[...]), l2[...])
        e = [jnp.exp(l[...] - m) for l in (l0, l1, l2)]
        den = e[0] + e[1] + e[2]
        inv = 1.0 / den
        lse_ref[...] = m + jnp.log(den)
        acc = jnp.zeros((tr, gw), F32)
        for gi, o in enumerate((o0, o1, o2)):
            cols = slice(gi * gw, (gi + 1) * gw)
            z = z_ref[:, cols].astype(F32)
            dy = dy_ref[:, cols]
            sg = _sigmoid(z)
            a = o[...] * (e[gi] * inv)
            da = dy * (z * sg)
            da_ref[:, cols] = da
            dz_ref[:, cols] = (dy * a * (sg * (1.0 + z * (1.0 - sg)))).astype(BF16)
            acc = acc + da * a
        for hh in range(HPG):
            cols = slice(hh * HD, (hh + 1) * HD)
            dl_ref[:, cols] = jnp.broadcast_to(jnp.sum(acc[:, cols], axis=1, keepdims=True), (tr, HD))

    grp = pl.BlockSpec((tr, gw), lambda i: (i, 0))
    return pl.pallas_call(
        body, name="attn_mix_bwd", grid=(T // tr,),
        in_specs=[pl.BlockSpec((tr, AW), lambda i: (i, 0))] + [grp] * 6 + [pl.BlockSpec((tr, AW), lambda i: (i, ZA // AW))]
        + [pl.BlockSpec(memory_space=pl.ANY)] * len(deps),
        out_specs=[pl.BlockSpec((tr, AW), lambda i: (i, 0)), grp, grp, pl.BlockSpec((tr, AW), lambda i: (i, ZA // AW))],
        out_shape=[jax.ShapeDtypeStruct((T, AW), F32), jax.ShapeDtypeStruct((T, gw), F32),
                   jax.ShapeDtypeStruct((T, gw), F32), jax.ShapeDtypeStruct((T, NC), BF16)],
        compiler_params=_cp(("parallel",)),
    )(dcat, *os_, *ls_, proj, *deps)


CT = 256


def _shift_down(x, n):
    rows = lax.broadcasted_iota(jnp.int32, x.shape, 0)
    return jnp.where(rows >= n, pltpu.roll(x, n, 0), 0.0)


def _shift_up(x, n):
    rows = lax.broadcasted_iota(jnp.int32, x.shape, 0)
    return jnp.where(rows < x.shape[0] - n, pltpu.roll(x, x.shape[0] - n, 0), 0.0)


def _conv_fwd(proj, conv_w, cat):
    proj_v = proj.reshape(NB, S, NC)
    cat_v = cat.reshape(NB, S, D)

    def body(u_ref, b_ref, c_ref, z_ref, w_ref, cat_in, o_ref):
        del cat_in
        cu = c_ref[...].astype(F32) * u_ref[...].astype(F32)
        cv = _shift_down(cu, 2) * w_ref[0:1, :] + _shift_down(cu, 1) * w_ref[1:2, :] + cu * w_ref[2:3, :]
        z = z_ref[...].astype(F32)
        o_ref[...] = (b_ref[...].astype(F32) * cv * (z * _sigmoid(z))).astype(BF16)

    def seg(off):
        return pl.BlockSpec((None, S, CT), lambda b, j: (b, 0, off // CT + j))

    out = pl.pallas_call(
        body, name="conv_fwd", grid=(NB, CW // CT),
        in_specs=[seg(UC), seg(BC), seg(CC), seg(ZC), pl.BlockSpec((3, CT), lambda b, j: (0, j)),
                  pl.BlockSpec(memory_space=pl.ANY)],
        out_specs=pl.BlockSpec((None, S, CT), lambda b, j: (b, 0, AW // CT + j)),
        out_shape=jax.ShapeDtypeStruct((NB, S, D), BF16),
        input_output_aliases={5: 0},
        compiler_params=_cp(("parallel", "parallel")),
    )(proj_v, proj_v, proj_v, proj_v, conv_w, cat_v)
    return out.reshape(T, D)


def _conv_bwd(dcat, proj, conv_w, dproj, dep=None):
    deps = [] if dep is None else [dep]
    proj_v = proj.reshape(NB, S, NC)
    dproj_v = dproj.reshape(NB, S, NC)
    dcat_v = dcat.reshape(NB, S, D)

    def body(dy_ref, u_ref, b_ref, c_ref, z_ref, w_ref, *rest):
        o_ref, dw_ref, st = rest[1 + len(deps):]
        b = pl.program_id(1)
        w = pl.program_id(2)

        @pl.when((b == 0) & (w == 0))
        def _():
            dw_ref[...] = jnp.zeros_like(dw_ref)

        @pl.when(w == 0)
        def _():
            u, c, z, bb = (r[...].astype(F32) for r in (u_ref, c_ref, z_ref, b_ref))
            dy = dy_ref[...]
            cu = c * u
            s1 = _shift_down(cu, 1)
            s2 = _shift_down(cu, 2)
            cv = s2 * w_ref[0:1, :] + s1 * w_ref[1:2, :] + cu * w_ref[2:3, :]
            sg = _sigmoid(z)
            sz = z * sg
            dcv = dy * bb * sz
            st[0] = dy * cv * sz
            st[2] = dy * bb * cv * (sg * (1.0 + z * (1.0 - sg)))
            dw_ref[0:1, :] += jnp.sum(dcv * s2, axis=0, keepdims=True)
            dw_ref[1:2, :] += jnp.sum(dcv * s1, axis=0, keepdims=True)
            dw_ref[2:3, :] += jnp.sum(dcv * cu, axis=0, keepdims=True)
            dcu = dcv * w_ref[2:3, :] + _shift_up(dcv, 1) * w_ref[1:2, :] + _shift_up(dcv, 2) * w_ref[0:1, :]
            st[1] = dcu * u
            o_ref[...] = (dcu * c).astype(BF16)

        for k in range(3):
            @pl.when(w == k + 1)
            def _(k=k):
                o_ref[...] = st[k].astype(BF16)

    def ahead(j, b, w):
        flat = jnp.minimum(j * NB + b + jnp.where(w > 0, 1, 0), (CW // CT) * NB - 1)
        return flat // NB, flat % NB

    def seg(off):
        return pl.BlockSpec((None, S, CT), lambda j, b, w: (ahead(j, b, w)[1], 0, off // CT + ahead(j, b, w)[0]))

    out, dw = pl.pallas_call(
        body, name="conv_bwd", grid=(CW // CT, NB, 4),
        in_specs=[seg(AW), seg(UC), seg(BC), seg(CC), seg(ZC),
                  pl.BlockSpec((3, CT), lambda j, b, w: (0, ahead(j, b, w)[0])),
                  pl.BlockSpec(memory_space=pl.ANY)] + [pl.BlockSpec(memory_space=pl.ANY)] * len(deps),
        out_specs=[pl.BlockSpec((None, S, CT), lambda j, b, w: (b, 0, (UC + w * CW) // CT + j)),
                   pl.BlockSpec((3, CT), lambda j, b, w: (0, j))],
        out_shape=[jax.ShapeDtypeStruct((NB, S, NC), BF16), jax.ShapeDtypeStruct((3, CW), F32)],
        input_output_aliases={6: 0},
        scratch_shapes=[pltpu.VMEM((3, S, CT), F32)],
        compiler_params=_cp(("arbitrary",) * 3),
    )(dcat_v, proj_v, proj_v, proj_v, proj_v, conv_w, dproj_v, *deps)
    return out.reshape(T, NC), dw


XT = 1024


def _cross_fwd(proj, mkv, cosq, sinq, cosm, sinm, cat):
    proj_v = proj.reshape(NB, S, NC)
    mkv_v = mkv.reshape(NB, MLEN, 2 * XW)
    cat_v = cat.reshape(NB, S, D)

    def body(q_ref, z_ref, mk_ref, mv_ref, cq, sq, cm, sm, cat_in, o_ref):
        del cat_in
        mkr = _rope(mk_ref[...], cm[...], sm[...], XHD // 2).astype(BF16)
        qr = _rope(q_ref[...].astype(F32), cq[...], sq[...], XHD // 2).astype(BF16)
        sc = lax.dot_general(qr, mkr, NT, preferred_element_type=F32) * X_SCALE
        p = jnp.exp(sc - jnp.max(sc, axis=1, keepdims=True))
        p = p / jnp.sum(p, axis=1, keepdims=True)
        ox = jnp.dot(p.astype(BF16), mv_ref[...].astype(BF16), preferred_element_type=F32)
        z = z_ref[...].astype(F32)
        o_ref[...] = (ox * (z * _sigmoid(z))).astype(BF16)

    def seg(off):
        return pl.BlockSpec((None, XT, XHD), lambda b, h, t: (b, t, off // XHD + h))

    qtab = pl.BlockSpec((XT, XHD), lambda b, h, t: (t, 0))
    mtab = pl.BlockSpec((MLEN, XHD), lambda b, h, t: (0, 0))
    out = pl.pallas_call(
        body, name="cross_fwd", grid=(NB, NXH, S // XT),
        in_specs=[seg(QX), seg(ZX),
                  pl.BlockSpec((None, MLEN, XHD), lambda b, h, t: (b, 0, h)),
                  pl.BlockSpec((None, MLEN, XHD), lambda b, h, t: (b, 0, NXH + h)),
                  qtab, qtab, mtab, mtab, pl.BlockSpec(memory_space=pl.ANY)],
        out_specs=pl.BlockSpec((None, XT, XHD), lambda b, h, t: (b, t, (AW + CW) // XHD + h)),
        out_shape=jax.ShapeDtypeStruct((NB, S, D), BF16),
        input_output_aliases={8: 0},
        compiler_params=_cp(("parallel",) * 3),
    )(proj_v, proj_v, mkv_v, mkv_v, cosq, sinq, cosm, sinm, cat_v)
    return out.reshape(T, D)


def _cross_bwd(dcat, proj, mkv, cosq, sinq, cosm, sinm, dproj):
    proj_v = proj.reshape(NB, S, NC)
    dproj_v = dproj.reshape(NB, S, NC)
    dcat_v = dcat.reshape(NB, S, D)
    mkv_v = mkv.reshape(NB, MLEN, 2 * XW)
    nt = S // XT

    def body(dy_ref, q_ref, z_ref, mk_ref, mv_ref, cq, sq, cm, sm, dp_in, o_ref, dmk_ref, dmv_ref, dz_s):
        del dp_in
        t = pl.program_id(2)
        w = pl.program_id(3)

        @pl.when((t == 0) & (w == 0))
        def _():
            dmk_ref[...] = jnp.zeros_like(dmk_ref)
            dmv_ref[...] = jnp.zeros_like(dmv_ref)

        @pl.when(w == 0)
        def _():
            mkr = _rope(mk_ref[...], cm[...], sm[...], XHD // 2).astype(BF16)
            mv_b = mv_ref[...].astype(BF16)
            qr = _rope(q_ref[...].astype(F32), cq[...], sq[...], XHD // 2).astype(BF16)
            sc = lax.dot_general(qr, mkr, NT, preferred_element_type=F32) * X_SCALE
            p = jnp.exp(sc - jnp.max(sc, axis=1, keepdims=True))
            p = p / jnp.sum(p, axis=1, keepdims=True)
            p_b = p.astype(BF16)
            ox = jnp.dot(p_b, mv_b, preferred_element_type=F32)
            z = z_ref[...].astype(F32)
            dy = dy_ref[...]
            sg = _sigmoid(z)
            dz_s[...] = dy * ox * (sg * (1.0 + z * (1.0 - sg)))
            dox_b = (dy * (z * sg)).astype(BF16)
            dp = lax.dot_general(dox_b, mv_b, NT, preferred_element_type=F32)
            ds_b = (p * (dp - jnp.sum(dp * p, axis=1, keepdims=True)) * X_SCALE).astype(BF16)
            dq = jnp.dot(ds_b, mkr, preferred_element_type=F32)
            o_ref[...] = _rope_t(dq, cq[...], sq[...], XHD // 2).astype(BF16)
            dmk_ref[...] += lax.dot_general(ds_b, qr, TN, preferred_element_type=F32)
            dmv_ref[...] += lax.dot_general(p_b, dox_b, TN, preferred_element_type=F32)

        @pl.when(w == 1)
        def _():
            o_ref[...] = dz_s[...].astype(BF16)

        @pl.when((t == nt - 1) & (w == 1))
        def _():
            dmk_ref[...] = _rope_t(dmk_ref[...], cm[...], sm[...], XHD // 2)

    def ahead(b, h, t, w):
        flat = jnp.minimum((b * NXH + h) * nt + t + jnp.where(w > 0, 1, 0), NB * NXH * nt - 1)
        return flat // (NXH * nt), (flat // nt) % NXH, flat % nt

    def seg(off):
        def index(b, h, t, w):
            b2, h2, t2 = ahead(b, h, t, w)
            return b2, t2, off // XHD + h2
        return pl.BlockSpec((None, XT, XHD), index)

    def mem(off):
        def index(b, h, t, w):
            b2, h2, _ = ahead(b, h, t, w)
            return b2, 0, off + h2
        return pl.BlockSpec((None, MLEN, XHD), index)

    qtab = pl.BlockSpec((XT, XHD), lambda b, h, t, w: (ahead(b, h, t, w)[2], 0))
    mtab = pl.BlockSpec((MLEN, XHD), lambda b, h, t, w: (0, 0))
    macc = pl.BlockSpec((None, MLEN, XHD), lambda b, h, t, w: (b, 0, h))
    out, dmk, dmv = pl.pallas_call(
        body, name="cross_bwd", grid=(NB, NXH, nt, 2),
        in_specs=[seg(AW + CW), seg(QX), seg(ZX), mem(0), mem(NXH),
                  qtab, qtab, mtab, mtab, pl.BlockSpec(memory_space=pl.ANY)],
        out_specs=[pl.BlockSpec((None, XT, XHD), lambda b, h, t, w: (b, t, (QX + w * XW) // XHD + h)), macc, macc],
        out_shape=[jax.ShapeDtypeStruct((NB, S, NC), BF16), jax.ShapeDtypeStruct((NB, MLEN, XW), F32),
                   jax.ShapeDtypeStruct((NB, MLEN, XW), F32)],
        input_output_aliases={9: 0},
        scratch_shapes=[pltpu.VMEM((XT, XHD), F32)],
        compiler_params=_cp(("arbitrary",) * 4),
    )(dcat_v, proj_v, proj_v, mkv_v, mkv_v, cosq, sinq, cosm, sinm, dproj_v)
    return out.reshape(T, NC), dmk, dmv


def _local_step(x, mem, pre_norm, conv_w, mem_norm, post_norm, tgt, chip, core, comm, moments=None):
    x2 = x.reshape(T, D)
    mem2 = mem.reshape(NB * MLEN, D)
    tgt2 = tgt.reshape(T, D)
    cosa, sina = _rope_tables(jnp.arange(S), HD // 2)
    cosq, sinq = _rope_tables(jnp.arange(S) + MLEN, XHD // 2)
    cosm, sinm = _rope_tables(jnp.arange(MLEN), XHD // 2)

    h = _rms_fwd(x2, pre_norm, "pre_norm_fwd", dep=comm.gather_started())
    memn = _rms_fwd(mem2, mem_norm, "mem_norm_fwd")
    proj = _proj_part(h, comm.w_in_own(), chip, (0,), "proj_own")
    wg_in, tok = comm.w_in_near(after=[proj, memn, conv_w])
    proj = _proj_part(h, wg_in, chip, (REL_XOR[0], REL_XOR[1]), "proj_near", prev=proj, dep=tok)
    wg_in, tok = comm.w_in_all(after=proj)
    proj = _proj_part(h, wg_in, chip, (REL_XOR[2],), "proj_far", prev=proj, dep=tok)
    fw = [_attn_fwd(proj, cosa, sina, g) for g in range(3)]
    os_ = [f[0] for f in fw]
    ls_ = [f[1] for f in fw]
    tok = comm.w_rest_landed(after=ls_)
    cat = _attn_mix(proj, os_, ls_, dep=tok)
    cat = _conv_fwd(proj, conv_w, cat)
    wg_kv, wg_out = comm.w_rest(after=[cat])
    mkv = _mm_nn(memn, wg_kv, "mkv", NB * MLEN, 1024, 512)
    cat = _cross_fwd(proj, mkv, cosq, sinq, cosm, sinm, cat)
    y = _mm_nn(cat, wg_out, "out_proj", 1024, 2048, 2048)
    dy, dout, d_post, loss = _post_norm_loss(y, x2, tgt2, post_norm)

    dcat = _mm_nt(dy, wg_out, "dcat", 1024, 2048, 2048)
    g_send = _grad_rows(cat, dy,1 - core, BF16, "grad_w_out_send")
    sent = comm.sibling_start(g_send.reshape(NCHIP, D // NCHIP // 2, D), "a")
    g_keep = _grad_rows(cat, dy,core, BF16, "grad_w_out_keep", dep=sent[-1])
    red_a = comm.reduce_start_summed(g_keep.reshape(NCHIP, D // NCHIP // 2, D), sent, "a")
    da, lse, delta, dproj = _attn_mix_bwd(dcat, proj, os_, ls_, dep=red_a[-1])
    for g in range(3):
        dproj = _attn_bwd(fw[g][2], cosa, sina, da, lse, delta, dproj, g)
    dproj, dmk, dmv = _cross_bwd(dcat, proj, mkv, cosq, sinq, cosm, sinm, dproj)
    dmkv = jnp.concatenate([dmk, dmv], axis=-1).reshape(NB * MLEN, 2 * XW)
    dmkv_b = dmkv.astype(BF16)
    gk_send = _grad_rows(memn, dmkv_b, 1 - core, BF16, "grad_w_mem_kv_send")
    sent = comm.sibling_start(gk_send.reshape(NCHIP, D // NCHIP // 2, 2 * XW), "k")
    gk_keep = _grad_rows(memn, dmkv_b, core, BF16, "grad_w_mem_kv_keep", dep=sent[-1])
    dmemn = _mm_nt(dmkv_b, wg_kv, "dmemn", NB * MLEN, 1024, 512)
    d_mem = _norm_gain_grad(dmemn, mem2, "mem_norm_bwd")
    red_k = comm.reduce_start_summed(gk_keep.reshape(NCHIP, D // NCHIP // 2, 2 * XW), sent, "k", after=d_mem)
    dproj, d_conv = _conv_bwd(dcat, proj, conv_w, dproj, dep=red_k[-1])
    (r_out,) = comm.reduce_finish(red_a, "a", after=d_conv)
    (r_kv,) = comm.reduce_finish(red_k, "k", after=d_conv)

    g_send = _grad_w_in(h, dproj, 1 - core, BF16, "grad_w_in_send", dep=r_kv)
    sent = comm.sibling_start(g_send.reshape(NCHIP, D // 2, WB), "b")
    q = _grad_w_in(h, dproj, core, BF16, "grad_w_in_keep_0", dep=sent[-1], tiles=(0, 1))
    got = comm.sibling_wait(sent, [q], "b")
    q = _pair_sum_rows(q.reshape(NCHIP, D // 2, WB), got, "pair_sum_b", rows=(0, GW_TM))
    red_b0 = comm.scatter_rows(q, "b0", (0, GW_TM))
    q = _grad_w_in(h, dproj, core, BF16, "grad_w_in_keep_1", dep=red_b0[4], tiles=(1, 1),
                   addend=got.reshape(NCHIP * D // 2, WB), prev=red_b0[2][0].reshape(NCHIP * D // 2, WB))
    red_b1 = comm.scatter_rows(q.reshape(NCHIP, D // 2, WB), "b1", (GW_TM, GW_TM), before=red_b0)
    sets = [] if moments is None else [(moments["w_out"][0], r_out, *moments["w_out"][1:]),
                                      (moments["w_mem_kv"][0], r_kv, *moments["w_mem_kv"][1:])]
    dh, updates = _dh(dproj, wg_in, dep=red_b1[4], adam=sets)
    r_in = comm.reduce_finish_start([red_b0, red_b1], "b", after=dh)
    grad_x, d_pre = _pre_norm_bwd(dh, x2, pre_norm, dout, dep=r_in[-1])
    return loss, grad_x.reshape(NB, S, D), d_pre, d_mem, d_post, d_conv, r_in, (r_kv, r_out), updates


def _adamw(w, g, m, v, name):
    rows, cols = w.shape
    tr = rows if rows <= 512 else 512
    tc = cols if cols <= 1024 else 1024
    if cols % tc:
        tc = 896

    blk = pl.BlockSpec((tr, tc), lambda i, j: (i, j))
    sds = jax.ShapeDtypeStruct((rows, cols), F32)
    def body(*refs):
        _adamw_math(*refs)

    return pl.pallas_call(
        body, name=name, grid=(rows // tr, cols // tc),
        in_specs=[blk] * 4, out_specs=[blk] * 4, out_shape=[sds] * 4,
        compiler_params=_cp(("parallel", "parallel")),
    )(w, g, m, v)


def _place():
    return lax.axis_index("x"), lax.axis_index("y"), lax.axis_index("c")


def _other_chips(x, y):
    return [(1 - x, y), (x, 1 - y), (1 - x, 1 - y)]


def _tile_cols(cols):
    return cols if cols <= 1024 else (1024 if cols % 1024 == 0 else 896)


def _cast_own(w, chip, name, half=None, prev=None):
    rows, cols = w.shape
    tr, tc = 512, _tile_cols(cols)
    nrt = rows // tr if half is None else rows // tr // 2
    index = jnp.stack([chip, 0 if half is None else half]).astype(jnp.int32)
    extra = [] if prev is None else [prev]

    def body(ix_ref, w_ref, *rest):
        rest[-1][...] = w_ref[...].astype(BF16)

    grid_spec = pltpu.PrefetchScalarGridSpec(
        num_scalar_prefetch=1, grid=(nrt, cols // tc),
        in_specs=[pl.BlockSpec((tr, tc), lambda i, j, ix: (ix[1] * nrt + i, j))] + [ANY_SPEC] * len(extra),
        out_specs=pl.BlockSpec((None, tr, tc), lambda i, j, ix: (ix[0], ix[1] * nrt + i, j)))
    return pl.pallas_call(
        body, name=name, grid_spec=grid_spec,
        out_shape=jax.ShapeDtypeStruct((NCHIP, rows, cols), BF16),
        input_output_aliases={} if prev is None else {2: 0},
        compiler_params=_cp(("parallel", "parallel")),
    )(index, w, *extra)


HBM_SPEC = pl.BlockSpec(memory_space=pltpu.HBM)
SEM_SPEC = pl.BlockSpec(memory_space=pltpu.SEMAPHORE)
ANY_SPEC = pl.BlockSpec(memory_space=pl.ANY)
EFFECT = pltpu.SideEffectType.DATAFLOW_SIDE_EFFECTING
TOKEN = jax.ShapeDtypeStruct((8, 128), F32)


def _half(ref, chip, hc):
    hr = ref.shape[1] // 2
    return ref.at[chip, pl.ds(hc * hr, hr), :]


NEAR = (0, 1)
FAR = (2,)
REL_XOR = (2, 1, 3)


def _gather_copies(refs, send_sems, recv_sems, rels):
    x, y, c = _place()
    chips = _other_chips(x, y)
    out, inc = [], []
    for a, ref in enumerate(refs):
        for p, j in enumerate(rels):
            px, py = chips[j]
            mine = _half(ref, 2 * x + y, c)
            theirs = _half(ref, 2 * px + py, c)
            sems = dict(send_sem=send_sems.at[len(rels) * a + p], recv_sem=recv_sems.at[len(rels) * a + p],
                        device_id=(px, py, c), device_id_type=MESH)
            out.append(pltpu.make_async_remote_copy(src_ref=mine, dst_ref=mine, **sems))
            inc.append(pltpu.make_async_remote_copy(src_ref=theirs, dst_ref=theirs, **sems))
    return out, inc


def _gather_start(bufs, groups, name):
    n = len(bufs)
    ng = len(groups)

    def body(*refs):
        ins = refs[:n]
        token = refs[-1]
        for gi, rels in enumerate(groups):
            out, _ = _gather_copies(ins, refs[n + 2 * gi], refs[n + 2 * gi + 1], rels)
            for cp in out:
                cp.start()
        token[...] = jnp.zeros_like(token)

    sems = []
    for rels in groups:
        sems += [pltpu.SemaphoreType.DMA((len(rels) * n,))] * 2
    res = pl.pallas_call(
        body, name=name,
        in_specs=[HBM_SPEC] * n,
        out_specs=[SEM_SPEC] * (2 * ng) + [HBM_SPEC] * n + [pl.BlockSpec(memory_space=pltpu.VMEM)],
        out_shape=sems + [pltpu.HBM(b.shape, b.dtype) for b in bufs] + [TOKEN],
        input_output_aliases={a: 2 * ng + a for a in range(n)},
        compiler_params=pltpu.CompilerParams(has_side_effects=EFFECT),
    )(*[pltpu.with_memory_space_constraint(b, pltpu.HBM) for b in bufs])
    return [(res[2 * gi], res[2 * gi + 1]) for gi in range(ng)], list(res[2 * ng:2 * ng + n]), res[-1]


def _gather_wait(bufs, sems, rels, after, name):
    n = len(bufs)
    send_sems, recv_sems = sems
    after = list(after) if isinstance(after, (list, tuple)) else [after]

    def body(*refs):
        ins = refs[:n]
        out, inc = _gather_copies(ins, refs[n], refs[n + 1], rels)
        for cp in out:
            cp.wait_send()
        for cp in inc:
            cp.wait_recv()

    return pl.pallas_call(
        body, name=name,
        in_specs=[HBM_SPEC] * n + [SEM_SPEC, SEM_SPEC] + [ANY_SPEC] * len(after),
        out_specs=[HBM_SPEC] * n,
        out_shape=[pltpu.HBM(b.shape, b.dtype) for b in bufs],
        input_output_aliases={a: a for a in range(n)},
        compiler_params=pltpu.CompilerParams(has_side_effects=EFFECT),
    )(*bufs, send_sems, recv_sems, *after)


def _forward_halves(bufs, rels, name):
    n = len(bufs)

    def body(*refs):
        cps, waits = _forward_copies(refs[n:2 * n], rels, refs[2 * n], refs[2 * n + 1])
        for cp in cps:
            cp.start()
        for cp in waits:
            cp.wait_recv()
        for cp in cps:
            cp.wait_send()

    return pl.pallas_call(
        body, name=name,
        in_specs=[ANY_SPEC] * n, out_specs=[ANY_SPEC] * n,
        out_shape=[jax.ShapeDtypeStruct(s.shape, s.dtype) for s in bufs],
        input_output_aliases={a: a for a in range(n)},
        scratch_shapes=[pltpu.SemaphoreType.DMA((len(rels) * n,)), pltpu.SemaphoreType.DMA((len(rels) * n,))],
    )(*bufs)


def _forward_copies(refs, rels, send_sems, recv_sems):
    x, y, c = _place()
    chips = _other_chips(x, y)
    cps, waits = [], []
    for a, ref in enumerate(refs):
        for p, j in enumerate(rels):
            px, py = chips[j]
            sems = dict(send_sem=send_sems.at[len(rels) * a + p], recv_sem=recv_sems.at[len(rels) * a + p],
                        device_id=(x, y, 1 - c), device_id_type=MESH)
            got = _half(ref, 2 * px + py, c)
            want = _half(ref, 2 * px + py, 1 - c)
            cps.append(pltpu.make_async_remote_copy(src_ref=got, dst_ref=got, **sems))
            waits.append(pltpu.make_async_remote_copy(src_ref=want, dst_ref=want, **sems))
    return cps, waits


def _forward_start(bufs, rels, name):
    n = len(bufs)

    def body(*refs):
        cps, _ = _forward_copies(refs[:n], rels, refs[n], refs[n + 1])
        for cp in cps:
            cp.start()
        refs[-1][...] = jnp.zeros_like(refs[-1])

    res = pl.pallas_call(
        body, name=name,
        in_specs=[HBM_SPEC] * n,
        out_specs=[SEM_SPEC, SEM_SPEC] + [HBM_SPEC] * n + [pl.BlockSpec(memory_space=pltpu.VMEM)],
        out_shape=[pltpu.SemaphoreType.DMA((len(rels) * n,))] * 2 + [pltpu.HBM(b.shape, b.dtype) for b in bufs] + [TOKEN],
        input_output_aliases={a: 2 + a for a in range(n)},
        compiler_params=pltpu.CompilerParams(has_side_effects=EFFECT),
    )(*[pltpu.with_memory_space_constraint(b, pltpu.HBM) for b in bufs])
    return (res[0], res[1]), list(res[2:2 + n]), res[-1]


def _forward_wait(bufs, sems, rels, after, name):
    n = len(bufs)

    def body(*refs):
        cps, waits = _forward_copies(refs[:n], rels, refs[n], refs[n + 1])
        for cp in cps:
            cp.wait_send()
        for cp in waits:
            cp.wait_recv()

    return pl.pallas_call(
        body, name=name,
        in_specs=[HBM_SPEC] * n + [SEM_SPEC, SEM_SPEC] + [ANY_SPEC] * len(after),
        out_specs=[HBM_SPEC] * n,
        out_shape=[pltpu.HBM(b.shape, b.dtype) for b in bufs],
        input_output_aliases={a: a for a in range(n)},
        compiler_params=pltpu.CompilerParams(has_side_effects=EFFECT),
    )(*bufs, sems[0], sems[1], *after)


def _sibling_copy(src, land, send_sems, recv_sems):
    x, y, c = _place()
    return pltpu.make_async_remote_copy(src_ref=src, dst_ref=land, send_sem=send_sems.at[0], recv_sem=recv_sems.at[0],
                                        device_id=(x, y, 1 - c), device_id_type=MESH)


def _sibling_start(part, name):
    def body(src, land, send_sems, recv_sems, src_thru, land_thru, token):
        _sibling_copy(src, land, send_sems, recv_sems).start()
        token[...] = jnp.zeros_like(token)

    land = lax.empty(part.shape, part.dtype)
    return pl.pallas_call(
        body, name=name,
        in_specs=[HBM_SPEC] * 2,
        out_specs=[SEM_SPEC, SEM_SPEC, HBM_SPEC, HBM_SPEC, pl.BlockSpec(memory_space=pltpu.VMEM)],
        out_shape=[pltpu.SemaphoreType.DMA((1,)), pltpu.SemaphoreType.DMA((1,)), pltpu.HBM(part.shape, part.dtype),
                   pltpu.HBM(part.shape, part.dtype), TOKEN],
        input_output_aliases={0: 2, 1: 3},
        compiler_params=pltpu.CompilerParams(has_side_effects=EFFECT),
    )(pltpu.with_memory_space_constraint(part, pltpu.HBM), pltpu.with_memory_space_constraint(land, pltpu.HBM))


def _sibling_wait(state, after, name):
    send_sems, recv_sems, part, land, _ = state

    def body(src, land_ref, send_ref, recv_ref, *rest):
        cp = _sibling_copy(src, land_ref, send_ref, recv_ref)
        cp.wait_send()
        cp.wait_recv()

    return pl.pallas_call(
        body, name=name,
        in_specs=[HBM_SPEC, HBM_SPEC, SEM_SPEC, SEM_SPEC] + [ANY_SPEC] * len(after),
        out_specs=[HBM_SPEC, HBM_SPEC],
        out_shape=[pltpu.HBM(part.shape, part.dtype), pltpu.HBM(land.shape, land.dtype)],
        input_output_aliases={0: 0, 1: 1},
        compiler_params=pltpu.CompilerParams(has_side_effects=EFFECT),
    )(part, land, send_sems, recv_sems, *after)[1]


def _pair_sum_rows(keep, got, name, rows=None):
    nblk, nrows, cols = keep.shape
    tr, tc = 512, _tile_cols(cols)
    r0, nr = (0, nrows) if rows is None else rows

    def body(k_ref, g_ref, o_ref):
        o_ref[...] = (k_ref[...].astype(F32) + g_ref[...].astype(F32)).astype(BF16)

    blk = pl.BlockSpec((None, tr, tc), lambda b, i, j: (b, r0 // tr + i, j))
    return pl.pallas_call(
        body, name=name, grid=(nblk, nr // tr, cols // tc),
        in_specs=[blk, blk], out_specs=blk,
        out_shape=jax.ShapeDtypeStruct(keep.shape, BF16),
        input_output_aliases={} if rows is None else {0: 0},
        compiler_params=_cp(("parallel",) * 3),
    )(keep, got)


def _scatter_start(qs, name, rows=None, lands=None):
    n = len(qs)

    def body(*refs):
        ins, lands = refs[:n], refs[n:2 * n]
        token = refs[-1]
        for cp in _scatter_copies(ins, lands, refs[2 * n], refs[2 * n + 1], rows):
            cp.start()
        token[...] = jnp.zeros_like(token)

    if lands is None:
        lands = [lax.empty((3,) + q.shape[1:], q.dtype) for q in qs]
    res = pl.pallas_call(
        body, name=name,
        in_specs=[HBM_SPEC] * (2 * n),
        out_specs=[SEM_SPEC, SEM_SPEC] + [HBM_SPEC] * (2 * n) + [pl.BlockSpec(memory_space=pltpu.VMEM)],
        out_shape=[pltpu.SemaphoreType.DMA((3 * n,)), pltpu.SemaphoreType.DMA((3 * n,))]
        + [pltpu.HBM(b.shape, b.dtype) for b in qs + lands] + [TOKEN],
        input_output_aliases={a: 2 + a for a in range(2 * n)},
        compiler_params=pltpu.CompilerParams(has_side_effects=EFFECT),
    )(*[pltpu.with_memory_space_constraint(b, pltpu.HBM) for b in qs + lands])
    return res[0], res[1], list(res[2:2 + n]), list(res[2 + n:2 + 2 * n]), res[-1]


def _scatter_copies(ins, lands, send_sems, recv_sems, rows=None):
    x, y, c = _place()
    sel = slice(None) if rows is None else pl.ds(rows[0], rows[1])
    cps = []
    for a in range(len(ins)):
        for j, (px, py) in enumerate(_other_chips(x, y)):
            cps.append(pltpu.make_async_remote_copy(
                src_ref=ins[a].at[2 * px + py, sel, :], dst_ref=lands[a].at[j, sel, :],
                send_sem=send_sems.at[3 * a + j], recv_sem=recv_sems.at[3 * a + j], device_id=(px, py, c), device_id_type=MESH))
    return cps


def _scatter_wait(qs, lands, send_sems, recv_sems, after, name, rows=None):
    n = len(qs)

    def body(*refs):
        for cp in _scatter_copies(refs[:n], refs[n:2 * n], refs[2 * n], refs[2 * n + 1], rows):
            cp.wait_send()
            cp.wait_recv()

    res = pl.pallas_call(
        body, name=name,
        in_specs=[HBM_SPEC] * (2 * n) + [SEM_SPEC, SEM_SPEC, ANY_SPEC],
        out_specs=[HBM_SPEC] * (2 * n),
        out_shape=[pltpu.HBM(b.shape, b.dtype) for b in qs + lands],
        input_output_aliases={a: a for a in range(2 * n)},
        compiler_params=pltpu.CompilerParams(has_side_effects=EFFECT),
    )(*qs, *lands, send_sems, recv_sems, after)
    return list(res[:n]), list(res[n:])


def _chip_sum(q, got, name):
    _, hr, cols = got.shape
    tr, tc = 512, _tile_cols(cols)
    chip = 2 * lax.axis_index("x") + lax.axis_index("y")
    c = lax.axis_index("c")

    def body(idx_ref, q_ref, g_ref, o_ref):
        del idx_ref
        acc = q_ref[...].astype(F32)
        for i in range(3):
            acc = acc + g_ref[i].astype(F32)
        o_ref[...] = acc

    grid_spec = pltpu.PrefetchScalarGridSpec(
        num_scalar_prefetch=1, grid=(hr // tr, cols // tc),
        in_specs=[pl.BlockSpec((None, tr, tc), lambda i, j, ix: (ix[0], i, j)),
                  pl.BlockSpec((3, tr, tc), lambda i, j, ix: (0, i, j))],
        out_specs=pl.BlockSpec((None, tr, tc), lambda i, j, ix: (ix[1], i, j)))
    return pl.pallas_call(
        body, name=name, grid_spec=grid_spec,
        out_shape=jax.ShapeDtypeStruct((2, hr, cols), F32),
        compiler_params=_cp(("parallel", "parallel")),
    )(jnp.stack([chip, c]).astype(jnp.int32), q, got)


def _join_halves(bufs, name):
    n = len(bufs)

    def body(*refs):
        outs = refs[n:2 * n]
        send_sems, recv_sems = refs[2 * n:]
        x, y, c = _place()
        cps = []
        for a in range(n):
            cps.append(pltpu.make_async_remote_copy(src_ref=outs[a].at[c], dst_ref=outs[a].at[c], send_sem=send_sems.at[a],
                                                    recv_sem=recv_sems.at[a], device_id=(x, y, 1 - c), device_id_type=MESH))
        for cp in cps:
            cp.start()
        for a in range(n):
            theirs = outs[a].at[1 - c]
            pltpu.make_async_remote_copy(src_ref=theirs, dst_ref=theirs, send_sem=send_sems.at[a], recv_sem=recv_sems.at[a],
                                         device_id=(x, y, 1 - c), device_id_type=MESH).wait_recv()
        for cp in cps:
            cp.wait_send()

    hbm = pl.BlockSpec(memory_space=pl.ANY)
    return pl.pallas_call(
        body, name=name,
        in_specs=[hbm] * n, out_specs=[hbm] * n,
        out_shape=[jax.ShapeDtypeStruct(b.shape, b.dtype) for b in bufs],
        input_output_aliases={a: a for a in range(n)},
        scratch_shapes=[pltpu.SemaphoreType.DMA((n,)), pltpu.SemaphoreType.DMA((n,))],
    )(*bufs)


def _join_copy(buf, send_sems, recv_sems):
    x, y, c = _place()
    sems = dict(send_sem=send_sems.at[0], recv_sem=recv_sems.at[0], device_id=(x, y, 1 - c), device_id_type=MESH)
    return (pltpu.make_async_remote_copy(src_ref=buf.at[c], dst_ref=buf.at[c], **sems),
            pltpu.make_async_remote_copy(src_ref=buf.at[1 - c], dst_ref=buf.at[1 - c], **sems))


def _join_start(buf, name):
    def body(b_ref, send_sems, recv_sems, thru, token):
        _join_copy(b_ref, send_sems, recv_sems)[0].start()
        token[...] = jnp.zeros_like(token)

    return pl.pallas_call(
        body, name=name,
        in_specs=[HBM_SPEC],
        out_specs=[SEM_SPEC, SEM_SPEC, HBM_SPEC, pl.BlockSpec(memory_space=pltpu.VMEM)],
        out_shape=[pltpu.SemaphoreType.DMA((1,)), pltpu.SemaphoreType.DMA((1,)), pltpu.HBM(buf.shape, buf.dtype), TOKEN],
        input_output_aliases={0: 2},
        compiler_params=pltpu.CompilerParams(has_side_effects=EFFECT),
    )(pltpu.with_memory_space_constraint(buf, pltpu.HBM))


def _join_wait(state, after, name):
    send_sems, recv_sems, buf, _ = state

    def body(b_ref, send_ref, recv_ref, *rest):
        out, inc = _join_copy(b_ref, send_ref, recv_ref)
        out.wait_send()
        inc.wait_recv()

    return pl.pallas_call(
        body, name=name,
        in_specs=[HBM_SPEC, SEM_SPEC, SEM_SPEC] + [ANY_SPEC] * len(after),
        out_specs=HBM_SPEC,
        out_shape=pltpu.HBM(buf.shape, buf.dtype),
        input_output_aliases={0: 0},
        compiler_params=pltpu.CompilerParams(has_side_effects=EFFECT),
    )(buf, send_sems, recv_sems, *after)


N_DEV = 8


def _gather_small(block, reduce, name, deps=()):
    m_per, cols = block.shape
    deps = list(deps)

    def body(x_ref, *rest):
        out_ref, all_ref, send_sems, recv_sems, local_sem = rest[len(deps):]
        x, y, c = _place()
        me, sibling = (x, y, c), (x, y, 1 - c)
        chips = _other_chips(x, y)

        def rows(px, py, pc):
            return all_ref.at[pl.ds((4 * px + 2 * py + pc) * m_per, m_per), :]

        def copy(k, block_of, to, src=None):
            return pltpu.make_async_remote_copy(
                src_ref=rows(*block_of) if src is None else src, dst_ref=rows(*block_of),
                send_sem=send_sems.at[k], recv_sem=recv_sems.at[k], device_id=to, device_id_type=MESH)

        mine = pltpu.make_async_copy(x_ref, rows(*me), local_sem)
        mine.start()
        first = [copy(0, me, sibling, src=x_ref)]
        first += [copy(1 + j, me, (*chip, c), src=x_ref) for j, chip in enumerate(chips)]
        for cp in first:
            cp.start()
        passed = [copy(4 + j, (*chip, c), sibling) for j, chip in enumerate(chips)]
        for j, chip in enumerate(chips):
            copy(1 + j, (*chip, c), me).wait_recv()
            passed[j].start()
        copy(0, sibling, me).wait_recv()
        for j, chip in enumerate(chips):
            copy(4 + j, (*chip, 1 - c), me).wait_recv()
        for cp in first + passed:
            cp.wait_send()
        mine.wait()
        if reduce:
            acc = all_ref[pl.ds(0, m_per), :]
            for i in range(1, N_DEV):
                acc = acc + all_ref[pl.ds(i * m_per, m_per), :]
            out_ref[...] = acc
        else:
            out_ref[...] = all_ref[...]

    out_rows = m_per if reduce else N_DEV * m_per
    return pl.pallas_call(
        body, name=name,
        in_specs=[pl.BlockSpec(memory_space=pltpu.VMEM)] + [pl.BlockSpec(memory_space=pl.ANY)] * len(deps),
        out_specs=pl.BlockSpec(memory_space=pltpu.VMEM),
        out_shape=jax.ShapeDtypeStruct((out_rows, cols), F32),
        scratch_shapes=[pltpu.VMEM((N_DEV * m_per, cols), F32), pltpu.SemaphoreType.DMA((7,)),
                        pltpu.SemaphoreType.DMA((7,)), pltpu.SemaphoreType.DMA],
    )(block, *deps)


class _Comm:
    def __init__(self, w_in, w_kv, w_out, chip, core):
        self.w_in, self.chip, self.core = w_in, chip, core
        self.bufs = [_cast_own(w_in, chip, "cast_w_in_sent", half=core), _cast_own(w_kv, chip, "cast_w_mem_kv"),
                     _cast_own(w_out, chip, "cast_w_out")]


    def gather_started(self):
        (self.sems,), (b_in,), tok = _gather_start(self.bufs[:1], (NEAR,), "gather_start_in_near")
        self.b_in = _cast_own(self.w_in, self.chip, "cast_w_in_kept", half=1 - self.core, prev=b_in)
        return [tok]

    def w_in_own(self):
        return self.b_in.reshape(NCHIP * D, WB)

    def w_in_near(self, after):
        (b_in,) = _gather_wait([self.b_in], self.sems, NEAR, list(after) + self.bufs[1:], "gather_wait_in_near")
        (b_in,) = _forward_halves([b_in], NEAR, "forward_in_near")
        (self.sems,), (self.b_in,), tok = _gather_start([b_in], (FAR,), "gather_start_in_far")
        return self.b_in.reshape(NCHIP * D, WB), tok

    def w_in_all(self, after):
        (b_in,) = _gather_wait([self.b_in], self.sems, FAR, after, "gather_wait_in_far")
        (b_in,) = _forward_halves([b_in], FAR, "forward_in_far")
        (self.sems,), self.b_rest, tok = _gather_start(self.bufs[1:], (NEAR + FAR,), "gather_start_rest")
        return b_in.reshape(NCHIP * D, WB), tok

    def w_rest_landed(self, after):
        b_rest = _gather_wait(self.b_rest, self.sems, NEAR + FAR, after, "gather_wait_rest")
        self.sems, self.b_rest, tok = _forward_start(b_rest, NEAR + FAR, "forward_start_rest")
        return tok

    def w_rest(self, after):
        b_kv, b_out = _forward_wait(self.b_rest, self.sems, NEAR + FAR, after, "forward_wait_rest")
        return b_kv.reshape(D, 2 * XW), b_out.reshape(D, D)

    def sibling_start(self, send, tag):
        return _sibling_start(send, "sibling_start_" + tag)

    def reduce_start_summed(self, keep, sent, tag, after=None):
        got = _sibling_wait(sent, [keep] + ([] if after is None else [after]), "sibling_wait_" + tag)
        return _scatter_start([_pair_sum_rows(keep, got, "pair_sum_" + tag)], "scatter_start_" + tag)

    def sibling_wait(self, sent, after, tag):
        return _sibling_wait(sent, after, "sibling_wait_" + tag)

    def scatter_rows(self, q, tag, rows, before=None):
        lands = None if before is None else before[3]
        return _scatter_start([q], "scatter_start_" + tag, rows=rows, lands=lands) + (rows,)

    def reduce_finish_start(self, states, tag, after):
        qs, lands = states[-1][2], states[-1][3]
        for i, st in enumerate(states):
            qs, lands = _scatter_wait(qs, lands, st[0], st[1], after, f"scatter_wait_{tag}{i}", rows=st[5])
        return _join_start(_chip_sum(qs[0], lands[0], f"chip_sum_{tag}0"), "join_start_" + tag)

    def reduce_finish_wait(self, pending, tag, after):
        j = _join_wait(pending, after, "join_wait_" + tag)
        return j.reshape(2 * j.shape[1], j.shape[2])

    def reduce_finish(self, state, tag, after):
        send_sems, recv_sems, qs, lands, _ = state
        qs, lands = _scatter_wait(qs, lands, send_sems, recv_sems, after, "scatter_wait_" + tag)
        halves = [_chip_sum(q, l, f"chip_sum_{tag}{i}") for i, (q, l) in enumerate(zip(qs, lands))]
        return [j.reshape(2 * j.shape[1], j.shape[2]) for j in _join_halves(halves, "join_halves_" + tag)]


def kernel(x, mem, pre_norm, w_in, conv_w, mem_norm, w_mem_kv, w_out, post_norm, loss_target, m_pre_norm, m_w_in, m_conv_w, m_mem_norm, m_w_mem_kv, m_w_out, m_post_norm, v_pre_norm, v_w_in, v_conv_w, v_mem_norm, v_w_mem_kv, v_w_out, v_post_norm):
    chip = 2 * lax.axis_index("x") + lax.axis_index("y")

    comm = _Comm(w_in[0], w_mem_kv[0], w_out[0], chip, lax.axis_index("c"))
    cw_blk = jnp.zeros((8, 384), F32).at[:3].set(conv_w[0])
    cw_all = _gather_small(cw_blk, False, "gather_conv_w", deps=comm.bufs[1:]).reshape(NCHIP, 2, 8, 384)[:, 0, :3]
    conv_full = jnp.transpose(cw_all, (1, 0, 2)).reshape(3, CW)
    moments = {"w_out": (w_out[0], m_w_out[0], v_w_out[0]), "w_mem_kv": (w_mem_kv[0], m_w_mem_kv[0], v_w_mem_kv[0])}
    loss, grad_x, d_pre, d_mem, d_post, d_conv, r_in, _, (upd_out, upd_kv) = _local_step(
        x, mem, pre_norm, conv_full, mem_norm, post_norm, loss_target, chip, lax.axis_index("c"), comm, moments)

    pack = jnp.concatenate([d_pre, d_mem, d_post, jnp.pad(d_conv, ((0, 0), (0, D - CW))),
                            jnp.pad(loss, ((0, 0), (0, D - 128))), jnp.zeros((1, D), F32)], axis=0)
    tot = _gather_small(pack, True, "reduce_small")
    g_pre, g_mem, g_post = tot[0:1], tot[1:2], tot[2:3]
    g_conv = lax.dynamic_slice(tot[3:6, :CW], (0, chip * 384), (3, 384))
    loss_out = tot[6, 0]

    names = ("pre_norm", "w_in", "conv_w", "mem_norm", "w_mem_kv", "w_out", "post_norm")
    ws = (pre_norm, w_in[0], conv_w[0], mem_norm, w_mem_kv[0], w_out[0], post_norm)
    gs = [g_pre, None, g_conv, g_mem, None, None, g_post]
    ms = (m_pre_norm, m_w_in[0], m_conv_w[0], m_mem_norm, m_w_mem_kv[0], m_w_out[0], m_post_norm)
    vs = (v_pre_norm, v_w_in[0], v_conv_w[0], v_mem_norm, v_w_mem_kv[0], v_w_out[0], v_post_norm)
    upd = [None if g is None else _adamw(w, g, m, v, "adamw_" + nm) for nm, w, g, m, v in zip(names, ws, gs, ms, vs)]
    upd[4], upd[5] = upd_kv, upd_out
    g_in = comm.reduce_finish_wait(r_in, "b", after=[u[1] for u in upd if u is not None])
    upd[1] = _adamw(ws[1], g_in, ms[1], vs[1], "adamw_w_in")

    def shaped(arrs):
        return [a.reshape(w.shape) if w.ndim == a.ndim else a.reshape((1,) + a.shape)
                for a, w in zip(arrs, (pre_norm, w_in, conv_w, mem_norm, w_mem_kv, w_out, post_norm))]

    grads = shaped([u[0] for u in upd])
    deltas = shaped([u[1] for u in upd])
    new_m = shaped([u[2] for u in upd])
    new_v = shaped([u[3] for u in upd])
    return (loss_out, grad_x, *grads, *deltas, *new_m, *new_v)
```

```python
import jax
import jax.numpy as jnp
from jax import lax
from jax.experimental import pallas as pl
from jax.experimental.pallas import tpu as pltpu

F32 = jnp.float32
BF16 = jnp.bfloat16

D = 4096
S = 2048
NB = 2
T = NB * S
MLEN = 256
HD = 128
AW = 1536
CW = 1536
XW = 1024
XHD = 256
NXH = 4
NC = 14336
QA, KA, VA, ZA, UC, BC, CC, ZC, QX, ZX = 0, 1536, 3072, 4608, 6144, 7680, 9216, 10752, 12288, 13312
NCHIP = 4
WB = NC // NCHIP
DIL = (1, 4, 16)
HPG = 4
EPS = 1e-6
NEG = -1e30
ROPE_THETA = 10000.0
A_SCALE = HD ** -0.5
X_SCALE = XHD ** -0.5

ADAM_LR = 0.001
ADAM_B1 = 0.9
ADAM_B2 = 0.999
ADAM_EPS = 1e-08
ADAM_WD = 0.01
ADAM_STEP = 10

MESH = pl.DeviceIdType.MESH
MIB = 1024 * 1024


def _cp(sem, vmem_mib=48):
    return pltpu.CompilerParams(dimension_semantics=sem, vmem_limit_bytes=vmem_mib * MIB)


def _sigmoid(z):
    return 1.0 / (1.0 + jnp.exp(-z))


def _rope(x, cos, sin, half):
    return x * cos + pltpu.roll(x, half, 1) * sin


def _rope_t(g, cos, sin, half):
    return g * cos + pltpu.roll(g * sin, half, 1)


def _rms_fwd(x2, g, name, dep=None):
    rows = x2.shape[0]
    tr = 256
    deps = [] if dep is None else list(dep)

    def body(x_ref, g_ref, *rest):
        o_ref = rest[-1]
        x = x_ref[...]
        r = lax.rsqrt(jnp.mean(x * x, axis=-1, keepdims=True) + EPS)
        o_ref[...] = (x * r * g_ref[...]).astype(BF16)

    return pl.pallas_call(
        body, name=name, grid=(rows // tr,),
        in_specs=[pl.BlockSpec((tr, D), lambda i: (i, 0)), pl.BlockSpec((1, D), lambda i: (0, 0))]
        + [pl.BlockSpec(memory_space=pl.ANY)] * len(deps),
        out_specs=pl.BlockSpec((tr, D), lambda i: (i, 0)),
        out_shape=jax.ShapeDtypeStruct((rows, D), BF16),
        compiler_params=_cp(("parallel",)),
    )(x2, g, *deps)


def _norm_gain_grad(dn, x2, name):
    rows = x2.shape[0]
    tr = 256

    def body(dn_ref, x_ref, dg_ref):
        @pl.when(pl.program_id(0) == 0)
        def _():
            dg_ref[...] = jnp.zeros_like(dg_ref)
        x = x_ref[...]
        r = lax.rsqrt(jnp.mean(x * x, axis=-1, keepdims=True) + EPS)
        dg_ref[...] += jnp.sum(dn_ref[...] * (x * r), axis=0, keepdims=True)

    return pl.pallas_call(
        body, name=name, grid=(rows // tr,),
        in_specs=[pl.BlockSpec((tr, D), lambda i: (i, 0)), pl.BlockSpec((tr, D), lambda i: (i, 0))],
        out_specs=pl.BlockSpec((1, D), lambda i: (0, 0)),
        out_shape=jax.ShapeDtypeStruct((1, D), F32),
        compiler_params=_cp(("arbitrary",)),
    )(dn, x2)


def _pre_norm_bwd(dh, x2, g, dout, dep=None):
    tr = 256
    deps = [] if dep is None else [dep]

    def body(dh_ref, x_ref, g_ref, dout_ref, *rest):
        gx_ref, dg_ref = rest[len(deps):]

        @pl.when(pl.program_id(0) == 0)
        def _():
            dg_ref[...] = jnp.zeros_like(dg_ref)

        x = x_ref[...]
        dh_ = dh_ref[...].astype(F32)
        r = lax.rsqrt(jnp.mean(x * x, axis=-1, keepdims=True) + EPS)
        xhat = x * r
        dg_ref[...] += jnp.sum(dh_ * xhat, axis=0, keepdims=True)
        dxn = dh_ * g_ref[...]
        gx_ref[...] = dout_ref[...].astype(F32) + r * (dxn - xhat * jnp.mean(dxn * xhat, axis=-1, keepdims=True))

    row = pl.BlockSpec((tr, D), lambda i: (i, 0))
    vec = pl.BlockSpec((1, D), lambda i: (0, 0))
    return pl.pallas_call(
        body, name="pre_norm_bwd", grid=(T // tr,),
        in_specs=[row, row, vec, row] + [pl.BlockSpec(memory_space=pl.ANY)] * len(deps),
        out_specs=[row, vec],
        out_shape=[jax.ShapeDtypeStruct((T, D), F32), jax.ShapeDtypeStruct((1, D), F32)],
        compiler_params=_cp(("arbitrary",)),
    )(dh, x2, g, dout, *deps)


def _post_norm_loss(y, x2, tgt, g):
    tr = 256

    def body(y_ref, x_ref, t_ref, g_ref, dy_ref, dout_ref, dg_ref, loss_ref):
        @pl.when(pl.program_id(0) == 0)
        def _():
            dg_ref[...] = jnp.zeros_like(dg_ref)
            loss_ref[...] = jnp.zeros_like(loss_ref)
        yv = y_ref[...]
        gv = g_ref[...]
        r = lax.rsqrt(jnp.mean(yv * yv, axis=-1, keepdims=True) + EPS)
        yhat = yv * r
        err = x_ref[...] + yhat * gv - t_ref[...]
        loss_ref[...] += jnp.sum(jnp.sum(err * err, axis=1, keepdims=True), axis=0, keepdims=True) * (0.5 / D)
        dout = err * (1.0 / D)
        dout_ref[...] = dout.astype(BF16)
        dg_ref[...] += jnp.sum(dout * yhat, axis=0, keepdims=True)
        dyn = dout * gv
        dy_ref[...] = (r * (dyn - yhat * jnp.mean(dyn * yhat, axis=-1, keepdims=True))).astype(BF16)

    row = pl.BlockSpec((tr, D), lambda i: (i, 0))
    vec = pl.BlockSpec((1, D), lambda i: (0, 0))
    return pl.pallas_call(
        body, name="post_norm_loss", grid=(T // tr,),
        in_specs=[row, row, row, vec],
        out_specs=[row, row, vec, pl.BlockSpec((1, 128), lambda i: (0, 0))],
        out_shape=[jax.ShapeDtypeStruct((T, D), BF16), jax.ShapeDtypeStruct((T, D), BF16),
                   jax.ShapeDtypeStruct((1, D), F32), jax.ShapeDtypeStruct((1, 128), F32)],
        compiler_params=_cp(("arbitrary",)),
    )(y, x2, tgt, g)


NN = (((1,), (0,)), ((), ()))
NT = (((1,), (1,)), ((), ()))
TN = (((0,), (0,)), ((), ()))


def _as_index(v):
    return jnp.reshape(v, (1,)).astype(jnp.int32)


def _matmul(a, b, *, name, dims, grid, a_block, a_map, b_block, b_map, o_block, o_map, out_shape, out_dtype=F32,
            index=None, prev=None, deps=(), addend=None):
    extra = ([] if prev is None else [prev]) + [d for d in deps if d is not None]
    first = 0 if index is None else 1
    nk = grid[2]
    in_place = out_dtype == F32
    assert addend is None or not in_place
    n_add = 0 if addend is None else 1

    def body(*refs):
        a_ref, b_ref, o_ref = refs[first], refs[first + 1], refs[first + 2 + n_add + len(extra)]
        acc_ref = o_ref if in_place else refs[-1]

        @pl.when(pl.program_id(2) == 0)
        def _():
            acc_ref[...] = lax.dot_general(a_ref[...], b_ref[...], dims, preferred_element_type=F32)

        @pl.when(pl.program_id(2) > 0)
        def _():
            acc_ref[...] += lax.dot_general(a_ref[...], b_ref[...], dims, preferred_element_type=F32)

        if not in_place:
            @pl.when(pl.program_id(2) == nk - 1)
            def _():
                total = acc_ref[...] if addend is None else acc_ref[...] + refs[first + 2][...].astype(F32)
                o_ref[...] = total.astype(o_ref.dtype)

    adds = [] if addend is None else [addend]
    in_specs = ([pl.BlockSpec(a_block, a_map), pl.BlockSpec(b_block, b_map)] + [pl.BlockSpec(o_block, o_map)] * n_add
                + [pl.BlockSpec(memory_space=pl.ANY)] * len(extra))
    out_specs = pl.BlockSpec(o_block, o_map)
    scratch = [] if in_place else [pltpu.VMEM(o_block, F32)]
    kwargs = dict(name=name, out_shape=jax.ShapeDtypeStruct(out_shape, out_dtype),
                  input_output_aliases={} if prev is None else {first + 2 + n_add: 0},
                  compiler_params=_cp(("parallel", "parallel", "arbitrary"), vmem_mib=56))
    if index is None:
        return pl.pallas_call(body, grid=grid, in_specs=in_specs, out_specs=out_specs, scratch_shapes=scratch,
                              **kwargs)(a, b, *adds, *extra)
    grid_spec = pltpu.PrefetchScalarGridSpec(num_scalar_prefetch=1, grid=grid, in_specs=in_specs, out_specs=out_specs,
                                             scratch_shapes=scratch)
    return pl.pallas_call(body, grid_spec=grid_spec, **kwargs)(_as_index(index), a, b, *adds, *extra)


def _mm_nn(a, b, name, tm, tn, tk):
    m, kd = a.shape
    n = b.shape[1]
    return _matmul(a, b, name=name, dims=NN, grid=(m // tm, n // tn, kd // tk),
                   a_block=(tm, tk), a_map=lambda i, j, k: (i, k),
                   b_block=(tk, tn), b_map=lambda i, j, k: (k, j),
                   o_block=(tm, tn), o_map=lambda i, j, k: (i, j), out_shape=(m, n))


def _mm_nt(a, b, name, tm, tn, tk):
    m, kd = a.shape
    n = b.shape[0]
    return _matmul(a, b, name=name, dims=NT, grid=(m // tm, n // tn, kd // tk),
                   a_block=(tm, tk), a_map=lambda i, j, k: (i, k),
                   b_block=(tn, tk), b_map=lambda i, j, k: (j, k),
                   o_block=(tm, tn), o_map=lambda i, j, k: (i, j), out_shape=(m, n))


W_TN = 1792
W_NJ = WB // W_TN


def _proj_part(h, wg, chip, masks, name, prev=None, dep=None):
    tm, tk = 1024, 2048

    def blk(j, ix):
        m = masks[0]
        for t in range(1, len(masks)):
            m = jnp.where(j // W_NJ == t, masks[t], m)
        return jnp.bitwise_xor(ix[0], m)

    return _matmul(h, wg, name=name, dims=NN, grid=(T // tm, len(masks) * W_NJ, D // tk), index=chip, prev=prev, deps=(dep,),
                   a_block=(tm, tk), a_map=lambda i, j, k, ix: (i, k),
                   b_block=(tk, W_TN), b_map=lambda i, j, k, ix: (blk(j, ix) * (D // tk) + k, j % W_NJ),
                   o_block=(tm, W_TN), o_map=lambda i, j, k, ix: (i, blk(j, ix) * W_NJ + j % W_NJ), out_shape=(T, NC),
                   out_dtype=BF16)


def _adamw_math(w_ref, g_ref, m_ref, v_ref, go_ref, d_ref, nm_ref, nv_ref):
    gv = g_ref[...]
    go_ref[...] = gv
    nm = ADAM_B1 * m_ref[...] + (1.0 - ADAM_B1) * gv
    nv = ADAM_B2 * v_ref[...] + (1.0 - ADAM_B2) * (gv * gv)
    m_hat = nm / (1.0 - ADAM_B1 ** ADAM_STEP)
    v_hat = nv / (1.0 - ADAM_B2 ** ADAM_STEP)
    d_ref[...] = -ADAM_LR * (m_hat / (jnp.sqrt(v_hat) + ADAM_EPS) + ADAM_WD * w_ref[...])
    nm_ref[...] = nm
    nv_ref[...] = nv


def _dh(dproj, wg, dep=None, adam=()):
    tm, tn = 1024, 2048
    grid = (T // tm, D // tn, NC // W_TN)
    nsteps = grid[0] * grid[1] * grid[2]
    deps = [] if dep is None else [dep]
    na = len(adam)
    nk = grid[2]

    def body(*refs):
        a_ref, b_ref = refs[0], refs[1]
        o_ref = refs[2 + 4 * na + len(deps)]
        acc_ref = refs[-1]

        @pl.when(pl.program_id(2) == 0)
        def _():
            acc_ref[...] = lax.dot_general(a_ref[...], b_ref[...], NT, preferred_element_type=F32)

        @pl.when(pl.program_id(2) > 0)
        def _():
            acc_ref[...] += lax.dot_general(a_ref[...], b_ref[...], NT, preferred_element_type=F32)

        @pl.when(pl.program_id(2) == nk - 1)
        def _():
            o_ref[...] = acc_ref[...].astype(BF16)

        for s in range(na):
            ins = refs[2 + 4 * s:6 + 4 * s]
            outs = refs[3 + 4 * na + len(deps) + 4 * s:7 + 4 * na + len(deps) + 4 * s]
            _adamw_math(*ins, *outs)

    def rows_of(arr):
        r, c = arr.shape
        return pl.BlockSpec((r // nsteps, c), lambda i, j, k: ((i * grid[1] + j) * grid[2] + k, 0))

    adam_specs = [rows_of(a) for st in adam for a in st]
    res = pl.pallas_call(
        body, name="dh", grid=grid,
        in_specs=[pl.BlockSpec((tm, W_TN), lambda i, j, k: (i, k)),
                  pl.BlockSpec((tn, W_TN), lambda i, j, k: ((k // W_NJ) * (D // tn) + j, k % W_NJ))]
        + adam_specs + [pl.BlockSpec(memory_space=pl.ANY)] * len(deps),
        out_specs=[pl.BlockSpec((tm, tn), lambda i, j, k: (i, j))] + adam_specs,
        out_shape=[jax.ShapeDtypeStruct((T, D), BF16)] + [jax.ShapeDtypeStruct(a.shape, F32) for st in adam for a in st],
        scratch_shapes=[pltpu.VMEM((tm, tn), F32)],
        compiler_params=_cp(("arbitrary",) * 3, vmem_mib=56),
    )(dproj, wg, *[a for st in adam for a in st], *deps)
    return res[0], [tuple(res[1 + 4 * s:5 + 4 * s]) for s in range(na)]


def _grad_rows(a, b, half, out_dtype, name, dep=None):
    kd, n = b.shape
    tm, tn, tk = D // NCHIP // 2, min(n, 2048), min(kd, 2048)
    return _matmul(a, b, name=name, dims=TN, grid=(NCHIP, n // tn, kd // tk), index=half, deps=(dep,),
                   a_block=(tk, tm), a_map=lambda i, j, k, ix: (k, 2 * i + ix[0]),
                   b_block=(tk, tn), b_map=lambda i, j, k, ix: (k, j),
                   o_block=(tm, tn), o_map=lambda i, j, k, ix: (i, j),
                   out_shape=(NCHIP * tm, n), out_dtype=out_dtype)


GW_TM = 1024


def _grad_w_in(h, dproj, half, out_dtype, name, dep=None, tiles=(0, 2), addend=None, prev=None):
    tm, tk = GW_TM, 2048
    nh = D // 2 // tm
    t0, nt = tiles

    def o_map(i, j, k, ix):
        return ((j // W_NJ) * nh + t0 + i, j % W_NJ)

    return _matmul(h, dproj, name=name, dims=TN, grid=(nt, NC // W_TN, T // tk), index=half, deps=(dep,), prev=prev,
                   a_block=(tk, tm), a_map=lambda i, j, k, ix: (k, ix[0] * nh + t0 + i),
                   b_block=(tk, W_TN), b_map=lambda i, j, k, ix: (k, j),
                   o_block=(tm, W_TN), o_map=o_map, addend=addend,
                   out_shape=(NCHIP * D // 2, WB), out_dtype=out_dtype)


def _rope_tables(pos, half):
    inv = 1.0 / (ROPE_THETA ** (jnp.arange(half, dtype=F32) / half))
    ang = pos.astype(F32)[:, None] * inv[None, :]
    cos, sin = jnp.cos(ang), jnp.sin(ang)
    return jnp.concatenate([cos, cos], axis=1), jnp.concatenate([-sin, sin], axis=1)


def _band_mask(r0):
    qi = lax.broadcasted_iota(jnp.int32, (128, 256), 0)
    kk = lax.broadcasted_iota(jnp.int32, (128, 256), 1)
    return (kk >= qi) & (kk <= qi + 128) & (kk + r0 >= 128)


def _window(r0, nblk):
    if nblk == 1:
        qi = lax.broadcasted_iota(jnp.int32, (128, 128), 0)
        kk = lax.broadcasted_iota(jnp.int32, (128, 128), 1)
        return pl.ds(128, 128), kk <= qi
    return pl.ds(r0, 256), _band_mask(r0)


def _dil_rows(r, n, d):
    if d == 1:
        return pl.ds(pl.multiple_of(n * 128, 128), 128)
    return pl.ds(r + d * 128 * n, 128, stride=d)


def _widen(refs, wide):
    if not wide:
        return refs

    def copy(n, carry):
        rows = pl.ds(pl.multiple_of(n * 256, 256), 256)
        for src, dst in zip(refs, wide):
            dst[rows, :] = src[rows, :].astype(F32)
        return carry

    lax.fori_loop(0, S // 256, copy, 0)
    return wide


def _attn_fwd(proj, cosf, sinf, g):
    d = DIL[g]
    ln = S // d
    nblk = ln // 128
    proj_v = proj.reshape(NB, S, NC)

    def body(q_ref, k_ref, v_ref, cos_ref, sin_ref, o_ref, l_ref, k_s, v_s, q_s, *wide):
        k_s[:, pl.ds(0, 128), :] = jnp.zeros((d, 128, HD), BF16)
        v_s[:, pl.ds(0, 128), :] = jnp.zeros((d, 128, HD), BF16)
        q_ref, k_ref, v_ref = _widen((q_ref, k_ref, v_ref), wide)

        def prep(i, carry):
            r, n = i // nblk, i % nblk
            rows = _dil_rows(r, n, d)
            dst = pl.ds(pl.multiple_of(n * 128 + 128, 128), 128)
            k_s[r, dst, :] = _rope(k_ref[rows, :].astype(F32), cos_ref[rows, :], sin_ref[rows, :], HD // 2).astype(BF16)
            v_s[r, dst, :] = v_ref[rows, :].astype(BF16)
            return carry

        lax.fori_loop(0, d * nblk, prep, 0, unroll=4)

        def step(i, carry):
            r, n = i // nblk, i % nblk
            rows = _dil_rows(r, n, d)
            r0 = pl.multiple_of(n * 128, 128)
            qr = _rope(q_ref[rows, :].astype(F32), cos_ref[rows, :], sin_ref[rows, :], HD // 2).astype(BF16)
            q_s[r, pl.ds(r0, 128), :] = qr
            win, mask = _window(r0, nblk)
            kw = k_s[r, win, :]
            vw = v_s[r, win, :]
            sc = lax.dot_general(qr, kw, NT, preferred_element_type=F32) * A_SCALE
            sc = jnp.where(mask, sc, NEG)
            m = jnp.max(sc, axis=1, keepdims=True)
            p = jnp.exp(sc - m)
            l = jnp.sum(p, axis=1, keepdims=True)
            o_ref[rows, :] = jnp.dot(p.astype(BF16), vw, preferred_element_type=F32) / l
            l_ref[rows, :] = jnp.broadcast_to(m + jnp.log(l), (128, HD))
            return carry

        lax.fori_loop(0, d * nblk, step, 0, unroll=8 if d <= 4 else 4)

    def col(off):
        return lambda b, h: (b, 0, off // HD + HPG * g + h)

    blk = (None, S, HD)
    tab = pl.BlockSpec((S, HD), lambda b, h: (0, 0))
    out = pl.BlockSpec(blk, lambda b, h: (b, 0, h))
    kv = pl.BlockSpec((None, None, d, ln + 128, HD), lambda b, h: (b, h, 0, 0, 0))
    qq = pl.BlockSpec((None, None, d, ln, HD), lambda b, h: (b, h, 0, 0, 0))
    o, l, kr, vr, qr = pl.pallas_call(
        body, name=f"attn_fwd_d{d}", grid=(NB, HPG),
        in_specs=[pl.BlockSpec(blk, col(QA)), pl.BlockSpec(blk, col(KA)), pl.BlockSpec(blk, col(VA)), tab, tab],
        out_specs=[out, out, kv, kv, qq],
        out_shape=[jax.ShapeDtypeStruct((NB, S, HPG * HD), F32)] * 2
        + [jax.ShapeDtypeStruct((NB, HPG, d, ln + 128, HD), BF16)] * 2 + [jax.ShapeDtypeStruct((NB, HPG, d, ln, HD), BF16)],
        scratch_shapes=[pltpu.VMEM((S, HD), F32)] * (3 if d > 1 else 0),
        compiler_params=_cp(("parallel", "parallel")),
    )(proj_v, proj_v, proj_v, cosf, sinf)
    return o.reshape(T, HPG * HD), l.reshape(T, HPG * HD), (qr, kr, vr)


def _attn_bwd(qkv, cosf, sinf, da, lse, delta, dproj, g):
    d = DIL[g]
    ln = S // d
    nblk = ln // 128
    dproj_v = dproj.reshape(NB, S, NC)
    da_v = da.reshape(NB, S, AW)
    lse_v = lse.reshape(NB, S, HPG * HD)
    delta_v = delta.reshape(NB, S, HPG * HD)

    def body(q_s, k_s, v_s, cos_ref, sin_ref, da_ref, lse_ref, dl_ref, dp_in_ref, o_ref, stg, dk_s, dv_s):
        del dp_in_ref
        w = pl.program_id(2)

        def emit():
            def cast(n, carry):
                rows = pl.ds(pl.multiple_of(n * 256, 256), 256)
                o_ref[rows, :] = stg[rows, :].astype(BF16)
                return carry

            lax.fori_loop(0, S // 256, cast, 0)

        @pl.when(w == 0)
        def _():
            dk_s[...] = jnp.zeros_like(dk_s)
            dv_s[...] = jnp.zeros_like(dv_s)

            def step(i, carry):
                r, n = i // nblk, i % nblk
                rows = _dil_rows(r, n, d)
                r0 = pl.multiple_of(n * 128, 128)
                win, mask = _window(r0, nblk)
                cos, sin = cos_ref[rows, :], sin_ref[rows, :]
                qr = q_s[r, pl.ds(r0, 128), :]
                kw = k_s[r, win, :]
                vw = v_s[r, win, :]
                sc = lax.dot_general(qr, kw, NT, preferred_element_type=F32) * A_SCALE
                sc = jnp.where(mask, sc, NEG)
                p = jnp.exp(sc - lse_ref[rows, :][:, :1])
                da_b = da_ref[rows, :].astype(BF16)
                dp = lax.dot_general(da_b, vw, NT, preferred_element_type=F32)
                ds_b = (p * (dp - dl_ref[rows, :][:, :1]) * A_SCALE).astype(BF16)
                p_b = p.astype(BF16)
                dq = jnp.dot(ds_b, kw, preferred_element_type=F32)
                stg[rows, :] = _rope_t(dq, cos, sin, HD // 2)
                dk_s[r, win, :] += lax.dot_general(ds_b, qr, TN, preferred_element_type=F32)
                dv_s[r, win, :] += lax.dot_general(p_b, da_b, TN, preferred_element_type=F32)
                return carry

            lax.fori_loop(0, d * nblk, step, 0, unroll=8 if d <= 4 else 4)
            emit()

        @pl.when(w == 1)
        def _():
            def put(i, carry):
                r, n = i // nblk, i % nblk
                rows = _dil_rows(r, n, d)
                src = pl.ds(pl.multiple_of(n * 128 + 128, 128), 128)
                stg[rows, :] = _rope_t(dk_s[r, src, :], cos_ref[rows, :], sin_ref[rows, :], HD // 2)
                return carry

            lax.fori_loop(0, d * nblk, put, 0, unroll=4)
            emit()

        @pl.when(w == 2)
        def _():
            def put(i, carry):
                r, n = i // nblk, i % nblk
                src = pl.ds(pl.multiple_of(n * 128 + 128, 128), 128)
                stg[_dil_rows(r, n, d), :] = dv_s[r, src, :]
                return carry

            lax.fori_loop(0, d * nblk, put, 0, unroll=4)
            emit()

    def col(off):
        return lambda b, h, w: (ahead(b, h, w)[0], 0, off // HD + HPG * g + ahead(b, h, w)[1])

    def ahead(b, h, w):
        flat = jnp.minimum(b * HPG + h + jnp.where(w > 0, 1, 0), NB * HPG - 1)
        return flat // HPG, flat % HPG

    blk = (None, S, HD)
    tab = pl.BlockSpec((S, HD), lambda b, h, w: (0, 0))
    per_head = pl.BlockSpec(blk, lambda b, h, w: (ahead(b, h, w)[0], 0, ahead(b, h, w)[1]))
    kv = pl.BlockSpec((None, None, d, ln + 128, HD), lambda b, h, w: (*ahead(b, h, w), 0, 0, 0))
    qq = pl.BlockSpec((None, None, d, ln, HD), lambda b, h, w: (*ahead(b, h, w), 0, 0, 0))
    out = pl.pallas_call(
        body, name=f"attn_bwd_d{d}", grid=(NB, HPG, 3),
        in_specs=[qq, kv, kv, tab, tab,
                  pl.BlockSpec(blk, col(0)), per_head, per_head, pl.BlockSpec(memory_space=pl.ANY)],
        out_specs=pl.BlockSpec(blk, lambda b, h, w: (b, 0, (AW // HD) * w + HPG * g + h)),
        out_shape=jax.ShapeDtypeStruct(dproj_v.shape, BF16),
        input_output_aliases={8: 0},
        scratch_shapes=[pltpu.VMEM((S, HD), F32), pltpu.VMEM((d, ln + 128, HD), F32), pltpu.VMEM((d, ln + 128, HD), F32)],
        compiler_params=_cp(("arbitrary",) * 3),
    )(*qkv, cosf, sinf, da_v, lse_v, delta_v, dproj_v)
    return out.reshape(T, NC)


def _attn_mix(proj, os_, ls_, dep=None):
    tr = 256
    gw = HPG * HD
    deps = [] if dep is None else [dep]

    def body(o0, o1, o2, l0, l1, l2, z_ref, *rest):
        cat_ref = rest[-1]
        m = jnp.maximum(jnp.maximum(l0[...], l1[...]), l2[...])
        e = [jnp.exp(l[...] - m) for l in (l0, l1, l2)]
        inv = 1.0 / (e[0] + e[1] + e[2])
        for gi, o in enumerate((o0, o1, o2)):
            z = z_ref[:, gi * gw:(gi + 1) * gw].astype(F32)
            cat_ref[:, gi * gw:(gi + 1) * gw] = (o[...] * (e[gi] * inv) * (z * _sigmoid(z))).astype(BF16)

    grp = pl.BlockSpec((tr, gw), lambda i: (i, 0))
    return pl.pallas_call(
        body, name="attn_mix", grid=(T // tr,),
        in_specs=[grp] * 6 + [pl.BlockSpec((tr, AW), lambda i: (i, ZA // AW))] + [ANY_SPEC] * len(deps),
        out_specs=pl.BlockSpec((tr, AW), lambda i: (i, 0)),
        out_shape=jax.ShapeDtypeStruct((T, D), BF16),
        compiler_params=_cp(("parallel",)),
    )(*os_, *ls_, proj, *deps)


def _attn_mix_bwd(dcat, proj, os_, ls_, dep=None):
    tr = 256
    gw = HPG * HD
    deps = [] if dep is None else [dep]

    def body(dy_ref, o0, o1, o2, l0, l1, l2, z_ref, *rest):
        da_ref, lse_ref, dl_ref, dz_ref = rest[len(deps):]
        m = jnp.maximum(jnp.maximum(l0[...], l1[...]), l2[...])
        e = [jnp.exp(l[...] - m) for l in (l0, l1, l2)]
        den = e[0] + e[1] + e[2]
        inv = 1.0 / den
        lse_ref[...] = m + jnp.log(den)
        acc = jnp.zeros((tr, gw), F32)
        for gi, o in enumerate((o0, o1, o2)):
            cols = slice(gi * gw, (gi + 1) * gw)
            z = z_ref[:, cols].astype(F32)
            dy = dy_ref[:, cols]
            sg = _sigmoid(z)
            a = o[...] * (e[gi] * inv)
            da = dy * (z * sg)
            da_ref[:, cols] = da
            dz_ref[:, cols] = (dy * a * (sg * (1.0 + z * (1.0 - sg)))).astype(BF16)
            acc = acc + da * a
        for hh in range(HPG):
            cols = slice(hh * HD, (hh + 1) * HD)
            dl_ref[:, cols] = jnp.broadcast_to(jnp.sum(acc[:, cols], axis=1, keepdims=True), (tr, HD))

    grp = pl.BlockSpec((tr, gw), lambda i: (i, 0))
    return pl.pallas_call(
        body, name="attn_mix_bwd", grid=(T // tr,),
        in_specs=[pl.BlockSpec((tr, AW), lambda i: (i, 0))] + [grp] * 6 + [pl.BlockSpec((tr, AW), lambda i: (i, ZA // AW))]
        + [pl.BlockSpec(memory_space=pl.ANY)] * len(deps),
        out_specs=[pl.BlockSpec((tr, AW), lambda i: (i, 0)), grp, grp, pl.BlockSpec((tr, AW), lambda i: (i, ZA // AW))],
        out_shape=[jax.ShapeDtypeStruct((T, AW), F32), jax.ShapeDtypeStruct((T, gw), F32),
                   jax.ShapeDtypeStruct((T, gw), F32), jax.ShapeDtypeStruct((T, NC), BF16)],
        compiler_params=_cp(("parallel",)),
    )(dcat, *os_, *ls_, proj, *deps)


CT = 256


def _shift_down(x, n):
    rows = lax.broadcasted_iota(jnp.int32, x.shape, 0)
    return jnp.where(rows >= n, pltpu.roll(x, n, 0), 0.0)


def _shift_up(x, n):
    rows = lax.broadcasted_iota(jnp.int32, x.shape, 0)
    return jnp.where(rows < x.shape[0] - n, pltpu.roll(x, x.shape[0] - n, 0), 0.0)


def _conv_fwd(proj, conv_w, cat):
    proj_v = proj.reshape(NB, S, NC)
    cat_v = cat.reshape(NB, S, D)

    def body(u_ref, b_ref, c_ref, z_ref, w_ref, cat_in, o_ref):
        del cat_in
        cu = c_ref[...].astype(F32) * u_ref[...].astype(F32)
        cv = _shift_down(cu, 2) * w_ref[0:1, :] + _shift_down(cu, 1) * w_ref[1:2, :] + cu * w_ref[2:3, :]
        z = z_ref[...].astype(F32)
        o_ref[...] = (b_ref[...].astype(F32) * cv * (z * _sigmoid(z))).astype(BF16)

    def seg(off):
        return pl.BlockSpec((None, S, CT), lambda b, j: (b, 0, off // CT + j))

    out = pl.pallas_call(
        body, name="conv_fwd", grid=(NB, CW // CT),
        in_specs=[seg(UC), seg(BC), seg(CC), seg(ZC), pl.BlockSpec((3, CT), lambda b, j: (0, j)),
                  pl.BlockSpec(memory_space=pl.ANY)],
        out_specs=pl.BlockSpec((None, S, CT), lambda b, j: (b, 0, AW // CT + j)),
        out_shape=jax.ShapeDtypeStruct((NB, S, D), BF16),
        input_output_aliases={5: 0},
        compiler_params=_cp(("parallel", "parallel")),
    )(proj_v, proj_v, proj_v, proj_v, conv_w, cat_v)
    return out.reshape(T, D)


def _conv_bwd(dcat, proj, conv_w, dproj, dep=None):
    deps = [] if dep is None else [dep]
    proj_v = proj.reshape(NB, S, NC)
    dproj_v = dproj.reshape(NB, S, NC)
    dcat_v = dcat.reshape(NB, S, D)

    def body(dy_ref, u_ref, b_ref, c_ref, z_ref, w_ref, *rest):
        o_ref, dw_ref, st = rest[1 + len(deps):]
        b = pl.program_id(1)
        w = pl.program_id(2)

        @pl.when((b == 0) & (w == 0))
        def _():
            dw_ref[...] = jnp.zeros_like(dw_ref)

        @pl.when(w == 0)
        def _():
            u, c, z, bb = (r[...].astype(F32) for r in (u_ref, c_ref, z_ref, b_ref))
            dy = dy_ref[...]
            cu = c * u
            s1 = _shift_down(cu, 1)
            s2 = _shift_down(cu, 2)
            cv = s2 * w_ref[0:1, :] + s1 * w_ref[1:2, :] + cu * w_ref[2:3, :]
            sg = _sigmoid(z)
            sz = z * sg
            dcv = dy * bb * sz
            st[0] = dy * cv * sz
            st[2] = dy * bb * cv * (sg * (1.0 + z * (1.0 - sg)))
            dw_ref[0:1, :] += jnp.sum(dcv * s2, axis=0, keepdims=True)
            dw_ref[1:2, :] += jnp.sum(dcv * s1, axis=0, keepdims=True)
            dw_ref[2:3, :] += jnp.sum(dcv * cu, axis=0, keepdims=True)
            dcu = dcv * w_ref[2:3, :] + _shift_up(dcv, 1) * w_ref[1:2, :] + _shift_up(dcv, 2) * w_ref[0:1, :]
            st[1] = dcu * u
            o_ref[...] = (dcu * c).astype(BF16)

        for k in range(3):
            @pl.when(w == k + 1)
            def _(k=k):
                o_ref[...] = st[k].astype(BF16)

    def ahead(j, b, w):
        flat = jnp.minimum(j * NB + b + jnp.where(w > 0, 1, 0), (CW // CT) * NB - 1)
        return flat // NB, flat % NB

    def seg(off):
        return pl.BlockSpec((None, S, CT), lambda j, b, w: (ahead(j, b, w)[1], 0, off // CT + ahead(j, b, w)[0]))

    out, dw = pl.pallas_call(
        body, name="conv_bwd", grid=(CW // CT, NB, 4),
        in_specs=[seg(AW), seg(UC), seg(BC), seg(CC), seg(ZC),
                  pl.BlockSpec((3, CT), lambda j, b, w: (0, ahead(j, b, w)[0])),
                  pl.BlockSpec(memory_space=pl.ANY)] + [pl.BlockSpec(memory_space=pl.ANY)] * len(deps),
        out_specs=[pl.BlockSpec((None, S, CT), lambda j, b, w: (b, 0, (UC + w * CW) // CT + j)),
                   pl.BlockSpec((3, CT), lambda j, b, w: (0, j))],
        out_shape=[jax.ShapeDtypeStruct((NB, S, NC), BF16), jax.ShapeDtypeStruct((3, CW), F32)],
        input_output_aliases={6: 0},
        scratch_shapes=[pltpu.VMEM((3, S, CT), F32)],
        compiler_params=_cp(("arbitrary",) * 3),
    )(dcat_v, proj_v, proj_v, proj_v, proj_v, conv_w, dproj_v, *deps)
    return out.reshape(T, NC), dw


XT = 1024


def _cross_fwd(proj, mkv, cosq, sinq, cosm, sinm, cat):
    proj_v = proj.reshape(NB, S, NC)
    mkv_v = mkv.reshape(NB, MLEN, 2 * XW)
    cat_v = cat.reshape(NB, S, D)

    def body(q_ref, z_ref, mk_ref, mv_ref, cq, sq, cm, sm, cat_in, o_ref):
        del cat_in
        mkr = _rope(mk_ref[...], cm[...], sm[...], XHD // 2).astype(BF16)
        qr = _rope(q_ref[...].astype(F32), cq[...], sq[...], XHD // 2).astype(BF16)
        sc = lax.dot_general(qr, mkr, NT, preferred_element_type=F32) * X_SCALE
        p = jnp.exp(sc - jnp.max(sc, axis=1, keepdims=True))
        p = p / jnp.sum(p, axis=1, keepdims=True)
        ox = jnp.dot(p.astype(BF16), mv_ref[...].astype(BF16), preferred_element_type=F32)
        z = z_ref[...].astype(F32)
        o_ref[...] = (ox * (z * _sigmoid(z))).astype(BF16)

    def seg(off):
        return pl.BlockSpec((None, XT, XHD), lambda b, h, t: (b, t, off // XHD + h))

    qtab = pl.BlockSpec((XT, XHD), lambda b, h, t: (t, 0))
    mtab = pl.BlockSpec((MLEN, XHD), lambda b, h, t: (0, 0))
    out = pl.pallas_call(
        body, name="cross_fwd", grid=(NB, NXH, S // XT),
        in_specs=[seg(QX), seg(ZX),
                  pl.BlockSpec((None, MLEN, XHD), lambda b, h, t: (b, 0, h)),
                  pl.BlockSpec((None, MLEN, XHD), lambda b, h, t: (b, 0, NXH + h)),
                  qtab, qtab, mtab, mtab, pl.BlockSpec(memory_space=pl.ANY)],
        out_specs=pl.BlockSpec((None, XT, XHD), lambda b, h, t: (b, t, (AW + CW) // XHD + h)),
        out_shape=jax.ShapeDtypeStruct((NB, S, D), BF16),
        input_output_aliases={8: 0},
        compiler_params=_cp(("parallel",) * 3),
    )(proj_v, proj_v, mkv_v, mkv_v, cosq, sinq, cosm, sinm, cat_v)
    return out.reshape(T, D)


def _cross_bwd(dcat, proj, mkv, cosq, sinq, cosm, sinm, dproj):
    proj_v = proj.reshape(NB, S, NC)
    dproj_v = dproj.reshape(NB, S, NC)
    dcat_v = dcat.reshape(NB, S, D)
    mkv_v = mkv.reshape(NB, MLEN, 2 * XW)
    nt = S // XT

    def body(dy_ref, q_ref, z_ref, mk_ref, mv_ref, cq, sq, cm, sm, dp_in, o_ref, dmk_ref, dmv_ref, dz_s):
        del dp_in
        t = pl.program_id(2)
        w = pl.program_id(3)

        @pl.when((t == 0) & (w == 0))
        def _():
            dmk_ref[...] = jnp.zeros_like(dmk_ref)
            dmv_ref[...] = jnp.zeros_like(dmv_ref)

        @pl.when(w == 0)
        def _():
            mkr = _rope(mk_ref[...], cm[...], sm[...], XHD // 2).astype(BF16)
            mv_b = mv_ref[...].astype(BF16)
            qr = _rope(q_ref[...].astype(F32), cq[...], sq[...], XHD // 2).astype(BF16)
            sc = lax.dot_general(qr, mkr, NT, preferred_element_type=F32) * X_SCALE
            p = jnp.exp(sc - jnp.max(sc, axis=1, keepdims=True))
            p = p / jnp.sum(p, axis=1, keepdims=True)
            p_b = p.astype(BF16)
            ox = jnp.dot(p_b, mv_b, preferred_element_type=F32)
            z = z_ref[...].astype(F32)
            dy = dy_ref[...]
            sg = _sigmoid(z)
            dz_s[...] = dy * ox * (sg * (1.0 + z * (1.0 - sg)))
            dox_b = (dy * (z * sg)).astype(BF16)
            dp = lax.dot_general(dox_b, mv_b, NT, preferred_element_type=F32)
            ds_b = (p * (dp - jnp.sum(dp * p, axis=1, keepdims=True)) * X_SCALE).astype(BF16)
            dq = jnp.dot(ds_b, mkr, preferred_element_type=F32)
            o_ref[...] = _rope_t(dq, cq[...], sq[...], XHD // 2).astype(BF16)
            dmk_ref[...] += lax.dot_general(ds_b, qr, TN, preferred_element_type=F32)
            dmv_ref[...] += lax.dot_general(p_b, dox_b, TN, preferred_element_type=F32)

        @pl.when(w == 1)
        def _():
            o_ref[...] = dz_s[...].astype(BF16)

        @pl.when((t == nt - 1) & (w == 1))
        def _():
            dmk_ref[...] = _rope_t(dmk_ref[...], cm[...], sm[...], XHD // 2)

    def ahead(b, h, t, w):
        flat = jnp.minimum((b * NXH + h) * nt + t + jnp.where(w > 0, 1, 0), NB * NXH * nt - 1)
        return flat // (NXH * nt), (flat // nt) % NXH, flat % nt

    def seg(off):
        def index(b, h, t, w):
            b2, h2, t2 = ahead(b, h, t, w)
            return b2, t2, off // XHD + h2
        return pl.BlockSpec((None, XT, XHD), index)

    def mem(off):
        def index(b, h, t, w):
            b2, h2, _ = ahead(b, h, t, w)
            return b2, 0, off + h2
        return pl.BlockSpec((None, MLEN, XHD), index)

    qtab = pl.BlockSpec((XT, XHD), lambda b, h, t, w: (ahead(b, h, t, w)[2], 0))
    mtab = pl.BlockSpec((MLEN, XHD), lambda b, h, t, w: (0, 0))
    macc = pl.BlockSpec((None, MLEN, XHD), lambda b, h, t, w: (b, 0, h))
    out, dmk, dmv = pl.pallas_call(
        body, name="cross_bwd", grid=(NB, NXH, nt, 2),
        in_specs=[seg(AW + CW), seg(QX), seg(ZX), mem(0), mem(NXH),
                  qtab, qtab, mtab, mtab, pl.BlockSpec(memory_space=pl.ANY)],
        out_specs=[pl.BlockSpec((None, XT, XHD), lambda b, h, t, w: (b, t, (QX + w * XW) // XHD + h)), macc, macc],
        out_shape=[jax.ShapeDtypeStruct((NB, S, NC), BF16), jax.ShapeDtypeStruct((NB, MLEN, XW), F32),
                   jax.ShapeDtypeStruct((NB, MLEN, XW), F32)],
        input_output_aliases={9: 0},
        scratch_shapes=[pltpu.VMEM((XT, XHD), F32)],
        compiler_params=_cp(("arbitrary",) * 4),
    )(dcat_v, proj_v, proj_v, mkv_v, mkv_v, cosq, sinq, cosm, sinm, dproj_v)
    return out.reshape(T, NC), dmk, dmv


def _local_step(x, mem, pre_norm, conv_w, mem_norm, post_norm, tgt, chip, core, comm, moments=None):
    x2 = x.reshape(T, D)
    mem2 = mem.reshape(NB * MLEN, D)
    tgt2 = tgt.reshape(T, D)
    cosa, sina = _rope_tables(jnp.arange(S), HD // 2)
    cosq, sinq = _rope_tables(jnp.arange(S) + MLEN, XHD // 2)
    cosm, sinm = _rope_tables(jnp.arange(MLEN), XHD // 2)

    h = _rms_fwd(x2, pre_norm, "pre_norm_fwd", dep=comm.gather_started())
    memn = _rms_fwd(mem2, mem_norm, "mem_norm_fwd")
    proj = _proj_part(h, comm.w_in_own(), chip, (0,), "proj_own")
    wg_in, tok = comm.w_in_near(after=[proj, memn, conv_w])
    proj = _proj_part(h, wg_in, chip, (REL_XOR[0], REL_XOR[1]), "proj_near", prev=proj, dep=tok)
    wg_in, tok = comm.w_in_all(after=proj)
    proj = _proj_part(h, wg_in, chip, (REL_XOR[2],), "proj_far", prev=proj, dep=tok)
    fw = [_attn_fwd(proj, cosa, sina, g) for g in range(3)]
    os_ = [f[0] for f in fw]
    ls_ = [f[1] for f in fw]
    tok = comm.w_rest_landed(after=ls_)
    cat = _attn_mix(proj, os_, ls_, dep=tok)
    cat = _conv_fwd(proj, conv_w, cat)
    wg_kv, wg_out = comm.w_rest(after=[cat])
    mkv = _mm_nn(memn, wg_kv, "mkv", NB * MLEN, 1024, 512)
    cat = _cross_fwd(proj, mkv, cosq, sinq, cosm, sinm, cat)
    y = _mm_nn(cat, wg_out, "out_proj", 1024, 2048, 2048)
    dy, dout, d_post, loss = _post_norm_loss(y, x2, tgt2, post_norm)

    dcat = _mm_nt(dy, wg_out, "dcat", 1024, 2048, 2048)
    g_send = _grad_rows(cat, dy,1 - core, BF16, "grad_w_out_send")
    sent = comm.sibling_start(g_send.reshape(NCHIP, D // NCHIP // 2, D), "a")
    g_keep = _grad_rows(cat, dy,core, BF16, "grad_w_out_keep", dep=sent[-1])
    red_a = comm.reduce_start_summed(g_keep.reshape(NCHIP, D // NCHIP // 2, D), sent, "a")
    da, lse, delta, dproj = _attn_mix_bwd(dcat, proj, os_, ls_, dep=red_a[-1])
    for g in range(3):
        dproj = _attn_bwd(fw[g][2], cosa, sina, da, lse, delta, dproj, g)
    dproj, dmk, dmv = _cross_bwd(dcat, proj, mkv, cosq, sinq, cosm, sinm, dproj)
    dmkv = jnp.concatenate([dmk, dmv], axis=-1).reshape(NB * MLEN, 2 * XW)
    dmkv_b = dmkv.astype(BF16)
    gk_send = _grad_rows(memn, dmkv_b, 1 - core, BF16, "grad_w_mem_kv_send")
    sent = comm.sibling_start(gk_send.reshape(NCHIP, D // NCHIP // 2, 2 * XW), "k")
    gk_keep = _grad_rows(memn, dmkv_b, core, BF16, "grad_w_mem_kv_keep", dep=sent[-1])
    dmemn = _mm_nt(dmkv_b, wg_kv, "dmemn", NB * MLEN, 1024, 512)
    d_mem = _norm_gain_grad(dmemn, mem2, "mem_norm_bwd")
    red_k = comm.reduce_start_summed(gk_keep.reshape(NCHIP, D // NCHIP // 2, 2 * XW), sent, "k", after=d_mem)
    dproj, d_conv = _conv_bwd(dcat, proj, conv_w, dproj, dep=red_k[-1])
    (r_out,) = comm.reduce_finish(red_a, "a", after=d_conv)
    (r_kv,) = comm.reduce_finish(red_k, "k", after=d_conv)

    g_send = _grad_w_in(h, dproj, 1 - core, BF16, "grad_w_in_send", dep=r_kv)
    sent = comm.sibling_start(g_send.reshape(NCHIP, D // 2, WB), "b")
    q = _grad_w_in(h, dproj, core, BF16, "grad_w_in_keep_0", dep=sent[-1], tiles=(0, 1))
    got = comm.sibling_wait(sent, [q], "b")
    q = _pair_sum_rows(q.reshape(NCHIP, D // 2, WB), got, "pair_sum_b", rows=(0, GW_TM))
    red_b0 = comm.scatter_rows(q, "b0", (0, GW_TM))
    q = _grad_w_in(h, dproj, core, BF16, "grad_w_in_keep_1", dep=red_b0[4], tiles=(1, 1),
                   addend=got.reshape(NCHIP * D // 2, WB), prev=red_b0[2][0].reshape(NCHIP * D // 2, WB))
    red_b1 = comm.scatter_rows(q.reshape(NCHIP, D // 2, WB), "b1", (GW_TM, GW_TM), before=red_b0)
    sets = [] if moments is None else [(moments["w_out"][0], r_out, *moments["w_out"][1:]),
                                      (moments["w_mem_kv"][0], r_kv, *moments["w_mem_kv"][1:])]
    dh, updates = _dh(dproj, wg_in, dep=red_b1[4], adam=sets)
    r_in = comm.reduce_finish_start([red_b0, red_b1], "b", after=dh)
    grad_x, d_pre = _pre_norm_bwd(dh, x2, pre_norm, dout, dep=r_in[-1])
    return loss, grad_x.reshape(NB, S, D), d_pre, d_mem, d_post, d_conv, r_in, (r_kv, r_out), updates


def _adamw(w, g, m, v, name):
    rows, cols = w.shape
    tr = rows if rows <= 512 else 512
    tc = cols if cols <= 1024 else 1024
    if cols % tc:
        tc = 896

    blk = pl.BlockSpec((tr, tc), lambda i, j: (i, j))
    sds = jax.ShapeDtypeStruct((rows, cols), F32)
    def body(*refs):
        _adamw_math(*refs)

    return pl.pallas_call(
        body, name=name, grid=(rows // tr, cols // tc),
        in_specs=[blk] * 4, out_specs=[blk] * 4, out_shape=[sds] * 4,
        compiler_params=_cp(("parallel", "parallel")),
    )(w, g, m, v)


def _place():
    return lax.axis_index("x"), lax.axis_index("y"), lax.axis_index("c")


def _other_chips(x, y):
    return [(1 - x, y), (x, 1 - y), (1 - x, 1 - y)]


def _tile_cols(cols):
    return cols if cols <= 1024 else (1024 if cols % 1024 == 0 else 896)


def _cast_own(w, chip, name, half=None, prev=None):
    rows, cols = w.shape
    tr, tc = 512, _tile_cols(cols)
    nrt = rows // tr if half is None else rows // tr // 2
    index = jnp.stack([chip, 0 if half is None else half]).astype(jnp.int32)
    extra = [] if prev is None else [prev]

    def body(ix_ref, w_ref, *rest):
        rest[-1][...] = w_ref[...].astype(BF16)

    grid_spec = pltpu.PrefetchScalarGridSpec(
        num_scalar_prefetch=1, grid=(nrt, cols // tc),
        in_specs=[pl.BlockSpec((tr, tc), lambda i, j, ix: (ix[1] * nrt + i, j))] + [ANY_SPEC] * len(extra),
        out_specs=pl.BlockSpec((None, tr, tc), lambda i, j, ix: (ix[0], ix[1] * nrt + i, j)))
    return pl.pallas_call(
        body, name=name, grid_spec=grid_spec,
        out_shape=jax.ShapeDtypeStruct((NCHIP, rows, cols), BF16),
        input_output_aliases={} if prev is None else {2: 0},
        compiler_params=_cp(("parallel", "parallel")),
    )(index, w, *extra)


HBM_SPEC = pl.BlockSpec(memory_space=pltpu.HBM)
SEM_SPEC = pl.BlockSpec(memory_space=pltpu.SEMAPHORE)
ANY_SPEC = pl.BlockSpec(memory_space=pl.ANY)
EFFECT = pltpu.SideEffectType.DATAFLOW_SIDE_EFFECTING
TOKEN = jax.ShapeDtypeStruct((8, 128), F32)


def _half(ref, chip, hc):
    hr = ref.shape[1] // 2
    return ref.at[chip, pl.ds(hc * hr, hr), :]


NEAR = (0, 1)
FAR = (2,)
REL_XOR = (2, 1, 3)


def _gather_copies(refs, send_sems, recv_sems, rels):
    x, y, c = _place()
    chips = _other_chips(x, y)
    out, inc = [], []
    for a, ref in enumerate(refs):
        for p, j in enumerate(rels):
            px, py = chips[j]
            mine = _half(ref, 2 * x + y, c)
            theirs = _half(ref, 2 * px + py, c)
            sems = dict(send_sem=send_sems.at[len(rels) * a + p], recv_sem=recv_sems.at[len(rels) * a + p],
                        device_id=(px, py, c), device_id_type=MESH)
            out.append(pltpu.make_async_remote_copy(src_ref=mine, dst_ref=mine, **sems))
            inc.append(pltpu.make_async_remote_copy(src_ref=theirs, dst_ref=theirs, **sems))
    return out, inc


def _gather_start(bufs, groups, name):
    n = len(bufs)
    ng = len(groups)

    def body(*refs):
        ins = refs[:n]
        token = refs[-1]
        for gi, rels in enumerate(groups):
            out, _ = _gather_copies(ins, refs[n + 2 * gi], refs[n + 2 * gi + 1], rels)
            for cp in out:
                cp.start()
        token[...] = jnp.zeros_like(token)

    sems = []
    for rels in groups:
        sems += [pltpu.SemaphoreType.DMA((len(rels) * n,))] * 2
    res = pl.pallas_call(
        body, name=name,
        in_specs=[HBM_SPEC] * n,
        out_specs=[SEM_SPEC] * (2 * ng) + [HBM_SPEC] * n + [pl.BlockSpec(memory_space=pltpu.VMEM)],
        out_shape=sems + [pltpu.HBM(b.shape, b.dtype) for b in bufs] + [TOKEN],
        input_output_aliases={a: 2 * ng + a for a in range(n)},
        compiler_params=pltpu.CompilerParams(has_side_effects=EFFECT),
    )(*[pltpu.with_memory_space_constraint(b, pltpu.HBM) for b in bufs])
    return [(res[2 * gi], res[2 * gi + 1]) for gi in range(ng)], list(res[2 * ng:2 * ng + n]), res[-1]


def _gather_wait(bufs, sems, rels, after, name):
    n = len(bufs)
    send_sems, recv_sems = sems
    after = list(after) if isinstance(after, (list, tuple)) else [after]

    def body(*refs):
        ins = refs[:n]
        out, inc = _gather_copies(ins, refs[n], refs[n + 1], rels)
        for cp in out:
            cp.wait_send()
        for cp in inc:
            cp.wait_recv()

    return pl.pallas_call(
        body, name=name,
        in_specs=[HBM_SPEC] * n + [SEM_SPEC, SEM_SPEC] + [ANY_SPEC] * len(after),
        out_specs=[HBM_SPEC] * n,
        out_shape=[pltpu.HBM(b.shape, b.dtype) for b in bufs],
        input_output_aliases={a: a for a in range(n)},
        compiler_params=pltpu.CompilerParams(has_side_effects=EFFECT),
    )(*bufs, send_sems, recv_sems, *after)


def _forward_halves(bufs, rels, name):
    n = len(bufs)

    def body(*refs):
        cps, waits = _forward_copies(refs[n:2 * n], rels, refs[2 * n], refs[2 * n + 1])
        for cp in cps:
            cp.start()
        for cp in waits:
            cp.wait_recv()
        for cp in cps:
            cp.wait_send()

    return pl.pallas_call(
        body, name=name,
        in_specs=[ANY_SPEC] * n, out_specs=[ANY_SPEC] * n,
        out_shape=[jax.ShapeDtypeStruct(s.shape, s.dtype) for s in bufs],
        input_output_aliases={a: a for a in range(n)},
        scratch_shapes=[pltpu.SemaphoreType.DMA((len(rels) * n,)), pltpu.SemaphoreType.DMA((len(rels) * n,))],
    )(*bufs)


def _forward_copies(refs, rels, send_sems, recv_sems):
    x, y, c = _place()
    chips = _other_chips(x, y)
    cps, waits = [], []
    for a, ref in enumerate(refs):
        for p, j in enumerate(rels):
            px, py = chips[j]
            sems = dict(send_sem=send_sems.at[len(rels) * a + p], recv_sem=recv_sems.at[len(rels) * a + p],
                        device_id=(x, y, 1 - c), device_id_type=MESH)
            got = _half(ref, 2 * px + py, c)
            want = _half(ref, 2 * px + py, 1 - c)
            cps.append(pltpu.make_async_remote_copy(src_ref=got, dst_ref=got, **sems))
            waits.append(pltpu.make_async_remote_copy(src_ref=want, dst_ref=want, **sems))
    return cps, waits


def _forward_start(bufs, rels, name):
    n = len(bufs)

    def body(*refs):
        cps, _ = _forward_copies(refs[:n], rels, refs[n], refs[n + 1])
        for cp in cps:
            cp.start()
        refs[-1][...] = jnp.zeros_like(refs[-1])

    res = pl.pallas_call(
        body, name=name,
        in_specs=[HBM_SPEC] * n,
        out_specs=[SEM_SPEC, SEM_SPEC] + [HBM_SPEC] * n + [pl.BlockSpec(memory_space=pltpu.VMEM)],
        out_shape=[pltpu.SemaphoreType.DMA((len(rels) * n,))] * 2 + [pltpu.HBM(b.shape, b.dtype) for b in bufs] + [TOKEN],
        input_output_aliases={a: 2 + a for a in range(n)},
        compiler_params=pltpu.CompilerParams(has_side_effects=EFFECT),
    )(*[pltpu.with_memory_space_constraint(b, pltpu.HBM) for b in bufs])
    return (res[0], res[1]), list(res[2:2 + n]), res[-1]


def _forward_wait(bufs, sems, rels, after, name):
    n = len(bufs)

    def body(*refs):
        cps, waits = _forward_copies(refs[:n], rels, refs[n], refs[n + 1])
        for cp in cps:
            cp.wait_send()
        for cp in waits:
            cp.wait_recv()

    return pl.pallas_call(
        body, name=name,
        in_specs=[HBM_SPEC] * n + [SEM_SPEC, SEM_SPEC] + [ANY_SPEC] * len(after),
        out_specs=[HBM_SPEC] * n,
        out_shape=[pltpu.HBM(b.shape, b.dtype) for b in bufs],
        input_output_aliases={a: a for a in range(n)},
        compiler_params=pltpu.CompilerParams(has_side_effects=EFFECT),
    )(*bufs, sems[0], sems[1], *after)


def _sibling_copy(src, land, send_sems, recv_sems):
    x, y, c = _place()
    return pltpu.make_async_remote_copy(src_ref=src, dst_ref=land, send_sem=send_sems.at[0], recv_sem=recv_sems.at[0],
                                        device_id=(x, y, 1 - c), device_id_type=MESH)


def _sibling_start(part, name):
    def body(src, land, send_sems, recv_sems, src_thru, land_thru, token):
        _sibling_copy(src, land, send_sems, recv_sems).start()
        token[...] = jnp.zeros_like(token)

    land = lax.empty(part.shape, part.dtype)
    return pl.pallas_call(
        body, name=name,
        in_specs=[HBM_SPEC] * 2,
        out_specs=[SEM_SPEC, SEM_SPEC, HBM_SPEC, HBM_SPEC, pl.BlockSpec(memory_space=pltpu.VMEM)],
        out_shape=[pltpu.SemaphoreType.DMA((1,)), pltpu.SemaphoreType.DMA((1,)), pltpu.HBM(part.shape, part.dtype),
                   pltpu.HBM(part.shape, part.dtype), TOKEN],
        input_output_aliases={0: 2, 1: 3},
        compiler_params=pltpu.CompilerParams(has_side_effects=EFFECT),
    )(pltpu.with_memory_space_constraint(part, pltpu.HBM), pltpu.with_memory_space_constraint(land, pltpu.HBM))


def _sibling_wait(state, after, name):
    send_sems, recv_sems, part, land, _ = state

    def body(src, land_ref, send_ref, recv_ref, *rest):
        cp = _sibling_copy(src, land_ref, send_ref, recv_ref)
        cp.wait_send()
        cp.wait_recv()

    return pl.pallas_call(
        body, name=name,
        in_specs=[HBM_SPEC, HBM_SPEC, SEM_SPEC, SEM_SPEC] + [ANY_SPEC] * len(after),
        out_specs=[HBM_SPEC, HBM_SPEC],
        out_shape=[pltpu.HBM(part.shape, part.dtype), pltpu.HBM(land.shape, land.dtype)],
        input_output_aliases={0: 0, 1: 1},
        compiler_params=pltpu.CompilerParams(has_side_effects=EFFECT),
    )(part, land, send_sems, recv_sems, *after)[1]


def _pair_sum_rows(keep, got, name, rows=None):
    nblk, nrows, cols = keep.shape
    tr, tc = 512, _tile_cols(cols)
    r0, nr = (0, nrows) if rows is None else rows

    def body(k_ref, g_ref, o_ref):
        o_ref[...] = (k_ref[...].astype(F32) + g_ref[...].astype(F32)).astype(BF16)

    blk = pl.BlockSpec((None, tr, tc), lambda b, i, j: (b, r0 // tr + i, j))
    return pl.pallas_call(
        body, name=name, grid=(nblk, nr // tr, cols // tc),
        in_specs=[blk, blk], out_specs=blk,
        out_shape=jax.ShapeDtypeStruct(keep.shape, BF16),
        input_output_aliases={} if rows is None else {0: 0},
        compiler_params=_cp(("parallel",) * 3),
    )(keep, got)


def _scatter_start(qs, name, rows=None, lands=None):
    n = len(qs)

    def body(*refs):
        ins, lands = refs[:n], refs[n:2 * n]
        token = refs[-1]
        for cp in _scatter_copies(ins, lands, refs[2 * n], refs[2 * n + 1], rows):
            cp.start()
        token[...] = jnp.zeros_like(token)

    if lands is None:
        lands = [lax.empty((3,) + q.shape[1:], q.dtype) for q in qs]
    res = pl.pallas_call(
        body, name=name,
        in_specs=[HBM_SPEC] * (2 * n),
        out_specs=[SEM_SPEC, SEM_SPEC] + [HBM_SPEC] * (2 * n) + [pl.BlockSpec(memory_space=pltpu.VMEM)],
        out_shape=[pltpu.SemaphoreType.DMA((3 * n,)), pltpu.SemaphoreType.DMA((3 * n,))]
        + [pltpu.HBM(b.shape, b.dtype) for b in qs + lands] + [TOKEN],
        input_output_aliases={a: 2 + a for a in range(2 * n)},
        compiler_params=pltpu.CompilerParams(has_side_effects=EFFECT),
    )(*[pltpu.with_memory_space_constraint(b, pltpu.HBM) for b in qs + lands])
    return res[0], res[1], list(res[2:2 + n]), list(res[2 + n:2 + 2 * n]), res[-1]


def _scatter_copies(ins, lands, send_sems, recv_sems, rows=None):
    x, y, c = _place()
    sel = slice(None) if rows is None else pl.ds(rows[0], rows[1])
    cps = []
    for a in range(len(ins)):
        for j, (px, py) in enumerate(_other_chips(x, y)):
            cps.append(pltpu.make_async_remote_copy(
                src_ref=ins[a].at[2 * px + py, sel, :], dst_ref=lands[a].at[j, sel, :],
                send_sem=send_sems.at[3 * a + j], recv_sem=recv_sems.at[3 * a + j], device_id=(px, py, c), device_id_type=MESH))
    return cps


def _scatter_wait(qs, lands, send_sems, recv_sems, after, name, rows=None):
    n = len(qs)

    def body(*refs):
        for cp in _scatter_copies(refs[:n], refs[n:2 * n], refs[2 * n], refs[2 * n + 1], rows):
            cp.wait_send()
            cp.wait_recv()

    res = pl.pallas_call(
        body, name=name,
        in_specs=[HBM_SPEC] * (2 * n) + [SEM_SPEC, SEM_SPEC, ANY_SPEC],
        out_specs=[HBM_SPEC] * (2 * n),
        out_shape=[pltpu.HBM(b.shape, b.dtype) for b in qs + lands],
        input_output_aliases={a: a for a in range(2 * n)},
        compiler_params=pltpu.CompilerParams(has_side_effects=EFFECT),
    )(*qs, *lands, send_sems, recv_sems, after)
    return list(res[:n]), list(res[n:])


def _chip_sum(q, got, name):
    _, hr, cols = got.shape
    tr, tc = 512, _tile_cols(cols)
    chip = 2 * lax.axis_index("x") + lax.axis_index("y")
    c = lax.axis_index("c")

    def body(idx_ref, q_ref, g_ref, o_ref):
        del idx_ref
        acc = q_ref[...].astype(F32)
        for i in range(3):
            acc = acc + g_ref[i].astype(F32)
        o_ref[...] = acc

    grid_spec = pltpu.PrefetchScalarGridSpec(
        num_scalar_prefetch=1, grid=(hr // tr, cols // tc),
        in_specs=[pl.BlockSpec((None, tr, tc), lambda i, j, ix: (ix[0], i, j)),
                  pl.BlockSpec((3, tr, tc), lambda i, j, ix: (0, i, j))],
        out_specs=pl.BlockSpec((None, tr, tc), lambda i, j, ix: (ix[1], i, j)))
    return pl.pallas_call(
        body, name=name, grid_spec=grid_spec,
        out_shape=jax.ShapeDtypeStruct((2, hr, cols), F32),
        compiler_params=_cp(("parallel", "parallel")),
    )(jnp.stack([chip, c]).astype(jnp.int32), q, got)


def _join_halves(bufs, name):
    n = len(bufs)

    def body(*refs):
        outs = refs[n:2 * n]
        send_sems, recv_sems = refs[2 * n:]
        x, y, c = _place()
        cps = []
        for a in range(n):
            cps.append(pltpu.make_async_remote_copy(src_ref=outs[a].at[c], dst_ref=outs[a].at[c], send_sem=send_sems.at[a],
                                                    recv_sem=recv_sems.at[a], device_id=(x, y, 1 - c), device_id_type=MESH))
        for cp in cps:
            cp.start()
        for a in range(n):
            theirs = outs[a].at[1 - c]
            pltpu.make_async_remote_copy(src_ref=theirs, dst_ref=theirs, send_sem=send_sems.at[a], recv_sem=recv_sems.at[a],
                                         device_id=(x, y, 1 - c), device_id_type=MESH).wait_recv()
        for cp in cps:
            cp.wait_send()

    hbm = pl.BlockSpec(memory_space=pl.ANY)
    return pl.pallas_call(
        body, name=name,
        in_specs=[hbm] * n, out_specs=[hbm] * n,
        out_shape=[jax.ShapeDtypeStruct(b.shape, b.dtype) for b in bufs],
        input_output_aliases={a: a for a in range(n)},
        scratch_shapes=[pltpu.SemaphoreType.DMA((n,)), pltpu.SemaphoreType.DMA((n,))],
    )(*bufs)


def _join_copy(buf, send_sems, recv_sems):
    x, y, c = _place()
    sems = dict(send_sem=send_sems.at[0], recv_sem=recv_sems.at[0], device_id=(x, y, 1 - c), device_id_type=MESH)
    return (pltpu.make_async_remote_copy(src_ref=buf.at[c], dst_ref=buf.at[c], **sems),
            pltpu.make_async_remote_copy(src_ref=buf.at[1 - c], dst_ref=buf.at[1 - c], **sems))


def _join_start(buf, name):
    def body(b_ref, send_sems, recv_sems, thru, token):
        _join_copy(b_ref, send_sems, recv_sems)[0].start()
        token[...] = jnp.zeros_like(token)

    return pl.pallas_call(
        body, name=name,
        in_specs=[HBM_SPEC],
        out_specs=[SEM_SPEC, SEM_SPEC, HBM_SPEC, pl.BlockSpec(memory_space=pltpu.VMEM)],
        out_shape=[pltpu.SemaphoreType.DMA((1,)), pltpu.SemaphoreType.DMA((1,)), pltpu.HBM(buf.shape, buf.dtype), TOKEN],
        input_output_aliases={0: 2},
        compiler_params=pltpu.CompilerParams(has_side_effects=EFFECT),
    )(pltpu.with_memory_space_constraint(buf, pltpu.HBM))


def _join_wait(state, after, name):
    send_sems, recv_sems, buf, _ = state

    def body(b_ref, send_ref, recv_ref, *rest):
        out, inc = _join_copy(b_ref, send_ref, recv_ref)
        out.wait_send()
        inc.wait_recv()

    return pl.pallas_call(
        body, name=name,
        in_specs=[HBM_SPEC, SEM_SPEC, SEM_SPEC] + [ANY_SPEC] * len(after),
        out_specs=HBM_SPEC,
        out_shape=pltpu.HBM(buf.shape, buf.dtype),
        input_output_aliases={0: 0},
        compiler_params=pltpu.CompilerParams(has_side_effects=EFFECT),
    )(buf, send_sems, recv_sems, *after)


N_DEV = 8


def _gather_small(block, reduce, name, deps=()):
    m_per, cols = block.shape
    deps = list(deps)

    def body(x_ref, *rest):
        out_ref, all_ref, send_sems, recv_sems, local_sem = rest[len(deps):]
        x, y, c = _place()
        me, sibling = (x, y, c), (x, y, 1 - c)
        chips = _other_chips(x, y)

        def rows(px, py, pc):
            return all_ref.at[pl.ds((4 * px + 2 * py + pc) * m_per, m_per), :]

        def copy(k, block_of, to, src=None):
            return pltpu.make_async_remote_copy(
                src_ref=rows(*block_of) if src is None else src, dst_ref=rows(*block_of),
                send_sem=send_sems.at[k], recv_sem=recv_sems.at[k], device_id=to, device_id_type=MESH)

        mine = pltpu.make_async_copy(x_ref, rows(*me), local_sem)
        mine.start()
        first = [copy(0, me, sibling, src=x_ref)]
        first += [copy(1 + j, me, (*chip, c), src=x_ref) for j, chip in enumerate(chips)]
        for cp in first:
            cp.start()
        passed = [copy(4 + j, (*chip, c), sibling) for j, chip in enumerate(chips)]
        for j, chip in enumerate(chips):
            copy(1 + j, (*chip, c), me).wait_recv()
            passed[j].start()
        copy(0, sibling, me).wait_recv()
        for j, chip in enumerate(chips):
            copy(4 + j, (*chip, 1 - c), me).wait_recv()
        for cp in first + passed:
            cp.wait_send()
        mine.wait()
        if reduce:
            acc = all_ref[pl.ds(0, m_per), :]
            for i in range(1, N_DEV):
                acc = acc + all_ref[pl.ds(i * m_per, m_per), :]
            out_ref[...] = acc
        else:
            out_ref[...] = all_ref[...]

    out_rows = m_per if reduce else N_DEV * m_per
    return pl.pallas_call(
        body, name=name,
        in_specs=[pl.BlockSpec(memory_space=pltpu.VMEM)] + [pl.BlockSpec(memory_space=pl.ANY)] * len(deps),
        out_specs=pl.BlockSpec(memory_space=pltpu.VMEM),
        out_shape=jax.ShapeDtypeStruct((out_rows, cols), F32),
        scratch_shapes=[pltpu.VMEM((N_DEV * m_per, cols), F32), pltpu.SemaphoreType.DMA((7,)),
                        pltpu.SemaphoreType.DMA((7,)), pltpu.SemaphoreType.DMA],
    )(block, *deps)


class _Comm:
    def __init__(self, w_in, w_kv, w_out, chip, core):
        self.w_in, self.chip, self.core = w_in, chip, core
        self.bufs = [_cast_own(w_in, chip, "cast_w_in_sent", half=core), _cast_own(w_kv, chip, "cast_w_mem_kv"),
                     _cast_own(w_out, chip, "cast_w_out")]


    def gather_started(self):
        (self.sems,), (b_in,), tok = _gather_start(self.bufs[:1], (NEAR,), "gather_start_in_near")
        self.b_in = _cast_own(self.w_in, self.chip, "cast_w_in_kept", half=1 - self.core, prev=b_in)
        return [tok]

    def w_in_own(self):
        return self.b_in.reshape(NCHIP * D, WB)

    def w_in_near(self, after):
        (b_in,) = _gather_wait([self.b_in], self.sems, NEAR, list(after) + self.bufs[1:], "gather_wait_in_near")
        (b_in,) = _forward_halves([b_in], NEAR, "forward_in_near")
        (self.sems,), (self.b_in,), tok = _gather_start([b_in], (FAR,), "gather_start_in_far")
        return self.b_in.reshape(NCHIP * D, WB), tok

    def w_in_all(self, after):
        (b_in,) = _gather_wait([self.b_in], self.sems, FAR, after, "gather_wait_in_far")
        (b_in,) = _forward_halves([b_in], FAR, "forward_in_far")
        (self.sems,), self.b_rest, tok = _gather_start(self.bufs[1:], (NEAR + FAR,), "gather_start_rest")
        return b_in.reshape(NCHIP * D, WB), tok

    def w_rest_landed(self, after):
        b_rest = _gather_wait(self.b_rest, self.sems, NEAR + FAR, after, "gather_wait_rest")
        self.sems, self.b_rest, tok = _forward_start(b_rest, NEAR + FAR, "forward_start_rest")
        return tok

    def w_rest(self, after):
        b_kv, b_out = _forward_wait(self.b_rest, self.sems, NEAR + FAR, after, "forward_wait_rest")
        return b_kv.reshape(D, 2 * XW), b_out.reshape(D, D)

    def sibling_start(self, send, tag):
        return _sibling_start(send, "sibling_start_" + tag)

    def reduce_start_summed(self, keep, sent, tag, after=None):
        got = _sibling_wait(sent, [keep] + ([] if after is None else [after]), "sibling_wait_" + tag)
        return _scatter_start([_pair_sum_rows(keep, got, "pair_sum_" + tag)], "scatter_start_" + tag)

    def sibling_wait(self, sent, after, tag):
        return _sibling_wait(sent, after, "sibling_wait_" + tag)

    def scatter_rows(self, q, tag, rows, before=None):
        lands = None if before is None else before[3]
        return _scatter_start([q], "scatter_start_" + tag, rows=rows, lands=lands) + (rows,)

    def reduce_finish_start(self, states, tag, after):
        qs, lands = states[-1][2], states[-1][3]
        for i, st in enumerate(states):
            qs, lands = _scatter_wait(qs, lands, st[0], st[1], after, f"scatter_wait_{tag}{i}", rows=st[5])
        return _join_start(_chip_sum(qs[0], lands[0], f"chip_sum_{tag}0"), "join_start_" + tag)

    def reduce_finish_wait(self, pending, tag, after):
        j = _join_wait(pending, after, "join_wait_" + tag)
        return j.reshape(2 * j.shape[1], j.shape[2])

    def reduce_finish(self, state, tag, after):
        send_sems, recv_sems, qs, lands, _ = state
        qs, lands = _scatter_wait(qs, lands, send_sems, recv_sems, after, "scatter_wait_" + tag)
        halves = [_chip_sum(q, l, f"chip_sum_{tag}{i}") for i, (q, l) in enumerate(zip(qs, lands))]
        return [j.reshape(2 * j.shape[1], j.shape[2]) for j in _join_halves(halves, "join_halves_" + tag)]


def kernel(x, mem, pre_norm, w_in, conv_w, mem_norm, w_mem_kv, w_out, post_norm, loss_target, m_pre_norm, m_w_in, m_conv_w, m_mem_norm, m_w_mem_kv, m_w_out, m_post_norm, v_pre_norm, v_w_in, v_conv_w, v_mem_norm, v_w_mem_kv, v_w_out, v_post_norm):
    chip = 2 * lax.axis_index("x") + lax.axis_index("y")

    comm = _Comm(w_in[0], w_mem_kv[0], w_out[0], chip, lax.axis_index("c"))
    cw_blk = jnp.zeros((8, 384), F32).at[:3].set(conv_w[0])
    cw_all = _gather_small(cw_blk, False, "gather_conv_w", deps=comm.bufs[1:]).reshape(NCHIP, 2, 8, 384)[:, 0, :3]
    conv_full = jnp.transpose(cw_all, (1, 0, 2)).reshape(3, CW)
    moments = {"w_out": (w_out[0], m_w_out[0], v_w_out[0]), "w_mem_kv": (w_mem_kv[0], m_w_mem_kv[0], v_w_mem_kv[0])}
    loss, grad_x, d_pre, d_mem, d_post, d_conv, r_in, _, (upd_out, upd_kv) = _local_step(
        x, mem, pre_norm, conv_full, mem_norm, post_norm, loss_target, chip, lax.axis_index("c"), comm, moments)

    pack = jnp.concatenate([d_pre, d_mem, d_post, jnp.pad(d_conv, ((0, 0), (0, D - CW))),
                            jnp.pad(loss, ((0, 0), (0, D - 128))), jnp.zeros((1, D), F32)], axis=0)
    tot = _gather_small(pack, True, "reduce_small")
    g_pre, g_mem, g_post = tot[0:1], tot[1:2], tot[2:3]
    g_conv = lax.dynamic_slice(tot[3:6, :CW], (0, chip * 384), (3, 384))
    loss_out = tot[6, 0]

    names = ("pre_norm", "w_in", "conv_w", "mem_norm", "w_mem_kv", "w_out", "post_norm")
    ws = (pre_norm, w_in[0], conv_w[0], mem_norm, w_mem_kv[0], w_out[0], post_norm)
    gs = [g_pre, None, g_conv, g_mem, None, None, g_post]
    ms = (m_pre_norm, m_w_in[0], m_conv_w[0], m_mem_norm, m_w_mem_kv[0], m_w_out[0], m_post_norm)
    vs = (v_pre_norm, v_w_in[0], v_conv_w[0], v_mem_norm, v_w_mem_kv[0], v_w_out[0], v_post_norm)
    upd = [None if g is None else _adamw(w, g, m, v, "adamw_" + nm) for nm, w, g, m, v in zip(names, ws, gs, ms, vs)]
    upd[4], upd[5] = upd_kv, upd_out
    g_in = comm.reduce_finish_wait(r_in, "b", after=[u[1] for u in upd if u is not None])
    upd[1] = _adamw(ws[1], g_in, ms[1], vs[1], "adamw_w_in")

    def shaped(arrs):
        return [a.reshape(w.shape) if w.ndim == a.ndim else a.reshape((1,) + a.shape)
                for a, w in zip(arrs, (pre_norm, w_in, conv_w, mem_norm, w_mem_kv, w_out, post_norm))]

    grads = shaped([u[0] for u in upd])
    deltas = shaped([u[1] for u in upd])
    new_m = shaped([u[2] for u in upd])
    new_v = shaped([u[3] for u in upd])
    return (loss_out, grad_x, *grads, *deltas, *new_m, *new_v)
```

```python
import jax
import jax.numpy as jnp
from jax import lax
from jax.experimental import pallas as pl
from jax.experimental.pallas import tpu as pltpu

F32 = jnp.float32
BF16 = jnp.bfloat16

D = 4096
S = 2048
NB = 2
T = NB * S
MLEN = 256
HD = 128
AW = 1536
CW = 1536
XW = 1024
XHD = 256
NXH = 4
NC = 14336
QA, KA, VA, ZA, UC, BC, CC, ZC, QX, ZX = 0, 1536, 3072, 4608, 6144, 7680, 9216, 10752, 12288, 13312
NCHIP = 4
WB = NC // NCHIP
DIL = (1, 4, 16)
HPG = 4
EPS = 1e-6
NEG = -1e30
ROPE_THETA = 10000.0
A_SCALE = HD ** -0.5
X_SCALE = XHD ** -0.5

ADAM_LR = 0.001
ADAM_B1 = 0.9
ADAM_B2 = 0.999
ADAM_EPS = 1e-08
ADAM_WD = 0.01
ADAM_STEP = 10

MESH = pl.DeviceIdType.MESH
MIB = 1024 * 1024


def _cp(sem, vmem_mib=48):
    return pltpu.CompilerParams(dimension_semantics=sem, vmem_limit_bytes=vmem_mib * MIB)


def _sigmoid(z):
    return 1.0 / (1.0 + jnp.exp(-z))


def _rope(x, cos, sin, half):
    return x * cos + pltpu.roll(x, half, 1) * sin


def _rope_t(g, cos, sin, half):
    return g * cos + pltpu.roll(g * sin, half, 1)


def _rms_fwd(x2, g, name, dep=None):
    rows = x2.shape[0]
    tr = 256
    deps = [] if dep is None else list(dep)

    def body(x_ref, g_ref, *rest):
        o_ref = rest[-1]
        x = x_ref[...]
        r = lax.rsqrt(jnp.mean(x * x, axis=-1, keepdims=True) + EPS)
        o_ref[...] = (x * r * g_ref[...]).astype(BF16)

    return pl.pallas_call(
        body, name=name, grid=(rows // tr,),
        in_specs=[pl.BlockSpec((tr, D), lambda i: (i, 0)), pl.BlockSpec((1, D), lambda i: (0, 0))]
        + [pl.BlockSpec(memory_space=pl.ANY)] * len(deps),
        out_specs=pl.BlockSpec((tr, D), lambda i: (i, 0)),
        out_shape=jax.ShapeDtypeStruct((rows, D), BF16),
        compiler_params=_cp(("parallel",)),
    )(x2, g, *deps)


def _norm_gain_grad(dn, x2, name):
    rows = x2.shape[0]
    tr = 256

    def body(dn_ref, x_ref, dg_ref):
        @pl.when(pl.program_id(0) == 0)
        def _():
            dg_ref[...] = jnp.zeros_like(dg_ref)
        x = x_ref[...]
        r = lax.rsqrt(jnp.mean(x * x, axis=-1, keepdims=True) + EPS)
        dg_ref[...] += jnp.sum(dn_ref[...] * (x * r), axis=0, keepdims=True)

    return pl.pallas_call(
        body, name=name, grid=(rows // tr,),
        in_specs=[pl.BlockSpec((tr, D), lambda i: (i, 0)), pl.BlockSpec((tr, D), lambda i: (i, 0))],
        out_specs=pl.BlockSpec((1, D), lambda i: (0, 0)),
        out_shape=jax.ShapeDtypeStruct((1, D), F32),
        compiler_params=_cp(("arbitrary",)),
    )(dn, x2)


def _pre_norm_bwd(dh, x2, g, dout, dep=None):
    tr = 256
    deps = [] if dep is None else [dep]

    def body(dh_ref, x_ref, g_ref, dout_ref, *rest):
        gx_ref, dg_ref = rest[len(deps):]

        @pl.when(pl.program_id(0) == 0)
        def _():
            dg_ref[...] = jnp.zeros_like(dg_ref)

        x = x_ref[...]
        dh_ = dh_ref[...].astype(F32)
        r = lax.rsqrt(jnp.mean(x * x, axis=-1, keepdims=True) + EPS)
        xhat = x * r
        dg_ref[...] += jnp.sum(dh_ * xhat, axis=0, keepdims=True)
        dxn = dh_ * g_ref[...]
        gx_ref[...] = dout_ref[...].astype(F32) + r * (dxn - xhat * jnp.mean(dxn * xhat, axis=-1, keepdims=True))

    row = pl.BlockSpec((tr, D), lambda i: (i, 0))
    vec = pl.BlockSpec((1, D), lambda i: (0, 0))
    return pl.pallas_call(
        body, name="pre_norm_bwd", grid=(T // tr,),
        in_specs=[row, row, vec, row] + [pl.BlockSpec(memory_space=pl.ANY)] * len(deps),
        out_specs=[row, vec],
        out_shape=[jax.ShapeDtypeStruct((T, D), F32), jax.ShapeDtypeStruct((1, D), F32)],
        compiler_params=_cp(("arbitrary",)),
    )(dh, x2, g, dout, *deps)


def _post_norm_loss(y, x2, tgt, g):
    tr = 256

    def body(y_ref, x_ref, t_ref, g_ref, dy_ref, dout_ref, dg_ref, loss_ref):
        @pl.when(pl.program_id(0) == 0)
        def _():
            dg_ref[...] = jnp.zeros_like(dg_ref)
            loss_ref[...] = jnp.zeros_like(loss_ref)
        yv = y_ref[...]
        gv = g_ref[...]
        r = lax.rsqrt(jnp.mean(yv * yv, axis=-1, keepdims=True) + EPS)
        yhat = yv * r
        err = x_ref[...] + yhat * gv - t_ref[...]
        loss_ref[...] += jnp.sum(jnp.sum(err * err, axis=1, keepdims=True), axis=0, keepdims=True) * (0.5 / D)
        dout = err * (1.0 / D)
        dout_ref[...] = dout.astype(BF16)
        dg_ref[...] += jnp.sum(dout * yhat, axis=0, keepdims=True)
        dyn = dout * gv
        dy_ref[...] = (r * (dyn - yhat * jnp.mean(dyn * yhat, axis=-1, keepdims=True))).astype(BF16)

    row = pl.BlockSpec((tr, D), lambda i: (i, 0))
    vec = pl.BlockSpec((1, D), lambda i: (0, 0))
    return pl.pallas_call(
        body, name="post_norm_loss", grid=(T // tr,),
        in_specs=[row, row, row, vec],
        out_specs=[row, row, vec, pl.BlockSpec((1, 128), lambda i: (0, 0))],
        out_shape=[jax.ShapeDtypeStruct((T, D), BF16), jax.ShapeDtypeStruct((T, D), BF16),
                   jax.ShapeDtypeStruct((1, D), F32), jax.ShapeDtypeStruct((1, 128), F32)],
        compiler_params=_cp(("arbitrary",)),
    )(y, x2, tgt, g)


NN = (((1,), (0,)), ((), ()))
NT = (((1,), (1,)), ((), ()))
TN = (((0,), (0,)), ((), ()))


def _as_index(v):
    return jnp.reshape(v, (1,)).astype(jnp.int32)


def _matmul(a, b, *, name, dims, grid, a_block, a_map, b_block, b_map, o_block, o_map, out_shape, out_dtype=F32,
            index=None, prev=None, deps=(), addend=None):
    extra = ([] if prev is None else [prev]) + [d for d in deps if d is not None]
    first = 0 if index is None else 1
    nk = grid[2]
    in_place = out_dtype == F32
    assert addend is None or not in_place
    n_add = 0 if addend is None else 1

    def body(*refs):
        a_ref, b_ref, o_ref = refs[first], refs[first + 1], refs[first + 2 + n_add + len(extra)]
        acc_ref = o_ref if in_place else refs[-1]

        @pl.when(pl.program_id(2) == 0)
        def _():
            acc_ref[...] = lax.dot_general(a_ref[...], b_ref[...], dims, preferred_element_type=F32)

        @pl.when(pl.program_id(2) > 0)
        def _():
            acc_ref[...] += lax.dot_general(a_ref[...], b_ref[...], dims, preferred_element_type=F32)

        if not in_place:
            @pl.when(pl.program_id(2) == nk - 1)
            def _():
                total = acc_ref[...] if addend is None else acc_ref[...] + refs[first + 2][...].astype(F32)
                o_ref[...] = total.astype(o_ref.dtype)

    adds = [] if addend is None else [addend]
    in_specs = ([pl.BlockSpec(a_block, a_map), pl.BlockSpec(b_block, b_map)] + [pl.BlockSpec(o_block, o_map)] * n_add
                + [pl.BlockSpec(memory_space=pl.ANY)] * len(extra))
    out_specs = pl.BlockSpec(o_block, o_map)
    scratch = [] if in_place else [pltpu.VMEM(o_block, F32)]
    kwargs = dict(name=name, out_shape=jax.ShapeDtypeStruct(out_shape, out_dtype),
                  input_output_aliases={} if prev is None else {first + 2 + n_add: 0},
                  compiler_params=_cp(("parallel", "parallel", "arbitrary"), vmem_mib=56))
    if index is None:
        return pl.pallas_call(body, grid=grid, in_specs=in_specs, out_specs=out_specs, scratch_shapes=scratch,
                              **kwargs)(a, b, *adds, *extra)
    grid_spec = pltpu.PrefetchScalarGridSpec(num_scalar_prefetch=1, grid=grid, in_specs=in_specs, out_specs=out_specs,
                                             scratch_shapes=scratch)
    return pl.pallas_call(body, grid_spec=grid_spec, **kwargs)(_as_index(index), a, b, *adds, *extra)


def _mm_nn(a, b, name, tm, tn, tk):
    m, kd = a.shape
    n = b.shape[1]
    return _matmul(a, b, name=name, dims=NN, grid=(m // tm, n // tn, kd // tk),
                   a_block=(tm, tk), a_map=lambda i, j, k: (i, k),
                   b_block=(tk, tn), b_map=lambda i, j, k: (k, j),
                   o_block=(tm, tn), o_map=lambda i, j, k: (i, j), out_shape=(m, n))


def _mm_nt(a, b, name, tm, tn, tk):
    m, kd = a.shape
    n = b.shape[0]
    return _matmul(a, b, name=name, dims=NT, grid=(m // tm, n // tn, kd // tk),
                   a_block=(tm, tk), a_map=lambda i, j, k: (i, k),
                   b_block=(tn, tk), b_map=lambda i, j, k: (j, k),
                   o_block=(tm, tn), o_map=lambda i, j, k: (i, j), out_shape=(m, n))


W_TN = 1792
W_NJ = WB // W_TN


def _proj_part(h, wg, chip, masks, name, prev=None, dep=None):
    tm, tk = 1024, 2048

    def blk(j, ix):
        m = masks[0]
        for t in range(1, len(masks)):
            m = jnp.where(j // W_NJ == t, masks[t], m)
        return jnp.bitwise_xor(ix[0], m)

    return _matmul(h, wg, name=name, dims=NN, grid=(T // tm, len(masks) * W_NJ, D // tk), index=chip, prev=prev, deps=(dep,),
                   a_block=(tm, tk), a_map=lambda i, j, k, ix: (i, k),
                   b_block=(tk, W_TN), b_map=lambda i, j, k, ix: (blk(j, ix) * (D // tk) + k, j % W_NJ),
                   o_block=(tm, W_TN), o_map=lambda i, j, k, ix: (i, blk(j, ix) * W_NJ + j % W_NJ), out_shape=(T, NC),
                   out_dtype=BF16)


def _adamw_math(w_ref, g_ref, m_ref, v_ref, go_ref, d_ref, nm_ref, nv_ref):
    gv = g_ref[...]
    go_ref[...] = gv
    nm = ADAM_B1 * m_ref[...] + (1.0 - ADAM_B1) * gv
    nv = ADAM_B2 * v_ref[...] + (1.0 - ADAM_B2) * (gv * gv)
    m_hat = nm / (1.0 - ADAM_B1 ** ADAM_STEP)
    v_hat = nv / (1.0 - ADAM_B2 ** ADAM_STEP)
    d_ref[...] = -ADAM_LR * (m_hat / (jnp.sqrt(v_hat) + ADAM_EPS) + ADAM_WD * w_ref[...])
    nm_ref[...] = nm
    nv_ref[...] = nv


def _dh(dproj, wg, dep=None, adam=()):
    tm, tn = 1024, 2048
    grid = (T // tm, D // tn, NC // W_TN)
    nsteps = grid[0] * grid[1] * grid[2]
    deps = [] if dep is None else [dep]
    na = len(adam)
    nk = grid[2]

    def body(*refs):
        a_ref, b_ref = refs[0], refs[1]
        o_ref = refs[2 + 4 * na + len(deps)]
        acc_ref = refs[-1]

        @pl.when(pl.program_id(2) == 0)
        def _():
            acc_ref[...] = lax.dot_general(a_ref[...], b_ref[...], NT, preferred_element_type=F32)

        @pl.when(pl.program_id(2) > 0)
        def _():
            acc_ref[...] += lax.dot_general(a_ref[...], b_ref[...], NT, preferred_element_type=F32)

        @pl.when(pl.program_id(2) == nk - 1)
        def _():
            o_ref[...] = acc_ref[...].astype(BF16)

        for s in range(na):
            ins = refs[2 + 4 * s:6 + 4 * s]
            outs = refs[3 + 4 * na + len(deps) + 4 * s:7 + 4 * na + len(deps) + 4 * s]
            _adamw_math(*ins, *outs)

    def rows_of(arr):
        r, c = arr.shape
        return pl.BlockSpec((r // nsteps, c), lambda i, j, k: ((i * grid[1] + j) * grid[2] + k, 0))

    adam_specs = [rows_of(a) for st in adam for a in st]
    res = pl.pallas_call(
        body, name="dh", grid=grid,
        in_specs=[pl.BlockSpec((tm, W_TN), lambda i, j, k: (i, k)),
                  pl.BlockSpec((tn, W_TN), lambda i, j, k: ((k // W_NJ) * (D // tn) + j, k % W_NJ))]
        + adam_specs + [pl.BlockSpec(memory_space=pl.ANY)] * len(deps),
        out_specs=[pl.BlockSpec((tm, tn), lambda i, j, k: (i, j))] + adam_specs,
        out_shape=[jax.ShapeDtypeStruct((T, D), BF16)] + [jax.ShapeDtypeStruct(a.shape, F32) for st in adam for a in st],
        scratch_shapes=[pltpu.VMEM((tm, tn), F32)],
        compiler_params=_cp(("arbitrary",) * 3, vmem_mib=56),
    )(dproj, wg, *[a for st in adam for a in st], *deps)
    return res[0], [tuple(res[1 + 4 * s:5 + 4 * s]) for s in range(na)]


def _grad_rows(a, b, half, out_dtype, name, dep=None):
    kd, n = b.shape
    tm, tn, tk = D // NCHIP // 2, min(n, 2048), min(kd, 2048)
    return _matmul(a, b, name=name, dims=TN, grid=(NCHIP, n // tn, kd // tk), index=half, deps=(dep,),
                   a_block=(tk, tm), a_map=lambda i, j, k, ix: (k, 2 * i + ix[0]),
                   b_block=(tk, tn), b_map=lambda i, j, k, ix: (k, j),
                   o_block=(tm, tn), o_map=lambda i, j, k, ix: (i, j),
                   out_shape=(NCHIP * tm, n), out_dtype=out_dtype)


GW_TM = 1024


def _grad_w_in(h, dproj, half, out_dtype, name, dep=None, tiles=(0, 2), addend=None, prev=None):
    tm, tk = GW_TM, 2048
    nh = D // 2 // tm
    t0, nt = tiles

    def o_map(i, j, k, ix):
        return ((j // W_NJ) * nh + t0 + i, j % W_NJ)

    return _matmul(h, dproj, name=name, dims=TN, grid=(nt, NC // W_TN, T // tk), index=half, deps=(dep,), prev=prev,
                   a_block=(tk, tm), a_map=lambda i, j, k, ix: (k, ix[0] * nh + t0 + i),
                   b_block=(tk, W_TN), b_map=lambda i, j, k, ix: (k, j),
                   o_block=(tm, W_TN), o_map=o_map, addend=addend,
                   out_shape=(NCHIP * D // 2, WB), out_dtype=out_dtype)


def _rope_tables(pos, half):
    inv = 1.0 / (ROPE_THETA ** (jnp.arange(half, dtype=F32) / half))
    ang = pos.astype(F32)[:, None] * inv[None, :]
    cos, sin = jnp.cos(ang), jnp.sin(ang)
    return jnp.concatenate([cos, cos], axis=1), jnp.concatenate([-sin, sin], axis=1)


def _band_mask(r0):
    qi = lax.broadcasted_iota(jnp.int32, (128, 256), 0)
    kk = lax.broadcasted_iota(jnp.int32, (128, 256), 1)
    return (kk >= qi) & (kk <= qi + 128) & (kk + r0 >= 128)


def _window(r0, nblk):
    if nblk == 1:
        qi = lax.broadcasted_iota(jnp.int32, (128, 128), 0)
        kk = lax.broadcasted_iota(jnp.int32, (128, 128), 1)
        return pl.ds(128, 128), kk <= qi
    return pl.ds(r0, 256), _band_mask(r0)


def _dil_rows(r, n, d):
    if d == 1:
        return pl.ds(pl.multiple_of(n * 128, 128), 128)
    return pl.ds(r + d * 128 * n, 128, stride=d)


def _widen(refs, wide):
    if not wide:
        return refs

    def copy(n, carry):
        rows = pl.ds(pl.multiple_of(n * 256, 256), 256)
        for src, dst in zip(refs, wide):
            dst[rows, :] = src[rows, :].astype(F32)
        return carry

    lax.fori_loop(0, S // 256, copy, 0)
    return wide


def _attn_fwd(proj, cosf, sinf, g):
    d = DIL[g]
    ln = S // d
    nblk = ln // 128
    by_class = d % 8 == 0
    proj_v = proj.reshape(NB, S, NC)

    def body(q_ref, k_ref, v_ref, cos_ref, sin_ref, o_ref, l_ref, k_s, v_s, q_s, *wide):
        k_s[:, pl.ds(0, 128), :] = jnp.zeros((d, 128, HD), BF16)
        v_s[:, pl.ds(0, 128), :] = jnp.zeros((d, 128, HD), BF16)
        q_ref, k_ref, v_ref = _widen((q_ref, k_ref, v_ref), wide[:3] if d > 1 else ())

        def prep(i, carry):
            r, n = i // nblk, i % nblk
            rows = _dil_rows(r, n, d)
            dst = pl.ds(pl.multiple_of(n * 128 + 128, 128), 128)
            k_s[r, dst, :] = _rope(k_ref[rows, :].astype(F32), cos_ref[rows, :], sin_ref[rows, :], HD // 2).astype(BF16)
            v_s[r, dst, :] = v_ref[rows, :].astype(BF16)
            return carry

        lax.fori_loop(0, d * nblk, prep, 0, unroll=4)

        def step(i, carry):
            r, n = i // nblk, i % nblk
            rows = _dil_rows(r, n, d)
            r0 = pl.multiple_of(n * 128, 128)
            qr = _rope(q_ref[rows, :].astype(F32), cos_ref[rows, :], sin_ref[rows, :], HD // 2).astype(BF16)
            q_s[r, pl.ds(r0, 128), :] = qr
            win, mask = _window(r0, nblk)
            kw = k_s[r, win, :]
            vw = v_s[r, win, :]
            sc = lax.dot_general(qr, kw, NT, preferred_element_type=F32) * A_SCALE
            sc = jnp.where(mask, sc, NEG)
            m = jnp.max(sc, axis=1, keepdims=True)
            p = jnp.exp(sc - m)
            l = jnp.sum(p, axis=1, keepdims=True)
            o_blk = jnp.dot(p.astype(BF16), vw, preferred_element_type=F32) / l
            l_blk = jnp.broadcast_to(m + jnp.log(l), (128, HD))
            if by_class:
                dst = pl.ds(pl.multiple_of(r * ln + r0, 128), 128)
                wide[-2][dst, :] = o_blk
                wide[-1][dst, :] = l_blk
            else:
                o_ref[rows, :] = o_blk
                l_ref[rows, :] = l_blk
            return carry

        lax.fori_loop(0, d * nblk, step, 0, unroll=8 if d <= 4 else 4)

        if by_class:
            per_pos = d // 8

            def order(t, carry):
                src = pl.ds((t % per_pos) * 8 * ln + t // per_pos, 8, stride=ln)
                dst = pl.ds(pl.multiple_of(t * 8, 8), 8)
                o_ref[dst, :] = wide[-2][src, :]
                l_ref[dst, :] = wide[-1][src, :]
                return carry

            lax.fori_loop(0, S // 8, order, 0, unroll=8)

    def col(off):
        return lambda b, h: (b, 0, off // HD + HPG * g + h)

    blk = (None, S, HD)
    tab = pl.BlockSpec((S, HD), lambda b, h: (0, 0))
    out = pl.BlockSpec(blk, lambda b, h: (b, 0, h))
    kv = pl.BlockSpec((None, None, d, ln + 128, HD), lambda b, h: (b, h, 0, 0, 0))
    qq = pl.BlockSpec((None, None, d, ln, HD), lambda b, h: (b, h, 0, 0, 0))
    o, l, kr, vr, qr = pl.pallas_call(
        body, name=f"attn_fwd_d{d}", grid=(NB, HPG),
        in_specs=[pl.BlockSpec(blk, col(QA)), pl.BlockSpec(blk, col(KA)), pl.BlockSpec(blk, col(VA)), tab, tab],
        out_specs=[out, out, kv, kv, qq],
        out_shape=[jax.ShapeDtypeStruct((NB, S, HPG * HD), F32)] * 2
        + [jax.ShapeDtypeStruct((NB, HPG, d, ln + 128, HD), BF16)] * 2 + [jax.ShapeDtypeStruct((NB, HPG, d, ln, HD), BF16)],
        scratch_shapes=[pltpu.VMEM((S, HD), F32)] * ((3 if d > 1 else 0) + (2 if by_class else 0)),
        compiler_params=_cp(("parallel", "parallel")),
    )(proj_v, proj_v, proj_v, cosf, sinf)
    return o.reshape(T, HPG * HD), l.reshape(T, HPG * HD), (qr, kr, vr)


def _attn_bwd(qkv, cosf, sinf, da, lse, delta, dproj, g):
    d = DIL[g]
    ln = S // d
    nblk = ln // 128
    dproj_v = dproj.reshape(NB, S, NC)
    da_v = da.reshape(NB, S, AW)
    lse_v = lse.reshape(NB, S, HPG * HD)
    delta_v = delta.reshape(NB, S, HPG * HD)

    def body(q_s, k_s, v_s, cos_ref, sin_ref, da_ref, lse_ref, dl_ref, dp_in_ref, o_ref, stg, dk_s, dv_s):
        del dp_in_ref
        w = pl.program_id(2)

        def emit():
            def cast(n, carry):
                rows = pl.ds(pl.multiple_of(n * 256, 256), 256)
                o_ref[rows, :] = stg[rows, :].astype(BF16)
                return carry

            lax.fori_loop(0, S // 256, cast, 0)

        @pl.when(w == 0)
        def _():
            dk_s[...] = jnp.zeros_like(dk_s)
            dv_s[...] = jnp.zeros_like(dv_s)

            def step(i, carry):
                r, n = i // nblk, i % nblk
                rows = _dil_rows(r, n, d)
                r0 = pl.multiple_of(n * 128, 128)
                win, mask = _window(r0, nblk)
                cos, sin = cos_ref[rows, :], sin_ref[rows, :]
                qr = q_s[r, pl.ds(r0, 128), :]
                kw = k_s[r, win, :]
                vw = v_s[r, win, :]
                sc = lax.dot_general(qr, kw, NT, preferred_element_type=F32) * A_SCALE
                sc = jnp.where(mask, sc, NEG)
                p = jnp.exp(sc - lse_ref[rows, :][:, :1])
                da_b = da_ref[rows, :].astype(BF16)
                dp = lax.dot_general(da_b, vw, NT, preferred_element_type=F32)
                ds_b = (p * (dp - dl_ref[rows, :][:, :1]) * A_SCALE).astype(BF16)
                p_b = p.astype(BF16)
                dq = jnp.dot(ds_b, kw, preferred_element_type=F32)
                stg[rows, :] = _rope_t(dq, cos, sin, HD // 2)
                dk_s[r, win, :] += lax.dot_general(ds_b, qr, TN, preferred_element_type=F32)
                dv_s[r, win, :] += lax.dot_general(p_b, da_b, TN, preferred_element_type=F32)
                return carry

            lax.fori_loop(0, d * nblk, step, 0, unroll=8 if d <= 4 else 4)
            emit()

        @pl.when(w == 1)
        def _():
            def put(i, carry):
                r, n = i // nblk, i % nblk
                rows = _dil_rows(r, n, d)
                src = pl.ds(pl.multiple_of(n * 128 + 128, 128), 128)
                stg[rows, :] = _rope_t(dk_s[r, src, :], cos_ref[rows, :], sin_ref[rows, :], HD // 2)
                return carry

            lax.fori_loop(0, d * nblk, put, 0, unroll=4)
            emit()

        @pl.when(w == 2)
        def _():
            def put(i, carry):
                r, n = i // nblk, i % nblk
                src = pl.ds(pl.multiple_of(n * 128 + 128, 128), 128)
                stg[_dil_rows(r, n, d), :] = dv_s[r, src, :]
                return carry

            lax.fori_loop(0, d * nblk, put, 0, unroll=4)
            emit()

    def col(off):
        return lambda b, h, w: (ahead(b, h, w)[0], 0, off // HD + HPG * g + ahead(b, h, w)[1])

    def ahead(b, h, w):
        flat = jnp.minimum(b * HPG + h + jnp.where(w > 0, 1, 0), NB * HPG - 1)
        return flat // HPG, flat % HPG

    blk = (None, S, HD)
    tab = pl.BlockSpec((S, HD), lambda b, h, w: (0, 0))
    per_head = pl.BlockSpec(blk, lambda b, h, w: (ahead(b, h, w)[0], 0, ahead(b, h, w)[1]))
    kv = pl.BlockSpec((None, None, d, ln + 128, HD), lambda b, h, w: (*ahead(b, h, w), 0, 0, 0))
    qq = pl.BlockSpec((None, None, d, ln, HD), lambda b, h, w: (*ahead(b, h, w), 0, 0, 0))
    out = pl.pallas_call(
        body, name=f"attn_bwd_d{d}", grid=(NB, HPG, 3),
        in_specs=[qq, kv, kv, tab, tab,
                  pl.BlockSpec(blk, col(0)), per_head, per_head, pl.BlockSpec(memory_space=pl.ANY)],
        out_specs=pl.BlockSpec(blk, lambda b, h, w: (b, 0, (AW // HD) * w + HPG * g + h)),
        out_shape=jax.ShapeDtypeStruct(dproj_v.shape, BF16),
        input_output_aliases={8: 0},
        scratch_shapes=[pltpu.VMEM((S, HD), F32), pltpu.VMEM((d, ln + 128, HD), F32), pltpu.VMEM((d, ln + 128, HD), F32)],
        compiler_params=_cp(("arbitrary",) * 3),
    )(*qkv, cosf, sinf, da_v, lse_v, delta_v, dproj_v)
    return out.reshape(T, NC)


def _attn_mix(proj, os_, ls_, dep=None):
    tr = 256
    gw = HPG * HD
    deps = [] if dep is None else [dep]

    def body(o0, o1, o2, l0, l1, l2, z_ref, *rest):
        cat_ref = rest[-1]
        m = jnp.maximum(jnp.maximum(l0[...], l1[...]), l2[...])
        e = [jnp.exp(l[...] - m) for l in (l0, l1, l2)]
        inv = 1.0 / (e[0] + e[1] + e[2])
        for gi, o in enumerate((o0, o1, o2)):
            z = z_ref[:, gi * gw:(gi + 1) * gw].astype(F32)
            cat_ref[:, gi * gw:(gi + 1) * gw] = (o[...] * (e[gi] * inv) * (z * _sigmoid(z))).astype(BF16)

    grp = pl.BlockSpec((tr, gw), lambda i: (i, 0))
    return pl.pallas_call(
        body, name="attn_mix", grid=(T // tr,),
        in_specs=[grp] * 6 + [pl.BlockSpec((tr, AW), lambda i: (i, ZA // AW))] + [ANY_SPEC] * len(deps),
        out_specs=pl.BlockSpec((tr, AW), lambda i: (i, 0)),
        out_shape=jax.ShapeDtypeStruct((T, D), BF16),
        compiler_params=_cp(("parallel",)),
    )(*os_, *ls_, proj, *deps)


def _attn_mix_bwd(dcat, proj, os_, ls_, dep=None):
    tr = 256
    gw = HPG * HD
    deps = [] if dep is None else [dep]

    def body(dy_ref, o0, o1, o2, l0, l1, l2, z_ref, *rest):
        da_ref, lse_ref, dl_ref, dz_ref = rest[len(deps):]
        m = jnp.maximum(jnp.maximum(l0[...], l1[...]), l2[...])
        e = [jnp.exp(l[...] - m) for l in (l0, l1, l2)]
        den = e[0] + e[1] + e[2]
        inv = 1.0 / den
        lse_ref[...] = m + jnp.log(den)
        acc = jnp.zeros((tr, gw), F32)
        for gi, o in enumerate((o0, o1, o2)):
            cols = slice(gi * gw, (gi + 1) * gw)
            z = z_ref[:, cols].astype(F32)
            dy = dy_ref[:, cols]
            sg = _sigmoid(z)
            a = o[...] * (e[gi] * inv)
            da = dy * (z * sg)
            da_ref[:, cols] = da
            dz_ref[:, cols] = (dy * a * (sg * (1.0 + z * (1.0 - sg)))).astype(BF16)
            acc = acc + da * a
        for hh in range(HPG):
            cols = slice(hh * HD, (hh + 1) * HD)
            dl_ref[:, cols] = jnp.broadcast_to(jnp.sum(acc[:, cols], axis=1, keepdims=True), (tr, HD))

    grp = pl.BlockSpec((tr, gw), lambda i: (i, 0))
    return pl.pallas_call(
        body, name="attn_mix_bwd", grid=(T // tr,),
        in_specs=[pl.BlockSpec((tr, AW), lambda i: (i, 0))] + [grp] * 6 + [pl.BlockSpec((tr, AW), lambda i: (i, ZA // AW))]
        + [pl.BlockSpec(memory_space=pl.ANY)] * len(deps),
        out_specs=[pl.BlockSpec((tr, AW), lambda i: (i, 0)), grp, grp, pl.BlockSpec((tr, AW), lambda i: (i, ZA // AW))],
        out_shape=[jax.ShapeDtypeStruct((T, AW), F32), jax.ShapeDtypeStruct((T, gw), F32),
                   jax.ShapeDtypeStruct((T, gw), F32), jax.ShapeDtypeStruct((T, NC), BF16)],
        compiler_params=_cp(("parallel",)),
    )(dcat, *os_, *ls_, proj, *deps)


CT = 256


def _shift_down(x, n):
    rows = lax.broadcasted_iota(jnp.int32, x.shape, 0)
    return jnp.where(rows >= n, pltpu.roll(x, n, 0), 0.0)


def _shift_up(x, n):
    rows = lax.broadcasted_iota(jnp.int32, x.shape, 0)
    return jnp.where(rows < x.shape[0] - n, pltpu.roll(x, x.shape[0] - n, 0), 0.0)


def _conv_fwd(proj, conv_w, cat):
    proj_v = proj.reshape(NB, S, NC)
    cat_v = cat.reshape(NB, S, D)

    def body(u_ref, b_ref, c_ref, z_ref, w_ref, cat_in, o_ref):
        del cat_in
        cu = c_ref[...].astype(F32) * u_ref[...].astype(F32)
        cv = _shift_down(cu, 2) * w_ref[0:1, :] + _shift_down(cu, 1) * w_ref[1:2, :] + cu * w_ref[2:3, :]
        z = z_ref[...].astype(F32)
        o_ref[...] = (b_ref[...].astype(F32) * cv * (z * _sigmoid(z))).astype(BF16)

    def seg(off):
        return pl.BlockSpec((None, S, CT), lambda b, j: (b, 0, off // CT + j))

    out = pl.pallas_call(
        body, name="conv_fwd", grid=(NB, CW // CT),
        in_specs=[seg(UC), seg(BC), seg(CC), seg(ZC), pl.BlockSpec((3, CT), lambda b, j: (0, j)),
                  pl.BlockSpec(memory_space=pl.ANY)],
        out_specs=pl.BlockSpec((None, S, CT), lambda b, j: (b, 0, AW // CT + j)),
        out_shape=jax.ShapeDtypeStruct((NB, S, D), BF16),
        input_output_aliases={5: 0},
        compiler_params=_cp(("parallel", "parallel")),
    )(proj_v, proj_v, proj_v, proj_v, conv_w, cat_v)
    return out.reshape(T, D)


def _conv_bwd(dcat, proj, conv_w, dproj, dep=None):
    deps = [] if dep is None else [dep]
    proj_v = proj.reshape(NB, S, NC)
    dproj_v = dproj.reshape(NB, S, NC)
    dcat_v = dcat.reshape(NB, S, D)

    def body(dy_ref, u_ref, b_ref, c_ref, z_ref, w_ref, *rest):
        o_ref, dw_ref, st = rest[1 + len(deps):]
        b = pl.program_id(1)
        w = pl.program_id(2)

        @pl.when((b == 0) & (w == 0))
        def _():
            dw_ref[...] = jnp.zeros_like(dw_ref)

        @pl.when(w == 0)
        def _():
            u, c, z, bb = (r[...].astype(F32) for r in (u_ref, c_ref, z_ref, b_ref))
            dy = dy_ref[...]
            cu = c * u
            s1 = _shift_down(cu, 1)
            s2 = _shift_down(cu, 2)
            cv = s2 * w_ref[0:1, :] + s1 * w_ref[1:2, :] + cu * w_ref[2:3, :]
            sg = _sigmoid(z)
            sz = z * sg
            dcv = dy * bb * sz
            st[0] = dy * cv * sz
            st[2] = dy * bb * cv * (sg * (1.0 + z * (1.0 - sg)))
            dw_ref[0:1, :] += jnp.sum(dcv * s2, axis=0, keepdims=True)
            dw_ref[1:2, :] += jnp.sum(dcv * s1, axis=0, keepdims=True)
            dw_ref[2:3, :] += jnp.sum(dcv * cu, axis=0, keepdims=True)
            dcu = dcv * w_ref[2:3, :] + _shift_up(dcv, 1) * w_ref[1:2, :] + _shift_up(dcv, 2) * w_ref[0:1, :]
            st[1] = dcu * u
            o_ref[...] = (dcu * c).astype(BF16)

        for k in range(3):
            @pl.when(w == k + 1)
            def _(k=k):
                o_ref[...] = st[k].astype(BF16)

    def ahead(j, b, w):
        flat = jnp.minimum(j * NB + b + jnp.where(w > 0, 1, 0), (CW // CT) * NB - 1)
        return flat // NB, flat % NB

    def seg(off):
        return pl.BlockSpec((None, S, CT), lambda j, b, w: (ahead(j, b, w)[1], 0, off // CT + ahead(j, b, w)[0]))

    out, dw = pl.pallas_call(
        body, name="conv_bwd", grid=(CW // CT, NB, 4),
        in_specs=[seg(AW), seg(UC), seg(BC), seg(CC), seg(ZC),
                  pl.BlockSpec((3, CT), lambda j, b, w: (0, ahead(j, b, w)[0])),
                  pl.BlockSpec(memory_space=pl.ANY)] + [pl.BlockSpec(memory_space=pl.ANY)] * len(deps),
        out_specs=[pl.BlockSpec((None, S, CT), lambda j, b, w: (b, 0, (UC + w * CW) // CT + j)),
                   pl.BlockSpec((3, CT), lambda j, b, w: (0, j))],
        out_shape=[jax.ShapeDtypeStruct((NB, S, NC), BF16), jax.ShapeDtypeStruct((3, CW), F32)],
        input_output_aliases={6: 0},
        scratch_shapes=[pltpu.VMEM((3, S, CT), F32)],
        compiler_params=_cp(("arbitrary",) * 3),
    )(dcat_v, proj_v, proj_v, proj_v, proj_v, conv_w, dproj_v, *deps)
    return out.reshape(T, NC), dw


XT = 1024


def _cross_fwd(proj, mkv, cosq, sinq, cosm, sinm, cat):
    proj_v = proj.reshape(NB, S, NC)
    mkv_v = mkv.reshape(NB, MLEN, 2 * XW)
    cat_v = cat.reshape(NB, S, D)

    def body(q_ref, z_ref, mk_ref, mv_ref, cq, sq, cm, sm, cat_in, o_ref):
        del cat_in
        mkr = _rope(mk_ref[...], cm[...], sm[...], XHD // 2).astype(BF16)
        qr = _rope(q_ref[...].astype(F32), cq[...], sq[...], XHD // 2).astype(BF16)
        sc = lax.dot_general(qr, mkr, NT, preferred_element_type=F32) * X_SCALE
        p = jnp.exp(sc - jnp.max(sc, axis=1, keepdims=True))
        p = p / jnp.sum(p, axis=1, keepdims=True)
        ox = jnp.dot(p.astype(BF16), mv_ref[...].astype(BF16), preferred_element_type=F32)
        z = z_ref[...].astype(F32)
        o_ref[...] = (ox * (z * _sigmoid(z))).astype(BF16)

    def seg(off):
        return pl.BlockSpec((None, XT, XHD), lambda b, h, t: (b, t, off // XHD + h))

    qtab = pl.BlockSpec((XT, XHD), lambda b, h, t: (t, 0))
    mtab = pl.BlockSpec((MLEN, XHD), lambda b, h, t: (0, 0))
    out = pl.pallas_call(
        body, name="cross_fwd", grid=(NB, NXH, S // XT),
        in_specs=[seg(QX), seg(ZX),
                  pl.BlockSpec((None, MLEN, XHD), lambda b, h, t: (b, 0, h)),
                  pl.BlockSpec((None, MLEN, XHD), lambda b, h, t: (b, 0, NXH + h)),
                  qtab, qtab, mtab, mtab, pl.BlockSpec(memory_space=pl.ANY)],
        out_specs=pl.BlockSpec((None, XT, XHD), lambda b, h, t: (b, t, (AW + CW) // XHD + h)),
        out_shape=jax.ShapeDtypeStruct((NB, S, D), BF16),
        input_output_aliases={8: 0},
        compiler_params=_cp(("parallel",) * 3),
    )(proj_v, proj_v, mkv_v, mkv_v, cosq, sinq, cosm, sinm, cat_v)
    return out.reshape(T, D)


def _cross_bwd(dcat, proj, mkv, cosq, sinq, cosm, sinm, dproj):
    proj_v = proj.reshape(NB, S, NC)
    dproj_v = dproj.reshape(NB, S, NC)
    dcat_v = dcat.reshape(NB, S, D)
    mkv_v = mkv.reshape(NB, MLEN, 2 * XW)
    nt = S // XT

    def body(dy_ref, q_ref, z_ref, mk_ref, mv_ref, cq, sq, cm, sm, dp_in, o_ref, dmk_ref, dmv_ref, dz_s):
        del dp_in
        t = pl.program_id(2)
        w = pl.program_id(3)

        @pl.when((t == 0) & (w == 0))
        def _():
            dmk_ref[...] = jnp.zeros_like(dmk_ref)
            dmv_ref[...] = jnp.zeros_like(dmv_ref)

        @pl.when(w == 0)
        def _():
            mkr = _rope(mk_ref[...], cm[...], sm[...], XHD // 2).astype(BF16)
            mv_b = mv_ref[...].astype(BF16)
            qr = _rope(q_ref[...].astype(F32), cq[...], sq[...], XHD // 2).astype(BF16)
            sc = lax.dot_general(qr, mkr, NT, preferred_element_type=F32) * X_SCALE
            p = jnp.exp(sc - jnp.max(sc, axis=1, keepdims=True))
            p = p / jnp.sum(p, axis=1, keepdims=True)
            p_b = p.astype(BF16)
            ox = jnp.dot(p_b, mv_b, preferred_element_type=F32)
            z = z_ref[...].astype(F32)
            dy = dy_ref[...]
            sg = _sigmoid(z)
            dz_s[...] = dy * ox * (sg * (1.0 + z * (1.0 - sg)))
            dox_b = (dy * (z * sg)).astype(BF16)
            dp = lax.dot_general(dox_b, mv_b, NT, preferred_element_type=F32)
            ds_b = (p * (dp - jnp.sum(dp * p, axis=1, keepdims=True)) * X_SCALE).astype(BF16)
            dq = jnp.dot(ds_b, mkr, preferred_element_type=F32)
            o_ref[...] = _rope_t(dq, cq[...], sq[...], XHD // 2).astype(BF16)
            dmk_ref[...] += lax.dot_general(ds_b, qr, TN, preferred_element_type=F32)
            dmv_ref[...] += lax.dot_general(p_b, dox_b, TN, preferred_element_type=F32)

        @pl.when(w == 1)
        def _():
            o_ref[...] = dz_s[...].astype(BF16)

        @pl.when((t == nt - 1) & (w == 1))
        def _():
            dmk_ref[...] = _rope_t(dmk_ref[...], cm[...], sm[...], XHD // 2)

    def ahead(b, h, t, w):
        flat = jnp.minimum((b * NXH + h) * nt + t + jnp.where(w > 0, 1, 0), NB * NXH * nt - 1)
        return flat // (NXH * nt), (flat // nt) % NXH, flat % nt

    def seg(off):
        def index(b, h, t, w):
            b2, h2, t2 = ahead(b, h, t, w)
            return b2, t2, off // XHD + h2
        return pl.BlockSpec((None, XT, XHD), index)

    def mem(off):
        def index(b, h, t, w):
            b2, h2, _ = ahead(b, h, t, w)
            return b2, 0, off + h2
        return pl.BlockSpec((None, MLEN, XHD), index)

    qtab = pl.BlockSpec((XT, XHD), lambda b, h, t, w: (ahead(b, h, t, w)[2], 0))
    mtab = pl.BlockSpec((MLEN, XHD), lambda b, h, t, w: (0, 0))
    macc = pl.BlockSpec((None, MLEN, XHD), lambda b, h, t, w: (b, 0, h))
    out, dmk, dmv = pl.pallas_call(
        body, name="cross_bwd", grid=(NB, NXH, nt, 2),
        in_specs=[seg(AW + CW), seg(QX), seg(ZX), mem(0), mem(NXH),
                  qtab, qtab, mtab, mtab, pl.BlockSpec(memory_space=pl.ANY)],
        out_specs=[pl.BlockSpec((None, XT, XHD), lambda b, h, t, w: (b, t, (QX + w * XW) // XHD + h)), macc, macc],
        out_shape=[jax.ShapeDtypeStruct((NB, S, NC), BF16), jax.ShapeDtypeStruct((NB, MLEN, XW), F32),
                   jax.ShapeDtypeStruct((NB, MLEN, XW), F32)],
        input_output_aliases={9: 0},
        scratch_shapes=[pltpu.VMEM((XT, XHD), F32)],
        compiler_params=_cp(("arbitrary",) * 4),
    )(dcat_v, proj_v, proj_v, mkv_v, mkv_v, cosq, sinq, cosm, sinm, dproj_v)
    return out.reshape(T, NC), dmk, dmv


def _local_step(x, mem, pre_norm, conv_w, mem_norm, post_norm, tgt, chip, core, comm, moments=None):
    x2 = x.reshape(T, D)
    mem2 = mem.reshape(NB * MLEN, D)
    tgt2 = tgt.reshape(T, D)
    cosa, sina = _rope_tables(jnp.arange(S), HD // 2)
    cosq, sinq = _rope_tables(jnp.arange(S) + MLEN, XHD // 2)
    cosm, sinm = _rope_tables(jnp.arange(MLEN), XHD // 2)

    h = _rms_fwd(x2, pre_norm, "pre_norm_fwd", dep=comm.gather_started())
    memn = _rms_fwd(mem2, mem_norm, "mem_norm_fwd")
    proj = _proj_part(h, comm.w_in_own(), chip, (0,), "proj_own")
    wg_in, tok = comm.w_in_near(after=[proj, memn, conv_w])
    proj = _proj_part(h, wg_in, chip, (REL_XOR[0], REL_XOR[1]), "proj_near", prev=proj, dep=tok)
    wg_in, tok = comm.w_in_all(after=proj)
    proj = _proj_part(h, wg_in, chip, (REL_XOR[2],), "proj_far", prev=proj, dep=tok)
    fw = [_attn_fwd(proj, cosa, sina, g) for g in range(3)]
    os_ = [f[0] for f in fw]
    ls_ = [f[1] for f in fw]
    tok = comm.w_rest_landed(after=ls_)
    cat = _attn_mix(proj, os_, ls_, dep=tok)
    cat = _conv_fwd(proj, conv_w, cat)
    wg_kv, wg_out = comm.w_rest(after=[cat])
    mkv = _mm_nn(memn, wg_kv, "mkv", NB * MLEN, 1024, 512)
    cat = _cross_fwd(proj, mkv, cosq, sinq, cosm, sinm, cat)
    y = _mm_nn(cat, wg_out, "out_proj", 1024, 2048, 2048)
    dy, dout, d_post, loss = _post_norm_loss(y, x2, tgt2, post_norm)

    dcat = _mm_nt(dy, wg_out, "dcat", 1024, 2048, 2048)
    g_send = _grad_rows(cat, dy,1 - core, BF16, "grad_w_out_send")
    sent = comm.sibling_start(g_send.reshape(NCHIP, D // NCHIP // 2, D), "a")
    g_keep = _grad_rows(cat, dy,core, BF16, "grad_w_out_keep", dep=sent[-1])
    red_a = comm.reduce_start_summed(g_keep.reshape(NCHIP, D // NCHIP // 2, D), sent, "a")
    da, lse, delta, dproj = _attn_mix_bwd(dcat, proj, os_, ls_, dep=red_a[-1])
    for g in range(3):
        dproj = _attn_bwd(fw[g][2], cosa, sina, da, lse, delta, dproj, g)
    dproj, dmk, dmv = _cross_bwd(dcat, proj, mkv, cosq, sinq, cosm, sinm, dproj)
    dmkv = jnp.concatenate([dmk, dmv], axis=-1).reshape(NB * MLEN, 2 * XW)
    dmkv_b = dmkv.astype(BF16)
    gk_send = _grad_rows(memn, dmkv_b, 1 - core, BF16, "grad_w_mem_kv_send")
    sent = comm.sibling_start(gk_send.reshape(NCHIP, D // NCHIP // 2, 2 * XW), "k")
    gk_keep = _grad_rows(memn, dmkv_b, core, BF16, "grad_w_mem_kv_keep", dep=sent[-1])
    dmemn = _mm_nt(dmkv_b, wg_kv, "dmemn", NB * MLEN, 1024, 512)
    d_mem = _norm_gain_grad(dmemn, mem2, "mem_norm_bwd")
    red_k = comm.reduce_start_summed(gk_keep.reshape(NCHIP, D // NCHIP // 2, 2 * XW), sent, "k", after=d_mem)
    dproj, d_conv = _conv_bwd(dcat, proj, conv_w, dproj, dep=red_k[-1])
    (r_out,) = comm.reduce_finish(red_a, "a", after=d_conv)
    (r_kv,) = comm.reduce_finish(red_k, "k", after=d_conv)

    g_send = _grad_w_in(h, dproj, 1 - core, BF16, "grad_w_in_send", dep=r_kv)
    sent = comm.sibling_start(g_send.reshape(NCHIP, D // 2, WB), "b")
    q = _grad_w_in(h, dproj, core, BF16, "grad_w_in_keep_0", dep=sent[-1], tiles=(0, 1))
    got = comm.sibling_wait(sent, [q], "b")
    q = _pair_sum_rows(q.reshape(NCHIP, D // 2, WB), got, "pair_sum_b", rows=(0, GW_TM))
    red_b0 = comm.scatter_rows(q, "b0", (0, GW_TM))
    q = _grad_w_in(h, dproj, core, BF16, "grad_w_in_keep_1", dep=red_b0[4], tiles=(1, 1),
                   addend=got.reshape(NCHIP * D // 2, WB), prev=red_b0[2][0].reshape(NCHIP * D // 2, WB))
    red_b1 = comm.scatter_rows(q.reshape(NCHIP, D // 2, WB), "b1", (GW_TM, GW_TM), before=red_b0)
    sets = [] if moments is None else [(moments["w_out"][0], r_out, *moments["w_out"][1:]),
                                      (moments["w_mem_kv"][0], r_kv, *moments["w_mem_kv"][1:])]
    dh, updates = _dh(dproj, wg_in, dep=red_b1[4], adam=sets)
    r_in = comm.reduce_finish_start([red_b0, red_b1], "b", after=dh)
    grad_x, d_pre = _pre_norm_bwd(dh, x2, pre_norm, dout, dep=r_in[-1])
    return loss, grad_x.reshape(NB, S, D), d_pre, d_mem, d_post, d_conv, r_in, (r_kv, r_out), updates


def _adamw(w, g, m, v, name):
    rows, cols = w.shape
    tr = rows if rows <= 512 else 512
    tc = cols if cols <= 1024 else 1024
    if cols % tc:
        tc = 896

    blk = pl.BlockSpec((tr, tc), lambda i, j: (i, j))
    sds = jax.ShapeDtypeStruct((rows, cols), F32)
    def body(*refs):
        _adamw_math(*refs)

    return pl.pallas_call(
        body, name=name, grid=(rows // tr, cols // tc),
        in_specs=[blk] * 4, out_specs=[blk] * 4, out_shape=[sds] * 4,
        compiler_params=_cp(("parallel", "parallel")),
    )(w, g, m, v)


def _place():
    return lax.axis_index("x"), lax.axis_index("y"), lax.axis_index("c")


def _other_chips(x, y):
    return [(1 - x, y), (x, 1 - y), (1 - x, 1 - y)]


def _tile_cols(cols):
    return cols if cols <= 1024 else (1024 if cols % 1024 == 0 else 896)


def _cast_own(w, chip, name, half=None, prev=None):
    rows, cols = w.shape
    tr, tc = 512, _tile_cols(cols)
    nrt = rows // tr if half is None else rows // tr // 2
    index = jnp.stack([chip, 0 if half is None else half]).astype(jnp.int32)
    extra = [] if prev is None else [prev]

    def body(ix_ref, w_ref, *rest):
        rest[-1][...] = w_ref[...].astype(BF16)

    grid_spec = pltpu.PrefetchScalarGridSpec(
        num_scalar_prefetch=1, grid=(nrt, cols // tc),
        in_specs=[pl.BlockSpec((tr, tc), lambda i, j, ix: (ix[1] * nrt + i, j))] + [ANY_SPEC] * len(extra),
        out_specs=pl.BlockSpec((None, tr, tc), lambda i, j, ix: (ix[0], ix[1] * nrt + i, j)))
    return pl.pallas_call(
        body, name=name, grid_spec=grid_spec,
        out_shape=jax.ShapeDtypeStruct((NCHIP, rows, cols), BF16),
        input_output_aliases={} if prev is None else {2: 0},
        compiler_params=_cp(("parallel", "parallel")),
    )(index, w, *extra)


HBM_SPEC = pl.BlockSpec(memory_space=pltpu.HBM)
SEM_SPEC = pl.BlockSpec(memory_space=pltpu.SEMAPHORE)
ANY_SPEC = pl.BlockSpec(memory_space=pl.ANY)
EFFECT = pltpu.SideEffectType.DATAFLOW_SIDE_EFFECTING
TOKEN = jax.ShapeDtypeStruct((8, 128), F32)


def _half(ref, chip, hc):
    hr = ref.shape[1] // 2
    return ref.at[chip, pl.ds(hc * hr, hr), :]


NEAR = (0, 1)
FAR = (2,)
REL_XOR = (2, 1, 3)


def _gather_copies(refs, send_sems, recv_sems, rels):
    x, y, c = _place()
    chips = _other_chips(x, y)
    out, inc = [], []
    for a, ref in enumerate(refs):
        for p, j in enumerate(rels):
            px, py = chips[j]
            mine = _half(ref, 2 * x + y, c)
            theirs = _half(ref, 2 * px + py, c)
            sems = dict(send_sem=send_sems.at[len(rels) * a + p], recv_sem=recv_sems.at[len(rels) * a + p],
                        device_id=(px, py, c), device_id_type=MESH)
            out.append(pltpu.make_async_remote_copy(src_ref=mine, dst_ref=mine, **sems))
            inc.append(pltpu.make_async_remote_copy(src_ref=theirs, dst_ref=theirs, **sems))
    return out, inc


def _gather_start(bufs, groups, name):
    n = len(bufs)
    ng = len(groups)

    def body(*refs):
        ins = refs[:n]
        token = refs[-1]
        for gi, rels in enumerate(groups):
            out, _ = _gather_copies(ins, refs[n + 2 * gi], refs[n + 2 * gi + 1], rels)
            for cp in out:
                cp.start()
        token[...] = jnp.zeros_like(token)

    sems = []
    for rels in groups:
        sems += [pltpu.SemaphoreType.DMA((len(rels) * n,))] * 2
    res = pl.pallas_call(
        body, name=name,
        in_specs=[HBM_SPEC] * n,
        out_specs=[SEM_SPEC] * (2 * ng) + [HBM_SPEC] * n + [pl.BlockSpec(memory_space=pltpu.VMEM)],
        out_shape=sems + [pltpu.HBM(b.shape, b.dtype) for b in bufs] + [TOKEN],
        input_output_aliases={a: 2 * ng + a for a in range(n)},
        compiler_params=pltpu.CompilerParams(has_side_effects=EFFECT),
    )(*[pltpu.with_memory_space_constraint(b, pltpu.HBM) for b in bufs])
    return [(res[2 * gi], res[2 * gi + 1]) for gi in range(ng)], list(res[2 * ng:2 * ng + n]), res[-1]


def _gather_wait(bufs, sems, rels, after, name):
    n = len(bufs)
    send_sems, recv_sems = sems
    after = list(after) if isinstance(after, (list, tuple)) else [after]

    def body(*refs):
        ins = refs[:n]
        out, inc = _gather_copies(ins, refs[n], refs[n + 1], rels)
        for cp in out:
            cp.wait_send()
        for cp in inc:
            cp.wait_recv()

    return pl.pallas_call(
        body, name=name,
        in_specs=[HBM_SPEC] * n + [SEM_SPEC, SEM_SPEC] + [ANY_SPEC] * len(after),
        out_specs=[HBM_SPEC] * n,
        out_shape=[pltpu.HBM(b.shape, b.dtype) for b in bufs],
        input_output_aliases={a: a for a in range(n)},
        compiler_params=pltpu.CompilerParams(has_side_effects=EFFECT),
    )(*bufs, send_sems, recv_sems, *after)


def _forward_halves(bufs, rels, name):
    n = len(bufs)

    def body(*refs):
        cps, waits = _forward_copies(refs[n:2 * n], rels, refs[2 * n], refs[2 * n + 1])
        for cp in cps:
            cp.start()
        for cp in waits:
            cp.wait_recv()
        for cp in cps:
            cp.wait_send()

    return pl.pallas_call(
        body, name=name,
        in_specs=[ANY_SPEC] * n, out_specs=[ANY_SPEC] * n,
        out_shape=[jax.ShapeDtypeStruct(s.shape, s.dtype) for s in bufs],
        input_output_aliases={a: a for a in range(n)},
        scratch_shapes=[pltpu.SemaphoreType.DMA((len(rels) * n,)), pltpu.SemaphoreType.DMA((len(rels) * n,))],
    )(*bufs)


def _forward_copies(refs, rels, send_sems, recv_sems):
    x, y, c = _place()
    chips = _other_chips(x, y)
    cps, waits = [], []
    for a, ref in enumerate(refs):
        for p, j in enumerate(rels):
            px, py = chips[j]
            sems = dict(send_sem=send_sems.at[len(rels) * a + p], recv_sem=recv_sems.at[len(rels) * a + p],
                        device_id=(x, y, 1 - c), device_id_type=MESH)
            got = _half(ref, 2 * px + py, c)
            want = _half(ref, 2 * px + py, 1 - c)
            cps.append(pltpu.make_async_remote_copy(src_ref=got, dst_ref=got, **sems))
            waits.append(pltpu.make_async_remote_copy(src_ref=want, dst_ref=want, **sems))
    return cps, waits


def _forward_start(bufs, rels, name):
    n = len(bufs)

    def body(*refs):
        cps, _ = _forward_copies(refs[:n], rels, refs[n], refs[n + 1])
        for cp in cps:
            cp.start()
        refs[-1][...] = jnp.zeros_like(refs[-1])

    res = pl.pallas_call(
        body, name=name,
        in_specs=[HBM_SPEC] * n,
        out_specs=[SEM_SPEC, SEM_SPEC] + [HBM_SPEC] * n + [pl.BlockSpec(memory_space=pltpu.VMEM)],
        out_shape=[pltpu.SemaphoreType.DMA((len(rels) * n,))] * 2 + [pltpu.HBM(b.shape, b.dtype) for b in bufs] + [TOKEN],
        input_output_aliases={a: 2 + a for a in range(n)},
        compiler_params=pltpu.CompilerParams(has_side_effects=EFFECT),
    )(*[pltpu.with_memory_space_constraint(b, pltpu.HBM) for b in bufs])
    return (res[0], res[1]), list(res[2:2 + n]), res[-1]


def _forward_wait(bufs, sems, rels, after, name):
    n = len(bufs)

    def body(*refs):
        cps, waits = _forward_copies(refs[:n], rels, refs[n], refs[n + 1])
        for cp in cps:
            cp.wait_send()
        for cp in waits:
            cp.wait_recv()

    return pl.pallas_call(
        body, name=name,
        in_specs=[HBM_SPEC] * n + [SEM_SPEC, SEM_SPEC] + [ANY_SPEC] * len(after),
        out_specs=[HBM_SPEC] * n,
        out_shape=[pltpu.HBM(b.shape, b.dtype) for b in bufs],
        input_output_aliases={a: a for a in range(n)},
        compiler_params=pltpu.CompilerParams(has_side_effects=EFFECT),
    )(*bufs, sems[0], sems[1], *after)


def _sibling_copy(src, land, send_sems, recv_sems):
    x, y, c = _place()
    return pltpu.make_async_remote_copy(src_ref=src, dst_ref=land, send_sem=send_sems.at[0], recv_sem=recv_sems.at[0],
                                        device_id=(x, y, 1 - c), device_id_type=MESH)


def _sibling_start(part, name):
    def body(src, land, send_sems, recv_sems, src_thru, land_thru, token):
        _sibling_copy(src, land, send_sems, recv_sems).start()
        token[...] = jnp.zeros_like(token)

    land = lax.empty(part.shape, part.dtype)
    return pl.pallas_call(
        body, name=name,
        in_specs=[HBM_SPEC] * 2,
        out_specs=[SEM_SPEC, SEM_SPEC, HBM_SPEC, HBM_SPEC, pl.BlockSpec(memory_space=pltpu.VMEM)],
        out_shape=[pltpu.SemaphoreType.DMA((1,)), pltpu.SemaphoreType.DMA((1,)), pltpu.HBM(part.shape, part.dtype),
                   pltpu.HBM(part.shape, part.dtype), TOKEN],
        input_output_aliases={0: 2, 1: 3},
        compiler_params=pltpu.CompilerParams(has_side_effects=EFFECT),
    )(pltpu.with_memory_space_constraint(part, pltpu.HBM), pltpu.with_memory_space_constraint(land, pltpu.HBM))


def _sibling_wait(state, after, name):
    send_sems, recv_sems, part, land, _ = state

    def body(src, land_ref, send_ref, recv_ref, *rest):
        cp = _sibling_copy(src, land_ref, send_ref, recv_ref)
        cp.wait_send()
        cp.wait_recv()

    return pl.pallas_call(
        body, name=name,
        in_specs=[HBM_SPEC, HBM_SPEC, SEM_SPEC, SEM_SPEC] + [ANY_SPEC] * len(after),
        out_specs=[HBM_SPEC, HBM_SPEC],
        out_shape=[pltpu.HBM(part.shape, part.dtype), pltpu.HBM(land.shape, land.dtype)],
        input_output_aliases={0: 0, 1: 1},
        compiler_params=pltpu.CompilerParams(has_side_effects=EFFECT),
    )(part, land, send_sems, recv_sems, *after)[1]


def _pair_sum_rows(keep, got, name, rows=None):
    nblk, nrows, cols = keep.shape
    tr, tc = 512, _tile_cols(cols)
    r0, nr = (0, nrows) if rows is None else rows

    def body(k_ref, g_ref, o_ref):
        o_ref[...] = (k_ref[...].astype(F32) + g_ref[...].astype(F32)).astype(BF16)

    blk = pl.BlockSpec((None, tr, tc), lambda b, i, j: (b, r0 // tr + i, j))
    return pl.pallas_call(
        body, name=name, grid=(nblk, nr // tr, cols // tc),
        in_specs=[blk, blk], out_specs=blk,
        out_shape=jax.ShapeDtypeStruct(keep.shape, BF16),
        input_output_aliases={} if rows is None else {0: 0},
        compiler_params=_cp(("parallel",) * 3),
    )(keep, got)


def _scatter_start(qs, name, rows=None, lands=None):
    n = len(qs)

    def body(*refs):
        ins, lands = refs[:n], refs[n:2 * n]
        token = refs[-1]
        for cp in _scatter_copies(ins, lands, refs[2 * n], refs[2 * n + 1], rows):
            cp.start()
        token[...] = jnp.zeros_like(token)

    if lands is None:
        lands = [lax.empty((3,) + q.shape[1:], q.dtype) for q in qs]
    res = pl.pallas_call(
        body, name=name,
        in_specs=[HBM_SPEC] * (2 * n),
        out_specs=[SEM_SPEC, SEM_SPEC] + [HBM_SPEC] * (2 * n) + [pl.BlockSpec(memory_space=pltpu.VMEM)],
        out_shape=[pltpu.SemaphoreType.DMA((3 * n,)), pltpu.SemaphoreType.DMA((3 * n,))]
        + [pltpu.HBM(b.shape, b.dtype) for b in qs + lands] + [TOKEN],
        input_output_aliases={a: 2 + a for a in range(2 * n)},
        compiler_params=pltpu.CompilerParams(has_side_effects=EFFECT),
    )(*[pltpu.with_memory_space_constraint(b, pltpu.HBM) for b in qs + lands])
    return res[0], res[1], list(res[2:2 + n]), list(res[2 + n:2 + 2 * n]), res[-1]


def _scatter_copies(ins, lands, send_sems, recv_sems, rows=None):
    x, y, c = _place()
    sel = slice(None) if rows is None else pl.ds(rows[0], rows[1])
    cps = []
    for a in range(len(ins)):
        for j, (px, py) in enumerate(_other_chips(x, y)):
            cps.append(pltpu.make_async_remote_copy(
                src_ref=ins[a].at[2 * px + py, sel, :], dst_ref=lands[a].at[j, sel, :],
                send_sem=send_sems.at[3 * a + j], recv_sem=recv_sems.at[3 * a + j], device_id=(px, py, c), device_id_type=MESH))
    return cps


def _scatter_wait(qs, lands, send_sems, recv_sems, after, name, rows=None):
    n = len(qs)

    def body(*refs):
        for cp in _scatter_copies(refs[:n], refs[n:2 * n], refs[2 * n], refs[2 * n + 1], rows):
            cp.wait_send()
            cp.wait_recv()

    res = pl.pallas_call(
        body, name=name,
        in_specs=[HBM_SPEC] * (2 * n) + [SEM_SPEC, SEM_SPEC, ANY_SPEC],
        out_specs=[HBM_SPEC] * (2 * n),
        out_shape=[pltpu.HBM(b.shape, b.dtype) for b in qs + lands],
        input_output_aliases={a: a for a in range(2 * n)},
        compiler_params=pltpu.CompilerParams(has_side_effects=EFFECT),
    )(*qs, *lands, send_sems, recv_sems, after)
    return list(res[:n]), list(res[n:])


def _chip_sum(q, got, name):
    _, hr, cols = got.shape
    tr, tc = 512, _tile_cols(cols)
    chip = 2 * lax.axis_index("x") + lax.axis_index("y")
    c = lax.axis_index("c")

    def body(idx_ref, q_ref, g_ref, o_ref):
        del idx_ref
        acc = q_ref[...].astype(F32)
        for i in range(3):
            acc = acc + g_ref[i].astype(F32)
        o_ref[...] = acc

    grid_spec = pltpu.PrefetchScalarGridSpec(
        num_scalar_prefetch=1, grid=(hr // tr, cols // tc),
        in_specs=[pl.BlockSpec((None, tr, tc), lambda i, j, ix: (ix[0], i, j)),
                  pl.BlockSpec((3, tr, tc), lambda i, j, ix: (0, i, j))],
        out_specs=pl.BlockSpec((None, tr, tc), lambda i, j, ix: (ix[1], i, j)))
    return pl.pallas_call(
        body, name=name, grid_spec=grid_spec,
        out_shape=jax.ShapeDtypeStruct((2, hr, cols), F32),
        compiler_params=_cp(("parallel", "parallel")),
    )(jnp.stack([chip, c]).astype(jnp.int32), q, got)


def _join_halves(bufs, name):
    n = len(bufs)

    def body(*refs):
        outs = refs[n:2 * n]
        send_sems, recv_sems = refs[2 * n:]
        x, y, c = _place()
        cps = []
        for a in range(n):
            cps.append(pltpu.make_async_remote_copy(src_ref=outs[a].at[c], dst_ref=outs[a].at[c], send_sem=send_sems.at[a],
                                                    recv_sem=recv_sems.at[a], device_id=(x, y, 1 - c), device_id_type=MESH))
        for cp in cps:
            cp.start()
        for a in range(n):
            theirs = outs[a].at[1 - c]
            pltpu.make_async_remote_copy(src_ref=theirs, dst_ref=theirs, send_sem=send_sems.at[a], recv_sem=recv_sems.at[a],
                                         device_id=(x, y, 1 - c), device_id_type=MESH).wait_recv()
        for cp in cps:
            cp.wait_send()

    hbm = pl.BlockSpec(memory_space=pl.ANY)
    return pl.pallas_call(
        body, name=name,
        in_specs=[hbm] * n, out_specs=[hbm] * n,
        out_shape=[jax.ShapeDtypeStruct(b.shape, b.dtype) for b in bufs],
        input_output_aliases={a: a for a in range(n)},
        scratch_shapes=[pltpu.SemaphoreType.DMA((n,)), pltpu.SemaphoreType.DMA((n,))],
    )(*bufs)


def _join_copy(buf, send_sems, recv_sems):
    x, y, c = _place()
    sems = dict(send_sem=send_sems.at[0], recv_sem=recv_sems.at[0], device_id=(x, y, 1 - c), device_id_type=MESH)
    return (pltpu.make_async_remote_copy(src_ref=buf.at[c], dst_ref=buf.at[c], **sems),
            pltpu.make_async_remote_copy(src_ref=buf.at[1 - c], dst_ref=buf.at[1 - c], **sems))


def _join_start(buf, name):
    def body(b_ref, send_sems, recv_sems, thru, token):
        _join_copy(b_ref, send_sems, recv_sems)[0].start()
        token[...] = jnp.zeros_like(token)

    return pl.pallas_call(
        body, name=name,
        in_specs=[HBM_SPEC],
        out_specs=[SEM_SPEC, SEM_SPEC, HBM_SPEC, pl.BlockSpec(memory_space=pltpu.VMEM)],
        out_shape=[pltpu.SemaphoreType.DMA((1,)), pltpu.SemaphoreType.DMA((1,)), pltpu.HBM(buf.shape, buf.dtype), TOKEN],
        input_output_aliases={0: 2},
        compiler_params=pltpu.CompilerParams(has_side_effects=EFFECT),
    )(pltpu.with_memory_space_constraint(buf, pltpu.HBM))


def _join_wait(state, after, name):
    send_sems, recv_sems, buf, _ = state

    def body(b_ref, send_ref, recv_ref, *rest):
        out, inc = _join_copy(b_ref, send_ref, recv_ref)
        out.wait_send()
        inc.wait_recv()

    return pl.pallas_call(
        body, name=name,
        in_specs=[HBM_SPEC, SEM_SPEC, SEM_SPEC] + [ANY_SPEC] * len(after),
        out_specs=HBM_SPEC,
        out_shape=pltpu.HBM(buf.shape, buf.dtype),
        input_output_aliases={0: 0},
        compiler_params=pltpu.CompilerParams(has_side_effects=EFFECT),
    )(buf, send_sems, recv_sems, *after)


N_DEV = 8


def _gather_small(block, reduce, name, deps=()):
    m_per, cols = block.shape
    deps = list(deps)

    def body(x_ref, *rest):
        out_ref, all_ref, send_sems, recv_sems, local_sem = rest[len(deps):]
        x, y, c = _place()
        me, sibling = (x, y, c), (x, y, 1 - c)
        chips = _other_chips(x, y)

        def rows(px, py, pc):
            return all_ref.at[pl.ds((4 * px + 2 * py + pc) * m_per, m_per), :]

        def copy(k, block_of, to, src=None):
            return pltpu.make_async_remote_copy(
                src_ref=rows(*block_of) if src is None else src, dst_ref=rows(*block_of),
                send_sem=send_sems.at[k], recv_sem=recv_sems.at[k], device_id=to, device_id_type=MESH)

        mine = pltpu.make_async_copy(x_ref, rows(*me), local_sem)
        mine.start()
        first = [copy(0, me, sibling, src=x_ref)]
        first += [copy(1 + j, me, (*chip, c), src=x_ref) for j, chip in enumerate(chips)]
        for cp in first:
            cp.start()
        passed = [copy(4 + j, (*chip, c), sibling) for j, chip in enumerate(chips)]
        for j, chip in enumerate(chips):
            copy(1 + j, (*chip, c), me).wait_recv()
            passed[j].start()
        copy(0, sibling, me).wait_recv()
        for j, chip in enumerate(chips):
            copy(4 + j, (*chip, 1 - c), me).wait_recv()
        for cp in first + passed:
            cp.wait_send()
        mine.wait()
        if reduce:
            acc = all_ref[pl.ds(0, m_per), :]
            for i in range(1, N_DEV):
                acc = acc + all_ref[pl.ds(i * m_per, m_per), :]
            out_ref[...] = acc
        else:
            out_ref[...] = all_ref[...]

    out_rows = m_per if reduce else N_DEV * m_per
    return pl.pallas_call(
        body, name=name,
        in_specs=[pl.BlockSpec(memory_space=pltpu.VMEM)] + [pl.BlockSpec(memory_space=pl.ANY)] * len(deps),
        out_specs=pl.BlockSpec(memory_space=pltpu.VMEM),
        out_shape=jax.ShapeDtypeStruct((out_rows, cols), F32),
        scratch_shapes=[pltpu.VMEM((N_DEV * m_per, cols), F32), pltpu.SemaphoreType.DMA((7,)),
                        pltpu.SemaphoreType.DMA((7,)), pltpu.SemaphoreType.DMA],
    )(block, *deps)


class _Comm:
    def __init__(self, w_in, w_kv, w_out, chip, core):
        self.w_in, self.chip, self.core = w_in, chip, core
        self.bufs = [_cast_own(w_in, chip, "cast_w_in_sent", half=core), _cast_own(w_kv, chip, "cast_w_mem_kv"),
                     _cast_own(w_out, chip, "cast_w_out")]


    def gather_started(self):
        (self.sems,), (b_in,), tok = _gather_start(self.bufs[:1], (NEAR,), "gather_start_in_near")
        self.b_in = _cast_own(self.w_in, self.chip, "cast_w_in_kept", half=1 - self.core, prev=b_in)
        return [tok]

    def w_in_own(self):
        return self.b_in.reshape(NCHIP * D, WB)

    def w_in_near(self, after):
        (b_in,) = _gather_wait([self.b_in], self.sems, NEAR, list(after) + self.bufs[1:], "gather_wait_in_near")
        (b_in,) = _forward_halves([b_in], NEAR, "forward_in_near")
        (self.sems,), (self.b_in,), tok = _gather_start([b_in], (FAR,), "gather_start_in_far")
        return self.b_in.reshape(NCHIP * D, WB), tok

    def w_in_all(self, after):
        (b_in,) = _gather_wait([self.b_in], self.sems, FAR, after, "gather_wait_in_far")
        (b_in,) = _forward_halves([b_in], FAR, "forward_in_far")
        (self.sems,), self.b_rest, tok = _gather_start(self.bufs[1:], (NEAR + FAR,), "gather_start_rest")
        return b_in.reshape(NCHIP * D, WB), tok

    def w_rest_landed(self, after):
        b_rest = _gather_wait(self.b_rest, self.sems, NEAR + FAR, after, "gather_wait_rest")
        self.sems, self.b_rest, tok = _forward_start(b_rest, NEAR + FAR, "forward_start_rest")
        return tok

    def w_rest(self, after):
        b_kv, b_out = _forward_wait(self.b_rest, self.sems, NEAR + FAR, after, "forward_wait_rest")
        return b_kv.reshape(D, 2 * XW), b_out.reshape(D, D)

    def sibling_start(self, send, tag):
        return _sibling_start(send, "sibling_start_" + tag)

    def reduce_start_summed(self, keep, sent, tag, after=None):
        got = _sibling_wait(sent, [keep] + ([] if after is None else [after]), "sibling_wait_" + tag)
        return _scatter_start([_pair_sum_rows(keep, got, "pair_sum_" + tag)], "scatter_start_" + tag)

    def sibling_wait(self, sent, after, tag):
        return _sibling_wait(sent, after, "sibling_wait_" + tag)

    def scatter_rows(self, q, tag, rows, before=None):
        lands = None if before is None else before[3]
        return _scatter_start([q], "scatter_start_" + tag, rows=rows, lands=lands) + (rows,)

    def reduce_finish_start(self, states, tag, after):
        qs, lands = states[-1][2], states[-1][3]
        for i, st in enumerate(states):
            qs, lands = _scatter_wait(qs, lands, st[0], st[1], after, f"scatter_wait_{tag}{i}", rows=st[5])
        return _join_start(_chip_sum(qs[0], lands[0], f"chip_sum_{tag}0"), "join_start_" + tag)

    def reduce_finish_wait(self, pending, tag, after):
        j = _join_wait(pending, after, "join_wait_" + tag)
        return j.reshape(2 * j.shape[1], j.shape[2])

    def reduce_finish(self, state, tag, after):
        send_sems, recv_sems, qs, lands, _ = state
        qs, lands = _scatter_wait(qs, lands, send_sems, recv_sems, after, "scatter_wait_" + tag)
        halves = [_chip_sum(q, l, f"chip_sum_{tag}{i}") for i, (q, l) in enumerate(zip(qs, lands))]
        return [j.reshape(2 * j.shape[1], j.shape[2]) for j in _join_halves(halves, "join_halves_" + tag)]


def kernel(x, mem, pre_norm, w_in, conv_w, mem_norm, w_mem_kv, w_out, post_norm, loss_target, m_pre_norm, m_w_in, m_conv_w, m_mem_norm, m_w_mem_kv, m_w_out, m_post_norm, v_pre_norm, v_w_in, v_conv_w, v_mem_norm, v_w_mem_kv, v_w_out, v_post_norm):
    chip = 2 * lax.axis_index("x") + lax.axis_index("y")

    comm = _Comm(w_in[0], w_mem_kv[0], w_out[0], chip, lax.axis_index("c"))
    cw_blk = jnp.zeros((8, 384), F32).at[:3].set(conv_w[0])
    cw_all = _gather_small(cw_blk, False, "gather_conv_w", deps=comm.bufs[1:]).reshape(NCHIP, 2, 8, 384)[:, 0, :3]
    conv_full = jnp.transpose(cw_all, (1, 0, 2)).reshape(3, CW)
    moments = {"w_out": (w_out[0], m_w_out[0], v_w_out[0]), "w_mem_kv": (w_mem_kv[0], m_w_mem_kv[0], v_w_mem_kv[0])}
    loss, grad_x, d_pre, d_mem, d_post, d_conv, r_in, _, (upd_out, upd_kv) = _local_step(
        x, mem, pre_norm, conv_full, mem_norm, post_norm, loss_target, chip, lax.axis_index("c"), comm, moments)

    pack = jnp.concatenate([d_pre, d_mem, d_post, jnp.pad(d_conv, ((0, 0), (0, D - CW))),
                            jnp.pad(loss, ((0, 0), (0, D - 128))), jnp.zeros((1, D), F32)], axis=0)
    tot = _gather_small(pack, True, "reduce_small")
    g_pre, g_mem, g_post = tot[0:1], tot[1:2], tot[2:3]
    g_conv = lax.dynamic_slice(tot[3:6, :CW], (0, chip * 384), (3, 384))
    loss_out = tot[6, 0]

    names = ("pre_norm", "w_in", "conv_w", "mem_norm", "w_mem_kv", "w_out", "post_norm")
    ws = (pre_norm, w_in[0], conv_w[0], mem_norm, w_mem_kv[0], w_out[0], post_norm)
    gs = [g_pre, None, g_conv, g_mem, None, None, g_post]
    ms = (m_pre_norm, m_w_in[0], m_conv_w[0], m_mem_norm, m_w_mem_kv[0], m_w_out[0], m_post_norm)
    vs = (v_pre_norm, v_w_in[0], v_conv_w[0], v_mem_norm, v_w_mem_kv[0], v_w_out[0], v_post_norm)
    upd = [None if g is None else _adamw(w, g, m, v, "adamw_" + nm) for nm, w, g, m, v in zip(names, ws, gs, ms, vs)]
    upd[4], upd[5] = upd_kv, upd_out
    g_in = comm.reduce_finish_wait(r_in, "b", after=[u[1] for u in upd if u is not None])
    upd[1] = _adamw(ws[1], g_in, ms[1], vs[1], "adamw_w_in")

    def shaped(arrs):
        return [a.reshape(w.shape) if w.ndim == a.ndim else a.reshape((1,) + a.shape)
                for a, w in zip(arrs, (pre_norm, w_in, conv_w, mem_norm, w_mem_kv, w_out, post_norm))]

    grads = shaped([u[0] for u in upd])
    deltas = shaped([u[1] for u in upd])
    new_m = shaped([u[2] for u in upd])
    new_v = shaped([u[3] for u in upd])
    return (loss_out, grad_x, *grads, *deltas, *new_m, *new_v)
```
